```python
import math
import jax, jax.numpy as jnp
from jax import lax
import numpy as np

D_MODEL = 2048
BATCH = 8
SEQ = 2048
DEPTH = 1

GRID_W = 64
ROPE_THETA = 10000.0
Q_BLOCK = 128
EPS = 1e-6

HEAD_DIM = 128
HA = 8
HKV = 2
G_A = HA // HKV

HB = 8
QK_NOPE = 128
QK_ROPE = 64
V_DIM = 128
KV_RANK = 512
QK_B = QK_NOPE + QK_ROPE

W_QA = HA * HEAD_DIM
W_KA = HKV * HEAD_DIM
W_VA = HKV * HEAD_DIM
W_QB = HB * QK_B
W_CKV = KV_RANK
W_KPE = QK_ROPE
W_IN = W_QA + W_KA + W_VA + W_QB + W_CKV + W_KPE

MIX_A = HA * HEAD_DIM
MIX_B = HB * V_DIM
MIX_W = MIX_A + MIX_B

D_FF = 4 * D_MODEL
N_MOD = 6

kernel_name = "hybrid_gqa_mla_axialrope_block"


def rms_norm(x, g):
    xf = x.astype(jnp.float32)
    y = xf * lax.rsqrt(jnp.mean(xf * xf, axis=-1, keepdims=True) + EPS)
    return (y * g.astype(jnp.float32)).astype(x.dtype)


def rope_tables_2d(seq_len, dim):
    rows = seq_len // GRID_W
    r, cl = jnp.meshgrid(jnp.arange(rows), jnp.arange(GRID_W), indexing="ij")
    row = r.reshape(-1).astype(jnp.float32)
    col = cl.reshape(-1).astype(jnp.float32)
    half = dim // 2
    inv = ROPE_THETA ** (-jnp.arange(0, half, 2, dtype=jnp.float32) / half)
    ang_r = row[:, None] * inv[None, :]
    ang_c = col[:, None] * inv[None, :]
    ang = jnp.concatenate([ang_r, ang_r, ang_c, ang_c], axis=-1)
    return jnp.cos(ang), jnp.sin(ang)


def _rotate_half(v):
    h = v.shape[-1] // 2
    return jnp.concatenate([-v[..., h:], v[..., :h]], axis=-1)


def apply_rope_2d(x, cos, sin):
    half = x.shape[-1] // 2
    rot = jnp.concatenate([_rotate_half(x[..., :half]), _rotate_half(x[..., half:])], axis=-1)
    return (x.astype(jnp.float32) * cos + rot.astype(jnp.float32) * sin).astype(x.dtype)


def gqa_blocked(q, k, v):
    B, _, _, S, D = q.shape
    scale = 1.0 / math.sqrt(D)

    def block(i):
        qb = lax.dynamic_slice_in_dim(q, i * Q_BLOCK, Q_BLOCK, axis=3)
        s = jnp.einsum("bkgqd,bksd->bkgqs", qb, k).astype(jnp.float32) * scale
        p = jax.nn.softmax(s, axis=-1).astype(v.dtype)
        return jnp.einsum("bkgqs,bksd->bkgqd", p, v)

    o = lax.map(block, jnp.arange(S // Q_BLOCK))
    o = jnp.transpose(o, (1, 0, 4, 2, 3, 5))
    return o.reshape(B, S, HA * D)


def mla_blocked(q_nope, q_pe, k_nope, k_pe, v):
    B, H, S, _ = q_nope.shape
    scale = 1.0 / math.sqrt(QK_B)

    def block(i):
        qn = lax.dynamic_slice_in_dim(q_nope, i * Q_BLOCK, Q_BLOCK, axis=2)
        qp = lax.dynamic_slice_in_dim(q_pe, i * Q_BLOCK, Q_BLOCK, axis=2)
        s = (jnp.einsum("bhqd,bhsd->bhqs", qn, k_nope)
             + jnp.einsum("bhqr,bsr->bhqs", qp, k_pe)).astype(jnp.float32) * scale
        p = jax.nn.softmax(s, axis=-1).astype(v.dtype)
        return jnp.einsum("bhqs,bhsd->bhqd", p, v)

    o = lax.map(block, jnp.arange(S // Q_BLOCK))
    o = jnp.transpose(o, (1, 0, 3, 2, 4))
    return o.reshape(B, S, H * V_DIM)


def _fwd_setup_inputs(seed: int = 0) -> dict:
    key = jax.random.key(seed)
    ks = jax.random.split(key, 20)
    f = jnp.float32

    def w(k, shape, fan_in):
        return jax.random.normal(k, shape, f) * (fan_in ** -0.5)

    def gain(k, shape):
        return 1.0 + 0.05 * jax.random.normal(k, shape, f)

    L = DEPTH
    return {
        "x": jax.random.normal(ks[0], (BATCH, SEQ, D_MODEL), f),
        "c": jax.random.normal(ks[1], (BATCH, D_MODEL), f),
        "w_ada": w(ks[2], (L, D_MODEL, N_MOD * D_MODEL), D_MODEL),
        "b_ada": 0.02 * jax.random.normal(ks[3], (L, N_MOD * D_MODEL), f),
        "g_pre_attn": gain(ks[4], (L, D_MODEL)),
        "w_in": w(ks[5], (L, D_MODEL, W_IN), D_MODEL),
        "g_q_a": gain(ks[6], (L, HEAD_DIM)),
        "g_k_a": gain(ks[7], (L, HEAD_DIM)),
        "g_ckv": gain(ks[8], (L, KV_RANK)),
        "w_kv_b": w(ks[9], (L, KV_RANK, HB * (QK_NOPE + V_DIM)), KV_RANK),
        "g_out_a": gain(ks[10], (L, MIX_A)),
        "g_out_b": gain(ks[11], (L, MIX_B)),
        "w_out": w(ks[12], (L, MIX_W, D_MODEL), MIX_W),
        "g_post_attn": gain(ks[13], (L, D_MODEL)),
        "g_pre_mlp": gain(ks[14], (L, D_MODEL)),
        "w_mlp_in": w(ks[15], (L, D_MODEL, D_FF), D_MODEL),
        "w_mlp_out": w(ks[16], (L, D_FF, D_MODEL), D_FF),
        "g_post_mlp": gain(ks[17], (L, D_MODEL)),
    }


def _fwd_reference(x, c, w_ada, b_ada, g_pre_attn, w_in, g_q_a, g_k_a, g_ckv, w_kv_b,
              g_out_a, g_out_b, w_out, g_post_attn, g_pre_mlp, w_mlp_in, w_mlp_out,
              g_post_mlp):
    B, S, _ = x.shape
    cos_a, sin_a = rope_tables_2d(S, HEAD_DIM)
    cos_b, sin_b = rope_tables_2d(S, QK_ROPE)
    c_act = jax.nn.silu(c)

    offs = np.cumsum([0, W_QA, W_KA, W_VA, W_QB, W_CKV, W_KPE])

    for l in range(DEPTH):
        mod = c_act @ w_ada[l] + b_ada[l]
        sh_a, sc_a, gt_a, sh_m, sc_m, gt_m = [m[:, None, :] for m in jnp.split(mod, N_MOD, axis=-1)]

        h = rms_norm(x, g_pre_attn[l]) * (1.0 + sc_a) + sh_a
        proj = h @ w_in[l]
        q_a, k_a, v_a, q_b, ckv, k_pe = [proj[..., offs[i]:offs[i + 1]] for i in range(6)]

        q_a = rms_norm(q_a.reshape(B, S, HA, HEAD_DIM), g_q_a[l]).transpose(0, 2, 1, 3)
        k_a = rms_norm(k_a.reshape(B, S, HKV, HEAD_DIM), g_k_a[l]).transpose(0, 2, 1, 3)
        v_a = v_a.reshape(B, S, HKV, HEAD_DIM).transpose(0, 2, 1, 3)
        q_a = apply_rope_2d(q_a, cos_a, sin_a).reshape(B, HKV, G_A, S, HEAD_DIM)
        k_a = apply_rope_2d(k_a, cos_a, sin_a)
        o_a = gqa_blocked(q_a, k_a, v_a)

        q_b = q_b.reshape(B, S, HB, QK_B).transpose(0, 2, 1, 3)
        q_nope = q_b[..., :QK_NOPE]
        q_pe = apply_rope_2d(q_b[..., QK_NOPE:], cos_b, sin_b)
        kv = rms_norm(ckv, g_ckv[l]) @ w_kv_b[l]
        kv = kv.reshape(B, S, HB, QK_NOPE + V_DIM).transpose(0, 2, 1, 3)
        k_nope, v_b = kv[..., :QK_NOPE], kv[..., QK_NOPE:]
        k_pe = apply_rope_2d(k_pe, cos_b, sin_b)
        o_b = mla_blocked(q_nope, q_pe, k_nope, k_pe, v_b)

        o = jnp.concatenate([rms_norm(o_a, g_out_a[l]), rms_norm(o_b, g_out_b[l])], axis=-1)
        o = o @ w_out[l]
        x = x + gt_a * rms_norm(o, g_post_attn[l])

        h = rms_norm(x, g_pre_mlp[l]) * (1.0 + sc_m) + sh_m
        y = jnp.square(jax.nn.relu(h @ w_mlp_in[l])) @ w_mlp_out[l]
        x = x + gt_m * rms_norm(y, g_post_mlp[l])

    return x


import jax as _jax
import jax.numpy as _jnp

TWIN_FORMAT = 'train_step'
FWD_PARAMS = ['x', 'c', 'w_ada', 'b_ada', 'g_pre_attn', 'w_in', 'g_q_a', 'g_k_a', 'g_ckv', 'w_kv_b', 'g_out_a', 'g_out_b', 'w_out', 'g_post_attn', 'g_pre_mlp', 'w_mlp_in', 'w_mlp_out', 'g_post_mlp']
TWIN_WEIGHTS = ['w_ada', 'b_ada', 'g_pre_attn', 'w_in', 'g_q_a', 'g_k_a', 'g_ckv', 'w_kv_b', 'g_out_a', 'g_out_b', 'w_out', 'g_post_attn', 'g_pre_mlp', 'w_mlp_in', 'w_mlp_out', 'g_post_mlp']
TWIN_DIFF_INPUT = 'x'
TWIN_INPUTS = ['x', 'c', 'w_ada', 'b_ada', 'g_pre_attn', 'w_in', 'g_q_a', 'g_k_a', 'g_ckv', 'w_kv_b', 'g_out_a', 'g_out_b', 'w_out', 'g_post_attn', 'g_pre_mlp', 'w_mlp_in', 'w_mlp_out', 'g_post_mlp', 'loss_target', 'm_w_ada', 'm_b_ada', 'm_g_pre_attn', 'm_w_in', 'm_g_q_a', 'm_g_k_a', 'm_g_ckv', 'm_w_kv_b', 'm_g_out_a', 'm_g_out_b', 'm_w_out', 'm_g_post_attn', 'm_g_pre_mlp', 'm_w_mlp_in', 'm_w_mlp_out', 'm_g_post_mlp', 'v_w_ada', 'v_b_ada', 'v_g_pre_attn', 'v_w_in', 'v_g_q_a', 'v_g_k_a', 'v_g_ckv', 'v_w_kv_b', 'v_g_out_a', 'v_g_out_b', 'v_w_out', 'v_g_post_attn', 'v_g_pre_mlp', 'v_w_mlp_in', 'v_w_mlp_out', 'v_g_post_mlp']
TWIN_OUTPUTS = ['loss', 'grad_x', 'grad_w_ada', 'grad_b_ada', 'grad_g_pre_attn', 'grad_w_in', 'grad_g_q_a', 'grad_g_k_a', 'grad_g_ckv', 'grad_w_kv_b', 'grad_g_out_a', 'grad_g_out_b', 'grad_w_out', 'grad_g_post_attn', 'grad_g_pre_mlp', 'grad_w_mlp_in', 'grad_w_mlp_out', 'grad_g_post_mlp', 'delta_w_ada', 'delta_b_ada', 'delta_g_pre_attn', 'delta_w_in', 'delta_g_q_a', 'delta_g_k_a', 'delta_g_ckv', 'delta_w_kv_b', 'delta_g_out_a', 'delta_g_out_b', 'delta_w_out', 'delta_g_post_attn', 'delta_g_pre_mlp', 'delta_w_mlp_in', 'delta_w_mlp_out', 'delta_g_post_mlp', 'new_m_w_ada', 'new_m_b_ada', 'new_m_g_pre_attn', 'new_m_w_in', 'new_m_g_q_a', 'new_m_g_k_a', 'new_m_g_ckv', 'new_m_w_kv_b', 'new_m_g_out_a', 'new_m_g_out_b', 'new_m_w_out', 'new_m_g_post_attn', 'new_m_g_pre_mlp', 'new_m_w_mlp_in', 'new_m_w_mlp_out', 'new_m_g_post_mlp', 'new_v_w_ada', 'new_v_b_ada', 'new_v_g_pre_attn', 'new_v_w_in', 'new_v_g_q_a', 'new_v_g_k_a', 'new_v_g_ckv', 'new_v_w_kv_b', 'new_v_g_out_a', 'new_v_g_out_b', 'new_v_w_out', 'new_v_g_post_attn', 'new_v_g_pre_mlp', 'new_v_w_mlp_in', 'new_v_w_mlp_out', 'new_v_g_post_mlp']
TWIN_LEAF_KINDS = {'loss': 'loss', 'grad_x': 'grad_x', 'grad_w_ada': 'grad_w', 'grad_b_ada': 'grad_w', 'grad_g_pre_attn': 'grad_w', 'grad_w_in': 'grad_w', 'grad_g_q_a': 'grad_w', 'grad_g_k_a': 'grad_w', 'grad_g_ckv': 'grad_w', 'grad_w_kv_b': 'grad_w', 'grad_g_out_a': 'grad_w', 'grad_g_out_b': 'grad_w', 'grad_w_out': 'grad_w', 'grad_g_post_attn': 'grad_w', 'grad_g_pre_mlp': 'grad_w', 'grad_w_mlp_in': 'grad_w', 'grad_w_mlp_out': 'grad_w', 'grad_g_post_mlp': 'grad_w', 'delta_w_ada': 'delta_w', 'delta_b_ada': 'delta_w', 'delta_g_pre_attn': 'delta_w', 'delta_w_in': 'delta_w', 'delta_g_q_a': 'delta_w', 'delta_g_k_a': 'delta_w', 'delta_g_ckv': 'delta_w', 'delta_w_kv_b': 'delta_w', 'delta_g_out_a': 'delta_w', 'delta_g_out_b': 'delta_w', 'delta_w_out': 'delta_w', 'delta_g_post_attn': 'delta_w', 'delta_g_pre_mlp': 'delta_w', 'delta_w_mlp_in': 'delta_w', 'delta_w_mlp_out': 'delta_w', 'delta_g_post_mlp': 'delta_w', 'new_m_w_ada': 'new_m', 'new_m_b_ada': 'new_m', 'new_m_g_pre_attn': 'new_m', 'new_m_w_in': 'new_m', 'new_m_g_q_a': 'new_m', 'new_m_g_k_a': 'new_m', 'new_m_g_ckv': 'new_m', 'new_m_w_kv_b': 'new_m', 'new_m_g_out_a': 'new_m', 'new_m_g_out_b': 'new_m', 'new_m_w_out': 'new_m', 'new_m_g_post_attn': 'new_m', 'new_m_g_pre_mlp': 'new_m', 'new_m_w_mlp_in': 'new_m', 'new_m_w_mlp_out': 'new_m', 'new_m_g_post_mlp': 'new_m', 'new_v_w_ada': 'new_v', 'new_v_b_ada': 'new_v', 'new_v_g_pre_attn': 'new_v', 'new_v_w_in': 'new_v', 'new_v_g_q_a': 'new_v', 'new_v_g_k_a': 'new_v', 'new_v_g_ckv': 'new_v', 'new_v_w_kv_b': 'new_v', 'new_v_g_out_a': 'new_v', 'new_v_g_out_b': 'new_v', 'new_v_w_out': 'new_v', 'new_v_g_post_attn': 'new_v', 'new_v_g_pre_mlp': 'new_v', 'new_v_w_mlp_in': 'new_v', 'new_v_w_mlp_out': 'new_v', 'new_v_g_post_mlp': 'new_v'}


def _forward(args):
    return _fwd_reference(*[args[k] for k in FWD_PARAMS])


def _output_shape():
    out = _jax.eval_shape(lambda: _forward(_fwd_setup_inputs(0)))
    return out.shape, out.dtype

N_MICROBATCH = 1
ADAM_LR = 0.001
ADAM_B1 = 0.9
ADAM_B2 = 0.999
ADAM_EPS = 1e-08
ADAM_WD = 0.01
ADAM_STEP = 10
PER_EXAMPLE_BATCH_AXIS = {'x': 0, 'c': 0, 'loss_target': 0}
SHARED_INPUTS = []
_WEIGHT_DTYPES = {'w_ada': _jnp.float32, 'b_ada': _jnp.float32, 'g_pre_attn': _jnp.float32, 'w_in': _jnp.float32, 'g_q_a': _jnp.float32, 'g_k_a': _jnp.float32, 'g_ckv': _jnp.float32, 'w_kv_b': _jnp.float32, 'g_out_a': _jnp.float32, 'g_out_b': _jnp.float32, 'w_out': _jnp.float32, 'g_post_attn': _jnp.float32, 'g_pre_mlp': _jnp.float32, 'w_mlp_in': _jnp.float32, 'w_mlp_out': _jnp.float32, 'g_post_mlp': _jnp.float32}
MOMENT_SCALE = {'w_ada': 1.755148e+00, 'b_ada': 2.919205e+00, 'g_pre_attn': 1.258797e-01, 'w_in': 1.828682e+00, 'g_q_a': 9.969358e-02, 'g_k_a': 1.018803e-01, 'g_ckv': 3.393811e+00, 'w_kv_b': 1.686207e+00, 'g_out_a': 2.353359e+00, 'g_out_b': 2.428681e+00, 'w_out': 2.408636e+00, 'g_post_attn': 5.350100e+00, 'g_pre_mlp': 7.788717e-01, 'w_mlp_in': 5.582400e-01, 'w_mlp_out': 1.269884e+00, 'g_post_mlp': 3.962597e+00}


def _to_microbatches(a, axis):
    t = _jnp.moveaxis(a, axis, 0)
    t = t.reshape((N_MICROBATCH, t.shape[0] // N_MICROBATCH) + t.shape[1:])
    return _jnp.moveaxis(t, 1, axis + 1)


def setup_inputs(seed: int = 0) -> dict:
    inp = _fwd_setup_inputs(seed)
    key = _jax.random.fold_in(_jax.random.key(seed), 7919)
    shape, _ = _output_shape()
    out = dict(inp)
    out["loss_target"] = _jax.random.normal(_jax.random.fold_in(key, 0), shape, _jnp.float32)
    for i, name in enumerate(TWIN_WEIGHTS):
        w = inp[name].astype(_jnp.float32)
        if MOMENT_SCALE is None:
            s = _jnp.sqrt(_jnp.mean(_jnp.square(w)) + 1e-30)
        else:
            s = MOMENT_SCALE[name]
        km, kv = _jax.random.split(_jax.random.fold_in(key, i + 1))
        out[name] = w
        out["m_" + name] = s * _jax.random.normal(km, w.shape, _jnp.float32)
        out["v_" + name] = (s * s) * _jax.random.uniform(kv, w.shape, _jnp.float32, 0.5, 1.5)
    if N_MICROBATCH > 1:
        for name, axis in PER_EXAMPLE_BATCH_AXIS.items():
            out[name] = _to_microbatches(out[name], axis)
    return {'x': out['x'], 'c': out['c'], 'w_ada': out['w_ada'], 'b_ada': out['b_ada'], 'g_pre_attn': out['g_pre_attn'], 'w_in': out['w_in'], 'g_q_a': out['g_q_a'], 'g_k_a': out['g_k_a'], 'g_ckv': out['g_ckv'], 'w_kv_b': out['w_kv_b'], 'g_out_a': out['g_out_a'], 'g_out_b': out['g_out_b'], 'w_out': out['w_out'], 'g_post_attn': out['g_post_attn'], 'g_pre_mlp': out['g_pre_mlp'], 'w_mlp_in': out['w_mlp_in'], 'w_mlp_out': out['w_mlp_out'], 'g_post_mlp': out['g_post_mlp'], 'loss_target': out['loss_target'], 'm_w_ada': out['m_w_ada'], 'm_b_ada': out['m_b_ada'], 'm_g_pre_attn': out['m_g_pre_attn'], 'm_w_in': out['m_w_in'], 'm_g_q_a': out['m_g_q_a'], 'm_g_k_a': out['m_g_k_a'], 'm_g_ckv': out['m_g_ckv'], 'm_w_kv_b': out['m_w_kv_b'], 'm_g_out_a': out['m_g_out_a'], 'm_g_out_b': out['m_g_out_b'], 'm_w_out': out['m_w_out'], 'm_g_post_attn': out['m_g_post_attn'], 'm_g_pre_mlp': out['m_g_pre_mlp'], 'm_w_mlp_in': out['m_w_mlp_in'], 'm_w_mlp_out': out['m_w_mlp_out'], 'm_g_post_mlp': out['m_g_post_mlp'], 'v_w_ada': out['v_w_ada'], 'v_b_ada': out['v_b_ada'], 'v_g_pre_attn': out['v_g_pre_attn'], 'v_w_in': out['v_w_in'], 'v_g_q_a': out['v_g_q_a'], 'v_g_k_a': out['v_g_k_a'], 'v_g_ckv': out['v_g_ckv'], 'v_w_kv_b': out['v_w_kv_b'], 'v_g_out_a': out['v_g_out_a'], 'v_g_out_b': out['v_g_out_b'], 'v_w_out': out['v_w_out'], 'v_g_post_attn': out['v_g_post_attn'], 'v_g_pre_mlp': out['v_g_pre_mlp'], 'v_w_mlp_in': out['v_w_mlp_in'], 'v_w_mlp_out': out['v_w_mlp_out'], 'v_g_post_mlp': out['v_g_post_mlp']}


def _loss(weights, diff, rest, loss_target):
    with _jax.named_scope("forward"):
        args = {**rest, TWIN_DIFF_INPUT: diff, **{k: w.astype(_WEIGHT_DTYPES[k]) for k, w in weights.items()}}
        y = _forward(args)
    with _jax.named_scope("loss_head"):
        err = _jnp.square(y.astype(_jnp.float32) - loss_target)
        return 0.5 * _jnp.sum(_jnp.mean(err, axis=-1)) if err.ndim else 0.5 * err


def _adamw(w, g, m, v):
    m = ADAM_B1 * m + (1.0 - ADAM_B1) * g
    v = ADAM_B2 * v + (1.0 - ADAM_B2) * _jnp.square(g)
    m_hat = m / (1.0 - ADAM_B1 ** ADAM_STEP)
    v_hat = v / (1.0 - ADAM_B2 ** ADAM_STEP)
    delta = -ADAM_LR * (m_hat / (_jnp.sqrt(v_hat) + ADAM_EPS) + ADAM_WD * w)
    return delta, m, v


def reference(x, c, w_ada, b_ada, g_pre_attn, w_in, g_q_a, g_k_a, g_ckv, w_kv_b, g_out_a, g_out_b, w_out, g_post_attn, g_pre_mlp, w_mlp_in, w_mlp_out, g_post_mlp, loss_target, m_w_ada, m_b_ada, m_g_pre_attn, m_w_in, m_g_q_a, m_g_k_a, m_g_ckv, m_w_kv_b, m_g_out_a, m_g_out_b, m_w_out, m_g_post_attn, m_g_pre_mlp, m_w_mlp_in, m_w_mlp_out, m_g_post_mlp, v_w_ada, v_b_ada, v_g_pre_attn, v_w_in, v_g_q_a, v_g_k_a, v_g_ckv, v_w_kv_b, v_g_out_a, v_g_out_b, v_w_out, v_g_post_attn, v_g_pre_mlp, v_w_mlp_in, v_w_mlp_out, v_g_post_mlp):
    given = dict(x=x, c=c, w_ada=w_ada, b_ada=b_ada, g_pre_attn=g_pre_attn, w_in=w_in, g_q_a=g_q_a, g_k_a=g_k_a, g_ckv=g_ckv, w_kv_b=w_kv_b, g_out_a=g_out_a, g_out_b=g_out_b, w_out=w_out, g_post_attn=g_post_attn, g_pre_mlp=g_pre_mlp, w_mlp_in=w_mlp_in, w_mlp_out=w_mlp_out, g_post_mlp=g_post_mlp, loss_target=loss_target, m_w_ada=m_w_ada, m_b_ada=m_b_ada, m_g_pre_attn=m_g_pre_attn, m_w_in=m_w_in, m_g_q_a=m_g_q_a, m_g_k_a=m_g_k_a, m_g_ckv=m_g_ckv, m_w_kv_b=m_w_kv_b, m_g_out_a=m_g_out_a, m_g_out_b=m_g_out_b, m_w_out=m_w_out, m_g_post_attn=m_g_post_attn, m_g_pre_mlp=m_g_pre_mlp, m_w_mlp_in=m_w_mlp_in, m_w_mlp_out=m_w_mlp_out, m_g_post_mlp=m_g_post_mlp, v_w_ada=v_w_ada, v_b_ada=v_b_ada, v_g_pre_attn=v_g_pre_attn, v_w_in=v_w_in, v_g_q_a=v_g_q_a, v_g_k_a=v_g_k_a, v_g_ckv=v_g_ckv, v_w_kv_b=v_w_kv_b, v_g_out_a=v_g_out_a, v_g_out_b=v_g_out_b, v_w_out=v_w_out, v_g_post_attn=v_g_post_attn, v_g_pre_mlp=v_g_pre_mlp, v_w_mlp_in=v_w_mlp_in, v_w_mlp_out=v_w_mlp_out, v_g_post_mlp=v_g_post_mlp)
    weights = {n: given[n] for n in TWIN_WEIGHTS}
    shared = {n: given[n] for n in SHARED_INPUTS}
    per_example = {n: given[n] for n in ['x', 'c']}
    grad_fn = _jax.value_and_grad(_loss, argnums=(0, 1))

    def one_microbatch(ex, loss_target):
        ex = dict(ex)
        diff = ex.pop(TWIN_DIFF_INPUT)
        return grad_fn(weights, diff, {**shared, **ex}, loss_target)

    if N_MICROBATCH == 1:
        loss, (grad_w, grad_x) = one_microbatch(per_example, given["loss_target"])
    else:
        def body(carry, xs):
            loss_sum, grad_sum = carry
            l_k, (gw_k, gx_k) = one_microbatch(xs[0], xs[1])
            with _jax.named_scope("update"):
                return (loss_sum + l_k, _jax.tree.map(_jnp.add, grad_sum, gw_k)), gx_k

        init = (_jnp.zeros((), _jnp.float32), _jax.tree.map(_jnp.zeros_like, weights))
        (loss, grad_w), grad_x = _jax.lax.scan(body, init, (per_example, given["loss_target"]))
    with _jax.named_scope("update"):
        delta_w, new_m, new_v = {}, {}, {}
        for n in TWIN_WEIGHTS:
            delta_w[n], new_m[n], new_v[n] = _adamw(weights[n], grad_w[n], given["m_" + n], given["v_" + n])
    return (loss, grad_x, *[grad_w[n] for n in TWIN_WEIGHTS], *[delta_w[n] for n in TWIN_WEIGHTS],
            *[new_m[n] for n in TWIN_WEIGHTS], *[new_v[n] for n in TWIN_WEIGHTS])
```

```python
import functools
import math

import numpy as np
import jax
import jax.numpy as jnp
from jax import lax
from jax.experimental import pallas as pl
from jax.experimental.pallas import tpu as pltpu

F32 = jnp.float32
BF16 = jnp.bfloat16
MESH = pl.DeviceIdType.MESH

EPS = 1e-6
GRID_W = 64
ROPE_THETA = 10000.0
HEAD_DIM = 128
HA = 8
HKV = 2
HB = 8
QK_NOPE = 128
QK_ROPE = 64
V_DIM = 128
KV_RANK = 512
QK_B = QK_NOPE + QK_ROPE
W_QA = HA * HEAD_DIM
W_KA = HKV * HEAD_DIM
W_QB = HB * QK_B
W_IN = W_QA + 2 * W_KA + W_QB + KV_RANK + QK_ROPE
N_CHIPS = 4
N_DEV = 8

LANE = 128
O_QA = 0
O_KA = O_QA + W_QA
O_VA = O_KA + W_KA
O_QN = O_VA + W_KA
O_QP = O_QN + HB * QK_NOPE
O_CKV = O_QP + HB * LANE
O_KPE = O_CKV + KV_RANK
W_INP = O_KPE + LANE

ADAM_LR = 0.001
ADAM_B1 = 0.9
ADAM_B2 = 0.999
ADAM_EPS = 1e-08
ADAM_WD = 0.01
ADAM_STEP = 10

VMEM_LIMIT = 56 * 1024 * 1024


def _params(sem):
    return pltpu.CompilerParams(dimension_semantics=sem, vmem_limit_bytes=VMEM_LIMIT)


def _pick(dim, target, unit):
    best = None
    t = unit
    while t <= min(dim, target):
        if dim % t == 0:
            best = t
        t += unit
    return dim if best is None else best


def _rn(x):
    return x * lax.rsqrt(jnp.mean(x * x, axis=-1, keepdims=True) + EPS)


def _rowwise(name, fn, rows, vecs, row_outs, vec_outs, tile):
    S = rows[0].shape[0]
    assert S % tile == 0
    n_r, n_v, n_ro, n_vo = len(rows), len(vecs), len(row_outs), len(vec_outs)

    def body(*refs):
        r_in = refs[:n_r]
        v_in = refs[n_r:n_r + n_v]
        r_out = refs[n_r + n_v:n_r + n_v + n_ro]
        v_out = refs[n_r + n_v + n_ro:]
        ro, vo = fn(*[r[...] for r in r_in], *[v[...] for v in v_in])
        for ref, val in zip(r_out, ro):
            ref[...] = val.astype(ref.dtype)
        if n_vo:
            @pl.when(pl.program_id(0) == 0)
            def _():
                for ref in v_out:
                    ref[...] = jnp.zeros(ref.shape, ref.dtype)
            for ref, val in zip(v_out, vo):
                ref[...] += val

    in_specs = [pl.BlockSpec((tile, a.shape[1]), lambda i: (i, 0)) for a in rows]
    in_specs += [pl.BlockSpec(a.shape, lambda i: (0, 0)) for a in vecs]
    out_specs = [pl.BlockSpec((tile, w), lambda i: (i, 0)) for w, _ in row_outs]
    out_specs += [pl.BlockSpec((1, w), lambda i: (0, 0)) for w in vec_outs]
    out_shape = [jax.ShapeDtypeStruct((S, w), dt) for w, dt in row_outs]
    out_shape += [jax.ShapeDtypeStruct((1, w), F32) for w in vec_outs]
    res = pl.pallas_call(
        body, name=name, grid=(S // tile,), in_specs=in_specs, out_specs=out_specs, out_shape=out_shape,
        compiler_params=_params(("arbitrary",)),
    )(*rows, *vecs)
    return res[:n_ro], res[n_ro:]


def _mm_nn(name, a, b3, out_dtypes, epilogue=None, extras=()):
    M, K = a.shape
    nB, K2, Nsh = b3.shape
    assert K == K2
    N = nB * Nsh
    tm, tn, tk = _pick(M, 1024, 16), _pick(Nsh, 1536, LANE), _pick(K, 512, LANE)
    nps, nk, ne, no = Nsh // tn, K // tk, len(extras), len(out_dtypes)

    def body(a_ref, b_ref, *rest):
        e_refs, o_refs, acc = rest[:ne], rest[ne:ne + no], rest[-1]
        k = pl.program_id(2)

        @pl.when(k == 0)
        def _():
            acc[...] = jnp.zeros(acc.shape, F32)

        acc[...] += jnp.dot(a_ref[...].astype(BF16), b_ref[...].astype(BF16), preferred_element_type=F32)

        @pl.when(k == nk - 1)
        def _():
            vals = (acc[...],) if epilogue is None else epilogue(acc[...], *[e[...] for e in e_refs])
            for ref, val in zip(o_refs, vals):
                ref[...] = val.astype(ref.dtype)

    in_specs = [pl.BlockSpec((tm, tk), lambda i, j, k: (i, k)),
                pl.BlockSpec((None, tk, tn), lambda i, j, k: (j // nps, k, j % nps))]
    in_specs += [pl.BlockSpec((tm, tn), lambda i, j, k: (i, j)) for _ in extras]
    return pl.pallas_call(
        body, name=name, grid=(M // tm, N // tn, nk), in_specs=in_specs,
        out_specs=[pl.BlockSpec((tm, tn), lambda i, j, k: (i, j)) for _ in out_dtypes],
        out_shape=[jax.ShapeDtypeStruct((M, N), dt) for dt in out_dtypes],
        scratch_shapes=[pltpu.VMEM((tm, tn), F32)],
        compiler_params=_params(("parallel", "parallel", "arbitrary")),
    )(a, b3, *extras)


def _mm_nt(name, a, b3, out_dtypes, epilogue=None, extras=()):
    M, N = a.shape
    nB, K, Nsh = b3.shape
    assert N == nB * Nsh
    tm, to, tn = _pick(M, 1024, 16), _pick(K, 1024, LANE), _pick(Nsh, 512, LANE)
    nps, nn, ne, no = Nsh // tn, N // tn, len(extras), len(out_dtypes)

    def body(a_ref, b_ref, *rest):
        e_refs, o_refs, acc = rest[:ne], rest[ne:ne + no], rest[-1]
        n = pl.program_id(2)

        @pl.when(n == 0)
        def _():
            acc[...] = jnp.zeros(acc.shape, F32)

        acc[...] += lax.dot_general(a_ref[...].astype(BF16), b_ref[...].astype(BF16),
                                    (((1,), (1,)), ((), ())), preferred_element_type=F32)

        @pl.when(n == nn - 1)
        def _():
            vals = (acc[...],) if epilogue is None else epilogue(acc[...], *[e[...] for e in e_refs])
            for ref, val in zip(o_refs, vals):
                ref[...] = val.astype(ref.dtype)

    in_specs = [pl.BlockSpec((tm, tn), lambda i, o, n: (i, n)),
                pl.BlockSpec((None, to, tn), lambda i, o, n: (n // nps, o, n % nps))]
    in_specs += [pl.BlockSpec((tm, to), lambda i, o, n: (i, o)) for _ in extras]
    return pl.pallas_call(
        body, name=name, grid=(M // tm, K // to, nn), in_specs=in_specs,
        out_specs=[pl.BlockSpec((tm, to), lambda i, o, n: (i, o)) for _ in out_dtypes],
        out_shape=[jax.ShapeDtypeStruct((M, K), dt) for dt in out_dtypes],
        scratch_shapes=[pltpu.VMEM((tm, to), F32)],
        compiler_params=_params(("parallel", "parallel", "arbitrary")),
    )(a, b3, *extras)


def _mm_tn(name, a, g, nB):
    M, K = a.shape
    M2, N = g.shape
    assert M == M2 and N % nB == 0
    Nsh = N // nB
    tk, tn, tm = _pick(K, 1024, LANE), _pick(Nsh, 1024, LANE), _pick(M, 512, 16)
    nps, nm = Nsh // tn, M // tm

    def body(a_ref, g_ref, o_ref, acc):
        m = pl.program_id(2)

        @pl.when(m == 0)
        def _():
            acc[...] = jnp.zeros(acc.shape, F32)

        acc[...] += lax.dot_general(a_ref[...].astype(BF16), g_ref[...].astype(BF16),
                                    (((0,), (0,)), ((), ())), preferred_element_type=F32)

        @pl.when(m == nm - 1)
        def _():
            o_ref[...] = acc[...]

    return pl.pallas_call(
        body, name=name, grid=(K // tk, N // tn, nm),
        in_specs=[pl.BlockSpec((tm, tk), lambda k, n, m: (m, k)),
                  pl.BlockSpec((tm, tn), lambda k, n, m: (m, n))],
        out_specs=pl.BlockSpec((None, tk, tn), lambda k, n, m: (n // nps, k, n % nps)),
        out_shape=jax.ShapeDtypeStruct((nB, K, Nsh), F32),
        scratch_shapes=[pltpu.VMEM((tk, tn), F32)],
        compiler_params=_params(("parallel", "parallel", "arbitrary")),
    )(a, g)


def _softmax(s):
    m = jnp.max(s, axis=-1, keepdims=True)
    e = jnp.exp(s - m)
    return e * (1.0 / jnp.sum(e, axis=-1, keepdims=True))


def _attn_fwd(name, q, k, v, n_heads, group, dqk, dv, scale, tq):
    S = q.shape[0]

    def body(q_ref, k_ref, v_ref, o_ref):
        s = lax.dot_general(q_ref[...], k_ref[...], (((1,), (1,)), ((), ())), preferred_element_type=F32) * scale
        p = _softmax(s)
        o_ref[...] = jnp.dot(p.astype(BF16), v_ref[...], preferred_element_type=F32)

    return pl.pallas_call(
        body, name=name, grid=(n_heads, S // tq),
        in_specs=[pl.BlockSpec((tq, dqk), lambda h, i: (i, h)),
                  pl.BlockSpec((S, dqk), lambda h, i: (0, h // group)),
                  pl.BlockSpec((S, dv), lambda h, i: (0, h // group))],
        out_specs=pl.BlockSpec((tq, dv), lambda h, i: (i, h)),
        out_shape=jax.ShapeDtypeStruct((S, n_heads * dv), F32),
        compiler_params=_params(("arbitrary", "arbitrary")),
    )(q, k, v)


def _attn_bwd(name, q, k, v, do, n_heads, group, dqk, dv, scale, tq):
    S = q.shape[0]
    n_kv = n_heads // group

    def body(q_ref, k_ref, v_ref, do_ref, dq_ref, dk_ref, dv_ref):
        h, i = pl.program_id(0), pl.program_id(1)

        @pl.when((h % group == 0) & (i == 0))
        def _():
            dk_ref[...] = jnp.zeros(dk_ref.shape, F32)
            dv_ref[...] = jnp.zeros(dv_ref.shape, F32)

        qb, kb, vb, dob = q_ref[...], k_ref[...], v_ref[...], do_ref[...]
        s = lax.dot_general(qb, kb, (((1,), (1,)), ((), ())), preferred_element_type=F32) * scale
        p = _softmax(s)
        dv_ref[...] += lax.dot_general(p.astype(BF16), dob, (((0,), (0,)), ((), ())), preferred_element_type=F32)
        dp = lax.dot_general(dob, vb, (((1,), (1,)), ((), ())), preferred_element_type=F32)
        ds = (p * (dp - jnp.sum(dp * p, axis=-1, keepdims=True)) * scale).astype(BF16)
        dq_ref[...] = jnp.dot(ds, kb, preferred_element_type=F32)
        dk_ref[...] += lax.dot_general(ds, qb, (((0,), (0,)), ((), ())), preferred_element_type=F32)

    return pl.pallas_call(
        body, name=name, grid=(n_heads, S // tq),
        in_specs=[pl.BlockSpec((tq, dqk), lambda h, i: (i, h)),
                  pl.BlockSpec((S, dqk), lambda h, i: (0, h // group)),
                  pl.BlockSpec((S, dv), lambda h, i: (0, h // group)),
                  pl.BlockSpec((tq, dv), lambda h, i: (i, h))],
        out_specs=[pl.BlockSpec((tq, dqk), lambda h, i: (i, h)),
                   pl.BlockSpec((S, dqk), lambda h, i: (0, h // group)),
                   pl.BlockSpec((S, dv), lambda h, i: (0, h // group))],
        out_shape=[jax.ShapeDtypeStruct((S, n_heads * dqk), F32),
                   jax.ShapeDtypeStruct((S, n_kv * dqk), F32),
                   jax.ShapeDtypeStruct((S, n_kv * dv), F32)],
        compiler_params=_params(("arbitrary", "arbitrary")),
    )(q, k, v, do)


def _adamw(name, w, g, m, v):
    R, C = w.shape
    tile = _pick(R, max(8, (1 << 19) // C // 8 * 8), 8)
    c1 = 1.0 - ADAM_B1 ** ADAM_STEP
    c2 = 1.0 - ADAM_B2 ** ADAM_STEP

    def body(w_ref, g_ref, m_ref, v_ref, d_ref, nm_ref, nv_ref):
        gg = g_ref[...]
        nm = ADAM_B1 * m_ref[...] + (1.0 - ADAM_B1) * gg
        nv = ADAM_B2 * v_ref[...] + (1.0 - ADAM_B2) * (gg * gg)
        d_ref[...] = -ADAM_LR * ((nm / c1) / (jnp.sqrt(nv / c2) + ADAM_EPS) + ADAM_WD * w_ref[...])
        nm_ref[...] = nm
        nv_ref[...] = nv

    spec = pl.BlockSpec((tile, C), lambda i: (i, 0))
    return pl.pallas_call(
        body, name=name, grid=(R // tile,), in_specs=[spec] * 4, out_specs=[spec] * 3,
        out_shape=[jax.ShapeDtypeStruct((R, C), F32)] * 3,
        compiler_params=_params(("parallel",)),
    )(w, g, m, v)


def _flip(me, rel):
    return tuple(1 - p if r else p for p, r in zip(me, rel))


def _push(name, ins, out_shapes, transfers, aliases=None):
    ni, no, nt = len(ins), len(out_shapes), len(transfers)

    def body(*refs):
        in_refs, out_refs = refs[:ni], refs[ni:ni + no]
        send_sems, recv_sems = refs[ni + no], refs[ni + no + 1]
        me = (lax.axis_index("x"), lax.axis_index("y"), lax.axis_index("c"))
        copies = []
        for t, (si, sfn, oi, dfn, rel) in enumerate(transfers):
            src_ref = out_refs[si[1]] if isinstance(si, tuple) else in_refs[si]
            src = src_ref.at[sfn(*me)]
            dst = out_refs[oi].at[dfn(*me)]
            if rel is None:
                cp = pltpu.make_async_copy(src, dst, send_sems.at[t])
            else:
                cp = pltpu.make_async_remote_copy(src_ref=src, dst_ref=dst, send_sem=send_sems.at[t],
                                                  recv_sem=recv_sems.at[t], device_id=_flip(me, rel),
                                                  device_id_type=MESH)
            cp.start()
            copies.append(cp)
        for cp in copies:
            cp.wait()

    any_spec = pl.BlockSpec(memory_space=pl.ANY)
    return pl.pallas_call(
        body, name=name, in_specs=[any_spec] * ni, out_specs=[any_spec] * no, out_shape=out_shapes,
        scratch_shapes=[pltpu.SemaphoreType.DMA((nt,)), pltpu.SemaphoreType.DMA((nt,))],
        input_output_aliases=aliases or {},
        compiler_params=pltpu.CompilerParams(has_side_effects=True),
    )(*ins)


ICI_RELS = ((1, 0, 0), (0, 1, 0), (1, 1, 0))
SIBLING = (0, 0, 1)


def _chip(x, y):
    return 2 * x + y


def _gather_weights(shards):
    n = len(shards)
    halves = [s.shape[0] // 2 for s in shards]
    outs = [jax.ShapeDtypeStruct((N_CHIPS,) + s.shape, s.dtype) for s in shards]

    first = []
    for w in range(n):
        rh = halves[w]
        first.append((w, lambda x, y, c: (slice(None), slice(None)), w,
                      lambda x, y, c: (_chip(x, y), slice(None), slice(None)), None))
        for rel in ICI_RELS:
            first.append((w, lambda x, y, c, rh=rh: (pl.ds(c * rh, rh), slice(None)), w,
                          lambda x, y, c, rh=rh: (_chip(x, y), pl.ds(c * rh, rh), slice(None)), rel))
    part = _push("gather_w_ici", list(shards), outs, first)

    second = []
    for w in range(n):
        rh = halves[w]
        for rel in ICI_RELS:
            def idx(x, y, c, rh=rh, rel=rel):
                return (_chip(x ^ rel[0], y ^ rel[1]), pl.ds(c * rh, rh), slice(None))
            second.append((("out", w), idx, w, idx, SIBLING))
    return _push("gather_w_d2d", list(part), outs, second, aliases={w: w for w in range(n)})


def _reduce_grads(grads):
    n = len(grads)
    c_me = lax.axis_index("c")
    j_me = _chip(lax.axis_index("x"), lax.axis_index("y"))
    where = jnp.stack([c_me, j_me]).astype(jnp.int32)
    halves = [g.shape[1] // 2 for g in grads]

    t1 = []
    for w in range(n):
        rh = halves[w]
        t1.append((w, lambda x, y, c, rh=rh: (slice(None), pl.ds((1 - c) * rh, rh), slice(None)), w,
                   lambda x, y, c: (slice(None), slice(None), slice(None)), SIBLING))
    from_sib = _push("reduce_g_d2d", list(grads),
                     [jax.ShapeDtypeStruct((N_CHIPS, halves[w], grads[w].shape[2]), F32) for w in range(n)], t1)

    pair = [_pair_sum(f"pair_sum_{w}", where, grads[w], from_sib[w]) for w in range(n)]

    t3 = []
    for w in range(n):
        for r, rel in enumerate(ICI_RELS):
            t3.append((w, lambda x, y, c, rel=rel: (_chip(x ^ rel[0], y ^ rel[1]), slice(None), slice(None)), w,
                       lambda x, y, c, r=r: (r, slice(None), slice(None)), rel))
    from_chips = _push("reduce_g_ici", pair,
                       [jax.ShapeDtypeStruct((3, halves[w], grads[w].shape[2]), BF16) for w in range(n)], t3)

    mine = [_chip_sum(f"chip_sum_{w}", where, grads[w], from_sib[w], from_chips[w]) for w in range(n)]

    t5 = []
    for w in range(n):
        rh = halves[w]
        for rel in (None, SIBLING):
            t5.append((w, lambda x, y, c: (slice(None), slice(None)), w,
                       lambda x, y, c, rh=rh: (pl.ds(c * rh, rh), slice(None)), rel))
    return _push("reduce_g_swap", mine,
                 [jax.ShapeDtypeStruct((2 * halves[w], grads[w].shape[2]), F32) for w in range(n)], t5)


def _pair_sum(name, where, g, from_sib):
    _, R, C = g.shape
    rh = R // 2
    tile = _pick(rh, max(16, (1 << 19) // C // 16 * 16), 16)
    nb = rh // tile

    def body(where_ref, g_ref, s_ref, o_ref):
        o_ref[...] = (g_ref[...] + s_ref[...]).astype(BF16)

    return pl.pallas_call(
        body, name=name,
        grid_spec=pltpu.PrefetchScalarGridSpec(
            num_scalar_prefetch=1, grid=(N_CHIPS, nb),
            in_specs=[pl.BlockSpec((None, tile, C), lambda j, i, wh: (j, wh[0] * nb + i, 0)),
                      pl.BlockSpec((None, tile, C), lambda j, i, wh: (j, i, 0))],
            out_specs=pl.BlockSpec((None, tile, C), lambda j, i, wh: (j, i, 0))),
        out_shape=jax.ShapeDtypeStruct((N_CHIPS, rh, C), BF16),
        compiler_params=_params(("parallel", "parallel")),
    )(where, g, from_sib)


def _chip_sum(name, where, g, from_sib, from_chips):
    _, R, C = g.shape
    rh = R // 2
    tile = _pick(rh, max(16, (1 << 19) // C // 16 * 16), 16)
    nb = rh // tile

    def body(where_ref, g_ref, s_ref, r_ref, o_ref):
        acc = g_ref[...] + s_ref[...]
        for r in range(3):
            acc = acc + r_ref[r].astype(F32)
        o_ref[...] = acc

    return pl.pallas_call(
        body, name=name,
        grid_spec=pltpu.PrefetchScalarGridSpec(
            num_scalar_prefetch=1, grid=(nb,),
            in_specs=[pl.BlockSpec((None, tile, C), lambda i, wh: (wh[1], wh[0] * nb + i, 0)),
                      pl.BlockSpec((None, tile, C), lambda i, wh: (wh[1], i, 0)),
                      pl.BlockSpec((3, tile, C), lambda i, wh: (0, i, 0))],
            out_specs=pl.BlockSpec((tile, C), lambda i, wh: (i, 0))),
        out_shape=jax.ShapeDtypeStruct((rh, C), F32),
        compiler_params=_params(("parallel",)),
    )(where, g, from_sib, from_chips)


def _allgather8(name, blk):
    m_per, n = blk.shape
    rels = [(0, 0, 1), (1, 0, 0), (0, 1, 0), (1, 1, 0), (1, 0, 1), (0, 1, 1), (1, 1, 1)]

    def body(x_ref, out_ref, send_sems, recv_sems):
        me = (lax.axis_index("x"), lax.axis_index("y"), lax.axis_index("c"))
        my_rows = out_ref.at[pl.ds((4 * me[0] + 2 * me[1] + me[2]) * m_per, m_per), :]
        out_ref[pl.ds((4 * me[0] + 2 * me[1] + me[2]) * m_per, m_per), :] = x_ref[...]
        copies = []
        for t, rel in enumerate(rels):
            cp = pltpu.make_async_remote_copy(src_ref=x_ref, dst_ref=my_rows, send_sem=send_sems.at[t],
                                              recv_sem=recv_sems.at[t], device_id=_flip(me, rel),
                                              device_id_type=MESH)
            cp.start()
            copies.append(cp)
        for cp in copies:
            cp.wait()

    return pl.pallas_call(
        body, name=name, out_shape=jax.ShapeDtypeStruct((N_DEV * m_per, n), blk.dtype),
        in_specs=[pl.BlockSpec(memory_space=pltpu.VMEM)], out_specs=pl.BlockSpec(memory_space=pltpu.VMEM),
        scratch_shapes=[pltpu.SemaphoreType.DMA((7,)), pltpu.SemaphoreType.DMA((7,))],
        compiler_params=pltpu.CompilerParams(has_side_effects=True, vmem_limit_bytes=VMEM_LIMIT),
    )(blk)


def _rope_tables(seq_len, dim, reps):
    rows = seq_len // GRID_W
    t = np.arange(seq_len)
    row, col = (t // GRID_W).astype(np.float32), (t % GRID_W).astype(np.float32)
    half = dim // 2
    inv = jnp.asarray(ROPE_THETA, F32) ** (-jnp.arange(0, half, 2, dtype=F32) / half)
    ang_r = jnp.asarray(row)[:, None] * inv[None, :]
    ang_c = jnp.asarray(col)[:, None] * inv[None, :]
    ang = jnp.concatenate([ang_r, ang_r, ang_c, ang_c], axis=-1)
    cos, sin = jnp.cos(ang), jnp.sin(ang)
    low = (np.arange(dim) % (dim // 2)) < (dim // 4)
    s_lo = jnp.where(jnp.asarray(low)[None, :], -sin, 0.0)
    s_hi = jnp.where(jnp.asarray(low)[None, :], 0.0, sin)
    del rows
    return tuple(jnp.tile(a, (1, reps)) for a in (cos, s_lo, s_hi))


def _make_rope(shift):
    def up(v):
        return pltpu.roll(v, LANE - shift, 1)

    def down(v):
        return pltpu.roll(v, shift, 1)

    @jax.custom_vjp
    def rope(v, cos, s_lo, s_hi):
        return v * cos + up(v) * s_lo + down(v) * s_hi

    def fwd(v, cos, s_lo, s_hi):
        return rope(v, cos, s_lo, s_hi), (cos, s_lo, s_hi)

    def bwd(res, dy):
        cos, s_lo, s_hi = res
        return dy * cos + down(dy * s_lo) + up(dy * s_hi), jnp.zeros_like(cos), jnp.zeros_like(cos), jnp.zeros_like(cos)

    rope.defvjp(fwd, bwd)
    return rope


_rope_a = _make_rope(HEAD_DIM // 4)
_rope_b = _make_rope(QK_ROPE // 4)


def _prep(segs, tabs, g_q, g_k, g_ckv):
    qa, ka, va, qn, qp, ckv, kpe = segs
    ta, tb = tabs
    q_a = jnp.concatenate([_rope_a(_rn(s) * g_q, *ta) for s in qa], axis=1)
    k_a = jnp.concatenate([_rope_a(_rn(s) * g_k, *ta) for s in ka], axis=1)
    cat = []
    for h in range(HB):
        cat += [qn[h], _rope_b(qp[h], *tb)]
    q_cat = jnp.concatenate(cat, axis=1)
    return q_a, k_a, va, q_cat, _rn(ckv) * g_ckv, _rope_b(kpe, *tb)


def _split_proj(proj):
    qa = [proj[:, O_QA + h * LANE:O_QA + (h + 1) * LANE] for h in range(HA)]
    ka = [proj[:, O_KA + h * LANE:O_KA + (h + 1) * LANE] for h in range(HKV)]
    va = proj[:, O_VA:O_QN]
    qn = [proj[:, O_QN + h * LANE:O_QN + (h + 1) * LANE] for h in range(HB)]
    qp = [proj[:, O_QP + h * LANE:O_QP + (h + 1) * LANE] for h in range(HB)]
    return qa, ka, va, qn, qp, proj[:, O_CKV:O_KPE], proj[:, O_KPE:W_INP]


def _permute_w_in(nat):
    K = nat.shape[0]
    qb = nat[:, W_QA + 2 * W_KA:W_QA + 2 * W_KA + W_QB].reshape(K, HB, QK_B)
    z = lambda w: jnp.zeros((K, w), nat.dtype)
    qp = jnp.concatenate([qb[:, :, QK_NOPE:], jnp.zeros((K, HB, LANE - QK_ROPE), nat.dtype)], axis=2)
    tail = nat[:, W_QA + 2 * W_KA + W_QB:]
    return jnp.concatenate([nat[:, :O_QN], qb[:, :, :QK_NOPE].reshape(K, HB * QK_NOPE), qp.reshape(K, HB * LANE),
                            tail, z(LANE - QK_ROPE)], axis=1)


def _unpermute_w_in(p):
    K = p.shape[0]
    qn = p[:, O_QN:O_QP].reshape(K, HB, QK_NOPE)
    qp = p[:, O_QP:O_CKV].reshape(K, HB, LANE)[:, :, :QK_ROPE]
    qb = jnp.concatenate([qn, qp], axis=2).reshape(K, W_QB)
    return jnp.concatenate([p[:, :O_QN], qb, p[:, O_CKV:O_KPE + QK_ROPE]], axis=1)


def kernel(x, c, w_ada, b_ada, g_pre_attn, w_in, g_q_a, g_k_a, g_ckv, w_kv_b, g_out_a, g_out_b, w_out, g_post_attn, g_pre_mlp, w_mlp_in, w_mlp_out, g_post_mlp, loss_target, m_w_ada, m_b_ada, m_g_pre_attn, m_w_in, m_g_q_a, m_g_k_a, m_g_ckv, m_w_kv_b, m_g_out_a, m_g_out_b, m_w_out, m_g_post_attn, m_g_pre_mlp, m_w_mlp_in, m_w_mlp_out, m_g_post_mlp, v_w_ada, v_b_ada, v_g_pre_attn, v_w_in, v_g_q_a, v_g_k_a, v_g_ckv, v_w_kv_b, v_g_out_a, v_g_out_b, v_w_out, v_g_post_attn, v_g_pre_mlp, v_w_mlp_in, v_w_mlp_out, v_g_post_mlp):
    S, D = x.shape[1], x.shape[2]
    x2d, tgt = x[0], loss_target[0]
    ix, iy, ic = lax.axis_index("x"), lax.axis_index("y"), lax.axis_index("c")
    j_me = _chip(ix, iy)
    e_me = 4 * ix + 2 * iy + ic
    T = _pick(S, 256, 8)
    TB = _pick(S, 128, 8)

    big = [w_in[0], w_kv_b[0], w_out[0], w_mlp_in[0], w_mlp_out[0]]
    g_in, g_kvb, g_wout, g_mlp_in, g_mlp_out = _gather_weights([w.astype(BF16) for w in big])
    wp = _permute_w_in(jnp.transpose(g_in, (1, 0, 2)).reshape(D, W_IN))[None]
    w_out_f = g_wout.reshape(1, N_CHIPS * g_wout.shape[1], D)
    w_mlp_out_f = g_mlp_out.reshape(1, N_CHIPS * g_mlp_out.shape[1], D)
    DFF = w_mlp_out_f.shape[1]

    c_all = _allgather8("gather_c", c.reshape(8, D // 8)).reshape(N_DEV, D)

    def silu_fn(cc):
        return (cc * (1.0 / (1.0 + jnp.exp(-cc))),), ()

    (c_act,), _ = _rowwise("silu_c", silu_fn, [c_all], [], [(D, BF16)], [], N_DEV)
    ca_pad = jnp.concatenate([c_act, jnp.zeros_like(c_act)], axis=0)
    n_mod = w_ada.shape[2]
    b_cols = lax.dynamic_slice(b_ada, (0, j_me * n_mod), (1, n_mod))
    (mod_part,) = _mm_nn("mod_mm", ca_pad, w_ada, [F32], epilogue=lambda acc, b: (acc + b,),
                         extras=(jnp.broadcast_to(b_cols, (16, n_mod)),))
    mod_all = _allgather8("gather_mod", mod_part[:8]).reshape(N_DEV, 8, n_mod)
    mod = jnp.concatenate([lax.dynamic_slice(mod_all, (2 * j, e_me, 0), (1, 1, n_mod))[0] for j in range(N_CHIPS)],
                          axis=1)
    sh_a, sc_a, gt_a, sh_m, sc_m, gt_m = [mod[:, i * D:(i + 1) * D] for i in range(6)]

    tabs_a = _rope_tables(S, HEAD_DIM, 1)
    tabs_b = _rope_tables(S, QK_ROPE, LANE // QK_ROPE)

    def pre_attn_fn(xt, g, sc, sh):
        return (_rn(xt) * g * (1.0 + sc) + sh,), ()

    (h_b,), _ = _rowwise("pre_attn", pre_attn_fn, [x2d], [g_pre_attn, sc_a, sh_a], [(D, BF16)], [], T)
    (proj,) = _mm_nn("proj_mm", h_b, wp, [F32])

    def prep_fn(pt, ca_, sla, sha, cb_, slb, shb, gq, gk, gc):
        return _prep(_split_proj(pt), ((ca_, sla, sha), (cb_, slb, shb)), gq, gk, gc), ()

    (qa_b, ka_b, va_b, qcat_b, ckvn_b, kpe_b), _ = _rowwise(
        "prep", prep_fn, [proj, *tabs_a, *tabs_b], [g_q_a, g_k_a, g_ckv],
        [(W_QA, BF16), (W_KA, BF16), (W_KA, BF16), (2 * HB * LANE, BF16), (KV_RANK, BF16), (LANE, BF16)], [], TB)
    (kv,) = _mm_nn("kv_mm", ckvn_b, g_kvb, [F32])

    def kcat_fn(kvt, kpet):
        kc, vs = [], []
        for h in range(HB):
            kc += [kvt[:, 2 * h * LANE:(2 * h + 1) * LANE], kpet.astype(F32)]
            vs.append(kvt[:, (2 * h + 1) * LANE:(2 * h + 2) * LANE])
        return (jnp.concatenate(kc, axis=1), jnp.concatenate(vs, axis=1)), ()

    (kcat_b, vb_b), _ = _rowwise("kcat", kcat_fn, [kv, kpe_b], [], [(2 * HB * LANE, BF16), (HB * V_DIM, BF16)], [], T)

    TQ = _pick(S, 256, 16)
    sc_a_ = 1.0 / math.sqrt(HEAD_DIM)
    sc_b_ = 1.0 / math.sqrt(QK_B)
    o_a = _attn_fwd("attn_a_fwd", qa_b, ka_b, va_b, HA, HA // HKV, HEAD_DIM, HEAD_DIM, sc_a_, TQ)
    o_b = _attn_fwd("attn_b_fwd", qcat_b, kcat_b, vb_b, HB, 1, 2 * LANE, V_DIM, sc_b_, TQ)

    def mix_fn(oa, ob, ga, gb):
        return (jnp.concatenate([_rn(oa) * ga, _rn(ob) * gb], axis=1),), ()

    (on_b,), _ = _rowwise("mix_norm", mix_fn, [o_a, o_b], [g_out_a, g_out_b], [(2 * W_QA, BF16)], [], T)
    (o2,) = _mm_nn("out_mm", on_b, w_out_f, [F32])

    def mid(xt, o2t, gta, gpa, gpm, scm, shm):
        x1 = xt + gta * (_rn(o2t) * gpa)
        return x1, _rn(x1) * gpm * (1.0 + scm) + shm

    def mid_fn(*a):
        return mid(*a), ()

    mid_vecs = [gt_a, g_post_attn, g_pre_mlp, sc_m, sh_m]
    (x1, h2_b), _ = _rowwise("mid", mid_fn, [x2d, o2], mid_vecs, [(D, F32), (D, BF16)], [], T)
    u, a_b = _mm_nn("mlp_in_mm", h2_b, g_mlp_in, [F32, BF16],
                    epilogue=lambda acc: (acc, jnp.square(jnp.maximum(acc, 0.0))))
    (y,) = _mm_nn("mlp_out_mm", a_b, w_mlp_out_f, [F32])

    def last(yt, x1t, gtm, gpo):
        return x1t + gtm * (_rn(yt) * gpo)

    def last_fn(yt, x1t, tt, gtm, gpo):
        x2, vjp = jax.vjp(last, yt, x1t, gtm, gpo)
        err = x2 - tt
        dy, dx1, dgt, dg = vjp(err * (1.0 / D))
        loss = 0.5 * jnp.sum(jnp.mean(err * err, axis=-1))
        return (dy, dx1), (jnp.full((1, LANE), loss, F32), dgt, dg)

    (dy_b, dx2), (loss_v, d_gt_m, d_g_post_mlp) = _rowwise(
        "loss_bwd", last_fn, [y, x1, tgt], [gt_m, g_post_mlp], [(D, BF16), (D, F32)], [LANE, D, D], TB)

    (du_b,) = _mm_nt("d_mlp_out_mm", dy_b, w_mlp_out_f, [BF16],
                     epilogue=lambda acc, ut: (acc * (2.0 * jnp.maximum(ut, 0.0)),), extras=(u,))
    gw_mlp_out = _mm_tn("gw_mlp_out_mm", a_b, dy_b, 1).reshape(N_CHIPS, DFF // N_CHIPS, D)
    (dh2,) = _mm_nt("d_mlp_in_mm", du_b, g_mlp_in, [F32])
    gw_mlp_in = _mm_tn("gw_mlp_in_mm", h2_b, du_b, N_CHIPS)

    def mid_bwd_fn(xt, o2t, dh2t, dx2t, *vecs):
        _, vjp = jax.vjp(mid, xt, o2t, *vecs)
        dx, do2, dgta, dgpa, dgpm, dscm, dshm = vjp((dx2t, dh2t))
        return (dx, do2), (dgta, dgpa, dgpm, dscm, dshm)

    (dx1, do2_b), (d_gt_a, d_g_post_attn, d_g_pre_mlp, d_sc_m, d_sh_m) = _rowwise(
        "mid_bwd", mid_bwd_fn, [x2d, o2, dh2, dx2], mid_vecs, [(D, F32), (D, BF16)], [D] * 5, TB)

    (d_on,) = _mm_nt("d_out_mm", do2_b, w_out_f, [F32])
    gw_out = _mm_tn("gw_out_mm", on_b, do2_b, 1).reshape(N_CHIPS, w_out.shape[1], D)

    def mix_bwd_fn(oa, ob, dont, ga, gb):
        _, vjp = jax.vjp(lambda a_, b_, g1, g2: mix_fn(a_, b_, g1, g2)[0][0], oa, ob, ga, gb)
        doa, dob, dga, dgb = vjp(dont)
        return (doa, dob), (dga, dgb)

    (doa_b, dob_b), (d_g_out_a, d_g_out_b) = _rowwise(
        "mix_bwd", mix_bwd_fn, [o_a, o_b, d_on], [g_out_a, g_out_b], [(W_QA, BF16), (HB * V_DIM, BF16)],
        [W_QA, HB * V_DIM], T)

    dqa, dka, dva = _attn_bwd("attn_a_bwd", qa_b, ka_b, va_b, doa_b, HA, HA // HKV, HEAD_DIM, HEAD_DIM, sc_a_, TQ)
    dqcat, dkcat, dvb = _attn_bwd("attn_b_bwd", qcat_b, kcat_b, vb_b, dob_b, HB, 1, 2 * LANE, V_DIM, sc_b_, TQ)

    def kcat_bwd_fn(dkc, dvt):
        dkv, dkpe = [], None
        for h in range(HB):
            dkv += [dkc[:, 2 * h * LANE:(2 * h + 1) * LANE], dvt[:, h * V_DIM:(h + 1) * V_DIM]]
            piece = dkc[:, (2 * h + 1) * LANE:(2 * h + 2) * LANE]
            dkpe = piece if dkpe is None else dkpe + piece
        return (jnp.concatenate(dkv, axis=1), dkpe), ()

    (dkv_b, dkpe), _ = _rowwise("kcat_bwd", kcat_bwd_fn, [dkcat, dvb], [], [(2 * HB * LANE, BF16), (LANE, F32)], [], T)
    (d_ckvn,) = _mm_nt("d_kv_mm", dkv_b, g_kvb, [F32])
    gw_kvb = _mm_tn("gw_kv_mm", ckvn_b, dkv_b, N_CHIPS)

    def prep_bwd_fn(pt, dqat, dkat, dvat, dqct, dckt, dkpt, ca_, sla, sha, cb_, slb, shb, gq, gk, gc):
        tabs = ((ca_, sla, sha), (cb_, slb, shb))
        _, vjp = jax.vjp(lambda sg, a1, a2, a3: _prep(sg, tabs, a1, a2, a3), _split_proj(pt), gq, gk, gc)
        (dqa_s, dka_s, dva_s, dqn_s, dqp_s, dck_s, dkp_s), dgq, dgk, dgc = vjp((dqat, dkat, dvat, dqct, dckt, dkpt))
        dproj = jnp.concatenate([*dqa_s, *dka_s, dva_s, *dqn_s, *dqp_s, dck_s, dkp_s], axis=1)
        return (dproj,), (dgq, dgk, dgc)

    (dproj_b,), (d_g_q_a, d_g_k_a, d_g_ckv) = _rowwise(
        "prep_bwd", prep_bwd_fn, [proj, dqa, dka, dva, dqcat, d_ckvn, dkpe, *tabs_a, *tabs_b],
        [g_q_a, g_k_a, g_ckv], [(W_INP, BF16)], [HEAD_DIM, HEAD_DIM, KV_RANK], TB)

    (dh,) = _mm_nt("d_proj_mm", dproj_b, wp, [F32])
    gwp = _mm_tn("gw_in_mm", h_b, dproj_b, 1)[0]
    gw_in = jnp.transpose(_unpermute_w_in(gwp).reshape(D, N_CHIPS, W_IN // N_CHIPS), (1, 0, 2))

    def pre_attn_bwd_fn(xt, dht, dx1t, g, sc, sh):
        _, vjp = jax.vjp(lambda a_, b_, c_, d_: pre_attn_fn(a_, b_, c_, d_)[0][0], xt, g, sc, sh)
        dxt, dg, dsc, dsh = vjp(dht)
        return (dx1t + dxt,), (dg, dsc, dsh)

    (grad_x,), (d_g_pre_attn, d_sc_a, d_sh_a) = _rowwise(
        "pre_attn_bwd", pre_attn_bwd_fn, [x2d, dh, dx1], [g_pre_attn, sc_a, sh_a], [(D, F32)], [D] * 3, TB)

    small_names = ["b_ada", "g_pre_attn", "g_q_a", "g_k_a", "g_ckv", "g_out_a", "g_out_b", "g_post_attn",
                   "g_pre_mlp", "g_post_mlp"]
    small_local = jnp.concatenate([d_sh_a, d_sc_a, d_gt_a, d_sh_m, d_sc_m, d_gt_m, d_g_pre_attn, d_g_q_a, d_g_k_a,
                                   d_g_ckv, d_g_out_a, d_g_out_b, d_g_post_attn, d_g_pre_mlp, d_g_post_mlp], axis=1)
    n_small = small_local.shape[1]
    small_all = _allgather8("gather_small", small_local.reshape(8, n_small // 8)).reshape(N_DEV, n_small)

    def sum8_fn(t):
        return (), (jnp.sum(t, axis=0, keepdims=True),)

    _, (small_sum,) = _rowwise("sum_small", sum8_fn, [small_all], [], [], [n_small], N_DEV)
    dmod_cols = lax.dynamic_slice(small_all, (0, j_me * n_mod), (N_DEV, n_mod))
    dmod_pad = jnp.concatenate([dmod_cols, jnp.zeros_like(dmod_cols)], axis=0).astype(BF16)
    gw_ada = _mm_tn("gw_ada_mm", ca_pad, dmod_pad, 1)[0]

    g_big = _reduce_grads([gw_in, gw_kvb, gw_out, gw_mlp_in, gw_mlp_out])

    weights = dict(w_ada=w_ada, b_ada=b_ada, g_pre_attn=g_pre_attn, w_in=w_in, g_q_a=g_q_a, g_k_a=g_k_a, g_ckv=g_ckv,
                   w_kv_b=w_kv_b, g_out_a=g_out_a, g_out_b=g_out_b, w_out=w_out, g_post_attn=g_post_attn,
                   g_pre_mlp=g_pre_mlp, w_mlp_in=w_mlp_in, w_mlp_out=w_mlp_out, g_post_mlp=g_post_mlp)
    ms = dict(w_ada=m_w_ada, b_ada=m_b_ada, g_pre_attn=m_g_pre_attn, w_in=m_w_in, g_q_a=m_g_q_a, g_k_a=m_g_k_a,
              g_ckv=m_g_ckv, w_kv_b=m_w_kv_b, g_out_a=m_g_out_a, g_out_b=m_g_out_b, w_out=m_w_out,
              g_post_attn=m_g_post_attn, g_pre_mlp=m_g_pre_mlp, w_mlp_in=m_w_mlp_in, w_mlp_out=m_w_mlp_out,
              g_post_mlp=m_g_post_mlp)
    vs = dict(w_ada=v_w_ada, b_ada=v_b_ada, g_pre_attn=v_g_pre_attn, w_in=v_w_in, g_q_a=v_g_q_a, g_k_a=v_g_k_a,
              g_ckv=v_g_ckv, w_kv_b=v_w_kv_b, g_out_a=v_g_out_a, g_out_b=v_g_out_b, w_out=v_w_out,
              g_post_attn=v_g_post_attn, g_pre_mlp=v_g_pre_mlp, w_mlp_in=v_w_mlp_in, w_mlp_out=v_w_mlp_out,
              g_post_mlp=v_g_post_mlp)
    order = list(weights)
    grads, deltas, new_m, new_v = {}, {}, {}, {}

    big_names = ["w_in", "w_kv_b", "w_out", "w_mlp_in", "w_mlp_out", "w_ada"]
    for name, g in zip(big_names, [*g_big, gw_ada]):
        d_, m_, v_ = _adamw("adamw_" + name, weights[name][0], g, ms[name][0], vs[name][0])
        grads[name], deltas[name], new_m[name], new_v[name] = g[None], d_[None], m_[None], v_[None]

    cat = lambda d: jnp.concatenate([d[n] for n in small_names], axis=1)
    d_, m_, v_ = _adamw("adamw_small", cat(weights), small_sum, cat(ms), cat(vs))
    off = 0
    for n in small_names:
        w_ = weights[n].shape[1]
        grads[n], deltas[n], new_m[n], new_v[n] = (a[:, off:off + w_] for a in (small_sum, d_, m_, v_))
        off += w_

    loss = lax.psum(loss_v[0, 0], ("x", "y", "c"))
    return (loss, grad_x[None], *[grads[n] for n in order], *[deltas[n] for n in order],
            *[new_m[n] for n in order], *[new_v[n] for n in order])
```

```python
import functools
import math

import numpy as np
import jax
import jax.numpy as jnp
from jax import lax
from jax.experimental import pallas as pl
from jax.experimental.pallas import tpu as pltpu

F32 = jnp.float32
BF16 = jnp.bfloat16
MESH = pl.DeviceIdType.MESH

EPS = 1e-6
GRID_W = 64
ROPE_THETA = 10000.0
HEAD_DIM = 128
HA = 8
HKV = 2
HB = 8
QK_NOPE = 128
QK_ROPE = 64
V_DIM = 128
KV_RANK = 512
QK_B = QK_NOPE + QK_ROPE
W_QA = HA * HEAD_DIM
W_KA = HKV * HEAD_DIM
W_QB = HB * QK_B
W_IN = W_QA + 2 * W_KA + W_QB + KV_RANK + QK_ROPE
N_CHIPS = 4
N_DEV = 8

LANE = 128
O_QA = 0
O_KA = O_QA + W_QA
O_VA = O_KA + W_KA
O_QN = O_VA + W_KA
O_QP = O_QN + HB * QK_NOPE
O_CKV = O_QP + HB * LANE
O_KPE = O_CKV + KV_RANK
W_INP = O_KPE + LANE

ADAM_LR = 0.001
ADAM_B1 = 0.9
ADAM_B2 = 0.999
ADAM_EPS = 1e-08
ADAM_WD = 0.01
ADAM_STEP = 10

VMEM_LIMIT = 56 * 1024 * 1024


def _params(sem):
    return pltpu.CompilerParams(dimension_semantics=sem, vmem_limit_bytes=VMEM_LIMIT)


def _pick(dim, target, unit):
    best = None
    t = unit
    while t <= min(dim, target):
        if dim % t == 0:
            best = t
        t += unit
    return dim if best is None else best


def _rn(x):
    return x * lax.rsqrt(jnp.mean(x * x, axis=-1, keepdims=True) + EPS)


def _rowwise(name, fn, rows, vecs, row_outs, vec_outs, tile):
    S = rows[0].shape[0]
    assert S % tile == 0
    n_r, n_v, n_ro, n_vo = len(rows), len(vecs), len(row_outs), len(vec_outs)

    def body(*refs):
        r_in = refs[:n_r]
        v_in = refs[n_r:n_r + n_v]
        r_out = refs[n_r + n_v:n_r + n_v + n_ro]
        v_out = refs[n_r + n_v + n_ro:]
        ro, vo = fn(*[r[...] for r in r_in], *[v[...] for v in v_in])
        for ref, val in zip(r_out, ro):
            ref[...] = val.astype(ref.dtype)
        if n_vo:
            @pl.when(pl.program_id(0) == 0)
            def _():
                for ref in v_out:
                    ref[...] = jnp.zeros(ref.shape, ref.dtype)
            for ref, val in zip(v_out, vo):
                ref[...] += val

    in_specs = [pl.BlockSpec((tile, a.shape[1]), lambda i: (i, 0)) for a in rows]
    in_specs += [pl.BlockSpec(a.shape, lambda i: (0, 0)) for a in vecs]
    out_specs = [pl.BlockSpec((tile, w), lambda i: (i, 0)) for w, _ in row_outs]
    out_specs += [pl.BlockSpec((1, w), lambda i: (0, 0)) for w in vec_outs]
    out_shape = [jax.ShapeDtypeStruct((S, w), dt) for w, dt in row_outs]
    out_shape += [jax.ShapeDtypeStruct((1, w), F32) for w in vec_outs]
    res = pl.pallas_call(
        body, name=name, grid=(S // tile,), in_specs=in_specs, out_specs=out_specs, out_shape=out_shape,
        compiler_params=_params(("arbitrary",)),
    )(*rows, *vecs)
    return res[:n_ro], res[n_ro:]


def _mm_nn(name, a, b3, out_dtypes, epilogue=None, extras=()):
    M, K = a.shape
    nB, K2, Nsh = b3.shape
    assert K == K2
    N = nB * Nsh
    tm, tn, tk = _pick(M, 1024, 16), _pick(Nsh, 1536, LANE), _pick(K, 512, LANE)
    nps, nk, ne, no = Nsh // tn, K // tk, len(extras), len(out_dtypes)

    def body(a_ref, b_ref, *rest):
        e_refs, o_refs, acc = rest[:ne], rest[ne:ne + no], rest[-1]
        k = pl.program_id(2)

        @pl.when(k == 0)
        def _():
            acc[...] = jnp.zeros(acc.shape, F32)

        acc[...] += jnp.dot(a_ref[...].astype(BF16), b_ref[...].astype(BF16), preferred_element_type=F32)

        @pl.when(k == nk - 1)
        def _():
            vals = (acc[...],) if epilogue is None else epilogue(acc[...], *[e[...] for e in e_refs])
            for ref, val in zip(o_refs, vals):
                ref[...] = val.astype(ref.dtype)

    in_specs = [pl.BlockSpec((tm, tk), lambda i, j, k: (i, k)),
                pl.BlockSpec((None, tk, tn), lambda i, j, k: (j // nps, k, j % nps))]
    in_specs += [pl.BlockSpec((tm, tn), lambda i, j, k: (i, j)) for _ in extras]
    return pl.pallas_call(
        body, name=name, grid=(M // tm, N // tn, nk), in_specs=in_specs,
        out_specs=[pl.BlockSpec((tm, tn), lambda i, j, k: (i, j)) for _ in out_dtypes],
        out_shape=[jax.ShapeDtypeStruct((M, N), dt) for dt in out_dtypes],
        scratch_shapes=[pltpu.VMEM((tm, tn), F32)],
        compiler_params=_params(("parallel", "parallel", "arbitrary")),
    )(a, b3, *extras)


def _mm_nt(name, a, b3, out_dtypes, epilogue=None, extras=()):
    M, N = a.shape
    nB, K, Nsh = b3.shape
    assert N == nB * Nsh
    tm, to, tn = _pick(M, 1024, 16), _pick(K, 1024, LANE), _pick(Nsh, 512, LANE)
    nps, nn, ne, no = Nsh // tn, N // tn, len(extras), len(out_dtypes)

    def body(a_ref, b_ref, *rest):
        e_refs, o_refs, acc = rest[:ne], rest[ne:ne + no], rest[-1]
        n = pl.program_id(2)

        @pl.when(n == 0)
        def _():
            acc[...] = jnp.zeros(acc.shape, F32)

        acc[...] += lax.dot_general(a_ref[...].astype(BF16), b_ref[...].astype(BF16),
                                    (((1,), (1,)), ((), ())), preferred_element_type=F32)

        @pl.when(n == nn - 1)
        def _():
            vals = (acc[...],) if epilogue is None else epilogue(acc[...], *[e[...] for e in e_refs])
            for ref, val in zip(o_refs, vals):
                ref[...] = val.astype(ref.dtype)

    in_specs = [pl.BlockSpec((tm, tn), lambda i, o, n: (i, n)),
                pl.BlockSpec((None, to, tn), lambda i, o, n: (n // nps, o, n % nps))]
    in_specs += [pl.BlockSpec((tm, to), lambda i, o, n: (i, o)) for _ in extras]
    return pl.pallas_call(
        body, name=name, grid=(M // tm, K // to, nn), in_specs=in_specs,
        out_specs=[pl.BlockSpec((tm, to), lambda i, o, n: (i, o)) for _ in out_dtypes],
        out_shape=[jax.ShapeDtypeStruct((M, K), dt) for dt in out_dtypes],
        scratch_shapes=[pltpu.VMEM((tm, to), F32)],
        compiler_params=_params(("parallel", "parallel", "arbitrary")),
    )(a, b3, *extras)


def _mm_tn(name, a, g, nB):
    M, K = a.shape
    M2, N = g.shape
    assert M == M2 and N % nB == 0
    Nsh = N // nB
    tk, tn, tm = _pick(K, 1024, LANE), _pick(Nsh, 1024, LANE), _pick(M, 512, 16)
    nps, nm = Nsh // tn, M // tm

    def body(a_ref, g_ref, o_ref, acc):
        m = pl.program_id(2)

        @pl.when(m == 0)
        def _():
            acc[...] = jnp.zeros(acc.shape, F32)

        acc[...] += lax.dot_general(a_ref[...].astype(BF16), g_ref[...].astype(BF16),
                                    (((0,), (0,)), ((), ())), preferred_element_type=F32)

        @pl.when(m == nm - 1)
        def _():
            o_ref[...] = acc[...]

    return pl.pallas_call(
        body, name=name, grid=(K // tk, N // tn, nm),
        in_specs=[pl.BlockSpec((tm, tk), lambda k, n, m: (m, k)),
                  pl.BlockSpec((tm, tn), lambda k, n, m: (m, n))],
        out_specs=pl.BlockSpec((None, tk, tn), lambda k, n, m: (n // nps, k, n % nps)),
        out_shape=jax.ShapeDtypeStruct((nB, K, Nsh), F32),
        scratch_shapes=[pltpu.VMEM((tk, tn), F32)],
        compiler_params=_params(("parallel", "parallel", "arbitrary")),
    )(a, g)


def _softmax(s):
    m = jnp.max(s, axis=-1, keepdims=True)
    e = jnp.exp(s - m)
    return e * (1.0 / jnp.sum(e, axis=-1, keepdims=True))


def _attn_fwd(name, q, k, v, n_heads, group, dqk, dv, scale, tq):
    S = q.shape[0]

    def body(q_ref, k_ref, v_ref, o_ref):
        s = lax.dot_general(q_ref[...], k_ref[...], (((1,), (1,)), ((), ())), preferred_element_type=F32) * scale
        p = _softmax(s)
        o_ref[...] = jnp.dot(p.astype(BF16), v_ref[...], preferred_element_type=F32)

    return pl.pallas_call(
        body, name=name, grid=(n_heads, S // tq),
        in_specs=[pl.BlockSpec((tq, dqk), lambda h, i: (i, h)),
                  pl.BlockSpec((S, dqk), lambda h, i: (0, h // group)),
                  pl.BlockSpec((S, dv), lambda h, i: (0, h // group))],
        out_specs=pl.BlockSpec((tq, dv), lambda h, i: (i, h)),
        out_shape=jax.ShapeDtypeStruct((S, n_heads * dv), F32),
        compiler_params=_params(("arbitrary", "arbitrary")),
    )(q, k, v)


def _attn_bwd(name, q, k, v, do, n_heads, group, dqk, dv, scale, tq):
    S = q.shape[0]
    n_kv = n_heads // group

    def body(q_ref, k_ref, v_ref, do_ref, dq_ref, dk_ref, dv_ref):
        h, i = pl.program_id(0), pl.program_id(1)

        @pl.when((h % group == 0) & (i == 0))
        def _():
            dk_ref[...] = jnp.zeros(dk_ref.shape, F32)
            dv_ref[...] = jnp.zeros(dv_ref.shape, F32)

        qb, kb, vb, dob = q_ref[...], k_ref[...], v_ref[...], do_ref[...]
        s = lax.dot_general(qb, kb, (((1,), (1,)), ((), ())), preferred_element_type=F32) * scale
        p = _softmax(s)
        dv_ref[...] += lax.dot_general(p.astype(BF16), dob, (((0,), (0,)), ((), ())), preferred_element_type=F32)
        dp = lax.dot_general(dob, vb, (((1,), (1,)), ((), ())), preferred_element_type=F32)
        ds = (p * (dp - jnp.sum(dp * p, axis=-1, keepdims=True)) * scale).astype(BF16)
        dq_ref[...] = jnp.dot(ds, kb, preferred_element_type=F32)
        dk_ref[...] += lax.dot_general(ds, qb, (((0,), (0,)), ((), ())), preferred_element_type=F32)

    return pl.pallas_call(
        body, name=name, grid=(n_heads, S // tq),
        in_specs=[pl.BlockSpec((tq, dqk), lambda h, i: (i, h)),
                  pl.BlockSpec((S, dqk), lambda h, i: (0, h // group)),
                  pl.BlockSpec((S, dv), lambda h, i: (0, h // group)),
                  pl.BlockSpec((tq, dv), lambda h, i: (i, h))],
        out_specs=[pl.BlockSpec((tq, dqk), lambda h, i: (i, h)),
                   pl.BlockSpec((S, dqk), lambda h, i: (0, h // group)),
                   pl.BlockSpec((S, dv), lambda h, i: (0, h // group))],
        out_shape=[jax.ShapeDtypeStruct((S, n_heads * dqk), F32),
                   jax.ShapeDtypeStruct((S, n_kv * dqk), F32),
                   jax.ShapeDtypeStruct((S, n_kv * dv), F32)],
        compiler_params=_params(("arbitrary", "arbitrary")),
    )(q, k, v, do)


def _adamw(name, w, g, m, v):
    R, C = w.shape
    tile = _pick(R, max(8, (1 << 19) // C // 8 * 8), 8)
    c1 = 1.0 - ADAM_B1 ** ADAM_STEP
    c2 = 1.0 - ADAM_B2 ** ADAM_STEP

    def body(w_ref, g_ref, m_ref, v_ref, d_ref, nm_ref, nv_ref):
        gg = g_ref[...]
        nm = ADAM_B1 * m_ref[...] + (1.0 - ADAM_B1) * gg
        nv = ADAM_B2 * v_ref[...] + (1.0 - ADAM_B2) * (gg * gg)
        d_ref[...] = -ADAM_LR * ((nm / c1) / (jnp.sqrt(nv / c2) + ADAM_EPS) + ADAM_WD * w_ref[...])
        nm_ref[...] = nm
        nv_ref[...] = nv

    spec = pl.BlockSpec((tile, C), lambda i: (i, 0))
    return pl.pallas_call(
        body, name=name, grid=(R // tile,), in_specs=[spec] * 4, out_specs=[spec] * 3,
        out_shape=[jax.ShapeDtypeStruct((R, C), F32)] * 3,
        compiler_params=_params(("parallel",)),
    )(w, g, m, v)


def _flip(me, rel):
    return tuple(1 - p if r else p for p, r in zip(me, rel))


def _push(name, ins, out_shapes, transfers, aliases=None):
    ni, no, nt = len(ins), len(out_shapes), len(transfers)

    def body(*refs):
        in_refs, out_refs = refs[:ni], refs[ni:ni + no]
        send_sems, recv_sems = refs[ni + no], refs[ni + no + 1]
        me = (lax.axis_index("x"), lax.axis_index("y"), lax.axis_index("c"))
        copies = []
        for t, (si, sfn, oi, dfn, rel) in enumerate(transfers):
            src_ref = out_refs[si[1]] if isinstance(si, tuple) else in_refs[si]
            src = src_ref.at[sfn(*me)]
            dst = out_refs[oi].at[dfn(*me)]
            if rel is None:
                cp = pltpu.make_async_copy(src, dst, send_sems.at[t])
            else:
                cp = pltpu.make_async_remote_copy(src_ref=src, dst_ref=dst, send_sem=send_sems.at[t],
                                                  recv_sem=recv_sems.at[t], device_id=_flip(me, rel),
                                                  device_id_type=MESH)
            cp.start()
            copies.append(cp)
        for cp in copies:
            cp.wait()

    any_spec = pl.BlockSpec(memory_space=pl.ANY)
    return pl.pallas_call(
        body, name=name, in_specs=[any_spec] * ni, out_specs=[any_spec] * no, out_shape=out_shapes,
        scratch_shapes=[pltpu.SemaphoreType.DMA((nt,)), pltpu.SemaphoreType.DMA((nt,))],
        input_output_aliases=aliases or {},
        compiler_params=pltpu.CompilerParams(has_side_effects=True),
    )(*ins)


ICI_RELS = ((1, 0, 0), (0, 1, 0), (1, 1, 0))
SIBLING = (0, 0, 1)


def _chip(x, y):
    return 2 * x + y


def _cast_into_slot(name, where, w):
    R, C = w.shape
    tile = _pick(R, 256, 16)

    def body(where_ref, w_ref, o_ref):
        o_ref[...] = w_ref[...].astype(BF16)

    return pl.pallas_call(
        body, name=name,
        grid_spec=pltpu.PrefetchScalarGridSpec(
            num_scalar_prefetch=1, grid=(R // tile,),
            in_specs=[pl.BlockSpec((tile, C), lambda i, wh: (i, 0))],
            out_specs=pl.BlockSpec((None, tile, C), lambda i, wh: (wh[1], i, 0))),
        out_shape=jax.ShapeDtypeStruct((N_CHIPS, R, C), BF16),
        compiler_params=_params(("parallel",)),
    )(where, w)


def _gather_weights(where, shards):
    n = len(shards)
    halves = [s.shape[0] // 2 for s in shards]
    outs = [jax.ShapeDtypeStruct((N_CHIPS,) + s.shape, BF16) for s in shards]
    own = [_cast_into_slot(f"cast_w_{w}", where, shards[w]) for w in range(n)]

    first = []
    for w in range(n):
        rh = halves[w]
        for rel in ICI_RELS:
            def idx(x, y, c, rh=rh):
                return (_chip(x, y), pl.ds(c * rh, rh), slice(None))
            first.append((("out", w), idx, w, idx, rel))
    part = _push("gather_w_ici", own, outs, first, aliases={w: w for w in range(n)})

    second = []
    for w in range(n):
        rh = halves[w]
        for rel in ICI_RELS:
            def idx(x, y, c, rh=rh, rel=rel):
                return (_chip(x ^ rel[0], y ^ rel[1]), pl.ds(c * rh, rh), slice(None))
            second.append((("out", w), idx, w, idx, SIBLING))
    return _push("gather_w_d2d", list(part), outs, second, aliases={w: w for w in range(n)})


def _reduce_grads(grads):
    n = len(grads)
    c_me = lax.axis_index("c")
    j_me = _chip(lax.axis_index("x"), lax.axis_index("y"))
    where = jnp.stack([c_me, j_me]).astype(jnp.int32)
    halves = [g.shape[1] // 2 for g in grads]

    t1 = []
    for w in range(n):
        rh = halves[w]
        t1.append((w, lambda x, y, c, rh=rh: (slice(None), pl.ds((1 - c) * rh, rh), slice(None)), w,
                   lambda x, y, c: (slice(None), slice(None), slice(None)), SIBLING))
    from_sib = _push("reduce_g_d2d", list(grads),
                     [jax.ShapeDtypeStruct((N_CHIPS, halves[w], grads[w].shape[2]), F32) for w in range(n)], t1)

    pair = [_pair_sum(f"pair_sum_{w}", where, grads[w], from_sib[w]) for w in range(n)]

    t3 = []
    for w in range(n):
        for r, rel in enumerate(ICI_RELS):
            t3.append((w, lambda x, y, c, rel=rel: (_chip(x ^ rel[0], y ^ rel[1]), slice(None), slice(None)), w,
                       lambda x, y, c, r=r: (r, slice(None), slice(None)), rel))
    from_chips = _push("reduce_g_ici", pair,
                       [jax.ShapeDtypeStruct((3, halves[w], grads[w].shape[2]), BF16) for w in range(n)], t3)

    mine = [_chip_sum(f"chip_sum_{w}", where, grads[w], from_sib[w], from_chips[w]) for w in range(n)]

    t5 = []
    for w in range(n):
        def idx(x, y, c, rh=halves[w]):
            return (pl.ds(c * rh, rh), slice(None))
        t5.append((("out", w), idx, w, idx, SIBLING))
    return _push("reduce_g_swap", mine,
                 [jax.ShapeDtypeStruct((2 * halves[w], grads[w].shape[2]), F32) for w in range(n)], t5,
                 aliases={w: w for w in range(n)})


def _pair_sum(name, where, g, from_sib):
    _, R, C = g.shape
    rh = R // 2
    tile = _pick(rh, max(16, (1 << 19) // C // 16 * 16), 16)
    nb = rh // tile

    def body(where_ref, g_ref, s_ref, o_ref):
        o_ref[...] = (g_ref[...] + s_ref[...]).astype(BF16)

    return pl.pallas_call(
        body, name=name,
        grid_spec=pltpu.PrefetchScalarGridSpec(
            num_scalar_prefetch=1, grid=(N_CHIPS, nb),
            in_specs=[pl.BlockSpec((None, tile, C), lambda j, i, wh: (j, wh[0] * nb + i, 0)),
                      pl.BlockSpec((None, tile, C), lambda j, i, wh: (j, i, 0))],
            out_specs=pl.BlockSpec((None, tile, C), lambda j, i, wh: (j, i, 0))),
        out_shape=jax.ShapeDtypeStruct((N_CHIPS, rh, C), BF16),
        compiler_params=_params(("parallel", "parallel")),
    )(where, g, from_sib)


def _chip_sum(name, where, g, from_sib, from_chips):
    _, R, C = g.shape
    rh = R // 2
    tile = _pick(rh, max(16, (1 << 19) // C // 16 * 16), 16)
    nb = rh // tile

    def body(where_ref, g_ref, s_ref, r_ref, o_ref):
        acc = g_ref[...] + s_ref[...]
        for r in range(3):
            acc = acc + r_ref[r].astype(F32)
        o_ref[...] = acc

    return pl.pallas_call(
        body, name=name,
        grid_spec=pltpu.PrefetchScalarGridSpec(
            num_scalar_prefetch=1, grid=(nb,),
            in_specs=[pl.BlockSpec((None, tile, C), lambda i, wh: (wh[1], wh[0] * nb + i, 0)),
                      pl.BlockSpec((None, tile, C), lambda i, wh: (wh[1], i, 0)),
                      pl.BlockSpec((3, tile, C), lambda i, wh: (0, i, 0))],
            out_specs=pl.BlockSpec((tile, C), lambda i, wh: (wh[0] * nb + i, 0))),
        out_shape=jax.ShapeDtypeStruct((R, C), F32),
        compiler_params=_params(("parallel",)),
    )(where, g, from_sib, from_chips)


def _allgather8(name, blk):
    m_per, n = blk.shape
    rels = [(0, 0, 1), (1, 0, 0), (0, 1, 0), (1, 1, 0), (1, 0, 1), (0, 1, 1), (1, 1, 1)]

    def body(x_ref, out_ref, send_sems, recv_sems):
        me = (lax.axis_index("x"), lax.axis_index("y"), lax.axis_index("c"))
        my_rows = out_ref.at[pl.ds((4 * me[0] + 2 * me[1] + me[2]) * m_per, m_per), :]
        out_ref[pl.ds((4 * me[0] + 2 * me[1] + me[2]) * m_per, m_per), :] = x_ref[...]
        copies = []
        for t, rel in enumerate(rels):
            cp = pltpu.make_async_remote_copy(src_ref=x_ref, dst_ref=my_rows, send_sem=send_sems.at[t],
                                              recv_sem=recv_sems.at[t], device_id=_flip(me, rel),
                                              device_id_type=MESH)
            cp.start()
            copies.append(cp)
        for cp in copies:
            cp.wait()

    return pl.pallas_call(
        body, name=name, out_shape=jax.ShapeDtypeStruct((N_DEV * m_per, n), blk.dtype),
        in_specs=[pl.BlockSpec(memory_space=pltpu.VMEM)], out_specs=pl.BlockSpec(memory_space=pltpu.VMEM),
        scratch_shapes=[pltpu.SemaphoreType.DMA((7,)), pltpu.SemaphoreType.DMA((7,))],
        compiler_params=pltpu.CompilerParams(has_side_effects=True, vmem_limit_bytes=VMEM_LIMIT),
    )(blk)


def _rope_tables(seq_len, dim, reps):
    rows = seq_len // GRID_W
    t = np.arange(seq_len)
    row, col = (t // GRID_W).astype(np.float32), (t % GRID_W).astype(np.float32)
    half = dim // 2
    inv = jnp.asarray(ROPE_THETA, F32) ** (-jnp.arange(0, half, 2, dtype=F32) / half)
    ang_r = jnp.asarray(row)[:, None] * inv[None, :]
    ang_c = jnp.asarray(col)[:, None] * inv[None, :]
    ang = jnp.concatenate([ang_r, ang_r, ang_c, ang_c], axis=-1)
    cos, sin = jnp.cos(ang), jnp.sin(ang)
    low = (np.arange(dim) % (dim // 2)) < (dim // 4)
    s_lo = jnp.where(jnp.asarray(low)[None, :], -sin, 0.0)
    s_hi = jnp.where(jnp.asarray(low)[None, :], 0.0, sin)
    del rows
    return tuple(jnp.tile(a, (1, reps)) for a in (cos, s_lo, s_hi))


def _make_rope(shift):
    def up(v):
        return pltpu.roll(v, LANE - shift, 1)

    def down(v):
        return pltpu.roll(v, shift, 1)

    @jax.custom_vjp
    def rope(v, cos, s_lo, s_hi):
        return v * cos + up(v) * s_lo + down(v) * s_hi

    def fwd(v, cos, s_lo, s_hi):
        return rope(v, cos, s_lo, s_hi), (cos, s_lo, s_hi)

    def bwd(res, dy):
        cos, s_lo, s_hi = res
        return dy * cos + down(dy * s_lo) + up(dy * s_hi), jnp.zeros_like(cos), jnp.zeros_like(cos), jnp.zeros_like(cos)

    rope.defvjp(fwd, bwd)
    return rope


_rope_a = _make_rope(HEAD_DIM // 4)
_rope_b = _make_rope(QK_ROPE // 4)


def _prep(segs, tabs, g_q, g_k, g_ckv):
    qa, ka, va, qn, qp, ckv, kpe = segs
    ta, tb = tabs
    q_a = jnp.concatenate([_rope_a(_rn(s) * g_q, *ta) for s in qa], axis=1)
    k_a = jnp.concatenate([_rope_a(_rn(s) * g_k, *ta) for s in ka], axis=1)
    cat = []
    for h in range(HB):
        cat += [qn[h], _rope_b(qp[h], *tb)]
    q_cat = jnp.concatenate(cat, axis=1)
    return q_a, k_a, va, q_cat, _rn(ckv) * g_ckv, _rope_b(kpe, *tb)


def _split_proj(proj):
    qa = [proj[:, O_QA + h * LANE:O_QA + (h + 1) * LANE] for h in range(HA)]
    ka = [proj[:, O_KA + h * LANE:O_KA + (h + 1) * LANE] for h in range(HKV)]
    va = proj[:, O_VA:O_QN]
    qn = [proj[:, O_QN + h * LANE:O_QN + (h + 1) * LANE] for h in range(HB)]
    qp = [proj[:, O_QP + h * LANE:O_QP + (h + 1) * LANE] for h in range(HB)]
    return qa, ka, va, qn, qp, proj[:, O_CKV:O_KPE], proj[:, O_KPE:W_INP]


def _permute_w_in(nat):
    K = nat.shape[0]
    qb = nat[:, W_QA + 2 * W_KA:W_QA + 2 * W_KA + W_QB].reshape(K, HB, QK_B)
    z = lambda w: jnp.zeros((K, w), nat.dtype)
    qp = jnp.concatenate([qb[:, :, QK_NOPE:], jnp.zeros((K, HB, LANE - QK_ROPE), nat.dtype)], axis=2)
    tail = nat[:, W_QA + 2 * W_KA + W_QB:]
    return jnp.concatenate([nat[:, :O_QN], qb[:, :, :QK_NOPE].reshape(K, HB * QK_NOPE), qp.reshape(K, HB * LANE),
                            tail, z(LANE - QK_ROPE)], axis=1)


def _unpermute_w_in(p):
    K = p.shape[0]
    qn = p[:, O_QN:O_QP].reshape(K, HB, QK_NOPE)
    qp = p[:, O_QP:O_CKV].reshape(K, HB, LANE)[:, :, :QK_ROPE]
    qb = jnp.concatenate([qn, qp], axis=2).reshape(K, W_QB)
    return jnp.concatenate([p[:, :O_QN], qb, p[:, O_CKV:O_KPE + QK_ROPE]], axis=1)


def kernel(x, c, w_ada, b_ada, g_pre_attn, w_in, g_q_a, g_k_a, g_ckv, w_kv_b, g_out_a, g_out_b, w_out, g_post_attn, g_pre_mlp, w_mlp_in, w_mlp_out, g_post_mlp, loss_target, m_w_ada, m_b_ada, m_g_pre_attn, m_w_in, m_g_q_a, m_g_k_a, m_g_ckv, m_w_kv_b, m_g_out_a, m_g_out_b, m_w_out, m_g_post_attn, m_g_pre_mlp, m_w_mlp_in, m_w_mlp_out, m_g_post_mlp, v_w_ada, v_b_ada, v_g_pre_attn, v_w_in, v_g_q_a, v_g_k_a, v_g_ckv, v_w_kv_b, v_g_out_a, v_g_out_b, v_w_out, v_g_post_attn, v_g_pre_mlp, v_w_mlp_in, v_w_mlp_out, v_g_post_mlp):
    S, D = x.shape[1], x.shape[2]
    x2d, tgt = x[0], loss_target[0]
    ix, iy, ic = lax.axis_index("x"), lax.axis_index("y"), lax.axis_index("c")
    j_me = _chip(ix, iy)
    e_me = 4 * ix + 2 * iy + ic
    T = _pick(S, 256, 8)
    TB = _pick(S, 128, 8)

    big = [w_in[0], w_kv_b[0], w_out[0], w_mlp_in[0], w_mlp_out[0]]
    where = jnp.stack([ic, j_me]).astype(jnp.int32)
    g_in, g_kvb, g_wout, g_mlp_in, g_mlp_out = _gather_weights(where, big)
    wp = _permute_w_in(jnp.transpose(g_in, (1, 0, 2)).reshape(D, W_IN))[None]
    w_out_f = g_wout.reshape(1, N_CHIPS * g_wout.shape[1], D)
    w_mlp_out_f = g_mlp_out.reshape(1, N_CHIPS * g_mlp_out.shape[1], D)
    DFF = w_mlp_out_f.shape[1]

    c_all = _allgather8("gather_c", c.reshape(8, D // 8)).reshape(N_DEV, D)

    def silu_fn(cc):
        return (cc * (1.0 / (1.0 + jnp.exp(-cc))),), ()

    (c_act,), _ = _rowwise("silu_c", silu_fn, [c_all], [], [(D, BF16)], [], N_DEV)
    ca_pad = jnp.concatenate([c_act, jnp.zeros_like(c_act)], axis=0)
    n_mod = w_ada.shape[2]
    b_cols = lax.dynamic_slice(b_ada, (0, j_me * n_mod), (1, n_mod))
    (mod_part,) = _mm_nn("mod_mm", ca_pad, w_ada, [F32], epilogue=lambda acc, b: (acc + b,),
                         extras=(jnp.broadcast_to(b_cols, (16, n_mod)),))
    mod_all = _allgather8("gather_mod", mod_part[:8]).reshape(N_DEV, 8, n_mod)
    mod = jnp.concatenate([lax.dynamic_slice(mod_all, (2 * j, e_me, 0), (1, 1, n_mod))[0] for j in range(N_CHIPS)],
                          axis=1)
    sh_a, sc_a, gt_a, sh_m, sc_m, gt_m = [mod[:, i * D:(i + 1) * D] for i in range(6)]

    tabs_a = _rope_tables(S, HEAD_DIM, 1)
    tabs_b = _rope_tables(S, QK_ROPE, LANE // QK_ROPE)

    def pre_attn_fn(xt, g, sc, sh):
        return (_rn(xt) * g * (1.0 + sc) + sh,), ()

    (h_b,), _ = _rowwise("pre_attn", pre_attn_fn, [x2d], [g_pre_attn, sc_a, sh_a], [(D, BF16)], [], T)
    (proj,) = _mm_nn("proj_mm", h_b, wp, [F32])

    def prep_fn(pt, ca_, sla, sha, cb_, slb, shb, gq, gk, gc):
        return _prep(_split_proj(pt), ((ca_, sla, sha), (cb_, slb, shb)), gq, gk, gc), ()

    (qa_b, ka_b, va_b, qcat_b, ckvn_b, kpe_b), _ = _rowwise(
        "prep", prep_fn, [proj, *tabs_a, *tabs_b], [g_q_a, g_k_a, g_ckv],
        [(W_QA, BF16), (W_KA, BF16), (W_KA, BF16), (2 * HB * LANE, BF16), (KV_RANK, BF16), (LANE, BF16)], [], TB)
    (kv,) = _mm_nn("kv_mm", ckvn_b, g_kvb, [F32])

    def kcat_fn(kvt, kpet):
        kc, vs = [], []
        for h in range(HB):
            kc += [kvt[:, 2 * h * LANE:(2 * h + 1) * LANE], kpet.astype(F32)]
            vs.append(kvt[:, (2 * h + 1) * LANE:(2 * h + 2) * LANE])
        return (jnp.concatenate(kc, axis=1), jnp.concatenate(vs, axis=1)), ()

    (kcat_b, vb_b), _ = _rowwise("kcat", kcat_fn, [kv, kpe_b], [], [(2 * HB * LANE, BF16), (HB * V_DIM, BF16)], [], T)

    TQ = _pick(S, 256, 16)
    sc_a_ = 1.0 / math.sqrt(HEAD_DIM)
    sc_b_ = 1.0 / math.sqrt(QK_B)
    o_a = _attn_fwd("attn_a_fwd", qa_b, ka_b, va_b, HA, HA // HKV, HEAD_DIM, HEAD_DIM, sc_a_, TQ)
    o_b = _attn_fwd("attn_b_fwd", qcat_b, kcat_b, vb_b, HB, 1, 2 * LANE, V_DIM, sc_b_, TQ)

    def mix_fn(oa, ob, ga, gb):
        return (jnp.concatenate([_rn(oa) * ga, _rn(ob) * gb], axis=1),), ()

    (on_b,), _ = _rowwise("mix_norm", mix_fn, [o_a, o_b], [g_out_a, g_out_b], [(2 * W_QA, BF16)], [], T)
    (o2,) = _mm_nn("out_mm", on_b, w_out_f, [F32])

    def mid(xt, o2t, gta, gpa, gpm, scm, shm):
        x1 = xt + gta * (_rn(o2t) * gpa)
        return x1, _rn(x1) * gpm * (1.0 + scm) + shm

    def mid_fn(*a):
        return mid(*a), ()

    mid_vecs = [gt_a, g_post_attn, g_pre_mlp, sc_m, sh_m]
    (x1, h2_b), _ = _rowwise("mid", mid_fn, [x2d, o2], mid_vecs, [(D, F32), (D, BF16)], [], T)
    u, a_b = _mm_nn("mlp_in_mm", h2_b, g_mlp_in, [F32, BF16],
                    epilogue=lambda acc: (acc, jnp.square(jnp.maximum(acc, 0.0))))
    (y,) = _mm_nn("mlp_out_mm", a_b, w_mlp_out_f, [F32])

    def last(yt, x1t, gtm, gpo):
        return x1t + gtm * (_rn(yt) * gpo)

    def last_fn(yt, x1t, tt, gtm, gpo):
        x2, vjp = jax.vjp(last, yt, x1t, gtm, gpo)
        err = x2 - tt
        dy, dx1, dgt, dg = vjp(err * (1.0 / D))
        loss = 0.5 * jnp.sum(jnp.mean(err * err, axis=-1))
        return (dy, dx1), (jnp.full((1, LANE), loss, F32), dgt, dg)

    (dy_b, dx2), (loss_v, d_gt_m, d_g_post_mlp) = _rowwise(
        "loss_bwd", last_fn, [y, x1, tgt], [gt_m, g_post_mlp], [(D, BF16), (D, F32)], [LANE, D, D], TB)

    (du_b,) = _mm_nt("d_mlp_out_mm", dy_b, w_mlp_out_f, [BF16],
                     epilogue=lambda acc, ut: (acc * (2.0 * jnp.maximum(ut, 0.0)),), extras=(u,))
    gw_mlp_out = _mm_tn("gw_mlp_out_mm", a_b, dy_b, 1).reshape(N_CHIPS, DFF // N_CHIPS, D)
    (dh2,) = _mm_nt("d_mlp_in_mm", du_b, g_mlp_in, [F32])
    gw_mlp_in = _mm_tn("gw_mlp_in_mm", h2_b, du_b, N_CHIPS)

    def mid_bwd_fn(xt, o2t, dh2t, dx2t, *vecs):
        _, vjp = jax.vjp(mid, xt, o2t, *vecs)
        dx, do2, dgta, dgpa, dgpm, dscm, dshm = vjp((dx2t, dh2t))
        return (dx, do2), (dgta, dgpa, dgpm, dscm, dshm)

    (dx1, do2_b), (d_gt_a, d_g_post_attn, d_g_pre_mlp, d_sc_m, d_sh_m) = _rowwise(
        "mid_bwd", mid_bwd_fn, [x2d, o2, dh2, dx2], mid_vecs, [(D, F32), (D, BF16)], [D] * 5, TB)

    (d_on,) = _mm_nt("d_out_mm", do2_b, w_out_f, [F32])
    gw_out = _mm_tn("gw_out_mm", on_b, do2_b, 1).reshape(N_CHIPS, w_out.shape[1], D)

    def mix_bwd_fn(oa, ob, dont, ga, gb):
        _, vjp = jax.vjp(lambda a_, b_, g1, g2: mix_fn(a_, b_, g1, g2)[0][0], oa, ob, ga, gb)
        doa, dob, dga, dgb = vjp(dont)
        return (doa, dob), (dga, dgb)

    (doa_b, dob_b), (d_g_out_a, d_g_out_b) = _rowwise(
        "mix_bwd", mix_bwd_fn, [o_a, o_b, d_on], [g_out_a, g_out_b], [(W_QA, BF16), (HB * V_DIM, BF16)],
        [W_QA, HB * V_DIM], T)

    dqa, dka, dva = _attn_bwd("attn_a_bwd", qa_b, ka_b, va_b, doa_b, HA, HA // HKV, HEAD_DIM, HEAD_DIM, sc_a_, TQ)
    dqcat, dkcat, dvb = _attn_bwd("attn_b_bwd", qcat_b, kcat_b, vb_b, dob_b, HB, 1, 2 * LANE, V_DIM, sc_b_, TQ)

    def kcat_bwd_fn(dkc, dvt):
        dkv, dkpe = [], None
        for h in range(HB):
            dkv += [dkc[:, 2 * h * LANE:(2 * h + 1) * LANE], dvt[:, h * V_DIM:(h + 1) * V_DIM]]
            piece = dkc[:, (2 * h + 1) * LANE:(2 * h + 2) * LANE]
            dkpe = piece if dkpe is None else dkpe + piece
        return (jnp.concatenate(dkv, axis=1), dkpe), ()

    (dkv_b, dkpe), _ = _rowwise("kcat_bwd", kcat_bwd_fn, [dkcat, dvb], [], [(2 * HB * LANE, BF16), (LANE, F32)], [], T)
    (d_ckvn,) = _mm_nt("d_kv_mm", dkv_b, g_kvb, [F32])
    gw_kvb = _mm_tn("gw_kv_mm", ckvn_b, dkv_b, N_CHIPS)

    def prep_bwd_fn(pt, dqat, dkat, dvat, dqct, dckt, dkpt, ca_, sla, sha, cb_, slb, shb, gq, gk, gc):
        tabs = ((ca_, sla, sha), (cb_, slb, shb))
        _, vjp = jax.vjp(lambda sg, a1, a2, a3: _prep(sg, tabs, a1, a2, a3), _split_proj(pt), gq, gk, gc)
        (dqa_s, dka_s, dva_s, dqn_s, dqp_s, dck_s, dkp_s), dgq, dgk, dgc = vjp((dqat, dkat, dvat, dqct, dckt, dkpt))
        dproj = jnp.concatenate([*dqa_s, *dka_s, dva_s, *dqn_s, *dqp_s, dck_s, dkp_s], axis=1)
        return (dproj,), (dgq, dgk, dgc)

    (dproj_b,), (d_g_q_a, d_g_k_a, d_g_ckv) = _rowwise(
        "prep_bwd", prep_bwd_fn, [proj, dqa, dka, dva, dqcat, d_ckvn, dkpe, *tabs_a, *tabs_b],
        [g_q_a, g_k_a, g_ckv], [(W_INP, BF16)], [HEAD_DIM, HEAD_DIM, KV_RANK], TB)

    (dh,) = _mm_nt("d_proj_mm", dproj_b, wp, [F32])
    gwp = _mm_tn("gw_in_mm", h_b, dproj_b, 1)[0]
    gw_in = jnp.transpose(_unpermute_w_in(gwp).reshape(D, N_CHIPS, W_IN // N_CHIPS), (1, 0, 2))

    def pre_attn_bwd_fn(xt, dht, dx1t, g, sc, sh):
        _, vjp = jax.vjp(lambda a_, b_, c_, d_: pre_attn_fn(a_, b_, c_, d_)[0][0], xt, g, sc, sh)
        dxt, dg, dsc, dsh = vjp(dht)
        return (dx1t + dxt,), (dg, dsc, dsh)

    (grad_x,), (d_g_pre_attn, d_sc_a, d_sh_a) = _rowwise(
        "pre_attn_bwd", pre_attn_bwd_fn, [x2d, dh, dx1], [g_pre_attn, sc_a, sh_a], [(D, F32)], [D] * 3, TB)

    small_names = ["b_ada", "g_pre_attn", "g_q_a", "g_k_a", "g_ckv", "g_out_a", "g_out_b", "g_post_attn",
                   "g_pre_mlp", "g_post_mlp"]
    small_local = jnp.concatenate([d_sh_a, d_sc_a, d_gt_a, d_sh_m, d_sc_m, d_gt_m, d_g_pre_attn, d_g_q_a, d_g_k_a,
                                   d_g_ckv, d_g_out_a, d_g_out_b, d_g_post_attn, d_g_pre_mlp, d_g_post_mlp], axis=1)
    n_small = small_local.shape[1]
    small_all = _allgather8("gather_small", small_local.reshape(8, n_small // 8)).reshape(N_DEV, n_small)

    def sum8_fn(t):
        return (), (jnp.sum(t, axis=0, keepdims=True),)

    _, (small_sum,) = _rowwise("sum_small", sum8_fn, [small_all], [], [], [n_small], N_DEV)
    dmod_cols = lax.dynamic_slice(small_all, (0, j_me * n_mod), (N_DEV, n_mod))
    dmod_pad = jnp.concatenate([dmod_cols, jnp.zeros_like(dmod_cols)], axis=0).astype(BF16)
    gw_ada = _mm_tn("gw_ada_mm", ca_pad, dmod_pad, 1)[0]

    g_big = _reduce_grads([gw_in, gw_kvb, gw_out, gw_mlp_in, gw_mlp_out])

    weights = dict(w_ada=w_ada, b_ada=b_ada, g_pre_attn=g_pre_attn, w_in=w_in, g_q_a=g_q_a, g_k_a=g_k_a, g_ckv=g_ckv,
                   w_kv_b=w_kv_b, g_out_a=g_out_a, g_out_b=g_out_b, w_out=w_out, g_post_attn=g_post_attn,
                   g_pre_mlp=g_pre_mlp, w_mlp_in=w_mlp_in, w_mlp_out=w_mlp_out, g_post_mlp=g_post_mlp)
    ms = dict(w_ada=m_w_ada, b_ada=m_b_ada, g_pre_attn=m_g_pre_attn, w_in=m_w_in, g_q_a=m_g_q_a, g_k_a=m_g_k_a,
              g_ckv=m_g_ckv, w_kv_b=m_w_kv_b, g_out_a=m_g_out_a, g_out_b=m_g_out_b, w_out=m_w_out,
              g_post_attn=m_g_post_attn, g_pre_mlp=m_g_pre_mlp, w_mlp_in=m_w_mlp_in, w_mlp_out=m_w_mlp_out,
              g_post_mlp=m_g_post_mlp)
    vs = dict(w_ada=v_w_ada, b_ada=v_b_ada, g_pre_attn=v_g_pre_attn, w_in=v_w_in, g_q_a=v_g_q_a, g_k_a=v_g_k_a,
              g_ckv=v_g_ckv, w_kv_b=v_w_kv_b, g_out_a=v_g_out_a, g_out_b=v_g_out_b, w_out=v_w_out,
              g_post_attn=v_g_post_attn, g_pre_mlp=v_g_pre_mlp, w_mlp_in=v_w_mlp_in, w_mlp_out=v_w_mlp_out,
              g_post_mlp=v_g_post_mlp)
    order = list(weights)
    grads, deltas, new_m, new_v = {}, {}, {}, {}

    big_names = ["w_in", "w_kv_b", "w_out", "w_mlp_in", "w_mlp_out", "w_ada"]
    for name, g in zip(big_names, [*g_big, gw_ada]):
        d_, m_, v_ = _adamw("adamw_" + name, weights[name][0], g, ms[name][0], vs[name][0])
        grads[name], deltas[name], new_m[name], new_v[name] = g[None], d_[None], m_[None], v_[None]

    cat = lambda d: jnp.concatenate([d[n] for n in small_names], axis=1)
    d_, m_, v_ = _adamw("adamw_small", cat(weights), small_sum, cat(ms), cat(vs))
    off = 0
    for n in small_names:
        w_ = weights[n].shape[1]
        grads[n], deltas[n], new_m[n], new_v[n] = (a[:, off:off + w_] for a in (small_sum, d_, m_, v_))
        off += w_

    loss = lax.psum(loss_v[0, 0], ("x", "y", "c"))
    return (loss, grad_x[None], *[grads[n] for n in order], *[deltas[n] for n in order],
            *[new_m[n] for n in order], *[new_v[n] for n in order])
```

```python
import functools
import math

import numpy as np
import jax
import jax.numpy as jnp
from jax import lax
from jax.experimental import pallas as pl
from jax.experimental.pallas import tpu as pltpu

F32 = jnp.float32
BF16 = jnp.bfloat16
MESH = pl.DeviceIdType.MESH

EPS = 1e-6
GRID_W = 64
ROPE_THETA = 10000.0
HEAD_DIM = 128
HA = 8
HKV = 2
HB = 8
QK_NOPE = 128
QK_ROPE = 64
V_DIM = 128
KV_RANK = 512
QK_B = QK_NOPE + QK_ROPE
W_QA = HA * HEAD_DIM
W_KA = HKV * HEAD_DIM
W_QB = HB * QK_B
W_IN = W_QA + 2 * W_KA + W_QB + KV_RANK + QK_ROPE
N_CHIPS = 4
N_DEV = 8

LANE = 128
O_QA = 0
O_KA = O_QA + W_QA
O_VA = O_KA + W_KA
O_QN = O_VA + W_KA
O_QP = O_QN + HB * QK_NOPE
O_CKV = O_QP + HB * LANE
O_KPE = O_CKV + KV_RANK
W_INP = O_KPE + LANE

ADAM_LR = 0.001
ADAM_B1 = 0.9
ADAM_B2 = 0.999
ADAM_EPS = 1e-08
ADAM_WD = 0.01
ADAM_STEP = 10

VMEM_LIMIT = 56 * 1024 * 1024


def _params(sem):
    return pltpu.CompilerParams(dimension_semantics=sem, vmem_limit_bytes=VMEM_LIMIT)


def _pick(dim, target, unit):
    best = None
    t = unit
    while t <= min(dim, target):
        if dim % t == 0:
            best = t
        t += unit
    return dim if best is None else best


def _rn(x):
    return x * lax.rsqrt(jnp.mean(x * x, axis=-1, keepdims=True) + EPS)


def _rowwise(name, fn, rows, vecs, row_outs, vec_outs, tile):
    S = rows[0].shape[0]
    assert S % tile == 0
    n_r, n_v, n_ro, n_vo = len(rows), len(vecs), len(row_outs), len(vec_outs)

    def body(*refs):
        r_in = refs[:n_r]
        v_in = refs[n_r:n_r + n_v]
        r_out = refs[n_r + n_v:n_r + n_v + n_ro]
        v_out = refs[n_r + n_v + n_ro:]
        ro, vo = fn(*[r[...] for r in r_in], *[v[...] for v in v_in])
        for ref, val in zip(r_out, ro):
            ref[...] = val.astype(ref.dtype)
        if n_vo:
            @pl.when(pl.program_id(0) == 0)
            def _():
                for ref in v_out:
                    ref[...] = jnp.zeros(ref.shape, ref.dtype)
            for ref, val in zip(v_out, vo):
                ref[...] += val

    in_specs = [pl.BlockSpec((tile, a.shape[1]), lambda i: (i, 0)) for a in rows]
    in_specs += [pl.BlockSpec(a.shape, lambda i: (0, 0)) for a in vecs]
    out_specs = [pl.BlockSpec((tile, w), lambda i: (i, 0)) for w, _ in row_outs]
    out_specs += [pl.BlockSpec((1, w), lambda i: (0, 0)) for w in vec_outs]
    out_shape = [jax.ShapeDtypeStruct((S, w), dt) for w, dt in row_outs]
    out_shape += [jax.ShapeDtypeStruct((1, w), F32) for w in vec_outs]
    res = pl.pallas_call(
        body, name=name, grid=(S // tile,), in_specs=in_specs, out_specs=out_specs, out_shape=out_shape,
        compiler_params=_params(("arbitrary",)),
    )(*rows, *vecs)
    return res[:n_ro], res[n_ro:]


def _mm_nn(name, a, b3, out_dtypes, epilogue=None, extras=()):
    M, K = a.shape
    nB, K2, Nsh = b3.shape
    assert K == K2
    N = nB * Nsh
    tm, tn, tk = _pick(M, 1024, 16), _pick(Nsh, 1536, LANE), _pick(K, 512, LANE)
    nps, nk, ne, no = Nsh // tn, K // tk, len(extras), len(out_dtypes)

    def body(a_ref, b_ref, *rest):
        e_refs, o_refs, acc = rest[:ne], rest[ne:ne + no], rest[-1]
        k = pl.program_id(2)

        @pl.when(k == 0)
        def _():
            acc[...] = jnp.zeros(acc.shape, F32)

        acc[...] += jnp.dot(a_ref[...].astype(BF16), b_ref[...].astype(BF16), preferred_element_type=F32)

        @pl.when(k == nk - 1)
        def _():
            vals = (acc[...],) if epilogue is None else epilogue(acc[...], *[e[...] for e in e_refs])
            for ref, val in zip(o_refs, vals):
                ref[...] = val.astype(ref.dtype)

    in_specs = [pl.BlockSpec((tm, tk), lambda i, j, k: (i, k)),
                pl.BlockSpec((None, tk, tn), lambda i, j, k: (j // nps, k, j % nps))]
    in_specs += [pl.BlockSpec((tm, tn), lambda i, j, k: (i, j)) for _ in extras]
    return pl.pallas_call(
        body, name=name, grid=(M // tm, N // tn, nk), in_specs=in_specs,
        out_specs=[pl.BlockSpec((tm, tn), lambda i, j, k: (i, j)) for _ in out_dtypes],
        out_shape=[jax.ShapeDtypeStruct((M, N), dt) for dt in out_dtypes],
        scratch_shapes=[pltpu.VMEM((tm, tn), F32)],
        compiler_params=_params(("parallel", "parallel", "arbitrary")),
    )(a, b3, *extras)


def _mm_nt(name, a, b3, out_dtypes, epilogue=None, extras=()):
    M, N = a.shape
    nB, K, Nsh = b3.shape
    assert N == nB * Nsh
    tm, to, tn = _pick(M, 1024, 16), _pick(K, 1024, LANE), _pick(Nsh, 512, LANE)
    nps, nn, ne, no = Nsh // tn, N // tn, len(extras), len(out_dtypes)

    def body(a_ref, b_ref, *rest):
        e_refs, o_refs, acc = rest[:ne], rest[ne:ne + no], rest[-1]
        n = pl.program_id(2)

        @pl.when(n == 0)
        def _():
            acc[...] = jnp.zeros(acc.shape, F32)

        acc[...] += lax.dot_general(a_ref[...].astype(BF16), b_ref[...].astype(BF16),
                                    (((1,), (1,)), ((), ())), preferred_element_type=F32)

        @pl.when(n == nn - 1)
        def _():
            vals = (acc[...],) if epilogue is None else epilogue(acc[...], *[e[...] for e in e_refs])
            for ref, val in zip(o_refs, vals):
                ref[...] = val.astype(ref.dtype)

    in_specs = [pl.BlockSpec((tm, tn), lambda i, o, n: (i, n)),
                pl.BlockSpec((None, to, tn), lambda i, o, n: (n // nps, o, n % nps))]
    in_specs += [pl.BlockSpec((tm, to), lambda i, o, n: (i, o)) for _ in extras]
    return pl.pallas_call(
        body, name=name, grid=(M // tm, K // to, nn), in_specs=in_specs,
        out_specs=[pl.BlockSpec((tm, to), lambda i, o, n: (i, o)) for _ in out_dtypes],
        out_shape=[jax.ShapeDtypeStruct((M, K), dt) for dt in out_dtypes],
        scratch_shapes=[pltpu.VMEM((tm, to), F32)],
        compiler_params=_params(("parallel", "parallel", "arbitrary")),
    )(a, b3, *extras)


def _mm_tn(name, a, g, nB):
    M, K = a.shape
    M2, N = g.shape
    assert M == M2 and N % nB == 0
    Nsh = N // nB
    tk, tn, tm = _pick(K, 1024, LANE), _pick(Nsh, 1024, LANE), _pick(M, 512, 16)
    nps, nm = Nsh // tn, M // tm

    def body(a_ref, g_ref, o_ref, acc):
        m = pl.program_id(2)

        @pl.when(m == 0)
        def _():
            acc[...] = jnp.zeros(acc.shape, F32)

        acc[...] += lax.dot_general(a_ref[...].astype(BF16), g_ref[...].astype(BF16),
                                    (((0,), (0,)), ((), ())), preferred_element_type=F32)

        @pl.when(m == nm - 1)
        def _():
            o_ref[...] = acc[...]

    return pl.pallas_call(
        body, name=name, grid=(K // tk, N // tn, nm),
        in_specs=[pl.BlockSpec((tm, tk), lambda k, n, m: (m, k)),
                  pl.BlockSpec((tm, tn), lambda k, n, m: (m, n))],
        out_specs=pl.BlockSpec((None, tk, tn), lambda k, n, m: (n // nps, k, n % nps)),
        out_shape=jax.ShapeDtypeStruct((nB, K, Nsh), F32),
        scratch_shapes=[pltpu.VMEM((tk, tn), F32)],
        compiler_params=_params(("parallel", "parallel", "arbitrary")),
    )(a, g)


def _softmax(s):
    m = jnp.max(s, axis=-1, keepdims=True)
    e = jnp.exp(s - m)
    return e * (1.0 / jnp.sum(e, axis=-1, keepdims=True))


def _attn_fwd(name, q, k, v, n_heads, group, dqk, dv, scale, tq):
    S = q.shape[0]

    def body(q_ref, k_ref, v_ref, o_ref):
        s = lax.dot_general(q_ref[...], k_ref[...], (((1,), (1,)), ((), ())), preferred_element_type=F32) * scale
        p = _softmax(s)
        o_ref[...] = jnp.dot(p.astype(BF16), v_ref[...], preferred_element_type=F32)

    return pl.pallas_call(
        body, name=name, grid=(n_heads, S // tq),
        in_specs=[pl.BlockSpec((tq, dqk), lambda h, i: (i, h)),
                  pl.BlockSpec((S, dqk), lambda h, i: (0, h // group)),
                  pl.BlockSpec((S, dv), lambda h, i: (0, h // group))],
        out_specs=pl.BlockSpec((tq, dv), lambda h, i: (i, h)),
        out_shape=jax.ShapeDtypeStruct((S, n_heads * dv), F32),
        compiler_params=_params(("arbitrary", "arbitrary")),
    )(q, k, v)


def _attn_bwd(name, q, k, v, do, n_heads, group, dqk, dv, scale, tq):
    S = q.shape[0]
    n_kv = n_heads // group

    def body(q_ref, k_ref, v_ref, do_ref, dq_ref, dk_ref, dv_ref):
        h, i = pl.program_id(0), pl.program_id(1)

        @pl.when((h % group == 0) & (i == 0))
        def _():
            dk_ref[...] = jnp.zeros(dk_ref.shape, F32)
            dv_ref[...] = jnp.zeros(dv_ref.shape, F32)

        qb, kb, vb, dob = q_ref[...], k_ref[...], v_ref[...], do_ref[...]
        s = lax.dot_general(qb, kb, (((1,), (1,)), ((), ())), preferred_element_type=F32) * scale
        p = _softmax(s)
        dv_ref[...] += lax.dot_general(p.astype(BF16), dob, (((0,), (0,)), ((), ())), preferred_element_type=F32)
        dp = lax.dot_general(dob, vb, (((1,), (1,)), ((), ())), preferred_element_type=F32)
        ds = (p * (dp - jnp.sum(dp * p, axis=-1, keepdims=True)) * scale).astype(BF16)
        dq_ref[...] = jnp.dot(ds, kb, preferred_element_type=F32)
        dk_ref[...] += lax.dot_general(ds, qb, (((0,), (0,)), ((), ())), preferred_element_type=F32)

    return pl.pallas_call(
        body, name=name, grid=(n_heads, S // tq),
        in_specs=[pl.BlockSpec((tq, dqk), lambda h, i: (i, h)),
                  pl.BlockSpec((S, dqk), lambda h, i: (0, h // group)),
                  pl.BlockSpec((S, dv), lambda h, i: (0, h // group)),
                  pl.BlockSpec((tq, dv), lambda h, i: (i, h))],
        out_specs=[pl.BlockSpec((tq, dqk), lambda h, i: (i, h)),
                   pl.BlockSpec((S, dqk), lambda h, i: (0, h // group)),
                   pl.BlockSpec((S, dv), lambda h, i: (0, h // group))],
        out_shape=[jax.ShapeDtypeStruct((S, n_heads * dqk), F32),
                   jax.ShapeDtypeStruct((S, n_kv * dqk), F32),
                   jax.ShapeDtypeStruct((S, n_kv * dv), F32)],
        compiler_params=_params(("arbitrary", "arbitrary")),
    )(q, k, v, do)


def _adamw(name, w, g, m, v):
    R, C = w.shape
    tile = _pick(R, max(8, (1 << 19) // C // 8 * 8), 8)
    c1 = 1.0 - ADAM_B1 ** ADAM_STEP
    c2 = 1.0 - ADAM_B2 ** ADAM_STEP

    def body(w_ref, g_ref, m_ref, v_ref, d_ref, nm_ref, nv_ref):
        gg = g_ref[...]
        nm = ADAM_B1 * m_ref[...] + (1.0 - ADAM_B1) * gg
        nv = ADAM_B2 * v_ref[...] + (1.0 - ADAM_B2) * (gg * gg)
        d_ref[...] = -ADAM_LR * ((nm / c1) / (jnp.sqrt(nv / c2) + ADAM_EPS) + ADAM_WD * w_ref[...])
        nm_ref[...] = nm
        nv_ref[...] = nv

    spec = pl.BlockSpec((tile, C), lambda i: (i, 0))
    return pl.pallas_call(
        body, name=name, grid=(R // tile,), in_specs=[spec] * 4, out_specs=[spec] * 3,
        out_shape=[jax.ShapeDtypeStruct((R, C), F32)] * 3,
        compiler_params=_params(("parallel",)),
    )(w, g, m, v)


def _flip(me, rel):
    return tuple(1 - p if r else p for p, r in zip(me, rel))


def _push(name, ins, out_shapes, transfers, aliases=None):
    ni, no, nt = len(ins), len(out_shapes), len(transfers)

    def body(*refs):
        in_refs, out_refs = refs[:ni], refs[ni:ni + no]
        send_sems, recv_sems = refs[ni + no], refs[ni + no + 1]
        me = (lax.axis_index("x"), lax.axis_index("y"), lax.axis_index("c"))
        copies = []
        for t, (si, sfn, oi, dfn, rel) in enumerate(transfers):
            src_ref = out_refs[si[1]] if isinstance(si, tuple) else in_refs[si]
            src = src_ref.at[sfn(*me)]
            dst = out_refs[oi].at[dfn(*me)]
            if rel is None:
                cp = pltpu.make_async_copy(src, dst, send_sems.at[t])
            else:
                cp = pltpu.make_async_remote_copy(src_ref=src, dst_ref=dst, send_sem=send_sems.at[t],
                                                  recv_sem=recv_sems.at[t], device_id=_flip(me, rel),
                                                  device_id_type=MESH)
            cp.start()
            copies.append(cp)
        for cp in copies:
            cp.wait()

    any_spec = pl.BlockSpec(memory_space=pl.ANY)
    return pl.pallas_call(
        body, name=name, in_specs=[any_spec] * ni, out_specs=[any_spec] * no, out_shape=out_shapes,
        scratch_shapes=[pltpu.SemaphoreType.DMA((nt,)), pltpu.SemaphoreType.DMA((nt,))],
        input_output_aliases=aliases or {},
        compiler_params=pltpu.CompilerParams(has_side_effects=True),
    )(*ins)


_HBM = pl.BlockSpec(memory_space=pltpu.HBM)
_SEM = pl.BlockSpec(memory_space=pltpu.SEMAPHORE)
_DATAFLOW = pltpu.SideEffectType.DATAFLOW_SIDE_EFFECTING


def _split_copies(refs, transfers, send_sems, recv_sems):
    me = (lax.axis_index("x"), lax.axis_index("y"), lax.axis_index("c"))
    return [pltpu.make_async_remote_copy(src_ref=refs[b].at[sfn(*me)], dst_ref=refs[b].at[dfn(*me)],
                                         send_sem=send_sems.at[t], recv_sem=recv_sems.at[t],
                                         device_id=_flip(me, rel), device_id_type=MESH)
            for t, (b, sfn, dfn, rel) in enumerate(transfers)]


def _push_start(name, bufs, transfers):
    nb, nt = len(bufs), len(transfers)

    def body(*refs):
        send_sems, recv_sems, token = refs[nb], refs[nb + 1], refs[-1]
        for cp in _split_copies(refs[nb + 2:2 * nb + 2], transfers, send_sems, recv_sems):
            cp.start()
        token[...] = jnp.zeros(token.shape, token.dtype)

    res = pl.pallas_call(
        body, name=name,
        out_shape=(pltpu.SemaphoreType.DMA((nt,)), pltpu.SemaphoreType.DMA((nt,)),
                   *[pltpu.HBM(b.shape, b.dtype) for b in bufs], jax.ShapeDtypeStruct((8, LANE), F32)),
        in_specs=[_HBM] * nb, out_specs=(_SEM, _SEM, *[_HBM] * nb, pl.BlockSpec(memory_space=pltpu.VMEM)),
        input_output_aliases={i: i + 2 for i in range(nb)},
        compiler_params=pltpu.CompilerParams(has_side_effects=_DATAFLOW),
    )(*[pltpu.with_memory_space_constraint(b, pltpu.HBM) for b in bufs])
    return res[0], res[1], list(res[2:2 + nb]), res[-1]


def _push_wait(name, send_sems, recv_sems, bufs, transfers, after):
    nb = len(bufs)

    def body(*refs):
        for cp in _split_copies(refs[:nb], transfers, refs[nb], refs[nb + 1]):
            cp.wait_send()
            cp.wait_recv()

    res = pl.pallas_call(
        body, name=name, out_shape=[pltpu.HBM(b.shape, b.dtype) for b in bufs],
        in_specs=[_HBM] * nb + [_SEM, _SEM, pl.BlockSpec(memory_space=pl.ANY)], out_specs=[_HBM] * nb,
        input_output_aliases={i: i for i in range(nb)},
        compiler_params=pltpu.CompilerParams(has_side_effects=_DATAFLOW),
    )(*bufs, send_sems, recv_sems, after)
    return list(res)


ICI_RELS = ((1, 0, 0), (0, 1, 0), (1, 1, 0))
SIBLING = (0, 0, 1)


def _chip(x, y):
    return 2 * x + y


def _cast_into_slot(name, where, w):
    R, C = w.shape
    tile = _pick(R, 256, 16)

    def body(where_ref, w_ref, o_ref):
        o_ref[...] = w_ref[...].astype(BF16)

    return pl.pallas_call(
        body, name=name,
        grid_spec=pltpu.PrefetchScalarGridSpec(
            num_scalar_prefetch=1, grid=(R // tile,),
            in_specs=[pl.BlockSpec((tile, C), lambda i, wh: (i, 0))],
            out_specs=pl.BlockSpec((None, tile, C), lambda i, wh: (wh[1], i, 0))),
        out_shape=jax.ShapeDtypeStruct((N_CHIPS, R, C), BF16),
        compiler_params=_params(("parallel",)),
    )(where, w)


def _gather_start(tag, where, shards):
    n = len(shards)
    own = [_cast_into_slot(f"cast_w_{tag}{w}", where, shards[w]) for w in range(n)]
    transfers = []
    for w in range(n):
        def idx(x, y, c, rh=shards[w].shape[0] // 2):
            return (_chip(x, y), pl.ds(c * rh, rh), slice(None))
        transfers += [(w, idx, idx, rel) for rel in ICI_RELS]
    send_sems, recv_sems, bufs, token = _push_start(f"gather_{tag}_start", own, transfers)
    return (tag, send_sems, recv_sems, bufs, transfers), token


def _gather_finish(handle, after):
    tag, send_sems, recv_sems, bufs, transfers = handle
    part = _push_wait(f"gather_{tag}_wait", send_sems, recv_sems, bufs, transfers, after)
    n = len(part)
    halves = [p.shape[1] // 2 for p in part]
    outs = [jax.ShapeDtypeStruct(p.shape, p.dtype) for p in part]

    second = []
    for w in range(n):
        rh = halves[w]
        for rel in ICI_RELS:
            def idx(x, y, c, rh=rh, rel=rel):
                return (_chip(x ^ rel[0], y ^ rel[1]), pl.ds(c * rh, rh), slice(None))
            second.append((("out", w), idx, w, idx, SIBLING))
    return _push(f"gather_{tag}_d2d", list(part), outs, second, aliases={w: w for w in range(n)})


def _reduce_grads(grads):
    n = len(grads)
    c_me = lax.axis_index("c")
    j_me = _chip(lax.axis_index("x"), lax.axis_index("y"))
    where = jnp.stack([c_me, j_me]).astype(jnp.int32)
    halves = [g.shape[1] // 2 for g in grads]

    t1 = []
    for w in range(n):
        rh = halves[w]
        t1.append((w, lambda x, y, c, rh=rh: (slice(None), pl.ds((1 - c) * rh, rh), slice(None)), w,
                   lambda x, y, c: (slice(None), slice(None), slice(None)), SIBLING))
    from_sib = _push("reduce_g_d2d", list(grads),
                     [jax.ShapeDtypeStruct((N_CHIPS, halves[w], grads[w].shape[2]), F32) for w in range(n)], t1)

    pair = [_pair_sum(f"pair_sum_{w}", where, grads[w], from_sib[w]) for w in range(n)]

    t3 = []
    for w in range(n):
        for r, rel in enumerate(ICI_RELS):
            t3.append((w, lambda x, y, c, rel=rel: (_chip(x ^ rel[0], y ^ rel[1]), slice(None), slice(None)), w,
                       lambda x, y, c, r=r: (r, slice(None), slice(None)), rel))
    from_chips = _push("reduce_g_ici", pair,
                       [jax.ShapeDtypeStruct((3, halves[w], grads[w].shape[2]), BF16) for w in range(n)], t3)

    mine = [_chip_sum(f"chip_sum_{w}", where, grads[w], from_sib[w], from_chips[w]) for w in range(n)]

    t5 = []
    for w in range(n):
        def idx(x, y, c, rh=halves[w]):
            return (pl.ds(c * rh, rh), slice(None))
        t5.append((("out", w), idx, w, idx, SIBLING))
    return _push("reduce_g_swap", mine,
                 [jax.ShapeDtypeStruct((2 * halves[w], grads[w].shape[2]), F32) for w in range(n)], t5,
                 aliases={w: w for w in range(n)})


def _pair_sum(name, where, g, from_sib):
    _, R, C = g.shape
    rh = R // 2
    tile = _pick(rh, max(16, (1 << 19) // C // 16 * 16), 16)
    nb = rh // tile

    def body(where_ref, g_ref, s_ref, o_ref):
        o_ref[...] = (g_ref[...] + s_ref[...]).astype(BF16)

    return pl.pallas_call(
        body, name=name,
        grid_spec=pltpu.PrefetchScalarGridSpec(
            num_scalar_prefetch=1, grid=(N_CHIPS, nb),
            in_specs=[pl.BlockSpec((None, tile, C), lambda j, i, wh: (j, wh[0] * nb + i, 0)),
                      pl.BlockSpec((None, tile, C), lambda j, i, wh: (j, i, 0))],
            out_specs=pl.BlockSpec((None, tile, C), lambda j, i, wh: (j, i, 0))),
        out_shape=jax.ShapeDtypeStruct((N_CHIPS, rh, C), BF16),
        compiler_params=_params(("parallel", "parallel")),
    )(where, g, from_sib)


def _chip_sum(name, where, g, from_sib, from_chips):
    _, R, C = g.shape
    rh = R // 2
    tile = _pick(rh, max(16, (1 << 19) // C // 16 * 16), 16)
    nb = rh // tile

    def body(where_ref, g_ref, s_ref, r_ref, o_ref):
        acc = g_ref[...] + s_ref[...]
        for r in range(3):
            acc = acc + r_ref[r].astype(F32)
        o_ref[...] = acc

    return pl.pallas_call(
        body, name=name,
        grid_spec=pltpu.PrefetchScalarGridSpec(
            num_scalar_prefetch=1, grid=(nb,),
            in_specs=[pl.BlockSpec((None, tile, C), lambda i, wh: (wh[1], wh[0] * nb + i, 0)),
                      pl.BlockSpec((None, tile, C), lambda i, wh: (wh[1], i, 0)),
                      pl.BlockSpec((3, tile, C), lambda i, wh: (0, i, 0))],
            out_specs=pl.BlockSpec((tile, C), lambda i, wh: (wh[0] * nb + i, 0))),
        out_shape=jax.ShapeDtypeStruct((R, C), F32),
        compiler_params=_params(("parallel",)),
    )(where, g, from_sib, from_chips)


def _allgather8(name, blk):
    m_per, n = blk.shape
    rels = [(0, 0, 1), (1, 0, 0), (0, 1, 0), (1, 1, 0), (1, 0, 1), (0, 1, 1), (1, 1, 1)]

    def body(x_ref, out_ref, send_sems, recv_sems):
        me = (lax.axis_index("x"), lax.axis_index("y"), lax.axis_index("c"))
        my_rows = out_ref.at[pl.ds((4 * me[0] + 2 * me[1] + me[2]) * m_per, m_per), :]
        out_ref[pl.ds((4 * me[0] + 2 * me[1] + me[2]) * m_per, m_per), :] = x_ref[...]
        copies = []
        for t, rel in enumerate(rels):
            cp = pltpu.make_async_remote_copy(src_ref=x_ref, dst_ref=my_rows, send_sem=send_sems.at[t],
                                              recv_sem=recv_sems.at[t], device_id=_flip(me, rel),
                                              device_id_type=MESH)
            cp.start()
            copies.append(cp)
        for cp in copies:
            cp.wait()

    return pl.pallas_call(
        body, name=name, out_shape=jax.ShapeDtypeStruct((N_DEV * m_per, n), blk.dtype),
        in_specs=[pl.BlockSpec(memory_space=pltpu.VMEM)], out_specs=pl.BlockSpec(memory_space=pltpu.VMEM),
        scratch_shapes=[pltpu.SemaphoreType.DMA((7,)), pltpu.SemaphoreType.DMA((7,))],
        compiler_params=pltpu.CompilerParams(has_side_effects=True, vmem_limit_bytes=VMEM_LIMIT),
    )(blk)


def _rope_tables(seq_len, dim, reps):
    rows = seq_len // GRID_W
    t = np.arange(seq_len)
    row, col = (t // GRID_W).astype(np.float32), (t % GRID_W).astype(np.float32)
    half = dim // 2
    inv = jnp.asarray(ROPE_THETA, F32) ** (-jnp.arange(0, half, 2, dtype=F32) / half)
    ang_r = jnp.asarray(row)[:, None] * inv[None, :]
    ang_c = jnp.asarray(col)[:, None] * inv[None, :]
    ang = jnp.concatenate([ang_r, ang_r, ang_c, ang_c], axis=-1)
    cos, sin = jnp.cos(ang), jnp.sin(ang)
    low = (np.arange(dim) % (dim // 2)) < (dim // 4)
    s_lo = jnp.where(jnp.asarray(low)[None, :], -sin, 0.0)
    s_hi = jnp.where(jnp.asarray(low)[None, :], 0.0, sin)
    del rows
    return tuple(jnp.tile(a, (1, reps)) for a in (cos, s_lo, s_hi))


def _make_rope(shift):
    def up(v):
        return pltpu.roll(v, LANE - shift, 1)

    def down(v):
        return pltpu.roll(v, shift, 1)

    @jax.custom_vjp
    def rope(v, cos, s_lo, s_hi):
        return v * cos + up(v) * s_lo + down(v) * s_hi

    def fwd(v, cos, s_lo, s_hi):
        return rope(v, cos, s_lo, s_hi), (cos, s_lo, s_hi)

    def bwd(res, dy):
        cos, s_lo, s_hi = res
        return dy * cos + down(dy * s_lo) + up(dy * s_hi), jnp.zeros_like(cos), jnp.zeros_like(cos), jnp.zeros_like(cos)

    rope.defvjp(fwd, bwd)
    return rope


_rope_a = _make_rope(HEAD_DIM // 4)
_rope_b = _make_rope(QK_ROPE // 4)


def _prep(segs, tabs, g_q, g_k, g_ckv):
    qa, ka, va, qn, qp, ckv, kpe = segs
    ta, tb = tabs
    q_a = jnp.concatenate([_rope_a(_rn(s) * g_q, *ta) for s in qa], axis=1)
    k_a = jnp.concatenate([_rope_a(_rn(s) * g_k, *ta) for s in ka], axis=1)
    cat = []
    for h in range(HB):
        cat += [qn[h], _rope_b(qp[h], *tb)]
    q_cat = jnp.concatenate(cat, axis=1)
    return q_a, k_a, va, q_cat, _rn(ckv) * g_ckv, _rope_b(kpe, *tb)


def _split_proj(proj):
    qa = [proj[:, O_QA + h * LANE:O_QA + (h + 1) * LANE] for h in range(HA)]
    ka = [proj[:, O_KA + h * LANE:O_KA + (h + 1) * LANE] for h in range(HKV)]
    va = proj[:, O_VA:O_QN]
    qn = [proj[:, O_QN + h * LANE:O_QN + (h + 1) * LANE] for h in range(HB)]
    qp = [proj[:, O_QP + h * LANE:O_QP + (h + 1) * LANE] for h in range(HB)]
    return qa, ka, va, qn, qp, proj[:, O_CKV:O_KPE], proj[:, O_KPE:W_INP]


def _permute_w_in(nat):
    K = nat.shape[0]
    qb = nat[:, W_QA + 2 * W_KA:W_QA + 2 * W_KA + W_QB].reshape(K, HB, QK_B)
    z = lambda w: jnp.zeros((K, w), nat.dtype)
    qp = jnp.concatenate([qb[:, :, QK_NOPE:], jnp.zeros((K, HB, LANE - QK_ROPE), nat.dtype)], axis=2)
    tail = nat[:, W_QA + 2 * W_KA + W_QB:]
    return jnp.concatenate([nat[:, :O_QN], qb[:, :, :QK_NOPE].reshape(K, HB * QK_NOPE), qp.reshape(K, HB * LANE),
                            tail, z(LANE - QK_ROPE)], axis=1)


def _unpermute_w_in(p):
    K = p.shape[0]
    qn = p[:, O_QN:O_QP].reshape(K, HB, QK_NOPE)
    qp = p[:, O_QP:O_CKV].reshape(K, HB, LANE)[:, :, :QK_ROPE]
    qb = jnp.concatenate([qn, qp], axis=2).reshape(K, W_QB)
    return jnp.concatenate([p[:, :O_QN], qb, p[:, O_CKV:O_KPE + QK_ROPE]], axis=1)


def kernel(x, c, w_ada, b_ada, g_pre_attn, w_in, g_q_a, g_k_a, g_ckv, w_kv_b, g_out_a, g_out_b, w_out, g_post_attn, g_pre_mlp, w_mlp_in, w_mlp_out, g_post_mlp, loss_target, m_w_ada, m_b_ada, m_g_pre_attn, m_w_in, m_g_q_a, m_g_k_a, m_g_ckv, m_w_kv_b, m_g_out_a, m_g_out_b, m_w_out, m_g_post_attn, m_g_pre_mlp, m_w_mlp_in, m_w_mlp_out, m_g_post_mlp, v_w_ada, v_b_ada, v_g_pre_attn, v_w_in, v_g_q_a, v_g_k_a, v_g_ckv, v_w_kv_b, v_g_out_a, v_g_out_b, v_w_out, v_g_post_attn, v_g_pre_mlp, v_w_mlp_in, v_w_mlp_out, v_g_post_mlp):
    S, D = x.shape[1], x.shape[2]
    x2d, tgt = x[0], loss_target[0]
    ix, iy, ic = lax.axis_index("x"), lax.axis_index("y"), lax.axis_index("c")
    j_me = _chip(ix, iy)
    e_me = 4 * ix + 2 * iy + ic
    T = _pick(S, 256, 8)
    TB = _pick(S, 128, 8)

    where = jnp.stack([ic, j_me]).astype(jnp.int32)
    h_qkv, tok_a = _gather_start("qkv", where, [w_in[0], w_kv_b[0]])
    h_out, tok_b = _gather_start("out", where, [w_out[0]])
    h_up, tok_c = _gather_start("up", where, [w_mlp_in[0]])
    h_down, tok_d = _gather_start("down", where, [w_mlp_out[0]])
    DFF = w_mlp_out.shape[1] * N_CHIPS
    started = tok_a[0, 0] + tok_b[0, 0] + tok_c[0, 0] + tok_d[0, 0]

    c_all = _allgather8("gather_c", (c + started).reshape(8, D // 8)).reshape(N_DEV, D)

    def silu_fn(cc):
        return (cc * (1.0 / (1.0 + jnp.exp(-cc))),), ()

    (c_act,), _ = _rowwise("silu_c", silu_fn, [c_all], [], [(D, BF16)], [], N_DEV)
    ca_pad = jnp.concatenate([c_act, jnp.zeros_like(c_act)], axis=0)
    n_mod = w_ada.shape[2]
    b_cols = lax.dynamic_slice(b_ada, (0, j_me * n_mod), (1, n_mod))
    (mod_part,) = _mm_nn("mod_mm", ca_pad, w_ada, [F32], epilogue=lambda acc, b: (acc + b,),
                         extras=(jnp.broadcast_to(b_cols, (16, n_mod)),))
    mod_all = _allgather8("gather_mod", mod_part[:8]).reshape(N_DEV, 8, n_mod)
    mod = jnp.concatenate([lax.dynamic_slice(mod_all, (2 * j, e_me, 0), (1, 1, n_mod))[0] for j in range(N_CHIPS)],
                          axis=1)
    sh_a, sc_a, gt_a, sh_m, sc_m, gt_m = [mod[:, i * D:(i + 1) * D] for i in range(6)]

    tabs_a = _rope_tables(S, HEAD_DIM, 1)
    tabs_b = _rope_tables(S, QK_ROPE, LANE // QK_ROPE)

    def pre_attn_fn(xt, g, sc, sh):
        return (_rn(xt) * g * (1.0 + sc) + sh,), ()

    (h_b,), _ = _rowwise("pre_attn", pre_attn_fn, [x2d], [g_pre_attn, sc_a, sh_a], [(D, BF16)], [], T)
    g_in, g_kvb = _gather_finish(h_qkv, h_b)
    wp = _permute_w_in(jnp.transpose(g_in, (1, 0, 2)).reshape(D, W_IN))[None]
    (proj,) = _mm_nn("proj_mm", h_b, wp, [F32])

    def prep_fn(pt, ca_, sla, sha, cb_, slb, shb, gq, gk, gc):
        return _prep(_split_proj(pt), ((ca_, sla, sha), (cb_, slb, shb)), gq, gk, gc), ()

    (qa_b, ka_b, va_b, qcat_b, ckvn_b, kpe_b), _ = _rowwise(
        "prep", prep_fn, [proj, *tabs_a, *tabs_b], [g_q_a, g_k_a, g_ckv],
        [(W_QA, BF16), (W_KA, BF16), (W_KA, BF16), (2 * HB * LANE, BF16), (KV_RANK, BF16), (LANE, BF16)], [], TB)
    (kv,) = _mm_nn("kv_mm", ckvn_b, g_kvb, [F32])

    def kcat_fn(kvt, kpet):
        kc, vs = [], []
        for h in range(HB):
            kc += [kvt[:, 2 * h * LANE:(2 * h + 1) * LANE], kpet.astype(F32)]
            vs.append(kvt[:, (2 * h + 1) * LANE:(2 * h + 2) * LANE])
        return (jnp.concatenate(kc, axis=1), jnp.concatenate(vs, axis=1)), ()

    (kcat_b, vb_b), _ = _rowwise("kcat", kcat_fn, [kv, kpe_b], [], [(2 * HB * LANE, BF16), (HB * V_DIM, BF16)], [], T)

    TQ = _pick(S, 256, 16)
    sc_a_ = 1.0 / math.sqrt(HEAD_DIM)
    sc_b_ = 1.0 / math.sqrt(QK_B)
    o_a = _attn_fwd("attn_a_fwd", qa_b, ka_b, va_b, HA, HA // HKV, HEAD_DIM, HEAD_DIM, sc_a_, TQ)
    o_b = _attn_fwd("attn_b_fwd", qcat_b, kcat_b, vb_b, HB, 1, 2 * LANE, V_DIM, sc_b_, TQ)

    def mix_fn(oa, ob, ga, gb):
        return (jnp.concatenate([_rn(oa) * ga, _rn(ob) * gb], axis=1),), ()

    (on_b,), _ = _rowwise("mix_norm", mix_fn, [o_a, o_b], [g_out_a, g_out_b], [(2 * W_QA, BF16)], [], T)
    (g_wout,) = _gather_finish(h_out, on_b)
    w_out_f = g_wout.reshape(1, N_CHIPS * g_wout.shape[1], D)
    (o2,) = _mm_nn("out_mm", on_b, w_out_f, [F32])

    def mid(xt, o2t, gta, gpa, gpm, scm, shm):
        x1 = xt + gta * (_rn(o2t) * gpa)
        return x1, _rn(x1) * gpm * (1.0 + scm) + shm

    def mid_fn(*a):
        return mid(*a), ()

    mid_vecs = [gt_a, g_post_attn, g_pre_mlp, sc_m, sh_m]
    (x1, h2_b), _ = _rowwise("mid", mid_fn, [x2d, o2], mid_vecs, [(D, F32), (D, BF16)], [], T)
    (g_mlp_in,) = _gather_finish(h_up, h2_b)
    u, a_b = _mm_nn("mlp_in_mm", h2_b, g_mlp_in, [F32, BF16],
                    epilogue=lambda acc: (acc, jnp.square(jnp.maximum(acc, 0.0))))
    (g_mlp_out,) = _gather_finish(h_down, a_b)
    w_mlp_out_f = g_mlp_out.reshape(1, DFF, D)
    (y,) = _mm_nn("mlp_out_mm", a_b, w_mlp_out_f, [F32])

    def last(yt, x1t, gtm, gpo):
        return x1t + gtm * (_rn(yt) * gpo)

    def last_fn(yt, x1t, tt, gtm, gpo):
        x2, vjp = jax.vjp(last, yt, x1t, gtm, gpo)
        err = x2 - tt
        dy, dx1, dgt, dg = vjp(err * (1.0 / D))
        loss = 0.5 * jnp.sum(jnp.mean(err * err, axis=-1))
        return (dy, dx1), (jnp.full((1, LANE), loss, F32), dgt, dg)

    (dy_b, dx2), (loss_v, d_gt_m, d_g_post_mlp) = _rowwise(
        "loss_bwd", last_fn, [y, x1, tgt], [gt_m, g_post_mlp], [(D, BF16), (D, F32)], [LANE, D, D], TB)

    (du_b,) = _mm_nt("d_mlp_out_mm", dy_b, w_mlp_out_f, [BF16],
                     epilogue=lambda acc, ut: (acc * (2.0 * jnp.maximum(ut, 0.0)),), extras=(u,))
    gw_mlp_out = _mm_tn("gw_mlp_out_mm", a_b, dy_b, 1).reshape(N_CHIPS, DFF // N_CHIPS, D)
    (dh2,) = _mm_nt("d_mlp_in_mm", du_b, g_mlp_in, [F32])
    gw_mlp_in = _mm_tn("gw_mlp_in_mm", h2_b, du_b, N_CHIPS)

    def mid_bwd_fn(xt, o2t, dh2t, dx2t, *vecs):
        _, vjp = jax.vjp(mid, xt, o2t, *vecs)
        dx, do2, dgta, dgpa, dgpm, dscm, dshm = vjp((dx2t, dh2t))
        return (dx, do2), (dgta, dgpa, dgpm, dscm, dshm)

    (dx1, do2_b), (d_gt_a, d_g_post_attn, d_g_pre_mlp, d_sc_m, d_sh_m) = _rowwise(
        "mid_bwd", mid_bwd_fn, [x2d, o2, dh2, dx2], mid_vecs, [(D, F32), (D, BF16)], [D] * 5, TB)

    (d_on,) = _mm_nt("d_out_mm", do2_b, w_out_f, [F32])
    gw_out = _mm_tn("gw_out_mm", on_b, do2_b, 1).reshape(N_CHIPS, w_out.shape[1], D)

    def mix_bwd_fn(oa, ob, dont, ga, gb):
        _, vjp = jax.vjp(lambda a_, b_, g1, g2: mix_fn(a_, b_, g1, g2)[0][0], oa, ob, ga, gb)
        doa, dob, dga, dgb = vjp(dont)
        return (doa, dob), (dga, dgb)

    (doa_b, dob_b), (d_g_out_a, d_g_out_b) = _rowwise(
        "mix_bwd", mix_bwd_fn, [o_a, o_b, d_on], [g_out_a, g_out_b], [(W_QA, BF16), (HB * V_DIM, BF16)],
        [W_QA, HB * V_DIM], T)

    dqa, dka, dva = _attn_bwd("attn_a_bwd", qa_b, ka_b, va_b, doa_b, HA, HA // HKV, HEAD_DIM, HEAD_DIM, sc_a_, TQ)
    dqcat, dkcat, dvb = _attn_bwd("attn_b_bwd", qcat_b, kcat_b, vb_b, dob_b, HB, 1, 2 * LANE, V_DIM, sc_b_, TQ)

    def kcat_bwd_fn(dkc, dvt):
        dkv, dkpe = [], None
        for h in range(HB):
            dkv += [dkc[:, 2 * h * LANE:(2 * h + 1) * LANE], dvt[:, h * V_DIM:(h + 1) * V_DIM]]
            piece = dkc[:, (2 * h + 1) * LANE:(2 * h + 2) * LANE]
            dkpe = piece if dkpe is None else dkpe + piece
        return (jnp.concatenate(dkv, axis=1), dkpe), ()

    (dkv_b, dkpe), _ = _rowwise("kcat_bwd", kcat_bwd_fn, [dkcat, dvb], [], [(2 * HB * LANE, BF16), (LANE, F32)], [], T)
    (d_ckvn,) = _mm_nt("d_kv_mm", dkv_b, g_kvb, [F32])
    gw_kvb = _mm_tn("gw_kv_mm", ckvn_b, dkv_b, N_CHIPS)

    def prep_bwd_fn(pt, dqat, dkat, dvat, dqct, dckt, dkpt, ca_, sla, sha, cb_, slb, shb, gq, gk, gc):
        tabs = ((ca_, sla, sha), (cb_, slb, shb))
        _, vjp = jax.vjp(lambda sg, a1, a2, a3: _prep(sg, tabs, a1, a2, a3), _split_proj(pt), gq, gk, gc)
        (dqa_s, dka_s, dva_s, dqn_s, dqp_s, dck_s, dkp_s), dgq, dgk, dgc = vjp((dqat, dkat, dvat, dqct, dckt, dkpt))
        dproj = jnp.concatenate([*dqa_s, *dka_s, dva_s, *dqn_s, *dqp_s, dck_s, dkp_s], axis=1)
        return (dproj,), (dgq, dgk, dgc)

    (dproj_b,), (d_g_q_a, d_g_k_a, d_g_ckv) = _rowwise(
        "prep_bwd", prep_bwd_fn, [proj, dqa, dka, dva, dqcat, d_ckvn, dkpe, *tabs_a, *tabs_b],
        [g_q_a, g_k_a, g_ckv], [(W_INP, BF16)], [HEAD_DIM, HEAD_DIM, KV_RANK], TB)

    (dh,) = _mm_nt("d_proj_mm", dproj_b, wp, [F32])
    gwp = _mm_tn("gw_in_mm", h_b, dproj_b, 1)[0]
    gw_in = jnp.transpose(_unpermute_w_in(gwp).reshape(D, N_CHIPS, W_IN // N_CHIPS), (1, 0, 2))

    def pre_attn_bwd_fn(xt, dht, dx1t, g, sc, sh):
        _, vjp = jax.vjp(lambda a_, b_, c_, d_: pre_attn_fn(a_, b_, c_, d_)[0][0], xt, g, sc, sh)
        dxt, dg, dsc, dsh = vjp(dht)
        return (dx1t + dxt,), (dg, dsc, dsh)

    (grad_x,), (d_g_pre_attn, d_sc_a, d_sh_a) = _rowwise(
        "pre_attn_bwd", pre_attn_bwd_fn, [x2d, dh, dx1], [g_pre_attn, sc_a, sh_a], [(D, F32)], [D] * 3, TB)

    small_names = ["b_ada", "g_pre_attn", "g_q_a", "g_k_a", "g_ckv", "g_out_a", "g_out_b", "g_post_attn",
                   "g_pre_mlp", "g_post_mlp"]
    small_local = jnp.concatenate([d_sh_a, d_sc_a, d_gt_a, d_sh_m, d_sc_m, d_gt_m, d_g_pre_attn, d_g_q_a, d_g_k_a,
                                   d_g_ckv, d_g_out_a, d_g_out_b, d_g_post_attn, d_g_pre_mlp, d_g_post_mlp], axis=1)
    n_small = small_local.shape[1]
    small_all = _allgather8("gather_small", small_local.reshape(8, n_small // 8)).reshape(N_DEV, n_small)

    def sum8_fn(t):
        return (), (jnp.sum(t, axis=0, keepdims=True),)

    _, (small_sum,) = _rowwise("sum_small", sum8_fn, [small_all], [], [], [n_small], N_DEV)
    dmod_cols = lax.dynamic_slice(small_all, (0, j_me * n_mod), (N_DEV, n_mod))
    dmod_pad = jnp.concatenate([dmod_cols, jnp.zeros_like(dmod_cols)], axis=0).astype(BF16)
    gw_ada = _mm_tn("gw_ada_mm", ca_pad, dmod_pad, 1)[0]

    g_big = _reduce_grads([gw_in, gw_kvb, gw_out, gw_mlp_in, gw_mlp_out])

    weights = dict(w_ada=w_ada, b_ada=b_ada, g_pre_attn=g_pre_attn, w_in=w_in, g_q_a=g_q_a, g_k_a=g_k_a, g_ckv=g_ckv,
                   w_kv_b=w_kv_b, g_out_a=g_out_a, g_out_b=g_out_b, w_out=w_out, g_post_attn=g_post_attn,
                   g_pre_mlp=g_pre_mlp, w_mlp_in=w_mlp_in, w_mlp_out=w_mlp_out, g_post_mlp=g_post_mlp)
    ms = dict(w_ada=m_w_ada, b_ada=m_b_ada, g_pre_attn=m_g_pre_attn, w_in=m_w_in, g_q_a=m_g_q_a, g_k_a=m_g_k_a,
              g_ckv=m_g_ckv, w_kv_b=m_w_kv_b, g_out_a=m_g_out_a, g_out_b=m_g_out_b, w_out=m_w_out,
              g_post_attn=m_g_post_attn, g_pre_mlp=m_g_pre_mlp, w_mlp_in=m_w_mlp_in, w_mlp_out=m_w_mlp_out,
              g_post_mlp=m_g_post_mlp)
    vs = dict(w_ada=v_w_ada, b_ada=v_b_ada, g_pre_attn=v_g_pre_attn, w_in=v_w_in, g_q_a=v_g_q_a, g_k_a=v_g_k_a,
              g_ckv=v_g_ckv, w_kv_b=v_w_kv_b, g_out_a=v_g_out_a, g_out_b=v_g_out_b, w_out=v_w_out,
              g_post_attn=v_g_post_attn, g_pre_mlp=v_g_pre_mlp, w_mlp_in=v_w_mlp_in, w_mlp_out=v_w_mlp_out,
              g_post_mlp=v_g_post_mlp)
    order = list(weights)
    grads, deltas, new_m, new_v = {}, {}, {}, {}

    big_names = ["w_in", "w_kv_b", "w_out", "w_mlp_in", "w_mlp_out", "w_ada"]
    for name, g in zip(big_names, [*g_big, gw_ada]):
        d_, m_, v_ = _adamw("adamw_" + name, weights[name][0], g, ms[name][0], vs[name][0])
        grads[name], deltas[name], new_m[name], new_v[name] = g[None], d_[None], m_[None], v_[None]

    cat = lambda d: jnp.concatenate([d[n] for n in small_names], axis=1)
    d_, m_, v_ = _adamw("adamw_small", cat(weights), small_sum, cat(ms), cat(vs))
    off = 0
    for n in small_names:
        w_ = weights[n].shape[1]
        grads[n], deltas[n], new_m[n], new_v[n] = (a[:, off:off + w_] for a in (small_sum, d_, m_, v_))
        off += w_

    loss = lax.psum(loss_v[0, 0], ("x", "y", "c"))
    return (loss, grad_x[None], *[grads[n] for n in order], *[deltas[n] for n in order],
            *[new_m[n] for n in order], *[new_v[n] for n in order])
```

```python
import functools
import math

import numpy as np
import jax
import jax.numpy as jnp
from jax import lax
from jax.experimental import pallas as pl
from jax.experimental.pallas import tpu as pltpu

F32 = jnp.float32
BF16 = jnp.bfloat16
MESH = pl.DeviceIdType.MESH

EPS = 1e-6
GRID_W = 64
ROPE_THETA = 10000.0
HEAD_DIM = 128
HA = 8
HKV = 2
HB = 8
QK_NOPE = 128
QK_ROPE = 64
V_DIM = 128
KV_RANK = 512
QK_B = QK_NOPE + QK_ROPE
W_QA = HA * HEAD_DIM
W_KA = HKV * HEAD_DIM
W_QB = HB * QK_B
W_IN = W_QA + 2 * W_KA + W_QB + KV_RANK + QK_ROPE
N_CHIPS = 4
N_DEV = 8

LANE = 128
O_QA = 0
O_KA = O_QA + W_QA
O_VA = O_KA + W_KA
O_QN = O_VA + W_KA
O_QP = O_QN + HB * QK_NOPE
O_CKV = O_QP + HB * LANE
O_KPE = O_CKV + KV_RANK
W_INP = O_KPE + LANE

ADAM_LR = 0.001
ADAM_B1 = 0.9
ADAM_B2 = 0.999
ADAM_EPS = 1e-08
ADAM_WD = 0.01
ADAM_STEP = 10

VMEM_LIMIT = 56 * 1024 * 1024


def _params(sem):
    return pltpu.CompilerParams(dimension_semantics=sem, vmem_limit_bytes=VMEM_LIMIT)


def _pick(dim, target, unit):
    best = None
    t = unit
    while t <= min(dim, target):
        if dim % t == 0:
            best = t
        t += unit
    return dim if best is None else best


def _rn(x):
    return x * lax.rsqrt(jnp.mean(x * x, axis=-1, keepdims=True) + EPS)


def _rowwise(name, fn, rows, vecs, row_outs, vec_outs, tile):
    S = rows[0].shape[0]
    assert S % tile == 0
    n_r, n_v, n_ro, n_vo = len(rows), len(vecs), len(row_outs), len(vec_outs)

    def body(*refs):
        r_in = refs[:n_r]
        v_in = refs[n_r:n_r + n_v]
        r_out = refs[n_r + n_v:n_r + n_v + n_ro]
        v_out = refs[n_r + n_v + n_ro:]
        ro, vo = fn(*[r[...] for r in r_in], *[v[...] for v in v_in])
        for ref, val in zip(r_out, ro):
            ref[...] = val.astype(ref.dtype)
        if n_vo:
            @pl.when(pl.program_id(0) == 0)
            def _():
                for ref in v_out:
                    ref[...] = jnp.zeros(ref.shape, ref.dtype)
            for ref, val in zip(v_out, vo):
                ref[...] += val

    in_specs = [pl.BlockSpec((tile, a.shape[1]), lambda i: (i, 0)) for a in rows]
    in_specs += [pl.BlockSpec(a.shape, lambda i: (0, 0)) for a in vecs]
    out_specs = [pl.BlockSpec((tile, w), lambda i: (i, 0)) for w, _ in row_outs]
    out_specs += [pl.BlockSpec((1, w), lambda i: (0, 0)) for w in vec_outs]
    out_shape = [jax.ShapeDtypeStruct((S, w), dt) for w, dt in row_outs]
    out_shape += [jax.ShapeDtypeStruct((1, w), F32) for w in vec_outs]
    res = pl.pallas_call(
        body, name=name, grid=(S // tile,), in_specs=in_specs, out_specs=out_specs, out_shape=out_shape,
        compiler_params=_params(("arbitrary",)),
    )(*rows, *vecs)
    return res[:n_ro], res[n_ro:]


def _mm_nn(name, a, b3, out_dtypes, epilogue=None, extras=()):
    M, K = a.shape
    nB, K2, Nsh = b3.shape
    assert K == K2
    N = nB * Nsh
    tm, tn, tk = _pick(M, 1024, 16), _pick(Nsh, 1536, LANE), _pick(K, 512, LANE)
    nps, nk, ne, no = Nsh // tn, K // tk, len(extras), len(out_dtypes)

    def body(a_ref, b_ref, *rest):
        e_refs, o_refs, acc = rest[:ne], rest[ne:ne + no], rest[-1]
        k = pl.program_id(2)

        @pl.when(k == 0)
        def _():
            acc[...] = jnp.zeros(acc.shape, F32)

        acc[...] += jnp.dot(a_ref[...].astype(BF16), b_ref[...].astype(BF16), preferred_element_type=F32)

        @pl.when(k == nk - 1)
        def _():
            vals = (acc[...],) if epilogue is None else epilogue(acc[...], *[e[...] for e in e_refs])
            for ref, val in zip(o_refs, vals):
                ref[...] = val.astype(ref.dtype)

    in_specs = [pl.BlockSpec((tm, tk), lambda i, j, k: (i, k)),
                pl.BlockSpec((None, tk, tn), lambda i, j, k: (j // nps, k, j % nps))]
    in_specs += [pl.BlockSpec((tm, tn), lambda i, j, k: (i, j)) for _ in extras]
    return pl.pallas_call(
        body, name=name, grid=(M // tm, N // tn, nk), in_specs=in_specs,
        out_specs=[pl.BlockSpec((tm, tn), lambda i, j, k: (i, j)) for _ in out_dtypes],
        out_shape=[jax.ShapeDtypeStruct((M, N), dt) for dt in out_dtypes],
        scratch_shapes=[pltpu.VMEM((tm, tn), F32)],
        compiler_params=_params(("parallel", "parallel", "arbitrary")),
    )(a, b3, *extras)


def _mm_nt(name, a, b3, out_dtypes, epilogue=None, extras=()):
    M, N = a.shape
    nB, K, Nsh = b3.shape
    assert N == nB * Nsh
    tm, to, tn = _pick(M, 1024, 16), _pick(K, 1024, LANE), _pick(Nsh, 512, LANE)
    nps, nn, ne, no = Nsh // tn, N // tn, len(extras), len(out_dtypes)

    def body(a_ref, b_ref, *rest):
        e_refs, o_refs, acc = rest[:ne], rest[ne:ne + no], rest[-1]
        n = pl.program_id(2)

        @pl.when(n == 0)
        def _():
            acc[...] = jnp.zeros(acc.shape, F32)

        acc[...] += lax.dot_general(a_ref[...].astype(BF16), b_ref[...].astype(BF16),
                                    (((1,), (1,)), ((), ())), preferred_element_type=F32)

        @pl.when(n == nn - 1)
        def _():
            vals = (acc[...],) if epilogue is None else epilogue(acc[...], *[e[...] for e in e_refs])
            for ref, val in zip(o_refs, vals):
                ref[...] = val.astype(ref.dtype)

    in_specs = [pl.BlockSpec((tm, tn), lambda i, o, n: (i, n)),
                pl.BlockSpec((None, to, tn), lambda i, o, n: (n // nps, o, n % nps))]
    in_specs += [pl.BlockSpec((tm, to), lambda i, o, n: (i, o)) for _ in extras]
    return pl.pallas_call(
        body, name=name, grid=(M // tm, K // to, nn), in_specs=in_specs,
        out_specs=[pl.BlockSpec((tm, to), lambda i, o, n: (i, o)) for _ in out_dtypes],
        out_shape=[jax.ShapeDtypeStruct((M, K), dt) for dt in out_dtypes],
        scratch_shapes=[pltpu.VMEM((tm, to), F32)],
        compiler_params=_params(("parallel", "parallel", "arbitrary")),
    )(a, b3, *extras)


def _mm_tn(name, a, g, nB):
    M, K = a.shape
    M2, N = g.shape
    assert M == M2 and N % nB == 0
    Nsh = N // nB
    tk, tn, tm = _pick(K, 1024, LANE), _pick(Nsh, 1024, LANE), _pick(M, 512, 16)
    nps, nm = Nsh // tn, M // tm

    def body(a_ref, g_ref, o_ref, acc):
        m = pl.program_id(2)

        @pl.when(m == 0)
        def _():
            acc[...] = jnp.zeros(acc.shape, F32)

        acc[...] += lax.dot_general(a_ref[...].astype(BF16), g_ref[...].astype(BF16),
                                    (((0,), (0,)), ((), ())), preferred_element_type=F32)

        @pl.when(m == nm - 1)
        def _():
            o_ref[...] = acc[...]

    return pl.pallas_call(
        body, name=name, grid=(K // tk, N // tn, nm),
        in_specs=[pl.BlockSpec((tm, tk), lambda k, n, m: (m, k)),
                  pl.BlockSpec((tm, tn), lambda k, n, m: (m, n))],
        out_specs=pl.BlockSpec((None, tk, tn), lambda k, n, m: (n // nps, k, n % nps)),
        out_shape=jax.ShapeDtypeStruct((nB, K, Nsh), F32),
        scratch_shapes=[pltpu.VMEM((tk, tn), F32)],
        compiler_params=_params(("parallel", "parallel", "arbitrary")),
    )(a, g)


def _softmax(s):
    m = jnp.max(s, axis=-1, keepdims=True)
    e = jnp.exp(s - m)
    return e * (1.0 / jnp.sum(e, axis=-1, keepdims=True))


def _attn_fwd(name, q, k, v, n_heads, group, dqk, dv, scale, tq):
    S = q.shape[0]

    def body(q_ref, k_ref, v_ref, o_ref):
        s = lax.dot_general(q_ref[...], k_ref[...], (((1,), (1,)), ((), ())), preferred_element_type=F32) * scale
        p = _softmax(s)
        o_ref[...] = jnp.dot(p.astype(BF16), v_ref[...], preferred_element_type=F32)

    return pl.pallas_call(
        body, name=name, grid=(n_heads, S // tq),
        in_specs=[pl.BlockSpec((tq, dqk), lambda h, i: (i, h)),
                  pl.BlockSpec((S, dqk), lambda h, i: (0, h // group)),
                  pl.BlockSpec((S, dv), lambda h, i: (0, h // group))],
        out_specs=pl.BlockSpec((tq, dv), lambda h, i: (i, h)),
        out_shape=jax.ShapeDtypeStruct((S, n_heads * dv), F32),
        compiler_params=_params(("arbitrary", "arbitrary")),
    )(q, k, v)


def _attn_bwd(name, q, k, v, do, n_heads, group, dqk, dv, scale, tq):
    S = q.shape[0]
    n_kv = n_heads // group

    def body(q_ref, k_ref, v_ref, do_ref, dq_ref, dk_ref, dv_ref):
        h, i = pl.program_id(0), pl.program_id(1)

        @pl.when((h % group == 0) & (i == 0))
        def _():
            dk_ref[...] = jnp.zeros(dk_ref.shape, F32)
            dv_ref[...] = jnp.zeros(dv_ref.shape, F32)

        qb, kb, vb, dob = q_ref[...], k_ref[...], v_ref[...], do_ref[...]
        s = lax.dot_general(qb, kb, (((1,), (1,)), ((), ())), preferred_element_type=F32) * scale
        p = _softmax(s)
        dv_ref[...] += lax.dot_general(p.astype(BF16), dob, (((0,), (0,)), ((), ())), preferred_element_type=F32)
        dp = lax.dot_general(dob, vb, (((1,), (1,)), ((), ())), preferred_element_type=F32)
        ds = (p * (dp - jnp.sum(dp * p, axis=-1, keepdims=True)) * scale).astype(BF16)
        dq_ref[...] = jnp.dot(ds, kb, preferred_element_type=F32)
        dk_ref[...] += lax.dot_general(ds, qb, (((0,), (0,)), ((), ())), preferred_element_type=F32)

    return pl.pallas_call(
        body, name=name, grid=(n_heads, S // tq),
        in_specs=[pl.BlockSpec((tq, dqk), lambda h, i: (i, h)),
                  pl.BlockSpec((S, dqk), lambda h, i: (0, h // group)),
                  pl.BlockSpec((S, dv), lambda h, i: (0, h // group)),
                  pl.BlockSpec((tq, dv), lambda h, i: (i, h))],
        out_specs=[pl.BlockSpec((tq, dqk), lambda h, i: (i, h)),
                   pl.BlockSpec((S, dqk), lambda h, i: (0, h // group)),
                   pl.BlockSpec((S, dv), lambda h, i: (0, h // group))],
        out_shape=[jax.ShapeDtypeStruct((S, n_heads * dqk), F32),
                   jax.ShapeDtypeStruct((S, n_kv * dqk), F32),
                   jax.ShapeDtypeStruct((S, n_kv * dv), F32)],
        compiler_params=_params(("arbitrary", "arbitrary")),
    )(q, k, v, do)


def _adamw(name, w, g, m, v):
    R, C = w.shape
    tile = _pick(R, max(8, (1 << 19) // C // 8 * 8), 8)
    c1 = 1.0 - ADAM_B1 ** ADAM_STEP
    c2 = 1.0 - ADAM_B2 ** ADAM_STEP

    def body(w_ref, g_ref, m_ref, v_ref, d_ref, nm_ref, nv_ref):
        gg = g_ref[...]
        nm = ADAM_B1 * m_ref[...] + (1.0 - ADAM_B1) * gg
        nv = ADAM_B2 * v_ref[...] + (1.0 - ADAM_B2) * (gg * gg)
        d_ref[...] = -ADAM_LR * ((nm / c1) / (jnp.sqrt(nv / c2) + ADAM_EPS) + ADAM_WD * w_ref[...])
        nm_ref[...] = nm
        nv_ref[...] = nv

    spec = pl.BlockSpec((tile, C), lambda i: (i, 0))
    return pl.pallas_call(
        body, name=name, grid=(R // tile,), in_specs=[spec] * 4, out_specs=[spec] * 3,
        out_shape=[jax.ShapeDtypeStruct((R, C), F32)] * 3,
        compiler_params=_params(("parallel",)),
    )(w, g, m, v)


def _flip(me, rel):
    return tuple(1 - p if r else p for p, r in zip(me, rel))


def _push(name, ins, out_shapes, transfers, aliases=None):
    ni, no, nt = len(ins), len(out_shapes), len(transfers)

    def body(*refs):
        in_refs, out_refs = refs[:ni], refs[ni:ni + no]
        send_sems, recv_sems = refs[ni + no], refs[ni + no + 1]
        me = (lax.axis_index("x"), lax.axis_index("y"), lax.axis_index("c"))
        copies = []
        for t, (si, sfn, oi, dfn, rel) in enumerate(transfers):
            src_ref = out_refs[si[1]] if isinstance(si, tuple) else in_refs[si]
            src = src_ref.at[sfn(*me)]
            dst = out_refs[oi].at[dfn(*me)]
            if rel is None:
                cp = pltpu.make_async_copy(src, dst, send_sems.at[t])
            else:
                cp = pltpu.make_async_remote_copy(src_ref=src, dst_ref=dst, send_sem=send_sems.at[t],
                                                  recv_sem=recv_sems.at[t], device_id=_flip(me, rel),
                                                  device_id_type=MESH)
            cp.start()
            copies.append(cp)
        for cp in copies:
            cp.wait()

    any_spec = pl.BlockSpec(memory_space=pl.ANY)
    return pl.pallas_call(
        body, name=name, in_specs=[any_spec] * ni, out_specs=[any_spec] * no, out_shape=out_shapes,
        scratch_shapes=[pltpu.SemaphoreType.DMA((nt,)), pltpu.SemaphoreType.DMA((nt,))],
        input_output_aliases=aliases or {},
        compiler_params=pltpu.CompilerParams(has_side_effects=True),
    )(*ins)


_HBM = pl.BlockSpec(memory_space=pltpu.HBM)
_SEM = pl.BlockSpec(memory_space=pltpu.SEMAPHORE)
_DATAFLOW = pltpu.SideEffectType.DATAFLOW_SIDE_EFFECTING


def _split_copies(refs, transfers, send_sems, recv_sems):
    me = (lax.axis_index("x"), lax.axis_index("y"), lax.axis_index("c"))
    return [pltpu.make_async_remote_copy(src_ref=refs[sb].at[sfn(*me)], dst_ref=refs[db].at[dfn(*me)],
                                         send_sem=send_sems.at[t], recv_sem=recv_sems.at[t],
                                         device_id=_flip(me, rel), device_id_type=MESH)
            for t, (sb, sfn, db, dfn, rel) in enumerate(transfers)]


def _push_start(name, bufs, transfers, after=()):
    nb, na, nt = len(bufs), len(after), len(transfers)

    def body(*refs):
        send_sems, recv_sems, token = refs[nb + na], refs[nb + na + 1], refs[-1]
        for cp in _split_copies(refs[nb + na + 2:2 * nb + na + 2], transfers, send_sems, recv_sems):
            cp.start()
        token[...] = jnp.zeros(token.shape, token.dtype)

    res = pl.pallas_call(
        body, name=name,
        out_shape=(pltpu.SemaphoreType.DMA((nt,)), pltpu.SemaphoreType.DMA((nt,)),
                   *[pltpu.HBM(b.shape, b.dtype) for b in bufs], jax.ShapeDtypeStruct((8, LANE), F32)),
        in_specs=[_HBM] * nb + [pl.BlockSpec(memory_space=pl.ANY)] * na,
        out_specs=(_SEM, _SEM, *[_HBM] * nb, pl.BlockSpec(memory_space=pltpu.VMEM)),
        input_output_aliases={i: i + 2 for i in range(nb)},
        compiler_params=pltpu.CompilerParams(has_side_effects=_DATAFLOW),
    )(*[pltpu.with_memory_space_constraint(b, pltpu.HBM) for b in bufs], *after)
    return res[0], res[1], list(res[2:2 + nb]), res[-1]


def _push_wait(name, send_sems, recv_sems, bufs, transfers, after):
    nb = len(bufs)

    def body(*refs):
        for cp in _split_copies(refs[:nb], transfers, refs[nb], refs[nb + 1]):
            cp.wait_send()
            cp.wait_recv()

    res = pl.pallas_call(
        body, name=name, out_shape=[pltpu.HBM(b.shape, b.dtype) for b in bufs],
        in_specs=[_HBM] * nb + [_SEM, _SEM, pl.BlockSpec(memory_space=pl.ANY)], out_specs=[_HBM] * nb,
        input_output_aliases={i: i for i in range(nb)},
        compiler_params=pltpu.CompilerParams(has_side_effects=_DATAFLOW),
    )(*bufs, send_sems, recv_sems, after)
    return list(res)


ICI_RELS = ((1, 0, 0), (0, 1, 0), (1, 1, 0))
SIBLING = (0, 0, 1)


def _chip(x, y):
    return 2 * x + y


def _cast_into_slot(name, where, w):
    R, C = w.shape
    tile = _pick(R, 256, 16)

    def body(where_ref, w_ref, o_ref):
        o_ref[...] = w_ref[...].astype(BF16)

    return pl.pallas_call(
        body, name=name,
        grid_spec=pltpu.PrefetchScalarGridSpec(
            num_scalar_prefetch=1, grid=(R // tile,),
            in_specs=[pl.BlockSpec((tile, C), lambda i, wh: (i, 0))],
            out_specs=pl.BlockSpec((None, tile, C), lambda i, wh: (wh[1], i, 0))),
        out_shape=jax.ShapeDtypeStruct((N_CHIPS, R, C), BF16),
        compiler_params=_params(("parallel",)),
    )(where, w)


def _gather_start(tag, where, shards, after):
    n = len(shards)
    own = [_cast_into_slot(f"cast_w_{tag}{w}", where, shards[w]) for w in range(n)]
    transfers = []
    for w in range(n):
        def idx(x, y, c, rh=shards[w].shape[0] // 2):
            return (_chip(x, y), pl.ds(c * rh, rh), slice(None))
        transfers += [(w, idx, w, idx, rel) for rel in ICI_RELS]
    send_sems, recv_sems, bufs, token = _push_start(f"gather_{tag}_start", own, transfers, after)
    return (tag, send_sems, recv_sems, bufs, transfers), token


def _gather_finish(handle, after):
    tag, send_sems, recv_sems, bufs, transfers = handle
    part = _push_wait(f"gather_{tag}_wait", send_sems, recv_sems, bufs, transfers, after)
    n = len(part)
    halves = [p.shape[1] // 2 for p in part]
    outs = [jax.ShapeDtypeStruct(p.shape, p.dtype) for p in part]

    second = []
    for w in range(n):
        rh = halves[w]
        for rel in ICI_RELS:
            def idx(x, y, c, rh=rh, rel=rel):
                return (_chip(x ^ rel[0], y ^ rel[1]), pl.ds(c * rh, rh), slice(None))
            second.append((("out", w), idx, w, idx, SIBLING))
    return _push(f"gather_{tag}_d2d", list(part), outs, second, aliases={w: w for w in range(n)})


def _reduce_start(tag, where, grads, after=()):
    n = len(grads)
    halves = [g.shape[1] // 2 for g in grads]
    t1 = []
    for w in range(n):
        t1.append((w, lambda x, y, c, rh=halves[w]: (slice(None), pl.ds((1 - c) * rh, rh), slice(None)), w,
                   lambda x, y, c: (slice(None), slice(None), slice(None)), SIBLING))
    from_sib = _push(f"reduce_{tag}_d2d", list(grads),
                     [jax.ShapeDtypeStruct((N_CHIPS, halves[w], grads[w].shape[2]), F32) for w in range(n)], t1)
    pair = [_pair_sum(f"pair_sum_{tag}{w}", where, grads[w], from_sib[w]) for w in range(n)]
    land = [lax.empty((3, halves[w], grads[w].shape[2]), BF16) for w in range(n)]
    t3 = []
    for w in range(n):
        for r, rel in enumerate(ICI_RELS):
            t3.append((w, lambda x, y, c, rel=rel: (_chip(x ^ rel[0], y ^ rel[1]), slice(None), slice(None)), n + w,
                       lambda x, y, c, r=r: (r, slice(None), slice(None)), rel))
    send_sems, recv_sems, bufs, token = _push_start(f"reduce_{tag}_start", pair + land, t3, after)
    return (tag, send_sems, recv_sems, bufs, t3, list(grads), list(from_sib)), token


def _reduce_finish(handle, where, after):
    tag, send_sems, recv_sems, bufs, t3, grads, from_sib = handle
    n = len(grads)
    halves = [g.shape[1] // 2 for g in grads]
    from_chips = _push_wait(f"reduce_{tag}_wait", send_sems, recv_sems, bufs, t3, after)[n:]
    mine = [_chip_sum(f"chip_sum_{tag}{w}", where, grads[w], from_sib[w], from_chips[w]) for w in range(n)]
    t5 = []
    for w in range(n):
        def idx(x, y, c, rh=halves[w]):
            return (pl.ds(c * rh, rh), slice(None))
        t5.append((("out", w), idx, w, idx, SIBLING))
    return _push(f"reduce_{tag}_swap", mine,
                 [jax.ShapeDtypeStruct((2 * halves[w], grads[w].shape[2]), F32) for w in range(n)], t5,
                 aliases={w: w for w in range(n)})


def _pair_sum(name, where, g, from_sib):
    _, R, C = g.shape
    rh = R // 2
    tile = _pick(rh, max(16, (1 << 19) // C // 16 * 16), 16)
    nb = rh // tile

    def body(where_ref, g_ref, s_ref, o_ref):
        o_ref[...] = (g_ref[...] + s_ref[...]).astype(BF16)

    return pl.pallas_call(
        body, name=name,
        grid_spec=pltpu.PrefetchScalarGridSpec(
            num_scalar_prefetch=1, grid=(N_CHIPS, nb),
            in_specs=[pl.BlockSpec((None, tile, C), lambda j, i, wh: (j, wh[0] * nb + i, 0)),
                      pl.BlockSpec((None, tile, C), lambda j, i, wh: (j, i, 0))],
            out_specs=pl.BlockSpec((None, tile, C), lambda j, i, wh: (j, i, 0))),
        out_shape=jax.ShapeDtypeStruct((N_CHIPS, rh, C), BF16),
        compiler_params=_params(("parallel", "parallel")),
    )(where, g, from_sib)


def _chip_sum(name, where, g, from_sib, from_chips):
    _, R, C = g.shape
    rh = R // 2
    tile = _pick(rh, max(16, (1 << 19) // C // 16 * 16), 16)
    nb = rh // tile

    def body(where_ref, g_ref, s_ref, r_ref, o_ref):
        acc = g_ref[...] + s_ref[...]
        for r in range(3):
            acc = acc + r_ref[r].astype(F32)
        o_ref[...] = acc

    return pl.pallas_call(
        body, name=name,
        grid_spec=pltpu.PrefetchScalarGridSpec(
            num_scalar_prefetch=1, grid=(nb,),
            in_specs=[pl.BlockSpec((None, tile, C), lambda i, wh: (wh[1], wh[0] * nb + i, 0)),
                      pl.BlockSpec((None, tile, C), lambda i, wh: (wh[1], i, 0)),
                      pl.BlockSpec((3, tile, C), lambda i, wh: (0, i, 0))],
            out_specs=pl.BlockSpec((tile, C), lambda i, wh: (wh[0] * nb + i, 0))),
        out_shape=jax.ShapeDtypeStruct((R, C), F32),
        compiler_params=_params(("parallel",)),
    )(where, g, from_sib, from_chips)


def _allgather8(name, blk):
    m_per, n = blk.shape
    rels = [(0, 0, 1), (1, 0, 0), (0, 1, 0), (1, 1, 0), (1, 0, 1), (0, 1, 1), (1, 1, 1)]

    def body(x_ref, out_ref, send_sems, recv_sems):
        me = (lax.axis_index("x"), lax.axis_index("y"), lax.axis_index("c"))
        my_rows = out_ref.at[pl.ds((4 * me[0] + 2 * me[1] + me[2]) * m_per, m_per), :]
        out_ref[pl.ds((4 * me[0] + 2 * me[1] + me[2]) * m_per, m_per), :] = x_ref[...]
        copies = []
        for t, rel in enumerate(rels):
            cp = pltpu.make_async_remote_copy(src_ref=x_ref, dst_ref=my_rows, send_sem=send_sems.at[t],
                                              recv_sem=recv_sems.at[t], device_id=_flip(me, rel),
                                              device_id_type=MESH)
            cp.start()
            copies.append(cp)
        for cp in copies:
            cp.wait()

    return pl.pallas_call(
        body, name=name, out_shape=jax.ShapeDtypeStruct((N_DEV * m_per, n), blk.dtype),
        in_specs=[pl.BlockSpec(memory_space=pltpu.VMEM)], out_specs=pl.BlockSpec(memory_space=pltpu.VMEM),
        scratch_shapes=[pltpu.SemaphoreType.DMA((7,)), pltpu.SemaphoreType.DMA((7,))],
        compiler_params=pltpu.CompilerParams(has_side_effects=True, vmem_limit_bytes=VMEM_LIMIT),
    )(blk)


def _rope_tables(seq_len, dim, reps):
    rows = seq_len // GRID_W
    t = np.arange(seq_len)
    row, col = (t // GRID_W).astype(np.float32), (t % GRID_W).astype(np.float32)
    half = dim // 2
    inv = jnp.asarray(ROPE_THETA, F32) ** (-jnp.arange(0, half, 2, dtype=F32) / half)
    ang_r = jnp.asarray(row)[:, None] * inv[None, :]
    ang_c = jnp.asarray(col)[:, None] * inv[None, :]
    ang = jnp.concatenate([ang_r, ang_r, ang_c, ang_c], axis=-1)
    cos, sin = jnp.cos(ang), jnp.sin(ang)
    low = (np.arange(dim) % (dim // 2)) < (dim // 4)
    s_lo = jnp.where(jnp.asarray(low)[None, :], -sin, 0.0)
    s_hi = jnp.where(jnp.asarray(low)[None, :], 0.0, sin)
    del rows
    return tuple(jnp.tile(a, (1, reps)) for a in (cos, s_lo, s_hi))


def _make_rope(shift):
    def up(v):
        return pltpu.roll(v, LANE - shift, 1)

    def down(v):
        return pltpu.roll(v, shift, 1)

    @jax.custom_vjp
    def rope(v, cos, s_lo, s_hi):
        return v * cos + up(v) * s_lo + down(v) * s_hi

    def fwd(v, cos, s_lo, s_hi):
        return rope(v, cos, s_lo, s_hi), (cos, s_lo, s_hi)

    def bwd(res, dy):
        cos, s_lo, s_hi = res
        return dy * cos + down(dy * s_lo) + up(dy * s_hi), jnp.zeros_like(cos), jnp.zeros_like(cos), jnp.zeros_like(cos)

    rope.defvjp(fwd, bwd)
    return rope


_rope_a = _make_rope(HEAD_DIM // 4)
_rope_b = _make_rope(QK_ROPE // 4)


def _prep(segs, tabs, g_q, g_k, g_ckv):
    qa, ka, va, qn, qp, ckv, kpe = segs
    ta, tb = tabs
    q_a = jnp.concatenate([_rope_a(_rn(s) * g_q, *ta) for s in qa], axis=1)
    k_a = jnp.concatenate([_rope_a(_rn(s) * g_k, *ta) for s in ka], axis=1)
    cat = []
    for h in range(HB):
        cat += [qn[h], _rope_b(qp[h], *tb)]
    q_cat = jnp.concatenate(cat, axis=1)
    return q_a, k_a, va, q_cat, _rn(ckv) * g_ckv, _rope_b(kpe, *tb)


def _split_proj(proj):
    qa = [proj[:, O_QA + h * LANE:O_QA + (h + 1) * LANE] for h in range(HA)]
    ka = [proj[:, O_KA + h * LANE:O_KA + (h + 1) * LANE] for h in range(HKV)]
    va = proj[:, O_VA:O_QN]
    qn = [proj[:, O_QN + h * LANE:O_QN + (h + 1) * LANE] for h in range(HB)]
    qp = [proj[:, O_QP + h * LANE:O_QP + (h + 1) * LANE] for h in range(HB)]
    return qa, ka, va, qn, qp, proj[:, O_CKV:O_KPE], proj[:, O_KPE:W_INP]


def _permute_w_in(nat):
    K = nat.shape[0]
    qb = nat[:, W_QA + 2 * W_KA:W_QA + 2 * W_KA + W_QB].reshape(K, HB, QK_B)
    z = lambda w: jnp.zeros((K, w), nat.dtype)
    qp = jnp.concatenate([qb[:, :, QK_NOPE:], jnp.zeros((K, HB, LANE - QK_ROPE), nat.dtype)], axis=2)
    tail = nat[:, W_QA + 2 * W_KA + W_QB:]
    return jnp.concatenate([nat[:, :O_QN], qb[:, :, :QK_NOPE].reshape(K, HB * QK_NOPE), qp.reshape(K, HB * LANE),
                            tail, z(LANE - QK_ROPE)], axis=1)


def _unpermute_w_in(p):
    K = p.shape[0]
    qn = p[:, O_QN:O_QP].reshape(K, HB, QK_NOPE)
    qp = p[:, O_QP:O_CKV].reshape(K, HB, LANE)[:, :, :QK_ROPE]
    qb = jnp.concatenate([qn, qp], axis=2).reshape(K, W_QB)
    return jnp.concatenate([p[:, :O_QN], qb, p[:, O_CKV:O_KPE + QK_ROPE]], axis=1)


def kernel(x, c, w_ada, b_ada, g_pre_attn, w_in, g_q_a, g_k_a, g_ckv, w_kv_b, g_out_a, g_out_b, w_out, g_post_attn, g_pre_mlp, w_mlp_in, w_mlp_out, g_post_mlp, loss_target, m_w_ada, m_b_ada, m_g_pre_attn, m_w_in, m_g_q_a, m_g_k_a, m_g_ckv, m_w_kv_b, m_g_out_a, m_g_out_b, m_w_out, m_g_post_attn, m_g_pre_mlp, m_w_mlp_in, m_w_mlp_out, m_g_post_mlp, v_w_ada, v_b_ada, v_g_pre_attn, v_w_in, v_g_q_a, v_g_k_a, v_g_ckv, v_w_kv_b, v_g_out_a, v_g_out_b, v_w_out, v_g_post_attn, v_g_pre_mlp, v_w_mlp_in, v_w_mlp_out, v_g_post_mlp):
    S, D = x.shape[1], x.shape[2]
    x2d, tgt = x[0], loss_target[0]
    ix, iy, ic = lax.axis_index("x"), lax.axis_index("y"), lax.axis_index("c")
    j_me = _chip(ix, iy)
    e_me = 4 * ix + 2 * iy + ic
    T = _pick(S, 256, 8)
    TB = _pick(S, 128, 8)

    where = jnp.stack([ic, j_me]).astype(jnp.int32)
    DFF = w_mlp_out.shape[1] * N_CHIPS

    c_all = _allgather8("gather_c", c.reshape(8, D // 8)).reshape(N_DEV, D)

    def silu_fn(cc):
        return (cc * (1.0 / (1.0 + jnp.exp(-cc))),), ()

    (c_act,), _ = _rowwise("silu_c", silu_fn, [c_all], [], [(D, BF16)], [], N_DEV)
    ca_pad = jnp.concatenate([c_act, jnp.zeros_like(c_act)], axis=0)
    n_mod = w_ada.shape[2]
    b_cols = lax.dynamic_slice(b_ada, (0, j_me * n_mod), (1, n_mod))
    (mod_part,) = _mm_nn("mod_mm", ca_pad, w_ada, [F32], epilogue=lambda acc, b: (acc + b,),
                         extras=(jnp.broadcast_to(b_cols, (16, n_mod)),))
    mod_all = _allgather8("gather_mod", mod_part[:8]).reshape(N_DEV, 8, n_mod)
    mod = jnp.concatenate([lax.dynamic_slice(mod_all, (2 * j, e_me, 0), (1, 1, n_mod))[0] for j in range(N_CHIPS)],
                          axis=1)

    h_qkv, tok_a = _gather_start("qkv", where, [w_in[0], w_kv_b[0]], (mod_all,))
    h_out, tok_b = _gather_start("out", where, [w_out[0]], (tok_a,))
    h_up, tok_c = _gather_start("up", where, [w_mlp_in[0]], (tok_b,))
    h_down, tok_d = _gather_start("down", where, [w_mlp_out[0]], (tok_c,))
    mod = mod + tok_d[0, 0]
    sh_a, sc_a, gt_a, sh_m, sc_m, gt_m = [mod[:, i * D:(i + 1) * D] for i in range(6)]

    tabs_a = _rope_tables(S, HEAD_DIM, 1)
    tabs_b = _rope_tables(S, QK_ROPE, LANE // QK_ROPE)

    def pre_attn_fn(xt, g, sc, sh):
        return (_rn(xt) * g * (1.0 + sc) + sh,), ()

    (h_b,), _ = _rowwise("pre_attn", pre_attn_fn, [x2d], [g_pre_attn, sc_a, sh_a], [(D, BF16)], [], T)
    g_in, g_kvb = _gather_finish(h_qkv, h_b)
    wp = _permute_w_in(jnp.transpose(g_in, (1, 0, 2)).reshape(D, W_IN))[None]
    (proj,) = _mm_nn("proj_mm", h_b, wp, [F32])

    def prep_fn(pt, ca_, sla, sha, cb_, slb, shb, gq, gk, gc):
        return _prep(_split_proj(pt), ((ca_, sla, sha), (cb_, slb, shb)), gq, gk, gc), ()

    (qa_b, ka_b, va_b, qcat_b, ckvn_b, kpe_b), _ = _rowwise(
        "prep", prep_fn, [proj, *tabs_a, *tabs_b], [g_q_a, g_k_a, g_ckv],
        [(W_QA, BF16), (W_KA, BF16), (W_KA, BF16), (2 * HB * LANE, BF16), (KV_RANK, BF16), (LANE, BF16)], [], TB)
    (kv,) = _mm_nn("kv_mm", ckvn_b, g_kvb, [F32])

    def kcat_fn(kvt, kpet):
        kc, vs = [], []
        for h in range(HB):
            kc += [kvt[:, 2 * h * LANE:(2 * h + 1) * LANE], kpet.astype(F32)]
            vs.append(kvt[:, (2 * h + 1) * LANE:(2 * h + 2) * LANE])
        return (jnp.concatenate(kc, axis=1), jnp.concatenate(vs, axis=1)), ()

    (kcat_b, vb_b), _ = _rowwise("kcat", kcat_fn, [kv, kpe_b], [], [(2 * HB * LANE, BF16), (HB * V_DIM, BF16)], [], T)

    TQ = _pick(S, 256, 16)
    sc_a_ = 1.0 / math.sqrt(HEAD_DIM)
    sc_b_ = 1.0 / math.sqrt(QK_B)
    o_a = _attn_fwd("attn_a_fwd", qa_b, ka_b, va_b, HA, HA // HKV, HEAD_DIM, HEAD_DIM, sc_a_, TQ)
    o_b = _attn_fwd("attn_b_fwd", qcat_b, kcat_b, vb_b, HB, 1, 2 * LANE, V_DIM, sc_b_, TQ)

    def mix_fn(oa, ob, ga, gb):
        return (jnp.concatenate([_rn(oa) * ga, _rn(ob) * gb], axis=1),), ()

    (on_b,), _ = _rowwise("mix_norm", mix_fn, [o_a, o_b], [g_out_a, g_out_b], [(2 * W_QA, BF16)], [], T)
    (g_wout,) = _gather_finish(h_out, on_b)
    w_out_f = g_wout.reshape(1, N_CHIPS * g_wout.shape[1], D)
    (o2,) = _mm_nn("out_mm", on_b, w_out_f, [F32])

    def mid(xt, o2t, gta, gpa, gpm, scm, shm):
        x1 = xt + gta * (_rn(o2t) * gpa)
        return x1, _rn(x1) * gpm * (1.0 + scm) + shm

    def mid_fn(*a):
        return mid(*a), ()

    mid_vecs = [gt_a, g_post_attn, g_pre_mlp, sc_m, sh_m]
    (x1, h2_b), _ = _rowwise("mid", mid_fn, [x2d, o2], mid_vecs, [(D, F32), (D, BF16)], [], T)
    (g_mlp_in,) = _gather_finish(h_up, h2_b)
    u, a_b = _mm_nn("mlp_in_mm", h2_b, g_mlp_in, [F32, BF16],
                    epilogue=lambda acc: (acc, jnp.square(jnp.maximum(acc, 0.0))))
    (g_mlp_out,) = _gather_finish(h_down, a_b)
    w_mlp_out_f = g_mlp_out.reshape(1, DFF, D)
    (y,) = _mm_nn("mlp_out_mm", a_b, w_mlp_out_f, [F32])

    def last(yt, x1t, gtm, gpo):
        return x1t + gtm * (_rn(yt) * gpo)

    def last_fn(yt, x1t, tt, gtm, gpo):
        x2, vjp = jax.vjp(last, yt, x1t, gtm, gpo)
        err = x2 - tt
        dy, dx1, dgt, dg = vjp(err * (1.0 / D))
        loss = 0.5 * jnp.sum(jnp.mean(err * err, axis=-1))
        return (dy, dx1), (jnp.full((1, LANE), loss, F32), dgt, dg)

    (dy_b, dx2), (loss_v, d_gt_m, d_g_post_mlp) = _rowwise(
        "loss_bwd", last_fn, [y, x1, tgt], [gt_m, g_post_mlp], [(D, BF16), (D, F32)], [LANE, D, D], TB)

    (du_b,) = _mm_nt("d_mlp_out_mm", dy_b, w_mlp_out_f, [BF16],
                     epilogue=lambda acc, ut: (acc * (2.0 * jnp.maximum(ut, 0.0)),), extras=(u,))
    gw_mlp_out = _mm_tn("gw_mlp_out_mm", a_b, dy_b, 1).reshape(N_CHIPS, DFF // N_CHIPS, D)
    r_down, _ = _reduce_start("down", where, [gw_mlp_out])
    (dh2,) = _mm_nt("d_mlp_in_mm", du_b, g_mlp_in, [F32])
    gw_mlp_in = _mm_tn("gw_mlp_in_mm", h2_b, du_b, N_CHIPS)
    r_up, _ = _reduce_start("up", where, [gw_mlp_in])

    def mid_bwd_fn(xt, o2t, dh2t, dx2t, *vecs):
        _, vjp = jax.vjp(mid, xt, o2t, *vecs)
        dx, do2, dgta, dgpa, dgpm, dscm, dshm = vjp((dx2t, dh2t))
        return (dx, do2), (dgta, dgpa, dgpm, dscm, dshm)

    (dx1, do2_b), (d_gt_a, d_g_post_attn, d_g_pre_mlp, d_sc_m, d_sh_m) = _rowwise(
        "mid_bwd", mid_bwd_fn, [x2d, o2, dh2, dx2], mid_vecs, [(D, F32), (D, BF16)], [D] * 5, TB)

    (d_on,) = _mm_nt("d_out_mm", do2_b, w_out_f, [F32])
    gw_out = _mm_tn("gw_out_mm", on_b, do2_b, 1).reshape(N_CHIPS, w_out.shape[1], D)
    r_out, _ = _reduce_start("out", where, [gw_out])

    def mix_bwd_fn(oa, ob, dont, ga, gb):
        _, vjp = jax.vjp(lambda a_, b_, g1, g2: mix_fn(a_, b_, g1, g2)[0][0], oa, ob, ga, gb)
        doa, dob, dga, dgb = vjp(dont)
        return (doa, dob), (dga, dgb)

    (doa_b, dob_b), (d_g_out_a, d_g_out_b) = _rowwise(
        "mix_bwd", mix_bwd_fn, [o_a, o_b, d_on], [g_out_a, g_out_b], [(W_QA, BF16), (HB * V_DIM, BF16)],
        [W_QA, HB * V_DIM], T)

    dqa, dka, dva = _attn_bwd("attn_a_bwd", qa_b, ka_b, va_b, doa_b, HA, HA // HKV, HEAD_DIM, HEAD_DIM, sc_a_, TQ)
    dqcat, dkcat, dvb = _attn_bwd("attn_b_bwd", qcat_b, kcat_b, vb_b, dob_b, HB, 1, 2 * LANE, V_DIM, sc_b_, TQ)

    def kcat_bwd_fn(dkc, dvt):
        dkv, dkpe = [], None
        for h in range(HB):
            dkv += [dkc[:, 2 * h * LANE:(2 * h + 1) * LANE], dvt[:, h * V_DIM:(h + 1) * V_DIM]]
            piece = dkc[:, (2 * h + 1) * LANE:(2 * h + 2) * LANE]
            dkpe = piece if dkpe is None else dkpe + piece
        return (jnp.concatenate(dkv, axis=1), dkpe), ()

    (dkv_b, dkpe), _ = _rowwise("kcat_bwd", kcat_bwd_fn, [dkcat, dvb], [], [(2 * HB * LANE, BF16), (LANE, F32)], [], T)
    (d_ckvn,) = _mm_nt("d_kv_mm", dkv_b, g_kvb, [F32])
    gw_kvb = _mm_tn("gw_kv_mm", ckvn_b, dkv_b, N_CHIPS)
    r_kvb, _ = _reduce_start("kvb", where, [gw_kvb])

    def prep_bwd_fn(pt, dqat, dkat, dvat, dqct, dckt, dkpt, ca_, sla, sha, cb_, slb, shb, gq, gk, gc):
        tabs = ((ca_, sla, sha), (cb_, slb, shb))
        _, vjp = jax.vjp(lambda sg, a1, a2, a3: _prep(sg, tabs, a1, a2, a3), _split_proj(pt), gq, gk, gc)
        (dqa_s, dka_s, dva_s, dqn_s, dqp_s, dck_s, dkp_s), dgq, dgk, dgc = vjp((dqat, dkat, dvat, dqct, dckt, dkpt))
        dproj = jnp.concatenate([*dqa_s, *dka_s, dva_s, *dqn_s, *dqp_s, dck_s, dkp_s], axis=1)
        return (dproj,), (dgq, dgk, dgc)

    (dproj_b,), (d_g_q_a, d_g_k_a, d_g_ckv) = _rowwise(
        "prep_bwd", prep_bwd_fn, [proj, dqa, dka, dva, dqcat, d_ckvn, dkpe, *tabs_a, *tabs_b],
        [g_q_a, g_k_a, g_ckv], [(W_INP, BF16)], [HEAD_DIM, HEAD_DIM, KV_RANK], TB)

    (dh,) = _mm_nt("d_proj_mm", dproj_b, wp, [F32])
    gwp = _mm_tn("gw_in_mm", h_b, dproj_b, 1)[0]
    gw_in = jnp.transpose(_unpermute_w_in(gwp).reshape(D, N_CHIPS, W_IN // N_CHIPS), (1, 0, 2))

    def pre_attn_bwd_fn(xt, dht, dx1t, g, sc, sh):
        _, vjp = jax.vjp(lambda a_, b_, c_, d_: pre_attn_fn(a_, b_, c_, d_)[0][0], xt, g, sc, sh)
        dxt, dg, dsc, dsh = vjp(dht)
        return (dx1t + dxt,), (dg, dsc, dsh)

    (grad_x,), (d_g_pre_attn, d_sc_a, d_sh_a) = _rowwise(
        "pre_attn_bwd", pre_attn_bwd_fn, [x2d, dh, dx1], [g_pre_attn, sc_a, sh_a], [(D, F32)], [D] * 3, TB)

    small_names = ["b_ada", "g_pre_attn", "g_q_a", "g_k_a", "g_ckv", "g_out_a", "g_out_b", "g_post_attn",
                   "g_pre_mlp", "g_post_mlp"]
    small_local = jnp.concatenate([d_sh_a, d_sc_a, d_gt_a, d_sh_m, d_sc_m, d_gt_m, d_g_pre_attn, d_g_q_a, d_g_k_a,
                                   d_g_ckv, d_g_out_a, d_g_out_b, d_g_post_attn, d_g_pre_mlp, d_g_post_mlp], axis=1)
    n_small = small_local.shape[1]
    small_all = _allgather8("gather_small", small_local.reshape(8, n_small // 8)).reshape(N_DEV, n_small)

    def sum8_fn(t):
        return (), (jnp.sum(t, axis=0, keepdims=True),)

    _, (small_sum,) = _rowwise("sum_small", sum8_fn, [small_all], [], [], [n_small], N_DEV)
    dmod_cols = lax.dynamic_slice(small_all, (0, j_me * n_mod), (N_DEV, n_mod))
    dmod_pad = jnp.concatenate([dmod_cols, jnp.zeros_like(dmod_cols)], axis=0).astype(BF16)
    gw_ada = _mm_tn("gw_ada_mm", ca_pad, dmod_pad, 1)[0]
    r_in, _ = _reduce_start("in", where, [gw_in], (small_all,))

    weights = dict(w_ada=w_ada, b_ada=b_ada, g_pre_attn=g_pre_attn, w_in=w_in, g_q_a=g_q_a, g_k_a=g_k_a, g_ckv=g_ckv,
                   w_kv_b=w_kv_b, g_out_a=g_out_a, g_out_b=g_out_b, w_out=w_out, g_post_attn=g_post_attn,
                   g_pre_mlp=g_pre_mlp, w_mlp_in=w_mlp_in, w_mlp_out=w_mlp_out, g_post_mlp=g_post_mlp)
    ms = dict(w_ada=m_w_ada, b_ada=m_b_ada, g_pre_attn=m_g_pre_attn, w_in=m_w_in, g_q_a=m_g_q_a, g_k_a=m_g_k_a,
              g_ckv=m_g_ckv, w_kv_b=m_w_kv_b, g_out_a=m_g_out_a, g_out_b=m_g_out_b, w_out=m_w_out,
              g_post_attn=m_g_post_attn, g_pre_mlp=m_g_pre_mlp, w_mlp_in=m_w_mlp_in, w_mlp_out=m_w_mlp_out,
              g_post_mlp=m_g_post_mlp)
    vs = dict(w_ada=v_w_ada, b_ada=v_b_ada, g_pre_attn=v_g_pre_attn, w_in=v_w_in, g_q_a=v_g_q_a, g_k_a=v_g_k_a,
              g_ckv=v_g_ckv, w_kv_b=v_w_kv_b, g_out_a=v_g_out_a, g_out_b=v_g_out_b, w_out=v_w_out,
              g_post_attn=v_g_post_attn, g_pre_mlp=v_g_pre_mlp, w_mlp_in=v_w_mlp_in, w_mlp_out=v_w_mlp_out,
              g_post_mlp=v_g_post_mlp)
    order = list(weights)
    grads, deltas, new_m, new_v = {}, {}, {}, {}

    def update(name, g):
        d_, m_, v_ = _adamw("adamw_" + name, weights[name][0], g, ms[name][0], vs[name][0])
        grads[name], deltas[name], new_m[name], new_v[name] = g[None], d_[None], m_[None], v_[None]
        return d_

    last = update("w_ada", gw_ada)
    for name, handle in (("w_mlp_out", r_down), ("w_mlp_in", r_up), ("w_out", r_out), ("w_kv_b", r_kvb),
                         ("w_in", r_in)):
        (g,) = _reduce_finish(handle, where, last)
        last = update(name, g)

    cat = lambda d: jnp.concatenate([d[n] for n in small_names], axis=1)
    d_, m_, v_ = _adamw("adamw_small", cat(weights), small_sum, cat(ms), cat(vs))
    off = 0
    for n in small_names:
        w_ = weights[n].shape[1]
        grads[n], deltas[n], new_m[n], new_v[n] = (a[:, off:off + w_] for a in (small_sum, d_, m_, v_))
        off += w_

    loss = lax.psum(loss_v[0, 0], ("x", "y", "c"))
    return (loss, grad_x[None], *[grads[n] for n in order], *[deltas[n] for n in order],
            *[new_m[n] for n in order], *[new_v[n] for n in order])
```

```python
import functools
import math

import numpy as np
import jax
import jax.numpy as jnp
from jax import lax
from jax.experimental import pallas as pl
from jax.experimental.pallas import tpu as pltpu

F32 = jnp.float32
BF16 = jnp.bfloat16
MESH = pl.DeviceIdType.MESH

EPS = 1e-6
GRID_W = 64
ROPE_THETA = 10000.0
HEAD_DIM = 128
HA = 8
HKV = 2
HB = 8
QK_NOPE = 128
QK_ROPE = 64
V_DIM = 128
KV_RANK = 512
QK_B = QK_NOPE + QK_ROPE
W_QA = HA * HEAD_DIM
W_KA = HKV * HEAD_DIM
W_QB = HB * QK_B
W_IN = W_QA + 2 * W_KA + W_QB + KV_RANK + QK_ROPE
N_CHIPS = 4
N_DEV = 8

LANE = 128
O_QA = 0
O_KA = O_QA + W_QA
O_VA = O_KA + W_KA
O_QN = O_VA + W_KA
O_QP = O_QN + HB * QK_NOPE
O_CKV = O_QP + HB * LANE
O_KPE = O_CKV + KV_RANK
W_INP = O_KPE + LANE

ADAM_LR = 0.001
ADAM_B1 = 0.9
ADAM_B2 = 0.999
ADAM_EPS = 1e-08
ADAM_WD = 0.01
ADAM_STEP = 10

VMEM_LIMIT = 56 * 1024 * 1024


def _params(sem):
    return pltpu.CompilerParams(dimension_semantics=sem, vmem_limit_bytes=VMEM_LIMIT)


def _pick(dim, target, unit):
    best = None
    t = unit
    while t <= min(dim, target):
        if dim % t == 0:
            best = t
        t += unit
    return dim if best is None else best


def _rn(x):
    return x * lax.rsqrt(jnp.mean(x * x, axis=-1, keepdims=True) + EPS)


def _rowwise(name, fn, rows, vecs, row_outs, vec_outs, tile, after=()):
    S = rows[0].shape[0]
    assert S % tile == 0
    n_r, n_v, n_ro, n_vo, n_a = len(rows), len(vecs), len(row_outs), len(vec_outs), len(after)

    def body(*refs):
        r_in = refs[:n_r]
        v_in = refs[n_r:n_r + n_v]
        r_out = refs[n_r + n_v + n_a:n_r + n_v + n_a + n_ro]
        v_out = refs[n_r + n_v + n_a + n_ro:]
        ro, vo = fn(*[r[...] for r in r_in], *[v[...] for v in v_in])
        for ref, val in zip(r_out, ro):
            ref[...] = val.astype(ref.dtype)
        if n_vo:
            @pl.when(pl.program_id(0) == 0)
            def _():
                for ref in v_out:
                    ref[...] = jnp.zeros(ref.shape, ref.dtype)
            for ref, val in zip(v_out, vo):
                ref[...] += val

    in_specs = [pl.BlockSpec((tile, a.shape[1]), lambda i: (i, 0)) for a in rows]
    in_specs += [pl.BlockSpec(a.shape, lambda i: (0, 0)) for a in vecs]
    in_specs += [pl.BlockSpec(memory_space=pl.ANY) for _ in after]
    out_specs = [pl.BlockSpec((tile, w), lambda i: (i, 0)) for w, _ in row_outs]
    out_specs += [pl.BlockSpec((1, w), lambda i: (0, 0)) for w in vec_outs]
    out_shape = [jax.ShapeDtypeStruct((S, w), dt) for w, dt in row_outs]
    out_shape += [jax.ShapeDtypeStruct((1, w), F32) for w in vec_outs]
    res = pl.pallas_call(
        body, name=name, grid=(S // tile,), in_specs=in_specs, out_specs=out_specs, out_shape=out_shape,
        compiler_params=_params(("arbitrary",)),
    )(*rows, *vecs, *after)
    return res[:n_ro], res[n_ro:]


def _mm_nn(name, a, b3, out_dtypes, epilogue=None, extras=()):
    M, K = a.shape
    nB, K2, Nsh = b3.shape
    assert K == K2
    N = nB * Nsh
    tm, tn, tk = _pick(M, 1024, 16), _pick(Nsh, 1536, LANE), _pick(K, 512, LANE)
    nps, nk, ne, no = Nsh // tn, K // tk, len(extras), len(out_dtypes)

    def body(a_ref, b_ref, *rest):
        e_refs, o_refs, acc = rest[:ne], rest[ne:ne + no], rest[-1]
        k = pl.program_id(2)

        @pl.when(k == 0)
        def _():
            acc[...] = jnp.zeros(acc.shape, F32)

        acc[...] += jnp.dot(a_ref[...].astype(BF16), b_ref[...].astype(BF16), preferred_element_type=F32)

        @pl.when(k == nk - 1)
        def _():
            vals = (acc[...],) if epilogue is None else epilogue(acc[...], *[e[...] for e in e_refs])
            for ref, val in zip(o_refs, vals):
                ref[...] = val.astype(ref.dtype)

    in_specs = [pl.BlockSpec((tm, tk), lambda i, j, k: (i, k)),
                pl.BlockSpec((None, tk, tn), lambda i, j, k: (j // nps, k, j % nps))]
    in_specs += [pl.BlockSpec((tm, tn), lambda i, j, k: (i, j)) for _ in extras]
    return pl.pallas_call(
        body, name=name, grid=(M // tm, N // tn, nk), in_specs=in_specs,
        out_specs=[pl.BlockSpec((tm, tn), lambda i, j, k: (i, j)) for _ in out_dtypes],
        out_shape=[jax.ShapeDtypeStruct((M, N), dt) for dt in out_dtypes],
        scratch_shapes=[pltpu.VMEM((tm, tn), F32)],
        compiler_params=_params(("parallel", "parallel", "arbitrary")),
    )(a, b3, *extras)


def _mm_nt(name, a, b3, out_dtypes, epilogue=None, extras=(), after=()):
    M, N = a.shape
    nB, K, Nsh = b3.shape
    assert N == nB * Nsh
    tm, to, tn = _pick(M, 1024, 16), _pick(K, 1024, LANE), _pick(Nsh, 512, LANE)
    nps, nn, ne, no, na = Nsh // tn, N // tn, len(extras), len(out_dtypes), len(after)

    def body(a_ref, b_ref, *rest):
        e_refs, o_refs, acc = rest[:ne], rest[ne + na:ne + na + no], rest[-1]
        n = pl.program_id(2)

        @pl.when(n == 0)
        def _():
            acc[...] = jnp.zeros(acc.shape, F32)

        acc[...] += lax.dot_general(a_ref[...].astype(BF16), b_ref[...].astype(BF16),
                                    (((1,), (1,)), ((), ())), preferred_element_type=F32)

        @pl.when(n == nn - 1)
        def _():
            vals = (acc[...],) if epilogue is None else epilogue(acc[...], *[e[...] for e in e_refs])
            for ref, val in zip(o_refs, vals):
                ref[...] = val.astype(ref.dtype)

    in_specs = [pl.BlockSpec((tm, tn), lambda i, o, n: (i, n)),
                pl.BlockSpec((None, to, tn), lambda i, o, n: (n // nps, o, n % nps))]
    in_specs += [pl.BlockSpec((tm, to), lambda i, o, n: (i, o)) for _ in extras]
    in_specs += [pl.BlockSpec(memory_space=pl.ANY) for _ in after]
    return pl.pallas_call(
        body, name=name, grid=(M // tm, K // to, nn), in_specs=in_specs,
        out_specs=[pl.BlockSpec((tm, to), lambda i, o, n: (i, o)) for _ in out_dtypes],
        out_shape=[jax.ShapeDtypeStruct((M, K), dt) for dt in out_dtypes],
        scratch_shapes=[pltpu.VMEM((tm, to), F32)],
        compiler_params=_params(("parallel", "parallel", "arbitrary")),
    )(a, b3, *extras, *after)


def _mm_tn(name, a, g, nB):
    M, K = a.shape
    M2, N = g.shape
    assert M == M2 and N % nB == 0
    Nsh = N // nB
    tk, tn, tm = _pick(K, 1024, LANE), _pick(Nsh, 1024, LANE), _pick(M, 512, 16)
    nps, nm = Nsh // tn, M // tm

    def body(a_ref, g_ref, o_ref, acc):
        m = pl.program_id(2)

        @pl.when(m == 0)
        def _():
            acc[...] = jnp.zeros(acc.shape, F32)

        acc[...] += lax.dot_general(a_ref[...].astype(BF16), g_ref[...].astype(BF16),
                                    (((0,), (0,)), ((), ())), preferred_element_type=F32)

        @pl.when(m == nm - 1)
        def _():
            o_ref[...] = acc[...]

    return pl.pallas_call(
        body, name=name, grid=(K // tk, N // tn, nm),
        in_specs=[pl.BlockSpec((tm, tk), lambda k, n, m: (m, k)),
                  pl.BlockSpec((tm, tn), lambda k, n, m: (m, n))],
        out_specs=pl.BlockSpec((None, tk, tn), lambda k, n, m: (n // nps, k, n % nps)),
        out_shape=jax.ShapeDtypeStruct((nB, K, Nsh), F32),
        scratch_shapes=[pltpu.VMEM((tk, tn), F32)],
        compiler_params=_params(("parallel", "parallel", "arbitrary")),
    )(a, g)


def _softmax(s):
    m = jnp.max(s, axis=-1, keepdims=True)
    e = jnp.exp(s - m)
    return e * (1.0 / jnp.sum(e, axis=-1, keepdims=True))


def _attn_fwd(name, q, k, v, n_heads, group, dqk, dv, scale, tq):
    S = q.shape[0]

    def body(q_ref, k_ref, v_ref, o_ref):
        s = lax.dot_general(q_ref[...], k_ref[...], (((1,), (1,)), ((), ())), preferred_element_type=F32) * scale
        p = _softmax(s)
        o_ref[...] = jnp.dot(p.astype(BF16), v_ref[...], preferred_element_type=F32)

    return pl.pallas_call(
        body, name=name, grid=(n_heads, S // tq),
        in_specs=[pl.BlockSpec((tq, dqk), lambda h, i: (i, h)),
                  pl.BlockSpec((S, dqk), lambda h, i: (0, h // group)),
                  pl.BlockSpec((S, dv), lambda h, i: (0, h // group))],
        out_specs=pl.BlockSpec((tq, dv), lambda h, i: (i, h)),
        out_shape=jax.ShapeDtypeStruct((S, n_heads * dv), F32),
        compiler_params=_params(("arbitrary", "arbitrary")),
    )(q, k, v)


def _attn_bwd(name, q, k, v, do, n_heads, group, dqk, dv, scale, tq):
    S = q.shape[0]
    n_kv = n_heads // group

    def body(q_ref, k_ref, v_ref, do_ref, dq_ref, dk_ref, dv_ref):
        h, i = pl.program_id(0), pl.program_id(1)

        @pl.when((h % group == 0) & (i == 0))
        def _():
            dk_ref[...] = jnp.zeros(dk_ref.shape, F32)
            dv_ref[...] = jnp.zeros(dv_ref.shape, F32)

        qb, kb, vb, dob = q_ref[...], k_ref[...], v_ref[...], do_ref[...]
        s = lax.dot_general(qb, kb, (((1,), (1,)), ((), ())), preferred_element_type=F32) * scale
        p = _softmax(s)
        dv_ref[...] += lax.dot_general(p.astype(BF16), dob, (((0,), (0,)), ((), ())), preferred_element_type=F32)
        dp = lax.dot_general(dob, vb, (((1,), (1,)), ((), ())), preferred_element_type=F32)
        ds = (p * (dp - jnp.sum(dp * p, axis=-1, keepdims=True)) * scale).astype(BF16)
        dq_ref[...] = jnp.dot(ds, kb, preferred_element_type=F32)
        dk_ref[...] += lax.dot_general(ds, qb, (((0,), (0,)), ((), ())), preferred_element_type=F32)

    return pl.pallas_call(
        body, name=name, grid=(n_heads, S // tq),
        in_specs=[pl.BlockSpec((tq, dqk), lambda h, i: (i, h)),
                  pl.BlockSpec((S, dqk), lambda h, i: (0, h // group)),
                  pl.BlockSpec((S, dv), lambda h, i: (0, h // group)),
                  pl.BlockSpec((tq, dv), lambda h, i: (i, h))],
        out_specs=[pl.BlockSpec((tq, dqk), lambda h, i: (i, h)),
                   pl.BlockSpec((S, dqk), lambda h, i: (0, h // group)),
                   pl.BlockSpec((S, dv), lambda h, i: (0, h // group))],
        out_shape=[jax.ShapeDtypeStruct((S, n_heads * dqk), F32),
                   jax.ShapeDtypeStruct((S, n_kv * dqk), F32),
                   jax.ShapeDtypeStruct((S, n_kv * dv), F32)],
        compiler_params=_params(("arbitrary", "arbitrary")),
    )(q, k, v, do)


def _adamw(name, w, g, m, v):
    R, C = w.shape
    tile = _pick(R, max(8, (1 << 19) // C // 8 * 8), 8)
    c1 = 1.0 - ADAM_B1 ** ADAM_STEP
    c2 = 1.0 - ADAM_B2 ** ADAM_STEP

    def body(w_ref, g_ref, m_ref, v_ref, d_ref, nm_ref, nv_ref):
        gg = g_ref[...]
        nm = ADAM_B1 * m_ref[...] + (1.0 - ADAM_B1) * gg
        nv = ADAM_B2 * v_ref[...] + (1.0 - ADAM_B2) * (gg * gg)
        d_ref[...] = -ADAM_LR * ((nm / c1) / (jnp.sqrt(nv / c2) + ADAM_EPS) + ADAM_WD * w_ref[...])
        nm_ref[...] = nm
        nv_ref[...] = nv

    spec = pl.BlockSpec((tile, C), lambda i: (i, 0))
    return pl.pallas_call(
        body, name=name, grid=(R // tile,), in_specs=[spec] * 4, out_specs=[spec] * 3,
        out_shape=[jax.ShapeDtypeStruct((R, C), F32)] * 3,
        compiler_params=_params(("parallel",)),
    )(w, g, m, v)


def _flip(me, rel):
    return tuple(1 - p if r else p for p, r in zip(me, rel))


def _push(name, ins, out_shapes, transfers, aliases=None):
    ni, no, nt = len(ins), len(out_shapes), len(transfers)

    def body(*refs):
        in_refs, out_refs = refs[:ni], refs[ni:ni + no]
        send_sems, recv_sems = refs[ni + no], refs[ni + no + 1]
        me = (lax.axis_index("x"), lax.axis_index("y"), lax.axis_index("c"))
        copies = []
        for t, (si, sfn, oi, dfn, rel) in enumerate(transfers):
            src_ref = out_refs[si[1]] if isinstance(si, tuple) else in_refs[si]
            src = src_ref.at[sfn(*me)]
            dst = out_refs[oi].at[dfn(*me)]
            if rel is None:
                cp = pltpu.make_async_copy(src, dst, send_sems.at[t])
            else:
                cp = pltpu.make_async_remote_copy(src_ref=src, dst_ref=dst, send_sem=send_sems.at[t],
                                                  recv_sem=recv_sems.at[t], device_id=_flip(me, rel),
                                                  device_id_type=MESH)
            cp.start()
            copies.append(cp)
        for cp in copies:
            cp.wait()

    any_spec = pl.BlockSpec(memory_space=pl.ANY)
    return pl.pallas_call(
        body, name=name, in_specs=[any_spec] * ni, out_specs=[any_spec] * no, out_shape=out_shapes,
        scratch_shapes=[pltpu.SemaphoreType.DMA((nt,)), pltpu.SemaphoreType.DMA((nt,))],
        input_output_aliases=aliases or {},
        compiler_params=pltpu.CompilerParams(has_side_effects=True),
    )(*ins)


_HBM = pl.BlockSpec(memory_space=pltpu.HBM)
_SEM = pl.BlockSpec(memory_space=pltpu.SEMAPHORE)
_DATAFLOW = pltpu.SideEffectType.DATAFLOW_SIDE_EFFECTING


def _split_copies(refs, transfers, send_sems, recv_sems):
    me = (lax.axis_index("x"), lax.axis_index("y"), lax.axis_index("c"))
    return [pltpu.make_async_remote_copy(src_ref=refs[sb].at[sfn(*me)], dst_ref=refs[db].at[dfn(*me)],
                                         send_sem=send_sems.at[t], recv_sem=recv_sems.at[t],
                                         device_id=_flip(me, rel), device_id_type=MESH)
            for t, (sb, sfn, db, dfn, rel) in enumerate(transfers)]


def _push_start(name, bufs, transfers, after=()):
    nb, na, nt = len(bufs), len(after), len(transfers)

    def body(*refs):
        send_sems, recv_sems, token = refs[nb + na], refs[nb + na + 1], refs[-1]
        for cp in _split_copies(refs[nb + na + 2:2 * nb + na + 2], transfers, send_sems, recv_sems):
            cp.start()
        token[...] = jnp.zeros(token.shape, token.dtype)

    res = pl.pallas_call(
        body, name=name,
        out_shape=(pltpu.SemaphoreType.DMA((nt,)), pltpu.SemaphoreType.DMA((nt,)),
                   *[pltpu.HBM(b.shape, b.dtype) for b in bufs], jax.ShapeDtypeStruct((8, LANE), F32)),
        in_specs=[_HBM] * nb + [pl.BlockSpec(memory_space=pl.ANY)] * na,
        out_specs=(_SEM, _SEM, *[_HBM] * nb, pl.BlockSpec(memory_space=pltpu.VMEM)),
        input_output_aliases={i: i + 2 for i in range(nb)},
        compiler_params=pltpu.CompilerParams(has_side_effects=_DATAFLOW),
    )(*[pltpu.with_memory_space_constraint(b, pltpu.HBM) for b in bufs], *after)
    return res[0], res[1], list(res[2:2 + nb]), res[-1]


def _push_wait(name, send_sems, recv_sems, bufs, transfers, after):
    nb = len(bufs)

    def body(*refs):
        for cp in _split_copies(refs[:nb], transfers, refs[nb], refs[nb + 1]):
            cp.wait_send()
            cp.wait_recv()

    res = pl.pallas_call(
        body, name=name, out_shape=[pltpu.HBM(b.shape, b.dtype) for b in bufs],
        in_specs=[_HBM] * nb + [_SEM, _SEM, pl.BlockSpec(memory_space=pl.ANY)], out_specs=[_HBM] * nb,
        input_output_aliases={i: i for i in range(nb)},
        compiler_params=pltpu.CompilerParams(has_side_effects=_DATAFLOW),
    )(*bufs, send_sems, recv_sems, after)
    return list(res)


ICI_RELS = ((1, 0, 0), (0, 1, 0), (1, 1, 0))
SIBLING = (0, 0, 1)


def _chip(x, y):
    return 2 * x + y


def _cast_into_slot(name, where, w):
    R, C = w.shape
    tile = _pick(R, 256, 16)

    def body(where_ref, w_ref, o_ref):
        o_ref[...] = w_ref[...].astype(BF16)

    return pl.pallas_call(
        body, name=name,
        grid_spec=pltpu.PrefetchScalarGridSpec(
            num_scalar_prefetch=1, grid=(R // tile,),
            in_specs=[pl.BlockSpec((tile, C), lambda i, wh: (i, 0))],
            out_specs=pl.BlockSpec((None, tile, C), lambda i, wh: (wh[1], i, 0))),
        out_shape=jax.ShapeDtypeStruct((N_CHIPS, R, C), BF16),
        compiler_params=_params(("parallel",)),
    )(where, w)


def _gather_start(tag, where, shards, after):
    n = len(shards)
    own = [_cast_into_slot(f"cast_w_{tag}{w}", where, shards[w]) for w in range(n)]
    transfers = []
    for w in range(n):
        def idx(x, y, c, rh=shards[w].shape[0] // 2):
            return (_chip(x, y), pl.ds(c * rh, rh), slice(None))
        transfers += [(w, idx, w, idx, rel) for rel in ICI_RELS]
    send_sems, recv_sems, bufs, token = _push_start(f"gather_{tag}_start", own, transfers, after)
    return (tag, send_sems, recv_sems, bufs, transfers), token


def _gather_finish(handle, after):
    tag, send_sems, recv_sems, bufs, transfers = handle
    part = _push_wait(f"gather_{tag}_wait", send_sems, recv_sems, bufs, transfers, after)
    n = len(part)
    halves = [p.shape[1] // 2 for p in part]
    outs = [jax.ShapeDtypeStruct(p.shape, p.dtype) for p in part]

    second = []
    for w in range(n):
        rh = halves[w]
        for rel in ICI_RELS:
            def idx(x, y, c, rh=rh, rel=rel):
                return (_chip(x ^ rel[0], y ^ rel[1]), pl.ds(c * rh, rh), slice(None))
            second.append((("out", w), idx, w, idx, SIBLING))
    return _push(f"gather_{tag}_d2d", list(part), outs, second, aliases={w: w for w in range(n)})


def _reduce_start(tag, where, grads, after=()):
    n = len(grads)
    halves = [g.shape[1] // 2 for g in grads]
    t1 = []
    for w in range(n):
        t1.append((w, lambda x, y, c, rh=halves[w]: (slice(None), pl.ds((1 - c) * rh, rh), slice(None)), w,
                   lambda x, y, c: (slice(None), slice(None), slice(None)), SIBLING))
    from_sib = _push(f"reduce_{tag}_d2d", list(grads),
                     [jax.ShapeDtypeStruct((N_CHIPS, halves[w], grads[w].shape[2]), F32) for w in range(n)], t1)
    pair = [_pair_sum(f"pair_sum_{tag}{w}", where, grads[w], from_sib[w]) for w in range(n)]
    land = [lax.empty((3, halves[w], grads[w].shape[2]), BF16) for w in range(n)]
    t3 = []
    for w in range(n):
        for r, rel in enumerate(ICI_RELS):
            t3.append((w, lambda x, y, c, rel=rel: (_chip(x ^ rel[0], y ^ rel[1]), slice(None), slice(None)), n + w,
                       lambda x, y, c, r=r: (r, slice(None), slice(None)), rel))
    send_sems, recv_sems, bufs, token = _push_start(f"reduce_{tag}_start", pair + land, t3, after)
    return (tag, send_sems, recv_sems, bufs, t3, list(grads), list(from_sib)), token


def _reduce_finish(handle, where, after):
    tag, send_sems, recv_sems, bufs, t3, grads, from_sib = handle
    n = len(grads)
    halves = [g.shape[1] // 2 for g in grads]
    from_chips = _push_wait(f"reduce_{tag}_wait", send_sems, recv_sems, bufs, t3, after)[n:]
    mine = [_chip_sum(f"chip_sum_{tag}{w}", where, grads[w], from_sib[w], from_chips[w]) for w in range(n)]
    t5 = []
    for w in range(n):
        def idx(x, y, c, rh=halves[w]):
            return (pl.ds(c * rh, rh), slice(None))
        t5.append((("out", w), idx, w, idx, SIBLING))
    return _push(f"reduce_{tag}_swap", mine,
                 [jax.ShapeDtypeStruct((2 * halves[w], grads[w].shape[2]), F32) for w in range(n)], t5,
                 aliases={w: w for w in range(n)})


def _pair_sum(name, where, g, from_sib):
    _, R, C = g.shape
    rh = R // 2
    tile = _pick(rh, max(16, (1 << 19) // C // 16 * 16), 16)
    nb = rh // tile

    def body(where_ref, g_ref, s_ref, o_ref):
        o_ref[...] = (g_ref[...] + s_ref[...]).astype(BF16)

    return pl.pallas_call(
        body, name=name,
        grid_spec=pltpu.PrefetchScalarGridSpec(
            num_scalar_prefetch=1, grid=(N_CHIPS, nb),
            in_specs=[pl.BlockSpec((None, tile, C), lambda j, i, wh: (j, wh[0] * nb + i, 0)),
                      pl.BlockSpec((None, tile, C), lambda j, i, wh: (j, i, 0))],
            out_specs=pl.BlockSpec((None, tile, C), lambda j, i, wh: (j, i, 0))),
        out_shape=jax.ShapeDtypeStruct((N_CHIPS, rh, C), BF16),
        compiler_params=_params(("parallel", "parallel")),
    )(where, g, from_sib)


def _chip_sum(name, where, g, from_sib, from_chips):
    _, R, C = g.shape
    rh = R // 2
    tile = _pick(rh, max(16, (1 << 19) // C // 16 * 16), 16)
    nb = rh // tile

    def body(where_ref, g_ref, s_ref, r_ref, o_ref):
        acc = g_ref[...] + s_ref[...]
        for r in range(3):
            acc = acc + r_ref[r].astype(F32)
        o_ref[...] = acc

    return pl.pallas_call(
        body, name=name,
        grid_spec=pltpu.PrefetchScalarGridSpec(
            num_scalar_prefetch=1, grid=(nb,),
            in_specs=[pl.BlockSpec((None, tile, C), lambda i, wh: (wh[1], wh[0] * nb + i, 0)),
                      pl.BlockSpec((None, tile, C), lambda i, wh: (wh[1], i, 0)),
                      pl.BlockSpec((3, tile, C), lambda i, wh: (0, i, 0))],
            out_specs=pl.BlockSpec((tile, C), lambda i, wh: (wh[0] * nb + i, 0))),
        out_shape=jax.ShapeDtypeStruct((R, C), F32),
        compiler_params=_params(("parallel",)),
    )(where, g, from_sib, from_chips)


def _allgather8(name, blk):
    m_per, n = blk.shape
    rels = [(0, 0, 1), (1, 0, 0), (0, 1, 0), (1, 1, 0), (1, 0, 1), (0, 1, 1), (1, 1, 1)]

    def body(x_ref, out_ref, send_sems, recv_sems):
        me = (lax.axis_index("x"), lax.axis_index("y"), lax.axis_index("c"))
        my_rows = out_ref.at[pl.ds((4 * me[0] + 2 * me[1] + me[2]) * m_per, m_per), :]
        out_ref[pl.ds((4 * me[0] + 2 * me[1] + me[2]) * m_per, m_per), :] = x_ref[...]
        copies = []
        for t, rel in enumerate(rels):
            cp = pltpu.make_async_remote_copy(src_ref=x_ref, dst_ref=my_rows, send_sem=send_sems.at[t],
                                              recv_sem=recv_sems.at[t], device_id=_flip(me, rel),
                                              device_id_type=MESH)
            cp.start()
            copies.append(cp)
        for cp in copies:
            cp.wait()

    return pl.pallas_call(
        body, name=name, out_shape=jax.ShapeDtypeStruct((N_DEV * m_per, n), blk.dtype),
        in_specs=[pl.BlockSpec(memory_space=pltpu.VMEM)], out_specs=pl.BlockSpec(memory_space=pltpu.VMEM),
        scratch_shapes=[pltpu.SemaphoreType.DMA((7,)), pltpu.SemaphoreType.DMA((7,))],
        compiler_params=pltpu.CompilerParams(has_side_effects=True, vmem_limit_bytes=VMEM_LIMIT),
    )(blk)


def _rope_tables(seq_len, dim, reps):
    rows = seq_len // GRID_W
    t = np.arange(seq_len)
    row, col = (t // GRID_W).astype(np.float32), (t % GRID_W).astype(np.float32)
    half = dim // 2
    inv = jnp.asarray(ROPE_THETA, F32) ** (-jnp.arange(0, half, 2, dtype=F32) / half)
    ang_r = jnp.asarray(row)[:, None] * inv[None, :]
    ang_c = jnp.asarray(col)[:, None] * inv[None, :]
    ang = jnp.concatenate([ang_r, ang_r, ang_c, ang_c], axis=-1)
    cos, sin = jnp.cos(ang), jnp.sin(ang)
    low = (np.arange(dim) % (dim // 2)) < (dim // 4)
    s_lo = jnp.where(jnp.asarray(low)[None, :], -sin, 0.0)
    s_hi = jnp.where(jnp.asarray(low)[None, :], 0.0, sin)
    del rows
    return tuple(jnp.tile(a, (1, reps)) for a in (cos, s_lo, s_hi))


def _make_rope(shift):
    def up(v):
        return pltpu.roll(v, LANE - shift, 1)

    def down(v):
        return pltpu.roll(v, shift, 1)

    @jax.custom_vjp
    def rope(v, cos, s_lo, s_hi):
        return v * cos + up(v) * s_lo + down(v) * s_hi

    def fwd(v, cos, s_lo, s_hi):
        return rope(v, cos, s_lo, s_hi), (cos, s_lo, s_hi)

    def bwd(res, dy):
        cos, s_lo, s_hi = res
        return dy * cos + down(dy * s_lo) + up(dy * s_hi), jnp.zeros_like(cos), jnp.zeros_like(cos), jnp.zeros_like(cos)

    rope.defvjp(fwd, bwd)
    return rope


_rope_a = _make_rope(HEAD_DIM // 4)
_rope_b = _make_rope(QK_ROPE // 4)


def _prep(segs, tabs, g_q, g_k, g_ckv):
    qa, ka, va, qn, qp, ckv, kpe = segs
    ta, tb = tabs
    q_a = jnp.concatenate([_rope_a(_rn(s) * g_q, *ta) for s in qa], axis=1)
    k_a = jnp.concatenate([_rope_a(_rn(s) * g_k, *ta) for s in ka], axis=1)
    cat = []
    for h in range(HB):
        cat += [qn[h], _rope_b(qp[h], *tb)]
    q_cat = jnp.concatenate(cat, axis=1)
    return q_a, k_a, va, q_cat, _rn(ckv) * g_ckv, _rope_b(kpe, *tb)


def _split_proj(proj):
    qa = [proj[:, O_QA + h * LANE:O_QA + (h + 1) * LANE] for h in range(HA)]
    ka = [proj[:, O_KA + h * LANE:O_KA + (h + 1) * LANE] for h in range(HKV)]
    va = proj[:, O_VA:O_QN]
    qn = [proj[:, O_QN + h * LANE:O_QN + (h + 1) * LANE] for h in range(HB)]
    qp = [proj[:, O_QP + h * LANE:O_QP + (h + 1) * LANE] for h in range(HB)]
    return qa, ka, va, qn, qp, proj[:, O_CKV:O_KPE], proj[:, O_KPE:W_INP]


def _permute_w_in(nat):
    K = nat.shape[0]
    qb = nat[:, W_QA + 2 * W_KA:W_QA + 2 * W_KA + W_QB].reshape(K, HB, QK_B)
    z = lambda w: jnp.zeros((K, w), nat.dtype)
    qp = jnp.concatenate([qb[:, :, QK_NOPE:], jnp.zeros((K, HB, LANE - QK_ROPE), nat.dtype)], axis=2)
    tail = nat[:, W_QA + 2 * W_KA + W_QB:]
    return jnp.concatenate([nat[:, :O_QN], qb[:, :, :QK_NOPE].reshape(K, HB * QK_NOPE), qp.reshape(K, HB * LANE),
                            tail, z(LANE - QK_ROPE)], axis=1)


def _unpermute_w_in(p):
    K = p.shape[0]
    qn = p[:, O_QN:O_QP].reshape(K, HB, QK_NOPE)
    qp = p[:, O_QP:O_CKV].reshape(K, HB, LANE)[:, :, :QK_ROPE]
    qb = jnp.concatenate([qn, qp], axis=2).reshape(K, W_QB)
    return jnp.concatenate([p[:, :O_QN], qb, p[:, O_CKV:O_KPE + QK_ROPE]], axis=1)


def kernel(x, c, w_ada, b_ada, g_pre_attn, w_in, g_q_a, g_k_a, g_ckv, w_kv_b, g_out_a, g_out_b, w_out, g_post_attn, g_pre_mlp, w_mlp_in, w_mlp_out, g_post_mlp, loss_target, m_w_ada, m_b_ada, m_g_pre_attn, m_w_in, m_g_q_a, m_g_k_a, m_g_ckv, m_w_kv_b, m_g_out_a, m_g_out_b, m_w_out, m_g_post_attn, m_g_pre_mlp, m_w_mlp_in, m_w_mlp_out, m_g_post_mlp, v_w_ada, v_b_ada, v_g_pre_attn, v_w_in, v_g_q_a, v_g_k_a, v_g_ckv, v_w_kv_b, v_g_out_a, v_g_out_b, v_w_out, v_g_post_attn, v_g_pre_mlp, v_w_mlp_in, v_w_mlp_out, v_g_post_mlp):
    S, D = x.shape[1], x.shape[2]
    x2d, tgt = x[0], loss_target[0]
    ix, iy, ic = lax.axis_index("x"), lax.axis_index("y"), lax.axis_index("c")
    j_me = _chip(ix, iy)
    e_me = 4 * ix + 2 * iy + ic
    T = _pick(S, 256, 8)
    TB = _pick(S, 128, 8)

    where = jnp.stack([ic, j_me]).astype(jnp.int32)
    DFF = w_mlp_out.shape[1] * N_CHIPS

    c_all = _allgather8("gather_c", c.reshape(8, D // 8)).reshape(N_DEV, D)

    def silu_fn(cc):
        return (cc * (1.0 / (1.0 + jnp.exp(-cc))),), ()

    (c_act,), _ = _rowwise("silu_c", silu_fn, [c_all], [], [(D, BF16)], [], N_DEV)
    ca_pad = jnp.concatenate([c_act, jnp.zeros_like(c_act)], axis=0)
    n_mod = w_ada.shape[2]
    b_cols = lax.dynamic_slice(b_ada, (0, j_me * n_mod), (1, n_mod))
    (mod_part,) = _mm_nn("mod_mm", ca_pad, w_ada, [F32], epilogue=lambda acc, b: (acc + b,),
                         extras=(jnp.broadcast_to(b_cols, (16, n_mod)),))
    mod_all = _allgather8("gather_mod", mod_part[:8]).reshape(N_DEV, 8, n_mod)
    mod = jnp.concatenate([lax.dynamic_slice(mod_all, (2 * j, e_me, 0), (1, 1, n_mod))[0] for j in range(N_CHIPS)],
                          axis=1)

    h_qkv, tok_a = _gather_start("qkv", where, [w_in[0], w_kv_b[0]], (mod_all,))
    h_out, tok_b = _gather_start("out", where, [w_out[0]], (tok_a,))
    h_up, tok_c = _gather_start("up", where, [w_mlp_in[0]], (tok_b,))
    h_down, tok_d = _gather_start("down", where, [w_mlp_out[0]], (tok_c,))
    mod = mod + tok_d[0, 0]
    sh_a, sc_a, gt_a, sh_m, sc_m, gt_m = [mod[:, i * D:(i + 1) * D] for i in range(6)]

    tabs_a = _rope_tables(S, HEAD_DIM, 1)
    tabs_b = _rope_tables(S, QK_ROPE, LANE // QK_ROPE)

    def pre_attn_fn(xt, g, sc, sh):
        return (_rn(xt) * g * (1.0 + sc) + sh,), ()

    (h_b,), _ = _rowwise("pre_attn", pre_attn_fn, [x2d], [g_pre_attn, sc_a, sh_a], [(D, BF16)], [], T)
    g_in, g_kvb = _gather_finish(h_qkv, h_b)
    wp = _permute_w_in(jnp.transpose(g_in, (1, 0, 2)).reshape(D, W_IN))[None]
    (proj,) = _mm_nn("proj_mm", h_b, wp, [F32])

    def prep_fn(pt, ca_, sla, sha, cb_, slb, shb, gq, gk, gc):
        return _prep(_split_proj(pt), ((ca_, sla, sha), (cb_, slb, shb)), gq, gk, gc), ()

    (qa_b, ka_b, va_b, qcat_b, ckvn_b, kpe_b), _ = _rowwise(
        "prep", prep_fn, [proj, *tabs_a, *tabs_b], [g_q_a, g_k_a, g_ckv],
        [(W_QA, BF16), (W_KA, BF16), (W_KA, BF16), (2 * HB * LANE, BF16), (KV_RANK, BF16), (LANE, BF16)], [], TB)
    (kv,) = _mm_nn("kv_mm", ckvn_b, g_kvb, [F32])

    def kcat_fn(kvt, kpet):
        kc, vs = [], []
        for h in range(HB):
            kc += [kvt[:, 2 * h * LANE:(2 * h + 1) * LANE], kpet.astype(F32)]
            vs.append(kvt[:, (2 * h + 1) * LANE:(2 * h + 2) * LANE])
        return (jnp.concatenate(kc, axis=1), jnp.concatenate(vs, axis=1)), ()

    (kcat_b, vb_b), _ = _rowwise("kcat", kcat_fn, [kv, kpe_b], [], [(2 * HB * LANE, BF16), (HB * V_DIM, BF16)], [], T)

    TQ = _pick(S, 256, 16)
    sc_a_ = 1.0 / math.sqrt(HEAD_DIM)
    sc_b_ = 1.0 / math.sqrt(QK_B)
    o_a = _attn_fwd("attn_a_fwd", qa_b, ka_b, va_b, HA, HA // HKV, HEAD_DIM, HEAD_DIM, sc_a_, TQ)
    o_b = _attn_fwd("attn_b_fwd", qcat_b, kcat_b, vb_b, HB, 1, 2 * LANE, V_DIM, sc_b_, TQ)

    def mix_fn(oa, ob, ga, gb):
        return (jnp.concatenate([_rn(oa) * ga, _rn(ob) * gb], axis=1),), ()

    (on_b,), _ = _rowwise("mix_norm", mix_fn, [o_a, o_b], [g_out_a, g_out_b], [(2 * W_QA, BF16)], [], T)
    (g_wout,) = _gather_finish(h_out, on_b)
    w_out_f = g_wout.reshape(1, N_CHIPS * g_wout.shape[1], D)
    (o2,) = _mm_nn("out_mm", on_b, w_out_f, [F32])

    def mid(xt, o2t, gta, gpa, gpm, scm, shm):
        x1 = xt + gta * (_rn(o2t) * gpa)
        return x1, _rn(x1) * gpm * (1.0 + scm) + shm

    def mid_fn(*a):
        return mid(*a), ()

    mid_vecs = [gt_a, g_post_attn, g_pre_mlp, sc_m, sh_m]
    (x1, h2_b), _ = _rowwise("mid", mid_fn, [x2d, o2], mid_vecs, [(D, F32), (D, BF16)], [], T)
    (g_mlp_in,) = _gather_finish(h_up, h2_b)
    u, a_b = _mm_nn("mlp_in_mm", h2_b, g_mlp_in, [F32, BF16],
                    epilogue=lambda acc: (acc, jnp.square(jnp.maximum(acc, 0.0))))
    (g_mlp_out,) = _gather_finish(h_down, a_b)
    w_mlp_out_f = g_mlp_out.reshape(1, DFF, D)
    (y,) = _mm_nn("mlp_out_mm", a_b, w_mlp_out_f, [F32])

    def last(yt, x1t, gtm, gpo):
        return x1t + gtm * (_rn(yt) * gpo)

    def last_fn(yt, x1t, tt, gtm, gpo):
        x2, vjp = jax.vjp(last, yt, x1t, gtm, gpo)
        err = x2 - tt
        dy, dx1, dgt, dg = vjp(err * (1.0 / D))
        loss = 0.5 * jnp.sum(jnp.mean(err * err, axis=-1))
        return (dy, dx1), (jnp.full((1, LANE), loss, F32), dgt, dg)

    (dy_b, dx2), (loss_v, d_gt_m, d_g_post_mlp) = _rowwise(
        "loss_bwd", last_fn, [y, x1, tgt], [gt_m, g_post_mlp], [(D, BF16), (D, F32)], [LANE, D, D], TB)

    (du_b,) = _mm_nt("d_mlp_out_mm", dy_b, w_mlp_out_f, [BF16],
                     epilogue=lambda acc, ut: (acc * (2.0 * jnp.maximum(ut, 0.0)),), extras=(u,))
    gw_mlp_out = _mm_tn("gw_mlp_out_mm", a_b, dy_b, 1).reshape(N_CHIPS, DFF // N_CHIPS, D)
    r_down, tok_down = _reduce_start("down", where, [gw_mlp_out])
    (dh2,) = _mm_nt("d_mlp_in_mm", du_b, g_mlp_in, [F32], after=(tok_down,))
    gw_mlp_in = _mm_tn("gw_mlp_in_mm", h2_b, du_b, N_CHIPS)
    r_up, tok_up = _reduce_start("up", where, [gw_mlp_in])

    def mid_bwd_fn(xt, o2t, dh2t, dx2t, *vecs):
        _, vjp = jax.vjp(mid, xt, o2t, *vecs)
        dx, do2, dgta, dgpa, dgpm, dscm, dshm = vjp((dx2t, dh2t))
        return (dx, do2), (dgta, dgpa, dgpm, dscm, dshm)

    (dx1, do2_b), (d_gt_a, d_g_post_attn, d_g_pre_mlp, d_sc_m, d_sh_m) = _rowwise(
        "mid_bwd", mid_bwd_fn, [x2d, o2, dh2, dx2], mid_vecs, [(D, F32), (D, BF16)], [D] * 5, TB, after=(tok_up,))

    (d_on,) = _mm_nt("d_out_mm", do2_b, w_out_f, [F32])
    gw_out = _mm_tn("gw_out_mm", on_b, do2_b, 1).reshape(N_CHIPS, w_out.shape[1], D)
    r_out, tok_out = _reduce_start("out", where, [gw_out])

    def mix_bwd_fn(oa, ob, dont, ga, gb):
        _, vjp = jax.vjp(lambda a_, b_, g1, g2: mix_fn(a_, b_, g1, g2)[0][0], oa, ob, ga, gb)
        doa, dob, dga, dgb = vjp(dont)
        return (doa, dob), (dga, dgb)

    (doa_b, dob_b), (d_g_out_a, d_g_out_b) = _rowwise(
        "mix_bwd", mix_bwd_fn, [o_a, o_b, d_on], [g_out_a, g_out_b], [(W_QA, BF16), (HB * V_DIM, BF16)],
        [W_QA, HB * V_DIM], T, after=(tok_out,))

    dqa, dka, dva = _attn_bwd("attn_a_bwd", qa_b, ka_b, va_b, doa_b, HA, HA // HKV, HEAD_DIM, HEAD_DIM, sc_a_, TQ)
    dqcat, dkcat, dvb = _attn_bwd("attn_b_bwd", qcat_b, kcat_b, vb_b, dob_b, HB, 1, 2 * LANE, V_DIM, sc_b_, TQ)

    def kcat_bwd_fn(dkc, dvt):
        dkv, dkpe = [], None
        for h in range(HB):
            dkv += [dkc[:, 2 * h * LANE:(2 * h + 1) * LANE], dvt[:, h * V_DIM:(h + 1) * V_DIM]]
            piece = dkc[:, (2 * h + 1) * LANE:(2 * h + 2) * LANE]
            dkpe = piece if dkpe is None else dkpe + piece
        return (jnp.concatenate(dkv, axis=1), dkpe), ()

    (dkv_b, dkpe), _ = _rowwise("kcat_bwd", kcat_bwd_fn, [dkcat, dvb], [], [(2 * HB * LANE, BF16), (LANE, F32)], [], T)
    (d_ckvn,) = _mm_nt("d_kv_mm", dkv_b, g_kvb, [F32])
    gw_kvb = _mm_tn("gw_kv_mm", ckvn_b, dkv_b, N_CHIPS)
    r_kvb, tok_kvb = _reduce_start("kvb", where, [gw_kvb])

    def prep_bwd_fn(pt, dqat, dkat, dvat, dqct, dckt, dkpt, ca_, sla, sha, cb_, slb, shb, gq, gk, gc):
        tabs = ((ca_, sla, sha), (cb_, slb, shb))
        _, vjp = jax.vjp(lambda sg, a1, a2, a3: _prep(sg, tabs, a1, a2, a3), _split_proj(pt), gq, gk, gc)
        (dqa_s, dka_s, dva_s, dqn_s, dqp_s, dck_s, dkp_s), dgq, dgk, dgc = vjp((dqat, dkat, dvat, dqct, dckt, dkpt))
        dproj = jnp.concatenate([*dqa_s, *dka_s, dva_s, *dqn_s, *dqp_s, dck_s, dkp_s], axis=1)
        return (dproj,), (dgq, dgk, dgc)

    (dproj_b,), (d_g_q_a, d_g_k_a, d_g_ckv) = _rowwise(
        "prep_bwd", prep_bwd_fn, [proj, dqa, dka, dva, dqcat, d_ckvn, dkpe, *tabs_a, *tabs_b],
        [g_q_a, g_k_a, g_ckv], [(W_INP, BF16)], [HEAD_DIM, HEAD_DIM, KV_RANK], TB, after=(tok_kvb,))

    (dh,) = _mm_nt("d_proj_mm", dproj_b, wp, [F32])
    gwp = _mm_tn("gw_in_mm", h_b, dproj_b, 1)[0]
    gw_in = jnp.transpose(_unpermute_w_in(gwp).reshape(D, N_CHIPS, W_IN // N_CHIPS), (1, 0, 2))

    def pre_attn_bwd_fn(xt, dht, dx1t, g, sc, sh):
        _, vjp = jax.vjp(lambda a_, b_, c_, d_: pre_attn_fn(a_, b_, c_, d_)[0][0], xt, g, sc, sh)
        dxt, dg, dsc, dsh = vjp(dht)
        return (dx1t + dxt,), (dg, dsc, dsh)

    (grad_x,), (d_g_pre_attn, d_sc_a, d_sh_a) = _rowwise(
        "pre_attn_bwd", pre_attn_bwd_fn, [x2d, dh, dx1], [g_pre_attn, sc_a, sh_a], [(D, F32)], [D] * 3, TB)

    small_names = ["b_ada", "g_pre_attn", "g_q_a", "g_k_a", "g_ckv", "g_out_a", "g_out_b", "g_post_attn",
                   "g_pre_mlp", "g_post_mlp"]
    small_local = jnp.concatenate([d_sh_a, d_sc_a, d_gt_a, d_sh_m, d_sc_m, d_gt_m, d_g_pre_attn, d_g_q_a, d_g_k_a,
                                   d_g_ckv, d_g_out_a, d_g_out_b, d_g_post_attn, d_g_pre_mlp, d_g_post_mlp], axis=1)
    n_small = small_local.shape[1]
    small_all = _allgather8("gather_small", small_local.reshape(8, n_small // 8)).reshape(N_DEV, n_small)

    def sum8_fn(t):
        return (), (jnp.sum(t, axis=0, keepdims=True),)

    r_in, tok_in = _reduce_start("in", where, [gw_in], (small_all,))
    _, (small_sum,) = _rowwise("sum_small", sum8_fn, [small_all], [], [], [n_small], N_DEV, after=(tok_in,))
    dmod_cols = lax.dynamic_slice(small_all, (0, j_me * n_mod), (N_DEV, n_mod))
    dmod_pad = jnp.concatenate([dmod_cols, jnp.zeros_like(dmod_cols)], axis=0).astype(BF16)
    gw_ada = _mm_tn("gw_ada_mm", ca_pad, dmod_pad, 1)[0]

    weights = dict(w_ada=w_ada, b_ada=b_ada, g_pre_attn=g_pre_attn, w_in=w_in, g_q_a=g_q_a, g_k_a=g_k_a, g_ckv=g_ckv,
                   w_kv_b=w_kv_b, g_out_a=g_out_a, g_out_b=g_out_b, w_out=w_out, g_post_attn=g_post_attn,
                   g_pre_mlp=g_pre_mlp, w_mlp_in=w_mlp_in, w_mlp_out=w_mlp_out, g_post_mlp=g_post_mlp)
    ms = dict(w_ada=m_w_ada, b_ada=m_b_ada, g_pre_attn=m_g_pre_attn, w_in=m_w_in, g_q_a=m_g_q_a, g_k_a=m_g_k_a,
              g_ckv=m_g_ckv, w_kv_b=m_w_kv_b, g_out_a=m_g_out_a, g_out_b=m_g_out_b, w_out=m_w_out,
              g_post_attn=m_g_post_attn, g_pre_mlp=m_g_pre_mlp, w_mlp_in=m_w_mlp_in, w_mlp_out=m_w_mlp_out,
              g_post_mlp=m_g_post_mlp)
    vs = dict(w_ada=v_w_ada, b_ada=v_b_ada, g_pre_attn=v_g_pre_attn, w_in=v_w_in, g_q_a=v_g_q_a, g_k_a=v_g_k_a,
              g_ckv=v_g_ckv, w_kv_b=v_w_kv_b, g_out_a=v_g_out_a, g_out_b=v_g_out_b, w_out=v_w_out,
              g_post_attn=v_g_post_attn, g_pre_mlp=v_g_pre_mlp, w_mlp_in=v_w_mlp_in, w_mlp_out=v_w_mlp_out,
              g_post_mlp=v_g_post_mlp)
    order = list(weights)
    grads, deltas, new_m, new_v = {}, {}, {}, {}

    def update(name, g):
        d_, m_, v_ = _adamw("adamw_" + name, weights[name][0], g, ms[name][0], vs[name][0])
        grads[name], deltas[name], new_m[name], new_v[name] = g[None], d_[None], m_[None], v_[None]
        return d_

    last = update("w_ada", gw_ada)
    for name, handle in (("w_mlp_out", r_down), ("w_mlp_in", r_up), ("w_out", r_out), ("w_kv_b", r_kvb),
                         ("w_in", r_in)):
        (g,) = _reduce_finish(handle, where, last)
        last = update(name, g)

    cat = lambda d: jnp.concatenate([d[n] for n in small_names], axis=1)
    d_, m_, v_ = _adamw("adamw_small", cat(weights), small_sum, cat(ms), cat(vs))
    off = 0
    for n in small_names:
        w_ = weights[n].shape[1]
        grads[n], deltas[n], new_m[n], new_v[n] = (a[:, off:off + w_] for a in (small_sum, d_, m_, v_))
        off += w_

    loss = lax.psum(loss_v[0, 0], ("x", "y", "c"))
    return (loss, grad_x[None], *[grads[n] for n in order], *[deltas[n] for n in order],
            *[new_m[n] for n in order], *[new_v[n] for n in order])
```

```python
import functools
import math

import numpy as np
import jax
import jax.numpy as jnp
from jax import lax
from jax.experimental import pallas as pl
from jax.experimental.pallas import tpu as pltpu

F32 = jnp.float32
BF16 = jnp.bfloat16
MESH = pl.DeviceIdType.MESH

EPS = 1e-6
GRID_W = 64
ROPE_THETA = 10000.0
HEAD_DIM = 128
HA = 8
HKV = 2
HB = 8
QK_NOPE = 128
QK_ROPE = 64
V_DIM = 128
KV_RANK = 512
QK_B = QK_NOPE + QK_ROPE
W_QA = HA * HEAD_DIM
W_KA = HKV * HEAD_DIM
W_QB = HB * QK_B
W_IN = W_QA + 2 * W_KA + W_QB + KV_RANK + QK_ROPE
N_CHIPS = 4
N_DEV = 8

LANE = 128
O_QA = 0
O_KA = O_QA + W_QA
O_VA = O_KA + W_KA
O_QN = O_VA + W_KA
O_QP = O_QN + HB * QK_NOPE
O_CKV = O_QP + HB * LANE
O_KPE = O_CKV + KV_RANK
W_INP = O_KPE + LANE

ADAM_LR = 0.001
ADAM_B1 = 0.9
ADAM_B2 = 0.999
ADAM_EPS = 1e-08
ADAM_WD = 0.01
ADAM_STEP = 10

VMEM_LIMIT = 56 * 1024 * 1024


def _params(sem):
    return pltpu.CompilerParams(dimension_semantics=sem, vmem_limit_bytes=VMEM_LIMIT)


def _pick(dim, target, unit):
    best = None
    t = unit
    while t <= min(dim, target):
        if dim % t == 0:
            best = t
        t += unit
    return dim if best is None else best


def _rn(x):
    return x * lax.rsqrt(jnp.mean(x * x, axis=-1, keepdims=True) + EPS)


def _rowwise(name, fn, rows, vecs, row_outs, vec_outs, tile, after=()):
    S = rows[0].shape[0]
    assert S % tile == 0
    n_r, n_v, n_ro, n_vo, n_a = len(rows), len(vecs), len(row_outs), len(vec_outs), len(after)

    def body(*refs):
        r_in = refs[:n_r]
        v_in = refs[n_r:n_r + n_v]
        r_out = refs[n_r + n_v + n_a:n_r + n_v + n_a + n_ro]
        v_out = refs[n_r + n_v + n_a + n_ro:]
        ro, vo = fn(*[r[...] for r in r_in], *[v[...] for v in v_in])
        for ref, val in zip(r_out, ro):
            ref[...] = val.astype(ref.dtype)
        if n_vo:
            @pl.when(pl.program_id(0) == 0)
            def _():
                for ref in v_out:
                    ref[...] = jnp.zeros(ref.shape, ref.dtype)
            for ref, val in zip(v_out, vo):
                ref[...] += val

    in_specs = [pl.BlockSpec((tile, a.shape[1]), lambda i: (i, 0)) for a in rows]
    in_specs += [pl.BlockSpec(a.shape, lambda i: (0, 0)) for a in vecs]
    in_specs += [pl.BlockSpec(memory_space=pl.ANY) for _ in after]
    out_specs = [pl.BlockSpec((tile, w), lambda i: (i, 0)) for w, _ in row_outs]
    out_specs += [pl.BlockSpec((1, w), lambda i: (0, 0)) for w in vec_outs]
    out_shape = [jax.ShapeDtypeStruct((S, w), dt) for w, dt in row_outs]
    out_shape += [jax.ShapeDtypeStruct((1, w), F32) for w in vec_outs]
    res = pl.pallas_call(
        body, name=name, grid=(S // tile,), in_specs=in_specs, out_specs=out_specs, out_shape=out_shape,
        compiler_params=_params(("arbitrary",)),
    )(*rows, *vecs, *after)
    return res[:n_ro], res[n_ro:]


def _mm_nn(name, a, b3, out_dtypes, epilogue=None, extras=()):
    M, K = a.shape
    nB, K2, Nsh = b3.shape
    assert K == K2
    N = nB * Nsh
    tm, tn, tk = _pick(M, 1024, 16), _pick(Nsh, 1536, LANE), _pick(K, 512, LANE)
    nps, nk, ne, no = Nsh // tn, K // tk, len(extras), len(out_dtypes)

    def body(a_ref, b_ref, *rest):
        e_refs, o_refs, acc = rest[:ne], rest[ne:ne + no], rest[-1]
        k = pl.program_id(2)

        @pl.when(k == 0)
        def _():
            acc[...] = jnp.zeros(acc.shape, F32)

        acc[...] += jnp.dot(a_ref[...].astype(BF16), b_ref[...].astype(BF16), preferred_element_type=F32)

        @pl.when(k == nk - 1)
        def _():
            vals = (acc[...],) if epilogue is None else epilogue(acc[...], *[e[...] for e in e_refs])
            for ref, val in zip(o_refs, vals):
                ref[...] = val.astype(ref.dtype)

    in_specs = [pl.BlockSpec((tm, tk), lambda i, j, k: (i, k)),
                pl.BlockSpec((None, tk, tn), lambda i, j, k: (j // nps, k, j % nps))]
    in_specs += [pl.BlockSpec((tm, tn), lambda i, j, k: (i, j)) for _ in extras]
    return pl.pallas_call(
        body, name=name, grid=(M // tm, N // tn, nk), in_specs=in_specs,
        out_specs=[pl.BlockSpec((tm, tn), lambda i, j, k: (i, j)) for _ in out_dtypes],
        out_shape=[jax.ShapeDtypeStruct((M, N), dt) for dt in out_dtypes],
        scratch_shapes=[pltpu.VMEM((tm, tn), F32)],
        compiler_params=_params(("parallel", "parallel", "arbitrary")),
    )(a, b3, *extras)


def _mm_nt(name, a, b3, out_dtypes, epilogue=None, extras=(), after=()):
    M, N = a.shape
    nB, K, Nsh = b3.shape
    assert N == nB * Nsh
    tm, to, tn = _pick(M, 1024, 16), _pick(K, 1024, LANE), _pick(Nsh, 512, LANE)
    nps, nn, ne, no, na = Nsh // tn, N // tn, len(extras), len(out_dtypes), len(after)

    def body(a_ref, b_ref, *rest):
        e_refs, o_refs, acc = rest[:ne], rest[ne + na:ne + na + no], rest[-1]
        n = pl.program_id(2)

        @pl.when(n == 0)
        def _():
            acc[...] = jnp.zeros(acc.shape, F32)

        acc[...] += lax.dot_general(a_ref[...].astype(BF16), b_ref[...].astype(BF16),
                                    (((1,), (1,)), ((), ())), preferred_element_type=F32)

        @pl.when(n == nn - 1)
        def _():
            vals = (acc[...],) if epilogue is None else epilogue(acc[...], *[e[...] for e in e_refs])
            for ref, val in zip(o_refs, vals):
                ref[...] = val.astype(ref.dtype)

    in_specs = [pl.BlockSpec((tm, tn), lambda i, o, n: (i, n)),
                pl.BlockSpec((None, to, tn), lambda i, o, n: (n // nps, o, n % nps))]
    in_specs += [pl.BlockSpec((tm, to), lambda i, o, n: (i, o)) for _ in extras]
    in_specs += [pl.BlockSpec(memory_space=pl.ANY) for _ in after]
    return pl.pallas_call(
        body, name=name, grid=(M // tm, K // to, nn), in_specs=in_specs,
        out_specs=[pl.BlockSpec((tm, to), lambda i, o, n: (i, o)) for _ in out_dtypes],
        out_shape=[jax.ShapeDtypeStruct((M, K), dt) for dt in out_dtypes],
        scratch_shapes=[pltpu.VMEM((tm, to), F32)],
        compiler_params=_params(("parallel", "parallel", "arbitrary")),
    )(a, b3, *extras, *after)


def _mm_tn(name, a, g, nB, after=()):
    M, K = a.shape
    M2, N = g.shape
    assert M == M2 and N % nB == 0
    Nsh = N // nB
    tk, tn, tm = _pick(K, 1024, LANE), _pick(Nsh, 1024, LANE), _pick(M, 512, 16)
    nps, nm = Nsh // tn, M // tm

    def body(a_ref, g_ref, *rest):
        o_ref, acc = rest[len(after):]
        m = pl.program_id(2)

        @pl.when(m == 0)
        def _():
            acc[...] = jnp.zeros(acc.shape, F32)

        acc[...] += lax.dot_general(a_ref[...].astype(BF16), g_ref[...].astype(BF16),
                                    (((0,), (0,)), ((), ())), preferred_element_type=F32)

        @pl.when(m == nm - 1)
        def _():
            o_ref[...] = acc[...]

    return pl.pallas_call(
        body, name=name, grid=(K // tk, N // tn, nm),
        in_specs=[pl.BlockSpec((tm, tk), lambda k, n, m: (m, k)),
                  pl.BlockSpec((tm, tn), lambda k, n, m: (m, n))] + [pl.BlockSpec(memory_space=pl.ANY) for _ in after],
        out_specs=pl.BlockSpec((None, tk, tn), lambda k, n, m: (n // nps, k, n % nps)),
        out_shape=jax.ShapeDtypeStruct((nB, K, Nsh), F32),
        scratch_shapes=[pltpu.VMEM((tk, tn), F32)],
        compiler_params=_params(("parallel", "parallel", "arbitrary")),
    )(a, g, *after)


def _softmax(s):
    m = jnp.max(s, axis=-1, keepdims=True)
    e = jnp.exp(s - m)
    return e * (1.0 / jnp.sum(e, axis=-1, keepdims=True))


def _attn_fwd(name, q, k, v, n_heads, group, dqk, dv, scale, tq, after=()):
    S = q.shape[0]

    def body(q_ref, k_ref, v_ref, *rest):
        o_ref = rest[-1]
        s = lax.dot_general(q_ref[...], k_ref[...], (((1,), (1,)), ((), ())), preferred_element_type=F32) * scale
        p = _softmax(s)
        o_ref[...] = jnp.dot(p.astype(BF16), v_ref[...], preferred_element_type=F32)

    return pl.pallas_call(
        body, name=name, grid=(n_heads, S // tq),
        in_specs=[pl.BlockSpec((tq, dqk), lambda h, i: (i, h)),
                  pl.BlockSpec((S, dqk), lambda h, i: (0, h // group)),
                  pl.BlockSpec((S, dv), lambda h, i: (0, h // group))] + [pl.BlockSpec(memory_space=pl.ANY) for _ in after],
        out_specs=pl.BlockSpec((tq, dv), lambda h, i: (i, h)),
        out_shape=jax.ShapeDtypeStruct((S, n_heads * dv), F32),
        compiler_params=_params(("arbitrary", "arbitrary")),
    )(q, k, v, *after)


def _attn_bwd(name, q, k, v, do, n_heads, group, dqk, dv, scale, tq, after=()):
    S = q.shape[0]
    n_kv = n_heads // group

    def body(q_ref, k_ref, v_ref, do_ref, *rest):
        dq_ref, dk_ref, dv_ref = rest[len(after):]
        h, i = pl.program_id(0), pl.program_id(1)

        @pl.when((h % group == 0) & (i == 0))
        def _():
            dk_ref[...] = jnp.zeros(dk_ref.shape, F32)
            dv_ref[...] = jnp.zeros(dv_ref.shape, F32)

        qb, kb, vb, dob = q_ref[...], k_ref[...], v_ref[...], do_ref[...]
        s = lax.dot_general(qb, kb, (((1,), (1,)), ((), ())), preferred_element_type=F32) * scale
        p = _softmax(s)
        dv_ref[...] += lax.dot_general(p.astype(BF16), dob, (((0,), (0,)), ((), ())), preferred_element_type=F32)
        dp = lax.dot_general(dob, vb, (((1,), (1,)), ((), ())), preferred_element_type=F32)
        ds = (p * (dp - jnp.sum(dp * p, axis=-1, keepdims=True)) * scale).astype(BF16)
        dq_ref[...] = jnp.dot(ds, kb, preferred_element_type=F32)
        dk_ref[...] += lax.dot_general(ds, qb, (((0,), (0,)), ((), ())), preferred_element_type=F32)

    return pl.pallas_call(
        body, name=name, grid=(n_heads, S // tq),
        in_specs=[pl.BlockSpec((tq, dqk), lambda h, i: (i, h)),
                  pl.BlockSpec((S, dqk), lambda h, i: (0, h // group)),
                  pl.BlockSpec((S, dv), lambda h, i: (0, h // group)),
                  pl.BlockSpec((tq, dv), lambda h, i: (i, h))] + [pl.BlockSpec(memory_space=pl.ANY) for _ in after],
        out_specs=[pl.BlockSpec((tq, dqk), lambda h, i: (i, h)),
                   pl.BlockSpec((S, dqk), lambda h, i: (0, h // group)),
                   pl.BlockSpec((S, dv), lambda h, i: (0, h // group))],
        out_shape=[jax.ShapeDtypeStruct((S, n_heads * dqk), F32),
                   jax.ShapeDtypeStruct((S, n_kv * dqk), F32),
                   jax.ShapeDtypeStruct((S, n_kv * dv), F32)],
        compiler_params=_params(("arbitrary", "arbitrary")),
    )(q, k, v, do, *after)


def _adamw(name, w, g, m, v, after=()):
    R, C = w.shape
    tile = _pick(R, max(8, (1 << 19) // C // 8 * 8), 8)
    c1 = 1.0 - ADAM_B1 ** ADAM_STEP
    c2 = 1.0 - ADAM_B2 ** ADAM_STEP

    def body(w_ref, g_ref, m_ref, v_ref, *rest):
        d_ref, nm_ref, nv_ref = rest[len(after):]
        gg = g_ref[...]
        nm = ADAM_B1 * m_ref[...] + (1.0 - ADAM_B1) * gg
        nv = ADAM_B2 * v_ref[...] + (1.0 - ADAM_B2) * (gg * gg)
        d_ref[...] = -ADAM_LR * ((nm / c1) / (jnp.sqrt(nv / c2) + ADAM_EPS) + ADAM_WD * w_ref[...])
        nm_ref[...] = nm
        nv_ref[...] = nv

    spec = pl.BlockSpec((tile, C), lambda i: (i, 0))
    return pl.pallas_call(
        body, name=name, grid=(R // tile,), in_specs=[spec] * 4 + [pl.BlockSpec(memory_space=pl.ANY) for _ in after],
        out_specs=[spec] * 3, out_shape=[jax.ShapeDtypeStruct((R, C), F32)] * 3,
        compiler_params=_params(("parallel",)),
    )(w, g, m, v, *after)


def _flip(me, rel):
    return tuple(1 - p if r else p for p, r in zip(me, rel))


def _push(name, ins, out_shapes, transfers, aliases=None):
    ni, no, nt = len(ins), len(out_shapes), len(transfers)

    def body(*refs):
        in_refs, out_refs = refs[:ni], refs[ni:ni + no]
        send_sems, recv_sems = refs[ni + no], refs[ni + no + 1]
        me = (lax.axis_index("x"), lax.axis_index("y"), lax.axis_index("c"))
        copies = []
        for t, (si, sfn, oi, dfn, rel) in enumerate(transfers):
            src_ref = out_refs[si[1]] if isinstance(si, tuple) else in_refs[si]
            src = src_ref.at[sfn(*me)]
            dst = out_refs[oi].at[dfn(*me)]
            if rel is None:
                cp = pltpu.make_async_copy(src, dst, send_sems.at[t])
            else:
                cp = pltpu.make_async_remote_copy(src_ref=src, dst_ref=dst, send_sem=send_sems.at[t],
                                                  recv_sem=recv_sems.at[t], device_id=_flip(me, rel),
                                                  device_id_type=MESH)
            cp.start()
            copies.append(cp)
        for cp in copies:
            cp.wait()

    any_spec = pl.BlockSpec(memory_space=pl.ANY)
    return pl.pallas_call(
        body, name=name, in_specs=[any_spec] * ni, out_specs=[any_spec] * no, out_shape=out_shapes,
        scratch_shapes=[pltpu.SemaphoreType.DMA((nt,)), pltpu.SemaphoreType.DMA((nt,))],
        input_output_aliases=aliases or {},
        compiler_params=pltpu.CompilerParams(has_side_effects=True),
    )(*ins)


_HBM = pl.BlockSpec(memory_space=pltpu.HBM)
_SEM = pl.BlockSpec(memory_space=pltpu.SEMAPHORE)
_DATAFLOW = pltpu.SideEffectType.DATAFLOW_SIDE_EFFECTING


def _split_copies(refs, transfers, send_sems, recv_sems):
    me = (lax.axis_index("x"), lax.axis_index("y"), lax.axis_index("c"))
    return [pltpu.make_async_remote_copy(src_ref=refs[sb].at[sfn(*me)], dst_ref=refs[db].at[dfn(*me)],
                                         send_sem=send_sems.at[t], recv_sem=recv_sems.at[t],
                                         device_id=_flip(me, rel), device_id_type=MESH)
            for t, (sb, sfn, db, dfn, rel) in enumerate(transfers)]


def _push_start(name, bufs, transfers, after=()):
    nb, na, nt = len(bufs), len(after), len(transfers)

    def body(*refs):
        send_sems, recv_sems, token = refs[nb + na], refs[nb + na + 1], refs[-1]
        for cp in _split_copies(refs[nb + na + 2:2 * nb + na + 2], transfers, send_sems, recv_sems):
            cp.start()
        token[...] = jnp.zeros(token.shape, token.dtype)

    res = pl.pallas_call(
        body, name=name,
        out_shape=(pltpu.SemaphoreType.DMA((nt,)), pltpu.SemaphoreType.DMA((nt,)),
                   *[pltpu.HBM(b.shape, b.dtype) for b in bufs], jax.ShapeDtypeStruct((8, LANE), F32)),
        in_specs=[_HBM] * nb + [pl.BlockSpec(memory_space=pl.ANY)] * na,
        out_specs=(_SEM, _SEM, *[_HBM] * nb, pl.BlockSpec(memory_space=pltpu.VMEM)),
        input_output_aliases={i: i + 2 for i in range(nb)},
        compiler_params=pltpu.CompilerParams(has_side_effects=_DATAFLOW),
    )(*[pltpu.with_memory_space_constraint(b, pltpu.HBM) for b in bufs], *after)
    return res[0], res[1], list(res[2:2 + nb]), res[-1]


def _push_wait(name, send_sems, recv_sems, bufs, transfers, after):
    nb = len(bufs)

    def body(*refs):
        for cp in _split_copies(refs[:nb], transfers, refs[nb], refs[nb + 1]):
            cp.wait_send()
            cp.wait_recv()

    res = pl.pallas_call(
        body, name=name, out_shape=[pltpu.HBM(b.shape, b.dtype) for b in bufs],
        in_specs=[_HBM] * nb + [_SEM, _SEM, pl.BlockSpec(memory_space=pl.ANY)], out_specs=[_HBM] * nb,
        input_output_aliases={i: i for i in range(nb)},
        compiler_params=pltpu.CompilerParams(has_side_effects=_DATAFLOW),
    )(*bufs, send_sems, recv_sems, after)
    return list(res)


ICI_RELS = ((1, 0, 0), (0, 1, 0), (1, 1, 0))
SIBLING = (0, 0, 1)


def _chip(x, y):
    return 2 * x + y


def _cast_into_slot(name, where, w, after=()):
    R, C = w.shape
    tile = _pick(R, 256, 16)

    def body(where_ref, w_ref, *rest):
        rest[-1][...] = w_ref[...].astype(BF16)

    return pl.pallas_call(
        body, name=name,
        grid_spec=pltpu.PrefetchScalarGridSpec(
            num_scalar_prefetch=1, grid=(R // tile,),
            in_specs=[pl.BlockSpec((tile, C), lambda i, wh: (i, 0))] + [pl.BlockSpec(memory_space=pl.ANY) for _ in after],
            out_specs=pl.BlockSpec((None, tile, C), lambda i, wh: (wh[1], i, 0))),
        out_shape=jax.ShapeDtypeStruct((N_CHIPS, R, C), BF16),
        compiler_params=_params(("parallel",)),
    )(where, w, *after)


def _gather_start(tag, where, shards, after, cast_after=()):
    n = len(shards)
    own = [_cast_into_slot(f"cast_w_{tag}{w}", where, shards[w], cast_after) for w in range(n)]
    transfers = []
    for w in range(n):
        def idx(x, y, c, rh=shards[w].shape[0] // 2):
            return (_chip(x, y), pl.ds(c * rh, rh), slice(None))
        transfers += [(w, idx, w, idx, rel) for rel in ICI_RELS]
    send_sems, recv_sems, bufs, token = _push_start(f"gather_{tag}_start", own, transfers, after)
    return (tag, send_sems, recv_sems, bufs, transfers), token


def _forward_transfers(part):
    second = []
    for w, p in enumerate(part):
        for rel in ICI_RELS:
            def idx(x, y, c, rh=p.shape[1] // 2, rel=rel):
                return (_chip(x ^ rel[0], y ^ rel[1]), pl.ds(c * rh, rh), slice(None))
            second.append((w, idx, w, idx, SIBLING))
    return second


def _gather_finish(handle, after):
    tag, send_sems, recv_sems, bufs, transfers = handle
    part = _push_wait(f"gather_{tag}_wait", send_sems, recv_sems, bufs, transfers, after)
    outs = [jax.ShapeDtypeStruct(p.shape, p.dtype) for p in part]
    second = [(("out", sb), sfn, db, dfn, rel) for sb, sfn, db, dfn, rel in _forward_transfers(part)]
    return _push(f"gather_{tag}_d2d", list(part), outs, second, aliases={w: w for w in range(len(part))})


def _gather_mid(handle, after):
    tag, send_sems, recv_sems, bufs, transfers = handle
    part = _push_wait(f"gather_{tag}_wait", send_sems, recv_sems, bufs, transfers, after)
    second = _forward_transfers(part)
    s2, r2, bufs2, token = _push_start(f"gather_{tag}_d2d_start", part, second)
    return (tag, s2, r2, bufs2, second), token


def _gather_end(handle, after):
    tag, send_sems, recv_sems, bufs, transfers = handle
    return _push_wait(f"gather_{tag}_d2d_wait", send_sems, recv_sems, bufs, transfers, after)


def _reduce_begin(tag, grads, after=()):
    n = len(grads)
    halves = [g.shape[1] // 2 for g in grads]
    land = [lax.empty((N_CHIPS, halves[w], grads[w].shape[2]), F32) for w in range(n)]
    t1 = []
    for w in range(n):
        t1.append((w, lambda x, y, c, rh=halves[w]: (slice(None), pl.ds((1 - c) * rh, rh), slice(None)), n + w,
                   lambda x, y, c: (slice(None), slice(None), slice(None)), SIBLING))
    send_sems, recv_sems, bufs, token = _push_start(f"reduce_{tag}_d2d_start", list(grads) + land, t1, after)
    return (tag, send_sems, recv_sems, bufs, t1), token


def _reduce_start(handle, where, after, after_start=()):
    tag, send_sems, recv_sems, bufs, t1 = handle
    n = len(bufs) // 2
    res = _push_wait(f"reduce_{tag}_d2d_wait", send_sems, recv_sems, bufs, t1, after)
    grads, from_sib = res[:n], res[n:]
    halves = [g.shape[1] // 2 for g in grads]
    pair = [_pair_sum(f"pair_sum_{tag}{w}", where, grads[w], from_sib[w]) for w in range(n)]
    land = [lax.empty((3, halves[w], grads[w].shape[2]), BF16) for w in range(n)]
    t3 = []
    for w in range(n):
        for r, rel in enumerate(ICI_RELS):
            t3.append((w, lambda x, y, c, rel=rel: (_chip(x ^ rel[0], y ^ rel[1]), slice(None), slice(None)), n + w,
                       lambda x, y, c, r=r: (r, slice(None), slice(None)), rel))
    send_sems, recv_sems, bufs, token = _push_start(f"reduce_{tag}_start", pair + land, t3, after_start)
    return (tag, send_sems, recv_sems, bufs, t3, list(grads), list(from_sib)), token


def _reduce_finish(handle, where, after):
    tag, send_sems, recv_sems, bufs, t3, grads, from_sib = handle
    n = len(grads)
    halves = [g.shape[1] // 2 for g in grads]
    from_chips = _push_wait(f"reduce_{tag}_wait", send_sems, recv_sems, bufs, t3, after)[n:]
    mine = [_chip_sum(f"chip_sum_{tag}{w}", where, grads[w], from_sib[w], from_chips[w]) for w in range(n)]
    t5 = []
    for w in range(n):
        def idx(x, y, c, rh=halves[w]):
            return (pl.ds(c * rh, rh), slice(None))
        t5.append((("out", w), idx, w, idx, SIBLING))
    return _push(f"reduce_{tag}_swap", mine,
                 [jax.ShapeDtypeStruct((2 * halves[w], grads[w].shape[2]), F32) for w in range(n)], t5,
                 aliases={w: w for w in range(n)})


def _pair_sum(name, where, g, from_sib):
    _, R, C = g.shape
    rh = R // 2
    tile = _pick(rh, max(16, (1 << 19) // C // 16 * 16), 16)
    nb = rh // tile

    def body(where_ref, g_ref, s_ref, o_ref):
        o_ref[...] = (g_ref[...] + s_ref[...]).astype(BF16)

    return pl.pallas_call(
        body, name=name,
        grid_spec=pltpu.PrefetchScalarGridSpec(
            num_scalar_prefetch=1, grid=(N_CHIPS, nb),
            in_specs=[pl.BlockSpec((None, tile, C), lambda j, i, wh: (j, wh[0] * nb + i, 0)),
                      pl.BlockSpec((None, tile, C), lambda j, i, wh: (j, i, 0))],
            out_specs=pl.BlockSpec((None, tile, C), lambda j, i, wh: (j, i, 0))),
        out_shape=jax.ShapeDtypeStruct((N_CHIPS, rh, C), BF16),
        compiler_params=_params(("parallel", "parallel")),
    )(where, g, from_sib)


def _chip_sum(name, where, g, from_sib, from_chips):
    _, R, C = g.shape
    rh = R // 2
    tile = _pick(rh, max(16, (1 << 19) // C // 16 * 16), 16)
    nb = rh // tile

    def body(where_ref, g_ref, s_ref, r_ref, o_ref):
        acc = g_ref[...] + s_ref[...]
        for r in range(3):
            acc = acc + r_ref[r].astype(F32)
        o_ref[...] = acc

    return pl.pallas_call(
        body, name=name,
        grid_spec=pltpu.PrefetchScalarGridSpec(
            num_scalar_prefetch=1, grid=(nb,),
            in_specs=[pl.BlockSpec((None, tile, C), lambda i, wh: (wh[1], wh[0] * nb + i, 0)),
                      pl.BlockSpec((None, tile, C), lambda i, wh: (wh[1], i, 0)),
                      pl.BlockSpec((3, tile, C), lambda i, wh: (0, i, 0))],
            out_specs=pl.BlockSpec((tile, C), lambda i, wh: (wh[0] * nb + i, 0))),
        out_shape=jax.ShapeDtypeStruct((R, C), F32),
        compiler_params=_params(("parallel",)),
    )(where, g, from_sib, from_chips)


def _allgather8(name, blk):
    m_per, n = blk.shape
    rels = [(0, 0, 1), (1, 0, 0), (0, 1, 0), (1, 1, 0), (1, 0, 1), (0, 1, 1), (1, 1, 1)]

    def body(x_ref, out_ref, send_sems, recv_sems):
        me = (lax.axis_index("x"), lax.axis_index("y"), lax.axis_index("c"))
        my_rows = out_ref.at[pl.ds((4 * me[0] + 2 * me[1] + me[2]) * m_per, m_per), :]
        out_ref[pl.ds((4 * me[0] + 2 * me[1] + me[2]) * m_per, m_per), :] = x_ref[...]
        copies = []
        for t, rel in enumerate(rels):
            cp = pltpu.make_async_remote_copy(src_ref=x_ref, dst_ref=my_rows, send_sem=send_sems.at[t],
                                              recv_sem=recv_sems.at[t], device_id=_flip(me, rel),
                                              device_id_type=MESH)
            cp.start()
            copies.append(cp)
        for cp in copies:
            cp.wait()

    return pl.pallas_call(
        body, name=name, out_shape=jax.ShapeDtypeStruct((N_DEV * m_per, n), blk.dtype),
        in_specs=[pl.BlockSpec(memory_space=pltpu.VMEM)], out_specs=pl.BlockSpec(memory_space=pltpu.VMEM),
        scratch_shapes=[pltpu.SemaphoreType.DMA((7,)), pltpu.SemaphoreType.DMA((7,))],
        compiler_params=pltpu.CompilerParams(has_side_effects=True, vmem_limit_bytes=VMEM_LIMIT),
    )(blk)


def _rope_tables(seq_len, dim, reps):
    rows = seq_len // GRID_W
    t = np.arange(seq_len)
    row, col = (t // GRID_W).astype(np.float32), (t % GRID_W).astype(np.float32)
    half = dim // 2
    inv = jnp.asarray(ROPE_THETA, F32) ** (-jnp.arange(0, half, 2, dtype=F32) / half)
    ang_r = jnp.asarray(row)[:, None] * inv[None, :]
    ang_c = jnp.asarray(col)[:, None] * inv[None, :]
    ang = jnp.concatenate([ang_r, ang_r, ang_c, ang_c], axis=-1)
    cos, sin = jnp.cos(ang), jnp.sin(ang)
    low = (np.arange(dim) % (dim // 2)) < (dim // 4)
    s_lo = jnp.where(jnp.asarray(low)[None, :], -sin, 0.0)
    s_hi = jnp.where(jnp.asarray(low)[None, :], 0.0, sin)
    del rows
    return tuple(jnp.tile(a, (1, reps)) for a in (cos, s_lo, s_hi))


def _make_rope(shift):
    def up(v):
        return pltpu.roll(v, LANE - shift, 1)

    def down(v):
        return pltpu.roll(v, shift, 1)

    @jax.custom_vjp
    def rope(v, cos, s_lo, s_hi):
        return v * cos + up(v) * s_lo + down(v) * s_hi

    def fwd(v, cos, s_lo, s_hi):
        return rope(v, cos, s_lo, s_hi), (cos, s_lo, s_hi)

    def bwd(res, dy):
        cos, s_lo, s_hi = res
        return dy * cos + down(dy * s_lo) + up(dy * s_hi), jnp.zeros_like(cos), jnp.zeros_like(cos), jnp.zeros_like(cos)

    rope.defvjp(fwd, bwd)
    return rope


_rope_a = _make_rope(HEAD_DIM // 4)
_rope_b = _make_rope(QK_ROPE // 4)


def _prep(segs, tabs, g_q, g_k, g_ckv):
    qa, ka, va, qn, qp, ckv, kpe = segs
    ta, tb = tabs
    q_a = jnp.concatenate([_rope_a(_rn(s) * g_q, *ta) for s in qa], axis=1)
    k_a = jnp.concatenate([_rope_a(_rn(s) * g_k, *ta) for s in ka], axis=1)
    cat = []
    for h in range(HB):
        cat += [qn[h], _rope_b(qp[h], *tb)]
    q_cat = jnp.concatenate(cat, axis=1)
    return q_a, k_a, va, q_cat, _rn(ckv) * g_ckv, _rope_b(kpe, *tb)


def _split_proj(proj):
    qa = [proj[:, O_QA + h * LANE:O_QA + (h + 1) * LANE] for h in range(HA)]
    ka = [proj[:, O_KA + h * LANE:O_KA + (h + 1) * LANE] for h in range(HKV)]
    va = proj[:, O_VA:O_QN]
    qn = [proj[:, O_QN + h * LANE:O_QN + (h + 1) * LANE] for h in range(HB)]
    qp = [proj[:, O_QP + h * LANE:O_QP + (h + 1) * LANE] for h in range(HB)]
    return qa, ka, va, qn, qp, proj[:, O_CKV:O_KPE], proj[:, O_KPE:W_INP]


def _permute_w_in(nat):
    K = nat.shape[0]
    qb = nat[:, W_QA + 2 * W_KA:W_QA + 2 * W_KA + W_QB].reshape(K, HB, QK_B)
    z = lambda w: jnp.zeros((K, w), nat.dtype)
    qp = jnp.concatenate([qb[:, :, QK_NOPE:], jnp.zeros((K, HB, LANE - QK_ROPE), nat.dtype)], axis=2)
    tail = nat[:, W_QA + 2 * W_KA + W_QB:]
    return jnp.concatenate([nat[:, :O_QN], qb[:, :, :QK_NOPE].reshape(K, HB * QK_NOPE), qp.reshape(K, HB * LANE),
                            tail, z(LANE - QK_ROPE)], axis=1)


def _unpermute_w_in(p):
    K = p.shape[0]
    qn = p[:, O_QN:O_QP].reshape(K, HB, QK_NOPE)
    qp = p[:, O_QP:O_CKV].reshape(K, HB, LANE)[:, :, :QK_ROPE]
    qb = jnp.concatenate([qn, qp], axis=2).reshape(K, W_QB)
    return jnp.concatenate([p[:, :O_QN], qb, p[:, O_CKV:O_KPE + QK_ROPE]], axis=1)


def kernel(x, c, w_ada, b_ada, g_pre_attn, w_in, g_q_a, g_k_a, g_ckv, w_kv_b, g_out_a, g_out_b, w_out, g_post_attn, g_pre_mlp, w_mlp_in, w_mlp_out, g_post_mlp, loss_target, m_w_ada, m_b_ada, m_g_pre_attn, m_w_in, m_g_q_a, m_g_k_a, m_g_ckv, m_w_kv_b, m_g_out_a, m_g_out_b, m_w_out, m_g_post_attn, m_g_pre_mlp, m_w_mlp_in, m_w_mlp_out, m_g_post_mlp, v_w_ada, v_b_ada, v_g_pre_attn, v_w_in, v_g_q_a, v_g_k_a, v_g_ckv, v_w_kv_b, v_g_out_a, v_g_out_b, v_w_out, v_g_post_attn, v_g_pre_mlp, v_w_mlp_in, v_w_mlp_out, v_g_post_mlp):
    S, D = x.shape[1], x.shape[2]
    x2d, tgt = x[0], loss_target[0]
    ix, iy, ic = lax.axis_index("x"), lax.axis_index("y"), lax.axis_index("c")
    j_me = _chip(ix, iy)
    e_me = 4 * ix + 2 * iy + ic
    T = _pick(S, 256, 8)
    TB = _pick(S, 128, 8)

    where = jnp.stack([ic, j_me]).astype(jnp.int32)
    DFF = w_mlp_out.shape[1] * N_CHIPS

    c_all = _allgather8("gather_c", c.reshape(8, D // 8)).reshape(N_DEV, D)

    def silu_fn(cc):
        return (cc * (1.0 / (1.0 + jnp.exp(-cc))),), ()

    (c_act,), _ = _rowwise("silu_c", silu_fn, [c_all], [], [(D, BF16)], [], N_DEV)
    ca_pad = jnp.concatenate([c_act, jnp.zeros_like(c_act)], axis=0)
    n_mod = w_ada.shape[2]
    b_cols = lax.dynamic_slice(b_ada, (0, j_me * n_mod), (1, n_mod))
    (mod_part,) = _mm_nn("mod_mm", ca_pad, w_ada, [F32], epilogue=lambda acc, b: (acc + b,),
                         extras=(jnp.broadcast_to(b_cols, (16, n_mod)),))
    mod_all = _allgather8("gather_mod", mod_part[:8]).reshape(N_DEV, 8, n_mod)
    mod = jnp.concatenate([lax.dynamic_slice(mod_all, (2 * j, e_me, 0), (1, 1, n_mod))[0] for j in range(N_CHIPS)],
                          axis=1)

    h_qkv, tok_a = _gather_start("qkv", where, [w_in[0], w_kv_b[0]], (mod_all,))
    h_out, tok_b = _gather_start("out", where, [w_out[0]], (tok_a,), (tok_a,))
    h_up, tok_c = _gather_start("up", where, [w_mlp_in[0]], (tok_b,), (tok_b,))
    h_down, tok_d = _gather_start("down", where, [w_mlp_out[0]], (tok_c,), (tok_c,))
    mod = mod + tok_d[0, 0]
    sh_a, sc_a, gt_a, sh_m, sc_m, gt_m = [mod[:, i * D:(i + 1) * D] for i in range(6)]

    tabs_a = _rope_tables(S, HEAD_DIM, 1)
    tabs_b = _rope_tables(S, QK_ROPE, LANE // QK_ROPE)

    def pre_attn_fn(xt, g, sc, sh):
        return (_rn(xt) * g * (1.0 + sc) + sh,), ()

    (h_b,), _ = _rowwise("pre_attn", pre_attn_fn, [x2d], [g_pre_attn, sc_a, sh_a], [(D, BF16)], [], T)
    g_in, g_kvb = _gather_finish(h_qkv, h_b)
    wp = _permute_w_in(jnp.transpose(g_in, (1, 0, 2)).reshape(D, W_IN))[None]
    (proj,) = _mm_nn("proj_mm", h_b, wp, [F32])

    def prep_fn(pt, ca_, sla, sha, cb_, slb, shb, gq, gk, gc):
        return _prep(_split_proj(pt), ((ca_, sla, sha), (cb_, slb, shb)), gq, gk, gc), ()

    (qa_b, ka_b, va_b, qcat_b, ckvn_b, kpe_b), _ = _rowwise(
        "prep", prep_fn, [proj, *tabs_a, *tabs_b], [g_q_a, g_k_a, g_ckv],
        [(W_QA, BF16), (W_KA, BF16), (W_KA, BF16), (2 * HB * LANE, BF16), (KV_RANK, BF16), (LANE, BF16)], [], TB)
    (kv,) = _mm_nn("kv_mm", ckvn_b, g_kvb, [F32])

    def kcat_fn(kvt, kpet):
        kc, vs = [], []
        for h in range(HB):
            kc += [kvt[:, 2 * h * LANE:(2 * h + 1) * LANE], kpet.astype(F32)]
            vs.append(kvt[:, (2 * h + 1) * LANE:(2 * h + 2) * LANE])
        return (jnp.concatenate(kc, axis=1), jnp.concatenate(vs, axis=1)), ()

    (kcat_b, vb_b), _ = _rowwise("kcat", kcat_fn, [kv, kpe_b], [], [(2 * HB * LANE, BF16), (HB * V_DIM, BF16)], [], T)

    TQ = _pick(S, 256, 16)
    sc_a_ = 1.0 / math.sqrt(HEAD_DIM)
    sc_b_ = 1.0 / math.sqrt(QK_B)
    o_a = _attn_fwd("attn_a_fwd", qa_b, ka_b, va_b, HA, HA // HKV, HEAD_DIM, HEAD_DIM, sc_a_, TQ)
    m_out, tok_m = _gather_mid(h_out, o_a)
    m_up, tok_m = _gather_mid(h_up, tok_m)
    o_b = _attn_fwd("attn_b_fwd", qcat_b, kcat_b, vb_b, HB, 1, 2 * LANE, V_DIM, sc_b_, TQ, after=(tok_m,))
    m_down, tok_m = _gather_mid(h_down, o_b)

    def mix_fn(oa, ob, ga, gb):
        return (jnp.concatenate([_rn(oa) * ga, _rn(ob) * gb], axis=1),), ()

    (on_b,), _ = _rowwise("mix_norm", mix_fn, [o_a, o_b], [g_out_a, g_out_b], [(2 * W_QA, BF16)], [], T,
                          after=(tok_m,))
    (g_wout,) = _gather_end(m_out, on_b)
    w_out_f = g_wout.reshape(1, N_CHIPS * g_wout.shape[1], D)
    (o2,) = _mm_nn("out_mm", on_b, w_out_f, [F32])

    def mid(xt, o2t, gta, gpa, gpm, scm, shm):
        x1 = xt + gta * (_rn(o2t) * gpa)
        return x1, _rn(x1) * gpm * (1.0 + scm) + shm

    def mid_fn(*a):
        return mid(*a), ()

    mid_vecs = [gt_a, g_post_attn, g_pre_mlp, sc_m, sh_m]
    (x1, h2_b), _ = _rowwise("mid", mid_fn, [x2d, o2], mid_vecs, [(D, F32), (D, BF16)], [], T)
    (g_mlp_in,) = _gather_end(m_up, h2_b)
    u, a_b = _mm_nn("mlp_in_mm", h2_b, g_mlp_in, [F32, BF16],
                    epilogue=lambda acc: (acc, jnp.square(jnp.maximum(acc, 0.0))))
    (g_mlp_out,) = _gather_end(m_down, a_b)
    w_mlp_out_f = g_mlp_out.reshape(1, DFF, D)
    (y,) = _mm_nn("mlp_out_mm", a_b, w_mlp_out_f, [F32])

    def last(yt, x1t, gtm, gpo):
        return x1t + gtm * (_rn(yt) * gpo)

    def last_fn(yt, x1t, tt, gtm, gpo):
        x2, vjp = jax.vjp(last, yt, x1t, gtm, gpo)
        err = x2 - tt
        dy, dx1, dgt, dg = vjp(err * (1.0 / D))
        loss = 0.5 * jnp.sum(jnp.mean(err * err, axis=-1))
        return (dy, dx1), (jnp.full((1, LANE), loss, F32), dgt, dg)

    (dy_b, dx2), (loss_v, d_gt_m, d_g_post_mlp) = _rowwise(
        "loss_bwd", last_fn, [y, x1, tgt], [gt_m, g_post_mlp], [(D, BF16), (D, F32)], [LANE, D, D], TB)

    gw_mlp_out = _mm_tn("gw_mlp_out_mm", a_b, dy_b, 1).reshape(N_CHIPS, DFF // N_CHIPS, D)
    b_down, tok = _reduce_begin("down", [gw_mlp_out])
    (du_b,) = _mm_nt("d_mlp_out_mm", dy_b, w_mlp_out_f, [BF16],
                     epilogue=lambda acc, ut: (acc * (2.0 * jnp.maximum(ut, 0.0)),), extras=(u,), after=(tok,))
    r_down, tok = _reduce_start(b_down, where, du_b)
    gw_mlp_in = _mm_tn("gw_mlp_in_mm", h2_b, du_b, N_CHIPS, after=(tok,))
    b_up, tok = _reduce_begin("up", [gw_mlp_in])
    (dh2,) = _mm_nt("d_mlp_in_mm", du_b, g_mlp_in, [F32], after=(tok,))
    r_up, tok_up = _reduce_start(b_up, where, dh2)

    def mid_bwd_fn(xt, o2t, dh2t, dx2t, *vecs):
        _, vjp = jax.vjp(mid, xt, o2t, *vecs)
        dx, do2, dgta, dgpa, dgpm, dscm, dshm = vjp((dx2t, dh2t))
        return (dx, do2), (dgta, dgpa, dgpm, dscm, dshm)

    (dx1, do2_b), (d_gt_a, d_g_post_attn, d_g_pre_mlp, d_sc_m, d_sh_m) = _rowwise(
        "mid_bwd", mid_bwd_fn, [x2d, o2, dh2, dx2], mid_vecs, [(D, F32), (D, BF16)], [D] * 5, TB, after=(tok_up,))

    (d_on,) = _mm_nt("d_out_mm", do2_b, w_out_f, [F32])
    gw_out = _mm_tn("gw_out_mm", on_b, do2_b, 1).reshape(N_CHIPS, w_out.shape[1], D)
    b_out, tok_out = _reduce_begin("out", [gw_out])

    def mix_bwd_fn(oa, ob, dont, ga, gb):
        _, vjp = jax.vjp(lambda a_, b_, g1, g2: mix_fn(a_, b_, g1, g2)[0][0], oa, ob, ga, gb)
        doa, dob, dga, dgb = vjp(dont)
        return (doa, dob), (dga, dgb)

    (doa_b, dob_b), (d_g_out_a, d_g_out_b) = _rowwise(
        "mix_bwd", mix_bwd_fn, [o_a, o_b, d_on], [g_out_a, g_out_b], [(W_QA, BF16), (HB * V_DIM, BF16)],
        [W_QA, HB * V_DIM], T, after=(tok_out,))

    dqa, dka, dva = _attn_bwd("attn_a_bwd", qa_b, ka_b, va_b, doa_b, HA, HA // HKV, HEAD_DIM, HEAD_DIM, sc_a_, TQ)
    r_out, tok = _reduce_start(b_out, where, dqa)
    dqcat, dkcat, dvb = _attn_bwd("attn_b_bwd", qcat_b, kcat_b, vb_b, dob_b, HB, 1, 2 * LANE, V_DIM, sc_b_, TQ,
                                  after=(tok,))

    def kcat_bwd_fn(dkc, dvt):
        dkv, dkpe = [], None
        for h in range(HB):
            dkv += [dkc[:, 2 * h * LANE:(2 * h + 1) * LANE], dvt[:, h * V_DIM:(h + 1) * V_DIM]]
            piece = dkc[:, (2 * h + 1) * LANE:(2 * h + 2) * LANE]
            dkpe = piece if dkpe is None else dkpe + piece
        return (jnp.concatenate(dkv, axis=1), dkpe), ()

    (dkv_b, dkpe), _ = _rowwise("kcat_bwd", kcat_bwd_fn, [dkcat, dvb], [], [(2 * HB * LANE, BF16), (LANE, F32)], [], T)
    (d_ckvn,) = _mm_nt("d_kv_mm", dkv_b, g_kvb, [F32])
    gw_kvb = _mm_tn("gw_kv_mm", ckvn_b, dkv_b, N_CHIPS)
    b_kvb, tok_kvb = _reduce_begin("kvb", [gw_kvb])

    def prep_bwd_fn(pt, dqat, dkat, dvat, dqct, dckt, dkpt, ca_, sla, sha, cb_, slb, shb, gq, gk, gc):
        tabs = ((ca_, sla, sha), (cb_, slb, shb))
        _, vjp = jax.vjp(lambda sg, a1, a2, a3: _prep(sg, tabs, a1, a2, a3), _split_proj(pt), gq, gk, gc)
        (dqa_s, dka_s, dva_s, dqn_s, dqp_s, dck_s, dkp_s), dgq, dgk, dgc = vjp((dqat, dkat, dvat, dqct, dckt, dkpt))
        dproj = jnp.concatenate([*dqa_s, *dka_s, dva_s, *dqn_s, *dqp_s, dck_s, dkp_s], axis=1)
        return (dproj,), (dgq, dgk, dgc)

    (dproj_b,), (d_g_q_a, d_g_k_a, d_g_ckv) = _rowwise(
        "prep_bwd", prep_bwd_fn, [proj, dqa, dka, dva, dqcat, d_ckvn, dkpe, *tabs_a, *tabs_b],
        [g_q_a, g_k_a, g_ckv], [(W_INP, BF16)], [HEAD_DIM, HEAD_DIM, KV_RANK], TB, after=(tok_kvb,))

    r_kvb, tok = _reduce_start(b_kvb, where, dproj_b)
    (dh,) = _mm_nt("d_proj_mm", dproj_b, wp, [F32], after=(tok,))
    gwp = _mm_tn("gw_in_mm", h_b, dproj_b, 1)[0]
    gw_in = jnp.transpose(_unpermute_w_in(gwp).reshape(D, N_CHIPS, W_IN // N_CHIPS), (1, 0, 2))
    b_in, tok = _reduce_begin("in", [gw_in])

    def pre_attn_bwd_fn(xt, dht, dx1t, g, sc, sh):
        _, vjp = jax.vjp(lambda a_, b_, c_, d_: pre_attn_fn(a_, b_, c_, d_)[0][0], xt, g, sc, sh)
        dxt, dg, dsc, dsh = vjp(dht)
        return (dx1t + dxt,), (dg, dsc, dsh)

    (grad_x,), (d_g_pre_attn, d_sc_a, d_sh_a) = _rowwise(
        "pre_attn_bwd", pre_attn_bwd_fn, [x2d, dh, dx1], [g_pre_attn, sc_a, sh_a], [(D, F32)], [D] * 3, TB,
        after=(tok,))

    small_names = ["b_ada", "g_pre_attn", "g_q_a", "g_k_a", "g_ckv", "g_out_a", "g_out_b", "g_post_attn",
                   "g_pre_mlp", "g_post_mlp"]
    small_local = jnp.concatenate([d_sh_a, d_sc_a, d_gt_a, d_sh_m, d_sc_m, d_gt_m, d_g_pre_attn, d_g_q_a, d_g_k_a,
                                   d_g_ckv, d_g_out_a, d_g_out_b, d_g_post_attn, d_g_pre_mlp, d_g_post_mlp], axis=1)
    n_small = small_local.shape[1]
    small_all = _allgather8("gather_small", small_local.reshape(8, n_small // 8)).reshape(N_DEV, n_small)

    def sum8_fn(t):
        return (), (jnp.sum(t, axis=0, keepdims=True),)

    r_in, tok_in = _reduce_start(b_in, where, grad_x, (small_all,))
    _, (small_sum,) = _rowwise("sum_small", sum8_fn, [small_all], [], [], [n_small], N_DEV, after=(tok_in,))
    dmod_cols = lax.dynamic_slice(small_all, (0, j_me * n_mod), (N_DEV, n_mod))
    dmod_pad = jnp.concatenate([dmod_cols, jnp.zeros_like(dmod_cols)], axis=0).astype(BF16)
    gw_ada = _mm_tn("gw_ada_mm", ca_pad, dmod_pad, 1)[0]

    weights = dict(w_ada=w_ada, b_ada=b_ada, g_pre_attn=g_pre_attn, w_in=w_in, g_q_a=g_q_a, g_k_a=g_k_a, g_ckv=g_ckv,
                   w_kv_b=w_kv_b, g_out_a=g_out_a, g_out_b=g_out_b, w_out=w_out, g_post_attn=g_post_attn,
                   g_pre_mlp=g_pre_mlp, w_mlp_in=w_mlp_in, w_mlp_out=w_mlp_out, g_post_mlp=g_post_mlp)
    ms = dict(w_ada=m_w_ada, b_ada=m_b_ada, g_pre_attn=m_g_pre_attn, w_in=m_w_in, g_q_a=m_g_q_a, g_k_a=m_g_k_a,
              g_ckv=m_g_ckv, w_kv_b=m_w_kv_b, g_out_a=m_g_out_a, g_out_b=m_g_out_b, w_out=m_w_out,
              g_post_attn=m_g_post_attn, g_pre_mlp=m_g_pre_mlp, w_mlp_in=m_w_mlp_in, w_mlp_out=m_w_mlp_out,
              g_post_mlp=m_g_post_mlp)
    vs = dict(w_ada=v_w_ada, b_ada=v_b_ada, g_pre_attn=v_g_pre_attn, w_in=v_w_in, g_q_a=v_g_q_a, g_k_a=v_g_k_a,
              g_ckv=v_g_ckv, w_kv_b=v_w_kv_b, g_out_a=v_g_out_a, g_out_b=v_g_out_b, w_out=v_w_out,
              g_post_attn=v_g_post_attn, g_pre_mlp=v_g_pre_mlp, w_mlp_in=v_w_mlp_in, w_mlp_out=v_w_mlp_out,
              g_post_mlp=v_g_post_mlp)
    order = list(weights)
    grads, deltas, new_m, new_v = {}, {}, {}, {}

    def update(name, g, after=()):
        d_, m_, v_ = _adamw("adamw_" + name, weights[name][0], g, ms[name][0], vs[name][0], after)
        grads[name], deltas[name], new_m[name], new_v[name] = g[None], d_[None], m_[None], v_[None]
        return d_

    last = update("w_ada", gw_ada, (small_sum,))
    for name, handle in (("w_mlp_out", r_down), ("w_mlp_in", r_up), ("w_out", r_out), ("w_kv_b", r_kvb),
                         ("w_in", r_in)):
        (g,) = _reduce_finish(handle, where, last)
        last = update(name, g)

    cat = lambda d: jnp.concatenate([d[n] for n in small_names], axis=1)
    d_, m_, v_ = _adamw("adamw_small", cat(weights), small_sum, cat(ms), cat(vs))
    off = 0
    for n in small_names:
        w_ = weights[n].shape[1]
        grads[n], deltas[n], new_m[n], new_v[n] = (a[:, off:off + w_] for a in (small_sum, d_, m_, v_))
        off += w_

    loss = lax.psum(loss_v[0, 0], ("x", "y", "c"))
    return (loss, grad_x[None], *[grads[n] for n in order], *[deltas[n] for n in order],
            *[new_m[n] for n in order], *[new_v[n] for n in order])
```

```python
import functools
import math

import numpy as np
import jax
import jax.numpy as jnp
from jax import lax
from jax.experimental import pallas as pl
from jax.experimental.pallas import tpu as pltpu

F32 = jnp.float32
BF16 = jnp.bfloat16
MESH = pl.DeviceIdType.MESH

EPS = 1e-6
GRID_W = 64
ROPE_THETA = 10000.0
HEAD_DIM = 128
HA = 8
HKV = 2
HB = 8
QK_NOPE = 128
QK_ROPE = 64
V_DIM = 128
KV_RANK = 512
QK_B = QK_NOPE + QK_ROPE
W_QA = HA * HEAD_DIM
W_KA = HKV * HEAD_DIM
W_QB = HB * QK_B
W_IN = W_QA + 2 * W_KA + W_QB + KV_RANK + QK_ROPE
N_CHIPS = 4
N_DEV = 8

LANE = 128
O_QA = 0
O_KA = O_QA + W_QA
O_VA = O_KA + W_KA
O_QN = O_VA + W_KA
O_QP = O_QN + HB * QK_NOPE
O_CKV = O_QP + HB * LANE
O_KPE = O_CKV + KV_RANK
W_INP = O_KPE + LANE

ADAM_LR = 0.001
ADAM_B1 = 0.9
ADAM_B2 = 0.999
ADAM_EPS = 1e-08
ADAM_WD = 0.01
ADAM_STEP = 10

VMEM_LIMIT = 56 * 1024 * 1024


def _params(sem):
    return pltpu.CompilerParams(dimension_semantics=sem, vmem_limit_bytes=VMEM_LIMIT)


def _pick(dim, target, unit):
    best = None
    t = unit
    while t <= min(dim, target):
        if dim % t == 0:
            best = t
        t += unit
    return dim if best is None else best


def _rn(x):
    return x * lax.rsqrt(jnp.mean(x * x, axis=-1, keepdims=True) + EPS)


def _rowwise(name, fn, rows, vecs, row_outs, vec_outs, tile, after=()):
    S = rows[0].shape[0]
    assert S % tile == 0
    n_r, n_v, n_ro, n_vo, n_a = len(rows), len(vecs), len(row_outs), len(vec_outs), len(after)

    def body(*refs):
        r_in = refs[:n_r]
        v_in = refs[n_r:n_r + n_v]
        r_out = refs[n_r + n_v + n_a:n_r + n_v + n_a + n_ro]
        v_out = refs[n_r + n_v + n_a + n_ro:]
        ro, vo = fn(*[r[...] for r in r_in], *[v[...] for v in v_in])
        for ref, val in zip(r_out, ro):
            ref[...] = val.astype(ref.dtype)
        if n_vo:
            @pl.when(pl.program_id(0) == 0)
            def _():
                for ref in v_out:
                    ref[...] = jnp.zeros(ref.shape, ref.dtype)
            for ref, val in zip(v_out, vo):
                ref[...] += val

    in_specs = [pl.BlockSpec((tile, a.shape[1]), lambda i: (i, 0)) for a in rows]
    in_specs += [pl.BlockSpec(a.shape, lambda i: (0, 0)) for a in vecs]
    in_specs += [pl.BlockSpec(memory_space=pl.ANY) for _ in after]
    out_specs = [pl.BlockSpec((tile, w), lambda i: (i, 0)) for w, _ in row_outs]
    out_specs += [pl.BlockSpec((1, w), lambda i: (0, 0)) for w in vec_outs]
    out_shape = [jax.ShapeDtypeStruct((S, w), dt) for w, dt in row_outs]
    out_shape += [jax.ShapeDtypeStruct((1, w), F32) for w in vec_outs]
    res = pl.pallas_call(
        body, name=name, grid=(S // tile,), in_specs=in_specs, out_specs=out_specs, out_shape=out_shape,
        compiler_params=_params(("arbitrary",)),
    )(*rows, *vecs, *after)
    return res[:n_ro], res[n_ro:]


CONTRACT_TILE = 2048


def _accumulate(acc, part, step, n_steps, finish):
    if n_steps == 1:
        finish(part)
        return

    @pl.when(step == 0)
    def _():
        acc[...] = part

    @pl.when(step > 0)
    def _():
        acc[...] += part

    @pl.when(step == n_steps - 1)
    def _():
        finish(acc[...])


def _mm_nn(name, a, b3, out_dtypes, epilogue=None, extras=()):
    M, K = a.shape
    nB, K2, Nsh = b3.shape
    assert K == K2
    N = nB * Nsh
    tm, tn, tk = _pick(M, 1024, 16), _pick(Nsh, 1536, LANE), _pick(K, CONTRACT_TILE, LANE)
    nps, nk, ne, no = Nsh // tn, K // tk, len(extras), len(out_dtypes)

    def body(a_ref, b_ref, *rest):
        e_refs, o_refs = rest[:ne], rest[ne:ne + no]

        def finish(total):
            vals = (total,) if epilogue is None else epilogue(total, *[e[...] for e in e_refs])
            for ref, val in zip(o_refs, vals):
                ref[...] = val.astype(ref.dtype)

        part = jnp.dot(a_ref[...].astype(BF16), b_ref[...].astype(BF16), preferred_element_type=F32)
        _accumulate(rest[-1], part, pl.program_id(2), nk, finish)

    in_specs = [pl.BlockSpec((tm, tk), lambda i, j, k: (i, k)),
                pl.BlockSpec((None, tk, tn), lambda i, j, k: (j // nps, k, j % nps))]
    in_specs += [pl.BlockSpec((tm, tn), lambda i, j, k: (i, j)) for _ in extras]
    return pl.pallas_call(
        body, name=name, grid=(M // tm, N // tn, nk), in_specs=in_specs,
        out_specs=[pl.BlockSpec((tm, tn), lambda i, j, k: (i, j)) for _ in out_dtypes],
        out_shape=[jax.ShapeDtypeStruct((M, N), dt) for dt in out_dtypes],
        scratch_shapes=[pltpu.VMEM((tm, tn), F32)] if nk > 1 else [],
        compiler_params=_params(("parallel", "parallel", "arbitrary")),
    )(a, b3, *extras)


def _mm_nt(name, a, b3, out_dtypes, epilogue=None, extras=(), after=()):
    M, N = a.shape
    nB, K, Nsh = b3.shape
    assert N == nB * Nsh
    tm, to, tn = _pick(M, 1024, 16), _pick(K, 1024, LANE), _pick(Nsh, CONTRACT_TILE, LANE)
    nps, nn, ne, no, na = Nsh // tn, N // tn, len(extras), len(out_dtypes), len(after)

    def body(a_ref, b_ref, *rest):
        e_refs, o_refs = rest[:ne], rest[ne + na:ne + na + no]

        def finish(total):
            vals = (total,) if epilogue is None else epilogue(total, *[e[...] for e in e_refs])
            for ref, val in zip(o_refs, vals):
                ref[...] = val.astype(ref.dtype)

        part = lax.dot_general(a_ref[...].astype(BF16), b_ref[...].astype(BF16),
                               (((1,), (1,)), ((), ())), preferred_element_type=F32)
        _accumulate(rest[-1], part, pl.program_id(2), nn, finish)

    in_specs = [pl.BlockSpec((tm, tn), lambda i, o, n: (i, n)),
                pl.BlockSpec((None, to, tn), lambda i, o, n: (n // nps, o, n % nps))]
    in_specs += [pl.BlockSpec((tm, to), lambda i, o, n: (i, o)) for _ in extras]
    in_specs += [pl.BlockSpec(memory_space=pl.ANY) for _ in after]
    return pl.pallas_call(
        body, name=name, grid=(M // tm, K // to, nn), in_specs=in_specs,
        out_specs=[pl.BlockSpec((tm, to), lambda i, o, n: (i, o)) for _ in out_dtypes],
        out_shape=[jax.ShapeDtypeStruct((M, K), dt) for dt in out_dtypes],
        scratch_shapes=[pltpu.VMEM((tm, to), F32)] if nn > 1 else [],
        compiler_params=_params(("parallel", "parallel", "arbitrary")),
    )(a, b3, *extras, *after)


def _mm_tn(name, a, g, nB, after=()):
    M, K = a.shape
    M2, N = g.shape
    assert M == M2 and N % nB == 0
    Nsh = N // nB
    tk, tn, tm = _pick(K, 1024, LANE), _pick(Nsh, 1536, LANE), _pick(M, CONTRACT_TILE, 16)
    nps, nm = Nsh // tn, M // tm

    def body(a_ref, g_ref, *rest):
        o_ref = rest[len(after)]

        def finish(total):
            o_ref[...] = total

        part = lax.dot_general(a_ref[...].astype(BF16), g_ref[...].astype(BF16),
                               (((0,), (0,)), ((), ())), preferred_element_type=F32)
        _accumulate(rest[-1], part, pl.program_id(2), nm, finish)

    return pl.pallas_call(
        body, name=name, grid=(K // tk, N // tn, nm),
        in_specs=[pl.BlockSpec((tm, tk), lambda k, n, m: (m, k)),
                  pl.BlockSpec((tm, tn), lambda k, n, m: (m, n))] + [pl.BlockSpec(memory_space=pl.ANY) for _ in after],
        out_specs=pl.BlockSpec((None, tk, tn), lambda k, n, m: (n // nps, k, n % nps)),
        out_shape=jax.ShapeDtypeStruct((nB, K, Nsh), F32),
        scratch_shapes=[pltpu.VMEM((tk, tn), F32)] if nm > 1 else [],
        compiler_params=_params(("parallel", "parallel", "arbitrary")),
    )(a, g, *after)


def _softmax(s):
    m = jnp.max(s, axis=-1, keepdims=True)
    e = jnp.exp(s - m)
    return e * (1.0 / jnp.sum(e, axis=-1, keepdims=True))


def _attn_fwd(name, q, k, v, n_heads, group, dqk, dv, scale, tq, after=()):
    S = q.shape[0]

    def body(q_ref, k_ref, v_ref, *rest):
        o_ref = rest[-1]
        s = lax.dot_general(q_ref[...], k_ref[...], (((1,), (1,)), ((), ())), preferred_element_type=F32) * scale
        p = _softmax(s)
        o_ref[...] = jnp.dot(p.astype(BF16), v_ref[...], preferred_element_type=F32)

    return pl.pallas_call(
        body, name=name, grid=(n_heads, S // tq),
        in_specs=[pl.BlockSpec((tq, dqk), lambda h, i: (i, h)),
                  pl.BlockSpec((S, dqk), lambda h, i: (0, h // group)),
                  pl.BlockSpec((S, dv), lambda h, i: (0, h // group))] + [pl.BlockSpec(memory_space=pl.ANY) for _ in after],
        out_specs=pl.BlockSpec((tq, dv), lambda h, i: (i, h)),
        out_shape=jax.ShapeDtypeStruct((S, n_heads * dv), F32),
        compiler_params=_params(("arbitrary", "arbitrary")),
    )(q, k, v, *after)


def _attn_bwd(name, q, k, v, do, n_heads, group, dqk, dv, scale, tq, after=()):
    S = q.shape[0]
    n_kv = n_heads // group

    def body(q_ref, k_ref, v_ref, do_ref, *rest):
        dq_ref, dk_ref, dv_ref = rest[len(after):]
        h, i = pl.program_id(0), pl.program_id(1)

        @pl.when((h % group == 0) & (i == 0))
        def _():
            dk_ref[...] = jnp.zeros(dk_ref.shape, F32)
            dv_ref[...] = jnp.zeros(dv_ref.shape, F32)

        qb, kb, vb, dob = q_ref[...], k_ref[...], v_ref[...], do_ref[...]
        s = lax.dot_general(qb, kb, (((1,), (1,)), ((), ())), preferred_element_type=F32) * scale
        p = _softmax(s)
        dv_ref[...] += lax.dot_general(p.astype(BF16), dob, (((0,), (0,)), ((), ())), preferred_element_type=F32)
        dp = lax.dot_general(dob, vb, (((1,), (1,)), ((), ())), preferred_element_type=F32)
        ds = (p * (dp - jnp.sum(dp * p, axis=-1, keepdims=True)) * scale).astype(BF16)
        dq_ref[...] = jnp.dot(ds, kb, preferred_element_type=F32)
        dk_ref[...] += lax.dot_general(ds, qb, (((0,), (0,)), ((), ())), preferred_element_type=F32)

    return pl.pallas_call(
        body, name=name, grid=(n_heads, S // tq),
        in_specs=[pl.BlockSpec((tq, dqk), lambda h, i: (i, h)),
                  pl.BlockSpec((S, dqk), lambda h, i: (0, h // group)),
                  pl.BlockSpec((S, dv), lambda h, i: (0, h // group)),
                  pl.BlockSpec((tq, dv), lambda h, i: (i, h))] + [pl.BlockSpec(memory_space=pl.ANY) for _ in after],
        out_specs=[pl.BlockSpec((tq, dqk), lambda h, i: (i, h)),
                   pl.BlockSpec((S, dqk), lambda h, i: (0, h // group)),
                   pl.BlockSpec((S, dv), lambda h, i: (0, h // group))],
        out_shape=[jax.ShapeDtypeStruct((S, n_heads * dqk), F32),
                   jax.ShapeDtypeStruct((S, n_kv * dqk), F32),
                   jax.ShapeDtypeStruct((S, n_kv * dv), F32)],
        compiler_params=_params(("arbitrary", "arbitrary")),
    )(q, k, v, do, *after)


def _adamw(name, w, g, m, v, after=()):
    R, C = w.shape
    tile = _pick(R, max(8, (1 << 19) // C // 8 * 8), 8)
    c1 = 1.0 - ADAM_B1 ** ADAM_STEP
    c2 = 1.0 - ADAM_B2 ** ADAM_STEP

    def body(w_ref, g_ref, m_ref, v_ref, *rest):
        d_ref, nm_ref, nv_ref = rest[len(after):]
        gg = g_ref[...]
        nm = ADAM_B1 * m_ref[...] + (1.0 - ADAM_B1) * gg
        nv = ADAM_B2 * v_ref[...] + (1.0 - ADAM_B2) * (gg * gg)
        d_ref[...] = -ADAM_LR * ((nm / c1) / (jnp.sqrt(nv / c2) + ADAM_EPS) + ADAM_WD * w_ref[...])
        nm_ref[...] = nm
        nv_ref[...] = nv

    spec = pl.BlockSpec((tile, C), lambda i: (i, 0))
    return pl.pallas_call(
        body, name=name, grid=(R // tile,), in_specs=[spec] * 4 + [pl.BlockSpec(memory_space=pl.ANY) for _ in after],
        out_specs=[spec] * 3, out_shape=[jax.ShapeDtypeStruct((R, C), F32)] * 3,
        compiler_params=_params(("parallel",)),
    )(w, g, m, v, *after)


def _flip(me, rel):
    return tuple(1 - p if r else p for p, r in zip(me, rel))


def _push(name, ins, out_shapes, transfers, aliases=None):
    ni, no, nt = len(ins), len(out_shapes), len(transfers)

    def body(*refs):
        in_refs, out_refs = refs[:ni], refs[ni:ni + no]
        send_sems, recv_sems = refs[ni + no], refs[ni + no + 1]
        me = (lax.axis_index("x"), lax.axis_index("y"), lax.axis_index("c"))
        copies = []
        for t, (si, sfn, oi, dfn, rel) in enumerate(transfers):
            src_ref = out_refs[si[1]] if isinstance(si, tuple) else in_refs[si]
            src = src_ref.at[sfn(*me)]
            dst = out_refs[oi].at[dfn(*me)]
            if rel is None:
                cp = pltpu.make_async_copy(src, dst, send_sems.at[t])
            else:
                cp = pltpu.make_async_remote_copy(src_ref=src, dst_ref=dst, send_sem=send_sems.at[t],
                                                  recv_sem=recv_sems.at[t], device_id=_flip(me, rel),
                                                  device_id_type=MESH)
            cp.start()
            copies.append(cp)
        for cp in copies:
            cp.wait()

    any_spec = pl.BlockSpec(memory_space=pl.ANY)
    return pl.pallas_call(
        body, name=name, in_specs=[any_spec] * ni, out_specs=[any_spec] * no, out_shape=out_shapes,
        scratch_shapes=[pltpu.SemaphoreType.DMA((nt,)), pltpu.SemaphoreType.DMA((nt,))],
        input_output_aliases=aliases or {},
        compiler_params=pltpu.CompilerParams(has_side_effects=True),
    )(*ins)


_HBM = pl.BlockSpec(memory_space=pltpu.HBM)
_SEM = pl.BlockSpec(memory_space=pltpu.SEMAPHORE)
_DATAFLOW = pltpu.SideEffectType.DATAFLOW_SIDE_EFFECTING


def _split_copies(refs, transfers, send_sems, recv_sems):
    me = (lax.axis_index("x"), lax.axis_index("y"), lax.axis_index("c"))
    return [pltpu.make_async_remote_copy(src_ref=refs[sb].at[sfn(*me)], dst_ref=refs[db].at[dfn(*me)],
                                         send_sem=send_sems.at[t], recv_sem=recv_sems.at[t],
                                         device_id=_flip(me, rel), device_id_type=MESH)
            for t, (sb, sfn, db, dfn, rel) in enumerate(transfers)]


def _push_start(name, bufs, transfers, after=()):
    nb, na, nt = len(bufs), len(after), len(transfers)

    def body(*refs):
        send_sems, recv_sems, token = refs[nb + na], refs[nb + na + 1], refs[-1]
        for cp in _split_copies(refs[nb + na + 2:2 * nb + na + 2], transfers, send_sems, recv_sems):
            cp.start()
        token[...] = jnp.zeros(token.shape, token.dtype)

    res = pl.pallas_call(
        body, name=name,
        out_shape=(pltpu.SemaphoreType.DMA((nt,)), pltpu.SemaphoreType.DMA((nt,)),
                   *[pltpu.HBM(b.shape, b.dtype) for b in bufs], jax.ShapeDtypeStruct((8, LANE), F32)),
        in_specs=[_HBM] * nb + [pl.BlockSpec(memory_space=pl.ANY)] * na,
        out_specs=(_SEM, _SEM, *[_HBM] * nb, pl.BlockSpec(memory_space=pltpu.VMEM)),
        input_output_aliases={i: i + 2 for i in range(nb)},
        compiler_params=pltpu.CompilerParams(has_side_effects=_DATAFLOW),
    )(*[pltpu.with_memory_space_constraint(b, pltpu.HBM) for b in bufs], *after)
    return res[0], res[1], list(res[2:2 + nb]), res[-1]


def _push_wait(name, send_sems, recv_sems, bufs, transfers, after):
    nb = len(bufs)

    def body(*refs):
        for cp in _split_copies(refs[:nb], transfers, refs[nb], refs[nb + 1]):
            cp.wait_send()
            cp.wait_recv()

    res = pl.pallas_call(
        body, name=name, out_shape=[pltpu.HBM(b.shape, b.dtype) for b in bufs],
        in_specs=[_HBM] * nb + [_SEM, _SEM, pl.BlockSpec(memory_space=pl.ANY)], out_specs=[_HBM] * nb,
        input_output_aliases={i: i for i in range(nb)},
        compiler_params=pltpu.CompilerParams(has_side_effects=_DATAFLOW),
    )(*bufs, send_sems, recv_sems, after)
    return list(res)


ICI_RELS = ((1, 0, 0), (0, 1, 0), (1, 1, 0))
SIBLING = (0, 0, 1)


def _chip(x, y):
    return 2 * x + y


def _cast_into_slot(name, where, w, after=()):
    R, C = w.shape
    tile = _pick(R, 256, 16)

    def body(where_ref, w_ref, *rest):
        rest[-1][...] = w_ref[...].astype(BF16)

    return pl.pallas_call(
        body, name=name,
        grid_spec=pltpu.PrefetchScalarGridSpec(
            num_scalar_prefetch=1, grid=(R // tile,),
            in_specs=[pl.BlockSpec((tile, C), lambda i, wh: (i, 0))] + [pl.BlockSpec(memory_space=pl.ANY) for _ in after],
            out_specs=pl.BlockSpec((None, tile, C), lambda i, wh: (wh[1], i, 0))),
        out_shape=jax.ShapeDtypeStruct((N_CHIPS, R, C), BF16),
        compiler_params=_params(("parallel",)),
    )(where, w, *after)


def _gather_start(tag, where, shards, after, cast_after=()):
    n = len(shards)
    own = [_cast_into_slot(f"cast_w_{tag}{w}", where, shards[w], cast_after) for w in range(n)]
    transfers = []
    for w in range(n):
        def idx(x, y, c, rh=shards[w].shape[0] // 2):
            return (_chip(x, y), pl.ds(c * rh, rh), slice(None))
        transfers += [(w, idx, w, idx, rel) for rel in ICI_RELS]
    send_sems, recv_sems, bufs, token = _push_start(f"gather_{tag}_start", own, transfers, after)
    return (tag, send_sems, recv_sems, bufs, transfers), token


def _forward_transfers(part):
    second = []
    for w, p in enumerate(part):
        for rel in ICI_RELS:
            def idx(x, y, c, rh=p.shape[1] // 2, rel=rel):
                return (_chip(x ^ rel[0], y ^ rel[1]), pl.ds(c * rh, rh), slice(None))
            second.append((w, idx, w, idx, SIBLING))
    return second


def _gather_finish(handle, after):
    tag, send_sems, recv_sems, bufs, transfers = handle
    part = _push_wait(f"gather_{tag}_wait", send_sems, recv_sems, bufs, transfers, after)
    outs = [jax.ShapeDtypeStruct(p.shape, p.dtype) for p in part]
    second = [(("out", sb), sfn, db, dfn, rel) for sb, sfn, db, dfn, rel in _forward_transfers(part)]
    return _push(f"gather_{tag}_d2d", list(part), outs, second, aliases={w: w for w in range(len(part))})


def _gather_mid(handle, after):
    tag, send_sems, recv_sems, bufs, transfers = handle
    part = _push_wait(f"gather_{tag}_wait", send_sems, recv_sems, bufs, transfers, after)
    second = _forward_transfers(part)
    s2, r2, bufs2, token = _push_start(f"gather_{tag}_d2d_start", part, second)
    return (tag, s2, r2, bufs2, second), token


def _gather_end(handle, after):
    tag, send_sems, recv_sems, bufs, transfers = handle
    return _push_wait(f"gather_{tag}_d2d_wait", send_sems, recv_sems, bufs, transfers, after)


def _reduce_begin(tag, grads, after=()):
    n = len(grads)
    halves = [g.shape[1] // 2 for g in grads]
    land = [lax.empty((N_CHIPS, halves[w], grads[w].shape[2]), F32) for w in range(n)]
    t1 = []
    for w in range(n):
        t1.append((w, lambda x, y, c, rh=halves[w]: (slice(None), pl.ds((1 - c) * rh, rh), slice(None)), n + w,
                   lambda x, y, c: (slice(None), slice(None), slice(None)), SIBLING))
    send_sems, recv_sems, bufs, token = _push_start(f"reduce_{tag}_d2d_start", list(grads) + land, t1, after)
    return (tag, send_sems, recv_sems, bufs, t1), token


def _reduce_start(handle, where, after, after_start=()):
    tag, send_sems, recv_sems, bufs, t1 = handle
    n = len(bufs) // 2
    res = _push_wait(f"reduce_{tag}_d2d_wait", send_sems, recv_sems, bufs, t1, after)
    grads, from_sib = res[:n], res[n:]
    halves = [g.shape[1] // 2 for g in grads]
    pair = [_pair_sum(f"pair_sum_{tag}{w}", where, grads[w], from_sib[w]) for w in range(n)]
    land = [lax.empty((3, halves[w], grads[w].shape[2]), BF16) for w in range(n)]
    t3 = []
    for w in range(n):
        for r, rel in enumerate(ICI_RELS):
            t3.append((w, lambda x, y, c, rel=rel: (_chip(x ^ rel[0], y ^ rel[1]), slice(None), slice(None)), n + w,
                       lambda x, y, c, r=r: (r, slice(None), slice(None)), rel))
    send_sems, recv_sems, bufs, token = _push_start(f"reduce_{tag}_start", pair + land, t3, after_start)
    return (tag, send_sems, recv_sems, bufs, t3, list(grads), list(from_sib)), token


def _reduce_finish(handle, where, after):
    tag, send_sems, recv_sems, bufs, t3, grads, from_sib = handle
    n = len(grads)
    halves = [g.shape[1] // 2 for g in grads]
    from_chips = _push_wait(f"reduce_{tag}_wait", send_sems, recv_sems, bufs, t3, after)[n:]
    mine = [_chip_sum(f"chip_sum_{tag}{w}", where, grads[w], from_sib[w], from_chips[w]) for w in range(n)]
    t5 = []
    for w in range(n):
        def idx(x, y, c, rh=halves[w]):
            return (pl.ds(c * rh, rh), slice(None))
        t5.append((("out", w), idx, w, idx, SIBLING))
    return _push(f"reduce_{tag}_swap", mine,
                 [jax.ShapeDtypeStruct((2 * halves[w], grads[w].shape[2]), F32) for w in range(n)], t5,
                 aliases={w: w for w in range(n)})


def _pair_sum(name, where, g, from_sib):
    _, R, C = g.shape
    rh = R // 2
    tile = _pick(rh, max(16, (1 << 19) // C // 16 * 16), 16)
    nb = rh // tile

    def body(where_ref, g_ref, s_ref, o_ref):
        o_ref[...] = (g_ref[...] + s_ref[...]).astype(BF16)

    return pl.pallas_call(
        body, name=name,
        grid_spec=pltpu.PrefetchScalarGridSpec(
            num_scalar_prefetch=1, grid=(N_CHIPS, nb),
            in_specs=[pl.BlockSpec((None, tile, C), lambda j, i, wh: (j, wh[0] * nb + i, 0)),
                      pl.BlockSpec((None, tile, C), lambda j, i, wh: (j, i, 0))],
            out_specs=pl.BlockSpec((None, tile, C), lambda j, i, wh: (j, i, 0))),
        out_shape=jax.ShapeDtypeStruct((N_CHIPS, rh, C), BF16),
        compiler_params=_params(("parallel", "parallel")),
    )(where, g, from_sib)


def _chip_sum(name, where, g, from_sib, from_chips):
    _, R, C = g.shape
    rh = R // 2
    tile = _pick(rh, max(16, (1 << 19) // C // 16 * 16), 16)
    nb = rh // tile

    def body(where_ref, g_ref, s_ref, r_ref, o_ref):
        acc = g_ref[...] + s_ref[...]
        for r in range(3):
            acc = acc + r_ref[r].astype(F32)
        o_ref[...] = acc

    return pl.pallas_call(
        body, name=name,
        grid_spec=pltpu.PrefetchScalarGridSpec(
            num_scalar_prefetch=1, grid=(nb,),
            in_specs=[pl.BlockSpec((None, tile, C), lambda i, wh: (wh[1], wh[0] * nb + i, 0)),
                      pl.BlockSpec((None, tile, C), lambda i, wh: (wh[1], i, 0)),
                      pl.BlockSpec((3, tile, C), lambda i, wh: (0, i, 0))],
            out_specs=pl.BlockSpec((tile, C), lambda i, wh: (wh[0] * nb + i, 0))),
        out_shape=jax.ShapeDtypeStruct((R, C), F32),
        compiler_params=_params(("parallel",)),
    )(where, g, from_sib, from_chips)


def _allgather8(name, blk):
    m_per, n = blk.shape
    rels = [(0, 0, 1), (1, 0, 0), (0, 1, 0), (1, 1, 0), (1, 0, 1), (0, 1, 1), (1, 1, 1)]

    def body(x_ref, out_ref, send_sems, recv_sems):
        me = (lax.axis_index("x"), lax.axis_index("y"), lax.axis_index("c"))
        my_rows = out_ref.at[pl.ds((4 * me[0] + 2 * me[1] + me[2]) * m_per, m_per), :]
        out_ref[pl.ds((4 * me[0] + 2 * me[1] + me[2]) * m_per, m_per), :] = x_ref[...]
        copies = []
        for t, rel in enumerate(rels):
            cp = pltpu.make_async_remote_copy(src_ref=x_ref, dst_ref=my_rows, send_sem=send_sems.at[t],
                                              recv_sem=recv_sems.at[t], device_id=_flip(me, rel),
                                              device_id_type=MESH)
            cp.start()
            copies.append(cp)
        for cp in copies:
            cp.wait()

    return pl.pallas_call(
        body, name=name, out_shape=jax.ShapeDtypeStruct((N_DEV * m_per, n), blk.dtype),
        in_specs=[pl.BlockSpec(memory_space=pltpu.VMEM)], out_specs=pl.BlockSpec(memory_space=pltpu.VMEM),
        scratch_shapes=[pltpu.SemaphoreType.DMA((7,)), pltpu.SemaphoreType.DMA((7,))],
        compiler_params=pltpu.CompilerParams(has_side_effects=True, vmem_limit_bytes=VMEM_LIMIT),
    )(blk)


def _rope_tables(seq_len, dim, reps):
    rows = seq_len // GRID_W
    t = np.arange(seq_len)
    row, col = (t // GRID_W).astype(np.float32), (t % GRID_W).astype(np.float32)
    half = dim // 2
    inv = jnp.asarray(ROPE_THETA, F32) ** (-jnp.arange(0, half, 2, dtype=F32) / half)
    ang_r = jnp.asarray(row)[:, None] * inv[None, :]
    ang_c = jnp.asarray(col)[:, None] * inv[None, :]
    ang = jnp.concatenate([ang_r, ang_r, ang_c, ang_c], axis=-1)
    cos, sin = jnp.cos(ang), jnp.sin(ang)
    low = (np.arange(dim) % (dim // 2)) < (dim // 4)
    s_lo = jnp.where(jnp.asarray(low)[None, :], -sin, 0.0)
    s_hi = jnp.where(jnp.asarray(low)[None, :], 0.0, sin)
    del rows
    return tuple(jnp.tile(a, (1, reps)) for a in (cos, s_lo, s_hi))


def _make_rope(shift):
    def up(v):
        return pltpu.roll(v, LANE - shift, 1)

    def down(v):
        return pltpu.roll(v, shift, 1)

    @jax.custom_vjp
    def rope(v, cos, s_lo, s_hi):
        return v * cos + up(v) * s_lo + down(v) * s_hi

    def fwd(v, cos, s_lo, s_hi):
        return rope(v, cos, s_lo, s_hi), (cos, s_lo, s_hi)

    def bwd(res, dy):
        cos, s_lo, s_hi = res
        return dy * cos + down(dy * s_lo) + up(dy * s_hi), jnp.zeros_like(cos), jnp.zeros_like(cos), jnp.zeros_like(cos)

    rope.defvjp(fwd, bwd)
    return rope


_rope_a = _make_rope(HEAD_DIM // 4)
_rope_b = _make_rope(QK_ROPE // 4)


def _prep(segs, tabs, g_q, g_k, g_ckv):
    qa, ka, va, qn, qp, ckv, kpe = segs
    ta, tb = tabs
    q_a = jnp.concatenate([_rope_a(_rn(s) * g_q, *ta) for s in qa], axis=1)
    k_a = jnp.concatenate([_rope_a(_rn(s) * g_k, *ta) for s in ka], axis=1)
    cat = []
    for h in range(HB):
        cat += [qn[h], _rope_b(qp[h], *tb)]
    q_cat = jnp.concatenate(cat, axis=1)
    return q_a, k_a, va, q_cat, _rn(ckv) * g_ckv, _rope_b(kpe, *tb)


def _split_proj(proj):
    qa = [proj[:, O_QA + h * LANE:O_QA + (h + 1) * LANE] for h in range(HA)]
    ka = [proj[:, O_KA + h * LANE:O_KA + (h + 1) * LANE] for h in range(HKV)]
    va = proj[:, O_VA:O_QN]
    qn = [proj[:, O_QN + h * LANE:O_QN + (h + 1) * LANE] for h in range(HB)]
    qp = [proj[:, O_QP + h * LANE:O_QP + (h + 1) * LANE] for h in range(HB)]
    return qa, ka, va, qn, qp, proj[:, O_CKV:O_KPE], proj[:, O_KPE:W_INP]


def _permute_w_in(nat):
    K = nat.shape[0]
    qb = nat[:, W_QA + 2 * W_KA:W_QA + 2 * W_KA + W_QB].reshape(K, HB, QK_B)
    z = lambda w: jnp.zeros((K, w), nat.dtype)
    qp = jnp.concatenate([qb[:, :, QK_NOPE:], jnp.zeros((K, HB, LANE - QK_ROPE), nat.dtype)], axis=2)
    tail = nat[:, W_QA + 2 * W_KA + W_QB:]
    return jnp.concatenate([nat[:, :O_QN], qb[:, :, :QK_NOPE].reshape(K, HB * QK_NOPE), qp.reshape(K, HB * LANE),
                            tail, z(LANE - QK_ROPE)], axis=1)


def _unpermute_w_in(p):
    K = p.shape[0]
    qn = p[:, O_QN:O_QP].reshape(K, HB, QK_NOPE)
    qp = p[:, O_QP:O_CKV].reshape(K, HB, LANE)[:, :, :QK_ROPE]
    qb = jnp.concatenate([qn, qp], axis=2).reshape(K, W_QB)
    return jnp.concatenate([p[:, :O_QN], qb, p[:, O_CKV:O_KPE + QK_ROPE]], axis=1)


def kernel(x, c, w_ada, b_ada, g_pre_attn, w_in, g_q_a, g_k_a, g_ckv, w_kv_b, g_out_a, g_out_b, w_out, g_post_attn, g_pre_mlp, w_mlp_in, w_mlp_out, g_post_mlp, loss_target, m_w_ada, m_b_ada, m_g_pre_attn, m_w_in, m_g_q_a, m_g_k_a, m_g_ckv, m_w_kv_b, m_g_out_a, m_g_out_b, m_w_out, m_g_post_attn, m_g_pre_mlp, m_w_mlp_in, m_w_mlp_out, m_g_post_mlp, v_w_ada, v_b_ada, v_g_pre_attn, v_w_in, v_g_q_a, v_g_k_a, v_g_ckv, v_w_kv_b, v_g_out_a, v_g_out_b, v_w_out, v_g_post_attn, v_g_pre_mlp, v_w_mlp_in, v_w_mlp_out, v_g_post_mlp):
    S, D = x.shape[1], x.shape[2]
    x2d, tgt = x[0], loss_target[0]
    ix, iy, ic = lax.axis_index("x"), lax.axis_index("y"), lax.axis_index("c")
    j_me = _chip(ix, iy)
    e_me = 4 * ix + 2 * iy + ic
    T = _pick(S, 256, 8)
    TB = _pick(S, 128, 8)

    where = jnp.stack([ic, j_me]).astype(jnp.int32)
    DFF = w_mlp_out.shape[1] * N_CHIPS

    c_all = _allgather8("gather_c", c.reshape(8, D // 8)).reshape(N_DEV, D)

    def silu_fn(cc):
        return (cc * (1.0 / (1.0 + jnp.exp(-cc))),), ()

    (c_act,), _ = _rowwise("silu_c", silu_fn, [c_all], [], [(D, BF16)], [], N_DEV)
    ca_pad = jnp.concatenate([c_act, jnp.zeros_like(c_act)], axis=0)
    n_mod = w_ada.shape[2]
    b_cols = lax.dynamic_slice(b_ada, (0, j_me * n_mod), (1, n_mod))
    (mod_part,) = _mm_nn("mod_mm", ca_pad, w_ada, [F32], epilogue=lambda acc, b: (acc + b,),
                         extras=(jnp.broadcast_to(b_cols, (16, n_mod)),))
    mod_all = _allgather8("gather_mod", mod_part[:8]).reshape(N_DEV, 8, n_mod)
    mod = jnp.concatenate([lax.dynamic_slice(mod_all, (2 * j, e_me, 0), (1, 1, n_mod))[0] for j in range(N_CHIPS)],
                          axis=1)

    h_qkv, tok_a = _gather_start("qkv", where, [w_in[0], w_kv_b[0]], (mod_all,))
    h_out, tok_b = _gather_start("out", where, [w_out[0]], (tok_a,), (tok_a,))
    h_up, tok_c = _gather_start("up", where, [w_mlp_in[0]], (tok_b,), (tok_b,))
    h_down, tok_d = _gather_start("down", where, [w_mlp_out[0]], (tok_c,), (tok_c,))
    mod = mod + tok_d[0, 0]
    sh_a, sc_a, gt_a, sh_m, sc_m, gt_m = [mod[:, i * D:(i + 1) * D] for i in range(6)]

    tabs_a = _rope_tables(S, HEAD_DIM, 1)
    tabs_b = _rope_tables(S, QK_ROPE, LANE // QK_ROPE)

    def pre_attn_fn(xt, g, sc, sh):
        return (_rn(xt) * g * (1.0 + sc) + sh,), ()

    (h_b,), _ = _rowwise("pre_attn", pre_attn_fn, [x2d], [g_pre_attn, sc_a, sh_a], [(D, BF16)], [], T)
    g_in, g_kvb = _gather_finish(h_qkv, h_b)
    wp = _permute_w_in(jnp.transpose(g_in, (1, 0, 2)).reshape(D, W_IN))[None]
    (proj,) = _mm_nn("proj_mm", h_b, wp, [F32])

    def prep_fn(pt, ca_, sla, sha, cb_, slb, shb, gq, gk, gc):
        return _prep(_split_proj(pt), ((ca_, sla, sha), (cb_, slb, shb)), gq, gk, gc), ()

    (qa_b, ka_b, va_b, qcat_b, ckvn_b, kpe_b), _ = _rowwise(
        "prep", prep_fn, [proj, *tabs_a, *tabs_b], [g_q_a, g_k_a, g_ckv],
        [(W_QA, BF16), (W_KA, BF16), (W_KA, BF16), (2 * HB * LANE, BF16), (KV_RANK, BF16), (LANE, BF16)], [], TB)
    (kv,) = _mm_nn("kv_mm", ckvn_b, g_kvb, [F32])

    def kcat_fn(kvt, kpet):
        kc, vs = [], []
        for h in range(HB):
            kc += [kvt[:, 2 * h * LANE:(2 * h + 1) * LANE], kpet.astype(F32)]
            vs.append(kvt[:, (2 * h + 1) * LANE:(2 * h + 2) * LANE])
        return (jnp.concatenate(kc, axis=1), jnp.concatenate(vs, axis=1)), ()

    (kcat_b, vb_b), _ = _rowwise("kcat", kcat_fn, [kv, kpe_b], [], [(2 * HB * LANE, BF16), (HB * V_DIM, BF16)], [], T)

    TQ = _pick(S, 256, 16)
    sc_a_ = 1.0 / math.sqrt(HEAD_DIM)
    sc_b_ = 1.0 / math.sqrt(QK_B)
    o_a = _attn_fwd("attn_a_fwd", qa_b, ka_b, va_b, HA, HA // HKV, HEAD_DIM, HEAD_DIM, sc_a_, TQ)
    m_out, tok_m = _gather_mid(h_out, o_a)
    m_up, tok_m = _gather_mid(h_up, tok_m)
    o_b = _attn_fwd("attn_b_fwd", qcat_b, kcat_b, vb_b, HB, 1, 2 * LANE, V_DIM, sc_b_, TQ, after=(tok_m,))
    m_down, tok_m = _gather_mid(h_down, o_b)

    def mix_fn(oa, ob, ga, gb):
        return (jnp.concatenate([_rn(oa) * ga, _rn(ob) * gb], axis=1),), ()

    (on_b,), _ = _rowwise("mix_norm", mix_fn, [o_a, o_b], [g_out_a, g_out_b], [(2 * W_QA, BF16)], [], T,
                          after=(tok_m,))
    (g_wout,) = _gather_end(m_out, on_b)
    w_out_f = g_wout.reshape(1, N_CHIPS * g_wout.shape[1], D)
    (o2,) = _mm_nn("out_mm", on_b, w_out_f, [F32])

    def mid(xt, o2t, gta, gpa, gpm, scm, shm):
        x1 = xt + gta * (_rn(o2t) * gpa)
        return x1, _rn(x1) * gpm * (1.0 + scm) + shm

    def mid_fn(*a):
        return mid(*a), ()

    mid_vecs = [gt_a, g_post_attn, g_pre_mlp, sc_m, sh_m]
    (x1, h2_b), _ = _rowwise("mid", mid_fn, [x2d, o2], mid_vecs, [(D, F32), (D, BF16)], [], T)
    (g_mlp_in,) = _gather_end(m_up, h2_b)
    u, a_b = _mm_nn("mlp_in_mm", h2_b, g_mlp_in, [F32, BF16],
                    epilogue=lambda acc: (acc, jnp.square(jnp.maximum(acc, 0.0))))
    (g_mlp_out,) = _gather_end(m_down, a_b)
    w_mlp_out_f = g_mlp_out.reshape(1, DFF, D)
    (y,) = _mm_nn("mlp_out_mm", a_b, w_mlp_out_f, [F32])

    def last(yt, x1t, gtm, gpo):
        return x1t + gtm * (_rn(yt) * gpo)

    def last_fn(yt, x1t, tt, gtm, gpo):
        x2, vjp = jax.vjp(last, yt, x1t, gtm, gpo)
        err = x2 - tt
        dy, dx1, dgt, dg = vjp(err * (1.0 / D))
        loss = 0.5 * jnp.sum(jnp.mean(err * err, axis=-1))
        return (dy, dx1), (jnp.full((1, LANE), loss, F32), dgt, dg)

    (dy_b, dx2), (loss_v, d_gt_m, d_g_post_mlp) = _rowwise(
        "loss_bwd", last_fn, [y, x1, tgt], [gt_m, g_post_mlp], [(D, BF16), (D, F32)], [LANE, D, D], TB)

    gw_mlp_out = _mm_tn("gw_mlp_out_mm", a_b, dy_b, 1).reshape(N_CHIPS, DFF // N_CHIPS, D)
    b_down, tok = _reduce_begin("down", [gw_mlp_out])
    (du_b,) = _mm_nt("d_mlp_out_mm", dy_b, w_mlp_out_f, [BF16],
                     epilogue=lambda acc, ut: (acc * (2.0 * jnp.maximum(ut, 0.0)),), extras=(u,), after=(tok,))
    r_down, tok = _reduce_start(b_down, where, du_b)
    gw_mlp_in = _mm_tn("gw_mlp_in_mm", h2_b, du_b, N_CHIPS, after=(tok,))
    b_up, tok = _reduce_begin("up", [gw_mlp_in])
    (dh2,) = _mm_nt("d_mlp_in_mm", du_b, g_mlp_in, [F32], after=(tok,))
    r_up, tok_up = _reduce_start(b_up, where, dh2)

    def mid_bwd_fn(xt, o2t, dh2t, dx2t, *vecs):
        _, vjp = jax.vjp(mid, xt, o2t, *vecs)
        dx, do2, dgta, dgpa, dgpm, dscm, dshm = vjp((dx2t, dh2t))
        return (dx, do2), (dgta, dgpa, dgpm, dscm, dshm)

    (dx1, do2_b), (d_gt_a, d_g_post_attn, d_g_pre_mlp, d_sc_m, d_sh_m) = _rowwise(
        "mid_bwd", mid_bwd_fn, [x2d, o2, dh2, dx2], mid_vecs, [(D, F32), (D, BF16)], [D] * 5, TB, after=(tok_up,))

    (d_on,) = _mm_nt("d_out_mm", do2_b, w_out_f, [F32])
    gw_out = _mm_tn("gw_out_mm", on_b, do2_b, 1).reshape(N_CHIPS, w_out.shape[1], D)
    b_out, tok_out = _reduce_begin("out", [gw_out])

    def mix_bwd_fn(oa, ob, dont, ga, gb):
        _, vjp = jax.vjp(lambda a_, b_, g1, g2: mix_fn(a_, b_, g1, g2)[0][0], oa, ob, ga, gb)
        doa, dob, dga, dgb = vjp(dont)
        return (doa, dob), (dga, dgb)

    (doa_b, dob_b), (d_g_out_a, d_g_out_b) = _rowwise(
        "mix_bwd", mix_bwd_fn, [o_a, o_b, d_on], [g_out_a, g_out_b], [(W_QA, BF16), (HB * V_DIM, BF16)],
        [W_QA, HB * V_DIM], T, after=(tok_out,))

    dqa, dka, dva = _attn_bwd("attn_a_bwd", qa_b, ka_b, va_b, doa_b, HA, HA // HKV, HEAD_DIM, HEAD_DIM, sc_a_, TQ)
    r_out, tok = _reduce_start(b_out, where, dqa)
    dqcat, dkcat, dvb = _attn_bwd("attn_b_bwd", qcat_b, kcat_b, vb_b, dob_b, HB, 1, 2 * LANE, V_DIM, sc_b_, TQ,
                                  after=(tok,))

    def kcat_bwd_fn(dkc, dvt):
        dkv, dkpe = [], None
        for h in range(HB):
            dkv += [dkc[:, 2 * h * LANE:(2 * h + 1) * LANE], dvt[:, h * V_DIM:(h + 1) * V_DIM]]
            piece = dkc[:, (2 * h + 1) * LANE:(2 * h + 2) * LANE]
            dkpe = piece if dkpe is None else dkpe + piece
        return (jnp.concatenate(dkv, axis=1), dkpe), ()

    (dkv_b, dkpe), _ = _rowwise("kcat_bwd", kcat_bwd_fn, [dkcat, dvb], [], [(2 * HB * LANE, BF16), (LANE, F32)], [], T)
    (d_ckvn,) = _mm_nt("d_kv_mm", dkv_b, g_kvb, [F32])
    gw_kvb = _mm_tn("gw_kv_mm", ckvn_b, dkv_b, N_CHIPS)
    b_kvb, tok_kvb = _reduce_begin("kvb", [gw_kvb])

    def prep_bwd_fn(pt, dqat, dkat, dvat, dqct, dckt, dkpt, ca_, sla, sha, cb_, slb, shb, gq, gk, gc):
        tabs = ((ca_, sla, sha), (cb_, slb, shb))
        _, vjp = jax.vjp(lambda sg, a1, a2, a3: _prep(sg, tabs, a1, a2, a3), _split_proj(pt), gq, gk, gc)
        (dqa_s, dka_s, dva_s, dqn_s, dqp_s, dck_s, dkp_s), dgq, dgk, dgc = vjp((dqat, dkat, dvat, dqct, dckt, dkpt))
        dproj = jnp.concatenate([*dqa_s, *dka_s, dva_s, *dqn_s, *dqp_s, dck_s, dkp_s], axis=1)
        return (dproj,), (dgq, dgk, dgc)

    (dproj_b,), (d_g_q_a, d_g_k_a, d_g_ckv) = _rowwise(
        "prep_bwd", prep_bwd_fn, [proj, dqa, dka, dva, dqcat, d_ckvn, dkpe, *tabs_a, *tabs_b],
        [g_q_a, g_k_a, g_ckv], [(W_INP, BF16)], [HEAD_DIM, HEAD_DIM, KV_RANK], TB, after=(tok_kvb,))

    r_kvb, tok = _reduce_start(b_kvb, where, dproj_b)
    (dh,) = _mm_nt("d_proj_mm", dproj_b, wp, [F32], after=(tok,))
    gwp = _mm_tn("gw_in_mm", h_b, dproj_b, 1)[0]
    gw_in = jnp.transpose(_unpermute_w_in(gwp).reshape(D, N_CHIPS, W_IN // N_CHIPS), (1, 0, 2))
    b_in, tok = _reduce_begin("in", [gw_in])

    def pre_attn_bwd_fn(xt, dht, dx1t, g, sc, sh):
        _, vjp = jax.vjp(lambda a_, b_, c_, d_: pre_attn_fn(a_, b_, c_, d_)[0][0], xt, g, sc, sh)
        dxt, dg, dsc, dsh = vjp(dht)
        return (dx1t + dxt,), (dg, dsc, dsh)

    (grad_x,), (d_g_pre_attn, d_sc_a, d_sh_a) = _rowwise(
        "pre_attn_bwd", pre_attn_bwd_fn, [x2d, dh, dx1], [g_pre_attn, sc_a, sh_a], [(D, F32)], [D] * 3, TB,
        after=(tok,))

    small_names = ["b_ada", "g_pre_attn", "g_q_a", "g_k_a", "g_ckv", "g_out_a", "g_out_b", "g_post_attn",
                   "g_pre_mlp", "g_post_mlp"]
    small_local = jnp.concatenate([d_sh_a, d_sc_a, d_gt_a, d_sh_m, d_sc_m, d_gt_m, d_g_pre_attn, d_g_q_a, d_g_k_a,
                                   d_g_ckv, d_g_out_a, d_g_out_b, d_g_post_attn, d_g_pre_mlp, d_g_post_mlp], axis=1)
    n_small = small_local.shape[1]
    small_all = _allgather8("gather_small", small_local.reshape(8, n_small // 8)).reshape(N_DEV, n_small)

    def sum8_fn(t):
        return (), (jnp.sum(t, axis=0, keepdims=True),)

    r_in, tok_in = _reduce_start(b_in, where, grad_x, (small_all,))
    _, (small_sum,) = _rowwise("sum_small", sum8_fn, [small_all], [], [], [n_small], N_DEV, after=(tok_in,))
    dmod_cols = lax.dynamic_slice(small_all, (0, j_me * n_mod), (N_DEV, n_mod))
    dmod_pad = jnp.concatenate([dmod_cols, jnp.zeros_like(dmod_cols)], axis=0).astype(BF16)
    gw_ada = _mm_tn("gw_ada_mm", ca_pad, dmod_pad, 1)[0]

    weights = dict(w_ada=w_ada, b_ada=b_ada, g_pre_attn=g_pre_attn, w_in=w_in, g_q_a=g_q_a, g_k_a=g_k_a, g_ckv=g_ckv,
                   w_kv_b=w_kv_b, g_out_a=g_out_a, g_out_b=g_out_b, w_out=w_out, g_post_attn=g_post_attn,
                   g_pre_mlp=g_pre_mlp, w_mlp_in=w_mlp_in, w_mlp_out=w_mlp_out, g_post_mlp=g_post_mlp)
    ms = dict(w_ada=m_w_ada, b_ada=m_b_ada, g_pre_attn=m_g_pre_attn, w_in=m_w_in, g_q_a=m_g_q_a, g_k_a=m_g_k_a,
              g_ckv=m_g_ckv, w_kv_b=m_w_kv_b, g_out_a=m_g_out_a, g_out_b=m_g_out_b, w_out=m_w_out,
              g_post_attn=m_g_post_attn, g_pre_mlp=m_g_pre_mlp, w_mlp_in=m_w_mlp_in, w_mlp_out=m_w_mlp_out,
              g_post_mlp=m_g_post_mlp)
    vs = dict(w_ada=v_w_ada, b_ada=v_b_ada, g_pre_attn=v_g_pre_attn, w_in=v_w_in, g_q_a=v_g_q_a, g_k_a=v_g_k_a,
              g_ckv=v_g_ckv, w_kv_b=v_w_kv_b, g_out_a=v_g_out_a, g_out_b=v_g_out_b, w_out=v_w_out,
              g_post_attn=v_g_post_attn, g_pre_mlp=v_g_pre_mlp, w_mlp_in=v_w_mlp_in, w_mlp_out=v_w_mlp_out,
              g_post_mlp=v_g_post_mlp)
    order = list(weights)
    grads, deltas, new_m, new_v = {}, {}, {}, {}

    def update(name, g, after=()):
        d_, m_, v_ = _adamw("adamw_" + name, weights[name][0], g, ms[name][0], vs[name][0], after)
        grads[name], deltas[name], new_m[name], new_v[name] = g[None], d_[None], m_[None], v_[None]
        return d_

    last = update("w_ada", gw_ada, (small_sum,))
    for name, handle in (("w_mlp_out", r_down), ("w_mlp_in", r_up), ("w_out", r_out), ("w_kv_b", r_kvb),
                         ("w_in", r_in)):
        (g,) = _reduce_finish(handle, where, last)
        last = update(name, g)

    cat = lambda d: jnp.concatenate([d[n] for n in small_names], axis=1)
    d_, m_, v_ = _adamw("adamw_small", cat(weights), small_sum, cat(ms), cat(vs))
    off = 0
    for n in small_names:
        w_ = weights[n].shape[1]
        grads[n], deltas[n], new_m[n], new_v[n] = (a[:, off:off + w_] for a in (small_sum, d_, m_, v_))
        off += w_

    loss = lax.psum(loss_v[0, 0], ("x", "y", "c"))
    return (loss, grad_x[None], *[grads[n] for n in order], *[deltas[n] for n in order],
            *[new_m[n] for n in order], *[new_v[n] for n in order])
```

```python
import functools
import math

import numpy as np
import jax
import jax.numpy as jnp
from jax import lax
from jax.experimental import pallas as pl
from jax.experimental.pallas import tpu as pltpu

F32 = jnp.float32
BF16 = jnp.bfloat16
MESH = pl.DeviceIdType.MESH

EPS = 1e-6
GRID_W = 64
ROPE_THETA = 10000.0
HEAD_DIM = 128
HA = 8
HKV = 2
HB = 8
QK_NOPE = 128
QK_ROPE = 64
V_DIM = 128
KV_RANK = 512
QK_B = QK_NOPE + QK_ROPE
W_QA = HA * HEAD_DIM
W_KA = HKV * HEAD_DIM
W_QB = HB * QK_B
W_IN = W_QA + 2 * W_KA + W_QB + KV_RANK + QK_ROPE
N_CHIPS = 4
N_DEV = 8

LANE = 128
O_QA = 0
O_KA = O_QA + W_QA
O_VA = O_KA + W_KA
O_QN = O_VA + W_KA
O_QP = O_QN + HB * QK_NOPE
O_CKV = O_QP + HB * LANE
O_KPE = O_CKV + KV_RANK
W_INP = O_KPE + LANE

ADAM_LR = 0.001
ADAM_B1 = 0.9
ADAM_B2 = 0.999
ADAM_EPS = 1e-08
ADAM_WD = 0.01
ADAM_STEP = 10

VMEM_LIMIT = 56 * 1024 * 1024


def _params(sem):
    return pltpu.CompilerParams(dimension_semantics=sem, vmem_limit_bytes=VMEM_LIMIT)


def _pick(dim, target, unit):
    best = None
    t = unit
    while t <= min(dim, target):
        if dim % t == 0:
            best = t
        t += unit
    return dim if best is None else best


def _rn(x):
    return x * lax.rsqrt(jnp.mean(x * x, axis=-1, keepdims=True) + EPS)


def _rowwise(name, fn, rows, vecs, row_outs, vec_outs, tile, after=()):
    S = rows[0].shape[0]
    assert S % tile == 0
    n_r, n_v, n_ro, n_vo, n_a = len(rows), len(vecs), len(row_outs), len(vec_outs), len(after)

    def body(*refs):
        r_in = refs[:n_r]
        v_in = refs[n_r:n_r + n_v]
        r_out = refs[n_r + n_v + n_a:n_r + n_v + n_a + n_ro]
        v_out = refs[n_r + n_v + n_a + n_ro:]
        ro, vo = fn(*[r[...] for r in r_in], *[v[...] for v in v_in])
        for ref, val in zip(r_out, ro):
            ref[...] = val.astype(ref.dtype)
        if n_vo:
            @pl.when(pl.program_id(0) == 0)
            def _():
                for ref in v_out:
                    ref[...] = jnp.zeros(ref.shape, ref.dtype)
            for ref, val in zip(v_out, vo):
                ref[...] += val

    in_specs = [pl.BlockSpec((tile, a.shape[1]), lambda i: (i, 0)) for a in rows]
    in_specs += [pl.BlockSpec(a.shape, lambda i: (0, 0)) for a in vecs]
    in_specs += [pl.BlockSpec(memory_space=pl.ANY) for _ in after]
    out_specs = [pl.BlockSpec((tile, w), lambda i: (i, 0)) for w, _ in row_outs]
    out_specs += [pl.BlockSpec((1, w), lambda i: (0, 0)) for w in vec_outs]
    out_shape = [jax.ShapeDtypeStruct((S, w), dt) for w, dt in row_outs]
    out_shape += [jax.ShapeDtypeStruct((1, w), F32) for w in vec_outs]
    res = pl.pallas_call(
        body, name=name, grid=(S // tile,), in_specs=in_specs, out_specs=out_specs, out_shape=out_shape,
        compiler_params=_params(("arbitrary",)),
    )(*rows, *vecs, *after)
    return res[:n_ro], res[n_ro:]


CONTRACT_TILE = 2048


def _accumulate(acc, part, step, n_steps, finish):
    if n_steps == 1:
        finish(part)
        return

    @pl.when(step == 0)
    def _():
        acc[...] = part

    @pl.when(step > 0)
    def _():
        acc[...] += part

    @pl.when(step == n_steps - 1)
    def _():
        finish(acc[...])


def _mm_nn(name, a, b3, out_dtypes, epilogue=None, extras=(), after=()):
    M, K = a.shape
    nB, K2, Nsh = b3.shape
    assert K == K2
    N = nB * Nsh
    tm, tn, tk = _pick(M, 1024, 16), _pick(Nsh, 1536, LANE), _pick(K, CONTRACT_TILE, LANE)
    nps, nk, ne, no, na = Nsh // tn, K // tk, len(extras), len(out_dtypes), len(after)

    def body(a_ref, b_ref, *rest):
        e_refs, o_refs = rest[:ne], rest[ne + na:ne + na + no]

        def finish(total):
            vals = (total,) if epilogue is None else epilogue(total, *[e[...] for e in e_refs])
            for ref, val in zip(o_refs, vals):
                ref[...] = val.astype(ref.dtype)

        part = jnp.dot(a_ref[...].astype(BF16), b_ref[...].astype(BF16), preferred_element_type=F32)
        _accumulate(rest[-1], part, pl.program_id(2), nk, finish)

    in_specs = [pl.BlockSpec((tm, tk), lambda i, j, k: (i, k)),
                pl.BlockSpec((None, tk, tn), lambda i, j, k: (j // nps, k, j % nps))]
    in_specs += [pl.BlockSpec((tm, tn), lambda i, j, k: (i, j)) for _ in extras]
    in_specs += [pl.BlockSpec(memory_space=pl.ANY) for _ in after]
    return pl.pallas_call(
        body, name=name, grid=(M // tm, N // tn, nk), in_specs=in_specs,
        out_specs=[pl.BlockSpec((tm, tn), lambda i, j, k: (i, j)) for _ in out_dtypes],
        out_shape=[jax.ShapeDtypeStruct((M, N), dt) for dt in out_dtypes],
        scratch_shapes=[pltpu.VMEM((tm, tn), F32)] if nk > 1 else [],
        compiler_params=_params(("parallel", "parallel", "arbitrary")),
    )(a, b3, *extras, *after)


def _mm_nt(name, a, b3, out_dtypes, epilogue=None, extras=(), after=()):
    M, N = a.shape
    nB, K, Nsh = b3.shape
    assert N == nB * Nsh
    tm, to, tn = _pick(M, 1024, 16), _pick(K, 1536, LANE), _pick(Nsh, CONTRACT_TILE, LANE)
    nps, nn, ne, no, na = Nsh // tn, N // tn, len(extras), len(out_dtypes), len(after)

    def body(a_ref, b_ref, *rest):
        e_refs, o_refs = rest[:ne], rest[ne + na:ne + na + no]

        def finish(total):
            vals = (total,) if epilogue is None else epilogue(total, *[e[...] for e in e_refs])
            for ref, val in zip(o_refs, vals):
                ref[...] = val.astype(ref.dtype)

        part = lax.dot_general(a_ref[...].astype(BF16), b_ref[...].astype(BF16),
                               (((1,), (1,)), ((), ())), preferred_element_type=F32)
        _accumulate(rest[-1], part, pl.program_id(2), nn, finish)

    in_specs = [pl.BlockSpec((tm, tn), lambda i, o, n: (i, n)),
                pl.BlockSpec((None, to, tn), lambda i, o, n: (n // nps, o, n % nps))]
    in_specs += [pl.BlockSpec((tm, to), lambda i, o, n: (i, o)) for _ in extras]
    in_specs += [pl.BlockSpec(memory_space=pl.ANY) for _ in after]
    return pl.pallas_call(
        body, name=name, grid=(M // tm, K // to, nn), in_specs=in_specs,
        out_specs=[pl.BlockSpec((tm, to), lambda i, o, n: (i, o)) for _ in out_dtypes],
        out_shape=[jax.ShapeDtypeStruct((M, K), dt) for dt in out_dtypes],
        scratch_shapes=[pltpu.VMEM((tm, to), F32)] if nn > 1 else [],
        compiler_params=_params(("parallel", "parallel", "arbitrary")),
    )(a, b3, *extras, *after)


def _mm_tn(name, a, g, nB, after=()):
    M, K = a.shape
    M2, N = g.shape
    assert M == M2 and N % nB == 0
    Nsh = N // nB
    tk, tn, tm = _pick(K, 1536, LANE), _pick(Nsh, 1024, LANE), _pick(M, CONTRACT_TILE, 16)
    nps, nm = Nsh // tn, M // tm

    def body(a_ref, g_ref, *rest):
        o_ref = rest[len(after)]

        def finish(total):
            o_ref[...] = total

        part = lax.dot_general(a_ref[...].astype(BF16), g_ref[...].astype(BF16),
                               (((0,), (0,)), ((), ())), preferred_element_type=F32)
        _accumulate(rest[-1], part, pl.program_id(2), nm, finish)

    return pl.pallas_call(
        body, name=name, grid=(K // tk, N // tn, nm),
        in_specs=[pl.BlockSpec((tm, tk), lambda k, n, m: (m, k)),
                  pl.BlockSpec((tm, tn), lambda k, n, m: (m, n))] + [pl.BlockSpec(memory_space=pl.ANY) for _ in after],
        out_specs=pl.BlockSpec((None, tk, tn), lambda k, n, m: (n // nps, k, n % nps)),
        out_shape=jax.ShapeDtypeStruct((nB, K, Nsh), F32),
        scratch_shapes=[pltpu.VMEM((tk, tn), F32)] if nm > 1 else [],
        compiler_params=_params(("parallel", "parallel", "arbitrary")),
    )(a, g, *after)


def _softmax(s):
    m = jnp.max(s, axis=-1, keepdims=True)
    e = jnp.exp(s - m)
    return e * (1.0 / jnp.sum(e, axis=-1, keepdims=True))


def _attn_fwd(name, q, k, v, n_heads, group, dqk, dv, scale, tq, after=()):
    S = q.shape[0]

    def body(q_ref, k_ref, v_ref, *rest):
        o_ref = rest[-1]
        s = lax.dot_general(q_ref[...], k_ref[...], (((1,), (1,)), ((), ())), preferred_element_type=F32) * scale
        p = _softmax(s)
        o_ref[...] = jnp.dot(p.astype(BF16), v_ref[...], preferred_element_type=F32)

    return pl.pallas_call(
        body, name=name, grid=(n_heads, S // tq),
        in_specs=[pl.BlockSpec((tq, dqk), lambda h, i: (i, h)),
                  pl.BlockSpec((S, dqk), lambda h, i: (0, h // group)),
                  pl.BlockSpec((S, dv), lambda h, i: (0, h // group))] + [pl.BlockSpec(memory_space=pl.ANY) for _ in after],
        out_specs=pl.BlockSpec((tq, dv), lambda h, i: (i, h)),
        out_shape=jax.ShapeDtypeStruct((S, n_heads * dv), F32),
        compiler_params=_params(("arbitrary", "arbitrary")),
    )(q, k, v, *after)


def _attn_bwd(name, q, k, v, do, n_heads, group, dqk, dv, scale, tq, after=()):
    S = q.shape[0]
    n_kv = n_heads // group

    def body(q_ref, k_ref, v_ref, do_ref, *rest):
        dq_ref, dk_ref, dv_ref = rest[len(after):]
        h, i = pl.program_id(0), pl.program_id(1)

        @pl.when((h % group == 0) & (i == 0))
        def _():
            dk_ref[...] = jnp.zeros(dk_ref.shape, F32)
            dv_ref[...] = jnp.zeros(dv_ref.shape, F32)

        qb, kb, vb, dob = q_ref[...], k_ref[...], v_ref[...], do_ref[...]
        s = lax.dot_general(qb, kb, (((1,), (1,)), ((), ())), preferred_element_type=F32) * scale
        p = _softmax(s)
        dv_ref[...] += lax.dot_general(p.astype(BF16), dob, (((0,), (0,)), ((), ())), preferred_element_type=F32)
        dp = lax.dot_general(dob, vb, (((1,), (1,)), ((), ())), preferred_element_type=F32)
        ds = (p * (dp - jnp.sum(dp * p, axis=-1, keepdims=True)) * scale).astype(BF16)
        dq_ref[...] = jnp.dot(ds, kb, preferred_element_type=F32)
        dk_ref[...] += lax.dot_general(ds, qb, (((0,), (0,)), ((), ())), preferred_element_type=F32)

    return pl.pallas_call(
        body, name=name, grid=(n_heads, S // tq),
        in_specs=[pl.BlockSpec((tq, dqk), lambda h, i: (i, h)),
                  pl.BlockSpec((S, dqk), lambda h, i: (0, h // group)),
                  pl.BlockSpec((S, dv), lambda h, i: (0, h // group)),
                  pl.BlockSpec((tq, dv), lambda h, i: (i, h))] + [pl.BlockSpec(memory_space=pl.ANY) for _ in after],
        out_specs=[pl.BlockSpec((tq, dqk), lambda h, i: (i, h)),
                   pl.BlockSpec((S, dqk), lambda h, i: (0, h // group)),
                   pl.BlockSpec((S, dv), lambda h, i: (0, h // group))],
        out_shape=[jax.ShapeDtypeStruct((S, n_heads * dqk), F32),
                   jax.ShapeDtypeStruct((S, n_kv * dqk), F32),
                   jax.ShapeDtypeStruct((S, n_kv * dv), F32)],
        compiler_params=_params(("arbitrary", "arbitrary")),
    )(q, k, v, do, *after)


def _adamw(name, w, g, m, v, after=()):
    R, C = w.shape
    tile = _pick(R, max(8, (1 << 19) // C // 8 * 8), 8)
    c1 = 1.0 - ADAM_B1 ** ADAM_STEP
    c2 = 1.0 - ADAM_B2 ** ADAM_STEP

    def body(w_ref, g_ref, m_ref, v_ref, *rest):
        d_ref, nm_ref, nv_ref = rest[len(after):]
        gg = g_ref[...]
        nm = ADAM_B1 * m_ref[...] + (1.0 - ADAM_B1) * gg
        nv = ADAM_B2 * v_ref[...] + (1.0 - ADAM_B2) * (gg * gg)
        d_ref[...] = -ADAM_LR * ((nm / c1) / (jnp.sqrt(nv / c2) + ADAM_EPS) + ADAM_WD * w_ref[...])
        nm_ref[...] = nm
        nv_ref[...] = nv

    spec = pl.BlockSpec((tile, C), lambda i: (i, 0))
    return pl.pallas_call(
        body, name=name, grid=(R // tile,), in_specs=[spec] * 4 + [pl.BlockSpec(memory_space=pl.ANY) for _ in after],
        out_specs=[spec] * 3, out_shape=[jax.ShapeDtypeStruct((R, C), F32)] * 3,
        compiler_params=_params(("parallel",)),
    )(w, g, m, v, *after)


def _flip(me, rel):
    return tuple(1 - p if r else p for p, r in zip(me, rel))


def _push(name, ins, out_shapes, transfers, aliases=None):
    ni, no, nt = len(ins), len(out_shapes), len(transfers)

    def body(*refs):
        in_refs, out_refs = refs[:ni], refs[ni:ni + no]
        send_sems, recv_sems = refs[ni + no], refs[ni + no + 1]
        me = (lax.axis_index("x"), lax.axis_index("y"), lax.axis_index("c"))
        copies = []
        for t, (si, sfn, oi, dfn, rel) in enumerate(transfers):
            src_ref = out_refs[si[1]] if isinstance(si, tuple) else in_refs[si]
            src = src_ref.at[sfn(*me)]
            dst = out_refs[oi].at[dfn(*me)]
            if rel is None:
                cp = pltpu.make_async_copy(src, dst, send_sems.at[t])
            else:
                cp = pltpu.make_async_remote_copy(src_ref=src, dst_ref=dst, send_sem=send_sems.at[t],
                                                  recv_sem=recv_sems.at[t], device_id=_flip(me, rel),
                                                  device_id_type=MESH)
            cp.start()
            copies.append(cp)
        for cp in copies:
            cp.wait()

    any_spec = pl.BlockSpec(memory_space=pl.ANY)
    return pl.pallas_call(
        body, name=name, in_specs=[any_spec] * ni, out_specs=[any_spec] * no, out_shape=out_shapes,
        scratch_shapes=[pltpu.SemaphoreType.DMA((nt,)), pltpu.SemaphoreType.DMA((nt,))],
        input_output_aliases=aliases or {},
        compiler_params=pltpu.CompilerParams(has_side_effects=True),
    )(*ins)


_HBM = pl.BlockSpec(memory_space=pltpu.HBM)
_SEM = pl.BlockSpec(memory_space=pltpu.SEMAPHORE)
_DATAFLOW = pltpu.SideEffectType.DATAFLOW_SIDE_EFFECTING


def _split_copies(refs, transfers, send_sems, recv_sems):
    me = (lax.axis_index("x"), lax.axis_index("y"), lax.axis_index("c"))
    return [pltpu.make_async_remote_copy(src_ref=refs[sb].at[sfn(*me)], dst_ref=refs[db].at[dfn(*me)],
                                         send_sem=send_sems.at[t], recv_sem=recv_sems.at[t],
                                         device_id=_flip(me, rel), device_id_type=MESH)
            for t, (sb, sfn, db, dfn, rel) in enumerate(transfers)]


def _push_start(name, bufs, transfers, after=()):
    nb, na, nt = len(bufs), len(after), len(transfers)

    def body(*refs):
        send_sems, recv_sems, token = refs[nb + na], refs[nb + na + 1], refs[-1]
        for cp in _split_copies(refs[nb + na + 2:2 * nb + na + 2], transfers, send_sems, recv_sems):
            cp.start()
        token[...] = jnp.zeros(token.shape, token.dtype)

    res = pl.pallas_call(
        body, name=name,
        out_shape=(pltpu.SemaphoreType.DMA((nt,)), pltpu.SemaphoreType.DMA((nt,)),
                   *[pltpu.HBM(b.shape, b.dtype) for b in bufs], jax.ShapeDtypeStruct((8, LANE), F32)),
        in_specs=[_HBM] * nb + [pl.BlockSpec(memory_space=pl.ANY)] * na,
        out_specs=(_SEM, _SEM, *[_HBM] * nb, pl.BlockSpec(memory_space=pltpu.VMEM)),
        input_output_aliases={i: i + 2 for i in range(nb)},
        compiler_params=pltpu.CompilerParams(has_side_effects=_DATAFLOW),
    )(*[pltpu.with_memory_space_constraint(b, pltpu.HBM) for b in bufs], *after)
    return res[0], res[1], list(res[2:2 + nb]), res[-1]


def _push_wait(name, send_sems, recv_sems, bufs, transfers, after):
    nb = len(bufs)

    def body(*refs):
        for cp in _split_copies(refs[:nb], transfers, refs[nb], refs[nb + 1]):
            cp.wait_send()
            cp.wait_recv()

    res = pl.pallas_call(
        body, name=name, out_shape=[pltpu.HBM(b.shape, b.dtype) for b in bufs],
        in_specs=[_HBM] * nb + [_SEM, _SEM, pl.BlockSpec(memory_space=pl.ANY)], out_specs=[_HBM] * nb,
        input_output_aliases={i: i for i in range(nb)},
        compiler_params=pltpu.CompilerParams(has_side_effects=_DATAFLOW),
    )(*bufs, send_sems, recv_sems, after)
    return list(res)


ICI_RELS = ((1, 0, 0), (0, 1, 0), (1, 1, 0))
SIBLING = (0, 0, 1)


def _chip(x, y):
    return 2 * x + y


def _cast_into_slot(name, where, w, after=()):
    R, C = w.shape
    tile = _pick(R, 512, BF16_ROWS)

    def body(where_ref, w_ref, *rest):
        rest[-1][...] = w_ref[...].astype(BF16)

    return pl.pallas_call(
        body, name=name,
        grid_spec=pltpu.PrefetchScalarGridSpec(
            num_scalar_prefetch=1, grid=(R // tile,),
            in_specs=[pl.BlockSpec((tile, C), lambda i, wh: (i, 0))] + [pl.BlockSpec(memory_space=pl.ANY) for _ in after],
            out_specs=pl.BlockSpec((None, tile, C), lambda i, wh: (wh[1], i, 0))),
        out_shape=jax.ShapeDtypeStruct((N_CHIPS, R, C), BF16),
        compiler_params=_params(("parallel",)),
    )(where, w, *after)


BF16_ROWS = 16


def _rows_split(shape2):
    return (shape2[0] // 2) % BF16_ROWS == 0


def _half_shape(shape2):
    R, C = shape2
    return (R // 2, C) if _rows_split(shape2) else (R, C // 2)


def _half(shape2, c):
    R, C = shape2
    if _rows_split(shape2):
        return (pl.ds(c * (R // 2), R // 2), slice(None))
    return (slice(None), pl.ds(pl.multiple_of(c * (C // 2), LANE), C // 2))


def _gather_start(tag, where, shards, after, cast_after=()):
    n = len(shards)
    own = [_cast_into_slot(f"cast_w_{tag}{w}", where, shards[w], cast_after) for w in range(n)]
    transfers = []
    for w in range(n):
        def idx(x, y, c, shape2=shards[w].shape):
            return (_chip(x, y), *_half(shape2, c))
        transfers += [(w, idx, w, idx, rel) for rel in ICI_RELS]
    send_sems, recv_sems, bufs, token = _push_start(f"gather_{tag}_start", own, transfers, after)
    return (tag, send_sems, recv_sems, bufs, transfers), token


def _forward_transfers(part):
    second = []
    for w, p in enumerate(part):
        for rel in ICI_RELS:
            def idx(x, y, c, shape2=p.shape[1:], rel=rel):
                return (_chip(x ^ rel[0], y ^ rel[1]), *_half(shape2, c))
            second.append((w, idx, w, idx, SIBLING))
    return second


def _gather_finish(handle, after):
    tag, send_sems, recv_sems, bufs, transfers = handle
    part = _push_wait(f"gather_{tag}_wait", send_sems, recv_sems, bufs, transfers, after)
    outs = [jax.ShapeDtypeStruct(p.shape, p.dtype) for p in part]
    second = [(("out", sb), sfn, db, dfn, rel) for sb, sfn, db, dfn, rel in _forward_transfers(part)]
    return _push(f"gather_{tag}_d2d", list(part), outs, second, aliases={w: w for w in range(len(part))})


def _gather_mid(handle, after):
    tag, send_sems, recv_sems, bufs, transfers = handle
    part = _push_wait(f"gather_{tag}_wait", send_sems, recv_sems, bufs, transfers, after)
    second = _forward_transfers(part)
    s2, r2, bufs2, token = _push_start(f"gather_{tag}_d2d_start", part, second)
    return (tag, s2, r2, bufs2, second), token


def _gather_end(handle, after):
    tag, send_sems, recv_sems, bufs, transfers = handle
    return _push_wait(f"gather_{tag}_d2d_wait", send_sems, recv_sems, bufs, transfers, after)


def _reduce_begin(tag, grads, after=()):
    n = len(grads)
    land = [lax.empty((N_CHIPS, *_half_shape(g.shape[1:])), F32) for g in grads]
    t1 = []
    for w in range(n):
        t1.append((w, lambda x, y, c, shape2=grads[w].shape[1:]: (slice(None), *_half(shape2, 1 - c)), n + w,
                   lambda x, y, c: (slice(None), slice(None), slice(None)), SIBLING))
    send_sems, recv_sems, bufs, token = _push_start(f"reduce_{tag}_d2d_start", list(grads) + land, t1, after)
    return (tag, send_sems, recv_sems, bufs, t1), token


def _reduce_start(handle, where, after, after_start=()):
    tag, send_sems, recv_sems, bufs, t1 = handle
    n = len(bufs) // 2
    res = _push_wait(f"reduce_{tag}_d2d_wait", send_sems, recv_sems, bufs, t1, after)
    grads, from_sib = res[:n], res[n:]
    pair = [_pair_sum(f"pair_sum_{tag}{w}", where, grads[w], from_sib[w]) for w in range(n)]
    land = [lax.empty((3, *_half_shape(g.shape[1:])), BF16) for g in grads]
    t3 = []
    for w in range(n):
        for r, rel in enumerate(ICI_RELS):
            t3.append((w, lambda x, y, c, rel=rel: (_chip(x ^ rel[0], y ^ rel[1]), slice(None), slice(None)), n + w,
                       lambda x, y, c, r=r: (r, slice(None), slice(None)), rel))
    send_sems, recv_sems, bufs, token = _push_start(f"reduce_{tag}_start", pair + land, t3, after_start)
    return (tag, send_sems, recv_sems, bufs, t3, list(grads), list(from_sib)), token


def _reduce_finish(handle, where, after):
    tag, send_sems, recv_sems, bufs, t3, grads, from_sib = handle
    n = len(grads)
    from_chips = _push_wait(f"reduce_{tag}_wait", send_sems, recv_sems, bufs, t3, after)[n:]
    mine = [_chip_sum(f"chip_sum_{tag}{w}", where, grads[w], from_sib[w], from_chips[w]) for w in range(n)]
    t5 = []
    for w in range(n):
        def idx(x, y, c, shape2=grads[w].shape[1:]):
            return _half(shape2, c)
        t5.append((("out", w), idx, w, idx, SIBLING))
    return _push(f"reduce_{tag}_swap", mine, [jax.ShapeDtypeStruct(g.shape[1:], F32) for g in grads], t5,
                 aliases={w: w for w in range(n)})


def _pair_sum(name, where, g, from_sib):
    _, hr, hc = from_sib.shape
    tile = _pick(hr, max(16, (1 << 19) // hc // 16 * 16), 16)
    nb = hr // tile
    if _rows_split(g.shape[1:]):
        mine = lambda j, i, wh: (j, wh[0] * nb + i, 0)
    else:
        mine = lambda j, i, wh: (j, i, wh[0])

    def body(where_ref, g_ref, s_ref, o_ref):
        o_ref[...] = (g_ref[...] + s_ref[...]).astype(BF16)

    return pl.pallas_call(
        body, name=name,
        grid_spec=pltpu.PrefetchScalarGridSpec(
            num_scalar_prefetch=1, grid=(N_CHIPS, nb),
            in_specs=[pl.BlockSpec((None, tile, hc), mine),
                      pl.BlockSpec((None, tile, hc), lambda j, i, wh: (j, i, 0))],
            out_specs=pl.BlockSpec((None, tile, hc), lambda j, i, wh: (j, i, 0))),
        out_shape=jax.ShapeDtypeStruct((N_CHIPS, hr, hc), BF16),
        compiler_params=_params(("parallel", "parallel")),
    )(where, g, from_sib)


def _chip_sum(name, where, g, from_sib, from_chips):
    _, R, C = g.shape
    _, hr, hc = from_sib.shape
    tile = _pick(hr, max(16, (1 << 19) // hc // 16 * 16), 16)
    nb = hr // tile
    if _rows_split((R, C)):
        mine = lambda i, wh: (wh[0] * nb + i, 0)
    else:
        mine = lambda i, wh: (i, wh[0])

    def body(where_ref, g_ref, s_ref, r_ref, o_ref):
        acc = g_ref[...] + s_ref[...]
        for r in range(3):
            acc = acc + r_ref[r].astype(F32)
        o_ref[...] = acc

    return pl.pallas_call(
        body, name=name,
        grid_spec=pltpu.PrefetchScalarGridSpec(
            num_scalar_prefetch=1, grid=(nb,),
            in_specs=[pl.BlockSpec((None, tile, hc), lambda i, wh: (wh[1], *mine(i, wh))),
                      pl.BlockSpec((None, tile, hc), lambda i, wh: (wh[1], i, 0)),
                      pl.BlockSpec((3, tile, hc), lambda i, wh: (0, i, 0))],
            out_specs=pl.BlockSpec((tile, hc), mine)),
        out_shape=jax.ShapeDtypeStruct((R, C), F32),
        compiler_params=_params(("parallel",)),
    )(where, g, from_sib, from_chips)


def _allgather8(name, blk):
    m_per, n = blk.shape
    rels = [(0, 0, 1), (1, 0, 0), (0, 1, 0), (1, 1, 0), (1, 0, 1), (0, 1, 1), (1, 1, 1)]

    def body(x_ref, out_ref, send_sems, recv_sems):
        me = (lax.axis_index("x"), lax.axis_index("y"), lax.axis_index("c"))
        my_rows = out_ref.at[pl.ds((4 * me[0] + 2 * me[1] + me[2]) * m_per, m_per), :]
        out_ref[pl.ds((4 * me[0] + 2 * me[1] + me[2]) * m_per, m_per), :] = x_ref[...]
        copies = []
        for t, rel in enumerate(rels):
            cp = pltpu.make_async_remote_copy(src_ref=x_ref, dst_ref=my_rows, send_sem=send_sems.at[t],
                                              recv_sem=recv_sems.at[t], device_id=_flip(me, rel),
                                              device_id_type=MESH)
            cp.start()
            copies.append(cp)
        for cp in copies:
            cp.wait()

    return pl.pallas_call(
        body, name=name, out_shape=jax.ShapeDtypeStruct((N_DEV * m_per, n), blk.dtype),
        in_specs=[pl.BlockSpec(memory_space=pltpu.VMEM)], out_specs=pl.BlockSpec(memory_space=pltpu.VMEM),
        scratch_shapes=[pltpu.SemaphoreType.DMA((7,)), pltpu.SemaphoreType.DMA((7,))],
        compiler_params=pltpu.CompilerParams(has_side_effects=True, vmem_limit_bytes=VMEM_LIMIT),
    )(blk)


def _rope_tables(seq_len, dim, reps):
    rows = seq_len // GRID_W
    t = np.arange(seq_len)
    row, col = (t // GRID_W).astype(np.float32), (t % GRID_W).astype(np.float32)
    half = dim // 2
    inv = jnp.asarray(ROPE_THETA, F32) ** (-jnp.arange(0, half, 2, dtype=F32) / half)
    ang_r = jnp.asarray(row)[:, None] * inv[None, :]
    ang_c = jnp.asarray(col)[:, None] * inv[None, :]
    ang = jnp.concatenate([ang_r, ang_r, ang_c, ang_c], axis=-1)
    cos, sin = jnp.cos(ang), jnp.sin(ang)
    low = (np.arange(dim) % (dim // 2)) < (dim // 4)
    s_lo = jnp.where(jnp.asarray(low)[None, :], -sin, 0.0)
    s_hi = jnp.where(jnp.asarray(low)[None, :], 0.0, sin)
    del rows
    return tuple(jnp.tile(a, (1, reps)) for a in (cos, s_lo, s_hi))


def _make_rope(shift):
    def up(v):
        return pltpu.roll(v, LANE - shift, 1)

    def down(v):
        return pltpu.roll(v, shift, 1)

    @jax.custom_vjp
    def rope(v, cos, s_lo, s_hi):
        return v * cos + up(v) * s_lo + down(v) * s_hi

    def fwd(v, cos, s_lo, s_hi):
        return rope(v, cos, s_lo, s_hi), (cos, s_lo, s_hi)

    def bwd(res, dy):
        cos, s_lo, s_hi = res
        return dy * cos + down(dy * s_lo) + up(dy * s_hi), jnp.zeros_like(cos), jnp.zeros_like(cos), jnp.zeros_like(cos)

    rope.defvjp(fwd, bwd)
    return rope


_rope_a = _make_rope(HEAD_DIM // 4)
_rope_b = _make_rope(QK_ROPE // 4)


def _prep(segs, tabs, g_q, g_k, g_ckv):
    qa, ka, va, qn, qp, ckv, kpe = segs
    ta, tb = tabs
    q_a = jnp.concatenate([_rope_a(_rn(s) * g_q, *ta) for s in qa], axis=1)
    k_a = jnp.concatenate([_rope_a(_rn(s) * g_k, *ta) for s in ka], axis=1)
    cat = []
    for h in range(HB):
        cat += [qn[h], _rope_b(qp[h], *tb)]
    q_cat = jnp.concatenate(cat, axis=1)
    return q_a, k_a, va, q_cat, _rn(ckv) * g_ckv, _rope_b(kpe, *tb)


def _split_proj(proj):
    qa = [proj[:, O_QA + h * LANE:O_QA + (h + 1) * LANE] for h in range(HA)]
    ka = [proj[:, O_KA + h * LANE:O_KA + (h + 1) * LANE] for h in range(HKV)]
    va = proj[:, O_VA:O_QN]
    qn = [proj[:, O_QN + h * LANE:O_QN + (h + 1) * LANE] for h in range(HB)]
    qp = [proj[:, O_QP + h * LANE:O_QP + (h + 1) * LANE] for h in range(HB)]
    return qa, ka, va, qn, qp, proj[:, O_CKV:O_KPE], proj[:, O_KPE:W_INP]


def _permute_w_in(nat):
    K = nat.shape[1]
    qb = nat[W_QA + 2 * W_KA:W_QA + 2 * W_KA + W_QB].reshape(HB, QK_B, K)
    qp = jnp.concatenate([qb[:, QK_NOPE:], jnp.zeros((HB, LANE - QK_ROPE, K), nat.dtype)], axis=1)
    tail = nat[W_QA + 2 * W_KA + W_QB:]
    return jnp.concatenate([nat[:O_QN], qb[:, :QK_NOPE].reshape(HB * QK_NOPE, K), qp.reshape(HB * LANE, K),
                            tail, jnp.zeros((LANE - QK_ROPE, K), nat.dtype)], axis=0)


def _unpermute_w_in(p):
    K = p.shape[1]
    qn = p[O_QN:O_QP].reshape(HB, QK_NOPE, K)
    qp = p[O_QP:O_CKV].reshape(HB, LANE, K)[:, :QK_ROPE]
    qb = jnp.concatenate([qn, qp], axis=1).reshape(W_QB, K)
    return jnp.concatenate([p[:O_QN], qb, p[O_CKV:O_KPE + QK_ROPE]], axis=0)


def kernel(x, c, w_ada, b_ada, g_pre_attn, w_in, g_q_a, g_k_a, g_ckv, w_kv_b, g_out_a, g_out_b, w_out, g_post_attn, g_pre_mlp, w_mlp_in, w_mlp_out, g_post_mlp, loss_target, m_w_ada, m_b_ada, m_g_pre_attn, m_w_in, m_g_q_a, m_g_k_a, m_g_ckv, m_w_kv_b, m_g_out_a, m_g_out_b, m_w_out, m_g_post_attn, m_g_pre_mlp, m_w_mlp_in, m_w_mlp_out, m_g_post_mlp, v_w_ada, v_b_ada, v_g_pre_attn, v_w_in, v_g_q_a, v_g_k_a, v_g_ckv, v_w_kv_b, v_g_out_a, v_g_out_b, v_w_out, v_g_post_attn, v_g_pre_mlp, v_w_mlp_in, v_w_mlp_out, v_g_post_mlp):
    S, D = x.shape[1], x.shape[2]
    x2d, tgt = x[0], loss_target[0]
    ix, iy, ic = lax.axis_index("x"), lax.axis_index("y"), lax.axis_index("c")
    j_me = _chip(ix, iy)
    e_me = 4 * ix + 2 * iy + ic
    T = _pick(S, 256, 8)
    TB = _pick(S, 128, 8)

    where = jnp.stack([ic, j_me]).astype(jnp.int32)
    DFF = w_mlp_out.shape[1] * N_CHIPS

    c_all = _allgather8("gather_c", c.reshape(8, D // 8)).reshape(N_DEV, D)

    def silu_fn(cc):
        return (cc * (1.0 / (1.0 + jnp.exp(-cc))),), ()

    (c_act,), _ = _rowwise("silu_c", silu_fn, [c_all], [], [(D, BF16)], [], N_DEV)
    ca_pad = jnp.concatenate([c_act, jnp.zeros_like(c_act)], axis=0)
    n_mod = w_ada.shape[2]
    b_cols = lax.dynamic_slice(b_ada, (0, j_me * n_mod), (1, n_mod))
    (mod_part,) = _mm_nn("mod_mm", ca_pad, w_ada, [F32], epilogue=lambda acc, b: (acc + b,),
                         extras=(jnp.broadcast_to(b_cols, (16, n_mod)),))
    mod_all = _allgather8("gather_mod", mod_part[:8]).reshape(N_DEV, 8, n_mod)
    mod = jnp.concatenate([lax.dynamic_slice(mod_all, (2 * j, e_me, 0), (1, 1, n_mod))[0] for j in range(N_CHIPS)],
                          axis=1)

    w_in_t = jnp.swapaxes(w_in[0], 0, 1)
    h_qkv, tok_a = _gather_start("qkv", where, [w_in_t, w_kv_b[0]], (mod_all,))
    h_out, tok_b = _gather_start("out", where, [w_out[0]], (tok_a,), (tok_a,))
    h_up, tok_c = _gather_start("up", where, [w_mlp_in[0]], (tok_b,), (tok_b,))
    h_down, tok_d = _gather_start("down", where, [w_mlp_out[0]], (tok_c,), (tok_c,))
    mod = mod + tok_d[0, 0]
    sh_a, sc_a, gt_a, sh_m, sc_m, gt_m = [mod[:, i * D:(i + 1) * D] for i in range(6)]

    tabs_a = _rope_tables(S, HEAD_DIM, 1)
    tabs_b = _rope_tables(S, QK_ROPE, LANE // QK_ROPE)

    def pre_attn_fn(xt, g, sc, sh):
        return (_rn(xt) * g * (1.0 + sc) + sh,), ()

    (h_b,), _ = _rowwise("pre_attn", pre_attn_fn, [x2d], [g_pre_attn, sc_a, sh_a], [(D, BF16)], [], T)
    g_in, g_kvb = _gather_finish(h_qkv, h_b)
    wpt = _permute_w_in(g_in.reshape(W_IN, D))[None]
    (proj,) = _mm_nt("proj_mm", h_b, wpt, [F32])

    def prep_fn(pt, ca_, sla, sha, cb_, slb, shb, gq, gk, gc):
        return _prep(_split_proj(pt), ((ca_, sla, sha), (cb_, slb, shb)), gq, gk, gc), ()

    (qa_b, ka_b, va_b, qcat_b, ckvn_b, kpe_b), _ = _rowwise(
        "prep", prep_fn, [proj, *tabs_a, *tabs_b], [g_q_a, g_k_a, g_ckv],
        [(W_QA, BF16), (W_KA, BF16), (W_KA, BF16), (2 * HB * LANE, BF16), (KV_RANK, BF16), (LANE, BF16)], [], TB)
    (kv,) = _mm_nn("kv_mm", ckvn_b, g_kvb, [F32])

    def kcat_fn(kvt, kpet):
        kc, vs = [], []
        for h in range(HB):
            kc += [kvt[:, 2 * h * LANE:(2 * h + 1) * LANE], kpet.astype(F32)]
            vs.append(kvt[:, (2 * h + 1) * LANE:(2 * h + 2) * LANE])
        return (jnp.concatenate(kc, axis=1), jnp.concatenate(vs, axis=1)), ()

    (kcat_b, vb_b), _ = _rowwise("kcat", kcat_fn, [kv, kpe_b], [], [(2 * HB * LANE, BF16), (HB * V_DIM, BF16)], [], T)

    TQ = _pick(S, 256, 16)
    sc_a_ = 1.0 / math.sqrt(HEAD_DIM)
    sc_b_ = 1.0 / math.sqrt(QK_B)
    o_a = _attn_fwd("attn_a_fwd", qa_b, ka_b, va_b, HA, HA // HKV, HEAD_DIM, HEAD_DIM, sc_a_, TQ)
    m_out, tok_m = _gather_mid(h_out, o_a)
    m_up, tok_m = _gather_mid(h_up, tok_m)
    o_b = _attn_fwd("attn_b_fwd", qcat_b, kcat_b, vb_b, HB, 1, 2 * LANE, V_DIM, sc_b_, TQ, after=(tok_m,))
    m_down, tok_m = _gather_mid(h_down, o_b)

    def mix_fn(oa, ob, ga, gb):
        return (jnp.concatenate([_rn(oa) * ga, _rn(ob) * gb], axis=1),), ()

    (on_b,), _ = _rowwise("mix_norm", mix_fn, [o_a, o_b], [g_out_a, g_out_b], [(2 * W_QA, BF16)], [], T,
                          after=(tok_m,))
    (g_wout,) = _gather_end(m_out, on_b)
    w_out_f = g_wout.reshape(1, N_CHIPS * g_wout.shape[1], D)
    (o2,) = _mm_nn("out_mm", on_b, w_out_f, [F32])

    def mid(xt, o2t, gta, gpa, gpm, scm, shm):
        x1 = xt + gta * (_rn(o2t) * gpa)
        return x1, _rn(x1) * gpm * (1.0 + scm) + shm

    def mid_fn(*a):
        return mid(*a), ()

    mid_vecs = [gt_a, g_post_attn, g_pre_mlp, sc_m, sh_m]
    (x1, h2_b), _ = _rowwise("mid", mid_fn, [x2d, o2], mid_vecs, [(D, F32), (D, BF16)], [], T)
    (g_mlp_in,) = _gather_end(m_up, h2_b)
    u, a_b = _mm_nn("mlp_in_mm", h2_b, g_mlp_in, [F32, BF16],
                    epilogue=lambda acc: (acc, jnp.square(jnp.maximum(acc, 0.0))))
    (g_mlp_out,) = _gather_end(m_down, a_b)
    w_mlp_out_f = g_mlp_out.reshape(1, DFF, D)
    (y,) = _mm_nn("mlp_out_mm", a_b, w_mlp_out_f, [F32])

    def last(yt, x1t, gtm, gpo):
        return x1t + gtm * (_rn(yt) * gpo)

    def last_fn(yt, x1t, tt, gtm, gpo):
        x2, vjp = jax.vjp(last, yt, x1t, gtm, gpo)
        err = x2 - tt
        dy, dx1, dgt, dg = vjp(err * (1.0 / D))
        loss = 0.5 * jnp.sum(jnp.mean(err * err, axis=-1))
        return (dy, dx1), (jnp.full((1, LANE), loss, F32), dgt, dg)

    (dy_b, dx2), (loss_v, d_gt_m, d_g_post_mlp) = _rowwise(
        "loss_bwd", last_fn, [y, x1, tgt], [gt_m, g_post_mlp], [(D, BF16), (D, F32)], [LANE, D, D], TB)

    gw_mlp_out = _mm_tn("gw_mlp_out_mm", a_b, dy_b, 1).reshape(N_CHIPS, DFF // N_CHIPS, D)
    b_down, tok = _reduce_begin("down", [gw_mlp_out])
    (du_b,) = _mm_nt("d_mlp_out_mm", dy_b, w_mlp_out_f, [BF16],
                     epilogue=lambda acc, ut: (acc * (2.0 * jnp.maximum(ut, 0.0)),), extras=(u,), after=(tok,))
    r_down, tok = _reduce_start(b_down, where, du_b)
    gw_mlp_in = _mm_tn("gw_mlp_in_mm", h2_b, du_b, N_CHIPS, after=(tok,))
    b_up, tok = _reduce_begin("up", [gw_mlp_in])
    (dh2,) = _mm_nt("d_mlp_in_mm", du_b, g_mlp_in, [F32], after=(tok,))
    r_up, tok_up = _reduce_start(b_up, where, dh2)

    def mid_bwd_fn(xt, o2t, dh2t, dx2t, *vecs):
        _, vjp = jax.vjp(mid, xt, o2t, *vecs)
        dx, do2, dgta, dgpa, dgpm, dscm, dshm = vjp((dx2t, dh2t))
        return (dx, do2), (dgta, dgpa, dgpm, dscm, dshm)

    (dx1, do2_b), (d_gt_a, d_g_post_attn, d_g_pre_mlp, d_sc_m, d_sh_m) = _rowwise(
        "mid_bwd", mid_bwd_fn, [x2d, o2, dh2, dx2], mid_vecs, [(D, F32), (D, BF16)], [D] * 5, TB, after=(tok_up,))

    (d_on,) = _mm_nt("d_out_mm", do2_b, w_out_f, [F32])
    gw_out = _mm_tn("gw_out_mm", on_b, do2_b, 1).reshape(N_CHIPS, w_out.shape[1], D)
    b_out, tok_out = _reduce_begin("out", [gw_out])

    def mix_bwd_fn(oa, ob, dont, ga, gb):
        _, vjp = jax.vjp(lambda a_, b_, g1, g2: mix_fn(a_, b_, g1, g2)[0][0], oa, ob, ga, gb)
        doa, dob, dga, dgb = vjp(dont)
        return (doa, dob), (dga, dgb)

    (doa_b, dob_b), (d_g_out_a, d_g_out_b) = _rowwise(
        "mix_bwd", mix_bwd_fn, [o_a, o_b, d_on], [g_out_a, g_out_b], [(W_QA, BF16), (HB * V_DIM, BF16)],
        [W_QA, HB * V_DIM], T, after=(tok_out,))

    dqa, dka, dva = _attn_bwd("attn_a_bwd", qa_b, ka_b, va_b, doa_b, HA, HA // HKV, HEAD_DIM, HEAD_DIM, sc_a_, TQ)
    r_out, tok = _reduce_start(b_out, where, dqa)
    dqcat, dkcat, dvb = _attn_bwd("attn_b_bwd", qcat_b, kcat_b, vb_b, dob_b, HB, 1, 2 * LANE, V_DIM, sc_b_, TQ,
                                  after=(tok,))

    def kcat_bwd_fn(dkc, dvt):
        dkv, dkpe = [], None
        for h in range(HB):
            dkv += [dkc[:, 2 * h * LANE:(2 * h + 1) * LANE], dvt[:, h * V_DIM:(h + 1) * V_DIM]]
            piece = dkc[:, (2 * h + 1) * LANE:(2 * h + 2) * LANE]
            dkpe = piece if dkpe is None else dkpe + piece
        return (jnp.concatenate(dkv, axis=1), dkpe), ()

    (dkv_b, dkpe), _ = _rowwise("kcat_bwd", kcat_bwd_fn, [dkcat, dvb], [], [(2 * HB * LANE, BF16), (LANE, F32)], [], T)
    (d_ckvn,) = _mm_nt("d_kv_mm", dkv_b, g_kvb, [F32])
    gw_kvb = _mm_tn("gw_kv_mm", ckvn_b, dkv_b, N_CHIPS)
    b_kvb, tok_kvb = _reduce_begin("kvb", [gw_kvb])

    def prep_bwd_fn(pt, dqat, dkat, dvat, dqct, dckt, dkpt, ca_, sla, sha, cb_, slb, shb, gq, gk, gc):
        tabs = ((ca_, sla, sha), (cb_, slb, shb))
        _, vjp = jax.vjp(lambda sg, a1, a2, a3: _prep(sg, tabs, a1, a2, a3), _split_proj(pt), gq, gk, gc)
        (dqa_s, dka_s, dva_s, dqn_s, dqp_s, dck_s, dkp_s), dgq, dgk, dgc = vjp((dqat, dkat, dvat, dqct, dckt, dkpt))
        dproj = jnp.concatenate([*dqa_s, *dka_s, dva_s, *dqn_s, *dqp_s, dck_s, dkp_s], axis=1)
        return (dproj,), (dgq, dgk, dgc)

    (dproj_b,), (d_g_q_a, d_g_k_a, d_g_ckv) = _rowwise(
        "prep_bwd", prep_bwd_fn, [proj, dqa, dka, dva, dqcat, d_ckvn, dkpe, *tabs_a, *tabs_b],
        [g_q_a, g_k_a, g_ckv], [(W_INP, BF16)], [HEAD_DIM, HEAD_DIM, KV_RANK], TB, after=(tok_kvb,))

    r_kvb, tok = _reduce_start(b_kvb, where, dproj_b)
    (dh,) = _mm_nn("d_proj_mm", dproj_b, wpt, [F32], after=(tok,))
    gwpt = _mm_tn("gw_in_mm", dproj_b, h_b, 1)[0]
    gw_in = _unpermute_w_in(gwpt).reshape(N_CHIPS, W_IN // N_CHIPS, D)
    b_in, tok = _reduce_begin("in", [gw_in])

    def pre_attn_bwd_fn(xt, dht, dx1t, g, sc, sh):
        _, vjp = jax.vjp(lambda a_, b_, c_, d_: pre_attn_fn(a_, b_, c_, d_)[0][0], xt, g, sc, sh)
        dxt, dg, dsc, dsh = vjp(dht)
        return (dx1t + dxt,), (dg, dsc, dsh)

    (grad_x,), (d_g_pre_attn, d_sc_a, d_sh_a) = _rowwise(
        "pre_attn_bwd", pre_attn_bwd_fn, [x2d, dh, dx1], [g_pre_attn, sc_a, sh_a], [(D, F32)], [D] * 3, TB,
        after=(tok,))

    small_names = ["b_ada", "g_pre_attn", "g_q_a", "g_k_a", "g_ckv", "g_out_a", "g_out_b", "g_post_attn",
                   "g_pre_mlp", "g_post_mlp"]
    small_local = jnp.concatenate([d_sh_a, d_sc_a, d_gt_a, d_sh_m, d_sc_m, d_gt_m, d_g_pre_attn, d_g_q_a, d_g_k_a,
                                   d_g_ckv, d_g_out_a, d_g_out_b, d_g_post_attn, d_g_pre_mlp, d_g_post_mlp], axis=1)
    n_small = small_local.shape[1]
    small_all = _allgather8("gather_small", small_local.reshape(8, n_small // 8)).reshape(N_DEV, n_small)

    def sum8_fn(t):
        return (), (jnp.sum(t, axis=0, keepdims=True),)

    r_in, tok_in = _reduce_start(b_in, where, grad_x, (small_all,))
    _, (small_sum,) = _rowwise("sum_small", sum8_fn, [small_all], [], [], [n_small], N_DEV, after=(tok_in,))
    dmod_cols = lax.dynamic_slice(small_all, (0, j_me * n_mod), (N_DEV, n_mod))
    dmod_pad = jnp.concatenate([dmod_cols, jnp.zeros_like(dmod_cols)], axis=0).astype(BF16)
    gw_ada = _mm_tn("gw_ada_mm", ca_pad, dmod_pad, 1)[0]

    weights = dict(w_ada=w_ada, b_ada=b_ada, g_pre_attn=g_pre_attn, w_in=w_in, g_q_a=g_q_a, g_k_a=g_k_a, g_ckv=g_ckv,
                   w_kv_b=w_kv_b, g_out_a=g_out_a, g_out_b=g_out_b, w_out=w_out, g_post_attn=g_post_attn,
                   g_pre_mlp=g_pre_mlp, w_mlp_in=w_mlp_in, w_mlp_out=w_mlp_out, g_post_mlp=g_post_mlp)
    ms = dict(w_ada=m_w_ada, b_ada=m_b_ada, g_pre_attn=m_g_pre_attn, w_in=m_w_in, g_q_a=m_g_q_a, g_k_a=m_g_k_a,
              g_ckv=m_g_ckv, w_kv_b=m_w_kv_b, g_out_a=m_g_out_a, g_out_b=m_g_out_b, w_out=m_w_out,
              g_post_attn=m_g_post_attn, g_pre_mlp=m_g_pre_mlp, w_mlp_in=m_w_mlp_in, w_mlp_out=m_w_mlp_out,
              g_post_mlp=m_g_post_mlp)
    vs = dict(w_ada=v_w_ada, b_ada=v_b_ada, g_pre_attn=v_g_pre_attn, w_in=v_w_in, g_q_a=v_g_q_a, g_k_a=v_g_k_a,
              g_ckv=v_g_ckv, w_kv_b=v_w_kv_b, g_out_a=v_g_out_a, g_out_b=v_g_out_b, w_out=v_w_out,
              g_post_attn=v_g_post_attn, g_pre_mlp=v_g_pre_mlp, w_mlp_in=v_w_mlp_in, w_mlp_out=v_w_mlp_out,
              g_post_mlp=v_g_post_mlp)
    order = list(weights)
    grads, deltas, new_m, new_v = {}, {}, {}, {}

    def update(name, g, after=()):
        view = (lambda t: jnp.swapaxes(t, 0, 1)) if name == "w_in" else (lambda t: t)
        d_, m_, v_ = _adamw("adamw_" + name, view(weights[name][0]), g, view(ms[name][0]), view(vs[name][0]), after)
        grads[name], deltas[name], new_m[name], new_v[name] = (view(t)[None] for t in (g, d_, m_, v_))
        return d_

    last = update("w_ada", gw_ada, (small_sum,))
    for name, handle in (("w_mlp_out", r_down), ("w_mlp_in", r_up), ("w_out", r_out), ("w_kv_b", r_kvb),
                         ("w_in", r_in)):
        (g,) = _reduce_finish(handle, where, last)
        last = update(name, g)

    cat = lambda d: jnp.concatenate([d[n] for n in small_names], axis=1)
    d_, m_, v_ = _adamw("adamw_small", cat(weights), small_sum, cat(ms), cat(vs))
    off = 0
    for n in small_names:
        w_ = weights[n].shape[1]
        grads[n], deltas[n], new_m[n], new_v[n] = (a[:, off:off + w_] for a in (small_sum, d_, m_, v_))
        off += w_

    loss = lax.psum(loss_v[0, 0], ("x", "y", "c"))
    return (loss, grad_x[None], *[grads[n] for n in order], *[deltas[n] for n in order],
            *[new_m[n] for n in order], *[new_v[n] for n in order])
```

```python
import functools
import math

import numpy as np
import jax
import jax.numpy as jnp
from jax import lax
from jax.experimental import pallas as pl
from jax.experimental.pallas import tpu as pltpu

F32 = jnp.float32
BF16 = jnp.bfloat16
MESH = pl.DeviceIdType.MESH

EPS = 1e-6
GRID_W = 64
ROPE_THETA = 10000.0
HEAD_DIM = 128
HA = 8
HKV = 2
HB = 8
QK_NOPE = 128
QK_ROPE = 64
V_DIM = 128
KV_RANK = 512
QK_B = QK_NOPE + QK_ROPE
W_QA = HA * HEAD_DIM
W_KA = HKV * HEAD_DIM
W_QB = HB * QK_B
W_IN = W_QA + 2 * W_KA + W_QB + KV_RANK + QK_ROPE
N_CHIPS = 4
N_DEV = 8

LANE = 128
O_QA = 0
O_KA = O_QA + W_QA
O_VA = O_KA + W_KA
O_QN = O_VA + W_KA
O_QP = O_QN + HB * QK_NOPE
O_CKV = O_QP + HB * LANE
O_KPE = O_CKV + KV_RANK
W_INP = O_KPE + LANE

ADAM_LR = 0.001
ADAM_B1 = 0.9
ADAM_B2 = 0.999
ADAM_EPS = 1e-08
ADAM_WD = 0.01
ADAM_STEP = 10

VMEM_LIMIT = 56 * 1024 * 1024


def _params(sem):
    return pltpu.CompilerParams(dimension_semantics=sem, vmem_limit_bytes=VMEM_LIMIT)


def _pick(dim, target, unit):
    best = None
    t = unit
    while t <= min(dim, target):
        if dim % t == 0:
            best = t
        t += unit
    return dim if best is None else best


def _rn(x):
    return x * lax.rsqrt(jnp.mean(x * x, axis=-1, keepdims=True) + EPS)


def _rowwise(name, fn, rows, vecs, row_outs, vec_outs, tile, after=()):
    S = rows[0].shape[0]
    assert S % tile == 0
    n_r, n_v, n_ro, n_vo, n_a = len(rows), len(vecs), len(row_outs), len(vec_outs), len(after)

    def body(*refs):
        r_in = refs[:n_r]
        v_in = refs[n_r:n_r + n_v]
        r_out = refs[n_r + n_v + n_a:n_r + n_v + n_a + n_ro]
        v_out = refs[n_r + n_v + n_a + n_ro:]
        ro, vo = fn(*[r[...] for r in r_in], *[v[...] for v in v_in])
        for ref, val in zip(r_out, ro):
            ref[...] = val.astype(ref.dtype)
        if n_vo:
            @pl.when(pl.program_id(0) == 0)
            def _():
                for ref in v_out:
                    ref[...] = jnp.zeros(ref.shape, ref.dtype)
            for ref, val in zip(v_out, vo):
                ref[...] += val

    in_specs = [pl.BlockSpec((tile, a.shape[1]), lambda i: (i, 0)) for a in rows]
    in_specs += [pl.BlockSpec(a.shape, lambda i: (0, 0)) for a in vecs]
    in_specs += [pl.BlockSpec(memory_space=pl.ANY) for _ in after]
    out_specs = [pl.BlockSpec((tile, w), lambda i: (i, 0)) for w, _ in row_outs]
    out_specs += [pl.BlockSpec((1, w), lambda i: (0, 0)) for w in vec_outs]
    out_shape = [jax.ShapeDtypeStruct((S, w), dt) for w, dt in row_outs]
    out_shape += [jax.ShapeDtypeStruct((1, w), F32) for w in vec_outs]
    res = pl.pallas_call(
        body, name=name, grid=(S // tile,), in_specs=in_specs, out_specs=out_specs, out_shape=out_shape,
        compiler_params=_params(("arbitrary",)),
    )(*rows, *vecs, *after)
    return res[:n_ro], res[n_ro:]


CONTRACT_TILE = 2048


def _accumulate(acc, part, step, n_steps, finish):
    if n_steps == 1:
        finish(part)
        return

    @pl.when(step == 0)
    def _():
        acc[...] = part

    @pl.when(step > 0)
    def _():
        acc[...] += part

    @pl.when(step == n_steps - 1)
    def _():
        finish(acc[...])


def _mm_nn(name, a, b3, out_dtypes, epilogue=None, extras=(), after=()):
    M, K = a.shape
    nB, K2, Nsh = b3.shape
    assert K == K2
    N = nB * Nsh
    tm, tn, tk = _pick(M, 1024, 16), _pick(Nsh, 1536, LANE), _pick(K, CONTRACT_TILE, LANE)
    nps, nk, ne, no, na = Nsh // tn, K // tk, len(extras), len(out_dtypes), len(after)

    def body(a_ref, b_ref, *rest):
        e_refs, o_refs = rest[:ne], rest[ne + na:ne + na + no]

        def finish(total):
            vals = (total,) if epilogue is None else epilogue(total, *[e[...] for e in e_refs])
            for ref, val in zip(o_refs, vals):
                ref[...] = val.astype(ref.dtype)

        part = jnp.dot(a_ref[...].astype(BF16), b_ref[...].astype(BF16), preferred_element_type=F32)
        _accumulate(rest[-1], part, pl.program_id(2), nk, finish)

    in_specs = [pl.BlockSpec((tm, tk), lambda i, j, k: (i, k)),
                pl.BlockSpec((None, tk, tn), lambda i, j, k: (j // nps, k, j % nps))]
    in_specs += [pl.BlockSpec((tm, tn), lambda i, j, k: (i, j)) for _ in extras]
    in_specs += [pl.BlockSpec(memory_space=pl.ANY) for _ in after]
    return pl.pallas_call(
        body, name=name, grid=(M // tm, N // tn, nk), in_specs=in_specs,
        out_specs=[pl.BlockSpec((tm, tn), lambda i, j, k: (i, j)) for _ in out_dtypes],
        out_shape=[jax.ShapeDtypeStruct((M, N), dt) for dt in out_dtypes],
        scratch_shapes=[pltpu.VMEM((tm, tn), F32)] if nk > 1 else [],
        compiler_params=_params(("parallel", "parallel", "arbitrary")),
    )(a, b3, *extras, *after)


def _mm_nt(name, a, b3, out_dtypes, epilogue=None, extras=(), after=()):
    M, N = a.shape
    nB, K, Nsh = b3.shape
    assert N == nB * Nsh
    tm, to, tn = _pick(M, 1024, 16), _pick(K, 1536, LANE), _pick(Nsh, CONTRACT_TILE, LANE)
    nps, nn, ne, no, na = Nsh // tn, N // tn, len(extras), len(out_dtypes), len(after)

    def body(a_ref, b_ref, *rest):
        e_refs, o_refs = rest[:ne], rest[ne + na:ne + na + no]

        def finish(total):
            vals = (total,) if epilogue is None else epilogue(total, *[e[...] for e in e_refs])
            for ref, val in zip(o_refs, vals):
                ref[...] = val.astype(ref.dtype)

        part = lax.dot_general(a_ref[...].astype(BF16), b_ref[...].astype(BF16),
                               (((1,), (1,)), ((), ())), preferred_element_type=F32)
        _accumulate(rest[-1], part, pl.program_id(2), nn, finish)

    in_specs = [pl.BlockSpec((tm, tn), lambda i, o, n: (i, n)),
                pl.BlockSpec((None, to, tn), lambda i, o, n: (n // nps, o, n % nps))]
    in_specs += [pl.BlockSpec((tm, to), lambda i, o, n: (i, o)) for _ in extras]
    in_specs += [pl.BlockSpec(memory_space=pl.ANY) for _ in after]
    return pl.pallas_call(
        body, name=name, grid=(M // tm, K // to, nn), in_specs=in_specs,
        out_specs=[pl.BlockSpec((tm, to), lambda i, o, n: (i, o)) for _ in out_dtypes],
        out_shape=[jax.ShapeDtypeStruct((M, K), dt) for dt in out_dtypes],
        scratch_shapes=[pltpu.VMEM((tm, to), F32)] if nn > 1 else [],
        compiler_params=_params(("parallel", "parallel", "arbitrary")),
    )(a, b3, *extras, *after)


def _mm_tn(name, a, g, nB, after=()):
    M, K = a.shape
    M2, N = g.shape
    assert M == M2 and N % nB == 0
    Nsh = N // nB
    tk, tn, tm = _pick(K, 1536, LANE), _pick(Nsh, 1024, LANE), _pick(M, CONTRACT_TILE, 16)
    nps, nm = Nsh // tn, M // tm

    def body(a_ref, g_ref, *rest):
        o_ref = rest[len(after)]

        def finish(total):
            o_ref[...] = total

        part = lax.dot_general(a_ref[...].astype(BF16), g_ref[...].astype(BF16),
                               (((0,), (0,)), ((), ())), preferred_element_type=F32)
        _accumulate(rest[-1], part, pl.program_id(2), nm, finish)

    return pl.pallas_call(
        body, name=name, grid=(K // tk, N // tn, nm),
        in_specs=[pl.BlockSpec((tm, tk), lambda k, n, m: (m, k)),
                  pl.BlockSpec((tm, tn), lambda k, n, m: (m, n))] + [pl.BlockSpec(memory_space=pl.ANY) for _ in after],
        out_specs=pl.BlockSpec((None, tk, tn), lambda k, n, m: (n // nps, k, n % nps)),
        out_shape=jax.ShapeDtypeStruct((nB, K, Nsh), F32),
        scratch_shapes=[pltpu.VMEM((tk, tn), F32)] if nm > 1 else [],
        compiler_params=_params(("parallel", "parallel", "arbitrary")),
    )(a, g, *after)


def _attn_fwd(name, q, k, v, n_heads, group, dqk, dv, scale, tq, after=()):
    S = q.shape[0]

    def body(q_ref, k_ref, v_ref, *rest):
        o_ref, lse_ref = rest[-2:]
        s = lax.dot_general(q_ref[...], k_ref[...], (((1,), (1,)), ((), ())), preferred_element_type=F32)
        m = jnp.max(s, axis=-1, keepdims=True)
        e = jnp.exp((s - m) * scale)
        l = jnp.sum(e, axis=-1, keepdims=True)
        o_ref[...] = jnp.dot(e.astype(BF16), v_ref[...], preferred_element_type=F32) * (1.0 / l)
        lse_ref[...] = jnp.broadcast_to(m * scale + jnp.log(l), lse_ref.shape)

    return pl.pallas_call(
        body, name=name, grid=(n_heads, S // tq),
        in_specs=[pl.BlockSpec((tq, dqk), lambda h, i: (i, h)),
                  pl.BlockSpec((S, dqk), lambda h, i: (0, h // group)),
                  pl.BlockSpec((S, dv), lambda h, i: (0, h // group))] + [pl.BlockSpec(memory_space=pl.ANY) for _ in after],
        out_specs=[pl.BlockSpec((tq, dv), lambda h, i: (i, h)), pl.BlockSpec((tq, LANE), lambda h, i: (i, h))],
        out_shape=[jax.ShapeDtypeStruct((S, n_heads * dv), F32), jax.ShapeDtypeStruct((S, n_heads * LANE), F32)],
        compiler_params=_params(("arbitrary", "arbitrary")),
    )(q, k, v, *after)


def _attn_bwd(name, q, k, v, do, o, lse, n_heads, group, dqk, dv, scale, tq, after=()):
    S = q.shape[0]
    n_kv = n_heads // group

    def body(q_ref, k_ref, v_ref, do_ref, o_ref, lse_ref, *rest):
        dq_ref, dk_ref, dv_ref = rest[len(after):]
        h, i = pl.program_id(0), pl.program_id(1)

        @pl.when((h % group == 0) & (i == 0))
        def _():
            dk_ref[...] = jnp.zeros(dk_ref.shape, F32)
            dv_ref[...] = jnp.zeros(dv_ref.shape, F32)

        qb, kb, vb, dob = q_ref[...], k_ref[...], v_ref[...], do_ref[...]
        s = lax.dot_general(qb, kb, (((1,), (1,)), ((), ())), preferred_element_type=F32)
        p = jnp.exp(s * scale - jnp.tile(lse_ref[...], (1, S // LANE)))
        dp = lax.dot_general(dob, vb, (((1,), (1,)), ((), ())), preferred_element_type=F32)
        delta = jnp.sum(dob.astype(F32) * o_ref[...], axis=-1, keepdims=True)
        ds = (p * ((dp - delta) * scale)).astype(BF16)
        dv_ref[...] += lax.dot_general(p.astype(BF16), dob, (((0,), (0,)), ((), ())), preferred_element_type=F32)
        dq_ref[...] = jnp.dot(ds, kb, preferred_element_type=F32)
        dk_ref[...] += lax.dot_general(ds, qb, (((0,), (0,)), ((), ())), preferred_element_type=F32)

    return pl.pallas_call(
        body, name=name, grid=(n_heads, S // tq),
        in_specs=[pl.BlockSpec((tq, dqk), lambda h, i: (i, h)),
                  pl.BlockSpec((S, dqk), lambda h, i: (0, h // group)),
                  pl.BlockSpec((S, dv), lambda h, i: (0, h // group)),
                  pl.BlockSpec((tq, dv), lambda h, i: (i, h)),
                  pl.BlockSpec((tq, dv), lambda h, i: (i, h)),
                  pl.BlockSpec((tq, LANE), lambda h, i: (i, h))] + [pl.BlockSpec(memory_space=pl.ANY) for _ in after],
        out_specs=[pl.BlockSpec((tq, dqk), lambda h, i: (i, h)),
                   pl.BlockSpec((S, dqk), lambda h, i: (0, h // group)),
                   pl.BlockSpec((S, dv), lambda h, i: (0, h // group))],
        out_shape=[jax.ShapeDtypeStruct((S, n_heads * dqk), F32),
                   jax.ShapeDtypeStruct((S, n_kv * dqk), F32),
                   jax.ShapeDtypeStruct((S, n_kv * dv), F32)],
        compiler_params=_params(("arbitrary", "arbitrary")),
    )(q, k, v, do, o, lse, *after)


def _adamw(name, w, g, m, v, after=()):
    R, C = w.shape
    tile = _pick(R, max(8, (1 << 19) // C // 8 * 8), 8)
    c1 = 1.0 - ADAM_B1 ** ADAM_STEP
    c2 = 1.0 - ADAM_B2 ** ADAM_STEP

    def body(w_ref, g_ref, m_ref, v_ref, *rest):
        d_ref, nm_ref, nv_ref = rest[len(after):]
        gg = g_ref[...]
        nm = ADAM_B1 * m_ref[...] + (1.0 - ADAM_B1) * gg
        nv = ADAM_B2 * v_ref[...] + (1.0 - ADAM_B2) * (gg * gg)
        d_ref[...] = -ADAM_LR * ((nm / c1) / (jnp.sqrt(nv / c2) + ADAM_EPS) + ADAM_WD * w_ref[...])
        nm_ref[...] = nm
        nv_ref[...] = nv

    spec = pl.BlockSpec((tile, C), lambda i: (i, 0))
    return pl.pallas_call(
        body, name=name, grid=(R // tile,), in_specs=[spec] * 4 + [pl.BlockSpec(memory_space=pl.ANY) for _ in after],
        out_specs=[spec] * 3, out_shape=[jax.ShapeDtypeStruct((R, C), F32)] * 3,
        compiler_params=_params(("parallel",)),
    )(w, g, m, v, *after)


def _flip(me, rel):
    return tuple(1 - p if r else p for p, r in zip(me, rel))


def _push(name, ins, out_shapes, transfers, aliases=None):
    ni, no, nt = len(ins), len(out_shapes), len(transfers)

    def body(*refs):
        in_refs, out_refs = refs[:ni], refs[ni:ni + no]
        send_sems, recv_sems = refs[ni + no], refs[ni + no + 1]
        me = (lax.axis_index("x"), lax.axis_index("y"), lax.axis_index("c"))
        copies = []
        for t, (si, sfn, oi, dfn, rel) in enumerate(transfers):
            src_ref = out_refs[si[1]] if isinstance(si, tuple) else in_refs[si]
            src = src_ref.at[sfn(*me)]
            dst = out_refs[oi].at[dfn(*me)]
            if rel is None:
                cp = pltpu.make_async_copy(src, dst, send_sems.at[t])
            else:
                cp = pltpu.make_async_remote_copy(src_ref=src, dst_ref=dst, send_sem=send_sems.at[t],
                                                  recv_sem=recv_sems.at[t], device_id=_flip(me, rel),
                                                  device_id_type=MESH)
            cp.start()
            copies.append(cp)
        for cp in copies:
            cp.wait()

    any_spec = pl.BlockSpec(memory_space=pl.ANY)
    return pl.pallas_call(
        body, name=name, in_specs=[any_spec] * ni, out_specs=[any_spec] * no, out_shape=out_shapes,
        scratch_shapes=[pltpu.SemaphoreType.DMA((nt,)), pltpu.SemaphoreType.DMA((nt,))],
        input_output_aliases=aliases or {},
        compiler_params=pltpu.CompilerParams(has_side_effects=True),
    )(*ins)


_HBM = pl.BlockSpec(memory_space=pltpu.HBM)
_SEM = pl.BlockSpec(memory_space=pltpu.SEMAPHORE)
_DATAFLOW = pltpu.SideEffectType.DATAFLOW_SIDE_EFFECTING


def _split_copies(refs, transfers, send_sems, recv_sems):
    me = (lax.axis_index("x"), lax.axis_index("y"), lax.axis_index("c"))
    return [pltpu.make_async_remote_copy(src_ref=refs[sb].at[sfn(*me)], dst_ref=refs[db].at[dfn(*me)],
                                         send_sem=send_sems.at[t], recv_sem=recv_sems.at[t],
                                         device_id=_flip(me, rel), device_id_type=MESH)
            for t, (sb, sfn, db, dfn, rel) in enumerate(transfers)]


def _push_start(name, bufs, transfers, after=()):
    nb, na, nt = len(bufs), len(after), len(transfers)

    def body(*refs):
        send_sems, recv_sems, token = refs[nb + na], refs[nb + na + 1], refs[-1]
        for cp in _split_copies(refs[nb + na + 2:2 * nb + na + 2], transfers, send_sems, recv_sems):
            cp.start()
        token[...] = jnp.zeros(token.shape, token.dtype)

    res = pl.pallas_call(
        body, name=name,
        out_shape=(pltpu.SemaphoreType.DMA((nt,)), pltpu.SemaphoreType.DMA((nt,)),
                   *[pltpu.HBM(b.shape, b.dtype) for b in bufs], jax.ShapeDtypeStruct((8, LANE), F32)),
        in_specs=[_HBM] * nb + [pl.BlockSpec(memory_space=pl.ANY)] * na,
        out_specs=(_SEM, _SEM, *[_HBM] * nb, pl.BlockSpec(memory_space=pltpu.VMEM)),
        input_output_aliases={i: i + 2 for i in range(nb)},
        compiler_params=pltpu.CompilerParams(has_side_effects=_DATAFLOW),
    )(*[pltpu.with_memory_space_constraint(b, pltpu.HBM) for b in bufs], *after)
    return res[0], res[1], list(res[2:2 + nb]), res[-1]


def _push_wait(name, send_sems, recv_sems, bufs, transfers, after):
    nb = len(bufs)

    def body(*refs):
        for cp in _split_copies(refs[:nb], transfers, refs[nb], refs[nb + 1]):
            cp.wait_send()
            cp.wait_recv()

    res = pl.pallas_call(
        body, name=name, out_shape=[pltpu.HBM(b.shape, b.dtype) for b in bufs],
        in_specs=[_HBM] * nb + [_SEM, _SEM, pl.BlockSpec(memory_space=pl.ANY)], out_specs=[_HBM] * nb,
        input_output_aliases={i: i for i in range(nb)},
        compiler_params=pltpu.CompilerParams(has_side_effects=_DATAFLOW),
    )(*bufs, send_sems, recv_sems, after)
    return list(res)


ICI_RELS = ((1, 0, 0), (0, 1, 0), (1, 1, 0))
SIBLING = (0, 0, 1)


def _chip(x, y):
    return 2 * x + y


def _cast_into_slot(name, where, w, after=()):
    R, C = w.shape
    tile = _pick(R, 512, BF16_ROWS)

    def body(where_ref, w_ref, *rest):
        rest[-1][...] = w_ref[...].astype(BF16)

    return pl.pallas_call(
        body, name=name,
        grid_spec=pltpu.PrefetchScalarGridSpec(
            num_scalar_prefetch=1, grid=(R // tile,),
            in_specs=[pl.BlockSpec((tile, C), lambda i, wh: (i, 0))] + [pl.BlockSpec(memory_space=pl.ANY) for _ in after],
            out_specs=pl.BlockSpec((None, tile, C), lambda i, wh: (wh[1], i, 0))),
        out_shape=jax.ShapeDtypeStruct((N_CHIPS, R, C), BF16),
        compiler_params=_params(("parallel",)),
    )(where, w, *after)


BF16_ROWS = 16


def _rows_split(shape2):
    return (shape2[0] // 2) % BF16_ROWS == 0


def _half_shape(shape2):
    R, C = shape2
    return (R // 2, C) if _rows_split(shape2) else (R, C // 2)


def _half(shape2, c):
    R, C = shape2
    if _rows_split(shape2):
        return (pl.ds(c * (R // 2), R // 2), slice(None))
    return (slice(None), pl.ds(pl.multiple_of(c * (C // 2), LANE), C // 2))


def _gather_start(tag, where, shards, after, cast_after=()):
    n = len(shards)
    own = [_cast_into_slot(f"cast_w_{tag}{w}", where, shards[w], cast_after) for w in range(n)]
    transfers = []
    for w in range(n):
        def idx(x, y, c, shape2=shards[w].shape):
            return (_chip(x, y), *_half(shape2, c))
        transfers += [(w, idx, w, idx, rel) for rel in ICI_RELS]
    send_sems, recv_sems, bufs, token = _push_start(f"gather_{tag}_start", own, transfers, after)
    return (tag, send_sems, recv_sems, bufs, transfers), token


def _forward_transfers(part):
    second = []
    for w, p in enumerate(part):
        for rel in ICI_RELS:
            def idx(x, y, c, shape2=p.shape[1:], rel=rel):
                return (_chip(x ^ rel[0], y ^ rel[1]), *_half(shape2, c))
            second.append((w, idx, w, idx, SIBLING))
    return second


def _gather_finish(handle, after):
    tag, send_sems, recv_sems, bufs, transfers = handle
    part = _push_wait(f"gather_{tag}_wait", send_sems, recv_sems, bufs, transfers, after)
    outs = [jax.ShapeDtypeStruct(p.shape, p.dtype) for p in part]
    second = [(("out", sb), sfn, db, dfn, rel) for sb, sfn, db, dfn, rel in _forward_transfers(part)]
    return _push(f"gather_{tag}_d2d", list(part), outs, second, aliases={w: w for w in range(len(part))})


def _gather_mid(handle, after):
    tag, send_sems, recv_sems, bufs, transfers = handle
    part = _push_wait(f"gather_{tag}_wait", send_sems, recv_sems, bufs, transfers, after)
    second = _forward_transfers(part)
    s2, r2, bufs2, token = _push_start(f"gather_{tag}_d2d_start", part, second)
    return (tag, s2, r2, bufs2, second), token


def _gather_end(handle, after):
    tag, send_sems, recv_sems, bufs, transfers = handle
    return _push_wait(f"gather_{tag}_d2d_wait", send_sems, recv_sems, bufs, transfers, after)


def _reduce_begin(tag, grads, after=()):
    n = len(grads)
    land = [lax.empty((N_CHIPS, *_half_shape(g.shape[1:])), F32) for g in grads]
    t1 = []
    for w in range(n):
        t1.append((w, lambda x, y, c, shape2=grads[w].shape[1:]: (slice(None), *_half(shape2, 1 - c)), n + w,
                   lambda x, y, c: (slice(None), slice(None), slice(None)), SIBLING))
    send_sems, recv_sems, bufs, token = _push_start(f"reduce_{tag}_d2d_start", list(grads) + land, t1, after)
    return (tag, send_sems, recv_sems, bufs, t1), token


def _reduce_start(handle, where, after, after_start=()):
    tag, send_sems, recv_sems, bufs, t1 = handle
    n = len(bufs) // 2
    res = _push_wait(f"reduce_{tag}_d2d_wait", send_sems, recv_sems, bufs, t1, after)
    grads, from_sib = res[:n], res[n:]
    pair = [_pair_sum(f"pair_sum_{tag}{w}", where, grads[w], from_sib[w]) for w in range(n)]
    land = [lax.empty((3, *_half_shape(g.shape[1:])), BF16) for g in grads]
    t3 = []
    for w in range(n):
        for r, rel in enumerate(ICI_RELS):
            t3.append((w, lambda x, y, c, rel=rel: (_chip(x ^ rel[0], y ^ rel[1]), slice(None), slice(None)), n + w,
                       lambda x, y, c, r=r: (r, slice(None), slice(None)), rel))
    send_sems, recv_sems, bufs, token = _push_start(f"reduce_{tag}_start", pair + land, t3, after_start)
    return (tag, send_sems, recv_sems, bufs, t3, list(grads), list(from_sib)), token


def _reduce_finish(handle, where, after):
    tag, send_sems, recv_sems, bufs, t3, grads, from_sib = handle
    n = len(grads)
    from_chips = _push_wait(f"reduce_{tag}_wait", send_sems, recv_sems, bufs, t3, after)[n:]
    mine = [_chip_sum(f"chip_sum_{tag}{w}", where, grads[w], from_sib[w], from_chips[w]) for w in range(n)]
    t5 = []
    for w in range(n):
        def idx(x, y, c, shape2=grads[w].shape[1:]):
            return _half(shape2, c)
        t5.append((w, idx, w, idx, SIBLING))
    send_sems, recv_sems, bufs, token = _push_start(f"reduce_{tag}_swap_start", mine, t5)
    return (tag, send_sems, recv_sems, bufs, t5), token


def _reduce_end(handle, after):
    tag, send_sems, recv_sems, bufs, t5 = handle
    return _push_wait(f"reduce_{tag}_swap_wait", send_sems, recv_sems, bufs, t5, after)


def _pair_sum(name, where, g, from_sib):
    _, hr, hc = from_sib.shape
    tile = _pick(hr, max(16, (1 << 19) // hc // 16 * 16), 16)
    nb = hr // tile
    if _rows_split(g.shape[1:]):
        mine = lambda j, i, wh: (j, wh[0] * nb + i, 0)
    else:
        mine = lambda j, i, wh: (j, i, wh[0])

    def body(where_ref, g_ref, s_ref, o_ref):
        o_ref[...] = (g_ref[...] + s_ref[...]).astype(BF16)

    return pl.pallas_call(
        body, name=name,
        grid_spec=pltpu.PrefetchScalarGridSpec(
            num_scalar_prefetch=1, grid=(N_CHIPS, nb),
            in_specs=[pl.BlockSpec((None, tile, hc), mine),
                      pl.BlockSpec((None, tile, hc), lambda j, i, wh: (j, i, 0))],
            out_specs=pl.BlockSpec((None, tile, hc), lambda j, i, wh: (j, i, 0))),
        out_shape=jax.ShapeDtypeStruct((N_CHIPS, hr, hc), BF16),
        compiler_params=_params(("parallel", "parallel")),
    )(where, g, from_sib)


def _chip_sum(name, where, g, from_sib, from_chips):
    _, R, C = g.shape
    _, hr, hc = from_sib.shape
    tile = _pick(hr, max(16, (1 << 19) // hc // 16 * 16), 16)
    nb = hr // tile
    if _rows_split((R, C)):
        mine = lambda i, wh: (wh[0] * nb + i, 0)
    else:
        mine = lambda i, wh: (i, wh[0])

    def body(where_ref, g_ref, s_ref, r_ref, o_ref):
        acc = g_ref[...] + s_ref[...]
        for r in range(3):
            acc = acc + r_ref[r].astype(F32)
        o_ref[...] = acc

    return pl.pallas_call(
        body, name=name,
        grid_spec=pltpu.PrefetchScalarGridSpec(
            num_scalar_prefetch=1, grid=(nb,),
            in_specs=[pl.BlockSpec((None, tile, hc), lambda i, wh: (wh[1], *mine(i, wh))),
                      pl.BlockSpec((None, tile, hc), lambda i, wh: (wh[1], i, 0)),
                      pl.BlockSpec((3, tile, hc), lambda i, wh: (0, i, 0))],
            out_specs=pl.BlockSpec((tile, hc), mine)),
        out_shape=jax.ShapeDtypeStruct((R, C), F32),
        compiler_params=_params(("parallel",)),
    )(where, g, from_sib, from_chips)


def _allgather8(name, blk):
    m_per, n = blk.shape
    rels = [(0, 0, 1), (1, 0, 0), (0, 1, 0), (1, 1, 0), (1, 0, 1), (0, 1, 1), (1, 1, 1)]

    def body(x_ref, out_ref, send_sems, recv_sems):
        me = (lax.axis_index("x"), lax.axis_index("y"), lax.axis_index("c"))
        my_rows = out_ref.at[pl.ds((4 * me[0] + 2 * me[1] + me[2]) * m_per, m_per), :]
        out_ref[pl.ds((4 * me[0] + 2 * me[1] + me[2]) * m_per, m_per), :] = x_ref[...]
        copies = []
        for t, rel in enumerate(rels):
            cp = pltpu.make_async_remote_copy(src_ref=x_ref, dst_ref=my_rows, send_sem=send_sems.at[t],
                                              recv_sem=recv_sems.at[t], device_id=_flip(me, rel),
                                              device_id_type=MESH)
            cp.start()
            copies.append(cp)
        for cp in copies:
            cp.wait()

    return pl.pallas_call(
        body, name=name, out_shape=jax.ShapeDtypeStruct((N_DEV * m_per, n), blk.dtype),
        in_specs=[pl.BlockSpec(memory_space=pltpu.VMEM)], out_specs=pl.BlockSpec(memory_space=pltpu.VMEM),
        scratch_shapes=[pltpu.SemaphoreType.DMA((7,)), pltpu.SemaphoreType.DMA((7,))],
        compiler_params=pltpu.CompilerParams(has_side_effects=True, vmem_limit_bytes=VMEM_LIMIT),
    )(blk)


def _rope_tables(seq_len, dim, reps):
    rows = seq_len // GRID_W
    t = np.arange(seq_len)
    row, col = (t // GRID_W).astype(np.float32), (t % GRID_W).astype(np.float32)
    half = dim // 2
    inv = jnp.asarray(ROPE_THETA, F32) ** (-jnp.arange(0, half, 2, dtype=F32) / half)
    ang_r = jnp.asarray(row)[:, None] * inv[None, :]
    ang_c = jnp.asarray(col)[:, None] * inv[None, :]
    ang = jnp.concatenate([ang_r, ang_r, ang_c, ang_c], axis=-1)
    cos, sin = jnp.cos(ang), jnp.sin(ang)
    low = (np.arange(dim) % (dim // 2)) < (dim // 4)
    s_lo = jnp.where(jnp.asarray(low)[None, :], -sin, 0.0)
    s_hi = jnp.where(jnp.asarray(low)[None, :], 0.0, sin)
    del rows
    return tuple(jnp.tile(a, (1, reps)) for a in (cos, s_lo, s_hi))


def _make_rope(shift):
    def up(v):
        return pltpu.roll(v, LANE - shift, 1)

    def down(v):
        return pltpu.roll(v, shift, 1)

    @jax.custom_vjp
    def rope(v, cos, s_lo, s_hi):
        return v * cos + up(v) * s_lo + down(v) * s_hi

    def fwd(v, cos, s_lo, s_hi):
        return rope(v, cos, s_lo, s_hi), (cos, s_lo, s_hi)

    def bwd(res, dy):
        cos, s_lo, s_hi = res
        return dy * cos + down(dy * s_lo) + up(dy * s_hi), jnp.zeros_like(cos), jnp.zeros_like(cos), jnp.zeros_like(cos)

    rope.defvjp(fwd, bwd)
    return rope


_rope_a = _make_rope(HEAD_DIM // 4)
_rope_b = _make_rope(QK_ROPE // 4)


def _prep(segs, tabs, g_q, g_k, g_ckv):
    qa, ka, va, qn, qp, ckv, kpe = segs
    ta, tb = tabs
    q_a = jnp.concatenate([_rope_a(_rn(s) * g_q, *ta) for s in qa], axis=1)
    k_a = jnp.concatenate([_rope_a(_rn(s) * g_k, *ta) for s in ka], axis=1)
    cat = []
    for h in range(HB):
        cat += [qn[h], _rope_b(qp[h], *tb)]
    q_cat = jnp.concatenate(cat, axis=1)
    return q_a, k_a, va, q_cat, _rn(ckv) * g_ckv, _rope_b(kpe, *tb)


def _split_proj(proj):
    qa = [proj[:, O_QA + h * LANE:O_QA + (h + 1) * LANE] for h in range(HA)]
    ka = [proj[:, O_KA + h * LANE:O_KA + (h + 1) * LANE] for h in range(HKV)]
    va = proj[:, O_VA:O_QN]
    qn = [proj[:, O_QN + h * LANE:O_QN + (h + 1) * LANE] for h in range(HB)]
    qp = [proj[:, O_QP + h * LANE:O_QP + (h + 1) * LANE] for h in range(HB)]
    return qa, ka, va, qn, qp, proj[:, O_CKV:O_KPE], proj[:, O_KPE:W_INP]


def _permute_w_in(nat):
    K = nat.shape[1]
    qb = nat[W_QA + 2 * W_KA:W_QA + 2 * W_KA + W_QB].reshape(HB, QK_B, K)
    qp = jnp.concatenate([qb[:, QK_NOPE:], jnp.zeros((HB, LANE - QK_ROPE, K), nat.dtype)], axis=1)
    tail = nat[W_QA + 2 * W_KA + W_QB:]
    return jnp.concatenate([nat[:O_QN], qb[:, :QK_NOPE].reshape(HB * QK_NOPE, K), qp.reshape(HB * LANE, K),
                            tail, jnp.zeros((LANE - QK_ROPE, K), nat.dtype)], axis=0)


def _unpermute_w_in(p):
    K = p.shape[1]
    qn = p[O_QN:O_QP].reshape(HB, QK_NOPE, K)
    qp = p[O_QP:O_CKV].reshape(HB, LANE, K)[:, :QK_ROPE]
    qb = jnp.concatenate([qn, qp], axis=1).reshape(W_QB, K)
    return jnp.concatenate([p[:O_QN], qb, p[O_CKV:O_KPE + QK_ROPE]], axis=0)


def kernel(x, c, w_ada, b_ada, g_pre_attn, w_in, g_q_a, g_k_a, g_ckv, w_kv_b, g_out_a, g_out_b, w_out, g_post_attn, g_pre_mlp, w_mlp_in, w_mlp_out, g_post_mlp, loss_target, m_w_ada, m_b_ada, m_g_pre_attn, m_w_in, m_g_q_a, m_g_k_a, m_g_ckv, m_w_kv_b, m_g_out_a, m_g_out_b, m_w_out, m_g_post_attn, m_g_pre_mlp, m_w_mlp_in, m_w_mlp_out, m_g_post_mlp, v_w_ada, v_b_ada, v_g_pre_attn, v_w_in, v_g_q_a, v_g_k_a, v_g_ckv, v_w_kv_b, v_g_out_a, v_g_out_b, v_w_out, v_g_post_attn, v_g_pre_mlp, v_w_mlp_in, v_w_mlp_out, v_g_post_mlp):
    S, D = x.shape[1], x.shape[2]
    x2d, tgt = x[0], loss_target[0]
    ix, iy, ic = lax.axis_index("x"), lax.axis_index("y"), lax.axis_index("c")
    j_me = _chip(ix, iy)
    e_me = 4 * ix + 2 * iy + ic
    T = _pick(S, 256, 8)
    TB = _pick(S, 128, 8)

    where = jnp.stack([ic, j_me]).astype(jnp.int32)
    DFF = w_mlp_out.shape[1] * N_CHIPS

    c_all = _allgather8("gather_c", c.reshape(8, D // 8)).reshape(N_DEV, D)

    def silu_fn(cc):
        return (cc * (1.0 / (1.0 + jnp.exp(-cc))),), ()

    (c_act,), _ = _rowwise("silu_c", silu_fn, [c_all], [], [(D, BF16)], [], N_DEV)
    ca_pad = jnp.concatenate([c_act, jnp.zeros_like(c_act)], axis=0)
    n_mod = w_ada.shape[2]
    b_cols = lax.dynamic_slice(b_ada, (0, j_me * n_mod), (1, n_mod))
    (mod_part,) = _mm_nn("mod_mm", ca_pad, w_ada, [F32], epilogue=lambda acc, b: (acc + b,),
                         extras=(jnp.broadcast_to(b_cols, (16, n_mod)),))
    mod_all = _allgather8("gather_mod", mod_part[:8]).reshape(N_DEV, 8, n_mod)
    mod = jnp.concatenate([lax.dynamic_slice(mod_all, (2 * j, e_me, 0), (1, 1, n_mod))[0] for j in range(N_CHIPS)],
                          axis=1)

    w_in_t = jnp.swapaxes(w_in[0], 0, 1)
    h_qkv, tok_a = _gather_start("qkv", where, [w_in_t, w_kv_b[0]], (mod_all,))
    h_out, tok_b = _gather_start("out", where, [w_out[0]], (tok_a,), (tok_a,))
    h_up, tok_c = _gather_start("up", where, [w_mlp_in[0]], (tok_b,), (tok_b,))
    h_down, tok_d = _gather_start("down", where, [w_mlp_out[0]], (tok_c,), (tok_c,))
    mod = mod + tok_d[0, 0]
    sh_a, sc_a, gt_a, sh_m, sc_m, gt_m = [mod[:, i * D:(i + 1) * D] for i in range(6)]

    tabs_a = _rope_tables(S, HEAD_DIM, 1)
    tabs_b = _rope_tables(S, QK_ROPE, LANE // QK_ROPE)

    def pre_attn_fn(xt, g, sc, sh):
        return (_rn(xt) * g * (1.0 + sc) + sh,), ()

    (h_b,), _ = _rowwise("pre_attn", pre_attn_fn, [x2d], [g_pre_attn, sc_a, sh_a], [(D, BF16)], [], T)
    g_in, g_kvb = _gather_finish(h_qkv, h_b)
    wpt = _permute_w_in(g_in.reshape(W_IN, D))[None]
    (proj,) = _mm_nt("proj_mm", h_b, wpt, [F32])

    def prep_fn(pt, ca_, sla, sha, cb_, slb, shb, gq, gk, gc):
        return _prep(_split_proj(pt), ((ca_, sla, sha), (cb_, slb, shb)), gq, gk, gc), ()

    (qa_b, ka_b, va_b, qcat_b, ckvn_b, kpe_b), _ = _rowwise(
        "prep", prep_fn, [proj, *tabs_a, *tabs_b], [g_q_a, g_k_a, g_ckv],
        [(W_QA, BF16), (W_KA, BF16), (W_KA, BF16), (2 * HB * LANE, BF16), (KV_RANK, BF16), (LANE, BF16)], [], TB)
    (kv,) = _mm_nn("kv_mm", ckvn_b, g_kvb, [F32])

    def kcat_fn(kvt, kpet):
        kc, vs = [], []
        for h in range(HB):
            kc += [kvt[:, 2 * h * LANE:(2 * h + 1) * LANE], kpet.astype(F32)]
            vs.append(kvt[:, (2 * h + 1) * LANE:(2 * h + 2) * LANE])
        return (jnp.concatenate(kc, axis=1), jnp.concatenate(vs, axis=1)), ()

    (kcat_b, vb_b), _ = _rowwise("kcat", kcat_fn, [kv, kpe_b], [], [(2 * HB * LANE, BF16), (HB * V_DIM, BF16)], [], T)

    TQ = _pick(S, 512, 16)
    sc_a_ = 1.0 / math.sqrt(HEAD_DIM)
    sc_b_ = 1.0 / math.sqrt(QK_B)
    o_a, lse_a = _attn_fwd("attn_a_fwd", qa_b, ka_b, va_b, HA, HA // HKV, HEAD_DIM, HEAD_DIM, sc_a_, TQ)
    m_out, tok_m = _gather_mid(h_out, o_a)
    m_up, tok_m = _gather_mid(h_up, tok_m)
    o_b, lse_b = _attn_fwd("attn_b_fwd", qcat_b, kcat_b, vb_b, HB, 1, 2 * LANE, V_DIM, sc_b_, TQ, after=(tok_m,))

    def mix_fn(oa, ob, ga, gb):
        return (jnp.concatenate([_rn(oa) * ga, _rn(ob) * gb], axis=1),), ()

    (on_b,), _ = _rowwise("mix_norm", mix_fn, [o_a, o_b], [g_out_a, g_out_b], [(2 * W_QA, BF16)], [], T)
    (g_wout,) = _gather_end(m_out, on_b)
    w_out_f = g_wout.reshape(1, N_CHIPS * g_wout.shape[1], D)
    (o2,) = _mm_nn("out_mm", on_b, w_out_f, [F32])

    def mid(xt, o2t, gta, gpa, gpm, scm, shm):
        x1 = xt + gta * (_rn(o2t) * gpa)
        return x1, _rn(x1) * gpm * (1.0 + scm) + shm

    def mid_fn(*a):
        return mid(*a), ()

    mid_vecs = [gt_a, g_post_attn, g_pre_mlp, sc_m, sh_m]
    (x1, h2_b), _ = _rowwise("mid", mid_fn, [x2d, o2], mid_vecs, [(D, F32), (D, BF16)], [], T)
    (g_mlp_in,) = _gather_end(m_up, h2_b)
    m_down, tok_m = _gather_mid(h_down, h2_b)
    u, a_b = _mm_nn("mlp_in_mm", h2_b, g_mlp_in, [F32, BF16],
                    epilogue=lambda acc: (acc, jnp.square(jnp.maximum(acc, 0.0))), after=(tok_m,))
    (g_mlp_out,) = _gather_end(m_down, a_b)
    w_mlp_out_f = g_mlp_out.reshape(1, DFF, D)
    (y,) = _mm_nn("mlp_out_mm", a_b, w_mlp_out_f, [F32])

    def last(yt, x1t, gtm, gpo):
        return x1t + gtm * (_rn(yt) * gpo)

    def last_fn(yt, x1t, tt, gtm, gpo):
        x2, vjp = jax.vjp(last, yt, x1t, gtm, gpo)
        err = x2 - tt
        dy, dx1, dgt, dg = vjp(err * (1.0 / D))
        loss = 0.5 * jnp.sum(jnp.mean(err * err, axis=-1))
        return (dy, dx1), (jnp.full((1, LANE), loss, F32), dgt, dg)

    (dy_b, dx2), (loss_v, d_gt_m, d_g_post_mlp) = _rowwise(
        "loss_bwd", last_fn, [y, x1, tgt], [gt_m, g_post_mlp], [(D, BF16), (D, F32)], [LANE, D, D], TB)

    gw_mlp_out = _mm_tn("gw_mlp_out_mm", a_b, dy_b, 1).reshape(N_CHIPS, DFF // N_CHIPS, D)
    b_down, tok = _reduce_begin("down", [gw_mlp_out])
    (du_b,) = _mm_nt("d_mlp_out_mm", dy_b, w_mlp_out_f, [BF16],
                     epilogue=lambda acc, ut: (acc * (2.0 * jnp.maximum(ut, 0.0)),), extras=(u,), after=(tok,))
    r_down, tok = _reduce_start(b_down, where, du_b)
    gw_mlp_in = _mm_tn("gw_mlp_in_mm", h2_b, du_b, N_CHIPS, after=(tok,))
    b_up, tok = _reduce_begin("up", [gw_mlp_in])
    (dh2,) = _mm_nt("d_mlp_in_mm", du_b, g_mlp_in, [F32], after=(tok,))
    r_up, tok_up = _reduce_start(b_up, where, dh2)

    def mid_bwd_fn(xt, o2t, dh2t, dx2t, *vecs):
        _, vjp = jax.vjp(mid, xt, o2t, *vecs)
        dx, do2, dgta, dgpa, dgpm, dscm, dshm = vjp((dx2t, dh2t))
        return (dx, do2), (dgta, dgpa, dgpm, dscm, dshm)

    (dx1, do2_b), (d_gt_a, d_g_post_attn, d_g_pre_mlp, d_sc_m, d_sh_m) = _rowwise(
        "mid_bwd", mid_bwd_fn, [x2d, o2, dh2, dx2], mid_vecs, [(D, F32), (D, BF16)], [D] * 5, TB, after=(tok_up,))

    (d_on,) = _mm_nt("d_out_mm", do2_b, w_out_f, [F32])
    gw_out = _mm_tn("gw_out_mm", on_b, do2_b, 1).reshape(N_CHIPS, w_out.shape[1], D)
    b_out, tok_out = _reduce_begin("out", [gw_out])

    def mix_bwd_fn(oa, ob, dont, ga, gb):
        _, vjp = jax.vjp(lambda a_, b_, g1, g2: mix_fn(a_, b_, g1, g2)[0][0], oa, ob, ga, gb)
        doa, dob, dga, dgb = vjp(dont)
        return (doa, dob), (dga, dgb)

    (doa_b, dob_b), (d_g_out_a, d_g_out_b) = _rowwise(
        "mix_bwd", mix_bwd_fn, [o_a, o_b, d_on], [g_out_a, g_out_b], [(W_QA, BF16), (HB * V_DIM, BF16)],
        [W_QA, HB * V_DIM], T, after=(tok_out,))

    dqa, dka, dva = _attn_bwd("attn_a_bwd", qa_b, ka_b, va_b, doa_b, o_a, lse_a, HA, HA // HKV, HEAD_DIM, HEAD_DIM,
                              sc_a_, TQ)
    r_out, tok = _reduce_start(b_out, where, dqa)
    dqcat, dkcat, dvb = _attn_bwd("attn_b_bwd", qcat_b, kcat_b, vb_b, dob_b, o_b, lse_b, HB, 1, 2 * LANE, V_DIM,
                                  sc_b_, TQ, after=(tok,))

    def kcat_bwd_fn(dkc, dvt):
        dkv, dkpe = [], None
        for h in range(HB):
            dkv += [dkc[:, 2 * h * LANE:(2 * h + 1) * LANE], dvt[:, h * V_DIM:(h + 1) * V_DIM]]
            piece = dkc[:, (2 * h + 1) * LANE:(2 * h + 2) * LANE]
            dkpe = piece if dkpe is None else dkpe + piece
        return (jnp.concatenate(dkv, axis=1), dkpe), ()

    (dkv_b, dkpe), _ = _rowwise("kcat_bwd", kcat_bwd_fn, [dkcat, dvb], [], [(2 * HB * LANE, BF16), (LANE, F32)], [], T)
    (d_ckvn,) = _mm_nt("d_kv_mm", dkv_b, g_kvb, [F32])
    gw_kvb = _mm_tn("gw_kv_mm", ckvn_b, dkv_b, N_CHIPS)
    b_kvb, tok_kvb = _reduce_begin("kvb", [gw_kvb])

    def prep_bwd_fn(pt, dqat, dkat, dvat, dqct, dckt, dkpt, ca_, sla, sha, cb_, slb, shb, gq, gk, gc):
        tabs = ((ca_, sla, sha), (cb_, slb, shb))
        _, vjp = jax.vjp(lambda sg, a1, a2, a3: _prep(sg, tabs, a1, a2, a3), _split_proj(pt), gq, gk, gc)
        (dqa_s, dka_s, dva_s, dqn_s, dqp_s, dck_s, dkp_s), dgq, dgk, dgc = vjp((dqat, dkat, dvat, dqct, dckt, dkpt))
        dproj = jnp.concatenate([*dqa_s, *dka_s, dva_s, *dqn_s, *dqp_s, dck_s, dkp_s], axis=1)
        return (dproj,), (dgq, dgk, dgc)

    (dproj_b,), (d_g_q_a, d_g_k_a, d_g_ckv) = _rowwise(
        "prep_bwd", prep_bwd_fn, [proj, dqa, dka, dva, dqcat, d_ckvn, dkpe, *tabs_a, *tabs_b],
        [g_q_a, g_k_a, g_ckv], [(W_INP, BF16)], [HEAD_DIM, HEAD_DIM, KV_RANK], TB, after=(tok_kvb,))

    r_kvb, tok = _reduce_start(b_kvb, where, dproj_b)
    (dh,) = _mm_nn("d_proj_mm", dproj_b, wpt, [F32], after=(tok,))
    gwpt = _mm_tn("gw_in_mm", dproj_b, h_b, 1)[0]
    gw_in = _unpermute_w_in(gwpt).reshape(N_CHIPS, W_IN // N_CHIPS, D)
    b_in, tok = _reduce_begin("in", [gw_in])

    def pre_attn_bwd_fn(xt, dht, dx1t, g, sc, sh):
        _, vjp = jax.vjp(lambda a_, b_, c_, d_: pre_attn_fn(a_, b_, c_, d_)[0][0], xt, g, sc, sh)
        dxt, dg, dsc, dsh = vjp(dht)
        return (dx1t + dxt,), (dg, dsc, dsh)

    (grad_x,), (d_g_pre_attn, d_sc_a, d_sh_a) = _rowwise(
        "pre_attn_bwd", pre_attn_bwd_fn, [x2d, dh, dx1], [g_pre_attn, sc_a, sh_a], [(D, F32)], [D] * 3, TB,
        after=(tok,))

    small_names = ["b_ada", "g_pre_attn", "g_q_a", "g_k_a", "g_ckv", "g_out_a", "g_out_b", "g_post_attn",
                   "g_pre_mlp", "g_post_mlp"]
    small_local = jnp.concatenate([d_sh_a, d_sc_a, d_gt_a, d_sh_m, d_sc_m, d_gt_m, d_g_pre_attn, d_g_q_a, d_g_k_a,
                                   d_g_ckv, d_g_out_a, d_g_out_b, d_g_post_attn, d_g_pre_mlp, d_g_post_mlp], axis=1)
    n_small = small_local.shape[1]
    small_all = _allgather8("gather_small", small_local.reshape(8, n_small // 8)).reshape(N_DEV, n_small)

    def sum8_fn(t):
        return (), (jnp.sum(t, axis=0, keepdims=True),)

    r_in, tok_in = _reduce_start(b_in, where, grad_x, (small_all,))
    _, (small_sum,) = _rowwise("sum_small", sum8_fn, [small_all], [], [], [n_small], N_DEV, after=(tok_in,))
    dmod_cols = lax.dynamic_slice(small_all, (0, j_me * n_mod), (N_DEV, n_mod))
    dmod_pad = jnp.concatenate([dmod_cols, jnp.zeros_like(dmod_cols)], axis=0).astype(BF16)
    gw_ada = _mm_tn("gw_ada_mm", ca_pad, dmod_pad, 1)[0]

    weights = dict(w_ada=w_ada, b_ada=b_ada, g_pre_attn=g_pre_attn, w_in=w_in, g_q_a=g_q_a, g_k_a=g_k_a, g_ckv=g_ckv,
                   w_kv_b=w_kv_b, g_out_a=g_out_a, g_out_b=g_out_b, w_out=w_out, g_post_attn=g_post_attn,
                   g_pre_mlp=g_pre_mlp, w_mlp_in=w_mlp_in, w_mlp_out=w_mlp_out, g_post_mlp=g_post_mlp)
    ms = dict(w_ada=m_w_ada, b_ada=m_b_ada, g_pre_attn=m_g_pre_attn, w_in=m_w_in, g_q_a=m_g_q_a, g_k_a=m_g_k_a,
              g_ckv=m_g_ckv, w_kv_b=m_w_kv_b, g_out_a=m_g_out_a, g_out_b=m_g_out_b, w_out=m_w_out,
              g_post_attn=m_g_post_attn, g_pre_mlp=m_g_pre_mlp, w_mlp_in=m_w_mlp_in, w_mlp_out=m_w_mlp_out,
              g_post_mlp=m_g_post_mlp)
    vs = dict(w_ada=v_w_ada, b_ada=v_b_ada, g_pre_attn=v_g_pre_attn, w_in=v_w_in, g_q_a=v_g_q_a, g_k_a=v_g_k_a,
              g_ckv=v_g_ckv, w_kv_b=v_w_kv_b, g_out_a=v_g_out_a, g_out_b=v_g_out_b, w_out=v_w_out,
              g_post_attn=v_g_post_attn, g_pre_mlp=v_g_pre_mlp, w_mlp_in=v_w_mlp_in, w_mlp_out=v_w_mlp_out,
              g_post_mlp=v_g_post_mlp)
    order = list(weights)
    grads, deltas, new_m, new_v = {}, {}, {}, {}

    def update(name, g, after=()):
        view = (lambda t: jnp.swapaxes(t, 0, 1)) if name == "w_in" else (lambda t: t)
        d_, m_, v_ = _adamw("adamw_" + name, view(weights[name][0]), g, view(ms[name][0]), view(vs[name][0]), after)
        grads[name], deltas[name], new_m[name], new_v[name] = (view(t)[None] for t in (g, d_, m_, v_))
        return d_

    chain = (("w_mlp_out", r_down), ("w_mlp_in", r_up), ("w_out", r_out), ("w_kv_b", r_kvb), ("w_in", r_in))
    swap, tok = _reduce_finish(chain[0][1], where, small_sum)
    last = update("w_ada", gw_ada, (tok,))
    for k, (name, _) in enumerate(chain):
        (g,) = _reduce_end(swap, last)
        if k + 1 < len(chain):
            swap, tok = _reduce_finish(chain[k + 1][1], where, g)
        last = update(name, g, (tok,))

    cat = lambda d: jnp.concatenate([d[n] for n in small_names], axis=1)
    d_, m_, v_ = _adamw("adamw_small", cat(weights), small_sum, cat(ms), cat(vs))
    off = 0
    for n in small_names:
        w_ = weights[n].shape[1]
        grads[n], deltas[n], new_m[n], new_v[n] = (a[:, off:off + w_] for a in (small_sum, d_, m_, v_))
        off += w_

    loss = lax.psum(loss_v[0, 0], ("x", "y", "c"))
    return (loss, grad_x[None], *[grads[n] for n in order], *[deltas[n] for n in order],
            *[new_m[n] for n in order], *[new_v[n] for n in order])
```

```python
import functools
import math

import numpy as np
import jax
import jax.numpy as jnp
from jax import lax
from jax.experimental import pallas as pl
from jax.experimental.pallas import tpu as pltpu

F32 = jnp.float32
BF16 = jnp.bfloat16
MESH = pl.DeviceIdType.MESH

EPS = 1e-6
GRID_W = 64
ROPE_THETA = 10000.0
HEAD_DIM = 128
HA = 8
HKV = 2
HB = 8
QK_NOPE = 128
QK_ROPE = 64
V_DIM = 128
KV_RANK = 512
QK_B = QK_NOPE + QK_ROPE
W_QA = HA * HEAD_DIM
W_KA = HKV * HEAD_DIM
W_QB = HB * QK_B
W_IN = W_QA + 2 * W_KA + W_QB + KV_RANK + QK_ROPE
N_CHIPS = 4
N_DEV = 8

LANE = 128
O_QA = 0
O_KA = O_QA + W_QA
O_VA = O_KA + W_KA
O_QN = O_VA + W_KA
O_QP = O_QN + HB * QK_NOPE
O_CKV = O_QP + HB * LANE
O_KPE = O_CKV + KV_RANK
W_INP = O_KPE + LANE

ADAM_LR = 0.001
ADAM_B1 = 0.9
ADAM_B2 = 0.999
ADAM_EPS = 1e-08
ADAM_WD = 0.01
ADAM_STEP = 10

VMEM_LIMIT = 56 * 1024 * 1024
HEADS_PER_STEP = 2


def _params(sem):
    return pltpu.CompilerParams(dimension_semantics=sem, vmem_limit_bytes=VMEM_LIMIT)


def _pick(dim, target, unit):
    best = None
    t = unit
    while t <= min(dim, target):
        if dim % t == 0:
            best = t
        t += unit
    return dim if best is None else best


def _rn(x):
    return x * lax.rsqrt(jnp.mean(x * x, axis=-1, keepdims=True) + EPS)


def _rowwise(name, fn, rows, vecs, row_outs, vec_outs, tile, after=()):
    S = rows[0].shape[0]
    assert S % tile == 0
    n_r, n_v, n_ro, n_vo, n_a = len(rows), len(vecs), len(row_outs), len(vec_outs), len(after)

    def body(*refs):
        r_in = refs[:n_r]
        v_in = refs[n_r:n_r + n_v]
        r_out = refs[n_r + n_v + n_a:n_r + n_v + n_a + n_ro]
        v_out = refs[n_r + n_v + n_a + n_ro:]
        ro, vo = fn(*[r[...] for r in r_in], *[v[...] for v in v_in])
        for ref, val in zip(r_out, ro):
            ref[...] = val.astype(ref.dtype)
        if n_vo:
            @pl.when(pl.program_id(0) == 0)
            def _():
                for ref in v_out:
                    ref[...] = jnp.zeros(ref.shape, ref.dtype)
            for ref, val in zip(v_out, vo):
                ref[...] += val

    in_specs = [pl.BlockSpec((tile, a.shape[1]), lambda i: (i, 0)) for a in rows]
    in_specs += [pl.BlockSpec(a.shape, lambda i: (0, 0)) for a in vecs]
    in_specs += [pl.BlockSpec(memory_space=pl.ANY) for _ in after]
    out_specs = [pl.BlockSpec((tile, w), lambda i: (i, 0)) for w, _ in row_outs]
    out_specs += [pl.BlockSpec((1, w), lambda i: (0, 0)) for w in vec_outs]
    out_shape = [jax.ShapeDtypeStruct((S, w), dt) for w, dt in row_outs]
    out_shape += [jax.ShapeDtypeStruct((1, w), F32) for w in vec_outs]
    res = pl.pallas_call(
        body, name=name, grid=(S // tile,), in_specs=in_specs, out_specs=out_specs, out_shape=out_shape,
        compiler_params=_params(("arbitrary",)),
    )(*rows, *vecs, *after)
    return res[:n_ro], res[n_ro:]


CONTRACT_TILE = 2048


def _accumulate(acc, part, step, n_steps, finish):
    if n_steps == 1:
        finish(part)
        return

    @pl.when(step == 0)
    def _():
        acc[...] = part

    @pl.when(step > 0)
    def _():
        acc[...] += part

    @pl.when(step == n_steps - 1)
    def _():
        finish(acc[...])


def _mm_nn(name, a, b3, out_dtypes, epilogue=None, extras=(), after=()):
    M, K = a.shape
    nB, K2, Nsh = b3.shape
    assert K == K2
    N = nB * Nsh
    tm, tn, tk = _pick(M, 1024, 16), _pick(Nsh, 1536, LANE), _pick(K, CONTRACT_TILE, LANE)
    nps, nk, ne, no, na = Nsh // tn, K // tk, len(extras), len(out_dtypes), len(after)

    def body(a_ref, b_ref, *rest):
        e_refs, o_refs = rest[:ne], rest[ne + na:ne + na + no]

        def finish(total):
            vals = (total,) if epilogue is None else epilogue(total, *[e[...] for e in e_refs])
            for ref, val in zip(o_refs, vals):
                ref[...] = val.astype(ref.dtype)

        part = jnp.dot(a_ref[...].astype(BF16), b_ref[...].astype(BF16), preferred_element_type=F32)
        _accumulate(rest[-1], part, pl.program_id(2), nk, finish)

    in_specs = [pl.BlockSpec((tm, tk), lambda i, j, k: (i, k)),
                pl.BlockSpec((None, tk, tn), lambda i, j, k: (j // nps, k, j % nps))]
    in_specs += [pl.BlockSpec((tm, tn), lambda i, j, k: (i, j)) for _ in extras]
    in_specs += [pl.BlockSpec(memory_space=pl.ANY) for _ in after]
    return pl.pallas_call(
        body, name=name, grid=(M // tm, N // tn, nk), in_specs=in_specs,
        out_specs=[pl.BlockSpec((tm, tn), lambda i, j, k: (i, j)) for _ in out_dtypes],
        out_shape=[jax.ShapeDtypeStruct((M, N), dt) for dt in out_dtypes],
        scratch_shapes=[pltpu.VMEM((tm, tn), F32)] if nk > 1 else [],
        compiler_params=_params(("parallel", "parallel", "arbitrary")),
    )(a, b3, *extras, *after)


def _mm_nt(name, a, b3, out_dtypes, epilogue=None, extras=(), after=()):
    M, N = a.shape
    nB, K, Nsh = b3.shape
    assert N == nB * Nsh
    tm, to, tn = _pick(M, 1024, 16), _pick(K, 1536, LANE), _pick(Nsh, CONTRACT_TILE, LANE)
    nps, nn, ne, no, na = Nsh // tn, N // tn, len(extras), len(out_dtypes), len(after)

    def body(a_ref, b_ref, *rest):
        e_refs, o_refs = rest[:ne], rest[ne + na:ne + na + no]

        def finish(total):
            vals = (total,) if epilogue is None else epilogue(total, *[e[...] for e in e_refs])
            for ref, val in zip(o_refs, vals):
                ref[...] = val.astype(ref.dtype)

        part = lax.dot_general(a_ref[...].astype(BF16), b_ref[...].astype(BF16),
                               (((1,), (1,)), ((), ())), preferred_element_type=F32)
        _accumulate(rest[-1], part, pl.program_id(2), nn, finish)

    in_specs = [pl.BlockSpec((tm, tn), lambda i, o, n: (i, n)),
                pl.BlockSpec((None, to, tn), lambda i, o, n: (n // nps, o, n % nps))]
    in_specs += [pl.BlockSpec((tm, to), lambda i, o, n: (i, o)) for _ in extras]
    in_specs += [pl.BlockSpec(memory_space=pl.ANY) for _ in after]
    return pl.pallas_call(
        body, name=name, grid=(M // tm, K // to, nn), in_specs=in_specs,
        out_specs=[pl.BlockSpec((tm, to), lambda i, o, n: (i, o)) for _ in out_dtypes],
        out_shape=[jax.ShapeDtypeStruct((M, K), dt) for dt in out_dtypes],
        scratch_shapes=[pltpu.VMEM((tm, to), F32)] if nn > 1 else [],
        compiler_params=_params(("parallel", "parallel", "arbitrary")),
    )(a, b3, *extras, *after)


def _mm_tn(name, a, g, nB, after=()):
    M, K = a.shape
    M2, N = g.shape
    assert M == M2 and N % nB == 0
    Nsh = N // nB
    tk, tn, tm = _pick(K, 1536, LANE), _pick(Nsh, 1024, LANE), _pick(M, CONTRACT_TILE, 16)
    nps, nm = Nsh // tn, M // tm

    def body(a_ref, g_ref, *rest):
        o_ref = rest[len(after)]

        def finish(total):
            o_ref[...] = total

        part = lax.dot_general(a_ref[...].astype(BF16), g_ref[...].astype(BF16),
                               (((0,), (0,)), ((), ())), preferred_element_type=F32)
        _accumulate(rest[-1], part, pl.program_id(2), nm, finish)

    return pl.pallas_call(
        body, name=name, grid=(K // tk, N // tn, nm),
        in_specs=[pl.BlockSpec((tm, tk), lambda k, n, m: (m, k)),
                  pl.BlockSpec((tm, tn), lambda k, n, m: (m, n))] + [pl.BlockSpec(memory_space=pl.ANY) for _ in after],
        out_specs=pl.BlockSpec((None, tk, tn), lambda k, n, m: (n // nps, k, n % nps)),
        out_shape=jax.ShapeDtypeStruct((nB, K, Nsh), F32),
        scratch_shapes=[pltpu.VMEM((tk, tn), F32)] if nm > 1 else [],
        compiler_params=_params(("parallel", "parallel", "arbitrary")),
    )(a, g, *after)


def _attn_fwd(name, q, k, v, n_heads, group, dqk, dv, scale, tq, after=()):
    S = q.shape[0]
    hp = HEADS_PER_STEP
    kvp = max(1, hp // group)

    def body(q_ref, k_ref, v_ref, *rest):
        o_ref, lse_ref = rest[-2:]
        for j in range(hp):
            jk = j * kvp // hp
            s = lax.dot_general(q_ref[:, j * dqk:(j + 1) * dqk], k_ref[:, jk * dqk:(jk + 1) * dqk],
                                (((1,), (1,)), ((), ())), preferred_element_type=F32)
            m = jnp.max(s, axis=-1, keepdims=True)
            e = jnp.exp((s - m) * scale)
            l = jnp.sum(e, axis=-1, keepdims=True)
            o_ref[:, j * dv:(j + 1) * dv] = jnp.dot(e.astype(BF16), v_ref[:, jk * dv:(jk + 1) * dv],
                                                    preferred_element_type=F32) * (1.0 / l)
            lse_ref[:, j * LANE:(j + 1) * LANE] = jnp.broadcast_to(m * scale + jnp.log(l), (tq, LANE))

    kv_block = lambda h, i: (0, (h * hp // group) // kvp)
    return pl.pallas_call(
        body, name=name, grid=(n_heads // hp, S // tq),
        in_specs=[pl.BlockSpec((tq, hp * dqk), lambda h, i: (i, h)),
                  pl.BlockSpec((S, kvp * dqk), kv_block),
                  pl.BlockSpec((S, kvp * dv), kv_block)] + [pl.BlockSpec(memory_space=pl.ANY) for _ in after],
        out_specs=[pl.BlockSpec((tq, hp * dv), lambda h, i: (i, h)),
                   pl.BlockSpec((tq, hp * LANE), lambda h, i: (i, h))],
        out_shape=[jax.ShapeDtypeStruct((S, n_heads * dv), F32), jax.ShapeDtypeStruct((S, n_heads * LANE), F32)],
        compiler_params=_params(("arbitrary", "arbitrary")),
    )(q, k, v, *after)


def _attn_bwd(name, q, k, v, do, o, lse, n_heads, group, dqk, dv, scale, tq, after=()):
    S = q.shape[0]
    n_kv = n_heads // group
    hp = HEADS_PER_STEP
    kvp = max(1, hp // group)

    def body(q_ref, k_ref, v_ref, do_ref, o_ref, lse_ref, *rest):
        dq_ref, dk_ref, dv_ref = rest[len(after):]
        h, i = pl.program_id(0), pl.program_id(1)

        @pl.when(((h * hp) % group == 0) & (i == 0))
        def _():
            dk_ref[...] = jnp.zeros(dk_ref.shape, F32)
            dv_ref[...] = jnp.zeros(dv_ref.shape, F32)

        for j in range(hp):
            jk = j * kvp // hp
            qb, dob = q_ref[:, j * dqk:(j + 1) * dqk], do_ref[:, j * dv:(j + 1) * dv]
            kb, vb = k_ref[:, jk * dqk:(jk + 1) * dqk], v_ref[:, jk * dv:(jk + 1) * dv]
            s = lax.dot_general(qb, kb, (((1,), (1,)), ((), ())), preferred_element_type=F32)
            p = jnp.exp(s * scale - jnp.tile(lse_ref[:, j * LANE:(j + 1) * LANE], (1, S // LANE)))
            dp = lax.dot_general(dob, vb, (((1,), (1,)), ((), ())), preferred_element_type=F32)
            delta = jnp.sum(dob.astype(F32) * o_ref[:, j * dv:(j + 1) * dv], axis=-1, keepdims=True)
            ds = (p * ((dp - delta) * scale)).astype(BF16)
            dv_ref[:, jk * dv:(jk + 1) * dv] += lax.dot_general(p.astype(BF16), dob, (((0,), (0,)), ((), ())),
                                                                preferred_element_type=F32)
            dq_ref[:, j * dqk:(j + 1) * dqk] = jnp.dot(ds, kb, preferred_element_type=F32)
            dk_ref[:, jk * dqk:(jk + 1) * dqk] += lax.dot_general(ds, qb, (((0,), (0,)), ((), ())),
                                                                  preferred_element_type=F32)

    kv_block = lambda h, i: (0, (h * hp // group) // kvp)
    return pl.pallas_call(
        body, name=name, grid=(n_heads // hp, S // tq),
        in_specs=[pl.BlockSpec((tq, hp * dqk), lambda h, i: (i, h)),
                  pl.BlockSpec((S, kvp * dqk), kv_block),
                  pl.BlockSpec((S, kvp * dv), kv_block),
                  pl.BlockSpec((tq, hp * dv), lambda h, i: (i, h)),
                  pl.BlockSpec((tq, hp * dv), lambda h, i: (i, h)),
                  pl.BlockSpec((tq, hp * LANE), lambda h, i: (i, h))] + [pl.BlockSpec(memory_space=pl.ANY) for _ in after],
        out_specs=[pl.BlockSpec((tq, hp * dqk), lambda h, i: (i, h)),
                   pl.BlockSpec((S, kvp * dqk), kv_block),
                   pl.BlockSpec((S, kvp * dv), kv_block)],
        out_shape=[jax.ShapeDtypeStruct((S, n_heads * dqk), F32),
                   jax.ShapeDtypeStruct((S, n_kv * dqk), F32),
                   jax.ShapeDtypeStruct((S, n_kv * dv), F32)],
        compiler_params=_params(("arbitrary", "arbitrary")),
    )(q, k, v, do, o, lse, *after)


def _adamw(name, w, g, m, v, after=()):
    R, C = w.shape
    tile = _pick(R, max(8, (1 << 19) // C // 8 * 8), 8)
    c1 = 1.0 - ADAM_B1 ** ADAM_STEP
    c2 = 1.0 - ADAM_B2 ** ADAM_STEP

    def body(w_ref, g_ref, m_ref, v_ref, *rest):
        d_ref, nm_ref, nv_ref = rest[len(after):]
        gg = g_ref[...]
        nm = ADAM_B1 * m_ref[...] + (1.0 - ADAM_B1) * gg
        nv = ADAM_B2 * v_ref[...] + (1.0 - ADAM_B2) * (gg * gg)
        d_ref[...] = -ADAM_LR * ((nm / c1) / (jnp.sqrt(nv / c2) + ADAM_EPS) + ADAM_WD * w_ref[...])
        nm_ref[...] = nm
        nv_ref[...] = nv

    spec = pl.BlockSpec((tile, C), lambda i: (i, 0))
    return pl.pallas_call(
        body, name=name, grid=(R // tile,), in_specs=[spec] * 4 + [pl.BlockSpec(memory_space=pl.ANY) for _ in after],
        out_specs=[spec] * 3, out_shape=[jax.ShapeDtypeStruct((R, C), F32)] * 3,
        compiler_params=_params(("parallel",)),
    )(w, g, m, v, *after)


def _flip(me, rel):
    return tuple(1 - p if r else p for p, r in zip(me, rel))


def _push(name, ins, out_shapes, transfers, aliases=None):
    ni, no, nt = len(ins), len(out_shapes), len(transfers)

    def body(*refs):
        in_refs, out_refs = refs[:ni], refs[ni:ni + no]
        send_sems, recv_sems = refs[ni + no], refs[ni + no + 1]
        me = (lax.axis_index("x"), lax.axis_index("y"), lax.axis_index("c"))
        copies = []
        for t, (si, sfn, oi, dfn, rel) in enumerate(transfers):
            src_ref = out_refs[si[1]] if isinstance(si, tuple) else in_refs[si]
            src = src_ref.at[sfn(*me)]
            dst = out_refs[oi].at[dfn(*me)]
            if rel is None:
                cp = pltpu.make_async_copy(src, dst, send_sems.at[t])
            else:
                cp = pltpu.make_async_remote_copy(src_ref=src, dst_ref=dst, send_sem=send_sems.at[t],
                                                  recv_sem=recv_sems.at[t], device_id=_flip(me, rel),
                                                  device_id_type=MESH)
            cp.start()
            copies.append(cp)
        for cp in copies:
            cp.wait()

    any_spec = pl.BlockSpec(memory_space=pl.ANY)
    return pl.pallas_call(
        body, name=name, in_specs=[any_spec] * ni, out_specs=[any_spec] * no, out_shape=out_shapes,
        scratch_shapes=[pltpu.SemaphoreType.DMA((nt,)), pltpu.SemaphoreType.DMA((nt,))],
        input_output_aliases=aliases or {},
        compiler_params=pltpu.CompilerParams(has_side_effects=True),
    )(*ins)


_HBM = pl.BlockSpec(memory_space=pltpu.HBM)
_SEM = pl.BlockSpec(memory_space=pltpu.SEMAPHORE)
_DATAFLOW = pltpu.SideEffectType.DATAFLOW_SIDE_EFFECTING


def _split_copies(refs, transfers, send_sems, recv_sems):
    me = (lax.axis_index("x"), lax.axis_index("y"), lax.axis_index("c"))
    return [pltpu.make_async_remote_copy(src_ref=refs[sb].at[sfn(*me)], dst_ref=refs[db].at[dfn(*me)],
                                         send_sem=send_sems.at[t], recv_sem=recv_sems.at[t],
                                         device_id=_flip(me, rel), device_id_type=MESH)
            for t, (sb, sfn, db, dfn, rel) in enumerate(transfers)]


def _push_start(name, bufs, transfers, after=()):
    nb, na, nt = len(bufs), len(after), len(transfers)

    def body(*refs):
        send_sems, recv_sems, token = refs[nb + na], refs[nb + na + 1], refs[-1]
        for cp in _split_copies(refs[nb + na + 2:2 * nb + na + 2], transfers, send_sems, recv_sems):
            cp.start()
        token[...] = jnp.zeros(token.shape, token.dtype)

    res = pl.pallas_call(
        body, name=name,
        out_shape=(pltpu.SemaphoreType.DMA((nt,)), pltpu.SemaphoreType.DMA((nt,)),
                   *[pltpu.HBM(b.shape, b.dtype) for b in bufs], jax.ShapeDtypeStruct((8, LANE), F32)),
        in_specs=[_HBM] * nb + [pl.BlockSpec(memory_space=pl.ANY)] * na,
        out_specs=(_SEM, _SEM, *[_HBM] * nb, pl.BlockSpec(memory_space=pltpu.VMEM)),
        input_output_aliases={i: i + 2 for i in range(nb)},
        compiler_params=pltpu.CompilerParams(has_side_effects=_DATAFLOW),
    )(*[pltpu.with_memory_space_constraint(b, pltpu.HBM) for b in bufs], *after)
    return res[0], res[1], list(res[2:2 + nb]), res[-1]


def _push_wait(name, send_sems, recv_sems, bufs, transfers, after):
    nb = len(bufs)

    def body(*refs):
        for cp in _split_copies(refs[:nb], transfers, refs[nb], refs[nb + 1]):
            cp.wait_send()
            cp.wait_recv()

    res = pl.pallas_call(
        body, name=name, out_shape=[pltpu.HBM(b.shape, b.dtype) for b in bufs],
        in_specs=[_HBM] * nb + [_SEM, _SEM, pl.BlockSpec(memory_space=pl.ANY)], out_specs=[_HBM] * nb,
        input_output_aliases={i: i for i in range(nb)},
        compiler_params=pltpu.CompilerParams(has_side_effects=_DATAFLOW),
    )(*bufs, send_sems, recv_sems, after)
    return list(res)


ICI_RELS = ((1, 0, 0), (0, 1, 0), (1, 1, 0))
SIBLING = (0, 0, 1)


def _chip(x, y):
    return 2 * x + y


def _cast_into_slot(name, where, w, after=()):
    R, C = w.shape
    tile = _pick(R, 512, BF16_ROWS)

    def body(where_ref, w_ref, *rest):
        rest[-1][...] = w_ref[...].astype(BF16)

    return pl.pallas_call(
        body, name=name,
        grid_spec=pltpu.PrefetchScalarGridSpec(
            num_scalar_prefetch=1, grid=(R // tile,),
            in_specs=[pl.BlockSpec((tile, C), lambda i, wh: (i, 0))] + [pl.BlockSpec(memory_space=pl.ANY) for _ in after],
            out_specs=pl.BlockSpec((None, tile, C), lambda i, wh: (wh[1], i, 0))),
        out_shape=jax.ShapeDtypeStruct((N_CHIPS, R, C), BF16),
        compiler_params=_params(("parallel",)),
    )(where, w, *after)


BF16_ROWS = 16


def _rows_split(shape2):
    return (shape2[0] // 2) % BF16_ROWS == 0


def _half_shape(shape2):
    R, C = shape2
    return (R // 2, C) if _rows_split(shape2) else (R, C // 2)


def _half(shape2, c):
    R, C = shape2
    if _rows_split(shape2):
        return (pl.ds(c * (R // 2), R // 2), slice(None))
    return (slice(None), pl.ds(pl.multiple_of(c * (C // 2), LANE), C // 2))


def _gather_start(tag, where, shards, after, cast_after=()):
    n = len(shards)
    own = [_cast_into_slot(f"cast_w_{tag}{w}", where, shards[w], cast_after) for w in range(n)]
    transfers = []
    for w in range(n):
        def idx(x, y, c, shape2=shards[w].shape):
            return (_chip(x, y), *_half(shape2, c))
        transfers += [(w, idx, w, idx, rel) for rel in ICI_RELS]
    send_sems, recv_sems, bufs, token = _push_start(f"gather_{tag}_start", own, transfers, after)
    return (tag, send_sems, recv_sems, bufs, transfers), token


def _forward_transfers(part):
    second = []
    for w, p in enumerate(part):
        for rel in ICI_RELS:
            def idx(x, y, c, shape2=p.shape[1:], rel=rel):
                return (_chip(x ^ rel[0], y ^ rel[1]), *_half(shape2, c))
            second.append((w, idx, w, idx, SIBLING))
    return second


def _gather_finish(handle, after):
    tag, send_sems, recv_sems, bufs, transfers = handle
    part = _push_wait(f"gather_{tag}_wait", send_sems, recv_sems, bufs, transfers, after)
    outs = [jax.ShapeDtypeStruct(p.shape, p.dtype) for p in part]
    second = [(("out", sb), sfn, db, dfn, rel) for sb, sfn, db, dfn, rel in _forward_transfers(part)]
    return _push(f"gather_{tag}_d2d", list(part), outs, second, aliases={w: w for w in range(len(part))})


def _gather_mid(handle, after):
    tag, send_sems, recv_sems, bufs, transfers = handle
    part = _push_wait(f"gather_{tag}_wait", send_sems, recv_sems, bufs, transfers, after)
    second = _forward_transfers(part)
    s2, r2, bufs2, token = _push_start(f"gather_{tag}_d2d_start", part, second)
    return (tag, s2, r2, bufs2, second), token


def _gather_end(handle, after):
    tag, send_sems, recv_sems, bufs, transfers = handle
    return _push_wait(f"gather_{tag}_d2d_wait", send_sems, recv_sems, bufs, transfers, after)


def _reduce_begin(tag, grads, after=()):
    n = len(grads)
    land = [lax.empty((N_CHIPS, *_half_shape(g.shape[1:])), F32) for g in grads]
    t1 = []
    for w in range(n):
        t1.append((w, lambda x, y, c, shape2=grads[w].shape[1:]: (slice(None), *_half(shape2, 1 - c)), n + w,
                   lambda x, y, c: (slice(None), slice(None), slice(None)), SIBLING))
    send_sems, recv_sems, bufs, token = _push_start(f"reduce_{tag}_d2d_start", list(grads) + land, t1, after)
    return (tag, send_sems, recv_sems, bufs, t1), token


def _reduce_start(handle, where, after, after_start=()):
    tag, send_sems, recv_sems, bufs, t1 = handle
    n = len(bufs) // 2
    res = _push_wait(f"reduce_{tag}_d2d_wait", send_sems, recv_sems, bufs, t1, after)
    grads, from_sib = res[:n], res[n:]
    pair = [_pair_sum(f"pair_sum_{tag}{w}", where, grads[w], from_sib[w]) for w in range(n)]
    land = [lax.empty((3, *_half_shape(g.shape[1:])), BF16) for g in grads]
    t3 = []
    for w in range(n):
        for r, rel in enumerate(ICI_RELS):
            t3.append((w, lambda x, y, c, rel=rel: (_chip(x ^ rel[0], y ^ rel[1]), slice(None), slice(None)), n + w,
                       lambda x, y, c, r=r: (r, slice(None), slice(None)), rel))
    send_sems, recv_sems, bufs, token = _push_start(f"reduce_{tag}_start", pair + land, t3, after_start)
    return (tag, send_sems, recv_sems, bufs, t3, list(grads), list(from_sib)), token


def _reduce_finish(handle, where, after):
    tag, send_sems, recv_sems, bufs, t3, grads, from_sib = handle
    n = len(grads)
    from_chips = _push_wait(f"reduce_{tag}_wait", send_sems, recv_sems, bufs, t3, after)[n:]
    mine = [_chip_sum(f"chip_sum_{tag}{w}", where, grads[w], from_sib[w], from_chips[w]) for w in range(n)]
    t5 = []
    for w in range(n):
        def idx(x, y, c, shape2=grads[w].shape[1:]):
            return _half(shape2, c)
        t5.append((w, idx, w, idx, SIBLING))
    send_sems, recv_sems, bufs, token = _push_start(f"reduce_{tag}_swap_start", mine, t5)
    return (tag, send_sems, recv_sems, bufs, t5), token


def _reduce_end(handle, after):
    tag, send_sems, recv_sems, bufs, t5 = handle
    return _push_wait(f"reduce_{tag}_swap_wait", send_sems, recv_sems, bufs, t5, after)


def _pair_sum(name, where, g, from_sib):
    _, hr, hc = from_sib.shape
    tile = _pick(hr, max(16, (1 << 19) // hc // 16 * 16), 16)
    nb = hr // tile
    slot = lambda k, wh: (wh[1] + 1 + k) % N_CHIPS
    if _rows_split(g.shape[1:]):
        mine = lambda k, i, wh: (slot(k, wh), wh[0] * nb + i, 0)
    else:
        mine = lambda k, i, wh: (slot(k, wh), i, wh[0])

    def body(where_ref, g_ref, s_ref, o_ref):
        o_ref[...] = (g_ref[...] + s_ref[...]).astype(BF16)

    return pl.pallas_call(
        body, name=name,
        grid_spec=pltpu.PrefetchScalarGridSpec(
            num_scalar_prefetch=1, grid=(N_CHIPS - 1, nb),
            in_specs=[pl.BlockSpec((None, tile, hc), mine),
                      pl.BlockSpec((None, tile, hc), lambda k, i, wh: (slot(k, wh), i, 0))],
            out_specs=pl.BlockSpec((None, tile, hc), lambda k, i, wh: (slot(k, wh), i, 0))),
        out_shape=jax.ShapeDtypeStruct((N_CHIPS, hr, hc), BF16),
        compiler_params=_params(("parallel", "parallel")),
    )(where, g, from_sib)


def _chip_sum(name, where, g, from_sib, from_chips):
    _, R, C = g.shape
    _, hr, hc = from_sib.shape
    tile = _pick(hr, max(16, (1 << 19) // hc // 16 * 16), 16)
    nb = hr // tile
    if _rows_split((R, C)):
        mine = lambda i, wh: (wh[0] * nb + i, 0)
    else:
        mine = lambda i, wh: (i, wh[0])

    def body(where_ref, g_ref, s_ref, r_ref, o_ref):
        acc = g_ref[...] + s_ref[...]
        for r in range(3):
            acc = acc + r_ref[r].astype(F32)
        o_ref[...] = acc

    return pl.pallas_call(
        body, name=name,
        grid_spec=pltpu.PrefetchScalarGridSpec(
            num_scalar_prefetch=1, grid=(nb,),
            in_specs=[pl.BlockSpec((None, tile, hc), lambda i, wh: (wh[1], *mine(i, wh))),
                      pl.BlockSpec((None, tile, hc), lambda i, wh: (wh[1], i, 0)),
                      pl.BlockSpec((3, tile, hc), lambda i, wh: (0, i, 0))],
            out_specs=pl.BlockSpec((tile, hc), mine)),
        out_shape=jax.ShapeDtypeStruct((R, C), F32),
        compiler_params=_params(("parallel",)),
    )(where, g, from_sib, from_chips)


def _allgather8(name, blk):
    m_per, n = blk.shape
    rels = [(0, 0, 1), (1, 0, 0), (0, 1, 0), (1, 1, 0), (1, 0, 1), (0, 1, 1), (1, 1, 1)]

    def body(x_ref, out_ref, send_sems, recv_sems):
        me = (lax.axis_index("x"), lax.axis_index("y"), lax.axis_index("c"))
        my_rows = out_ref.at[pl.ds((4 * me[0] + 2 * me[1] + me[2]) * m_per, m_per), :]
        out_ref[pl.ds((4 * me[0] + 2 * me[1] + me[2]) * m_per, m_per), :] = x_ref[...]
        copies = []
        for t, rel in enumerate(rels):
            cp = pltpu.make_async_remote_copy(src_ref=x_ref, dst_ref=my_rows, send_sem=send_sems.at[t],
                                              recv_sem=recv_sems.at[t], device_id=_flip(me, rel),
                                              device_id_type=MESH)
            cp.start()
            copies.append(cp)
        for cp in copies:
            cp.wait()

    return pl.pallas_call(
        body, name=name, out_shape=jax.ShapeDtypeStruct((N_DEV * m_per, n), blk.dtype),
        in_specs=[pl.BlockSpec(memory_space=pltpu.VMEM)], out_specs=pl.BlockSpec(memory_space=pltpu.VMEM),
        scratch_shapes=[pltpu.SemaphoreType.DMA((7,)), pltpu.SemaphoreType.DMA((7,))],
        compiler_params=pltpu.CompilerParams(has_side_effects=True, vmem_limit_bytes=VMEM_LIMIT),
    )(blk)


def _rope_tables(seq_len, dim, reps):
    rows = seq_len // GRID_W
    t = np.arange(seq_len)
    row, col = (t // GRID_W).astype(np.float32), (t % GRID_W).astype(np.float32)
    half = dim // 2
    inv = jnp.asarray(ROPE_THETA, F32) ** (-jnp.arange(0, half, 2, dtype=F32) / half)
    ang_r = jnp.asarray(row)[:, None] * inv[None, :]
    ang_c = jnp.asarray(col)[:, None] * inv[None, :]
    ang = jnp.concatenate([ang_r, ang_r, ang_c, ang_c], axis=-1)
    cos, sin = jnp.cos(ang), jnp.sin(ang)
    low = (np.arange(dim) % (dim // 2)) < (dim // 4)
    s_lo = jnp.where(jnp.asarray(low)[None, :], -sin, 0.0)
    s_hi = jnp.where(jnp.asarray(low)[None, :], 0.0, sin)
    del rows
    return tuple(jnp.tile(a, (1, reps)) for a in (cos, s_lo, s_hi))


def _make_rope(shift):
    def up(v):
        return pltpu.roll(v, LANE - shift, 1)

    def down(v):
        return pltpu.roll(v, shift, 1)

    @jax.custom_vjp
    def rope(v, cos, s_lo, s_hi):
        return v * cos + up(v) * s_lo + down(v) * s_hi

    def fwd(v, cos, s_lo, s_hi):
        return rope(v, cos, s_lo, s_hi), (cos, s_lo, s_hi)

    def bwd(res, dy):
        cos, s_lo, s_hi = res
        return dy * cos + down(dy * s_lo) + up(dy * s_hi), jnp.zeros_like(cos), jnp.zeros_like(cos), jnp.zeros_like(cos)

    rope.defvjp(fwd, bwd)
    return rope


_rope_a = _make_rope(HEAD_DIM // 4)
_rope_b = _make_rope(QK_ROPE // 4)


def _prep(segs, tabs, g_q, g_k, g_ckv):
    qa, ka, va, qn, qp, ckv, kpe = segs
    ta, tb = tabs
    q_a = jnp.concatenate([_rope_a(_rn(s) * g_q, *ta) for s in qa], axis=1)
    k_a = jnp.concatenate([_rope_a(_rn(s) * g_k, *ta) for s in ka], axis=1)
    cat = []
    for h in range(HB):
        cat += [qn[h], _rope_b(qp[h], *tb)]
    q_cat = jnp.concatenate(cat, axis=1)
    return q_a, k_a, va, q_cat, _rn(ckv) * g_ckv, _rope_b(kpe, *tb)


def _split_proj(proj):
    qa = [proj[:, O_QA + h * LANE:O_QA + (h + 1) * LANE] for h in range(HA)]
    ka = [proj[:, O_KA + h * LANE:O_KA + (h + 1) * LANE] for h in range(HKV)]
    va = proj[:, O_VA:O_QN]
    qn = [proj[:, O_QN + h * LANE:O_QN + (h + 1) * LANE] for h in range(HB)]
    qp = [proj[:, O_QP + h * LANE:O_QP + (h + 1) * LANE] for h in range(HB)]
    return qa, ka, va, qn, qp, proj[:, O_CKV:O_KPE], proj[:, O_KPE:W_INP]


def _permute_w_in(nat):
    K = nat.shape[1]
    qb = nat[W_QA + 2 * W_KA:W_QA + 2 * W_KA + W_QB].reshape(HB, QK_B, K)
    qp = jnp.concatenate([qb[:, QK_NOPE:], jnp.zeros((HB, LANE - QK_ROPE, K), nat.dtype)], axis=1)
    tail = nat[W_QA + 2 * W_KA + W_QB:]
    return jnp.concatenate([nat[:O_QN], qb[:, :QK_NOPE].reshape(HB * QK_NOPE, K), qp.reshape(HB * LANE, K),
                            tail, jnp.zeros((LANE - QK_ROPE, K), nat.dtype)], axis=0)


def _unpermute_w_in(p):
    K = p.shape[1]
    qn = p[O_QN:O_QP].reshape(HB, QK_NOPE, K)
    qp = p[O_QP:O_CKV].reshape(HB, LANE, K)[:, :QK_ROPE]
    qb = jnp.concatenate([qn, qp], axis=1).reshape(W_QB, K)
    return jnp.concatenate([p[:O_QN], qb, p[O_CKV:O_KPE + QK_ROPE]], axis=0)


def kernel(x, c, w_ada, b_ada, g_pre_attn, w_in, g_q_a, g_k_a, g_ckv, w_kv_b, g_out_a, g_out_b, w_out, g_post_attn, g_pre_mlp, w_mlp_in, w_mlp_out, g_post_mlp, loss_target, m_w_ada, m_b_ada, m_g_pre_attn, m_w_in, m_g_q_a, m_g_k_a, m_g_ckv, m_w_kv_b, m_g_out_a, m_g_out_b, m_w_out, m_g_post_attn, m_g_pre_mlp, m_w_mlp_in, m_w_mlp_out, m_g_post_mlp, v_w_ada, v_b_ada, v_g_pre_attn, v_w_in, v_g_q_a, v_g_k_a, v_g_ckv, v_w_kv_b, v_g_out_a, v_g_out_b, v_w_out, v_g_post_attn, v_g_pre_mlp, v_w_mlp_in, v_w_mlp_out, v_g_post_mlp):
    S, D = x.shape[1], x.shape[2]
    x2d, tgt = x[0], loss_target[0]
    ix, iy, ic = lax.axis_index("x"), lax.axis_index("y"), lax.axis_index("c")
    j_me = _chip(ix, iy)
    e_me = 4 * ix + 2 * iy + ic
    T = _pick(S, 256, 8)
    TB = _pick(S, 128, 8)

    where = jnp.stack([ic, j_me]).astype(jnp.int32)
    DFF = w_mlp_out.shape[1] * N_CHIPS

    c_all = _allgather8("gather_c", c.reshape(8, D // 8)).reshape(N_DEV, D)

    def silu_fn(cc):
        return (cc * (1.0 / (1.0 + jnp.exp(-cc))),), ()

    (c_act,), _ = _rowwise("silu_c", silu_fn, [c_all], [], [(D, BF16)], [], N_DEV)
    ca_pad = jnp.concatenate([c_act, jnp.zeros_like(c_act)], axis=0)
    n_mod = w_ada.shape[2]
    b_cols = lax.dynamic_slice(b_ada, (0, j_me * n_mod), (1, n_mod))
    (mod_part,) = _mm_nn("mod_mm", ca_pad, w_ada, [F32], epilogue=lambda acc, b: (acc + b,),
                         extras=(jnp.broadcast_to(b_cols, (16, n_mod)),))
    mod_all = _allgather8("gather_mod", mod_part[:8]).reshape(N_DEV, 8, n_mod)
    mod = jnp.concatenate([lax.dynamic_slice(mod_all, (2 * j, e_me, 0), (1, 1, n_mod))[0] for j in range(N_CHIPS)],
                          axis=1)

    w_in_t = jnp.swapaxes(w_in[0], 0, 1)
    h_qkv, tok_a = _gather_start("qkv", where, [w_in_t, w_kv_b[0]], (mod_all,))
    h_out, tok_b = _gather_start("out", where, [w_out[0]], (tok_a,), (tok_a,))
    h_up, tok_c = _gather_start("up", where, [w_mlp_in[0]], (tok_b,), (tok_b,))
    h_down, tok_d = _gather_start("down", where, [w_mlp_out[0]], (tok_c,), (tok_c,))
    mod = mod + tok_d[0, 0]
    sh_a, sc_a, gt_a, sh_m, sc_m, gt_m = [mod[:, i * D:(i + 1) * D] for i in range(6)]

    tabs_a = _rope_tables(S, HEAD_DIM, 1)
    tabs_b = _rope_tables(S, QK_ROPE, LANE // QK_ROPE)

    def pre_attn_fn(xt, g, sc, sh):
        return (_rn(xt) * g * (1.0 + sc) + sh,), ()

    (h_b,), _ = _rowwise("pre_attn", pre_attn_fn, [x2d], [g_pre_attn, sc_a, sh_a], [(D, BF16)], [], T)
    g_in, g_kvb = _gather_finish(h_qkv, h_b)
    wpt = _permute_w_in(g_in.reshape(W_IN, D))[None]
    (proj,) = _mm_nt("proj_mm", h_b, wpt, [F32])

    def prep_fn(pt, ca_, sla, sha, cb_, slb, shb, gq, gk, gc):
        return _prep(_split_proj(pt), ((ca_, sla, sha), (cb_, slb, shb)), gq, gk, gc), ()

    (qa_b, ka_b, va_b, qcat_b, ckvn_b, kpe_b), _ = _rowwise(
        "prep", prep_fn, [proj, *tabs_a, *tabs_b], [g_q_a, g_k_a, g_ckv],
        [(W_QA, BF16), (W_KA, BF16), (W_KA, BF16), (2 * HB * LANE, BF16), (KV_RANK, BF16), (LANE, BF16)], [], TB)
    (kv,) = _mm_nn("kv_mm", ckvn_b, g_kvb, [F32])

    def kcat_fn(kvt, kpet):
        kc, vs = [], []
        for h in range(HB):
            kc += [kvt[:, 2 * h * LANE:(2 * h + 1) * LANE], kpet.astype(F32)]
            vs.append(kvt[:, (2 * h + 1) * LANE:(2 * h + 2) * LANE])
        return (jnp.concatenate(kc, axis=1), jnp.concatenate(vs, axis=1)), ()

    (kcat_b, vb_b), _ = _rowwise("kcat", kcat_fn, [kv, kpe_b], [], [(2 * HB * LANE, BF16), (HB * V_DIM, BF16)], [], T)

    TQ = _pick(S, 512, 16)
    sc_a_ = 1.0 / math.sqrt(HEAD_DIM)
    sc_b_ = 1.0 / math.sqrt(QK_B)
    o_a, lse_a = _attn_fwd("attn_a_fwd", qa_b, ka_b, va_b, HA, HA // HKV, HEAD_DIM, HEAD_DIM, sc_a_, TQ)
    m_out, tok_m = _gather_mid(h_out, o_a)
    m_up, tok_m = _gather_mid(h_up, tok_m)
    o_b, lse_b = _attn_fwd("attn_b_fwd", qcat_b, kcat_b, vb_b, HB, 1, 2 * LANE, V_DIM, sc_b_, TQ, after=(tok_m,))

    def mix_fn(oa, ob, ga, gb):
        return (jnp.concatenate([_rn(oa) * ga, _rn(ob) * gb], axis=1),), ()

    (on_b,), _ = _rowwise("mix_norm", mix_fn, [o_a, o_b], [g_out_a, g_out_b], [(2 * W_QA, BF16)], [], T)
    (g_wout,) = _gather_end(m_out, on_b)
    w_out_f = g_wout.reshape(1, N_CHIPS * g_wout.shape[1], D)
    (o2,) = _mm_nn("out_mm", on_b, w_out_f, [F32])

    def mid(xt, o2t, gta, gpa, gpm, scm, shm):
        x1 = xt + gta * (_rn(o2t) * gpa)
        return x1, _rn(x1) * gpm * (1.0 + scm) + shm

    def mid_fn(*a):
        return mid(*a), ()

    mid_vecs = [gt_a, g_post_attn, g_pre_mlp, sc_m, sh_m]
    (x1, h2_b), _ = _rowwise("mid", mid_fn, [x2d, o2], mid_vecs, [(D, F32), (D, BF16)], [], T)
    (g_mlp_in,) = _gather_end(m_up, h2_b)
    m_down, tok_m = _gather_mid(h_down, h2_b)
    u, a_b = _mm_nn("mlp_in_mm", h2_b, g_mlp_in, [F32, BF16],
                    epilogue=lambda acc: (acc, jnp.square(jnp.maximum(acc, 0.0))), after=(tok_m,))
    (g_mlp_out,) = _gather_end(m_down, a_b)
    w_mlp_out_f = g_mlp_out.reshape(1, DFF, D)
    (y,) = _mm_nn("mlp_out_mm", a_b, w_mlp_out_f, [F32])

    def last(yt, x1t, gtm, gpo):
        return x1t + gtm * (_rn(yt) * gpo)

    def last_fn(yt, x1t, tt, gtm, gpo):
        x2, vjp = jax.vjp(last, yt, x1t, gtm, gpo)
        err = x2 - tt
        dy, dx1, dgt, dg = vjp(err * (1.0 / D))
        loss = 0.5 * jnp.sum(jnp.mean(err * err, axis=-1))
        return (dy, dx1), (jnp.full((1, LANE), loss, F32), dgt, dg)

    (dy_b, dx2), (loss_v, d_gt_m, d_g_post_mlp) = _rowwise(
        "loss_bwd", last_fn, [y, x1, tgt], [gt_m, g_post_mlp], [(D, BF16), (D, F32)], [LANE, D, D], TB)

    gw_mlp_out = _mm_tn("gw_mlp_out_mm", a_b, dy_b, 1).reshape(N_CHIPS, DFF // N_CHIPS, D)
    b_down, tok = _reduce_begin("down", [gw_mlp_out])
    (du_b,) = _mm_nt("d_mlp_out_mm", dy_b, w_mlp_out_f, [BF16],
                     epilogue=lambda acc, ut: (acc * (2.0 * jnp.maximum(ut, 0.0)),), extras=(u,), after=(tok,))
    r_down, tok = _reduce_start(b_down, where, du_b)
    gw_mlp_in = _mm_tn("gw_mlp_in_mm", h2_b, du_b, N_CHIPS, after=(tok,))
    b_up, tok = _reduce_begin("up", [gw_mlp_in])
    (dh2,) = _mm_nt("d_mlp_in_mm", du_b, g_mlp_in, [F32], after=(tok,))
    r_up, tok_up = _reduce_start(b_up, where, dh2)

    def mid_bwd_fn(xt, o2t, dh2t, dx2t, *vecs):
        _, vjp = jax.vjp(mid, xt, o2t, *vecs)
        dx, do2, dgta, dgpa, dgpm, dscm, dshm = vjp((dx2t, dh2t))
        return (dx, do2), (dgta, dgpa, dgpm, dscm, dshm)

    (dx1, do2_b), (d_gt_a, d_g_post_attn, d_g_pre_mlp, d_sc_m, d_sh_m) = _rowwise(
        "mid_bwd", mid_bwd_fn, [x2d, o2, dh2, dx2], mid_vecs, [(D, F32), (D, BF16)], [D] * 5, TB, after=(tok_up,))

    (d_on,) = _mm_nt("d_out_mm", do2_b, w_out_f, [F32])
    gw_out = _mm_tn("gw_out_mm", on_b, do2_b, 1).reshape(N_CHIPS, w_out.shape[1], D)
    b_out, tok_out = _reduce_begin("out", [gw_out])

    def mix_bwd_fn(oa, ob, dont, ga, gb):
        _, vjp = jax.vjp(lambda a_, b_, g1, g2: mix_fn(a_, b_, g1, g2)[0][0], oa, ob, ga, gb)
        doa, dob, dga, dgb = vjp(dont)
        return (doa, dob), (dga, dgb)

    (doa_b, dob_b), (d_g_out_a, d_g_out_b) = _rowwise(
        "mix_bwd", mix_bwd_fn, [o_a, o_b, d_on], [g_out_a, g_out_b], [(W_QA, BF16), (HB * V_DIM, BF16)],
        [W_QA, HB * V_DIM], T, after=(tok_out,))

    dqa, dka, dva = _attn_bwd("attn_a_bwd", qa_b, ka_b, va_b, doa_b, o_a, lse_a, HA, HA // HKV, HEAD_DIM, HEAD_DIM,
                              sc_a_, TQ)
    r_out, tok = _reduce_start(b_out, where, dqa)
    dqcat, dkcat, dvb = _attn_bwd("attn_b_bwd", qcat_b, kcat_b, vb_b, dob_b, o_b, lse_b, HB, 1, 2 * LANE, V_DIM,
                                  sc_b_, TQ, after=(tok,))

    def kcat_bwd_fn(dkc, dvt):
        dkv, dkpe = [], None
        for h in range(HB):
            dkv += [dkc[:, 2 * h * LANE:(2 * h + 1) * LANE], dvt[:, h * V_DIM:(h + 1) * V_DIM]]
            piece = dkc[:, (2 * h + 1) * LANE:(2 * h + 2) * LANE]
            dkpe = piece if dkpe is None else dkpe + piece
        return (jnp.concatenate(dkv, axis=1), dkpe), ()

    (dkv_b, dkpe), _ = _rowwise("kcat_bwd", kcat_bwd_fn, [dkcat, dvb], [], [(2 * HB * LANE, BF16), (LANE, F32)], [], T)
    (d_ckvn,) = _mm_nt("d_kv_mm", dkv_b, g_kvb, [F32])
    gw_kvb = _mm_tn("gw_kv_mm", ckvn_b, dkv_b, N_CHIPS)
    b_kvb, tok_kvb = _reduce_begin("kvb", [gw_kvb])

    def prep_bwd_fn(pt, dqat, dkat, dvat, dqct, dckt, dkpt, ca_, sla, sha, cb_, slb, shb, gq, gk, gc):
        tabs = ((ca_, sla, sha), (cb_, slb, shb))
        _, vjp = jax.vjp(lambda sg, a1, a2, a3: _prep(sg, tabs, a1, a2, a3), _split_proj(pt), gq, gk, gc)
        (dqa_s, dka_s, dva_s, dqn_s, dqp_s, dck_s, dkp_s), dgq, dgk, dgc = vjp((dqat, dkat, dvat, dqct, dckt, dkpt))
        dproj = jnp.concatenate([*dqa_s, *dka_s, dva_s, *dqn_s, *dqp_s, dck_s, dkp_s], axis=1)
        return (dproj,), (dgq, dgk, dgc)

    (dproj_b,), (d_g_q_a, d_g_k_a, d_g_ckv) = _rowwise(
        "prep_bwd", prep_bwd_fn, [proj, dqa, dka, dva, dqcat, d_ckvn, dkpe, *tabs_a, *tabs_b],
        [g_q_a, g_k_a, g_ckv], [(W_INP, BF16)], [HEAD_DIM, HEAD_DIM, KV_RANK], TB, after=(tok_kvb,))

    r_kvb, tok = _reduce_start(b_kvb, where, dproj_b)
    (dh,) = _mm_nn("d_proj_mm", dproj_b, wpt, [F32], after=(tok,))
    gwpt = _mm_tn("gw_in_mm", dproj_b, h_b, 1)[0]
    gw_in = _unpermute_w_in(gwpt).reshape(N_CHIPS, W_IN // N_CHIPS, D)
    b_in, tok = _reduce_begin("in", [gw_in])

    def pre_attn_bwd_fn(xt, dht, dx1t, g, sc, sh):
        _, vjp = jax.vjp(lambda a_, b_, c_, d_: pre_attn_fn(a_, b_, c_, d_)[0][0], xt, g, sc, sh)
        dxt, dg, dsc, dsh = vjp(dht)
        return (dx1t + dxt,), (dg, dsc, dsh)

    (grad_x,), (d_g_pre_attn, d_sc_a, d_sh_a) = _rowwise(
        "pre_attn_bwd", pre_attn_bwd_fn, [x2d, dh, dx1], [g_pre_attn, sc_a, sh_a], [(D, F32)], [D] * 3, TB,
        after=(tok,))

    small_names = ["b_ada", "g_pre_attn", "g_q_a", "g_k_a", "g_ckv", "g_out_a", "g_out_b", "g_post_attn",
                   "g_pre_mlp", "g_post_mlp"]
    small_local = jnp.concatenate([d_sh_a, d_sc_a, d_gt_a, d_sh_m, d_sc_m, d_gt_m, d_g_pre_attn, d_g_q_a, d_g_k_a,
                                   d_g_ckv, d_g_out_a, d_g_out_b, d_g_post_attn, d_g_pre_mlp, d_g_post_mlp], axis=1)
    n_small = small_local.shape[1]
    small_all = _allgather8("gather_small", small_local.reshape(8, n_small // 8)).reshape(N_DEV, n_small)

    def sum8_fn(t):
        return (), (jnp.sum(t, axis=0, keepdims=True),)

    r_in, tok_in = _reduce_start(b_in, where, grad_x, (small_all,))
    _, (small_sum,) = _rowwise("sum_small", sum8_fn, [small_all], [], [], [n_small], N_DEV, after=(tok_in,))
    dmod_cols = lax.dynamic_slice(small_all, (0, j_me * n_mod), (N_DEV, n_mod))
    dmod_pad = jnp.concatenate([dmod_cols, jnp.zeros_like(dmod_cols)], axis=0).astype(BF16)
    gw_ada = _mm_tn("gw_ada_mm", ca_pad, dmod_pad, 1)[0]

    weights = dict(w_ada=w_ada, b_ada=b_ada, g_pre_attn=g_pre_attn, w_in=w_in, g_q_a=g_q_a, g_k_a=g_k_a, g_ckv=g_ckv,
                   w_kv_b=w_kv_b, g_out_a=g_out_a, g_out_b=g_out_b, w_out=w_out, g_post_attn=g_post_attn,
                   g_pre_mlp=g_pre_mlp, w_mlp_in=w_mlp_in, w_mlp_out=w_mlp_out, g_post_mlp=g_post_mlp)
    ms = dict(w_ada=m_w_ada, b_ada=m_b_ada, g_pre_attn=m_g_pre_attn, w_in=m_w_in, g_q_a=m_g_q_a, g_k_a=m_g_k_a,
              g_ckv=m_g_ckv, w_kv_b=m_w_kv_b, g_out_a=m_g_out_a, g_out_b=m_g_out_b, w_out=m_w_out,
              g_post_attn=m_g_post_attn, g_pre_mlp=m_g_pre_mlp, w_mlp_in=m_w_mlp_in, w_mlp_out=m_w_mlp_out,
              g_post_mlp=m_g_post_mlp)
    vs = dict(w_ada=v_w_ada, b_ada=v_b_ada, g_pre_attn=v_g_pre_attn, w_in=v_w_in, g_q_a=v_g_q_a, g_k_a=v_g_k_a,
              g_ckv=v_g_ckv, w_kv_b=v_w_kv_b, g_out_a=v_g_out_a, g_out_b=v_g_out_b, w_out=v_w_out,
              g_post_attn=v_g_post_attn, g_pre_mlp=v_g_pre_mlp, w_mlp_in=v_w_mlp_in, w_mlp_out=v_w_mlp_out,
              g_post_mlp=v_g_post_mlp)
    order = list(weights)
    grads, deltas, new_m, new_v = {}, {}, {}, {}

    def update(name, g, after=()):
        view = (lambda t: jnp.swapaxes(t, 0, 1)) if name == "w_in" else (lambda t: t)
        d_, m_, v_ = _adamw("adamw_" + name, view(weights[name][0]), g, view(ms[name][0]), view(vs[name][0]), after)
        grads[name], deltas[name], new_m[name], new_v[name] = (view(t)[None] for t in (g, d_, m_, v_))
        return d_

    chain = (("w_mlp_out", r_down), ("w_mlp_in", r_up), ("w_out", r_out), ("w_kv_b", r_kvb), ("w_in", r_in))
    swap, tok = _reduce_finish(chain[0][1], where, small_sum)
    last = update("w_ada", gw_ada, (tok,))
    for k, (name, _) in enumerate(chain):
        (g,) = _reduce_end(swap, last)
        if k + 1 < len(chain):
            swap, tok = _reduce_finish(chain[k + 1][1], where, g)
        last = update(name, g, (tok,))

    cat = lambda d: jnp.concatenate([d[n] for n in small_names], axis=1)
    d_, m_, v_ = _adamw("adamw_small", cat(weights), small_sum, cat(ms), cat(vs))
    off = 0
    for n in small_names:
        w_ = weights[n].shape[1]
        grads[n], deltas[n], new_m[n], new_v[n] = (a[:, off:off + w_] for a in (small_sum, d_, m_, v_))
        off += w_

    loss = lax.psum(loss_v[0, 0], ("x", "y", "c"))
    return (loss, grad_x[None], *[grads[n] for n in order], *[deltas[n] for n in order],
            *[new_m[n] for n in order], *[new_v[n] for n in order])
```

```python
import functools
import math

import numpy as np
import jax
import jax.numpy as jnp
from jax import lax
from jax.experimental import pallas as pl
from jax.experimental.pallas import tpu as pltpu

F32 = jnp.float32
BF16 = jnp.bfloat16
MESH = pl.DeviceIdType.MESH

EPS = 1e-6
GRID_W = 64
ROPE_THETA = 10000.0
HEAD_DIM = 128
HA = 8
HKV = 2
HB = 8
QK_NOPE = 128
QK_ROPE = 64
V_DIM = 128
KV_RANK = 512
QK_B = QK_NOPE + QK_ROPE
W_QA = HA * HEAD_DIM
W_KA = HKV * HEAD_DIM
W_QB = HB * QK_B
W_IN = W_QA + 2 * W_KA + W_QB + KV_RANK + QK_ROPE
N_CHIPS = 4
N_DEV = 8

LANE = 128
O_QA = 0
O_KA = O_QA + W_QA
O_VA = O_KA + W_KA
O_QB = O_VA + W_KA
O_CKV = O_QB + HB * 2 * LANE
O_KPE = O_CKV + KV_RANK
W_INP = O_KPE + LANE

ADAM_LR = 0.001
ADAM_B1 = 0.9
ADAM_B2 = 0.999
ADAM_EPS = 1e-08
ADAM_WD = 0.01
ADAM_STEP = 10

VMEM_LIMIT = 56 * 1024 * 1024
HEADS_PER_STEP = 2


def _params(sem):
    return pltpu.CompilerParams(dimension_semantics=sem, vmem_limit_bytes=VMEM_LIMIT)


def _pick(dim, target, unit):
    best = None
    t = unit
    while t <= min(dim, target):
        if dim % t == 0:
            best = t
        t += unit
    return dim if best is None else best


def _rn(x):
    return x * lax.rsqrt(jnp.mean(x * x, axis=-1, keepdims=True) + EPS)


def _rowwise(name, fn, rows, vecs, row_outs, vec_outs, tile, after=()):
    S = rows[0].shape[0]
    assert S % tile == 0
    n_r, n_v, n_ro, n_vo, n_a = len(rows), len(vecs), len(row_outs), len(vec_outs), len(after)

    def body(*refs):
        r_in = refs[:n_r]
        v_in = refs[n_r:n_r + n_v]
        r_out = refs[n_r + n_v + n_a:n_r + n_v + n_a + n_ro]
        v_out = refs[n_r + n_v + n_a + n_ro:]
        ro, vo = fn(*[r[...] for r in r_in], *[v[...] for v in v_in])
        for ref, val in zip(r_out, ro):
            ref[...] = val.astype(ref.dtype)
        if n_vo:
            @pl.when(pl.program_id(0) == 0)
            def _():
                for ref in v_out:
                    ref[...] = jnp.zeros(ref.shape, ref.dtype)
            for ref, val in zip(v_out, vo):
                ref[...] += val

    in_specs = [pl.BlockSpec((tile, a.shape[1]), lambda i: (i, 0)) for a in rows]
    in_specs += [pl.BlockSpec(a.shape, lambda i: (0, 0)) for a in vecs]
    in_specs += [pl.BlockSpec(memory_space=pl.ANY) for _ in after]
    out_specs = [pl.BlockSpec((tile, w), lambda i: (i, 0)) for w, _ in row_outs]
    out_specs += [pl.BlockSpec((1, w), lambda i: (0, 0)) for w in vec_outs]
    out_shape = [jax.ShapeDtypeStruct((S, w), dt) for w, dt in row_outs]
    out_shape += [jax.ShapeDtypeStruct((1, w), F32) for w in vec_outs]
    res = pl.pallas_call(
        body, name=name, grid=(S // tile,), in_specs=in_specs, out_specs=out_specs, out_shape=out_shape,
        compiler_params=_params(("arbitrary",)),
    )(*rows, *vecs, *after)
    return res[:n_ro], res[n_ro:]


CONTRACT_TILE = 2048
MM_TILE_BYTES = 40 * 1024 * 1024


def _rows_that_fit(M, tile_bytes):
    t = M
    while t % 32 == 0 and tile_bytes(t) > MM_TILE_BYTES:
        t //= 2
    return t


def _accumulate(acc, part, step, n_steps, finish):
    if n_steps == 1:
        finish(part)
        return

    @pl.when(step == 0)
    def _():
        acc[...] = part

    @pl.when(step > 0)
    def _():
        acc[...] += part

    @pl.when(step == n_steps - 1)
    def _():
        finish(acc[...])


def _mm_nn(name, a, b3, out_dtypes, epilogue=None, extras=(), after=()):
    M, K = a.shape
    nB, K2, Nsh = b3.shape
    assert K == K2
    N = nB * Nsh
    tn, tk = _pick(Nsh, 1536, LANE), _pick(K, CONTRACT_TILE, LANE)
    out_bytes = sum(jnp.dtype(dt).itemsize for dt in out_dtypes) + sum(e.dtype.itemsize for e in extras)
    tm = _rows_that_fit(M, lambda t: 2 * (t * tk * a.dtype.itemsize + tk * tn * b3.dtype.itemsize)
                        + t * tn * (4 * (K > tk) + 2 * out_bytes))
    nps, nk, ne, no, na = Nsh // tn, K // tk, len(extras), len(out_dtypes), len(after)

    def body(a_ref, b_ref, *rest):
        e_refs, o_refs = rest[:ne], rest[ne + na:ne + na + no]

        def finish(total):
            vals = (total,) if epilogue is None else epilogue(total, *[e[...] for e in e_refs])
            for ref, val in zip(o_refs, vals):
                ref[...] = val.astype(ref.dtype)

        part = jnp.dot(a_ref[...].astype(BF16), b_ref[...].astype(BF16), preferred_element_type=F32)
        _accumulate(rest[-1], part, pl.program_id(2), nk, finish)

    in_specs = [pl.BlockSpec((tm, tk), lambda i, j, k: (i, k)),
                pl.BlockSpec((None, tk, tn), lambda i, j, k: (j // nps, k, j % nps))]
    in_specs += [pl.BlockSpec((tm, tn), lambda i, j, k: (i, j)) for _ in extras]
    in_specs += [pl.BlockSpec(memory_space=pl.ANY) for _ in after]
    return pl.pallas_call(
        body, name=name, grid=(M // tm, N // tn, nk), in_specs=in_specs,
        out_specs=[pl.BlockSpec((tm, tn), lambda i, j, k: (i, j)) for _ in out_dtypes],
        out_shape=[jax.ShapeDtypeStruct((M, N), dt) for dt in out_dtypes],
        scratch_shapes=[pltpu.VMEM((tm, tn), F32)] if nk > 1 else [],
        compiler_params=_params(("parallel", "parallel", "arbitrary")),
    )(a, b3, *extras, *after)


def _mm_nt(name, a, b3, out_dtypes, epilogue=None, extras=(), after=()):
    M, N = a.shape
    nB, K, Nsh = b3.shape
    assert N == nB * Nsh
    to, tn = _pick(K, 1536, LANE), _pick(Nsh, CONTRACT_TILE, LANE)
    out_bytes = sum(jnp.dtype(dt).itemsize for dt in out_dtypes) + sum(e.dtype.itemsize for e in extras)
    tm = _rows_that_fit(M, lambda t: 2 * (t * tn * a.dtype.itemsize + to * tn * b3.dtype.itemsize)
                        + t * to * (4 * (N > tn) + 2 * out_bytes))
    nps, nn, ne, no, na = Nsh // tn, N // tn, len(extras), len(out_dtypes), len(after)

    def body(a_ref, b_ref, *rest):
        e_refs, o_refs = rest[:ne], rest[ne + na:ne + na + no]

        def finish(total):
            vals = (total,) if epilogue is None else epilogue(total, *[e[...] for e in e_refs])
            for ref, val in zip(o_refs, vals):
                ref[...] = val.astype(ref.dtype)

        part = lax.dot_general(a_ref[...].astype(BF16), b_ref[...].astype(BF16),
                               (((1,), (1,)), ((), ())), preferred_element_type=F32)
        _accumulate(rest[-1], part, pl.program_id(2), nn, finish)

    in_specs = [pl.BlockSpec((tm, tn), lambda i, o, n: (i, n)),
                pl.BlockSpec((None, to, tn), lambda i, o, n: (n // nps, o, n % nps))]
    in_specs += [pl.BlockSpec((tm, to), lambda i, o, n: (i, o)) for _ in extras]
    in_specs += [pl.BlockSpec(memory_space=pl.ANY) for _ in after]
    return pl.pallas_call(
        body, name=name, grid=(M // tm, K // to, nn), in_specs=in_specs,
        out_specs=[pl.BlockSpec((tm, to), lambda i, o, n: (i, o)) for _ in out_dtypes],
        out_shape=[jax.ShapeDtypeStruct((M, K), dt) for dt in out_dtypes],
        scratch_shapes=[pltpu.VMEM((tm, to), F32)] if nn > 1 else [],
        compiler_params=_params(("parallel", "parallel", "arbitrary")),
    )(a, b3, *extras, *after)


def _mm_tn(name, a, g, nB, after=()):
    M, K = a.shape
    M2, N = g.shape
    assert M == M2 and N % nB == 0
    Nsh = N // nB
    tk, tn, tm = _pick(K, 1536, LANE), _pick(Nsh, 1024, LANE), _pick(M, CONTRACT_TILE, 16)
    nps, nm = Nsh // tn, M // tm

    def body(a_ref, g_ref, *rest):
        o_ref = rest[len(after)]

        def finish(total):
            o_ref[...] = total

        part = lax.dot_general(a_ref[...].astype(BF16), g_ref[...].astype(BF16),
                               (((0,), (0,)), ((), ())), preferred_element_type=F32)
        _accumulate(rest[-1], part, pl.program_id(2), nm, finish)

    return pl.pallas_call(
        body, name=name, grid=(K // tk, N // tn, nm),
        in_specs=[pl.BlockSpec((tm, tk), lambda k, n, m: (m, k)),
                  pl.BlockSpec((tm, tn), lambda k, n, m: (m, n))] + [pl.BlockSpec(memory_space=pl.ANY) for _ in after],
        out_specs=pl.BlockSpec((None, tk, tn), lambda k, n, m: (n // nps, k, n % nps)),
        out_shape=jax.ShapeDtypeStruct((nB, K, Nsh), F32),
        scratch_shapes=[pltpu.VMEM((tk, tn), F32)] if nm > 1 else [],
        compiler_params=_params(("parallel", "parallel", "arbitrary")),
    )(a, g, *after)


def _attn_fwd(name, q, k, v, n_heads, group, dqk, dv, scale, tq, after=()):
    S = q.shape[0]
    hp = 2 * HEADS_PER_STEP
    kvp = max(1, hp // group)

    def body(q_ref, k_ref, v_ref, *rest):
        o_ref, lse_ref = rest[-2:]
        for j in range(hp):
            jk = j * kvp // hp
            s = lax.dot_general(q_ref[:, j * dqk:(j + 1) * dqk], k_ref[:, jk * dqk:(jk + 1) * dqk],
                                (((1,), (1,)), ((), ())), preferred_element_type=F32)
            m = jnp.max(s, axis=-1, keepdims=True)
            e = jnp.exp((s - m) * scale)
            l = jnp.sum(e, axis=-1, keepdims=True)
            o_ref[:, j * dv:(j + 1) * dv] = jnp.dot(e.astype(BF16), v_ref[:, jk * dv:(jk + 1) * dv],
                                                    preferred_element_type=F32) * (1.0 / l)
            lse_ref[:, j * LANE:(j + 1) * LANE] = jnp.broadcast_to(m * scale + jnp.log(l), (tq, LANE))

    kv_block = lambda h, i: (0, (h * hp // group) // kvp)
    return pl.pallas_call(
        body, name=name, grid=(n_heads // hp, S // tq),
        in_specs=[pl.BlockSpec((tq, hp * dqk), lambda h, i: (i, h)),
                  pl.BlockSpec((S, kvp * dqk), kv_block),
                  pl.BlockSpec((S, kvp * dv), kv_block)] + [pl.BlockSpec(memory_space=pl.ANY) for _ in after],
        out_specs=[pl.BlockSpec((tq, hp * dv), lambda h, i: (i, h)),
                   pl.BlockSpec((tq, hp * LANE), lambda h, i: (i, h))],
        out_shape=[jax.ShapeDtypeStruct((S, n_heads * dv), F32), jax.ShapeDtypeStruct((S, n_heads * LANE), F32)],
        compiler_params=_params(("arbitrary", "arbitrary")),
    )(q, k, v, *after)


def _attn_bwd(name, q, k, v, do, o, lse, n_heads, group, dqk, dv, scale, tq, after=()):
    S = q.shape[0]
    n_kv = n_heads // group
    hp = HEADS_PER_STEP
    kvp = max(1, hp // group)

    def body(q_ref, k_ref, v_ref, do_ref, o_ref, lse_ref, *rest):
        dq_ref, dk_ref, dv_ref = rest[len(after):]
        h, i = pl.program_id(0), pl.program_id(1)

        @pl.when(((h * hp) % group == 0) & (i == 0))
        def _():
            dk_ref[...] = jnp.zeros(dk_ref.shape, F32)
            dv_ref[...] = jnp.zeros(dv_ref.shape, F32)

        for j in range(hp):
            jk = j * kvp // hp
            qb, dob = q_ref[:, j * dqk:(j + 1) * dqk], do_ref[:, j * dv:(j + 1) * dv]
            kb, vb = k_ref[:, jk * dqk:(jk + 1) * dqk], v_ref[:, jk * dv:(jk + 1) * dv]
            s = lax.dot_general(qb, kb, (((1,), (1,)), ((), ())), preferred_element_type=F32)
            p = jnp.exp(s * scale - jnp.tile(lse_ref[:, j * LANE:(j + 1) * LANE], (1, S // LANE)))
            dp = lax.dot_general(dob, vb, (((1,), (1,)), ((), ())), preferred_element_type=F32)
            delta = jnp.sum(dob.astype(F32) * o_ref[:, j * dv:(j + 1) * dv], axis=-1, keepdims=True)
            ds = (p * ((dp - delta) * scale)).astype(BF16)
            dv_ref[:, jk * dv:(jk + 1) * dv] += lax.dot_general(p.astype(BF16), dob, (((0,), (0,)), ((), ())),
                                                                preferred_element_type=F32)
            dq_ref[:, j * dqk:(j + 1) * dqk] = jnp.dot(ds, kb, preferred_element_type=F32)
            dk_ref[:, jk * dqk:(jk + 1) * dqk] += lax.dot_general(ds, qb, (((0,), (0,)), ((), ())),
                                                                  preferred_element_type=F32)

    kv_block = lambda h, i: (0, (h * hp // group) // kvp)
    return pl.pallas_call(
        body, name=name, grid=(n_heads // hp, S // tq),
        in_specs=[pl.BlockSpec((tq, hp * dqk), lambda h, i: (i, h)),
                  pl.BlockSpec((S, kvp * dqk), kv_block),
                  pl.BlockSpec((S, kvp * dv), kv_block),
                  pl.BlockSpec((tq, hp * dv), lambda h, i: (i, h)),
                  pl.BlockSpec((tq, hp * dv), lambda h, i: (i, h)),
                  pl.BlockSpec((tq, hp * LANE), lambda h, i: (i, h))] + [pl.BlockSpec(memory_space=pl.ANY) for _ in after],
        out_specs=[pl.BlockSpec((tq, hp * dqk), lambda h, i: (i, h)),
                   pl.BlockSpec((S, kvp * dqk), kv_block),
                   pl.BlockSpec((S, kvp * dv), kv_block)],
        out_shape=[jax.ShapeDtypeStruct((S, n_heads * dqk), F32),
                   jax.ShapeDtypeStruct((S, n_kv * dqk), F32),
                   jax.ShapeDtypeStruct((S, n_kv * dv), F32)],
        compiler_params=_params(("arbitrary", "arbitrary")),
    )(q, k, v, do, o, lse, *after)


def _adamw(name, w, g, m, v, after=()):
    R, C = w.shape
    tile = _pick(R, max(8, (1 << 19) // C // 8 * 8), 8)
    c1 = 1.0 - ADAM_B1 ** ADAM_STEP
    c2 = 1.0 - ADAM_B2 ** ADAM_STEP

    def body(w_ref, g_ref, m_ref, v_ref, *rest):
        d_ref, nm_ref, nv_ref = rest[len(after):]
        gg = g_ref[...]
        nm = ADAM_B1 * m_ref[...] + (1.0 - ADAM_B1) * gg
        nv = ADAM_B2 * v_ref[...] + (1.0 - ADAM_B2) * (gg * gg)
        d_ref[...] = -ADAM_LR * ((nm / c1) / (jnp.sqrt(nv / c2) + ADAM_EPS) + ADAM_WD * w_ref[...])
        nm_ref[...] = nm
        nv_ref[...] = nv

    spec = pl.BlockSpec((tile, C), lambda i: (i, 0))
    return pl.pallas_call(
        body, name=name, grid=(R // tile,), in_specs=[spec] * 4 + [pl.BlockSpec(memory_space=pl.ANY) for _ in after],
        out_specs=[spec] * 3, out_shape=[jax.ShapeDtypeStruct((R, C), F32)] * 3,
        compiler_params=_params(("parallel",)),
    )(w, g, m, v, *after)


def _flip(me, rel):
    return tuple(1 - p if r else p for p, r in zip(me, rel))


def _push(name, ins, out_shapes, transfers, aliases=None):
    ni, no, nt = len(ins), len(out_shapes), len(transfers)

    def body(*refs):
        in_refs, out_refs = refs[:ni], refs[ni:ni + no]
        send_sems, recv_sems = refs[ni + no], refs[ni + no + 1]
        me = (lax.axis_index("x"), lax.axis_index("y"), lax.axis_index("c"))
        copies = []
        for t, (si, sfn, oi, dfn, rel) in enumerate(transfers):
            src_ref = out_refs[si[1]] if isinstance(si, tuple) else in_refs[si]
            src = src_ref.at[sfn(*me)]
            dst = out_refs[oi].at[dfn(*me)]
            if rel is None:
                cp = pltpu.make_async_copy(src, dst, send_sems.at[t])
            else:
                cp = pltpu.make_async_remote_copy(src_ref=src, dst_ref=dst, send_sem=send_sems.at[t],
                                                  recv_sem=recv_sems.at[t], device_id=_flip(me, rel),
                                                  device_id_type=MESH)
            cp.start()
            copies.append(cp)
        for cp in copies:
            cp.wait()

    any_spec = pl.BlockSpec(memory_space=pl.ANY)
    return pl.pallas_call(
        body, name=name, in_specs=[any_spec] * ni, out_specs=[any_spec] * no, out_shape=out_shapes,
        scratch_shapes=[pltpu.SemaphoreType.DMA((nt,)), pltpu.SemaphoreType.DMA((nt,))],
        input_output_aliases=aliases or {},
        compiler_params=pltpu.CompilerParams(has_side_effects=True),
    )(*ins)


_HBM = pl.BlockSpec(memory_space=pltpu.HBM)
_SEM = pl.BlockSpec(memory_space=pltpu.SEMAPHORE)
_DATAFLOW = pltpu.SideEffectType.DATAFLOW_SIDE_EFFECTING


def _split_copies(refs, transfers, send_sems, recv_sems):
    me = (lax.axis_index("x"), lax.axis_index("y"), lax.axis_index("c"))
    return [pltpu.make_async_remote_copy(src_ref=refs[sb].at[sfn(*me)], dst_ref=refs[db].at[dfn(*me)],
                                         send_sem=send_sems.at[t], recv_sem=recv_sems.at[t],
                                         device_id=_flip(me, rel), device_id_type=MESH)
            for t, (sb, sfn, db, dfn, rel) in enumerate(transfers)]


def _push_start(name, bufs, transfers, after=()):
    nb, na, nt = len(bufs), len(after), len(transfers)

    def body(*refs):
        send_sems, recv_sems, token = refs[nb + na], refs[nb + na + 1], refs[-1]
        for cp in _split_copies(refs[nb + na + 2:2 * nb + na + 2], transfers, send_sems, recv_sems):
            cp.start()
        token[...] = jnp.zeros(token.shape, token.dtype)

    res = pl.pallas_call(
        body, name=name,
        out_shape=(pltpu.SemaphoreType.DMA((nt,)), pltpu.SemaphoreType.DMA((nt,)),
                   *[pltpu.HBM(b.shape, b.dtype) for b in bufs], jax.ShapeDtypeStruct((8, LANE), F32)),
        in_specs=[_HBM] * nb + [pl.BlockSpec(memory_space=pl.ANY)] * na,
        out_specs=(_SEM, _SEM, *[_HBM] * nb, pl.BlockSpec(memory_space=pltpu.VMEM)),
        input_output_aliases={i: i + 2 for i in range(nb)},
        compiler_params=pltpu.CompilerParams(has_side_effects=_DATAFLOW),
    )(*[pltpu.with_memory_space_constraint(b, pltpu.HBM) for b in bufs], *after)
    return res[0], res[1], list(res[2:2 + nb]), res[-1]


def _push_wait(name, send_sems, recv_sems, bufs, transfers, after):
    nb = len(bufs)

    def body(*refs):
        for cp in _split_copies(refs[:nb], transfers, refs[nb], refs[nb + 1]):
            cp.wait_send()
            cp.wait_recv()

    res = pl.pallas_call(
        body, name=name, out_shape=[pltpu.HBM(b.shape, b.dtype) for b in bufs],
        in_specs=[_HBM] * nb + [_SEM, _SEM, pl.BlockSpec(memory_space=pl.ANY)], out_specs=[_HBM] * nb,
        input_output_aliases={i: i for i in range(nb)},
        compiler_params=pltpu.CompilerParams(has_side_effects=_DATAFLOW),
    )(*bufs, send_sems, recv_sems, after)
    return list(res)


ICI_RELS = ((1, 0, 0), (0, 1, 0), (1, 1, 0))
SIBLING = (0, 0, 1)


def _chip(x, y):
    return 2 * x + y


def _cast_into_slot(name, where, w, after=()):
    R, C = w.shape
    tile = _pick(R, 512, BF16_ROWS)

    def body(where_ref, w_ref, *rest):
        rest[-1][...] = w_ref[...].astype(BF16)

    return pl.pallas_call(
        body, name=name,
        grid_spec=pltpu.PrefetchScalarGridSpec(
            num_scalar_prefetch=1, grid=(R // tile,),
            in_specs=[pl.BlockSpec((tile, C), lambda i, wh: (i, 0))] + [pl.BlockSpec(memory_space=pl.ANY) for _ in after],
            out_specs=pl.BlockSpec((None, tile, C), lambda i, wh: (wh[1], i, 0))),
        out_shape=jax.ShapeDtypeStruct((N_CHIPS, R, C), BF16),
        compiler_params=_params(("parallel",)),
    )(where, w, *after)


BF16_ROWS = 16


def _rows_split(shape2):
    return (shape2[0] // 2) % BF16_ROWS == 0


def _half_shape(shape2):
    R, C = shape2
    return (R // 2, C) if _rows_split(shape2) else (R, C // 2)


def _half(shape2, c):
    R, C = shape2
    if _rows_split(shape2):
        return (pl.ds(c * (R // 2), R // 2), slice(None))
    return (slice(None), pl.ds(pl.multiple_of(c * (C // 2), LANE), C // 2))


def _gather_start(tag, where, shards, after, cast_after=()):
    n = len(shards)
    own = [_cast_into_slot(f"cast_w_{tag}{w}", where, shards[w], cast_after) for w in range(n)]
    transfers = []
    for w in range(n):
        def idx(x, y, c, shape2=shards[w].shape):
            return (_chip(x, y), *_half(shape2, c))
        transfers += [(w, idx, w, idx, rel) for rel in ICI_RELS]
    send_sems, recv_sems, bufs, token = _push_start(f"gather_{tag}_start", own, transfers, after)
    return (tag, send_sems, recv_sems, bufs, transfers), token


def _forward_transfers(part):
    second = []
    for w, p in enumerate(part):
        for rel in ICI_RELS:
            def idx(x, y, c, shape2=p.shape[1:], rel=rel):
                return (_chip(x ^ rel[0], y ^ rel[1]), *_half(shape2, c))
            second.append((w, idx, w, idx, SIBLING))
    return second


def _gather_finish(handle, after):
    tag, send_sems, recv_sems, bufs, transfers = handle
    part = _push_wait(f"gather_{tag}_wait", send_sems, recv_sems, bufs, transfers, after)
    outs = [jax.ShapeDtypeStruct(p.shape, p.dtype) for p in part]
    second = [(("out", sb), sfn, db, dfn, rel) for sb, sfn, db, dfn, rel in _forward_transfers(part)]
    return _push(f"gather_{tag}_d2d", list(part), outs, second, aliases={w: w for w in range(len(part))})


def _gather_mid(handle, after):
    tag, send_sems, recv_sems, bufs, transfers = handle
    part = _push_wait(f"gather_{tag}_wait", send_sems, recv_sems, bufs, transfers, after)
    second = _forward_transfers(part)
    s2, r2, bufs2, token = _push_start(f"gather_{tag}_d2d_start", part, second)
    return (tag, s2, r2, bufs2, second), token


def _gather_end(handle, after):
    tag, send_sems, recv_sems, bufs, transfers = handle
    return _push_wait(f"gather_{tag}_d2d_wait", send_sems, recv_sems, bufs, transfers, after)


def _reduce_begin(tag, grads, after=()):
    n = len(grads)
    land = [lax.empty((N_CHIPS, *_half_shape(g.shape[1:])), F32) for g in grads]
    t1 = []
    for w in range(n):
        t1.append((w, lambda x, y, c, shape2=grads[w].shape[1:]: (slice(None), *_half(shape2, 1 - c)), n + w,
                   lambda x, y, c: (slice(None), slice(None), slice(None)), SIBLING))
    send_sems, recv_sems, bufs, token = _push_start(f"reduce_{tag}_d2d_start", list(grads) + land, t1, after)
    return (tag, send_sems, recv_sems, bufs, t1), token


def _reduce_start(handle, where, after, after_start=()):
    tag, send_sems, recv_sems, bufs, t1 = handle
    n = len(bufs) // 2
    res = _push_wait(f"reduce_{tag}_d2d_wait", send_sems, recv_sems, bufs, t1, after)
    grads, from_sib = res[:n], res[n:]
    pair = [_pair_sum(f"pair_sum_{tag}{w}", where, grads[w], from_sib[w]) for w in range(n)]
    land = [lax.empty((3, *_half_shape(g.shape[1:])), BF16) for g in grads]
    t3 = []
    for w in range(n):
        for r, rel in enumerate(ICI_RELS):
            t3.append((w, lambda x, y, c, rel=rel: (_chip(x ^ rel[0], y ^ rel[1]), slice(None), slice(None)), n + w,
                       lambda x, y, c, r=r: (r, slice(None), slice(None)), rel))
    send_sems, recv_sems, bufs, token = _push_start(f"reduce_{tag}_start", pair + land, t3, after_start)
    return (tag, send_sems, recv_sems, bufs, t3, list(grads), list(from_sib)), token


def _reduce_finish(handle, where, after):
    tag, send_sems, recv_sems, bufs, t3, grads, from_sib = handle
    n = len(grads)
    from_chips = _push_wait(f"reduce_{tag}_wait", send_sems, recv_sems, bufs, t3, after)[n:]
    mine = [_chip_sum(f"chip_sum_{tag}{w}", where, grads[w], from_sib[w], from_chips[w]) for w in range(n)]
    t5 = []
    for w in range(n):
        def idx(x, y, c, shape2=grads[w].shape[1:]):
            return _half(shape2, c)
        t5.append((w, idx, w, idx, SIBLING))
    send_sems, recv_sems, bufs, token = _push_start(f"reduce_{tag}_swap_start", mine, t5)
    return (tag, send_sems, recv_sems, bufs, t5), token


def _reduce_end(handle, after):
    tag, send_sems, recv_sems, bufs, t5 = handle
    return _push_wait(f"reduce_{tag}_swap_wait", send_sems, recv_sems, bufs, t5, after)


def _pair_sum(name, where, g, from_sib):
    _, hr, hc = from_sib.shape
    tile = _pick(hr, max(16, (1 << 19) // hc // 16 * 16), 16)
    nb = hr // tile
    slot = lambda k, wh: (wh[1] + 1 + k) % N_CHIPS
    if _rows_split(g.shape[1:]):
        mine = lambda k, i, wh: (slot(k, wh), wh[0] * nb + i, 0)
    else:
        mine = lambda k, i, wh: (slot(k, wh), i, wh[0])

    def body(where_ref, g_ref, s_ref, o_ref):
        o_ref[...] = (g_ref[...] + s_ref[...]).astype(BF16)

    return pl.pallas_call(
        body, name=name,
        grid_spec=pltpu.PrefetchScalarGridSpec(
            num_scalar_prefetch=1, grid=(N_CHIPS - 1, nb),
            in_specs=[pl.BlockSpec((None, tile, hc), mine),
                      pl.BlockSpec((None, tile, hc), lambda k, i, wh: (slot(k, wh), i, 0))],
            out_specs=pl.BlockSpec((None, tile, hc), lambda k, i, wh: (slot(k, wh), i, 0))),
        out_shape=jax.ShapeDtypeStruct((N_CHIPS, hr, hc), BF16),
        compiler_params=_params(("parallel", "parallel")),
    )(where, g, from_sib)


def _chip_sum(name, where, g, from_sib, from_chips):
    _, R, C = g.shape
    _, hr, hc = from_sib.shape
    tile = _pick(hr, max(16, (1 << 19) // hc // 16 * 16), 16)
    nb = hr // tile
    if _rows_split((R, C)):
        mine = lambda i, wh: (wh[0] * nb + i, 0)
    else:
        mine = lambda i, wh: (i, wh[0])

    def body(where_ref, g_ref, s_ref, r_ref, o_ref):
        acc = g_ref[...] + s_ref[...]
        for r in range(3):
            acc = acc + r_ref[r].astype(F32)
        o_ref[...] = acc

    return pl.pallas_call(
        body, name=name,
        grid_spec=pltpu.PrefetchScalarGridSpec(
            num_scalar_prefetch=1, grid=(nb,),
            in_specs=[pl.BlockSpec((None, tile, hc), lambda i, wh: (wh[1], *mine(i, wh))),
                      pl.BlockSpec((None, tile, hc), lambda i, wh: (wh[1], i, 0)),
                      pl.BlockSpec((3, tile, hc), lambda i, wh: (0, i, 0))],
            out_specs=pl.BlockSpec((tile, hc), mine)),
        out_shape=jax.ShapeDtypeStruct((R, C), F32),
        compiler_params=_params(("parallel",)),
    )(where, g, from_sib, from_chips)


def _allgather8(name, blk):
    m_per, n = blk.shape
    rels = [(0, 0, 1), (1, 0, 0), (0, 1, 0), (1, 1, 0), (1, 0, 1), (0, 1, 1), (1, 1, 1)]

    def body(x_ref, out_ref, send_sems, recv_sems):
        me = (lax.axis_index("x"), lax.axis_index("y"), lax.axis_index("c"))
        my_rows = out_ref.at[pl.ds((4 * me[0] + 2 * me[1] + me[2]) * m_per, m_per), :]
        out_ref[pl.ds((4 * me[0] + 2 * me[1] + me[2]) * m_per, m_per), :] = x_ref[...]
        copies = []
        for t, rel in enumerate(rels):
            cp = pltpu.make_async_remote_copy(src_ref=x_ref, dst_ref=my_rows, send_sem=send_sems.at[t],
                                              recv_sem=recv_sems.at[t], device_id=_flip(me, rel),
                                              device_id_type=MESH)
            cp.start()
            copies.append(cp)
        for cp in copies:
            cp.wait()

    return pl.pallas_call(
        body, name=name, out_shape=jax.ShapeDtypeStruct((N_DEV * m_per, n), blk.dtype),
        in_specs=[pl.BlockSpec(memory_space=pltpu.VMEM)], out_specs=pl.BlockSpec(memory_space=pltpu.VMEM),
        scratch_shapes=[pltpu.SemaphoreType.DMA((7,)), pltpu.SemaphoreType.DMA((7,))],
        compiler_params=pltpu.CompilerParams(has_side_effects=True, vmem_limit_bytes=VMEM_LIMIT),
    )(blk)


def _rope_tables(seq_len, dim, reps):
    rows = seq_len // GRID_W
    t = np.arange(seq_len)
    row, col = (t // GRID_W).astype(np.float32), (t % GRID_W).astype(np.float32)
    half = dim // 2
    inv = jnp.asarray(ROPE_THETA, F32) ** (-jnp.arange(0, half, 2, dtype=F32) / half)
    ang_r = jnp.asarray(row)[:, None] * inv[None, :]
    ang_c = jnp.asarray(col)[:, None] * inv[None, :]
    ang = jnp.concatenate([ang_r, ang_r, ang_c, ang_c], axis=-1)
    cos, sin = jnp.cos(ang), jnp.sin(ang)
    low = (np.arange(dim) % (dim // 2)) < (dim // 4)
    s_lo = jnp.where(jnp.asarray(low)[None, :], -sin, 0.0)
    s_hi = jnp.where(jnp.asarray(low)[None, :], 0.0, sin)
    del rows
    return tuple(jnp.tile(a, (1, reps)) for a in (cos, s_lo, s_hi))


def _make_rope(shift):
    def up(v):
        return pltpu.roll(v, LANE - shift, 1)

    def down(v):
        return pltpu.roll(v, shift, 1)

    @jax.custom_vjp
    def rope(v, cos, s_lo, s_hi):
        return v * cos + up(v) * s_lo + down(v) * s_hi

    def fwd(v, cos, s_lo, s_hi):
        return rope(v, cos, s_lo, s_hi), (cos, s_lo, s_hi)

    def bwd(res, dy):
        cos, s_lo, s_hi = res
        return dy * cos + down(dy * s_lo) + up(dy * s_hi), jnp.zeros_like(cos), jnp.zeros_like(cos), jnp.zeros_like(cos)

    rope.defvjp(fwd, bwd)
    return rope


_rope_a = _make_rope(HEAD_DIM // 4)
_rope_b = _make_rope(QK_ROPE // 4)


def _prep(segs, tabs, g_q, g_k, g_ckv):
    qa, ka, va, qn, qp, ckv, kpe = segs
    ta, tb = tabs
    q_a = jnp.concatenate([_rope_a(_rn(s) * g_q, *ta) for s in qa], axis=1)
    k_a = jnp.concatenate([_rope_a(_rn(s) * g_k, *ta) for s in ka], axis=1)
    cat = []
    for h in range(HB):
        cat += [qn[h], _rope_b(qp[h], *tb)]
    q_cat = jnp.concatenate(cat, axis=1)
    return q_a, k_a, va, q_cat, _rn(ckv) * g_ckv, _rope_b(kpe, *tb)


def _split_proj(proj):
    qa = [proj[:, O_QA + h * LANE:O_QA + (h + 1) * LANE] for h in range(HA)]
    ka = [proj[:, O_KA + h * LANE:O_KA + (h + 1) * LANE] for h in range(HKV)]
    va = proj[:, O_VA:O_QB]
    qn = [proj[:, O_QB + 2 * h * LANE:O_QB + (2 * h + 1) * LANE] for h in range(HB)]
    qp = [proj[:, O_QB + (2 * h + 1) * LANE:O_QB + (2 * h + 2) * LANE] for h in range(HB)]
    return qa, ka, va, qn, qp, proj[:, O_CKV:O_KPE], proj[:, O_KPE:W_INP]


def _permute_w_in(nat):
    K = nat.shape[1]
    pad = LANE - QK_ROPE
    qb = jnp.pad(nat[O_QB:O_QB + W_QB].reshape(HB, QK_B, K), ((0, 0), (0, pad), (0, 0)))
    return jnp.concatenate([nat[:O_QB], qb.reshape(HB * 2 * LANE, K), nat[O_QB + W_QB:],
                            jnp.zeros((pad, K), nat.dtype)], axis=0)


def _unpermute_w_in(p):
    K = p.shape[1]
    qb = p[O_QB:O_CKV].reshape(HB, 2 * LANE, K)[:, :QK_B].reshape(W_QB, K)
    return jnp.concatenate([p[:O_QB], qb, p[O_CKV:O_KPE + QK_ROPE]], axis=0)


def kernel(x, c, w_ada, b_ada, g_pre_attn, w_in, g_q_a, g_k_a, g_ckv, w_kv_b, g_out_a, g_out_b, w_out, g_post_attn, g_pre_mlp, w_mlp_in, w_mlp_out, g_post_mlp, loss_target, m_w_ada, m_b_ada, m_g_pre_attn, m_w_in, m_g_q_a, m_g_k_a, m_g_ckv, m_w_kv_b, m_g_out_a, m_g_out_b, m_w_out, m_g_post_attn, m_g_pre_mlp, m_w_mlp_in, m_w_mlp_out, m_g_post_mlp, v_w_ada, v_b_ada, v_g_pre_attn, v_w_in, v_g_q_a, v_g_k_a, v_g_ckv, v_w_kv_b, v_g_out_a, v_g_out_b, v_w_out, v_g_post_attn, v_g_pre_mlp, v_w_mlp_in, v_w_mlp_out, v_g_post_mlp):
    S, D = x.shape[1], x.shape[2]
    x2d, tgt = x[0], loss_target[0]
    ix, iy, ic = lax.axis_index("x"), lax.axis_index("y"), lax.axis_index("c")
    j_me = _chip(ix, iy)
    e_me = 4 * ix + 2 * iy + ic
    T = _pick(S, 256, 8)
    TB = _pick(S, 128, 8)

    where = jnp.stack([ic, j_me]).astype(jnp.int32)
    DFF = w_mlp_out.shape[1] * N_CHIPS

    c_all = _allgather8("gather_c", c.reshape(8, D // 8)).reshape(N_DEV, D)

    def silu_fn(cc):
        return (cc * (1.0 / (1.0 + jnp.exp(-cc))),), ()

    (c_act,), _ = _rowwise("silu_c", silu_fn, [c_all], [], [(D, BF16)], [], N_DEV)
    ca_pad = jnp.concatenate([c_act, jnp.zeros_like(c_act)], axis=0)
    n_mod = w_ada.shape[2]
    b_cols = lax.dynamic_slice(b_ada, (0, j_me * n_mod), (1, n_mod))
    (mod_part,) = _mm_nn("mod_mm", ca_pad, w_ada, [F32], epilogue=lambda acc, b: (acc + b,),
                         extras=(jnp.broadcast_to(b_cols, (16, n_mod)),))
    mod_all = _allgather8("gather_mod", mod_part[:8]).reshape(N_DEV, 8, n_mod)
    mod = jnp.concatenate([lax.dynamic_slice(mod_all, (2 * j, e_me, 0), (1, 1, n_mod))[0] for j in range(N_CHIPS)],
                          axis=1)

    w_in_t = jnp.swapaxes(w_in[0], 0, 1)
    h_qkv, tok_a = _gather_start("qkv", where, [w_in_t, w_kv_b[0]], (mod_all,))
    h_out, tok_b = _gather_start("out", where, [w_out[0]], (tok_a,), (tok_a,))
    h_up, tok_c = _gather_start("up", where, [w_mlp_in[0]], (tok_b,), (tok_b,))
    h_down, tok_d = _gather_start("down", where, [w_mlp_out[0]], (tok_c,), (tok_c,))
    mod = mod + tok_d[0, 0]
    sh_a, sc_a, gt_a, sh_m, sc_m, gt_m = [mod[:, i * D:(i + 1) * D] for i in range(6)]

    tabs_a = _rope_tables(S, HEAD_DIM, 1)
    tabs_b = _rope_tables(S, QK_ROPE, LANE // QK_ROPE)

    def pre_attn_fn(xt, g, sc, sh):
        return (_rn(xt) * g * (1.0 + sc) + sh,), ()

    (h_b,), _ = _rowwise("pre_attn", pre_attn_fn, [x2d], [g_pre_attn, sc_a, sh_a], [(D, BF16)], [], T)
    g_in, g_kvb = _gather_finish(h_qkv, h_b)
    wpt = _permute_w_in(g_in.reshape(W_IN, D))[None]
    (proj,) = _mm_nt("proj_mm", h_b, wpt, [F32])

    def prep_fn(pt, ca_, sla, sha, cb_, slb, shb, gq, gk, gc):
        return _prep(_split_proj(pt), ((ca_, sla, sha), (cb_, slb, shb)), gq, gk, gc), ()

    (qa_b, ka_b, va_b, qcat_b, ckvn_b, kpe_b), _ = _rowwise(
        "prep", prep_fn, [proj, *tabs_a, *tabs_b], [g_q_a, g_k_a, g_ckv],
        [(W_QA, BF16), (W_KA, BF16), (W_KA, BF16), (2 * HB * LANE, BF16), (KV_RANK, BF16), (LANE, BF16)], [], TB)
    (kv,) = _mm_nn("kv_mm", ckvn_b, g_kvb, [F32])

    def kcat_fn(kvt, kpet):
        kc, vs = [], []
        for h in range(HB):
            kc += [kvt[:, 2 * h * LANE:(2 * h + 1) * LANE], kpet.astype(F32)]
            vs.append(kvt[:, (2 * h + 1) * LANE:(2 * h + 2) * LANE])
        return (jnp.concatenate(kc, axis=1), jnp.concatenate(vs, axis=1)), ()

    (kcat_b, vb_b), _ = _rowwise("kcat", kcat_fn, [kv, kpe_b], [], [(2 * HB * LANE, BF16), (HB * V_DIM, BF16)], [], T)

    TQ = _pick(S, 512, 16)
    sc_a_ = 1.0 / math.sqrt(HEAD_DIM)
    sc_b_ = 1.0 / math.sqrt(QK_B)
    o_a, lse_a = _attn_fwd("attn_a_fwd", qa_b, ka_b, va_b, HA, HA // HKV, HEAD_DIM, HEAD_DIM, sc_a_, TQ)
    m_out, tok_m = _gather_mid(h_out, o_a)
    m_up, tok_m = _gather_mid(h_up, tok_m)
    o_b, lse_b = _attn_fwd("attn_b_fwd", qcat_b, kcat_b, vb_b, HB, 1, 2 * LANE, V_DIM, sc_b_, TQ, after=(tok_m,))

    def mix_fn(oa, ob, ga, gb):
        return (jnp.concatenate([_rn(oa) * ga, _rn(ob) * gb], axis=1),), ()

    (on_b,), _ = _rowwise("mix_norm", mix_fn, [o_a, o_b], [g_out_a, g_out_b], [(2 * W_QA, BF16)], [], T)
    (g_wout,) = _gather_end(m_out, on_b)
    w_out_f = g_wout.reshape(1, N_CHIPS * g_wout.shape[1], D)
    (o2,) = _mm_nn("out_mm", on_b, w_out_f, [F32])

    def mid(xt, o2t, gta, gpa, gpm, scm, shm):
        x1 = xt + gta * (_rn(o2t) * gpa)
        return x1, _rn(x1) * gpm * (1.0 + scm) + shm

    def mid_fn(*a):
        return mid(*a), ()

    mid_vecs = [gt_a, g_post_attn, g_pre_mlp, sc_m, sh_m]
    (x1, h2_b), _ = _rowwise("mid", mid_fn, [x2d, o2], mid_vecs, [(D, F32), (D, BF16)], [], T)
    (g_mlp_in,) = _gather_end(m_up, h2_b)
    m_down, tok_m = _gather_mid(h_down, h2_b)
    u, a_b = _mm_nn("mlp_in_mm", h2_b, g_mlp_in, [F32, BF16],
                    epilogue=lambda acc: (acc, jnp.square(jnp.maximum(acc, 0.0))), after=(tok_m,))
    (g_mlp_out,) = _gather_end(m_down, a_b)
    w_mlp_out_f = g_mlp_out.reshape(1, DFF, D)
    (y,) = _mm_nn("mlp_out_mm", a_b, w_mlp_out_f, [F32])

    def last(yt, x1t, gtm, gpo):
        return x1t + gtm * (_rn(yt) * gpo)

    def last_fn(yt, x1t, tt, gtm, gpo):
        x2, vjp = jax.vjp(last, yt, x1t, gtm, gpo)
        err = x2 - tt
        dy, dx1, dgt, dg = vjp(err * (1.0 / D))
        loss = 0.5 * jnp.sum(jnp.mean(err * err, axis=-1))
        return (dy, dx1), (jnp.full((1, LANE), loss, F32), dgt, dg)

    (dy_b, dx2), (loss_v, d_gt_m, d_g_post_mlp) = _rowwise(
        "loss_bwd", last_fn, [y, x1, tgt], [gt_m, g_post_mlp], [(D, BF16), (D, F32)], [LANE, D, D], TB)

    gw_mlp_out = _mm_tn("gw_mlp_out_mm", a_b, dy_b, 1).reshape(N_CHIPS, DFF // N_CHIPS, D)
    b_down, tok = _reduce_begin("down", [gw_mlp_out])
    (du_b,) = _mm_nt("d_mlp_out_mm", dy_b, w_mlp_out_f, [BF16],
                     epilogue=lambda acc, ut: (acc * (2.0 * jnp.maximum(ut, 0.0)),), extras=(u,), after=(tok,))
    r_down, tok = _reduce_start(b_down, where, du_b)
    gw_mlp_in = _mm_tn("gw_mlp_in_mm", h2_b, du_b, N_CHIPS, after=(tok,))
    b_up, tok = _reduce_begin("up", [gw_mlp_in])
    (dh2,) = _mm_nt("d_mlp_in_mm", du_b, g_mlp_in, [F32], after=(tok,))
    r_up, tok_up = _reduce_start(b_up, where, dh2)

    def mid_bwd_fn(xt, o2t, dh2t, dx2t, *vecs):
        _, vjp = jax.vjp(mid, xt, o2t, *vecs)
        dx, do2, dgta, dgpa, dgpm, dscm, dshm = vjp((dx2t, dh2t))
        return (dx, do2), (dgta, dgpa, dgpm, dscm, dshm)

    (dx1, do2_b), (d_gt_a, d_g_post_attn, d_g_pre_mlp, d_sc_m, d_sh_m) = _rowwise(
        "mid_bwd", mid_bwd_fn, [x2d, o2, dh2, dx2], mid_vecs, [(D, F32), (D, BF16)], [D] * 5, TB, after=(tok_up,))

    (d_on,) = _mm_nt("d_out_mm", do2_b, w_out_f, [F32])
    gw_out = _mm_tn("gw_out_mm", on_b, do2_b, 1).reshape(N_CHIPS, w_out.shape[1], D)
    b_out, tok_out = _reduce_begin("out", [gw_out])

    def mix_bwd_fn(oa, ob, dont, ga, gb):
        _, vjp = jax.vjp(lambda a_, b_, g1, g2: mix_fn(a_, b_, g1, g2)[0][0], oa, ob, ga, gb)
        doa, dob, dga, dgb = vjp(dont)
        return (doa, dob), (dga, dgb)

    (doa_b, dob_b), (d_g_out_a, d_g_out_b) = _rowwise(
        "mix_bwd", mix_bwd_fn, [o_a, o_b, d_on], [g_out_a, g_out_b], [(W_QA, BF16), (HB * V_DIM, BF16)],
        [W_QA, HB * V_DIM], T, after=(tok_out,))

    dqa, dka, dva = _attn_bwd("attn_a_bwd", qa_b, ka_b, va_b, doa_b, o_a, lse_a, HA, HA // HKV, HEAD_DIM, HEAD_DIM,
                              sc_a_, TQ)
    r_out, tok = _reduce_start(b_out, where, dqa)
    dqcat, dkcat, dvb = _attn_bwd("attn_b_bwd", qcat_b, kcat_b, vb_b, dob_b, o_b, lse_b, HB, 1, 2 * LANE, V_DIM,
                                  sc_b_, TQ, after=(tok,))

    def kcat_bwd_fn(dkc, dvt):
        dkv, dkpe = [], None
        for h in range(HB):
            dkv += [dkc[:, 2 * h * LANE:(2 * h + 1) * LANE], dvt[:, h * V_DIM:(h + 1) * V_DIM]]
            piece = dkc[:, (2 * h + 1) * LANE:(2 * h + 2) * LANE]
            dkpe = piece if dkpe is None else dkpe + piece
        return (jnp.concatenate(dkv, axis=1), dkpe), ()

    (dkv_b, dkpe), _ = _rowwise("kcat_bwd", kcat_bwd_fn, [dkcat, dvb], [], [(2 * HB * LANE, BF16), (LANE, F32)], [], T)
    (d_ckvn,) = _mm_nt("d_kv_mm", dkv_b, g_kvb, [F32])
    gw_kvb = _mm_tn("gw_kv_mm", ckvn_b, dkv_b, N_CHIPS)
    b_kvb, tok_kvb = _reduce_begin("kvb", [gw_kvb])

    def prep_bwd_fn(pt, dqat, dkat, dvat, dqct, dckt, dkpt, ca_, sla, sha, cb_, slb, shb, gq, gk, gc):
        tabs = ((ca_, sla, sha), (cb_, slb, shb))
        _, vjp = jax.vjp(lambda sg, a1, a2, a3: _prep(sg, tabs, a1, a2, a3), _split_proj(pt), gq, gk, gc)
        (dqa_s, dka_s, dva_s, dqn_s, dqp_s, dck_s, dkp_s), dgq, dgk, dgc = vjp((dqat, dkat, dvat, dqct, dckt, dkpt))
        dqb_s = [t for pair in zip(dqn_s, dqp_s) for t in pair]
        dproj = jnp.concatenate([*dqa_s, *dka_s, dva_s, *dqb_s, dck_s, dkp_s], axis=1)
        return (dproj,), (dgq, dgk, dgc)

    (dproj_b,), (d_g_q_a, d_g_k_a, d_g_ckv) = _rowwise(
        "prep_bwd", prep_bwd_fn, [proj, dqa, dka, dva, dqcat, d_ckvn, dkpe, *tabs_a, *tabs_b],
        [g_q_a, g_k_a, g_ckv], [(W_INP, BF16)], [HEAD_DIM, HEAD_DIM, KV_RANK], TB, after=(tok_kvb,))

    r_kvb, tok = _reduce_start(b_kvb, where, dproj_b)
    (dh,) = _mm_nn("d_proj_mm", dproj_b, wpt, [F32], after=(tok,))
    gwpt = _mm_tn("gw_in_mm", dproj_b, h_b, 1)[0]
    gw_in = _unpermute_w_in(gwpt).reshape(N_CHIPS, W_IN // N_CHIPS, D)
    b_in, tok = _reduce_begin("in", [gw_in])

    def pre_attn_bwd_fn(xt, dht, dx1t, g, sc, sh):
        _, vjp = jax.vjp(lambda a_, b_, c_, d_: pre_attn_fn(a_, b_, c_, d_)[0][0], xt, g, sc, sh)
        dxt, dg, dsc, dsh = vjp(dht)
        return (dx1t + dxt,), (dg, dsc, dsh)

    (grad_x,), (d_g_pre_attn, d_sc_a, d_sh_a) = _rowwise(
        "pre_attn_bwd", pre_attn_bwd_fn, [x2d, dh, dx1], [g_pre_attn, sc_a, sh_a], [(D, F32)], [D] * 3, TB,
        after=(tok,))

    small_names = ["b_ada", "g_pre_attn", "g_q_a", "g_k_a", "g_ckv", "g_out_a", "g_out_b", "g_post_attn",
                   "g_pre_mlp", "g_post_mlp"]
    small_local = jnp.concatenate([d_sh_a, d_sc_a, d_gt_a, d_sh_m, d_sc_m, d_gt_m, d_g_pre_attn, d_g_q_a, d_g_k_a,
                                   d_g_ckv, d_g_out_a, d_g_out_b, d_g_post_attn, d_g_pre_mlp, d_g_post_mlp], axis=1)
    n_small = small_local.shape[1]
    small_all = _allgather8("gather_small", small_local.reshape(8, n_small // 8)).reshape(N_DEV, n_small)

    def sum8_fn(t):
        return (), (jnp.sum(t, axis=0, keepdims=True),)

    r_in, tok_in = _reduce_start(b_in, where, grad_x, (small_all,))
    _, (small_sum,) = _rowwise("sum_small", sum8_fn, [small_all], [], [], [n_small], N_DEV, after=(tok_in,))
    dmod_cols = lax.dynamic_slice(small_all, (0, j_me * n_mod), (N_DEV, n_mod))
    dmod_pad = jnp.concatenate([dmod_cols, jnp.zeros_like(dmod_cols)], axis=0).astype(BF16)
    gw_ada = _mm_tn("gw_ada_mm", ca_pad, dmod_pad, 1)[0]

    weights = dict(w_ada=w_ada, b_ada=b_ada, g_pre_attn=g_pre_attn, w_in=w_in, g_q_a=g_q_a, g_k_a=g_k_a, g_ckv=g_ckv,
                   w_kv_b=w_kv_b, g_out_a=g_out_a, g_out_b=g_out_b, w_out=w_out, g_post_attn=g_post_attn,
                   g_pre_mlp=g_pre_mlp, w_mlp_in=w_mlp_in, w_mlp_out=w_mlp_out, g_post_mlp=g_post_mlp)
    ms = dict(w_ada=m_w_ada, b_ada=m_b_ada, g_pre_attn=m_g_pre_attn, w_in=m_w_in, g_q_a=m_g_q_a, g_k_a=m_g_k_a,
              g_ckv=m_g_ckv, w_kv_b=m_w_kv_b, g_out_a=m_g_out_a, g_out_b=m_g_out_b, w_out=m_w_out,
              g_post_attn=m_g_post_attn, g_pre_mlp=m_g_pre_mlp, w_mlp_in=m_w_mlp_in, w_mlp_out=m_w_mlp_out,
              g_post_mlp=m_g_post_mlp)
    vs = dict(w_ada=v_w_ada, b_ada=v_b_ada, g_pre_attn=v_g_pre_attn, w_in=v_w_in, g_q_a=v_g_q_a, g_k_a=v_g_k_a,
              g_ckv=v_g_ckv, w_kv_b=v_w_kv_b, g_out_a=v_g_out_a, g_out_b=v_g_out_b, w_out=v_w_out,
              g_post_attn=v_g_post_attn, g_pre_mlp=v_g_pre_mlp, w_mlp_in=v_w_mlp_in, w_mlp_out=v_w_mlp_out,
              g_post_mlp=v_g_post_mlp)
    order = list(weights)
    grads, deltas, new_m, new_v = {}, {}, {}, {}

    def update(name, g, after=()):
        view = (lambda t: jnp.swapaxes(t, 0, 1)) if name == "w_in" else (lambda t: t)
        d_, m_, v_ = _adamw("adamw_" + name, view(weights[name][0]), g, view(ms[name][0]), view(vs[name][0]), after)
        grads[name], deltas[name], new_m[name], new_v[name] = (view(t)[None] for t in (g, d_, m_, v_))
        return d_

    chain = (("w_mlp_out", r_down), ("w_mlp_in", r_up), ("w_out", r_out), ("w_kv_b", r_kvb), ("w_in", r_in))
    swap, tok = _reduce_finish(chain[0][1], where, small_sum)
    last = update("w_ada", gw_ada, (tok,))
    for k, (name, _) in enumerate(chain):
        (g,) = _reduce_end(swap, last)
        if k + 1 < len(chain):
            swap, tok = _reduce_finish(chain[k + 1][1], where, g)
        last = update(name, g, (tok,))

    cat = lambda d: jnp.concatenate([d[n] for n in small_names], axis=1)
    d_, m_, v_ = _adamw("adamw_small", cat(weights), small_sum, cat(ms), cat(vs))
    off = 0
    for n in small_names:
        w_ = weights[n].shape[1]
        grads[n], deltas[n], new_m[n], new_v[n] = (a[:, off:off + w_] for a in (small_sum, d_, m_, v_))
        off += w_

    loss = lax.psum(loss_v[0, 0], ("x", "y", "c"))
    return (loss, grad_x[None], *[grads[n] for n in order], *[deltas[n] for n in order],
            *[new_m[n] for n in order], *[new_v[n] for n in order])
```

```python
import functools
import math

import numpy as np
import jax
import jax.numpy as jnp
from jax import lax
from jax.experimental import pallas as pl
from jax.experimental.pallas import tpu as pltpu

F32 = jnp.float32
BF16 = jnp.bfloat16
MESH = pl.DeviceIdType.MESH

EPS = 1e-6
GRID_W = 64
ROPE_THETA = 10000.0
HEAD_DIM = 128
HA = 8
HKV = 2
HB = 8
QK_NOPE = 128
QK_ROPE = 64
V_DIM = 128
KV_RANK = 512
QK_B = QK_NOPE + QK_ROPE
W_QA = HA * HEAD_DIM
W_KA = HKV * HEAD_DIM
W_QB = HB * QK_B
W_IN = W_QA + 2 * W_KA + W_QB + KV_RANK + QK_ROPE
N_CHIPS = 4
N_DEV = 8

LANE = 128
O_QA = 0
O_KA = O_QA + W_QA
O_VA = O_KA + W_KA
O_QN = O_VA + W_KA
O_QP = O_QN + HB * QK_NOPE
O_CKV = O_QP + HB * LANE
O_KPE = O_CKV + KV_RANK
W_INP = O_KPE + LANE

ADAM_LR = 0.001
ADAM_B1 = 0.9
ADAM_B2 = 0.999
ADAM_EPS = 1e-08
ADAM_WD = 0.01
ADAM_STEP = 10

VMEM_LIMIT = 56 * 1024 * 1024
HEADS_PER_STEP = 2


def _params(sem):
    return pltpu.CompilerParams(dimension_semantics=sem, vmem_limit_bytes=VMEM_LIMIT)


def _pick(dim, target, unit):
    best = None
    t = unit
    while t <= min(dim, target):
        if dim % t == 0:
            best = t
        t += unit
    return dim if best is None else best


def _rn(x):
    return x * lax.rsqrt(jnp.mean(x * x, axis=-1, keepdims=True) + EPS)


def _rowwise(name, fn, rows, vecs, row_outs, vec_outs, tile, after=()):
    S = rows[0].shape[0]
    assert S % tile == 0
    n_r, n_v, n_ro, n_vo, n_a = len(rows), len(vecs), len(row_outs), len(vec_outs), len(after)

    def body(*refs):
        r_in = refs[:n_r]
        v_in = refs[n_r:n_r + n_v]
        r_out = refs[n_r + n_v + n_a:n_r + n_v + n_a + n_ro]
        v_out = refs[n_r + n_v + n_a + n_ro:]
        ro, vo = fn(*[r[...] for r in r_in], *[v[...] for v in v_in])
        for ref, val in zip(r_out, ro):
            ref[...] = val.astype(ref.dtype)
        if n_vo:
            @pl.when(pl.program_id(0) == 0)
            def _():
                for ref in v_out:
                    ref[...] = jnp.zeros(ref.shape, ref.dtype)
            for ref, val in zip(v_out, vo):
                ref[...] += val

    in_specs = [pl.BlockSpec((tile, a.shape[1]), lambda i: (i, 0)) for a in rows]
    in_specs += [pl.BlockSpec(a.shape, lambda i: (0, 0)) for a in vecs]
    in_specs += [pl.BlockSpec(memory_space=pl.ANY) for _ in after]
    out_specs = [pl.BlockSpec((tile, w), lambda i: (i, 0)) for w, _ in row_outs]
    out_specs += [pl.BlockSpec((1, w), lambda i: (0, 0)) for w in vec_outs]
    out_shape = [jax.ShapeDtypeStruct((S, w), dt) for w, dt in row_outs]
    out_shape += [jax.ShapeDtypeStruct((1, w), F32) for w in vec_outs]
    res = pl.pallas_call(
        body, name=name, grid=(S // tile,), in_specs=in_specs, out_specs=out_specs, out_shape=out_shape,
        compiler_params=_params(("arbitrary",)),
    )(*rows, *vecs, *after)
    return res[:n_ro], res[n_ro:]


CONTRACT_TILE = 2048
MM_TILE_BYTES = 40 * 1024 * 1024


def _rows_that_fit(M, tile_bytes):
    t = _pick(M, 1024, 16)
    while t % 32 == 0 and tile_bytes(t) > MM_TILE_BYTES:
        t //= 2
    return t


def _accumulate(acc, part, step, n_steps, finish):
    if n_steps == 1:
        finish(part)
        return

    @pl.when(step == 0)
    def _():
        acc[...] = part

    @pl.when(step > 0)
    def _():
        acc[...] += part

    @pl.when(step == n_steps - 1)
    def _():
        finish(acc[...])


def _mm_nn(name, a, b3, out_dtypes, epilogue=None, extras=(), after=()):
    M, K = a.shape
    nB, K2, Nsh = b3.shape
    assert K == K2
    N = nB * Nsh
    tn, tk = _pick(Nsh, 1536, LANE), _pick(K, CONTRACT_TILE, LANE)
    out_bytes = sum(jnp.dtype(dt).itemsize for dt in out_dtypes) + sum(e.dtype.itemsize for e in extras)
    tm = _rows_that_fit(M, lambda t: 2 * (t * tk * a.dtype.itemsize + tk * tn * b3.dtype.itemsize)
                        + t * tn * (4 * (K > tk) + 2 * out_bytes))
    nps, nk, ne, no, na = Nsh // tn, K // tk, len(extras), len(out_dtypes), len(after)

    def body(a_ref, b_ref, *rest):
        e_refs, o_refs = rest[:ne], rest[ne + na:ne + na + no]

        def finish(total):
            vals = (total,) if epilogue is None else epilogue(total, *[e[...] for e in e_refs])
            for ref, val in zip(o_refs, vals):
                ref[...] = val.astype(ref.dtype)

        part = jnp.dot(a_ref[...].astype(BF16), b_ref[...].astype(BF16), preferred_element_type=F32)
        _accumulate(rest[-1], part, pl.program_id(2), nk, finish)

    in_specs = [pl.BlockSpec((tm, tk), lambda i, j, k: (i, k)),
                pl.BlockSpec((None, tk, tn), lambda i, j, k: (j // nps, k, j % nps))]
    in_specs += [pl.BlockSpec((tm, tn), lambda i, j, k: (i, j)) for _ in extras]
    in_specs += [pl.BlockSpec(memory_space=pl.ANY) for _ in after]
    return pl.pallas_call(
        body, name=name, grid=(M // tm, N // tn, nk), in_specs=in_specs,
        out_specs=[pl.BlockSpec((tm, tn), lambda i, j, k: (i, j)) for _ in out_dtypes],
        out_shape=[jax.ShapeDtypeStruct((M, N), dt) for dt in out_dtypes],
        scratch_shapes=[pltpu.VMEM((tm, tn), F32)] if nk > 1 else [],
        compiler_params=_params(("parallel", "parallel", "arbitrary")),
    )(a, b3, *extras, *after)


def _mm_nt(name, a, b3, out_dtypes, epilogue=None, extras=(), after=()):
    M, N = a.shape
    nB, K, Nsh = b3.shape
    assert N == nB * Nsh
    to, tn = _pick(K, 1536, LANE), _pick(Nsh, CONTRACT_TILE, LANE)
    out_bytes = sum(jnp.dtype(dt).itemsize for dt in out_dtypes) + sum(e.dtype.itemsize for e in extras)
    tm = _rows_that_fit(M, lambda t: 2 * (t * tn * a.dtype.itemsize + to * tn * b3.dtype.itemsize)
                        + t * to * (4 * (N > tn) + 2 * out_bytes))
    nps, nn, ne, no, na = Nsh // tn, N // tn, len(extras), len(out_dtypes), len(after)

    def body(a_ref, b_ref, *rest):
        e_refs, o_refs = rest[:ne], rest[ne + na:ne + na + no]

        def finish(total):
            vals = (total,) if epilogue is None else epilogue(total, *[e[...] for e in e_refs])
            for ref, val in zip(o_refs, vals):
                ref[...] = val.astype(ref.dtype)

        part = lax.dot_general(a_ref[...].astype(BF16), b_ref[...].astype(BF16),
                               (((1,), (1,)), ((), ())), preferred_element_type=F32)
        _accumulate(rest[-1], part, pl.program_id(2), nn, finish)

    in_specs = [pl.BlockSpec((tm, tn), lambda i, o, n: (i, n)),
                pl.BlockSpec((None, to, tn), lambda i, o, n: (n // nps, o, n % nps))]
    in_specs += [pl.BlockSpec((tm, to), lambda i, o, n: (i, o)) for _ in extras]
    in_specs += [pl.BlockSpec(memory_space=pl.ANY) for _ in after]
    return pl.pallas_call(
        body, name=name, grid=(M // tm, K // to, nn), in_specs=in_specs,
        out_specs=[pl.BlockSpec((tm, to), lambda i, o, n: (i, o)) for _ in out_dtypes],
        out_shape=[jax.ShapeDtypeStruct((M, K), dt) for dt in out_dtypes],
        scratch_shapes=[pltpu.VMEM((tm, to), F32)] if nn > 1 else [],
        compiler_params=_params(("parallel", "parallel", "arbitrary")),
    )(a, b3, *extras, *after)


def _mm_tn(name, a, g, nB, after=()):
    M, K = a.shape
    M2, N = g.shape
    assert M == M2 and N % nB == 0
    Nsh = N // nB
    tk, tn, tm = _pick(K, 1536, LANE), _pick(Nsh, 1024, LANE), _pick(M, CONTRACT_TILE, 16)
    nps, nm = Nsh // tn, M // tm

    def body(a_ref, g_ref, *rest):
        o_ref = rest[len(after)]

        def finish(total):
            o_ref[...] = total

        part = lax.dot_general(a_ref[...].astype(BF16), g_ref[...].astype(BF16),
                               (((0,), (0,)), ((), ())), preferred_element_type=F32)
        _accumulate(rest[-1], part, pl.program_id(2), nm, finish)

    return pl.pallas_call(
        body, name=name, grid=(K // tk, N // tn, nm),
        in_specs=[pl.BlockSpec((tm, tk), lambda k, n, m: (m, k)),
                  pl.BlockSpec((tm, tn), lambda k, n, m: (m, n))] + [pl.BlockSpec(memory_space=pl.ANY) for _ in after],
        out_specs=pl.BlockSpec((None, tk, tn), lambda k, n, m: (n // nps, k, n % nps)),
        out_shape=jax.ShapeDtypeStruct((nB, K, Nsh), F32),
        scratch_shapes=[pltpu.VMEM((tk, tn), F32)] if nm > 1 else [],
        compiler_params=_params(("parallel", "parallel", "arbitrary")),
    )(a, g, *after)


def _mm_tn_rows(name, a, g, n_rows, pieces):
    M, K = a.shape
    M2, N = g.shape
    assert M == M2
    tk, tn, tm = _pick(K, 1536, LANE), _pick(N, 512, LANE), _pick(M, CONTRACT_TILE, 16)
    nk, nm = K // tk, M // tm

    def body(a_ref, g_ref, o_ref, *acc):
        kk = pl.program_id(1)

        def finish(total):
            for t in range(nk):
                @pl.when(kk == t)
                def _(t=t):
                    for src, dst, n in pieces:
                        lo, hi = max(src, t * tk), min(src + n, (t + 1) * tk)
                        if lo < hi:
                            o_ref[dst + lo - src:dst + hi - src, :] = total[lo - t * tk:hi - t * tk, :]

        part = lax.dot_general(a_ref[...].astype(BF16), g_ref[...].astype(BF16),
                               (((0,), (0,)), ((), ())), preferred_element_type=F32)
        _accumulate(acc[0] if acc else None, part, pl.program_id(2), nm, finish)

    return pl.pallas_call(
        body, name=name, grid=(N // tn, nk, nm),
        in_specs=[pl.BlockSpec((tm, tk), lambda n, k, m: (m, k)),
                  pl.BlockSpec((tm, tn), lambda n, k, m: (m, n))],
        out_specs=pl.BlockSpec((n_rows, tn), lambda n, k, m: (0, n)),
        out_shape=jax.ShapeDtypeStruct((n_rows, N), F32),
        scratch_shapes=[pltpu.VMEM((tk, tn), F32)] if nm > 1 else [],
        compiler_params=_params(("arbitrary", "arbitrary", "arbitrary")),
    )(a, g)


def _attn_fwd(name, q, k, v, n_heads, group, dqk, dv, scale, tq, after=()):
    S = q.shape[0]
    hp = 2 * HEADS_PER_STEP
    kvp = max(1, hp // group)

    def body(q_ref, k_ref, v_ref, *rest):
        o_ref, lse_ref = rest[-2:]
        for j in range(hp):
            jk = j * kvp // hp
            s = lax.dot_general(q_ref[:, j * dqk:(j + 1) * dqk], k_ref[:, jk * dqk:(jk + 1) * dqk],
                                (((1,), (1,)), ((), ())), preferred_element_type=F32)
            m = jnp.max(s, axis=-1, keepdims=True)
            e = jnp.exp((s - m) * scale)
            l = jnp.sum(e, axis=-1, keepdims=True)
            o_ref[:, j * dv:(j + 1) * dv] = jnp.dot(e.astype(BF16), v_ref[:, jk * dv:(jk + 1) * dv],
                                                    preferred_element_type=F32) * (1.0 / l)
            lse_ref[:, j * LANE:(j + 1) * LANE] = jnp.broadcast_to(m * scale + jnp.log(l), (tq, LANE))

    kv_block = lambda h, i: (0, (h * hp // group) // kvp)
    return pl.pallas_call(
        body, name=name, grid=(n_heads // hp, S // tq),
        in_specs=[pl.BlockSpec((tq, hp * dqk), lambda h, i: (i, h)),
                  pl.BlockSpec((S, kvp * dqk), kv_block),
                  pl.BlockSpec((S, kvp * dv), kv_block)] + [pl.BlockSpec(memory_space=pl.ANY) for _ in after],
        out_specs=[pl.BlockSpec((tq, hp * dv), lambda h, i: (i, h)),
                   pl.BlockSpec((tq, hp * LANE), lambda h, i: (i, h))],
        out_shape=[jax.ShapeDtypeStruct((S, n_heads * dv), F32), jax.ShapeDtypeStruct((S, n_heads * LANE), F32)],
        compiler_params=_params(("arbitrary", "arbitrary")),
    )(q, k, v, *after)


def _attn_bwd(name, q, k, v, do, o, lse, n_heads, group, dqk, dv, scale, tq, after=()):
    S = q.shape[0]
    n_kv = n_heads // group
    hp = HEADS_PER_STEP
    kvp = max(1, hp // group)

    def body(q_ref, k_ref, v_ref, do_ref, o_ref, lse_ref, *rest):
        dq_ref, dk_ref, dv_ref = rest[len(after):]
        h, i = pl.program_id(0), pl.program_id(1)

        @pl.when(((h * hp) % group == 0) & (i == 0))
        def _():
            dk_ref[...] = jnp.zeros(dk_ref.shape, F32)
            dv_ref[...] = jnp.zeros(dv_ref.shape, F32)

        for j in range(hp):
            jk = j * kvp // hp
            qb, dob = q_ref[:, j * dqk:(j + 1) * dqk], do_ref[:, j * dv:(j + 1) * dv]
            kb, vb = k_ref[:, jk * dqk:(jk + 1) * dqk], v_ref[:, jk * dv:(jk + 1) * dv]
            s = lax.dot_general(qb, kb, (((1,), (1,)), ((), ())), preferred_element_type=F32)
            p = jnp.exp(s * scale - jnp.tile(lse_ref[:, j * LANE:(j + 1) * LANE], (1, S // LANE)))
            dp = lax.dot_general(dob, vb, (((1,), (1,)), ((), ())), preferred_element_type=F32)
            delta = jnp.sum(dob.astype(F32) * o_ref[:, j * dv:(j + 1) * dv], axis=-1, keepdims=True)
            ds = (p * ((dp - delta) * scale)).astype(BF16)
            dv_ref[:, jk * dv:(jk + 1) * dv] += lax.dot_general(p.astype(BF16), dob, (((0,), (0,)), ((), ())),
                                                                preferred_element_type=F32)
            dq_ref[:, j * dqk:(j + 1) * dqk] = jnp.dot(ds, kb, preferred_element_type=F32)
            dk_ref[:, jk * dqk:(jk + 1) * dqk] += lax.dot_general(ds, qb, (((0,), (0,)), ((), ())),
                                                                  preferred_element_type=F32)

    kv_block = lambda h, i: (0, (h * hp // group) // kvp)
    return pl.pallas_call(
        body, name=name, grid=(n_heads // hp, S // tq),
        in_specs=[pl.BlockSpec((tq, hp * dqk), lambda h, i: (i, h)),
                  pl.BlockSpec((S, kvp * dqk), kv_block),
                  pl.BlockSpec((S, kvp * dv), kv_block),
                  pl.BlockSpec((tq, hp * dv), lambda h, i: (i, h)),
                  pl.BlockSpec((tq, hp * dv), lambda h, i: (i, h)),
                  pl.BlockSpec((tq, hp * LANE), lambda h, i: (i, h))] + [pl.BlockSpec(memory_space=pl.ANY) for _ in after],
        out_specs=[pl.BlockSpec((tq, hp * dqk), lambda h, i: (i, h)),
                   pl.BlockSpec((S, kvp * dqk), kv_block),
                   pl.BlockSpec((S, kvp * dv), kv_block)],
        out_shape=[jax.ShapeDtypeStruct((S, n_heads * dqk), F32),
                   jax.ShapeDtypeStruct((S, n_kv * dqk), F32),
                   jax.ShapeDtypeStruct((S, n_kv * dv), F32)],
        compiler_params=_params(("arbitrary", "arbitrary")),
    )(q, k, v, do, o, lse, *after)


def _adamw(name, w, g, m, v, after=()):
    R, C = w.shape
    tile = _pick(R, max(8, (1 << 19) // C // 8 * 8), 8)
    c1 = 1.0 - ADAM_B1 ** ADAM_STEP
    c2 = 1.0 - ADAM_B2 ** ADAM_STEP

    def body(w_ref, g_ref, m_ref, v_ref, *rest):
        d_ref, nm_ref, nv_ref = rest[len(after):]
        gg = g_ref[...]
        nm = ADAM_B1 * m_ref[...] + (1.0 - ADAM_B1) * gg
        nv = ADAM_B2 * v_ref[...] + (1.0 - ADAM_B2) * (gg * gg)
        d_ref[...] = -ADAM_LR * ((nm / c1) / (jnp.sqrt(nv / c2) + ADAM_EPS) + ADAM_WD * w_ref[...])
        nm_ref[...] = nm
        nv_ref[...] = nv

    spec = pl.BlockSpec((tile, C), lambda i: (i, 0))
    return pl.pallas_call(
        body, name=name, grid=(R // tile,), in_specs=[spec] * 4 + [pl.BlockSpec(memory_space=pl.ANY) for _ in after],
        out_specs=[spec] * 3, out_shape=[jax.ShapeDtypeStruct((R, C), F32)] * 3,
        compiler_params=_params(("parallel",)),
    )(w, g, m, v, *after)


def _flip(me, rel):
    return tuple(1 - p if r else p for p, r in zip(me, rel))


def _push(name, ins, out_shapes, transfers, aliases=None):
    ni, no, nt = len(ins), len(out_shapes), len(transfers)

    def body(*refs):
        in_refs, out_refs = refs[:ni], refs[ni:ni + no]
        send_sems, recv_sems = refs[ni + no], refs[ni + no + 1]
        me = (lax.axis_index("x"), lax.axis_index("y"), lax.axis_index("c"))
        copies = []
        for t, (si, sfn, oi, dfn, rel) in enumerate(transfers):
            src_ref = out_refs[si[1]] if isinstance(si, tuple) else in_refs[si]
            src = src_ref.at[sfn(*me)]
            dst = out_refs[oi].at[dfn(*me)]
            if rel is None:
                cp = pltpu.make_async_copy(src, dst, send_sems.at[t])
            else:
                cp = pltpu.make_async_remote_copy(src_ref=src, dst_ref=dst, send_sem=send_sems.at[t],
                                                  recv_sem=recv_sems.at[t], device_id=_flip(me, rel),
                                                  device_id_type=MESH)
            cp.start()
            copies.append(cp)
        for cp in copies:
            cp.wait()

    any_spec = pl.BlockSpec(memory_space=pl.ANY)
    return pl.pallas_call(
        body, name=name, in_specs=[any_spec] * ni, out_specs=[any_spec] * no, out_shape=out_shapes,
        scratch_shapes=[pltpu.SemaphoreType.DMA((nt,)), pltpu.SemaphoreType.DMA((nt,))],
        input_output_aliases=aliases or {},
        compiler_params=pltpu.CompilerParams(has_side_effects=True),
    )(*ins)


_HBM = pl.BlockSpec(memory_space=pltpu.HBM)
_SEM = pl.BlockSpec(memory_space=pltpu.SEMAPHORE)
_DATAFLOW = pltpu.SideEffectType.DATAFLOW_SIDE_EFFECTING


def _split_copies(refs, transfers, send_sems, recv_sems):
    me = (lax.axis_index("x"), lax.axis_index("y"), lax.axis_index("c"))
    return [pltpu.make_async_remote_copy(src_ref=refs[sb].at[sfn(*me)], dst_ref=refs[db].at[dfn(*me)],
                                         send_sem=send_sems.at[t], recv_sem=recv_sems.at[t],
                                         device_id=_flip(me, rel), device_id_type=MESH)
            for t, (sb, sfn, db, dfn, rel) in enumerate(transfers)]


def _push_start(name, bufs, transfers, after=()):
    nb, na, nt = len(bufs), len(after), len(transfers)

    def body(*refs):
        send_sems, recv_sems, token = refs[nb + na], refs[nb + na + 1], refs[-1]
        for cp in _split_copies(refs[nb + na + 2:2 * nb + na + 2], transfers, send_sems, recv_sems):
            cp.start()
        token[...] = jnp.zeros(token.shape, token.dtype)

    res = pl.pallas_call(
        body, name=name,
        out_shape=(pltpu.SemaphoreType.DMA((nt,)), pltpu.SemaphoreType.DMA((nt,)),
                   *[pltpu.HBM(b.shape, b.dtype) for b in bufs], jax.ShapeDtypeStruct((8, LANE), F32)),
        in_specs=[_HBM] * nb + [pl.BlockSpec(memory_space=pl.ANY)] * na,
        out_specs=(_SEM, _SEM, *[_HBM] * nb, pl.BlockSpec(memory_space=pltpu.VMEM)),
        input_output_aliases={i: i + 2 for i in range(nb)},
        compiler_params=pltpu.CompilerParams(has_side_effects=_DATAFLOW),
    )(*[pltpu.with_memory_space_constraint(b, pltpu.HBM) for b in bufs], *after)
    return res[0], res[1], list(res[2:2 + nb]), res[-1]


def _push_wait(name, send_sems, recv_sems, bufs, transfers, after):
    nb = len(bufs)

    def body(*refs):
        for cp in _split_copies(refs[:nb], transfers, refs[nb], refs[nb + 1]):
            cp.wait_send()
            cp.wait_recv()

    res = pl.pallas_call(
        body, name=name, out_shape=[pltpu.HBM(b.shape, b.dtype) for b in bufs],
        in_specs=[_HBM] * nb + [_SEM, _SEM, pl.BlockSpec(memory_space=pl.ANY)], out_specs=[_HBM] * nb,
        input_output_aliases={i: i for i in range(nb)},
        compiler_params=pltpu.CompilerParams(has_side_effects=_DATAFLOW),
    )(*bufs, send_sems, recv_sems, after)
    return list(res)


ICI_RELS = ((1, 0, 0), (0, 1, 0), (1, 1, 0))
SIBLING = (0, 0, 1)


def _chip(x, y):
    return 2 * x + y


def _cast_into_slot(name, where, w, after=()):
    R, C = w.shape
    tile = _pick(R, 512, BF16_ROWS)

    def body(where_ref, w_ref, *rest):
        rest[-1][...] = w_ref[...].astype(BF16)

    return pl.pallas_call(
        body, name=name,
        grid_spec=pltpu.PrefetchScalarGridSpec(
            num_scalar_prefetch=1, grid=(R // tile,),
            in_specs=[pl.BlockSpec((tile, C), lambda i, wh: (i, 0))] + [pl.BlockSpec(memory_space=pl.ANY) for _ in after],
            out_specs=pl.BlockSpec((None, tile, C), lambda i, wh: (wh[1], i, 0))),
        out_shape=jax.ShapeDtypeStruct((N_CHIPS, R, C), BF16),
        compiler_params=_params(("parallel",)),
    )(where, w, *after)


BF16_ROWS = 16


def _rows_split(shape2):
    return (shape2[0] // 2) % BF16_ROWS == 0


def _half_shape(shape2):
    R, C = shape2
    return (R // 2, C) if _rows_split(shape2) else (R, C // 2)


def _half(shape2, c):
    R, C = shape2
    if _rows_split(shape2):
        return (pl.ds(c * (R // 2), R // 2), slice(None))
    return (slice(None), pl.ds(pl.multiple_of(c * (C // 2), LANE), C // 2))


def _gather_start(tag, where, shards, after, cast_after=()):
    n = len(shards)
    own = [_cast_into_slot(f"cast_w_{tag}{w}", where, shards[w], cast_after) for w in range(n)]
    transfers = []
    for w in range(n):
        def idx(x, y, c, shape2=shards[w].shape):
            return (_chip(x, y), *_half(shape2, c))
        transfers += [(w, idx, w, idx, rel) for rel in ICI_RELS]
    send_sems, recv_sems, bufs, token = _push_start(f"gather_{tag}_start", own, transfers, after)
    return (tag, send_sems, recv_sems, bufs, transfers), token


def _forward_transfers(part):
    second = []
    for w, p in enumerate(part):
        for rel in ICI_RELS:
            def idx(x, y, c, shape2=p.shape[1:], rel=rel):
                return (_chip(x ^ rel[0], y ^ rel[1]), *_half(shape2, c))
            second.append((w, idx, w, idx, SIBLING))
    return second


def _gather_finish(handle, after):
    tag, send_sems, recv_sems, bufs, transfers = handle
    part = _push_wait(f"gather_{tag}_wait", send_sems, recv_sems, bufs, transfers, after)
    outs = [jax.ShapeDtypeStruct(p.shape, p.dtype) for p in part]
    second = [(("out", sb), sfn, db, dfn, rel) for sb, sfn, db, dfn, rel in _forward_transfers(part)]
    return _push(f"gather_{tag}_d2d", list(part), outs, second, aliases={w: w for w in range(len(part))})


def _gather_mid(handle, after):
    tag, send_sems, recv_sems, bufs, transfers = handle
    part = _push_wait(f"gather_{tag}_wait", send_sems, recv_sems, bufs, transfers, after)
    second = _forward_transfers(part)
    s2, r2, bufs2, token = _push_start(f"gather_{tag}_d2d_start", part, second)
    return (tag, s2, r2, bufs2, second), token


def _gather_end(handle, after):
    tag, send_sems, recv_sems, bufs, transfers = handle
    return _push_wait(f"gather_{tag}_d2d_wait", send_sems, recv_sems, bufs, transfers, after)


def _reduce_begin(tag, grads, after=()):
    n = len(grads)
    land = [lax.empty((N_CHIPS, *_half_shape(g.shape[1:])), F32) for g in grads]
    t1 = []
    for w in range(n):
        t1.append((w, lambda x, y, c, shape2=grads[w].shape[1:]: (slice(None), *_half(shape2, 1 - c)), n + w,
                   lambda x, y, c: (slice(None), slice(None), slice(None)), SIBLING))
    send_sems, recv_sems, bufs, token = _push_start(f"reduce_{tag}_d2d_start", list(grads) + land, t1, after)
    return (tag, send_sems, recv_sems, bufs, t1), token


def _reduce_start(handle, where, after, after_start=()):
    tag, send_sems, recv_sems, bufs, t1 = handle
    n = len(bufs) // 2
    res = _push_wait(f"reduce_{tag}_d2d_wait", send_sems, recv_sems, bufs, t1, after)
    grads, from_sib = res[:n], res[n:]
    pair = [_pair_sum(f"pair_sum_{tag}{w}", where, grads[w], from_sib[w]) for w in range(n)]
    land = [lax.empty((3, *_half_shape(g.shape[1:])), BF16) for g in grads]
    t3 = []
    for w in range(n):
        for r, rel in enumerate(ICI_RELS):
            t3.append((w, lambda x, y, c, rel=rel: (_chip(x ^ rel[0], y ^ rel[1]), slice(None), slice(None)), n + w,
                       lambda x, y, c, r=r: (r, slice(None), slice(None)), rel))
    send_sems, recv_sems, bufs, token = _push_start(f"reduce_{tag}_start", pair + land, t3, after_start)
    return (tag, send_sems, recv_sems, bufs, t3, list(grads), list(from_sib)), token


def _reduce_finish(handle, where, after):
    tag, send_sems, recv_sems, bufs, t3, grads, from_sib = handle
    n = len(grads)
    from_chips = _push_wait(f"reduce_{tag}_wait", send_sems, recv_sems, bufs, t3, after)[n:]
    mine = [_chip_sum(f"chip_sum_{tag}{w}", where, grads[w], from_sib[w], from_chips[w]) for w in range(n)]
    t5 = []
    for w in range(n):
        def idx(x, y, c, shape2=grads[w].shape[1:]):
            return _half(shape2, c)
        t5.append((w, idx, w, idx, SIBLING))
    send_sems, recv_sems, bufs, token = _push_start(f"reduce_{tag}_swap_start", mine, t5)
    return (tag, send_sems, recv_sems, bufs, t5), token


def _reduce_end(handle, after):
    tag, send_sems, recv_sems, bufs, t5 = handle
    return _push_wait(f"reduce_{tag}_swap_wait", send_sems, recv_sems, bufs, t5, after)


def _pair_sum(name, where, g, from_sib):
    _, hr, hc = from_sib.shape
    tile = _pick(hr, max(16, (1 << 19) // hc // 16 * 16), 16)
    nb = hr // tile
    slot = lambda k, wh: (wh[1] + 1 + k) % N_CHIPS
    if _rows_split(g.shape[1:]):
        mine = lambda k, i, wh: (slot(k, wh), wh[0] * nb + i, 0)
    else:
        mine = lambda k, i, wh: (slot(k, wh), i, wh[0])

    def body(where_ref, g_ref, s_ref, o_ref):
        o_ref[...] = (g_ref[...] + s_ref[...]).astype(BF16)

    return pl.pallas_call(
        body, name=name,
        grid_spec=pltpu.PrefetchScalarGridSpec(
            num_scalar_prefetch=1, grid=(N_CHIPS - 1, nb),
            in_specs=[pl.BlockSpec((None, tile, hc), mine),
                      pl.BlockSpec((None, tile, hc), lambda k, i, wh: (slot(k, wh), i, 0))],
            out_specs=pl.BlockSpec((None, tile, hc), lambda k, i, wh: (slot(k, wh), i, 0))),
        out_shape=jax.ShapeDtypeStruct((N_CHIPS, hr, hc), BF16),
        compiler_params=_params(("parallel", "parallel")),
    )(where, g, from_sib)


def _chip_sum(name, where, g, from_sib, from_chips):
    _, R, C = g.shape
    _, hr, hc = from_sib.shape
    tile = _pick(hr, max(16, (1 << 19) // hc // 16 * 16), 16)
    nb = hr // tile
    if _rows_split((R, C)):
        mine = lambda i, wh: (wh[0] * nb + i, 0)
    else:
        mine = lambda i, wh: (i, wh[0])

    def body(where_ref, g_ref, s_ref, r_ref, o_ref):
        acc = g_ref[...] + s_ref[...]
        for r in range(3):
            acc = acc + r_ref[r].astype(F32)
        o_ref[...] = acc

    return pl.pallas_call(
        body, name=name,
        grid_spec=pltpu.PrefetchScalarGridSpec(
            num_scalar_prefetch=1, grid=(nb,),
            in_specs=[pl.BlockSpec((None, tile, hc), lambda i, wh: (wh[1], *mine(i, wh))),
                      pl.BlockSpec((None, tile, hc), lambda i, wh: (wh[1], i, 0)),
                      pl.BlockSpec((3, tile, hc), lambda i, wh: (0, i, 0))],
            out_specs=pl.BlockSpec((tile, hc), mine)),
        out_shape=jax.ShapeDtypeStruct((R, C), F32),
        compiler_params=_params(("parallel",)),
    )(where, g, from_sib, from_chips)


def _allgather8(name, blk):
    m_per, n = blk.shape
    rels = [(0, 0, 1), (1, 0, 0), (0, 1, 0), (1, 1, 0), (1, 0, 1), (0, 1, 1), (1, 1, 1)]

    def body(x_ref, out_ref, send_sems, recv_sems):
        me = (lax.axis_index("x"), lax.axis_index("y"), lax.axis_index("c"))
        my_rows = out_ref.at[pl.ds((4 * me[0] + 2 * me[1] + me[2]) * m_per, m_per), :]
        out_ref[pl.ds((4 * me[0] + 2 * me[1] + me[2]) * m_per, m_per), :] = x_ref[...]
        copies = []
        for t, rel in enumerate(rels):
            cp = pltpu.make_async_remote_copy(src_ref=x_ref, dst_ref=my_rows, send_sem=send_sems.at[t],
                                              recv_sem=recv_sems.at[t], device_id=_flip(me, rel),
                                              device_id_type=MESH)
            cp.start()
            copies.append(cp)
        for cp in copies:
            cp.wait()

    return pl.pallas_call(
        body, name=name, out_shape=jax.ShapeDtypeStruct((N_DEV * m_per, n), blk.dtype),
        in_specs=[pl.BlockSpec(memory_space=pltpu.VMEM)], out_specs=pl.BlockSpec(memory_space=pltpu.VMEM),
        scratch_shapes=[pltpu.SemaphoreType.DMA((7,)), pltpu.SemaphoreType.DMA((7,))],
        compiler_params=pltpu.CompilerParams(has_side_effects=True, vmem_limit_bytes=VMEM_LIMIT),
    )(blk)


def _rope_tables(seq_len, dim, reps):
    rows = seq_len // GRID_W
    t = np.arange(seq_len)
    row, col = (t // GRID_W).astype(np.float32), (t % GRID_W).astype(np.float32)
    half = dim // 2
    inv = jnp.asarray(ROPE_THETA, F32) ** (-jnp.arange(0, half, 2, dtype=F32) / half)
    ang_r = jnp.asarray(row)[:, None] * inv[None, :]
    ang_c = jnp.asarray(col)[:, None] * inv[None, :]
    ang = jnp.concatenate([ang_r, ang_r, ang_c, ang_c], axis=-1)
    cos, sin = jnp.cos(ang), jnp.sin(ang)
    low = (np.arange(dim) % (dim // 2)) < (dim // 4)
    s_lo = jnp.where(jnp.asarray(low)[None, :], -sin, 0.0)
    s_hi = jnp.where(jnp.asarray(low)[None, :], 0.0, sin)
    del rows
    return tuple(jnp.tile(a, (1, reps)) for a in (cos, s_lo, s_hi))


def _make_rope(shift):
    def up(v):
        return pltpu.roll(v, LANE - shift, 1)

    def down(v):
        return pltpu.roll(v, shift, 1)

    @jax.custom_vjp
    def rope(v, cos, s_lo, s_hi):
        return v * cos + up(v) * s_lo + down(v) * s_hi

    def fwd(v, cos, s_lo, s_hi):
        return rope(v, cos, s_lo, s_hi), (cos, s_lo, s_hi)

    def bwd(res, dy):
        cos, s_lo, s_hi = res
        return dy * cos + down(dy * s_lo) + up(dy * s_hi), jnp.zeros_like(cos), jnp.zeros_like(cos), jnp.zeros_like(cos)

    rope.defvjp(fwd, bwd)
    return rope


_rope_a = _make_rope(HEAD_DIM // 4)
_rope_b = _make_rope(QK_ROPE // 4)


def _prep(segs, tabs, g_q, g_k, g_ckv):
    qa, ka, va, qn, qp, ckv, kpe = segs
    ta, tb = tabs
    q_a = jnp.concatenate([_rope_a(_rn(s) * g_q, *ta) for s in qa], axis=1)
    k_a = jnp.concatenate([_rope_a(_rn(s) * g_k, *ta) for s in ka], axis=1)
    cat = []
    for h in range(HB):
        cat += [qn[h], _rope_b(qp[h], *tb)]
    q_cat = jnp.concatenate(cat, axis=1)
    return q_a, k_a, va, q_cat, _rn(ckv) * g_ckv, _rope_b(kpe, *tb)


def _split_proj(proj):
    qa = [proj[:, O_QA + h * LANE:O_QA + (h + 1) * LANE] for h in range(HA)]
    ka = [proj[:, O_KA + h * LANE:O_KA + (h + 1) * LANE] for h in range(HKV)]
    va = proj[:, O_VA:O_QN]
    qn = [proj[:, O_QN + h * LANE:O_QN + (h + 1) * LANE] for h in range(HB)]
    qp = [proj[:, O_QP + h * LANE:O_QP + (h + 1) * LANE] for h in range(HB)]
    return qa, ka, va, qn, qp, proj[:, O_CKV:O_KPE], proj[:, O_KPE:W_INP]


def _permute_w_in(nat):
    K = nat.shape[1]
    qb = nat[W_QA + 2 * W_KA:W_QA + 2 * W_KA + W_QB].reshape(HB, QK_B, K)
    qp = jnp.concatenate([qb[:, QK_NOPE:], jnp.zeros((HB, LANE - QK_ROPE, K), nat.dtype)], axis=1)
    tail = nat[W_QA + 2 * W_KA + W_QB:]
    return jnp.concatenate([nat[:O_QN], qb[:, :QK_NOPE].reshape(HB * QK_NOPE, K), qp.reshape(HB * LANE, K),
                            tail, jnp.zeros((LANE - QK_ROPE, K), nat.dtype)], axis=0)


def _unpermute_pieces():
    pieces = [(0, 0, O_QN)]
    for h in range(HB):
        pieces.append((O_QN + h * QK_NOPE, O_QN + h * QK_B, QK_NOPE))
        pieces.append((O_QP + h * LANE, O_QN + h * QK_B + QK_NOPE, QK_ROPE))
    pieces.append((O_CKV, O_QN + W_QB, KV_RANK))
    pieces.append((O_KPE, O_QN + W_QB + KV_RANK, QK_ROPE))
    return pieces


def kernel(x, c, w_ada, b_ada, g_pre_attn, w_in, g_q_a, g_k_a, g_ckv, w_kv_b, g_out_a, g_out_b, w_out, g_post_attn, g_pre_mlp, w_mlp_in, w_mlp_out, g_post_mlp, loss_target, m_w_ada, m_b_ada, m_g_pre_attn, m_w_in, m_g_q_a, m_g_k_a, m_g_ckv, m_w_kv_b, m_g_out_a, m_g_out_b, m_w_out, m_g_post_attn, m_g_pre_mlp, m_w_mlp_in, m_w_mlp_out, m_g_post_mlp, v_w_ada, v_b_ada, v_g_pre_attn, v_w_in, v_g_q_a, v_g_k_a, v_g_ckv, v_w_kv_b, v_g_out_a, v_g_out_b, v_w_out, v_g_post_attn, v_g_pre_mlp, v_w_mlp_in, v_w_mlp_out, v_g_post_mlp):
    S, D = x.shape[1], x.shape[2]
    x2d, tgt = x[0], loss_target[0]
    ix, iy, ic = lax.axis_index("x"), lax.axis_index("y"), lax.axis_index("c")
    j_me = _chip(ix, iy)
    e_me = 4 * ix + 2 * iy + ic
    T = _pick(S, 256, 8)
    TB = _pick(S, 128, 8)

    where = jnp.stack([ic, j_me]).astype(jnp.int32)
    DFF = w_mlp_out.shape[1] * N_CHIPS

    c_all = _allgather8("gather_c", c.reshape(8, D // 8)).reshape(N_DEV, D)

    def silu_fn(cc):
        return (cc * (1.0 / (1.0 + jnp.exp(-cc))),), ()

    (c_act,), _ = _rowwise("silu_c", silu_fn, [c_all], [], [(D, BF16)], [], N_DEV)
    ca_pad = jnp.concatenate([c_act, jnp.zeros_like(c_act)], axis=0)
    n_mod = w_ada.shape[2]
    b_cols = lax.dynamic_slice(b_ada, (0, j_me * n_mod), (1, n_mod))
    (mod_part,) = _mm_nn("mod_mm", ca_pad, w_ada, [F32], epilogue=lambda acc, b: (acc + b,),
                         extras=(jnp.broadcast_to(b_cols, (16, n_mod)),))
    mod_all = _allgather8("gather_mod", mod_part[:8]).reshape(N_DEV, 8, n_mod)
    mod = jnp.concatenate([lax.dynamic_slice(mod_all, (2 * j, e_me, 0), (1, 1, n_mod))[0] for j in range(N_CHIPS)],
                          axis=1)

    w_in_t = jnp.swapaxes(w_in[0], 0, 1)
    h_qkv, tok_a = _gather_start("qkv", where, [w_in_t, w_kv_b[0]], (mod_all,))
    h_out, tok_b = _gather_start("out", where, [w_out[0]], (tok_a,), (tok_a,))
    h_up, tok_c = _gather_start("up", where, [w_mlp_in[0]], (tok_b,), (tok_b,))
    h_down, tok_d = _gather_start("down", where, [w_mlp_out[0]], (tok_c,), (tok_c,))
    mod = mod + tok_d[0, 0]
    sh_a, sc_a, gt_a, sh_m, sc_m, gt_m = [mod[:, i * D:(i + 1) * D] for i in range(6)]

    tabs_a = _rope_tables(S, HEAD_DIM, 1)
    tabs_b = _rope_tables(S, QK_ROPE, LANE // QK_ROPE)

    def pre_attn_fn(xt, g, sc, sh):
        return (_rn(xt) * g * (1.0 + sc) + sh,), ()

    (h_b,), _ = _rowwise("pre_attn", pre_attn_fn, [x2d], [g_pre_attn, sc_a, sh_a], [(D, BF16)], [], T)
    g_in, g_kvb = _gather_finish(h_qkv, h_b)
    wpt = _permute_w_in(g_in.reshape(W_IN, D))[None]
    (proj,) = _mm_nt("proj_mm", h_b, wpt, [F32])

    def prep_fn(pt, ca_, sla, sha, cb_, slb, shb, gq, gk, gc):
        return _prep(_split_proj(pt), ((ca_, sla, sha), (cb_, slb, shb)), gq, gk, gc), ()

    (qa_b, ka_b, va_b, qcat_b, ckvn_b, kpe_b), _ = _rowwise(
        "prep", prep_fn, [proj, *tabs_a, *tabs_b], [g_q_a, g_k_a, g_ckv],
        [(W_QA, BF16), (W_KA, BF16), (W_KA, BF16), (2 * HB * LANE, BF16), (KV_RANK, BF16), (LANE, BF16)], [], TB)
    (kv,) = _mm_nn("kv_mm", ckvn_b, g_kvb, [F32])

    def kcat_fn(kvt, kpet):
        kc, vs = [], []
        for h in range(HB):
            kc += [kvt[:, 2 * h * LANE:(2 * h + 1) * LANE], kpet.astype(F32)]
            vs.append(kvt[:, (2 * h + 1) * LANE:(2 * h + 2) * LANE])
        return (jnp.concatenate(kc, axis=1), jnp.concatenate(vs, axis=1)), ()

    (kcat_b, vb_b), _ = _rowwise("kcat", kcat_fn, [kv, kpe_b], [], [(2 * HB * LANE, BF16), (HB * V_DIM, BF16)], [], T)

    TQ = _pick(S, 512, 16)
    sc_a_ = 1.0 / math.sqrt(HEAD_DIM)
    sc_b_ = 1.0 / math.sqrt(QK_B)
    o_a, lse_a = _attn_fwd("attn_a_fwd", qa_b, ka_b, va_b, HA, HA // HKV, HEAD_DIM, HEAD_DIM, sc_a_, TQ)
    m_out, tok_m = _gather_mid(h_out, o_a)
    m_up, tok_m = _gather_mid(h_up, tok_m)
    o_b, lse_b = _attn_fwd("attn_b_fwd", qcat_b, kcat_b, vb_b, HB, 1, 2 * LANE, V_DIM, sc_b_, TQ, after=(tok_m,))

    def mix_fn(oa, ob, ga, gb):
        return (jnp.concatenate([_rn(oa) * ga, _rn(ob) * gb], axis=1),), ()

    (on_b,), _ = _rowwise("mix_norm", mix_fn, [o_a, o_b], [g_out_a, g_out_b], [(2 * W_QA, BF16)], [], T)
    (g_wout,) = _gather_end(m_out, on_b)
    w_out_f = g_wout.reshape(1, N_CHIPS * g_wout.shape[1], D)
    (o2,) = _mm_nn("out_mm", on_b, w_out_f, [F32])

    def mid(xt, o2t, gta, gpa, gpm, scm, shm):
        x1 = xt + gta * (_rn(o2t) * gpa)
        return x1, _rn(x1) * gpm * (1.0 + scm) + shm

    def mid_fn(*a):
        return mid(*a), ()

    mid_vecs = [gt_a, g_post_attn, g_pre_mlp, sc_m, sh_m]
    (x1, h2_b), _ = _rowwise("mid", mid_fn, [x2d, o2], mid_vecs, [(D, F32), (D, BF16)], [], T)
    (g_mlp_in,) = _gather_end(m_up, h2_b)
    m_down, tok_m = _gather_mid(h_down, h2_b)
    u, a_b = _mm_nn("mlp_in_mm", h2_b, g_mlp_in, [F32, BF16],
                    epilogue=lambda acc: (acc, jnp.square(jnp.maximum(acc, 0.0))), after=(tok_m,))
    (g_mlp_out,) = _gather_end(m_down, a_b)
    w_mlp_out_f = g_mlp_out.reshape(1, DFF, D)
    (y,) = _mm_nn("mlp_out_mm", a_b, w_mlp_out_f, [F32])

    def last(yt, x1t, gtm, gpo):
        return x1t + gtm * (_rn(yt) * gpo)

    def last_fn(yt, x1t, tt, gtm, gpo):
        x2, vjp = jax.vjp(last, yt, x1t, gtm, gpo)
        err = x2 - tt
        dy, dx1, dgt, dg = vjp(err * (1.0 / D))
        loss = 0.5 * jnp.sum(jnp.mean(err * err, axis=-1))
        return (dy, dx1), (jnp.full((1, LANE), loss, F32), dgt, dg)

    (dy_b, dx2), (loss_v, d_gt_m, d_g_post_mlp) = _rowwise(
        "loss_bwd", last_fn, [y, x1, tgt], [gt_m, g_post_mlp], [(D, BF16), (D, F32)], [LANE, D, D], TB)

    gw_mlp_out = _mm_tn("gw_mlp_out_mm", a_b, dy_b, 1).reshape(N_CHIPS, DFF // N_CHIPS, D)
    b_down, tok = _reduce_begin("down", [gw_mlp_out])
    (du_b,) = _mm_nt("d_mlp_out_mm", dy_b, w_mlp_out_f, [BF16],
                     epilogue=lambda acc, ut: (acc * (2.0 * jnp.maximum(ut, 0.0)),), extras=(u,), after=(tok,))
    r_down, tok = _reduce_start(b_down, where, du_b)
    gw_mlp_in = _mm_tn("gw_mlp_in_mm", h2_b, du_b, N_CHIPS, after=(tok,))
    b_up, tok = _reduce_begin("up", [gw_mlp_in])
    (dh2,) = _mm_nt("d_mlp_in_mm", du_b, g_mlp_in, [F32], after=(tok,))
    r_up, tok_up = _reduce_start(b_up, where, dh2)

    def mid_bwd_fn(xt, o2t, dh2t, dx2t, *vecs):
        _, vjp = jax.vjp(mid, xt, o2t, *vecs)
        dx, do2, dgta, dgpa, dgpm, dscm, dshm = vjp((dx2t, dh2t))
        return (dx, do2), (dgta, dgpa, dgpm, dscm, dshm)

    (dx1, do2_b), (d_gt_a, d_g_post_attn, d_g_pre_mlp, d_sc_m, d_sh_m) = _rowwise(
        "mid_bwd", mid_bwd_fn, [x2d, o2, dh2, dx2], mid_vecs, [(D, F32), (D, BF16)], [D] * 5, TB, after=(tok_up,))

    (d_on,) = _mm_nt("d_out_mm", do2_b, w_out_f, [F32])
    gw_out = _mm_tn("gw_out_mm", on_b, do2_b, 1).reshape(N_CHIPS, w_out.shape[1], D)
    b_out, tok_out = _reduce_begin("out", [gw_out])

    def mix_bwd_fn(oa, ob, dont, ga, gb):
        _, vjp = jax.vjp(lambda a_, b_, g1, g2: mix_fn(a_, b_, g1, g2)[0][0], oa, ob, ga, gb)
        doa, dob, dga, dgb = vjp(dont)
        return (doa, dob), (dga, dgb)

    (doa_b, dob_b), (d_g_out_a, d_g_out_b) = _rowwise(
        "mix_bwd", mix_bwd_fn, [o_a, o_b, d_on], [g_out_a, g_out_b], [(W_QA, BF16), (HB * V_DIM, BF16)],
        [W_QA, HB * V_DIM], T, after=(tok_out,))

    dqa, dka, dva = _attn_bwd("attn_a_bwd", qa_b, ka_b, va_b, doa_b, o_a, lse_a, HA, HA // HKV, HEAD_DIM, HEAD_DIM,
                              sc_a_, TQ)
    r_out, tok = _reduce_start(b_out, where, dqa)
    dqcat, dkcat, dvb = _attn_bwd("attn_b_bwd", qcat_b, kcat_b, vb_b, dob_b, o_b, lse_b, HB, 1, 2 * LANE, V_DIM,
                                  sc_b_, TQ, after=(tok,))

    def kcat_bwd_fn(dkc, dvt):
        dkv, dkpe = [], None
        for h in range(HB):
            dkv += [dkc[:, 2 * h * LANE:(2 * h + 1) * LANE], dvt[:, h * V_DIM:(h + 1) * V_DIM]]
            piece = dkc[:, (2 * h + 1) * LANE:(2 * h + 2) * LANE]
            dkpe = piece if dkpe is None else dkpe + piece
        return (jnp.concatenate(dkv, axis=1), dkpe), ()

    (dkv_b, dkpe), _ = _rowwise("kcat_bwd", kcat_bwd_fn, [dkcat, dvb], [], [(2 * HB * LANE, BF16), (LANE, F32)], [], T)
    (d_ckvn,) = _mm_nt("d_kv_mm", dkv_b, g_kvb, [F32])
    gw_kvb = _mm_tn("gw_kv_mm", ckvn_b, dkv_b, N_CHIPS)
    b_kvb, tok_kvb = _reduce_begin("kvb", [gw_kvb])

    def prep_bwd_fn(pt, dqat, dkat, dvat, dqct, dckt, dkpt, ca_, sla, sha, cb_, slb, shb, gq, gk, gc):
        tabs = ((ca_, sla, sha), (cb_, slb, shb))
        _, vjp = jax.vjp(lambda sg, a1, a2, a3: _prep(sg, tabs, a1, a2, a3), _split_proj(pt), gq, gk, gc)
        (dqa_s, dka_s, dva_s, dqn_s, dqp_s, dck_s, dkp_s), dgq, dgk, dgc = vjp((dqat, dkat, dvat, dqct, dckt, dkpt))
        dproj = jnp.concatenate([*dqa_s, *dka_s, dva_s, *dqn_s, *dqp_s, dck_s, dkp_s], axis=1)
        return (dproj,), (dgq, dgk, dgc)

    (dproj_b,), (d_g_q_a, d_g_k_a, d_g_ckv) = _rowwise(
        "prep_bwd", prep_bwd_fn, [proj, dqa, dka, dva, dqcat, d_ckvn, dkpe, *tabs_a, *tabs_b],
        [g_q_a, g_k_a, g_ckv], [(W_INP, BF16)], [HEAD_DIM, HEAD_DIM, KV_RANK], TB, after=(tok_kvb,))

    r_kvb, tok = _reduce_start(b_kvb, where, dproj_b)
    (dh,) = _mm_nn("d_proj_mm", dproj_b, wpt, [F32], after=(tok,))
    gw_in = _mm_tn_rows("gw_in_mm", dproj_b, h_b, W_IN, _unpermute_pieces()).reshape(N_CHIPS, W_IN // N_CHIPS, D)
    b_in, tok = _reduce_begin("in", [gw_in])

    def pre_attn_bwd_fn(xt, dht, dx1t, g, sc, sh):
        _, vjp = jax.vjp(lambda a_, b_, c_, d_: pre_attn_fn(a_, b_, c_, d_)[0][0], xt, g, sc, sh)
        dxt, dg, dsc, dsh = vjp(dht)
        return (dx1t + dxt,), (dg, dsc, dsh)

    (grad_x,), (d_g_pre_attn, d_sc_a, d_sh_a) = _rowwise(
        "pre_attn_bwd", pre_attn_bwd_fn, [x2d, dh, dx1], [g_pre_attn, sc_a, sh_a], [(D, F32)], [D] * 3, TB,
        after=(tok,))

    small_names = ["b_ada", "g_pre_attn", "g_q_a", "g_k_a", "g_ckv", "g_out_a", "g_out_b", "g_post_attn",
                   "g_pre_mlp", "g_post_mlp"]
    small_local = jnp.concatenate([d_sh_a, d_sc_a, d_gt_a, d_sh_m, d_sc_m, d_gt_m, d_g_pre_attn, d_g_q_a, d_g_k_a,
                                   d_g_ckv, d_g_out_a, d_g_out_b, d_g_post_attn, d_g_pre_mlp, d_g_post_mlp], axis=1)
    n_small = small_local.shape[1]
    small_all = _allgather8("gather_small", small_local.reshape(8, n_small // 8)).reshape(N_DEV, n_small)

    def sum8_fn(t):
        return (), (jnp.sum(t, axis=0, keepdims=True),)

    r_in, tok_in = _reduce_start(b_in, where, grad_x, (small_all,))
    _, (small_sum,) = _rowwise("sum_small", sum8_fn, [small_all], [], [], [n_small], N_DEV, after=(tok_in,))
    dmod_cols = lax.dynamic_slice(small_all, (0, j_me * n_mod), (N_DEV, n_mod))
    dmod_pad = jnp.concatenate([dmod_cols, jnp.zeros_like(dmod_cols)], axis=0).astype(BF16)
    gw_ada = _mm_tn("gw_ada_mm", ca_pad, dmod_pad, 1)[0]

    weights = dict(w_ada=w_ada, b_ada=b_ada, g_pre_attn=g_pre_attn, w_in=w_in, g_q_a=g_q_a, g_k_a=g_k_a, g_ckv=g_ckv,
                   w_kv_b=w_kv_b, g_out_a=g_out_a, g_out_b=g_out_b, w_out=w_out, g_post_attn=g_post_attn,
                   g_pre_mlp=g_pre_mlp, w_mlp_in=w_mlp_in, w_mlp_out=w_mlp_out, g_post_mlp=g_post_mlp)
    ms = dict(w_ada=m_w_ada, b_ada=m_b_ada, g_pre_attn=m_g_pre_attn, w_in=m_w_in, g_q_a=m_g_q_a, g_k_a=m_g_k_a,
              g_ckv=m_g_ckv, w_kv_b=m_w_kv_b, g_out_a=m_g_out_a, g_out_b=m_g_out_b, w_out=m_w_out,
              g_post_attn=m_g_post_attn, g_pre_mlp=m_g_pre_mlp, w_mlp_in=m_w_mlp_in, w_mlp_out=m_w_mlp_out,
              g_post_mlp=m_g_post_mlp)
    vs = dict(w_ada=v_w_ada, b_ada=v_b_ada, g_pre_attn=v_g_pre_attn, w_in=v_w_in, g_q_a=v_g_q_a, g_k_a=v_g_k_a,
              g_ckv=v_g_ckv, w_kv_b=v_w_kv_b, g_out_a=v_g_out_a, g_out_b=v_g_out_b, w_out=v_w_out,
              g_post_attn=v_g_post_attn, g_pre_mlp=v_g_pre_mlp, w_mlp_in=v_w_mlp_in, w_mlp_out=v_w_mlp_out,
              g_post_mlp=v_g_post_mlp)
    order = list(weights)
    grads, deltas, new_m, new_v = {}, {}, {}, {}

    def update(name, g, after=()):
        view = (lambda t: jnp.swapaxes(t, 0, 1)) if name == "w_in" else (lambda t: t)
        d_, m_, v_ = _adamw("adamw_" + name, view(weights[name][0]), g, view(ms[name][0]), view(vs[name][0]), after)
        grads[name], deltas[name], new_m[name], new_v[name] = (view(t)[None] for t in (g, d_, m_, v_))
        return d_

    chain = (("w_mlp_out", r_down), ("w_mlp_in", r_up), ("w_out", r_out), ("w_kv_b", r_kvb), ("w_in", r_in))
    swap, tok = _reduce_finish(chain[0][1], where, small_sum)
    last = update("w_ada", gw_ada, (tok,))
    for k, (name, _) in enumerate(chain):
        (g,) = _reduce_end(swap, last)
        if k + 1 < len(chain):
            swap, tok = _reduce_finish(chain[k + 1][1], where, g)
        last = update(name, g, (tok,))

    cat = lambda d: jnp.concatenate([d[n] for n in small_names], axis=1)
    d_, m_, v_ = _adamw("adamw_small", cat(weights), small_sum, cat(ms), cat(vs))
    off = 0
    for n in small_names:
        w_ = weights[n].shape[1]
        grads[n], deltas[n], new_m[n], new_v[n] = (a[:, off:off + w_] for a in (small_sum, d_, m_, v_))
        off += w_

    loss = lax.psum(loss_v[0, 0], ("x", "y", "c"))
    return (loss, grad_x[None], *[grads[n] for n in order], *[deltas[n] for n in order],
            *[new_m[n] for n in order], *[new_v[n] for n in order])
```

```python
import functools
import math

import numpy as np
import jax
import jax.numpy as jnp
from jax import lax
from jax.experimental import pallas as pl
from jax.experimental.pallas import tpu as pltpu

F32 = jnp.float32
BF16 = jnp.bfloat16
MESH = pl.DeviceIdType.MESH

EPS = 1e-6
GRID_W = 64
ROPE_THETA = 10000.0
HEAD_DIM = 128
HA = 8
HKV = 2
HB = 8
QK_NOPE = 128
QK_ROPE = 64
V_DIM = 128
KV_RANK = 512
QK_B = QK_NOPE + QK_ROPE
W_QA = HA * HEAD_DIM
W_KA = HKV * HEAD_DIM
W_QB = HB * QK_B
W_IN = W_QA + 2 * W_KA + W_QB + KV_RANK + QK_ROPE
N_CHIPS = 4
N_DEV = 8

LANE = 128
O_QA = 0
O_KA = O_QA + W_QA
O_VA = O_KA + W_KA
O_QN = O_VA + W_KA
O_QP = O_QN + HB * QK_NOPE
O_CKV = O_QP + HB * LANE
O_KPE = O_CKV + KV_RANK
W_INP = O_KPE + LANE

ADAM_LR = 0.001
ADAM_B1 = 0.9
ADAM_B2 = 0.999
ADAM_EPS = 1e-08
ADAM_WD = 0.01
ADAM_STEP = 10

VMEM_LIMIT = 56 * 1024 * 1024
HEADS_PER_STEP = 2


def _params(sem):
    return pltpu.CompilerParams(dimension_semantics=sem, vmem_limit_bytes=VMEM_LIMIT)


def _pick(dim, target, unit):
    best = None
    t = unit
    while t <= min(dim, target):
        if dim % t == 0:
            best = t
        t += unit
    return dim if best is None else best


def _rn(x):
    return x * lax.rsqrt(jnp.mean(x * x, axis=-1, keepdims=True) + EPS)


def _rowwise(name, fn, rows, vecs, row_outs, vec_outs, tile, after=()):
    S = rows[0].shape[0]
    assert S % tile == 0
    n_r, n_v, n_ro, n_vo, n_a = len(rows), len(vecs), len(row_outs), len(vec_outs), len(after)

    def body(*refs):
        r_in = refs[:n_r]
        v_in = refs[n_r:n_r + n_v]
        r_out = refs[n_r + n_v + n_a:n_r + n_v + n_a + n_ro]
        v_out = refs[n_r + n_v + n_a + n_ro:]
        ro, vo = fn(*[r[...] for r in r_in], *[v[...] for v in v_in])
        for ref, val in zip(r_out, ro):
            ref[...] = val.astype(ref.dtype)
        if n_vo:
            @pl.when(pl.program_id(0) == 0)
            def _():
                for ref in v_out:
                    ref[...] = jnp.zeros(ref.shape, ref.dtype)
            for ref, val in zip(v_out, vo):
                ref[...] += val

    in_specs = [pl.BlockSpec((tile, a.shape[1]), lambda i: (i, 0)) for a in rows]
    in_specs += [pl.BlockSpec(a.shape, lambda i: (0, 0)) for a in vecs]
    in_specs += [pl.BlockSpec(memory_space=pl.ANY) for _ in after]
    out_specs = [pl.BlockSpec((tile, w), lambda i: (i, 0)) for w, _ in row_outs]
    out_specs += [pl.BlockSpec((1, w), lambda i: (0, 0)) for w in vec_outs]
    out_shape = [jax.ShapeDtypeStruct((S, w), dt) for w, dt in row_outs]
    out_shape += [jax.ShapeDtypeStruct((1, w), F32) for w in vec_outs]
    res = pl.pallas_call(
        body, name=name, grid=(S // tile,), in_specs=in_specs, out_specs=out_specs, out_shape=out_shape,
        compiler_params=_params(("arbitrary",)),
    )(*rows, *vecs, *after)
    return res[:n_ro], res[n_ro:]


CONTRACT_TILE = 2048
MM_TILE_BYTES = 40 * 1024 * 1024


def _rows_that_fit(M, tile_bytes):
    t = _pick(M, 1024, 16)
    while t % 32 == 0 and tile_bytes(t) > MM_TILE_BYTES:
        t //= 2
    return t


def _accumulate(acc, part, step, n_steps, finish):
    if n_steps == 1:
        finish(part)
        return

    @pl.when(step == 0)
    def _():
        acc[...] = part

    @pl.when(step > 0)
    def _():
        acc[...] += part

    @pl.when(step == n_steps - 1)
    def _():
        finish(acc[...])


def _mm_nn(name, a, b3, out_dtypes, epilogue=None, extras=(), after=()):
    M, K = a.shape
    nB, K2, Nsh = b3.shape
    assert K == K2
    N = nB * Nsh
    tn, tk = _pick(Nsh, 1536, LANE), _pick(K, CONTRACT_TILE, LANE)
    out_bytes = sum(jnp.dtype(dt).itemsize for dt in out_dtypes) + sum(e.dtype.itemsize for e in extras)
    tm = _rows_that_fit(M, lambda t: 2 * (t * tk * a.dtype.itemsize + tk * tn * b3.dtype.itemsize)
                        + t * tn * (4 * (K > tk) + 2 * out_bytes))
    nps, nk, ne, no, na = Nsh // tn, K // tk, len(extras), len(out_dtypes), len(after)

    def body(a_ref, b_ref, *rest):
        e_refs, o_refs = rest[:ne], rest[ne + na:ne + na + no]

        def finish(total):
            vals = (total,) if epilogue is None else epilogue(total, *[e[...] for e in e_refs])
            for ref, val in zip(o_refs, vals):
                ref[...] = val.astype(ref.dtype)

        part = jnp.dot(a_ref[...].astype(BF16), b_ref[...].astype(BF16), preferred_element_type=F32)
        _accumulate(rest[-1], part, pl.program_id(2), nk, finish)

    in_specs = [pl.BlockSpec((tm, tk), lambda i, j, k: (i, k)),
                pl.BlockSpec((None, tk, tn), lambda i, j, k: (j // nps, k, j % nps))]
    in_specs += [pl.BlockSpec((tm, tn), lambda i, j, k: (i, j)) for _ in extras]
    in_specs += [pl.BlockSpec(memory_space=pl.ANY) for _ in after]
    return pl.pallas_call(
        body, name=name, grid=(M // tm, N // tn, nk), in_specs=in_specs,
        out_specs=[pl.BlockSpec((tm, tn), lambda i, j, k: (i, j)) for _ in out_dtypes],
        out_shape=[jax.ShapeDtypeStruct((M, N), dt) for dt in out_dtypes],
        scratch_shapes=[pltpu.VMEM((tm, tn), F32)] if nk > 1 else [],
        compiler_params=_params(("parallel", "parallel", "arbitrary")),
    )(a, b3, *extras, *after)


def _mm_nt(name, a, b3, out_dtypes, epilogue=None, extras=(), after=()):
    M, N = a.shape
    nB, K, Nsh = b3.shape
    assert N == nB * Nsh
    to, tn = _pick(K, 1536, LANE), _pick(Nsh, CONTRACT_TILE, LANE)
    out_bytes = sum(jnp.dtype(dt).itemsize for dt in out_dtypes) + sum(e.dtype.itemsize for e in extras)
    tm = _rows_that_fit(M, lambda t: 2 * (t * tn * a.dtype.itemsize + to * tn * b3.dtype.itemsize)
                        + t * to * (4 * (N > tn) + 2 * out_bytes))
    nps, nn, ne, no, na = Nsh // tn, N // tn, len(extras), len(out_dtypes), len(after)

    def body(a_ref, b_ref, *rest):
        e_refs, o_refs = rest[:ne], rest[ne + na:ne + na + no]

        def finish(total):
            vals = (total,) if epilogue is None else epilogue(total, *[e[...] for e in e_refs])
            for ref, val in zip(o_refs, vals):
                ref[...] = val.astype(ref.dtype)

        part = lax.dot_general(a_ref[...].astype(BF16), b_ref[...].astype(BF16),
                               (((1,), (1,)), ((), ())), preferred_element_type=F32)
        _accumulate(rest[-1], part, pl.program_id(2), nn, finish)

    in_specs = [pl.BlockSpec((tm, tn), lambda i, o, n: (i, n)),
                pl.BlockSpec((None, to, tn), lambda i, o, n: (n // nps, o, n % nps))]
    in_specs += [pl.BlockSpec((tm, to), lambda i, o, n: (i, o)) for _ in extras]
    in_specs += [pl.BlockSpec(memory_space=pl.ANY) for _ in after]
    return pl.pallas_call(
        body, name=name, grid=(M // tm, K // to, nn), in_specs=in_specs,
        out_specs=[pl.BlockSpec((tm, to), lambda i, o, n: (i, o)) for _ in out_dtypes],
        out_shape=[jax.ShapeDtypeStruct((M, K), dt) for dt in out_dtypes],
        scratch_shapes=[pltpu.VMEM((tm, to), F32)] if nn > 1 else [],
        compiler_params=_params(("parallel", "parallel", "arbitrary")),
    )(a, b3, *extras, *after)


def _mm_tn(name, a, g, nB, after=()):
    M, K = a.shape
    M2, N = g.shape
    assert M == M2 and N % nB == 0
    Nsh = N // nB
    tk, tn, tm = _pick(K, 1536, LANE), _pick(Nsh, 1024, LANE), _pick(M, CONTRACT_TILE, 16)
    nps, nm = Nsh // tn, M // tm

    def body(a_ref, g_ref, *rest):
        o_ref = rest[len(after)]

        def finish(total):
            o_ref[...] = total

        part = lax.dot_general(a_ref[...].astype(BF16), g_ref[...].astype(BF16),
                               (((0,), (0,)), ((), ())), preferred_element_type=F32)
        _accumulate(rest[-1], part, pl.program_id(2), nm, finish)

    return pl.pallas_call(
        body, name=name, grid=(K // tk, N // tn, nm),
        in_specs=[pl.BlockSpec((tm, tk), lambda k, n, m: (m, k)),
                  pl.BlockSpec((tm, tn), lambda k, n, m: (m, n))] + [pl.BlockSpec(memory_space=pl.ANY) for _ in after],
        out_specs=pl.BlockSpec((None, tk, tn), lambda k, n, m: (n // nps, k, n % nps)),
        out_shape=jax.ShapeDtypeStruct((nB, K, Nsh), F32),
        scratch_shapes=[pltpu.VMEM((tk, tn), F32)] if nm > 1 else [],
        compiler_params=_params(("parallel", "parallel", "arbitrary")),
    )(a, g, *after)


def _mm_tn_rows(name, a, g, n_rows, pieces):
    M, K = a.shape
    M2, N = g.shape
    assert M == M2
    tk, tn, tm = _pick(K, 1536, LANE), _pick(N, 512, LANE), _pick(M, CONTRACT_TILE, 16)
    nk, nm = K // tk, M // tm

    def body(a_ref, g_ref, o_ref, *acc):
        kk = pl.program_id(1)

        def finish(total):
            for t in range(nk):
                @pl.when(kk == t)
                def _(t=t):
                    for src, dst, n in pieces:
                        lo, hi = max(src, t * tk), min(src + n, (t + 1) * tk)
                        if lo < hi:
                            o_ref[dst + lo - src:dst + hi - src, :] = total[lo - t * tk:hi - t * tk, :]

        part = lax.dot_general(a_ref[...].astype(BF16), g_ref[...].astype(BF16),
                               (((0,), (0,)), ((), ())), preferred_element_type=F32)
        _accumulate(acc[0] if acc else None, part, pl.program_id(2), nm, finish)

    return pl.pallas_call(
        body, name=name, grid=(N // tn, nk, nm),
        in_specs=[pl.BlockSpec((tm, tk), lambda n, k, m: (m, k)),
                  pl.BlockSpec((tm, tn), lambda n, k, m: (m, n))],
        out_specs=pl.BlockSpec((n_rows, tn), lambda n, k, m: (0, n)),
        out_shape=jax.ShapeDtypeStruct((n_rows, N), F32),
        scratch_shapes=[pltpu.VMEM((tk, tn), F32)] if nm > 1 else [],
        compiler_params=_params(("arbitrary", "arbitrary", "arbitrary")),
    )(a, g)


def _attn_fwd(name, q, k, v, n_heads, group, dqk, dv, scale, tq, after=()):
    S = q.shape[0]
    hp = 2 * HEADS_PER_STEP
    kvp = max(1, hp // group)

    def body(q_ref, k_ref, v_ref, *rest):
        o_ref, lse_ref = rest[-2:]
        for j in range(hp):
            jk = j * kvp // hp
            s = lax.dot_general(q_ref[:, j * dqk:(j + 1) * dqk], k_ref[:, jk * dqk:(jk + 1) * dqk],
                                (((1,), (1,)), ((), ())), preferred_element_type=F32)
            m = jnp.max(s, axis=-1, keepdims=True)
            e = jnp.exp((s - m) * scale)
            l = jnp.sum(e, axis=-1, keepdims=True)
            o_ref[:, j * dv:(j + 1) * dv] = jnp.dot(e.astype(BF16), v_ref[:, jk * dv:(jk + 1) * dv],
                                                    preferred_element_type=F32) * (1.0 / l)
            lse_ref[:, j * LANE:(j + 1) * LANE] = jnp.broadcast_to(m * scale + jnp.log(l), (tq, LANE))

    kv_block = lambda h, i: (0, (h * hp // group) // kvp)
    return pl.pallas_call(
        body, name=name, grid=(n_heads // hp, S // tq),
        in_specs=[pl.BlockSpec((tq, hp * dqk), lambda h, i: (i, h)),
                  pl.BlockSpec((S, kvp * dqk), kv_block),
                  pl.BlockSpec((S, kvp * dv), kv_block)] + [pl.BlockSpec(memory_space=pl.ANY) for _ in after],
        out_specs=[pl.BlockSpec((tq, hp * dv), lambda h, i: (i, h)),
                   pl.BlockSpec((tq, hp * LANE), lambda h, i: (i, h))],
        out_shape=[jax.ShapeDtypeStruct((S, n_heads * dv), F32), jax.ShapeDtypeStruct((S, n_heads * LANE), F32)],
        compiler_params=_params(("arbitrary", "arbitrary")),
    )(q, k, v, *after)


def _attn_bwd(name, q, k, v, do, o, lse, n_heads, group, dqk, dv, scale, tq, after=()):
    S = q.shape[0]
    n_kv = n_heads // group
    hp = HEADS_PER_STEP
    kvp = max(1, hp // group)

    def body(q_ref, k_ref, v_ref, do_ref, o_ref, lse_ref, *rest):
        dq_ref, dk_ref, dv_ref = rest[len(after):]
        h, i = pl.program_id(0), pl.program_id(1)

        @pl.when(((h * hp) % group == 0) & (i == 0))
        def _():
            dk_ref[...] = jnp.zeros(dk_ref.shape, F32)
            dv_ref[...] = jnp.zeros(dv_ref.shape, F32)

        for j in range(hp):
            jk = j * kvp // hp
            qb, dob = q_ref[:, j * dqk:(j + 1) * dqk], do_ref[:, j * dv:(j + 1) * dv]
            kb, vb = k_ref[:, jk * dqk:(jk + 1) * dqk], v_ref[:, jk * dv:(jk + 1) * dv]
            s = lax.dot_general(qb, kb, (((1,), (1,)), ((), ())), preferred_element_type=F32)
            p = jnp.exp(s * scale - jnp.tile(lse_ref[:, j * LANE:(j + 1) * LANE], (1, S // LANE)))
            dp = lax.dot_general(dob, vb, (((1,), (1,)), ((), ())), preferred_element_type=F32)
            delta = jnp.sum(dob.astype(F32) * o_ref[:, j * dv:(j + 1) * dv], axis=-1, keepdims=True)
            ds = (p * ((dp - delta) * scale)).astype(BF16)
            dv_ref[:, jk * dv:(jk + 1) * dv] += lax.dot_general(p.astype(BF16), dob, (((0,), (0,)), ((), ())),
                                                                preferred_element_type=F32)
            dq_ref[:, j * dqk:(j + 1) * dqk] = jnp.dot(ds, kb, preferred_element_type=F32)
            dk_ref[:, jk * dqk:(jk + 1) * dqk] += lax.dot_general(ds, qb, (((0,), (0,)), ((), ())),
                                                                  preferred_element_type=F32)

    kv_block = lambda h, i: (0, (h * hp // group) // kvp)
    return pl.pallas_call(
        body, name=name, grid=(n_heads // hp, S // tq),
        in_specs=[pl.BlockSpec((tq, hp * dqk), lambda h, i: (i, h)),
                  pl.BlockSpec((S, kvp * dqk), kv_block),
                  pl.BlockSpec((S, kvp * dv), kv_block),
                  pl.BlockSpec((tq, hp * dv), lambda h, i: (i, h)),
                  pl.BlockSpec((tq, hp * dv), lambda h, i: (i, h)),
                  pl.BlockSpec((tq, hp * LANE), lambda h, i: (i, h))] + [pl.BlockSpec(memory_space=pl.ANY) for _ in after],
        out_specs=[pl.BlockSpec((tq, hp * dqk), lambda h, i: (i, h)),
                   pl.BlockSpec((S, kvp * dqk), kv_block),
                   pl.BlockSpec((S, kvp * dv), kv_block)],
        out_shape=[jax.ShapeDtypeStruct((S, n_heads * dqk), F32),
                   jax.ShapeDtypeStruct((S, n_kv * dqk), F32),
                   jax.ShapeDtypeStruct((S, n_kv * dv), F32)],
        compiler_params=_params(("arbitrary", "arbitrary")),
    )(q, k, v, do, o, lse, *after)


def _adamw(name, w, g, m, v, after=()):
    R, C = w.shape
    tile = _pick(R, max(8, (1 << 19) // C // 8 * 8), 8)
    c1 = 1.0 - ADAM_B1 ** ADAM_STEP
    c2 = 1.0 - ADAM_B2 ** ADAM_STEP

    def body(w_ref, g_ref, m_ref, v_ref, *rest):
        d_ref, nm_ref, nv_ref = rest[len(after):]
        gg = g_ref[...]
        nm = ADAM_B1 * m_ref[...] + (1.0 - ADAM_B1) * gg
        nv = ADAM_B2 * v_ref[...] + (1.0 - ADAM_B2) * (gg * gg)
        d_ref[...] = -ADAM_LR * ((nm / c1) / (jnp.sqrt(nv / c2) + ADAM_EPS) + ADAM_WD * w_ref[...])
        nm_ref[...] = nm
        nv_ref[...] = nv

    spec = pl.BlockSpec((tile, C), lambda i: (i, 0))
    return pl.pallas_call(
        body, name=name, grid=(R // tile,), in_specs=[spec] * 4 + [pl.BlockSpec(memory_space=pl.ANY) for _ in after],
        out_specs=[spec] * 3, out_shape=[jax.ShapeDtypeStruct((R, C), F32)] * 3,
        compiler_params=_params(("parallel",)),
    )(w, g, m, v, *after)


def _flip(me, rel):
    return tuple(1 - p if r else p for p, r in zip(me, rel))


def _push(name, ins, out_shapes, transfers, aliases=None):
    ni, no, nt = len(ins), len(out_shapes), len(transfers)

    def body(*refs):
        in_refs, out_refs = refs[:ni], refs[ni:ni + no]
        send_sems, recv_sems = refs[ni + no], refs[ni + no + 1]
        me = (lax.axis_index("x"), lax.axis_index("y"), lax.axis_index("c"))
        copies = []
        for t, (si, sfn, oi, dfn, rel) in enumerate(transfers):
            src_ref = out_refs[si[1]] if isinstance(si, tuple) else in_refs[si]
            src = src_ref.at[sfn(*me)]
            dst = out_refs[oi].at[dfn(*me)]
            if rel is None:
                cp = pltpu.make_async_copy(src, dst, send_sems.at[t])
            else:
                cp = pltpu.make_async_remote_copy(src_ref=src, dst_ref=dst, send_sem=send_sems.at[t],
                                                  recv_sem=recv_sems.at[t], device_id=_flip(me, rel),
                                                  device_id_type=MESH)
            cp.start()
            copies.append(cp)
        for cp in copies:
            cp.wait()

    any_spec = pl.BlockSpec(memory_space=pl.ANY)
    return pl.pallas_call(
        body, name=name, in_specs=[any_spec] * ni, out_specs=[any_spec] * no, out_shape=out_shapes,
        scratch_shapes=[pltpu.SemaphoreType.DMA((nt,)), pltpu.SemaphoreType.DMA((nt,))],
        input_output_aliases=aliases or {},
        compiler_params=pltpu.CompilerParams(has_side_effects=True),
    )(*ins)


_HBM = pl.BlockSpec(memory_space=pltpu.HBM)
_SEM = pl.BlockSpec(memory_space=pltpu.SEMAPHORE)
_DATAFLOW = pltpu.SideEffectType.DATAFLOW_SIDE_EFFECTING


def _split_copies(refs, transfers, send_sems, recv_sems):
    me = (lax.axis_index("x"), lax.axis_index("y"), lax.axis_index("c"))
    return [pltpu.make_async_remote_copy(src_ref=refs[sb].at[sfn(*me)], dst_ref=refs[db].at[dfn(*me)],
                                         send_sem=send_sems.at[t], recv_sem=recv_sems.at[t],
                                         device_id=_flip(me, rel), device_id_type=MESH)
            for t, (sb, sfn, db, dfn, rel) in enumerate(transfers)]


def _push_start(name, bufs, transfers, after=()):
    nb, na, nt = len(bufs), len(after), len(transfers)

    def body(*refs):
        send_sems, recv_sems, token = refs[nb + na], refs[nb + na + 1], refs[-1]
        for cp in _split_copies(refs[nb + na + 2:2 * nb + na + 2], transfers, send_sems, recv_sems):
            cp.start()
        token[...] = jnp.zeros(token.shape, token.dtype)

    res = pl.pallas_call(
        body, name=name,
        out_shape=(pltpu.SemaphoreType.DMA((nt,)), pltpu.SemaphoreType.DMA((nt,)),
                   *[pltpu.HBM(b.shape, b.dtype) for b in bufs], jax.ShapeDtypeStruct((8, LANE), F32)),
        in_specs=[_HBM] * nb + [pl.BlockSpec(memory_space=pl.ANY)] * na,
        out_specs=(_SEM, _SEM, *[_HBM] * nb, pl.BlockSpec(memory_space=pltpu.VMEM)),
        input_output_aliases={i: i + 2 for i in range(nb)},
        compiler_params=pltpu.CompilerParams(has_side_effects=_DATAFLOW),
    )(*[pltpu.with_memory_space_constraint(b, pltpu.HBM) for b in bufs], *after)
    return res[0], res[1], list(res[2:2 + nb]), res[-1]


def _push_wait(name, send_sems, recv_sems, bufs, transfers, after):
    nb = len(bufs)

    def body(*refs):
        for cp in _split_copies(refs[:nb], transfers, refs[nb], refs[nb + 1]):
            cp.wait_send()
            cp.wait_recv()

    res = pl.pallas_call(
        body, name=name, out_shape=[pltpu.HBM(b.shape, b.dtype) for b in bufs],
        in_specs=[_HBM] * nb + [_SEM, _SEM, pl.BlockSpec(memory_space=pl.ANY)], out_specs=[_HBM] * nb,
        input_output_aliases={i: i for i in range(nb)},
        compiler_params=pltpu.CompilerParams(has_side_effects=_DATAFLOW),
    )(*bufs, send_sems, recv_sems, after)
    return list(res)


ICI_RELS = ((1, 0, 0), (0, 1, 0), (1, 1, 0))
SIBLING = (0, 0, 1)


def _chip(x, y):
    return 2 * x + y


def _cast_into_slot(name, where, w, after=()):
    R, C = w.shape
    tile = _pick(R, 512, BF16_ROWS)

    def body(where_ref, w_ref, *rest):
        rest[-1][...] = w_ref[...].astype(BF16)

    return pl.pallas_call(
        body, name=name,
        grid_spec=pltpu.PrefetchScalarGridSpec(
            num_scalar_prefetch=1, grid=(R // tile,),
            in_specs=[pl.BlockSpec((tile, C), lambda i, wh: (i, 0))] + [pl.BlockSpec(memory_space=pl.ANY) for _ in after],
            out_specs=pl.BlockSpec((None, tile, C), lambda i, wh: (wh[1], i, 0))),
        out_shape=jax.ShapeDtypeStruct((N_CHIPS, R, C), BF16),
        compiler_params=_params(("parallel",)),
    )(where, w, *after)


BF16_ROWS = 16


def _rows_split(shape2):
    return (shape2[0] // 2) % BF16_ROWS == 0


def _half_shape(shape2):
    R, C = shape2
    return (R // 2, C) if _rows_split(shape2) else (R, C // 2)


def _half(shape2, c):
    R, C = shape2
    if _rows_split(shape2):
        return (pl.ds(c * (R // 2), R // 2), slice(None))
    return (slice(None), pl.ds(pl.multiple_of(c * (C // 2), LANE), C // 2))


def _gather_start(tag, where, shards, after, cast_after=()):
    n = len(shards)
    own = [_cast_into_slot(f"cast_w_{tag}{w}", where, shards[w], cast_after) for w in range(n)]
    transfers = []
    for w in range(n):
        def idx(x, y, c, shape2=shards[w].shape):
            return (_chip(x, y), *_half(shape2, c))
        transfers += [(w, idx, w, idx, rel) for rel in ICI_RELS]
    send_sems, recv_sems, bufs, token = _push_start(f"gather_{tag}_start", own, transfers, after)
    return (tag, send_sems, recv_sems, bufs, transfers), token


def _forward_transfers(part):
    second = []
    for w, p in enumerate(part):
        for rel in ICI_RELS:
            def idx(x, y, c, shape2=p.shape[1:], rel=rel):
                return (_chip(x ^ rel[0], y ^ rel[1]), *_half(shape2, c))
            second.append((w, idx, w, idx, SIBLING))
    return second


def _gather_finish(handle, after):
    tag, send_sems, recv_sems, bufs, transfers = handle
    part = _push_wait(f"gather_{tag}_wait", send_sems, recv_sems, bufs, transfers, after)
    outs = [jax.ShapeDtypeStruct(p.shape, p.dtype) for p in part]
    second = [(("out", sb), sfn, db, dfn, rel) for sb, sfn, db, dfn, rel in _forward_transfers(part)]
    return _push(f"gather_{tag}_d2d", list(part), outs, second, aliases={w: w for w in range(len(part))})


def _gather_mid(handle, after):
    tag, send_sems, recv_sems, bufs, transfers = handle
    part = _push_wait(f"gather_{tag}_wait", send_sems, recv_sems, bufs, transfers, after)
    second = _forward_transfers(part)
    s2, r2, bufs2, token = _push_start(f"gather_{tag}_d2d_start", part, second)
    return (tag, s2, r2, bufs2, second), token


def _gather_end(handle, after):
    tag, send_sems, recv_sems, bufs, transfers = handle
    return _push_wait(f"gather_{tag}_d2d_wait", send_sems, recv_sems, bufs, transfers, after)


def _reduce_begin(tag, grads, after=()):
    n = len(grads)
    land = [lax.empty((N_CHIPS, *_half_shape(g.shape[1:])), F32) for g in grads]
    t1 = []
    for w in range(n):
        t1.append((w, lambda x, y, c, shape2=grads[w].shape[1:]: (slice(None), *_half(shape2, 1 - c)), n + w,
                   lambda x, y, c: (slice(None), slice(None), slice(None)), SIBLING))
    send_sems, recv_sems, bufs, token = _push_start(f"reduce_{tag}_d2d_start", list(grads) + land, t1, after)
    return (tag, send_sems, recv_sems, bufs, t1), token


def _reduce_start(handle, where, after, after_start=()):
    tag, send_sems, recv_sems, bufs, t1 = handle
    n = len(bufs) // 2
    res = _push_wait(f"reduce_{tag}_d2d_wait", send_sems, recv_sems, bufs, t1, after)
    grads, from_sib = res[:n], res[n:]
    pair = [_pair_sum(f"pair_sum_{tag}{w}", where, grads[w], from_sib[w]) for w in range(n)]
    land = [lax.empty((3, *_half_shape(g.shape[1:])), BF16) for g in grads]
    t3 = []
    for w in range(n):
        for r, rel in enumerate(ICI_RELS):
            t3.append((w, lambda x, y, c, rel=rel: (_chip(x ^ rel[0], y ^ rel[1]), slice(None), slice(None)), n + w,
                       lambda x, y, c, r=r: (r, slice(None), slice(None)), rel))
    send_sems, recv_sems, bufs, token = _push_start(f"reduce_{tag}_start", pair + land, t3, after_start)
    return (tag, send_sems, recv_sems, bufs, t3, list(grads), list(from_sib)), token


def _reduce_finish(handle, where, after):
    tag, send_sems, recv_sems, bufs, t3, grads, from_sib = handle
    n = len(grads)
    from_chips = _push_wait(f"reduce_{tag}_wait", send_sems, recv_sems, bufs, t3, after)[n:]
    mine = [_chip_sum(f"chip_sum_{tag}{w}", where, grads[w], from_sib[w], from_chips[w]) for w in range(n)]
    t5 = []
    for w in range(n):
        def idx(x, y, c, shape2=grads[w].shape[1:]):
            return _half(shape2, c)
        t5.append((w, idx, w, idx, SIBLING))
    send_sems, recv_sems, bufs, token = _push_start(f"reduce_{tag}_swap_start", mine, t5)
    return (tag, send_sems, recv_sems, bufs, t5), token


def _reduce_end(handle, after):
    tag, send_sems, recv_sems, bufs, t5 = handle
    return _push_wait(f"reduce_{tag}_swap_wait", send_sems, recv_sems, bufs, t5, after)


def _pair_sum(name, where, g, from_sib):
    _, hr, hc = from_sib.shape
    tile = _pick(hr, max(16, (1 << 19) // hc // 16 * 16), 16)
    nb = hr // tile
    slot = lambda k, wh: (wh[1] + 1 + k) % N_CHIPS
    if _rows_split(g.shape[1:]):
        mine = lambda k, i, wh: (slot(k, wh), wh[0] * nb + i, 0)
    else:
        mine = lambda k, i, wh: (slot(k, wh), i, wh[0])

    def body(where_ref, g_ref, s_ref, o_ref):
        o_ref[...] = (g_ref[...] + s_ref[...]).astype(BF16)

    return pl.pallas_call(
        body, name=name,
        grid_spec=pltpu.PrefetchScalarGridSpec(
            num_scalar_prefetch=1, grid=(N_CHIPS - 1, nb),
            in_specs=[pl.BlockSpec((None, tile, hc), mine),
                      pl.BlockSpec((None, tile, hc), lambda k, i, wh: (slot(k, wh), i, 0))],
            out_specs=pl.BlockSpec((None, tile, hc), lambda k, i, wh: (slot(k, wh), i, 0))),
        out_shape=jax.ShapeDtypeStruct((N_CHIPS, hr, hc), BF16),
        compiler_params=_params(("parallel", "parallel")),
    )(where, g, from_sib)


def _chip_sum(name, where, g, from_sib, from_chips):
    _, R, C = g.shape
    _, hr, hc = from_sib.shape
    tile = _pick(hr, max(16, (1 << 19) // hc // 16 * 16), 16)
    nb = hr // tile
    if _rows_split((R, C)):
        mine = lambda i, wh: (wh[0] * nb + i, 0)
    else:
        mine = lambda i, wh: (i, wh[0])

    def body(where_ref, g_ref, s_ref, r_ref, o_ref):
        acc = g_ref[...] + s_ref[...]
        for r in range(3):
            acc = acc + r_ref[r].astype(F32)
        o_ref[...] = acc

    return pl.pallas_call(
        body, name=name,
        grid_spec=pltpu.PrefetchScalarGridSpec(
            num_scalar_prefetch=1, grid=(nb,),
            in_specs=[pl.BlockSpec((None, tile, hc), lambda i, wh: (wh[1], *mine(i, wh))),
                      pl.BlockSpec((None, tile, hc), lambda i, wh: (wh[1], i, 0)),
                      pl.BlockSpec((3, tile, hc), lambda i, wh: (0, i, 0))],
            out_specs=pl.BlockSpec((tile, hc), mine)),
        out_shape=jax.ShapeDtypeStruct((R, C), F32),
        compiler_params=_params(("parallel",)),
    )(where, g, from_sib, from_chips)


def _allgather8(name, blk):
    m_per, n = blk.shape
    rels = [(0, 0, 1), (1, 0, 0), (0, 1, 0), (1, 1, 0), (1, 0, 1), (0, 1, 1), (1, 1, 1)]

    def body(x_ref, out_ref, send_sems, recv_sems):
        me = (lax.axis_index("x"), lax.axis_index("y"), lax.axis_index("c"))
        my_rows = out_ref.at[pl.ds((4 * me[0] + 2 * me[1] + me[2]) * m_per, m_per), :]
        out_ref[pl.ds((4 * me[0] + 2 * me[1] + me[2]) * m_per, m_per), :] = x_ref[...]
        copies = []
        for t, rel in enumerate(rels):
            cp = pltpu.make_async_remote_copy(src_ref=x_ref, dst_ref=my_rows, send_sem=send_sems.at[t],
                                              recv_sem=recv_sems.at[t], device_id=_flip(me, rel),
                                              device_id_type=MESH)
            cp.start()
            copies.append(cp)
        for cp in copies:
            cp.wait()

    return pl.pallas_call(
        body, name=name, out_shape=jax.ShapeDtypeStruct((N_DEV * m_per, n), blk.dtype),
        in_specs=[pl.BlockSpec(memory_space=pltpu.VMEM)], out_specs=pl.BlockSpec(memory_space=pltpu.VMEM),
        scratch_shapes=[pltpu.SemaphoreType.DMA((7,)), pltpu.SemaphoreType.DMA((7,))],
        compiler_params=pltpu.CompilerParams(has_side_effects=True, vmem_limit_bytes=VMEM_LIMIT),
    )(blk)


def _rope_tables(seq_len, dim, reps):
    rows = seq_len // GRID_W
    t = np.arange(seq_len)
    row, col = (t // GRID_W).astype(np.float32), (t % GRID_W).astype(np.float32)
    half = dim // 2
    inv = jnp.asarray(ROPE_THETA, F32) ** (-jnp.arange(0, half, 2, dtype=F32) / half)
    ang_r = jnp.asarray(row)[:, None] * inv[None, :]
    ang_c = jnp.asarray(col)[:, None] * inv[None, :]
    ang = jnp.concatenate([ang_r, ang_r, ang_c, ang_c], axis=-1)
    cos, sin = jnp.cos(ang), jnp.sin(ang)
    low = (np.arange(dim) % (dim // 2)) < (dim // 4)
    s_lo = jnp.where(jnp.asarray(low)[None, :], -sin, 0.0)
    s_hi = jnp.where(jnp.asarray(low)[None, :], 0.0, sin)
    del rows
    return tuple(jnp.tile(a, (1, reps)) for a in (cos, s_lo, s_hi))


def _make_rope(shift):
    def up(v):
        return pltpu.roll(v, LANE - shift, 1)

    def down(v):
        return pltpu.roll(v, shift, 1)

    @jax.custom_vjp
    def rope(v, cos, s_lo, s_hi):
        return v * cos + up(v) * s_lo + down(v) * s_hi

    def fwd(v, cos, s_lo, s_hi):
        return rope(v, cos, s_lo, s_hi), (cos, s_lo, s_hi)

    def bwd(res, dy):
        cos, s_lo, s_hi = res
        return dy * cos + down(dy * s_lo) + up(dy * s_hi), jnp.zeros_like(cos), jnp.zeros_like(cos), jnp.zeros_like(cos)

    rope.defvjp(fwd, bwd)
    return rope


_rope_a = _make_rope(HEAD_DIM // 4)
_rope_b = _make_rope(QK_ROPE // 4)


def _prep(segs, tabs, g_q, g_k, g_ckv):
    qa, ka, va, qn, qp, ckv, kpe = segs
    ta, tb = tabs
    q_a = jnp.concatenate([_rope_a(_rn(s) * g_q, *ta) for s in qa], axis=1)
    k_a = jnp.concatenate([_rope_a(_rn(s) * g_k, *ta) for s in ka], axis=1)
    cat = []
    for h in range(HB):
        cat += [qn[h], _rope_b(qp[h], *tb)]
    q_cat = jnp.concatenate(cat, axis=1)
    return q_a, k_a, va, q_cat, _rn(ckv) * g_ckv, _rope_b(kpe, *tb)


def _split_proj(proj):
    qa = [proj[:, O_QA + h * LANE:O_QA + (h + 1) * LANE] for h in range(HA)]
    ka = [proj[:, O_KA + h * LANE:O_KA + (h + 1) * LANE] for h in range(HKV)]
    va = proj[:, O_VA:O_QN]
    qn = [proj[:, O_QN + h * LANE:O_QN + (h + 1) * LANE] for h in range(HB)]
    qp = [proj[:, O_QP + h * LANE:O_QP + (h + 1) * LANE] for h in range(HB)]
    return qa, ka, va, qn, qp, proj[:, O_CKV:O_KPE], proj[:, O_KPE:W_INP]


def _permute_w_in(nat):
    K = nat.shape[1]
    qb = nat[W_QA + 2 * W_KA:W_QA + 2 * W_KA + W_QB].reshape(HB, QK_B, K)
    qp = jnp.concatenate([qb[:, QK_NOPE:], jnp.zeros((HB, LANE - QK_ROPE, K), nat.dtype)], axis=1)
    tail = nat[W_QA + 2 * W_KA + W_QB:]
    return jnp.concatenate([nat[:O_QN], qb[:, :QK_NOPE].reshape(HB * QK_NOPE, K), qp.reshape(HB * LANE, K),
                            tail, jnp.zeros((LANE - QK_ROPE, K), nat.dtype)], axis=0)


def _unpermute_pieces():
    pieces = [(0, 0, O_QN)]
    for h in range(HB):
        pieces.append((O_QN + h * QK_NOPE, O_QN + h * QK_B, QK_NOPE))
        pieces.append((O_QP + h * LANE, O_QN + h * QK_B + QK_NOPE, QK_ROPE))
    pieces.append((O_CKV, O_QN + W_QB, KV_RANK))
    pieces.append((O_KPE, O_QN + W_QB + KV_RANK, QK_ROPE))
    return pieces


def kernel(x, c, w_ada, b_ada, g_pre_attn, w_in, g_q_a, g_k_a, g_ckv, w_kv_b, g_out_a, g_out_b, w_out, g_post_attn, g_pre_mlp, w_mlp_in, w_mlp_out, g_post_mlp, loss_target, m_w_ada, m_b_ada, m_g_pre_attn, m_w_in, m_g_q_a, m_g_k_a, m_g_ckv, m_w_kv_b, m_g_out_a, m_g_out_b, m_w_out, m_g_post_attn, m_g_pre_mlp, m_w_mlp_in, m_w_mlp_out, m_g_post_mlp, v_w_ada, v_b_ada, v_g_pre_attn, v_w_in, v_g_q_a, v_g_k_a, v_g_ckv, v_w_kv_b, v_g_out_a, v_g_out_b, v_w_out, v_g_post_attn, v_g_pre_mlp, v_w_mlp_in, v_w_mlp_out, v_g_post_mlp):
    S, D = x.shape[1], x.shape[2]
    x2d, tgt = x[0], loss_target[0]
    ix, iy, ic = lax.axis_index("x"), lax.axis_index("y"), lax.axis_index("c")
    j_me = _chip(ix, iy)
    e_me = 4 * ix + 2 * iy + ic
    T = _pick(S, 256, 8)
    TB = _pick(S, 128, 8)

    where = jnp.stack([ic, j_me]).astype(jnp.int32)
    DFF = w_mlp_out.shape[1] * N_CHIPS

    c_all = _allgather8("gather_c", c.reshape(8, D // 8)).reshape(N_DEV, D)

    def silu_fn(cc):
        return (cc * (1.0 / (1.0 + jnp.exp(-cc))),), ()

    (c_act,), _ = _rowwise("silu_c", silu_fn, [c_all], [], [(D, BF16)], [], N_DEV)
    ca_pad = jnp.concatenate([c_act, jnp.zeros_like(c_act)], axis=0)
    n_mod = w_ada.shape[2]
    b_cols = lax.dynamic_slice(b_ada, (0, j_me * n_mod), (1, n_mod))
    (mod_part,) = _mm_nn("mod_mm", ca_pad, w_ada, [F32], epilogue=lambda acc, b: (acc + b,),
                         extras=(jnp.broadcast_to(b_cols, (16, n_mod)),))
    mod_all = _allgather8("gather_mod", mod_part[:8]).reshape(N_DEV, 8, n_mod)
    mod = jnp.concatenate([lax.dynamic_slice(mod_all, (2 * j, e_me, 0), (1, 1, n_mod))[0] for j in range(N_CHIPS)],
                          axis=1)

    w_in_t = jnp.swapaxes(w_in[0], 0, 1)
    h_qkv, tok_a = _gather_start("qkv", where, [w_in_t, w_kv_b[0]], (mod_all,))
    h_out, tok_b = _gather_start("out", where, [w_out[0]], (tok_a,), (tok_a,))
    h_up, tok_c = _gather_start("up", where, [w_mlp_in[0]], (tok_b,), (tok_b,))
    h_down, tok_d = _gather_start("down", where, [w_mlp_out[0]], (tok_c,), (tok_c,))
    mod = mod + tok_d[0, 0]
    sh_a, sc_a, gt_a, sh_m, sc_m, gt_m = [mod[:, i * D:(i + 1) * D] for i in range(6)]

    tabs_a = _rope_tables(S, HEAD_DIM, 1)
    tabs_b = _rope_tables(S, QK_ROPE, LANE // QK_ROPE)

    def pre_attn_fn(xt, g, sc, sh):
        return (_rn(xt) * g * (1.0 + sc) + sh,), ()

    (h_b,), _ = _rowwise("pre_attn", pre_attn_fn, [x2d], [g_pre_attn, sc_a, sh_a], [(D, BF16)], [], T)
    g_in, g_kvb = _gather_finish(h_qkv, h_b)
    wpt = _permute_w_in(g_in.reshape(W_IN, D))[None]
    (proj,) = _mm_nt("proj_mm", h_b, wpt, [F32])

    def prep_fn(pt, ca_, sla, sha, cb_, slb, shb, gq, gk, gc):
        return _prep(_split_proj(pt), ((ca_, sla, sha), (cb_, slb, shb)), gq, gk, gc), ()

    (qa_b, ka_b, va_b, qcat_b, ckvn_b, kpe_b), _ = _rowwise(
        "prep", prep_fn, [proj, *tabs_a, *tabs_b], [g_q_a, g_k_a, g_ckv],
        [(W_QA, BF16), (W_KA, BF16), (W_KA, BF16), (2 * HB * LANE, BF16), (KV_RANK, BF16), (LANE, BF16)], [], TB)
    (kv,) = _mm_nn("kv_mm", ckvn_b, g_kvb, [F32])

    def kcat_fn(kvt, kpet):
        kc, vs = [], []
        for h in range(HB):
            kc += [kvt[:, 2 * h * LANE:(2 * h + 1) * LANE], kpet.astype(F32)]
            vs.append(kvt[:, (2 * h + 1) * LANE:(2 * h + 2) * LANE])
        return (jnp.concatenate(kc, axis=1), jnp.concatenate(vs, axis=1)), ()

    (kcat_b, vb_b), _ = _rowwise("kcat", kcat_fn, [kv, kpe_b], [], [(2 * HB * LANE, BF16), (HB * V_DIM, BF16)], [], T)

    TQ = _pick(S, 512, 16)
    sc_a_ = 1.0 / math.sqrt(HEAD_DIM)
    sc_b_ = 1.0 / math.sqrt(QK_B)
    o_a, lse_a = _attn_fwd("attn_a_fwd", qa_b, ka_b, va_b, HA, HA // HKV, HEAD_DIM, HEAD_DIM, sc_a_, TQ)
    m_out, tok_m = _gather_mid(h_out, o_a)
    o_b, lse_b = _attn_fwd("attn_b_fwd", qcat_b, kcat_b, vb_b, HB, 1, 2 * LANE, V_DIM, sc_b_, TQ, after=(tok_m,))
    m_up, tok_m = _gather_mid(h_up, o_b)

    def mix_fn(oa, ob, ga, gb):
        return (jnp.concatenate([_rn(oa) * ga, _rn(ob) * gb], axis=1),), ()

    (on_b,), _ = _rowwise("mix_norm", mix_fn, [o_a, o_b], [g_out_a, g_out_b], [(2 * W_QA, BF16)], [], T,
                          after=(tok_m,))
    (g_wout,) = _gather_end(m_out, on_b)
    w_out_f = g_wout.reshape(1, N_CHIPS * g_wout.shape[1], D)
    (o2,) = _mm_nn("out_mm", on_b, w_out_f, [F32])

    def mid(xt, o2t, gta, gpa, gpm, scm, shm):
        x1 = xt + gta * (_rn(o2t) * gpa)
        return x1, _rn(x1) * gpm * (1.0 + scm) + shm

    def mid_fn(*a):
        return mid(*a), ()

    mid_vecs = [gt_a, g_post_attn, g_pre_mlp, sc_m, sh_m]
    (x1, h2_b), _ = _rowwise("mid", mid_fn, [x2d, o2], mid_vecs, [(D, F32), (D, BF16)], [], T)
    (g_mlp_in,) = _gather_end(m_up, h2_b)
    m_down, tok_m = _gather_mid(h_down, h2_b)
    u, a_b = _mm_nn("mlp_in_mm", h2_b, g_mlp_in, [F32, BF16],
                    epilogue=lambda acc: (acc, jnp.square(jnp.maximum(acc, 0.0))), after=(tok_m,))
    (g_mlp_out,) = _gather_end(m_down, a_b)
    w_mlp_out_f = g_mlp_out.reshape(1, DFF, D)
    (y,) = _mm_nn("mlp_out_mm", a_b, w_mlp_out_f, [F32])

    def last(yt, x1t, gtm, gpo):
        return x1t + gtm * (_rn(yt) * gpo)

    def last_fn(yt, x1t, tt, gtm, gpo):
        x2, vjp = jax.vjp(last, yt, x1t, gtm, gpo)
        err = x2 - tt
        dy, dx1, dgt, dg = vjp(err * (1.0 / D))
        loss = 0.5 * jnp.sum(jnp.mean(err * err, axis=-1))
        return (dy, dx1), (jnp.full((1, LANE), loss, F32), dgt, dg)

    (dy_b, dx2), (loss_v, d_gt_m, d_g_post_mlp) = _rowwise(
        "loss_bwd", last_fn, [y, x1, tgt], [gt_m, g_post_mlp], [(D, BF16), (D, F32)], [LANE, D, D], TB)

    gw_mlp_out = _mm_tn("gw_mlp_out_mm", a_b, dy_b, 1).reshape(N_CHIPS, DFF // N_CHIPS, D)
    b_down, tok = _reduce_begin("down", [gw_mlp_out])
    (du_b,) = _mm_nt("d_mlp_out_mm", dy_b, w_mlp_out_f, [BF16],
                     epilogue=lambda acc, ut: (acc * (2.0 * jnp.maximum(ut, 0.0)),), extras=(u,), after=(tok,))
    r_down, tok = _reduce_start(b_down, where, du_b)
    gw_mlp_in = _mm_tn("gw_mlp_in_mm", h2_b, du_b, N_CHIPS, after=(tok,))
    b_up, tok = _reduce_begin("up", [gw_mlp_in])
    (dh2,) = _mm_nt("d_mlp_in_mm", du_b, g_mlp_in, [F32], after=(tok,))
    r_up, tok_up = _reduce_start(b_up, where, dh2)

    def mid_bwd_fn(xt, o2t, dh2t, dx2t, *vecs):
        _, vjp = jax.vjp(mid, xt, o2t, *vecs)
        dx, do2, dgta, dgpa, dgpm, dscm, dshm = vjp((dx2t, dh2t))
        return (dx, do2), (dgta, dgpa, dgpm, dscm, dshm)

    (dx1, do2_b), (d_gt_a, d_g_post_attn, d_g_pre_mlp, d_sc_m, d_sh_m) = _rowwise(
        "mid_bwd", mid_bwd_fn, [x2d, o2, dh2, dx2], mid_vecs, [(D, F32), (D, BF16)], [D] * 5, TB, after=(tok_up,))

    (d_on,) = _mm_nt("d_out_mm", do2_b, w_out_f, [F32])
    gw_out = _mm_tn("gw_out_mm", on_b, do2_b, 1).reshape(N_CHIPS, w_out.shape[1], D)
    b_out, tok_out = _reduce_begin("out", [gw_out])

    def mix_bwd_fn(oa, ob, dont, ga, gb):
        _, vjp = jax.vjp(lambda a_, b_, g1, g2: mix_fn(a_, b_, g1, g2)[0][0], oa, ob, ga, gb)
        doa, dob, dga, dgb = vjp(dont)
        return (doa, dob), (dga, dgb)

    (doa_b, dob_b), (d_g_out_a, d_g_out_b) = _rowwise(
        "mix_bwd", mix_bwd_fn, [o_a, o_b, d_on], [g_out_a, g_out_b], [(W_QA, BF16), (HB * V_DIM, BF16)],
        [W_QA, HB * V_DIM], T, after=(tok_out,))

    dqa, dka, dva = _attn_bwd("attn_a_bwd", qa_b, ka_b, va_b, doa_b, o_a, lse_a, HA, HA // HKV, HEAD_DIM, HEAD_DIM,
                              sc_a_, TQ)
    r_out, tok = _reduce_start(b_out, where, dqa)
    dqcat, dkcat, dvb = _attn_bwd("attn_b_bwd", qcat_b, kcat_b, vb_b, dob_b, o_b, lse_b, HB, 1, 2 * LANE, V_DIM,
                                  sc_b_, TQ, after=(tok,))

    def kcat_bwd_fn(dkc, dvt):
        dkv, dkpe = [], None
        for h in range(HB):
            dkv += [dkc[:, 2 * h * LANE:(2 * h + 1) * LANE], dvt[:, h * V_DIM:(h + 1) * V_DIM]]
            piece = dkc[:, (2 * h + 1) * LANE:(2 * h + 2) * LANE]
            dkpe = piece if dkpe is None else dkpe + piece
        return (jnp.concatenate(dkv, axis=1), dkpe), ()

    (dkv_b, dkpe), _ = _rowwise("kcat_bwd", kcat_bwd_fn, [dkcat, dvb], [], [(2 * HB * LANE, BF16), (LANE, F32)], [], T)
    (d_ckvn,) = _mm_nt("d_kv_mm", dkv_b, g_kvb, [F32])
    gw_kvb = _mm_tn("gw_kv_mm", ckvn_b, dkv_b, N_CHIPS)
    b_kvb, tok_kvb = _reduce_begin("kvb", [gw_kvb])

    def prep_bwd_fn(pt, dqat, dkat, dvat, dqct, dckt, dkpt, ca_, sla, sha, cb_, slb, shb, gq, gk, gc):
        tabs = ((ca_, sla, sha), (cb_, slb, shb))
        _, vjp = jax.vjp(lambda sg, a1, a2, a3: _prep(sg, tabs, a1, a2, a3), _split_proj(pt), gq, gk, gc)
        (dqa_s, dka_s, dva_s, dqn_s, dqp_s, dck_s, dkp_s), dgq, dgk, dgc = vjp((dqat, dkat, dvat, dqct, dckt, dkpt))
        dproj = jnp.concatenate([*dqa_s, *dka_s, dva_s, *dqn_s, *dqp_s, dck_s, dkp_s], axis=1)
        return (dproj,), (dgq, dgk, dgc)

    (dproj_b,), (d_g_q_a, d_g_k_a, d_g_ckv) = _rowwise(
        "prep_bwd", prep_bwd_fn, [proj, dqa, dka, dva, dqcat, d_ckvn, dkpe, *tabs_a, *tabs_b],
        [g_q_a, g_k_a, g_ckv], [(W_INP, BF16)], [HEAD_DIM, HEAD_DIM, KV_RANK], TB, after=(tok_kvb,))

    r_kvb, tok = _reduce_start(b_kvb, where, dproj_b)
    (dh,) = _mm_nn("d_proj_mm", dproj_b, wpt, [F32], after=(tok,))
    gw_in = _mm_tn_rows("gw_in_mm", dproj_b, h_b, W_IN, _unpermute_pieces()).reshape(N_CHIPS, W_IN // N_CHIPS, D)
    b_in, tok = _reduce_begin("in", [gw_in])

    def pre_attn_bwd_fn(xt, dht, dx1t, g, sc, sh):
        _, vjp = jax.vjp(lambda a_, b_, c_, d_: pre_attn_fn(a_, b_, c_, d_)[0][0], xt, g, sc, sh)
        dxt, dg, dsc, dsh = vjp(dht)
        return (dx1t + dxt,), (dg, dsc, dsh)

    (grad_x,), (d_g_pre_attn, d_sc_a, d_sh_a) = _rowwise(
        "pre_attn_bwd", pre_attn_bwd_fn, [x2d, dh, dx1], [g_pre_attn, sc_a, sh_a], [(D, F32)], [D] * 3, TB,
        after=(tok,))

    small_names = ["b_ada", "g_pre_attn", "g_q_a", "g_k_a", "g_ckv", "g_out_a", "g_out_b", "g_post_attn",
                   "g_pre_mlp", "g_post_mlp"]
    small_local = jnp.concatenate([d_sh_a, d_sc_a, d_gt_a, d_sh_m, d_sc_m, d_gt_m, d_g_pre_attn, d_g_q_a, d_g_k_a,
                                   d_g_ckv, d_g_out_a, d_g_out_b, d_g_post_attn, d_g_pre_mlp, d_g_post_mlp], axis=1)
    n_small = small_local.shape[1]
    small_all = _allgather8("gather_small", small_local.reshape(8, n_small // 8)).reshape(N_DEV, n_small)

    def sum8_fn(t):
        return (), (jnp.sum(t, axis=0, keepdims=True),)

    r_in, tok_in = _reduce_start(b_in, where, grad_x, (small_all,))
    _, (small_sum,) = _rowwise("sum_small", sum8_fn, [small_all], [], [], [n_small], N_DEV, after=(tok_in,))
    dmod_cols = lax.dynamic_slice(small_all, (0, j_me * n_mod), (N_DEV, n_mod))
    dmod_pad = jnp.concatenate([dmod_cols, jnp.zeros_like(dmod_cols)], axis=0).astype(BF16)
    gw_ada = _mm_tn("gw_ada_mm", ca_pad, dmod_pad, 1)[0]

    weights = dict(w_ada=w_ada, b_ada=b_ada, g_pre_attn=g_pre_attn, w_in=w_in, g_q_a=g_q_a, g_k_a=g_k_a, g_ckv=g_ckv,
                   w_kv_b=w_kv_b, g_out_a=g_out_a, g_out_b=g_out_b, w_out=w_out, g_post_attn=g_post_attn,
                   g_pre_mlp=g_pre_mlp, w_mlp_in=w_mlp_in, w_mlp_out=w_mlp_out, g_post_mlp=g_post_mlp)
    ms = dict(w_ada=m_w_ada, b_ada=m_b_ada, g_pre_attn=m_g_pre_attn, w_in=m_w_in, g_q_a=m_g_q_a, g_k_a=m_g_k_a,
              g_ckv=m_g_ckv, w_kv_b=m_w_kv_b, g_out_a=m_g_out_a, g_out_b=m_g_out_b, w_out=m_w_out,
              g_post_attn=m_g_post_attn, g_pre_mlp=m_g_pre_mlp, w_mlp_in=m_w_mlp_in, w_mlp_out=m_w_mlp_out,
              g_post_mlp=m_g_post_mlp)
    vs = dict(w_ada=v_w_ada, b_ada=v_b_ada, g_pre_attn=v_g_pre_attn, w_in=v_w_in, g_q_a=v_g_q_a, g_k_a=v_g_k_a,
              g_ckv=v_g_ckv, w_kv_b=v_w_kv_b, g_out_a=v_g_out_a, g_out_b=v_g_out_b, w_out=v_w_out,
              g_post_attn=v_g_post_attn, g_pre_mlp=v_g_pre_mlp, w_mlp_in=v_w_mlp_in, w_mlp_out=v_w_mlp_out,
              g_post_mlp=v_g_post_mlp)
    order = list(weights)
    grads, deltas, new_m, new_v = {}, {}, {}, {}

    def update(name, g, after=()):
        view = (lambda t: jnp.swapaxes(t, 0, 1)) if name == "w_in" else (lambda t: t)
        d_, m_, v_ = _adamw("adamw_" + name, view(weights[name][0]), g, view(ms[name][0]), view(vs[name][0]), after)
        grads[name], deltas[name], new_m[name], new_v[name] = (view(t)[None] for t in (g, d_, m_, v_))
        return d_

    chain = (("w_mlp_out", r_down), ("w_mlp_in", r_up), ("w_out", r_out), ("w_kv_b", r_kvb), ("w_in", r_in))
    swap, tok = _reduce_finish(chain[0][1], where, small_sum)
    last = update("w_ada", gw_ada, (tok,))
    for k, (name, _) in enumerate(chain):
        (g,) = _reduce_end(swap, last)
        if k + 1 < len(chain):
            swap, tok = _reduce_finish(chain[k + 1][1], where, g)
        last = update(name, g, (tok,))

    cat = lambda d: jnp.concatenate([d[n] for n in small_names], axis=1)
    d_, m_, v_ = _adamw("adamw_small", cat(weights), small_sum, cat(ms), cat(vs))
    off = 0
    for n in small_names:
        w_ = weights[n].shape[1]
        grads[n], deltas[n], new_m[n], new_v[n] = (a[:, off:off + w_] for a in (small_sum, d_, m_, v_))
        off += w_

    loss = lax.psum(loss_v[0, 0], ("x", "y", "c"))
    return (loss, grad_x[None], *[grads[n] for n in order], *[deltas[n] for n in order],
            *[new_m[n] for n in order], *[new_v[n] for n in order])
```

```python
import functools
import math

import numpy as np
import jax
import jax.numpy as jnp
from jax import lax
from jax.experimental import pallas as pl
from jax.experimental.pallas import tpu as pltpu

F32 = jnp.float32
BF16 = jnp.bfloat16
MESH = pl.DeviceIdType.MESH

EPS = 1e-6
GRID_W = 64
ROPE_THETA = 10000.0
HEAD_DIM = 128
HA = 8
HKV = 2
HB = 8
QK_NOPE = 128
QK_ROPE = 64
V_DIM = 128
KV_RANK = 512
QK_B = QK_NOPE + QK_ROPE
W_QA = HA * HEAD_DIM
W_KA = HKV * HEAD_DIM
W_QB = HB * QK_B
W_IN = W_QA + 2 * W_KA + W_QB + KV_RANK + QK_ROPE
N_CHIPS = 4
N_DEV = 8

LANE = 128
O_QA = 0
O_KA = O_QA + W_QA
O_VA = O_KA + W_KA
O_QN = O_VA + W_KA
O_QP = O_QN + HB * QK_NOPE
O_CKV = O_QP + HB * LANE
O_KPE = O_CKV + KV_RANK
W_INP = O_KPE + LANE

ADAM_LR = 0.001
ADAM_B1 = 0.9
ADAM_B2 = 0.999
ADAM_EPS = 1e-08
ADAM_WD = 0.01
ADAM_STEP = 10

VMEM_LIMIT = 56 * 1024 * 1024
HEADS_PER_STEP = 2


def _params(sem):
    return pltpu.CompilerParams(dimension_semantics=sem, vmem_limit_bytes=VMEM_LIMIT)


def _pick(dim, target, unit):
    best = None
    t = unit
    while t <= min(dim, target):
        if dim % t == 0:
            best = t
        t += unit
    return dim if best is None else best


def _rn(x):
    return x * lax.rsqrt(jnp.mean(x * x, axis=-1, keepdims=True) + EPS)


def _rowwise(name, fn, rows, vecs, row_outs, vec_outs, tile, after=()):
    S = rows[0].shape[0]
    assert S % tile == 0
    n_r, n_v, n_ro, n_vo, n_a = len(rows), len(vecs), len(row_outs), len(vec_outs), len(after)

    def body(*refs):
        r_in = refs[:n_r]
        v_in = refs[n_r:n_r + n_v]
        r_out = refs[n_r + n_v + n_a:n_r + n_v + n_a + n_ro]
        v_out = refs[n_r + n_v + n_a + n_ro:]
        ro, vo = fn(*[r[...] for r in r_in], *[v[...] for v in v_in])
        for ref, val in zip(r_out, ro):
            ref[...] = val.astype(ref.dtype)
        if n_vo:
            @pl.when(pl.program_id(0) == 0)
            def _():
                for ref in v_out:
                    ref[...] = jnp.zeros(ref.shape, ref.dtype)
            for ref, val in zip(v_out, vo):
                ref[...] += val

    in_specs = [pl.BlockSpec((tile, a.shape[1]), lambda i: (i, 0)) for a in rows]
    in_specs += [pl.BlockSpec(a.shape, lambda i: (0, 0)) for a in vecs]
    in_specs += [pl.BlockSpec(memory_space=pl.ANY) for _ in after]
    out_specs = [pl.BlockSpec((tile, w), lambda i: (i, 0)) for w, _ in row_outs]
    out_specs += [pl.BlockSpec((1, w), lambda i: (0, 0)) for w in vec_outs]
    out_shape = [jax.ShapeDtypeStruct((S, w), dt) for w, dt in row_outs]
    out_shape += [jax.ShapeDtypeStruct((1, w), F32) for w in vec_outs]
    res = pl.pallas_call(
        body, name=name, grid=(S // tile,), in_specs=in_specs, out_specs=out_specs, out_shape=out_shape,
        compiler_params=_params(("arbitrary",)),
    )(*rows, *vecs, *after)
    return res[:n_ro], res[n_ro:]


CONTRACT_TILE = 2048
MM_TILE_BYTES = 40 * 1024 * 1024


def _rows_that_fit(M, tile_bytes):
    t = _pick(M, 1024, 16)
    while t % 32 == 0 and tile_bytes(t) > MM_TILE_BYTES:
        t //= 2
    return t


def _accumulate(acc, part, step, n_steps, finish):
    if n_steps == 1:
        finish(part)
        return

    @pl.when(step == 0)
    def _():
        acc[...] = part

    @pl.when(step > 0)
    def _():
        acc[...] += part

    @pl.when(step == n_steps - 1)
    def _():
        finish(acc[...])


def _mm_nn(name, a, b3, out_dtypes, epilogue=None, extras=(), after=()):
    M, K = a.shape
    nB, K2, Nsh = b3.shape
    assert K == K2
    N = nB * Nsh
    tn, tk = _pick(Nsh, 1536, LANE), _pick(K, CONTRACT_TILE, LANE)
    out_bytes = sum(jnp.dtype(dt).itemsize for dt in out_dtypes) + sum(e.dtype.itemsize for e in extras)
    tm = _rows_that_fit(M, lambda t: 2 * (t * tk * a.dtype.itemsize + tk * tn * b3.dtype.itemsize)
                        + t * tn * (4 * (K > tk) + 2 * out_bytes))
    nps, nk, ne, no, na = Nsh // tn, K // tk, len(extras), len(out_dtypes), len(after)

    def body(a_ref, b_ref, *rest):
        e_refs, o_refs = rest[:ne], rest[ne + na:ne + na + no]

        def finish(total):
            vals = (total,) if epilogue is None else epilogue(total, *[e[...] for e in e_refs])
            for ref, val in zip(o_refs, vals):
                ref[...] = val.astype(ref.dtype)

        part = jnp.dot(a_ref[...].astype(BF16), b_ref[...].astype(BF16), preferred_element_type=F32)
        _accumulate(rest[-1], part, pl.program_id(2), nk, finish)

    in_specs = [pl.BlockSpec((tm, tk), lambda i, j, k: (i, k)),
                pl.BlockSpec((None, tk, tn), lambda i, j, k: (j // nps, k, j % nps))]
    in_specs += [pl.BlockSpec((tm, tn), lambda i, j, k: (i, j)) for _ in extras]
    in_specs += [pl.BlockSpec(memory_space=pl.ANY) for _ in after]
    return pl.pallas_call(
        body, name=name, grid=(M // tm, N // tn, nk), in_specs=in_specs,
        out_specs=[pl.BlockSpec((tm, tn), lambda i, j, k: (i, j)) for _ in out_dtypes],
        out_shape=[jax.ShapeDtypeStruct((M, N), dt) for dt in out_dtypes],
        scratch_shapes=[pltpu.VMEM((tm, tn), F32)] if nk > 1 else [],
        compiler_params=_params(("parallel", "parallel", "arbitrary")),
    )(a, b3, *extras, *after)


def _mm_nt(name, a, b3, out_dtypes, epilogue=None, extras=(), after=()):
    M, N = a.shape
    nB, K, Nsh = b3.shape
    assert N == nB * Nsh
    to, tn = _pick(K, 1536, LANE), _pick(Nsh, CONTRACT_TILE, LANE)
    out_bytes = sum(jnp.dtype(dt).itemsize for dt in out_dtypes) + sum(e.dtype.itemsize for e in extras)
    tm = _rows_that_fit(M, lambda t: 2 * (t * tn * a.dtype.itemsize + to * tn * b3.dtype.itemsize)
                        + t * to * (4 * (N > tn) + 2 * out_bytes))
    nps, nn, ne, no, na = Nsh // tn, N // tn, len(extras), len(out_dtypes), len(after)

    def body(a_ref, b_ref, *rest):
        e_refs, o_refs = rest[:ne], rest[ne + na:ne + na + no]

        def finish(total):
            vals = (total,) if epilogue is None else epilogue(total, *[e[...] for e in e_refs])
            for ref, val in zip(o_refs, vals):
                ref[...] = val.astype(ref.dtype)

        part = lax.dot_general(a_ref[...].astype(BF16), b_ref[...].astype(BF16),
                               (((1,), (1,)), ((), ())), preferred_element_type=F32)
        _accumulate(rest[-1], part, pl.program_id(2), nn, finish)

    in_specs = [pl.BlockSpec((tm, tn), lambda i, o, n: (i, n)),
                pl.BlockSpec((None, to, tn), lambda i, o, n: (n // nps, o, n % nps))]
    in_specs += [pl.BlockSpec((tm, to), lambda i, o, n: (i, o)) for _ in extras]
    in_specs += [pl.BlockSpec(memory_space=pl.ANY) for _ in after]
    return pl.pallas_call(
        body, name=name, grid=(M // tm, K // to, nn), in_specs=in_specs,
        out_specs=[pl.BlockSpec((tm, to), lambda i, o, n: (i, o)) for _ in out_dtypes],
        out_shape=[jax.ShapeDtypeStruct((M, K), dt) for dt in out_dtypes],
        scratch_shapes=[pltpu.VMEM((tm, to), F32)] if nn > 1 else [],
        compiler_params=_params(("parallel", "parallel", "arbitrary")),
    )(a, b3, *extras, *after)


def _mm_tn(name, a, g, nB, after=()):
    M, K = a.shape
    M2, N = g.shape
    assert M == M2 and N % nB == 0
    Nsh = N // nB
    tk, tn, tm = _pick(K, 1536, LANE), _pick(Nsh, 1024, LANE), _pick(M, CONTRACT_TILE, 16)
    nps, nm = Nsh // tn, M // tm

    def body(a_ref, g_ref, *rest):
        o_ref = rest[len(after)]

        def finish(total):
            o_ref[...] = total

        part = lax.dot_general(a_ref[...].astype(BF16), g_ref[...].astype(BF16),
                               (((0,), (0,)), ((), ())), preferred_element_type=F32)
        _accumulate(rest[-1], part, pl.program_id(2), nm, finish)

    return pl.pallas_call(
        body, name=name, grid=(K // tk, N // tn, nm),
        in_specs=[pl.BlockSpec((tm, tk), lambda k, n, m: (m, k)),
                  pl.BlockSpec((tm, tn), lambda k, n, m: (m, n))] + [pl.BlockSpec(memory_space=pl.ANY) for _ in after],
        out_specs=pl.BlockSpec((None, tk, tn), lambda k, n, m: (n // nps, k, n % nps)),
        out_shape=jax.ShapeDtypeStruct((nB, K, Nsh), F32),
        scratch_shapes=[pltpu.VMEM((tk, tn), F32)] if nm > 1 else [],
        compiler_params=_params(("parallel", "parallel", "arbitrary")),
    )(a, g, *after)


def _mm_tn_rows(name, a, g, n_rows, pieces):
    M, K = a.shape
    M2, N = g.shape
    assert M == M2
    tk, tn, tm = _pick(K, 1536, LANE), _pick(N, 512, LANE), _pick(M, CONTRACT_TILE, 16)
    nk, nm = K // tk, M // tm

    def body(a_ref, g_ref, o_ref, *acc):
        kk = pl.program_id(1)

        def finish(total):
            for t in range(nk):
                @pl.when(kk == t)
                def _(t=t):
                    for src, dst, n in pieces:
                        lo, hi = max(src, t * tk), min(src + n, (t + 1) * tk)
                        if lo < hi:
                            o_ref[dst + lo - src:dst + hi - src, :] = total[lo - t * tk:hi - t * tk, :]

        part = lax.dot_general(a_ref[...].astype(BF16), g_ref[...].astype(BF16),
                               (((0,), (0,)), ((), ())), preferred_element_type=F32)
        _accumulate(acc[0] if acc else None, part, pl.program_id(2), nm, finish)

    return pl.pallas_call(
        body, name=name, grid=(N // tn, nk, nm),
        in_specs=[pl.BlockSpec((tm, tk), lambda n, k, m: (m, k)),
                  pl.BlockSpec((tm, tn), lambda n, k, m: (m, n))],
        out_specs=pl.BlockSpec((n_rows, tn), lambda n, k, m: (0, n)),
        out_shape=jax.ShapeDtypeStruct((n_rows, N), F32),
        scratch_shapes=[pltpu.VMEM((tk, tn), F32)] if nm > 1 else [],
        compiler_params=_params(("arbitrary", "arbitrary", "arbitrary")),
    )(a, g)


def _attn_fwd(name, q, k, v, n_heads, group, dqk, dv, scale, tq, after=()):
    S = q.shape[0]
    hp = 2 * HEADS_PER_STEP
    kvp = max(1, hp // group)

    def body(q_ref, k_ref, v_ref, *rest):
        o_ref, lse_ref = rest[-2:]
        for j in range(hp):
            jk = j * kvp // hp
            s = lax.dot_general(q_ref[:, j * dqk:(j + 1) * dqk], k_ref[:, jk * dqk:(jk + 1) * dqk],
                                (((1,), (1,)), ((), ())), preferred_element_type=F32)
            m = jnp.max(s, axis=-1, keepdims=True)
            e = jnp.exp((s - m) * scale)
            l = jnp.sum(e, axis=-1, keepdims=True)
            o_ref[:, j * dv:(j + 1) * dv] = jnp.dot(e.astype(BF16), v_ref[:, jk * dv:(jk + 1) * dv],
                                                    preferred_element_type=F32) * (1.0 / l)
            lse_ref[:, j * LANE:(j + 1) * LANE] = jnp.broadcast_to(m * scale + jnp.log(l), (tq, LANE))

    kv_block = lambda h, i: (0, (h * hp // group) // kvp)
    return pl.pallas_call(
        body, name=name, grid=(n_heads // hp, S // tq),
        in_specs=[pl.BlockSpec((tq, hp * dqk), lambda h, i: (i, h)),
                  pl.BlockSpec((S, kvp * dqk), kv_block),
                  pl.BlockSpec((S, kvp * dv), kv_block)] + [pl.BlockSpec(memory_space=pl.ANY) for _ in after],
        out_specs=[pl.BlockSpec((tq, hp * dv), lambda h, i: (i, h)),
                   pl.BlockSpec((tq, hp * LANE), lambda h, i: (i, h))],
        out_shape=[jax.ShapeDtypeStruct((S, n_heads * dv), F32), jax.ShapeDtypeStruct((S, n_heads * LANE), F32)],
        compiler_params=_params(("arbitrary", "arbitrary")),
    )(q, k, v, *after)


def _attn_bwd(name, q, k, v, do, o, lse, n_heads, group, dqk, dv, scale, tq, after=()):
    S = q.shape[0]
    n_kv = n_heads // group
    hp = HEADS_PER_STEP
    kvp = max(1, hp // group)

    def body(q_ref, k_ref, v_ref, do_ref, o_ref, lse_ref, *rest):
        dq_ref, dk_ref, dv_ref = rest[len(after):]
        h, i = pl.program_id(0), pl.program_id(1)

        @pl.when(((h * hp) % group == 0) & (i == 0))
        def _():
            dk_ref[...] = jnp.zeros(dk_ref.shape, F32)
            dv_ref[...] = jnp.zeros(dv_ref.shape, F32)

        for j in range(hp):
            jk = j * kvp // hp
            qb, dob = q_ref[:, j * dqk:(j + 1) * dqk], do_ref[:, j * dv:(j + 1) * dv]
            kb, vb = k_ref[:, jk * dqk:(jk + 1) * dqk], v_ref[:, jk * dv:(jk + 1) * dv]
            s = lax.dot_general(qb, kb, (((1,), (1,)), ((), ())), preferred_element_type=F32)
            p = jnp.exp(s * scale - jnp.tile(lse_ref[:, j * LANE:(j + 1) * LANE], (1, S // LANE)))
            dp = lax.dot_general(dob, vb, (((1,), (1,)), ((), ())), preferred_element_type=F32)
            delta = jnp.sum(dob.astype(F32) * o_ref[:, j * dv:(j + 1) * dv], axis=-1, keepdims=True)
            ds = (p * ((dp - delta) * scale)).astype(BF16)
            dv_ref[:, jk * dv:(jk + 1) * dv] += lax.dot_general(p.astype(BF16), dob, (((0,), (0,)), ((), ())),
                                                                preferred_element_type=F32)
            dq_ref[:, j * dqk:(j + 1) * dqk] = jnp.dot(ds, kb, preferred_element_type=F32)
            dk_ref[:, jk * dqk:(jk + 1) * dqk] += lax.dot_general(ds, qb, (((0,), (0,)), ((), ())),
                                                                  preferred_element_type=F32)

    kv_block = lambda h, i: (0, (h * hp // group) // kvp)
    return pl.pallas_call(
        body, name=name, grid=(n_heads // hp, S // tq),
        in_specs=[pl.BlockSpec((tq, hp * dqk), lambda h, i: (i, h)),
                  pl.BlockSpec((S, kvp * dqk), kv_block),
                  pl.BlockSpec((S, kvp * dv), kv_block),
                  pl.BlockSpec((tq, hp * dv), lambda h, i: (i, h)),
                  pl.BlockSpec((tq, hp * dv), lambda h, i: (i, h)),
                  pl.BlockSpec((tq, hp * LANE), lambda h, i: (i, h))] + [pl.BlockSpec(memory_space=pl.ANY) for _ in after],
        out_specs=[pl.BlockSpec((tq, hp * dqk), lambda h, i: (i, h)),
                   pl.BlockSpec((S, kvp * dqk), kv_block),
                   pl.BlockSpec((S, kvp * dv), kv_block)],
        out_shape=[jax.ShapeDtypeStruct((S, n_heads * dqk), F32),
                   jax.ShapeDtypeStruct((S, n_kv * dqk), F32),
                   jax.ShapeDtypeStruct((S, n_kv * dv), F32)],
        compiler_params=_params(("arbitrary", "arbitrary")),
    )(q, k, v, do, o, lse, *after)


def _adamw(name, w, g, m, v, after=()):
    R, C = w.shape
    tile = _pick(R, max(8, (1 << 19) // C // 8 * 8), 8)
    c1 = 1.0 - ADAM_B1 ** ADAM_STEP
    c2 = 1.0 - ADAM_B2 ** ADAM_STEP

    def body(w_ref, g_ref, m_ref, v_ref, *rest):
        d_ref, nm_ref, nv_ref = rest[len(after):]
        gg = g_ref[...]
        nm = ADAM_B1 * m_ref[...] + (1.0 - ADAM_B1) * gg
        nv = ADAM_B2 * v_ref[...] + (1.0 - ADAM_B2) * (gg * gg)
        d_ref[...] = -ADAM_LR * ((nm / c1) / (jnp.sqrt(nv / c2) + ADAM_EPS) + ADAM_WD * w_ref[...])
        nm_ref[...] = nm
        nv_ref[...] = nv

    spec = pl.BlockSpec((tile, C), lambda i: (i, 0))
    return pl.pallas_call(
        body, name=name, grid=(R // tile,), in_specs=[spec] * 4 + [pl.BlockSpec(memory_space=pl.ANY) for _ in after],
        out_specs=[spec] * 3, out_shape=[jax.ShapeDtypeStruct((R, C), F32)] * 3,
        compiler_params=_params(("parallel",)),
    )(w, g, m, v, *after)


def _flip(me, rel):
    return tuple(1 - p if r else p for p, r in zip(me, rel))


def _push(name, ins, out_shapes, transfers, aliases=None):
    ni, no, nt = len(ins), len(out_shapes), len(transfers)

    def body(*refs):
        in_refs, out_refs = refs[:ni], refs[ni:ni + no]
        send_sems, recv_sems = refs[ni + no], refs[ni + no + 1]
        me = (lax.axis_index("x"), lax.axis_index("y"), lax.axis_index("c"))
        copies = []
        for t, (si, sfn, oi, dfn, rel) in enumerate(transfers):
            src_ref = out_refs[si[1]] if isinstance(si, tuple) else in_refs[si]
            src = src_ref.at[sfn(*me)]
            dst = out_refs[oi].at[dfn(*me)]
            if rel is None:
                cp = pltpu.make_async_copy(src, dst, send_sems.at[t])
            else:
                cp = pltpu.make_async_remote_copy(src_ref=src, dst_ref=dst, send_sem=send_sems.at[t],
                                                  recv_sem=recv_sems.at[t], device_id=_flip(me, rel),
                                                  device_id_type=MESH)
            cp.start()
            copies.append(cp)
        for cp in copies:
            cp.wait()

    any_spec = pl.BlockSpec(memory_space=pl.ANY)
    return pl.pallas_call(
        body, name=name, in_specs=[any_spec] * ni, out_specs=[any_spec] * no, out_shape=out_shapes,
        scratch_shapes=[pltpu.SemaphoreType.DMA((nt,)), pltpu.SemaphoreType.DMA((nt,))],
        input_output_aliases=aliases or {},
        compiler_params=pltpu.CompilerParams(has_side_effects=True),
    )(*ins)


_HBM = pl.BlockSpec(memory_space=pltpu.HBM)
_SEM = pl.BlockSpec(memory_space=pltpu.SEMAPHORE)
_DATAFLOW = pltpu.SideEffectType.DATAFLOW_SIDE_EFFECTING


def _split_copies(refs, transfers, send_sems, recv_sems):
    me = (lax.axis_index("x"), lax.axis_index("y"), lax.axis_index("c"))
    return [pltpu.make_async_remote_copy(src_ref=refs[sb].at[sfn(*me)], dst_ref=refs[db].at[dfn(*me)],
                                         send_sem=send_sems.at[t], recv_sem=recv_sems.at[t],
                                         device_id=_flip(me, rel), device_id_type=MESH)
            for t, (sb, sfn, db, dfn, rel) in enumerate(transfers)]


def _push_start(name, bufs, transfers, after=()):
    nb, na, nt = len(bufs), len(after), len(transfers)

    def body(*refs):
        send_sems, recv_sems, token = refs[nb + na], refs[nb + na + 1], refs[-1]
        for cp in _split_copies(refs[nb + na + 2:2 * nb + na + 2], transfers, send_sems, recv_sems):
            cp.start()
        token[...] = jnp.zeros(token.shape, token.dtype)

    res = pl.pallas_call(
        body, name=name,
        out_shape=(pltpu.SemaphoreType.DMA((nt,)), pltpu.SemaphoreType.DMA((nt,)),
                   *[pltpu.HBM(b.shape, b.dtype) for b in bufs], jax.ShapeDtypeStruct((8, LANE), F32)),
        in_specs=[_HBM] * nb + [pl.BlockSpec(memory_space=pl.ANY)] * na,
        out_specs=(_SEM, _SEM, *[_HBM] * nb, pl.BlockSpec(memory_space=pltpu.VMEM)),
        input_output_aliases={i: i + 2 for i in range(nb)},
        compiler_params=pltpu.CompilerParams(has_side_effects=_DATAFLOW),
    )(*[pltpu.with_memory_space_constraint(b, pltpu.HBM) for b in bufs], *after)
    return res[0], res[1], list(res[2:2 + nb]), res[-1]


def _push_wait(name, send_sems, recv_sems, bufs, transfers, after):
    nb = len(bufs)

    def body(*refs):
        for cp in _split_copies(refs[:nb], transfers, refs[nb], refs[nb + 1]):
            cp.wait_send()
            cp.wait_recv()

    res = pl.pallas_call(
        body, name=name, out_shape=[pltpu.HBM(b.shape, b.dtype) for b in bufs],
        in_specs=[_HBM] * nb + [_SEM, _SEM, pl.BlockSpec(memory_space=pl.ANY)], out_specs=[_HBM] * nb,
        input_output_aliases={i: i for i in range(nb)},
        compiler_params=pltpu.CompilerParams(has_side_effects=_DATAFLOW),
    )(*bufs, send_sems, recv_sems, after)
    return list(res)


ICI_RELS = ((1, 0, 0), (0, 1, 0), (1, 1, 0))
SIBLING = (0, 0, 1)


def _chip(x, y):
    return 2 * x + y


def _cast_into_slot(name, where, w, after=()):
    R, C = w.shape
    tile = _pick(R, 512, BF16_ROWS)

    def body(where_ref, w_ref, *rest):
        rest[-1][...] = w_ref[...].astype(BF16)

    return pl.pallas_call(
        body, name=name,
        grid_spec=pltpu.PrefetchScalarGridSpec(
            num_scalar_prefetch=1, grid=(R // tile,),
            in_specs=[pl.BlockSpec((tile, C), lambda i, wh: (i, 0))] + [pl.BlockSpec(memory_space=pl.ANY) for _ in after],
            out_specs=pl.BlockSpec((None, tile, C), lambda i, wh: (wh[1], i, 0))),
        out_shape=jax.ShapeDtypeStruct((N_CHIPS, R, C), BF16),
        compiler_params=_params(("parallel",)),
    )(where, w, *after)


BF16_ROWS = 16


def _rows_split(shape2):
    return (shape2[0] // 2) % BF16_ROWS == 0


def _half_shape(shape2):
    R, C = shape2
    return (R // 2, C) if _rows_split(shape2) else (R, C // 2)


def _half(shape2, c):
    R, C = shape2
    if _rows_split(shape2):
        return (pl.ds(c * (R // 2), R // 2), slice(None))
    return (slice(None), pl.ds(pl.multiple_of(c * (C // 2), LANE), C // 2))


def _gather_start(tag, where, shards, after, cast_after=()):
    n = len(shards)
    own = [_cast_into_slot(f"cast_w_{tag}{w}", where, shards[w], cast_after) for w in range(n)]
    transfers = []
    for w in range(n):
        def idx(x, y, c, shape2=shards[w].shape):
            return (_chip(x, y), *_half(shape2, c))
        transfers += [(w, idx, w, idx, rel) for rel in ICI_RELS]
    send_sems, recv_sems, bufs, token = _push_start(f"gather_{tag}_start", own, transfers, after)
    return (tag, send_sems, recv_sems, bufs, transfers), token


def _forward_transfers(part):
    second = []
    for w, p in enumerate(part):
        for rel in ICI_RELS:
            def idx(x, y, c, shape2=p.shape[1:], rel=rel):
                return (_chip(x ^ rel[0], y ^ rel[1]), *_half(shape2, c))
            second.append((w, idx, w, idx, SIBLING))
    return second


def _gather_finish(handle, after):
    tag, send_sems, recv_sems, bufs, transfers = handle
    part = _push_wait(f"gather_{tag}_wait", send_sems, recv_sems, bufs, transfers, after)
    outs = [jax.ShapeDtypeStruct(p.shape, p.dtype) for p in part]
    second = [(("out", sb), sfn, db, dfn, rel) for sb, sfn, db, dfn, rel in _forward_transfers(part)]
    return _push(f"gather_{tag}_d2d", list(part), outs, second, aliases={w: w for w in range(len(part))})


def _gather_mid(handle, after):
    tag, send_sems, recv_sems, bufs, transfers = handle
    part = _push_wait(f"gather_{tag}_wait", send_sems, recv_sems, bufs, transfers, after)
    second = _forward_transfers(part)
    s2, r2, bufs2, token = _push_start(f"gather_{tag}_d2d_start", part, second)
    return (tag, s2, r2, bufs2, second), token


def _gather_end(handle, after):
    tag, send_sems, recv_sems, bufs, transfers = handle
    return _push_wait(f"gather_{tag}_d2d_wait", send_sems, recv_sems, bufs, transfers, after)


def _reduce_begin(tag, grads, after=()):
    n = len(grads)
    land = [lax.empty((N_CHIPS, *_half_shape(g.shape[1:])), F32) for g in grads]
    t1 = []
    for w in range(n):
        t1.append((w, lambda x, y, c, shape2=grads[w].shape[1:]: (slice(None), *_half(shape2, 1 - c)), n + w,
                   lambda x, y, c: (slice(None), slice(None), slice(None)), SIBLING))
    send_sems, recv_sems, bufs, token = _push_start(f"reduce_{tag}_d2d_start", list(grads) + land, t1, after)
    return (tag, send_sems, recv_sems, bufs, t1), token


def _reduce_start(handle, where, after, after_start=()):
    tag, send_sems, recv_sems, bufs, t1 = handle
    n = len(bufs) // 2
    res = _push_wait(f"reduce_{tag}_d2d_wait", send_sems, recv_sems, bufs, t1, after)
    grads, from_sib = res[:n], res[n:]
    pair = [_pair_sum(f"pair_sum_{tag}{w}", where, grads[w], from_sib[w]) for w in range(n)]
    land = [lax.empty((3, *_half_shape(g.shape[1:])), BF16) for g in grads]
    t3 = []
    for w in range(n):
        for r, rel in enumerate(ICI_RELS):
            t3.append((w, lambda x, y, c, rel=rel: (_chip(x ^ rel[0], y ^ rel[1]), slice(None), slice(None)), n + w,
                       lambda x, y, c, r=r: (r, slice(None), slice(None)), rel))
    send_sems, recv_sems, bufs, token = _push_start(f"reduce_{tag}_start", pair + land, t3, after_start)
    return (tag, send_sems, recv_sems, bufs, t3, list(grads), list(from_sib)), token


def _reduce_finish(handle, where, after):
    tag, send_sems, recv_sems, bufs, t3, grads, from_sib = handle
    n = len(grads)
    from_chips = _push_wait(f"reduce_{tag}_wait", send_sems, recv_sems, bufs, t3, after)[n:]
    mine = [_chip_sum(f"chip_sum_{tag}{w}", where, grads[w], from_sib[w], from_chips[w]) for w in range(n)]
    t5 = []
    for w in range(n):
        def idx(x, y, c, shape2=grads[w].shape[1:]):
            return _half(shape2, c)
        t5.append((w, idx, w, idx, SIBLING))
    send_sems, recv_sems, bufs, token = _push_start(f"reduce_{tag}_swap_start", mine, t5)
    return (tag, send_sems, recv_sems, bufs, t5), token


def _reduce_end(handle, after):
    tag, send_sems, recv_sems, bufs, t5 = handle
    return _push_wait(f"reduce_{tag}_swap_wait", send_sems, recv_sems, bufs, t5, after)


def _pair_sum(name, where, g, from_sib):
    _, hr, hc = from_sib.shape
    tile = _pick(hr, max(16, (1 << 19) // hc // 16 * 16), 16)
    nb = hr // tile
    slot = lambda k, wh: (wh[1] + 1 + k) % N_CHIPS
    if _rows_split(g.shape[1:]):
        mine = lambda k, i, wh: (slot(k, wh), wh[0] * nb + i, 0)
    else:
        mine = lambda k, i, wh: (slot(k, wh), i, wh[0])

    def body(where_ref, g_ref, s_ref, o_ref):
        o_ref[...] = (g_ref[...] + s_ref[...]).astype(BF16)

    return pl.pallas_call(
        body, name=name,
        grid_spec=pltpu.PrefetchScalarGridSpec(
            num_scalar_prefetch=1, grid=(N_CHIPS - 1, nb),
            in_specs=[pl.BlockSpec((None, tile, hc), mine),
                      pl.BlockSpec((None, tile, hc), lambda k, i, wh: (slot(k, wh), i, 0))],
            out_specs=pl.BlockSpec((None, tile, hc), lambda k, i, wh: (slot(k, wh), i, 0))),
        out_shape=jax.ShapeDtypeStruct((N_CHIPS, hr, hc), BF16),
        compiler_params=_params(("parallel", "parallel")),
    )(where, g, from_sib)


def _chip_sum(name, where, g, from_sib, from_chips):
    _, R, C = g.shape
    _, hr, hc = from_sib.shape
    tile = _pick(hr, max(16, (1 << 19) // hc // 16 * 16), 16)
    nb = hr // tile
    if _rows_split((R, C)):
        mine = lambda i, wh: (wh[0] * nb + i, 0)
    else:
        mine = lambda i, wh: (i, wh[0])

    def body(where_ref, g_ref, s_ref, r_ref, o_ref):
        acc = g_ref[...] + s_ref[...]
        for r in range(3):
            acc = acc + r_ref[r].astype(F32)
        o_ref[...] = acc

    return pl.pallas_call(
        body, name=name,
        grid_spec=pltpu.PrefetchScalarGridSpec(
            num_scalar_prefetch=1, grid=(nb,),
            in_specs=[pl.BlockSpec((None, tile, hc), lambda i, wh: (wh[1], *mine(i, wh))),
                      pl.BlockSpec((None, tile, hc), lambda i, wh: (wh[1], i, 0)),
                      pl.BlockSpec((3, tile, hc), lambda i, wh: (0, i, 0))],
            out_specs=pl.BlockSpec((tile, hc), mine)),
        out_shape=jax.ShapeDtypeStruct((R, C), F32),
        compiler_params=_params(("parallel",)),
    )(where, g, from_sib, from_chips)


def _allgather8(name, blk):
    m_per, n = blk.shape
    rels = [(0, 0, 1), (1, 0, 0), (0, 1, 0), (1, 1, 0), (1, 0, 1), (0, 1, 1), (1, 1, 1)]

    def body(x_ref, out_ref, send_sems, recv_sems):
        me = (lax.axis_index("x"), lax.axis_index("y"), lax.axis_index("c"))
        my_rows = out_ref.at[pl.ds((4 * me[0] + 2 * me[1] + me[2]) * m_per, m_per), :]
        out_ref[pl.ds((4 * me[0] + 2 * me[1] + me[2]) * m_per, m_per), :] = x_ref[...]
        copies = []
        for t, rel in enumerate(rels):
            cp = pltpu.make_async_remote_copy(src_ref=x_ref, dst_ref=my_rows, send_sem=send_sems.at[t],
                                              recv_sem=recv_sems.at[t], device_id=_flip(me, rel),
                                              device_id_type=MESH)
            cp.start()
            copies.append(cp)
        for cp in copies:
            cp.wait()

    return pl.pallas_call(
        body, name=name, out_shape=jax.ShapeDtypeStruct((N_DEV * m_per, n), blk.dtype),
        in_specs=[pl.BlockSpec(memory_space=pltpu.VMEM)], out_specs=pl.BlockSpec(memory_space=pltpu.VMEM),
        scratch_shapes=[pltpu.SemaphoreType.DMA((7,)), pltpu.SemaphoreType.DMA((7,))],
        compiler_params=pltpu.CompilerParams(has_side_effects=True, vmem_limit_bytes=VMEM_LIMIT),
    )(blk)


def _rope_tables(seq_len, dim, reps):
    rows = seq_len // GRID_W
    t = np.arange(seq_len)
    row, col = (t // GRID_W).astype(np.float32), (t % GRID_W).astype(np.float32)
    half = dim // 2
    inv = jnp.asarray(ROPE_THETA, F32) ** (-jnp.arange(0, half, 2, dtype=F32) / half)
    ang_r = jnp.asarray(row)[:, None] * inv[None, :]
    ang_c = jnp.asarray(col)[:, None] * inv[None, :]
    ang = jnp.concatenate([ang_r, ang_r, ang_c, ang_c], axis=-1)
    cos, sin = jnp.cos(ang), jnp.sin(ang)
    low = (np.arange(dim) % (dim // 2)) < (dim // 4)
    s_lo = jnp.where(jnp.asarray(low)[None, :], -sin, 0.0)
    s_hi = jnp.where(jnp.asarray(low)[None, :], 0.0, sin)
    del rows
    return tuple(jnp.tile(a, (1, reps)) for a in (cos, s_lo, s_hi))


def _make_rope(shift):
    def up(v):
        return pltpu.roll(v, LANE - shift, 1)

    def down(v):
        return pltpu.roll(v, shift, 1)

    @jax.custom_vjp
    def rope(v, cos, s_lo, s_hi):
        return v * cos + up(v) * s_lo + down(v) * s_hi

    def fwd(v, cos, s_lo, s_hi):
        return rope(v, cos, s_lo, s_hi), (cos, s_lo, s_hi)

    def bwd(res, dy):
        cos, s_lo, s_hi = res
        return dy * cos + down(dy * s_lo) + up(dy * s_hi), jnp.zeros_like(cos), jnp.zeros_like(cos), jnp.zeros_like(cos)

    rope.defvjp(fwd, bwd)
    return rope


_rope_a = _make_rope(HEAD_DIM // 4)
_rope_b = _make_rope(QK_ROPE // 4)


def _prep(segs, tabs, g_q, g_k, g_ckv):
    qa, ka, va, qn, qp, ckv, kpe = segs
    ta, tb = tabs
    q_a = jnp.concatenate([_rope_a(_rn(s) * g_q, *ta) for s in qa], axis=1)
    k_a = jnp.concatenate([_rope_a(_rn(s) * g_k, *ta) for s in ka], axis=1)
    cat = []
    for h in range(HB):
        cat += [qn[h], _rope_b(qp[h], *tb)]
    q_cat = jnp.concatenate(cat, axis=1)
    return q_a, k_a, va, q_cat, _rn(ckv) * g_ckv, _rope_b(kpe, *tb)


def _split_proj(proj):
    qa = [proj[:, O_QA + h * LANE:O_QA + (h + 1) * LANE] for h in range(HA)]
    ka = [proj[:, O_KA + h * LANE:O_KA + (h + 1) * LANE] for h in range(HKV)]
    va = proj[:, O_VA:O_QN]
    qn = [proj[:, O_QN + h * LANE:O_QN + (h + 1) * LANE] for h in range(HB)]
    qp = [proj[:, O_QP + h * LANE:O_QP + (h + 1) * LANE] for h in range(HB)]
    return qa, ka, va, qn, qp, proj[:, O_CKV:O_KPE], proj[:, O_KPE:W_INP]


def _permute_w_in(nat):
    K = nat.shape[1]
    qb = nat[W_QA + 2 * W_KA:W_QA + 2 * W_KA + W_QB].reshape(HB, QK_B, K)
    qp = jnp.concatenate([qb[:, QK_NOPE:], jnp.zeros((HB, LANE - QK_ROPE, K), nat.dtype)], axis=1)
    tail = nat[W_QA + 2 * W_KA + W_QB:]
    return jnp.concatenate([nat[:O_QN], qb[:, :QK_NOPE].reshape(HB * QK_NOPE, K), qp.reshape(HB * LANE, K),
                            tail, jnp.zeros((LANE - QK_ROPE, K), nat.dtype)], axis=0)


def _unpermute_pieces():
    pieces = [(0, 0, O_QN)]
    for h in range(HB):
        pieces.append((O_QN + h * QK_NOPE, O_QN + h * QK_B, QK_NOPE))
        pieces.append((O_QP + h * LANE, O_QN + h * QK_B + QK_NOPE, QK_ROPE))
    pieces.append((O_CKV, O_QN + W_QB, KV_RANK))
    pieces.append((O_KPE, O_QN + W_QB + KV_RANK, QK_ROPE))
    return pieces


def kernel(x, c, w_ada, b_ada, g_pre_attn, w_in, g_q_a, g_k_a, g_ckv, w_kv_b, g_out_a, g_out_b, w_out, g_post_attn, g_pre_mlp, w_mlp_in, w_mlp_out, g_post_mlp, loss_target, m_w_ada, m_b_ada, m_g_pre_attn, m_w_in, m_g_q_a, m_g_k_a, m_g_ckv, m_w_kv_b, m_g_out_a, m_g_out_b, m_w_out, m_g_post_attn, m_g_pre_mlp, m_w_mlp_in, m_w_mlp_out, m_g_post_mlp, v_w_ada, v_b_ada, v_g_pre_attn, v_w_in, v_g_q_a, v_g_k_a, v_g_ckv, v_w_kv_b, v_g_out_a, v_g_out_b, v_w_out, v_g_post_attn, v_g_pre_mlp, v_w_mlp_in, v_w_mlp_out, v_g_post_mlp):
    S, D = x.shape[1], x.shape[2]
    x2d, tgt = x[0], loss_target[0]
    ix, iy, ic = lax.axis_index("x"), lax.axis_index("y"), lax.axis_index("c")
    j_me = _chip(ix, iy)
    e_me = 4 * ix + 2 * iy + ic
    T = _pick(S, 256, 8)
    TB = _pick(S, 128, 8)

    where = jnp.stack([ic, j_me]).astype(jnp.int32)
    DFF = w_mlp_out.shape[1] * N_CHIPS

    c_all = _allgather8("gather_c", c.reshape(8, D // 8)).reshape(N_DEV, D)
    w_in_t = jnp.swapaxes(w_in[0], 0, 1)
    h_qkv, tok_a = _gather_start("qkv", where, [w_in_t, w_kv_b[0]], (c_all,))

    def silu_fn(cc):
        return (cc * (1.0 / (1.0 + jnp.exp(-cc))),), ()

    (c_act,), _ = _rowwise("silu_c", silu_fn, [c_all], [], [(D, BF16)], [], N_DEV)
    ca_pad = jnp.concatenate([c_act, jnp.zeros_like(c_act)], axis=0)
    n_mod = w_ada.shape[2]
    b_cols = lax.dynamic_slice(b_ada, (0, j_me * n_mod), (1, n_mod))
    (mod_part,) = _mm_nn("mod_mm", ca_pad, w_ada, [F32], epilogue=lambda acc, b: (acc + b,),
                         extras=(jnp.broadcast_to(b_cols, (16, n_mod)),))
    mod_all = _allgather8("gather_mod", mod_part[:8] + tok_a[0, 0]).reshape(N_DEV, 8, n_mod)
    mod = jnp.concatenate([lax.dynamic_slice(mod_all, (2 * j, e_me, 0), (1, 1, n_mod))[0] for j in range(N_CHIPS)],
                          axis=1)

    h_out, tok_b = _gather_start("out", where, [w_out[0]], (mod_all,), (tok_a,))
    h_up, tok_c = _gather_start("up", where, [w_mlp_in[0]], (tok_b,), (tok_b,))
    h_down, tok_d = _gather_start("down", where, [w_mlp_out[0]], (tok_c,), (tok_c,))
    mod = mod + tok_d[0, 0]
    sh_a, sc_a, gt_a, sh_m, sc_m, gt_m = [mod[:, i * D:(i + 1) * D] for i in range(6)]

    tabs_a = _rope_tables(S, HEAD_DIM, 1)
    tabs_b = _rope_tables(S, QK_ROPE, LANE // QK_ROPE)

    def pre_attn_fn(xt, g, sc, sh):
        return (_rn(xt) * g * (1.0 + sc) + sh,), ()

    (h_b,), _ = _rowwise("pre_attn", pre_attn_fn, [x2d], [g_pre_attn, sc_a, sh_a], [(D, BF16)], [], T)
    g_in, g_kvb = _gather_finish(h_qkv, h_b)
    wpt = _permute_w_in(g_in.reshape(W_IN, D))[None]
    (proj,) = _mm_nt("proj_mm", h_b, wpt, [F32])

    def prep_fn(pt, ca_, sla, sha, cb_, slb, shb, gq, gk, gc):
        return _prep(_split_proj(pt), ((ca_, sla, sha), (cb_, slb, shb)), gq, gk, gc), ()

    (qa_b, ka_b, va_b, qcat_b, ckvn_b, kpe_b), _ = _rowwise(
        "prep", prep_fn, [proj, *tabs_a, *tabs_b], [g_q_a, g_k_a, g_ckv],
        [(W_QA, BF16), (W_KA, BF16), (W_KA, BF16), (2 * HB * LANE, BF16), (KV_RANK, BF16), (LANE, BF16)], [], TB)
    (kv,) = _mm_nn("kv_mm", ckvn_b, g_kvb, [F32])

    def kcat_fn(kvt, kpet):
        kc, vs = [], []
        for h in range(HB):
            kc += [kvt[:, 2 * h * LANE:(2 * h + 1) * LANE], kpet.astype(F32)]
            vs.append(kvt[:, (2 * h + 1) * LANE:(2 * h + 2) * LANE])
        return (jnp.concatenate(kc, axis=1), jnp.concatenate(vs, axis=1)), ()

    (kcat_b, vb_b), _ = _rowwise("kcat", kcat_fn, [kv, kpe_b], [], [(2 * HB * LANE, BF16), (HB * V_DIM, BF16)], [], T)

    TQ = _pick(S, 512, 16)
    sc_a_ = 1.0 / math.sqrt(HEAD_DIM)
    sc_b_ = 1.0 / math.sqrt(QK_B)
    o_a, lse_a = _attn_fwd("attn_a_fwd", qa_b, ka_b, va_b, HA, HA // HKV, HEAD_DIM, HEAD_DIM, sc_a_, TQ)
    m_out, tok_m = _gather_mid(h_out, o_a)
    o_b, lse_b = _attn_fwd("attn_b_fwd", qcat_b, kcat_b, vb_b, HB, 1, 2 * LANE, V_DIM, sc_b_, TQ, after=(tok_m,))
    m_up, tok_m = _gather_mid(h_up, o_b)

    def mix_fn(oa, ob, ga, gb):
        return (jnp.concatenate([_rn(oa) * ga, _rn(ob) * gb], axis=1),), ()

    (on_b,), _ = _rowwise("mix_norm", mix_fn, [o_a, o_b], [g_out_a, g_out_b], [(2 * W_QA, BF16)], [], T,
                          after=(tok_m,))
    (g_wout,) = _gather_end(m_out, on_b)
    w_out_f = g_wout.reshape(1, N_CHIPS * g_wout.shape[1], D)
    (o2,) = _mm_nn("out_mm", on_b, w_out_f, [F32])

    def mid(xt, o2t, gta, gpa, gpm, scm, shm):
        x1 = xt + gta * (_rn(o2t) * gpa)
        return x1, _rn(x1) * gpm * (1.0 + scm) + shm

    def mid_fn(*a):
        return mid(*a), ()

    mid_vecs = [gt_a, g_post_attn, g_pre_mlp, sc_m, sh_m]
    (x1, h2_b), _ = _rowwise("mid", mid_fn, [x2d, o2], mid_vecs, [(D, F32), (D, BF16)], [], T)
    (g_mlp_in,) = _gather_end(m_up, h2_b)
    u, a_b = _mm_nn("mlp_in_mm", h2_b, g_mlp_in, [F32, BF16],
                    epilogue=lambda acc: (acc, jnp.square(jnp.maximum(acc, 0.0))))
    (g_mlp_out,) = _gather_finish(h_down, a_b)
    w_mlp_out_f = g_mlp_out.reshape(1, DFF, D)
    (y,) = _mm_nn("mlp_out_mm", a_b, w_mlp_out_f, [F32])

    def last(yt, x1t, gtm, gpo):
        return x1t + gtm * (_rn(yt) * gpo)

    def last_fn(yt, x1t, tt, gtm, gpo):
        x2, vjp = jax.vjp(last, yt, x1t, gtm, gpo)
        err = x2 - tt
        dy, dx1, dgt, dg = vjp(err * (1.0 / D))
        loss = 0.5 * jnp.sum(jnp.mean(err * err, axis=-1))
        return (dy, dx1), (jnp.full((1, LANE), loss, F32), dgt, dg)

    (dy_b, dx2), (loss_v, d_gt_m, d_g_post_mlp) = _rowwise(
        "loss_bwd", last_fn, [y, x1, tgt], [gt_m, g_post_mlp], [(D, BF16), (D, F32)], [LANE, D, D], TB)

    gw_mlp_out = _mm_tn("gw_mlp_out_mm", a_b, dy_b, 1).reshape(N_CHIPS, DFF // N_CHIPS, D)
    b_down, tok = _reduce_begin("down", [gw_mlp_out])
    (du_b,) = _mm_nt("d_mlp_out_mm", dy_b, w_mlp_out_f, [BF16],
                     epilogue=lambda acc, ut: (acc * (2.0 * jnp.maximum(ut, 0.0)),), extras=(u,), after=(tok,))
    r_down, tok = _reduce_start(b_down, where, du_b)
    gw_mlp_in = _mm_tn("gw_mlp_in_mm", h2_b, du_b, N_CHIPS, after=(tok,))
    b_up, tok = _reduce_begin("up", [gw_mlp_in])
    (dh2,) = _mm_nt("d_mlp_in_mm", du_b, g_mlp_in, [F32], after=(tok,))
    r_up, tok_up = _reduce_start(b_up, where, dh2)

    def mid_bwd_fn(xt, o2t, dh2t, dx2t, *vecs):
        _, vjp = jax.vjp(mid, xt, o2t, *vecs)
        dx, do2, dgta, dgpa, dgpm, dscm, dshm = vjp((dx2t, dh2t))
        return (dx, do2), (dgta, dgpa, dgpm, dscm, dshm)

    (dx1, do2_b), (d_gt_a, d_g_post_attn, d_g_pre_mlp, d_sc_m, d_sh_m) = _rowwise(
        "mid_bwd", mid_bwd_fn, [x2d, o2, dh2, dx2], mid_vecs, [(D, F32), (D, BF16)], [D] * 5, TB, after=(tok_up,))

    (d_on,) = _mm_nt("d_out_mm", do2_b, w_out_f, [F32])
    gw_out = _mm_tn("gw_out_mm", on_b, do2_b, 1).reshape(N_CHIPS, w_out.shape[1], D)
    b_out, tok_out = _reduce_begin("out", [gw_out])

    def mix_bwd_fn(oa, ob, dont, ga, gb):
        _, vjp = jax.vjp(lambda a_, b_, g1, g2: mix_fn(a_, b_, g1, g2)[0][0], oa, ob, ga, gb)
        doa, dob, dga, dgb = vjp(dont)
        return (doa, dob), (dga, dgb)

    (doa_b, dob_b), (d_g_out_a, d_g_out_b) = _rowwise(
        "mix_bwd", mix_bwd_fn, [o_a, o_b, d_on], [g_out_a, g_out_b], [(W_QA, BF16), (HB * V_DIM, BF16)],
        [W_QA, HB * V_DIM], T, after=(tok_out,))

    dqa, dka, dva = _attn_bwd("attn_a_bwd", qa_b, ka_b, va_b, doa_b, o_a, lse_a, HA, HA // HKV, HEAD_DIM, HEAD_DIM,
                              sc_a_, TQ)
    r_out, tok = _reduce_start(b_out, where, dqa)
    dqcat, dkcat, dvb = _attn_bwd("attn_b_bwd", qcat_b, kcat_b, vb_b, dob_b, o_b, lse_b, HB, 1, 2 * LANE, V_DIM,
                                  sc_b_, TQ, after=(tok,))

    def kcat_bwd_fn(dkc, dvt):
        dkv, dkpe = [], None
        for h in range(HB):
            dkv += [dkc[:, 2 * h * LANE:(2 * h + 1) * LANE], dvt[:, h * V_DIM:(h + 1) * V_DIM]]
            piece = dkc[:, (2 * h + 1) * LANE:(2 * h + 2) * LANE]
            dkpe = piece if dkpe is None else dkpe + piece
        return (jnp.concatenate(dkv, axis=1), dkpe), ()

    (dkv_b, dkpe), _ = _rowwise("kcat_bwd", kcat_bwd_fn, [dkcat, dvb], [], [(2 * HB * LANE, BF16), (LANE, F32)], [], T)
    (d_ckvn,) = _mm_nt("d_kv_mm", dkv_b, g_kvb, [F32])
    gw_kvb = _mm_tn("gw_kv_mm", ckvn_b, dkv_b, N_CHIPS)
    b_kvb, tok_kvb = _reduce_begin("kvb", [gw_kvb])

    def prep_bwd_fn(pt, dqat, dkat, dvat, dqct, dckt, dkpt, ca_, sla, sha, cb_, slb, shb, gq, gk, gc):
        tabs = ((ca_, sla, sha), (cb_, slb, shb))
        _, vjp = jax.vjp(lambda sg, a1, a2, a3: _prep(sg, tabs, a1, a2, a3), _split_proj(pt), gq, gk, gc)
        (dqa_s, dka_s, dva_s, dqn_s, dqp_s, dck_s, dkp_s), dgq, dgk, dgc = vjp((dqat, dkat, dvat, dqct, dckt, dkpt))
        dproj = jnp.concatenate([*dqa_s, *dka_s, dva_s, *dqn_s, *dqp_s, dck_s, dkp_s], axis=1)
        return (dproj,), (dgq, dgk, dgc)

    (dproj_b,), (d_g_q_a, d_g_k_a, d_g_ckv) = _rowwise(
        "prep_bwd", prep_bwd_fn, [proj, dqa, dka, dva, dqcat, d_ckvn, dkpe, *tabs_a, *tabs_b],
        [g_q_a, g_k_a, g_ckv], [(W_INP, BF16)], [HEAD_DIM, HEAD_DIM, KV_RANK], TB, after=(tok_kvb,))

    r_kvb, tok = _reduce_start(b_kvb, where, dproj_b)
    (dh,) = _mm_nn("d_proj_mm", dproj_b, wpt, [F32], after=(tok,))
    gw_in = _mm_tn_rows("gw_in_mm", dproj_b, h_b, W_IN, _unpermute_pieces()).reshape(N_CHIPS, W_IN // N_CHIPS, D)
    b_in, tok = _reduce_begin("in", [gw_in])

    def pre_attn_bwd_fn(xt, dht, dx1t, g, sc, sh):
        _, vjp = jax.vjp(lambda a_, b_, c_, d_: pre_attn_fn(a_, b_, c_, d_)[0][0], xt, g, sc, sh)
        dxt, dg, dsc, dsh = vjp(dht)
        return (dx1t + dxt,), (dg, dsc, dsh)

    (grad_x,), (d_g_pre_attn, d_sc_a, d_sh_a) = _rowwise(
        "pre_attn_bwd", pre_attn_bwd_fn, [x2d, dh, dx1], [g_pre_attn, sc_a, sh_a], [(D, F32)], [D] * 3, TB,
        after=(tok,))

    small_names = ["b_ada", "g_pre_attn", "g_q_a", "g_k_a", "g_ckv", "g_out_a", "g_out_b", "g_post_attn",
                   "g_pre_mlp", "g_post_mlp"]
    small_local = jnp.concatenate([d_sh_a, d_sc_a, d_gt_a, d_sh_m, d_sc_m, d_gt_m, d_g_pre_attn, d_g_q_a, d_g_k_a,
                                   d_g_ckv, d_g_out_a, d_g_out_b, d_g_post_attn, d_g_pre_mlp, d_g_post_mlp], axis=1)
    n_small = small_local.shape[1]
    small_all = _allgather8("gather_small", small_local.reshape(8, n_small // 8)).reshape(N_DEV, n_small)

    def sum8_fn(t):
        return (), (jnp.sum(t, axis=0, keepdims=True),)

    r_in, tok_in = _reduce_start(b_in, where, grad_x, (small_all,))
    _, (small_sum,) = _rowwise("sum_small", sum8_fn, [small_all], [], [], [n_small], N_DEV, after=(tok_in,))
    dmod_cols = lax.dynamic_slice(small_all, (0, j_me * n_mod), (N_DEV, n_mod))
    dmod_pad = jnp.concatenate([dmod_cols, jnp.zeros_like(dmod_cols)], axis=0).astype(BF16)
    gw_ada = _mm_tn("gw_ada_mm", ca_pad, dmod_pad, 1)[0]

    weights = dict(w_ada=w_ada, b_ada=b_ada, g_pre_attn=g_pre_attn, w_in=w_in, g_q_a=g_q_a, g_k_a=g_k_a, g_ckv=g_ckv,
                   w_kv_b=w_kv_b, g_out_a=g_out_a, g_out_b=g_out_b, w_out=w_out, g_post_attn=g_post_attn,
                   g_pre_mlp=g_pre_mlp, w_mlp_in=w_mlp_in, w_mlp_out=w_mlp_out, g_post_mlp=g_post_mlp)
    ms = dict(w_ada=m_w_ada, b_ada=m_b_ada, g_pre_attn=m_g_pre_attn, w_in=m_w_in, g_q_a=m_g_q_a, g_k_a=m_g_k_a,
              g_ckv=m_g_ckv, w_kv_b=m_w_kv_b, g_out_a=m_g_out_a, g_out_b=m_g_out_b, w_out=m_w_out,
              g_post_attn=m_g_post_attn, g_pre_mlp=m_g_pre_mlp, w_mlp_in=m_w_mlp_in, w_mlp_out=m_w_mlp_out,
              g_post_mlp=m_g_post_mlp)
    vs = dict(w_ada=v_w_ada, b_ada=v_b_ada, g_pre_attn=v_g_pre_attn, w_in=v_w_in, g_q_a=v_g_q_a, g_k_a=v_g_k_a,
              g_ckv=v_g_ckv, w_kv_b=v_w_kv_b, g_out_a=v_g_out_a, g_out_b=v_g_out_b, w_out=v_w_out,
              g_post_attn=v_g_post_attn, g_pre_mlp=v_g_pre_mlp, w_mlp_in=v_w_mlp_in, w_mlp_out=v_w_mlp_out,
              g_post_mlp=v_g_post_mlp)
    order = list(weights)
    grads, deltas, new_m, new_v = {}, {}, {}, {}

    def update(name, g, after=()):
        view = (lambda t: jnp.swapaxes(t, 0, 1)) if name == "w_in" else (lambda t: t)
        d_, m_, v_ = _adamw("adamw_" + name, view(weights[name][0]), g, view(ms[name][0]), view(vs[name][0]), after)
        grads[name], deltas[name], new_m[name], new_v[name] = (view(t)[None] for t in (g, d_, m_, v_))
        return d_

    chain = (("w_mlp_out", r_down), ("w_mlp_in", r_up), ("w_out", r_out), ("w_kv_b", r_kvb), ("w_in", r_in))
    swap, tok = _reduce_finish(chain[0][1], where, small_sum)
    last = update("w_ada", gw_ada, (tok,))
    for k, (name, _) in enumerate(chain):
        (g,) = _reduce_end(swap, last)
        if k + 1 < len(chain):
            swap, tok = _reduce_finish(chain[k + 1][1], where, g)
        last = update(name, g, (tok,))

    cat = lambda d: jnp.concatenate([d[n] for n in small_names], axis=1)
    d_, m_, v_ = _adamw("adamw_small", cat(weights), small_sum, cat(ms), cat(vs))
    off = 0
    for n in small_names:
        w_ = weights[n].shape[1]
        grads[n], deltas[n], new_m[n], new_v[n] = (a[:, off:off + w_] for a in (small_sum, d_, m_, v_))
        off += w_

    loss = lax.psum(loss_v[0, 0], ("x", "y", "c"))
    return (loss, grad_x[None], *[grads[n] for n in order], *[deltas[n] for n in order],
            *[new_m[n] for n in order], *[new_v[n] for n in order])
```

```python
import functools
import math

import numpy as np
import jax
import jax.numpy as jnp
from jax import lax
from jax.experimental import pallas as pl
from jax.experimental.pallas import tpu as pltpu

F32 = jnp.float32
BF16 = jnp.bfloat16
MESH = pl.DeviceIdType.MESH

EPS = 1e-6
GRID_W = 64
ROPE_THETA = 10000.0
HEAD_DIM = 128
HA = 8
HKV = 2
HB = 8
QK_NOPE = 128
QK_ROPE = 64
V_DIM = 128
KV_RANK = 512
QK_B = QK_NOPE + QK_ROPE
W_QA = HA * HEAD_DIM
W_KA = HKV * HEAD_DIM
W_QB = HB * QK_B
W_IN = W_QA + 2 * W_KA + W_QB + KV_RANK + QK_ROPE
N_CHIPS = 4
N_DEV = 8

LANE = 128
O_QA = 0
O_KA = O_QA + W_QA
O_VA = O_KA + W_KA
O_QN = O_VA + W_KA
O_QP = O_QN + HB * QK_NOPE
O_CKV = O_QP + HB * LANE
O_KPE = O_CKV + KV_RANK
W_INP = O_KPE + LANE

ADAM_LR = 0.001
ADAM_B1 = 0.9
ADAM_B2 = 0.999
ADAM_EPS = 1e-08
ADAM_WD = 0.01
ADAM_STEP = 10

VMEM_LIMIT = 56 * 1024 * 1024
HEADS_PER_STEP = 2


def _params(sem):
    return pltpu.CompilerParams(dimension_semantics=sem, vmem_limit_bytes=VMEM_LIMIT)


def _pick(dim, target, unit):
    best = None
    t = unit
    while t <= min(dim, target):
        if dim % t == 0:
            best = t
        t += unit
    return dim if best is None else best


def _rn(x):
    return x * lax.rsqrt(jnp.mean(x * x, axis=-1, keepdims=True) + EPS)


def _rn_parts(x):
    r = lax.rsqrt(jnp.mean(x * x, axis=-1, keepdims=True) + EPS)
    return x * r, r


def _rn_bwd(n, r, dn):
    return r * (dn - n * jnp.mean(dn * n, axis=-1, keepdims=True))


def _rows(t):
    return jnp.sum(t, axis=0, keepdims=True)


def _rowwise(name, fn, rows, vecs, row_outs, vec_outs, tile, after=()):
    S = rows[0].shape[0]
    assert S % tile == 0
    n_r, n_v, n_ro, n_vo, n_a = len(rows), len(vecs), len(row_outs), len(vec_outs), len(after)

    def body(*refs):
        r_in = refs[:n_r]
        v_in = refs[n_r:n_r + n_v]
        r_out = refs[n_r + n_v + n_a:n_r + n_v + n_a + n_ro]
        v_out = refs[n_r + n_v + n_a + n_ro:]
        ro, vo = fn(*[r[...] for r in r_in], *[v[...] for v in v_in])
        for ref, val in zip(r_out, ro):
            ref[...] = val.astype(ref.dtype)
        if n_vo:
            @pl.when(pl.program_id(0) == 0)
            def _():
                for ref in v_out:
                    ref[...] = jnp.zeros(ref.shape, ref.dtype)
            for ref, val in zip(v_out, vo):
                ref[...] += val

    in_specs = [pl.BlockSpec((tile, a.shape[1]), lambda i: (i, 0)) for a in rows]
    in_specs += [pl.BlockSpec(a.shape, lambda i: (0, 0)) for a in vecs]
    in_specs += [pl.BlockSpec(memory_space=pl.ANY) for _ in after]
    out_specs = [pl.BlockSpec((tile, w), lambda i: (i, 0)) for w, _ in row_outs]
    out_specs += [pl.BlockSpec((1, w), lambda i: (0, 0)) for w in vec_outs]
    out_shape = [jax.ShapeDtypeStruct((S, w), dt) for w, dt in row_outs]
    out_shape += [jax.ShapeDtypeStruct((1, w), F32) for w in vec_outs]
    res = pl.pallas_call(
        body, name=name, grid=(S // tile,), in_specs=in_specs, out_specs=out_specs, out_shape=out_shape,
        compiler_params=_params(("arbitrary",)),
    )(*rows, *vecs, *after)
    return res[:n_ro], res[n_ro:]


CONTRACT_TILE = 2048
MM_TILE_BYTES = 40 * 1024 * 1024


def _rows_that_fit(M, tile_bytes):
    t = _pick(M, 1024, 16)
    while t % 32 == 0 and tile_bytes(t) > MM_TILE_BYTES:
        t //= 2
    return t


def _accumulate(acc, part, step, n_steps, finish):
    if n_steps == 1:
        finish(part)
        return

    @pl.when(step == 0)
    def _():
        acc[...] = part

    @pl.when(step > 0)
    def _():
        acc[...] += part

    @pl.when(step == n_steps - 1)
    def _():
        finish(acc[...])


def _mm_nn(name, a, b3, out_dtypes, epilogue=None, extras=(), after=()):
    M, K = a.shape
    nB, K2, Nsh = b3.shape
    assert K == K2
    N = nB * Nsh
    tn, tk = _pick(Nsh, 1536, LANE), _pick(K, CONTRACT_TILE, LANE)
    out_bytes = sum(jnp.dtype(dt).itemsize for dt in out_dtypes) + sum(e.dtype.itemsize for e in extras)
    tm = _rows_that_fit(M, lambda t: 2 * (t * tk * a.dtype.itemsize + tk * tn * b3.dtype.itemsize)
                        + t * tn * (4 * (K > tk) + 2 * out_bytes))
    nps, nk, ne, no, na = Nsh // tn, K // tk, len(extras), len(out_dtypes), len(after)

    def body(a_ref, b_ref, *rest):
        e_refs, o_refs = rest[:ne], rest[ne + na:ne + na + no]

        def finish(total):
            vals = (total,) if epilogue is None else epilogue(total, *[e[...] for e in e_refs])
            for ref, val in zip(o_refs, vals):
                ref[...] = val.astype(ref.dtype)

        part = jnp.dot(a_ref[...].astype(BF16), b_ref[...].astype(BF16), preferred_element_type=F32)
        _accumulate(rest[-1], part, pl.program_id(2), nk, finish)

    in_specs = [pl.BlockSpec((tm, tk), lambda i, j, k: (i, k)),
                pl.BlockSpec((None, tk, tn), lambda i, j, k: (j // nps, k, j % nps))]
    in_specs += [pl.BlockSpec((tm, tn), lambda i, j, k: (i, j)) for _ in extras]
    in_specs += [pl.BlockSpec(memory_space=pl.ANY) for _ in after]
    return pl.pallas_call(
        body, name=name, grid=(M // tm, N // tn, nk), in_specs=in_specs,
        out_specs=[pl.BlockSpec((tm, tn), lambda i, j, k: (i, j)) for _ in out_dtypes],
        out_shape=[jax.ShapeDtypeStruct((M, N), dt) for dt in out_dtypes],
        scratch_shapes=[pltpu.VMEM((tm, tn), F32)] if nk > 1 else [],
        compiler_params=_params(("parallel", "parallel", "arbitrary")),
    )(a, b3, *extras, *after)


def _mm_nt(name, a, b3, out_dtypes, epilogue=None, extras=(), after=()):
    M, N = a.shape
    nB, K, Nsh = b3.shape
    assert N == nB * Nsh
    to, tn = _pick(K, 1536, LANE), _pick(Nsh, CONTRACT_TILE, LANE)
    out_bytes = sum(jnp.dtype(dt).itemsize for dt in out_dtypes) + sum(e.dtype.itemsize for e in extras)
    tm = _rows_that_fit(M, lambda t: 2 * (t * tn * a.dtype.itemsize + to * tn * b3.dtype.itemsize)
                        + t * to * (4 * (N > tn) + 2 * out_bytes))
    nps, nn, ne, no, na = Nsh // tn, N // tn, len(extras), len(out_dtypes), len(after)

    def body(a_ref, b_ref, *rest):
        e_refs, o_refs = rest[:ne], rest[ne + na:ne + na + no]

        def finish(total):
            vals = (total,) if epilogue is None else epilogue(total, *[e[...] for e in e_refs])
            for ref, val in zip(o_refs, vals):
                ref[...] = val.astype(ref.dtype)

        part = lax.dot_general(a_ref[...].astype(BF16), b_ref[...].astype(BF16),
                               (((1,), (1,)), ((), ())), preferred_element_type=F32)
        _accumulate(rest[-1], part, pl.program_id(2), nn, finish)

    in_specs = [pl.BlockSpec((tm, tn), lambda i, o, n: (i, n)),
                pl.BlockSpec((None, to, tn), lambda i, o, n: (n // nps, o, n % nps))]
    in_specs += [pl.BlockSpec((tm, to), lambda i, o, n: (i, o)) for _ in extras]
    in_specs += [pl.BlockSpec(memory_space=pl.ANY) for _ in after]
    return pl.pallas_call(
        body, name=name, grid=(M // tm, K // to, nn), in_specs=in_specs,
        out_specs=[pl.BlockSpec((tm, to), lambda i, o, n: (i, o)) for _ in out_dtypes],
        out_shape=[jax.ShapeDtypeStruct((M, K), dt) for dt in out_dtypes],
        scratch_shapes=[pltpu.VMEM((tm, to), F32)] if nn > 1 else [],
        compiler_params=_params(("parallel", "parallel", "arbitrary")),
    )(a, b3, *extras, *after)


def _mm_tn(name, a, g, nB, after=()):
    M, K = a.shape
    M2, N = g.shape
    assert M == M2 and N % nB == 0
    Nsh = N // nB
    tk, tn, tm = _pick(K, 1536, LANE), _pick(Nsh, 1024, LANE), _pick(M, CONTRACT_TILE, 16)
    nps, nm = Nsh // tn, M // tm

    def body(a_ref, g_ref, *rest):
        o_ref = rest[len(after)]

        def finish(total):
            o_ref[...] = total

        part = lax.dot_general(a_ref[...].astype(BF16), g_ref[...].astype(BF16),
                               (((0,), (0,)), ((), ())), preferred_element_type=F32)
        _accumulate(rest[-1], part, pl.program_id(2), nm, finish)

    return pl.pallas_call(
        body, name=name, grid=(K // tk, N // tn, nm),
        in_specs=[pl.BlockSpec((tm, tk), lambda k, n, m: (m, k)),
                  pl.BlockSpec((tm, tn), lambda k, n, m: (m, n))] + [pl.BlockSpec(memory_space=pl.ANY) for _ in after],
        out_specs=pl.BlockSpec((None, tk, tn), lambda k, n, m: (n // nps, k, n % nps)),
        out_shape=jax.ShapeDtypeStruct((nB, K, Nsh), F32),
        scratch_shapes=[pltpu.VMEM((tk, tn), F32)] if nm > 1 else [],
        compiler_params=_params(("parallel", "parallel", "arbitrary")),
    )(a, g, *after)


def _mm_tn_rows(name, a, g, n_rows, pieces):
    M, K = a.shape
    M2, N = g.shape
    assert M == M2
    tk, tn, tm = _pick(K, 1536, LANE), _pick(N, 512, LANE), _pick(M, CONTRACT_TILE, 16)
    nk, nm = K // tk, M // tm

    def body(a_ref, g_ref, o_ref, *acc):
        kk = pl.program_id(1)

        def finish(total):
            for t in range(nk):
                @pl.when(kk == t)
                def _(t=t):
                    for src, dst, n in pieces:
                        lo, hi = max(src, t * tk), min(src + n, (t + 1) * tk)
                        if lo < hi:
                            o_ref[dst + lo - src:dst + hi - src, :] = total[lo - t * tk:hi - t * tk, :]

        part = lax.dot_general(a_ref[...].astype(BF16), g_ref[...].astype(BF16),
                               (((0,), (0,)), ((), ())), preferred_element_type=F32)
        _accumulate(acc[0] if acc else None, part, pl.program_id(2), nm, finish)

    return pl.pallas_call(
        body, name=name, grid=(N // tn, nk, nm),
        in_specs=[pl.BlockSpec((tm, tk), lambda n, k, m: (m, k)),
                  pl.BlockSpec((tm, tn), lambda n, k, m: (m, n))],
        out_specs=pl.BlockSpec((n_rows, tn), lambda n, k, m: (0, n)),
        out_shape=jax.ShapeDtypeStruct((n_rows, N), F32),
        scratch_shapes=[pltpu.VMEM((tk, tn), F32)] if nm > 1 else [],
        compiler_params=_params(("arbitrary", "arbitrary", "arbitrary")),
    )(a, g)


def _attn_fwd(name, q, k, v, n_heads, group, dqk, dv, scale, tq, after=()):
    S = q.shape[0]
    hp = 2 * HEADS_PER_STEP
    kvp = max(1, hp // group)

    def body(q_ref, k_ref, v_ref, *rest):
        o_ref, lse_ref = rest[-2:]
        for j in range(hp):
            jk = j * kvp // hp
            s = lax.dot_general(q_ref[:, j * dqk:(j + 1) * dqk], k_ref[:, jk * dqk:(jk + 1) * dqk],
                                (((1,), (1,)), ((), ())), preferred_element_type=F32)
            m = jnp.max(s, axis=-1, keepdims=True)
            e = jnp.exp((s - m) * scale)
            l = jnp.sum(e, axis=-1, keepdims=True)
            o_ref[:, j * dv:(j + 1) * dv] = jnp.dot(e.astype(BF16), v_ref[:, jk * dv:(jk + 1) * dv],
                                                    preferred_element_type=F32) * (1.0 / l)
            lse_ref[:, j * LANE:(j + 1) * LANE] = jnp.broadcast_to(m * scale + jnp.log(l), (tq, LANE))

    kv_block = lambda h, i: (0, (h * hp // group) // kvp)
    return pl.pallas_call(
        body, name=name, grid=(n_heads // hp, S // tq),
        in_specs=[pl.BlockSpec((tq, hp * dqk), lambda h, i: (i, h)),
                  pl.BlockSpec((S, kvp * dqk), kv_block),
                  pl.BlockSpec((S, kvp * dv), kv_block)] + [pl.BlockSpec(memory_space=pl.ANY) for _ in after],
        out_specs=[pl.BlockSpec((tq, hp * dv), lambda h, i: (i, h)),
                   pl.BlockSpec((tq, hp * LANE), lambda h, i: (i, h))],
        out_shape=[jax.ShapeDtypeStruct((S, n_heads * dv), F32), jax.ShapeDtypeStruct((S, n_heads * LANE), F32)],
        compiler_params=_params(("arbitrary", "arbitrary")),
    )(q, k, v, *after)


def _attn_bwd(name, q, k, v, do, o, lse, n_heads, group, dqk, dv, scale, tq, after=()):
    S = q.shape[0]
    n_kv = n_heads // group
    hp = HEADS_PER_STEP
    kvp = max(1, hp // group)

    def body(q_ref, k_ref, v_ref, do_ref, o_ref, lse_ref, *rest):
        dq_ref, dk_ref, dv_ref = rest[len(after):]
        h, i = pl.program_id(0), pl.program_id(1)

        @pl.when(((h * hp) % group == 0) & (i == 0))
        def _():
            dk_ref[...] = jnp.zeros(dk_ref.shape, F32)
            dv_ref[...] = jnp.zeros(dv_ref.shape, F32)

        for j in range(hp):
            jk = j * kvp // hp
            qb, dob = q_ref[:, j * dqk:(j + 1) * dqk], do_ref[:, j * dv:(j + 1) * dv]
            kb, vb = k_ref[:, jk * dqk:(jk + 1) * dqk], v_ref[:, jk * dv:(jk + 1) * dv]
            s = lax.dot_general(qb, kb, (((1,), (1,)), ((), ())), preferred_element_type=F32)
            p = jnp.exp(s * scale - jnp.tile(lse_ref[:, j * LANE:(j + 1) * LANE], (1, S // LANE)))
            dp = lax.dot_general(dob, vb, (((1,), (1,)), ((), ())), preferred_element_type=F32)
            delta = jnp.sum(dob.astype(F32) * o_ref[:, j * dv:(j + 1) * dv], axis=-1, keepdims=True)
            ds = (p * ((dp - delta) * scale)).astype(BF16)
            dv_ref[:, jk * dv:(jk + 1) * dv] += lax.dot_general(p.astype(BF16), dob, (((0,), (0,)), ((), ())),
                                                                preferred_element_type=F32)
            dq_ref[:, j * dqk:(j + 1) * dqk] = jnp.dot(ds, kb, preferred_element_type=F32)
            dk_ref[:, jk * dqk:(jk + 1) * dqk] += lax.dot_general(ds, qb, (((0,), (0,)), ((), ())),
                                                                  preferred_element_type=F32)

    kv_block = lambda h, i: (0, (h * hp // group) // kvp)
    return pl.pallas_call(
        body, name=name, grid=(n_heads // hp, S // tq),
        in_specs=[pl.BlockSpec((tq, hp * dqk), lambda h, i: (i, h)),
                  pl.BlockSpec((S, kvp * dqk), kv_block),
                  pl.BlockSpec((S, kvp * dv), kv_block),
                  pl.BlockSpec((tq, hp * dv), lambda h, i: (i, h)),
                  pl.BlockSpec((tq, hp * dv), lambda h, i: (i, h)),
                  pl.BlockSpec((tq, hp * LANE), lambda h, i: (i, h))] + [pl.BlockSpec(memory_space=pl.ANY) for _ in after],
        out_specs=[pl.BlockSpec((tq, hp * dqk), lambda h, i: (i, h)),
                   pl.BlockSpec((S, kvp * dqk), kv_block),
                   pl.BlockSpec((S, kvp * dv), kv_block)],
        out_shape=[jax.ShapeDtypeStruct((S, n_heads * dqk), F32),
                   jax.ShapeDtypeStruct((S, n_kv * dqk), F32),
                   jax.ShapeDtypeStruct((S, n_kv * dv), F32)],
        compiler_params=_params(("arbitrary", "arbitrary")),
    )(q, k, v, do, o, lse, *after)


def _adamw(name, w, g, m, v, after=()):
    R, C = w.shape
    tile = _pick(R, max(8, (1 << 19) // C // 8 * 8), 8)
    c1 = 1.0 - ADAM_B1 ** ADAM_STEP
    c2 = 1.0 - ADAM_B2 ** ADAM_STEP

    def body(w_ref, g_ref, m_ref, v_ref, *rest):
        d_ref, nm_ref, nv_ref = rest[len(after):]
        gg = g_ref[...]
        nm = ADAM_B1 * m_ref[...] + (1.0 - ADAM_B1) * gg
        nv = ADAM_B2 * v_ref[...] + (1.0 - ADAM_B2) * (gg * gg)
        d_ref[...] = -ADAM_LR * ((nm / c1) / (jnp.sqrt(nv / c2) + ADAM_EPS) + ADAM_WD * w_ref[...])
        nm_ref[...] = nm
        nv_ref[...] = nv

    spec = pl.BlockSpec((tile, C), lambda i: (i, 0))
    return pl.pallas_call(
        body, name=name, grid=(R // tile,), in_specs=[spec] * 4 + [pl.BlockSpec(memory_space=pl.ANY) for _ in after],
        out_specs=[spec] * 3, out_shape=[jax.ShapeDtypeStruct((R, C), F32)] * 3,
        compiler_params=_params(("parallel",)),
    )(w, g, m, v, *after)


def _flip(me, rel):
    return tuple(1 - p if r else p for p, r in zip(me, rel))


def _push(name, ins, out_shapes, transfers, aliases=None):
    ni, no, nt = len(ins), len(out_shapes), len(transfers)

    def body(*refs):
        in_refs, out_refs = refs[:ni], refs[ni:ni + no]
        send_sems, recv_sems = refs[ni + no], refs[ni + no + 1]
        me = (lax.axis_index("x"), lax.axis_index("y"), lax.axis_index("c"))
        copies = []
        for t, (si, sfn, oi, dfn, rel) in enumerate(transfers):
            src_ref = out_refs[si[1]] if isinstance(si, tuple) else in_refs[si]
            src = src_ref.at[sfn(*me)]
            dst = out_refs[oi].at[dfn(*me)]
            if rel is None:
                cp = pltpu.make_async_copy(src, dst, send_sems.at[t])
            else:
                cp = pltpu.make_async_remote_copy(src_ref=src, dst_ref=dst, send_sem=send_sems.at[t],
                                                  recv_sem=recv_sems.at[t], device_id=_flip(me, rel),
                                                  device_id_type=MESH)
            cp.start()
            copies.append(cp)
        for cp in copies:
            cp.wait()

    any_spec = pl.BlockSpec(memory_space=pl.ANY)
    return pl.pallas_call(
        body, name=name, in_specs=[any_spec] * ni, out_specs=[any_spec] * no, out_shape=out_shapes,
        scratch_shapes=[pltpu.SemaphoreType.DMA((nt,)), pltpu.SemaphoreType.DMA((nt,))],
        input_output_aliases=aliases or {},
        compiler_params=pltpu.CompilerParams(has_side_effects=True),
    )(*ins)


_HBM = pl.BlockSpec(memory_space=pltpu.HBM)
_SEM = pl.BlockSpec(memory_space=pltpu.SEMAPHORE)
_DATAFLOW = pltpu.SideEffectType.DATAFLOW_SIDE_EFFECTING


def _split_copies(refs, transfers, send_sems, recv_sems):
    me = (lax.axis_index("x"), lax.axis_index("y"), lax.axis_index("c"))
    return [pltpu.make_async_remote_copy(src_ref=refs[sb].at[sfn(*me)], dst_ref=refs[db].at[dfn(*me)],
                                         send_sem=send_sems.at[t], recv_sem=recv_sems.at[t],
                                         device_id=_flip(me, rel), device_id_type=MESH)
            for t, (sb, sfn, db, dfn, rel) in enumerate(transfers)]


def _push_start(name, bufs, transfers, after=()):
    nb, na, nt = len(bufs), len(after), len(transfers)

    def body(*refs):
        send_sems, recv_sems, token = refs[nb + na], refs[nb + na + 1], refs[-1]
        for cp in _split_copies(refs[nb + na + 2:2 * nb + na + 2], transfers, send_sems, recv_sems):
            cp.start()
        token[...] = jnp.zeros(token.shape, token.dtype)

    res = pl.pallas_call(
        body, name=name,
        out_shape=(pltpu.SemaphoreType.DMA((nt,)), pltpu.SemaphoreType.DMA((nt,)),
                   *[pltpu.HBM(b.shape, b.dtype) for b in bufs], jax.ShapeDtypeStruct((8, LANE), F32)),
        in_specs=[_HBM] * nb + [pl.BlockSpec(memory_space=pl.ANY)] * na,
        out_specs=(_SEM, _SEM, *[_HBM] * nb, pl.BlockSpec(memory_space=pltpu.VMEM)),
        input_output_aliases={i: i + 2 for i in range(nb)},
        compiler_params=pltpu.CompilerParams(has_side_effects=_DATAFLOW),
    )(*[pltpu.with_memory_space_constraint(b, pltpu.HBM) for b in bufs], *after)
    return res[0], res[1], list(res[2:2 + nb]), res[-1]


def _push_wait(name, send_sems, recv_sems, bufs, transfers, after):
    nb = len(bufs)

    def body(*refs):
        for cp in _split_copies(refs[:nb], transfers, refs[nb], refs[nb + 1]):
            cp.wait_send()
            cp.wait_recv()

    res = pl.pallas_call(
        body, name=name, out_shape=[pltpu.HBM(b.shape, b.dtype) for b in bufs],
        in_specs=[_HBM] * nb + [_SEM, _SEM, pl.BlockSpec(memory_space=pl.ANY)], out_specs=[_HBM] * nb,
        input_output_aliases={i: i for i in range(nb)},
        compiler_params=pltpu.CompilerParams(has_side_effects=_DATAFLOW),
    )(*bufs, send_sems, recv_sems, after)
    return list(res)


ICI_RELS = ((1, 0, 0), (0, 1, 0), (1, 1, 0))
SIBLING = (0, 0, 1)


def _chip(x, y):
    return 2 * x + y


def _cast_into_slot(name, where, w, after=()):
    R, C = w.shape
    tile = _pick(R, 512, BF16_ROWS)

    def body(where_ref, w_ref, *rest):
        rest[-1][...] = w_ref[...].astype(BF16)

    return pl.pallas_call(
        body, name=name,
        grid_spec=pltpu.PrefetchScalarGridSpec(
            num_scalar_prefetch=1, grid=(R // tile,),
            in_specs=[pl.BlockSpec((tile, C), lambda i, wh: (i, 0))] + [pl.BlockSpec(memory_space=pl.ANY) for _ in after],
            out_specs=pl.BlockSpec((None, tile, C), lambda i, wh: (wh[1], i, 0))),
        out_shape=jax.ShapeDtypeStruct((N_CHIPS, R, C), BF16),
        compiler_params=_params(("parallel",)),
    )(where, w, *after)


BF16_ROWS = 16


def _rows_split(shape2):
    return (shape2[0] // 2) % BF16_ROWS == 0


def _half_shape(shape2):
    R, C = shape2
    return (R // 2, C) if _rows_split(shape2) else (R, C // 2)


def _half(shape2, c):
    R, C = shape2
    if _rows_split(shape2):
        return (pl.ds(c * (R // 2), R // 2), slice(None))
    return (slice(None), pl.ds(pl.multiple_of(c * (C // 2), LANE), C // 2))


def _gather_start(tag, where, shards, after, cast_after=()):
    n = len(shards)
    own = [_cast_into_slot(f"cast_w_{tag}{w}", where, shards[w], cast_after) for w in range(n)]
    transfers = []
    for w in range(n):
        def idx(x, y, c, shape2=shards[w].shape):
            return (_chip(x, y), *_half(shape2, c))
        transfers += [(w, idx, w, idx, rel) for rel in ICI_RELS]
    send_sems, recv_sems, bufs, token = _push_start(f"gather_{tag}_start", own, transfers, after)
    return (tag, send_sems, recv_sems, bufs, transfers), token


def _forward_transfers(part):
    second = []
    for w, p in enumerate(part):
        for rel in ICI_RELS:
            def idx(x, y, c, shape2=p.shape[1:], rel=rel):
                return (_chip(x ^ rel[0], y ^ rel[1]), *_half(shape2, c))
            second.append((w, idx, w, idx, SIBLING))
    return second


def _gather_finish(handle, after):
    tag, send_sems, recv_sems, bufs, transfers = handle
    part = _push_wait(f"gather_{tag}_wait", send_sems, recv_sems, bufs, transfers, after)
    outs = [jax.ShapeDtypeStruct(p.shape, p.dtype) for p in part]
    second = [(("out", sb), sfn, db, dfn, rel) for sb, sfn, db, dfn, rel in _forward_transfers(part)]
    return _push(f"gather_{tag}_d2d", list(part), outs, second, aliases={w: w for w in range(len(part))})


def _gather_mid(handle, after):
    tag, send_sems, recv_sems, bufs, transfers = handle
    part = _push_wait(f"gather_{tag}_wait", send_sems, recv_sems, bufs, transfers, after)
    second = _forward_transfers(part)
    s2, r2, bufs2, token = _push_start(f"gather_{tag}_d2d_start", part, second)
    return (tag, s2, r2, bufs2, second), token


def _gather_end(handle, after):
    tag, send_sems, recv_sems, bufs, transfers = handle
    return _push_wait(f"gather_{tag}_d2d_wait", send_sems, recv_sems, bufs, transfers, after)


def _reduce_begin(tag, grads, after=()):
    n = len(grads)
    land = [lax.empty((N_CHIPS, *_half_shape(g.shape[1:])), F32) for g in grads]
    t1 = []
    for w in range(n):
        t1.append((w, lambda x, y, c, shape2=grads[w].shape[1:]: (slice(None), *_half(shape2, 1 - c)), n + w,
                   lambda x, y, c: (slice(None), slice(None), slice(None)), SIBLING))
    send_sems, recv_sems, bufs, token = _push_start(f"reduce_{tag}_d2d_start", list(grads) + land, t1, after)
    return (tag, send_sems, recv_sems, bufs, t1), token


def _reduce_start(handle, where, after, after_start=()):
    tag, send_sems, recv_sems, bufs, t1 = handle
    n = len(bufs) // 2
    res = _push_wait(f"reduce_{tag}_d2d_wait", send_sems, recv_sems, bufs, t1, after)
    grads, from_sib = res[:n], res[n:]
    pair = [_pair_sum(f"pair_sum_{tag}{w}", where, grads[w], from_sib[w]) for w in range(n)]
    land = [lax.empty((3, *_half_shape(g.shape[1:])), BF16) for g in grads]
    t3 = []
    for w in range(n):
        for r, rel in enumerate(ICI_RELS):
            t3.append((w, lambda x, y, c, rel=rel: (_chip(x ^ rel[0], y ^ rel[1]), slice(None), slice(None)), n + w,
                       lambda x, y, c, r=r: (r, slice(None), slice(None)), rel))
    send_sems, recv_sems, bufs, token = _push_start(f"reduce_{tag}_start", pair + land, t3, after_start)
    return (tag, send_sems, recv_sems, bufs, t3, list(grads), list(from_sib)), token


def _reduce_finish(handle, where, after):
    tag, send_sems, recv_sems, bufs, t3, grads, from_sib = handle
    n = len(grads)
    from_chips = _push_wait(f"reduce_{tag}_wait", send_sems, recv_sems, bufs, t3, after)[n:]
    mine = [_chip_sum(f"chip_sum_{tag}{w}", where, grads[w], from_sib[w], from_chips[w]) for w in range(n)]
    t5 = []
    for w in range(n):
        def idx(x, y, c, shape2=grads[w].shape[1:]):
            return _half(shape2, c)
        t5.append((w, idx, w, idx, SIBLING))
    send_sems, recv_sems, bufs, token = _push_start(f"reduce_{tag}_swap_start", mine, t5)
    return (tag, send_sems, recv_sems, bufs, t5), token


def _reduce_end(handle, after):
    tag, send_sems, recv_sems, bufs, t5 = handle
    return _push_wait(f"reduce_{tag}_swap_wait", send_sems, recv_sems, bufs, t5, after)


def _pair_sum(name, where, g, from_sib):
    _, hr, hc = from_sib.shape
    tile = _pick(hr, max(16, (1 << 19) // hc // 16 * 16), 16)
    nb = hr // tile
    slot = lambda k, wh: (wh[1] + 1 + k) % N_CHIPS
    if _rows_split(g.shape[1:]):
        mine = lambda k, i, wh: (slot(k, wh), wh[0] * nb + i, 0)
    else:
        mine = lambda k, i, wh: (slot(k, wh), i, wh[0])

    def body(where_ref, g_ref, s_ref, o_ref):
        o_ref[...] = (g_ref[...] + s_ref[...]).astype(BF16)

    return pl.pallas_call(
        body, name=name,
        grid_spec=pltpu.PrefetchScalarGridSpec(
            num_scalar_prefetch=1, grid=(N_CHIPS - 1, nb),
            in_specs=[pl.BlockSpec((None, tile, hc), mine),
                      pl.BlockSpec((None, tile, hc), lambda k, i, wh: (slot(k, wh), i, 0))],
            out_specs=pl.BlockSpec((None, tile, hc), lambda k, i, wh: (slot(k, wh), i, 0))),
        out_shape=jax.ShapeDtypeStruct((N_CHIPS, hr, hc), BF16),
        compiler_params=_params(("parallel", "parallel")),
    )(where, g, from_sib)


def _chip_sum(name, where, g, from_sib, from_chips):
    _, R, C = g.shape
    _, hr, hc = from_sib.shape
    tile = _pick(hr, max(16, (1 << 19) // hc // 16 * 16), 16)
    nb = hr // tile
    if _rows_split((R, C)):
        mine = lambda i, wh: (wh[0] * nb + i, 0)
    else:
        mine = lambda i, wh: (i, wh[0])

    def body(where_ref, g_ref, s_ref, r_ref, o_ref):
        acc = g_ref[...] + s_ref[...]
        for r in range(3):
            acc = acc + r_ref[r].astype(F32)
        o_ref[...] = acc

    return pl.pallas_call(
        body, name=name,
        grid_spec=pltpu.PrefetchScalarGridSpec(
            num_scalar_prefetch=1, grid=(nb,),
            in_specs=[pl.BlockSpec((None, tile, hc), lambda i, wh: (wh[1], *mine(i, wh))),
                      pl.BlockSpec((None, tile, hc), lambda i, wh: (wh[1], i, 0)),
                      pl.BlockSpec((3, tile, hc), lambda i, wh: (0, i, 0))],
            out_specs=pl.BlockSpec((tile, hc), mine)),
        out_shape=jax.ShapeDtypeStruct((R, C), F32),
        compiler_params=_params(("parallel",)),
    )(where, g, from_sib, from_chips)


def _allgather8(name, blk):
    m_per, n = blk.shape
    rels = [(0, 0, 1), (1, 0, 0), (0, 1, 0), (1, 1, 0), (1, 0, 1), (0, 1, 1), (1, 1, 1)]

    def body(x_ref, out_ref, send_sems, recv_sems):
        me = (lax.axis_index("x"), lax.axis_index("y"), lax.axis_index("c"))
        my_rows = out_ref.at[pl.ds((4 * me[0] + 2 * me[1] + me[2]) * m_per, m_per), :]
        out_ref[pl.ds((4 * me[0] + 2 * me[1] + me[2]) * m_per, m_per), :] = x_ref[...]
        copies = []
        for t, rel in enumerate(rels):
            cp = pltpu.make_async_remote_copy(src_ref=x_ref, dst_ref=my_rows, send_sem=send_sems.at[t],
                                              recv_sem=recv_sems.at[t], device_id=_flip(me, rel),
                                              device_id_type=MESH)
            cp.start()
            copies.append(cp)
        for cp in copies:
            cp.wait()

    return pl.pallas_call(
        body, name=name, out_shape=jax.ShapeDtypeStruct((N_DEV * m_per, n), blk.dtype),
        in_specs=[pl.BlockSpec(memory_space=pltpu.VMEM)], out_specs=pl.BlockSpec(memory_space=pltpu.VMEM),
        scratch_shapes=[pltpu.SemaphoreType.DMA((7,)), pltpu.SemaphoreType.DMA((7,))],
        compiler_params=pltpu.CompilerParams(has_side_effects=True, vmem_limit_bytes=VMEM_LIMIT),
    )(blk)


def _rope_tables(seq_len, dim, reps):
    rows = seq_len // GRID_W
    t = np.arange(seq_len)
    row, col = (t // GRID_W).astype(np.float32), (t % GRID_W).astype(np.float32)
    half = dim // 2
    inv = jnp.asarray(ROPE_THETA, F32) ** (-jnp.arange(0, half, 2, dtype=F32) / half)
    ang_r = jnp.asarray(row)[:, None] * inv[None, :]
    ang_c = jnp.asarray(col)[:, None] * inv[None, :]
    ang = jnp.concatenate([ang_r, ang_r, ang_c, ang_c], axis=-1)
    cos, sin = jnp.cos(ang), jnp.sin(ang)
    low = (np.arange(dim) % (dim // 2)) < (dim // 4)
    s_lo = jnp.where(jnp.asarray(low)[None, :], -sin, 0.0)
    s_hi = jnp.where(jnp.asarray(low)[None, :], 0.0, sin)
    del rows
    return tuple(jnp.tile(a, (1, reps)) for a in (cos, s_lo, s_hi))


def _make_rope(shift):
    def up(v):
        return pltpu.roll(v, LANE - shift, 1)

    def down(v):
        return pltpu.roll(v, shift, 1)

    @jax.custom_vjp
    def rope(v, cos, s_lo, s_hi):
        return v * cos + up(v) * s_lo + down(v) * s_hi

    def fwd(v, cos, s_lo, s_hi):
        return rope(v, cos, s_lo, s_hi), (cos, s_lo, s_hi)

    def bwd(res, dy):
        cos, s_lo, s_hi = res
        return dy * cos + down(dy * s_lo) + up(dy * s_hi), jnp.zeros_like(cos), jnp.zeros_like(cos), jnp.zeros_like(cos)

    rope.defvjp(fwd, bwd)
    return rope


_rope_a = _make_rope(HEAD_DIM // 4)
_rope_b = _make_rope(QK_ROPE // 4)


def _prep(segs, tabs, g_q, g_k, g_ckv):
    qa, ka, va, qn, qp, ckv, kpe = segs
    ta, tb = tabs
    q_a = jnp.concatenate([_rope_a(_rn(s) * g_q, *ta) for s in qa], axis=1)
    k_a = jnp.concatenate([_rope_a(_rn(s) * g_k, *ta) for s in ka], axis=1)
    cat = []
    for h in range(HB):
        cat += [qn[h], _rope_b(qp[h], *tb)]
    q_cat = jnp.concatenate(cat, axis=1)
    return q_a, k_a, va, q_cat, _rn(ckv) * g_ckv, _rope_b(kpe, *tb)


def _split_proj(proj):
    qa = [proj[:, O_QA + h * LANE:O_QA + (h + 1) * LANE] for h in range(HA)]
    ka = [proj[:, O_KA + h * LANE:O_KA + (h + 1) * LANE] for h in range(HKV)]
    va = proj[:, O_VA:O_QN]
    qn = [proj[:, O_QN + h * LANE:O_QN + (h + 1) * LANE] for h in range(HB)]
    qp = [proj[:, O_QP + h * LANE:O_QP + (h + 1) * LANE] for h in range(HB)]
    return qa, ka, va, qn, qp, proj[:, O_CKV:O_KPE], proj[:, O_KPE:W_INP]


def _permute_w_in(nat):
    K = nat.shape[1]
    qb = nat[W_QA + 2 * W_KA:W_QA + 2 * W_KA + W_QB].reshape(HB, QK_B, K)
    qp = jnp.concatenate([qb[:, QK_NOPE:], jnp.zeros((HB, LANE - QK_ROPE, K), nat.dtype)], axis=1)
    tail = nat[W_QA + 2 * W_KA + W_QB:]
    return jnp.concatenate([nat[:O_QN], qb[:, :QK_NOPE].reshape(HB * QK_NOPE, K), qp.reshape(HB * LANE, K),
                            tail, jnp.zeros((LANE - QK_ROPE, K), nat.dtype)], axis=0)


def _unpermute_pieces():
    pieces = [(0, 0, O_QN)]
    for h in range(HB):
        pieces.append((O_QN + h * QK_NOPE, O_QN + h * QK_B, QK_NOPE))
        pieces.append((O_QP + h * LANE, O_QN + h * QK_B + QK_NOPE, QK_ROPE))
    pieces.append((O_CKV, O_QN + W_QB, KV_RANK))
    pieces.append((O_KPE, O_QN + W_QB + KV_RANK, QK_ROPE))
    return pieces


def kernel(x, c, w_ada, b_ada, g_pre_attn, w_in, g_q_a, g_k_a, g_ckv, w_kv_b, g_out_a, g_out_b, w_out, g_post_attn, g_pre_mlp, w_mlp_in, w_mlp_out, g_post_mlp, loss_target, m_w_ada, m_b_ada, m_g_pre_attn, m_w_in, m_g_q_a, m_g_k_a, m_g_ckv, m_w_kv_b, m_g_out_a, m_g_out_b, m_w_out, m_g_post_attn, m_g_pre_mlp, m_w_mlp_in, m_w_mlp_out, m_g_post_mlp, v_w_ada, v_b_ada, v_g_pre_attn, v_w_in, v_g_q_a, v_g_k_a, v_g_ckv, v_w_kv_b, v_g_out_a, v_g_out_b, v_w_out, v_g_post_attn, v_g_pre_mlp, v_w_mlp_in, v_w_mlp_out, v_g_post_mlp):
    S, D = x.shape[1], x.shape[2]
    x2d, tgt = x[0], loss_target[0]
    ix, iy, ic = lax.axis_index("x"), lax.axis_index("y"), lax.axis_index("c")
    j_me = _chip(ix, iy)
    e_me = 4 * ix + 2 * iy + ic
    T = _pick(S, 256, 8)
    TB = _pick(S, 128, 8)

    where = jnp.stack([ic, j_me]).astype(jnp.int32)
    DFF = w_mlp_out.shape[1] * N_CHIPS

    c_all = _allgather8("gather_c", c.reshape(8, D // 8)).reshape(N_DEV, D)
    w_in_t = jnp.swapaxes(w_in[0], 0, 1)
    h_qkv, tok_a = _gather_start("qkv", where, [w_in_t, w_kv_b[0]], (c_all,))

    def silu_fn(cc):
        return (cc * (1.0 / (1.0 + jnp.exp(-cc))),), ()

    (c_act,), _ = _rowwise("silu_c", silu_fn, [c_all], [], [(D, BF16)], [], N_DEV)
    ca_pad = jnp.concatenate([c_act, jnp.zeros_like(c_act)], axis=0)
    n_mod = w_ada.shape[2]
    b_cols = lax.dynamic_slice(b_ada, (0, j_me * n_mod), (1, n_mod))
    (mod_part,) = _mm_nn("mod_mm", ca_pad, w_ada, [F32], epilogue=lambda acc, b: (acc + b,),
                         extras=(jnp.broadcast_to(b_cols, (16, n_mod)),))
    mod_all = _allgather8("gather_mod", mod_part[:8] + tok_a[0, 0]).reshape(N_DEV, 8, n_mod)
    mod = jnp.concatenate([lax.dynamic_slice(mod_all, (2 * j, e_me, 0), (1, 1, n_mod))[0] for j in range(N_CHIPS)],
                          axis=1)

    h_out, tok_b = _gather_start("out", where, [w_out[0]], (mod_all,), (tok_a,))
    h_up, tok_c = _gather_start("up", where, [w_mlp_in[0]], (tok_b,), (tok_b,))
    h_down, tok_d = _gather_start("down", where, [w_mlp_out[0]], (tok_c,), (tok_c,))
    mod = mod + tok_d[0, 0]
    sh_a, sc_a, gt_a, sh_m, sc_m, gt_m = [mod[:, i * D:(i + 1) * D] for i in range(6)]

    tabs_a = _rope_tables(S, HEAD_DIM, 1)
    tabs_b = _rope_tables(S, QK_ROPE, LANE // QK_ROPE)

    def pre_attn_fn(xt, g, sc, sh):
        return (_rn(xt) * g * (1.0 + sc) + sh,), ()

    (h_b,), _ = _rowwise("pre_attn", pre_attn_fn, [x2d], [g_pre_attn, sc_a, sh_a], [(D, BF16)], [], T)
    g_in, g_kvb = _gather_finish(h_qkv, h_b)
    wpt = _permute_w_in(g_in.reshape(W_IN, D))[None]
    (proj,) = _mm_nt("proj_mm", h_b, wpt, [F32])

    def prep_fn(pt, ca_, sla, sha, cb_, slb, shb, gq, gk, gc):
        return _prep(_split_proj(pt), ((ca_, sla, sha), (cb_, slb, shb)), gq, gk, gc), ()

    (qa_b, ka_b, va_b, qcat_b, ckvn_b, kpe_b), _ = _rowwise(
        "prep", prep_fn, [proj, *tabs_a, *tabs_b], [g_q_a, g_k_a, g_ckv],
        [(W_QA, BF16), (W_KA, BF16), (W_KA, BF16), (2 * HB * LANE, BF16), (KV_RANK, BF16), (LANE, BF16)], [], TB)
    (kv,) = _mm_nn("kv_mm", ckvn_b, g_kvb, [F32])

    def kcat_fn(kvt, kpet):
        kc, vs = [], []
        for h in range(HB):
            kc += [kvt[:, 2 * h * LANE:(2 * h + 1) * LANE], kpet.astype(F32)]
            vs.append(kvt[:, (2 * h + 1) * LANE:(2 * h + 2) * LANE])
        return (jnp.concatenate(kc, axis=1), jnp.concatenate(vs, axis=1)), ()

    (kcat_b, vb_b), _ = _rowwise("kcat", kcat_fn, [kv, kpe_b], [], [(2 * HB * LANE, BF16), (HB * V_DIM, BF16)], [], T)

    TQ = _pick(S, 512, 16)
    sc_a_ = 1.0 / math.sqrt(HEAD_DIM)
    sc_b_ = 1.0 / math.sqrt(QK_B)
    o_a, lse_a = _attn_fwd("attn_a_fwd", qa_b, ka_b, va_b, HA, HA // HKV, HEAD_DIM, HEAD_DIM, sc_a_, TQ)
    m_out, tok_m = _gather_mid(h_out, o_a)
    o_b, lse_b = _attn_fwd("attn_b_fwd", qcat_b, kcat_b, vb_b, HB, 1, 2 * LANE, V_DIM, sc_b_, TQ, after=(tok_m,))
    m_up, tok_m = _gather_mid(h_up, o_b)

    def mix_fn(oa, ob, ga, gb):
        return (jnp.concatenate([_rn(oa) * ga, _rn(ob) * gb], axis=1),), ()

    (on_b,), _ = _rowwise("mix_norm", mix_fn, [o_a, o_b], [g_out_a, g_out_b], [(2 * W_QA, BF16)], [], T,
                          after=(tok_m,))
    (g_wout,) = _gather_end(m_out, on_b)
    w_out_f = g_wout.reshape(1, N_CHIPS * g_wout.shape[1], D)
    (o2,) = _mm_nn("out_mm", on_b, w_out_f, [F32])

    def mid(xt, o2t, gta, gpa, gpm, scm, shm):
        x1 = xt + gta * (_rn(o2t) * gpa)
        return x1, _rn(x1) * gpm * (1.0 + scm) + shm

    def mid_fn(*a):
        return mid(*a), ()

    mid_vecs = [gt_a, g_post_attn, g_pre_mlp, sc_m, sh_m]
    (x1, h2_b), _ = _rowwise("mid", mid_fn, [x2d, o2], mid_vecs, [(D, F32), (D, BF16)], [], T)
    (g_mlp_in,) = _gather_end(m_up, h2_b)
    relu_b, a_b = _mm_nn("mlp_in_mm", h2_b, g_mlp_in, [BF16, BF16],
                         epilogue=lambda acc: (jnp.maximum(acc, 0.0), jnp.square(jnp.maximum(acc, 0.0))))
    (g_mlp_out,) = _gather_finish(h_down, a_b)
    w_mlp_out_f = g_mlp_out.reshape(1, DFF, D)
    (y,) = _mm_nn("mlp_out_mm", a_b, w_mlp_out_f, [F32])

    def last_fn(yt, x1t, tt, gtm, gpo):
        n, r = _rn_parts(yt)
        err = x1t + gtm * (n * gpo) - tt
        dx2 = err * (1.0 / D)
        loss = 0.5 * jnp.sum(jnp.mean(err * err, axis=-1))
        s = _rows(dx2 * n)
        return (_rn_bwd(n, r, dx2 * (gtm * gpo)), dx2), (jnp.full((1, LANE), loss, F32), gpo * s, gtm * s)

    (dy_b, dx2), (loss_v, d_gt_m, d_g_post_mlp) = _rowwise(
        "loss_bwd", last_fn, [y, x1, tgt], [gt_m, g_post_mlp], [(D, BF16), (D, F32)], [LANE, D, D], TB)

    gw_mlp_out = _mm_tn("gw_mlp_out_mm", a_b, dy_b, 1).reshape(N_CHIPS, DFF // N_CHIPS, D)
    b_down, tok = _reduce_begin("down", [gw_mlp_out])
    (du_b,) = _mm_nt("d_mlp_out_mm", dy_b, w_mlp_out_f, [BF16],
                     epilogue=lambda acc, rt: (acc * (2.0 * rt.astype(F32)),), extras=(relu_b,), after=(tok,))
    r_down, tok = _reduce_start(b_down, where, du_b)
    gw_mlp_in = _mm_tn("gw_mlp_in_mm", h2_b, du_b, N_CHIPS, after=(tok,))
    b_up, tok = _reduce_begin("up", [gw_mlp_in])
    (dh2,) = _mm_nt("d_mlp_in_mm", du_b, g_mlp_in, [F32], after=(tok,))
    r_up, tok_up = _reduce_start(b_up, where, dh2)

    def mid_bwd_fn(xt, o2t, dh2t, dx2t, gta, gpa, gpm, scm, shm):
        n2, r2 = _rn_parts(o2t)
        n1, r1 = _rn_parts(xt + gta * (n2 * gpa))
        dx1 = dx2t + _rn_bwd(n1, r1, dh2t * (gpm * (1.0 + scm)))
        s1, s2 = _rows(dh2t * n1), _rows(dx1 * n2)
        do2 = _rn_bwd(n2, r2, dx1 * (gta * gpa))
        return (dx1, do2), (gpa * s2, gta * s2, (1.0 + scm) * s1, gpm * s1, _rows(dh2t))

    (dx1, do2_b), (d_gt_a, d_g_post_attn, d_g_pre_mlp, d_sc_m, d_sh_m) = _rowwise(
        "mid_bwd", mid_bwd_fn, [x2d, o2, dh2, dx2], mid_vecs, [(D, F32), (D, BF16)], [D] * 5, TB, after=(tok_up,))

    (d_on,) = _mm_nt("d_out_mm", do2_b, w_out_f, [F32])
    gw_out = _mm_tn("gw_out_mm", on_b, do2_b, 1).reshape(N_CHIPS, w_out.shape[1], D)
    b_out, tok_out = _reduce_begin("out", [gw_out])

    def mix_bwd_fn(oa, ob, dont, ga, gb):
        na, ra = _rn_parts(oa)
        nb, rb = _rn_parts(ob)
        da, db = dont[:, :W_QA], dont[:, W_QA:]
        return (_rn_bwd(na, ra, da * ga), _rn_bwd(nb, rb, db * gb)), (_rows(da * na), _rows(db * nb))

    (doa_b, dob_b), (d_g_out_a, d_g_out_b) = _rowwise(
        "mix_bwd", mix_bwd_fn, [o_a, o_b, d_on], [g_out_a, g_out_b], [(W_QA, BF16), (HB * V_DIM, BF16)],
        [W_QA, HB * V_DIM], T, after=(tok_out,))

    dqa, dka, dva = _attn_bwd("attn_a_bwd", qa_b, ka_b, va_b, doa_b, o_a, lse_a, HA, HA // HKV, HEAD_DIM, HEAD_DIM,
                              sc_a_, TQ)
    r_out, tok = _reduce_start(b_out, where, dqa)
    dqcat, dkcat, dvb = _attn_bwd("attn_b_bwd", qcat_b, kcat_b, vb_b, dob_b, o_b, lse_b, HB, 1, 2 * LANE, V_DIM,
                                  sc_b_, TQ, after=(tok,))

    def kcat_bwd_fn(dkc, dvt):
        dkv, dkpe = [], None
        for h in range(HB):
            dkv += [dkc[:, 2 * h * LANE:(2 * h + 1) * LANE], dvt[:, h * V_DIM:(h + 1) * V_DIM]]
            piece = dkc[:, (2 * h + 1) * LANE:(2 * h + 2) * LANE]
            dkpe = piece if dkpe is None else dkpe + piece
        return (jnp.concatenate(dkv, axis=1), dkpe), ()

    (dkv_b, dkpe), _ = _rowwise("kcat_bwd", kcat_bwd_fn, [dkcat, dvb], [], [(2 * HB * LANE, BF16), (LANE, F32)], [], T)
    (d_ckvn,) = _mm_nt("d_kv_mm", dkv_b, g_kvb, [F32])
    gw_kvb = _mm_tn("gw_kv_mm", ckvn_b, dkv_b, N_CHIPS)
    b_kvb, tok_kvb = _reduce_begin("kvb", [gw_kvb])

    def prep_bwd_fn(pt, dqat, dkat, dvat, dqct, dckt, dkpt, ca_, sla, sha, cb_, slb, shb, gq, gk, gc):
        tabs = ((ca_, sla, sha), (cb_, slb, shb))
        _, vjp = jax.vjp(lambda sg, a1, a2, a3: _prep(sg, tabs, a1, a2, a3), _split_proj(pt), gq, gk, gc)
        (dqa_s, dka_s, dva_s, dqn_s, dqp_s, dck_s, dkp_s), dgq, dgk, dgc = vjp((dqat, dkat, dvat, dqct, dckt, dkpt))
        dproj = jnp.concatenate([*dqa_s, *dka_s, dva_s, *dqn_s, *dqp_s, dck_s, dkp_s], axis=1)
        return (dproj,), (dgq, dgk, dgc)

    (dproj_b,), (d_g_q_a, d_g_k_a, d_g_ckv) = _rowwise(
        "prep_bwd", prep_bwd_fn, [proj, dqa, dka, dva, dqcat, d_ckvn, dkpe, *tabs_a, *tabs_b],
        [g_q_a, g_k_a, g_ckv], [(W_INP, BF16)], [HEAD_DIM, HEAD_DIM, KV_RANK], TB, after=(tok_kvb,))

    r_kvb, tok = _reduce_start(b_kvb, where, dproj_b)
    (dh,) = _mm_nn("d_proj_mm", dproj_b, wpt, [F32], after=(tok,))
    gw_in = _mm_tn_rows("gw_in_mm", dproj_b, h_b, W_IN, _unpermute_pieces()).reshape(N_CHIPS, W_IN // N_CHIPS, D)
    b_in, tok = _reduce_begin("in", [gw_in])

    def pre_attn_bwd_fn(xt, dht, dx1t, g, sc, sh):
        n, r = _rn_parts(xt)
        s = _rows(dht * n)
        return (dx1t + _rn_bwd(n, r, dht * (g * (1.0 + sc))),), ((1.0 + sc) * s, g * s, _rows(dht))

    (grad_x,), (d_g_pre_attn, d_sc_a, d_sh_a) = _rowwise(
        "pre_attn_bwd", pre_attn_bwd_fn, [x2d, dh, dx1], [g_pre_attn, sc_a, sh_a], [(D, F32)], [D] * 3, TB,
        after=(tok,))

    small_names = ["b_ada", "g_pre_attn", "g_q_a", "g_k_a", "g_ckv", "g_out_a", "g_out_b", "g_post_attn",
                   "g_pre_mlp", "g_post_mlp"]
    small_local = jnp.concatenate([d_sh_a, d_sc_a, d_gt_a, d_sh_m, d_sc_m, d_gt_m, d_g_pre_attn, d_g_q_a, d_g_k_a,
                                   d_g_ckv, d_g_out_a, d_g_out_b, d_g_post_attn, d_g_pre_mlp, d_g_post_mlp], axis=1)
    n_small = small_local.shape[1]
    small_all = _allgather8("gather_small", small_local.reshape(8, n_small // 8)).reshape(N_DEV, n_small)

    def sum8_fn(t):
        return (), (jnp.sum(t, axis=0, keepdims=True),)

    r_in, tok_in = _reduce_start(b_in, where, grad_x, (small_all,))
    _, (small_sum,) = _rowwise("sum_small", sum8_fn, [small_all], [], [], [n_small], N_DEV, after=(tok_in,))
    dmod_cols = lax.dynamic_slice(small_all, (0, j_me * n_mod), (N_DEV, n_mod))
    dmod_pad = jnp.concatenate([dmod_cols, jnp.zeros_like(dmod_cols)], axis=0).astype(BF16)
    gw_ada = _mm_tn("gw_ada_mm", ca_pad, dmod_pad, 1)[0]

    weights = dict(w_ada=w_ada, b_ada=b_ada, g_pre_attn=g_pre_attn, w_in=w_in, g_q_a=g_q_a, g_k_a=g_k_a, g_ckv=g_ckv,
                   w_kv_b=w_kv_b, g_out_a=g_out_a, g_out_b=g_out_b, w_out=w_out, g_post_attn=g_post_attn,
                   g_pre_mlp=g_pre_mlp, w_mlp_in=w_mlp_in, w_mlp_out=w_mlp_out, g_post_mlp=g_post_mlp)
    ms = dict(w_ada=m_w_ada, b_ada=m_b_ada, g_pre_attn=m_g_pre_attn, w_in=m_w_in, g_q_a=m_g_q_a, g_k_a=m_g_k_a,
              g_ckv=m_g_ckv, w_kv_b=m_w_kv_b, g_out_a=m_g_out_a, g_out_b=m_g_out_b, w_out=m_w_out,
              g_post_attn=m_g_post_attn, g_pre_mlp=m_g_pre_mlp, w_mlp_in=m_w_mlp_in, w_mlp_out=m_w_mlp_out,
              g_post_mlp=m_g_post_mlp)
    vs = dict(w_ada=v_w_ada, b_ada=v_b_ada, g_pre_attn=v_g_pre_attn, w_in=v_w_in, g_q_a=v_g_q_a, g_k_a=v_g_k_a,
              g_ckv=v_g_ckv, w_kv_b=v_w_kv_b, g_out_a=v_g_out_a, g_out_b=v_g_out_b, w_out=v_w_out,
              g_post_attn=v_g_post_attn, g_pre_mlp=v_g_pre_mlp, w_mlp_in=v_w_mlp_in, w_mlp_out=v_w_mlp_out,
              g_post_mlp=v_g_post_mlp)
    order = list(weights)
    grads, deltas, new_m, new_v = {}, {}, {}, {}

    def update(name, g, after=()):
        view = (lambda t: jnp.swapaxes(t, 0, 1)) if name == "w_in" else (lambda t: t)
        d_, m_, v_ = _adamw("adamw_" + name, view(weights[name][0]), g, view(ms[name][0]), view(vs[name][0]), after)
        grads[name], deltas[name], new_m[name], new_v[name] = (view(t)[None] for t in (g, d_, m_, v_))
        return d_

    chain = (("w_mlp_out", r_down), ("w_mlp_in", r_up), ("w_out", r_out), ("w_kv_b", r_kvb), ("w_in", r_in))
    swap, tok = _reduce_finish(chain[0][1], where, small_sum)
    last = update("w_ada", gw_ada, (tok,))
    for k, (name, _) in enumerate(chain):
        (g,) = _reduce_end(swap, last)
        if k + 1 < len(chain):
            swap, tok = _reduce_finish(chain[k + 1][1], where, g)
        last = update(name, g, (tok,))

    cat = lambda d: jnp.concatenate([d[n] for n in small_names], axis=1)
    d_, m_, v_ = _adamw("adamw_small", cat(weights), small_sum, cat(ms), cat(vs))
    off = 0
    for n in small_names:
        w_ = weights[n].shape[1]
        grads[n], deltas[n], new_m[n], new_v[n] = (a[:, off:off + w_] for a in (small_sum, d_, m_, v_))
        off += w_

    loss = lax.psum(loss_v[0, 0], ("x", "y", "c"))
    return (loss, grad_x[None], *[grads[n] for n in order], *[deltas[n] for n in order],
            *[new_m[n] for n in order], *[new_v[n] for n in order])
```

```python
import functools
import math

import numpy as np
import jax
import jax.numpy as jnp
from jax import lax
from jax.experimental import pallas as pl
from jax.experimental.pallas import tpu as pltpu

F32 = jnp.float32
BF16 = jnp.bfloat16
MESH = pl.DeviceIdType.MESH

EPS = 1e-6
GRID_W = 64
ROPE_THETA = 10000.0
HEAD_DIM = 128
HA = 8
HKV = 2
HB = 8
QK_NOPE = 128
QK_ROPE = 64
V_DIM = 128
KV_RANK = 512
QK_B = QK_NOPE + QK_ROPE
W_QA = HA * HEAD_DIM
W_KA = HKV * HEAD_DIM
W_QB = HB * QK_B
W_IN = W_QA + 2 * W_KA + W_QB + KV_RANK + QK_ROPE
N_CHIPS = 4
N_DEV = 8

LANE = 128
O_QA = 0
O_KA = O_QA + W_QA
O_VA = O_KA + W_KA
O_QN = O_VA + W_KA
O_QP = O_QN + HB * QK_NOPE
O_CKV = O_QP + HB * LANE
O_KPE = O_CKV + KV_RANK
W_INP = O_KPE + LANE

ADAM_LR = 0.001
ADAM_B1 = 0.9
ADAM_B2 = 0.999
ADAM_EPS = 1e-08
ADAM_WD = 0.01
ADAM_STEP = 10

VMEM_LIMIT = 56 * 1024 * 1024
HEADS_PER_STEP = 2


def _params(sem):
    return pltpu.CompilerParams(dimension_semantics=sem, vmem_limit_bytes=VMEM_LIMIT)


def _pick(dim, target, unit):
    best = None
    t = unit
    while t <= min(dim, target):
        if dim % t == 0:
            best = t
        t += unit
    return dim if best is None else best


def _rn(x):
    return x * lax.rsqrt(jnp.mean(x * x, axis=-1, keepdims=True) + EPS)


def _rn_parts(x):
    r = lax.rsqrt(jnp.mean(x * x, axis=-1, keepdims=True) + EPS)
    return x * r, r


def _rn_bwd(n, r, dn):
    return r * (dn - n * jnp.mean(dn * n, axis=-1, keepdims=True))


def _rows(t):
    return jnp.sum(t, axis=0, keepdims=True)


def _rowwise(name, fn, rows, vecs, row_outs, vec_outs, tile, after=()):
    S = rows[0].shape[0]
    assert S % tile == 0
    n_r, n_v, n_ro, n_vo, n_a = len(rows), len(vecs), len(row_outs), len(vec_outs), len(after)

    def body(*refs):
        r_in = refs[:n_r]
        v_in = refs[n_r:n_r + n_v]
        r_out = refs[n_r + n_v + n_a:n_r + n_v + n_a + n_ro]
        v_out = refs[n_r + n_v + n_a + n_ro:]
        ro, vo = fn(*[r[...] for r in r_in], *[v[...] for v in v_in])
        for ref, val in zip(r_out, ro):
            ref[...] = val.astype(ref.dtype)
        if n_vo:
            @pl.when(pl.program_id(0) == 0)
            def _():
                for ref in v_out:
                    ref[...] = jnp.zeros(ref.shape, ref.dtype)
            for ref, val in zip(v_out, vo):
                ref[...] += val

    in_specs = [pl.BlockSpec((tile, a.shape[1]), lambda i: (i, 0)) for a in rows]
    in_specs += [pl.BlockSpec(a.shape, lambda i: (0, 0)) for a in vecs]
    in_specs += [pl.BlockSpec(memory_space=pl.ANY) for _ in after]
    out_specs = [pl.BlockSpec((tile, w), lambda i: (i, 0)) for w, _ in row_outs]
    out_specs += [pl.BlockSpec((1, w), lambda i: (0, 0)) for w in vec_outs]
    out_shape = [jax.ShapeDtypeStruct((S, w), dt) for w, dt in row_outs]
    out_shape += [jax.ShapeDtypeStruct((1, w), F32) for w in vec_outs]
    res = pl.pallas_call(
        body, name=name, grid=(S // tile,), in_specs=in_specs, out_specs=out_specs, out_shape=out_shape,
        compiler_params=_params(("arbitrary",)),
    )(*rows, *vecs, *after)
    return res[:n_ro], res[n_ro:]


CONTRACT_TILE = 2048
MM_TILE_BYTES = 40 * 1024 * 1024


def _rows_that_fit(M, tile_bytes):
    t = _pick(M, 1024, 16)
    while t % 32 == 0 and tile_bytes(t) > MM_TILE_BYTES:
        t //= 2
    return t


def _accumulate(acc, part, step, n_steps, finish):
    if n_steps == 1:
        finish(part)
        return

    @pl.when(step == 0)
    def _():
        acc[...] = part

    @pl.when(step > 0)
    def _():
        acc[...] += part

    @pl.when(step == n_steps - 1)
    def _():
        finish(acc[...])


def _mm_nn(name, a, b3, out_dtypes, epilogue=None, extras=(), after=()):
    M, K = a.shape
    nB, K2, Nsh = b3.shape
    assert K == K2
    N = nB * Nsh
    tn, tk = _pick(Nsh, 1536, LANE), _pick(K, CONTRACT_TILE, LANE)
    out_bytes = sum(jnp.dtype(dt).itemsize for dt in out_dtypes) + sum(e.dtype.itemsize for e in extras)
    tm = _rows_that_fit(M, lambda t: 2 * (t * tk * a.dtype.itemsize + tk * tn * b3.dtype.itemsize)
                        + t * tn * (4 * (K > tk) + 2 * out_bytes))
    nps, nk, ne, no, na = Nsh // tn, K // tk, len(extras), len(out_dtypes), len(after)

    def body(a_ref, b_ref, *rest):
        e_refs, o_refs = rest[:ne], rest[ne + na:ne + na + no]

        def finish(total):
            vals = (total,) if epilogue is None else epilogue(total, *[e[...] for e in e_refs])
            for ref, val in zip(o_refs, vals):
                ref[...] = val.astype(ref.dtype)

        part = jnp.dot(a_ref[...].astype(BF16), b_ref[...].astype(BF16), preferred_element_type=F32)
        _accumulate(rest[-1], part, pl.program_id(2), nk, finish)

    in_specs = [pl.BlockSpec((tm, tk), lambda i, j, k: (i, k)),
                pl.BlockSpec((None, tk, tn), lambda i, j, k: (j // nps, k, j % nps))]
    in_specs += [pl.BlockSpec((tm, tn), lambda i, j, k: (i, j)) for _ in extras]
    in_specs += [pl.BlockSpec(memory_space=pl.ANY) for _ in after]
    return pl.pallas_call(
        body, name=name, grid=(M // tm, N // tn, nk), in_specs=in_specs,
        out_specs=[pl.BlockSpec((tm, tn), lambda i, j, k: (i, j)) for _ in out_dtypes],
        out_shape=[jax.ShapeDtypeStruct((M, N), dt) for dt in out_dtypes],
        scratch_shapes=[pltpu.VMEM((tm, tn), F32)] if nk > 1 else [],
        compiler_params=_params(("parallel", "parallel", "arbitrary")),
    )(a, b3, *extras, *after)


def _mm_nt(name, a, b3, out_dtypes, epilogue=None, extras=(), after=()):
    M, N = a.shape
    nB, K, Nsh = b3.shape
    assert N == nB * Nsh
    to, tn = _pick(K, 1536, LANE), _pick(Nsh, CONTRACT_TILE, LANE)
    out_bytes = sum(jnp.dtype(dt).itemsize for dt in out_dtypes) + sum(e.dtype.itemsize for e in extras)
    tm = _rows_that_fit(M, lambda t: 2 * (t * tn * a.dtype.itemsize + to * tn * b3.dtype.itemsize)
                        + t * to * (4 * (N > tn) + 2 * out_bytes))
    nps, nn, ne, no, na = Nsh // tn, N // tn, len(extras), len(out_dtypes), len(after)

    def body(a_ref, b_ref, *rest):
        e_refs, o_refs = rest[:ne], rest[ne + na:ne + na + no]

        def finish(total):
            vals = (total,) if epilogue is None else epilogue(total, *[e[...] for e in e_refs])
            for ref, val in zip(o_refs, vals):
                ref[...] = val.astype(ref.dtype)

        part = lax.dot_general(a_ref[...].astype(BF16), b_ref[...].astype(BF16),
                               (((1,), (1,)), ((), ())), preferred_element_type=F32)
        _accumulate(rest[-1], part, pl.program_id(2), nn, finish)

    in_specs = [pl.BlockSpec((tm, tn), lambda i, o, n: (i, n)),
                pl.BlockSpec((None, to, tn), lambda i, o, n: (n // nps, o, n % nps))]
    in_specs += [pl.BlockSpec((tm, to), lambda i, o, n: (i, o)) for _ in extras]
    in_specs += [pl.BlockSpec(memory_space=pl.ANY) for _ in after]
    return pl.pallas_call(
        body, name=name, grid=(M // tm, K // to, nn), in_specs=in_specs,
        out_specs=[pl.BlockSpec((tm, to), lambda i, o, n: (i, o)) for _ in out_dtypes],
        out_shape=[jax.ShapeDtypeStruct((M, K), dt) for dt in out_dtypes],
        scratch_shapes=[pltpu.VMEM((tm, to), F32)] if nn > 1 else [],
        compiler_params=_params(("parallel", "parallel", "arbitrary")),
    )(a, b3, *extras, *after)


def _mm_tn(name, a, g, nB, after=()):
    M, K = a.shape
    M2, N = g.shape
    assert M == M2 and N % nB == 0
    Nsh = N // nB
    tk, tn, tm = _pick(K, 1536, LANE), _pick(Nsh, 1024, LANE), _pick(M, CONTRACT_TILE, 16)
    nps, nm = Nsh // tn, M // tm

    def body(a_ref, g_ref, *rest):
        o_ref = rest[len(after)]

        def finish(total):
            o_ref[...] = total

        part = lax.dot_general(a_ref[...].astype(BF16), g_ref[...].astype(BF16),
                               (((0,), (0,)), ((), ())), preferred_element_type=F32)
        _accumulate(rest[-1], part, pl.program_id(2), nm, finish)

    return pl.pallas_call(
        body, name=name, grid=(K // tk, N // tn, nm),
        in_specs=[pl.BlockSpec((tm, tk), lambda k, n, m: (m, k)),
                  pl.BlockSpec((tm, tn), lambda k, n, m: (m, n))] + [pl.BlockSpec(memory_space=pl.ANY) for _ in after],
        out_specs=pl.BlockSpec((None, tk, tn), lambda k, n, m: (n // nps, k, n % nps)),
        out_shape=jax.ShapeDtypeStruct((nB, K, Nsh), F32),
        scratch_shapes=[pltpu.VMEM((tk, tn), F32)] if nm > 1 else [],
        compiler_params=_params(("parallel", "parallel", "arbitrary")),
    )(a, g, *after)


def _mm_tn_rows(name, a, g, n_rows, pieces):
    M, K = a.shape
    M2, N = g.shape
    assert M == M2
    tk, tn, tm = _pick(K, 1536, LANE), _pick(N, 512, LANE), _pick(M, CONTRACT_TILE, 16)
    nk, nm = K // tk, M // tm

    def body(a_ref, g_ref, o_ref, *acc):
        kk = pl.program_id(1)

        def finish(total):
            for t in range(nk):
                @pl.when(kk == t)
                def _(t=t):
                    for src, dst, n in pieces:
                        lo, hi = max(src, t * tk), min(src + n, (t + 1) * tk)
                        if lo < hi:
                            o_ref[dst + lo - src:dst + hi - src, :] = total[lo - t * tk:hi - t * tk, :]

        part = lax.dot_general(a_ref[...].astype(BF16), g_ref[...].astype(BF16),
                               (((0,), (0,)), ((), ())), preferred_element_type=F32)
        _accumulate(acc[0] if acc else None, part, pl.program_id(2), nm, finish)

    return pl.pallas_call(
        body, name=name, grid=(N // tn, nk, nm),
        in_specs=[pl.BlockSpec((tm, tk), lambda n, k, m: (m, k)),
                  pl.BlockSpec((tm, tn), lambda n, k, m: (m, n))],
        out_specs=pl.BlockSpec((n_rows, tn), lambda n, k, m: (0, n)),
        out_shape=jax.ShapeDtypeStruct((n_rows, N), F32),
        scratch_shapes=[pltpu.VMEM((tk, tn), F32)] if nm > 1 else [],
        compiler_params=_params(("arbitrary", "arbitrary", "arbitrary")),
    )(a, g)


def _attn_fwd(name, q, k, v, n_heads, group, dqk, dv, scale, tq, after=()):
    S = q.shape[0]
    hp = 2 * HEADS_PER_STEP
    kvp = max(1, hp // group)

    def body(q_ref, k_ref, v_ref, *rest):
        o_ref, lse_ref = rest[-2:]
        for j in range(hp):
            jk = j * kvp // hp
            s = lax.dot_general(q_ref[:, j * dqk:(j + 1) * dqk], k_ref[:, jk * dqk:(jk + 1) * dqk],
                                (((1,), (1,)), ((), ())), preferred_element_type=F32)
            m = jnp.max(s, axis=-1, keepdims=True)
            e = jnp.exp((s - m) * scale)
            l = jnp.sum(e, axis=-1, keepdims=True)
            o_ref[:, j * dv:(j + 1) * dv] = jnp.dot(e.astype(BF16), v_ref[:, jk * dv:(jk + 1) * dv],
                                                    preferred_element_type=F32) * (1.0 / l)
            lse_ref[:, j * LANE:(j + 1) * LANE] = jnp.broadcast_to(m * scale + jnp.log(l), (tq, LANE))

    kv_block = lambda h, i: (0, (h * hp // group) // kvp)
    return pl.pallas_call(
        body, name=name, grid=(n_heads // hp, S // tq),
        in_specs=[pl.BlockSpec((tq, hp * dqk), lambda h, i: (i, h)),
                  pl.BlockSpec((S, kvp * dqk), kv_block),
                  pl.BlockSpec((S, kvp * dv), kv_block)] + [pl.BlockSpec(memory_space=pl.ANY) for _ in after],
        out_specs=[pl.BlockSpec((tq, hp * dv), lambda h, i: (i, h)),
                   pl.BlockSpec((tq, hp * LANE), lambda h, i: (i, h))],
        out_shape=[jax.ShapeDtypeStruct((S, n_heads * dv), F32), jax.ShapeDtypeStruct((S, n_heads * LANE), F32)],
        compiler_params=_params(("arbitrary", "arbitrary")),
    )(q, k, v, *after)


def _attn_bwd(name, q, k, v, do, o, lse, n_heads, group, dqk, dv, scale, tq, after=()):
    S = q.shape[0]
    n_kv = n_heads // group
    hp = HEADS_PER_STEP
    kvp = max(1, hp // group)

    def body(q_ref, k_ref, v_ref, do_ref, o_ref, lse_ref, *rest):
        dq_ref, dk_ref, dv_ref = rest[len(after):]
        h, i = pl.program_id(0), pl.program_id(1)

        @pl.when(((h * hp) % group == 0) & (i == 0))
        def _():
            dk_ref[...] = jnp.zeros(dk_ref.shape, F32)
            dv_ref[...] = jnp.zeros(dv_ref.shape, F32)

        for j in range(hp):
            jk = j * kvp // hp
            qb, dob = q_ref[:, j * dqk:(j + 1) * dqk], do_ref[:, j * dv:(j + 1) * dv]
            kb, vb = k_ref[:, jk * dqk:(jk + 1) * dqk], v_ref[:, jk * dv:(jk + 1) * dv]
            s = lax.dot_general(qb, kb, (((1,), (1,)), ((), ())), preferred_element_type=F32)
            p = jnp.exp(s * scale - jnp.tile(lse_ref[:, j * LANE:(j + 1) * LANE], (1, S // LANE)))
            dp = lax.dot_general(dob, vb, (((1,), (1,)), ((), ())), preferred_element_type=F32)
            delta = jnp.sum(dob.astype(F32) * o_ref[:, j * dv:(j + 1) * dv], axis=-1, keepdims=True)
            ds = (p * ((dp - delta) * scale)).astype(BF16)
            dv_ref[:, jk * dv:(jk + 1) * dv] += lax.dot_general(p.astype(BF16), dob, (((0,), (0,)), ((), ())),
                                                                preferred_element_type=F32)
            dq_ref[:, j * dqk:(j + 1) * dqk] = jnp.dot(ds, kb, preferred_element_type=F32)
            dk_ref[:, jk * dqk:(jk + 1) * dqk] += lax.dot_general(ds, qb, (((0,), (0,)), ((), ())),
                                                                  preferred_element_type=F32)

    kv_block = lambda h, i: (0, (h * hp // group) // kvp)
    return pl.pallas_call(
        body, name=name, grid=(n_heads // hp, S // tq),
        in_specs=[pl.BlockSpec((tq, hp * dqk), lambda h, i: (i, h)),
                  pl.BlockSpec((S, kvp * dqk), kv_block),
                  pl.BlockSpec((S, kvp * dv), kv_block),
                  pl.BlockSpec((tq, hp * dv), lambda h, i: (i, h)),
                  pl.BlockSpec((tq, hp * dv), lambda h, i: (i, h)),
                  pl.BlockSpec((tq, hp * LANE), lambda h, i: (i, h))] + [pl.BlockSpec(memory_space=pl.ANY) for _ in after],
        out_specs=[pl.BlockSpec((tq, hp * dqk), lambda h, i: (i, h)),
                   pl.BlockSpec((S, kvp * dqk), kv_block),
                   pl.BlockSpec((S, kvp * dv), kv_block)],
        out_shape=[jax.ShapeDtypeStruct((S, n_heads * dqk), F32),
                   jax.ShapeDtypeStruct((S, n_kv * dqk), F32),
                   jax.ShapeDtypeStruct((S, n_kv * dv), F32)],
        compiler_params=_params(("arbitrary", "arbitrary")),
    )(q, k, v, do, o, lse, *after)


def _adamw(name, w, g, m, v, after=()):
    R, C = w.shape
    tile = _pick(R, max(8, (1 << 19) // C // 8 * 8), 8)
    c1 = 1.0 - ADAM_B1 ** ADAM_STEP
    c2 = 1.0 - ADAM_B2 ** ADAM_STEP

    def body(w_ref, g_ref, m_ref, v_ref, *rest):
        d_ref, nm_ref, nv_ref = rest[len(after):]
        gg = g_ref[...]
        nm = ADAM_B1 * m_ref[...] + (1.0 - ADAM_B1) * gg
        nv = ADAM_B2 * v_ref[...] + (1.0 - ADAM_B2) * (gg * gg)
        d_ref[...] = -ADAM_LR * ((nm / c1) / (jnp.sqrt(nv / c2) + ADAM_EPS) + ADAM_WD * w_ref[...])
        nm_ref[...] = nm
        nv_ref[...] = nv

    spec = pl.BlockSpec((tile, C), lambda i: (i, 0))
    return pl.pallas_call(
        body, name=name, grid=(R // tile,), in_specs=[spec] * 4 + [pl.BlockSpec(memory_space=pl.ANY) for _ in after],
        out_specs=[spec] * 3, out_shape=[jax.ShapeDtypeStruct((R, C), F32)] * 3,
        compiler_params=_params(("parallel",)),
    )(w, g, m, v, *after)


def _flip(me, rel):
    return tuple(1 - p if r else p for p, r in zip(me, rel))


def _push(name, ins, out_shapes, transfers, aliases=None):
    ni, no, nt = len(ins), len(out_shapes), len(transfers)

    def body(*refs):
        in_refs, out_refs = refs[:ni], refs[ni:ni + no]
        send_sems, recv_sems = refs[ni + no], refs[ni + no + 1]
        me = (lax.axis_index("x"), lax.axis_index("y"), lax.axis_index("c"))
        copies = []
        for t, (si, sfn, oi, dfn, rel) in enumerate(transfers):
            src_ref = out_refs[si[1]] if isinstance(si, tuple) else in_refs[si]
            src = src_ref.at[sfn(*me)]
            dst = out_refs[oi].at[dfn(*me)]
            if rel is None:
                cp = pltpu.make_async_copy(src, dst, send_sems.at[t])
            else:
                cp = pltpu.make_async_remote_copy(src_ref=src, dst_ref=dst, send_sem=send_sems.at[t],
                                                  recv_sem=recv_sems.at[t], device_id=_flip(me, rel),
                                                  device_id_type=MESH)
            cp.start()
            copies.append(cp)
        for cp in copies:
            cp.wait()

    any_spec = pl.BlockSpec(memory_space=pl.ANY)
    return pl.pallas_call(
        body, name=name, in_specs=[any_spec] * ni, out_specs=[any_spec] * no, out_shape=out_shapes,
        scratch_shapes=[pltpu.SemaphoreType.DMA((nt,)), pltpu.SemaphoreType.DMA((nt,))],
        input_output_aliases=aliases or {},
        compiler_params=pltpu.CompilerParams(has_side_effects=True),
    )(*ins)


_HBM = pl.BlockSpec(memory_space=pltpu.HBM)
_SEM = pl.BlockSpec(memory_space=pltpu.SEMAPHORE)
_DATAFLOW = pltpu.SideEffectType.DATAFLOW_SIDE_EFFECTING


def _split_copies(refs, transfers, send_sems, recv_sems):
    me = (lax.axis_index("x"), lax.axis_index("y"), lax.axis_index("c"))
    return [pltpu.make_async_remote_copy(src_ref=refs[sb].at[sfn(*me)], dst_ref=refs[db].at[dfn(*me)],
                                         send_sem=send_sems.at[t], recv_sem=recv_sems.at[t],
                                         device_id=_flip(me, rel), device_id_type=MESH)
            for t, (sb, sfn, db, dfn, rel) in enumerate(transfers)]


def _push_start(name, bufs, transfers, after=()):
    nb, na, nt = len(bufs), len(after), len(transfers)

    def body(*refs):
        send_sems, recv_sems, token = refs[nb + na], refs[nb + na + 1], refs[-1]
        for cp in _split_copies(refs[nb + na + 2:2 * nb + na + 2], transfers, send_sems, recv_sems):
            cp.start()
        token[...] = jnp.zeros(token.shape, token.dtype)

    res = pl.pallas_call(
        body, name=name,
        out_shape=(pltpu.SemaphoreType.DMA((nt,)), pltpu.SemaphoreType.DMA((nt,)),
                   *[pltpu.HBM(b.shape, b.dtype) for b in bufs], jax.ShapeDtypeStruct((8, LANE), F32)),
        in_specs=[_HBM] * nb + [pl.BlockSpec(memory_space=pl.ANY)] * na,
        out_specs=(_SEM, _SEM, *[_HBM] * nb, pl.BlockSpec(memory_space=pltpu.VMEM)),
        input_output_aliases={i: i + 2 for i in range(nb)},
        compiler_params=pltpu.CompilerParams(has_side_effects=_DATAFLOW),
    )(*[pltpu.with_memory_space_constraint(b, pltpu.HBM) for b in bufs], *after)
    return res[0], res[1], list(res[2:2 + nb]), res[-1]


def _push_wait(name, send_sems, recv_sems, bufs, transfers, after):
    nb = len(bufs)

    def body(*refs):
        for cp in _split_copies(refs[:nb], transfers, refs[nb], refs[nb + 1]):
            cp.wait_send()
            cp.wait_recv()

    res = pl.pallas_call(
        body, name=name, out_shape=[pltpu.HBM(b.shape, b.dtype) for b in bufs],
        in_specs=[_HBM] * nb + [_SEM, _SEM, pl.BlockSpec(memory_space=pl.ANY)], out_specs=[_HBM] * nb,
        input_output_aliases={i: i for i in range(nb)},
        compiler_params=pltpu.CompilerParams(has_side_effects=_DATAFLOW),
    )(*bufs, send_sems, recv_sems, after)
    return list(res)


ICI_RELS = ((1, 0, 0), (0, 1, 0), (1, 1, 0))
SIBLING = (0, 0, 1)


def _chip(x, y):
    return 2 * x + y


def _cast_into_slot(name, where, w, after=()):
    R, C = w.shape
    tile = _pick(R, 512, BF16_ROWS)

    def body(where_ref, w_ref, *rest):
        rest[-1][...] = w_ref[...].astype(BF16)

    return pl.pallas_call(
        body, name=name,
        grid_spec=pltpu.PrefetchScalarGridSpec(
            num_scalar_prefetch=1, grid=(R // tile,),
            in_specs=[pl.BlockSpec((tile, C), lambda i, wh: (i, 0))] + [pl.BlockSpec(memory_space=pl.ANY) for _ in after],
            out_specs=pl.BlockSpec((None, tile, C), lambda i, wh: (wh[1], i, 0))),
        out_shape=jax.ShapeDtypeStruct((N_CHIPS, R, C), BF16),
        compiler_params=_params(("parallel",)),
    )(where, w, *after)


BF16_ROWS = 16


def _rows_split(shape2):
    return (shape2[0] // 2) % BF16_ROWS == 0


def _half_shape(shape2):
    R, C = shape2
    return (R // 2, C) if _rows_split(shape2) else (R, C // 2)


def _half(shape2, c):
    R, C = shape2
    if _rows_split(shape2):
        return (pl.ds(c * (R // 2), R // 2), slice(None))
    return (slice(None), pl.ds(pl.multiple_of(c * (C // 2), LANE), C // 2))


def _gather_casts(tag, where, shards, after=()):
    return [_cast_into_slot(f"cast_w_{tag}{w}", where, s, after) for w, s in enumerate(shards)]


def _gather_start(tag, own, after):
    transfers = []
    for w in range(len(own)):
        def idx(x, y, c, shape2=own[w].shape[1:]):
            return (_chip(x, y), *_half(shape2, c))
        transfers += [(w, idx, w, idx, rel) for rel in ICI_RELS]
    send_sems, recv_sems, bufs, token = _push_start(f"gather_{tag}_start", own, transfers, after)
    return (tag, send_sems, recv_sems, bufs, transfers), token


def _forward_transfers(part):
    second = []
    for w, p in enumerate(part):
        for rel in ICI_RELS:
            def idx(x, y, c, shape2=p.shape[1:], rel=rel):
                return (_chip(x ^ rel[0], y ^ rel[1]), *_half(shape2, c))
            second.append((w, idx, w, idx, SIBLING))
    return second


def _gather_finish(handle, after):
    tag, send_sems, recv_sems, bufs, transfers = handle
    part = _push_wait(f"gather_{tag}_wait", send_sems, recv_sems, bufs, transfers, after)
    outs = [jax.ShapeDtypeStruct(p.shape, p.dtype) for p in part]
    second = [(("out", sb), sfn, db, dfn, rel) for sb, sfn, db, dfn, rel in _forward_transfers(part)]
    return _push(f"gather_{tag}_d2d", list(part), outs, second, aliases={w: w for w in range(len(part))})


def _gather_mid(handle, after):
    tag, send_sems, recv_sems, bufs, transfers = handle
    part = _push_wait(f"gather_{tag}_wait", send_sems, recv_sems, bufs, transfers, after)
    second = _forward_transfers(part)
    s2, r2, bufs2, token = _push_start(f"gather_{tag}_d2d_start", part, second)
    return (tag, s2, r2, bufs2, second), token


def _gather_end(handle, after):
    tag, send_sems, recv_sems, bufs, transfers = handle
    return _push_wait(f"gather_{tag}_d2d_wait", send_sems, recv_sems, bufs, transfers, after)


def _reduce_begin(tag, grads, after=()):
    n = len(grads)
    land = [lax.empty((N_CHIPS, *_half_shape(g.shape[1:])), F32) for g in grads]
    t1 = []
    for w in range(n):
        t1.append((w, lambda x, y, c, shape2=grads[w].shape[1:]: (slice(None), *_half(shape2, 1 - c)), n + w,
                   lambda x, y, c: (slice(None), slice(None), slice(None)), SIBLING))
    send_sems, recv_sems, bufs, token = _push_start(f"reduce_{tag}_d2d_start", list(grads) + land, t1, after)
    return (tag, send_sems, recv_sems, bufs, t1), token


def _reduce_start(handle, where, after, after_start=()):
    tag, send_sems, recv_sems, bufs, t1 = handle
    n = len(bufs) // 2
    res = _push_wait(f"reduce_{tag}_d2d_wait", send_sems, recv_sems, bufs, t1, after)
    grads, from_sib = res[:n], res[n:]
    pair = [_pair_sum(f"pair_sum_{tag}{w}", where, grads[w], from_sib[w]) for w in range(n)]
    land = [lax.empty((3, *_half_shape(g.shape[1:])), BF16) for g in grads]
    t3 = []
    for w in range(n):
        for r, rel in enumerate(ICI_RELS):
            t3.append((w, lambda x, y, c, rel=rel: (_chip(x ^ rel[0], y ^ rel[1]), slice(None), slice(None)), n + w,
                       lambda x, y, c, r=r: (r, slice(None), slice(None)), rel))
    send_sems, recv_sems, bufs, token = _push_start(f"reduce_{tag}_start", pair + land, t3, after_start)
    return (tag, send_sems, recv_sems, bufs, t3, list(grads), list(from_sib)), token


def _reduce_finish(handle, where, after):
    tag, send_sems, recv_sems, bufs, t3, grads, from_sib = handle
    n = len(grads)
    from_chips = _push_wait(f"reduce_{tag}_wait", send_sems, recv_sems, bufs, t3, after)[n:]
    mine = [_chip_sum(f"chip_sum_{tag}{w}", where, grads[w], from_sib[w], from_chips[w]) for w in range(n)]
    t5 = []
    for w in range(n):
        def idx(x, y, c, shape2=grads[w].shape[1:]):
            return _half(shape2, c)
        t5.append((w, idx, w, idx, SIBLING))
    send_sems, recv_sems, bufs, token = _push_start(f"reduce_{tag}_swap_start", mine, t5)
    return (tag, send_sems, recv_sems, bufs, t5), token


def _reduce_end(handle, after):
    tag, send_sems, recv_sems, bufs, t5 = handle
    return _push_wait(f"reduce_{tag}_swap_wait", send_sems, recv_sems, bufs, t5, after)


def _pair_sum(name, where, g, from_sib):
    _, hr, hc = from_sib.shape
    tile = _pick(hr, max(16, (1 << 19) // hc // 16 * 16), 16)
    nb = hr // tile
    slot = lambda k, wh: (wh[1] + 1 + k) % N_CHIPS
    if _rows_split(g.shape[1:]):
        mine = lambda k, i, wh: (slot(k, wh), wh[0] * nb + i, 0)
    else:
        mine = lambda k, i, wh: (slot(k, wh), i, wh[0])

    def body(where_ref, g_ref, s_ref, o_ref):
        o_ref[...] = (g_ref[...] + s_ref[...]).astype(BF16)

    return pl.pallas_call(
        body, name=name,
        grid_spec=pltpu.PrefetchScalarGridSpec(
            num_scalar_prefetch=1, grid=(N_CHIPS - 1, nb),
            in_specs=[pl.BlockSpec((None, tile, hc), mine),
                      pl.BlockSpec((None, tile, hc), lambda k, i, wh: (slot(k, wh), i, 0))],
            out_specs=pl.BlockSpec((None, tile, hc), lambda k, i, wh: (slot(k, wh), i, 0))),
        out_shape=jax.ShapeDtypeStruct((N_CHIPS, hr, hc), BF16),
        compiler_params=_params(("parallel", "parallel")),
    )(where, g, from_sib)


def _chip_sum(name, where, g, from_sib, from_chips):
    _, R, C = g.shape
    _, hr, hc = from_sib.shape
    tile = _pick(hr, max(16, (1 << 19) // hc // 16 * 16), 16)
    nb = hr // tile
    if _rows_split((R, C)):
        mine = lambda i, wh: (wh[0] * nb + i, 0)
    else:
        mine = lambda i, wh: (i, wh[0])

    def body(where_ref, g_ref, s_ref, r_ref, o_ref):
        acc = g_ref[...] + s_ref[...]
        for r in range(3):
            acc = acc + r_ref[r].astype(F32)
        o_ref[...] = acc

    return pl.pallas_call(
        body, name=name,
        grid_spec=pltpu.PrefetchScalarGridSpec(
            num_scalar_prefetch=1, grid=(nb,),
            in_specs=[pl.BlockSpec((None, tile, hc), lambda i, wh: (wh[1], *mine(i, wh))),
                      pl.BlockSpec((None, tile, hc), lambda i, wh: (wh[1], i, 0)),
                      pl.BlockSpec((3, tile, hc), lambda i, wh: (0, i, 0))],
            out_specs=pl.BlockSpec((tile, hc), mine)),
        out_shape=jax.ShapeDtypeStruct((R, C), F32),
        compiler_params=_params(("parallel",)),
    )(where, g, from_sib, from_chips)


def _allgather8(name, blk, after=()):
    m_per, n = blk.shape
    rels = [(0, 0, 1), (1, 0, 0), (0, 1, 0), (1, 1, 0), (1, 0, 1), (0, 1, 1), (1, 1, 1)]

    def body(x_ref, *rest):
        out_ref, send_sems, recv_sems = rest[len(after):]
        me = (lax.axis_index("x"), lax.axis_index("y"), lax.axis_index("c"))
        my_rows = out_ref.at[pl.ds((4 * me[0] + 2 * me[1] + me[2]) * m_per, m_per), :]
        out_ref[pl.ds((4 * me[0] + 2 * me[1] + me[2]) * m_per, m_per), :] = x_ref[...]
        copies = []
        for t, rel in enumerate(rels):
            cp = pltpu.make_async_remote_copy(src_ref=x_ref, dst_ref=my_rows, send_sem=send_sems.at[t],
                                              recv_sem=recv_sems.at[t], device_id=_flip(me, rel),
                                              device_id_type=MESH)
            cp.start()
            copies.append(cp)
        for cp in copies:
            cp.wait()

    return pl.pallas_call(
        body, name=name, out_shape=jax.ShapeDtypeStruct((N_DEV * m_per, n), blk.dtype),
        in_specs=[pl.BlockSpec(memory_space=pltpu.VMEM)] + [pl.BlockSpec(memory_space=pl.ANY) for _ in after],
        out_specs=pl.BlockSpec(memory_space=pltpu.VMEM),
        scratch_shapes=[pltpu.SemaphoreType.DMA((7,)), pltpu.SemaphoreType.DMA((7,))],
        compiler_params=pltpu.CompilerParams(has_side_effects=True, vmem_limit_bytes=VMEM_LIMIT),
    )(blk, *after)


def _rope_tables(seq_len, dim, reps):
    rows = seq_len // GRID_W
    t = np.arange(seq_len)
    row, col = (t // GRID_W).astype(np.float32), (t % GRID_W).astype(np.float32)
    half = dim // 2
    inv = jnp.asarray(ROPE_THETA, F32) ** (-jnp.arange(0, half, 2, dtype=F32) / half)
    ang_r = jnp.asarray(row)[:, None] * inv[None, :]
    ang_c = jnp.asarray(col)[:, None] * inv[None, :]
    ang = jnp.concatenate([ang_r, ang_r, ang_c, ang_c], axis=-1)
    cos, sin = jnp.cos(ang), jnp.sin(ang)
    low = (np.arange(dim) % (dim // 2)) < (dim // 4)
    s_lo = jnp.where(jnp.asarray(low)[None, :], -sin, 0.0)
    s_hi = jnp.where(jnp.asarray(low)[None, :], 0.0, sin)
    del rows
    return tuple(jnp.tile(a, (1, reps)) for a in (cos, s_lo, s_hi))


def _make_rope(shift):
    def up(v):
        return pltpu.roll(v, LANE - shift, 1)

    def down(v):
        return pltpu.roll(v, shift, 1)

    @jax.custom_vjp
    def rope(v, cos, s_lo, s_hi):
        return v * cos + up(v) * s_lo + down(v) * s_hi

    def fwd(v, cos, s_lo, s_hi):
        return rope(v, cos, s_lo, s_hi), (cos, s_lo, s_hi)

    def bwd(res, dy):
        cos, s_lo, s_hi = res
        return dy * cos + down(dy * s_lo) + up(dy * s_hi), jnp.zeros_like(cos), jnp.zeros_like(cos), jnp.zeros_like(cos)

    rope.defvjp(fwd, bwd)
    return rope


_rope_a = _make_rope(HEAD_DIM // 4)
_rope_b = _make_rope(QK_ROPE // 4)


def _prep(segs, tabs, g_q, g_k, g_ckv):
    qa, ka, va, qn, qp, ckv, kpe = segs
    ta, tb = tabs
    q_a = jnp.concatenate([_rope_a(_rn(s) * g_q, *ta) for s in qa], axis=1)
    k_a = jnp.concatenate([_rope_a(_rn(s) * g_k, *ta) for s in ka], axis=1)
    cat = []
    for h in range(HB):
        cat += [qn[h], _rope_b(qp[h], *tb)]
    q_cat = jnp.concatenate(cat, axis=1)
    return q_a, k_a, va, q_cat, _rn(ckv) * g_ckv, _rope_b(kpe, *tb)


def _split_proj(proj):
    qa = [proj[:, O_QA + h * LANE:O_QA + (h + 1) * LANE] for h in range(HA)]
    ka = [proj[:, O_KA + h * LANE:O_KA + (h + 1) * LANE] for h in range(HKV)]
    va = proj[:, O_VA:O_QN]
    qn = [proj[:, O_QN + h * LANE:O_QN + (h + 1) * LANE] for h in range(HB)]
    qp = [proj[:, O_QP + h * LANE:O_QP + (h + 1) * LANE] for h in range(HB)]
    return qa, ka, va, qn, qp, proj[:, O_CKV:O_KPE], proj[:, O_KPE:W_INP]


def _permute_w_in(nat):
    K = nat.shape[1]
    qb = nat[W_QA + 2 * W_KA:W_QA + 2 * W_KA + W_QB].reshape(HB, QK_B, K)
    qp = jnp.concatenate([qb[:, QK_NOPE:], jnp.zeros((HB, LANE - QK_ROPE, K), nat.dtype)], axis=1)
    tail = nat[W_QA + 2 * W_KA + W_QB:]
    return jnp.concatenate([nat[:O_QN], qb[:, :QK_NOPE].reshape(HB * QK_NOPE, K), qp.reshape(HB * LANE, K),
                            tail, jnp.zeros((LANE - QK_ROPE, K), nat.dtype)], axis=0)


def _unpermute_pieces():
    pieces = [(0, 0, O_QN)]
    for h in range(HB):
        pieces.append((O_QN + h * QK_NOPE, O_QN + h * QK_B, QK_NOPE))
        pieces.append((O_QP + h * LANE, O_QN + h * QK_B + QK_NOPE, QK_ROPE))
    pieces.append((O_CKV, O_QN + W_QB, KV_RANK))
    pieces.append((O_KPE, O_QN + W_QB + KV_RANK, QK_ROPE))
    return pieces


def kernel(x, c, w_ada, b_ada, g_pre_attn, w_in, g_q_a, g_k_a, g_ckv, w_kv_b, g_out_a, g_out_b, w_out, g_post_attn, g_pre_mlp, w_mlp_in, w_mlp_out, g_post_mlp, loss_target, m_w_ada, m_b_ada, m_g_pre_attn, m_w_in, m_g_q_a, m_g_k_a, m_g_ckv, m_w_kv_b, m_g_out_a, m_g_out_b, m_w_out, m_g_post_attn, m_g_pre_mlp, m_w_mlp_in, m_w_mlp_out, m_g_post_mlp, v_w_ada, v_b_ada, v_g_pre_attn, v_w_in, v_g_q_a, v_g_k_a, v_g_ckv, v_w_kv_b, v_g_out_a, v_g_out_b, v_w_out, v_g_post_attn, v_g_pre_mlp, v_w_mlp_in, v_w_mlp_out, v_g_post_mlp):
    S, D = x.shape[1], x.shape[2]
    x2d, tgt = x[0], loss_target[0]
    ix, iy, ic = lax.axis_index("x"), lax.axis_index("y"), lax.axis_index("c")
    j_me = _chip(ix, iy)
    e_me = 4 * ix + 2 * iy + ic
    T = _pick(S, 256, 8)
    TB = _pick(S, 128, 8)

    where = jnp.stack([ic, j_me]).astype(jnp.int32)
    DFF = w_mlp_out.shape[1] * N_CHIPS

    c_all = _allgather8("gather_c", c.reshape(8, D // 8)).reshape(N_DEV, D)
    w_in_t = jnp.swapaxes(w_in[0], 0, 1)
    h_qkv, tok_a = _gather_start("qkv", _gather_casts("qkv", where, [w_in_t, w_kv_b[0]]), (c_all,))
    later = [_gather_casts(tag, where, [w[0]], (tok_a,))
             for tag, w in (("out", w_out), ("up", w_mlp_in), ("down", w_mlp_out))]

    def silu_fn(cc):
        return (cc * (1.0 / (1.0 + jnp.exp(-cc))),), ()

    (c_act,), _ = _rowwise("silu_c", silu_fn, [c_all], [], [(D, BF16)], [], N_DEV)
    ca_pad = jnp.concatenate([c_act, jnp.zeros_like(c_act)], axis=0)
    n_mod = w_ada.shape[2]
    b_cols = lax.dynamic_slice(b_ada, (0, j_me * n_mod), (1, n_mod))
    (mod_part,) = _mm_nn("mod_mm", ca_pad, w_ada, [F32], epilogue=lambda acc, b: (acc + b,),
                         extras=(jnp.broadcast_to(b_cols, (16, n_mod)),))
    mod_all = _allgather8("gather_mod", mod_part[:8] + tok_a[0, 0],
                          after=[c_[0] for c_ in later]).reshape(N_DEV, 8, n_mod)
    mod = jnp.concatenate([lax.dynamic_slice(mod_all, (2 * j, e_me, 0), (1, 1, n_mod))[0] for j in range(N_CHIPS)],
                          axis=1)

    h_out, tok_b = _gather_start("out", later[0], (mod_all,))
    h_up, tok_c = _gather_start("up", later[1], (tok_b,))
    h_down, tok_d = _gather_start("down", later[2], (tok_c,))
    mod = mod + tok_d[0, 0]
    sh_a, sc_a, gt_a, sh_m, sc_m, gt_m = [mod[:, i * D:(i + 1) * D] for i in range(6)]

    tabs_a = _rope_tables(S, HEAD_DIM, 1)
    tabs_b = _rope_tables(S, QK_ROPE, LANE // QK_ROPE)

    def pre_attn_fn(xt, g, sc, sh):
        return (_rn(xt) * g * (1.0 + sc) + sh,), ()

    (h_b,), _ = _rowwise("pre_attn", pre_attn_fn, [x2d], [g_pre_attn, sc_a, sh_a], [(D, BF16)], [], T)
    g_in, g_kvb = _gather_finish(h_qkv, h_b)
    wpt = _permute_w_in(g_in.reshape(W_IN, D))[None]
    (proj,) = _mm_nt("proj_mm", h_b, wpt, [F32])

    def prep_fn(pt, ca_, sla, sha, cb_, slb, shb, gq, gk, gc):
        return _prep(_split_proj(pt), ((ca_, sla, sha), (cb_, slb, shb)), gq, gk, gc), ()

    (qa_b, ka_b, va_b, qcat_b, ckvn_b, kpe_b), _ = _rowwise(
        "prep", prep_fn, [proj, *tabs_a, *tabs_b], [g_q_a, g_k_a, g_ckv],
        [(W_QA, BF16), (W_KA, BF16), (W_KA, BF16), (2 * HB * LANE, BF16), (KV_RANK, BF16), (LANE, BF16)], [], TB)
    (kv,) = _mm_nn("kv_mm", ckvn_b, g_kvb, [F32])

    def kcat_fn(kvt, kpet):
        kc, vs = [], []
        for h in range(HB):
            kc += [kvt[:, 2 * h * LANE:(2 * h + 1) * LANE], kpet.astype(F32)]
            vs.append(kvt[:, (2 * h + 1) * LANE:(2 * h + 2) * LANE])
        return (jnp.concatenate(kc, axis=1), jnp.concatenate(vs, axis=1)), ()

    (kcat_b, vb_b), _ = _rowwise("kcat", kcat_fn, [kv, kpe_b], [], [(2 * HB * LANE, BF16), (HB * V_DIM, BF16)], [], T)

    TQ = _pick(S, 512, 16)
    sc_a_ = 1.0 / math.sqrt(HEAD_DIM)
    sc_b_ = 1.0 / math.sqrt(QK_B)
    o_a, lse_a = _attn_fwd("attn_a_fwd", qa_b, ka_b, va_b, HA, HA // HKV, HEAD_DIM, HEAD_DIM, sc_a_, TQ)
    m_out, tok_m = _gather_mid(h_out, o_a)
    o_b, lse_b = _attn_fwd("attn_b_fwd", qcat_b, kcat_b, vb_b, HB, 1, 2 * LANE, V_DIM, sc_b_, TQ, after=(tok_m,))
    m_up, tok_m = _gather_mid(h_up, o_b)

    def mix_fn(oa, ob, ga, gb):
        return (jnp.concatenate([_rn(oa) * ga, _rn(ob) * gb], axis=1),), ()

    (on_b,), _ = _rowwise("mix_norm", mix_fn, [o_a, o_b], [g_out_a, g_out_b], [(2 * W_QA, BF16)], [], T,
                          after=(tok_m,))
    (g_wout,) = _gather_end(m_out, on_b)
    w_out_f = g_wout.reshape(1, N_CHIPS * g_wout.shape[1], D)
    (o2,) = _mm_nn("out_mm", on_b, w_out_f, [F32])

    def mid(xt, o2t, gta, gpa, gpm, scm, shm):
        x1 = xt + gta * (_rn(o2t) * gpa)
        return x1, _rn(x1) * gpm * (1.0 + scm) + shm

    def mid_fn(*a):
        return mid(*a), ()

    mid_vecs = [gt_a, g_post_attn, g_pre_mlp, sc_m, sh_m]
    (x1, h2_b), _ = _rowwise("mid", mid_fn, [x2d, o2], mid_vecs, [(D, F32), (D, BF16)], [], T)
    (g_mlp_in,) = _gather_end(m_up, h2_b)
    relu_b, a_b = _mm_nn("mlp_in_mm", h2_b, g_mlp_in, [BF16, BF16],
                         epilogue=lambda acc: (jnp.maximum(acc, 0.0), jnp.square(jnp.maximum(acc, 0.0))))
    (g_mlp_out,) = _gather_finish(h_down, a_b)
    w_mlp_out_f = g_mlp_out.reshape(1, DFF, D)
    (y,) = _mm_nn("mlp_out_mm", a_b, w_mlp_out_f, [F32])

    def last_fn(yt, x1t, tt, gtm, gpo):
        n, r = _rn_parts(yt)
        err = x1t + gtm * (n * gpo) - tt
        dx2 = err * (1.0 / D)
        loss = 0.5 * jnp.sum(jnp.mean(err * err, axis=-1))
        s = _rows(dx2 * n)
        return (_rn_bwd(n, r, dx2 * (gtm * gpo)), dx2), (jnp.full((1, LANE), loss, F32), gpo * s, gtm * s)

    (dy_b, dx2), (loss_v, d_gt_m, d_g_post_mlp) = _rowwise(
        "loss_bwd", last_fn, [y, x1, tgt], [gt_m, g_post_mlp], [(D, BF16), (D, F32)], [LANE, D, D], TB)

    gw_mlp_out = _mm_tn("gw_mlp_out_mm", a_b, dy_b, 1).reshape(N_CHIPS, DFF // N_CHIPS, D)
    b_down, tok = _reduce_begin("down", [gw_mlp_out])
    (du_b,) = _mm_nt("d_mlp_out_mm", dy_b, w_mlp_out_f, [BF16],
                     epilogue=lambda acc, rt: (acc * (2.0 * rt.astype(F32)),), extras=(relu_b,), after=(tok,))
    r_down, tok = _reduce_start(b_down, where, du_b)
    gw_mlp_in = _mm_tn("gw_mlp_in_mm", h2_b, du_b, N_CHIPS, after=(tok,))
    b_up, tok = _reduce_begin("up", [gw_mlp_in])
    (dh2,) = _mm_nt("d_mlp_in_mm", du_b, g_mlp_in, [F32], after=(tok,))
    r_up, tok_up = _reduce_start(b_up, where, dh2)

    def mid_bwd_fn(xt, o2t, dh2t, dx2t, gta, gpa, gpm, scm, shm):
        n2, r2 = _rn_parts(o2t)
        n1, r1 = _rn_parts(xt + gta * (n2 * gpa))
        dx1 = dx2t + _rn_bwd(n1, r1, dh2t * (gpm * (1.0 + scm)))
        s1, s2 = _rows(dh2t * n1), _rows(dx1 * n2)
        do2 = _rn_bwd(n2, r2, dx1 * (gta * gpa))
        return (dx1, do2), (gpa * s2, gta * s2, (1.0 + scm) * s1, gpm * s1, _rows(dh2t))

    (dx1, do2_b), (d_gt_a, d_g_post_attn, d_g_pre_mlp, d_sc_m, d_sh_m) = _rowwise(
        "mid_bwd", mid_bwd_fn, [x2d, o2, dh2, dx2], mid_vecs, [(D, F32), (D, BF16)], [D] * 5, TB, after=(tok_up,))

    (d_on,) = _mm_nt("d_out_mm", do2_b, w_out_f, [F32])
    gw_out = _mm_tn("gw_out_mm", on_b, do2_b, 1).reshape(N_CHIPS, w_out.shape[1], D)
    b_out, tok_out = _reduce_begin("out", [gw_out])

    def mix_bwd_fn(oa, ob, dont, ga, gb):
        na, ra = _rn_parts(oa)
        nb, rb = _rn_parts(ob)
        da, db = dont[:, :W_QA], dont[:, W_QA:]
        return (_rn_bwd(na, ra, da * ga), _rn_bwd(nb, rb, db * gb)), (_rows(da * na), _rows(db * nb))

    (doa_b, dob_b), (d_g_out_a, d_g_out_b) = _rowwise(
        "mix_bwd", mix_bwd_fn, [o_a, o_b, d_on], [g_out_a, g_out_b], [(W_QA, BF16), (HB * V_DIM, BF16)],
        [W_QA, HB * V_DIM], T, after=(tok_out,))

    dqa, dka, dva = _attn_bwd("attn_a_bwd", qa_b, ka_b, va_b, doa_b, o_a, lse_a, HA, HA // HKV, HEAD_DIM, HEAD_DIM,
                              sc_a_, TQ)
    r_out, tok = _reduce_start(b_out, where, dqa)
    dqcat, dkcat, dvb = _attn_bwd("attn_b_bwd", qcat_b, kcat_b, vb_b, dob_b, o_b, lse_b, HB, 1, 2 * LANE, V_DIM,
                                  sc_b_, TQ, after=(tok,))

    def kcat_bwd_fn(dkc, dvt):
        dkv, dkpe = [], None
        for h in range(HB):
            dkv += [dkc[:, 2 * h * LANE:(2 * h + 1) * LANE], dvt[:, h * V_DIM:(h + 1) * V_DIM]]
            piece = dkc[:, (2 * h + 1) * LANE:(2 * h + 2) * LANE]
            dkpe = piece if dkpe is None else dkpe + piece
        return (jnp.concatenate(dkv, axis=1), dkpe), ()

    (dkv_b, dkpe), _ = _rowwise("kcat_bwd", kcat_bwd_fn, [dkcat, dvb], [], [(2 * HB * LANE, BF16), (LANE, F32)], [], T)
    (d_ckvn,) = _mm_nt("d_kv_mm", dkv_b, g_kvb, [F32])
    gw_kvb = _mm_tn("gw_kv_mm", ckvn_b, dkv_b, N_CHIPS)
    b_kvb, tok_kvb = _reduce_begin("kvb", [gw_kvb])

    def prep_bwd_fn(pt, dqat, dkat, dvat, dqct, dckt, dkpt, ca_, sla, sha, cb_, slb, shb, gq, gk, gc):
        tabs = ((ca_, sla, sha), (cb_, slb, shb))
        _, vjp = jax.vjp(lambda sg, a1, a2, a3: _prep(sg, tabs, a1, a2, a3), _split_proj(pt), gq, gk, gc)
        (dqa_s, dka_s, dva_s, dqn_s, dqp_s, dck_s, dkp_s), dgq, dgk, dgc = vjp((dqat, dkat, dvat, dqct, dckt, dkpt))
        dproj = jnp.concatenate([*dqa_s, *dka_s, dva_s, *dqn_s, *dqp_s, dck_s, dkp_s], axis=1)
        return (dproj,), (dgq, dgk, dgc)

    (dproj_b,), (d_g_q_a, d_g_k_a, d_g_ckv) = _rowwise(
        "prep_bwd", prep_bwd_fn, [proj, dqa, dka, dva, dqcat, d_ckvn, dkpe, *tabs_a, *tabs_b],
        [g_q_a, g_k_a, g_ckv], [(W_INP, BF16)], [HEAD_DIM, HEAD_DIM, KV_RANK], TB, after=(tok_kvb,))

    r_kvb, tok = _reduce_start(b_kvb, where, dproj_b)
    (dh,) = _mm_nn("d_proj_mm", dproj_b, wpt, [F32], after=(tok,))
    gw_in = _mm_tn_rows("gw_in_mm", dproj_b, h_b, W_IN, _unpermute_pieces()).reshape(N_CHIPS, W_IN // N_CHIPS, D)
    b_in, tok = _reduce_begin("in", [gw_in])

    def pre_attn_bwd_fn(xt, dht, dx1t, g, sc, sh):
        n, r = _rn_parts(xt)
        s = _rows(dht * n)
        return (dx1t + _rn_bwd(n, r, dht * (g * (1.0 + sc))),), ((1.0 + sc) * s, g * s, _rows(dht))

    (grad_x,), (d_g_pre_attn, d_sc_a, d_sh_a) = _rowwise(
        "pre_attn_bwd", pre_attn_bwd_fn, [x2d, dh, dx1], [g_pre_attn, sc_a, sh_a], [(D, F32)], [D] * 3, TB,
        after=(tok,))

    small_names = ["b_ada", "g_pre_attn", "g_q_a", "g_k_a", "g_ckv", "g_out_a", "g_out_b", "g_post_attn",
                   "g_pre_mlp", "g_post_mlp"]
    small_local = jnp.concatenate([d_sh_a, d_sc_a, d_gt_a, d_sh_m, d_sc_m, d_gt_m, d_g_pre_attn, d_g_q_a, d_g_k_a,
                                   d_g_ckv, d_g_out_a, d_g_out_b, d_g_post_attn, d_g_pre_mlp, d_g_post_mlp,
                                   loss_v[:, :8]], axis=1)
    n_sent = small_local.shape[1]
    n_small = n_sent - 8
    small_all = _allgather8("gather_small", small_local.reshape(8, n_sent // 8)).reshape(N_DEV, n_sent)

    def sum8_fn(t):
        return (), (jnp.sum(t, axis=0, keepdims=True),)

    r_in, tok_in = _reduce_start(b_in, where, grad_x, (small_all,))
    _, (sent_sum,) = _rowwise("sum_small", sum8_fn, [small_all], [], [], [n_sent], N_DEV, after=(tok_in,))
    small_sum, loss = sent_sum[:, :n_small], sent_sum[0, n_small]
    dmod_cols = lax.dynamic_slice(small_all, (0, j_me * n_mod), (N_DEV, n_mod))
    dmod_pad = jnp.concatenate([dmod_cols, jnp.zeros_like(dmod_cols)], axis=0).astype(BF16)
    gw_ada = _mm_tn("gw_ada_mm", ca_pad, dmod_pad, 1)[0]

    weights = dict(w_ada=w_ada, b_ada=b_ada, g_pre_attn=g_pre_attn, w_in=w_in, g_q_a=g_q_a, g_k_a=g_k_a, g_ckv=g_ckv,
                   w_kv_b=w_kv_b, g_out_a=g_out_a, g_out_b=g_out_b, w_out=w_out, g_post_attn=g_post_attn,
                   g_pre_mlp=g_pre_mlp, w_mlp_in=w_mlp_in, w_mlp_out=w_mlp_out, g_post_mlp=g_post_mlp)
    ms = dict(w_ada=m_w_ada, b_ada=m_b_ada, g_pre_attn=m_g_pre_attn, w_in=m_w_in, g_q_a=m_g_q_a, g_k_a=m_g_k_a,
              g_ckv=m_g_ckv, w_kv_b=m_w_kv_b, g_out_a=m_g_out_a, g_out_b=m_g_out_b, w_out=m_w_out,
              g_post_attn=m_g_post_attn, g_pre_mlp=m_g_pre_mlp, w_mlp_in=m_w_mlp_in, w_mlp_out=m_w_mlp_out,
              g_post_mlp=m_g_post_mlp)
    vs = dict(w_ada=v_w_ada, b_ada=v_b_ada, g_pre_attn=v_g_pre_attn, w_in=v_w_in, g_q_a=v_g_q_a, g_k_a=v_g_k_a,
              g_ckv=v_g_ckv, w_kv_b=v_w_kv_b, g_out_a=v_g_out_a, g_out_b=v_g_out_b, w_out=v_w_out,
              g_post_attn=v_g_post_attn, g_pre_mlp=v_g_pre_mlp, w_mlp_in=v_w_mlp_in, w_mlp_out=v_w_mlp_out,
              g_post_mlp=v_g_post_mlp)
    order = list(weights)
    grads, deltas, new_m, new_v = {}, {}, {}, {}

    def update(name, g, after=()):
        view = (lambda t: jnp.swapaxes(t, 0, 1)) if name == "w_in" else (lambda t: t)
        d_, m_, v_ = _adamw("adamw_" + name, view(weights[name][0]), g, view(ms[name][0]), view(vs[name][0]), after)
        grads[name], deltas[name], new_m[name], new_v[name] = (view(t)[None] for t in (g, d_, m_, v_))
        return d_

    chain = (("w_mlp_out", r_down), ("w_mlp_in", r_up), ("w_out", r_out), ("w_kv_b", r_kvb), ("w_in", r_in))
    swap, tok = _reduce_finish(chain[0][1], where, small_sum)
    last = update("w_ada", gw_ada, (tok,))
    for k, (name, _) in enumerate(chain):
        (g,) = _reduce_end(swap, last)
        if k + 1 < len(chain):
            swap, tok = _reduce_finish(chain[k + 1][1], where, g)
        last = update(name, g, (tok,))

    cat = lambda d: jnp.concatenate([d[n] for n in small_names], axis=1)
    d_, m_, v_ = _adamw("adamw_small", cat(weights), small_sum, cat(ms), cat(vs))
    off = 0
    for n in small_names:
        w_ = weights[n].shape[1]
        grads[n], deltas[n], new_m[n], new_v[n] = (a[:, off:off + w_] for a in (small_sum, d_, m_, v_))
        off += w_

    return (loss, grad_x[None], *[grads[n] for n in order], *[deltas[n] for n in order],
            *[new_m[n] for n in order], *[new_v[n] for n in order])
```

```python
import functools
import math

import numpy as np
import jax
import jax.numpy as jnp
from jax import lax
from jax.experimental import pallas as pl
from jax.experimental.pallas import tpu as pltpu

F32 = jnp.float32
BF16 = jnp.bfloat16
MESH = pl.DeviceIdType.MESH

EPS = 1e-6
GRID_W = 64
ROPE_THETA = 10000.0
HEAD_DIM = 128
HA = 8
HKV = 2
HB = 8
QK_NOPE = 128
QK_ROPE = 64
V_DIM = 128
KV_RANK = 512
QK_B = QK_NOPE + QK_ROPE
W_QA = HA * HEAD_DIM
W_KA = HKV * HEAD_DIM
W_QB = HB * QK_B
W_IN = W_QA + 2 * W_KA + W_QB + KV_RANK + QK_ROPE
N_CHIPS = 4
N_DEV = 8

LANE = 128
O_QA = 0
O_KA = O_QA + W_QA
O_VA = O_KA + W_KA
O_QN = O_VA + W_KA
O_QP = O_QN + HB * QK_NOPE
O_CKV = O_QP + HB * LANE
O_KPE = O_CKV + KV_RANK
W_INP = O_KPE + LANE

ADAM_LR = 0.001
ADAM_B1 = 0.9
ADAM_B2 = 0.999
ADAM_EPS = 1e-08
ADAM_WD = 0.01
ADAM_STEP = 10

VMEM_LIMIT = 56 * 1024 * 1024
HEADS_PER_STEP = 2


def _params(sem):
    return pltpu.CompilerParams(dimension_semantics=sem, vmem_limit_bytes=VMEM_LIMIT)


def _pick(dim, target, unit):
    best = None
    t = unit
    while t <= min(dim, target):
        if dim % t == 0:
            best = t
        t += unit
    return dim if best is None else best


def _rn(x):
    return x * lax.rsqrt(jnp.mean(x * x, axis=-1, keepdims=True) + EPS)


def _rn_parts(x):
    r = lax.rsqrt(jnp.mean(x * x, axis=-1, keepdims=True) + EPS)
    return x * r, r


def _rn_bwd(n, r, dn):
    return r * (dn - n * jnp.mean(dn * n, axis=-1, keepdims=True))


def _rows(t):
    return jnp.sum(t, axis=0, keepdims=True)


def _rowwise(name, fn, rows, vecs, row_outs, vec_outs, tile, after=()):
    S = rows[0].shape[0]
    assert S % tile == 0
    n_r, n_v, n_ro, n_vo, n_a = len(rows), len(vecs), len(row_outs), len(vec_outs), len(after)

    def body(*refs):
        r_in = refs[:n_r]
        v_in = refs[n_r:n_r + n_v]
        r_out = refs[n_r + n_v + n_a:n_r + n_v + n_a + n_ro]
        v_out = refs[n_r + n_v + n_a + n_ro:]
        ro, vo = fn(*[r[...] for r in r_in], *[v[...] for v in v_in])
        for ref, val in zip(r_out, ro):
            ref[...] = val.astype(ref.dtype)
        if n_vo:
            @pl.when(pl.program_id(0) == 0)
            def _():
                for ref in v_out:
                    ref[...] = jnp.zeros(ref.shape, ref.dtype)
            for ref, val in zip(v_out, vo):
                ref[...] += val

    in_specs = [pl.BlockSpec((tile, a.shape[1]), lambda i: (i, 0)) for a in rows]
    in_specs += [pl.BlockSpec(a.shape, lambda i: (0, 0)) for a in vecs]
    in_specs += [pl.BlockSpec(memory_space=pl.ANY) for _ in after]
    out_specs = [pl.BlockSpec((tile, w), lambda i: (i, 0)) for w, _ in row_outs]
    out_specs += [pl.BlockSpec((1, w), lambda i: (0, 0)) for w in vec_outs]
    out_shape = [jax.ShapeDtypeStruct((S, w), dt) for w, dt in row_outs]
    out_shape += [jax.ShapeDtypeStruct((1, w), F32) for w in vec_outs]
    res = pl.pallas_call(
        body, name=name, grid=(S // tile,), in_specs=in_specs, out_specs=out_specs, out_shape=out_shape,
        compiler_params=_params(("arbitrary",)),
    )(*rows, *vecs, *after)
    return res[:n_ro], res[n_ro:]


CONTRACT_TILE = 2048
MM_TILE_BYTES = 40 * 1024 * 1024


def _rows_that_fit(M, tile_bytes):
    t = _pick(M, 1024, 16)
    while t % 32 == 0 and tile_bytes(t) > MM_TILE_BYTES:
        t //= 2
    return t


def _accumulate(acc, part, step, n_steps, finish):
    if n_steps == 1:
        finish(part)
        return

    @pl.when(step == 0)
    def _():
        acc[...] = part

    @pl.when(step > 0)
    def _():
        acc[...] += part

    @pl.when(step == n_steps - 1)
    def _():
        finish(acc[...])


def _mm_nn(name, a, b3, out_dtypes, epilogue=None, extras=(), after=()):
    M, K = a.shape
    nB, K2, Nsh = b3.shape
    assert K == K2
    N = nB * Nsh
    tn, tk = _pick(Nsh, 1536, LANE), _pick(K, CONTRACT_TILE, LANE)
    out_bytes = sum(jnp.dtype(dt).itemsize for dt in out_dtypes) + sum(e.dtype.itemsize for e in extras)
    tm = _rows_that_fit(M, lambda t: 2 * (t * tk * a.dtype.itemsize + tk * tn * b3.dtype.itemsize)
                        + t * tn * (4 * (K > tk) + 2 * out_bytes))
    nps, nk, ne, no, na = Nsh // tn, K // tk, len(extras), len(out_dtypes), len(after)

    def body(a_ref, b_ref, *rest):
        e_refs, o_refs = rest[:ne], rest[ne + na:ne + na + no]

        def finish(total):
            vals = (total,) if epilogue is None else epilogue(total, *[e[...] for e in e_refs])
            for ref, val in zip(o_refs, vals):
                ref[...] = val.astype(ref.dtype)

        part = jnp.dot(a_ref[...].astype(BF16), b_ref[...].astype(BF16), preferred_element_type=F32)
        _accumulate(rest[-1], part, pl.program_id(2), nk, finish)

    in_specs = [pl.BlockSpec((tm, tk), lambda i, j, k: (i, k)),
                pl.BlockSpec((None, tk, tn), lambda i, j, k: (j // nps, k, j % nps))]
    in_specs += [pl.BlockSpec((tm, tn), lambda i, j, k: (i, j)) for _ in extras]
    in_specs += [pl.BlockSpec(memory_space=pl.ANY) for _ in after]
    return pl.pallas_call(
        body, name=name, grid=(M // tm, N // tn, nk), in_specs=in_specs,
        out_specs=[pl.BlockSpec((tm, tn), lambda i, j, k: (i, j)) for _ in out_dtypes],
        out_shape=[jax.ShapeDtypeStruct((M, N), dt) for dt in out_dtypes],
        scratch_shapes=[pltpu.VMEM((tm, tn), F32)] if nk > 1 else [],
        compiler_params=_params(("parallel", "parallel", "arbitrary")),
    )(a, b3, *extras, *after)


def _mm_nt(name, a, b3, out_dtypes, epilogue=None, extras=(), after=()):
    M, N = a.shape
    nB, K, Nsh = b3.shape
    assert N == nB * Nsh
    to, tn = _pick(K, 1536, LANE), _pick(Nsh, CONTRACT_TILE, LANE)
    out_bytes = sum(jnp.dtype(dt).itemsize for dt in out_dtypes) + sum(e.dtype.itemsize for e in extras)
    tm = _rows_that_fit(M, lambda t: 2 * (t * tn * a.dtype.itemsize + to * tn * b3.dtype.itemsize)
                        + t * to * (4 * (N > tn) + 2 * out_bytes))
    nps, nn, ne, no, na = Nsh // tn, N // tn, len(extras), len(out_dtypes), len(after)

    def body(a_ref, b_ref, *rest):
        e_refs, o_refs = rest[:ne], rest[ne + na:ne + na + no]

        def finish(total):
            vals = (total,) if epilogue is None else epilogue(total, *[e[...] for e in e_refs])
            for ref, val in zip(o_refs, vals):
                ref[...] = val.astype(ref.dtype)

        part = lax.dot_general(a_ref[...].astype(BF16), b_ref[...].astype(BF16),
                               (((1,), (1,)), ((), ())), preferred_element_type=F32)
        _accumulate(rest[-1], part, pl.program_id(2), nn, finish)

    in_specs = [pl.BlockSpec((tm, tn), lambda i, o, n: (i, n)),
                pl.BlockSpec((None, to, tn), lambda i, o, n: (n // nps, o, n % nps))]
    in_specs += [pl.BlockSpec((tm, to), lambda i, o, n: (i, o)) for _ in extras]
    in_specs += [pl.BlockSpec(memory_space=pl.ANY) for _ in after]
    return pl.pallas_call(
        body, name=name, grid=(M // tm, K // to, nn), in_specs=in_specs,
        out_specs=[pl.BlockSpec((tm, to), lambda i, o, n: (i, o)) for _ in out_dtypes],
        out_shape=[jax.ShapeDtypeStruct((M, K), dt) for dt in out_dtypes],
        scratch_shapes=[pltpu.VMEM((tm, to), F32)] if nn > 1 else [],
        compiler_params=_params(("parallel", "parallel", "arbitrary")),
    )(a, b3, *extras, *after)


def _mm_tn(name, a, g, nB, after=()):
    M, K = a.shape
    M2, N = g.shape
    assert M == M2 and N % nB == 0
    Nsh = N // nB
    tk, tn, tm = _pick(K, 1536, LANE), _pick(Nsh, 1024, LANE), _pick(M, CONTRACT_TILE, 16)
    nps, nm = Nsh // tn, M // tm

    def body(a_ref, g_ref, *rest):
        o_ref = rest[len(after)]

        def finish(total):
            o_ref[...] = total

        part = lax.dot_general(a_ref[...].astype(BF16), g_ref[...].astype(BF16),
                               (((0,), (0,)), ((), ())), preferred_element_type=F32)
        _accumulate(rest[-1], part, pl.program_id(2), nm, finish)

    return pl.pallas_call(
        body, name=name, grid=(K // tk, N // tn, nm),
        in_specs=[pl.BlockSpec((tm, tk), lambda k, n, m: (m, k)),
                  pl.BlockSpec((tm, tn), lambda k, n, m: (m, n))] + [pl.BlockSpec(memory_space=pl.ANY) for _ in after],
        out_specs=pl.BlockSpec((None, tk, tn), lambda k, n, m: (n // nps, k, n % nps)),
        out_shape=jax.ShapeDtypeStruct((nB, K, Nsh), F32),
        scratch_shapes=[pltpu.VMEM((tk, tn), F32)] if nm > 1 else [],
        compiler_params=_params(("parallel", "parallel", "arbitrary")),
    )(a, g, *after)


def _mm_tn_rows(name, a, g, n_rows, pieces):
    M, K = a.shape
    M2, N = g.shape
    assert M == M2
    tk, tn, tm = _pick(K, 1536, LANE), _pick(N, 512, LANE), _pick(M, CONTRACT_TILE, 16)
    nk, nm = K // tk, M // tm

    def body(a_ref, g_ref, o_ref, *acc):
        kk = pl.program_id(1)

        def finish(total):
            for t in range(nk):
                @pl.when(kk == t)
                def _(t=t):
                    for src, dst, n in pieces:
                        lo, hi = max(src, t * tk), min(src + n, (t + 1) * tk)
                        if lo < hi:
                            o_ref[dst + lo - src:dst + hi - src, :] = total[lo - t * tk:hi - t * tk, :]

        part = lax.dot_general(a_ref[...].astype(BF16), g_ref[...].astype(BF16),
                               (((0,), (0,)), ((), ())), preferred_element_type=F32)
        _accumulate(acc[0] if acc else None, part, pl.program_id(2), nm, finish)

    return pl.pallas_call(
        body, name=name, grid=(N // tn, nk, nm),
        in_specs=[pl.BlockSpec((tm, tk), lambda n, k, m: (m, k)),
                  pl.BlockSpec((tm, tn), lambda n, k, m: (m, n))],
        out_specs=pl.BlockSpec((n_rows, tn), lambda n, k, m: (0, n)),
        out_shape=jax.ShapeDtypeStruct((n_rows, N), F32),
        scratch_shapes=[pltpu.VMEM((tk, tn), F32)] if nm > 1 else [],
        compiler_params=_params(("arbitrary", "arbitrary", "arbitrary")),
    )(a, g)


def _attn_fwd(name, q, k, v, n_heads, group, dqk, dv, scale, tq, after=()):
    S = q.shape[0]
    hp = 2 * HEADS_PER_STEP
    kvp = max(1, hp // group)

    def body(q_ref, k_ref, v_ref, *rest):
        o_ref, lse_ref = rest[-2:]
        for j in range(hp):
            jk = j * kvp // hp
            s = lax.dot_general(q_ref[:, j * dqk:(j + 1) * dqk], k_ref[:, jk * dqk:(jk + 1) * dqk],
                                (((1,), (1,)), ((), ())), preferred_element_type=F32)
            m = jnp.max(s, axis=-1, keepdims=True)
            e = jnp.exp((s - m) * scale)
            l = jnp.sum(e, axis=-1, keepdims=True)
            o_ref[:, j * dv:(j + 1) * dv] = jnp.dot(e.astype(BF16), v_ref[:, jk * dv:(jk + 1) * dv],
                                                    preferred_element_type=F32) * (1.0 / l)
            lse_ref[:, j * LANE:(j + 1) * LANE] = jnp.broadcast_to(m * scale + jnp.log(l), (tq, LANE))

    kv_block = lambda h, i: (0, (h * hp // group) // kvp)
    return pl.pallas_call(
        body, name=name, grid=(n_heads // hp, S // tq),
        in_specs=[pl.BlockSpec((tq, hp * dqk), lambda h, i: (i, h)),
                  pl.BlockSpec((S, kvp * dqk), kv_block),
                  pl.BlockSpec((S, kvp * dv), kv_block)] + [pl.BlockSpec(memory_space=pl.ANY) for _ in after],
        out_specs=[pl.BlockSpec((tq, hp * dv), lambda h, i: (i, h)),
                   pl.BlockSpec((tq, hp * LANE), lambda h, i: (i, h))],
        out_shape=[jax.ShapeDtypeStruct((S, n_heads * dv), F32), jax.ShapeDtypeStruct((S, n_heads * LANE), F32)],
        compiler_params=_params(("arbitrary", "arbitrary")),
    )(q, k, v, *after)


def _attn_bwd(name, q, k, v, do, o, lse, n_heads, group, dqk, dv, scale, tq, after=()):
    S = q.shape[0]
    n_kv = n_heads // group
    hp = HEADS_PER_STEP
    kvp = max(1, hp // group)

    def body(q_ref, k_ref, v_ref, do_ref, o_ref, lse_ref, *rest):
        dq_ref, dk_ref, dv_ref = rest[len(after):]
        h, i = pl.program_id(0), pl.program_id(1)

        @pl.when(((h * hp) % group == 0) & (i == 0))
        def _():
            dk_ref[...] = jnp.zeros(dk_ref.shape, F32)
            dv_ref[...] = jnp.zeros(dv_ref.shape, F32)

        for j in range(hp):
            jk = j * kvp // hp
            qb, dob = q_ref[:, j * dqk:(j + 1) * dqk], do_ref[:, j * dv:(j + 1) * dv]
            kb, vb = k_ref[:, jk * dqk:(jk + 1) * dqk], v_ref[:, jk * dv:(jk + 1) * dv]
            s = lax.dot_general(qb, kb, (((1,), (1,)), ((), ())), preferred_element_type=F32)
            p = jnp.exp(s * scale - jnp.tile(lse_ref[:, j * LANE:(j + 1) * LANE], (1, S // LANE)))
            dp = lax.dot_general(dob, vb, (((1,), (1,)), ((), ())), preferred_element_type=F32)
            delta = jnp.sum(dob.astype(F32) * o_ref[:, j * dv:(j + 1) * dv], axis=-1, keepdims=True)
            ds = (p * ((dp - delta) * scale)).astype(BF16)
            dv_ref[:, jk * dv:(jk + 1) * dv] += lax.dot_general(p.astype(BF16), dob, (((0,), (0,)), ((), ())),
                                                                preferred_element_type=F32)
            dq_ref[:, j * dqk:(j + 1) * dqk] = jnp.dot(ds, kb, preferred_element_type=F32)
            dk_ref[:, jk * dqk:(jk + 1) * dqk] += lax.dot_general(ds, qb, (((0,), (0,)), ((), ())),
                                                                  preferred_element_type=F32)

    kv_block = lambda h, i: (0, (h * hp // group) // kvp)
    return pl.pallas_call(
        body, name=name, grid=(n_heads // hp, S // tq),
        in_specs=[pl.BlockSpec((tq, hp * dqk), lambda h, i: (i, h)),
                  pl.BlockSpec((S, kvp * dqk), kv_block),
                  pl.BlockSpec((S, kvp * dv), kv_block),
                  pl.BlockSpec((tq, hp * dv), lambda h, i: (i, h)),
                  pl.BlockSpec((tq, hp * dv), lambda h, i: (i, h)),
                  pl.BlockSpec((tq, hp * LANE), lambda h, i: (i, h))] + [pl.BlockSpec(memory_space=pl.ANY) for _ in after],
        out_specs=[pl.BlockSpec((tq, hp * dqk), lambda h, i: (i, h)),
                   pl.BlockSpec((S, kvp * dqk), kv_block),
                   pl.BlockSpec((S, kvp * dv), kv_block)],
        out_shape=[jax.ShapeDtypeStruct((S, n_heads * dqk), F32),
                   jax.ShapeDtypeStruct((S, n_kv * dqk), F32),
                   jax.ShapeDtypeStruct((S, n_kv * dv), F32)],
        compiler_params=_params(("arbitrary", "arbitrary")),
    )(q, k, v, do, o, lse, *after)


def _adamw(name, w, g, m, v, after=(), rewrite_g=False):
    R, C = w.shape
    tile = _pick(R, max(8, (1 << 19) // C // 8 * 8), 8)
    c1 = 1.0 - ADAM_B1 ** ADAM_STEP
    c2 = 1.0 - ADAM_B2 ** ADAM_STEP
    n_out = 4 if rewrite_g else 3

    def body(w_ref, g_ref, m_ref, v_ref, *rest):
        d_ref, nm_ref, nv_ref = rest[len(after):len(after) + 3]
        gg = g_ref[...]
        nm = ADAM_B1 * m_ref[...] + (1.0 - ADAM_B1) * gg
        nv = ADAM_B2 * v_ref[...] + (1.0 - ADAM_B2) * (gg * gg)
        d_ref[...] = -ADAM_LR * ((nm / c1) / (jnp.sqrt(nv / c2) + ADAM_EPS) + ADAM_WD * w_ref[...])
        nm_ref[...] = nm
        nv_ref[...] = nv
        if rewrite_g:
            rest[-1][...] = gg

    spec = pl.BlockSpec((tile, C), lambda i: (i, 0))
    return pl.pallas_call(
        body, name=name, grid=(R // tile,), in_specs=[spec] * 4 + [pl.BlockSpec(memory_space=pl.ANY) for _ in after],
        out_specs=[spec] * n_out, out_shape=[jax.ShapeDtypeStruct((R, C), F32)] * n_out,
        compiler_params=_params(("parallel",)),
    )(w, g, m, v, *after)


def _flip(me, rel):
    return tuple(1 - p if r else p for p, r in zip(me, rel))


def _push(name, ins, out_shapes, transfers, aliases=None):
    ni, no, nt = len(ins), len(out_shapes), len(transfers)

    def body(*refs):
        in_refs, out_refs = refs[:ni], refs[ni:ni + no]
        send_sems, recv_sems = refs[ni + no], refs[ni + no + 1]
        me = (lax.axis_index("x"), lax.axis_index("y"), lax.axis_index("c"))
        copies = []
        for t, (si, sfn, oi, dfn, rel) in enumerate(transfers):
            src_ref = out_refs[si[1]] if isinstance(si, tuple) else in_refs[si]
            src = src_ref.at[sfn(*me)]
            dst = out_refs[oi].at[dfn(*me)]
            if rel is None:
                cp = pltpu.make_async_copy(src, dst, send_sems.at[t])
            else:
                cp = pltpu.make_async_remote_copy(src_ref=src, dst_ref=dst, send_sem=send_sems.at[t],
                                                  recv_sem=recv_sems.at[t], device_id=_flip(me, rel),
                                                  device_id_type=MESH)
            cp.start()
            copies.append(cp)
        for cp in copies:
            cp.wait()

    any_spec = pl.BlockSpec(memory_space=pl.ANY)
    return pl.pallas_call(
        body, name=name, in_specs=[any_spec] * ni, out_specs=[any_spec] * no, out_shape=out_shapes,
        scratch_shapes=[pltpu.SemaphoreType.DMA((nt,)), pltpu.SemaphoreType.DMA((nt,))],
        input_output_aliases=aliases or {},
        compiler_params=pltpu.CompilerParams(has_side_effects=True),
    )(*ins)


_HBM = pl.BlockSpec(memory_space=pltpu.HBM)
_SEM = pl.BlockSpec(memory_space=pltpu.SEMAPHORE)
_DATAFLOW = pltpu.SideEffectType.DATAFLOW_SIDE_EFFECTING


def _split_copies(refs, transfers, send_sems, recv_sems):
    me = (lax.axis_index("x"), lax.axis_index("y"), lax.axis_index("c"))
    return [pltpu.make_async_remote_copy(src_ref=refs[sb].at[sfn(*me)], dst_ref=refs[db].at[dfn(*me)],
                                         send_sem=send_sems.at[t], recv_sem=recv_sems.at[t],
                                         device_id=_flip(me, rel), device_id_type=MESH)
            for t, (sb, sfn, db, dfn, rel) in enumerate(transfers)]


def _push_start(name, bufs, transfers, after=()):
    nb, na, nt = len(bufs), len(after), len(transfers)

    def body(*refs):
        send_sems, recv_sems, token = refs[nb + na], refs[nb + na + 1], refs[-1]
        for cp in _split_copies(refs[nb + na + 2:2 * nb + na + 2], transfers, send_sems, recv_sems):
            cp.start()
        token[...] = jnp.zeros(token.shape, token.dtype)

    res = pl.pallas_call(
        body, name=name,
        out_shape=(pltpu.SemaphoreType.DMA((nt,)), pltpu.SemaphoreType.DMA((nt,)),
                   *[pltpu.HBM(b.shape, b.dtype) for b in bufs], jax.ShapeDtypeStruct((8, LANE), F32)),
        in_specs=[_HBM] * nb + [pl.BlockSpec(memory_space=pl.ANY)] * na,
        out_specs=(_SEM, _SEM, *[_HBM] * nb, pl.BlockSpec(memory_space=pltpu.VMEM)),
        input_output_aliases={i: i + 2 for i in range(nb)},
        compiler_params=pltpu.CompilerParams(has_side_effects=_DATAFLOW),
    )(*[pltpu.with_memory_space_constraint(b, pltpu.HBM) for b in bufs], *after)
    return res[0], res[1], list(res[2:2 + nb]), res[-1]


def _push_wait(name, send_sems, recv_sems, bufs, transfers, after):
    nb = len(bufs)

    def body(*refs):
        for cp in _split_copies(refs[:nb], transfers, refs[nb], refs[nb + 1]):
            cp.wait_send()
            cp.wait_recv()

    res = pl.pallas_call(
        body, name=name, out_shape=[pltpu.HBM(b.shape, b.dtype) for b in bufs],
        in_specs=[_HBM] * nb + [_SEM, _SEM, pl.BlockSpec(memory_space=pl.ANY)], out_specs=[_HBM] * nb,
        input_output_aliases={i: i for i in range(nb)},
        compiler_params=pltpu.CompilerParams(has_side_effects=_DATAFLOW),
    )(*bufs, send_sems, recv_sems, after)
    return list(res)


ICI_RELS = ((1, 0, 0), (0, 1, 0), (1, 1, 0))
SIBLING = (0, 0, 1)


def _chip(x, y):
    return 2 * x + y


def _cast_into_slot(name, where, w, after=()):
    R, C = w.shape
    tile = _pick(R, 512, BF16_ROWS)

    def body(where_ref, w_ref, *rest):
        rest[-1][...] = w_ref[...].astype(BF16)

    return pl.pallas_call(
        body, name=name,
        grid_spec=pltpu.PrefetchScalarGridSpec(
            num_scalar_prefetch=1, grid=(R // tile,),
            in_specs=[pl.BlockSpec((tile, C), lambda i, wh: (i, 0))] + [pl.BlockSpec(memory_space=pl.ANY) for _ in after],
            out_specs=pl.BlockSpec((None, tile, C), lambda i, wh: (wh[1], i, 0))),
        out_shape=jax.ShapeDtypeStruct((N_CHIPS, R, C), BF16),
        compiler_params=_params(("parallel",)),
    )(where, w, *after)


BF16_ROWS = 16


def _rows_split(shape2):
    return (shape2[0] // 2) % BF16_ROWS == 0


def _half_shape(shape2):
    R, C = shape2
    return (R // 2, C) if _rows_split(shape2) else (R, C // 2)


def _half(shape2, c):
    R, C = shape2
    if _rows_split(shape2):
        return (pl.ds(c * (R // 2), R // 2), slice(None))
    return (slice(None), pl.ds(pl.multiple_of(c * (C // 2), LANE), C // 2))


def _gather_casts(tag, where, shards, after=()):
    return [_cast_into_slot(f"cast_w_{tag}{w}", where, s, after) for w, s in enumerate(shards)]


def _gather_start(tag, own, after):
    transfers = []
    for w in range(len(own)):
        def idx(x, y, c, shape2=own[w].shape[1:]):
            return (_chip(x, y), *_half(shape2, c))
        transfers += [(w, idx, w, idx, rel) for rel in ICI_RELS]
    send_sems, recv_sems, bufs, token = _push_start(f"gather_{tag}_start", own, transfers, after)
    return (tag, send_sems, recv_sems, bufs, transfers), token


def _forward_transfers(part):
    second = []
    for w, p in enumerate(part):
        for rel in ICI_RELS:
            def idx(x, y, c, shape2=p.shape[1:], rel=rel):
                return (_chip(x ^ rel[0], y ^ rel[1]), *_half(shape2, c))
            second.append((w, idx, w, idx, SIBLING))
    return second


def _gather_finish(handle, after):
    tag, send_sems, recv_sems, bufs, transfers = handle
    part = _push_wait(f"gather_{tag}_wait", send_sems, recv_sems, bufs, transfers, after)
    outs = [jax.ShapeDtypeStruct(p.shape, p.dtype) for p in part]
    second = [(("out", sb), sfn, db, dfn, rel) for sb, sfn, db, dfn, rel in _forward_transfers(part)]
    return _push(f"gather_{tag}_d2d", list(part), outs, second, aliases={w: w for w in range(len(part))})


def _gather_mid(handle, after):
    tag, send_sems, recv_sems, bufs, transfers = handle
    part = _push_wait(f"gather_{tag}_wait", send_sems, recv_sems, bufs, transfers, after)
    second = _forward_transfers(part)
    s2, r2, bufs2, token = _push_start(f"gather_{tag}_d2d_start", part, second)
    return (tag, s2, r2, bufs2, second), token


def _gather_end(handle, after):
    tag, send_sems, recv_sems, bufs, transfers = handle
    return _push_wait(f"gather_{tag}_d2d_wait", send_sems, recv_sems, bufs, transfers, after)


def _reduce_begin(tag, grads, after=()):
    n = len(grads)
    land = [lax.empty((N_CHIPS, *_half_shape(g.shape[1:])), F32) for g in grads]
    t1 = []
    for w in range(n):
        t1.append((w, lambda x, y, c, shape2=grads[w].shape[1:]: (slice(None), *_half(shape2, 1 - c)), n + w,
                   lambda x, y, c: (slice(None), slice(None), slice(None)), SIBLING))
    send_sems, recv_sems, bufs, token = _push_start(f"reduce_{tag}_d2d_start", list(grads) + land, t1, after)
    return (tag, send_sems, recv_sems, bufs, t1), token


def _reduce_start(handle, where, after, after_start=()):
    tag, send_sems, recv_sems, bufs, t1 = handle
    n = len(bufs) // 2
    res = _push_wait(f"reduce_{tag}_d2d_wait", send_sems, recv_sems, bufs, t1, after)
    grads, from_sib = res[:n], res[n:]
    pair = [_pair_sum(f"pair_sum_{tag}{w}", where, grads[w], from_sib[w]) for w in range(n)]
    land = [lax.empty((3, *_half_shape(g.shape[1:])), BF16) for g in grads]
    t3 = []
    for w in range(n):
        for r, rel in enumerate(ICI_RELS):
            t3.append((w, lambda x, y, c, rel=rel: (_chip(x ^ rel[0], y ^ rel[1]), slice(None), slice(None)), n + w,
                       lambda x, y, c, r=r: (r, slice(None), slice(None)), rel))
    send_sems, recv_sems, bufs, token = _push_start(f"reduce_{tag}_start", pair + land, t3, after_start)
    return (tag, send_sems, recv_sems, bufs, t3, list(grads), list(from_sib)), token


def _reduce_finish(handle, where, after):
    tag, send_sems, recv_sems, bufs, t3, grads, from_sib = handle
    n = len(grads)
    from_chips = _push_wait(f"reduce_{tag}_wait", send_sems, recv_sems, bufs, t3, after)[n:]
    mine = [_chip_sum(f"chip_sum_{tag}{w}", where, grads[w], from_sib[w], from_chips[w]) for w in range(n)]
    t5 = []
    for w in range(n):
        def idx(x, y, c, shape2=grads[w].shape[1:]):
            return _half(shape2, c)
        t5.append((w, idx, w, idx, SIBLING))
    send_sems, recv_sems, bufs, token = _push_start(f"reduce_{tag}_swap_start", mine, t5)
    return (tag, send_sems, recv_sems, bufs, t5), token


def _reduce_end(handle, after):
    tag, send_sems, recv_sems, bufs, t5 = handle
    return _push_wait(f"reduce_{tag}_swap_wait", send_sems, recv_sems, bufs, t5, after)


def _pair_sum(name, where, g, from_sib):
    _, hr, hc = from_sib.shape
    tile = _pick(hr, max(16, (1 << 19) // hc // 16 * 16), 16)
    nb = hr // tile
    slot = lambda k, wh: (wh[1] + 1 + k) % N_CHIPS
    if _rows_split(g.shape[1:]):
        mine = lambda k, i, wh: (slot(k, wh), wh[0] * nb + i, 0)
    else:
        mine = lambda k, i, wh: (slot(k, wh), i, wh[0])

    def body(where_ref, g_ref, s_ref, o_ref):
        o_ref[...] = (g_ref[...] + s_ref[...]).astype(BF16)

    return pl.pallas_call(
        body, name=name,
        grid_spec=pltpu.PrefetchScalarGridSpec(
            num_scalar_prefetch=1, grid=(N_CHIPS - 1, nb),
            in_specs=[pl.BlockSpec((None, tile, hc), mine),
                      pl.BlockSpec((None, tile, hc), lambda k, i, wh: (slot(k, wh), i, 0))],
            out_specs=pl.BlockSpec((None, tile, hc), lambda k, i, wh: (slot(k, wh), i, 0))),
        out_shape=jax.ShapeDtypeStruct((N_CHIPS, hr, hc), BF16),
        compiler_params=_params(("parallel", "parallel")),
    )(where, g, from_sib)


def _chip_sum(name, where, g, from_sib, from_chips):
    _, R, C = g.shape
    _, hr, hc = from_sib.shape
    tile = _pick(hr, max(16, (1 << 19) // hc // 16 * 16), 16)
    nb = hr // tile
    if _rows_split((R, C)):
        mine = lambda i, wh: (wh[0] * nb + i, 0)
    else:
        mine = lambda i, wh: (i, wh[0])

    def body(where_ref, g_ref, s_ref, r_ref, o_ref):
        acc = g_ref[...] + s_ref[...]
        for r in range(3):
            acc = acc + r_ref[r].astype(F32)
        o_ref[...] = acc

    return pl.pallas_call(
        body, name=name,
        grid_spec=pltpu.PrefetchScalarGridSpec(
            num_scalar_prefetch=1, grid=(nb,),
            in_specs=[pl.BlockSpec((None, tile, hc), lambda i, wh: (wh[1], *mine(i, wh))),
                      pl.BlockSpec((None, tile, hc), lambda i, wh: (wh[1], i, 0)),
                      pl.BlockSpec((3, tile, hc), lambda i, wh: (0, i, 0))],
            out_specs=pl.BlockSpec((tile, hc), mine)),
        out_shape=jax.ShapeDtypeStruct((R, C), F32),
        compiler_params=_params(("parallel",)),
    )(where, g, from_sib, from_chips)


def _allgather8(name, blk, after=()):
    m_per, n = blk.shape
    rels = [(0, 0, 1), (1, 0, 0), (0, 1, 0), (1, 1, 0), (1, 0, 1), (0, 1, 1), (1, 1, 1)]

    def body(x_ref, *rest):
        out_ref, send_sems, recv_sems = rest[len(after):]
        me = (lax.axis_index("x"), lax.axis_index("y"), lax.axis_index("c"))
        my_rows = out_ref.at[pl.ds((4 * me[0] + 2 * me[1] + me[2]) * m_per, m_per), :]
        out_ref[pl.ds((4 * me[0] + 2 * me[1] + me[2]) * m_per, m_per), :] = x_ref[...]
        copies = []
        for t, rel in enumerate(rels):
            cp = pltpu.make_async_remote_copy(src_ref=x_ref, dst_ref=my_rows, send_sem=send_sems.at[t],
                                              recv_sem=recv_sems.at[t], device_id=_flip(me, rel),
                                              device_id_type=MESH)
            cp.start()
            copies.append(cp)
        for cp in copies:
            cp.wait()

    return pl.pallas_call(
        body, name=name, out_shape=jax.ShapeDtypeStruct((N_DEV * m_per, n), blk.dtype),
        in_specs=[pl.BlockSpec(memory_space=pltpu.VMEM)] + [pl.BlockSpec(memory_space=pl.ANY) for _ in after],
        out_specs=pl.BlockSpec(memory_space=pltpu.VMEM),
        scratch_shapes=[pltpu.SemaphoreType.DMA((7,)), pltpu.SemaphoreType.DMA((7,))],
        compiler_params=pltpu.CompilerParams(has_side_effects=True, vmem_limit_bytes=VMEM_LIMIT),
    )(blk, *after)


def _rope_tables(seq_len, dim, reps):
    rows = seq_len // GRID_W
    t = np.arange(seq_len)
    row, col = (t // GRID_W).astype(np.float32), (t % GRID_W).astype(np.float32)
    half = dim // 2
    inv = jnp.asarray(ROPE_THETA, F32) ** (-jnp.arange(0, half, 2, dtype=F32) / half)
    ang_r = jnp.asarray(row)[:, None] * inv[None, :]
    ang_c = jnp.asarray(col)[:, None] * inv[None, :]
    ang = jnp.concatenate([ang_r, ang_r, ang_c, ang_c], axis=-1)
    cos, sin = jnp.cos(ang), jnp.sin(ang)
    low = (np.arange(dim) % (dim // 2)) < (dim // 4)
    s_lo = jnp.where(jnp.asarray(low)[None, :], -sin, 0.0)
    s_hi = jnp.where(jnp.asarray(low)[None, :], 0.0, sin)
    del rows
    return tuple(jnp.tile(a, (1, reps)) for a in (cos, s_lo, s_hi))


def _make_rope(shift):
    def up(v):
        return pltpu.roll(v, LANE - shift, 1)

    def down(v):
        return pltpu.roll(v, shift, 1)

    @jax.custom_vjp
    def rope(v, cos, s_lo, s_hi):
        return v * cos + up(v) * s_lo + down(v) * s_hi

    def fwd(v, cos, s_lo, s_hi):
        return rope(v, cos, s_lo, s_hi), (cos, s_lo, s_hi)

    def bwd(res, dy):
        cos, s_lo, s_hi = res
        return dy * cos + down(dy * s_lo) + up(dy * s_hi), jnp.zeros_like(cos), jnp.zeros_like(cos), jnp.zeros_like(cos)

    rope.defvjp(fwd, bwd)
    return rope


_rope_a = _make_rope(HEAD_DIM // 4)
_rope_b = _make_rope(QK_ROPE // 4)


def _prep(segs, tabs, g_q, g_k, g_ckv):
    qa, ka, va, qn, qp, ckv, kpe = segs
    ta, tb = tabs
    q_a = jnp.concatenate([_rope_a(_rn(s) * g_q, *ta) for s in qa], axis=1)
    k_a = jnp.concatenate([_rope_a(_rn(s) * g_k, *ta) for s in ka], axis=1)
    cat = []
    for h in range(HB):
        cat += [qn[h], _rope_b(qp[h], *tb)]
    q_cat = jnp.concatenate(cat, axis=1)
    return q_a, k_a, va, q_cat, _rn(ckv) * g_ckv, _rope_b(kpe, *tb)


def _split_proj(proj):
    qa = [proj[:, O_QA + h * LANE:O_QA + (h + 1) * LANE] for h in range(HA)]
    ka = [proj[:, O_KA + h * LANE:O_KA + (h + 1) * LANE] for h in range(HKV)]
    va = proj[:, O_VA:O_QN]
    qn = [proj[:, O_QN + h * LANE:O_QN + (h + 1) * LANE] for h in range(HB)]
    qp = [proj[:, O_QP + h * LANE:O_QP + (h + 1) * LANE] for h in range(HB)]
    return qa, ka, va, qn, qp, proj[:, O_CKV:O_KPE], proj[:, O_KPE:W_INP]


def _permute_w_in(nat):
    K = nat.shape[1]
    qb = nat[W_QA + 2 * W_KA:W_QA + 2 * W_KA + W_QB].reshape(HB, QK_B, K)
    qp = jnp.concatenate([qb[:, QK_NOPE:], jnp.zeros((HB, LANE - QK_ROPE, K), nat.dtype)], axis=1)
    tail = nat[W_QA + 2 * W_KA + W_QB:]
    return jnp.concatenate([nat[:O_QN], qb[:, :QK_NOPE].reshape(HB * QK_NOPE, K), qp.reshape(HB * LANE, K),
                            tail, jnp.zeros((LANE - QK_ROPE, K), nat.dtype)], axis=0)


def _unpermute_pieces():
    pieces = [(0, 0, O_QN)]
    for h in range(HB):
        pieces.append((O_QN + h * QK_NOPE, O_QN + h * QK_B, QK_NOPE))
        pieces.append((O_QP + h * LANE, O_QN + h * QK_B + QK_NOPE, QK_ROPE))
    pieces.append((O_CKV, O_QN + W_QB, KV_RANK))
    pieces.append((O_KPE, O_QN + W_QB + KV_RANK, QK_ROPE))
    return pieces


def kernel(x, c, w_ada, b_ada, g_pre_attn, w_in, g_q_a, g_k_a, g_ckv, w_kv_b, g_out_a, g_out_b, w_out, g_post_attn, g_pre_mlp, w_mlp_in, w_mlp_out, g_post_mlp, loss_target, m_w_ada, m_b_ada, m_g_pre_attn, m_w_in, m_g_q_a, m_g_k_a, m_g_ckv, m_w_kv_b, m_g_out_a, m_g_out_b, m_w_out, m_g_post_attn, m_g_pre_mlp, m_w_mlp_in, m_w_mlp_out, m_g_post_mlp, v_w_ada, v_b_ada, v_g_pre_attn, v_w_in, v_g_q_a, v_g_k_a, v_g_ckv, v_w_kv_b, v_g_out_a, v_g_out_b, v_w_out, v_g_post_attn, v_g_pre_mlp, v_w_mlp_in, v_w_mlp_out, v_g_post_mlp):
    S, D = x.shape[1], x.shape[2]
    x2d, tgt = x[0], loss_target[0]
    ix, iy, ic = lax.axis_index("x"), lax.axis_index("y"), lax.axis_index("c")
    j_me = _chip(ix, iy)
    e_me = 4 * ix + 2 * iy + ic
    T = _pick(S, 256, 8)
    TB = _pick(S, 128, 8)

    where = jnp.stack([ic, j_me]).astype(jnp.int32)
    DFF = w_mlp_out.shape[1] * N_CHIPS

    c_all = _allgather8("gather_c", c.reshape(8, D // 8)).reshape(N_DEV, D)
    w_in_t = jnp.swapaxes(w_in[0], 0, 1)
    h_qkv, tok_a = _gather_start("qkv", _gather_casts("qkv", where, [w_in_t, w_kv_b[0]]), (c_all,))
    later = [_gather_casts(tag, where, [w[0]], (tok_a,))
             for tag, w in (("out", w_out), ("up", w_mlp_in), ("down", w_mlp_out))]

    def silu_fn(cc):
        return (cc * (1.0 / (1.0 + jnp.exp(-cc))),), ()

    (c_act,), _ = _rowwise("silu_c", silu_fn, [c_all], [], [(D, BF16)], [], N_DEV)
    ca_pad = jnp.concatenate([c_act, jnp.zeros_like(c_act)], axis=0)
    n_mod = w_ada.shape[2]
    b_cols = lax.dynamic_slice(b_ada, (0, j_me * n_mod), (1, n_mod))
    (mod_part,) = _mm_nn("mod_mm", ca_pad, w_ada, [F32], epilogue=lambda acc, b: (acc + b,),
                         extras=(jnp.broadcast_to(b_cols, (16, n_mod)),))
    mod_all = _allgather8("gather_mod", mod_part[:8] + tok_a[0, 0],
                          after=[c_[0] for c_ in later]).reshape(N_DEV, 8, n_mod)
    mod = jnp.concatenate([lax.dynamic_slice(mod_all, (2 * j, e_me, 0), (1, 1, n_mod))[0] for j in range(N_CHIPS)],
                          axis=1)

    h_out, tok_b = _gather_start("out", later[0], (mod_all,))
    h_up, tok_c = _gather_start("up", later[1], (tok_b,))
    h_down, tok_d = _gather_start("down", later[2], (tok_c,))
    mod = mod + tok_d[0, 0]
    sh_a, sc_a, gt_a, sh_m, sc_m, gt_m = [mod[:, i * D:(i + 1) * D] for i in range(6)]

    tabs_a = _rope_tables(S, HEAD_DIM, 1)
    tabs_b = _rope_tables(S, QK_ROPE, LANE // QK_ROPE)

    def pre_attn_fn(xt, g, sc, sh):
        return (_rn(xt) * g * (1.0 + sc) + sh,), ()

    (h_b,), _ = _rowwise("pre_attn", pre_attn_fn, [x2d], [g_pre_attn, sc_a, sh_a], [(D, BF16)], [], T)
    g_in, g_kvb = _gather_finish(h_qkv, h_b)
    wpt = _permute_w_in(g_in.reshape(W_IN, D))[None]
    (proj,) = _mm_nt("proj_mm", h_b, wpt, [F32])

    def prep_fn(pt, ca_, sla, sha, cb_, slb, shb, gq, gk, gc):
        return _prep(_split_proj(pt), ((ca_, sla, sha), (cb_, slb, shb)), gq, gk, gc), ()

    (qa_b, ka_b, va_b, qcat_b, ckvn_b, kpe_b), _ = _rowwise(
        "prep", prep_fn, [proj, *tabs_a, *tabs_b], [g_q_a, g_k_a, g_ckv],
        [(W_QA, BF16), (W_KA, BF16), (W_KA, BF16), (2 * HB * LANE, BF16), (KV_RANK, BF16), (LANE, BF16)], [], TB)
    (kv,) = _mm_nn("kv_mm", ckvn_b, g_kvb, [F32])

    def kcat_fn(kvt, kpet):
        kc, vs = [], []
        for h in range(HB):
            kc += [kvt[:, 2 * h * LANE:(2 * h + 1) * LANE], kpet.astype(F32)]
            vs.append(kvt[:, (2 * h + 1) * LANE:(2 * h + 2) * LANE])
        return (jnp.concatenate(kc, axis=1), jnp.concatenate(vs, axis=1)), ()

    (kcat_b, vb_b), _ = _rowwise("kcat", kcat_fn, [kv, kpe_b], [], [(2 * HB * LANE, BF16), (HB * V_DIM, BF16)], [], T)

    TQ = _pick(S, 512, 16)
    sc_a_ = 1.0 / math.sqrt(HEAD_DIM)
    sc_b_ = 1.0 / math.sqrt(QK_B)
    o_a, lse_a = _attn_fwd("attn_a_fwd", qa_b, ka_b, va_b, HA, HA // HKV, HEAD_DIM, HEAD_DIM, sc_a_, TQ)
    m_out, tok_m = _gather_mid(h_out, o_a)
    o_b, lse_b = _attn_fwd("attn_b_fwd", qcat_b, kcat_b, vb_b, HB, 1, 2 * LANE, V_DIM, sc_b_, TQ, after=(tok_m,))
    m_up, tok_m = _gather_mid(h_up, o_b)

    def mix_fn(oa, ob, ga, gb):
        return (jnp.concatenate([_rn(oa) * ga, _rn(ob) * gb], axis=1),), ()

    (on_b,), _ = _rowwise("mix_norm", mix_fn, [o_a, o_b], [g_out_a, g_out_b], [(2 * W_QA, BF16)], [], T,
                          after=(tok_m,))
    (g_wout,) = _gather_end(m_out, on_b)
    w_out_f = g_wout.reshape(1, N_CHIPS * g_wout.shape[1], D)
    (o2,) = _mm_nn("out_mm", on_b, w_out_f, [F32])

    def mid(xt, o2t, gta, gpa, gpm, scm, shm):
        x1 = xt + gta * (_rn(o2t) * gpa)
        return x1, _rn(x1) * gpm * (1.0 + scm) + shm

    def mid_fn(*a):
        return mid(*a), ()

    mid_vecs = [gt_a, g_post_attn, g_pre_mlp, sc_m, sh_m]
    (x1, h2_b), _ = _rowwise("mid", mid_fn, [x2d, o2], mid_vecs, [(D, F32), (D, BF16)], [], T)
    (g_mlp_in,) = _gather_end(m_up, h2_b)
    relu_b, a_b = _mm_nn("mlp_in_mm", h2_b, g_mlp_in, [BF16, BF16],
                         epilogue=lambda acc: (jnp.maximum(acc, 0.0), jnp.square(jnp.maximum(acc, 0.0))))
    (g_mlp_out,) = _gather_finish(h_down, a_b)
    w_mlp_out_f = g_mlp_out.reshape(1, DFF, D)
    (y,) = _mm_nn("mlp_out_mm", a_b, w_mlp_out_f, [F32])

    def last_fn(yt, x1t, tt, gtm, gpo):
        n, r = _rn_parts(yt)
        err = x1t + gtm * (n * gpo) - tt
        dx2 = err * (1.0 / D)
        loss = 0.5 * jnp.sum(jnp.mean(err * err, axis=-1))
        s = _rows(dx2 * n)
        return (_rn_bwd(n, r, dx2 * (gtm * gpo)), dx2), (jnp.full((1, LANE), loss, F32), gpo * s, gtm * s)

    (dy_b, dx2), (loss_v, d_gt_m, d_g_post_mlp) = _rowwise(
        "loss_bwd", last_fn, [y, x1, tgt], [gt_m, g_post_mlp], [(D, BF16), (D, F32)], [LANE, D, D], TB)

    gw_mlp_out = _mm_tn("gw_mlp_out_mm", a_b, dy_b, 1).reshape(N_CHIPS, DFF // N_CHIPS, D)
    b_down, tok = _reduce_begin("down", [gw_mlp_out])
    (du_b,) = _mm_nt("d_mlp_out_mm", dy_b, w_mlp_out_f, [BF16],
                     epilogue=lambda acc, rt: (acc * (2.0 * rt.astype(F32)),), extras=(relu_b,), after=(tok,))
    r_down, tok = _reduce_start(b_down, where, du_b)
    gw_mlp_in = _mm_tn("gw_mlp_in_mm", h2_b, du_b, N_CHIPS, after=(tok,))
    b_up, tok = _reduce_begin("up", [gw_mlp_in])
    (dh2,) = _mm_nt("d_mlp_in_mm", du_b, g_mlp_in, [F32], after=(tok,))
    r_up, tok_up = _reduce_start(b_up, where, dh2)

    def mid_bwd_fn(xt, o2t, dh2t, dx2t, gta, gpa, gpm, scm, shm):
        n2, r2 = _rn_parts(o2t)
        n1, r1 = _rn_parts(xt + gta * (n2 * gpa))
        dx1 = dx2t + _rn_bwd(n1, r1, dh2t * (gpm * (1.0 + scm)))
        s1, s2 = _rows(dh2t * n1), _rows(dx1 * n2)
        do2 = _rn_bwd(n2, r2, dx1 * (gta * gpa))
        return (dx1, do2), (gpa * s2, gta * s2, (1.0 + scm) * s1, gpm * s1, _rows(dh2t))

    (dx1, do2_b), (d_gt_a, d_g_post_attn, d_g_pre_mlp, d_sc_m, d_sh_m) = _rowwise(
        "mid_bwd", mid_bwd_fn, [x2d, o2, dh2, dx2], mid_vecs, [(D, F32), (D, BF16)], [D] * 5, TB, after=(tok_up,))

    (d_on,) = _mm_nt("d_out_mm", do2_b, w_out_f, [F32])
    gw_out = _mm_tn("gw_out_mm", on_b, do2_b, 1).reshape(N_CHIPS, w_out.shape[1], D)

    def mix_bwd_fn(oa, ob, dont, ga, gb):
        na, ra = _rn_parts(oa)
        nb, rb = _rn_parts(ob)
        da, db = dont[:, :W_QA], dont[:, W_QA:]
        return (_rn_bwd(na, ra, da * ga), _rn_bwd(nb, rb, db * gb)), (_rows(da * na), _rows(db * nb))

    (doa_b, dob_b), (d_g_out_a, d_g_out_b) = _rowwise(
        "mix_bwd", mix_bwd_fn, [o_a, o_b, d_on], [g_out_a, g_out_b], [(W_QA, BF16), (HB * V_DIM, BF16)],
        [W_QA, HB * V_DIM], T)

    dqa, dka, dva = _attn_bwd("attn_a_bwd", qa_b, ka_b, va_b, doa_b, o_a, lse_a, HA, HA // HKV, HEAD_DIM, HEAD_DIM,
                              sc_a_, TQ)
    dqcat, dkcat, dvb = _attn_bwd("attn_b_bwd", qcat_b, kcat_b, vb_b, dob_b, o_b, lse_b, HB, 1, 2 * LANE, V_DIM,
                                  sc_b_, TQ)

    def kcat_bwd_fn(dkc, dvt):
        dkv, dkpe = [], None
        for h in range(HB):
            dkv += [dkc[:, 2 * h * LANE:(2 * h + 1) * LANE], dvt[:, h * V_DIM:(h + 1) * V_DIM]]
            piece = dkc[:, (2 * h + 1) * LANE:(2 * h + 2) * LANE]
            dkpe = piece if dkpe is None else dkpe + piece
        return (jnp.concatenate(dkv, axis=1), dkpe), ()

    (dkv_b, dkpe), _ = _rowwise("kcat_bwd", kcat_bwd_fn, [dkcat, dvb], [], [(2 * HB * LANE, BF16), (LANE, F32)], [], T)
    (d_ckvn,) = _mm_nt("d_kv_mm", dkv_b, g_kvb, [F32])
    gw_kvb = _mm_tn("gw_kv_mm", ckvn_b, dkv_b, N_CHIPS)
    b_kvb, tok_kvb = _reduce_begin("okv", [gw_out, gw_kvb])

    def prep_bwd_fn(pt, dqat, dkat, dvat, dqct, dckt, dkpt, ca_, sla, sha, cb_, slb, shb, gq, gk, gc):
        tabs = ((ca_, sla, sha), (cb_, slb, shb))
        _, vjp = jax.vjp(lambda sg, a1, a2, a3: _prep(sg, tabs, a1, a2, a3), _split_proj(pt), gq, gk, gc)
        (dqa_s, dka_s, dva_s, dqn_s, dqp_s, dck_s, dkp_s), dgq, dgk, dgc = vjp((dqat, dkat, dvat, dqct, dckt, dkpt))
        dproj = jnp.concatenate([*dqa_s, *dka_s, dva_s, *dqn_s, *dqp_s, dck_s, dkp_s], axis=1)
        return (dproj,), (dgq, dgk, dgc)

    (dproj_b,), (d_g_q_a, d_g_k_a, d_g_ckv) = _rowwise(
        "prep_bwd", prep_bwd_fn, [proj, dqa, dka, dva, dqcat, d_ckvn, dkpe, *tabs_a, *tabs_b],
        [g_q_a, g_k_a, g_ckv], [(W_INP, BF16)], [HEAD_DIM, HEAD_DIM, KV_RANK], TB, after=(tok_kvb,))

    r_kvb, tok = _reduce_start(b_kvb, where, dproj_b)
    (dh,) = _mm_nn("d_proj_mm", dproj_b, wpt, [F32], after=(tok,))
    gw_in = _mm_tn_rows("gw_in_mm", dproj_b, h_b, W_IN, _unpermute_pieces()).reshape(N_CHIPS, W_IN // N_CHIPS, D)
    b_in, tok = _reduce_begin("in", [gw_in])

    def pre_attn_bwd_fn(xt, dht, dx1t, g, sc, sh):
        n, r = _rn_parts(xt)
        s = _rows(dht * n)
        return (dx1t + _rn_bwd(n, r, dht * (g * (1.0 + sc))),), ((1.0 + sc) * s, g * s, _rows(dht))

    (grad_x,), (d_g_pre_attn, d_sc_a, d_sh_a) = _rowwise(
        "pre_attn_bwd", pre_attn_bwd_fn, [x2d, dh, dx1], [g_pre_attn, sc_a, sh_a], [(D, F32)], [D] * 3, TB,
        after=(tok,))

    small_names = ["b_ada", "g_pre_attn", "g_q_a", "g_k_a", "g_ckv", "g_out_a", "g_out_b", "g_post_attn",
                   "g_pre_mlp", "g_post_mlp"]
    small_local = jnp.concatenate([d_sh_a, d_sc_a, d_gt_a, d_sh_m, d_sc_m, d_gt_m, d_g_pre_attn, d_g_q_a, d_g_k_a,
                                   d_g_ckv, d_g_out_a, d_g_out_b, d_g_post_attn, d_g_pre_mlp, d_g_post_mlp,
                                   loss_v[:, :8]], axis=1)
    n_sent = small_local.shape[1]
    n_small = n_sent - 8
    small_all = _allgather8("gather_small", small_local.reshape(8, n_sent // 8)).reshape(N_DEV, n_sent)

    def sum8_fn(t):
        return (), (jnp.sum(t, axis=0, keepdims=True),)

    r_in, tok_in = _reduce_start(b_in, where, grad_x, (small_all,))
    _, (sent_sum,) = _rowwise("sum_small", sum8_fn, [small_all], [], [], [n_sent], N_DEV, after=(tok_in,))
    small_sum, loss = sent_sum[:, :n_small], sent_sum[0, n_small]
    dmod_cols = lax.dynamic_slice(small_all, (0, j_me * n_mod), (N_DEV, n_mod))
    dmod_pad = jnp.concatenate([dmod_cols, jnp.zeros_like(dmod_cols)], axis=0).astype(BF16)
    gw_ada = _mm_tn("gw_ada_mm", ca_pad, dmod_pad, 1)[0]

    weights = dict(w_ada=w_ada, b_ada=b_ada, g_pre_attn=g_pre_attn, w_in=w_in, g_q_a=g_q_a, g_k_a=g_k_a, g_ckv=g_ckv,
                   w_kv_b=w_kv_b, g_out_a=g_out_a, g_out_b=g_out_b, w_out=w_out, g_post_attn=g_post_attn,
                   g_pre_mlp=g_pre_mlp, w_mlp_in=w_mlp_in, w_mlp_out=w_mlp_out, g_post_mlp=g_post_mlp)
    ms = dict(w_ada=m_w_ada, b_ada=m_b_ada, g_pre_attn=m_g_pre_attn, w_in=m_w_in, g_q_a=m_g_q_a, g_k_a=m_g_k_a,
              g_ckv=m_g_ckv, w_kv_b=m_w_kv_b, g_out_a=m_g_out_a, g_out_b=m_g_out_b, w_out=m_w_out,
              g_post_attn=m_g_post_attn, g_pre_mlp=m_g_pre_mlp, w_mlp_in=m_w_mlp_in, w_mlp_out=m_w_mlp_out,
              g_post_mlp=m_g_post_mlp)
    vs = dict(w_ada=v_w_ada, b_ada=v_b_ada, g_pre_attn=v_g_pre_attn, w_in=v_w_in, g_q_a=v_g_q_a, g_k_a=v_g_k_a,
              g_ckv=v_g_ckv, w_kv_b=v_w_kv_b, g_out_a=v_g_out_a, g_out_b=v_g_out_b, w_out=v_w_out,
              g_post_attn=v_g_post_attn, g_pre_mlp=v_g_pre_mlp, w_mlp_in=v_w_mlp_in, w_mlp_out=v_w_mlp_out,
              g_post_mlp=v_g_post_mlp)
    order = list(weights)
    grads, deltas, new_m, new_v = {}, {}, {}, {}

    def update(name, g, after=()):
        view = (lambda t: jnp.swapaxes(t, 0, 1)) if name == "w_in" else (lambda t: t)
        d_, m_, v_, *g_ = _adamw("adamw_" + name, view(weights[name][0]), g, view(ms[name][0]), view(vs[name][0]),
                                 after, rewrite_g=name != "w_ada")
        grads[name], deltas[name], new_m[name], new_v[name] = (view(t)[None] for t in ((g_ or [g])[0], d_, m_, v_))
        return d_

    chain = ((("w_mlp_out",), r_down), (("w_mlp_in",), r_up), (("w_out", "w_kv_b"), r_kvb), (("w_in",), r_in))
    swap, tok = _reduce_finish(chain[0][1], where, small_sum)
    last = update("w_ada", gw_ada, (tok,))
    for k, (names, _) in enumerate(chain):
        gs = _reduce_end(swap, last)
        if k + 1 < len(chain):
            swap, tok = _reduce_finish(chain[k + 1][1], where, gs[0])
        for name, g in zip(names, gs):
            last = update(name, g, (tok,))

    cat = lambda d: jnp.concatenate([d[n] for n in small_names], axis=1)
    d_, m_, v_ = _adamw("adamw_small", cat(weights), small_sum, cat(ms), cat(vs))
    off = 0
    for n in small_names:
        w_ = weights[n].shape[1]
        grads[n], deltas[n], new_m[n], new_v[n] = (a[:, off:off + w_] for a in (small_sum, d_, m_, v_))
        off += w_

    return (loss, grad_x[None], *[grads[n] for n in order], *[deltas[n] for n in order],
            *[new_m[n] for n in order], *[new_v[n] for n in order])
```

```python
import functools
import math

import numpy as np
import jax
import jax.numpy as jnp
from jax import lax
from jax.experimental import pallas as pl
from jax.experimental.pallas import tpu as pltpu

F32 = jnp.float32
BF16 = jnp.bfloat16
MESH = pl.DeviceIdType.MESH

EPS = 1e-6
GRID_W = 64
ROPE_THETA = 10000.0
HEAD_DIM = 128
HA = 8
HKV = 2
HB = 8
QK_NOPE = 128
QK_ROPE = 64
V_DIM = 128
KV_RANK = 512
QK_B = QK_NOPE + QK_ROPE
W_QA = HA * HEAD_DIM
W_KA = HKV * HEAD_DIM
W_QB = HB * QK_B
W_IN = W_QA + 2 * W_KA + W_QB + KV_RANK + QK_ROPE
N_CHIPS = 4
N_DEV = 8

LANE = 128
O_QA = 0
O_KA = O_QA + W_QA
O_VA = O_KA + W_KA
O_QN = O_VA + W_KA
O_QP = O_QN + HB * QK_NOPE
O_CKV = O_QP + HB * LANE
O_KPE = O_CKV + KV_RANK
W_INP = O_KPE + LANE

ADAM_LR = 0.001
ADAM_B1 = 0.9
ADAM_B2 = 0.999
ADAM_EPS = 1e-08
ADAM_WD = 0.01
ADAM_STEP = 10

VMEM_LIMIT = 56 * 1024 * 1024
HEADS_PER_STEP = 2


def _params(sem):
    return pltpu.CompilerParams(dimension_semantics=sem, vmem_limit_bytes=VMEM_LIMIT)


def _pick(dim, target, unit):
    best = None
    t = unit
    while t <= min(dim, target):
        if dim % t == 0:
            best = t
        t += unit
    return dim if best is None else best


def _rn(x):
    return x * lax.rsqrt(jnp.mean(x * x, axis=-1, keepdims=True) + EPS)


def _rn_parts(x):
    r = lax.rsqrt(jnp.mean(x * x, axis=-1, keepdims=True) + EPS)
    return x * r, r


def _rn_bwd(n, r, dn):
    return r * (dn - n * jnp.mean(dn * n, axis=-1, keepdims=True))


def _rows(t):
    return jnp.sum(t, axis=0, keepdims=True)


def _rowwise(name, fn, rows, vecs, row_outs, vec_outs, tile, after=()):
    S = rows[0].shape[0]
    assert S % tile == 0
    n_r, n_v, n_ro, n_vo, n_a = len(rows), len(vecs), len(row_outs), len(vec_outs), len(after)

    def body(*refs):
        r_in = refs[:n_r]
        v_in = refs[n_r:n_r + n_v]
        r_out = refs[n_r + n_v + n_a:n_r + n_v + n_a + n_ro]
        v_out = refs[n_r + n_v + n_a + n_ro:]
        ro, vo = fn(*[r[...] for r in r_in], *[v[...] for v in v_in])
        for ref, val in zip(r_out, ro):
            ref[...] = val.astype(ref.dtype)
        if n_vo:
            @pl.when(pl.program_id(0) == 0)
            def _():
                for ref in v_out:
                    ref[...] = jnp.zeros(ref.shape, ref.dtype)
            for ref, val in zip(v_out, vo):
                ref[...] += val

    in_specs = [pl.BlockSpec((tile, a.shape[1]), lambda i: (i, 0)) for a in rows]
    in_specs += [pl.BlockSpec(a.shape, lambda i: (0, 0)) for a in vecs]
    in_specs += [pl.BlockSpec(memory_space=pl.ANY) for _ in after]
    out_specs = [pl.BlockSpec((tile, w), lambda i: (i, 0)) for w, _ in row_outs]
    out_specs += [pl.BlockSpec((1, w), lambda i: (0, 0)) for w in vec_outs]
    out_shape = [jax.ShapeDtypeStruct((S, w), dt) for w, dt in row_outs]
    out_shape += [jax.ShapeDtypeStruct((1, w), F32) for w in vec_outs]
    res = pl.pallas_call(
        body, name=name, grid=(S // tile,), in_specs=in_specs, out_specs=out_specs, out_shape=out_shape,
        compiler_params=_params(("arbitrary",)),
    )(*rows, *vecs, *after)
    return res[:n_ro], res[n_ro:]


CONTRACT_TILE = 2048
MM_TILE_BYTES = 40 * 1024 * 1024


def _rows_that_fit(M, tile_bytes):
    t = _pick(M, 1024, 16)
    while t % 32 == 0 and tile_bytes(t) > MM_TILE_BYTES:
        t //= 2
    return t


def _accumulate(acc, part, step, n_steps, finish):
    if n_steps == 1:
        finish(part)
        return

    @pl.when(step == 0)
    def _():
        acc[...] = part

    @pl.when((step > 0) & (step < n_steps - 1))
    def _():
        acc[...] += part

    @pl.when(step == n_steps - 1)
    def _():
        finish(acc[...] + part)


def _mm_nn(name, a, b3, out_dtypes, epilogue=None, extras=(), after=()):
    M, K = a.shape
    nB, K2, Nsh = b3.shape
    assert K == K2
    N = nB * Nsh
    tn, tk = _pick(Nsh, 1536, LANE), _pick(K, CONTRACT_TILE, LANE)
    out_bytes = sum(jnp.dtype(dt).itemsize for dt in out_dtypes) + sum(e.dtype.itemsize for e in extras)
    tm = _rows_that_fit(M, lambda t: 2 * (t * tk * a.dtype.itemsize + tk * tn * b3.dtype.itemsize)
                        + t * tn * (4 * (K > tk) + 2 * out_bytes))
    nps, nk, ne, no, na = Nsh // tn, K // tk, len(extras), len(out_dtypes), len(after)

    def body(a_ref, b_ref, *rest):
        e_refs, o_refs = rest[:ne], rest[ne + na:ne + na + no]

        def finish(total):
            vals = (total,) if epilogue is None else epilogue(total, *[e[...] for e in e_refs])
            for ref, val in zip(o_refs, vals):
                ref[...] = val.astype(ref.dtype)

        part = jnp.dot(a_ref[...].astype(BF16), b_ref[...].astype(BF16), preferred_element_type=F32)
        _accumulate(rest[-1], part, pl.program_id(2), nk, finish)

    in_specs = [pl.BlockSpec((tm, tk), lambda i, j, k: (i, k)),
                pl.BlockSpec((None, tk, tn), lambda i, j, k: (j // nps, k, j % nps))]
    in_specs += [pl.BlockSpec((tm, tn), lambda i, j, k: (i, j)) for _ in extras]
    in_specs += [pl.BlockSpec(memory_space=pl.ANY) for _ in after]
    return pl.pallas_call(
        body, name=name, grid=(M // tm, N // tn, nk), in_specs=in_specs,
        out_specs=[pl.BlockSpec((tm, tn), lambda i, j, k: (i, j)) for _ in out_dtypes],
        out_shape=[jax.ShapeDtypeStruct((M, N), dt) for dt in out_dtypes],
        scratch_shapes=[pltpu.VMEM((tm, tn), F32)] if nk > 1 else [],
        compiler_params=_params(("parallel", "parallel", "arbitrary")),
    )(a, b3, *extras, *after)


def _mm_nt(name, a, b3, out_dtypes, epilogue=None, extras=(), after=()):
    M, N = a.shape
    nB, K, Nsh = b3.shape
    assert N == nB * Nsh
    to, tn = _pick(K, 1536, LANE), _pick(Nsh, CONTRACT_TILE, LANE)
    out_bytes = sum(jnp.dtype(dt).itemsize for dt in out_dtypes) + sum(e.dtype.itemsize for e in extras)
    tm = _rows_that_fit(M, lambda t: 2 * (t * tn * a.dtype.itemsize + to * tn * b3.dtype.itemsize)
                        + t * to * (4 * (N > tn) + 2 * out_bytes))
    nps, nn, ne, no, na = Nsh // tn, N // tn, len(extras), len(out_dtypes), len(after)

    def body(a_ref, b_ref, *rest):
        e_refs, o_refs = rest[:ne], rest[ne + na:ne + na + no]

        def finish(total):
            vals = (total,) if epilogue is None else epilogue(total, *[e[...] for e in e_refs])
            for ref, val in zip(o_refs, vals):
                ref[...] = val.astype(ref.dtype)

        part = lax.dot_general(a_ref[...].astype(BF16), b_ref[...].astype(BF16),
                               (((1,), (1,)), ((), ())), preferred_element_type=F32)
        _accumulate(rest[-1], part, pl.program_id(2), nn, finish)

    in_specs = [pl.BlockSpec((tm, tn), lambda i, o, n: (i, n)),
                pl.BlockSpec((None, to, tn), lambda i, o, n: (n // nps, o, n % nps))]
    in_specs += [pl.BlockSpec((tm, to), lambda i, o, n: (i, o)) for _ in extras]
    in_specs += [pl.BlockSpec(memory_space=pl.ANY) for _ in after]
    return pl.pallas_call(
        body, name=name, grid=(M // tm, K // to, nn), in_specs=in_specs,
        out_specs=[pl.BlockSpec((tm, to), lambda i, o, n: (i, o)) for _ in out_dtypes],
        out_shape=[jax.ShapeDtypeStruct((M, K), dt) for dt in out_dtypes],
        scratch_shapes=[pltpu.VMEM((tm, to), F32)] if nn > 1 else [],
        compiler_params=_params(("parallel", "parallel", "arbitrary")),
    )(a, b3, *extras, *after)


def _mm_tn(name, a, g, nB, after=()):
    M, K = a.shape
    M2, N = g.shape
    assert M == M2 and N % nB == 0
    Nsh = N // nB
    tk, tn, tm = _pick(K, 1536, LANE), _pick(Nsh, 1024, LANE), _pick(M, CONTRACT_TILE, 16)
    nps, nm = Nsh // tn, M // tm

    def body(a_ref, g_ref, *rest):
        o_ref = rest[len(after)]

        def finish(total):
            o_ref[...] = total

        part = lax.dot_general(a_ref[...].astype(BF16), g_ref[...].astype(BF16),
                               (((0,), (0,)), ((), ())), preferred_element_type=F32)
        _accumulate(rest[-1], part, pl.program_id(2), nm, finish)

    return pl.pallas_call(
        body, name=name, grid=(K // tk, N // tn, nm),
        in_specs=[pl.BlockSpec((tm, tk), lambda k, n, m: (m, k)),
                  pl.BlockSpec((tm, tn), lambda k, n, m: (m, n))] + [pl.BlockSpec(memory_space=pl.ANY) for _ in after],
        out_specs=pl.BlockSpec((None, tk, tn), lambda k, n, m: (n // nps, k, n % nps)),
        out_shape=jax.ShapeDtypeStruct((nB, K, Nsh), F32),
        scratch_shapes=[pltpu.VMEM((tk, tn), F32)] if nm > 1 else [],
        compiler_params=_params(("parallel", "parallel", "arbitrary")),
    )(a, g, *after)


def _mm_tn_rows(name, a, g, n_rows, pieces):
    M, K = a.shape
    M2, N = g.shape
    assert M == M2
    tk, tn, tm = _pick(K, 1536, LANE), _pick(N, 512, LANE), _pick(M, CONTRACT_TILE, 16)
    nk, nm = K // tk, M // tm

    def body(a_ref, g_ref, o_ref, *acc):
        kk = pl.program_id(1)

        def finish(total):
            for t in range(nk):
                @pl.when(kk == t)
                def _(t=t):
                    for src, dst, n in pieces:
                        lo, hi = max(src, t * tk), min(src + n, (t + 1) * tk)
                        if lo < hi:
                            o_ref[dst + lo - src:dst + hi - src, :] = total[lo - t * tk:hi - t * tk, :]

        part = lax.dot_general(a_ref[...].astype(BF16), g_ref[...].astype(BF16),
                               (((0,), (0,)), ((), ())), preferred_element_type=F32)
        _accumulate(acc[0] if acc else None, part, pl.program_id(2), nm, finish)

    return pl.pallas_call(
        body, name=name, grid=(N // tn, nk, nm),
        in_specs=[pl.BlockSpec((tm, tk), lambda n, k, m: (m, k)),
                  pl.BlockSpec((tm, tn), lambda n, k, m: (m, n))],
        out_specs=pl.BlockSpec((n_rows, tn), lambda n, k, m: (0, n)),
        out_shape=jax.ShapeDtypeStruct((n_rows, N), F32),
        scratch_shapes=[pltpu.VMEM((tk, tn), F32)] if nm > 1 else [],
        compiler_params=_params(("arbitrary", "arbitrary", "arbitrary")),
    )(a, g)


def _attn_fwd(name, q, k, v, n_heads, group, dqk, dv, scale, tq, after=()):
    S = q.shape[0]
    hp = 2 * HEADS_PER_STEP
    kvp = max(1, hp // group)

    def body(q_ref, k_ref, v_ref, *rest):
        o_ref, lse_ref = rest[-2:]
        for j in range(hp):
            jk = j * kvp // hp
            s = lax.dot_general(q_ref[:, j * dqk:(j + 1) * dqk], k_ref[:, jk * dqk:(jk + 1) * dqk],
                                (((1,), (1,)), ((), ())), preferred_element_type=F32)
            m = jnp.max(s, axis=-1, keepdims=True)
            e = jnp.exp((s - m) * scale)
            l = jnp.sum(e, axis=-1, keepdims=True)
            o_ref[:, j * dv:(j + 1) * dv] = jnp.dot(e.astype(BF16), v_ref[:, jk * dv:(jk + 1) * dv],
                                                    preferred_element_type=F32) * (1.0 / l)
            lse_ref[:, j * LANE:(j + 1) * LANE] = jnp.broadcast_to(m * scale + jnp.log(l), (tq, LANE))

    kv_block = lambda h, i: (0, (h * hp // group) // kvp)
    return pl.pallas_call(
        body, name=name, grid=(n_heads // hp, S // tq),
        in_specs=[pl.BlockSpec((tq, hp * dqk), lambda h, i: (i, h)),
                  pl.BlockSpec((S, kvp * dqk), kv_block),
                  pl.BlockSpec((S, kvp * dv), kv_block)] + [pl.BlockSpec(memory_space=pl.ANY) for _ in after],
        out_specs=[pl.BlockSpec((tq, hp * dv), lambda h, i: (i, h)),
                   pl.BlockSpec((tq, hp * LANE), lambda h, i: (i, h))],
        out_shape=[jax.ShapeDtypeStruct((S, n_heads * dv), F32), jax.ShapeDtypeStruct((S, n_heads * LANE), F32)],
        compiler_params=_params(("arbitrary", "arbitrary")),
    )(q, k, v, *after)


def _attn_bwd(name, q, k, v, do, o, lse, n_heads, group, dqk, dv, scale, tq, after=()):
    S = q.shape[0]
    n_kv = n_heads // group
    hp, tq = (group, tq // 2) if group > HEADS_PER_STEP else (HEADS_PER_STEP, tq)
    kvp = max(1, hp // group)

    def body(q_ref, k_ref, v_ref, do_ref, o_ref, lse_ref, *rest):
        dq_ref, dk_ref, dv_ref = rest[len(after):]
        h, i = pl.program_id(0), pl.program_id(1)

        @pl.when(((h * hp) % group == 0) & (i == 0))
        def _():
            dk_ref[...] = jnp.zeros(dk_ref.shape, F32)
            dv_ref[...] = jnp.zeros(dv_ref.shape, F32)

        for j in range(hp):
            jk = j * kvp // hp
            qb, dob = q_ref[:, j * dqk:(j + 1) * dqk], do_ref[:, j * dv:(j + 1) * dv]
            kb, vb = k_ref[:, jk * dqk:(jk + 1) * dqk], v_ref[:, jk * dv:(jk + 1) * dv]
            s = lax.dot_general(qb, kb, (((1,), (1,)), ((), ())), preferred_element_type=F32)
            p = jnp.exp(s * scale - jnp.tile(lse_ref[:, j * LANE:(j + 1) * LANE], (1, S // LANE)))
            dp = lax.dot_general(dob, vb, (((1,), (1,)), ((), ())), preferred_element_type=F32)
            delta = jnp.sum(dob.astype(F32) * o_ref[:, j * dv:(j + 1) * dv], axis=-1, keepdims=True)
            ds = (p * ((dp - delta) * scale)).astype(BF16)
            dv_ref[:, jk * dv:(jk + 1) * dv] += lax.dot_general(p.astype(BF16), dob, (((0,), (0,)), ((), ())),
                                                                preferred_element_type=F32)
            dq_ref[:, j * dqk:(j + 1) * dqk] = jnp.dot(ds, kb, preferred_element_type=F32)
            dk_ref[:, jk * dqk:(jk + 1) * dqk] += lax.dot_general(ds, qb, (((0,), (0,)), ((), ())),
                                                                  preferred_element_type=F32)

    kv_block = lambda h, i: (0, (h * hp // group) // kvp)
    return pl.pallas_call(
        body, name=name, grid=(n_heads // hp, S // tq),
        in_specs=[pl.BlockSpec((tq, hp * dqk), lambda h, i: (i, h)),
                  pl.BlockSpec((S, kvp * dqk), kv_block),
                  pl.BlockSpec((S, kvp * dv), kv_block),
                  pl.BlockSpec((tq, hp * dv), lambda h, i: (i, h)),
                  pl.BlockSpec((tq, hp * dv), lambda h, i: (i, h)),
                  pl.BlockSpec((tq, hp * LANE), lambda h, i: (i, h))] + [pl.BlockSpec(memory_space=pl.ANY) for _ in after],
        out_specs=[pl.BlockSpec((tq, hp * dqk), lambda h, i: (i, h)),
                   pl.BlockSpec((S, kvp * dqk), kv_block),
                   pl.BlockSpec((S, kvp * dv), kv_block)],
        out_shape=[jax.ShapeDtypeStruct((S, n_heads * dqk), F32),
                   jax.ShapeDtypeStruct((S, n_kv * dqk), F32),
                   jax.ShapeDtypeStruct((S, n_kv * dv), F32)],
        compiler_params=_params(("arbitrary", "arbitrary")),
    )(q, k, v, do, o, lse, *after)


def _adamw(name, w, g, m, v, after=(), rewrite_g=False):
    R, C = w.shape
    tile = _pick(R, max(8, (1 << 19) // C // 8 * 8), 8)
    c1 = 1.0 - ADAM_B1 ** ADAM_STEP
    c2 = 1.0 - ADAM_B2 ** ADAM_STEP
    n_out = 4 if rewrite_g else 3

    def body(w_ref, g_ref, m_ref, v_ref, *rest):
        d_ref, nm_ref, nv_ref = rest[len(after):len(after) + 3]
        gg = g_ref[...]
        nm = ADAM_B1 * m_ref[...] + (1.0 - ADAM_B1) * gg
        nv = ADAM_B2 * v_ref[...] + (1.0 - ADAM_B2) * (gg * gg)
        d_ref[...] = -ADAM_LR * ((nm / c1) / (jnp.sqrt(nv / c2) + ADAM_EPS) + ADAM_WD * w_ref[...])
        nm_ref[...] = nm
        nv_ref[...] = nv
        if rewrite_g:
            rest[-1][...] = gg

    spec = pl.BlockSpec((tile, C), lambda i: (i, 0))
    return pl.pallas_call(
        body, name=name, grid=(R // tile,), in_specs=[spec] * 4 + [pl.BlockSpec(memory_space=pl.ANY) for _ in after],
        out_specs=[spec] * n_out, out_shape=[jax.ShapeDtypeStruct((R, C), F32)] * n_out,
        compiler_params=_params(("parallel",)),
    )(w, g, m, v, *after)


def _flip(me, rel):
    return tuple(1 - p if r else p for p, r in zip(me, rel))


def _push(name, ins, out_shapes, transfers, aliases=None):
    ni, no, nt = len(ins), len(out_shapes), len(transfers)

    def body(*refs):
        in_refs, out_refs = refs[:ni], refs[ni:ni + no]
        send_sems, recv_sems = refs[ni + no], refs[ni + no + 1]
        me = (lax.axis_index("x"), lax.axis_index("y"), lax.axis_index("c"))
        copies = []
        for t, (si, sfn, oi, dfn, rel) in enumerate(transfers):
            src_ref = out_refs[si[1]] if isinstance(si, tuple) else in_refs[si]
            src = src_ref.at[sfn(*me)]
            dst = out_refs[oi].at[dfn(*me)]
            if rel is None:
                cp = pltpu.make_async_copy(src, dst, send_sems.at[t])
            else:
                cp = pltpu.make_async_remote_copy(src_ref=src, dst_ref=dst, send_sem=send_sems.at[t],
                                                  recv_sem=recv_sems.at[t], device_id=_flip(me, rel),
                                                  device_id_type=MESH)
            cp.start()
            copies.append(cp)
        for cp in copies:
            cp.wait()

    any_spec = pl.BlockSpec(memory_space=pl.ANY)
    return pl.pallas_call(
        body, name=name, in_specs=[any_spec] * ni, out_specs=[any_spec] * no, out_shape=out_shapes,
        scratch_shapes=[pltpu.SemaphoreType.DMA((nt,)), pltpu.SemaphoreType.DMA((nt,))],
        input_output_aliases=aliases or {},
        compiler_params=pltpu.CompilerParams(has_side_effects=True),
    )(*ins)


_HBM = pl.BlockSpec(memory_space=pltpu.HBM)
_SEM = pl.BlockSpec(memory_space=pltpu.SEMAPHORE)
_DATAFLOW = pltpu.SideEffectType.DATAFLOW_SIDE_EFFECTING


def _split_copies(refs, transfers, send_sems, recv_sems):
    me = (lax.axis_index("x"), lax.axis_index("y"), lax.axis_index("c"))
    return [pltpu.make_async_remote_copy(src_ref=refs[sb].at[sfn(*me)], dst_ref=refs[db].at[dfn(*me)],
                                         send_sem=send_sems.at[t], recv_sem=recv_sems.at[t],
                                         device_id=_flip(me, rel), device_id_type=MESH)
            for t, (sb, sfn, db, dfn, rel) in enumerate(transfers)]


def _push_start(name, bufs, transfers, after=()):
    nb, na, nt = len(bufs), len(after), len(transfers)

    def body(*refs):
        send_sems, recv_sems, token = refs[nb + na], refs[nb + na + 1], refs[-1]
        for cp in _split_copies(refs[nb + na + 2:2 * nb + na + 2], transfers, send_sems, recv_sems):
            cp.start()
        token[...] = jnp.zeros(token.shape, token.dtype)

    res = pl.pallas_call(
        body, name=name,
        out_shape=(pltpu.SemaphoreType.DMA((nt,)), pltpu.SemaphoreType.DMA((nt,)),
                   *[pltpu.HBM(b.shape, b.dtype) for b in bufs], jax.ShapeDtypeStruct((8, LANE), F32)),
        in_specs=[_HBM] * nb + [pl.BlockSpec(memory_space=pl.ANY)] * na,
        out_specs=(_SEM, _SEM, *[_HBM] * nb, pl.BlockSpec(memory_space=pltpu.VMEM)),
        input_output_aliases={i: i + 2 for i in range(nb)},
        compiler_params=pltpu.CompilerParams(has_side_effects=_DATAFLOW),
    )(*[pltpu.with_memory_space_constraint(b, pltpu.HBM) for b in bufs], *after)
    return res[0], res[1], list(res[2:2 + nb]), res[-1]


def _push_wait(name, send_sems, recv_sems, bufs, transfers, after):
    nb = len(bufs)

    def body(*refs):
        for cp in _split_copies(refs[:nb], transfers, refs[nb], refs[nb + 1]):
            cp.wait_send()
            cp.wait_recv()

    res = pl.pallas_call(
        body, name=name, out_shape=[pltpu.HBM(b.shape, b.dtype) for b in bufs],
        in_specs=[_HBM] * nb + [_SEM, _SEM, pl.BlockSpec(memory_space=pl.ANY)], out_specs=[_HBM] * nb,
        input_output_aliases={i: i for i in range(nb)},
        compiler_params=pltpu.CompilerParams(has_side_effects=_DATAFLOW),
    )(*bufs, send_sems, recv_sems, after)
    return list(res)


ICI_RELS = ((1, 0, 0), (0, 1, 0), (1, 1, 0))
SIBLING = (0, 0, 1)


def _chip(x, y):
    return 2 * x + y


def _cast_into_slot(name, where, w, after=()):
    R, C = w.shape
    tile = _pick(R, 512, BF16_ROWS)

    def body(where_ref, w_ref, *rest):
        rest[-1][...] = w_ref[...].astype(BF16)

    return pl.pallas_call(
        body, name=name,
        grid_spec=pltpu.PrefetchScalarGridSpec(
            num_scalar_prefetch=1, grid=(R // tile,),
            in_specs=[pl.BlockSpec((tile, C), lambda i, wh: (i, 0))] + [pl.BlockSpec(memory_space=pl.ANY) for _ in after],
            out_specs=pl.BlockSpec((None, tile, C), lambda i, wh: (wh[1], i, 0))),
        out_shape=jax.ShapeDtypeStruct((N_CHIPS, R, C), BF16),
        compiler_params=_params(("parallel",)),
    )(where, w, *after)


BF16_ROWS = 16


def _rows_split(shape2):
    return (shape2[0] // 2) % BF16_ROWS == 0


def _half_shape(shape2):
    R, C = shape2
    return (R // 2, C) if _rows_split(shape2) else (R, C // 2)


def _half(shape2, c):
    R, C = shape2
    if _rows_split(shape2):
        return (pl.ds(c * (R // 2), R // 2), slice(None))
    return (slice(None), pl.ds(pl.multiple_of(c * (C // 2), LANE), C // 2))


def _gather_casts(tag, where, shards, after=()):
    return [_cast_into_slot(f"cast_w_{tag}{w}", where, s, after) for w, s in enumerate(shards)]


def _gather_start(tag, own, after):
    transfers = []
    for w in range(len(own)):
        def idx(x, y, c, shape2=own[w].shape[1:]):
            return (_chip(x, y), *_half(shape2, c))
        transfers += [(w, idx, w, idx, rel) for rel in ICI_RELS]
    send_sems, recv_sems, bufs, token = _push_start(f"gather_{tag}_start", own, transfers, after)
    return (tag, send_sems, recv_sems, bufs, transfers), token


def _forward_transfers(part):
    second = []
    for w, p in enumerate(part):
        for rel in ICI_RELS:
            def idx(x, y, c, shape2=p.shape[1:], rel=rel):
                return (_chip(x ^ rel[0], y ^ rel[1]), *_half(shape2, c))
            second.append((w, idx, w, idx, SIBLING))
    return second


def _gather_finish(handle, after):
    tag, send_sems, recv_sems, bufs, transfers = handle
    part = _push_wait(f"gather_{tag}_wait", send_sems, recv_sems, bufs, transfers, after)
    outs = [jax.ShapeDtypeStruct(p.shape, p.dtype) for p in part]
    second = [(("out", sb), sfn, db, dfn, rel) for sb, sfn, db, dfn, rel in _forward_transfers(part)]
    return _push(f"gather_{tag}_d2d", list(part), outs, second, aliases={w: w for w in range(len(part))})


def _gather_mid(handle, after):
    tag, send_sems, recv_sems, bufs, transfers = handle
    part = _push_wait(f"gather_{tag}_wait", send_sems, recv_sems, bufs, transfers, after)
    second = _forward_transfers(part)
    s2, r2, bufs2, token = _push_start(f"gather_{tag}_d2d_start", part, second)
    return (tag, s2, r2, bufs2, second), token


def _gather_end(handle, after):
    tag, send_sems, recv_sems, bufs, transfers = handle
    return _push_wait(f"gather_{tag}_d2d_wait", send_sems, recv_sems, bufs, transfers, after)


def _reduce_begin(tag, grads, after=()):
    n = len(grads)
    land = [lax.empty((N_CHIPS, *_half_shape(g.shape[1:])), F32) for g in grads]
    t1 = []
    for w in range(n):
        t1.append((w, lambda x, y, c, shape2=grads[w].shape[1:]: (slice(None), *_half(shape2, 1 - c)), n + w,
                   lambda x, y, c: (slice(None), slice(None), slice(None)), SIBLING))
    send_sems, recv_sems, bufs, token = _push_start(f"reduce_{tag}_d2d_start", list(grads) + land, t1, after)
    return (tag, send_sems, recv_sems, bufs, t1), token


def _reduce_start(handle, where, after, after_start=()):
    tag, send_sems, recv_sems, bufs, t1 = handle
    n = len(bufs) // 2
    res = _push_wait(f"reduce_{tag}_d2d_wait", send_sems, recv_sems, bufs, t1, after)
    grads, from_sib = res[:n], res[n:]
    pair = [_pair_sum(f"pair_sum_{tag}{w}", where, grads[w], from_sib[w]) for w in range(n)]
    land = [lax.empty((3, *_half_shape(g.shape[1:])), BF16) for g in grads]
    t3 = []
    for w in range(n):
        for r, rel in enumerate(ICI_RELS):
            t3.append((w, lambda x, y, c, rel=rel: (_chip(x ^ rel[0], y ^ rel[1]), slice(None), slice(None)), n + w,
                       lambda x, y, c, r=r: (r, slice(None), slice(None)), rel))
    send_sems, recv_sems, bufs, token = _push_start(f"reduce_{tag}_start", pair + land, t3, after_start)
    return (tag, send_sems, recv_sems, bufs, t3, list(grads), list(from_sib)), token


def _reduce_finish(handle, where, after):
    tag, send_sems, recv_sems, bufs, t3, grads, from_sib = handle
    n = len(grads)
    from_chips = _push_wait(f"reduce_{tag}_wait", send_sems, recv_sems, bufs, t3, after)[n:]
    mine = [_chip_sum(f"chip_sum_{tag}{w}", where, grads[w], from_sib[w], from_chips[w]) for w in range(n)]
    t5 = []
    for w in range(n):
        def idx(x, y, c, shape2=grads[w].shape[1:]):
            return _half(shape2, c)
        t5.append((w, idx, w, idx, SIBLING))
    send_sems, recv_sems, bufs, token = _push_start(f"reduce_{tag}_swap_start", mine, t5)
    return (tag, send_sems, recv_sems, bufs, t5), token


def _reduce_end(handle, after):
    tag, send_sems, recv_sems, bufs, t5 = handle
    return _push_wait(f"reduce_{tag}_swap_wait", send_sems, recv_sems, bufs, t5, after)


def _pair_sum(name, where, g, from_sib):
    _, hr, hc = from_sib.shape
    tile = _pick(hr, max(16, (1 << 19) // hc // 16 * 16), 16)
    nb = hr // tile
    slot = lambda k, wh: (wh[1] + 1 + k) % N_CHIPS
    if _rows_split(g.shape[1:]):
        mine = lambda k, i, wh: (slot(k, wh), wh[0] * nb + i, 0)
    else:
        mine = lambda k, i, wh: (slot(k, wh), i, wh[0])

    def body(where_ref, g_ref, s_ref, o_ref):
        o_ref[...] = (g_ref[...] + s_ref[...]).astype(BF16)

    return pl.pallas_call(
        body, name=name,
        grid_spec=pltpu.PrefetchScalarGridSpec(
            num_scalar_prefetch=1, grid=(N_CHIPS - 1, nb),
            in_specs=[pl.BlockSpec((None, tile, hc), mine),
                      pl.BlockSpec((None, tile, hc), lambda k, i, wh: (slot(k, wh), i, 0))],
            out_specs=pl.BlockSpec((None, tile, hc), lambda k, i, wh: (slot(k, wh), i, 0))),
        out_shape=jax.ShapeDtypeStruct((N_CHIPS, hr, hc), BF16),
        compiler_params=_params(("parallel", "parallel")),
    )(where, g, from_sib)


def _chip_sum(name, where, g, from_sib, from_chips):
    _, R, C = g.shape
    _, hr, hc = from_sib.shape
    tile = _pick(hr, max(16, (1 << 19) // hc // 16 * 16), 16)
    nb = hr // tile
    if _rows_split((R, C)):
        mine = lambda i, wh: (wh[0] * nb + i, 0)
    else:
        mine = lambda i, wh: (i, wh[0])

    def body(where_ref, g_ref, s_ref, r_ref, o_ref):
        acc = g_ref[...] + s_ref[...]
        for r in range(3):
            acc = acc + r_ref[r].astype(F32)
        o_ref[...] = acc

    return pl.pallas_call(
        body, name=name,
        grid_spec=pltpu.PrefetchScalarGridSpec(
            num_scalar_prefetch=1, grid=(nb,),
            in_specs=[pl.BlockSpec((None, tile, hc), lambda i, wh: (wh[1], *mine(i, wh))),
                      pl.BlockSpec((None, tile, hc), lambda i, wh: (wh[1], i, 0)),
                      pl.BlockSpec((3, tile, hc), lambda i, wh: (0, i, 0))],
            out_specs=pl.BlockSpec((tile, hc), mine)),
        out_shape=jax.ShapeDtypeStruct((R, C), F32),
        compiler_params=_params(("parallel",)),
    )(where, g, from_sib, from_chips)


def _allgather8(name, blk, after=()):
    m_per, n = blk.shape
    rels = [(0, 0, 1), (1, 0, 0), (0, 1, 0), (1, 1, 0), (1, 0, 1), (0, 1, 1), (1, 1, 1)]

    def body(x_ref, *rest):
        out_ref, send_sems, recv_sems = rest[len(after):]
        me = (lax.axis_index("x"), lax.axis_index("y"), lax.axis_index("c"))
        my_rows = out_ref.at[pl.ds((4 * me[0] + 2 * me[1] + me[2]) * m_per, m_per), :]
        out_ref[pl.ds((4 * me[0] + 2 * me[1] + me[2]) * m_per, m_per), :] = x_ref[...]
        copies = []
        for t, rel in enumerate(rels):
            cp = pltpu.make_async_remote_copy(src_ref=x_ref, dst_ref=my_rows, send_sem=send_sems.at[t],
                                              recv_sem=recv_sems.at[t], device_id=_flip(me, rel),
                                              device_id_type=MESH)
            cp.start()
            copies.append(cp)
        for cp in copies:
            cp.wait()

    return pl.pallas_call(
        body, name=name, out_shape=jax.ShapeDtypeStruct((N_DEV * m_per, n), blk.dtype),
        in_specs=[pl.BlockSpec(memory_space=pltpu.VMEM)] + [pl.BlockSpec(memory_space=pl.ANY) for _ in after],
        out_specs=pl.BlockSpec(memory_space=pltpu.VMEM),
        scratch_shapes=[pltpu.SemaphoreType.DMA((7,)), pltpu.SemaphoreType.DMA((7,))],
        compiler_params=pltpu.CompilerParams(has_side_effects=True, vmem_limit_bytes=VMEM_LIMIT),
    )(blk, *after)


def _rope_tables(seq_len, dim, reps):
    rows = seq_len // GRID_W
    t = np.arange(seq_len)
    row, col = (t // GRID_W).astype(np.float32), (t % GRID_W).astype(np.float32)
    half = dim // 2
    inv = jnp.asarray(ROPE_THETA, F32) ** (-jnp.arange(0, half, 2, dtype=F32) / half)
    ang_r = jnp.asarray(row)[:, None] * inv[None, :]
    ang_c = jnp.asarray(col)[:, None] * inv[None, :]
    ang = jnp.concatenate([ang_r, ang_r, ang_c, ang_c], axis=-1)
    cos, sin = jnp.cos(ang), jnp.sin(ang)
    low = (np.arange(dim) % (dim // 2)) < (dim // 4)
    s_lo = jnp.where(jnp.asarray(low)[None, :], -sin, 0.0)
    s_hi = jnp.where(jnp.asarray(low)[None, :], 0.0, sin)
    del rows
    return tuple(jnp.tile(a, (1, reps)) for a in (cos, s_lo, s_hi))


def _make_rope(shift):
    def up(v):
        return pltpu.roll(v, LANE - shift, 1)

    def down(v):
        return pltpu.roll(v, shift, 1)

    @jax.custom_vjp
    def rope(v, cos, s_lo, s_hi):
        return v * cos + up(v) * s_lo + down(v) * s_hi

    def fwd(v, cos, s_lo, s_hi):
        return rope(v, cos, s_lo, s_hi), (cos, s_lo, s_hi)

    def bwd(res, dy):
        cos, s_lo, s_hi = res
        return dy * cos + down(dy * s_lo) + up(dy * s_hi), jnp.zeros_like(cos), jnp.zeros_like(cos), jnp.zeros_like(cos)

    rope.defvjp(fwd, bwd)
    return rope


_rope_a = _make_rope(HEAD_DIM // 4)
_rope_b = _make_rope(QK_ROPE // 4)


def _prep(segs, tabs, g_q, g_k, g_ckv):
    qa, ka, va, qn, qp, ckv, kpe = segs
    ta, tb = tabs
    q_a = jnp.concatenate([_rope_a(_rn(s) * g_q, *ta) for s in qa], axis=1)
    k_a = jnp.concatenate([_rope_a(_rn(s) * g_k, *ta) for s in ka], axis=1)
    cat = []
    for h in range(HB):
        cat += [qn[h], _rope_b(qp[h], *tb)]
    q_cat = jnp.concatenate(cat, axis=1)
    return q_a, k_a, va, q_cat, _rn(ckv) * g_ckv, _rope_b(kpe, *tb)


def _split_proj(proj):
    qa = [proj[:, O_QA + h * LANE:O_QA + (h + 1) * LANE] for h in range(HA)]
    ka = [proj[:, O_KA + h * LANE:O_KA + (h + 1) * LANE] for h in range(HKV)]
    va = proj[:, O_VA:O_QN]
    qn = [proj[:, O_QN + h * LANE:O_QN + (h + 1) * LANE] for h in range(HB)]
    qp = [proj[:, O_QP + h * LANE:O_QP + (h + 1) * LANE] for h in range(HB)]
    return qa, ka, va, qn, qp, proj[:, O_CKV:O_KPE], proj[:, O_KPE:W_INP]


def _permute_w_in(nat):
    K = nat.shape[1]
    qb = nat[W_QA + 2 * W_KA:W_QA + 2 * W_KA + W_QB].reshape(HB, QK_B, K)
    qp = jnp.concatenate([qb[:, QK_NOPE:], jnp.zeros((HB, LANE - QK_ROPE, K), nat.dtype)], axis=1)
    tail = nat[W_QA + 2 * W_KA + W_QB:]
    return jnp.concatenate([nat[:O_QN], qb[:, :QK_NOPE].reshape(HB * QK_NOPE, K), qp.reshape(HB * LANE, K),
                            tail, jnp.zeros((LANE - QK_ROPE, K), nat.dtype)], axis=0)


def _unpermute_pieces():
    pieces = [(0, 0, O_QN)]
    for h in range(HB):
        pieces.append((O_QN + h * QK_NOPE, O_QN + h * QK_B, QK_NOPE))
        pieces.append((O_QP + h * LANE, O_QN + h * QK_B + QK_NOPE, QK_ROPE))
    pieces.append((O_CKV, O_QN + W_QB, KV_RANK))
    pieces.append((O_KPE, O_QN + W_QB + KV_RANK, QK_ROPE))
    return pieces


def kernel(x, c, w_ada, b_ada, g_pre_attn, w_in, g_q_a, g_k_a, g_ckv, w_kv_b, g_out_a, g_out_b, w_out, g_post_attn, g_pre_mlp, w_mlp_in, w_mlp_out, g_post_mlp, loss_target, m_w_ada, m_b_ada, m_g_pre_attn, m_w_in, m_g_q_a, m_g_k_a, m_g_ckv, m_w_kv_b, m_g_out_a, m_g_out_b, m_w_out, m_g_post_attn, m_g_pre_mlp, m_w_mlp_in, m_w_mlp_out, m_g_post_mlp, v_w_ada, v_b_ada, v_g_pre_attn, v_w_in, v_g_q_a, v_g_k_a, v_g_ckv, v_w_kv_b, v_g_out_a, v_g_out_b, v_w_out, v_g_post_attn, v_g_pre_mlp, v_w_mlp_in, v_w_mlp_out, v_g_post_mlp):
    S, D = x.shape[1], x.shape[2]
    x2d, tgt = x[0], loss_target[0]
    ix, iy, ic = lax.axis_index("x"), lax.axis_index("y"), lax.axis_index("c")
    j_me = _chip(ix, iy)
    e_me = 4 * ix + 2 * iy + ic
    T = _pick(S, 256, 8)
    TB = _pick(S, 128, 8)

    where = jnp.stack([ic, j_me]).astype(jnp.int32)
    DFF = w_mlp_out.shape[1] * N_CHIPS

    c_all = _allgather8("gather_c", c.reshape(8, D // 8)).reshape(N_DEV, D)
    w_in_t = jnp.swapaxes(w_in[0], 0, 1)
    h_qkv, tok_a = _gather_start("qkv", _gather_casts("qkv", where, [w_in_t, w_kv_b[0]]), (c_all,))
    later = [_gather_casts(tag, where, [w[0]], (tok_a,))
             for tag, w in (("out", w_out), ("up", w_mlp_in), ("down", w_mlp_out))]

    def silu_fn(cc):
        return (cc * (1.0 / (1.0 + jnp.exp(-cc))),), ()

    (c_act,), _ = _rowwise("silu_c", silu_fn, [c_all], [], [(D, BF16)], [], N_DEV)
    ca_pad = jnp.concatenate([c_act, jnp.zeros_like(c_act)], axis=0)
    n_mod = w_ada.shape[2]
    b_cols = lax.dynamic_slice(b_ada, (0, j_me * n_mod), (1, n_mod))
    (mod_part,) = _mm_nn("mod_mm", ca_pad, w_ada, [F32], epilogue=lambda acc, b: (acc + b,),
                         extras=(jnp.broadcast_to(b_cols, (16, n_mod)),))
    mod_all = _allgather8("gather_mod", mod_part[:8] + tok_a[0, 0],
                          after=[c_[0] for c_ in later]).reshape(N_DEV, 8, n_mod)
    mod = jnp.concatenate([lax.dynamic_slice(mod_all, (2 * j, e_me, 0), (1, 1, n_mod))[0] for j in range(N_CHIPS)],
                          axis=1)

    h_out, tok_b = _gather_start("out", later[0], (mod_all,))
    h_up, tok_c = _gather_start("up", later[1], (tok_b,))
    h_down, tok_d = _gather_start("down", later[2], (tok_c,))
    mod = mod + tok_d[0, 0]
    sh_a, sc_a, gt_a, sh_m, sc_m, gt_m = [mod[:, i * D:(i + 1) * D] for i in range(6)]

    tabs_a = _rope_tables(S, HEAD_DIM, 1)
    tabs_b = _rope_tables(S, QK_ROPE, LANE // QK_ROPE)

    def pre_attn_fn(xt, g, sc, sh):
        return (_rn(xt) * g * (1.0 + sc) + sh,), ()

    (h_b,), _ = _rowwise("pre_attn", pre_attn_fn, [x2d], [g_pre_attn, sc_a, sh_a], [(D, BF16)], [], T)
    g_in, g_kvb = _gather_finish(h_qkv, h_b)
    wpt = _permute_w_in(g_in.reshape(W_IN, D))[None]
    (proj,) = _mm_nt("proj_mm", h_b, wpt, [F32])

    def prep_fn(pt, ca_, sla, sha, cb_, slb, shb, gq, gk, gc):
        return _prep(_split_proj(pt), ((ca_, sla, sha), (cb_, slb, shb)), gq, gk, gc), ()

    (qa_b, ka_b, va_b, qcat_b, ckvn_b, kpe_b), _ = _rowwise(
        "prep", prep_fn, [proj, *tabs_a, *tabs_b], [g_q_a, g_k_a, g_ckv],
        [(W_QA, BF16), (W_KA, BF16), (W_KA, BF16), (2 * HB * LANE, BF16), (KV_RANK, BF16), (LANE, BF16)], [], TB)
    (kv,) = _mm_nn("kv_mm", ckvn_b, g_kvb, [F32])

    def kcat_fn(kvt, kpet):
        kc, vs = [], []
        for h in range(HB):
            kc += [kvt[:, 2 * h * LANE:(2 * h + 1) * LANE], kpet.astype(F32)]
            vs.append(kvt[:, (2 * h + 1) * LANE:(2 * h + 2) * LANE])
        return (jnp.concatenate(kc, axis=1), jnp.concatenate(vs, axis=1)), ()

    (kcat_b, vb_b), _ = _rowwise("kcat", kcat_fn, [kv, kpe_b], [], [(2 * HB * LANE, BF16), (HB * V_DIM, BF16)], [], T)

    TQ = _pick(S, 512, 16)
    sc_a_ = 1.0 / math.sqrt(HEAD_DIM)
    sc_b_ = 1.0 / math.sqrt(QK_B)
    o_a, lse_a = _attn_fwd("attn_a_fwd", qa_b, ka_b, va_b, HA, HA // HKV, HEAD_DIM, HEAD_DIM, sc_a_, TQ)
    m_out, tok_m = _gather_mid(h_out, o_a)
    o_b, lse_b = _attn_fwd("attn_b_fwd", qcat_b, kcat_b, vb_b, HB, 1, 2 * LANE, V_DIM, sc_b_, TQ, after=(tok_m,))
    m_up, tok_m = _gather_mid(h_up, o_b)

    def mix_fn(oa, ob, ga, gb):
        return (jnp.concatenate([_rn(oa) * ga, _rn(ob) * gb], axis=1),), ()

    (on_b,), _ = _rowwise("mix_norm", mix_fn, [o_a, o_b], [g_out_a, g_out_b], [(2 * W_QA, BF16)], [], T,
                          after=(tok_m,))
    (g_wout,) = _gather_end(m_out, on_b)
    w_out_f = g_wout.reshape(1, N_CHIPS * g_wout.shape[1], D)
    (o2,) = _mm_nn("out_mm", on_b, w_out_f, [F32])

    def mid(xt, o2t, gta, gpa, gpm, scm, shm):
        x1 = xt + gta * (_rn(o2t) * gpa)
        return x1, _rn(x1) * gpm * (1.0 + scm) + shm

    def mid_fn(*a):
        return mid(*a), ()

    mid_vecs = [gt_a, g_post_attn, g_pre_mlp, sc_m, sh_m]
    (x1, h2_b), _ = _rowwise("mid", mid_fn, [x2d, o2], mid_vecs, [(D, F32), (D, BF16)], [], T)
    (g_mlp_in,) = _gather_end(m_up, h2_b)
    relu_b, a_b = _mm_nn("mlp_in_mm", h2_b, g_mlp_in, [BF16, BF16],
                         epilogue=lambda acc: (jnp.maximum(acc, 0.0), jnp.square(jnp.maximum(acc, 0.0))))
    (g_mlp_out,) = _gather_finish(h_down, a_b)
    w_mlp_out_f = g_mlp_out.reshape(1, DFF, D)
    (y,) = _mm_nn("mlp_out_mm", a_b, w_mlp_out_f, [F32])

    def last_fn(yt, x1t, tt, gtm, gpo):
        n, r = _rn_parts(yt)
        err = x1t + gtm * (n * gpo) - tt
        dx2 = err * (1.0 / D)
        loss = 0.5 * jnp.sum(jnp.mean(err * err, axis=-1))
        s = _rows(dx2 * n)
        return (_rn_bwd(n, r, dx2 * (gtm * gpo)), dx2), (jnp.full((1, LANE), loss, F32), gpo * s, gtm * s)

    (dy_b, dx2), (loss_v, d_gt_m, d_g_post_mlp) = _rowwise(
        "loss_bwd", last_fn, [y, x1, tgt], [gt_m, g_post_mlp], [(D, BF16), (D, F32)], [LANE, D, D], TB)

    gw_mlp_out = _mm_tn("gw_mlp_out_mm", a_b, dy_b, 1).reshape(N_CHIPS, DFF // N_CHIPS, D)
    b_down, tok = _reduce_begin("down", [gw_mlp_out])
    (du_b,) = _mm_nt("d_mlp_out_mm", dy_b, w_mlp_out_f, [BF16],
                     epilogue=lambda acc, rt: (acc * (2.0 * rt.astype(F32)),), extras=(relu_b,), after=(tok,))
    r_down, tok = _reduce_start(b_down, where, du_b)
    gw_mlp_in = _mm_tn("gw_mlp_in_mm", h2_b, du_b, N_CHIPS, after=(tok,))
    b_up, tok = _reduce_begin("up", [gw_mlp_in])
    (dh2,) = _mm_nt("d_mlp_in_mm", du_b, g_mlp_in, [F32], after=(tok,))
    r_up, tok_up = _reduce_start(b_up, where, dh2)

    def mid_bwd_fn(xt, o2t, dh2t, dx2t, gta, gpa, gpm, scm, shm):
        n2, r2 = _rn_parts(o2t)
        n1, r1 = _rn_parts(xt + gta * (n2 * gpa))
        dx1 = dx2t + _rn_bwd(n1, r1, dh2t * (gpm * (1.0 + scm)))
        s1, s2 = _rows(dh2t * n1), _rows(dx1 * n2)
        do2 = _rn_bwd(n2, r2, dx1 * (gta * gpa))
        return (dx1, do2), (gpa * s2, gta * s2, (1.0 + scm) * s1, gpm * s1, _rows(dh2t))

    (dx1, do2_b), (d_gt_a, d_g_post_attn, d_g_pre_mlp, d_sc_m, d_sh_m) = _rowwise(
        "mid_bwd", mid_bwd_fn, [x2d, o2, dh2, dx2], mid_vecs, [(D, F32), (D, BF16)], [D] * 5, TB, after=(tok_up,))

    (d_on,) = _mm_nt("d_out_mm", do2_b, w_out_f, [F32])
    gw_out = _mm_tn("gw_out_mm", on_b, do2_b, 1).reshape(N_CHIPS, w_out.shape[1], D)

    def mix_bwd_fn(oa, ob, dont, ga, gb):
        na, ra = _rn_parts(oa)
        nb, rb = _rn_parts(ob)
        da, db = dont[:, :W_QA], dont[:, W_QA:]
        return (_rn_bwd(na, ra, da * ga), _rn_bwd(nb, rb, db * gb)), (_rows(da * na), _rows(db * nb))

    (doa_b, dob_b), (d_g_out_a, d_g_out_b) = _rowwise(
        "mix_bwd", mix_bwd_fn, [o_a, o_b, d_on], [g_out_a, g_out_b], [(W_QA, BF16), (HB * V_DIM, BF16)],
        [W_QA, HB * V_DIM], T)

    dqa, dka, dva = _attn_bwd("attn_a_bwd", qa_b, ka_b, va_b, doa_b, o_a, lse_a, HA, HA // HKV, HEAD_DIM, HEAD_DIM,
                              sc_a_, TQ)
    dqcat, dkcat, dvb = _attn_bwd("attn_b_bwd", qcat_b, kcat_b, vb_b, dob_b, o_b, lse_b, HB, 1, 2 * LANE, V_DIM,
                                  sc_b_, TQ)

    def kcat_bwd_fn(dkc, dvt):
        dkv, dkpe = [], None
        for h in range(HB):
            dkv += [dkc[:, 2 * h * LANE:(2 * h + 1) * LANE], dvt[:, h * V_DIM:(h + 1) * V_DIM]]
            piece = dkc[:, (2 * h + 1) * LANE:(2 * h + 2) * LANE]
            dkpe = piece if dkpe is None else dkpe + piece
        return (jnp.concatenate(dkv, axis=1), dkpe), ()

    (dkv_b, dkpe), _ = _rowwise("kcat_bwd", kcat_bwd_fn, [dkcat, dvb], [], [(2 * HB * LANE, BF16), (LANE, F32)], [], T)
    (d_ckvn,) = _mm_nt("d_kv_mm", dkv_b, g_kvb, [F32])
    gw_kvb = _mm_tn("gw_kv_mm", ckvn_b, dkv_b, N_CHIPS)
    b_kvb, tok_kvb = _reduce_begin("okv", [gw_out, gw_kvb])

    def prep_bwd_fn(pt, dqat, dkat, dvat, dqct, dckt, dkpt, ca_, sla, sha, cb_, slb, shb, gq, gk, gc):
        tabs = ((ca_, sla, sha), (cb_, slb, shb))
        _, vjp = jax.vjp(lambda sg, a1, a2, a3: _prep(sg, tabs, a1, a2, a3), _split_proj(pt), gq, gk, gc)
        (dqa_s, dka_s, dva_s, dqn_s, dqp_s, dck_s, dkp_s), dgq, dgk, dgc = vjp((dqat, dkat, dvat, dqct, dckt, dkpt))
        dproj = jnp.concatenate([*dqa_s, *dka_s, dva_s, *dqn_s, *dqp_s, dck_s, dkp_s], axis=1)
        return (dproj,), (dgq, dgk, dgc)

    (dproj_b,), (d_g_q_a, d_g_k_a, d_g_ckv) = _rowwise(
        "prep_bwd", prep_bwd_fn, [proj, dqa, dka, dva, dqcat, d_ckvn, dkpe, *tabs_a, *tabs_b],
        [g_q_a, g_k_a, g_ckv], [(W_INP, BF16)], [HEAD_DIM, HEAD_DIM, KV_RANK], TB, after=(tok_kvb,))

    r_kvb, tok = _reduce_start(b_kvb, where, dproj_b)
    (dh,) = _mm_nn("d_proj_mm", dproj_b, wpt, [F32], after=(tok,))
    gw_in = _mm_tn_rows("gw_in_mm", dproj_b, h_b, W_IN, _unpermute_pieces()).reshape(N_CHIPS, W_IN // N_CHIPS, D)
    b_in, tok = _reduce_begin("in", [gw_in])

    def pre_attn_bwd_fn(xt, dht, dx1t, g, sc, sh):
        n, r = _rn_parts(xt)
        s = _rows(dht * n)
        return (dx1t + _rn_bwd(n, r, dht * (g * (1.0 + sc))),), ((1.0 + sc) * s, g * s, _rows(dht))

    (grad_x,), (d_g_pre_attn, d_sc_a, d_sh_a) = _rowwise(
        "pre_attn_bwd", pre_attn_bwd_fn, [x2d, dh, dx1], [g_pre_attn, sc_a, sh_a], [(D, F32)], [D] * 3, TB,
        after=(tok,))

    small_names = ["b_ada", "g_pre_attn", "g_q_a", "g_k_a", "g_ckv", "g_out_a", "g_out_b", "g_post_attn",
                   "g_pre_mlp", "g_post_mlp"]
    small_local = jnp.concatenate([d_sh_a, d_sc_a, d_gt_a, d_sh_m, d_sc_m, d_gt_m, d_g_pre_attn, d_g_q_a, d_g_k_a,
                                   d_g_ckv, d_g_out_a, d_g_out_b, d_g_post_attn, d_g_pre_mlp, d_g_post_mlp,
                                   loss_v[:, :8]], axis=1)
    n_sent = small_local.shape[1]
    n_small = n_sent - 8
    small_all = _allgather8("gather_small", small_local.reshape(8, n_sent // 8)).reshape(N_DEV, n_sent)

    def sum8_fn(t):
        return (), (jnp.sum(t, axis=0, keepdims=True),)

    r_in, tok_in = _reduce_start(b_in, where, grad_x, (small_all,))
    _, (sent_sum,) = _rowwise("sum_small", sum8_fn, [small_all], [], [], [n_sent], N_DEV, after=(tok_in,))
    small_sum, loss = sent_sum[:, :n_small], sent_sum[0, n_small]
    dmod_cols = lax.dynamic_slice(small_all, (0, j_me * n_mod), (N_DEV, n_mod))
    dmod_pad = jnp.concatenate([dmod_cols, jnp.zeros_like(dmod_cols)], axis=0).astype(BF16)
    gw_ada = _mm_tn("gw_ada_mm", ca_pad, dmod_pad, 1)[0]

    weights = dict(w_ada=w_ada, b_ada=b_ada, g_pre_attn=g_pre_attn, w_in=w_in, g_q_a=g_q_a, g_k_a=g_k_a, g_ckv=g_ckv,
                   w_kv_b=w_kv_b, g_out_a=g_out_a, g_out_b=g_out_b, w_out=w_out, g_post_attn=g_post_attn,
                   g_pre_mlp=g_pre_mlp, w_mlp_in=w_mlp_in, w_mlp_out=w_mlp_out, g_post_mlp=g_post_mlp)
    ms = dict(w_ada=m_w_ada, b_ada=m_b_ada, g_pre_attn=m_g_pre_attn, w_in=m_w_in, g_q_a=m_g_q_a, g_k_a=m_g_k_a,
              g_ckv=m_g_ckv, w_kv_b=m_w_kv_b, g_out_a=m_g_out_a, g_out_b=m_g_out_b, w_out=m_w_out,
              g_post_attn=m_g_post_attn, g_pre_mlp=m_g_pre_mlp, w_mlp_in=m_w_mlp_in, w_mlp_out=m_w_mlp_out,
              g_post_mlp=m_g_post_mlp)
    vs = dict(w_ada=v_w_ada, b_ada=v_b_ada, g_pre_attn=v_g_pre_attn, w_in=v_w_in, g_q_a=v_g_q_a, g_k_a=v_g_k_a,
              g_ckv=v_g_ckv, w_kv_b=v_w_kv_b, g_out_a=v_g_out_a, g_out_b=v_g_out_b, w_out=v_w_out,
              g_post_attn=v_g_post_attn, g_pre_mlp=v_g_pre_mlp, w_mlp_in=v_w_mlp_in, w_mlp_out=v_w_mlp_out,
              g_post_mlp=v_g_post_mlp)
    order = list(weights)
    grads, deltas, new_m, new_v = {}, {}, {}, {}

    def update(name, g, after=()):
        view = (lambda t: jnp.swapaxes(t, 0, 1)) if name == "w_in" else (lambda t: t)
        d_, m_, v_, *g_ = _adamw("adamw_" + name, view(weights[name][0]), g, view(ms[name][0]), view(vs[name][0]),
                                 after, rewrite_g=name != "w_ada")
        grads[name], deltas[name], new_m[name], new_v[name] = (view(t)[None] for t in ((g_ or [g])[0], d_, m_, v_))
        return d_

    chain = ((("w_mlp_out",), r_down), (("w_mlp_in",), r_up), (("w_out", "w_kv_b"), r_kvb), (("w_in",), r_in))
    swap, tok = _reduce_finish(chain[0][1], where, small_sum)
    last = update("w_ada", gw_ada, (tok,))
    for k, (names, _) in enumerate(chain):
        gs = _reduce_end(swap, last)
        if k + 1 < len(chain):
            swap, tok = _reduce_finish(chain[k + 1][1], where, gs[0])
        for name, g in zip(names, gs):
            last = update(name, g, (tok,))

    cat = lambda d: jnp.concatenate([d[n] for n in small_names], axis=1)
    d_, m_, v_ = _adamw("adamw_small", cat(weights), small_sum, cat(ms), cat(vs))
    off = 0
    for n in small_names:
        w_ = weights[n].shape[1]
        grads[n], deltas[n], new_m[n], new_v[n] = (a[:, off:off + w_] for a in (small_sum, d_, m_, v_))
        off += w_

    return (loss, grad_x[None], *[grads[n] for n in order], *[deltas[n] for n in order],
            *[new_m[n] for n in order], *[new_v[n] for n in order])
```

```python
import functools
import math

import numpy as np
import jax
import jax.numpy as jnp
from jax import lax
from jax.experimental import pallas as pl
from jax.experimental.pallas import tpu as pltpu

F32 = jnp.float32
BF16 = jnp.bfloat16
MESH = pl.DeviceIdType.MESH

EPS = 1e-6
GRID_W = 64
ROPE_THETA = 10000.0
HEAD_DIM = 128
HA = 8
HKV = 2
HB = 8
QK_NOPE = 128
QK_ROPE = 64
V_DIM = 128
KV_RANK = 512
QK_B = QK_NOPE + QK_ROPE
W_QA = HA * HEAD_DIM
W_KA = HKV * HEAD_DIM
W_QB = HB * QK_B
W_IN = W_QA + 2 * W_KA + W_QB + KV_RANK + QK_ROPE
N_CHIPS = 4
N_DEV = 8

LANE = 128
O_QA = 0
O_KA = O_QA + W_QA
O_VA = O_KA + W_KA
O_QN = O_VA + W_KA
O_QP = O_QN + HB * QK_NOPE
O_CKV = O_QP + HB * LANE
O_KPE = O_CKV + KV_RANK
W_INP = O_KPE + LANE

ADAM_LR = 0.001
ADAM_B1 = 0.9
ADAM_B2 = 0.999
ADAM_EPS = 1e-08
ADAM_WD = 0.01
ADAM_STEP = 10

VMEM_LIMIT = 56 * 1024 * 1024
HEADS_PER_STEP = 2


def _params(sem):
    return pltpu.CompilerParams(dimension_semantics=sem, vmem_limit_bytes=VMEM_LIMIT)


def _pick(dim, target, unit):
    best = None
    t = unit
    while t <= min(dim, target):
        if dim % t == 0:
            best = t
        t += unit
    return dim if best is None else best


def _rn(x):
    return x * lax.rsqrt(jnp.mean(x * x, axis=-1, keepdims=True) + EPS)


def _rn_parts(x):
    r = lax.rsqrt(jnp.mean(x * x, axis=-1, keepdims=True) + EPS)
    return x * r, r


def _rn_bwd(n, r, dn):
    return r * (dn - n * jnp.mean(dn * n, axis=-1, keepdims=True))


def _rows(t):
    return jnp.sum(t, axis=0, keepdims=True)


def _rowwise(name, fn, rows, vecs, row_outs, vec_outs, tile, after=()):
    S = rows[0].shape[0]
    assert S % tile == 0
    n_r, n_v, n_ro, n_vo, n_a = len(rows), len(vecs), len(row_outs), len(vec_outs), len(after)

    def body(*refs):
        r_in = refs[:n_r]
        v_in = refs[n_r:n_r + n_v]
        r_out = refs[n_r + n_v + n_a:n_r + n_v + n_a + n_ro]
        v_out = refs[n_r + n_v + n_a + n_ro:]
        ro, vo = fn(*[r[...] for r in r_in], *[v[...] for v in v_in])
        for ref, val in zip(r_out, ro):
            ref[...] = val.astype(ref.dtype)
        if n_vo:
            @pl.when(pl.program_id(0) == 0)
            def _():
                for ref in v_out:
                    ref[...] = jnp.zeros(ref.shape, ref.dtype)
            for ref, val in zip(v_out, vo):
                ref[...] += val

    in_specs = [pl.BlockSpec((tile, a.shape[1]), lambda i: (i, 0)) for a in rows]
    in_specs += [pl.BlockSpec(a.shape, lambda i: (0, 0)) for a in vecs]
    in_specs += [pl.BlockSpec(memory_space=pl.ANY) for _ in after]
    out_specs = [pl.BlockSpec((tile, w), lambda i: (i, 0)) for w, _ in row_outs]
    out_specs += [pl.BlockSpec((1, w), lambda i: (0, 0)) for w in vec_outs]
    out_shape = [jax.ShapeDtypeStruct((S, w), dt) for w, dt in row_outs]
    out_shape += [jax.ShapeDtypeStruct((1, w), F32) for w in vec_outs]
    res = pl.pallas_call(
        body, name=name, grid=(S // tile,), in_specs=in_specs, out_specs=out_specs, out_shape=out_shape,
        compiler_params=_params(("arbitrary",)),
    )(*rows, *vecs, *after)
    return res[:n_ro], res[n_ro:]


CONTRACT_TILE = 2048
MM_TILE_BYTES = 40 * 1024 * 1024


def _rows_that_fit(M, tile_bytes):
    t = _pick(M, 1024, 16)
    while t % 32 == 0 and tile_bytes(t) > MM_TILE_BYTES:
        t //= 2
    return t


def _accumulate(acc, part, step, n_steps, finish):
    if n_steps == 1:
        finish(part)
        return

    @pl.when(step == 0)
    def _():
        acc[...] = part

    @pl.when((step > 0) & (step < n_steps - 1))
    def _():
        acc[...] += part

    @pl.when(step == n_steps - 1)
    def _():
        finish(acc[...] + part)


def _mm_nn(name, a, b3, out_dtypes, epilogue=None, extras=(), after=()):
    M, K = a.shape
    nB, K2, Nsh = b3.shape
    assert K == K2
    N = nB * Nsh
    tn, tk = _pick(Nsh, 1536, LANE), _pick(K, CONTRACT_TILE, LANE)
    out_bytes = sum(jnp.dtype(dt).itemsize for dt in out_dtypes) + sum(e.dtype.itemsize for e in extras)
    tm = _rows_that_fit(M, lambda t: 2 * (t * tk * a.dtype.itemsize + tk * tn * b3.dtype.itemsize)
                        + t * tn * (4 * (K > tk) + 2 * out_bytes))
    nps, nk, ne, no, na = Nsh // tn, K // tk, len(extras), len(out_dtypes), len(after)

    def body(a_ref, b_ref, *rest):
        e_refs, o_refs = rest[:ne], rest[ne + na:ne + na + no]

        def finish(total):
            vals = (total,) if epilogue is None else epilogue(total, *[e[...] for e in e_refs])
            for ref, val in zip(o_refs, vals):
                ref[...] = val.astype(ref.dtype)

        part = jnp.dot(a_ref[...].astype(BF16), b_ref[...].astype(BF16), preferred_element_type=F32)
        _accumulate(rest[-1], part, pl.program_id(2), nk, finish)

    in_specs = [pl.BlockSpec((tm, tk), lambda i, j, k: (i, k)),
                pl.BlockSpec((None, tk, tn), lambda i, j, k: (j // nps, k, j % nps))]
    in_specs += [pl.BlockSpec((tm, tn), lambda i, j, k: (i, j)) for _ in extras]
    in_specs += [pl.BlockSpec(memory_space=pl.ANY) for _ in after]
    return pl.pallas_call(
        body, name=name, grid=(M // tm, N // tn, nk), in_specs=in_specs,
        out_specs=[pl.BlockSpec((tm, tn), lambda i, j, k: (i, j)) for _ in out_dtypes],
        out_shape=[jax.ShapeDtypeStruct((M, N), dt) for dt in out_dtypes],
        scratch_shapes=[pltpu.VMEM((tm, tn), F32)] if nk > 1 else [],
        compiler_params=_params(("parallel", "parallel", "arbitrary")),
    )(a, b3, *extras, *after)


def _mm_nt(name, a, b3, out_dtypes, epilogue=None, extras=(), after=()):
    M, N = a.shape
    nB, K, Nsh = b3.shape
    assert N == nB * Nsh
    to, tn = _pick(K, 1536, LANE), _pick(Nsh, CONTRACT_TILE, LANE)
    out_bytes = sum(jnp.dtype(dt).itemsize for dt in out_dtypes) + sum(e.dtype.itemsize for e in extras)
    tm = _rows_that_fit(M, lambda t: 2 * (t * tn * a.dtype.itemsize + to * tn * b3.dtype.itemsize)
                        + t * to * (4 * (N > tn) + 2 * out_bytes))
    nps, nn, ne, no, na = Nsh // tn, N // tn, len(extras), len(out_dtypes), len(after)

    def body(a_ref, b_ref, *rest):
        e_refs, o_refs = rest[:ne], rest[ne + na:ne + na + no]

        def finish(total):
            vals = (total,) if epilogue is None else epilogue(total, *[e[...] for e in e_refs])
            for ref, val in zip(o_refs, vals):
                ref[...] = val.astype(ref.dtype)

        part = lax.dot_general(a_ref[...].astype(BF16), b_ref[...].astype(BF16),
                               (((1,), (1,)), ((), ())), preferred_element_type=F32)
        _accumulate(rest[-1], part, pl.program_id(2), nn, finish)

    in_specs = [pl.BlockSpec((tm, tn), lambda i, o, n: (i, n)),
                pl.BlockSpec((None, to, tn), lambda i, o, n: (n // nps, o, n % nps))]
    in_specs += [pl.BlockSpec((tm, to), lambda i, o, n: (i, o)) for _ in extras]
    in_specs += [pl.BlockSpec(memory_space=pl.ANY) for _ in after]
    return pl.pallas_call(
        body, name=name, grid=(M // tm, K // to, nn), in_specs=in_specs,
        out_specs=[pl.BlockSpec((tm, to), lambda i, o, n: (i, o)) for _ in out_dtypes],
        out_shape=[jax.ShapeDtypeStruct((M, K), dt) for dt in out_dtypes],
        scratch_shapes=[pltpu.VMEM((tm, to), F32)] if nn > 1 else [],
        compiler_params=_params(("parallel", "parallel", "arbitrary")),
    )(a, b3, *extras, *after)


def _mm_tn(name, a, g, nB, after=()):
    M, K = a.shape
    M2, N = g.shape
    assert M == M2 and N % nB == 0
    Nsh = N // nB
    tk, tn, tm = _pick(K, 1536, LANE), _pick(Nsh, 1024, LANE), _pick(M, CONTRACT_TILE, 16)
    nps, nm = Nsh // tn, M // tm

    def body(a_ref, g_ref, *rest):
        o_ref = rest[len(after)]

        def finish(total):
            o_ref[...] = total

        part = lax.dot_general(a_ref[...].astype(BF16), g_ref[...].astype(BF16),
                               (((0,), (0,)), ((), ())), preferred_element_type=F32)
        _accumulate(rest[-1], part, pl.program_id(2), nm, finish)

    return pl.pallas_call(
        body, name=name, grid=(K // tk, N // tn, nm),
        in_specs=[pl.BlockSpec((tm, tk), lambda k, n, m: (m, k)),
                  pl.BlockSpec((tm, tn), lambda k, n, m: (m, n))] + [pl.BlockSpec(memory_space=pl.ANY) for _ in after],
        out_specs=pl.BlockSpec((None, tk, tn), lambda k, n, m: (n // nps, k, n % nps)),
        out_shape=jax.ShapeDtypeStruct((nB, K, Nsh), F32),
        scratch_shapes=[pltpu.VMEM((tk, tn), F32)] if nm > 1 else [],
        compiler_params=_params(("parallel", "parallel", "arbitrary")),
    )(a, g, *after)


def _mm_tn_rows(name, a, g, n_rows, pieces):
    M, K = a.shape
    M2, N = g.shape
    assert M == M2
    tk, tn, tm = _pick(K, 1536, LANE), _pick(N, 512, LANE), _pick(M, CONTRACT_TILE, 16)
    nk, nm = K // tk, M // tm

    def body(a_ref, g_ref, o_ref, *acc):
        kk = pl.program_id(1)

        def finish(total):
            for t in range(nk):
                @pl.when(kk == t)
                def _(t=t):
                    for src, dst, n in pieces:
                        lo, hi = max(src, t * tk), min(src + n, (t + 1) * tk)
                        if lo < hi:
                            o_ref[dst + lo - src:dst + hi - src, :] = total[lo - t * tk:hi - t * tk, :]

        part = lax.dot_general(a_ref[...].astype(BF16), g_ref[...].astype(BF16),
                               (((0,), (0,)), ((), ())), preferred_element_type=F32)
        _accumulate(acc[0] if acc else None, part, pl.program_id(2), nm, finish)

    return pl.pallas_call(
        body, name=name, grid=(N // tn, nk, nm),
        in_specs=[pl.BlockSpec((tm, tk), lambda n, k, m: (m, k)),
                  pl.BlockSpec((tm, tn), lambda n, k, m: (m, n))],
        out_specs=pl.BlockSpec((n_rows, tn), lambda n, k, m: (0, n)),
        out_shape=jax.ShapeDtypeStruct((n_rows, N), F32),
        scratch_shapes=[pltpu.VMEM((tk, tn), F32)] if nm > 1 else [],
        compiler_params=_params(("arbitrary", "arbitrary", "arbitrary")),
    )(a, g)


def _attn_fwd(name, q, k, v, n_heads, group, dqk, dv, scale, tq, after=()):
    S = q.shape[0]
    hp = 2 * HEADS_PER_STEP
    kvp = max(1, hp // group)

    def body(q_ref, k_ref, v_ref, *rest):
        o_ref, lse_ref = rest[-2:]
        for j in range(hp):
            jk = j * kvp // hp
            s = lax.dot_general(q_ref[:, j * dqk:(j + 1) * dqk], k_ref[:, jk * dqk:(jk + 1) * dqk],
                                (((1,), (1,)), ((), ())), preferred_element_type=F32)
            m = jnp.max(s, axis=-1, keepdims=True)
            e = jnp.exp((s - m) * scale)
            l = jnp.sum(e, axis=-1, keepdims=True)
            o_ref[:, j * dv:(j + 1) * dv] = jnp.dot(e.astype(BF16), v_ref[:, jk * dv:(jk + 1) * dv],
                                                    preferred_element_type=F32) * (1.0 / l)
            lse_ref[:, j * LANE:(j + 1) * LANE] = jnp.broadcast_to(m * scale + jnp.log(l), (tq, LANE))

    kv_block = lambda h, i: (0, (h * hp // group) // kvp)
    return pl.pallas_call(
        body, name=name, grid=(n_heads // hp, S // tq),
        in_specs=[pl.BlockSpec((tq, hp * dqk), lambda h, i: (i, h)),
                  pl.BlockSpec((S, kvp * dqk), kv_block),
                  pl.BlockSpec((S, kvp * dv), kv_block)] + [pl.BlockSpec(memory_space=pl.ANY) for _ in after],
        out_specs=[pl.BlockSpec((tq, hp * dv), lambda h, i: (i, h)),
                   pl.BlockSpec((tq, hp * LANE), lambda h, i: (i, h))],
        out_shape=[jax.ShapeDtypeStruct((S, n_heads * dv), F32), jax.ShapeDtypeStruct((S, n_heads * LANE), F32)],
        compiler_params=_params(("arbitrary", "arbitrary")),
    )(q, k, v, *after)


def _attn_bwd(name, q, k, v, do, o, lse, n_heads, group, dqk, dv, scale, tq, after=()):
    S = q.shape[0]
    n_kv = n_heads // group
    hp, tq = (group, tq // 2) if group > HEADS_PER_STEP else (HEADS_PER_STEP, tq)
    kvp = max(1, hp // group)

    def body(q_ref, k_ref, v_ref, do_ref, o_ref, lse_ref, *rest):
        dq_ref, dk_ref, dv_ref = rest[len(after):]
        h, i = pl.program_id(0), pl.program_id(1)

        @pl.when(((h * hp) % group == 0) & (i == 0))
        def _():
            dk_ref[...] = jnp.zeros(dk_ref.shape, F32)
            dv_ref[...] = jnp.zeros(dv_ref.shape, F32)

        for j in range(hp):
            jk = j * kvp // hp
            qb, dob = q_ref[:, j * dqk:(j + 1) * dqk], do_ref[:, j * dv:(j + 1) * dv]
            kb, vb = k_ref[:, jk * dqk:(jk + 1) * dqk], v_ref[:, jk * dv:(jk + 1) * dv]
            s = lax.dot_general(qb, kb, (((1,), (1,)), ((), ())), preferred_element_type=F32)
            p = jnp.exp(s * scale - jnp.tile(lse_ref[:, j * LANE:(j + 1) * LANE], (1, S // LANE)))
            dp = lax.dot_general(dob, vb, (((1,), (1,)), ((), ())), preferred_element_type=F32)
            delta = jnp.sum(dob.astype(F32) * o_ref[:, j * dv:(j + 1) * dv], axis=-1, keepdims=True)
            ds = (p * ((dp - delta) * scale)).astype(BF16)
            dv_ref[:, jk * dv:(jk + 1) * dv] += lax.dot_general(p.astype(BF16), dob, (((0,), (0,)), ((), ())),
                                                                preferred_element_type=F32)
            dq_ref[:, j * dqk:(j + 1) * dqk] = jnp.dot(ds, kb, preferred_element_type=F32)
            dk_ref[:, jk * dqk:(jk + 1) * dqk] += lax.dot_general(ds, qb, (((0,), (0,)), ((), ())),
                                                                  preferred_element_type=F32)

    kv_block = lambda h, i: (0, (h * hp // group) // kvp)
    return pl.pallas_call(
        body, name=name, grid=(n_heads // hp, S // tq),
        in_specs=[pl.BlockSpec((tq, hp * dqk), lambda h, i: (i, h)),
                  pl.BlockSpec((S, kvp * dqk), kv_block),
                  pl.BlockSpec((S, kvp * dv), kv_block),
                  pl.BlockSpec((tq, hp * dv), lambda h, i: (i, h)),
                  pl.BlockSpec((tq, hp * dv), lambda h, i: (i, h)),
                  pl.BlockSpec((tq, hp * LANE), lambda h, i: (i, h))] + [pl.BlockSpec(memory_space=pl.ANY) for _ in after],
        out_specs=[pl.BlockSpec((tq, hp * dqk), lambda h, i: (i, h)),
                   pl.BlockSpec((S, kvp * dqk), kv_block),
                   pl.BlockSpec((S, kvp * dv), kv_block)],
        out_shape=[jax.ShapeDtypeStruct((S, n_heads * dqk), F32),
                   jax.ShapeDtypeStruct((S, n_kv * dqk), F32),
                   jax.ShapeDtypeStruct((S, n_kv * dv), F32)],
        compiler_params=_params(("arbitrary", "arbitrary")),
    )(q, k, v, do, o, lse, *after)


def _adamw(name, w, g, m, v, after=(), rewrite_g=False):
    R, C = w.shape
    tile = _pick(R, max(8, (1 << 19) // C // 8 * 8), 8)
    c1 = 1.0 - ADAM_B1 ** ADAM_STEP
    c2 = 1.0 - ADAM_B2 ** ADAM_STEP
    n_out = 4 if rewrite_g else 3

    def body(w_ref, g_ref, m_ref, v_ref, *rest):
        d_ref, nm_ref, nv_ref = rest[len(after):len(after) + 3]
        gg = g_ref[...]
        nm = ADAM_B1 * m_ref[...] + (1.0 - ADAM_B1) * gg
        nv = ADAM_B2 * v_ref[...] + (1.0 - ADAM_B2) * (gg * gg)
        d_ref[...] = -ADAM_LR * ((nm / c1) / (jnp.sqrt(nv / c2) + ADAM_EPS) + ADAM_WD * w_ref[...])
        nm_ref[...] = nm
        nv_ref[...] = nv
        if rewrite_g:
            rest[-1][...] = gg

    spec = pl.BlockSpec((tile, C), lambda i: (i, 0))
    return pl.pallas_call(
        body, name=name, grid=(R // tile,), in_specs=[spec] * 4 + [pl.BlockSpec(memory_space=pl.ANY) for _ in after],
        out_specs=[spec] * n_out, out_shape=[jax.ShapeDtypeStruct((R, C), F32)] * n_out,
        compiler_params=_params(("parallel",)),
    )(w, g, m, v, *after)


def _flip(me, rel):
    return tuple(1 - p if r else p for p, r in zip(me, rel))


def _push(name, ins, out_shapes, transfers, aliases=None):
    ni, no, nt = len(ins), len(out_shapes), len(transfers)

    def body(*refs):
        in_refs, out_refs = refs[:ni], refs[ni:ni + no]
        send_sems, recv_sems = refs[ni + no], refs[ni + no + 1]
        me = (lax.axis_index("x"), lax.axis_index("y"), lax.axis_index("c"))
        copies = []
        for t, (si, sfn, oi, dfn, rel) in enumerate(transfers):
            src_ref = out_refs[si[1]] if isinstance(si, tuple) else in_refs[si]
            src = src_ref.at[sfn(*me)]
            dst = out_refs[oi].at[dfn(*me)]
            if rel is None:
                cp = pltpu.make_async_copy(src, dst, send_sems.at[t])
            else:
                cp = pltpu.make_async_remote_copy(src_ref=src, dst_ref=dst, send_sem=send_sems.at[t],
                                                  recv_sem=recv_sems.at[t], device_id=_flip(me, rel),
                                                  device_id_type=MESH)
            cp.start()
            copies.append(cp)
        for cp in copies:
            cp.wait()

    any_spec = pl.BlockSpec(memory_space=pl.ANY)
    return pl.pallas_call(
        body, name=name, in_specs=[any_spec] * ni, out_specs=[any_spec] * no, out_shape=out_shapes,
        scratch_shapes=[pltpu.SemaphoreType.DMA((nt,)), pltpu.SemaphoreType.DMA((nt,))],
        input_output_aliases=aliases or {},
        compiler_params=pltpu.CompilerParams(has_side_effects=True),
    )(*ins)


_HBM = pl.BlockSpec(memory_space=pltpu.HBM)
_SEM = pl.BlockSpec(memory_space=pltpu.SEMAPHORE)
_DATAFLOW = pltpu.SideEffectType.DATAFLOW_SIDE_EFFECTING


def _split_copies(refs, transfers, send_sems, recv_sems):
    me = (lax.axis_index("x"), lax.axis_index("y"), lax.axis_index("c"))
    return [pltpu.make_async_remote_copy(src_ref=refs[sb].at[sfn(*me)], dst_ref=refs[db].at[dfn(*me)],
                                         send_sem=send_sems.at[t], recv_sem=recv_sems.at[t],
                                         device_id=_flip(me, rel), device_id_type=MESH)
            for t, (sb, sfn, db, dfn, rel) in enumerate(transfers)]


def _push_start(name, bufs, transfers, after=()):
    nb, na, nt = len(bufs), len(after), len(transfers)

    def body(*refs):
        send_sems, recv_sems, token = refs[nb + na], refs[nb + na + 1], refs[-1]
        for cp in _split_copies(refs[nb + na + 2:2 * nb + na + 2], transfers, send_sems, recv_sems):
            cp.start()
        token[...] = jnp.zeros(token.shape, token.dtype)

    res = pl.pallas_call(
        body, name=name,
        out_shape=(pltpu.SemaphoreType.DMA((nt,)), pltpu.SemaphoreType.DMA((nt,)),
                   *[pltpu.HBM(b.shape, b.dtype) for b in bufs], jax.ShapeDtypeStruct((8, LANE), F32)),
        in_specs=[_HBM] * nb + [pl.BlockSpec(memory_space=pl.ANY)] * na,
        out_specs=(_SEM, _SEM, *[_HBM] * nb, pl.BlockSpec(memory_space=pltpu.VMEM)),
        input_output_aliases={i: i + 2 for i in range(nb)},
        compiler_params=pltpu.CompilerParams(has_side_effects=_DATAFLOW),
    )(*[pltpu.with_memory_space_constraint(b, pltpu.HBM) for b in bufs], *after)
    return res[0], res[1], list(res[2:2 + nb]), res[-1]


def _push_wait(name, send_sems, recv_sems, bufs, transfers, after):
    nb = len(bufs)

    def body(*refs):
        for cp in _split_copies(refs[:nb], transfers, refs[nb], refs[nb + 1]):
            cp.wait_send()
            cp.wait_recv()

    res = pl.pallas_call(
        body, name=name, out_shape=[pltpu.HBM(b.shape, b.dtype) for b in bufs],
        in_specs=[_HBM] * nb + [_SEM, _SEM, pl.BlockSpec(memory_space=pl.ANY)], out_specs=[_HBM] * nb,
        input_output_aliases={i: i for i in range(nb)},
        compiler_params=pltpu.CompilerParams(has_side_effects=_DATAFLOW),
    )(*bufs, send_sems, recv_sems, after)
    return list(res)


ICI_RELS = ((1, 0, 0), (0, 1, 0), (1, 1, 0))
SIBLING = (0, 0, 1)


def _chip(x, y):
    return 2 * x + y


def _cast_into_slot(name, where, w, after=()):
    R, C = w.shape
    tile = _pick(R, 512, BF16_ROWS)

    def body(where_ref, w_ref, *rest):
        rest[-1][...] = w_ref[...].astype(BF16)

    return pl.pallas_call(
        body, name=name,
        grid_spec=pltpu.PrefetchScalarGridSpec(
            num_scalar_prefetch=1, grid=(R // tile,),
            in_specs=[pl.BlockSpec((tile, C), lambda i, wh: (i, 0))] + [pl.BlockSpec(memory_space=pl.ANY) for _ in after],
            out_specs=pl.BlockSpec((None, tile, C), lambda i, wh: (wh[1], i, 0))),
        out_shape=jax.ShapeDtypeStruct((N_CHIPS, R, C), BF16),
        compiler_params=_params(("parallel",)),
    )(where, w, *after)


BF16_ROWS = 16


def _rows_split(shape2):
    return (shape2[0] // 2) % BF16_ROWS == 0


def _half_shape(shape2):
    R, C = shape2
    return (R // 2, C) if _rows_split(shape2) else (R, C // 2)


def _half(shape2, c):
    R, C = shape2
    if _rows_split(shape2):
        return (pl.ds(c * (R // 2), R // 2), slice(None))
    return (slice(None), pl.ds(pl.multiple_of(c * (C // 2), LANE), C // 2))


def _gather_casts(tag, where, shards, after=()):
    return [_cast_into_slot(f"cast_w_{tag}{w}", where, s, after) for w, s in enumerate(shards)]


def _gather_start(tag, own, after):
    transfers = []
    for w in range(len(own)):
        def idx(x, y, c, shape2=own[w].shape[1:]):
            return (_chip(x, y), *_half(shape2, c))
        transfers += [(w, idx, w, idx, rel) for rel in ICI_RELS]
    send_sems, recv_sems, bufs, token = _push_start(f"gather_{tag}_start", own, transfers, after)
    return (tag, send_sems, recv_sems, bufs, transfers), token


def _forward_transfers(part):
    second = []
    for w, p in enumerate(part):
        for rel in ICI_RELS:
            def idx(x, y, c, shape2=p.shape[1:], rel=rel):
                return (_chip(x ^ rel[0], y ^ rel[1]), *_half(shape2, c))
            second.append((w, idx, w, idx, SIBLING))
    return second


def _gather_finish(handle, after):
    tag, send_sems, recv_sems, bufs, transfers = handle
    part = _push_wait(f"gather_{tag}_wait", send_sems, recv_sems, bufs, transfers, after)
    outs = [jax.ShapeDtypeStruct(p.shape, p.dtype) for p in part]
    second = [(("out", sb), sfn, db, dfn, rel) for sb, sfn, db, dfn, rel in _forward_transfers(part)]
    return _push(f"gather_{tag}_d2d", list(part), outs, second, aliases={w: w for w in range(len(part))})


def _gather_mid(handle, after):
    tag, send_sems, recv_sems, bufs, transfers = handle
    part = _push_wait(f"gather_{tag}_wait", send_sems, recv_sems, bufs, transfers, after)
    second = _forward_transfers(part)
    s2, r2, bufs2, token = _push_start(f"gather_{tag}_d2d_start", part, second)
    return (tag, s2, r2, bufs2, second), token


def _gather_end(handle, after):
    tag, send_sems, recv_sems, bufs, transfers = handle
    return _push_wait(f"gather_{tag}_d2d_wait", send_sems, recv_sems, bufs, transfers, after)


def _reduce_begin(tag, grads, after=()):
    n = len(grads)
    land = [lax.empty((N_CHIPS, *_half_shape(g.shape[1:])), F32) for g in grads]
    t1 = []
    for w in range(n):
        t1.append((w, lambda x, y, c, shape2=grads[w].shape[1:]: (slice(None), *_half(shape2, 1 - c)), n + w,
                   lambda x, y, c: (slice(None), slice(None), slice(None)), SIBLING))
    send_sems, recv_sems, bufs, token = _push_start(f"reduce_{tag}_d2d_start", list(grads) + land, t1, after)
    return (tag, send_sems, recv_sems, bufs, t1), token


def _reduce_start(handle, where, after, after_start=()):
    tag, send_sems, recv_sems, bufs, t1 = handle
    n = len(bufs) // 2
    res = _push_wait(f"reduce_{tag}_d2d_wait", send_sems, recv_sems, bufs, t1, after)
    grads, from_sib = res[:n], res[n:]
    pair = [_pair_sum(f"pair_sum_{tag}{w}", where, grads[w], from_sib[w]) for w in range(n)]
    land = [lax.empty((3, *_half_shape(g.shape[1:])), BF16) for g in grads]
    t3 = []
    for w in range(n):
        for r, rel in enumerate(ICI_RELS):
            t3.append((w, lambda x, y, c, rel=rel: (_chip(x ^ rel[0], y ^ rel[1]), slice(None), slice(None)), n + w,
                       lambda x, y, c, r=r: (r, slice(None), slice(None)), rel))
    send_sems, recv_sems, bufs, token = _push_start(f"reduce_{tag}_start", pair + land, t3, after_start)
    return (tag, send_sems, recv_sems, bufs, t3, list(grads), list(from_sib)), token


def _reduce_finish(handle, where, after):
    tag, send_sems, recv_sems, bufs, t3, grads, from_sib = handle
    n = len(grads)
    from_chips = _push_wait(f"reduce_{tag}_wait", send_sems, recv_sems, bufs, t3, after)[n:]
    mine = [_chip_sum(f"chip_sum_{tag}{w}", where, grads[w], from_sib[w], from_chips[w]) for w in range(n)]
    t5 = []
    for w in range(n):
        def idx(x, y, c, shape2=grads[w].shape[1:]):
            return _half(shape2, c)
        t5.append((w, idx, w, idx, SIBLING))
    send_sems, recv_sems, bufs, token = _push_start(f"reduce_{tag}_swap_start", mine, t5)
    return (tag, send_sems, recv_sems, bufs, t5), token


def _reduce_end(handle, after):
    tag, send_sems, recv_sems, bufs, t5 = handle
    return _push_wait(f"reduce_{tag}_swap_wait", send_sems, recv_sems, bufs, t5, after)


def _pair_sum(name, where, g, from_sib):
    _, hr, hc = from_sib.shape
    tile = _pick(hr, max(16, (1 << 19) // hc // 16 * 16), 16)
    nb = hr // tile
    slot = lambda k, wh: (wh[1] + 1 + k) % N_CHIPS
    if _rows_split(g.shape[1:]):
        mine = lambda k, i, wh: (slot(k, wh), wh[0] * nb + i, 0)
    else:
        mine = lambda k, i, wh: (slot(k, wh), i, wh[0])

    def body(where_ref, g_ref, s_ref, o_ref):
        o_ref[...] = (g_ref[...] + s_ref[...]).astype(BF16)

    return pl.pallas_call(
        body, name=name,
        grid_spec=pltpu.PrefetchScalarGridSpec(
            num_scalar_prefetch=1, grid=(N_CHIPS - 1, nb),
            in_specs=[pl.BlockSpec((None, tile, hc), mine),
                      pl.BlockSpec((None, tile, hc), lambda k, i, wh: (slot(k, wh), i, 0))],
            out_specs=pl.BlockSpec((None, tile, hc), lambda k, i, wh: (slot(k, wh), i, 0))),
        out_shape=jax.ShapeDtypeStruct((N_CHIPS, hr, hc), BF16),
        compiler_params=_params(("parallel", "parallel")),
    )(where, g, from_sib)


def _chip_sum(name, where, g, from_sib, from_chips):
    _, R, C = g.shape
    _, hr, hc = from_sib.shape
    tile = _pick(hr, max(16, (1 << 19) // hc // 16 * 16), 16)
    nb = hr // tile
    if _rows_split((R, C)):
        mine = lambda i, wh: (wh[0] * nb + i, 0)
    else:
        mine = lambda i, wh: (i, wh[0])

    def body(where_ref, g_ref, s_ref, r_ref, o_ref):
        acc = g_ref[...] + s_ref[...]
        for r in range(3):
            acc = acc + r_ref[r].astype(F32)
        o_ref[...] = acc

    return pl.pallas_call(
        body, name=name,
        grid_spec=pltpu.PrefetchScalarGridSpec(
            num_scalar_prefetch=1, grid=(nb,),
            in_specs=[pl.BlockSpec((None, tile, hc), lambda i, wh: (wh[1], *mine(i, wh))),
                      pl.BlockSpec((None, tile, hc), lambda i, wh: (wh[1], i, 0)),
                      pl.BlockSpec((3, tile, hc), lambda i, wh: (0, i, 0))],
            out_specs=pl.BlockSpec((tile, hc), mine)),
        out_shape=jax.ShapeDtypeStruct((R, C), F32),
        compiler_params=_params(("parallel",)),
    )(where, g, from_sib, from_chips)


def _allgather8(name, blk, after=()):
    m_per, n = blk.shape
    rels = [(0, 0, 1), (1, 0, 0), (0, 1, 0), (1, 1, 0), (1, 0, 1), (0, 1, 1), (1, 1, 1)]

    def body(x_ref, *rest):
        out_ref, send_sems, recv_sems = rest[len(after):]
        me = (lax.axis_index("x"), lax.axis_index("y"), lax.axis_index("c"))
        my_rows = out_ref.at[pl.ds((4 * me[0] + 2 * me[1] + me[2]) * m_per, m_per), :]
        out_ref[pl.ds((4 * me[0] + 2 * me[1] + me[2]) * m_per, m_per), :] = x_ref[...]
        copies = []
        for t, rel in enumerate(rels):
            cp = pltpu.make_async_remote_copy(src_ref=x_ref, dst_ref=my_rows, send_sem=send_sems.at[t],
                                              recv_sem=recv_sems.at[t], device_id=_flip(me, rel),
                                              device_id_type=MESH)
            cp.start()
            copies.append(cp)
        for cp in copies:
            cp.wait()

    return pl.pallas_call(
        body, name=name, out_shape=jax.ShapeDtypeStruct((N_DEV * m_per, n), blk.dtype),
        in_specs=[pl.BlockSpec(memory_space=pltpu.VMEM)] + [pl.BlockSpec(memory_space=pl.ANY) for _ in after],
        out_specs=pl.BlockSpec(memory_space=pltpu.VMEM),
        scratch_shapes=[pltpu.SemaphoreType.DMA((7,)), pltpu.SemaphoreType.DMA((7,))],
        compiler_params=pltpu.CompilerParams(has_side_effects=True, vmem_limit_bytes=VMEM_LIMIT),
    )(blk, *after)


def _rope_tables(seq_len, dim, reps):
    rows = seq_len // GRID_W
    t = np.arange(seq_len)
    row, col = (t // GRID_W).astype(np.float32), (t % GRID_W).astype(np.float32)
    half = dim // 2
    inv = jnp.asarray(ROPE_THETA, F32) ** (-jnp.arange(0, half, 2, dtype=F32) / half)
    ang_r = jnp.asarray(row)[:, None] * inv[None, :]
    ang_c = jnp.asarray(col)[:, None] * inv[None, :]
    ang = jnp.concatenate([ang_r, ang_r, ang_c, ang_c], axis=-1)
    cos, sin = jnp.cos(ang), jnp.sin(ang)
    low = (np.arange(dim) % (dim // 2)) < (dim // 4)
    s_lo = jnp.where(jnp.asarray(low)[None, :], -sin, 0.0)
    s_hi = jnp.where(jnp.asarray(low)[None, :], 0.0, sin)
    del rows
    return tuple(jnp.tile(a, (1, reps)) for a in (cos, s_lo, s_hi))


def _make_rope(shift):
    def up(v):
        return pltpu.roll(v, LANE - shift, 1)

    def down(v):
        return pltpu.roll(v, shift, 1)

    @jax.custom_vjp
    def rope(v, cos, s_lo, s_hi):
        return v * cos + up(v) * s_lo + down(v) * s_hi

    def fwd(v, cos, s_lo, s_hi):
        return rope(v, cos, s_lo, s_hi), (cos, s_lo, s_hi)

    def bwd(res, dy):
        cos, s_lo, s_hi = res
        return dy * cos + down(dy * s_lo) + up(dy * s_hi), jnp.zeros_like(cos), jnp.zeros_like(cos), jnp.zeros_like(cos)

    rope.defvjp(fwd, bwd)
    return rope


_rope_a = _make_rope(HEAD_DIM // 4)
_rope_b = _make_rope(QK_ROPE // 4)


def _prep(segs, tabs, g_q, g_k, g_ckv):
    qa, ka, va, qn, qp, ckv, kpe = segs
    ta, tb = tabs
    q_a = jnp.concatenate([_rope_a(_rn(s) * g_q, *ta) for s in qa], axis=1)
    k_a = jnp.concatenate([_rope_a(_rn(s) * g_k, *ta) for s in ka], axis=1)
    cat = []
    for h in range(HB):
        cat += [qn[h], _rope_b(qp[h], *tb)]
    q_cat = jnp.concatenate(cat, axis=1)
    return q_a, k_a, va, q_cat, _rn(ckv) * g_ckv, _rope_b(kpe, *tb)


def _split_proj(proj):
    qa = [proj[:, O_QA + h * LANE:O_QA + (h + 1) * LANE] for h in range(HA)]
    ka = [proj[:, O_KA + h * LANE:O_KA + (h + 1) * LANE] for h in range(HKV)]
    va = proj[:, O_VA:O_QN]
    qn = [proj[:, O_QN + h * LANE:O_QN + (h + 1) * LANE] for h in range(HB)]
    qp = [proj[:, O_QP + h * LANE:O_QP + (h + 1) * LANE] for h in range(HB)]
    return qa, ka, va, qn, qp, proj[:, O_CKV:O_KPE], proj[:, O_KPE:W_INP]


def _permute_w_in(nat):
    K = nat.shape[1]
    qb = nat[W_QA + 2 * W_KA:W_QA + 2 * W_KA + W_QB].reshape(HB, QK_B, K)
    qp = jnp.concatenate([qb[:, QK_NOPE:], jnp.zeros((HB, LANE - QK_ROPE, K), nat.dtype)], axis=1)
    tail = nat[W_QA + 2 * W_KA + W_QB:]
    return jnp.concatenate([nat[:O_QN], qb[:, :QK_NOPE].reshape(HB * QK_NOPE, K), qp.reshape(HB * LANE, K),
                            tail, jnp.zeros((LANE - QK_ROPE, K), nat.dtype)], axis=0)


def _unpermute_pieces():
    pieces = [(0, 0, O_QN)]
    for h in range(HB):
        pieces.append((O_QN + h * QK_NOPE, O_QN + h * QK_B, QK_NOPE))
        pieces.append((O_QP + h * LANE, O_QN + h * QK_B + QK_NOPE, QK_ROPE))
    pieces.append((O_CKV, O_QN + W_QB, KV_RANK))
    pieces.append((O_KPE, O_QN + W_QB + KV_RANK, QK_ROPE))
    return pieces


def kernel(x, c, w_ada, b_ada, g_pre_attn, w_in, g_q_a, g_k_a, g_ckv, w_kv_b, g_out_a, g_out_b, w_out, g_post_attn, g_pre_mlp, w_mlp_in, w_mlp_out, g_post_mlp, loss_target, m_w_ada, m_b_ada, m_g_pre_attn, m_w_in, m_g_q_a, m_g_k_a, m_g_ckv, m_w_kv_b, m_g_out_a, m_g_out_b, m_w_out, m_g_post_attn, m_g_pre_mlp, m_w_mlp_in, m_w_mlp_out, m_g_post_mlp, v_w_ada, v_b_ada, v_g_pre_attn, v_w_in, v_g_q_a, v_g_k_a, v_g_ckv, v_w_kv_b, v_g_out_a, v_g_out_b, v_w_out, v_g_post_attn, v_g_pre_mlp, v_w_mlp_in, v_w_mlp_out, v_g_post_mlp):
    S, D = x.shape[1], x.shape[2]
    x2d, tgt = x[0], loss_target[0]
    ix, iy, ic = lax.axis_index("x"), lax.axis_index("y"), lax.axis_index("c")
    j_me = _chip(ix, iy)
    e_me = 4 * ix + 2 * iy + ic
    T = _pick(S, 256, 8)
    TB = _pick(S, 256, 8)

    where = jnp.stack([ic, j_me]).astype(jnp.int32)
    DFF = w_mlp_out.shape[1] * N_CHIPS

    c_all = _allgather8("gather_c", c.reshape(8, D // 8)).reshape(N_DEV, D)
    w_in_t = jnp.swapaxes(w_in[0], 0, 1)
    h_qkv, tok_a = _gather_start("qkv", _gather_casts("qkv", where, [w_in_t, w_kv_b[0]]), (c_all,))
    later = [_gather_casts(tag, where, [w[0]], (tok_a,))
             for tag, w in (("out", w_out), ("up", w_mlp_in), ("down", w_mlp_out))]

    def silu_fn(cc):
        return (cc * (1.0 / (1.0 + jnp.exp(-cc))),), ()

    (c_act,), _ = _rowwise("silu_c", silu_fn, [c_all], [], [(D, BF16)], [], N_DEV)
    ca_pad = jnp.concatenate([c_act, jnp.zeros_like(c_act)], axis=0)
    n_mod = w_ada.shape[2]
    b_cols = lax.dynamic_slice(b_ada, (0, j_me * n_mod), (1, n_mod))
    (mod_part,) = _mm_nn("mod_mm", ca_pad, w_ada, [F32], epilogue=lambda acc, b: (acc + b,),
                         extras=(jnp.broadcast_to(b_cols, (16, n_mod)),))
    mod_all = _allgather8("gather_mod", mod_part[:8] + tok_a[0, 0],
                          after=[c_[0] for c_ in later]).reshape(N_DEV, 8, n_mod)
    mod = jnp.concatenate([lax.dynamic_slice(mod_all, (2 * j, e_me, 0), (1, 1, n_mod))[0] for j in range(N_CHIPS)],
                          axis=1)

    h_out, tok_b = _gather_start("out", later[0], (mod_all,))
    h_up, tok_c = _gather_start("up", later[1], (tok_b,))
    h_down, tok_d = _gather_start("down", later[2], (tok_c,))
    mod = mod + tok_d[0, 0]
    sh_a, sc_a, gt_a, sh_m, sc_m, gt_m = [mod[:, i * D:(i + 1) * D] for i in range(6)]

    tabs_a = _rope_tables(S, HEAD_DIM, 1)
    tabs_b = _rope_tables(S, QK_ROPE, LANE // QK_ROPE)

    def pre_attn_fn(xt, g, sc, sh):
        return (_rn(xt) * g * (1.0 + sc) + sh,), ()

    (h_b,), _ = _rowwise("pre_attn", pre_attn_fn, [x2d], [g_pre_attn, sc_a, sh_a], [(D, BF16)], [], T)
    g_in, g_kvb = _gather_finish(h_qkv, h_b)
    wpt = _permute_w_in(g_in.reshape(W_IN, D))[None]
    (proj,) = _mm_nt("proj_mm", h_b, wpt, [F32])

    def prep_fn(pt, ca_, sla, sha, cb_, slb, shb, gq, gk, gc):
        return _prep(_split_proj(pt), ((ca_, sla, sha), (cb_, slb, shb)), gq, gk, gc), ()

    (qa_b, ka_b, va_b, qcat_b, ckvn_b, kpe_b), _ = _rowwise(
        "prep", prep_fn, [proj, *tabs_a, *tabs_b], [g_q_a, g_k_a, g_ckv],
        [(W_QA, BF16), (W_KA, BF16), (W_KA, BF16), (2 * HB * LANE, BF16), (KV_RANK, BF16), (LANE, BF16)], [], T)
    (kv,) = _mm_nn("kv_mm", ckvn_b, g_kvb, [F32])

    def kcat_fn(kvt, kpet):
        kc, vs = [], []
        for h in range(HB):
            kc += [kvt[:, 2 * h * LANE:(2 * h + 1) * LANE], kpet.astype(F32)]
            vs.append(kvt[:, (2 * h + 1) * LANE:(2 * h + 2) * LANE])
        return (jnp.concatenate(kc, axis=1), jnp.concatenate(vs, axis=1)), ()

    (kcat_b, vb_b), _ = _rowwise("kcat", kcat_fn, [kv, kpe_b], [], [(2 * HB * LANE, BF16), (HB * V_DIM, BF16)], [], T)

    TQ = _pick(S, 512, 16)
    sc_a_ = 1.0 / math.sqrt(HEAD_DIM)
    sc_b_ = 1.0 / math.sqrt(QK_B)
    o_a, lse_a = _attn_fwd("attn_a_fwd", qa_b, ka_b, va_b, HA, HA // HKV, HEAD_DIM, HEAD_DIM, sc_a_, TQ)
    m_out, tok_m = _gather_mid(h_out, o_a)
    o_b, lse_b = _attn_fwd("attn_b_fwd", qcat_b, kcat_b, vb_b, HB, 1, 2 * LANE, V_DIM, sc_b_, TQ, after=(tok_m,))
    m_up, tok_m = _gather_mid(h_up, o_b)

    def mix_fn(oa, ob, ga, gb):
        return (jnp.concatenate([_rn(oa) * ga, _rn(ob) * gb], axis=1),), ()

    (on_b,), _ = _rowwise("mix_norm", mix_fn, [o_a, o_b], [g_out_a, g_out_b], [(2 * W_QA, BF16)], [], T,
                          after=(tok_m,))
    (g_wout,) = _gather_end(m_out, on_b)
    w_out_f = g_wout.reshape(1, N_CHIPS * g_wout.shape[1], D)
    (o2,) = _mm_nn("out_mm", on_b, w_out_f, [F32])

    def mid(xt, o2t, gta, gpa, gpm, scm, shm):
        x1 = xt + gta * (_rn(o2t) * gpa)
        return x1, _rn(x1) * gpm * (1.0 + scm) + shm

    def mid_fn(*a):
        return mid(*a), ()

    mid_vecs = [gt_a, g_post_attn, g_pre_mlp, sc_m, sh_m]
    (x1, h2_b), _ = _rowwise("mid", mid_fn, [x2d, o2], mid_vecs, [(D, F32), (D, BF16)], [], T)
    (g_mlp_in,) = _gather_end(m_up, h2_b)
    relu_b, a_b = _mm_nn("mlp_in_mm", h2_b, g_mlp_in, [BF16, BF16],
                         epilogue=lambda acc: (jnp.maximum(acc, 0.0), jnp.square(jnp.maximum(acc, 0.0))))
    (g_mlp_out,) = _gather_finish(h_down, a_b)
    w_mlp_out_f = g_mlp_out.reshape(1, DFF, D)
    (y,) = _mm_nn("mlp_out_mm", a_b, w_mlp_out_f, [F32])

    def last_fn(yt, x1t, tt, gtm, gpo):
        n, r = _rn_parts(yt)
        err = x1t + gtm * (n * gpo) - tt
        dx2 = err * (1.0 / D)
        loss = 0.5 * jnp.sum(jnp.mean(err * err, axis=-1))
        s = _rows(dx2 * n)
        return (_rn_bwd(n, r, dx2 * (gtm * gpo)), dx2), (jnp.full((1, LANE), loss, F32), gpo * s, gtm * s)

    (dy_b, dx2), (loss_v, d_gt_m, d_g_post_mlp) = _rowwise(
        "loss_bwd", last_fn, [y, x1, tgt], [gt_m, g_post_mlp], [(D, BF16), (D, F32)], [LANE, D, D], TB)

    gw_mlp_out = _mm_tn("gw_mlp_out_mm", a_b, dy_b, 1).reshape(N_CHIPS, DFF // N_CHIPS, D)
    b_down, tok = _reduce_begin("down", [gw_mlp_out])
    (du_b,) = _mm_nt("d_mlp_out_mm", dy_b, w_mlp_out_f, [BF16],
                     epilogue=lambda acc, rt: (acc * (2.0 * rt.astype(F32)),), extras=(relu_b,), after=(tok,))
    r_down, tok = _reduce_start(b_down, where, du_b)
    gw_mlp_in = _mm_tn("gw_mlp_in_mm", h2_b, du_b, N_CHIPS, after=(tok,))
    b_up, tok = _reduce_begin("up", [gw_mlp_in])
    (dh2,) = _mm_nt("d_mlp_in_mm", du_b, g_mlp_in, [F32], after=(tok,))
    r_up, tok_up = _reduce_start(b_up, where, dh2)

    def mid_bwd_fn(xt, o2t, dh2t, dx2t, gta, gpa, gpm, scm, shm):
        n2, r2 = _rn_parts(o2t)
        n1, r1 = _rn_parts(xt + gta * (n2 * gpa))
        dx1 = dx2t + _rn_bwd(n1, r1, dh2t * (gpm * (1.0 + scm)))
        s1, s2 = _rows(dh2t * n1), _rows(dx1 * n2)
        do2 = _rn_bwd(n2, r2, dx1 * (gta * gpa))
        return (dx1, do2), (gpa * s2, gta * s2, (1.0 + scm) * s1, gpm * s1, _rows(dh2t))

    (dx1, do2_b), (d_gt_a, d_g_post_attn, d_g_pre_mlp, d_sc_m, d_sh_m) = _rowwise(
        "mid_bwd", mid_bwd_fn, [x2d, o2, dh2, dx2], mid_vecs, [(D, F32), (D, BF16)], [D] * 5, TB, after=(tok_up,))

    (d_on,) = _mm_nt("d_out_mm", do2_b, w_out_f, [F32])
    gw_out = _mm_tn("gw_out_mm", on_b, do2_b, 1).reshape(N_CHIPS, w_out.shape[1], D)

    def mix_bwd_fn(oa, ob, dont, ga, gb):
        na, ra = _rn_parts(oa)
        nb, rb = _rn_parts(ob)
        da, db = dont[:, :W_QA], dont[:, W_QA:]
        return (_rn_bwd(na, ra, da * ga), _rn_bwd(nb, rb, db * gb)), (_rows(da * na), _rows(db * nb))

    (doa_b, dob_b), (d_g_out_a, d_g_out_b) = _rowwise(
        "mix_bwd", mix_bwd_fn, [o_a, o_b, d_on], [g_out_a, g_out_b], [(W_QA, BF16), (HB * V_DIM, BF16)],
        [W_QA, HB * V_DIM], T)

    dqa, dka, dva = _attn_bwd("attn_a_bwd", qa_b, ka_b, va_b, doa_b, o_a, lse_a, HA, HA // HKV, HEAD_DIM, HEAD_DIM,
                              sc_a_, TQ)
    dqcat, dkcat, dvb = _attn_bwd("attn_b_bwd", qcat_b, kcat_b, vb_b, dob_b, o_b, lse_b, HB, 1, 2 * LANE, V_DIM,
                                  sc_b_, TQ)

    def kcat_bwd_fn(dkc, dvt):
        dkv, dkpe = [], None
        for h in range(HB):
            dkv += [dkc[:, 2 * h * LANE:(2 * h + 1) * LANE], dvt[:, h * V_DIM:(h + 1) * V_DIM]]
            piece = dkc[:, (2 * h + 1) * LANE:(2 * h + 2) * LANE]
            dkpe = piece if dkpe is None else dkpe + piece
        return (jnp.concatenate(dkv, axis=1), dkpe), ()

    (dkv_b, dkpe), _ = _rowwise("kcat_bwd", kcat_bwd_fn, [dkcat, dvb], [], [(2 * HB * LANE, BF16), (LANE, F32)], [], T)
    (d_ckvn,) = _mm_nt("d_kv_mm", dkv_b, g_kvb, [F32])
    gw_kvb = _mm_tn("gw_kv_mm", ckvn_b, dkv_b, N_CHIPS)
    b_kvb, tok_kvb = _reduce_begin("okv", [gw_out, gw_kvb])

    def prep_bwd_fn(pt, dqat, dkat, dvat, dqct, dckt, dkpt, ca_, sla, sha, cb_, slb, shb, gq, gk, gc):
        tabs = ((ca_, sla, sha), (cb_, slb, shb))
        _, vjp = jax.vjp(lambda sg, a1, a2, a3: _prep(sg, tabs, a1, a2, a3), _split_proj(pt), gq, gk, gc)
        (dqa_s, dka_s, dva_s, dqn_s, dqp_s, dck_s, dkp_s), dgq, dgk, dgc = vjp((dqat, dkat, dvat, dqct, dckt, dkpt))
        dproj = jnp.concatenate([*dqa_s, *dka_s, dva_s, *dqn_s, *dqp_s, dck_s, dkp_s], axis=1)
        return (dproj,), (dgq, dgk, dgc)

    (dproj_b,), (d_g_q_a, d_g_k_a, d_g_ckv) = _rowwise(
        "prep_bwd", prep_bwd_fn, [proj, dqa, dka, dva, dqcat, d_ckvn, dkpe, *tabs_a, *tabs_b],
        [g_q_a, g_k_a, g_ckv], [(W_INP, BF16)], [HEAD_DIM, HEAD_DIM, KV_RANK], T, after=(tok_kvb,))

    r_kvb, tok = _reduce_start(b_kvb, where, dproj_b)
    (dh,) = _mm_nn("d_proj_mm", dproj_b, wpt, [F32], after=(tok,))
    gw_in = _mm_tn_rows("gw_in_mm", dproj_b, h_b, W_IN, _unpermute_pieces()).reshape(N_CHIPS, W_IN // N_CHIPS, D)
    b_in, tok = _reduce_begin("in", [gw_in])

    def pre_attn_bwd_fn(xt, dht, dx1t, g, sc, sh):
        n, r = _rn_parts(xt)
        s = _rows(dht * n)
        return (dx1t + _rn_bwd(n, r, dht * (g * (1.0 + sc))),), ((1.0 + sc) * s, g * s, _rows(dht))

    (grad_x,), (d_g_pre_attn, d_sc_a, d_sh_a) = _rowwise(
        "pre_attn_bwd", pre_attn_bwd_fn, [x2d, dh, dx1], [g_pre_attn, sc_a, sh_a], [(D, F32)], [D] * 3, TB,
        after=(tok,))

    small_names = ["b_ada", "g_pre_attn", "g_q_a", "g_k_a", "g_ckv", "g_out_a", "g_out_b", "g_post_attn",
                   "g_pre_mlp", "g_post_mlp"]
    small_local = jnp.concatenate([d_sh_a, d_sc_a, d_gt_a, d_sh_m, d_sc_m, d_gt_m, d_g_pre_attn, d_g_q_a, d_g_k_a,
                                   d_g_ckv, d_g_out_a, d_g_out_b, d_g_post_attn, d_g_pre_mlp, d_g_post_mlp,
                                   loss_v[:, :8]], axis=1)
    n_sent = small_local.shape[1]
    n_small = n_sent - 8
    small_all = _allgather8("gather_small", small_local.reshape(8, n_sent // 8)).reshape(N_DEV, n_sent)

    def sum8_fn(t):
        return (), (jnp.sum(t, axis=0, keepdims=True),)

    r_in, tok_in = _reduce_start(b_in, where, grad_x, (small_all,))
    _, (sent_sum,) = _rowwise("sum_small", sum8_fn, [small_all], [], [], [n_sent], N_DEV, after=(tok_in,))
    small_sum, loss = sent_sum[:, :n_small], sent_sum[0, n_small]
    dmod_cols = lax.dynamic_slice(small_all, (0, j_me * n_mod), (N_DEV, n_mod))
    dmod_pad = jnp.concatenate([dmod_cols, jnp.zeros_like(dmod_cols)], axis=0).astype(BF16)
    gw_ada = _mm_tn("gw_ada_mm", ca_pad, dmod_pad, 1)[0]

    weights = dict(w_ada=w_ada, b_ada=b_ada, g_pre_attn=g_pre_attn, w_in=w_in, g_q_a=g_q_a, g_k_a=g_k_a, g_ckv=g_ckv,
                   w_kv_b=w_kv_b, g_out_a=g_out_a, g_out_b=g_out_b, w_out=w_out, g_post_attn=g_post_attn,
                   g_pre_mlp=g_pre_mlp, w_mlp_in=w_mlp_in, w_mlp_out=w_mlp_out, g_post_mlp=g_post_mlp)
    ms = dict(w_ada=m_w_ada, b_ada=m_b_ada, g_pre_attn=m_g_pre_attn, w_in=m_w_in, g_q_a=m_g_q_a, g_k_a=m_g_k_a,
              g_ckv=m_g_ckv, w_kv_b=m_w_kv_b, g_out_a=m_g_out_a, g_out_b=m_g_out_b, w_out=m_w_out,
              g_post_attn=m_g_post_attn, g_pre_mlp=m_g_pre_mlp, w_mlp_in=m_w_mlp_in, w_mlp_out=m_w_mlp_out,
              g_post_mlp=m_g_post_mlp)
    vs = dict(w_ada=v_w_ada, b_ada=v_b_ada, g_pre_attn=v_g_pre_attn, w_in=v_w_in, g_q_a=v_g_q_a, g_k_a=v_g_k_a,
              g_ckv=v_g_ckv, w_kv_b=v_w_kv_b, g_out_a=v_g_out_a, g_out_b=v_g_out_b, w_out=v_w_out,
              g_post_attn=v_g_post_attn, g_pre_mlp=v_g_pre_mlp, w_mlp_in=v_w_mlp_in, w_mlp_out=v_w_mlp_out,
              g_post_mlp=v_g_post_mlp)
    order = list(weights)
    grads, deltas, new_m, new_v = {}, {}, {}, {}

    def update(name, g, after=()):
        view = (lambda t: jnp.swapaxes(t, 0, 1)) if name == "w_in" else (lambda t: t)
        d_, m_, v_, *g_ = _adamw("adamw_" + name, view(weights[name][0]), g, view(ms[name][0]), view(vs[name][0]),
                                 after, rewrite_g=name != "w_ada")
        grads[name], deltas[name], new_m[name], new_v[name] = (view(t)[None] for t in ((g_ or [g])[0], d_, m_, v_))
        return d_

    chain = ((("w_mlp_out",), r_down), (("w_mlp_in",), r_up), (("w_out", "w_kv_b"), r_kvb), (("w_in",), r_in))
    swap, tok = _reduce_finish(chain[0][1], where, small_sum)
    last = update("w_ada", gw_ada, (tok,))
    for k, (names, _) in enumerate(chain):
        gs = _reduce_end(swap, last)
        if k + 1 < len(chain):
            swap, tok = _reduce_finish(chain[k + 1][1], where, gs[0])
        for name, g in zip(names, gs):
            last = update(name, g, (tok,))

    cat = lambda d: jnp.concatenate([d[n] for n in small_names], axis=1)
    d_, m_, v_ = _adamw("adamw_small", cat(weights), small_sum, cat(ms), cat(vs))
    off = 0
    for n in small_names:
        w_ = weights[n].shape[1]
        grads[n], deltas[n], new_m[n], new_v[n] = (a[:, off:off + w_] for a in (small_sum, d_, m_, v_))
        off += w_

    return (loss, grad_x[None], *[grads[n] for n in order], *[deltas[n] for n in order],
            *[new_m[n] for n in order], *[new_v[n] for n in order])
```

```python
import functools
import math

import numpy as np
import jax
import jax.numpy as jnp
from jax import lax
from jax.experimental import pallas as pl
from jax.experimental.pallas import tpu as pltpu

F32 = jnp.float32
BF16 = jnp.bfloat16
MESH = pl.DeviceIdType.MESH

EPS = 1e-6
GRID_W = 64
ROPE_THETA = 10000.0
HEAD_DIM = 128
HA = 8
HKV = 2
HB = 8
QK_NOPE = 128
QK_ROPE = 64
V_DIM = 128
KV_RANK = 512
QK_B = QK_NOPE + QK_ROPE
W_QA = HA * HEAD_DIM
W_KA = HKV * HEAD_DIM
W_QB = HB * QK_B
W_IN = W_QA + 2 * W_KA + W_QB + KV_RANK + QK_ROPE
N_CHIPS = 4
N_DEV = 8

LANE = 128
O_QA = 0
O_KA = O_QA + W_QA
O_VA = O_KA + W_KA
O_QN = O_VA + W_KA
O_QP = O_QN + HB * QK_NOPE
O_CKV = O_QP + HB * LANE
O_KPE = O_CKV + KV_RANK
W_INP = O_KPE + LANE

ADAM_LR = 0.001
ADAM_B1 = 0.9
ADAM_B2 = 0.999
ADAM_EPS = 1e-08
ADAM_WD = 0.01
ADAM_STEP = 10

VMEM_LIMIT = 56 * 1024 * 1024
HEADS_PER_STEP = 2


def _params(sem):
    return pltpu.CompilerParams(dimension_semantics=sem, vmem_limit_bytes=VMEM_LIMIT)


def _pick(dim, target, unit):
    best = None
    t = unit
    while t <= min(dim, target):
        if dim % t == 0:
            best = t
        t += unit
    return dim if best is None else best


def _rn(x):
    return x * lax.rsqrt(jnp.mean(x * x, axis=-1, keepdims=True) + EPS)


def _rn_parts(x):
    r = lax.rsqrt(jnp.mean(x * x, axis=-1, keepdims=True) + EPS)
    return x * r, r


def _rn_bwd(n, r, dn):
    return r * (dn - n * jnp.mean(dn * n, axis=-1, keepdims=True))


def _rows(t):
    return jnp.sum(t, axis=0, keepdims=True)


def _rowwise(name, fn, rows, vecs, row_outs, vec_outs, tile, after=()):
    S = rows[0].shape[0]
    assert S % tile == 0
    n_r, n_v, n_ro, n_vo, n_a = len(rows), len(vecs), len(row_outs), len(vec_outs), len(after)

    def body(*refs):
        r_in = refs[:n_r]
        v_in = refs[n_r:n_r + n_v]
        r_out = refs[n_r + n_v + n_a:n_r + n_v + n_a + n_ro]
        v_out = refs[n_r + n_v + n_a + n_ro:]
        ro, vo = fn(*[r[...] for r in r_in], *[v[...] for v in v_in])
        for ref, val in zip(r_out, ro):
            ref[...] = val.astype(ref.dtype)
        if n_vo:
            @pl.when(pl.program_id(0) == 0)
            def _():
                for ref in v_out:
                    ref[...] = jnp.zeros(ref.shape, ref.dtype)
            for ref, val in zip(v_out, vo):
                ref[...] += val

    in_specs = [pl.BlockSpec((tile, a.shape[1]), lambda i: (i, 0)) for a in rows]
    in_specs += [pl.BlockSpec(a.shape, lambda i: (0, 0)) for a in vecs]
    in_specs += [pl.BlockSpec(memory_space=pl.ANY) for _ in after]
    out_specs = [pl.BlockSpec((tile, w), lambda i: (i, 0)) for w, _ in row_outs]
    out_specs += [pl.BlockSpec((1, w), lambda i: (0, 0)) for w in vec_outs]
    out_shape = [jax.ShapeDtypeStruct((S, w), dt) for w, dt in row_outs]
    out_shape += [jax.ShapeDtypeStruct((1, w), F32) for w in vec_outs]
    res = pl.pallas_call(
        body, name=name, grid=(S // tile,), in_specs=in_specs, out_specs=out_specs, out_shape=out_shape,
        compiler_params=_params(("arbitrary",)),
    )(*rows, *vecs, *after)
    return res[:n_ro], res[n_ro:]


CONTRACT_TILE = 2048
MM_TILE_BYTES = 40 * 1024 * 1024


def _rows_that_fit(M, tile_bytes):
    t = _pick(M, 1024, 16)
    while t % 32 == 0 and tile_bytes(t) > MM_TILE_BYTES:
        t //= 2
    return t


def _accumulate(acc, part, step, n_steps, finish):
    if n_steps == 1:
        finish(part)
        return

    @pl.when(step == 0)
    def _():
        acc[...] = part

    @pl.when((step > 0) & (step < n_steps - 1))
    def _():
        acc[...] += part

    @pl.when(step == n_steps - 1)
    def _():
        finish(acc[...] + part)


def _mm_nn(name, a, b3, out_dtypes, epilogue=None, extras=(), after=()):
    M, K = a.shape
    nB, K2, Nsh = b3.shape
    assert K == K2
    N = nB * Nsh
    tn, tk = _pick(Nsh, 1536, LANE), _pick(K, CONTRACT_TILE, LANE)
    out_bytes = sum(jnp.dtype(dt).itemsize for dt in out_dtypes) + sum(e.dtype.itemsize for e in extras)
    tm = _rows_that_fit(M, lambda t: 2 * (t * tk * a.dtype.itemsize + tk * tn * b3.dtype.itemsize)
                        + t * tn * (4 * (K > tk) + 2 * out_bytes))
    nps, nk, ne, no, na = Nsh // tn, K // tk, len(extras), len(out_dtypes), len(after)

    def body(a_ref, b_ref, *rest):
        e_refs, o_refs = rest[:ne], rest[ne + na:ne + na + no]

        def finish(total):
            vals = (total,) if epilogue is None else epilogue(total, *[e[...] for e in e_refs])
            for ref, val in zip(o_refs, vals):
                ref[...] = val.astype(ref.dtype)

        part = jnp.dot(a_ref[...].astype(BF16), b_ref[...].astype(BF16), preferred_element_type=F32)
        _accumulate(rest[-1], part, pl.program_id(2), nk, finish)

    in_specs = [pl.BlockSpec((tm, tk), lambda i, j, k: (i, k)),
                pl.BlockSpec((None, tk, tn), lambda i, j, k: (j // nps, k, j % nps))]
    in_specs += [pl.BlockSpec((tm, tn), lambda i, j, k: (i, j)) for _ in extras]
    in_specs += [pl.BlockSpec(memory_space=pl.ANY) for _ in after]
    return pl.pallas_call(
        body, name=name, grid=(M // tm, N // tn, nk), in_specs=in_specs,
        out_specs=[pl.BlockSpec((tm, tn), lambda i, j, k: (i, j)) for _ in out_dtypes],
        out_shape=[jax.ShapeDtypeStruct((M, N), dt) for dt in out_dtypes],
        scratch_shapes=[pltpu.VMEM((tm, tn), F32)] if nk > 1 else [],
        compiler_params=_params(("parallel", "parallel", "arbitrary")),
    )(a, b3, *extras, *after)


def _mm_nt(name, a, b3, out_dtypes, epilogue=None, extras=(), after=()):
    M, N = a.shape
    nB, K, Nsh = b3.shape
    assert N == nB * Nsh
    to, tn = _pick(K, 1536, LANE), _pick(Nsh, CONTRACT_TILE, LANE)
    out_bytes = sum(jnp.dtype(dt).itemsize for dt in out_dtypes) + sum(e.dtype.itemsize for e in extras)
    tm = _rows_that_fit(M, lambda t: 2 * (t * tn * a.dtype.itemsize + to * tn * b3.dtype.itemsize)
                        + t * to * (4 * (N > tn) + 2 * out_bytes))
    nps, nn, ne, no, na = Nsh // tn, N // tn, len(extras), len(out_dtypes), len(after)

    def body(a_ref, b_ref, *rest):
        e_refs, o_refs = rest[:ne], rest[ne + na:ne + na + no]

        def finish(total):
            vals = (total,) if epilogue is None else epilogue(total, *[e[...] for e in e_refs])
            for ref, val in zip(o_refs, vals):
                ref[...] = val.astype(ref.dtype)

        part = lax.dot_general(a_ref[...].astype(BF16), b_ref[...].astype(BF16),
                               (((1,), (1,)), ((), ())), preferred_element_type=F32)
        _accumulate(rest[-1], part, pl.program_id(2), nn, finish)

    in_specs = [pl.BlockSpec((tm, tn), lambda i, o, n: (i, n)),
                pl.BlockSpec((None, to, tn), lambda i, o, n: (n // nps, o, n % nps))]
    in_specs += [pl.BlockSpec((tm, to), lambda i, o, n: (i, o)) for _ in extras]
    in_specs += [pl.BlockSpec(memory_space=pl.ANY) for _ in after]
    return pl.pallas_call(
        body, name=name, grid=(M // tm, K // to, nn), in_specs=in_specs,
        out_specs=[pl.BlockSpec((tm, to), lambda i, o, n: (i, o)) for _ in out_dtypes],
        out_shape=[jax.ShapeDtypeStruct((M, K), dt) for dt in out_dtypes],
        scratch_shapes=[pltpu.VMEM((tm, to), F32)] if nn > 1 else [],
        compiler_params=_params(("parallel", "parallel", "arbitrary")),
    )(a, b3, *extras, *after)


def _mm_tn(name, a, g, nB, after=()):
    M, K = a.shape
    M2, N = g.shape
    assert M == M2 and N % nB == 0
    Nsh = N // nB
    tk, tn, tm = _pick(K, 1536, LANE), _pick(Nsh, 1024, LANE), _pick(M, CONTRACT_TILE, 16)
    nps, nm = Nsh // tn, M // tm

    def body(a_ref, g_ref, *rest):
        o_ref = rest[len(after)]

        def finish(total):
            o_ref[...] = total

        part = lax.dot_general(a_ref[...].astype(BF16), g_ref[...].astype(BF16),
                               (((0,), (0,)), ((), ())), preferred_element_type=F32)
        _accumulate(rest[-1], part, pl.program_id(2), nm, finish)

    return pl.pallas_call(
        body, name=name, grid=(K // tk, N // tn, nm),
        in_specs=[pl.BlockSpec((tm, tk), lambda k, n, m: (m, k)),
                  pl.BlockSpec((tm, tn), lambda k, n, m: (m, n))] + [pl.BlockSpec(memory_space=pl.ANY) for _ in after],
        out_specs=pl.BlockSpec((None, tk, tn), lambda k, n, m: (n // nps, k, n % nps)),
        out_shape=jax.ShapeDtypeStruct((nB, K, Nsh), F32),
        scratch_shapes=[pltpu.VMEM((tk, tn), F32)] if nm > 1 else [],
        compiler_params=_params(("parallel", "parallel", "arbitrary")),
    )(a, g, *after)


def _mm_tn_rows(name, a, g, n_rows, pieces):
    M, K = a.shape
    M2, N = g.shape
    assert M == M2
    tk, tn, tm = _pick(K, 1536, LANE), _pick(N, 512, LANE), _pick(M, CONTRACT_TILE, 16)
    nk, nm = K // tk, M // tm

    def body(a_ref, g_ref, o_ref, *acc):
        kk = pl.program_id(1)

        def finish(total):
            for t in range(nk):
                @pl.when(kk == t)
                def _(t=t):
                    for src, dst, n in pieces:
                        lo, hi = max(src, t * tk), min(src + n, (t + 1) * tk)
                        if lo < hi:
                            o_ref[dst + lo - src:dst + hi - src, :] = total[lo - t * tk:hi - t * tk, :]

        part = lax.dot_general(a_ref[...].astype(BF16), g_ref[...].astype(BF16),
                               (((0,), (0,)), ((), ())), preferred_element_type=F32)
        _accumulate(acc[0] if acc else None, part, pl.program_id(2), nm, finish)

    return pl.pallas_call(
        body, name=name, grid=(N // tn, nk, nm),
        in_specs=[pl.BlockSpec((tm, tk), lambda n, k, m: (m, k)),
                  pl.BlockSpec((tm, tn), lambda n, k, m: (m, n))],
        out_specs=pl.BlockSpec((n_rows, tn), lambda n, k, m: (0, n)),
        out_shape=jax.ShapeDtypeStruct((n_rows, N), F32),
        scratch_shapes=[pltpu.VMEM((tk, tn), F32)] if nm > 1 else [],
        compiler_params=_params(("arbitrary", "arbitrary", "arbitrary")),
    )(a, g)


def _attn_fwd(name, q, k, v, n_heads, group, dqk, dv, scale, tq, after=()):
    S = q.shape[0]
    hp = 2 * HEADS_PER_STEP
    kvp = max(1, hp // group)

    def body(q_ref, k_ref, v_ref, *rest):
        o_ref, lse_ref = rest[-2:]
        for j in range(hp):
            jk = j * kvp // hp
            s = lax.dot_general(q_ref[:, j * dqk:(j + 1) * dqk], k_ref[:, jk * dqk:(jk + 1) * dqk],
                                (((1,), (1,)), ((), ())), preferred_element_type=F32)
            m = jnp.max(s, axis=-1, keepdims=True)
            e = jnp.exp((s - m) * scale)
            l = jnp.sum(e, axis=-1, keepdims=True)
            o_ref[:, j * dv:(j + 1) * dv] = jnp.dot(e.astype(BF16), v_ref[:, jk * dv:(jk + 1) * dv],
                                                    preferred_element_type=F32) * (1.0 / l)
            lse_ref[:, j * LANE:(j + 1) * LANE] = jnp.broadcast_to(m * scale + jnp.log(l), (tq, LANE))

    kv_block = lambda h, i: (0, (h * hp // group) // kvp)
    return pl.pallas_call(
        body, name=name, grid=(n_heads // hp, S // tq),
        in_specs=[pl.BlockSpec((tq, hp * dqk), lambda h, i: (i, h)),
                  pl.BlockSpec((S, kvp * dqk), kv_block),
                  pl.BlockSpec((S, kvp * dv), kv_block)] + [pl.BlockSpec(memory_space=pl.ANY) for _ in after],
        out_specs=[pl.BlockSpec((tq, hp * dv), lambda h, i: (i, h)),
                   pl.BlockSpec((tq, hp * LANE), lambda h, i: (i, h))],
        out_shape=[jax.ShapeDtypeStruct((S, n_heads * dv), F32), jax.ShapeDtypeStruct((S, n_heads * LANE), F32)],
        compiler_params=_params(("arbitrary", "arbitrary")),
    )(q, k, v, *after)


def _attn_bwd(name, q, k, v, do, o, lse, n_heads, group, dqk, dv, scale, tq, after=()):
    S = q.shape[0]
    n_kv = n_heads // group
    hp, tq = (group, tq // 2) if group > HEADS_PER_STEP else (HEADS_PER_STEP, tq)
    kvp = max(1, hp // group)

    def body(q_ref, k_ref, v_ref, do_ref, o_ref, lse_ref, *rest):
        dq_ref, dk_ref, dv_ref = rest[len(after):]
        h, i = pl.program_id(0), pl.program_id(1)

        @pl.when(((h * hp) % group == 0) & (i == 0))
        def _():
            dk_ref[...] = jnp.zeros(dk_ref.shape, F32)
            dv_ref[...] = jnp.zeros(dv_ref.shape, F32)

        for j in range(hp):
            jk = j * kvp // hp
            qb, dob = q_ref[:, j * dqk:(j + 1) * dqk], do_ref[:, j * dv:(j + 1) * dv]
            kb, vb = k_ref[:, jk * dqk:(jk + 1) * dqk], v_ref[:, jk * dv:(jk + 1) * dv]
            s = lax.dot_general(qb, kb, (((1,), (1,)), ((), ())), preferred_element_type=F32)
            p = jnp.exp(s * scale - jnp.tile(lse_ref[:, j * LANE:(j + 1) * LANE], (1, S // LANE)))
            dp = lax.dot_general(dob, vb, (((1,), (1,)), ((), ())), preferred_element_type=F32)
            delta = jnp.sum(dob.astype(F32) * o_ref[:, j * dv:(j + 1) * dv], axis=-1, keepdims=True)
            ds = (p * ((dp - delta) * scale)).astype(BF16)
            dv_ref[:, jk * dv:(jk + 1) * dv] += lax.dot_general(p.astype(BF16), dob, (((0,), (0,)), ((), ())),
                                                                preferred_element_type=F32)
            dq_ref[:, j * dqk:(j + 1) * dqk] = jnp.dot(ds, kb, preferred_element_type=F32)
            dk_ref[:, jk * dqk:(jk + 1) * dqk] += lax.dot_general(ds, qb, (((0,), (0,)), ((), ())),
                                                                  preferred_element_type=F32)

    kv_block = lambda h, i: (0, (h * hp // group) // kvp)
    return pl.pallas_call(
        body, name=name, grid=(n_heads // hp, S // tq),
        in_specs=[pl.BlockSpec((tq, hp * dqk), lambda h, i: (i, h)),
                  pl.BlockSpec((S, kvp * dqk), kv_block),
                  pl.BlockSpec((S, kvp * dv), kv_block),
                  pl.BlockSpec((tq, hp * dv), lambda h, i: (i, h)),
                  pl.BlockSpec((tq, hp * dv), lambda h, i: (i, h)),
                  pl.BlockSpec((tq, hp * LANE), lambda h, i: (i, h))] + [pl.BlockSpec(memory_space=pl.ANY) for _ in after],
        out_specs=[pl.BlockSpec((tq, hp * dqk), lambda h, i: (i, h)),
                   pl.BlockSpec((S, kvp * dqk), kv_block),
                   pl.BlockSpec((S, kvp * dv), kv_block)],
        out_shape=[jax.ShapeDtypeStruct((S, n_heads * dqk), F32),
                   jax.ShapeDtypeStruct((S, n_kv * dqk), F32),
                   jax.ShapeDtypeStruct((S, n_kv * dv), F32)],
        compiler_params=_params(("arbitrary", "arbitrary")),
    )(q, k, v, do, o, lse, *after)


def _adamw(name, w, g, m, v, after=(), rewrite_g=False):
    R, C = w.shape
    tile = _pick(R, max(8, (1 << 19) // C // 8 * 8), 8)
    c1 = 1.0 - ADAM_B1 ** ADAM_STEP
    c2 = 1.0 - ADAM_B2 ** ADAM_STEP
    n_out = 4 if rewrite_g else 3

    def body(w_ref, g_ref, m_ref, v_ref, *rest):
        d_ref, nm_ref, nv_ref = rest[len(after):len(after) + 3]
        gg = g_ref[...]
        nm = ADAM_B1 * m_ref[...] + (1.0 - ADAM_B1) * gg
        nv = ADAM_B2 * v_ref[...] + (1.0 - ADAM_B2) * (gg * gg)
        d_ref[...] = -ADAM_LR * ((nm / c1) / (jnp.sqrt(nv / c2) + ADAM_EPS) + ADAM_WD * w_ref[...])
        nm_ref[...] = nm
        nv_ref[...] = nv
        if rewrite_g:
            rest[-1][...] = gg

    spec = pl.BlockSpec((tile, C), lambda i: (i, 0))
    return pl.pallas_call(
        body, name=name, grid=(R // tile,), in_specs=[spec] * 4 + [pl.BlockSpec(memory_space=pl.ANY) for _ in after],
        out_specs=[spec] * n_out, out_shape=[jax.ShapeDtypeStruct((R, C), F32)] * n_out,
        compiler_params=_params(("parallel",)),
    )(w, g, m, v, *after)


def _flip(me, rel):
    return tuple(1 - p if r else p for p, r in zip(me, rel))


def _push(name, ins, out_shapes, transfers, aliases=None):
    ni, no, nt = len(ins), len(out_shapes), len(transfers)

    def body(*refs):
        in_refs, out_refs = refs[:ni], refs[ni:ni + no]
        send_sems, recv_sems = refs[ni + no], refs[ni + no + 1]
        me = (lax.axis_index("x"), lax.axis_index("y"), lax.axis_index("c"))
        copies = []
        for t, (si, sfn, oi, dfn, rel) in enumerate(transfers):
            src_ref = out_refs[si[1]] if isinstance(si, tuple) else in_refs[si]
            src = src_ref.at[sfn(*me)]
            dst = out_refs[oi].at[dfn(*me)]
            if rel is None:
                cp = pltpu.make_async_copy(src, dst, send_sems.at[t])
            else:
                cp = pltpu.make_async_remote_copy(src_ref=src, dst_ref=dst, send_sem=send_sems.at[t],
                                                  recv_sem=recv_sems.at[t], device_id=_flip(me, rel),
                                                  device_id_type=MESH)
            cp.start()
            copies.append(cp)
        for cp in copies:
            cp.wait()

    any_spec = pl.BlockSpec(memory_space=pl.ANY)
    return pl.pallas_call(
        body, name=name, in_specs=[any_spec] * ni, out_specs=[any_spec] * no, out_shape=out_shapes,
        scratch_shapes=[pltpu.SemaphoreType.DMA((nt,)), pltpu.SemaphoreType.DMA((nt,))],
        input_output_aliases=aliases or {},
        compiler_params=pltpu.CompilerParams(has_side_effects=True),
    )(*ins)


_HBM = pl.BlockSpec(memory_space=pltpu.HBM)
_SEM = pl.BlockSpec(memory_space=pltpu.SEMAPHORE)
_DATAFLOW = pltpu.SideEffectType.DATAFLOW_SIDE_EFFECTING


def _split_copies(refs, transfers, send_sems, recv_sems):
    me = (lax.axis_index("x"), lax.axis_index("y"), lax.axis_index("c"))
    return [pltpu.make_async_remote_copy(src_ref=refs[sb].at[sfn(*me)], dst_ref=refs[db].at[dfn(*me)],
                                         send_sem=send_sems.at[t], recv_sem=recv_sems.at[t],
                                         device_id=_flip(me, rel), device_id_type=MESH)
            for t, (sb, sfn, db, dfn, rel) in enumerate(transfers)]


def _push_start(name, bufs, transfers, after=()):
    nb, na, nt = len(bufs), len(after), len(transfers)

    def body(*refs):
        send_sems, recv_sems, token = refs[nb + na], refs[nb + na + 1], refs[-1]
        for cp in _split_copies(refs[nb + na + 2:2 * nb + na + 2], transfers, send_sems, recv_sems):
            cp.start()
        token[...] = jnp.zeros(token.shape, token.dtype)

    res = pl.pallas_call(
        body, name=name,
        out_shape=(pltpu.SemaphoreType.DMA((nt,)), pltpu.SemaphoreType.DMA((nt,)),
                   *[pltpu.HBM(b.shape, b.dtype) for b in bufs], jax.ShapeDtypeStruct((8, LANE), F32)),
        in_specs=[_HBM] * nb + [pl.BlockSpec(memory_space=pl.ANY)] * na,
        out_specs=(_SEM, _SEM, *[_HBM] * nb, pl.BlockSpec(memory_space=pltpu.VMEM)),
        input_output_aliases={i: i + 2 for i in range(nb)},
        compiler_params=pltpu.CompilerParams(has_side_effects=_DATAFLOW),
    )(*[pltpu.with_memory_space_constraint(b, pltpu.HBM) for b in bufs], *after)
    return res[0], res[1], list(res[2:2 + nb]), res[-1]


def _push_wait(name, send_sems, recv_sems, bufs, transfers, after):
    nb = len(bufs)

    def body(*refs):
        for cp in _split_copies(refs[:nb], transfers, refs[nb], refs[nb + 1]):
            cp.wait_send()
            cp.wait_recv()

    res = pl.pallas_call(
        body, name=name, out_shape=[pltpu.HBM(b.shape, b.dtype) for b in bufs],
        in_specs=[_HBM] * nb + [_SEM, _SEM, pl.BlockSpec(memory_space=pl.ANY)], out_specs=[_HBM] * nb,
        input_output_aliases={i: i for i in range(nb)},
        compiler_params=pltpu.CompilerParams(has_side_effects=_DATAFLOW),
    )(*bufs, send_sems, recv_sems, after)
    return list(res)


ICI_RELS = ((1, 0, 0), (0, 1, 0), (1, 1, 0))
SIBLING = (0, 0, 1)


def _chip(x, y):
    return 2 * x + y


def _cast_into_slot(name, where, w, after=()):
    R, C = w.shape
    tile = _pick(R, 512, BF16_ROWS)

    def body(where_ref, w_ref, *rest):
        rest[-1][...] = w_ref[...].astype(BF16)

    return pl.pallas_call(
        body, name=name,
        grid_spec=pltpu.PrefetchScalarGridSpec(
            num_scalar_prefetch=1, grid=(R // tile,),
            in_specs=[pl.BlockSpec((tile, C), lambda i, wh: (i, 0))] + [pl.BlockSpec(memory_space=pl.ANY) for _ in after],
            out_specs=pl.BlockSpec((None, tile, C), lambda i, wh: (wh[1], i, 0))),
        out_shape=jax.ShapeDtypeStruct((N_CHIPS, R, C), BF16),
        compiler_params=_params(("parallel",)),
    )(where, w, *after)


BF16_ROWS = 16


def _rows_split(shape2):
    return (shape2[0] // 2) % BF16_ROWS == 0


def _half_shape(shape2):
    R, C = shape2
    return (R // 2, C) if _rows_split(shape2) else (R, C // 2)


def _half(shape2, c):
    R, C = shape2
    if _rows_split(shape2):
        return (pl.ds(c * (R // 2), R // 2), slice(None))
    return (slice(None), pl.ds(pl.multiple_of(c * (C // 2), LANE), C // 2))


def _gather_casts(tag, where, shards, after=()):
    return [_cast_into_slot(f"cast_w_{tag}{w}", where, s, after) for w, s in enumerate(shards)]


def _gather_start(tag, own, after):
    transfers = []
    for w in range(len(own)):
        def idx(x, y, c, shape2=own[w].shape[1:]):
            return (_chip(x, y), *_half(shape2, c))
        transfers += [(w, idx, w, idx, rel) for rel in ICI_RELS]
    send_sems, recv_sems, bufs, token = _push_start(f"gather_{tag}_start", own, transfers, after)
    return (tag, send_sems, recv_sems, bufs, transfers), token


def _forward_transfers(part):
    second = []
    for w, p in enumerate(part):
        for rel in ICI_RELS:
            def idx(x, y, c, shape2=p.shape[1:], rel=rel):
                return (_chip(x ^ rel[0], y ^ rel[1]), *_half(shape2, c))
            second.append((w, idx, w, idx, SIBLING))
    return second


def _gather_finish(handle, after):
    tag, send_sems, recv_sems, bufs, transfers = handle
    part = _push_wait(f"gather_{tag}_wait", send_sems, recv_sems, bufs, transfers, after)
    outs = [jax.ShapeDtypeStruct(p.shape, p.dtype) for p in part]
    second = [(("out", sb), sfn, db, dfn, rel) for sb, sfn, db, dfn, rel in _forward_transfers(part)]
    return _push(f"gather_{tag}_d2d", list(part), outs, second, aliases={w: w for w in range(len(part))})


def _gather_mid(handle, after):
    tag, send_sems, recv_sems, bufs, transfers = handle
    part = _push_wait(f"gather_{tag}_wait", send_sems, recv_sems, bufs, transfers, after)
    second = _forward_transfers(part)
    s2, r2, bufs2, token = _push_start(f"gather_{tag}_d2d_start", part, second)
    return (tag, s2, r2, bufs2, second), token


def _gather_end(handle, after):
    tag, send_sems, recv_sems, bufs, transfers = handle
    return _push_wait(f"gather_{tag}_d2d_wait", send_sems, recv_sems, bufs, transfers, after)


def _reduce_begin(tag, grads, after=()):
    n = len(grads)
    land = [lax.empty((N_CHIPS, *_half_shape(g.shape[1:])), F32) for g in grads]
    t1 = []
    for w in range(n):
        t1.append((w, lambda x, y, c, shape2=grads[w].shape[1:]: (slice(None), *_half(shape2, 1 - c)), n + w,
                   lambda x, y, c: (slice(None), slice(None), slice(None)), SIBLING))
    send_sems, recv_sems, bufs, token = _push_start(f"reduce_{tag}_d2d_start", list(grads) + land, t1, after)
    return (tag, send_sems, recv_sems, bufs, t1), token


def _reduce_start(handle, where, after, after_start=()):
    tag, send_sems, recv_sems, bufs, t1 = handle
    n = len(bufs) // 2
    res = _push_wait(f"reduce_{tag}_d2d_wait", send_sems, recv_sems, bufs, t1, after)
    grads, from_sib = res[:n], res[n:]
    pair = [_pair_sum(f"pair_sum_{tag}{w}", where, grads[w], from_sib[w]) for w in range(n)]
    land = [lax.empty((3, *_half_shape(g.shape[1:])), BF16) for g in grads]
    t3 = []
    for w in range(n):
        for r, rel in enumerate(ICI_RELS):
            t3.append((w, lambda x, y, c, rel=rel: (_chip(x ^ rel[0], y ^ rel[1]), slice(None), slice(None)), n + w,
                       lambda x, y, c, r=r: (r, slice(None), slice(None)), rel))
    send_sems, recv_sems, bufs, token = _push_start(f"reduce_{tag}_start", pair + land, t3, after_start)
    return (tag, send_sems, recv_sems, bufs, t3, list(grads), list(from_sib)), token


def _reduce_finish(handle, where, after):
    tag, send_sems, recv_sems, bufs, t3, grads, from_sib = handle
    n = len(grads)
    from_chips = _push_wait(f"reduce_{tag}_wait", send_sems, recv_sems, bufs, t3, after)[n:]
    mine = [_chip_sum(f"chip_sum_{tag}{w}", where, grads[w], from_sib[w], from_chips[w]) for w in range(n)]
    t5 = []
    for w in range(n):
        def idx(x, y, c, shape2=grads[w].shape[1:]):
            return _half(shape2, c)
        t5.append((w, idx, w, idx, SIBLING))
    send_sems, recv_sems, bufs, token = _push_start(f"reduce_{tag}_swap_start", mine, t5)
    return (tag, send_sems, recv_sems, bufs, t5), token


def _reduce_end(handle, after):
    tag, send_sems, recv_sems, bufs, t5 = handle
    return _push_wait(f"reduce_{tag}_swap_wait", send_sems, recv_sems, bufs, t5, after)


def _pair_sum(name, where, g, from_sib):
    _, hr, hc = from_sib.shape
    tile = _pick(hr, max(16, (1 << 19) // hc // 16 * 16), 16)
    nb = hr // tile
    slot = lambda k, wh: (wh[1] + 1 + k) % N_CHIPS
    if _rows_split(g.shape[1:]):
        mine = lambda k, i, wh: (slot(k, wh), wh[0] * nb + i, 0)
    else:
        mine = lambda k, i, wh: (slot(k, wh), i, wh[0])

    def body(where_ref, g_ref, s_ref, o_ref):
        o_ref[...] = (g_ref[...] + s_ref[...]).astype(BF16)

    return pl.pallas_call(
        body, name=name,
        grid_spec=pltpu.PrefetchScalarGridSpec(
            num_scalar_prefetch=1, grid=(N_CHIPS - 1, nb),
            in_specs=[pl.BlockSpec((None, tile, hc), mine),
                      pl.BlockSpec((None, tile, hc), lambda k, i, wh: (slot(k, wh), i, 0))],
            out_specs=pl.BlockSpec((None, tile, hc), lambda k, i, wh: (slot(k, wh), i, 0))),
        out_shape=jax.ShapeDtypeStruct((N_CHIPS, hr, hc), BF16),
        compiler_params=_params(("parallel", "parallel")),
    )(where, g, from_sib)


def _chip_sum(name, where, g, from_sib, from_chips):
    _, R, C = g.shape
    _, hr, hc = from_sib.shape
    tile = _pick(hr, max(16, (1 << 19) // hc // 16 * 16), 16)
    nb = hr // tile
    if _rows_split((R, C)):
        mine = lambda i, wh: (wh[0] * nb + i, 0)
    else:
        mine = lambda i, wh: (i, wh[0])

    def body(where_ref, g_ref, s_ref, r_ref, o_ref):
        acc = g_ref[...] + s_ref[...]
        for r in range(3):
            acc = acc + r_ref[r].astype(F32)
        o_ref[...] = acc

    return pl.pallas_call(
        body, name=name,
        grid_spec=pltpu.PrefetchScalarGridSpec(
            num_scalar_prefetch=1, grid=(nb,),
            in_specs=[pl.BlockSpec((None, tile, hc), lambda i, wh: (wh[1], *mine(i, wh))),
                      pl.BlockSpec((None, tile, hc), lambda i, wh: (wh[1], i, 0)),
                      pl.BlockSpec((3, tile, hc), lambda i, wh: (0, i, 0))],
            out_specs=pl.BlockSpec((tile, hc), mine)),
        out_shape=jax.ShapeDtypeStruct((R, C), F32),
        compiler_params=_params(("parallel",)),
    )(where, g, from_sib, from_chips)


def _allgather8(name, blk, after=()):
    m_per, n = blk.shape
    rels = [(0, 0, 1), (1, 0, 0), (0, 1, 0), (1, 1, 0), (1, 0, 1), (0, 1, 1), (1, 1, 1)]

    def body(x_ref, *rest):
        out_ref, send_sems, recv_sems = rest[len(after):]
        me = (lax.axis_index("x"), lax.axis_index("y"), lax.axis_index("c"))
        my_rows = out_ref.at[pl.ds((4 * me[0] + 2 * me[1] + me[2]) * m_per, m_per), :]
        out_ref[pl.ds((4 * me[0] + 2 * me[1] + me[2]) * m_per, m_per), :] = x_ref[...]
        copies = []
        for t, rel in enumerate(rels):
            cp = pltpu.make_async_remote_copy(src_ref=x_ref, dst_ref=my_rows, send_sem=send_sems.at[t],
                                              recv_sem=recv_sems.at[t], device_id=_flip(me, rel),
                                              device_id_type=MESH)
            cp.start()
            copies.append(cp)
        for cp in copies:
            cp.wait()

    return pl.pallas_call(
        body, name=name, out_shape=jax.ShapeDtypeStruct((N_DEV * m_per, n), blk.dtype),
        in_specs=[pl.BlockSpec(memory_space=pltpu.VMEM)] + [pl.BlockSpec(memory_space=pl.ANY) for _ in after],
        out_specs=pl.BlockSpec(memory_space=pltpu.VMEM),
        scratch_shapes=[pltpu.SemaphoreType.DMA((7,)), pltpu.SemaphoreType.DMA((7,))],
        compiler_params=pltpu.CompilerParams(has_side_effects=True, vmem_limit_bytes=VMEM_LIMIT),
    )(blk, *after)


def _rope_tables(seq_len, dim, reps):
    rows = seq_len // GRID_W
    t = np.arange(seq_len)
    row, col = (t // GRID_W).astype(np.float32), (t % GRID_W).astype(np.float32)
    half = dim // 2
    inv = jnp.asarray(ROPE_THETA, F32) ** (-jnp.arange(0, half, 2, dtype=F32) / half)
    ang_r = jnp.asarray(row)[:, None] * inv[None, :]
    ang_c = jnp.asarray(col)[:, None] * inv[None, :]
    ang = jnp.concatenate([ang_r, ang_r, ang_c, ang_c], axis=-1)
    cos, sin = jnp.cos(ang), jnp.sin(ang)
    low = (np.arange(dim) % (dim // 2)) < (dim // 4)
    s_lo = jnp.where(jnp.asarray(low)[None, :], -sin, 0.0)
    s_hi = jnp.where(jnp.asarray(low)[None, :], 0.0, sin)
    del rows
    return tuple(jnp.tile(a, (1, reps)) for a in (cos, s_lo, s_hi))


def _make_rope(shift):
    def up(v):
        return pltpu.roll(v, LANE - shift, 1)

    def down(v):
        return pltpu.roll(v, shift, 1)

    @jax.custom_vjp
    def rope(v, cos, s_lo, s_hi):
        return v * cos + up(v) * s_lo + down(v) * s_hi

    def fwd(v, cos, s_lo, s_hi):
        return rope(v, cos, s_lo, s_hi), (cos, s_lo, s_hi)

    def bwd(res, dy):
        cos, s_lo, s_hi = res
        return dy * cos + down(dy * s_lo) + up(dy * s_hi), jnp.zeros_like(cos), jnp.zeros_like(cos), jnp.zeros_like(cos)

    rope.defvjp(fwd, bwd)
    return rope


_rope_a = _make_rope(HEAD_DIM // 4)
_rope_b = _make_rope(QK_ROPE // 4)


def _prep(segs, tabs, g_q, g_k, g_ckv):
    qa, ka, va, qn, qp, ckv, kpe = segs
    ta, tb = tabs
    q_a = jnp.concatenate([_rope_a(_rn(s) * g_q, *ta) for s in qa], axis=1)
    k_a = jnp.concatenate([_rope_a(_rn(s) * g_k, *ta) for s in ka], axis=1)
    cat = []
    for h in range(HB):
        cat += [qn[h], _rope_b(qp[h], *tb)]
    q_cat = jnp.concatenate(cat, axis=1)
    return q_a, k_a, va, q_cat, _rn(ckv) * g_ckv, _rope_b(kpe, *tb)


def _split_proj(proj):
    qa = [proj[:, O_QA + h * LANE:O_QA + (h + 1) * LANE] for h in range(HA)]
    ka = [proj[:, O_KA + h * LANE:O_KA + (h + 1) * LANE] for h in range(HKV)]
    va = proj[:, O_VA:O_QN]
    qn = [proj[:, O_QN + h * LANE:O_QN + (h + 1) * LANE] for h in range(HB)]
    qp = [proj[:, O_QP + h * LANE:O_QP + (h + 1) * LANE] for h in range(HB)]
    return qa, ka, va, qn, qp, proj[:, O_CKV:O_KPE], proj[:, O_KPE:W_INP]


def _permute_w_in(nat):
    K = nat.shape[1]
    qb = nat[W_QA + 2 * W_KA:W_QA + 2 * W_KA + W_QB].reshape(HB, QK_B, K)
    qp = jnp.concatenate([qb[:, QK_NOPE:], jnp.zeros((HB, LANE - QK_ROPE, K), nat.dtype)], axis=1)
    tail = nat[W_QA + 2 * W_KA + W_QB:]
    return jnp.concatenate([nat[:O_QN], qb[:, :QK_NOPE].reshape(HB * QK_NOPE, K), qp.reshape(HB * LANE, K),
                            tail, jnp.zeros((LANE - QK_ROPE, K), nat.dtype)], axis=0)


def _unpermute_pieces():
    pieces = [(0, 0, O_QN)]
    for h in range(HB):
        pieces.append((O_QN + h * QK_NOPE, O_QN + h * QK_B, QK_NOPE))
        pieces.append((O_QP + h * LANE, O_QN + h * QK_B + QK_NOPE, QK_ROPE))
    pieces.append((O_CKV, O_QN + W_QB, KV_RANK))
    pieces.append((O_KPE, O_QN + W_QB + KV_RANK, QK_ROPE))
    return pieces


def kernel(x, c, w_ada, b_ada, g_pre_attn, w_in, g_q_a, g_k_a, g_ckv, w_kv_b, g_out_a, g_out_b, w_out, g_post_attn, g_pre_mlp, w_mlp_in, w_mlp_out, g_post_mlp, loss_target, m_w_ada, m_b_ada, m_g_pre_attn, m_w_in, m_g_q_a, m_g_k_a, m_g_ckv, m_w_kv_b, m_g_out_a, m_g_out_b, m_w_out, m_g_post_attn, m_g_pre_mlp, m_w_mlp_in, m_w_mlp_out, m_g_post_mlp, v_w_ada, v_b_ada, v_g_pre_attn, v_w_in, v_g_q_a, v_g_k_a, v_g_ckv, v_w_kv_b, v_g_out_a, v_g_out_b, v_w_out, v_g_post_attn, v_g_pre_mlp, v_w_mlp_in, v_w_mlp_out, v_g_post_mlp):
    S, D = x.shape[1], x.shape[2]
    x2d, tgt = x[0], loss_target[0]
    ix, iy, ic = lax.axis_index("x"), lax.axis_index("y"), lax.axis_index("c")
    j_me = _chip(ix, iy)
    e_me = 4 * ix + 2 * iy + ic
    T = _pick(S, 512, 8)
    TB = _pick(S, 256, 8)

    where = jnp.stack([ic, j_me]).astype(jnp.int32)
    DFF = w_mlp_out.shape[1] * N_CHIPS

    c_all = _allgather8("gather_c", c.reshape(8, D // 8)).reshape(N_DEV, D)
    w_in_t = jnp.swapaxes(w_in[0], 0, 1)
    h_qkv, tok_a = _gather_start("qkv", _gather_casts("qkv", where, [w_in_t, w_kv_b[0]]), (c_all,))
    later = [_gather_casts(tag, where, [w[0]], (tok_a,))
             for tag, w in (("out", w_out), ("up", w_mlp_in), ("down", w_mlp_out))]

    def silu_fn(cc):
        return (cc * (1.0 / (1.0 + jnp.exp(-cc))),), ()

    (c_act,), _ = _rowwise("silu_c", silu_fn, [c_all], [], [(D, BF16)], [], N_DEV)
    ca_pad = jnp.concatenate([c_act, jnp.zeros_like(c_act)], axis=0)
    n_mod = w_ada.shape[2]
    b_cols = lax.dynamic_slice(b_ada, (0, j_me * n_mod), (1, n_mod))
    (mod_part,) = _mm_nn("mod_mm", ca_pad, w_ada, [F32], epilogue=lambda acc, b: (acc + b,),
                         extras=(jnp.broadcast_to(b_cols, (16, n_mod)),))
    mod_all = _allgather8("gather_mod", mod_part[:8] + tok_a[0, 0],
                          after=[c_[0] for c_ in later]).reshape(N_DEV, 8, n_mod)
    mod = jnp.concatenate([lax.dynamic_slice(mod_all, (2 * j, e_me, 0), (1, 1, n_mod))[0] for j in range(N_CHIPS)],
                          axis=1)

    m_qkv, tok_q = _gather_mid(h_qkv, mod_all)
    h_out, tok_b = _gather_start("out", later[0], (tok_q,))
    h_up, tok_c = _gather_start("up", later[1], (tok_b,))
    h_down, tok_d = _gather_start("down", later[2], (tok_c,))
    mod = mod + tok_d[0, 0]
    sh_a, sc_a, gt_a, sh_m, sc_m, gt_m = [mod[:, i * D:(i + 1) * D] for i in range(6)]

    tabs_a = _rope_tables(S, HEAD_DIM, 1)
    tabs_b = _rope_tables(S, QK_ROPE, LANE // QK_ROPE)

    def pre_attn_fn(xt, g, sc, sh):
        return (_rn(xt) * g * (1.0 + sc) + sh,), ()

    (h_b,), _ = _rowwise("pre_attn", pre_attn_fn, [x2d], [g_pre_attn, sc_a, sh_a], [(D, BF16)], [], T)
    g_in, g_kvb = _gather_end(m_qkv, h_b)
    wpt = _permute_w_in(g_in.reshape(W_IN, D))[None]
    (proj,) = _mm_nt("proj_mm", h_b, wpt, [F32])

    def prep_fn(pt, ca_, sla, sha, cb_, slb, shb, gq, gk, gc):
        return _prep(_split_proj(pt), ((ca_, sla, sha), (cb_, slb, shb)), gq, gk, gc), ()

    (qa_b, ka_b, va_b, qcat_b, ckvn_b, kpe_b), _ = _rowwise(
        "prep", prep_fn, [proj, *tabs_a, *tabs_b], [g_q_a, g_k_a, g_ckv],
        [(W_QA, BF16), (W_KA, BF16), (W_KA, BF16), (2 * HB * LANE, BF16), (KV_RANK, BF16), (LANE, BF16)], [], T)
    (kv,) = _mm_nn("kv_mm", ckvn_b, g_kvb, [F32])

    def kcat_fn(kvt, kpet):
        kc, vs = [], []
        for h in range(HB):
            kc += [kvt[:, 2 * h * LANE:(2 * h + 1) * LANE], kpet.astype(F32)]
            vs.append(kvt[:, (2 * h + 1) * LANE:(2 * h + 2) * LANE])
        return (jnp.concatenate(kc, axis=1), jnp.concatenate(vs, axis=1)), ()

    (kcat_b, vb_b), _ = _rowwise("kcat", kcat_fn, [kv, kpe_b], [], [(2 * HB * LANE, BF16), (HB * V_DIM, BF16)], [], T)

    TQ = _pick(S, 512, 16)
    sc_a_ = 1.0 / math.sqrt(HEAD_DIM)
    sc_b_ = 1.0 / math.sqrt(QK_B)
    o_a, lse_a = _attn_fwd("attn_a_fwd", qa_b, ka_b, va_b, HA, HA // HKV, HEAD_DIM, HEAD_DIM, sc_a_, TQ)
    m_out, tok_m = _gather_mid(h_out, o_a)
    o_b, lse_b = _attn_fwd("attn_b_fwd", qcat_b, kcat_b, vb_b, HB, 1, 2 * LANE, V_DIM, sc_b_, TQ, after=(tok_m,))
    m_up, tok_m = _gather_mid(h_up, o_b)

    def mix_fn(oa, ob, ga, gb):
        return (jnp.concatenate([_rn(oa) * ga, _rn(ob) * gb], axis=1),), ()

    (on_b,), _ = _rowwise("mix_norm", mix_fn, [o_a, o_b], [g_out_a, g_out_b], [(2 * W_QA, BF16)], [], T,
                          after=(tok_m,))
    (g_wout,) = _gather_end(m_out, on_b)
    w_out_f = g_wout.reshape(1, N_CHIPS * g_wout.shape[1], D)
    (o2,) = _mm_nn("out_mm", on_b, w_out_f, [F32])

    def mid(xt, o2t, gta, gpa, gpm, scm, shm):
        x1 = xt + gta * (_rn(o2t) * gpa)
        return x1, _rn(x1) * gpm * (1.0 + scm) + shm

    def mid_fn(*a):
        return mid(*a), ()

    mid_vecs = [gt_a, g_post_attn, g_pre_mlp, sc_m, sh_m]
    (x1, h2_b), _ = _rowwise("mid", mid_fn, [x2d, o2], mid_vecs, [(D, F32), (D, BF16)], [], T)
    (g_mlp_in,) = _gather_end(m_up, h2_b)
    relu_b, a_b = _mm_nn("mlp_in_mm", h2_b, g_mlp_in, [BF16, BF16],
                         epilogue=lambda acc: (jnp.maximum(acc, 0.0), jnp.square(jnp.maximum(acc, 0.0))))
    (g_mlp_out,) = _gather_finish(h_down, a_b)
    w_mlp_out_f = g_mlp_out.reshape(1, DFF, D)
    (y,) = _mm_nn("mlp_out_mm", a_b, w_mlp_out_f, [F32])

    def last_fn(yt, x1t, tt, gtm, gpo):
        n, r = _rn_parts(yt)
        err = x1t + gtm * (n * gpo) - tt
        dx2 = err * (1.0 / D)
        loss = 0.5 * jnp.sum(jnp.mean(err * err, axis=-1))
        s = _rows(dx2 * n)
        return (_rn_bwd(n, r, dx2 * (gtm * gpo)), dx2), (jnp.full((1, LANE), loss, F32), gpo * s, gtm * s)

    (dy_b, dx2), (loss_v, d_gt_m, d_g_post_mlp) = _rowwise(
        "loss_bwd", last_fn, [y, x1, tgt], [gt_m, g_post_mlp], [(D, BF16), (D, F32)], [LANE, D, D], TB)

    gw_mlp_out = _mm_tn("gw_mlp_out_mm", a_b, dy_b, 1).reshape(N_CHIPS, DFF // N_CHIPS, D)
    b_down, tok = _reduce_begin("down", [gw_mlp_out])
    (du_b,) = _mm_nt("d_mlp_out_mm", dy_b, w_mlp_out_f, [BF16],
                     epilogue=lambda acc, rt: (acc * (2.0 * rt.astype(F32)),), extras=(relu_b,), after=(tok,))
    r_down, tok = _reduce_start(b_down, where, du_b)
    gw_mlp_in = _mm_tn("gw_mlp_in_mm", h2_b, du_b, N_CHIPS, after=(tok,))
    b_up, tok = _reduce_begin("up", [gw_mlp_in])
    (dh2,) = _mm_nt("d_mlp_in_mm", du_b, g_mlp_in, [F32], after=(tok,))
    r_up, tok_up = _reduce_start(b_up, where, dh2)

    def mid_bwd_fn(xt, o2t, dh2t, dx2t, gta, gpa, gpm, scm, shm):
        n2, r2 = _rn_parts(o2t)
        n1, r1 = _rn_parts(xt + gta * (n2 * gpa))
        dx1 = dx2t + _rn_bwd(n1, r1, dh2t * (gpm * (1.0 + scm)))
        s1, s2 = _rows(dh2t * n1), _rows(dx1 * n2)
        do2 = _rn_bwd(n2, r2, dx1 * (gta * gpa))
        return (dx1, do2), (gpa * s2, gta * s2, (1.0 + scm) * s1, gpm * s1, _rows(dh2t))

    (dx1, do2_b), (d_gt_a, d_g_post_attn, d_g_pre_mlp, d_sc_m, d_sh_m) = _rowwise(
        "mid_bwd", mid_bwd_fn, [x2d, o2, dh2, dx2], mid_vecs, [(D, F32), (D, BF16)], [D] * 5, TB, after=(tok_up,))

    (d_on,) = _mm_nt("d_out_mm", do2_b, w_out_f, [F32])
    gw_out = _mm_tn("gw_out_mm", on_b, do2_b, 1).reshape(N_CHIPS, w_out.shape[1], D)

    def mix_bwd_fn(oa, ob, dont, ga, gb):
        na, ra = _rn_parts(oa)
        nb, rb = _rn_parts(ob)
        da, db = dont[:, :W_QA], dont[:, W_QA:]
        return (_rn_bwd(na, ra, da * ga), _rn_bwd(nb, rb, db * gb)), (_rows(da * na), _rows(db * nb))

    (doa_b, dob_b), (d_g_out_a, d_g_out_b) = _rowwise(
        "mix_bwd", mix_bwd_fn, [o_a, o_b, d_on], [g_out_a, g_out_b], [(W_QA, BF16), (HB * V_DIM, BF16)],
        [W_QA, HB * V_DIM], TB)

    dqa, dka, dva = _attn_bwd("attn_a_bwd", qa_b, ka_b, va_b, doa_b, o_a, lse_a, HA, HA // HKV, HEAD_DIM, HEAD_DIM,
                              sc_a_, TQ)
    dqcat, dkcat, dvb = _attn_bwd("attn_b_bwd", qcat_b, kcat_b, vb_b, dob_b, o_b, lse_b, HB, 1, 2 * LANE, V_DIM,
                                  sc_b_, TQ)

    def kcat_bwd_fn(dkc, dvt):
        dkv, dkpe = [], None
        for h in range(HB):
            dkv += [dkc[:, 2 * h * LANE:(2 * h + 1) * LANE], dvt[:, h * V_DIM:(h + 1) * V_DIM]]
            piece = dkc[:, (2 * h + 1) * LANE:(2 * h + 2) * LANE]
            dkpe = piece if dkpe is None else dkpe + piece
        return (jnp.concatenate(dkv, axis=1), dkpe), ()

    (dkv_b, dkpe), _ = _rowwise("kcat_bwd", kcat_bwd_fn, [dkcat, dvb], [], [(2 * HB * LANE, BF16), (LANE, F32)], [], T)
    (d_ckvn,) = _mm_nt("d_kv_mm", dkv_b, g_kvb, [F32])
    gw_kvb = _mm_tn("gw_kv_mm", ckvn_b, dkv_b, N_CHIPS)
    b_kvb, tok_kvb = _reduce_begin("okv", [gw_out, gw_kvb])

    def prep_bwd_fn(pt, dqat, dkat, dvat, dqct, dckt, dkpt, ca_, sla, sha, cb_, slb, shb, gq, gk, gc):
        tabs = ((ca_, sla, sha), (cb_, slb, shb))
        _, vjp = jax.vjp(lambda sg, a1, a2, a3: _prep(sg, tabs, a1, a2, a3), _split_proj(pt), gq, gk, gc)
        (dqa_s, dka_s, dva_s, dqn_s, dqp_s, dck_s, dkp_s), dgq, dgk, dgc = vjp((dqat, dkat, dvat, dqct, dckt, dkpt))
        dproj = jnp.concatenate([*dqa_s, *dka_s, dva_s, *dqn_s, *dqp_s, dck_s, dkp_s], axis=1)
        return (dproj,), (dgq, dgk, dgc)

    (dproj_b,), (d_g_q_a, d_g_k_a, d_g_ckv) = _rowwise(
        "prep_bwd", prep_bwd_fn, [proj, dqa, dka, dva, dqcat, d_ckvn, dkpe, *tabs_a, *tabs_b],
        [g_q_a, g_k_a, g_ckv], [(W_INP, BF16)], [HEAD_DIM, HEAD_DIM, KV_RANK], TB, after=(tok_kvb,))

    r_kvb, tok = _reduce_start(b_kvb, where, dproj_b)
    (dh,) = _mm_nn("d_proj_mm", dproj_b, wpt, [F32], after=(tok,))
    gw_in = _mm_tn_rows("gw_in_mm", dproj_b, h_b, W_IN, _unpermute_pieces()).reshape(N_CHIPS, W_IN // N_CHIPS, D)
    b_in, tok = _reduce_begin("in", [gw_in])

    def pre_attn_bwd_fn(xt, dht, dx1t, g, sc, sh):
        n, r = _rn_parts(xt)
        s = _rows(dht * n)
        return (dx1t + _rn_bwd(n, r, dht * (g * (1.0 + sc))),), ((1.0 + sc) * s, g * s, _rows(dht))

    (grad_x,), (d_g_pre_attn, d_sc_a, d_sh_a) = _rowwise(
        "pre_attn_bwd", pre_attn_bwd_fn, [x2d, dh, dx1], [g_pre_attn, sc_a, sh_a], [(D, F32)], [D] * 3, TB,
        after=(tok,))

    small_names = ["b_ada", "g_pre_attn", "g_q_a", "g_k_a", "g_ckv", "g_out_a", "g_out_b", "g_post_attn",
                   "g_pre_mlp", "g_post_mlp"]
    small_local = jnp.concatenate([d_sh_a, d_sc_a, d_gt_a, d_sh_m, d_sc_m, d_gt_m, d_g_pre_attn, d_g_q_a, d_g_k_a,
                                   d_g_ckv, d_g_out_a, d_g_out_b, d_g_post_attn, d_g_pre_mlp, d_g_post_mlp,
                                   loss_v[:, :8]], axis=1)
    n_sent = small_local.shape[1]
    n_small = n_sent - 8
    small_all = _allgather8("gather_small", small_local.reshape(8, n_sent // 8)).reshape(N_DEV, n_sent)

    def sum8_fn(t):
        return (), (jnp.sum(t, axis=0, keepdims=True),)

    r_in, tok_in = _reduce_start(b_in, where, grad_x, (small_all,))
    _, (sent_sum,) = _rowwise("sum_small", sum8_fn, [small_all], [], [], [n_sent], N_DEV, after=(tok_in,))
    small_sum, loss = sent_sum[:, :n_small], sent_sum[0, n_small]
    dmod_cols = lax.dynamic_slice(small_all, (0, j_me * n_mod), (N_DEV, n_mod))
    dmod_pad = jnp.concatenate([dmod_cols, jnp.zeros_like(dmod_cols)], axis=0).astype(BF16)
    gw_ada = _mm_tn("gw_ada_mm", ca_pad, dmod_pad, 1)[0]

    weights = dict(w_ada=w_ada, b_ada=b_ada, g_pre_attn=g_pre_attn, w_in=w_in, g_q_a=g_q_a, g_k_a=g_k_a, g_ckv=g_ckv,
                   w_kv_b=w_kv_b, g_out_a=g_out_a, g_out_b=g_out_b, w_out=w_out, g_post_attn=g_post_attn,
                   g_pre_mlp=g_pre_mlp, w_mlp_in=w_mlp_in, w_mlp_out=w_mlp_out, g_post_mlp=g_post_mlp)
    ms = dict(w_ada=m_w_ada, b_ada=m_b_ada, g_pre_attn=m_g_pre_attn, w_in=m_w_in, g_q_a=m_g_q_a, g_k_a=m_g_k_a,
              g_ckv=m_g_ckv, w_kv_b=m_w_kv_b, g_out_a=m_g_out_a, g_out_b=m_g_out_b, w_out=m_w_out,
              g_post_attn=m_g_post_attn, g_pre_mlp=m_g_pre_mlp, w_mlp_in=m_w_mlp_in, w_mlp_out=m_w_mlp_out,
              g_post_mlp=m_g_post_mlp)
    vs = dict(w_ada=v_w_ada, b_ada=v_b_ada, g_pre_attn=v_g_pre_attn, w_in=v_w_in, g_q_a=v_g_q_a, g_k_a=v_g_k_a,
              g_ckv=v_g_ckv, w_kv_b=v_w_kv_b, g_out_a=v_g_out_a, g_out_b=v_g_out_b, w_out=v_w_out,
              g_post_attn=v_g_post_attn, g_pre_mlp=v_g_pre_mlp, w_mlp_in=v_w_mlp_in, w_mlp_out=v_w_mlp_out,
              g_post_mlp=v_g_post_mlp)
    order = list(weights)
    grads, deltas, new_m, new_v = {}, {}, {}, {}

    def update(name, g, after=()):
        view = (lambda t: jnp.swapaxes(t, 0, 1)) if name == "w_in" else (lambda t: t)
        d_, m_, v_, *g_ = _adamw("adamw_" + name, view(weights[name][0]), g, view(ms[name][0]), view(vs[name][0]),
                                 after, rewrite_g=name != "w_ada")
        grads[name], deltas[name], new_m[name], new_v[name] = (view(t)[None] for t in ((g_ or [g])[0], d_, m_, v_))
        return d_

    chain = ((("w_mlp_out",), r_down), (("w_mlp_in",), r_up), (("w_out", "w_kv_b"), r_kvb), (("w_in",), r_in))
    swap, tok = _reduce_finish(chain[0][1], where, small_sum)
    last = update("w_ada", gw_ada, (tok,))
    for k, (names, _) in enumerate(chain):
        gs = _reduce_end(swap, last)
        if k + 1 < len(chain):
            swap, tok = _reduce_finish(chain[k + 1][1], where, gs[0])
        for name, g in zip(names, gs):
            last = update(name, g, (tok,))

    cat = lambda d: jnp.concatenate([d[n] for n in small_names], axis=1)
    d_, m_, v_ = _adamw("adamw_small", cat(weights), small_sum, cat(ms), cat(vs))
    off = 0
    for n in small_names:
        w_ = weights[n].shape[1]
        grads[n], deltas[n], new_m[n], new_v[n] = (a[:, off:off + w_] for a in (small_sum, d_, m_, v_))
        off += w_

    return (loss, grad_x[None], *[grads[n] for n in order], *[deltas[n] for n in order],
            *[new_m[n] for n in order], *[new_v[n] for n in order])
```

```python
import functools
import math

import numpy as np
import jax
import jax.numpy as jnp
from jax import lax
from jax.experimental import pallas as pl
from jax.experimental.pallas import tpu as pltpu

F32 = jnp.float32
BF16 = jnp.bfloat16
MESH = pl.DeviceIdType.MESH

EPS = 1e-6
GRID_W = 64
ROPE_THETA = 10000.0
HEAD_DIM = 128
HA = 8
HKV = 2
HB = 8
QK_NOPE = 128
QK_ROPE = 64
V_DIM = 128
KV_RANK = 512
QK_B = QK_NOPE + QK_ROPE
W_QA = HA * HEAD_DIM
W_KA = HKV * HEAD_DIM
W_QB = HB * QK_B
W_IN = W_QA + 2 * W_KA + W_QB + KV_RANK + QK_ROPE
N_CHIPS = 4
N_DEV = 8

LANE = 128
O_QA = 0
O_KA = O_QA + W_QA
O_VA = O_KA + W_KA
O_QN = O_VA + W_KA
O_QP = O_QN + HB * QK_NOPE
O_CKV = O_QP + HB * LANE
O_KPE = O_CKV + KV_RANK
W_INP = O_KPE + LANE

ADAM_LR = 0.001
ADAM_B1 = 0.9
ADAM_B2 = 0.999
ADAM_EPS = 1e-08
ADAM_WD = 0.01
ADAM_STEP = 10

VMEM_LIMIT = 56 * 1024 * 1024
HEADS_PER_STEP = 2


def _params(sem):
    return pltpu.CompilerParams(dimension_semantics=sem, vmem_limit_bytes=VMEM_LIMIT)


def _pick(dim, target, unit):
    best = None
    t = unit
    while t <= min(dim, target):
        if dim % t == 0:
            best = t
        t += unit
    return dim if best is None else best


def _rn(x):
    return x * lax.rsqrt(jnp.mean(x * x, axis=-1, keepdims=True) + EPS)


def _rn_parts(x):
    r = lax.rsqrt(jnp.mean(x * x, axis=-1, keepdims=True) + EPS)
    return x * r, r


def _rn_bwd(n, r, dn):
    return r * (dn - n * jnp.mean(dn * n, axis=-1, keepdims=True))


def _rows(t):
    return jnp.sum(t, axis=0, keepdims=True)


def _rowwise(name, fn, rows, vecs, row_outs, vec_outs, tile, after=()):
    S = rows[0].shape[0]
    assert S % tile == 0
    n_r, n_v, n_ro, n_vo, n_a = len(rows), len(vecs), len(row_outs), len(vec_outs), len(after)

    def body(*refs):
        r_in = refs[:n_r]
        v_in = refs[n_r:n_r + n_v]
        r_out = refs[n_r + n_v + n_a:n_r + n_v + n_a + n_ro]
        v_out = refs[n_r + n_v + n_a + n_ro:]
        ro, vo = fn(*[r[...] for r in r_in], *[v[...] for v in v_in])
        for ref, val in zip(r_out, ro):
            ref[...] = val.astype(ref.dtype)
        if n_vo:
            @pl.when(pl.program_id(0) == 0)
            def _():
                for ref in v_out:
                    ref[...] = jnp.zeros(ref.shape, ref.dtype)
            for ref, val in zip(v_out, vo):
                ref[...] += val

    in_specs = [pl.BlockSpec((tile, a.shape[1]), lambda i: (i, 0)) for a in rows]
    in_specs += [pl.BlockSpec(a.shape, lambda i: (0, 0)) for a in vecs]
    in_specs += [pl.BlockSpec(memory_space=pl.ANY) for _ in after]
    out_specs = [pl.BlockSpec((tile, w), lambda i: (i, 0)) for w, _ in row_outs]
    out_specs += [pl.BlockSpec((1, w), lambda i: (0, 0)) for w in vec_outs]
    out_shape = [jax.ShapeDtypeStruct((S, w), dt) for w, dt in row_outs]
    out_shape += [jax.ShapeDtypeStruct((1, w), F32) for w in vec_outs]
    res = pl.pallas_call(
        body, name=name, grid=(S // tile,), in_specs=in_specs, out_specs=out_specs, out_shape=out_shape,
        compiler_params=_params(("arbitrary",)),
    )(*rows, *vecs, *after)
    return res[:n_ro], res[n_ro:]


CONTRACT_TILE = 2048
MM_TILE_BYTES = 40 * 1024 * 1024


def _rows_that_fit(M, tile_bytes):
    t = _pick(M, 1024, 16)
    while t % 32 == 0 and tile_bytes(t) > MM_TILE_BYTES:
        t //= 2
    return t


def _accumulate(acc, part, step, n_steps, finish):
    if n_steps == 1:
        finish(part)
        return

    @pl.when(step == 0)
    def _():
        acc[...] = part

    @pl.when((step > 0) & (step < n_steps - 1))
    def _():
        acc[...] += part

    @pl.when(step == n_steps - 1)
    def _():
        finish(acc[...] + part)


def _mm_nn(name, a, b3, out_dtypes, epilogue=None, extras=(), after=()):
    M, K = a.shape
    nB, K2, Nsh = b3.shape
    assert K == K2
    N = nB * Nsh
    tn, tk = _pick(Nsh, 1536, LANE), _pick(K, CONTRACT_TILE, LANE)
    out_bytes = sum(jnp.dtype(dt).itemsize for dt in out_dtypes) + sum(e.dtype.itemsize for e in extras)
    tm = _rows_that_fit(M, lambda t: 2 * (t * tk * a.dtype.itemsize + tk * tn * b3.dtype.itemsize)
                        + t * tn * (4 * (K > tk) + 2 * out_bytes))
    nps, nk, ne, no, na = Nsh // tn, K // tk, len(extras), len(out_dtypes), len(after)

    def body(a_ref, b_ref, *rest):
        e_refs, o_refs = rest[:ne], rest[ne + na:ne + na + no]

        def finish(total):
            vals = (total,) if epilogue is None else epilogue(total, *[e[...] for e in e_refs])
            for ref, val in zip(o_refs, vals):
                ref[...] = val.astype(ref.dtype)

        part = jnp.dot(a_ref[...].astype(BF16), b_ref[...].astype(BF16), preferred_element_type=F32)
        _accumulate(rest[-1], part, pl.program_id(2), nk, finish)

    in_specs = [pl.BlockSpec((tm, tk), lambda i, j, k: (i, k)),
                pl.BlockSpec((None, tk, tn), lambda i, j, k: (j // nps, k, j % nps))]
    in_specs += [pl.BlockSpec((tm, tn), lambda i, j, k: (i, j)) for _ in extras]
    in_specs += [pl.BlockSpec(memory_space=pl.ANY) for _ in after]
    return pl.pallas_call(
        body, name=name, grid=(M // tm, N // tn, nk), in_specs=in_specs,
        out_specs=[pl.BlockSpec((tm, tn), lambda i, j, k: (i, j)) for _ in out_dtypes],
        out_shape=[jax.ShapeDtypeStruct((M, N), dt) for dt in out_dtypes],
        scratch_shapes=[pltpu.VMEM((tm, tn), F32)] if nk > 1 else [],
        compiler_params=_params(("parallel", "parallel", "arbitrary")),
    )(a, b3, *extras, *after)


def _mm_nt(name, a, b3, out_dtypes, epilogue=None, extras=(), after=()):
    M, N = a.shape
    nB, K, Nsh = b3.shape
    assert N == nB * Nsh
    to, tn = _pick(K, 1536, LANE), _pick(Nsh, CONTRACT_TILE, LANE)
    out_bytes = sum(jnp.dtype(dt).itemsize for dt in out_dtypes) + sum(e.dtype.itemsize for e in extras)
    tm = _rows_that_fit(M, lambda t: 2 * (t * tn * a.dtype.itemsize + to * tn * b3.dtype.itemsize)
                        + t * to * (4 * (N > tn) + 2 * out_bytes))
    nps, nn, ne, no, na = Nsh // tn, N // tn, len(extras), len(out_dtypes), len(after)

    def body(a_ref, b_ref, *rest):
        e_refs, o_refs = rest[:ne], rest[ne + na:ne + na + no]

        def finish(total):
            vals = (total,) if epilogue is None else epilogue(total, *[e[...] for e in e_refs])
            for ref, val in zip(o_refs, vals):
                ref[...] = val.astype(ref.dtype)

        part = lax.dot_general(a_ref[...].astype(BF16), b_ref[...].astype(BF16),
                               (((1,), (1,)), ((), ())), preferred_element_type=F32)
        _accumulate(rest[-1], part, pl.program_id(2), nn, finish)

    in_specs = [pl.BlockSpec((tm, tn), lambda i, o, n: (i, n)),
                pl.BlockSpec((None, to, tn), lambda i, o, n: (n // nps, o, n % nps))]
    in_specs += [pl.BlockSpec((tm, to), lambda i, o, n: (i, o)) for _ in extras]
    in_specs += [pl.BlockSpec(memory_space=pl.ANY) for _ in after]
    return pl.pallas_call(
        body, name=name, grid=(M // tm, K // to, nn), in_specs=in_specs,
        out_specs=[pl.BlockSpec((tm, to), lambda i, o, n: (i, o)) for _ in out_dtypes],
        out_shape=[jax.ShapeDtypeStruct((M, K), dt) for dt in out_dtypes],
        scratch_shapes=[pltpu.VMEM((tm, to), F32)] if nn > 1 else [],
        compiler_params=_params(("parallel", "parallel", "arbitrary")),
    )(a, b3, *extras, *after)


def _mm_tn(name, a, g, nB, after=()):
    M, K = a.shape
    M2, N = g.shape
    assert M == M2 and N % nB == 0
    Nsh = N // nB
    tk, tn, tm = _pick(K, 1536, LANE), _pick(Nsh, 1024, LANE), _pick(M, CONTRACT_TILE, 16)
    nps, nm = Nsh // tn, M // tm

    def body(a_ref, g_ref, *rest):
        o_ref = rest[len(after)]

        def finish(total):
            o_ref[...] = total

        part = lax.dot_general(a_ref[...].astype(BF16), g_ref[...].astype(BF16),
                               (((0,), (0,)), ((), ())), preferred_element_type=F32)
        _accumulate(rest[-1], part, pl.program_id(2), nm, finish)

    return pl.pallas_call(
        body, name=name, grid=(K // tk, N // tn, nm),
        in_specs=[pl.BlockSpec((tm, tk), lambda k, n, m: (m, k)),
                  pl.BlockSpec((tm, tn), lambda k, n, m: (m, n))] + [pl.BlockSpec(memory_space=pl.ANY) for _ in after],
        out_specs=pl.BlockSpec((None, tk, tn), lambda k, n, m: (n // nps, k, n % nps)),
        out_shape=jax.ShapeDtypeStruct((nB, K, Nsh), F32),
        scratch_shapes=[pltpu.VMEM((tk, tn), F32)] if nm > 1 else [],
        compiler_params=_params(("parallel", "parallel", "arbitrary")),
    )(a, g, *after)


def _mm_tn_rows(name, a, g, n_rows, pieces):
    M, K = a.shape
    M2, N = g.shape
    assert M == M2
    tk, tn, tm = _pick(K, 1536, LANE), _pick(N, 512, LANE), _pick(M, CONTRACT_TILE, 16)
    nk, nm = K // tk, M // tm

    def body(a_ref, g_ref, o_ref, *acc):
        kk = pl.program_id(1)

        def finish(total):
            for t in range(nk):
                @pl.when(kk == t)
                def _(t=t):
                    for src, dst, n in pieces:
                        lo, hi = max(src, t * tk), min(src + n, (t + 1) * tk)
                        if lo < hi:
                            o_ref[dst + lo - src:dst + hi - src, :] = total[lo - t * tk:hi - t * tk, :]

        part = lax.dot_general(a_ref[...].astype(BF16), g_ref[...].astype(BF16),
                               (((0,), (0,)), ((), ())), preferred_element_type=F32)
        _accumulate(acc[0] if acc else None, part, pl.program_id(2), nm, finish)

    return pl.pallas_call(
        body, name=name, grid=(N // tn, nk, nm),
        in_specs=[pl.BlockSpec((tm, tk), lambda n, k, m: (m, k)),
                  pl.BlockSpec((tm, tn), lambda n, k, m: (m, n))],
        out_specs=pl.BlockSpec((n_rows, tn), lambda n, k, m: (0, n)),
        out_shape=jax.ShapeDtypeStruct((n_rows, N), F32),
        scratch_shapes=[pltpu.VMEM((tk, tn), F32)] if nm > 1 else [],
        compiler_params=_params(("arbitrary", "arbitrary", "arbitrary")),
    )(a, g)


def _attn_fwd(name, q, k, v, n_heads, group, dqk, dv, scale, tq, after=()):
    S = q.shape[0]
    hp, tq = (2 * HEADS_PER_STEP, tq) if group > 1 else (HEADS_PER_STEP, min(2 * tq, S))
    kvp = max(1, hp // group)

    def body(q_ref, k_ref, v_ref, *rest):
        o_ref, lse_ref = rest[-2:]
        for j in range(hp):
            jk = j * kvp // hp
            s = lax.dot_general(q_ref[:, j * dqk:(j + 1) * dqk], k_ref[:, jk * dqk:(jk + 1) * dqk],
                                (((1,), (1,)), ((), ())), preferred_element_type=F32)
            m = jnp.max(s, axis=-1, keepdims=True)
            e = jnp.exp((s - m) * scale)
            l = jnp.sum(e, axis=-1, keepdims=True)
            o_ref[:, j * dv:(j + 1) * dv] = jnp.dot(e.astype(BF16), v_ref[:, jk * dv:(jk + 1) * dv],
                                                    preferred_element_type=F32) * (1.0 / l)
            lse_ref[:, j * LANE:(j + 1) * LANE] = jnp.broadcast_to(m * scale + jnp.log(l), (tq, LANE))

    kv_block = lambda h, i: (0, (h * hp // group) // kvp)
    return pl.pallas_call(
        body, name=name, grid=(n_heads // hp, S // tq),
        in_specs=[pl.BlockSpec((tq, hp * dqk), lambda h, i: (i, h)),
                  pl.BlockSpec((S, kvp * dqk), kv_block),
                  pl.BlockSpec((S, kvp * dv), kv_block)] + [pl.BlockSpec(memory_space=pl.ANY) for _ in after],
        out_specs=[pl.BlockSpec((tq, hp * dv), lambda h, i: (i, h)),
                   pl.BlockSpec((tq, hp * LANE), lambda h, i: (i, h))],
        out_shape=[jax.ShapeDtypeStruct((S, n_heads * dv), F32), jax.ShapeDtypeStruct((S, n_heads * LANE), F32)],
        compiler_params=_params(("arbitrary", "arbitrary")),
    )(q, k, v, *after)


def _attn_bwd(name, q, k, v, do, o, lse, n_heads, group, dqk, dv, scale, tq, after=()):
    S = q.shape[0]
    n_kv = n_heads // group
    hp, tq = (group, tq // 2) if group > HEADS_PER_STEP else (HEADS_PER_STEP, tq)
    kvp = max(1, hp // group)

    def body(q_ref, k_ref, v_ref, do_ref, o_ref, lse_ref, *rest):
        dq_ref, dk_ref, dv_ref = rest[len(after):]
        h, i = pl.program_id(0), pl.program_id(1)

        @pl.when(((h * hp) % group == 0) & (i == 0))
        def _():
            dk_ref[...] = jnp.zeros(dk_ref.shape, F32)
            dv_ref[...] = jnp.zeros(dv_ref.shape, F32)

        for j in range(hp):
            jk = j * kvp // hp
            qb, dob = q_ref[:, j * dqk:(j + 1) * dqk], do_ref[:, j * dv:(j + 1) * dv]
            kb, vb = k_ref[:, jk * dqk:(jk + 1) * dqk], v_ref[:, jk * dv:(jk + 1) * dv]
            s = lax.dot_general(qb, kb, (((1,), (1,)), ((), ())), preferred_element_type=F32)
            p = jnp.exp(s * scale - jnp.tile(lse_ref[:, j * LANE:(j + 1) * LANE], (1, S // LANE)))
            dp = lax.dot_general(dob, vb, (((1,), (1,)), ((), ())), preferred_element_type=F32)
            delta = jnp.sum(dob.astype(F32) * o_ref[:, j * dv:(j + 1) * dv], axis=-1, keepdims=True)
            ds = (p * ((dp - delta) * scale)).astype(BF16)
            dv_ref[:, jk * dv:(jk + 1) * dv] += lax.dot_general(p.astype(BF16), dob, (((0,), (0,)), ((), ())),
                                                                preferred_element_type=F32)
            dq_ref[:, j * dqk:(j + 1) * dqk] = jnp.dot(ds, kb, preferred_element_type=F32)
            dk_ref[:, jk * dqk:(jk + 1) * dqk] += lax.dot_general(ds, qb, (((0,), (0,)), ((), ())),
                                                                  preferred_element_type=F32)

    kv_block = lambda h, i: (0, (h * hp // group) // kvp)
    return pl.pallas_call(
        body, name=name, grid=(n_heads // hp, S // tq),
        in_specs=[pl.BlockSpec((tq, hp * dqk), lambda h, i: (i, h)),
                  pl.BlockSpec((S, kvp * dqk), kv_block),
                  pl.BlockSpec((S, kvp * dv), kv_block),
                  pl.BlockSpec((tq, hp * dv), lambda h, i: (i, h)),
                  pl.BlockSpec((tq, hp * dv), lambda h, i: (i, h)),
                  pl.BlockSpec((tq, hp * LANE), lambda h, i: (i, h))] + [pl.BlockSpec(memory_space=pl.ANY) for _ in after],
        out_specs=[pl.BlockSpec((tq, hp * dqk), lambda h, i: (i, h)),
                   pl.BlockSpec((S, kvp * dqk), kv_block),
                   pl.BlockSpec((S, kvp * dv), kv_block)],
        out_shape=[jax.ShapeDtypeStruct((S, n_heads * dqk), F32),
                   jax.ShapeDtypeStruct((S, n_kv * dqk), F32),
                   jax.ShapeDtypeStruct((S, n_kv * dv), F32)],
        compiler_params=_params(("arbitrary", "arbitrary")),
    )(q, k, v, do, o, lse, *after)


def _adamw(name, w, g, m, v, after=(), rewrite_g=False):
    R, C = w.shape
    tile = _pick(R, max(8, (1 << 19) // C // 8 * 8), 8)
    c1 = 1.0 - ADAM_B1 ** ADAM_STEP
    c2 = 1.0 - ADAM_B2 ** ADAM_STEP
    n_out = 4 if rewrite_g else 3

    def body(w_ref, g_ref, m_ref, v_ref, *rest):
        d_ref, nm_ref, nv_ref = rest[len(after):len(after) + 3]
        gg = g_ref[...]
        nm = ADAM_B1 * m_ref[...] + (1.0 - ADAM_B1) * gg
        nv = ADAM_B2 * v_ref[...] + (1.0 - ADAM_B2) * (gg * gg)
        d_ref[...] = -ADAM_LR * ((nm / c1) / (jnp.sqrt(nv / c2) + ADAM_EPS) + ADAM_WD * w_ref[...])
        nm_ref[...] = nm
        nv_ref[...] = nv
        if rewrite_g:
            rest[-1][...] = gg

    spec = pl.BlockSpec((tile, C), lambda i: (i, 0))
    return pl.pallas_call(
        body, name=name, grid=(R // tile,), in_specs=[spec] * 4 + [pl.BlockSpec(memory_space=pl.ANY) for _ in after],
        out_specs=[spec] * n_out, out_shape=[jax.ShapeDtypeStruct((R, C), F32)] * n_out,
        compiler_params=_params(("parallel",)),
    )(w, g, m, v, *after)


def _flip(me, rel):
    return tuple(1 - p if r else p for p, r in zip(me, rel))


def _push(name, ins, out_shapes, transfers, aliases=None):
    ni, no, nt = len(ins), len(out_shapes), len(transfers)

    def body(*refs):
        in_refs, out_refs = refs[:ni], refs[ni:ni + no]
        send_sems, recv_sems = refs[ni + no], refs[ni + no + 1]
        me = (lax.axis_index("x"), lax.axis_index("y"), lax.axis_index("c"))
        copies = []
        for t, (si, sfn, oi, dfn, rel) in enumerate(transfers):
            src_ref = out_refs[si[1]] if isinstance(si, tuple) else in_refs[si]
            src = src_ref.at[sfn(*me)]
            dst = out_refs[oi].at[dfn(*me)]
            if rel is None:
                cp = pltpu.make_async_copy(src, dst, send_sems.at[t])
            else:
                cp = pltpu.make_async_remote_copy(src_ref=src, dst_ref=dst, send_sem=send_sems.at[t],
                                                  recv_sem=recv_sems.at[t], device_id=_flip(me, rel),
                                                  device_id_type=MESH)
            cp.start()
            copies.append(cp)
        for cp in copies:
            cp.wait()

    any_spec = pl.BlockSpec(memory_space=pl.ANY)
    return pl.pallas_call(
        body, name=name, in_specs=[any_spec] * ni, out_specs=[any_spec] * no, out_shape=out_shapes,
        scratch_shapes=[pltpu.SemaphoreType.DMA((nt,)), pltpu.SemaphoreType.DMA((nt,))],
        input_output_aliases=aliases or {},
        compiler_params=pltpu.CompilerParams(has_side_effects=True),
    )(*ins)


_HBM = pl.BlockSpec(memory_space=pltpu.HBM)
_SEM = pl.BlockSpec(memory_space=pltpu.SEMAPHORE)
_DATAFLOW = pltpu.SideEffectType.DATAFLOW_SIDE_EFFECTING


def _split_copies(refs, transfers, send_sems, recv_sems):
    me = (lax.axis_index("x"), lax.axis_index("y"), lax.axis_index("c"))
    return [pltpu.make_async_remote_copy(src_ref=refs[sb].at[sfn(*me)], dst_ref=refs[db].at[dfn(*me)],
                                         send_sem=send_sems.at[t], recv_sem=recv_sems.at[t],
                                         device_id=_flip(me, rel), device_id_type=MESH)
            for t, (sb, sfn, db, dfn, rel) in enumerate(transfers)]


def _push_start(name, bufs, transfers, after=()):
    nb, na, nt = len(bufs), len(after), len(transfers)

    def body(*refs):
        send_sems, recv_sems, token = refs[nb + na], refs[nb + na + 1], refs[-1]
        for cp in _split_copies(refs[nb + na + 2:2 * nb + na + 2], transfers, send_sems, recv_sems):
            cp.start()
        token[...] = jnp.zeros(token.shape, token.dtype)

    res = pl.pallas_call(
        body, name=name,
        out_shape=(pltpu.SemaphoreType.DMA((nt,)), pltpu.SemaphoreType.DMA((nt,)),
                   *[pltpu.HBM(b.shape, b.dtype) for b in bufs], jax.ShapeDtypeStruct((8, LANE), F32)),
        in_specs=[_HBM] * nb + [pl.BlockSpec(memory_space=pl.ANY)] * na,
        out_specs=(_SEM, _SEM, *[_HBM] * nb, pl.BlockSpec(memory_space=pltpu.VMEM)),
        input_output_aliases={i: i + 2 for i in range(nb)},
        compiler_params=pltpu.CompilerParams(has_side_effects=_DATAFLOW),
    )(*[pltpu.with_memory_space_constraint(b, pltpu.HBM) for b in bufs], *after)
    return res[0], res[1], list(res[2:2 + nb]), res[-1]


def _push_wait(name, send_sems, recv_sems, bufs, transfers, after):
    nb = len(bufs)

    def body(*refs):
        for cp in _split_copies(refs[:nb], transfers, refs[nb], refs[nb + 1]):
            cp.wait_send()
            cp.wait_recv()

    res = pl.pallas_call(
        body, name=name, out_shape=[pltpu.HBM(b.shape, b.dtype) for b in bufs],
        in_specs=[_HBM] * nb + [_SEM, _SEM, pl.BlockSpec(memory_space=pl.ANY)], out_specs=[_HBM] * nb,
        input_output_aliases={i: i for i in range(nb)},
        compiler_params=pltpu.CompilerParams(has_side_effects=_DATAFLOW),
    )(*bufs, send_sems, recv_sems, after)
    return list(res)


ICI_RELS = ((1, 0, 0), (0, 1, 0), (1, 1, 0))
SIBLING = (0, 0, 1)


def _chip(x, y):
    return 2 * x + y


def _cast_into_slot(name, where, w, after=()):
    R, C = w.shape
    tile = _pick(R, 512, BF16_ROWS)

    def body(where_ref, w_ref, *rest):
        rest[-1][...] = w_ref[...].astype(BF16)

    return pl.pallas_call(
        body, name=name,
        grid_spec=pltpu.PrefetchScalarGridSpec(
            num_scalar_prefetch=1, grid=(R // tile,),
            in_specs=[pl.BlockSpec((tile, C), lambda i, wh: (i, 0))] + [pl.BlockSpec(memory_space=pl.ANY) for _ in after],
            out_specs=pl.BlockSpec((None, tile, C), lambda i, wh: (wh[1], i, 0))),
        out_shape=jax.ShapeDtypeStruct((N_CHIPS, R, C), BF16),
        compiler_params=_params(("parallel",)),
    )(where, w, *after)


BF16_ROWS = 16


def _rows_split(shape2):
    return (shape2[0] // 2) % BF16_ROWS == 0


def _half_shape(shape2):
    R, C = shape2
    return (R // 2, C) if _rows_split(shape2) else (R, C // 2)


def _half(shape2, c):
    R, C = shape2
    if _rows_split(shape2):
        return (pl.ds(c * (R // 2), R // 2), slice(None))
    return (slice(None), pl.ds(pl.multiple_of(c * (C // 2), LANE), C // 2))


def _gather_casts(tag, where, shards, after=()):
    return [_cast_into_slot(f"cast_w_{tag}{w}", where, s, after) for w, s in enumerate(shards)]


def _gather_start(tag, own, after):
    transfers = []
    for w in range(len(own)):
        def idx(x, y, c, shape2=own[w].shape[1:]):
            return (_chip(x, y), *_half(shape2, c))
        transfers += [(w, idx, w, idx, rel) for rel in ICI_RELS]
    send_sems, recv_sems, bufs, token = _push_start(f"gather_{tag}_start", own, transfers, after)
    return (tag, send_sems, recv_sems, bufs, transfers), token


def _forward_transfers(part):
    second = []
    for w, p in enumerate(part):
        for rel in ICI_RELS:
            def idx(x, y, c, shape2=p.shape[1:], rel=rel):
                return (_chip(x ^ rel[0], y ^ rel[1]), *_half(shape2, c))
            second.append((w, idx, w, idx, SIBLING))
    return second


def _gather_finish(handle, after):
    tag, send_sems, recv_sems, bufs, transfers = handle
    part = _push_wait(f"gather_{tag}_wait", send_sems, recv_sems, bufs, transfers, after)
    outs = [jax.ShapeDtypeStruct(p.shape, p.dtype) for p in part]
    second = [(("out", sb), sfn, db, dfn, rel) for sb, sfn, db, dfn, rel in _forward_transfers(part)]
    return _push(f"gather_{tag}_d2d", list(part), outs, second, aliases={w: w for w in range(len(part))})


def _gather_mid(handle, after):
    tag, send_sems, recv_sems, bufs, transfers = handle
    part = _push_wait(f"gather_{tag}_wait", send_sems, recv_sems, bufs, transfers, after)
    second = _forward_transfers(part)
    s2, r2, bufs2, token = _push_start(f"gather_{tag}_d2d_start", part, second)
    return (tag, s2, r2, bufs2, second), token


def _gather_end(handle, after):
    tag, send_sems, recv_sems, bufs, transfers = handle
    return _push_wait(f"gather_{tag}_d2d_wait", send_sems, recv_sems, bufs, transfers, after)


def _reduce_begin(tag, grads, after=()):
    n = len(grads)
    land = [lax.empty((N_CHIPS, *_half_shape(g.shape[1:])), F32) for g in grads]
    t1 = []
    for w in range(n):
        t1.append((w, lambda x, y, c, shape2=grads[w].shape[1:]: (slice(None), *_half(shape2, 1 - c)), n + w,
                   lambda x, y, c: (slice(None), slice(None), slice(None)), SIBLING))
    send_sems, recv_sems, bufs, token = _push_start(f"reduce_{tag}_d2d_start", list(grads) + land, t1, after)
    return (tag, send_sems, recv_sems, bufs, t1), token


def _reduce_start(handle, where, after, after_start=()):
    tag, send_sems, recv_sems, bufs, t1 = handle
    n = len(bufs) // 2
    res = _push_wait(f"reduce_{tag}_d2d_wait", send_sems, recv_sems, bufs, t1, after)
    grads, from_sib = res[:n], res[n:]
    pair = [_pair_sum(f"pair_sum_{tag}{w}", where, grads[w], from_sib[w]) for w in range(n)]
    land = [lax.empty((3, *_half_shape(g.shape[1:])), BF16) for g in grads]
    t3 = []
    for w in range(n):
        for r, rel in enumerate(ICI_RELS):
            t3.append((w, lambda x, y, c, rel=rel: (_chip(x ^ rel[0], y ^ rel[1]), slice(None), slice(None)), n + w,
                       lambda x, y, c, r=r: (r, slice(None), slice(None)), rel))
    send_sems, recv_sems, bufs, token = _push_start(f"reduce_{tag}_start", pair + land, t3, after_start)
    return (tag, send_sems, recv_sems, bufs, t3, list(grads), list(from_sib)), token


def _reduce_finish(handle, where, after):
    tag, send_sems, recv_sems, bufs, t3, grads, from_sib = handle
    n = len(grads)
    from_chips = _push_wait(f"reduce_{tag}_wait", send_sems, recv_sems, bufs, t3, after)[n:]
    mine = [_chip_sum(f"chip_sum_{tag}{w}", where, grads[w], from_sib[w], from_chips[w]) for w in range(n)]
    t5 = []
    for w in range(n):
        def idx(x, y, c, shape2=grads[w].shape[1:]):
            return _half(shape2, c)
        t5.append((w, idx, w, idx, SIBLING))
    send_sems, recv_sems, bufs, token = _push_start(f"reduce_{tag}_swap_start", mine, t5)
    return (tag, send_sems, recv_sems, bufs, t5), token


def _reduce_end(handle, after):
    tag, send_sems, recv_sems, bufs, t5 = handle
    return _push_wait(f"reduce_{tag}_swap_wait", send_sems, recv_sems, bufs, t5, after)


def _pair_sum(name, where, g, from_sib):
    _, hr, hc = from_sib.shape
    tile = _pick(hr, max(16, (1 << 19) // hc // 16 * 16), 16)
    nb = hr // tile
    slot = lambda k, wh: (wh[1] + 1 + k) % N_CHIPS
    if _rows_split(g.shape[1:]):
        mine = lambda k, i, wh: (slot(k, wh), wh[0] * nb + i, 0)
    else:
        mine = lambda k, i, wh: (slot(k, wh), i, wh[0])

    def body(where_ref, g_ref, s_ref, o_ref):
        o_ref[...] = (g_ref[...] + s_ref[...]).astype(BF16)

    return pl.pallas_call(
        body, name=name,
        grid_spec=pltpu.PrefetchScalarGridSpec(
            num_scalar_prefetch=1, grid=(N_CHIPS - 1, nb),
            in_specs=[pl.BlockSpec((None, tile, hc), mine),
                      pl.BlockSpec((None, tile, hc), lambda k, i, wh: (slot(k, wh), i, 0))],
            out_specs=pl.BlockSpec((None, tile, hc), lambda k, i, wh: (slot(k, wh), i, 0))),
        out_shape=jax.ShapeDtypeStruct((N_CHIPS, hr, hc), BF16),
        compiler_params=_params(("parallel", "parallel")),
    )(where, g, from_sib)


def _chip_sum(name, where, g, from_sib, from_chips):
    _, R, C = g.shape
    _, hr, hc = from_sib.shape
    tile = _pick(hr, max(16, (1 << 19) // hc // 16 * 16), 16)
    nb = hr // tile
    if _rows_split((R, C)):
        mine = lambda i, wh: (wh[0] * nb + i, 0)
    else:
        mine = lambda i, wh: (i, wh[0])

    def body(where_ref, g_ref, s_ref, r_ref, o_ref):
        acc = g_ref[...] + s_ref[...]
        for r in range(3):
            acc = acc + r_ref[r].astype(F32)
        o_ref[...] = acc

    return pl.pallas_call(
        body, name=name,
        grid_spec=pltpu.PrefetchScalarGridSpec(
            num_scalar_prefetch=1, grid=(nb,),
            in_specs=[pl.BlockSpec((None, tile, hc), lambda i, wh: (wh[1], *mine(i, wh))),
                      pl.BlockSpec((None, tile, hc), lambda i, wh: (wh[1], i, 0)),
                      pl.BlockSpec((3, tile, hc), lambda i, wh: (0, i, 0))],
            out_specs=pl.BlockSpec((tile, hc), mine)),
        out_shape=jax.ShapeDtypeStruct((R, C), F32),
        compiler_params=_params(("parallel",)),
    )(where, g, from_sib, from_chips)


def _allgather8(name, blk, after=()):
    m_per, n = blk.shape
    rels = [(0, 0, 1), (1, 0, 0), (0, 1, 0), (1, 1, 0), (1, 0, 1), (0, 1, 1), (1, 1, 1)]

    def body(x_ref, *rest):
        out_ref, send_sems, recv_sems = rest[len(after):]
        me = (lax.axis_index("x"), lax.axis_index("y"), lax.axis_index("c"))
        my_rows = out_ref.at[pl.ds((4 * me[0] + 2 * me[1] + me[2]) * m_per, m_per), :]
        out_ref[pl.ds((4 * me[0] + 2 * me[1] + me[2]) * m_per, m_per), :] = x_ref[...]
        copies = []
        for t, rel in enumerate(rels):
            cp = pltpu.make_async_remote_copy(src_ref=x_ref, dst_ref=my_rows, send_sem=send_sems.at[t],
                                              recv_sem=recv_sems.at[t], device_id=_flip(me, rel),
                                              device_id_type=MESH)
            cp.start()
            copies.append(cp)
        for cp in copies:
            cp.wait()

    return pl.pallas_call(
        body, name=name, out_shape=jax.ShapeDtypeStruct((N_DEV * m_per, n), blk.dtype),
        in_specs=[pl.BlockSpec(memory_space=pltpu.VMEM)] + [pl.BlockSpec(memory_space=pl.ANY) for _ in after],
        out_specs=pl.BlockSpec(memory_space=pltpu.VMEM),
        scratch_shapes=[pltpu.SemaphoreType.DMA((7,)), pltpu.SemaphoreType.DMA((7,))],
        compiler_params=pltpu.CompilerParams(has_side_effects=True, vmem_limit_bytes=VMEM_LIMIT),
    )(blk, *after)


def _rope_tables(seq_len, dim, reps):
    rows = seq_len // GRID_W
    t = np.arange(seq_len)
    row, col = (t // GRID_W).astype(np.float32), (t % GRID_W).astype(np.float32)
    half = dim // 2
    inv = jnp.asarray(ROPE_THETA, F32) ** (-jnp.arange(0, half, 2, dtype=F32) / half)
    ang_r = jnp.asarray(row)[:, None] * inv[None, :]
    ang_c = jnp.asarray(col)[:, None] * inv[None, :]
    ang = jnp.concatenate([ang_r, ang_r, ang_c, ang_c], axis=-1)
    cos, sin = jnp.cos(ang), jnp.sin(ang)
    low = (np.arange(dim) % (dim // 2)) < (dim // 4)
    s_lo = jnp.where(jnp.asarray(low)[None, :], -sin, 0.0)
    s_hi = jnp.where(jnp.asarray(low)[None, :], 0.0, sin)
    del rows
    return tuple(jnp.tile(a, (1, reps)) for a in (cos, s_lo, s_hi))


def _make_rope(shift):
    def up(v):
        return pltpu.roll(v, LANE - shift, 1)

    def down(v):
        return pltpu.roll(v, shift, 1)

    @jax.custom_vjp
    def rope(v, cos, s_lo, s_hi):
        return v * cos + up(v) * s_lo + down(v) * s_hi

    def fwd(v, cos, s_lo, s_hi):
        return rope(v, cos, s_lo, s_hi), (cos, s_lo, s_hi)

    def bwd(res, dy):
        cos, s_lo, s_hi = res
        return dy * cos + down(dy * s_lo) + up(dy * s_hi), jnp.zeros_like(cos), jnp.zeros_like(cos), jnp.zeros_like(cos)

    rope.defvjp(fwd, bwd)
    return rope


_rope_a = _make_rope(HEAD_DIM // 4)
_rope_b = _make_rope(QK_ROPE // 4)


def _prep(segs, tabs, g_q, g_k, g_ckv):
    qa, ka, va, qn, qp, ckv, kpe = segs
    ta, tb = tabs
    q_a = jnp.concatenate([_rope_a(_rn(s) * g_q, *ta) for s in qa], axis=1)
    k_a = jnp.concatenate([_rope_a(_rn(s) * g_k, *ta) for s in ka], axis=1)
    cat = []
    for h in range(HB):
        cat += [qn[h], _rope_b(qp[h], *tb)]
    q_cat = jnp.concatenate(cat, axis=1)
    return q_a, k_a, va, q_cat, _rn(ckv) * g_ckv, _rope_b(kpe, *tb)


def _split_proj(proj):
    qa = [proj[:, O_QA + h * LANE:O_QA + (h + 1) * LANE] for h in range(HA)]
    ka = [proj[:, O_KA + h * LANE:O_KA + (h + 1) * LANE] for h in range(HKV)]
    va = proj[:, O_VA:O_QN]
    qn = [proj[:, O_QN + h * LANE:O_QN + (h + 1) * LANE] for h in range(HB)]
    qp = [proj[:, O_QP + h * LANE:O_QP + (h + 1) * LANE] for h in range(HB)]
    return qa, ka, va, qn, qp, proj[:, O_CKV:O_KPE], proj[:, O_KPE:W_INP]


def _permute_w_in(nat):
    K = nat.shape[1]
    qb = nat[W_QA + 2 * W_KA:W_QA + 2 * W_KA + W_QB].reshape(HB, QK_B, K)
    qp = jnp.concatenate([qb[:, QK_NOPE:], jnp.zeros((HB, LANE - QK_ROPE, K), nat.dtype)], axis=1)
    tail = nat[W_QA + 2 * W_KA + W_QB:]
    return jnp.concatenate([nat[:O_QN], qb[:, :QK_NOPE].reshape(HB * QK_NOPE, K), qp.reshape(HB * LANE, K),
                            tail, jnp.zeros((LANE - QK_ROPE, K), nat.dtype)], axis=0)


def _unpermute_pieces():
    pieces = [(0, 0, O_QN)]
    for h in range(HB):
        pieces.append((O_QN + h * QK_NOPE, O_QN + h * QK_B, QK_NOPE))
        pieces.append((O_QP + h * LANE, O_QN + h * QK_B + QK_NOPE, QK_ROPE))
    pieces.append((O_CKV, O_QN + W_QB, KV_RANK))
    pieces.append((O_KPE, O_QN + W_QB + KV_RANK, QK_ROPE))
    return pieces


def kernel(x, c, w_ada, b_ada, g_pre_attn, w_in, g_q_a, g_k_a, g_ckv, w_kv_b, g_out_a, g_out_b, w_out, g_post_attn, g_pre_mlp, w_mlp_in, w_mlp_out, g_post_mlp, loss_target, m_w_ada, m_b_ada, m_g_pre_attn, m_w_in, m_g_q_a, m_g_k_a, m_g_ckv, m_w_kv_b, m_g_out_a, m_g_out_b, m_w_out, m_g_post_attn, m_g_pre_mlp, m_w_mlp_in, m_w_mlp_out, m_g_post_mlp, v_w_ada, v_b_ada, v_g_pre_attn, v_w_in, v_g_q_a, v_g_k_a, v_g_ckv, v_w_kv_b, v_g_out_a, v_g_out_b, v_w_out, v_g_post_attn, v_g_pre_mlp, v_w_mlp_in, v_w_mlp_out, v_g_post_mlp):
    S, D = x.shape[1], x.shape[2]
    x2d, tgt = x[0], loss_target[0]
    ix, iy, ic = lax.axis_index("x"), lax.axis_index("y"), lax.axis_index("c")
    j_me = _chip(ix, iy)
    e_me = 4 * ix + 2 * iy + ic
    T = _pick(S, 512, 8)
    TB = _pick(S, 256, 8)

    where = jnp.stack([ic, j_me]).astype(jnp.int32)
    DFF = w_mlp_out.shape[1] * N_CHIPS

    c_all = _allgather8("gather_c", c.reshape(8, D // 8)).reshape(N_DEV, D)
    w_in_t = jnp.swapaxes(w_in[0], 0, 1)
    h_qkv, tok_a = _gather_start("qkv", _gather_casts("qkv", where, [w_in_t, w_kv_b[0]]), (c_all,))
    later = [_gather_casts(tag, where, [w[0]], (tok_a,))
             for tag, w in (("out", w_out), ("up", w_mlp_in), ("down", w_mlp_out))]

    def silu_fn(cc):
        return (cc * (1.0 / (1.0 + jnp.exp(-cc))),), ()

    (c_act,), _ = _rowwise("silu_c", silu_fn, [c_all], [], [(D, BF16)], [], N_DEV)
    ca_pad = jnp.concatenate([c_act, jnp.zeros_like(c_act)], axis=0)
    n_mod = w_ada.shape[2]
    b_cols = lax.dynamic_slice(b_ada, (0, j_me * n_mod), (1, n_mod))
    (mod_part,) = _mm_nn("mod_mm", ca_pad, w_ada, [F32], epilogue=lambda acc, b: (acc + b,),
                         extras=(jnp.broadcast_to(b_cols, (16, n_mod)),))
    mod_all = _allgather8("gather_mod", mod_part[:8] + tok_a[0, 0],
                          after=[c_[0] for c_ in later]).reshape(N_DEV, 8, n_mod)
    mod = jnp.concatenate([lax.dynamic_slice(mod_all, (2 * j, e_me, 0), (1, 1, n_mod))[0] for j in range(N_CHIPS)],
                          axis=1)

    m_qkv, tok_q = _gather_mid(h_qkv, mod_all)
    h_out, tok_b = _gather_start("out", later[0], (tok_q,))
    h_up, tok_c = _gather_start("up", later[1], (tok_b,))
    h_down, tok_d = _gather_start("down", later[2], (tok_c,))
    mod = mod + tok_d[0, 0]
    sh_a, sc_a, gt_a, sh_m, sc_m, gt_m = [mod[:, i * D:(i + 1) * D] for i in range(6)]

    tabs_a = _rope_tables(S, HEAD_DIM, 1)
    tabs_b = _rope_tables(S, QK_ROPE, LANE // QK_ROPE)

    def pre_attn_fn(xt, g, sc, sh):
        return (_rn(xt) * g * (1.0 + sc) + sh,), ()

    (h_b,), _ = _rowwise("pre_attn", pre_attn_fn, [x2d], [g_pre_attn, sc_a, sh_a], [(D, BF16)], [], T)
    g_in, g_kvb = _gather_end(m_qkv, h_b)
    wpt = _permute_w_in(g_in.reshape(W_IN, D))[None]
    (proj,) = _mm_nt("proj_mm", h_b, wpt, [F32])

    def prep_fn(pt, ca_, sla, sha, cb_, slb, shb, gq, gk, gc):
        return _prep(_split_proj(pt), ((ca_, sla, sha), (cb_, slb, shb)), gq, gk, gc), ()

    (qa_b, ka_b, va_b, qcat_b, ckvn_b, kpe_b), _ = _rowwise(
        "prep", prep_fn, [proj, *tabs_a, *tabs_b], [g_q_a, g_k_a, g_ckv],
        [(W_QA, BF16), (W_KA, BF16), (W_KA, BF16), (2 * HB * LANE, BF16), (KV_RANK, BF16), (LANE, BF16)], [], T)
    (kv,) = _mm_nn("kv_mm", ckvn_b, g_kvb, [F32])

    def kcat_fn(kvt, kpet):
        kc, vs = [], []
        for h in range(HB):
            kc += [kvt[:, 2 * h * LANE:(2 * h + 1) * LANE], kpet.astype(F32)]
            vs.append(kvt[:, (2 * h + 1) * LANE:(2 * h + 2) * LANE])
        return (jnp.concatenate(kc, axis=1), jnp.concatenate(vs, axis=1)), ()

    (kcat_b, vb_b), _ = _rowwise("kcat", kcat_fn, [kv, kpe_b], [], [(2 * HB * LANE, BF16), (HB * V_DIM, BF16)], [], T)

    TQ = _pick(S, 512, 16)
    sc_a_ = 1.0 / math.sqrt(HEAD_DIM)
    sc_b_ = 1.0 / math.sqrt(QK_B)
    o_a, lse_a = _attn_fwd("attn_a_fwd", qa_b, ka_b, va_b, HA, HA // HKV, HEAD_DIM, HEAD_DIM, sc_a_, TQ)
    m_out, tok_m = _gather_mid(h_out, o_a)
    o_b, lse_b = _attn_fwd("attn_b_fwd", qcat_b, kcat_b, vb_b, HB, 1, 2 * LANE, V_DIM, sc_b_, TQ, after=(tok_m,))
    m_up, tok_m = _gather_mid(h_up, o_b)

    def mix_fn(oa, ob, ga, gb):
        return (jnp.concatenate([_rn(oa) * ga, _rn(ob) * gb], axis=1),), ()

    (on_b,), _ = _rowwise("mix_norm", mix_fn, [o_a, o_b], [g_out_a, g_out_b], [(2 * W_QA, BF16)], [], T,
                          after=(tok_m,))
    (g_wout,) = _gather_end(m_out, on_b)
    w_out_f = g_wout.reshape(1, N_CHIPS * g_wout.shape[1], D)
    (o2,) = _mm_nn("out_mm", on_b, w_out_f, [F32])

    def mid(xt, o2t, gta, gpa, gpm, scm, shm):
        x1 = xt + gta * (_rn(o2t) * gpa)
        return x1, _rn(x1) * gpm * (1.0 + scm) + shm

    def mid_fn(*a):
        return mid(*a), ()

    mid_vecs = [gt_a, g_post_attn, g_pre_mlp, sc_m, sh_m]
    (x1, h2_b), _ = _rowwise("mid", mid_fn, [x2d, o2], mid_vecs, [(D, F32), (D, BF16)], [], T)
    (g_mlp_in,) = _gather_end(m_up, h2_b)
    relu_b, a_b = _mm_nn("mlp_in_mm", h2_b, g_mlp_in, [BF16, BF16],
                         epilogue=lambda acc: (jnp.maximum(acc, 0.0), jnp.square(jnp.maximum(acc, 0.0))))
    (g_mlp_out,) = _gather_finish(h_down, a_b)
    w_mlp_out_f = g_mlp_out.reshape(1, DFF, D)
    (y,) = _mm_nn("mlp_out_mm", a_b, w_mlp_out_f, [F32])

    def last_fn(yt, x1t, tt, gtm, gpo):
        n, r = _rn_parts(yt)
        err = x1t + gtm * (n * gpo) - tt
        dx2 = err * (1.0 / D)
        loss = 0.5 * jnp.sum(jnp.mean(err * err, axis=-1))
        s = _rows(dx2 * n)
        return (_rn_bwd(n, r, dx2 * (gtm * gpo)), dx2), (jnp.full((1, LANE), loss, F32), gpo * s, gtm * s)

    (dy_b, dx2), (loss_v, d_gt_m, d_g_post_mlp) = _rowwise(
        "loss_bwd", last_fn, [y, x1, tgt], [gt_m, g_post_mlp], [(D, BF16), (D, F32)], [LANE, D, D], TB)

    gw_mlp_out = _mm_tn("gw_mlp_out_mm", a_b, dy_b, 1).reshape(N_CHIPS, DFF // N_CHIPS, D)
    b_down, tok = _reduce_begin("down", [gw_mlp_out])
    (du_b,) = _mm_nt("d_mlp_out_mm", dy_b, w_mlp_out_f, [BF16],
                     epilogue=lambda acc, rt: (acc * (2.0 * rt.astype(F32)),), extras=(relu_b,), after=(tok,))
    r_down, tok = _reduce_start(b_down, where, du_b)
    gw_mlp_in = _mm_tn("gw_mlp_in_mm", h2_b, du_b, N_CHIPS, after=(tok,))
    b_up, tok = _reduce_begin("up", [gw_mlp_in])
    (dh2,) = _mm_nt("d_mlp_in_mm", du_b, g_mlp_in, [F32], after=(tok,))
    r_up, tok_up = _reduce_start(b_up, where, dh2)

    def mid_bwd_fn(xt, o2t, dh2t, dx2t, gta, gpa, gpm, scm, shm):
        n2, r2 = _rn_parts(o2t)
        n1, r1 = _rn_parts(xt + gta * (n2 * gpa))
        dx1 = dx2t + _rn_bwd(n1, r1, dh2t * (gpm * (1.0 + scm)))
        s1, s2 = _rows(dh2t * n1), _rows(dx1 * n2)
        do2 = _rn_bwd(n2, r2, dx1 * (gta * gpa))
        return (dx1, do2), (gpa * s2, gta * s2, (1.0 + scm) * s1, gpm * s1, _rows(dh2t))

    (dx1, do2_b), (d_gt_a, d_g_post_attn, d_g_pre_mlp, d_sc_m, d_sh_m) = _rowwise(
        "mid_bwd", mid_bwd_fn, [x2d, o2, dh2, dx2], mid_vecs, [(D, F32), (D, BF16)], [D] * 5, TB, after=(tok_up,))

    (d_on,) = _mm_nt("d_out_mm", do2_b, w_out_f, [F32])
    gw_out = _mm_tn("gw_out_mm", on_b, do2_b, 1).reshape(N_CHIPS, w_out.shape[1], D)

    def mix_bwd_fn(oa, ob, dont, ga, gb):
        na, ra = _rn_parts(oa)
        nb, rb = _rn_parts(ob)
        da, db = dont[:, :W_QA], dont[:, W_QA:]
        return (_rn_bwd(na, ra, da * ga), _rn_bwd(nb, rb, db * gb)), (_rows(da * na), _rows(db * nb))

    (doa_b, dob_b), (d_g_out_a, d_g_out_b) = _rowwise(
        "mix_bwd", mix_bwd_fn, [o_a, o_b, d_on], [g_out_a, g_out_b], [(W_QA, BF16), (HB * V_DIM, BF16)],
        [W_QA, HB * V_DIM], TB)

    dqa, dka, dva = _attn_bwd("attn_a_bwd", qa_b, ka_b, va_b, doa_b, o_a, lse_a, HA, HA // HKV, HEAD_DIM, HEAD_DIM,
                              sc_a_, TQ)
    dqcat, dkcat, dvb = _attn_bwd("attn_b_bwd", qcat_b, kcat_b, vb_b, dob_b, o_b, lse_b, HB, 1, 2 * LANE, V_DIM,
                                  sc_b_, TQ)

    def kcat_bwd_fn(dkc, dvt):
        dkv, dkpe = [], None
        for h in range(HB):
            dkv += [dkc[:, 2 * h * LANE:(2 * h + 1) * LANE], dvt[:, h * V_DIM:(h + 1) * V_DIM]]
            piece = dkc[:, (2 * h + 1) * LANE:(2 * h + 2) * LANE]
            dkpe = piece if dkpe is None else dkpe + piece
        return (jnp.concatenate(dkv, axis=1), dkpe), ()

    (dkv_b, dkpe), _ = _rowwise("kcat_bwd", kcat_bwd_fn, [dkcat, dvb], [], [(2 * HB * LANE, BF16), (LANE, F32)], [], T)
    (d_ckvn,) = _mm_nt("d_kv_mm", dkv_b, g_kvb, [F32])
    gw_kvb = _mm_tn("gw_kv_mm", ckvn_b, dkv_b, N_CHIPS)
    b_kvb, tok_kvb = _reduce_begin("okv", [gw_out, gw_kvb])

    def prep_bwd_fn(pt, dqat, dkat, dvat, dqct, dckt, dkpt, ca_, sla, sha, cb_, slb, shb, gq, gk, gc):
        tabs = ((ca_, sla, sha), (cb_, slb, shb))
        _, vjp = jax.vjp(lambda sg, a1, a2, a3: _prep(sg, tabs, a1, a2, a3), _split_proj(pt), gq, gk, gc)
        (dqa_s, dka_s, dva_s, dqn_s, dqp_s, dck_s, dkp_s), dgq, dgk, dgc = vjp((dqat, dkat, dvat, dqct, dckt, dkpt))
        dproj = jnp.concatenate([*dqa_s, *dka_s, dva_s, *dqn_s, *dqp_s, dck_s, dkp_s], axis=1)
        return (dproj,), (dgq, dgk, dgc)

    (dproj_b,), (d_g_q_a, d_g_k_a, d_g_ckv) = _rowwise(
        "prep_bwd", prep_bwd_fn, [proj, dqa, dka, dva, dqcat, d_ckvn, dkpe, *tabs_a, *tabs_b],
        [g_q_a, g_k_a, g_ckv], [(W_INP, BF16)], [HEAD_DIM, HEAD_DIM, KV_RANK], TB, after=(tok_kvb,))

    r_kvb, tok = _reduce_start(b_kvb, where, dproj_b)
    (dh,) = _mm_nn("d_proj_mm", dproj_b, wpt, [F32], after=(tok,))
    gw_in = _mm_tn_rows("gw_in_mm", dproj_b, h_b, W_IN, _unpermute_pieces()).reshape(N_CHIPS, W_IN // N_CHIPS, D)
    b_in, tok = _reduce_begin("in", [gw_in])

    def pre_attn_bwd_fn(xt, dht, dx1t, g, sc, sh):
        n, r = _rn_parts(xt)
        s = _rows(dht * n)
        return (dx1t + _rn_bwd(n, r, dht * (g * (1.0 + sc))),), ((1.0 + sc) * s, g * s, _rows(dht))

    (grad_x,), (d_g_pre_attn, d_sc_a, d_sh_a) = _rowwise(
        "pre_attn_bwd", pre_attn_bwd_fn, [x2d, dh, dx1], [g_pre_attn, sc_a, sh_a], [(D, F32)], [D] * 3, TB,
        after=(tok,))

    small_names = ["b_ada", "g_pre_attn", "g_q_a", "g_k_a", "g_ckv", "g_out_a", "g_out_b", "g_post_attn",
                   "g_pre_mlp", "g_post_mlp"]
    small_local = jnp.concatenate([d_sh_a, d_sc_a, d_gt_a, d_sh_m, d_sc_m, d_gt_m, d_g_pre_attn, d_g_q_a, d_g_k_a,
                                   d_g_ckv, d_g_out_a, d_g_out_b, d_g_post_attn, d_g_pre_mlp, d_g_post_mlp,
                                   loss_v[:, :8]], axis=1)
    n_sent = small_local.shape[1]
    n_small = n_sent - 8
    small_all = _allgather8("gather_small", small_local.reshape(8, n_sent // 8)).reshape(N_DEV, n_sent)

    def sum8_fn(t):
        return (), (jnp.sum(t, axis=0, keepdims=True),)

    r_in, tok_in = _reduce_start(b_in, where, grad_x, (small_all,))
    _, (sent_sum,) = _rowwise("sum_small", sum8_fn, [small_all], [], [], [n_sent], N_DEV, after=(tok_in,))
    small_sum, loss = sent_sum[:, :n_small], sent_sum[0, n_small]
    dmod_cols = lax.dynamic_slice(small_all, (0, j_me * n_mod), (N_DEV, n_mod))
    dmod_pad = jnp.concatenate([dmod_cols, jnp.zeros_like(dmod_cols)], axis=0).astype(BF16)
    gw_ada = _mm_tn("gw_ada_mm", ca_pad, dmod_pad, 1)[0]

    weights = dict(w_ada=w_ada, b_ada=b_ada, g_pre_attn=g_pre_attn, w_in=w_in, g_q_a=g_q_a, g_k_a=g_k_a, g_ckv=g_ckv,
                   w_kv_b=w_kv_b, g_out_a=g_out_a, g_out_b=g_out_b, w_out=w_out, g_post_attn=g_post_attn,
                   g_pre_mlp=g_pre_mlp, w_mlp_in=w_mlp_in, w_mlp_out=w_mlp_out, g_post_mlp=g_post_mlp)
    ms = dict(w_ada=m_w_ada, b_ada=m_b_ada, g_pre_attn=m_g_pre_attn, w_in=m_w_in, g_q_a=m_g_q_a, g_k_a=m_g_k_a,
              g_ckv=m_g_ckv, w_kv_b=m_w_kv_b, g_out_a=m_g_out_a, g_out_b=m_g_out_b, w_out=m_w_out,
              g_post_attn=m_g_post_attn, g_pre_mlp=m_g_pre_mlp, w_mlp_in=m_w_mlp_in, w_mlp_out=m_w_mlp_out,
              g_post_mlp=m_g_post_mlp)
    vs = dict(w_ada=v_w_ada, b_ada=v_b_ada, g_pre_attn=v_g_pre_attn, w_in=v_w_in, g_q_a=v_g_q_a, g_k_a=v_g_k_a,
              g_ckv=v_g_ckv, w_kv_b=v_w_kv_b, g_out_a=v_g_out_a, g_out_b=v_g_out_b, w_out=v_w_out,
              g_post_attn=v_g_post_attn, g_pre_mlp=v_g_pre_mlp, w_mlp_in=v_w_mlp_in, w_mlp_out=v_w_mlp_out,
              g_post_mlp=v_g_post_mlp)
    order = list(weights)
    grads, deltas, new_m, new_v = {}, {}, {}, {}

    def update(name, g, after=()):
        view = (lambda t: jnp.swapaxes(t, 0, 1)) if name == "w_in" else (lambda t: t)
        d_, m_, v_, *g_ = _adamw("adamw_" + name, view(weights[name][0]), g, view(ms[name][0]), view(vs[name][0]),
                                 after, rewrite_g=name != "w_ada")
        grads[name], deltas[name], new_m[name], new_v[name] = (view(t)[None] for t in ((g_ or [g])[0], d_, m_, v_))
        return d_

    chain = ((("w_mlp_out",), r_down), (("w_mlp_in",), r_up), (("w_out", "w_kv_b"), r_kvb), (("w_in",), r_in))
    swap, tok = _reduce_finish(chain[0][1], where, small_sum)
    last = update("w_ada", gw_ada, (tok,))
    for k, (names, _) in enumerate(chain):
        gs = _reduce_end(swap, last)
        if k + 1 < len(chain):
            swap, tok = _reduce_finish(chain[k + 1][1], where, gs[0])
        for name, g in zip(names, gs):
            last = update(name, g, (tok,))

    cat = lambda d: jnp.concatenate([d[n] for n in small_names], axis=1)
    d_, m_, v_ = _adamw("adamw_small", cat(weights), small_sum, cat(ms), cat(vs))
    off = 0
    for n in small_names:
        w_ = weights[n].shape[1]
        grads[n], deltas[n], new_m[n], new_v[n] = (a[:, off:off + w_] for a in (small_sum, d_, m_, v_))
        off += w_

    return (loss, grad_x[None], *[grads[n] for n in order], *[deltas[n] for n in order],
            *[new_m[n] for n in order], *[new_v[n] for n in order])
```

```python
import functools
import math

import numpy as np
import jax
import jax.numpy as jnp
from jax import lax
from jax.experimental import pallas as pl
from jax.experimental.pallas import tpu as pltpu

F32 = jnp.float32
BF16 = jnp.bfloat16
MESH = pl.DeviceIdType.MESH

EPS = 1e-6
GRID_W = 64
ROPE_THETA = 10000.0
HEAD_DIM = 128
HA = 8
HKV = 2
HB = 8
QK_NOPE = 128
QK_ROPE = 64
V_DIM = 128
KV_RANK = 512
QK_B = QK_NOPE + QK_ROPE
W_QA = HA * HEAD_DIM
W_KA = HKV * HEAD_DIM
W_QB = HB * QK_B
W_IN = W_QA + 2 * W_KA + W_QB + KV_RANK + QK_ROPE
N_CHIPS = 4
N_DEV = 8

LANE = 128
O_QA = 0
O_KA = O_QA + W_QA
O_VA = O_KA + W_KA
O_QN = O_VA + W_KA
O_QP = O_QN + HB * QK_NOPE
O_CKV = O_QP + HB * LANE
O_KPE = O_CKV + KV_RANK
W_INP = O_KPE + LANE

ADAM_LR = 0.001
ADAM_B1 = 0.9
ADAM_B2 = 0.999
ADAM_EPS = 1e-08
ADAM_WD = 0.01
ADAM_STEP = 10

VMEM_LIMIT = 56 * 1024 * 1024
HEADS_PER_STEP = 2


def _params(sem):
    return pltpu.CompilerParams(dimension_semantics=sem, vmem_limit_bytes=VMEM_LIMIT)


def _pick(dim, target, unit):
    best = None
    t = unit
    while t <= min(dim, target):
        if dim % t == 0:
            best = t
        t += unit
    return dim if best is None else best


def _rn(x):
    return x * lax.rsqrt(jnp.mean(x * x, axis=-1, keepdims=True) + EPS)


def _rn_parts(x):
    r = lax.rsqrt(jnp.mean(x * x, axis=-1, keepdims=True) + EPS)
    return x * r, r


def _rn_bwd(n, r, dn):
    return r * (dn - n * jnp.mean(dn * n, axis=-1, keepdims=True))


def _rows(t):
    return jnp.sum(t, axis=0, keepdims=True)


def _rowwise(name, fn, rows, vecs, row_outs, vec_outs, tile, after=()):
    S = rows[0].shape[0]
    assert S % tile == 0
    n_r, n_v, n_ro, n_vo, n_a = len(rows), len(vecs), len(row_outs), len(vec_outs), len(after)

    def body(*refs):
        r_in = refs[:n_r]
        v_in = refs[n_r:n_r + n_v]
        r_out = refs[n_r + n_v + n_a:n_r + n_v + n_a + n_ro]
        v_out = refs[n_r + n_v + n_a + n_ro:]
        ro, vo = fn(*[r[...] for r in r_in], *[v[...] for v in v_in])
        for ref, val in zip(r_out, ro):
            ref[...] = val.astype(ref.dtype)
        if n_vo:
            @pl.when(pl.program_id(0) == 0)
            def _():
                for ref in v_out:
                    ref[...] = jnp.zeros(ref.shape, ref.dtype)
            for ref, val in zip(v_out, vo):
                ref[...] += val

    in_specs = [pl.BlockSpec((tile, a.shape[1]), lambda i: (i, 0)) for a in rows]
    in_specs += [pl.BlockSpec(a.shape, lambda i: (0, 0)) for a in vecs]
    in_specs += [pl.BlockSpec(memory_space=pl.ANY) for _ in after]
    out_specs = [pl.BlockSpec((tile, w), lambda i: (i, 0)) for w, _ in row_outs]
    out_specs += [pl.BlockSpec((1, w), lambda i: (0, 0)) for w in vec_outs]
    out_shape = [jax.ShapeDtypeStruct((S, w), dt) for w, dt in row_outs]
    out_shape += [jax.ShapeDtypeStruct((1, w), F32) for w in vec_outs]
    res = pl.pallas_call(
        body, name=name, grid=(S // tile,), in_specs=in_specs, out_specs=out_specs, out_shape=out_shape,
        compiler_params=_params(("arbitrary",)),
    )(*rows, *vecs, *after)
    return res[:n_ro], res[n_ro:]


CONTRACT_TILE = 2048
MM_TILE_BYTES = 40 * 1024 * 1024


def _rows_that_fit(M, tile_bytes, most=1024):
    t = _pick(M, most, 16)
    while t % 32 == 0 and tile_bytes(t) > MM_TILE_BYTES:
        t //= 2
    return t


def _accumulate(acc, part, step, n_steps, finish):
    if n_steps == 1:
        finish(part)
        return

    @pl.when(step == 0)
    def _():
        acc[...] = part

    @pl.when((step > 0) & (step < n_steps - 1))
    def _():
        acc[...] += part

    @pl.when(step == n_steps - 1)
    def _():
        finish(acc[...] + part)


def _mm_nn(name, a, b3, out_dtypes, epilogue=None, extras=(), after=()):
    M, K = a.shape
    nB, K2, Nsh = b3.shape
    assert K == K2
    N = nB * Nsh
    tn, tk = _pick(Nsh, 512, LANE), _pick(K, CONTRACT_TILE, LANE)
    out_bytes = sum(jnp.dtype(dt).itemsize for dt in out_dtypes) + sum(e.dtype.itemsize for e in extras)
    tm = _rows_that_fit(M, lambda t: 2 * (t * tk * a.dtype.itemsize + tk * tn * b3.dtype.itemsize)
                        + t * tn * (4 * (K > tk) + 2 * out_bytes), most=2048)
    nps, nk, ne, no, na = Nsh // tn, K // tk, len(extras), len(out_dtypes), len(after)

    def body(a_ref, b_ref, *rest):
        e_refs, o_refs = rest[:ne], rest[ne + na:ne + na + no]

        def finish(total):
            vals = (total,) if epilogue is None else epilogue(total, *[e[...] for e in e_refs])
            for ref, val in zip(o_refs, vals):
                ref[...] = val.astype(ref.dtype)

        part = jnp.dot(a_ref[...].astype(BF16), b_ref[...].astype(BF16), preferred_element_type=F32)
        _accumulate(rest[-1], part, pl.program_id(2), nk, finish)

    in_specs = [pl.BlockSpec((tm, tk), lambda i, j, k: (i, k)),
                pl.BlockSpec((None, tk, tn), lambda i, j, k: (j // nps, k, j % nps))]
    in_specs += [pl.BlockSpec((tm, tn), lambda i, j, k: (i, j)) for _ in extras]
    in_specs += [pl.BlockSpec(memory_space=pl.ANY) for _ in after]
    return pl.pallas_call(
        body, name=name, grid=(M // tm, N // tn, nk), in_specs=in_specs,
        out_specs=[pl.BlockSpec((tm, tn), lambda i, j, k: (i, j)) for _ in out_dtypes],
        out_shape=[jax.ShapeDtypeStruct((M, N), dt) for dt in out_dtypes],
        scratch_shapes=[pltpu.VMEM((tm, tn), F32)] if nk > 1 else [],
        compiler_params=_params(("parallel", "parallel", "arbitrary")),
    )(a, b3, *extras, *after)


def _mm_nt(name, a, b3, out_dtypes, epilogue=None, extras=(), after=()):
    M, N = a.shape
    nB, K, Nsh = b3.shape
    assert N == nB * Nsh
    to, tn = _pick(K, 1536, LANE), _pick(Nsh, CONTRACT_TILE, LANE)
    out_bytes = sum(jnp.dtype(dt).itemsize for dt in out_dtypes) + sum(e.dtype.itemsize for e in extras)
    tm = _rows_that_fit(M, lambda t: 2 * (t * tn * a.dtype.itemsize + to * tn * b3.dtype.itemsize)
                        + t * to * (4 * (N > tn) + 2 * out_bytes))
    nps, nn, ne, no, na = Nsh // tn, N // tn, len(extras), len(out_dtypes), len(after)

    def body(a_ref, b_ref, *rest):
        e_refs, o_refs = rest[:ne], rest[ne + na:ne + na + no]

        def finish(total):
            vals = (total,) if epilogue is None else epilogue(total, *[e[...] for e in e_refs])
            for ref, val in zip(o_refs, vals):
                ref[...] = val.astype(ref.dtype)

        part = lax.dot_general(a_ref[...].astype(BF16), b_ref[...].astype(BF16),
                               (((1,), (1,)), ((), ())), preferred_element_type=F32)
        _accumulate(rest[-1], part, pl.program_id(2), nn, finish)

    in_specs = [pl.BlockSpec((tm, tn), lambda i, o, n: (i, n)),
                pl.BlockSpec((None, to, tn), lambda i, o, n: (n // nps, o, n % nps))]
    in_specs += [pl.BlockSpec((tm, to), lambda i, o, n: (i, o)) for _ in extras]
    in_specs += [pl.BlockSpec(memory_space=pl.ANY) for _ in after]
    return pl.pallas_call(
        body, name=name, grid=(M // tm, K // to, nn), in_specs=in_specs,
        out_specs=[pl.BlockSpec((tm, to), lambda i, o, n: (i, o)) for _ in out_dtypes],
        out_shape=[jax.ShapeDtypeStruct((M, K), dt) for dt in out_dtypes],
        scratch_shapes=[pltpu.VMEM((tm, to), F32)] if nn > 1 else [],
        compiler_params=_params(("parallel", "parallel", "arbitrary")),
    )(a, b3, *extras, *after)


def _mm_tn(name, a, g, nB, after=()):
    M, K = a.shape
    M2, N = g.shape
    assert M == M2 and N % nB == 0
    Nsh = N // nB
    tk, tn, tm = _pick(K, 1536, LANE), _pick(Nsh, 1024, LANE), _pick(M, CONTRACT_TILE, 16)
    nps, nm = Nsh // tn, M // tm

    def body(a_ref, g_ref, *rest):
        o_ref = rest[len(after)]

        def finish(total):
            o_ref[...] = total

        part = lax.dot_general(a_ref[...].astype(BF16), g_ref[...].astype(BF16),
                               (((0,), (0,)), ((), ())), preferred_element_type=F32)
        _accumulate(rest[-1], part, pl.program_id(2), nm, finish)

    return pl.pallas_call(
        body, name=name, grid=(K // tk, N // tn, nm),
        in_specs=[pl.BlockSpec((tm, tk), lambda k, n, m: (m, k)),
                  pl.BlockSpec((tm, tn), lambda k, n, m: (m, n))] + [pl.BlockSpec(memory_space=pl.ANY) for _ in after],
        out_specs=pl.BlockSpec((None, tk, tn), lambda k, n, m: (n // nps, k, n % nps)),
        out_shape=jax.ShapeDtypeStruct((nB, K, Nsh), F32),
        scratch_shapes=[pltpu.VMEM((tk, tn), F32)] if nm > 1 else [],
        compiler_params=_params(("parallel", "parallel", "arbitrary")),
    )(a, g, *after)


def _mm_tn_rows(name, a, g, n_rows, pieces):
    M, K = a.shape
    M2, N = g.shape
    assert M == M2
    tk, tn, tm = _pick(K, 1536, LANE), _pick(N, 512, LANE), _pick(M, CONTRACT_TILE, 16)
    nk, nm = K // tk, M // tm

    def body(a_ref, g_ref, o_ref, *acc):
        kk = pl.program_id(1)

        def finish(total):
            for t in range(nk):
                @pl.when(kk == t)
                def _(t=t):
                    for src, dst, n in pieces:
                        lo, hi = max(src, t * tk), min(src + n, (t + 1) * tk)
                        if lo < hi:
                            o_ref[dst + lo - src:dst + hi - src, :] = total[lo - t * tk:hi - t * tk, :]

        part = lax.dot_general(a_ref[...].astype(BF16), g_ref[...].astype(BF16),
                               (((0,), (0,)), ((), ())), preferred_element_type=F32)
        _accumulate(acc[0] if acc else None, part, pl.program_id(2), nm, finish)

    return pl.pallas_call(
        body, name=name, grid=(N // tn, nk, nm),
        in_specs=[pl.BlockSpec((tm, tk), lambda n, k, m: (m, k)),
                  pl.BlockSpec((tm, tn), lambda n, k, m: (m, n))],
        out_specs=pl.BlockSpec((n_rows, tn), lambda n, k, m: (0, n)),
        out_shape=jax.ShapeDtypeStruct((n_rows, N), F32),
        scratch_shapes=[pltpu.VMEM((tk, tn), F32)] if nm > 1 else [],
        compiler_params=_params(("arbitrary", "arbitrary", "arbitrary")),
    )(a, g)


def _attn_fwd(name, q, k, v, n_heads, group, dqk, dv, scale, tq, after=()):
    S = q.shape[0]
    hp = 2 * HEADS_PER_STEP
    kvp = max(1, hp // group)

    def body(q_ref, k_ref, v_ref, *rest):
        o_ref, lse_ref = rest[-2:]
        for j in range(hp):
            jk = j * kvp // hp
            s = lax.dot_general(q_ref[:, j * dqk:(j + 1) * dqk], k_ref[:, jk * dqk:(jk + 1) * dqk],
                                (((1,), (1,)), ((), ())), preferred_element_type=F32)
            m = jnp.max(s, axis=-1, keepdims=True)
            e = jnp.exp((s - m) * scale)
            l = jnp.sum(e, axis=-1, keepdims=True)
            o_ref[:, j * dv:(j + 1) * dv] = jnp.dot(e.astype(BF16), v_ref[:, jk * dv:(jk + 1) * dv],
                                                    preferred_element_type=F32) * (1.0 / l)
            lse_ref[:, j * LANE:(j + 1) * LANE] = jnp.broadcast_to(m * scale + jnp.log(l), (tq, LANE))

    kv_block = lambda h, i: (0, (h * hp // group) // kvp)
    return pl.pallas_call(
        body, name=name, grid=(n_heads // hp, S // tq),
        in_specs=[pl.BlockSpec((tq, hp * dqk), lambda h, i: (i, h)),
                  pl.BlockSpec((S, kvp * dqk), kv_block),
                  pl.BlockSpec((S, kvp * dv), kv_block)] + [pl.BlockSpec(memory_space=pl.ANY) for _ in after],
        out_specs=[pl.BlockSpec((tq, hp * dv), lambda h, i: (i, h)),
                   pl.BlockSpec((tq, hp * LANE), lambda h, i: (i, h))],
        out_shape=[jax.ShapeDtypeStruct((S, n_heads * dv), F32), jax.ShapeDtypeStruct((S, n_heads * LANE), F32)],
        compiler_params=_params(("arbitrary", "arbitrary")),
    )(q, k, v, *after)


def _attn_bwd(name, q, k, v, do, o, lse, n_heads, group, dqk, dv, scale, tq, after=()):
    S = q.shape[0]
    n_kv = n_heads // group
    hp, tq = (group, tq // 2) if group > HEADS_PER_STEP else (HEADS_PER_STEP, tq)
    kvp = max(1, hp // group)

    def body(q_ref, k_ref, v_ref, do_ref, o_ref, lse_ref, *rest):
        dq_ref, dk_ref, dv_ref = rest[len(after):]
        h, i = pl.program_id(0), pl.program_id(1)

        @pl.when(((h * hp) % group == 0) & (i == 0))
        def _():
            dk_ref[...] = jnp.zeros(dk_ref.shape, F32)
            dv_ref[...] = jnp.zeros(dv_ref.shape, F32)

        for j in range(hp):
            jk = j * kvp // hp
            qb, dob = q_ref[:, j * dqk:(j + 1) * dqk], do_ref[:, j * dv:(j + 1) * dv]
            kb, vb = k_ref[:, jk * dqk:(jk + 1) * dqk], v_ref[:, jk * dv:(jk + 1) * dv]
            s = lax.dot_general(qb, kb, (((1,), (1,)), ((), ())), preferred_element_type=F32)
            p = jnp.exp(s * scale - jnp.tile(lse_ref[:, j * LANE:(j + 1) * LANE], (1, S // LANE)))
            dp = lax.dot_general(dob, vb, (((1,), (1,)), ((), ())), preferred_element_type=F32)
            delta = jnp.sum(dob.astype(F32) * o_ref[:, j * dv:(j + 1) * dv], axis=-1, keepdims=True)
            ds = (p * ((dp - delta) * scale)).astype(BF16)
            dv_ref[:, jk * dv:(jk + 1) * dv] += lax.dot_general(p.astype(BF16), dob, (((0,), (0,)), ((), ())),
                                                                preferred_element_type=F32)
            dq_ref[:, j * dqk:(j + 1) * dqk] = jnp.dot(ds, kb, preferred_element_type=F32)
            dk_ref[:, jk * dqk:(jk + 1) * dqk] += lax.dot_general(ds, qb, (((0,), (0,)), ((), ())),
                                                                  preferred_element_type=F32)

    kv_block = lambda h, i: (0, (h * hp // group) // kvp)
    return pl.pallas_call(
        body, name=name, grid=(n_heads // hp, S // tq),
        in_specs=[pl.BlockSpec((tq, hp * dqk), lambda h, i: (i, h)),
                  pl.BlockSpec((S, kvp * dqk), kv_block),
                  pl.BlockSpec((S, kvp * dv), kv_block),
                  pl.BlockSpec((tq, hp * dv), lambda h, i: (i, h)),
                  pl.BlockSpec((tq, hp * dv), lambda h, i: (i, h)),
                  pl.BlockSpec((tq, hp * LANE), lambda h, i: (i, h))] + [pl.BlockSpec(memory_space=pl.ANY) for _ in after],
        out_specs=[pl.BlockSpec((tq, hp * dqk), lambda h, i: (i, h)),
                   pl.BlockSpec((S, kvp * dqk), kv_block),
                   pl.BlockSpec((S, kvp * dv), kv_block)],
        out_shape=[jax.ShapeDtypeStruct((S, n_heads * dqk), F32),
                   jax.ShapeDtypeStruct((S, n_kv * dqk), F32),
                   jax.ShapeDtypeStruct((S, n_kv * dv), F32)],
        compiler_params=_params(("arbitrary", "arbitrary")),
    )(q, k, v, do, o, lse, *after)


def _adamw(name, w, g, m, v, after=(), rewrite_g=False):
    R, C = w.shape
    tile = _pick(R, max(8, (1 << 19) // C // 8 * 8), 8)
    c1 = 1.0 - ADAM_B1 ** ADAM_STEP
    c2 = 1.0 - ADAM_B2 ** ADAM_STEP
    n_out = 4 if rewrite_g else 3

    def body(w_ref, g_ref, m_ref, v_ref, *rest):
        d_ref, nm_ref, nv_ref = rest[len(after):len(after) + 3]
        gg = g_ref[...]
        nm = ADAM_B1 * m_ref[...] + (1.0 - ADAM_B1) * gg
        nv = ADAM_B2 * v_ref[...] + (1.0 - ADAM_B2) * (gg * gg)
        d_ref[...] = -ADAM_LR * ((nm / c1) / (jnp.sqrt(nv / c2) + ADAM_EPS) + ADAM_WD * w_ref[...])
        nm_ref[...] = nm
        nv_ref[...] = nv
        if rewrite_g:
            rest[-1][...] = gg

    spec = pl.BlockSpec((tile, C), lambda i: (i, 0))
    return pl.pallas_call(
        body, name=name, grid=(R // tile,), in_specs=[spec] * 4 + [pl.BlockSpec(memory_space=pl.ANY) for _ in after],
        out_specs=[spec] * n_out, out_shape=[jax.ShapeDtypeStruct((R, C), F32)] * n_out,
        compiler_params=_params(("parallel",)),
    )(w, g, m, v, *after)


def _flip(me, rel):
    return tuple(1 - p if r else p for p, r in zip(me, rel))


def _push(name, ins, out_shapes, transfers, aliases=None):
    ni, no, nt = len(ins), len(out_shapes), len(transfers)

    def body(*refs):
        in_refs, out_refs = refs[:ni], refs[ni:ni + no]
        send_sems, recv_sems = refs[ni + no], refs[ni + no + 1]
        me = (lax.axis_index("x"), lax.axis_index("y"), lax.axis_index("c"))
        copies = []
        for t, (si, sfn, oi, dfn, rel) in enumerate(transfers):
            src_ref = out_refs[si[1]] if isinstance(si, tuple) else in_refs[si]
            src = src_ref.at[sfn(*me)]
            dst = out_refs[oi].at[dfn(*me)]
            if rel is None:
                cp = pltpu.make_async_copy(src, dst, send_sems.at[t])
            else:
                cp = pltpu.make_async_remote_copy(src_ref=src, dst_ref=dst, send_sem=send_sems.at[t],
                                                  recv_sem=recv_sems.at[t], device_id=_flip(me, rel),
                                                  device_id_type=MESH)
            cp.start()
            copies.append(cp)
        for cp in copies:
            cp.wait()

    any_spec = pl.BlockSpec(memory_space=pl.ANY)
    return pl.pallas_call(
        body, name=name, in_specs=[any_spec] * ni, out_specs=[any_spec] * no, out_shape=out_shapes,
        scratch_shapes=[pltpu.SemaphoreType.DMA((nt,)), pltpu.SemaphoreType.DMA((nt,))],
        input_output_aliases=aliases or {},
        compiler_params=pltpu.CompilerParams(has_side_effects=True),
    )(*ins)


_HBM = pl.BlockSpec(memory_space=pltpu.HBM)
_SEM = pl.BlockSpec(memory_space=pltpu.SEMAPHORE)
_DATAFLOW = pltpu.SideEffectType.DATAFLOW_SIDE_EFFECTING


def _split_copies(refs, transfers, send_sems, recv_sems):
    me = (lax.axis_index("x"), lax.axis_index("y"), lax.axis_index("c"))
    return [pltpu.make_async_remote_copy(src_ref=refs[sb].at[sfn(*me)], dst_ref=refs[db].at[dfn(*me)],
                                         send_sem=send_sems.at[t], recv_sem=recv_sems.at[t],
                                         device_id=_flip(me, rel), device_id_type=MESH)
            for t, (sb, sfn, db, dfn, rel) in enumerate(transfers)]


def _push_start(name, bufs, transfers, after=()):
    nb, na, nt = len(bufs), len(after), len(transfers)

    def body(*refs):
        send_sems, recv_sems, token = refs[nb + na], refs[nb + na + 1], refs[-1]
        for cp in _split_copies(refs[nb + na + 2:2 * nb + na + 2], transfers, send_sems, recv_sems):
            cp.start()
        token[...] = jnp.zeros(token.shape, token.dtype)

    res = pl.pallas_call(
        body, name=name,
        out_shape=(pltpu.SemaphoreType.DMA((nt,)), pltpu.SemaphoreType.DMA((nt,)),
                   *[pltpu.HBM(b.shape, b.dtype) for b in bufs], jax.ShapeDtypeStruct((8, LANE), F32)),
        in_specs=[_HBM] * nb + [pl.BlockSpec(memory_space=pl.ANY)] * na,
        out_specs=(_SEM, _SEM, *[_HBM] * nb, pl.BlockSpec(memory_space=pltpu.VMEM)),
        input_output_aliases={i: i + 2 for i in range(nb)},
        compiler_params=pltpu.CompilerParams(has_side_effects=_DATAFLOW),
    )(*[pltpu.with_memory_space_constraint(b, pltpu.HBM) for b in bufs], *after)
    return res[0], res[1], list(res[2:2 + nb]), res[-1]


def _push_wait(name, send_sems, recv_sems, bufs, transfers, after):
    nb = len(bufs)

    def body(*refs):
        for cp in _split_copies(refs[:nb], transfers, refs[nb], refs[nb + 1]):
            cp.wait_send()
            cp.wait_recv()

    res = pl.pallas_call(
        body, name=name, out_shape=[pltpu.HBM(b.shape, b.dtype) for b in bufs],
        in_specs=[_HBM] * nb + [_SEM, _SEM, pl.BlockSpec(memory_space=pl.ANY)], out_specs=[_HBM] * nb,
        input_output_aliases={i: i for i in range(nb)},
        compiler_params=pltpu.CompilerParams(has_side_effects=_DATAFLOW),
    )(*bufs, send_sems, recv_sems, after)
    return list(res)


ICI_RELS = ((1, 0, 0), (0, 1, 0), (1, 1, 0))
SIBLING = (0, 0, 1)


def _chip(x, y):
    return 2 * x + y


def _cast_into_slot(name, where, w, after=()):
    R, C = w.shape
    tile = _pick(R, 512, BF16_ROWS)

    def body(where_ref, w_ref, *rest):
        rest[-1][...] = w_ref[...].astype(BF16)

    return pl.pallas_call(
        body, name=name,
        grid_spec=pltpu.PrefetchScalarGridSpec(
            num_scalar_prefetch=1, grid=(R // tile,),
            in_specs=[pl.BlockSpec((tile, C), lambda i, wh: (i, 0))] + [pl.BlockSpec(memory_space=pl.ANY) for _ in after],
            out_specs=pl.BlockSpec((None, tile, C), lambda i, wh: (wh[1], i, 0))),
        out_shape=jax.ShapeDtypeStruct((N_CHIPS, R, C), BF16),
        compiler_params=_params(("parallel",)),
    )(where, w, *after)


BF16_ROWS = 16


def _rows_split(shape2):
    return (shape2[0] // 2) % BF16_ROWS == 0


def _half_shape(shape2):
    R, C = shape2
    return (R // 2, C) if _rows_split(shape2) else (R, C // 2)


def _half(shape2, c):
    R, C = shape2
    if _rows_split(shape2):
        return (pl.ds(c * (R // 2), R // 2), slice(None))
    return (slice(None), pl.ds(pl.multiple_of(c * (C // 2), LANE), C // 2))


def _gather_casts(tag, where, shards, after=()):
    return [_cast_into_slot(f"cast_w_{tag}{w}", where, s, after) for w, s in enumerate(shards)]


def _gather_start(tag, own, after):
    transfers = []
    for w in range(len(own)):
        def idx(x, y, c, shape2=own[w].shape[1:]):
            return (_chip(x, y), *_half(shape2, c))
        transfers += [(w, idx, w, idx, rel) for rel in ICI_RELS]
    send_sems, recv_sems, bufs, token = _push_start(f"gather_{tag}_start", own, transfers, after)
    return (tag, send_sems, recv_sems, bufs, transfers), token


def _forward_transfers(part):
    second = []
    for w, p in enumerate(part):
        for rel in ICI_RELS:
            def idx(x, y, c, shape2=p.shape[1:], rel=rel):
                return (_chip(x ^ rel[0], y ^ rel[1]), *_half(shape2, c))
            second.append((w, idx, w, idx, SIBLING))
    return second


def _gather_finish(handle, after):
    tag, send_sems, recv_sems, bufs, transfers = handle
    part = _push_wait(f"gather_{tag}_wait", send_sems, recv_sems, bufs, transfers, after)
    outs = [jax.ShapeDtypeStruct(p.shape, p.dtype) for p in part]
    second = [(("out", sb), sfn, db, dfn, rel) for sb, sfn, db, dfn, rel in _forward_transfers(part)]
    return _push(f"gather_{tag}_d2d", list(part), outs, second, aliases={w: w for w in range(len(part))})


def _gather_mid(handle, after):
    tag, send_sems, recv_sems, bufs, transfers = handle
    part = _push_wait(f"gather_{tag}_wait", send_sems, recv_sems, bufs, transfers, after)
    second = _forward_transfers(part)
    s2, r2, bufs2, token = _push_start(f"gather_{tag}_d2d_start", part, second)
    return (tag, s2, r2, bufs2, second), token


def _gather_end(handle, after):
    tag, send_sems, recv_sems, bufs, transfers = handle
    return _push_wait(f"gather_{tag}_d2d_wait", send_sems, recv_sems, bufs, transfers, after)


def _reduce_begin(tag, grads, after=()):
    n = len(grads)
    land = [lax.empty((N_CHIPS, *_half_shape(g.shape[1:])), F32) for g in grads]
    t1 = []
    for w in range(n):
        t1.append((w, lambda x, y, c, shape2=grads[w].shape[1:]: (slice(None), *_half(shape2, 1 - c)), n + w,
                   lambda x, y, c: (slice(None), slice(None), slice(None)), SIBLING))
    send_sems, recv_sems, bufs, token = _push_start(f"reduce_{tag}_d2d_start", list(grads) + land, t1, after)
    return (tag, send_sems, recv_sems, bufs, t1), token


def _reduce_start(handle, where, after, after_start=()):
    tag, send_sems, recv_sems, bufs, t1 = handle
    n = len(bufs) // 2
    res = _push_wait(f"reduce_{tag}_d2d_wait", send_sems, recv_sems, bufs, t1, after)
    grads, from_sib = res[:n], res[n:]
    pair = [_pair_sum(f"pair_sum_{tag}{w}", where, grads[w], from_sib[w]) for w in range(n)]
    land = [lax.empty((3, *_half_shape(g.shape[1:])), BF16) for g in grads]
    t3 = []
    for w in range(n):
        for r, rel in enumerate(ICI_RELS):
            t3.append((w, lambda x, y, c, rel=rel: (_chip(x ^ rel[0], y ^ rel[1]), slice(None), slice(None)), n + w,
                       lambda x, y, c, r=r: (r, slice(None), slice(None)), rel))
    send_sems, recv_sems, bufs, token = _push_start(f"reduce_{tag}_start", pair + land, t3, after_start)
    return (tag, send_sems, recv_sems, bufs, t3, list(grads), list(from_sib)), token


def _reduce_finish(handle, where, after):
    tag, send_sems, recv_sems, bufs, t3, grads, from_sib = handle
    n = len(grads)
    from_chips = _push_wait(f"reduce_{tag}_wait", send_sems, recv_sems, bufs, t3, after)[n:]
    mine = [_chip_sum(f"chip_sum_{tag}{w}", where, grads[w], from_sib[w], from_chips[w]) for w in range(n)]
    t5 = []
    for w in range(n):
        def idx(x, y, c, shape2=grads[w].shape[1:]):
            return _half(shape2, c)
        t5.append((w, idx, w, idx, SIBLING))
    send_sems, recv_sems, bufs, token = _push_start(f"reduce_{tag}_swap_start", mine, t5)
    return (tag, send_sems, recv_sems, bufs, t5), token


def _reduce_end(handle, after):
    tag, send_sems, recv_sems, bufs, t5 = handle
    return _push_wait(f"reduce_{tag}_swap_wait", send_sems, recv_sems, bufs, t5, after)


def _pair_sum(name, where, g, from_sib):
    _, hr, hc = from_sib.shape
    tile = _pick(hr, max(16, (1 << 19) // hc // 16 * 16), 16)
    nb = hr // tile
    slot = lambda k, wh: (wh[1] + 1 + k) % N_CHIPS
    if _rows_split(g.shape[1:]):
        mine = lambda k, i, wh: (slot(k, wh), wh[0] * nb + i, 0)
    else:
        mine = lambda k, i, wh: (slot(k, wh), i, wh[0])

    def body(where_ref, g_ref, s_ref, o_ref):
        o_ref[...] = (g_ref[...] + s_ref[...]).astype(BF16)

    return pl.pallas_call(
        body, name=name,
        grid_spec=pltpu.PrefetchScalarGridSpec(
            num_scalar_prefetch=1, grid=(N_CHIPS - 1, nb),
            in_specs=[pl.BlockSpec((None, tile, hc), mine),
                      pl.BlockSpec((None, tile, hc), lambda k, i, wh: (slot(k, wh), i, 0))],
            out_specs=pl.BlockSpec((None, tile, hc), lambda k, i, wh: (slot(k, wh), i, 0))),
        out_shape=jax.ShapeDtypeStruct((N_CHIPS, hr, hc), BF16),
        compiler_params=_params(("parallel", "parallel")),
    )(where, g, from_sib)


def _chip_sum(name, where, g, from_sib, from_chips):
    _, R, C = g.shape
    _, hr, hc = from_sib.shape
    tile = _pick(hr, max(16, (1 << 19) // hc // 16 * 16), 16)
    nb = hr // tile
    if _rows_split((R, C)):
        mine = lambda i, wh: (wh[0] * nb + i, 0)
    else:
        mine = lambda i, wh: (i, wh[0])

    def body(where_ref, g_ref, s_ref, r_ref, o_ref):
        acc = g_ref[...] + s_ref[...]
        for r in range(3):
            acc = acc + r_ref[r].astype(F32)
        o_ref[...] = acc

    return pl.pallas_call(
        body, name=name,
        grid_spec=pltpu.PrefetchScalarGridSpec(
            num_scalar_prefetch=1, grid=(nb,),
            in_specs=[pl.BlockSpec((None, tile, hc), lambda i, wh: (wh[1], *mine(i, wh))),
                      pl.BlockSpec((None, tile, hc), lambda i, wh: (wh[1], i, 0)),
                      pl.BlockSpec((3, tile, hc), lambda i, wh: (0, i, 0))],
            out_specs=pl.BlockSpec((tile, hc), mine)),
        out_shape=jax.ShapeDtypeStruct((R, C), F32),
        compiler_params=_params(("parallel",)),
    )(where, g, from_sib, from_chips)


def _allgather8(name, blk, after=()):
    m_per, n = blk.shape
    rels = [(0, 0, 1), (1, 0, 0), (0, 1, 0), (1, 1, 0), (1, 0, 1), (0, 1, 1), (1, 1, 1)]

    def body(x_ref, *rest):
        out_ref, send_sems, recv_sems = rest[len(after):]
        me = (lax.axis_index("x"), lax.axis_index("y"), lax.axis_index("c"))
        my_rows = out_ref.at[pl.ds((4 * me[0] + 2 * me[1] + me[2]) * m_per, m_per), :]
        out_ref[pl.ds((4 * me[0] + 2 * me[1] + me[2]) * m_per, m_per), :] = x_ref[...]
        copies = []
        for t, rel in enumerate(rels):
            cp = pltpu.make_async_remote_copy(src_ref=x_ref, dst_ref=my_rows, send_sem=send_sems.at[t],
                                              recv_sem=recv_sems.at[t], device_id=_flip(me, rel),
                                              device_id_type=MESH)
            cp.start()
            copies.append(cp)
        for cp in copies:
            cp.wait()

    return pl.pallas_call(
        body, name=name, out_shape=jax.ShapeDtypeStruct((N_DEV * m_per, n), blk.dtype),
        in_specs=[pl.BlockSpec(memory_space=pltpu.VMEM)] + [pl.BlockSpec(memory_space=pl.ANY) for _ in after],
        out_specs=pl.BlockSpec(memory_space=pltpu.VMEM),
        scratch_shapes=[pltpu.SemaphoreType.DMA((7,)), pltpu.SemaphoreType.DMA((7,))],
        compiler_params=pltpu.CompilerParams(has_side_effects=True, vmem_limit_bytes=VMEM_LIMIT),
    )(blk, *after)


def _rope_tables(seq_len, dim, reps):
    rows = seq_len // GRID_W
    t = np.arange(seq_len)
    row, col = (t // GRID_W).astype(np.float32), (t % GRID_W).astype(np.float32)
    half = dim // 2
    inv = jnp.asarray(ROPE_THETA, F32) ** (-jnp.arange(0, half, 2, dtype=F32) / half)
    ang_r = jnp.asarray(row)[:, None] * inv[None, :]
    ang_c = jnp.asarray(col)[:, None] * inv[None, :]
    ang = jnp.concatenate([ang_r, ang_r, ang_c, ang_c], axis=-1)
    cos, sin = jnp.cos(ang), jnp.sin(ang)
    low = (np.arange(dim) % (dim // 2)) < (dim // 4)
    s_lo = jnp.where(jnp.asarray(low)[None, :], -sin, 0.0)
    s_hi = jnp.where(jnp.asarray(low)[None, :], 0.0, sin)
    del rows
    return tuple(jnp.tile(a, (1, reps)) for a in (cos, s_lo, s_hi))


def _make_rope(shift):
    def up(v):
        return pltpu.roll(v, LANE - shift, 1)

    def down(v):
        return pltpu.roll(v, shift, 1)

    @jax.custom_vjp
    def rope(v, cos, s_lo, s_hi):
        return v * cos + up(v) * s_lo + down(v) * s_hi

    def fwd(v, cos, s_lo, s_hi):
        return rope(v, cos, s_lo, s_hi), (cos, s_lo, s_hi)

    def bwd(res, dy):
        cos, s_lo, s_hi = res
        return dy * cos + down(dy * s_lo) + up(dy * s_hi), jnp.zeros_like(cos), jnp.zeros_like(cos), jnp.zeros_like(cos)

    rope.defvjp(fwd, bwd)
    return rope


_rope_a = _make_rope(HEAD_DIM // 4)
_rope_b = _make_rope(QK_ROPE // 4)


def _prep(segs, tabs, g_q, g_k, g_ckv):
    qa, ka, va, qn, qp, ckv, kpe = segs
    ta, tb = tabs
    q_a = jnp.concatenate([_rope_a(_rn(s) * g_q, *ta) for s in qa], axis=1)
    k_a = jnp.concatenate([_rope_a(_rn(s) * g_k, *ta) for s in ka], axis=1)
    cat = []
    for h in range(HB):
        cat += [qn[h], _rope_b(qp[h], *tb)]
    q_cat = jnp.concatenate(cat, axis=1)
    return q_a, k_a, va, q_cat, _rn(ckv) * g_ckv, _rope_b(kpe, *tb)


def _split_proj(proj):
    qa = [proj[:, O_QA + h * LANE:O_QA + (h + 1) * LANE] for h in range(HA)]
    ka = [proj[:, O_KA + h * LANE:O_KA + (h + 1) * LANE] for h in range(HKV)]
    va = proj[:, O_VA:O_QN]
    qn = [proj[:, O_QN + h * LANE:O_QN + (h + 1) * LANE] for h in range(HB)]
    qp = [proj[:, O_QP + h * LANE:O_QP + (h + 1) * LANE] for h in range(HB)]
    return qa, ka, va, qn, qp, proj[:, O_CKV:O_KPE], proj[:, O_KPE:W_INP]


def _permute_w_in(nat):
    K = nat.shape[1]
    qb = nat[W_QA + 2 * W_KA:W_QA + 2 * W_KA + W_QB].reshape(HB, QK_B, K)
    qp = jnp.concatenate([qb[:, QK_NOPE:], jnp.zeros((HB, LANE - QK_ROPE, K), nat.dtype)], axis=1)
    tail = nat[W_QA + 2 * W_KA + W_QB:]
    return jnp.concatenate([nat[:O_QN], qb[:, :QK_NOPE].reshape(HB * QK_NOPE, K), qp.reshape(HB * LANE, K),
                            tail, jnp.zeros((LANE - QK_ROPE, K), nat.dtype)], axis=0)


def _unpermute_pieces():
    pieces = [(0, 0, O_QN)]
    for h in range(HB):
        pieces.append((O_QN + h * QK_NOPE, O_QN + h * QK_B, QK_NOPE))
        pieces.append((O_QP + h * LANE, O_QN + h * QK_B + QK_NOPE, QK_ROPE))
    pieces.append((O_CKV, O_QN + W_QB, KV_RANK))
    pieces.append((O_KPE, O_QN + W_QB + KV_RANK, QK_ROPE))
    return pieces


def kernel(x, c, w_ada, b_ada, g_pre_attn, w_in, g_q_a, g_k_a, g_ckv, w_kv_b, g_out_a, g_out_b, w_out, g_post_attn, g_pre_mlp, w_mlp_in, w_mlp_out, g_post_mlp, loss_target, m_w_ada, m_b_ada, m_g_pre_attn, m_w_in, m_g_q_a, m_g_k_a, m_g_ckv, m_w_kv_b, m_g_out_a, m_g_out_b, m_w_out, m_g_post_attn, m_g_pre_mlp, m_w_mlp_in, m_w_mlp_out, m_g_post_mlp, v_w_ada, v_b_ada, v_g_pre_attn, v_w_in, v_g_q_a, v_g_k_a, v_g_ckv, v_w_kv_b, v_g_out_a, v_g_out_b, v_w_out, v_g_post_attn, v_g_pre_mlp, v_w_mlp_in, v_w_mlp_out, v_g_post_mlp):
    S, D = x.shape[1], x.shape[2]
    x2d, tgt = x[0], loss_target[0]
    ix, iy, ic = lax.axis_index("x"), lax.axis_index("y"), lax.axis_index("c")
    j_me = _chip(ix, iy)
    e_me = 4 * ix + 2 * iy + ic
    T = _pick(S, 512, 8)
    TB = _pick(S, 256, 8)

    where = jnp.stack([ic, j_me]).astype(jnp.int32)
    DFF = w_mlp_out.shape[1] * N_CHIPS

    c_all = _allgather8("gather_c", c.reshape(8, D // 8)).reshape(N_DEV, D)
    w_in_t = jnp.swapaxes(w_in[0], 0, 1)
    h_qkv, tok_a = _gather_start("qkv", _gather_casts("qkv", where, [w_in_t, w_kv_b[0]]), (c_all,))
    later = [_gather_casts(tag, where, [w[0]], (tok_a,))
             for tag, w in (("out", w_out), ("up", w_mlp_in), ("down", w_mlp_out))]

    def silu_fn(cc):
        return (cc * (1.0 / (1.0 + jnp.exp(-cc))),), ()

    (c_act,), _ = _rowwise("silu_c", silu_fn, [c_all], [], [(D, BF16)], [], N_DEV)
    ca_pad = jnp.concatenate([c_act, jnp.zeros_like(c_act)], axis=0)
    n_mod = w_ada.shape[2]
    b_cols = lax.dynamic_slice(b_ada, (0, j_me * n_mod), (1, n_mod))
    (mod_part,) = _mm_nn("mod_mm", ca_pad, w_ada, [F32], epilogue=lambda acc, b: (acc + b,),
                         extras=(jnp.broadcast_to(b_cols, (16, n_mod)),))
    mod_all = _allgather8("gather_mod", mod_part[:8] + tok_a[0, 0],
                          after=[c_[0] for c_ in later]).reshape(N_DEV, 8, n_mod)
    mod = jnp.concatenate([lax.dynamic_slice(mod_all, (2 * j, e_me, 0), (1, 1, n_mod))[0] for j in range(N_CHIPS)],
                          axis=1)

    m_qkv, tok_q = _gather_mid(h_qkv, mod_all)
    h_out, tok_b = _gather_start("out", later[0], (tok_q,))
    h_up, tok_c = _gather_start("up", later[1], (tok_b,))
    h_down, tok_d = _gather_start("down", later[2], (tok_c,))
    mod = mod + tok_d[0, 0]
    sh_a, sc_a, gt_a, sh_m, sc_m, gt_m = [mod[:, i * D:(i + 1) * D] for i in range(6)]

    tabs_a = _rope_tables(S, HEAD_DIM, 1)
    tabs_b = _rope_tables(S, QK_ROPE, LANE // QK_ROPE)

    def pre_attn_fn(xt, g, sc, sh):
        return (_rn(xt) * g * (1.0 + sc) + sh,), ()

    (h_b,), _ = _rowwise("pre_attn", pre_attn_fn, [x2d], [g_pre_attn, sc_a, sh_a], [(D, BF16)], [], T)
    g_in, g_kvb = _gather_end(m_qkv, h_b)
    wpt = _permute_w_in(g_in.reshape(W_IN, D))[None]
    (proj,) = _mm_nt("proj_mm", h_b, wpt, [F32])

    def prep_fn(pt, ca_, sla, sha, cb_, slb, shb, gq, gk, gc):
        return _prep(_split_proj(pt), ((ca_, sla, sha), (cb_, slb, shb)), gq, gk, gc), ()

    (qa_b, ka_b, va_b, qcat_b, ckvn_b, kpe_b), _ = _rowwise(
        "prep", prep_fn, [proj, *tabs_a, *tabs_b], [g_q_a, g_k_a, g_ckv],
        [(W_QA, BF16), (W_KA, BF16), (W_KA, BF16), (2 * HB * LANE, BF16), (KV_RANK, BF16), (LANE, BF16)], [], T)
    (kv,) = _mm_nn("kv_mm", ckvn_b, g_kvb, [F32])

    def kcat_fn(kvt, kpet):
        kc, vs = [], []
        for h in range(HB):
            kc += [kvt[:, 2 * h * LANE:(2 * h + 1) * LANE], kpet.astype(F32)]
            vs.append(kvt[:, (2 * h + 1) * LANE:(2 * h + 2) * LANE])
        return (jnp.concatenate(kc, axis=1), jnp.concatenate(vs, axis=1)), ()

    (kcat_b, vb_b), _ = _rowwise("kcat", kcat_fn, [kv, kpe_b], [], [(2 * HB * LANE, BF16), (HB * V_DIM, BF16)], [], T)

    TQ = _pick(S, 512, 16)
    sc_a_ = 1.0 / math.sqrt(HEAD_DIM)
    sc_b_ = 1.0 / math.sqrt(QK_B)
    o_a, lse_a = _attn_fwd("attn_a_fwd", qa_b, ka_b, va_b, HA, HA // HKV, HEAD_DIM, HEAD_DIM, sc_a_, TQ)
    m_out, tok_m = _gather_mid(h_out, o_a)
    o_b, lse_b = _attn_fwd("attn_b_fwd", qcat_b, kcat_b, vb_b, HB, 1, 2 * LANE, V_DIM, sc_b_, TQ, after=(tok_m,))
    m_up, tok_m = _gather_mid(h_up, o_b)

    def mix_fn(oa, ob, ga, gb):
        return (jnp.concatenate([_rn(oa) * ga, _rn(ob) * gb], axis=1),), ()

    (on_b,), _ = _rowwise("mix_norm", mix_fn, [o_a, o_b], [g_out_a, g_out_b], [(2 * W_QA, BF16)], [], T,
                          after=(tok_m,))
    (g_wout,) = _gather_end(m_out, on_b)
    w_out_f = g_wout.reshape(1, N_CHIPS * g_wout.shape[1], D)
    (o2,) = _mm_nn("out_mm", on_b, w_out_f, [F32])

    def mid(xt, o2t, gta, gpa, gpm, scm, shm):
        x1 = xt + gta * (_rn(o2t) * gpa)
        return x1, _rn(x1) * gpm * (1.0 + scm) + shm

    def mid_fn(*a):
        return mid(*a), ()

    mid_vecs = [gt_a, g_post_attn, g_pre_mlp, sc_m, sh_m]
    (x1, h2_b), _ = _rowwise("mid", mid_fn, [x2d, o2], mid_vecs, [(D, F32), (D, BF16)], [], T)
    (g_mlp_in,) = _gather_end(m_up, h2_b)
    relu_b, a_b = _mm_nn("mlp_in_mm", h2_b, g_mlp_in, [BF16, BF16],
                         epilogue=lambda acc: (jnp.maximum(acc, 0.0), jnp.square(jnp.maximum(acc, 0.0))))
    (g_mlp_out,) = _gather_finish(h_down, a_b)
    w_mlp_out_f = g_mlp_out.reshape(1, DFF, D)
    (y,) = _mm_nn("mlp_out_mm", a_b, w_mlp_out_f, [F32])

    def last_fn(yt, x1t, tt, gtm, gpo):
        n, r = _rn_parts(yt)
        err = x1t + gtm * (n * gpo) - tt
        dx2 = err * (1.0 / D)
        loss = 0.5 * jnp.sum(jnp.mean(err * err, axis=-1))
        s = _rows(dx2 * n)
        return (_rn_bwd(n, r, dx2 * (gtm * gpo)), dx2), (jnp.full((1, LANE), loss, F32), gpo * s, gtm * s)

    (dy_b, dx2), (loss_v, d_gt_m, d_g_post_mlp) = _rowwise(
        "loss_bwd", last_fn, [y, x1, tgt], [gt_m, g_post_mlp], [(D, BF16), (D, F32)], [LANE, D, D], TB)

    gw_mlp_out = _mm_tn("gw_mlp_out_mm", a_b, dy_b, 1).reshape(N_CHIPS, DFF // N_CHIPS, D)
    b_down, tok = _reduce_begin("down", [gw_mlp_out])
    (du_b,) = _mm_nt("d_mlp_out_mm", dy_b, w_mlp_out_f, [BF16],
                     epilogue=lambda acc, rt: (acc * (2.0 * rt.astype(F32)),), extras=(relu_b,), after=(tok,))
    r_down, tok = _reduce_start(b_down, where, du_b)
    gw_mlp_in = _mm_tn("gw_mlp_in_mm", h2_b, du_b, N_CHIPS, after=(tok,))
    b_up, tok = _reduce_begin("up", [gw_mlp_in])
    (dh2,) = _mm_nt("d_mlp_in_mm", du_b, g_mlp_in, [F32], after=(tok,))
    r_up, tok_up = _reduce_start(b_up, where, dh2)

    def mid_bwd_fn(xt, o2t, dh2t, dx2t, gta, gpa, gpm, scm, shm):
        n2, r2 = _rn_parts(o2t)
        n1, r1 = _rn_parts(xt + gta * (n2 * gpa))
        dx1 = dx2t + _rn_bwd(n1, r1, dh2t * (gpm * (1.0 + scm)))
        s1, s2 = _rows(dh2t * n1), _rows(dx1 * n2)
        do2 = _rn_bwd(n2, r2, dx1 * (gta * gpa))
        return (dx1, do2), (gpa * s2, gta * s2, (1.0 + scm) * s1, gpm * s1, _rows(dh2t))

    (dx1, do2_b), (d_gt_a, d_g_post_attn, d_g_pre_mlp, d_sc_m, d_sh_m) = _rowwise(
        "mid_bwd", mid_bwd_fn, [x2d, o2, dh2, dx2], mid_vecs, [(D, F32), (D, BF16)], [D] * 5, TB, after=(tok_up,))

    (d_on,) = _mm_nt("d_out_mm", do2_b, w_out_f, [F32])
    gw_out = _mm_tn("gw_out_mm", on_b, do2_b, 1).reshape(N_CHIPS, w_out.shape[1], D)

    def mix_bwd_fn(oa, ob, dont, ga, gb):
        na, ra = _rn_parts(oa)
        nb, rb = _rn_parts(ob)
        da, db = dont[:, :W_QA], dont[:, W_QA:]
        return (_rn_bwd(na, ra, da * ga), _rn_bwd(nb, rb, db * gb)), (_rows(da * na), _rows(db * nb))

    (doa_b, dob_b), (d_g_out_a, d_g_out_b) = _rowwise(
        "mix_bwd", mix_bwd_fn, [o_a, o_b, d_on], [g_out_a, g_out_b], [(W_QA, BF16), (HB * V_DIM, BF16)],
        [W_QA, HB * V_DIM], TB)

    dqa, dka, dva = _attn_bwd("attn_a_bwd", qa_b, ka_b, va_b, doa_b, o_a, lse_a, HA, HA // HKV, HEAD_DIM, HEAD_DIM,
                              sc_a_, TQ)
    dqcat, dkcat, dvb = _attn_bwd("attn_b_bwd", qcat_b, kcat_b, vb_b, dob_b, o_b, lse_b, HB, 1, 2 * LANE, V_DIM,
                                  sc_b_, TQ)

    def kcat_bwd_fn(dkc, dvt):
        dkv, dkpe = [], None
        for h in range(HB):
            dkv += [dkc[:, 2 * h * LANE:(2 * h + 1) * LANE], dvt[:, h * V_DIM:(h + 1) * V_DIM]]
            piece = dkc[:, (2 * h + 1) * LANE:(2 * h + 2) * LANE]
            dkpe = piece if dkpe is None else dkpe + piece
        return (jnp.concatenate(dkv, axis=1), dkpe), ()

    (dkv_b, dkpe), _ = _rowwise("kcat_bwd", kcat_bwd_fn, [dkcat, dvb], [], [(2 * HB * LANE, BF16), (LANE, F32)], [], T)
    (d_ckvn,) = _mm_nt("d_kv_mm", dkv_b, g_kvb, [F32])
    gw_kvb = _mm_tn("gw_kv_mm", ckvn_b, dkv_b, N_CHIPS)
    b_kvb, tok_kvb = _reduce_begin("okv", [gw_out, gw_kvb])

    def prep_bwd_fn(pt, dqat, dkat, dvat, dqct, dckt, dkpt, ca_, sla, sha, cb_, slb, shb, gq, gk, gc):
        tabs = ((ca_, sla, sha), (cb_, slb, shb))
        _, vjp = jax.vjp(lambda sg, a1, a2, a3: _prep(sg, tabs, a1, a2, a3), _split_proj(pt), gq, gk, gc)
        (dqa_s, dka_s, dva_s, dqn_s, dqp_s, dck_s, dkp_s), dgq, dgk, dgc = vjp((dqat, dkat, dvat, dqct, dckt, dkpt))
        dproj = jnp.concatenate([*dqa_s, *dka_s, dva_s, *dqn_s, *dqp_s, dck_s, dkp_s], axis=1)
        return (dproj,), (dgq, dgk, dgc)

    (dproj_b,), (d_g_q_a, d_g_k_a, d_g_ckv) = _rowwise(
        "prep_bwd", prep_bwd_fn, [proj, dqa, dka, dva, dqcat, d_ckvn, dkpe, *tabs_a, *tabs_b],
        [g_q_a, g_k_a, g_ckv], [(W_INP, BF16)], [HEAD_DIM, HEAD_DIM, KV_RANK], TB, after=(tok_kvb,))

    r_kvb, tok = _reduce_start(b_kvb, where, dproj_b)
    (dh,) = _mm_nn("d_proj_mm", dproj_b, wpt, [F32], after=(tok,))
    gw_in = _mm_tn_rows("gw_in_mm", dproj_b, h_b, W_IN, _unpermute_pieces()).reshape(N_CHIPS, W_IN // N_CHIPS, D)
    b_in, tok = _reduce_begin("in", [gw_in])

    def pre_attn_bwd_fn(xt, dht, dx1t, g, sc, sh):
        n, r = _rn_parts(xt)
        s = _rows(dht * n)
        return (dx1t + _rn_bwd(n, r, dht * (g * (1.0 + sc))),), ((1.0 + sc) * s, g * s, _rows(dht))

    (grad_x,), (d_g_pre_attn, d_sc_a, d_sh_a) = _rowwise(
        "pre_attn_bwd", pre_attn_bwd_fn, [x2d, dh, dx1], [g_pre_attn, sc_a, sh_a], [(D, F32)], [D] * 3, TB,
        after=(tok,))

    small_names = ["b_ada", "g_pre_attn", "g_q_a", "g_k_a", "g_ckv", "g_out_a", "g_out_b", "g_post_attn",
                   "g_pre_mlp", "g_post_mlp"]
    small_local = jnp.concatenate([d_sh_a, d_sc_a, d_gt_a, d_sh_m, d_sc_m, d_gt_m, d_g_pre_attn, d_g_q_a, d_g_k_a,
                                   d_g_ckv, d_g_out_a, d_g_out_b, d_g_post_attn, d_g_pre_mlp, d_g_post_mlp,
                                   loss_v[:, :8]], axis=1)
    n_sent = small_local.shape[1]
    n_small = n_sent - 8
    small_all = _allgather8("gather_small", small_local.reshape(8, n_sent // 8)).reshape(N_DEV, n_sent)

    def sum8_fn(t):
        return (), (jnp.sum(t, axis=0, keepdims=True),)

    r_in, tok_in = _reduce_start(b_in, where, grad_x, (small_all,))
    _, (sent_sum,) = _rowwise("sum_small", sum8_fn, [small_all], [], [], [n_sent], N_DEV, after=(tok_in,))
    small_sum, loss = sent_sum[:, :n_small], sent_sum[0, n_small]
    dmod_cols = lax.dynamic_slice(small_all, (0, j_me * n_mod), (N_DEV, n_mod))
    dmod_pad = jnp.concatenate([dmod_cols, jnp.zeros_like(dmod_cols)], axis=0).astype(BF16)
    gw_ada = _mm_tn("gw_ada_mm", ca_pad, dmod_pad, 1)[0]

    weights = dict(w_ada=w_ada, b_ada=b_ada, g_pre_attn=g_pre_attn, w_in=w_in, g_q_a=g_q_a, g_k_a=g_k_a, g_ckv=g_ckv,
                   w_kv_b=w_kv_b, g_out_a=g_out_a, g_out_b=g_out_b, w_out=w_out, g_post_attn=g_post_attn,
                   g_pre_mlp=g_pre_mlp, w_mlp_in=w_mlp_in, w_mlp_out=w_mlp_out, g_post_mlp=g_post_mlp)
    ms = dict(w_ada=m_w_ada, b_ada=m_b_ada, g_pre_attn=m_g_pre_attn, w_in=m_w_in, g_q_a=m_g_q_a, g_k_a=m_g_k_a,
              g_ckv=m_g_ckv, w_kv_b=m_w_kv_b, g_out_a=m_g_out_a, g_out_b=m_g_out_b, w_out=m_w_out,
              g_post_attn=m_g_post_attn, g_pre_mlp=m_g_pre_mlp, w_mlp_in=m_w_mlp_in, w_mlp_out=m_w_mlp_out,
              g_post_mlp=m_g_post_mlp)
    vs = dict(w_ada=v_w_ada, b_ada=v_b_ada, g_pre_attn=v_g_pre_attn, w_in=v_w_in, g_q_a=v_g_q_a, g_k_a=v_g_k_a,
              g_ckv=v_g_ckv, w_kv_b=v_w_kv_b, g_out_a=v_g_out_a, g_out_b=v_g_out_b, w_out=v_w_out,
              g_post_attn=v_g_post_attn, g_pre_mlp=v_g_pre_mlp, w_mlp_in=v_w_mlp_in, w_mlp_out=v_w_mlp_out,
              g_post_mlp=v_g_post_mlp)
    order = list(weights)
    grads, deltas, new_m, new_v = {}, {}, {}, {}

    def update(name, g, after=()):
        view = (lambda t: jnp.swapaxes(t, 0, 1)) if name == "w_in" else (lambda t: t)
        d_, m_, v_, *g_ = _adamw("adamw_" + name, view(weights[name][0]), g, view(ms[name][0]), view(vs[name][0]),
                                 after, rewrite_g=name != "w_ada")
        grads[name], deltas[name], new_m[name], new_v[name] = (view(t)[None] for t in ((g_ or [g])[0], d_, m_, v_))
        return d_

    chain = ((("w_mlp_out",), r_down), (("w_mlp_in",), r_up), (("w_out", "w_kv_b"), r_kvb), (("w_in",), r_in))
    swap, tok = _reduce_finish(chain[0][1], where, small_sum)
    last = update("w_ada", gw_ada, (tok,))
    for k, (names, _) in enumerate(chain):
        gs = _reduce_end(swap, last)
        if k + 1 < len(chain):
            swap, tok = _reduce_finish(chain[k + 1][1], where, gs[0])
        for name, g in zip(names, gs):
            last = update(name, g, (tok,))

    cat = lambda d: jnp.concatenate([d[n] for n in small_names], axis=1)
    d_, m_, v_ = _adamw("adamw_small", cat(weights), small_sum, cat(ms), cat(vs))
    off = 0
    for n in small_names:
        w_ = weights[n].shape[1]
        grads[n], deltas[n], new_m[n], new_v[n] = (a[:, off:off + w_] for a in (small_sum, d_, m_, v_))
        off += w_

    return (loss, grad_x[None], *[grads[n] for n in order], *[deltas[n] for n in order],
            *[new_m[n] for n in order], *[new_v[n] for n in order])
```

```python
import functools
import math

import numpy as np
import jax
import jax.numpy as jnp
from jax import lax
from jax.experimental import pallas as pl
from jax.experimental.pallas import tpu as pltpu

F32 = jnp.float32
BF16 = jnp.bfloat16
MESH = pl.DeviceIdType.MESH

EPS = 1e-6
GRID_W = 64
ROPE_THETA = 10000.0
HEAD_DIM = 128
HA = 8
HKV = 2
HB = 8
QK_NOPE = 128
QK_ROPE = 64
V_DIM = 128
KV_RANK = 512
QK_B = QK_NOPE + QK_ROPE
W_QA = HA * HEAD_DIM
W_KA = HKV * HEAD_DIM
W_QB = HB * QK_B
W_IN = W_QA + 2 * W_KA + W_QB + KV_RANK + QK_ROPE
N_CHIPS = 4
N_DEV = 8

LANE = 128
O_QA = 0
O_KA = O_QA + W_QA
O_VA = O_KA + W_KA
O_QN = O_VA + W_KA
O_QP = O_QN + HB * QK_NOPE
O_CKV = O_QP + HB * LANE
O_KPE = O_CKV + KV_RANK
W_INP = O_KPE + LANE

ADAM_LR = 0.001
ADAM_B1 = 0.9
ADAM_B2 = 0.999
ADAM_EPS = 1e-08
ADAM_WD = 0.01
ADAM_STEP = 10

VMEM_LIMIT = 56 * 1024 * 1024
HEADS_PER_STEP = 2


def _params(sem):
    return pltpu.CompilerParams(dimension_semantics=sem, vmem_limit_bytes=VMEM_LIMIT)


def _pick(dim, target, unit):
    best = None
    t = unit
    while t <= min(dim, target):
        if dim % t == 0:
            best = t
        t += unit
    return dim if best is None else best


def _rn(x):
    return x * lax.rsqrt(jnp.mean(x * x, axis=-1, keepdims=True) + EPS)


def _rn_parts(x):
    r = lax.rsqrt(jnp.mean(x * x, axis=-1, keepdims=True) + EPS)
    return x * r, r


def _rn_bwd(n, r, dn):
    return r * (dn - n * jnp.mean(dn * n, axis=-1, keepdims=True))


def _rows(t):
    return jnp.sum(t, axis=0, keepdims=True)


def _rowwise(name, fn, rows, vecs, row_outs, vec_outs, tile, after=()):
    S = rows[0].shape[0]
    assert S % tile == 0
    n_r, n_v, n_ro, n_vo, n_a = len(rows), len(vecs), len(row_outs), len(vec_outs), len(after)

    def body(*refs):
        r_in = refs[:n_r]
        v_in = refs[n_r:n_r + n_v]
        r_out = refs[n_r + n_v + n_a:n_r + n_v + n_a + n_ro]
        v_out = refs[n_r + n_v + n_a + n_ro:]
        ro, vo = fn(*[r[...] for r in r_in], *[v[...] for v in v_in])
        for ref, val in zip(r_out, ro):
            ref[...] = val.astype(ref.dtype)
        if n_vo:
            @pl.when(pl.program_id(0) == 0)
            def _():
                for ref in v_out:
                    ref[...] = jnp.zeros(ref.shape, ref.dtype)
            for ref, val in zip(v_out, vo):
                ref[...] += val

    in_specs = [pl.BlockSpec((tile, a.shape[1]), lambda i: (i, 0)) for a in rows]
    in_specs += [pl.BlockSpec(a.shape, lambda i: (0, 0)) for a in vecs]
    in_specs += [pl.BlockSpec(memory_space=pl.ANY) for _ in after]
    out_specs = [pl.BlockSpec((tile, w), lambda i: (i, 0)) for w, _ in row_outs]
    out_specs += [pl.BlockSpec((1, w), lambda i: (0, 0)) for w in vec_outs]
    out_shape = [jax.ShapeDtypeStruct((S, w), dt) for w, dt in row_outs]
    out_shape += [jax.ShapeDtypeStruct((1, w), F32) for w in vec_outs]
    res = pl.pallas_call(
        body, name=name, grid=(S // tile,), in_specs=in_specs, out_specs=out_specs, out_shape=out_shape,
        compiler_params=_params(("arbitrary",)),
    )(*rows, *vecs, *after)
    return res[:n_ro], res[n_ro:]


CONTRACT_TILE = 2048
MM_TILE_BYTES = 40 * 1024 * 1024


def _rows_that_fit(M, tile_bytes, most=1024):
    t = _pick(M, most, 16)
    while t % 32 == 0 and tile_bytes(t) > MM_TILE_BYTES:
        t //= 2
    return t


def _accumulate(acc, part, step, n_steps, finish):
    if n_steps == 1:
        finish(part)
        return

    @pl.when(step == 0)
    def _():
        acc[...] = part

    @pl.when((step > 0) & (step < n_steps - 1))
    def _():
        acc[...] += part

    @pl.when(step == n_steps - 1)
    def _():
        finish(acc[...] + part)


def _mm_nn(name, a, b3, out_dtypes, epilogue=None, extras=(), after=()):
    M, K = a.shape
    nB, K2, Nsh = b3.shape
    assert K == K2
    N = nB * Nsh
    tn, tk = _pick(Nsh, 512, LANE), _pick(K, CONTRACT_TILE, LANE)
    out_bytes = sum(jnp.dtype(dt).itemsize for dt in out_dtypes) + sum(e.dtype.itemsize for e in extras)
    tm = _rows_that_fit(M, lambda t: 2 * (t * tk * a.dtype.itemsize + tk * tn * b3.dtype.itemsize)
                        + t * tn * (4 * (K > tk) + 2 * out_bytes), most=2048)
    nps, nk, ne, no, na = Nsh // tn, K // tk, len(extras), len(out_dtypes), len(after)

    def body(a_ref, b_ref, *rest):
        e_refs, o_refs = rest[:ne], rest[ne + na:ne + na + no]

        def finish(total):
            vals = (total,) if epilogue is None else epilogue(total, *[e[...] for e in e_refs])
            for ref, val in zip(o_refs, vals):
                ref[...] = val.astype(ref.dtype)

        part = jnp.dot(a_ref[...].astype(BF16), b_ref[...].astype(BF16), preferred_element_type=F32)
        _accumulate(rest[-1], part, pl.program_id(2), nk, finish)

    in_specs = [pl.BlockSpec((tm, tk), lambda i, j, k: (i, k)),
                pl.BlockSpec((None, tk, tn), lambda i, j, k: (j // nps, k, j % nps))]
    in_specs += [pl.BlockSpec((tm, tn), lambda i, j, k: (i, j)) for _ in extras]
    in_specs += [pl.BlockSpec(memory_space=pl.ANY) for _ in after]
    return pl.pallas_call(
        body, name=name, grid=(M // tm, N // tn, nk), in_specs=in_specs,
        out_specs=[pl.BlockSpec((tm, tn), lambda i, j, k: (i, j)) for _ in out_dtypes],
        out_shape=[jax.ShapeDtypeStruct((M, N), dt) for dt in out_dtypes],
        scratch_shapes=[pltpu.VMEM((tm, tn), F32)] if nk > 1 else [],
        compiler_params=_params(("parallel", "parallel", "arbitrary")),
    )(a, b3, *extras, *after)


def _mm_nt(name, a, b3, out_dtypes, epilogue=None, extras=(), after=()):
    M, N = a.shape
    nB, K, Nsh = b3.shape
    assert N == nB * Nsh
    to, tn = _pick(K, 1536, LANE), _pick(Nsh, CONTRACT_TILE, LANE)
    out_bytes = sum(jnp.dtype(dt).itemsize for dt in out_dtypes) + sum(e.dtype.itemsize for e in extras)
    tm = _rows_that_fit(M, lambda t: 2 * (t * tn * a.dtype.itemsize + to * tn * b3.dtype.itemsize)
                        + t * to * (4 * (N > tn) + 2 * out_bytes))
    nps, nn, ne, no, na = Nsh // tn, N // tn, len(extras), len(out_dtypes), len(after)

    def body(a_ref, b_ref, *rest):
        e_refs, o_refs = rest[:ne], rest[ne + na:ne + na + no]

        def finish(total):
            vals = (total,) if epilogue is None else epilogue(total, *[e[...] for e in e_refs])
            for ref, val in zip(o_refs, vals):
                ref[...] = val.astype(ref.dtype)

        part = lax.dot_general(a_ref[...].astype(BF16), b_ref[...].astype(BF16),
                               (((1,), (1,)), ((), ())), preferred_element_type=F32)
        _accumulate(rest[-1], part, pl.program_id(2), nn, finish)

    in_specs = [pl.BlockSpec((tm, tn), lambda i, o, n: (i, n)),
                pl.BlockSpec((None, to, tn), lambda i, o, n: (n // nps, o, n % nps))]
    in_specs += [pl.BlockSpec((tm, to), lambda i, o, n: (i, o)) for _ in extras]
    in_specs += [pl.BlockSpec(memory_space=pl.ANY) for _ in after]
    return pl.pallas_call(
        body, name=name, grid=(M // tm, K // to, nn), in_specs=in_specs,
        out_specs=[pl.BlockSpec((tm, to), lambda i, o, n: (i, o)) for _ in out_dtypes],
        out_shape=[jax.ShapeDtypeStruct((M, K), dt) for dt in out_dtypes],
        scratch_shapes=[pltpu.VMEM((tm, to), F32)] if nn > 1 else [],
        compiler_params=_params(("parallel", "parallel", "arbitrary")),
    )(a, b3, *extras, *after)


def _mm_tn(name, a, g, nB, after=()):
    M, K = a.shape
    M2, N = g.shape
    assert M == M2 and N % nB == 0
    Nsh = N // nB
    tk, tn, tm = _pick(K, 1536, LANE), _pick(Nsh, 1024, LANE), _pick(M, CONTRACT_TILE, 16)
    nps, nm = Nsh // tn, M // tm

    def body(a_ref, g_ref, *rest):
        o_ref = rest[len(after)]

        def finish(total):
            o_ref[...] = total

        part = lax.dot_general(a_ref[...].astype(BF16), g_ref[...].astype(BF16),
                               (((0,), (0,)), ((), ())), preferred_element_type=F32)
        _accumulate(rest[-1], part, pl.program_id(2), nm, finish)

    return pl.pallas_call(
        body, name=name, grid=(K // tk, N // tn, nm),
        in_specs=[pl.BlockSpec((tm, tk), lambda k, n, m: (m, k)),
                  pl.BlockSpec((tm, tn), lambda k, n, m: (m, n))] + [pl.BlockSpec(memory_space=pl.ANY) for _ in after],
        out_specs=pl.BlockSpec((None, tk, tn), lambda k, n, m: (n // nps, k, n % nps)),
        out_shape=jax.ShapeDtypeStruct((nB, K, Nsh), F32),
        scratch_shapes=[pltpu.VMEM((tk, tn), F32)] if nm > 1 else [],
        compiler_params=_params(("parallel", "parallel", "arbitrary")),
    )(a, g, *after)


def _kv_mm(name, ckvn, w3, kpe):
    S, K = ckvn.shape
    nB, _, Nsh = w3.shape
    heads = Nsh // (2 * LANE)
    tm = _pick(S, 1024, 16)

    def body(a_ref, b_ref, kpe_ref, kc_ref, v_ref):
        acc = jnp.dot(a_ref[...], b_ref[...], preferred_element_type=F32)
        for h in range(heads):
            kc_ref[:, 2 * h * LANE:(2 * h + 1) * LANE] = acc[:, 2 * h * LANE:(2 * h + 1) * LANE].astype(BF16)
            kc_ref[:, (2 * h + 1) * LANE:(2 * h + 2) * LANE] = kpe_ref[...]
            v_ref[:, h * LANE:(h + 1) * LANE] = acc[:, (2 * h + 1) * LANE:(2 * h + 2) * LANE].astype(BF16)

    return pl.pallas_call(
        body, name=name, grid=(S // tm, nB),
        in_specs=[pl.BlockSpec((tm, K), lambda i, j: (i, 0)), pl.BlockSpec((None, K, Nsh), lambda i, j: (j, 0, 0)),
                  pl.BlockSpec((tm, LANE), lambda i, j: (i, 0))],
        out_specs=[pl.BlockSpec((tm, Nsh), lambda i, j: (i, j)), pl.BlockSpec((tm, Nsh // 2), lambda i, j: (i, j))],
        out_shape=[jax.ShapeDtypeStruct((S, nB * Nsh), BF16), jax.ShapeDtypeStruct((S, nB * Nsh // 2), BF16)],
        compiler_params=_params(("parallel", "arbitrary")),
    )(ckvn, w3, kpe)


def _kv_mm_bwd(name, dkc, dv, w3):
    S = dkc.shape[0]
    nB, K, Nsh = w3.shape
    heads = Nsh // (2 * LANE)
    tm = _pick(S, 1024, 16)

    def body(dkc_ref, dv_ref, b_ref, dck_ref, dkv_ref, dkpe_ref):
        j = pl.program_id(1)
        pieces, pe = [], None
        for h in range(heads):
            pieces += [dkc_ref[:, 2 * h * LANE:(2 * h + 1) * LANE], dv_ref[:, h * LANE:(h + 1) * LANE]]
            piece = dkc_ref[:, (2 * h + 1) * LANE:(2 * h + 2) * LANE]
            pe = piece if pe is None else pe + piece
        dkv = jnp.concatenate(pieces, axis=1).astype(BF16)
        dkv_ref[...] = dkv
        part = lax.dot_general(dkv, b_ref[...], (((1,), (1,)), ((), ())), preferred_element_type=F32)

        @pl.when(j == 0)
        def _():
            dck_ref[...] = part
            dkpe_ref[...] = pe

        @pl.when(j > 0)
        def _():
            dck_ref[...] += part
            dkpe_ref[...] += pe

    return pl.pallas_call(
        body, name=name, grid=(S // tm, nB),
        in_specs=[pl.BlockSpec((tm, Nsh), lambda i, j: (i, j)), pl.BlockSpec((tm, Nsh // 2), lambda i, j: (i, j)),
                  pl.BlockSpec((None, K, Nsh), lambda i, j: (j, 0, 0))],
        out_specs=[pl.BlockSpec((tm, K), lambda i, j: (i, 0)), pl.BlockSpec((tm, Nsh), lambda i, j: (i, j)),
                   pl.BlockSpec((tm, LANE), lambda i, j: (i, 0))],
        out_shape=[jax.ShapeDtypeStruct((S, K), F32), jax.ShapeDtypeStruct((S, nB * Nsh), BF16),
                   jax.ShapeDtypeStruct((S, LANE), F32)],
        compiler_params=_params(("parallel", "arbitrary")),
    )(dkc, dv, w3)


def _mm_tn_rows(name, a, g, n_rows, pieces):
    M, K = a.shape
    M2, N = g.shape
    assert M == M2
    tk, tn, tm = _pick(K, 1536, LANE), _pick(N, 512, LANE), _pick(M, CONTRACT_TILE, 16)
    nk, nm = K // tk, M // tm

    def body(a_ref, g_ref, o_ref, *acc):
        kk = pl.program_id(1)

        def finish(total):
            for t in range(nk):
                @pl.when(kk == t)
                def _(t=t):
                    for src, dst, n in pieces:
                        lo, hi = max(src, t * tk), min(src + n, (t + 1) * tk)
                        if lo < hi:
                            o_ref[dst + lo - src:dst + hi - src, :] = total[lo - t * tk:hi - t * tk, :]

        part = lax.dot_general(a_ref[...].astype(BF16), g_ref[...].astype(BF16),
                               (((0,), (0,)), ((), ())), preferred_element_type=F32)
        _accumulate(acc[0] if acc else None, part, pl.program_id(2), nm, finish)

    return pl.pallas_call(
        body, name=name, grid=(N // tn, nk, nm),
        in_specs=[pl.BlockSpec((tm, tk), lambda n, k, m: (m, k)),
                  pl.BlockSpec((tm, tn), lambda n, k, m: (m, n))],
        out_specs=pl.BlockSpec((n_rows, tn), lambda n, k, m: (0, n)),
        out_shape=jax.ShapeDtypeStruct((n_rows, N), F32),
        scratch_shapes=[pltpu.VMEM((tk, tn), F32)] if nm > 1 else [],
        compiler_params=_params(("arbitrary", "arbitrary", "arbitrary")),
    )(a, g)


def _attn_fwd(name, q, k, v, n_heads, group, dqk, dv, scale, tq, after=()):
    S = q.shape[0]
    hp = 2 * HEADS_PER_STEP
    kvp = max(1, hp // group)

    def body(q_ref, k_ref, v_ref, *rest):
        o_ref, lse_ref = rest[-2:]
        for j in range(hp):
            jk = j * kvp // hp
            s = lax.dot_general(q_ref[:, j * dqk:(j + 1) * dqk], k_ref[:, jk * dqk:(jk + 1) * dqk],
                                (((1,), (1,)), ((), ())), preferred_element_type=F32)
            m = jnp.max(s, axis=-1, keepdims=True)
            e = jnp.exp((s - m) * scale)
            l = jnp.sum(e, axis=-1, keepdims=True)
            o_ref[:, j * dv:(j + 1) * dv] = jnp.dot(e.astype(BF16), v_ref[:, jk * dv:(jk + 1) * dv],
                                                    preferred_element_type=F32) * (1.0 / l)
            lse_ref[:, j * LANE:(j + 1) * LANE] = jnp.broadcast_to(m * scale + jnp.log(l), (tq, LANE))

    kv_block = lambda h, i: (0, (h * hp // group) // kvp)
    return pl.pallas_call(
        body, name=name, grid=(n_heads // hp, S // tq),
        in_specs=[pl.BlockSpec((tq, hp * dqk), lambda h, i: (i, h)),
                  pl.BlockSpec((S, kvp * dqk), kv_block),
                  pl.BlockSpec((S, kvp * dv), kv_block)] + [pl.BlockSpec(memory_space=pl.ANY) for _ in after],
        out_specs=[pl.BlockSpec((tq, hp * dv), lambda h, i: (i, h)),
                   pl.BlockSpec((tq, hp * LANE), lambda h, i: (i, h))],
        out_shape=[jax.ShapeDtypeStruct((S, n_heads * dv), F32), jax.ShapeDtypeStruct((S, n_heads * LANE), F32)],
        compiler_params=_params(("arbitrary", "arbitrary")),
    )(q, k, v, *after)


def _attn_bwd(name, q, k, v, do, o, lse, n_heads, group, dqk, dv, scale, tq, after=()):
    S = q.shape[0]
    n_kv = n_heads // group
    hp, tq = (group, tq // 2) if group > HEADS_PER_STEP else (HEADS_PER_STEP, tq)
    kvp = max(1, hp // group)

    def body(q_ref, k_ref, v_ref, do_ref, o_ref, lse_ref, *rest):
        dq_ref, dk_ref, dv_ref = rest[len(after):]
        h, i = pl.program_id(0), pl.program_id(1)

        @pl.when(((h * hp) % group == 0) & (i == 0))
        def _():
            dk_ref[...] = jnp.zeros(dk_ref.shape, F32)
            dv_ref[...] = jnp.zeros(dv_ref.shape, F32)

        for j in range(hp):
            jk = j * kvp // hp
            qb, dob = q_ref[:, j * dqk:(j + 1) * dqk], do_ref[:, j * dv:(j + 1) * dv]
            kb, vb = k_ref[:, jk * dqk:(jk + 1) * dqk], v_ref[:, jk * dv:(jk + 1) * dv]
            s = lax.dot_general(qb, kb, (((1,), (1,)), ((), ())), preferred_element_type=F32)
            p = jnp.exp(s * scale - jnp.tile(lse_ref[:, j * LANE:(j + 1) * LANE], (1, S // LANE)))
            dp = lax.dot_general(dob, vb, (((1,), (1,)), ((), ())), preferred_element_type=F32)
            delta = jnp.sum(dob.astype(F32) * o_ref[:, j * dv:(j + 1) * dv], axis=-1, keepdims=True)
            ds = (p * ((dp - delta) * scale)).astype(BF16)
            dv_ref[:, jk * dv:(jk + 1) * dv] += lax.dot_general(p.astype(BF16), dob, (((0,), (0,)), ((), ())),
                                                                preferred_element_type=F32)
            dq_ref[:, j * dqk:(j + 1) * dqk] = jnp.dot(ds, kb, preferred_element_type=F32)
            dk_ref[:, jk * dqk:(jk + 1) * dqk] += lax.dot_general(ds, qb, (((0,), (0,)), ((), ())),
                                                                  preferred_element_type=F32)

    kv_block = lambda h, i: (0, (h * hp // group) // kvp)
    return pl.pallas_call(
        body, name=name, grid=(n_heads // hp, S // tq),
        in_specs=[pl.BlockSpec((tq, hp * dqk), lambda h, i: (i, h)),
                  pl.BlockSpec((S, kvp * dqk), kv_block),
                  pl.BlockSpec((S, kvp * dv), kv_block),
                  pl.BlockSpec((tq, hp * dv), lambda h, i: (i, h)),
                  pl.BlockSpec((tq, hp * dv), lambda h, i: (i, h)),
                  pl.BlockSpec((tq, hp * LANE), lambda h, i: (i, h))] + [pl.BlockSpec(memory_space=pl.ANY) for _ in after],
        out_specs=[pl.BlockSpec((tq, hp * dqk), lambda h, i: (i, h)),
                   pl.BlockSpec((S, kvp * dqk), kv_block),
                   pl.BlockSpec((S, kvp * dv), kv_block)],
        out_shape=[jax.ShapeDtypeStruct((S, n_heads * dqk), F32),
                   jax.ShapeDtypeStruct((S, n_kv * dqk), F32),
                   jax.ShapeDtypeStruct((S, n_kv * dv), F32)],
        compiler_params=_params(("arbitrary", "arbitrary")),
    )(q, k, v, do, o, lse, *after)


def _adamw(name, w, g, m, v, after=(), rewrite_g=False):
    R, C = w.shape
    tile = _pick(R, max(8, (1 << 19) // C // 8 * 8), 8)
    c1 = 1.0 - ADAM_B1 ** ADAM_STEP
    c2 = 1.0 - ADAM_B2 ** ADAM_STEP
    n_out = 4 if rewrite_g else 3

    def body(w_ref, g_ref, m_ref, v_ref, *rest):
        d_ref, nm_ref, nv_ref = rest[len(after):len(after) + 3]
        gg = g_ref[...]
        nm = ADAM_B1 * m_ref[...] + (1.0 - ADAM_B1) * gg
        nv = ADAM_B2 * v_ref[...] + (1.0 - ADAM_B2) * (gg * gg)
        d_ref[...] = -ADAM_LR * ((nm / c1) / (jnp.sqrt(nv / c2) + ADAM_EPS) + ADAM_WD * w_ref[...])
        nm_ref[...] = nm
        nv_ref[...] = nv
        if rewrite_g:
            rest[-1][...] = gg

    spec = pl.BlockSpec((tile, C), lambda i: (i, 0))
    return pl.pallas_call(
        body, name=name, grid=(R // tile,), in_specs=[spec] * 4 + [pl.BlockSpec(memory_space=pl.ANY) for _ in after],
        out_specs=[spec] * n_out, out_shape=[jax.ShapeDtypeStruct((R, C), F32)] * n_out,
        compiler_params=_params(("parallel",)),
    )(w, g, m, v, *after)


def _flip(me, rel):
    return tuple(1 - p if r else p for p, r in zip(me, rel))


def _push(name, ins, out_shapes, transfers, aliases=None):
    ni, no, nt = len(ins), len(out_shapes), len(transfers)

    def body(*refs):
        in_refs, out_refs = refs[:ni], refs[ni:ni + no]
        send_sems, recv_sems = refs[ni + no], refs[ni + no + 1]
        me = (lax.axis_index("x"), lax.axis_index("y"), lax.axis_index("c"))
        copies = []
        for t, (si, sfn, oi, dfn, rel) in enumerate(transfers):
            src_ref = out_refs[si[1]] if isinstance(si, tuple) else in_refs[si]
            src = src_ref.at[sfn(*me)]
            dst = out_refs[oi].at[dfn(*me)]
            if rel is None:
                cp = pltpu.make_async_copy(src, dst, send_sems.at[t])
            else:
                cp = pltpu.make_async_remote_copy(src_ref=src, dst_ref=dst, send_sem=send_sems.at[t],
                                                  recv_sem=recv_sems.at[t], device_id=_flip(me, rel),
                                                  device_id_type=MESH)
            cp.start()
            copies.append(cp)
        for cp in copies:
            cp.wait()

    any_spec = pl.BlockSpec(memory_space=pl.ANY)
    return pl.pallas_call(
        body, name=name, in_specs=[any_spec] * ni, out_specs=[any_spec] * no, out_shape=out_shapes,
        scratch_shapes=[pltpu.SemaphoreType.DMA((nt,)), pltpu.SemaphoreType.DMA((nt,))],
        input_output_aliases=aliases or {},
        compiler_params=pltpu.CompilerParams(has_side_effects=True),
    )(*ins)


_HBM = pl.BlockSpec(memory_space=pltpu.HBM)
_SEM = pl.BlockSpec(memory_space=pltpu.SEMAPHORE)
_DATAFLOW = pltpu.SideEffectType.DATAFLOW_SIDE_EFFECTING


def _split_copies(refs, transfers, send_sems, recv_sems):
    me = (lax.axis_index("x"), lax.axis_index("y"), lax.axis_index("c"))
    return [pltpu.make_async_remote_copy(src_ref=refs[sb].at[sfn(*me)], dst_ref=refs[db].at[dfn(*me)],
                                         send_sem=send_sems.at[t], recv_sem=recv_sems.at[t],
                                         device_id=_flip(me, rel), device_id_type=MESH)
            for t, (sb, sfn, db, dfn, rel) in enumerate(transfers)]


def _push_start(name, bufs, transfers, after=()):
    nb, na, nt = len(bufs), len(after), len(transfers)

    def body(*refs):
        send_sems, recv_sems, token = refs[nb + na], refs[nb + na + 1], refs[-1]
        for cp in _split_copies(refs[nb + na + 2:2 * nb + na + 2], transfers, send_sems, recv_sems):
            cp.start()
        token[...] = jnp.zeros(token.shape, token.dtype)

    res = pl.pallas_call(
        body, name=name,
        out_shape=(pltpu.SemaphoreType.DMA((nt,)), pltpu.SemaphoreType.DMA((nt,)),
                   *[pltpu.HBM(b.shape, b.dtype) for b in bufs], jax.ShapeDtypeStruct((8, LANE), F32)),
        in_specs=[_HBM] * nb + [pl.BlockSpec(memory_space=pl.ANY)] * na,
        out_specs=(_SEM, _SEM, *[_HBM] * nb, pl.BlockSpec(memory_space=pltpu.VMEM)),
        input_output_aliases={i: i + 2 for i in range(nb)},
        compiler_params=pltpu.CompilerParams(has_side_effects=_DATAFLOW),
    )(*[pltpu.with_memory_space_constraint(b, pltpu.HBM) for b in bufs], *after)
    return res[0], res[1], list(res[2:2 + nb]), res[-1]


def _push_wait(name, send_sems, recv_sems, bufs, transfers, after):
    nb = len(bufs)

    def body(*refs):
        for cp in _split_copies(refs[:nb], transfers, refs[nb], refs[nb + 1]):
            cp.wait_send()
            cp.wait_recv()

    res = pl.pallas_call(
        body, name=name, out_shape=[pltpu.HBM(b.shape, b.dtype) for b in bufs],
        in_specs=[_HBM] * nb + [_SEM, _SEM, pl.BlockSpec(memory_space=pl.ANY)], out_specs=[_HBM] * nb,
        input_output_aliases={i: i for i in range(nb)},
        compiler_params=pltpu.CompilerParams(has_side_effects=_DATAFLOW),
    )(*bufs, send_sems, recv_sems, after)
    return list(res)


ICI_RELS = ((1, 0, 0), (0, 1, 0), (1, 1, 0))
SIBLING = (0, 0, 1)


def _chip(x, y):
    return 2 * x + y


def _cast_into_slot(name, where, w, after=()):
    R, C = w.shape
    tile = _pick(R, 512, BF16_ROWS)

    def body(where_ref, w_ref, *rest):
        rest[-1][...] = w_ref[...].astype(BF16)

    return pl.pallas_call(
        body, name=name,
        grid_spec=pltpu.PrefetchScalarGridSpec(
            num_scalar_prefetch=1, grid=(R // tile,),
            in_specs=[pl.BlockSpec((tile, C), lambda i, wh: (i, 0))] + [pl.BlockSpec(memory_space=pl.ANY) for _ in after],
            out_specs=pl.BlockSpec((None, tile, C), lambda i, wh: (wh[1], i, 0))),
        out_shape=jax.ShapeDtypeStruct((N_CHIPS, R, C), BF16),
        compiler_params=_params(("parallel",)),
    )(where, w, *after)


BF16_ROWS = 16


def _rows_split(shape2):
    return (shape2[0] // 2) % BF16_ROWS == 0


def _half_shape(shape2):
    R, C = shape2
    return (R // 2, C) if _rows_split(shape2) else (R, C // 2)


def _half(shape2, c):
    R, C = shape2
    if _rows_split(shape2):
        return (pl.ds(c * (R // 2), R // 2), slice(None))
    return (slice(None), pl.ds(pl.multiple_of(c * (C // 2), LANE), C // 2))


def _gather_casts(tag, where, shards, after=()):
    return [_cast_into_slot(f"cast_w_{tag}{w}", where, s, after) for w, s in enumerate(shards)]


def _gather_start(tag, own, after):
    transfers = []
    for w in range(len(own)):
        def idx(x, y, c, shape2=own[w].shape[1:]):
            return (_chip(x, y), *_half(shape2, c))
        transfers += [(w, idx, w, idx, rel) for rel in ICI_RELS]
    send_sems, recv_sems, bufs, token = _push_start(f"gather_{tag}_start", own, transfers, after)
    return (tag, send_sems, recv_sems, bufs, transfers), token


def _forward_transfers(part):
    second = []
    for w, p in enumerate(part):
        for rel in ICI_RELS:
            def idx(x, y, c, shape2=p.shape[1:], rel=rel):
                return (_chip(x ^ rel[0], y ^ rel[1]), *_half(shape2, c))
            second.append((w, idx, w, idx, SIBLING))
    return second


def _gather_finish(handle, after):
    tag, send_sems, recv_sems, bufs, transfers = handle
    part = _push_wait(f"gather_{tag}_wait", send_sems, recv_sems, bufs, transfers, after)
    outs = [jax.ShapeDtypeStruct(p.shape, p.dtype) for p in part]
    second = [(("out", sb), sfn, db, dfn, rel) for sb, sfn, db, dfn, rel in _forward_transfers(part)]
    return _push(f"gather_{tag}_d2d", list(part), outs, second, aliases={w: w for w in range(len(part))})


def _gather_mid(handle, after):
    tag, send_sems, recv_sems, bufs, transfers = handle
    part = _push_wait(f"gather_{tag}_wait", send_sems, recv_sems, bufs, transfers, after)
    second = _forward_transfers(part)
    s2, r2, bufs2, token = _push_start(f"gather_{tag}_d2d_start", part, second)
    return (tag, s2, r2, bufs2, second), token


def _gather_end(handle, after):
    tag, send_sems, recv_sems, bufs, transfers = handle
    return _push_wait(f"gather_{tag}_d2d_wait", send_sems, recv_sems, bufs, transfers, after)


def _reduce_begin(tag, grads, after=()):
    n = len(grads)
    land = [lax.empty((N_CHIPS, *_half_shape(g.shape[1:])), F32) for g in grads]
    t1 = []
    for w in range(n):
        t1.append((w, lambda x, y, c, shape2=grads[w].shape[1:]: (slice(None), *_half(shape2, 1 - c)), n + w,
                   lambda x, y, c: (slice(None), slice(None), slice(None)), SIBLING))
    send_sems, recv_sems, bufs, token = _push_start(f"reduce_{tag}_d2d_start", list(grads) + land, t1, after)
    return (tag, send_sems, recv_sems, bufs, t1), token


def _reduce_start(handle, where, after, after_start=()):
    tag, send_sems, recv_sems, bufs, t1 = handle
    n = len(bufs) // 2
    res = _push_wait(f"reduce_{tag}_d2d_wait", send_sems, recv_sems, bufs, t1, after)
    grads, from_sib = res[:n], res[n:]
    pair = [_pair_sum(f"pair_sum_{tag}{w}", where, grads[w], from_sib[w]) for w in range(n)]
    land = [lax.empty((3, *_half_shape(g.shape[1:])), BF16) for g in grads]
    t3 = []
    for w in range(n):
        for r, rel in enumerate(ICI_RELS):
            t3.append((w, lambda x, y, c, rel=rel: (_chip(x ^ rel[0], y ^ rel[1]), slice(None), slice(None)), n + w,
                       lambda x, y, c, r=r: (r, slice(None), slice(None)), rel))
    send_sems, recv_sems, bufs, token = _push_start(f"reduce_{tag}_start", pair + land, t3, after_start)
    return (tag, send_sems, recv_sems, bufs, t3, list(grads), list(from_sib)), token


def _reduce_finish(handle, where, after):
    tag, send_sems, recv_sems, bufs, t3, grads, from_sib = handle
    n = len(grads)
    from_chips = _push_wait(f"reduce_{tag}_wait", send_sems, recv_sems, bufs, t3, after)[n:]
    mine = [_chip_sum(f"chip_sum_{tag}{w}", where, grads[w], from_sib[w], from_chips[w]) for w in range(n)]
    t5 = []
    for w in range(n):
        def idx(x, y, c, shape2=grads[w].shape[1:]):
            return _half(shape2, c)
        t5.append((w, idx, w, idx, SIBLING))
    send_sems, recv_sems, bufs, token = _push_start(f"reduce_{tag}_swap_start", mine, t5)
    return (tag, send_sems, recv_sems, bufs, t5), token


def _reduce_end(handle, after):
    tag, send_sems, recv_sems, bufs, t5 = handle
    return _push_wait(f"reduce_{tag}_swap_wait", send_sems, recv_sems, bufs, t5, after)


def _pair_sum(name, where, g, from_sib):
    _, hr, hc = from_sib.shape
    tile = _pick(hr, max(16, (1 << 19) // hc // 16 * 16), 16)
    nb = hr // tile
    slot = lambda k, wh: (wh[1] + 1 + k) % N_CHIPS
    if _rows_split(g.shape[1:]):
        mine = lambda k, i, wh: (slot(k, wh), wh[0] * nb + i, 0)
    else:
        mine = lambda k, i, wh: (slot(k, wh), i, wh[0])

    def body(where_ref, g_ref, s_ref, o_ref):
        o_ref[...] = (g_ref[...] + s_ref[...]).astype(BF16)

    return pl.pallas_call(
        body, name=name,
        grid_spec=pltpu.PrefetchScalarGridSpec(
            num_scalar_prefetch=1, grid=(N_CHIPS - 1, nb),
            in_specs=[pl.BlockSpec((None, tile, hc), mine),
                      pl.BlockSpec((None, tile, hc), lambda k, i, wh: (slot(k, wh), i, 0))],
            out_specs=pl.BlockSpec((None, tile, hc), lambda k, i, wh: (slot(k, wh), i, 0))),
        out_shape=jax.ShapeDtypeStruct((N_CHIPS, hr, hc), BF16),
        compiler_params=_params(("parallel", "parallel")),
    )(where, g, from_sib)


def _chip_sum(name, where, g, from_sib, from_chips):
    _, R, C = g.shape
    _, hr, hc = from_sib.shape
    tile = _pick(hr, max(16, (1 << 19) // hc // 16 * 16), 16)
    nb = hr // tile
    if _rows_split((R, C)):
        mine = lambda i, wh: (wh[0] * nb + i, 0)
    else:
        mine = lambda i, wh: (i, wh[0])

    def body(where_ref, g_ref, s_ref, r_ref, o_ref):
        acc = g_ref[...] + s_ref[...]
        for r in range(3):
            acc = acc + r_ref[r].astype(F32)
        o_ref[...] = acc

    return pl.pallas_call(
        body, name=name,
        grid_spec=pltpu.PrefetchScalarGridSpec(
            num_scalar_prefetch=1, grid=(nb,),
            in_specs=[pl.BlockSpec((None, tile, hc), lambda i, wh: (wh[1], *mine(i, wh))),
                      pl.BlockSpec((None, tile, hc), lambda i, wh: (wh[1], i, 0)),
                      pl.BlockSpec((3, tile, hc), lambda i, wh: (0, i, 0))],
            out_specs=pl.BlockSpec((tile, hc), mine)),
        out_shape=jax.ShapeDtypeStruct((R, C), F32),
        compiler_params=_params(("parallel",)),
    )(where, g, from_sib, from_chips)


def _allgather8(name, blk, after=()):
    m_per, n = blk.shape
    rels = [(0, 0, 1), (1, 0, 0), (0, 1, 0), (1, 1, 0), (1, 0, 1), (0, 1, 1), (1, 1, 1)]

    def body(x_ref, *rest):
        out_ref, send_sems, recv_sems = rest[len(after):]
        me = (lax.axis_index("x"), lax.axis_index("y"), lax.axis_index("c"))
        my_rows = out_ref.at[pl.ds((4 * me[0] + 2 * me[1] + me[2]) * m_per, m_per), :]
        out_ref[pl.ds((4 * me[0] + 2 * me[1] + me[2]) * m_per, m_per), :] = x_ref[...]
        copies = []
        for t, rel in enumerate(rels):
            cp = pltpu.make_async_remote_copy(src_ref=x_ref, dst_ref=my_rows, send_sem=send_sems.at[t],
                                              recv_sem=recv_sems.at[t], device_id=_flip(me, rel),
                                              device_id_type=MESH)
            cp.start()
            copies.append(cp)
        for cp in copies:
            cp.wait()

    return pl.pallas_call(
        body, name=name, out_shape=jax.ShapeDtypeStruct((N_DEV * m_per, n), blk.dtype),
        in_specs=[pl.BlockSpec(memory_space=pltpu.VMEM)] + [pl.BlockSpec(memory_space=pl.ANY) for _ in after],
        out_specs=pl.BlockSpec(memory_space=pltpu.VMEM),
        scratch_shapes=[pltpu.SemaphoreType.DMA((7,)), pltpu.SemaphoreType.DMA((7,))],
        compiler_params=pltpu.CompilerParams(has_side_effects=True, vmem_limit_bytes=VMEM_LIMIT),
    )(blk, *after)


def _rope_tables(seq_len, dim, reps):
    rows = seq_len // GRID_W
    t = np.arange(seq_len)
    row, col = (t // GRID_W).astype(np.float32), (t % GRID_W).astype(np.float32)
    half = dim // 2
    inv = jnp.asarray(ROPE_THETA, F32) ** (-jnp.arange(0, half, 2, dtype=F32) / half)
    ang_r = jnp.asarray(row)[:, None] * inv[None, :]
    ang_c = jnp.asarray(col)[:, None] * inv[None, :]
    ang = jnp.concatenate([ang_r, ang_r, ang_c, ang_c], axis=-1)
    cos, sin = jnp.cos(ang), jnp.sin(ang)
    low = (np.arange(dim) % (dim // 2)) < (dim // 4)
    s_lo = jnp.where(jnp.asarray(low)[None, :], -sin, 0.0)
    s_hi = jnp.where(jnp.asarray(low)[None, :], 0.0, sin)
    del rows
    return tuple(jnp.tile(a, (1, reps)) for a in (cos, s_lo, s_hi))


def _make_rope(shift):
    def up(v):
        return pltpu.roll(v, LANE - shift, 1)

    def down(v):
        return pltpu.roll(v, shift, 1)

    @jax.custom_vjp
    def rope(v, cos, s_lo, s_hi):
        return v * cos + up(v) * s_lo + down(v) * s_hi

    def fwd(v, cos, s_lo, s_hi):
        return rope(v, cos, s_lo, s_hi), (cos, s_lo, s_hi)

    def bwd(res, dy):
        cos, s_lo, s_hi = res
        return dy * cos + down(dy * s_lo) + up(dy * s_hi), jnp.zeros_like(cos), jnp.zeros_like(cos), jnp.zeros_like(cos)

    rope.defvjp(fwd, bwd)
    return rope


_rope_a = _make_rope(HEAD_DIM // 4)
_rope_b = _make_rope(QK_ROPE // 4)


def _prep(segs, tabs, g_q, g_k, g_ckv):
    qa, ka, va, qn, qp, ckv, kpe = segs
    ta, tb = tabs
    q_a = jnp.concatenate([_rope_a(_rn(s) * g_q, *ta) for s in qa], axis=1)
    k_a = jnp.concatenate([_rope_a(_rn(s) * g_k, *ta) for s in ka], axis=1)
    cat = []
    for h in range(HB):
        cat += [qn[h], _rope_b(qp[h], *tb)]
    q_cat = jnp.concatenate(cat, axis=1)
    return q_a, k_a, va, q_cat, _rn(ckv) * g_ckv, _rope_b(kpe, *tb)


def _split_proj(proj):
    qa = [proj[:, O_QA + h * LANE:O_QA + (h + 1) * LANE] for h in range(HA)]
    ka = [proj[:, O_KA + h * LANE:O_KA + (h + 1) * LANE] for h in range(HKV)]
    va = proj[:, O_VA:O_QN]
    qn = [proj[:, O_QN + h * LANE:O_QN + (h + 1) * LANE] for h in range(HB)]
    qp = [proj[:, O_QP + h * LANE:O_QP + (h + 1) * LANE] for h in range(HB)]
    return qa, ka, va, qn, qp, proj[:, O_CKV:O_KPE], proj[:, O_KPE:W_INP]


def _permute_w_in(nat):
    K = nat.shape[1]
    qb = nat[W_QA + 2 * W_KA:W_QA + 2 * W_KA + W_QB].reshape(HB, QK_B, K)
    qp = jnp.concatenate([qb[:, QK_NOPE:], jnp.zeros((HB, LANE - QK_ROPE, K), nat.dtype)], axis=1)
    tail = nat[W_QA + 2 * W_KA + W_QB:]
    return jnp.concatenate([nat[:O_QN], qb[:, :QK_NOPE].reshape(HB * QK_NOPE, K), qp.reshape(HB * LANE, K),
                            tail, jnp.zeros((LANE - QK_ROPE, K), nat.dtype)], axis=0)


def _unpermute_pieces():
    pieces = [(0, 0, O_QN)]
    for h in range(HB):
        pieces.append((O_QN + h * QK_NOPE, O_QN + h * QK_B, QK_NOPE))
        pieces.append((O_QP + h * LANE, O_QN + h * QK_B + QK_NOPE, QK_ROPE))
    pieces.append((O_CKV, O_QN + W_QB, KV_RANK))
    pieces.append((O_KPE, O_QN + W_QB + KV_RANK, QK_ROPE))
    return pieces


def kernel(x, c, w_ada, b_ada, g_pre_attn, w_in, g_q_a, g_k_a, g_ckv, w_kv_b, g_out_a, g_out_b, w_out, g_post_attn, g_pre_mlp, w_mlp_in, w_mlp_out, g_post_mlp, loss_target, m_w_ada, m_b_ada, m_g_pre_attn, m_w_in, m_g_q_a, m_g_k_a, m_g_ckv, m_w_kv_b, m_g_out_a, m_g_out_b, m_w_out, m_g_post_attn, m_g_pre_mlp, m_w_mlp_in, m_w_mlp_out, m_g_post_mlp, v_w_ada, v_b_ada, v_g_pre_attn, v_w_in, v_g_q_a, v_g_k_a, v_g_ckv, v_w_kv_b, v_g_out_a, v_g_out_b, v_w_out, v_g_post_attn, v_g_pre_mlp, v_w_mlp_in, v_w_mlp_out, v_g_post_mlp):
    S, D = x.shape[1], x.shape[2]
    x2d, tgt = x[0], loss_target[0]
    ix, iy, ic = lax.axis_index("x"), lax.axis_index("y"), lax.axis_index("c")
    j_me = _chip(ix, iy)
    e_me = 4 * ix + 2 * iy + ic
    T = _pick(S, 512, 8)
    TB = _pick(S, 256, 8)

    where = jnp.stack([ic, j_me]).astype(jnp.int32)
    DFF = w_mlp_out.shape[1] * N_CHIPS

    c_all = _allgather8("gather_c", c.reshape(8, D // 8)).reshape(N_DEV, D)
    w_in_t = jnp.swapaxes(w_in[0], 0, 1)
    h_qkv, tok_a = _gather_start("qkv", _gather_casts("qkv", where, [w_in_t, w_kv_b[0]]), (c_all,))
    later = [_gather_casts(tag, where, [w[0]], (tok_a,))
             for tag, w in (("out", w_out), ("up", w_mlp_in), ("down", w_mlp_out))]

    def silu_fn(cc):
        return (cc * (1.0 / (1.0 + jnp.exp(-cc))),), ()

    (c_act,), _ = _rowwise("silu_c", silu_fn, [c_all], [], [(D, BF16)], [], N_DEV)
    ca_pad = jnp.concatenate([c_act, jnp.zeros_like(c_act)], axis=0)
    n_mod = w_ada.shape[2]
    b_cols = lax.dynamic_slice(b_ada, (0, j_me * n_mod), (1, n_mod))
    (mod_part,) = _mm_nn("mod_mm", ca_pad, w_ada, [F32], epilogue=lambda acc, b: (acc + b,),
                         extras=(jnp.broadcast_to(b_cols, (16, n_mod)),))
    mod_all = _allgather8("gather_mod", mod_part[:8] + tok_a[0, 0],
                          after=[c_[0] for c_ in later]).reshape(N_DEV, 8, n_mod)
    mod = jnp.concatenate([lax.dynamic_slice(mod_all, (2 * j, e_me, 0), (1, 1, n_mod))[0] for j in range(N_CHIPS)],
                          axis=1)

    m_qkv, tok_q = _gather_mid(h_qkv, mod_all)
    h_out, tok_b = _gather_start("out", later[0], (tok_q,))
    h_up, tok_c = _gather_start("up", later[1], (tok_b,))
    h_down, tok_d = _gather_start("down", later[2], (tok_c,))
    mod = mod + tok_d[0, 0]
    sh_a, sc_a, gt_a, sh_m, sc_m, gt_m = [mod[:, i * D:(i + 1) * D] for i in range(6)]

    tabs_a = _rope_tables(S, HEAD_DIM, 1)
    tabs_b = _rope_tables(S, QK_ROPE, LANE // QK_ROPE)

    def pre_attn_fn(xt, g, sc, sh):
        return (_rn(xt) * g * (1.0 + sc) + sh,), ()

    (h_b,), _ = _rowwise("pre_attn", pre_attn_fn, [x2d], [g_pre_attn, sc_a, sh_a], [(D, BF16)], [], T)
    g_in, g_kvb = _gather_end(m_qkv, h_b)
    wpt = _permute_w_in(g_in.reshape(W_IN, D))[None]
    (proj,) = _mm_nt("proj_mm", h_b, wpt, [F32])

    def prep_fn(pt, ca_, sla, sha, cb_, slb, shb, gq, gk, gc):
        return _prep(_split_proj(pt), ((ca_, sla, sha), (cb_, slb, shb)), gq, gk, gc), ()

    (qa_b, ka_b, va_b, qcat_b, ckvn_b, kpe_b), _ = _rowwise(
        "prep", prep_fn, [proj, *tabs_a, *tabs_b], [g_q_a, g_k_a, g_ckv],
        [(W_QA, BF16), (W_KA, BF16), (W_KA, BF16), (2 * HB * LANE, BF16), (KV_RANK, BF16), (LANE, BF16)], [], T)
    kcat_b, vb_b = _kv_mm("kv_mm", ckvn_b, g_kvb, kpe_b)

    TQ = _pick(S, 512, 16)
    sc_a_ = 1.0 / math.sqrt(HEAD_DIM)
    sc_b_ = 1.0 / math.sqrt(QK_B)
    o_a, lse_a = _attn_fwd("attn_a_fwd", qa_b, ka_b, va_b, HA, HA // HKV, HEAD_DIM, HEAD_DIM, sc_a_, TQ)
    m_out, tok_m = _gather_mid(h_out, o_a)
    o_b, lse_b = _attn_fwd("attn_b_fwd", qcat_b, kcat_b, vb_b, HB, 1, 2 * LANE, V_DIM, sc_b_, TQ, after=(tok_m,))
    m_up, tok_m = _gather_mid(h_up, o_b)

    def mix_fn(oa, ob, ga, gb):
        return (jnp.concatenate([_rn(oa) * ga, _rn(ob) * gb], axis=1),), ()

    (on_b,), _ = _rowwise("mix_norm", mix_fn, [o_a, o_b], [g_out_a, g_out_b], [(2 * W_QA, BF16)], [], T,
                          after=(tok_m,))
    (g_wout,) = _gather_end(m_out, on_b)
    w_out_f = g_wout.reshape(1, N_CHIPS * g_wout.shape[1], D)
    (o2,) = _mm_nn("out_mm", on_b, w_out_f, [F32])

    def mid(xt, o2t, gta, gpa, gpm, scm, shm):
        x1 = xt + gta * (_rn(o2t) * gpa)
        return x1, _rn(x1) * gpm * (1.0 + scm) + shm

    def mid_fn(*a):
        return mid(*a), ()

    mid_vecs = [gt_a, g_post_attn, g_pre_mlp, sc_m, sh_m]
    (x1, h2_b), _ = _rowwise("mid", mid_fn, [x2d, o2], mid_vecs, [(D, F32), (D, BF16)], [], T)
    (g_mlp_in,) = _gather_end(m_up, h2_b)
    relu_b, a_b = _mm_nn("mlp_in_mm", h2_b, g_mlp_in, [BF16, BF16],
                         epilogue=lambda acc: (jnp.maximum(acc, 0.0), jnp.square(jnp.maximum(acc, 0.0))))
    (g_mlp_out,) = _gather_finish(h_down, a_b)
    w_mlp_out_f = g_mlp_out.reshape(1, DFF, D)
    (y,) = _mm_nn("mlp_out_mm", a_b, w_mlp_out_f, [F32])

    def last_fn(yt, x1t, tt, gtm, gpo):
        n, r = _rn_parts(yt)
        err = x1t + gtm * (n * gpo) - tt
        dx2 = err * (1.0 / D)
        loss = 0.5 * jnp.sum(jnp.mean(err * err, axis=-1))
        s = _rows(dx2 * n)
        return (_rn_bwd(n, r, dx2 * (gtm * gpo)), dx2), (jnp.full((1, LANE), loss, F32), gpo * s, gtm * s)

    (dy_b, dx2), (loss_v, d_gt_m, d_g_post_mlp) = _rowwise(
        "loss_bwd", last_fn, [y, x1, tgt], [gt_m, g_post_mlp], [(D, BF16), (D, F32)], [LANE, D, D], TB)

    gw_mlp_out = _mm_tn("gw_mlp_out_mm", a_b, dy_b, 1).reshape(N_CHIPS, DFF // N_CHIPS, D)
    b_down, tok = _reduce_begin("down", [gw_mlp_out])
    (du_b,) = _mm_nt("d_mlp_out_mm", dy_b, w_mlp_out_f, [BF16],
                     epilogue=lambda acc, rt: (acc * (2.0 * rt.astype(F32)),), extras=(relu_b,), after=(tok,))
    r_down, tok = _reduce_start(b_down, where, du_b)
    gw_mlp_in = _mm_tn("gw_mlp_in_mm", h2_b, du_b, N_CHIPS, after=(tok,))
    b_up, tok = _reduce_begin("up", [gw_mlp_in])
    (dh2,) = _mm_nt("d_mlp_in_mm", du_b, g_mlp_in, [F32], after=(tok,))
    r_up, tok_up = _reduce_start(b_up, where, dh2)

    def mid_bwd_fn(xt, o2t, dh2t, dx2t, gta, gpa, gpm, scm, shm):
        n2, r2 = _rn_parts(o2t)
        n1, r1 = _rn_parts(xt + gta * (n2 * gpa))
        dx1 = dx2t + _rn_bwd(n1, r1, dh2t * (gpm * (1.0 + scm)))
        s1, s2 = _rows(dh2t * n1), _rows(dx1 * n2)
        do2 = _rn_bwd(n2, r2, dx1 * (gta * gpa))
        return (dx1, do2), (gpa * s2, gta * s2, (1.0 + scm) * s1, gpm * s1, _rows(dh2t))

    (dx1, do2_b), (d_gt_a, d_g_post_attn, d_g_pre_mlp, d_sc_m, d_sh_m) = _rowwise(
        "mid_bwd", mid_bwd_fn, [x2d, o2, dh2, dx2], mid_vecs, [(D, F32), (D, BF16)], [D] * 5, TB, after=(tok_up,))

    (d_on,) = _mm_nt("d_out_mm", do2_b, w_out_f, [F32])
    gw_out = _mm_tn("gw_out_mm", on_b, do2_b, 1).reshape(N_CHIPS, w_out.shape[1], D)

    def mix_bwd_fn(oa, ob, dont, ga, gb):
        na, ra = _rn_parts(oa)
        nb, rb = _rn_parts(ob)
        da, db = dont[:, :W_QA], dont[:, W_QA:]
        return (_rn_bwd(na, ra, da * ga), _rn_bwd(nb, rb, db * gb)), (_rows(da * na), _rows(db * nb))

    (doa_b, dob_b), (d_g_out_a, d_g_out_b) = _rowwise(
        "mix_bwd", mix_bwd_fn, [o_a, o_b, d_on], [g_out_a, g_out_b], [(W_QA, BF16), (HB * V_DIM, BF16)],
        [W_QA, HB * V_DIM], TB)

    dqa, dka, dva = _attn_bwd("attn_a_bwd", qa_b, ka_b, va_b, doa_b, o_a, lse_a, HA, HA // HKV, HEAD_DIM, HEAD_DIM,
                              sc_a_, TQ)
    dqcat, dkcat, dvb = _attn_bwd("attn_b_bwd", qcat_b, kcat_b, vb_b, dob_b, o_b, lse_b, HB, 1, 2 * LANE, V_DIM,
                                  sc_b_, TQ)

    d_ckvn, dkv_b, dkpe = _kv_mm_bwd("d_kv_mm", dkcat, dvb, g_kvb)
    gw_kvb = _mm_tn("gw_kv_mm", ckvn_b, dkv_b, N_CHIPS)
    b_kvb, tok_kvb = _reduce_begin("okv", [gw_out, gw_kvb])

    def prep_bwd_fn(pt, dqat, dkat, dvat, dqct, dckt, dkpt, ca_, sla, sha, cb_, slb, shb, gq, gk, gc):
        tabs = ((ca_, sla, sha), (cb_, slb, shb))
        _, vjp = jax.vjp(lambda sg, a1, a2, a3: _prep(sg, tabs, a1, a2, a3), _split_proj(pt), gq, gk, gc)
        (dqa_s, dka_s, dva_s, dqn_s, dqp_s, dck_s, dkp_s), dgq, dgk, dgc = vjp((dqat, dkat, dvat, dqct, dckt, dkpt))
        dproj = jnp.concatenate([*dqa_s, *dka_s, dva_s, *dqn_s, *dqp_s, dck_s, dkp_s], axis=1)
        return (dproj,), (dgq, dgk, dgc)

    (dproj_b,), (d_g_q_a, d_g_k_a, d_g_ckv) = _rowwise(
        "prep_bwd", prep_bwd_fn, [proj, dqa, dka, dva, dqcat, d_ckvn, dkpe, *tabs_a, *tabs_b],
        [g_q_a, g_k_a, g_ckv], [(W_INP, BF16)], [HEAD_DIM, HEAD_DIM, KV_RANK], TB, after=(tok_kvb,))

    r_kvb, tok = _reduce_start(b_kvb, where, dproj_b)
    (dh,) = _mm_nn("d_proj_mm", dproj_b, wpt, [F32], after=(tok,))
    gw_in = _mm_tn_rows("gw_in_mm", dproj_b, h_b, W_IN, _unpermute_pieces()).reshape(N_CHIPS, W_IN // N_CHIPS, D)
    b_in, tok = _reduce_begin("in", [gw_in])

    def pre_attn_bwd_fn(xt, dht, dx1t, g, sc, sh):
        n, r = _rn_parts(xt)
        s = _rows(dht * n)
        return (dx1t + _rn_bwd(n, r, dht * (g * (1.0 + sc))),), ((1.0 + sc) * s, g * s, _rows(dht))

    (grad_x,), (d_g_pre_attn, d_sc_a, d_sh_a) = _rowwise(
        "pre_attn_bwd", pre_attn_bwd_fn, [x2d, dh, dx1], [g_pre_attn, sc_a, sh_a], [(D, F32)], [D] * 3, TB,
        after=(tok,))

    small_names = ["b_ada", "g_pre_attn", "g_q_a", "g_k_a", "g_ckv", "g_out_a", "g_out_b", "g_post_attn",
                   "g_pre_mlp", "g_post_mlp"]
    small_local = jnp.concatenate([d_sh_a, d_sc_a, d_gt_a, d_sh_m, d_sc_m, d_gt_m, d_g_pre_attn, d_g_q_a, d_g_k_a,
                                   d_g_ckv, d_g_out_a, d_g_out_b, d_g_post_attn, d_g_pre_mlp, d_g_post_mlp,
                                   loss_v[:, :8]], axis=1)
    n_sent = small_local.shape[1]
    n_small = n_sent - 8
    small_all = _allgather8("gather_small", small_local.reshape(8, n_sent // 8)).reshape(N_DEV, n_sent)

    def sum8_fn(t):
        return (), (jnp.sum(t, axis=0, keepdims=True),)

    r_in, tok_in = _reduce_start(b_in, where, grad_x, (small_all,))
    _, (sent_sum,) = _rowwise("sum_small", sum8_fn, [small_all], [], [], [n_sent], N_DEV, after=(tok_in,))
    small_sum, loss = sent_sum[:, :n_small], sent_sum[0, n_small]
    dmod_cols = lax.dynamic_slice(small_all, (0, j_me * n_mod), (N_DEV, n_mod))
    dmod_pad = jnp.concatenate([dmod_cols, jnp.zeros_like(dmod_cols)], axis=0).astype(BF16)
    gw_ada = _mm_tn("gw_ada_mm", ca_pad, dmod_pad, 1)[0]

    weights = dict(w_ada=w_ada, b_ada=b_ada, g_pre_attn=g_pre_attn, w_in=w_in, g_q_a=g_q_a, g_k_a=g_k_a, g_ckv=g_ckv,
                   w_kv_b=w_kv_b, g_out_a=g_out_a, g_out_b=g_out_b, w_out=w_out, g_post_attn=g_post_attn,
                   g_pre_mlp=g_pre_mlp, w_mlp_in=w_mlp_in, w_mlp_out=w_mlp_out, g_post_mlp=g_post_mlp)
    ms = dict(w_ada=m_w_ada, b_ada=m_b_ada, g_pre_attn=m_g_pre_attn, w_in=m_w_in, g_q_a=m_g_q_a, g_k_a=m_g_k_a,
              g_ckv=m_g_ckv, w_kv_b=m_w_kv_b, g_out_a=m_g_out_a, g_out_b=m_g_out_b, w_out=m_w_out,
              g_post_attn=m_g_post_attn, g_pre_mlp=m_g_pre_mlp, w_mlp_in=m_w_mlp_in, w_mlp_out=m_w_mlp_out,
              g_post_mlp=m_g_post_mlp)
    vs = dict(w_ada=v_w_ada, b_ada=v_b_ada, g_pre_attn=v_g_pre_attn, w_in=v_w_in, g_q_a=v_g_q_a, g_k_a=v_g_k_a,
              g_ckv=v_g_ckv, w_kv_b=v_w_kv_b, g_out_a=v_g_out_a, g_out_b=v_g_out_b, w_out=v_w_out,
              g_post_attn=v_g_post_attn, g_pre_mlp=v_g_pre_mlp, w_mlp_in=v_w_mlp_in, w_mlp_out=v_w_mlp_out,
              g_post_mlp=v_g_post_mlp)
    order = list(weights)
    grads, deltas, new_m, new_v = {}, {}, {}, {}

    def update(name, g, after=()):
        view = (lambda t: jnp.swapaxes(t, 0, 1)) if name == "w_in" else (lambda t: t)
        d_, m_, v_, *g_ = _adamw("adamw_" + name, view(weights[name][0]), g, view(ms[name][0]), view(vs[name][0]),
                                 after, rewrite_g=name != "w_ada")
        grads[name], deltas[name], new_m[name], new_v[name] = (view(t)[None] for t in ((g_ or [g])[0], d_, m_, v_))
        return d_

    chain = ((("w_mlp_out",), r_down), (("w_mlp_in",), r_up), (("w_out", "w_kv_b"), r_kvb), (("w_in",), r_in))
    swap, tok = _reduce_finish(chain[0][1], where, small_sum)
    last = update("w_ada", gw_ada, (tok,))
    for k, (names, _) in enumerate(chain):
        gs = _reduce_end(swap, last)
        if k + 1 < len(chain):
            swap, tok = _reduce_finish(chain[k + 1][1], where, gs[0])
        for name, g in zip(names, gs):
            last = update(name, g, (tok,))

    cat = lambda d: jnp.concatenate([d[n] for n in small_names], axis=1)
    d_, m_, v_ = _adamw("adamw_small", cat(weights), small_sum, cat(ms), cat(vs))
    off = 0
    for n in small_names:
        w_ = weights[n].shape[1]
        grads[n], deltas[n], new_m[n], new_v[n] = (a[:, off:off + w_] for a in (small_sum, d_, m_, v_))
        off += w_

    return (loss, grad_x[None], *[grads[n] for n in order], *[deltas[n] for n in order],
            *[new_m[n] for n in order], *[new_v[n] for n in order])
```

```python
import functools
import math

import numpy as np
import jax
import jax.numpy as jnp
from jax import lax
from jax.experimental import pallas as pl
from jax.experimental.pallas import tpu as pltpu

F32 = jnp.float32
BF16 = jnp.bfloat16
MESH = pl.DeviceIdType.MESH

EPS = 1e-6
GRID_W = 64
ROPE_THETA = 10000.0
HEAD_DIM = 128
HA = 8
HKV = 2
HB = 8
QK_NOPE = 128
QK_ROPE = 64
V_DIM = 128
KV_RANK = 512
QK_B = QK_NOPE + QK_ROPE
W_QA = HA * HEAD_DIM
W_KA = HKV * HEAD_DIM
W_QB = HB * QK_B
W_IN = W_QA + 2 * W_KA + W_QB + KV_RANK + QK_ROPE
N_CHIPS = 4
N_DEV = 8

LANE = 128
O_QA = 0
O_KA = O_QA + W_QA
O_VA = O_KA + W_KA
O_QN = O_VA + W_KA
O_QP = O_QN + HB * QK_NOPE
O_CKV = O_QP + HB * LANE
O_KPE = O_CKV + KV_RANK
W_INP = O_KPE + LANE

ADAM_LR = 0.001
ADAM_B1 = 0.9
ADAM_B2 = 0.999
ADAM_EPS = 1e-08
ADAM_WD = 0.01
ADAM_STEP = 10

VMEM_LIMIT = 56 * 1024 * 1024
HEADS_PER_STEP = 2


def _params(sem):
    return pltpu.CompilerParams(dimension_semantics=sem, vmem_limit_bytes=VMEM_LIMIT)


def _pick(dim, target, unit):
    best = None
    t = unit
    while t <= min(dim, target):
        if dim % t == 0:
            best = t
        t += unit
    return dim if best is None else best


def _rn(x):
    return x * lax.rsqrt(jnp.mean(x * x, axis=-1, keepdims=True) + EPS)


def _rn_parts(x):
    r = lax.rsqrt(jnp.mean(x * x, axis=-1, keepdims=True) + EPS)
    return x * r, r


def _rn_bwd(n, r, dn):
    return r * (dn - n * jnp.mean(dn * n, axis=-1, keepdims=True))


def _rows(t):
    return jnp.sum(t, axis=0, keepdims=True)


def _rowwise(name, fn, rows, vecs, row_outs, vec_outs, tile, after=()):
    S = rows[0].shape[0]
    assert S % tile == 0
    n_r, n_v, n_ro, n_vo, n_a = len(rows), len(vecs), len(row_outs), len(vec_outs), len(after)

    def body(*refs):
        r_in = refs[:n_r]
        v_in = refs[n_r:n_r + n_v]
        r_out = refs[n_r + n_v + n_a:n_r + n_v + n_a + n_ro]
        v_out = refs[n_r + n_v + n_a + n_ro:]
        ro, vo = fn(*[r[...] for r in r_in], *[v[...] for v in v_in])
        for ref, val in zip(r_out, ro):
            ref[...] = val.astype(ref.dtype)
        if n_vo:
            @pl.when(pl.program_id(0) == 0)
            def _():
                for ref in v_out:
                    ref[...] = jnp.zeros(ref.shape, ref.dtype)
            for ref, val in zip(v_out, vo):
                ref[...] += val

    in_specs = [pl.BlockSpec((tile, a.shape[1]), lambda i: (i, 0)) for a in rows]
    in_specs += [pl.BlockSpec(a.shape, lambda i: (0, 0)) for a in vecs]
    in_specs += [pl.BlockSpec(memory_space=pl.ANY) for _ in after]
    out_specs = [pl.BlockSpec((tile, w), lambda i: (i, 0)) for w, _ in row_outs]
    out_specs += [pl.BlockSpec((1, w), lambda i: (0, 0)) for w in vec_outs]
    out_shape = [jax.ShapeDtypeStruct((S, w), dt) for w, dt in row_outs]
    out_shape += [jax.ShapeDtypeStruct((1, w), F32) for w in vec_outs]
    res = pl.pallas_call(
        body, name=name, grid=(S // tile,), in_specs=in_specs, out_specs=out_specs, out_shape=out_shape,
        compiler_params=_params(("arbitrary",)),
    )(*rows, *vecs, *after)
    return res[:n_ro], res[n_ro:]


CONTRACT_TILE = 2048
MM_TILE_BYTES = 40 * 1024 * 1024


def _rows_that_fit(M, tile_bytes, most=1024):
    t = _pick(M, most, 16)
    while t % 32 == 0 and tile_bytes(t) > MM_TILE_BYTES:
        t //= 2
    return t


def _accumulate(acc, part, step, n_steps, finish):
    if n_steps == 1:
        finish(part)
        return

    @pl.when(step == 0)
    def _():
        acc[...] = part

    @pl.when((step > 0) & (step < n_steps - 1))
    def _():
        acc[...] += part

    @pl.when(step == n_steps - 1)
    def _():
        finish(acc[...] + part)


def _mm_nn(name, a, b3, out_dtypes, epilogue=None, extras=(), after=()):
    M, K = a.shape
    nB, K2, Nsh = b3.shape
    assert K == K2
    N = nB * Nsh
    tn, tk = _pick(Nsh, 512, LANE), _pick(K, CONTRACT_TILE, LANE)
    out_bytes = sum(jnp.dtype(dt).itemsize for dt in out_dtypes) + sum(e.dtype.itemsize for e in extras)
    tm = _rows_that_fit(M, lambda t: 2 * (t * tk * a.dtype.itemsize + tk * tn * b3.dtype.itemsize)
                        + t * tn * (4 * (K > tk) + 2 * out_bytes), most=2048)
    nps, nk, ne, no, na = Nsh // tn, K // tk, len(extras), len(out_dtypes), len(after)

    def body(a_ref, b_ref, *rest):
        e_refs, o_refs = rest[:ne], rest[ne + na:ne + na + no]

        def finish(total):
            vals = (total,) if epilogue is None else epilogue(total, *[e[...] for e in e_refs])
            for ref, val in zip(o_refs, vals):
                ref[...] = val.astype(ref.dtype)

        part = jnp.dot(a_ref[...].astype(BF16), b_ref[...].astype(BF16), preferred_element_type=F32)
        _accumulate(rest[-1], part, pl.program_id(2), nk, finish)

    in_specs = [pl.BlockSpec((tm, tk), lambda i, j, k: (i, k)),
                pl.BlockSpec((None, tk, tn), lambda i, j, k: (j // nps, k, j % nps))]
    in_specs += [pl.BlockSpec((tm, tn), lambda i, j, k: (i, j)) for _ in extras]
    in_specs += [pl.BlockSpec(memory_space=pl.ANY) for _ in after]
    return pl.pallas_call(
        body, name=name, grid=(M // tm, N // tn, nk), in_specs=in_specs,
        out_specs=[pl.BlockSpec((tm, tn), lambda i, j, k: (i, j)) for _ in out_dtypes],
        out_shape=[jax.ShapeDtypeStruct((M, N), dt) for dt in out_dtypes],
        scratch_shapes=[pltpu.VMEM((tm, tn), F32)] if nk > 1 else [],
        compiler_params=_params(("parallel", "parallel", "arbitrary")),
    )(a, b3, *extras, *after)


def _mm_nt(name, a, b3, out_dtypes, epilogue=None, extras=(), after=()):
    M, N = a.shape
    nB, K, Nsh = b3.shape
    assert N == nB * Nsh
    to, tn = _pick(K, 1536, LANE), _pick(Nsh, CONTRACT_TILE, LANE)
    out_bytes = sum(jnp.dtype(dt).itemsize for dt in out_dtypes) + sum(e.dtype.itemsize for e in extras)
    tm = _rows_that_fit(M, lambda t: 2 * (t * tn * a.dtype.itemsize + to * tn * b3.dtype.itemsize)
                        + t * to * (4 * (N > tn) + 2 * out_bytes))
    nps, nn, ne, no, na = Nsh // tn, N // tn, len(extras), len(out_dtypes), len(after)

    def body(a_ref, b_ref, *rest):
        e_refs, o_refs = rest[:ne], rest[ne + na:ne + na + no]

        def finish(total):
            vals = (total,) if epilogue is None else epilogue(total, *[e[...] for e in e_refs])
            for ref, val in zip(o_refs, vals):
                ref[...] = val.astype(ref.dtype)

        part = lax.dot_general(a_ref[...].astype(BF16), b_ref[...].astype(BF16),
                               (((1,), (1,)), ((), ())), preferred_element_type=F32)
        _accumulate(rest[-1], part, pl.program_id(2), nn, finish)

    in_specs = [pl.BlockSpec((tm, tn), lambda i, o, n: (i, n)),
                pl.BlockSpec((None, to, tn), lambda i, o, n: (n // nps, o, n % nps))]
    in_specs += [pl.BlockSpec((tm, to), lambda i, o, n: (i, o)) for _ in extras]
    in_specs += [pl.BlockSpec(memory_space=pl.ANY) for _ in after]
    return pl.pallas_call(
        body, name=name, grid=(M // tm, K // to, nn), in_specs=in_specs,
        out_specs=[pl.BlockSpec((tm, to), lambda i, o, n: (i, o)) for _ in out_dtypes],
        out_shape=[jax.ShapeDtypeStruct((M, K), dt) for dt in out_dtypes],
        scratch_shapes=[pltpu.VMEM((tm, to), F32)] if nn > 1 else [],
        compiler_params=_params(("parallel", "parallel", "arbitrary")),
    )(a, b3, *extras, *after)


def _mm_tn(name, a, g, nB, after=()):
    M, K = a.shape
    M2, N = g.shape
    assert M == M2 and N % nB == 0
    Nsh = N // nB
    tk, tn, tm = _pick(K, 1536, LANE), _pick(Nsh, 1024, LANE), _pick(M, CONTRACT_TILE, 16)
    nps, nm = Nsh // tn, M // tm

    def body(a_ref, g_ref, *rest):
        o_ref = rest[len(after)]

        def finish(total):
            o_ref[...] = total

        part = lax.dot_general(a_ref[...].astype(BF16), g_ref[...].astype(BF16),
                               (((0,), (0,)), ((), ())), preferred_element_type=F32)
        _accumulate(rest[-1], part, pl.program_id(2), nm, finish)

    return pl.pallas_call(
        body, name=name, grid=(K // tk, N // tn, nm),
        in_specs=[pl.BlockSpec((tm, tk), lambda k, n, m: (m, k)),
                  pl.BlockSpec((tm, tn), lambda k, n, m: (m, n))] + [pl.BlockSpec(memory_space=pl.ANY) for _ in after],
        out_specs=pl.BlockSpec((None, tk, tn), lambda k, n, m: (n // nps, k, n % nps)),
        out_shape=jax.ShapeDtypeStruct((nB, K, Nsh), F32),
        scratch_shapes=[pltpu.VMEM((tk, tn), F32)] if nm > 1 else [],
        compiler_params=_params(("parallel", "parallel", "arbitrary")),
    )(a, g, *after)


def _rows_mm(name, fn, rows, vecs, b3, after=()):
    S = rows[0].shape[0]
    _, K, N = b3.shape
    tm, tn = _pick(S, 1024, 16), _pick(N, 512, LANE)
    nr, nv, na = len(rows), len(vecs), len(after)

    def body(*refs):
        b_ref = refs[nr + nv]
        a_out, o_ref, a_scr = refs[nr + nv + 1 + na:]

        @pl.when(pl.program_id(1) == 0)
        def _():
            a = fn(*[r[...] for r in refs[:nr]], *[v[...] for v in refs[nr:nr + nv]]).astype(BF16)
            a_scr[...] = a
            a_out[...] = a

        o_ref[...] = jnp.dot(a_scr[...], b_ref[...], preferred_element_type=F32)

    in_specs = [pl.BlockSpec((tm, r.shape[1]), lambda i, j: (i, 0)) for r in rows]
    in_specs += [pl.BlockSpec(v.shape, lambda i, j: (0, 0)) for v in vecs]
    in_specs += [pl.BlockSpec((None, K, tn), lambda i, j: (0, 0, j))]
    in_specs += [pl.BlockSpec(memory_space=pl.ANY) for _ in after]
    return pl.pallas_call(
        body, name=name, grid=(S // tm, N // tn), in_specs=in_specs,
        out_specs=[pl.BlockSpec((tm, K), lambda i, j: (i, 0)), pl.BlockSpec((tm, tn), lambda i, j: (i, j))],
        out_shape=[jax.ShapeDtypeStruct((S, K), BF16), jax.ShapeDtypeStruct((S, N), F32)],
        scratch_shapes=[pltpu.VMEM((tm, K), BF16)],
        compiler_params=_params(("parallel", "arbitrary")),
    )(*rows, *vecs, b3, *after)


def _kv_mm(name, ckvn, w3, kpe):
    S, K = ckvn.shape
    nB, _, Nsh = w3.shape
    heads = Nsh // (2 * LANE)
    tm = _pick(S, 1024, 16)

    def body(a_ref, b_ref, kpe_ref, kc_ref, v_ref):
        acc = jnp.dot(a_ref[...], b_ref[...], preferred_element_type=F32)
        for h in range(heads):
            kc_ref[:, 2 * h * LANE:(2 * h + 1) * LANE] = acc[:, 2 * h * LANE:(2 * h + 1) * LANE].astype(BF16)
            kc_ref[:, (2 * h + 1) * LANE:(2 * h + 2) * LANE] = kpe_ref[...]
            v_ref[:, h * LANE:(h + 1) * LANE] = acc[:, (2 * h + 1) * LANE:(2 * h + 2) * LANE].astype(BF16)

    return pl.pallas_call(
        body, name=name, grid=(S // tm, nB),
        in_specs=[pl.BlockSpec((tm, K), lambda i, j: (i, 0)), pl.BlockSpec((None, K, Nsh), lambda i, j: (j, 0, 0)),
                  pl.BlockSpec((tm, LANE), lambda i, j: (i, 0))],
        out_specs=[pl.BlockSpec((tm, Nsh), lambda i, j: (i, j)), pl.BlockSpec((tm, Nsh // 2), lambda i, j: (i, j))],
        out_shape=[jax.ShapeDtypeStruct((S, nB * Nsh), BF16), jax.ShapeDtypeStruct((S, nB * Nsh // 2), BF16)],
        compiler_params=_params(("parallel", "arbitrary")),
    )(ckvn, w3, kpe)


def _kv_mm_bwd(name, dkc, dv, w3):
    S = dkc.shape[0]
    nB, K, Nsh = w3.shape
    heads = Nsh // (2 * LANE)
    tm = _pick(S, 1024, 16)

    def body(dkc_ref, dv_ref, b_ref, dck_ref, dkv_ref, dkpe_ref):
        j = pl.program_id(1)
        pieces, pe = [], None
        for h in range(heads):
            pieces += [dkc_ref[:, 2 * h * LANE:(2 * h + 1) * LANE], dv_ref[:, h * LANE:(h + 1) * LANE]]
            piece = dkc_ref[:, (2 * h + 1) * LANE:(2 * h + 2) * LANE]
            pe = piece if pe is None else pe + piece
        dkv = jnp.concatenate(pieces, axis=1).astype(BF16)
        dkv_ref[...] = dkv
        part = lax.dot_general(dkv, b_ref[...], (((1,), (1,)), ((), ())), preferred_element_type=F32)

        @pl.when(j == 0)
        def _():
            dck_ref[...] = part
            dkpe_ref[...] = pe

        @pl.when(j > 0)
        def _():
            dck_ref[...] += part
            dkpe_ref[...] += pe

    return pl.pallas_call(
        body, name=name, grid=(S // tm, nB),
        in_specs=[pl.BlockSpec((tm, Nsh), lambda i, j: (i, j)), pl.BlockSpec((tm, Nsh // 2), lambda i, j: (i, j)),
                  pl.BlockSpec((None, K, Nsh), lambda i, j: (j, 0, 0))],
        out_specs=[pl.BlockSpec((tm, K), lambda i, j: (i, 0)), pl.BlockSpec((tm, Nsh), lambda i, j: (i, j)),
                   pl.BlockSpec((tm, LANE), lambda i, j: (i, 0))],
        out_shape=[jax.ShapeDtypeStruct((S, K), F32), jax.ShapeDtypeStruct((S, nB * Nsh), BF16),
                   jax.ShapeDtypeStruct((S, LANE), F32)],
        compiler_params=_params(("parallel", "arbitrary")),
    )(dkc, dv, w3)


def _mm_tn_rows(name, a, g, n_rows, pieces):
    M, K = a.shape
    M2, N = g.shape
    assert M == M2
    tk, tn, tm = _pick(K, 1536, LANE), _pick(N, 512, LANE), _pick(M, CONTRACT_TILE, 16)
    nk, nm = K // tk, M // tm

    def body(a_ref, g_ref, o_ref, *acc):
        kk = pl.program_id(1)

        def finish(total):
            for t in range(nk):
                @pl.when(kk == t)
                def _(t=t):
                    for src, dst, n in pieces:
                        lo, hi = max(src, t * tk), min(src + n, (t + 1) * tk)
                        if lo < hi:
                            o_ref[dst + lo - src:dst + hi - src, :] = total[lo - t * tk:hi - t * tk, :]

        part = lax.dot_general(a_ref[...].astype(BF16), g_ref[...].astype(BF16),
                               (((0,), (0,)), ((), ())), preferred_element_type=F32)
        _accumulate(acc[0] if acc else None, part, pl.program_id(2), nm, finish)

    return pl.pallas_call(
        body, name=name, grid=(N // tn, nk, nm),
        in_specs=[pl.BlockSpec((tm, tk), lambda n, k, m: (m, k)),
                  pl.BlockSpec((tm, tn), lambda n, k, m: (m, n))],
        out_specs=pl.BlockSpec((n_rows, tn), lambda n, k, m: (0, n)),
        out_shape=jax.ShapeDtypeStruct((n_rows, N), F32),
        scratch_shapes=[pltpu.VMEM((tk, tn), F32)] if nm > 1 else [],
        compiler_params=_params(("arbitrary", "arbitrary", "arbitrary")),
    )(a, g)


def _attn_fwd(name, q, k, v, n_heads, group, dqk, dv, scale, tq, after=()):
    S = q.shape[0]
    hp = 2 * HEADS_PER_STEP
    kvp = max(1, hp // group)

    def body(q_ref, k_ref, v_ref, *rest):
        o_ref, lse_ref = rest[-2:]
        for j in range(hp):
            jk = j * kvp // hp
            s = lax.dot_general(q_ref[:, j * dqk:(j + 1) * dqk], k_ref[:, jk * dqk:(jk + 1) * dqk],
                                (((1,), (1,)), ((), ())), preferred_element_type=F32)
            m = jnp.max(s, axis=-1, keepdims=True)
            e = jnp.exp((s - m) * scale)
            l = jnp.sum(e, axis=-1, keepdims=True)
            o_ref[:, j * dv:(j + 1) * dv] = jnp.dot(e.astype(BF16), v_ref[:, jk * dv:(jk + 1) * dv],
                                                    preferred_element_type=F32) * (1.0 / l)
            lse_ref[:, j * LANE:(j + 1) * LANE] = jnp.broadcast_to(m * scale + jnp.log(l), (tq, LANE))

    kv_block = lambda h, i: (0, (h * hp // group) // kvp)
    return pl.pallas_call(
        body, name=name, grid=(n_heads // hp, S // tq),
        in_specs=[pl.BlockSpec((tq, hp * dqk), lambda h, i: (i, h)),
                  pl.BlockSpec((S, kvp * dqk), kv_block),
                  pl.BlockSpec((S, kvp * dv), kv_block)] + [pl.BlockSpec(memory_space=pl.ANY) for _ in after],
        out_specs=[pl.BlockSpec((tq, hp * dv), lambda h, i: (i, h)),
                   pl.BlockSpec((tq, hp * LANE), lambda h, i: (i, h))],
        out_shape=[jax.ShapeDtypeStruct((S, n_heads * dv), F32), jax.ShapeDtypeStruct((S, n_heads * LANE), F32)],
        compiler_params=_params(("arbitrary", "arbitrary")),
    )(q, k, v, *after)


def _attn_bwd(name, q, k, v, do, o, lse, n_heads, group, dqk, dv, scale, tq, after=()):
    S = q.shape[0]
    n_kv = n_heads // group
    hp, tq = (group, tq // 2) if group > HEADS_PER_STEP else (HEADS_PER_STEP, tq)
    kvp = max(1, hp // group)

    def body(q_ref, k_ref, v_ref, do_ref, o_ref, lse_ref, *rest):
        dq_ref, dk_ref, dv_ref = rest[len(after):]
        h, i = pl.program_id(0), pl.program_id(1)

        @pl.when(((h * hp) % group == 0) & (i == 0))
        def _():
            dk_ref[...] = jnp.zeros(dk_ref.shape, F32)
            dv_ref[...] = jnp.zeros(dv_ref.shape, F32)

        for j in range(hp):
            jk = j * kvp // hp
            qb, dob = q_ref[:, j * dqk:(j + 1) * dqk], do_ref[:, j * dv:(j + 1) * dv]
            kb, vb = k_ref[:, jk * dqk:(jk + 1) * dqk], v_ref[:, jk * dv:(jk + 1) * dv]
            s = lax.dot_general(qb, kb, (((1,), (1,)), ((), ())), preferred_element_type=F32)
            p = jnp.exp(s * scale - jnp.tile(lse_ref[:, j * LANE:(j + 1) * LANE], (1, S // LANE)))
            dp = lax.dot_general(dob, vb, (((1,), (1,)), ((), ())), preferred_element_type=F32)
            delta = jnp.sum(dob.astype(F32) * o_ref[:, j * dv:(j + 1) * dv], axis=-1, keepdims=True)
            ds = (p * ((dp - delta) * scale)).astype(BF16)
            dv_ref[:, jk * dv:(jk + 1) * dv] += lax.dot_general(p.astype(BF16), dob, (((0,), (0,)), ((), ())),
                                                                preferred_element_type=F32)
            dq_ref[:, j * dqk:(j + 1) * dqk] = jnp.dot(ds, kb, preferred_element_type=F32)
            dk_ref[:, jk * dqk:(jk + 1) * dqk] += lax.dot_general(ds, qb, (((0,), (0,)), ((), ())),
                                                                  preferred_element_type=F32)

    kv_block = lambda h, i: (0, (h * hp // group) // kvp)
    return pl.pallas_call(
        body, name=name, grid=(n_heads // hp, S // tq),
        in_specs=[pl.BlockSpec((tq, hp * dqk), lambda h, i: (i, h)),
                  pl.BlockSpec((S, kvp * dqk), kv_block),
                  pl.BlockSpec((S, kvp * dv), kv_block),
                  pl.BlockSpec((tq, hp * dv), lambda h, i: (i, h)),
                  pl.BlockSpec((tq, hp * dv), lambda h, i: (i, h)),
                  pl.BlockSpec((tq, hp * LANE), lambda h, i: (i, h))] + [pl.BlockSpec(memory_space=pl.ANY) for _ in after],
        out_specs=[pl.BlockSpec((tq, hp * dqk), lambda h, i: (i, h)),
                   pl.BlockSpec((S, kvp * dqk), kv_block),
                   pl.BlockSpec((S, kvp * dv), kv_block)],
        out_shape=[jax.ShapeDtypeStruct((S, n_heads * dqk), F32),
                   jax.ShapeDtypeStruct((S, n_kv * dqk), F32),
                   jax.ShapeDtypeStruct((S, n_kv * dv), F32)],
        compiler_params=_params(("arbitrary", "arbitrary")),
    )(q, k, v, do, o, lse, *after)


def _adamw(name, w, g, m, v, after=(), rewrite_g=False):
    R, C = w.shape
    tile = _pick(R, max(8, (1 << 19) // C // 8 * 8), 8)
    c1 = 1.0 - ADAM_B1 ** ADAM_STEP
    c2 = 1.0 - ADAM_B2 ** ADAM_STEP
    n_out = 4 if rewrite_g else 3

    def body(w_ref, g_ref, m_ref, v_ref, *rest):
        d_ref, nm_ref, nv_ref = rest[len(after):len(after) + 3]
        gg = g_ref[...]
        nm = ADAM_B1 * m_ref[...] + (1.0 - ADAM_B1) * gg
        nv = ADAM_B2 * v_ref[...] + (1.0 - ADAM_B2) * (gg * gg)
        d_ref[...] = -ADAM_LR * ((nm / c1) / (jnp.sqrt(nv / c2) + ADAM_EPS) + ADAM_WD * w_ref[...])
        nm_ref[...] = nm
        nv_ref[...] = nv
        if rewrite_g:
            rest[-1][...] = gg

    spec = pl.BlockSpec((tile, C), lambda i: (i, 0))
    return pl.pallas_call(
        body, name=name, grid=(R // tile,), in_specs=[spec] * 4 + [pl.BlockSpec(memory_space=pl.ANY) for _ in after],
        out_specs=[spec] * n_out, out_shape=[jax.ShapeDtypeStruct((R, C), F32)] * n_out,
        compiler_params=_params(("parallel",)),
    )(w, g, m, v, *after)


def _flip(me, rel):
    return tuple(1 - p if r else p for p, r in zip(me, rel))


def _push(name, ins, out_shapes, transfers, aliases=None):
    ni, no, nt = len(ins), len(out_shapes), len(transfers)

    def body(*refs):
        in_refs, out_refs = refs[:ni], refs[ni:ni + no]
        send_sems, recv_sems = refs[ni + no], refs[ni + no + 1]
        me = (lax.axis_index("x"), lax.axis_index("y"), lax.axis_index("c"))
        copies = []
        for t, (si, sfn, oi, dfn, rel) in enumerate(transfers):
            src_ref = out_refs[si[1]] if isinstance(si, tuple) else in_refs[si]
            src = src_ref.at[sfn(*me)]
            dst = out_refs[oi].at[dfn(*me)]
            if rel is None:
                cp = pltpu.make_async_copy(src, dst, send_sems.at[t])
            else:
                cp = pltpu.make_async_remote_copy(src_ref=src, dst_ref=dst, send_sem=send_sems.at[t],
                                                  recv_sem=recv_sems.at[t], device_id=_flip(me, rel),
                                                  device_id_type=MESH)
            cp.start()
            copies.append(cp)
        for cp in copies:
            cp.wait()

    any_spec = pl.BlockSpec(memory_space=pl.ANY)
    return pl.pallas_call(
        body, name=name, in_specs=[any_spec] * ni, out_specs=[any_spec] * no, out_shape=out_shapes,
        scratch_shapes=[pltpu.SemaphoreType.DMA((nt,)), pltpu.SemaphoreType.DMA((nt,))],
        input_output_aliases=aliases or {},
        compiler_params=pltpu.CompilerParams(has_side_effects=True),
    )(*ins)


_HBM = pl.BlockSpec(memory_space=pltpu.HBM)
_SEM = pl.BlockSpec(memory_space=pltpu.SEMAPHORE)
_DATAFLOW = pltpu.SideEffectType.DATAFLOW_SIDE_EFFECTING


def _split_copies(refs, transfers, send_sems, recv_sems):
    me = (lax.axis_index("x"), lax.axis_index("y"), lax.axis_index("c"))
    return [pltpu.make_async_remote_copy(src_ref=refs[sb].at[sfn(*me)], dst_ref=refs[db].at[dfn(*me)],
                                         send_sem=send_sems.at[t], recv_sem=recv_sems.at[t],
                                         device_id=_flip(me, rel), device_id_type=MESH)
            for t, (sb, sfn, db, dfn, rel) in enumerate(transfers)]


def _push_start(name, bufs, transfers, after=()):
    nb, na, nt = len(bufs), len(after), len(transfers)

    def body(*refs):
        send_sems, recv_sems, token = refs[nb + na], refs[nb + na + 1], refs[-1]
        for cp in _split_copies(refs[nb + na + 2:2 * nb + na + 2], transfers, send_sems, recv_sems):
            cp.start()
        token[...] = jnp.zeros(token.shape, token.dtype)

    res = pl.pallas_call(
        body, name=name,
        out_shape=(pltpu.SemaphoreType.DMA((nt,)), pltpu.SemaphoreType.DMA((nt,)),
                   *[pltpu.HBM(b.shape, b.dtype) for b in bufs], jax.ShapeDtypeStruct((8, LANE), F32)),
        in_specs=[_HBM] * nb + [pl.BlockSpec(memory_space=pl.ANY)] * na,
        out_specs=(_SEM, _SEM, *[_HBM] * nb, pl.BlockSpec(memory_space=pltpu.VMEM)),
        input_output_aliases={i: i + 2 for i in range(nb)},
        compiler_params=pltpu.CompilerParams(has_side_effects=_DATAFLOW),
    )(*[pltpu.with_memory_space_constraint(b, pltpu.HBM) for b in bufs], *after)
    return res[0], res[1], list(res[2:2 + nb]), res[-1]


def _push_wait(name, send_sems, recv_sems, bufs, transfers, after):
    nb = len(bufs)

    def body(*refs):
        for cp in _split_copies(refs[:nb], transfers, refs[nb], refs[nb + 1]):
            cp.wait_send()
            cp.wait_recv()

    res = pl.pallas_call(
        body, name=name, out_shape=[pltpu.HBM(b.shape, b.dtype) for b in bufs],
        in_specs=[_HBM] * nb + [_SEM, _SEM, pl.BlockSpec(memory_space=pl.ANY)], out_specs=[_HBM] * nb,
        input_output_aliases={i: i for i in range(nb)},
        compiler_params=pltpu.CompilerParams(has_side_effects=_DATAFLOW),
    )(*bufs, send_sems, recv_sems, after)
    return list(res)


ICI_RELS = ((1, 0, 0), (0, 1, 0), (1, 1, 0))
SIBLING = (0, 0, 1)


def _chip(x, y):
    return 2 * x + y


def _cast_into_slot(name, where, w, after=()):
    R, C = w.shape
    tile = _pick(R, 512, BF16_ROWS)

    def body(where_ref, w_ref, *rest):
        rest[-1][...] = w_ref[...].astype(BF16)

    return pl.pallas_call(
        body, name=name,
        grid_spec=pltpu.PrefetchScalarGridSpec(
            num_scalar_prefetch=1, grid=(R // tile,),
            in_specs=[pl.BlockSpec((tile, C), lambda i, wh: (i, 0))] + [pl.BlockSpec(memory_space=pl.ANY) for _ in after],
            out_specs=pl.BlockSpec((None, tile, C), lambda i, wh: (wh[1], i, 0))),
        out_shape=jax.ShapeDtypeStruct((N_CHIPS, R, C), BF16),
        compiler_params=_params(("parallel",)),
    )(where, w, *after)


BF16_ROWS = 16


def _rows_split(shape2):
    return (shape2[0] // 2) % BF16_ROWS == 0


def _half_shape(shape2):
    R, C = shape2
    return (R // 2, C) if _rows_split(shape2) else (R, C // 2)


def _half(shape2, c):
    R, C = shape2
    if _rows_split(shape2):
        return (pl.ds(c * (R // 2), R // 2), slice(None))
    return (slice(None), pl.ds(pl.multiple_of(c * (C // 2), LANE), C // 2))


def _gather_casts(tag, where, shards, after=()):
    return [_cast_into_slot(f"cast_w_{tag}{w}", where, s, after) for w, s in enumerate(shards)]


def _gather_start(tag, own, after):
    transfers = []
    for w in range(len(own)):
        def idx(x, y, c, shape2=own[w].shape[1:]):
            return (_chip(x, y), *_half(shape2, c))
        transfers += [(w, idx, w, idx, rel) for rel in ICI_RELS]
    send_sems, recv_sems, bufs, token = _push_start(f"gather_{tag}_start", own, transfers, after)
    return (tag, send_sems, recv_sems, bufs, transfers), token


def _forward_transfers(part):
    second = []
    for w, p in enumerate(part):
        for rel in ICI_RELS:
            def idx(x, y, c, shape2=p.shape[1:], rel=rel):
                return (_chip(x ^ rel[0], y ^ rel[1]), *_half(shape2, c))
            second.append((w, idx, w, idx, SIBLING))
    return second


def _gather_finish(handle, after):
    tag, send_sems, recv_sems, bufs, transfers = handle
    part = _push_wait(f"gather_{tag}_wait", send_sems, recv_sems, bufs, transfers, after)
    outs = [jax.ShapeDtypeStruct(p.shape, p.dtype) for p in part]
    second = [(("out", sb), sfn, db, dfn, rel) for sb, sfn, db, dfn, rel in _forward_transfers(part)]
    return _push(f"gather_{tag}_d2d", list(part), outs, second, aliases={w: w for w in range(len(part))})


def _gather_mid(handle, after):
    tag, send_sems, recv_sems, bufs, transfers = handle
    part = _push_wait(f"gather_{tag}_wait", send_sems, recv_sems, bufs, transfers, after)
    second = _forward_transfers(part)
    s2, r2, bufs2, token = _push_start(f"gather_{tag}_d2d_start", part, second)
    return (tag, s2, r2, bufs2, second), token


def _gather_end(handle, after):
    tag, send_sems, recv_sems, bufs, transfers = handle
    return _push_wait(f"gather_{tag}_d2d_wait", send_sems, recv_sems, bufs, transfers, after)


def _reduce_begin(tag, grads, after=()):
    n = len(grads)
    land = [lax.empty((N_CHIPS, *_half_shape(g.shape[1:])), F32) for g in grads]
    t1 = []
    for w in range(n):
        t1.append((w, lambda x, y, c, shape2=grads[w].shape[1:]: (slice(None), *_half(shape2, 1 - c)), n + w,
                   lambda x, y, c: (slice(None), slice(None), slice(None)), SIBLING))
    send_sems, recv_sems, bufs, token = _push_start(f"reduce_{tag}_d2d_start", list(grads) + land, t1, after)
    return (tag, send_sems, recv_sems, bufs, t1), token


def _reduce_start(handle, where, after, after_start=()):
    tag, send_sems, recv_sems, bufs, t1 = handle
    n = len(bufs) // 2
    res = _push_wait(f"reduce_{tag}_d2d_wait", send_sems, recv_sems, bufs, t1, after)
    grads, from_sib = res[:n], res[n:]
    pair = [_pair_sum(f"pair_sum_{tag}{w}", where, grads[w], from_sib[w]) for w in range(n)]
    land = [lax.empty((3, *_half_shape(g.shape[1:])), BF16) for g in grads]
    t3 = []
    for w in range(n):
        for r, rel in enumerate(ICI_RELS):
            t3.append((w, lambda x, y, c, rel=rel: (_chip(x ^ rel[0], y ^ rel[1]), slice(None), slice(None)), n + w,
                       lambda x, y, c, r=r: (r, slice(None), slice(None)), rel))
    send_sems, recv_sems, bufs, token = _push_start(f"reduce_{tag}_start", pair + land, t3, after_start)
    return (tag, send_sems, recv_sems, bufs, t3, list(grads), list(from_sib)), token


def _reduce_finish(handle, where, after):
    tag, send_sems, recv_sems, bufs, t3, grads, from_sib = handle
    n = len(grads)
    from_chips = _push_wait(f"reduce_{tag}_wait", send_sems, recv_sems, bufs, t3, after)[n:]
    mine = [_chip_sum(f"chip_sum_{tag}{w}", where, grads[w], from_sib[w], from_chips[w]) for w in range(n)]
    t5 = []
    for w in range(n):
        def idx(x, y, c, shape2=grads[w].shape[1:]):
            return _half(shape2, c)
        t5.append((w, idx, w, idx, SIBLING))
    send_sems, recv_sems, bufs, token = _push_start(f"reduce_{tag}_swap_start", mine, t5)
    return (tag, send_sems, recv_sems, bufs, t5), token


def _reduce_end(handle, after):
    tag, send_sems, recv_sems, bufs, t5 = handle
    return _push_wait(f"reduce_{tag}_swap_wait", send_sems, recv_sems, bufs, t5, after)


def _pair_sum(name, where, g, from_sib):
    _, hr, hc = from_sib.shape
    tile = _pick(hr, max(16, (1 << 19) // hc // 16 * 16), 16)
    nb = hr // tile
    slot = lambda k, wh: (wh[1] + 1 + k) % N_CHIPS
    if _rows_split(g.shape[1:]):
        mine = lambda k, i, wh: (slot(k, wh), wh[0] * nb + i, 0)
    else:
        mine = lambda k, i, wh: (slot(k, wh), i, wh[0])

    def body(where_ref, g_ref, s_ref, o_ref):
        o_ref[...] = (g_ref[...] + s_ref[...]).astype(BF16)

    return pl.pallas_call(
        body, name=name,
        grid_spec=pltpu.PrefetchScalarGridSpec(
            num_scalar_prefetch=1, grid=(N_CHIPS - 1, nb),
            in_specs=[pl.BlockSpec((None, tile, hc), mine),
                      pl.BlockSpec((None, tile, hc), lambda k, i, wh: (slot(k, wh), i, 0))],
            out_specs=pl.BlockSpec((None, tile, hc), lambda k, i, wh: (slot(k, wh), i, 0))),
        out_shape=jax.ShapeDtypeStruct((N_CHIPS, hr, hc), BF16),
        compiler_params=_params(("parallel", "parallel")),
    )(where, g, from_sib)


def _chip_sum(name, where, g, from_sib, from_chips):
    _, R, C = g.shape
    _, hr, hc = from_sib.shape
    tile = _pick(hr, max(16, (1 << 19) // hc // 16 * 16), 16)
    nb = hr // tile
    if _rows_split((R, C)):
        mine = lambda i, wh: (wh[0] * nb + i, 0)
    else:
        mine = lambda i, wh: (i, wh[0])

    def body(where_ref, g_ref, s_ref, r_ref, o_ref):
        acc = g_ref[...] + s_ref[...]
        for r in range(3):
            acc = acc + r_ref[r].astype(F32)
        o_ref[...] = acc

    return pl.pallas_call(
        body, name=name,
        grid_spec=pltpu.PrefetchScalarGridSpec(
            num_scalar_prefetch=1, grid=(nb,),
            in_specs=[pl.BlockSpec((None, tile, hc), lambda i, wh: (wh[1], *mine(i, wh))),
                      pl.BlockSpec((None, tile, hc), lambda i, wh: (wh[1], i, 0)),
                      pl.BlockSpec((3, tile, hc), lambda i, wh: (0, i, 0))],
            out_specs=pl.BlockSpec((tile, hc), mine)),
        out_shape=jax.ShapeDtypeStruct((R, C), F32),
        compiler_params=_params(("parallel",)),
    )(where, g, from_sib, from_chips)


def _allgather8(name, blk, after=()):
    m_per, n = blk.shape
    rels = [(0, 0, 1), (1, 0, 0), (0, 1, 0), (1, 1, 0), (1, 0, 1), (0, 1, 1), (1, 1, 1)]

    def body(x_ref, *rest):
        out_ref, send_sems, recv_sems = rest[len(after):]
        me = (lax.axis_index("x"), lax.axis_index("y"), lax.axis_index("c"))
        my_rows = out_ref.at[pl.ds((4 * me[0] + 2 * me[1] + me[2]) * m_per, m_per), :]
        out_ref[pl.ds((4 * me[0] + 2 * me[1] + me[2]) * m_per, m_per), :] = x_ref[...]
        copies = []
        for t, rel in enumerate(rels):
            cp = pltpu.make_async_remote_copy(src_ref=x_ref, dst_ref=my_rows, send_sem=send_sems.at[t],
                                              recv_sem=recv_sems.at[t], device_id=_flip(me, rel),
                                              device_id_type=MESH)
            cp.start()
            copies.append(cp)
        for cp in copies:
            cp.wait()

    return pl.pallas_call(
        body, name=name, out_shape=jax.ShapeDtypeStruct((N_DEV * m_per, n), blk.dtype),
        in_specs=[pl.BlockSpec(memory_space=pltpu.VMEM)] + [pl.BlockSpec(memory_space=pl.ANY) for _ in after],
        out_specs=pl.BlockSpec(memory_space=pltpu.VMEM),
        scratch_shapes=[pltpu.SemaphoreType.DMA((7,)), pltpu.SemaphoreType.DMA((7,))],
        compiler_params=pltpu.CompilerParams(has_side_effects=True, vmem_limit_bytes=VMEM_LIMIT),
    )(blk, *after)


def _rope_tables(seq_len, dim, reps):
    rows = seq_len // GRID_W
    t = np.arange(seq_len)
    row, col = (t // GRID_W).astype(np.float32), (t % GRID_W).astype(np.float32)
    half = dim // 2
    inv = jnp.asarray(ROPE_THETA, F32) ** (-jnp.arange(0, half, 2, dtype=F32) / half)
    ang_r = jnp.asarray(row)[:, None] * inv[None, :]
    ang_c = jnp.asarray(col)[:, None] * inv[None, :]
    ang = jnp.concatenate([ang_r, ang_r, ang_c, ang_c], axis=-1)
    cos, sin = jnp.cos(ang), jnp.sin(ang)
    low = (np.arange(dim) % (dim // 2)) < (dim // 4)
    s_lo = jnp.where(jnp.asarray(low)[None, :], -sin, 0.0)
    s_hi = jnp.where(jnp.asarray(low)[None, :], 0.0, sin)
    del rows
    return tuple(jnp.tile(a, (1, reps)) for a in (cos, s_lo, s_hi))


def _make_rope(shift):
    def up(v):
        return pltpu.roll(v, LANE - shift, 1)

    def down(v):
        return pltpu.roll(v, shift, 1)

    @jax.custom_vjp
    def rope(v, cos, s_lo, s_hi):
        return v * cos + up(v) * s_lo + down(v) * s_hi

    def fwd(v, cos, s_lo, s_hi):
        return rope(v, cos, s_lo, s_hi), (cos, s_lo, s_hi)

    def bwd(res, dy):
        cos, s_lo, s_hi = res
        return dy * cos + down(dy * s_lo) + up(dy * s_hi), jnp.zeros_like(cos), jnp.zeros_like(cos), jnp.zeros_like(cos)

    rope.defvjp(fwd, bwd)
    return rope


_rope_a = _make_rope(HEAD_DIM // 4)
_rope_b = _make_rope(QK_ROPE // 4)


def _prep(segs, tabs, g_q, g_k, g_ckv):
    qa, ka, va, qn, qp, ckv, kpe = segs
    ta, tb = tabs
    q_a = jnp.concatenate([_rope_a(_rn(s) * g_q, *ta) for s in qa], axis=1)
    k_a = jnp.concatenate([_rope_a(_rn(s) * g_k, *ta) for s in ka], axis=1)
    cat = []
    for h in range(HB):
        cat += [qn[h], _rope_b(qp[h], *tb)]
    q_cat = jnp.concatenate(cat, axis=1)
    return q_a, k_a, va, q_cat, _rn(ckv) * g_ckv, _rope_b(kpe, *tb)


def _split_proj(proj):
    qa = [proj[:, O_QA + h * LANE:O_QA + (h + 1) * LANE] for h in range(HA)]
    ka = [proj[:, O_KA + h * LANE:O_KA + (h + 1) * LANE] for h in range(HKV)]
    va = proj[:, O_VA:O_QN]
    qn = [proj[:, O_QN + h * LANE:O_QN + (h + 1) * LANE] for h in range(HB)]
    qp = [proj[:, O_QP + h * LANE:O_QP + (h + 1) * LANE] for h in range(HB)]
    return qa, ka, va, qn, qp, proj[:, O_CKV:O_KPE], proj[:, O_KPE:W_INP]


def _permute_w_in(nat):
    K = nat.shape[1]
    qb = nat[W_QA + 2 * W_KA:W_QA + 2 * W_KA + W_QB].reshape(HB, QK_B, K)
    qp = jnp.concatenate([qb[:, QK_NOPE:], jnp.zeros((HB, LANE - QK_ROPE, K), nat.dtype)], axis=1)
    tail = nat[W_QA + 2 * W_KA + W_QB:]
    return jnp.concatenate([nat[:O_QN], qb[:, :QK_NOPE].reshape(HB * QK_NOPE, K), qp.reshape(HB * LANE, K),
                            tail, jnp.zeros((LANE - QK_ROPE, K), nat.dtype)], axis=0)


def _unpermute_pieces():
    pieces = [(0, 0, O_QN)]
    for h in range(HB):
        pieces.append((O_QN + h * QK_NOPE, O_QN + h * QK_B, QK_NOPE))
        pieces.append((O_QP + h * LANE, O_QN + h * QK_B + QK_NOPE, QK_ROPE))
    pieces.append((O_CKV, O_QN + W_QB, KV_RANK))
    pieces.append((O_KPE, O_QN + W_QB + KV_RANK, QK_ROPE))
    return pieces


def kernel(x, c, w_ada, b_ada, g_pre_attn, w_in, g_q_a, g_k_a, g_ckv, w_kv_b, g_out_a, g_out_b, w_out, g_post_attn, g_pre_mlp, w_mlp_in, w_mlp_out, g_post_mlp, loss_target, m_w_ada, m_b_ada, m_g_pre_attn, m_w_in, m_g_q_a, m_g_k_a, m_g_ckv, m_w_kv_b, m_g_out_a, m_g_out_b, m_w_out, m_g_post_attn, m_g_pre_mlp, m_w_mlp_in, m_w_mlp_out, m_g_post_mlp, v_w_ada, v_b_ada, v_g_pre_attn, v_w_in, v_g_q_a, v_g_k_a, v_g_ckv, v_w_kv_b, v_g_out_a, v_g_out_b, v_w_out, v_g_post_attn, v_g_pre_mlp, v_w_mlp_in, v_w_mlp_out, v_g_post_mlp):
    S, D = x.shape[1], x.shape[2]
    x2d, tgt = x[0], loss_target[0]
    ix, iy, ic = lax.axis_index("x"), lax.axis_index("y"), lax.axis_index("c")
    j_me = _chip(ix, iy)
    e_me = 4 * ix + 2 * iy + ic
    T = _pick(S, 512, 8)
    TB = _pick(S, 256, 8)

    where = jnp.stack([ic, j_me]).astype(jnp.int32)
    DFF = w_mlp_out.shape[1] * N_CHIPS

    c_all = _allgather8("gather_c", c.reshape(8, D // 8)).reshape(N_DEV, D)
    w_in_t = jnp.swapaxes(w_in[0], 0, 1)
    h_qkv, tok_a = _gather_start("qkv", _gather_casts("qkv", where, [w_in_t, w_kv_b[0]]), (c_all,))
    later = [_gather_casts(tag, where, [w[0]], (tok_a,))
             for tag, w in (("out", w_out), ("up", w_mlp_in), ("down", w_mlp_out))]

    def silu_fn(cc):
        return (cc * (1.0 / (1.0 + jnp.exp(-cc))),), ()

    (c_act,), _ = _rowwise("silu_c", silu_fn, [c_all], [], [(D, BF16)], [], N_DEV)
    ca_pad = jnp.concatenate([c_act, jnp.zeros_like(c_act)], axis=0)
    n_mod = w_ada.shape[2]
    b_cols = lax.dynamic_slice(b_ada, (0, j_me * n_mod), (1, n_mod))
    (mod_part,) = _mm_nn("mod_mm", ca_pad, w_ada, [F32], epilogue=lambda acc, b: (acc + b,),
                         extras=(jnp.broadcast_to(b_cols, (16, n_mod)),))
    mod_all = _allgather8("gather_mod", mod_part[:8] + tok_a[0, 0],
                          after=[c_[0] for c_ in later]).reshape(N_DEV, 8, n_mod)
    mod = jnp.concatenate([lax.dynamic_slice(mod_all, (2 * j, e_me, 0), (1, 1, n_mod))[0] for j in range(N_CHIPS)],
                          axis=1)

    m_qkv, tok_q = _gather_mid(h_qkv, mod_all)
    h_out, tok_b = _gather_start("out", later[0], (tok_q,))
    h_up, tok_c = _gather_start("up", later[1], (tok_b,))
    h_down, tok_d = _gather_start("down", later[2], (tok_c,))
    mod = mod + tok_d[0, 0]
    sh_a, sc_a, gt_a, sh_m, sc_m, gt_m = [mod[:, i * D:(i + 1) * D] for i in range(6)]

    tabs_a = _rope_tables(S, HEAD_DIM, 1)
    tabs_b = _rope_tables(S, QK_ROPE, LANE // QK_ROPE)

    def pre_attn_fn(xt, g, sc, sh):
        return (_rn(xt) * g * (1.0 + sc) + sh,), ()

    (h_b,), _ = _rowwise("pre_attn", pre_attn_fn, [x2d], [g_pre_attn, sc_a, sh_a], [(D, BF16)], [], T)
    g_in, g_kvb = _gather_end(m_qkv, h_b)
    wpt = _permute_w_in(g_in.reshape(W_IN, D))[None]
    (proj,) = _mm_nt("proj_mm", h_b, wpt, [F32])

    def prep_fn(pt, ca_, sla, sha, cb_, slb, shb, gq, gk, gc):
        return _prep(_split_proj(pt), ((ca_, sla, sha), (cb_, slb, shb)), gq, gk, gc), ()

    (qa_b, ka_b, va_b, qcat_b, ckvn_b, kpe_b), _ = _rowwise(
        "prep", prep_fn, [proj, *tabs_a, *tabs_b], [g_q_a, g_k_a, g_ckv],
        [(W_QA, BF16), (W_KA, BF16), (W_KA, BF16), (2 * HB * LANE, BF16), (KV_RANK, BF16), (LANE, BF16)], [], T)
    kcat_b, vb_b = _kv_mm("kv_mm", ckvn_b, g_kvb, kpe_b)

    TQ = _pick(S, 512, 16)
    sc_a_ = 1.0 / math.sqrt(HEAD_DIM)
    sc_b_ = 1.0 / math.sqrt(QK_B)
    o_a, lse_a = _attn_fwd("attn_a_fwd", qa_b, ka_b, va_b, HA, HA // HKV, HEAD_DIM, HEAD_DIM, sc_a_, TQ)
    m_out, tok_m = _gather_mid(h_out, o_a)
    o_b, lse_b = _attn_fwd("attn_b_fwd", qcat_b, kcat_b, vb_b, HB, 1, 2 * LANE, V_DIM, sc_b_, TQ, after=(tok_m,))
    m_up, tok_m = _gather_mid(h_up, o_b)

    def mix_fn(oa, ob, ga, gb):
        return jnp.concatenate([_rn(oa) * ga, _rn(ob) * gb], axis=1)

    (g_wout,) = _gather_end(m_out, o_b)
    w_out_f = g_wout.reshape(1, N_CHIPS * g_wout.shape[1], D)
    on_b, o2 = _rows_mm("out_mm", mix_fn, [o_a, o_b], [g_out_a, g_out_b], w_out_f, after=(tok_m,))

    def mid(xt, o2t, gta, gpa, gpm, scm, shm):
        x1 = xt + gta * (_rn(o2t) * gpa)
        return x1, _rn(x1) * gpm * (1.0 + scm) + shm

    def mid_fn(*a):
        return mid(*a), ()

    mid_vecs = [gt_a, g_post_attn, g_pre_mlp, sc_m, sh_m]
    (x1, h2_b), _ = _rowwise("mid", mid_fn, [x2d, o2], mid_vecs, [(D, F32), (D, BF16)], [], T)
    (g_mlp_in,) = _gather_end(m_up, h2_b)
    relu_b, a_b = _mm_nn("mlp_in_mm", h2_b, g_mlp_in, [BF16, BF16],
                         epilogue=lambda acc: (jnp.maximum(acc, 0.0), jnp.square(jnp.maximum(acc, 0.0))))
    (g_mlp_out,) = _gather_finish(h_down, a_b)
    w_mlp_out_f = g_mlp_out.reshape(1, DFF, D)
    (y,) = _mm_nn("mlp_out_mm", a_b, w_mlp_out_f, [F32])

    def last_fn(yt, x1t, tt, gtm, gpo):
        n, r = _rn_parts(yt)
        err = x1t + gtm * (n * gpo) - tt
        dx2 = err * (1.0 / D)
        loss = 0.5 * jnp.sum(jnp.mean(err * err, axis=-1))
        s = _rows(dx2 * n)
        return (_rn_bwd(n, r, dx2 * (gtm * gpo)), dx2), (jnp.full((1, LANE), loss, F32), gpo * s, gtm * s)

    (dy_b, dx2), (loss_v, d_gt_m, d_g_post_mlp) = _rowwise(
        "loss_bwd", last_fn, [y, x1, tgt], [gt_m, g_post_mlp], [(D, BF16), (D, F32)], [LANE, D, D], TB)

    gw_mlp_out = _mm_tn("gw_mlp_out_mm", a_b, dy_b, 1).reshape(N_CHIPS, DFF // N_CHIPS, D)
    b_down, tok = _reduce_begin("down", [gw_mlp_out])
    (du_b,) = _mm_nt("d_mlp_out_mm", dy_b, w_mlp_out_f, [BF16],
                     epilogue=lambda acc, rt: (acc * (2.0 * rt.astype(F32)),), extras=(relu_b,), after=(tok,))
    r_down, tok = _reduce_start(b_down, where, du_b)
    gw_mlp_in = _mm_tn("gw_mlp_in_mm", h2_b, du_b, N_CHIPS, after=(tok,))
    b_up, tok = _reduce_begin("up", [gw_mlp_in])
    (dh2,) = _mm_nt("d_mlp_in_mm", du_b, g_mlp_in, [F32], after=(tok,))
    r_up, tok_up = _reduce_start(b_up, where, dh2)

    def mid_bwd_fn(xt, o2t, dh2t, dx2t, gta, gpa, gpm, scm, shm):
        n2, r2 = _rn_parts(o2t)
        n1, r1 = _rn_parts(xt + gta * (n2 * gpa))
        dx1 = dx2t + _rn_bwd(n1, r1, dh2t * (gpm * (1.0 + scm)))
        s1, s2 = _rows(dh2t * n1), _rows(dx1 * n2)
        do2 = _rn_bwd(n2, r2, dx1 * (gta * gpa))
        return (dx1, do2), (gpa * s2, gta * s2, (1.0 + scm) * s1, gpm * s1, _rows(dh2t))

    (dx1, do2_b), (d_gt_a, d_g_post_attn, d_g_pre_mlp, d_sc_m, d_sh_m) = _rowwise(
        "mid_bwd", mid_bwd_fn, [x2d, o2, dh2, dx2], mid_vecs, [(D, F32), (D, BF16)], [D] * 5, TB, after=(tok_up,))

    (d_on,) = _mm_nt("d_out_mm", do2_b, w_out_f, [F32])
    gw_out = _mm_tn("gw_out_mm", on_b, do2_b, 1).reshape(N_CHIPS, w_out.shape[1], D)

    def mix_bwd_fn(oa, ob, dont, ga, gb):
        na, ra = _rn_parts(oa)
        nb, rb = _rn_parts(ob)
        da, db = dont[:, :W_QA], dont[:, W_QA:]
        return (_rn_bwd(na, ra, da * ga), _rn_bwd(nb, rb, db * gb)), (_rows(da * na), _rows(db * nb))

    (doa_b, dob_b), (d_g_out_a, d_g_out_b) = _rowwise(
        "mix_bwd", mix_bwd_fn, [o_a, o_b, d_on], [g_out_a, g_out_b], [(W_QA, BF16), (HB * V_DIM, BF16)],
        [W_QA, HB * V_DIM], TB)

    dqa, dka, dva = _attn_bwd("attn_a_bwd", qa_b, ka_b, va_b, doa_b, o_a, lse_a, HA, HA // HKV, HEAD_DIM, HEAD_DIM,
                              sc_a_, TQ)
    dqcat, dkcat, dvb = _attn_bwd("attn_b_bwd", qcat_b, kcat_b, vb_b, dob_b, o_b, lse_b, HB, 1, 2 * LANE, V_DIM,
                                  sc_b_, TQ)

    d_ckvn, dkv_b, dkpe = _kv_mm_bwd("d_kv_mm", dkcat, dvb, g_kvb)
    gw_kvb = _mm_tn("gw_kv_mm", ckvn_b, dkv_b, N_CHIPS)
    b_kvb, tok_kvb = _reduce_begin("okv", [gw_out, gw_kvb])

    def prep_bwd_fn(pt, dqat, dkat, dvat, dqct, dckt, dkpt, ca_, sla, sha, cb_, slb, shb, gq, gk, gc):
        tabs = ((ca_, sla, sha), (cb_, slb, shb))
        _, vjp = jax.vjp(lambda sg, a1, a2, a3: _prep(sg, tabs, a1, a2, a3), _split_proj(pt), gq, gk, gc)
        (dqa_s, dka_s, dva_s, dqn_s, dqp_s, dck_s, dkp_s), dgq, dgk, dgc = vjp((dqat, dkat, dvat, dqct, dckt, dkpt))
        dproj = jnp.concatenate([*dqa_s, *dka_s, dva_s, *dqn_s, *dqp_s, dck_s, dkp_s], axis=1)
        return (dproj,), (dgq, dgk, dgc)

    (dproj_b,), (d_g_q_a, d_g_k_a, d_g_ckv) = _rowwise(
        "prep_bwd", prep_bwd_fn, [proj, dqa, dka, dva, dqcat, d_ckvn, dkpe, *tabs_a, *tabs_b],
        [g_q_a, g_k_a, g_ckv], [(W_INP, BF16)], [HEAD_DIM, HEAD_DIM, KV_RANK], TB, after=(tok_kvb,))

    r_kvb, tok = _reduce_start(b_kvb, where, dproj_b)
    (dh,) = _mm_nn("d_proj_mm", dproj_b, wpt, [F32], after=(tok,))
    gw_in = _mm_tn_rows("gw_in_mm", dproj_b, h_b, W_IN, _unpermute_pieces()).reshape(N_CHIPS, W_IN // N_CHIPS, D)
    b_in, tok = _reduce_begin("in", [gw_in])

    def pre_attn_bwd_fn(xt, dht, dx1t, g, sc, sh):
        n, r = _rn_parts(xt)
        s = _rows(dht * n)
        return (dx1t + _rn_bwd(n, r, dht * (g * (1.0 + sc))),), ((1.0 + sc) * s, g * s, _rows(dht))

    (grad_x,), (d_g_pre_attn, d_sc_a, d_sh_a) = _rowwise(
        "pre_attn_bwd", pre_attn_bwd_fn, [x2d, dh, dx1], [g_pre_attn, sc_a, sh_a], [(D, F32)], [D] * 3, TB,
        after=(tok,))

    small_names = ["b_ada", "g_pre_attn", "g_q_a", "g_k_a", "g_ckv", "g_out_a", "g_out_b", "g_post_attn",
                   "g_pre_mlp", "g_post_mlp"]
    small_local = jnp.concatenate([d_sh_a, d_sc_a, d_gt_a, d_sh_m, d_sc_m, d_gt_m, d_g_pre_attn, d_g_q_a, d_g_k_a,
                                   d_g_ckv, d_g_out_a, d_g_out_b, d_g_post_attn, d_g_pre_mlp, d_g_post_mlp,
                                   loss_v[:, :8]], axis=1)
    n_sent = small_local.shape[1]
    n_small = n_sent - 8
    small_all = _allgather8("gather_small", small_local.reshape(8, n_sent // 8)).reshape(N_DEV, n_sent)

    def sum8_fn(t):
        return (), (jnp.sum(t, axis=0, keepdims=True),)

    r_in, tok_in = _reduce_start(b_in, where, grad_x, (small_all,))
    _, (sent_sum,) = _rowwise("sum_small", sum8_fn, [small_all], [], [], [n_sent], N_DEV, after=(tok_in,))
    small_sum, loss = sent_sum[:, :n_small], sent_sum[0, n_small]
    dmod_cols = lax.dynamic_slice(small_all, (0, j_me * n_mod), (N_DEV, n_mod))
    dmod_pad = jnp.concatenate([dmod_cols, jnp.zeros_like(dmod_cols)], axis=0).astype(BF16)
    gw_ada = _mm_tn("gw_ada_mm", ca_pad, dmod_pad, 1)[0]

    weights = dict(w_ada=w_ada, b_ada=b_ada, g_pre_attn=g_pre_attn, w_in=w_in, g_q_a=g_q_a, g_k_a=g_k_a, g_ckv=g_ckv,
                   w_kv_b=w_kv_b, g_out_a=g_out_a, g_out_b=g_out_b, w_out=w_out, g_post_attn=g_post_attn,
                   g_pre_mlp=g_pre_mlp, w_mlp_in=w_mlp_in, w_mlp_out=w_mlp_out, g_post_mlp=g_post_mlp)
    ms = dict(w_ada=m_w_ada, b_ada=m_b_ada, g_pre_attn=m_g_pre_attn, w_in=m_w_in, g_q_a=m_g_q_a, g_k_a=m_g_k_a,
              g_ckv=m_g_ckv, w_kv_b=m_w_kv_b, g_out_a=m_g_out_a, g_out_b=m_g_out_b, w_out=m_w_out,
              g_post_attn=m_g_post_attn, g_pre_mlp=m_g_pre_mlp, w_mlp_in=m_w_mlp_in, w_mlp_out=m_w_mlp_out,
              g_post_mlp=m_g_post_mlp)
    vs = dict(w_ada=v_w_ada, b_ada=v_b_ada, g_pre_attn=v_g_pre_attn, w_in=v_w_in, g_q_a=v_g_q_a, g_k_a=v_g_k_a,
              g_ckv=v_g_ckv, w_kv_b=v_w_kv_b, g_out_a=v_g_out_a, g_out_b=v_g_out_b, w_out=v_w_out,
              g_post_attn=v_g_post_attn, g_pre_mlp=v_g_pre_mlp, w_mlp_in=v_w_mlp_in, w_mlp_out=v_w_mlp_out,
              g_post_mlp=v_g_post_mlp)
    order = list(weights)
    grads, deltas, new_m, new_v = {}, {}, {}, {}

    def update(name, g, after=()):
        view = (lambda t: jnp.swapaxes(t, 0, 1)) if name == "w_in" else (lambda t: t)
        d_, m_, v_, *g_ = _adamw("adamw_" + name, view(weights[name][0]), g, view(ms[name][0]), view(vs[name][0]),
                                 after, rewrite_g=name != "w_ada")
        grads[name], deltas[name], new_m[name], new_v[name] = (view(t)[None] for t in ((g_ or [g])[0], d_, m_, v_))
        return d_

    chain = ((("w_mlp_out",), r_down), (("w_mlp_in",), r_up), (("w_out", "w_kv_b"), r_kvb), (("w_in",), r_in))
    swap, tok = _reduce_finish(chain[0][1], where, small_sum)
    last = update("w_ada", gw_ada, (tok,))
    for k, (names, _) in enumerate(chain):
        gs = _reduce_end(swap, last)
        if k + 1 < len(chain):
            swap, tok = _reduce_finish(chain[k + 1][1], where, gs[0])
        for name, g in zip(names, gs):
            last = update(name, g, (tok,))

    cat = lambda d: jnp.concatenate([d[n] for n in small_names], axis=1)
    d_, m_, v_ = _adamw("adamw_small", cat(weights), small_sum, cat(ms), cat(vs))
    off = 0
    for n in small_names:
        w_ = weights[n].shape[1]
        grads[n], deltas[n], new_m[n], new_v[n] = (a[:, off:off + w_] for a in (small_sum, d_, m_, v_))
        off += w_

    return (loss, grad_x[None], *[grads[n] for n in order], *[deltas[n] for n in order],
            *[new_m[n] for n in order], *[new_v[n] for n in order])
```

```python
import functools
import math

import numpy as np
import jax
import jax.numpy as jnp
from jax import lax
from jax.experimental import pallas as pl
from jax.experimental.pallas import tpu as pltpu

F32 = jnp.float32
BF16 = jnp.bfloat16
MESH = pl.DeviceIdType.MESH

EPS = 1e-6
GRID_W = 64
ROPE_THETA = 10000.0
HEAD_DIM = 128
HA = 8
HKV = 2
HB = 8
QK_NOPE = 128
QK_ROPE = 64
V_DIM = 128
KV_RANK = 512
QK_B = QK_NOPE + QK_ROPE
W_QA = HA * HEAD_DIM
W_KA = HKV * HEAD_DIM
W_QB = HB * QK_B
W_IN = W_QA + 2 * W_KA + W_QB + KV_RANK + QK_ROPE
N_CHIPS = 4
N_DEV = 8

LANE = 128
O_QA = 0
O_KA = O_QA + W_QA
O_VA = O_KA + W_KA
O_QN = O_VA + W_KA
O_QP = O_QN + HB * QK_NOPE
O_CKV = O_QP + HB * LANE
O_KPE = O_CKV + KV_RANK
W_INP = O_KPE + LANE

ADAM_LR = 0.001
ADAM_B1 = 0.9
ADAM_B2 = 0.999
ADAM_EPS = 1e-08
ADAM_WD = 0.01
ADAM_STEP = 10

VMEM_LIMIT = 56 * 1024 * 1024
HEADS_PER_STEP = 2


def _params(sem):
    return pltpu.CompilerParams(dimension_semantics=sem, vmem_limit_bytes=VMEM_LIMIT)


def _pick(dim, target, unit):
    best = None
    t = unit
    while t <= min(dim, target):
        if dim % t == 0:
            best = t
        t += unit
    return dim if best is None else best


def _rn(x):
    return x * lax.rsqrt(jnp.mean(x * x, axis=-1, keepdims=True) + EPS)


def _rn_parts(x):
    r = lax.rsqrt(jnp.mean(x * x, axis=-1, keepdims=True) + EPS)
    return x * r, r


def _rn_bwd(n, r, dn):
    return r * (dn - n * jnp.mean(dn * n, axis=-1, keepdims=True))


def _rows(t):
    return jnp.sum(t, axis=0, keepdims=True)


def _rowwise(name, fn, rows, vecs, row_outs, vec_outs, tile, after=()):
    S = rows[0].shape[0]
    assert S % tile == 0
    n_r, n_v, n_ro, n_vo, n_a = len(rows), len(vecs), len(row_outs), len(vec_outs), len(after)

    def body(*refs):
        r_in = refs[:n_r]
        v_in = refs[n_r:n_r + n_v]
        r_out = refs[n_r + n_v + n_a:n_r + n_v + n_a + n_ro]
        v_out = refs[n_r + n_v + n_a + n_ro:]
        ro, vo = fn(*[r[...] for r in r_in], *[v[...] for v in v_in])
        for ref, val in zip(r_out, ro):
            ref[...] = val.astype(ref.dtype)
        if n_vo:
            @pl.when(pl.program_id(0) == 0)
            def _():
                for ref in v_out:
                    ref[...] = jnp.zeros(ref.shape, ref.dtype)
            for ref, val in zip(v_out, vo):
                ref[...] += val

    in_specs = [pl.BlockSpec((tile, a.shape[1]), lambda i: (i, 0)) for a in rows]
    in_specs += [pl.BlockSpec(a.shape, lambda i: (0, 0)) for a in vecs]
    in_specs += [pl.BlockSpec(memory_space=pl.ANY) for _ in after]
    out_specs = [pl.BlockSpec((tile, w), lambda i: (i, 0)) for w, _ in row_outs]
    out_specs += [pl.BlockSpec((1, w), lambda i: (0, 0)) for w in vec_outs]
    out_shape = [jax.ShapeDtypeStruct((S, w), dt) for w, dt in row_outs]
    out_shape += [jax.ShapeDtypeStruct((1, w), F32) for w in vec_outs]
    res = pl.pallas_call(
        body, name=name, grid=(S // tile,), in_specs=in_specs, out_specs=out_specs, out_shape=out_shape,
        compiler_params=_params(("arbitrary",)),
    )(*rows, *vecs, *after)
    return res[:n_ro], res[n_ro:]


CONTRACT_TILE = 2048
MM_TILE_BYTES = 40 * 1024 * 1024


def _rows_that_fit(M, tile_bytes, most=1024):
    t = _pick(M, most, 16)
    while t % 32 == 0 and tile_bytes(t) > MM_TILE_BYTES:
        t //= 2
    return t


def _accumulate(acc, part, step, n_steps, finish):
    if n_steps == 1:
        finish(part)
        return

    @pl.when(step == 0)
    def _():
        acc[...] = part

    @pl.when((step > 0) & (step < n_steps - 1))
    def _():
        acc[...] += part

    @pl.when(step == n_steps - 1)
    def _():
        finish(acc[...] + part)


def _mm_nn(name, a, b3, out_dtypes, epilogue=None, extras=(), after=()):
    M, K = a.shape
    nB, K2, Nsh = b3.shape
    assert K == K2
    N = nB * Nsh
    tn, tk = _pick(Nsh, 512, LANE), _pick(K, CONTRACT_TILE, LANE)
    out_bytes = sum(jnp.dtype(dt).itemsize for dt in out_dtypes) + sum(e.dtype.itemsize for e in extras)
    tm = _rows_that_fit(M, lambda t: 2 * (t * tk * a.dtype.itemsize + tk * tn * b3.dtype.itemsize)
                        + t * tn * (4 * (K > tk) + 2 * out_bytes), most=2048)
    nps, nk, ne, no, na = Nsh // tn, K // tk, len(extras), len(out_dtypes), len(after)

    def body(a_ref, b_ref, *rest):
        e_refs, o_refs = rest[:ne], rest[ne + na:ne + na + no]

        def finish(total):
            vals = (total,) if epilogue is None else epilogue(total, *[e[...] for e in e_refs])
            for ref, val in zip(o_refs, vals):
                ref[...] = val.astype(ref.dtype)

        part = jnp.dot(a_ref[...].astype(BF16), b_ref[...].astype(BF16), preferred_element_type=F32)
        _accumulate(rest[-1], part, pl.program_id(2), nk, finish)

    in_specs = [pl.BlockSpec((tm, tk), lambda i, j, k: (i, k)),
                pl.BlockSpec((None, tk, tn), lambda i, j, k: (j // nps, k, j % nps))]
    in_specs += [pl.BlockSpec((tm, tn), lambda i, j, k: (i, j)) for _ in extras]
    in_specs += [pl.BlockSpec(memory_space=pl.ANY) for _ in after]
    return pl.pallas_call(
        body, name=name, grid=(M // tm, N // tn, nk), in_specs=in_specs,
        out_specs=[pl.BlockSpec((tm, tn), lambda i, j, k: (i, j)) for _ in out_dtypes],
        out_shape=[jax.ShapeDtypeStruct((M, N), dt) for dt in out_dtypes],
        scratch_shapes=[pltpu.VMEM((tm, tn), F32)] if nk > 1 else [],
        compiler_params=_params(("parallel", "parallel", "arbitrary")),
    )(a, b3, *extras, *after)


def _mm_nt(name, a, b3, out_dtypes, epilogue=None, extras=(), after=()):
    M, N = a.shape
    nB, K, Nsh = b3.shape
    assert N == nB * Nsh
    to, tn = _pick(K, 1536, LANE), _pick(Nsh, CONTRACT_TILE, LANE)
    out_bytes = sum(jnp.dtype(dt).itemsize for dt in out_dtypes) + sum(e.dtype.itemsize for e in extras)
    tm = _rows_that_fit(M, lambda t: 2 * (t * tn * a.dtype.itemsize + to * tn * b3.dtype.itemsize)
                        + t * to * (4 * (N > tn) + 2 * out_bytes))
    nps, nn, ne, no, na = Nsh // tn, N // tn, len(extras), len(out_dtypes), len(after)

    def body(a_ref, b_ref, *rest):
        e_refs, o_refs = rest[:ne], rest[ne + na:ne + na + no]

        def finish(total):
            vals = (total,) if epilogue is None else epilogue(total, *[e[...] for e in e_refs])
            for ref, val in zip(o_refs, vals):
                ref[...] = val.astype(ref.dtype)

        part = lax.dot_general(a_ref[...].astype(BF16), b_ref[...].astype(BF16),
                               (((1,), (1,)), ((), ())), preferred_element_type=F32)
        _accumulate(rest[-1], part, pl.program_id(2), nn, finish)

    in_specs = [pl.BlockSpec((tm, tn), lambda i, o, n: (i, n)),
                pl.BlockSpec((None, to, tn), lambda i, o, n: (n // nps, o, n % nps))]
    in_specs += [pl.BlockSpec((tm, to), lambda i, o, n: (i, o)) for _ in extras]
    in_specs += [pl.BlockSpec(memory_space=pl.ANY) for _ in after]
    return pl.pallas_call(
        body, name=name, grid=(M // tm, K // to, nn), in_specs=in_specs,
        out_specs=[pl.BlockSpec((tm, to), lambda i, o, n: (i, o)) for _ in out_dtypes],
        out_shape=[jax.ShapeDtypeStruct((M, K), dt) for dt in out_dtypes],
        scratch_shapes=[pltpu.VMEM((tm, to), F32)] if nn > 1 else [],
        compiler_params=_params(("parallel", "parallel", "arbitrary")),
    )(a, b3, *extras, *after)


def _mm_tn(name, a, g, nB, after=()):
    M, K = a.shape
    M2, N = g.shape
    assert M == M2 and N % nB == 0
    Nsh = N // nB
    tk, tn, tm = _pick(K, 1536, LANE), _pick(Nsh, 1024, LANE), _pick(M, CONTRACT_TILE, 16)
    nps, nm = Nsh // tn, M // tm

    def body(a_ref, g_ref, *rest):
        o_ref = rest[len(after)]

        def finish(total):
            o_ref[...] = total

        part = lax.dot_general(a_ref[...].astype(BF16), g_ref[...].astype(BF16),
                               (((0,), (0,)), ((), ())), preferred_element_type=F32)
        _accumulate(rest[-1], part, pl.program_id(2), nm, finish)

    return pl.pallas_call(
        body, name=name, grid=(K // tk, N // tn, nm),
        in_specs=[pl.BlockSpec((tm, tk), lambda k, n, m: (m, k)),
                  pl.BlockSpec((tm, tn), lambda k, n, m: (m, n))] + [pl.BlockSpec(memory_space=pl.ANY) for _ in after],
        out_specs=pl.BlockSpec((None, tk, tn), lambda k, n, m: (n // nps, k, n % nps)),
        out_shape=jax.ShapeDtypeStruct((nB, K, Nsh), F32),
        scratch_shapes=[pltpu.VMEM((tk, tn), F32)] if nm > 1 else [],
        compiler_params=_params(("parallel", "parallel", "arbitrary")),
    )(a, g, *after)


def _rows_mm(name, fn, rows, vecs, b3, after=()):
    S = rows[0].shape[0]
    _, K, N = b3.shape
    tm, tn = _pick(S, 1024, 16), _pick(N, 512, LANE)
    nr, nv, na = len(rows), len(vecs), len(after)

    def body(*refs):
        b_ref = refs[nr + nv]
        a_out, o_ref, a_scr = refs[nr + nv + 1 + na:]

        @pl.when(pl.program_id(1) == 0)
        def _():
            a = fn(*[r[...] for r in refs[:nr]], *[v[...] for v in refs[nr:nr + nv]]).astype(BF16)
            a_scr[...] = a
            a_out[...] = a

        o_ref[...] = jnp.dot(a_scr[...], b_ref[...], preferred_element_type=F32)

    in_specs = [pl.BlockSpec((tm, r.shape[1]), lambda i, j: (i, 0)) for r in rows]
    in_specs += [pl.BlockSpec(v.shape, lambda i, j: (0, 0)) for v in vecs]
    in_specs += [pl.BlockSpec((None, K, tn), lambda i, j: (0, 0, j))]
    in_specs += [pl.BlockSpec(memory_space=pl.ANY) for _ in after]
    return pl.pallas_call(
        body, name=name, grid=(S // tm, N // tn), in_specs=in_specs,
        out_specs=[pl.BlockSpec((tm, K), lambda i, j: (i, 0)), pl.BlockSpec((tm, tn), lambda i, j: (i, j))],
        out_shape=[jax.ShapeDtypeStruct((S, K), BF16), jax.ShapeDtypeStruct((S, N), F32)],
        scratch_shapes=[pltpu.VMEM((tm, K), BF16)],
        compiler_params=_params(("parallel", "arbitrary")),
    )(*rows, *vecs, b3, *after)


def _kv_mm(name, ckvn, w3, kpe):
    S, K = ckvn.shape
    nB, _, Nsh = w3.shape
    heads = Nsh // (2 * LANE)
    tm = _pick(S, 1024, 16)

    def body(a_ref, b_ref, kpe_ref, kc_ref, v_ref):
        acc = jnp.dot(a_ref[...], b_ref[...], preferred_element_type=F32)
        for h in range(heads):
            kc_ref[:, 2 * h * LANE:(2 * h + 1) * LANE] = acc[:, 2 * h * LANE:(2 * h + 1) * LANE].astype(BF16)
            kc_ref[:, (2 * h + 1) * LANE:(2 * h + 2) * LANE] = kpe_ref[...]
            v_ref[:, h * LANE:(h + 1) * LANE] = acc[:, (2 * h + 1) * LANE:(2 * h + 2) * LANE].astype(BF16)

    return pl.pallas_call(
        body, name=name, grid=(S // tm, nB),
        in_specs=[pl.BlockSpec((tm, K), lambda i, j: (i, 0)), pl.BlockSpec((None, K, Nsh), lambda i, j: (j, 0, 0)),
                  pl.BlockSpec((tm, LANE), lambda i, j: (i, 0))],
        out_specs=[pl.BlockSpec((tm, Nsh), lambda i, j: (i, j)), pl.BlockSpec((tm, Nsh // 2), lambda i, j: (i, j))],
        out_shape=[jax.ShapeDtypeStruct((S, nB * Nsh), BF16), jax.ShapeDtypeStruct((S, nB * Nsh // 2), BF16)],
        compiler_params=_params(("parallel", "arbitrary")),
    )(ckvn, w3, kpe)


def _kv_mm_bwd(name, dkc, dv, w3):
    S = dkc.shape[0]
    nB, K, Nsh = w3.shape
    heads = Nsh // (2 * LANE)
    tm = _pick(S, 1024, 16)

    def body(dkc_ref, dv_ref, b_ref, dck_ref, dkv_ref, dkpe_ref):
        j = pl.program_id(1)
        pieces, pe = [], None
        for h in range(heads):
            pieces += [dkc_ref[:, 2 * h * LANE:(2 * h + 1) * LANE], dv_ref[:, h * LANE:(h + 1) * LANE]]
            piece = dkc_ref[:, (2 * h + 1) * LANE:(2 * h + 2) * LANE]
            pe = piece if pe is None else pe + piece
        dkv = jnp.concatenate(pieces, axis=1).astype(BF16)
        dkv_ref[...] = dkv
        part = lax.dot_general(dkv, b_ref[...], (((1,), (1,)), ((), ())), preferred_element_type=F32)

        @pl.when(j == 0)
        def _():
            dck_ref[...] = part
            dkpe_ref[...] = pe

        @pl.when(j > 0)
        def _():
            dck_ref[...] += part
            dkpe_ref[...] += pe

    return pl.pallas_call(
        body, name=name, grid=(S // tm, nB),
        in_specs=[pl.BlockSpec((tm, Nsh), lambda i, j: (i, j)), pl.BlockSpec((tm, Nsh // 2), lambda i, j: (i, j)),
                  pl.BlockSpec((None, K, Nsh), lambda i, j: (j, 0, 0))],
        out_specs=[pl.BlockSpec((tm, K), lambda i, j: (i, 0)), pl.BlockSpec((tm, Nsh), lambda i, j: (i, j)),
                   pl.BlockSpec((tm, LANE), lambda i, j: (i, 0))],
        out_shape=[jax.ShapeDtypeStruct((S, K), F32), jax.ShapeDtypeStruct((S, nB * Nsh), BF16),
                   jax.ShapeDtypeStruct((S, LANE), F32)],
        compiler_params=_params(("parallel", "arbitrary")),
    )(dkc, dv, w3)


def _mm_tn_rows(name, a, g, n_rows, pieces):
    M, K = a.shape
    M2, N = g.shape
    assert M == M2
    tk, tn, tm = _pick(K, 1536, LANE), _pick(N, 512, LANE), _pick(M, CONTRACT_TILE, 16)
    nk, nm = K // tk, M // tm

    def body(a_ref, g_ref, o_ref, *acc):
        kk = pl.program_id(1)

        def finish(total):
            for t in range(nk):
                @pl.when(kk == t)
                def _(t=t):
                    for src, dst, n in pieces:
                        lo, hi = max(src, t * tk), min(src + n, (t + 1) * tk)
                        if lo < hi:
                            o_ref[dst + lo - src:dst + hi - src, :] = total[lo - t * tk:hi - t * tk, :]

        part = lax.dot_general(a_ref[...].astype(BF16), g_ref[...].astype(BF16),
                               (((0,), (0,)), ((), ())), preferred_element_type=F32)
        _accumulate(acc[0] if acc else None, part, pl.program_id(2), nm, finish)

    return pl.pallas_call(
        body, name=name, grid=(N // tn, nk, nm),
        in_specs=[pl.BlockSpec((tm, tk), lambda n, k, m: (m, k)),
                  pl.BlockSpec((tm, tn), lambda n, k, m: (m, n))],
        out_specs=pl.BlockSpec((n_rows, tn), lambda n, k, m: (0, n)),
        out_shape=jax.ShapeDtypeStruct((n_rows, N), F32),
        scratch_shapes=[pltpu.VMEM((tk, tn), F32)] if nm > 1 else [],
        compiler_params=_params(("arbitrary", "arbitrary", "arbitrary")),
    )(a, g)


def _attn_fwd(name, q, k, v, n_heads, group, dqk, dv, scale, tq, after=()):
    S = q.shape[0]
    hp = 2 * HEADS_PER_STEP
    kvp = max(1, hp // group)

    def body(q_ref, k_ref, v_ref, *rest):
        o_ref, lse_ref = rest[-2:]
        for j in range(hp):
            jk = j * kvp // hp
            s = lax.dot_general(q_ref[:, j * dqk:(j + 1) * dqk], k_ref[:, jk * dqk:(jk + 1) * dqk],
                                (((1,), (1,)), ((), ())), preferred_element_type=F32)
            m = jnp.max(s, axis=-1, keepdims=True)
            e = jnp.exp((s - m) * scale)
            l = jnp.sum(e, axis=-1, keepdims=True)
            o_ref[:, j * dv:(j + 1) * dv] = jnp.dot(e.astype(BF16), v_ref[:, jk * dv:(jk + 1) * dv],
                                                    preferred_element_type=F32) * (1.0 / l)
            lse_ref[:, j * LANE:(j + 1) * LANE] = jnp.broadcast_to(m * scale + jnp.log(l), (tq, LANE))

    kv_block = lambda h, i: (0, (h * hp // group) // kvp)
    return pl.pallas_call(
        body, name=name, grid=(n_heads // hp, S // tq),
        in_specs=[pl.BlockSpec((tq, hp * dqk), lambda h, i: (i, h)),
                  pl.BlockSpec((S, kvp * dqk), kv_block),
                  pl.BlockSpec((S, kvp * dv), kv_block)] + [pl.BlockSpec(memory_space=pl.ANY) for _ in after],
        out_specs=[pl.BlockSpec((tq, hp * dv), lambda h, i: (i, h)),
                   pl.BlockSpec((tq, hp * LANE), lambda h, i: (i, h))],
        out_shape=[jax.ShapeDtypeStruct((S, n_heads * dv), F32), jax.ShapeDtypeStruct((S, n_heads * LANE), F32)],
        compiler_params=_params(("arbitrary", "arbitrary")),
    )(q, k, v, *after)


def _attn_bwd(name, q, k, v, do, o, lse, n_heads, group, dqk, dv, scale, tq, after=()):
    S = q.shape[0]
    n_kv = n_heads // group
    hp, tq = (group, tq // 2) if group > HEADS_PER_STEP else (HEADS_PER_STEP, tq)
    kvp = max(1, hp // group)
    chains = 4
    tr = tq * hp // chains

    def body(q_ref, k_ref, v_ref, do_ref, o_ref, lse_ref, *rest):
        dq_ref, dk_ref, dv_ref = rest[len(after):]
        h, i = pl.program_id(0), pl.program_id(1)

        @pl.when(((h * hp) % group == 0) & (i == 0))
        def _():
            dk_ref[...] = jnp.zeros(dk_ref.shape, F32)
            dv_ref[...] = jnp.zeros(dv_ref.shape, F32)

        for j in range(hp):
            jk = j * kvp // hp
            kb, vb = k_ref[:, jk * dqk:(jk + 1) * dqk], v_ref[:, jk * dv:(jk + 1) * dv]
            for r0 in range(0, tq, tr):
                rows = slice(r0, r0 + tr)
                qb, dob = q_ref[rows, j * dqk:(j + 1) * dqk], do_ref[rows, j * dv:(j + 1) * dv]
                s = lax.dot_general(qb, kb, (((1,), (1,)), ((), ())), preferred_element_type=F32)
                p = jnp.exp(s * scale - jnp.tile(lse_ref[rows, j * LANE:(j + 1) * LANE], (1, S // LANE)))
                dp = lax.dot_general(dob, vb, (((1,), (1,)), ((), ())), preferred_element_type=F32)
                delta = jnp.sum(dob.astype(F32) * o_ref[rows, j * dv:(j + 1) * dv], axis=-1, keepdims=True)
                ds = (p * ((dp - delta) * scale)).astype(BF16)
                dv_ref[:, jk * dv:(jk + 1) * dv] += lax.dot_general(p.astype(BF16), dob, (((0,), (0,)), ((), ())),
                                                                    preferred_element_type=F32)
                dq_ref[rows, j * dqk:(j + 1) * dqk] = jnp.dot(ds, kb, preferred_element_type=F32)
                dk_ref[:, jk * dqk:(jk + 1) * dqk] += lax.dot_general(ds, qb, (((0,), (0,)), ((), ())),
                                                                      preferred_element_type=F32)

    kv_block = lambda h, i: (0, (h * hp // group) // kvp)
    return pl.pallas_call(
        body, name=name, grid=(n_heads // hp, S // tq),
        in_specs=[pl.BlockSpec((tq, hp * dqk), lambda h, i: (i, h)),
                  pl.BlockSpec((S, kvp * dqk), kv_block),
                  pl.BlockSpec((S, kvp * dv), kv_block),
                  pl.BlockSpec((tq, hp * dv), lambda h, i: (i, h)),
                  pl.BlockSpec((tq, hp * dv), lambda h, i: (i, h)),
                  pl.BlockSpec((tq, hp * LANE), lambda h, i: (i, h))] + [pl.BlockSpec(memory_space=pl.ANY) for _ in after],
        out_specs=[pl.BlockSpec((tq, hp * dqk), lambda h, i: (i, h)),
                   pl.BlockSpec((S, kvp * dqk), kv_block),
                   pl.BlockSpec((S, kvp * dv), kv_block)],
        out_shape=[jax.ShapeDtypeStruct((S, n_heads * dqk), F32),
                   jax.ShapeDtypeStruct((S, n_kv * dqk), F32),
                   jax.ShapeDtypeStruct((S, n_kv * dv), F32)],
        compiler_params=_params(("arbitrary", "arbitrary")),
    )(q, k, v, do, o, lse, *after)


def _adamw(name, w, g, m, v, after=(), rewrite_g=False):
    R, C = w.shape
    tile = _pick(R, max(8, (1 << 19) // C // 8 * 8), 8)
    c1 = 1.0 - ADAM_B1 ** ADAM_STEP
    c2 = 1.0 - ADAM_B2 ** ADAM_STEP
    n_out = 4 if rewrite_g else 3

    def body(w_ref, g_ref, m_ref, v_ref, *rest):
        d_ref, nm_ref, nv_ref = rest[len(after):len(after) + 3]
        gg = g_ref[...]
        nm = ADAM_B1 * m_ref[...] + (1.0 - ADAM_B1) * gg
        nv = ADAM_B2 * v_ref[...] + (1.0 - ADAM_B2) * (gg * gg)
        d_ref[...] = -ADAM_LR * ((nm / c1) / (jnp.sqrt(nv / c2) + ADAM_EPS) + ADAM_WD * w_ref[...])
        nm_ref[...] = nm
        nv_ref[...] = nv
        if rewrite_g:
            rest[-1][...] = gg

    spec = pl.BlockSpec((tile, C), lambda i: (i, 0))
    return pl.pallas_call(
        body, name=name, grid=(R // tile,), in_specs=[spec] * 4 + [pl.BlockSpec(memory_space=pl.ANY) for _ in after],
        out_specs=[spec] * n_out, out_shape=[jax.ShapeDtypeStruct((R, C), F32)] * n_out,
        compiler_params=_params(("parallel",)),
    )(w, g, m, v, *after)


def _flip(me, rel):
    return tuple(1 - p if r else p for p, r in zip(me, rel))


def _push(name, ins, out_shapes, transfers, aliases=None):
    ni, no, nt = len(ins), len(out_shapes), len(transfers)

    def body(*refs):
        in_refs, out_refs = refs[:ni], refs[ni:ni + no]
        send_sems, recv_sems = refs[ni + no], refs[ni + no + 1]
        me = (lax.axis_index("x"), lax.axis_index("y"), lax.axis_index("c"))
        copies = []
        for t, (si, sfn, oi, dfn, rel) in enumerate(transfers):
            src_ref = out_refs[si[1]] if isinstance(si, tuple) else in_refs[si]
            src = src_ref.at[sfn(*me)]
            dst = out_refs[oi].at[dfn(*me)]
            if rel is None:
                cp = pltpu.make_async_copy(src, dst, send_sems.at[t])
            else:
                cp = pltpu.make_async_remote_copy(src_ref=src, dst_ref=dst, send_sem=send_sems.at[t],
                                                  recv_sem=recv_sems.at[t], device_id=_flip(me, rel),
                                                  device_id_type=MESH)
            cp.start()
            copies.append(cp)
        for cp in copies:
            cp.wait()

    any_spec = pl.BlockSpec(memory_space=pl.ANY)
    return pl.pallas_call(
        body, name=name, in_specs=[any_spec] * ni, out_specs=[any_spec] * no, out_shape=out_shapes,
        scratch_shapes=[pltpu.SemaphoreType.DMA((nt,)), pltpu.SemaphoreType.DMA((nt,))],
        input_output_aliases=aliases or {},
        compiler_params=pltpu.CompilerParams(has_side_effects=True),
    )(*ins)


_HBM = pl.BlockSpec(memory_space=pltpu.HBM)
_SEM = pl.BlockSpec(memory_space=pltpu.SEMAPHORE)
_DATAFLOW = pltpu.SideEffectType.DATAFLOW_SIDE_EFFECTING


def _split_copies(refs, transfers, send_sems, recv_sems):
    me = (lax.axis_index("x"), lax.axis_index("y"), lax.axis_index("c"))
    return [pltpu.make_async_remote_copy(src_ref=refs[sb].at[sfn(*me)], dst_ref=refs[db].at[dfn(*me)],
                                         send_sem=send_sems.at[t], recv_sem=recv_sems.at[t],
                                         device_id=_flip(me, rel), device_id_type=MESH)
            for t, (sb, sfn, db, dfn, rel) in enumerate(transfers)]


def _push_start(name, bufs, transfers, after=()):
    nb, na, nt = len(bufs), len(after), len(transfers)

    def body(*refs):
        send_sems, recv_sems, token = refs[nb + na], refs[nb + na + 1], refs[-1]
        for cp in _split_copies(refs[nb + na + 2:2 * nb + na + 2], transfers, send_sems, recv_sems):
            cp.start()
        token[...] = jnp.zeros(token.shape, token.dtype)

    res = pl.pallas_call(
        body, name=name,
        out_shape=(pltpu.SemaphoreType.DMA((nt,)), pltpu.SemaphoreType.DMA((nt,)),
                   *[pltpu.HBM(b.shape, b.dtype) for b in bufs], jax.ShapeDtypeStruct((8, LANE), F32)),
        in_specs=[_HBM] * nb + [pl.BlockSpec(memory_space=pl.ANY)] * na,
        out_specs=(_SEM, _SEM, *[_HBM] * nb, pl.BlockSpec(memory_space=pltpu.VMEM)),
        input_output_aliases={i: i + 2 for i in range(nb)},
        compiler_params=pltpu.CompilerParams(has_side_effects=_DATAFLOW),
    )(*[pltpu.with_memory_space_constraint(b, pltpu.HBM) for b in bufs], *after)
    return res[0], res[1], list(res[2:2 + nb]), res[-1]


def _push_wait(name, send_sems, recv_sems, bufs, transfers, after):
    nb = len(bufs)

    def body(*refs):
        for cp in _split_copies(refs[:nb], transfers, refs[nb], refs[nb + 1]):
            cp.wait_send()
            cp.wait_recv()

    res = pl.pallas_call(
        body, name=name, out_shape=[pltpu.HBM(b.shape, b.dtype) for b in bufs],
        in_specs=[_HBM] * nb + [_SEM, _SEM, pl.BlockSpec(memory_space=pl.ANY)], out_specs=[_HBM] * nb,
        input_output_aliases={i: i for i in range(nb)},
        compiler_params=pltpu.CompilerParams(has_side_effects=_DATAFLOW),
    )(*bufs, send_sems, recv_sems, after)
    return list(res)


ICI_RELS = ((1, 0, 0), (0, 1, 0), (1, 1, 0))
SIBLING = (0, 0, 1)


def _chip(x, y):
    return 2 * x + y


def _cast_into_slot(name, where, w, after=()):
    R, C = w.shape
    tile = _pick(R, 512, BF16_ROWS)

    def body(where_ref, w_ref, *rest):
        rest[-1][...] = w_ref[...].astype(BF16)

    return pl.pallas_call(
        body, name=name,
        grid_spec=pltpu.PrefetchScalarGridSpec(
            num_scalar_prefetch=1, grid=(R // tile,),
            in_specs=[pl.BlockSpec((tile, C), lambda i, wh: (i, 0))] + [pl.BlockSpec(memory_space=pl.ANY) for _ in after],
            out_specs=pl.BlockSpec((None, tile, C), lambda i, wh: (wh[1], i, 0))),
        out_shape=jax.ShapeDtypeStruct((N_CHIPS, R, C), BF16),
        compiler_params=_params(("parallel",)),
    )(where, w, *after)


BF16_ROWS = 16


def _rows_split(shape2):
    return (shape2[0] // 2) % BF16_ROWS == 0


def _half_shape(shape2):
    R, C = shape2
    return (R // 2, C) if _rows_split(shape2) else (R, C // 2)


def _half(shape2, c):
    R, C = shape2
    if _rows_split(shape2):
        return (pl.ds(c * (R // 2), R // 2), slice(None))
    return (slice(None), pl.ds(pl.multiple_of(c * (C // 2), LANE), C // 2))


def _gather_casts(tag, where, shards, after=()):
    return [_cast_into_slot(f"cast_w_{tag}{w}", where, s, after) for w, s in enumerate(shards)]


def _gather_start(tag, own, after):
    transfers = []
    for w in range(len(own)):
        def idx(x, y, c, shape2=own[w].shape[1:]):
            return (_chip(x, y), *_half(shape2, c))
        transfers += [(w, idx, w, idx, rel) for rel in ICI_RELS]
    send_sems, recv_sems, bufs, token = _push_start(f"gather_{tag}_start", own, transfers, after)
    return (tag, send_sems, recv_sems, bufs, transfers), token


def _forward_transfers(part):
    second = []
    for w, p in enumerate(part):
        for rel in ICI_RELS:
            def idx(x, y, c, shape2=p.shape[1:], rel=rel):
                return (_chip(x ^ rel[0], y ^ rel[1]), *_half(shape2, c))
            second.append((w, idx, w, idx, SIBLING))
    return second


def _gather_finish(handle, after):
    tag, send_sems, recv_sems, bufs, transfers = handle
    part = _push_wait(f"gather_{tag}_wait", send_sems, recv_sems, bufs, transfers, after)
    outs = [jax.ShapeDtypeStruct(p.shape, p.dtype) for p in part]
    second = [(("out", sb), sfn, db, dfn, rel) for sb, sfn, db, dfn, rel in _forward_transfers(part)]
    return _push(f"gather_{tag}_d2d", list(part), outs, second, aliases={w: w for w in range(len(part))})


def _gather_mid(handle, after):
    tag, send_sems, recv_sems, bufs, transfers = handle
    part = _push_wait(f"gather_{tag}_wait", send_sems, recv_sems, bufs, transfers, after)
    second = _forward_transfers(part)
    s2, r2, bufs2, token = _push_start(f"gather_{tag}_d2d_start", part, second)
    return (tag, s2, r2, bufs2, second), token


def _gather_end(handle, after):
    tag, send_sems, recv_sems, bufs, transfers = handle
    return _push_wait(f"gather_{tag}_d2d_wait", send_sems, recv_sems, bufs, transfers, after)


def _reduce_begin(tag, grads, after=()):
    n = len(grads)
    land = [lax.empty((N_CHIPS, *_half_shape(g.shape[1:])), F32) for g in grads]
    t1 = []
    for w in range(n):
        t1.append((w, lambda x, y, c, shape2=grads[w].shape[1:]: (slice(None), *_half(shape2, 1 - c)), n + w,
                   lambda x, y, c: (slice(None), slice(None), slice(None)), SIBLING))
    send_sems, recv_sems, bufs, token = _push_start(f"reduce_{tag}_d2d_start", list(grads) + land, t1, after)
    return (tag, send_sems, recv_sems, bufs, t1), token


def _reduce_start(handle, where, after, after_start=()):
    tag, send_sems, recv_sems, bufs, t1 = handle
    n = len(bufs) // 2
    res = _push_wait(f"reduce_{tag}_d2d_wait", send_sems, recv_sems, bufs, t1, after)
    grads, from_sib = res[:n], res[n:]
    pair = [_pair_sum(f"pair_sum_{tag}{w}", where, grads[w], from_sib[w]) for w in range(n)]
    land = [lax.empty((3, *_half_shape(g.shape[1:])), BF16) for g in grads]
    t3 = []
    for w in range(n):
        for r, rel in enumerate(ICI_RELS):
            t3.append((w, lambda x, y, c, rel=rel: (_chip(x ^ rel[0], y ^ rel[1]), slice(None), slice(None)), n + w,
                       lambda x, y, c, r=r: (r, slice(None), slice(None)), rel))
    send_sems, recv_sems, bufs, token = _push_start(f"reduce_{tag}_start", pair + land, t3, after_start)
    return (tag, send_sems, recv_sems, bufs, t3, list(grads), list(from_sib)), token


def _reduce_finish(handle, where, after):
    tag, send_sems, recv_sems, bufs, t3, grads, from_sib = handle
    n = len(grads)
    from_chips = _push_wait(f"reduce_{tag}_wait", send_sems, recv_sems, bufs, t3, after)[n:]
    mine = [_chip_sum(f"chip_sum_{tag}{w}", where, grads[w], from_sib[w], from_chips[w]) for w in range(n)]
    t5 = []
    for w in range(n):
        def idx(x, y, c, shape2=grads[w].shape[1:]):
            return _half(shape2, c)
        t5.append((w, idx, w, idx, SIBLING))
    send_sems, recv_sems, bufs, token = _push_start(f"reduce_{tag}_swap_start", mine, t5)
    return (tag, send_sems, recv_sems, bufs, t5), token


def _reduce_end(handle, after):
    tag, send_sems, recv_sems, bufs, t5 = handle
    return _push_wait(f"reduce_{tag}_swap_wait", send_sems, recv_sems, bufs, t5, after)


def _pair_sum(name, where, g, from_sib):
    _, hr, hc = from_sib.shape
    tile = _pick(hr, max(16, (1 << 19) // hc // 16 * 16), 16)
    nb = hr // tile
    slot = lambda k, wh: (wh[1] + 1 + k) % N_CHIPS
    if _rows_split(g.shape[1:]):
        mine = lambda k, i, wh: (slot(k, wh), wh[0] * nb + i, 0)
    else:
        mine = lambda k, i, wh: (slot(k, wh), i, wh[0])

    def body(where_ref, g_ref, s_ref, o_ref):
        o_ref[...] = (g_ref[...] + s_ref[...]).astype(BF16)

    return pl.pallas_call(
        body, name=name,
        grid_spec=pltpu.PrefetchScalarGridSpec(
            num_scalar_prefetch=1, grid=(N_CHIPS - 1, nb),
            in_specs=[pl.BlockSpec((None, tile, hc), mine),
                      pl.BlockSpec((None, tile, hc), lambda k, i, wh: (slot(k, wh), i, 0))],
            out_specs=pl.BlockSpec((None, tile, hc), lambda k, i, wh: (slot(k, wh), i, 0))),
        out_shape=jax.ShapeDtypeStruct((N_CHIPS, hr, hc), BF16),
        compiler_params=_params(("parallel", "parallel")),
    )(where, g, from_sib)


def _chip_sum(name, where, g, from_sib, from_chips):
    _, R, C = g.shape
    _, hr, hc = from_sib.shape
    tile = _pick(hr, max(16, (1 << 19) // hc // 16 * 16), 16)
    nb = hr // tile
    if _rows_split((R, C)):
        mine = lambda i, wh: (wh[0] * nb + i, 0)
    else:
        mine = lambda i, wh: (i, wh[0])

    def body(where_ref, g_ref, s_ref, r_ref, o_ref):
        acc = g_ref[...] + s_ref[...]
        for r in range(3):
            acc = acc + r_ref[r].astype(F32)
        o_ref[...] = acc

    return pl.pallas_call(
        body, name=name,
        grid_spec=pltpu.PrefetchScalarGridSpec(
            num_scalar_prefetch=1, grid=(nb,),
            in_specs=[pl.BlockSpec((None, tile, hc), lambda i, wh: (wh[1], *mine(i, wh))),
                      pl.BlockSpec((None, tile, hc), lambda i, wh: (wh[1], i, 0)),
                      pl.BlockSpec((3, tile, hc), lambda i, wh: (0, i, 0))],
            out_specs=pl.BlockSpec((tile, hc), mine)),
        out_shape=jax.ShapeDtypeStruct((R, C), F32),
        compiler_params=_params(("parallel",)),
    )(where, g, from_sib, from_chips)


def _allgather8(name, blk, after=()):
    m_per, n = blk.shape
    rels = [(0, 0, 1), (1, 0, 0), (0, 1, 0), (1, 1, 0), (1, 0, 1), (0, 1, 1), (1, 1, 1)]

    def body(x_ref, *rest):
        out_ref, send_sems, recv_sems = rest[len(after):]
        me = (lax.axis_index("x"), lax.axis_index("y"), lax.axis_index("c"))
        my_rows = out_ref.at[pl.ds((4 * me[0] + 2 * me[1] + me[2]) * m_per, m_per), :]
        out_ref[pl.ds((4 * me[0] + 2 * me[1] + me[2]) * m_per, m_per), :] = x_ref[...]
        copies = []
        for t, rel in enumerate(rels):
            cp = pltpu.make_async_remote_copy(src_ref=x_ref, dst_ref=my_rows, send_sem=send_sems.at[t],
                                              recv_sem=recv_sems.at[t], device_id=_flip(me, rel),
                                              device_id_type=MESH)
            cp.start()
            copies.append(cp)
        for cp in copies:
            cp.wait()

    return pl.pallas_call(
        body, name=name, out_shape=jax.ShapeDtypeStruct((N_DEV * m_per, n), blk.dtype),
        in_specs=[pl.BlockSpec(memory_space=pltpu.VMEM)] + [pl.BlockSpec(memory_space=pl.ANY) for _ in after],
        out_specs=pl.BlockSpec(memory_space=pltpu.VMEM),
        scratch_shapes=[pltpu.SemaphoreType.DMA((7,)), pltpu.SemaphoreType.DMA((7,))],
        compiler_params=pltpu.CompilerParams(has_side_effects=True, vmem_limit_bytes=VMEM_LIMIT),
    )(blk, *after)


def _rope_tables(seq_len, dim, reps):
    rows = seq_len // GRID_W
    t = np.arange(seq_len)
    row, col = (t // GRID_W).astype(np.float32), (t % GRID_W).astype(np.float32)
    half = dim // 2
    inv = jnp.asarray(ROPE_THETA, F32) ** (-jnp.arange(0, half, 2, dtype=F32) / half)
    ang_r = jnp.asarray(row)[:, None] * inv[None, :]
    ang_c = jnp.asarray(col)[:, None] * inv[None, :]
    ang = jnp.concatenate([ang_r, ang_r, ang_c, ang_c], axis=-1)
    cos, sin = jnp.cos(ang), jnp.sin(ang)
    low = (np.arange(dim) % (dim // 2)) < (dim // 4)
    s_lo = jnp.where(jnp.asarray(low)[None, :], -sin, 0.0)
    s_hi = jnp.where(jnp.asarray(low)[None, :], 0.0, sin)
    del rows
    return tuple(jnp.tile(a, (1, reps)) for a in (cos, s_lo, s_hi))


def _make_rope(shift):
    def up(v):
        return pltpu.roll(v, LANE - shift, 1)

    def down(v):
        return pltpu.roll(v, shift, 1)

    @jax.custom_vjp
    def rope(v, cos, s_lo, s_hi):
        return v * cos + up(v) * s_lo + down(v) * s_hi

    def fwd(v, cos, s_lo, s_hi):
        return rope(v, cos, s_lo, s_hi), (cos, s_lo, s_hi)

    def bwd(res, dy):
        cos, s_lo, s_hi = res
        return dy * cos + down(dy * s_lo) + up(dy * s_hi), jnp.zeros_like(cos), jnp.zeros_like(cos), jnp.zeros_like(cos)

    rope.defvjp(fwd, bwd)
    return rope


_rope_a = _make_rope(HEAD_DIM // 4)
_rope_b = _make_rope(QK_ROPE // 4)


def _prep(segs, tabs, g_q, g_k, g_ckv):
    qa, ka, va, qn, qp, ckv, kpe = segs
    ta, tb = tabs
    q_a = jnp.concatenate([_rope_a(_rn(s) * g_q, *ta) for s in qa], axis=1)
    k_a = jnp.concatenate([_rope_a(_rn(s) * g_k, *ta) for s in ka], axis=1)
    cat = []
    for h in range(HB):
        cat += [qn[h], _rope_b(qp[h], *tb)]
    q_cat = jnp.concatenate(cat, axis=1)
    return q_a, k_a, va, q_cat, _rn(ckv) * g_ckv, _rope_b(kpe, *tb)


def _split_proj(proj):
    qa = [proj[:, O_QA + h * LANE:O_QA + (h + 1) * LANE] for h in range(HA)]
    ka = [proj[:, O_KA + h * LANE:O_KA + (h + 1) * LANE] for h in range(HKV)]
    va = proj[:, O_VA:O_QN]
    qn = [proj[:, O_QN + h * LANE:O_QN + (h + 1) * LANE] for h in range(HB)]
    qp = [proj[:, O_QP + h * LANE:O_QP + (h + 1) * LANE] for h in range(HB)]
    return qa, ka, va, qn, qp, proj[:, O_CKV:O_KPE], proj[:, O_KPE:W_INP]


def _permute_w_in(nat):
    K = nat.shape[1]
    qb = nat[W_QA + 2 * W_KA:W_QA + 2 * W_KA + W_QB].reshape(HB, QK_B, K)
    qp = jnp.concatenate([qb[:, QK_NOPE:], jnp.zeros((HB, LANE - QK_ROPE, K), nat.dtype)], axis=1)
    tail = nat[W_QA + 2 * W_KA + W_QB:]
    return jnp.concatenate([nat[:O_QN], qb[:, :QK_NOPE].reshape(HB * QK_NOPE, K), qp.reshape(HB * LANE, K),
                            tail, jnp.zeros((LANE - QK_ROPE, K), nat.dtype)], axis=0)


def _unpermute_pieces():
    pieces = [(0, 0, O_QN)]
    for h in range(HB):
        pieces.append((O_QN + h * QK_NOPE, O_QN + h * QK_B, QK_NOPE))
        pieces.append((O_QP + h * LANE, O_QN + h * QK_B + QK_NOPE, QK_ROPE))
    pieces.append((O_CKV, O_QN + W_QB, KV_RANK))
    pieces.append((O_KPE, O_QN + W_QB + KV_RANK, QK_ROPE))
    return pieces


def kernel(x, c, w_ada, b_ada, g_pre_attn, w_in, g_q_a, g_k_a, g_ckv, w_kv_b, g_out_a, g_out_b, w_out, g_post_attn, g_pre_mlp, w_mlp_in, w_mlp_out, g_post_mlp, loss_target, m_w_ada, m_b_ada, m_g_pre_attn, m_w_in, m_g_q_a, m_g_k_a, m_g_ckv, m_w_kv_b, m_g_out_a, m_g_out_b, m_w_out, m_g_post_attn, m_g_pre_mlp, m_w_mlp_in, m_w_mlp_out, m_g_post_mlp, v_w_ada, v_b_ada, v_g_pre_attn, v_w_in, v_g_q_a, v_g_k_a, v_g_ckv, v_w_kv_b, v_g_out_a, v_g_out_b, v_w_out, v_g_post_attn, v_g_pre_mlp, v_w_mlp_in, v_w_mlp_out, v_g_post_mlp):
    S, D = x.shape[1], x.shape[2]
    x2d, tgt = x[0], loss_target[0]
    ix, iy, ic = lax.axis_index("x"), lax.axis_index("y"), lax.axis_index("c")
    j_me = _chip(ix, iy)
    e_me = 4 * ix + 2 * iy + ic
    T = _pick(S, 512, 8)
    TB = _pick(S, 256, 8)

    where = jnp.stack([ic, j_me]).astype(jnp.int32)
    DFF = w_mlp_out.shape[1] * N_CHIPS

    c_all = _allgather8("gather_c", c.reshape(8, D // 8)).reshape(N_DEV, D)
    w_in_t = jnp.swapaxes(w_in[0], 0, 1)
    h_qkv, tok_a = _gather_start("qkv", _gather_casts("qkv", where, [w_in_t, w_kv_b[0]]), (c_all,))
    later = [_gather_casts(tag, where, [w[0]], (tok_a,))
             for tag, w in (("out", w_out), ("up", w_mlp_in), ("down", w_mlp_out))]

    def silu_fn(cc):
        return (cc * (1.0 / (1.0 + jnp.exp(-cc))),), ()

    (c_act,), _ = _rowwise("silu_c", silu_fn, [c_all], [], [(D, BF16)], [], N_DEV)
    ca_pad = jnp.concatenate([c_act, jnp.zeros_like(c_act)], axis=0)
    n_mod = w_ada.shape[2]
    b_cols = lax.dynamic_slice(b_ada, (0, j_me * n_mod), (1, n_mod))
    (mod_part,) = _mm_nn("mod_mm", ca_pad, w_ada, [F32], epilogue=lambda acc, b: (acc + b,),
                         extras=(jnp.broadcast_to(b_cols, (16, n_mod)),))
    mod_all = _allgather8("gather_mod", mod_part[:8] + tok_a[0, 0],
                          after=[c_[0] for c_ in later]).reshape(N_DEV, 8, n_mod)
    mod = jnp.concatenate([lax.dynamic_slice(mod_all, (2 * j, e_me, 0), (1, 1, n_mod))[0] for j in range(N_CHIPS)],
                          axis=1)

    m_qkv, tok_q = _gather_mid(h_qkv, mod_all)
    h_out, tok_b = _gather_start("out", later[0], (tok_q,))
    h_up, tok_c = _gather_start("up", later[1], (tok_b,))
    h_down, tok_d = _gather_start("down", later[2], (tok_c,))
    mod = mod + tok_d[0, 0]
    sh_a, sc_a, gt_a, sh_m, sc_m, gt_m = [mod[:, i * D:(i + 1) * D] for i in range(6)]

    tabs_a = _rope_tables(S, HEAD_DIM, 1)
    tabs_b = _rope_tables(S, QK_ROPE, LANE // QK_ROPE)

    def pre_attn_fn(xt, g, sc, sh):
        return (_rn(xt) * g * (1.0 + sc) + sh,), ()

    (h_b,), _ = _rowwise("pre_attn", pre_attn_fn, [x2d], [g_pre_attn, sc_a, sh_a], [(D, BF16)], [], T)
    g_in, g_kvb = _gather_end(m_qkv, h_b)
    wpt = _permute_w_in(g_in.reshape(W_IN, D))[None]
    (proj,) = _mm_nt("proj_mm", h_b, wpt, [F32])

    def prep_fn(pt, ca_, sla, sha, cb_, slb, shb, gq, gk, gc):
        return _prep(_split_proj(pt), ((ca_, sla, sha), (cb_, slb, shb)), gq, gk, gc), ()

    (qa_b, ka_b, va_b, qcat_b, ckvn_b, kpe_b), _ = _rowwise(
        "prep", prep_fn, [proj, *tabs_a, *tabs_b], [g_q_a, g_k_a, g_ckv],
        [(W_QA, BF16), (W_KA, BF16), (W_KA, BF16), (2 * HB * LANE, BF16), (KV_RANK, BF16), (LANE, BF16)], [], T)
    kcat_b, vb_b = _kv_mm("kv_mm", ckvn_b, g_kvb, kpe_b)

    TQ = _pick(S, 512, 16)
    sc_a_ = 1.0 / math.sqrt(HEAD_DIM)
    sc_b_ = 1.0 / math.sqrt(QK_B)
    o_a, lse_a = _attn_fwd("attn_a_fwd", qa_b, ka_b, va_b, HA, HA // HKV, HEAD_DIM, HEAD_DIM, sc_a_, TQ)
    m_out, tok_m = _gather_mid(h_out, o_a)
    o_b, lse_b = _attn_fwd("attn_b_fwd", qcat_b, kcat_b, vb_b, HB, 1, 2 * LANE, V_DIM, sc_b_, TQ, after=(tok_m,))
    m_up, tok_m = _gather_mid(h_up, o_b)

    def mix_fn(oa, ob, ga, gb):
        return jnp.concatenate([_rn(oa) * ga, _rn(ob) * gb], axis=1)

    (g_wout,) = _gather_end(m_out, o_b)
    w_out_f = g_wout.reshape(1, N_CHIPS * g_wout.shape[1], D)
    on_b, o2 = _rows_mm("out_mm", mix_fn, [o_a, o_b], [g_out_a, g_out_b], w_out_f, after=(tok_m,))

    def mid(xt, o2t, gta, gpa, gpm, scm, shm):
        x1 = xt + gta * (_rn(o2t) * gpa)
        return x1, _rn(x1) * gpm * (1.0 + scm) + shm

    def mid_fn(*a):
        return mid(*a), ()

    mid_vecs = [gt_a, g_post_attn, g_pre_mlp, sc_m, sh_m]
    (x1, h2_b), _ = _rowwise("mid", mid_fn, [x2d, o2], mid_vecs, [(D, F32), (D, BF16)], [], T)
    (g_mlp_in,) = _gather_end(m_up, h2_b)
    relu_b, a_b = _mm_nn("mlp_in_mm", h2_b, g_mlp_in, [BF16, BF16],
                         epilogue=lambda acc: (jnp.maximum(acc, 0.0), jnp.square(jnp.maximum(acc, 0.0))))
    (g_mlp_out,) = _gather_finish(h_down, a_b)
    w_mlp_out_f = g_mlp_out.reshape(1, DFF, D)
    (y,) = _mm_nn("mlp_out_mm", a_b, w_mlp_out_f, [F32])

    def last_fn(yt, x1t, tt, gtm, gpo):
        n, r = _rn_parts(yt)
        err = x1t + gtm * (n * gpo) - tt
        dx2 = err * (1.0 / D)
        loss = 0.5 * jnp.sum(jnp.mean(err * err, axis=-1))
        s = _rows(dx2 * n)
        return (_rn_bwd(n, r, dx2 * (gtm * gpo)), dx2), (jnp.full((1, LANE), loss, F32), gpo * s, gtm * s)

    (dy_b, dx2), (loss_v, d_gt_m, d_g_post_mlp) = _rowwise(
        "loss_bwd", last_fn, [y, x1, tgt], [gt_m, g_post_mlp], [(D, BF16), (D, F32)], [LANE, D, D], TB)

    gw_mlp_out = _mm_tn("gw_mlp_out_mm", a_b, dy_b, 1).reshape(N_CHIPS, DFF // N_CHIPS, D)
    b_down, tok = _reduce_begin("down", [gw_mlp_out])
    (du_b,) = _mm_nt("d_mlp_out_mm", dy_b, w_mlp_out_f, [BF16],
                     epilogue=lambda acc, rt: (acc * (2.0 * rt.astype(F32)),), extras=(relu_b,), after=(tok,))
    r_down, tok = _reduce_start(b_down, where, du_b)
    gw_mlp_in = _mm_tn("gw_mlp_in_mm", h2_b, du_b, N_CHIPS, after=(tok,))
    b_up, tok = _reduce_begin("up", [gw_mlp_in])
    (dh2,) = _mm_nt("d_mlp_in_mm", du_b, g_mlp_in, [F32], after=(tok,))
    r_up, tok_up = _reduce_start(b_up, where, dh2)

    def mid_bwd_fn(xt, o2t, dh2t, dx2t, gta, gpa, gpm, scm, shm):
        n2, r2 = _rn_parts(o2t)
        n1, r1 = _rn_parts(xt + gta * (n2 * gpa))
        dx1 = dx2t + _rn_bwd(n1, r1, dh2t * (gpm * (1.0 + scm)))
        s1, s2 = _rows(dh2t * n1), _rows(dx1 * n2)
        do2 = _rn_bwd(n2, r2, dx1 * (gta * gpa))
        return (dx1, do2), (gpa * s2, gta * s2, (1.0 + scm) * s1, gpm * s1, _rows(dh2t))

    (dx1, do2_b), (d_gt_a, d_g_post_attn, d_g_pre_mlp, d_sc_m, d_sh_m) = _rowwise(
        "mid_bwd", mid_bwd_fn, [x2d, o2, dh2, dx2], mid_vecs, [(D, F32), (D, BF16)], [D] * 5, TB, after=(tok_up,))

    (d_on,) = _mm_nt("d_out_mm", do2_b, w_out_f, [F32])
    gw_out = _mm_tn("gw_out_mm", on_b, do2_b, 1).reshape(N_CHIPS, w_out.shape[1], D)

    def mix_bwd_fn(oa, ob, dont, ga, gb):
        na, ra = _rn_parts(oa)
        nb, rb = _rn_parts(ob)
        da, db = dont[:, :W_QA], dont[:, W_QA:]
        return (_rn_bwd(na, ra, da * ga), _rn_bwd(nb, rb, db * gb)), (_rows(da * na), _rows(db * nb))

    (doa_b, dob_b), (d_g_out_a, d_g_out_b) = _rowwise(
        "mix_bwd", mix_bwd_fn, [o_a, o_b, d_on], [g_out_a, g_out_b], [(W_QA, BF16), (HB * V_DIM, BF16)],
        [W_QA, HB * V_DIM], TB)

    dqa, dka, dva = _attn_bwd("attn_a_bwd", qa_b, ka_b, va_b, doa_b, o_a, lse_a, HA, HA // HKV, HEAD_DIM, HEAD_DIM,
                              sc_a_, TQ)
    dqcat, dkcat, dvb = _attn_bwd("attn_b_bwd", qcat_b, kcat_b, vb_b, dob_b, o_b, lse_b, HB, 1, 2 * LANE, V_DIM,
                                  sc_b_, TQ)

    d_ckvn, dkv_b, dkpe = _kv_mm_bwd("d_kv_mm", dkcat, dvb, g_kvb)
    gw_kvb = _mm_tn("gw_kv_mm", ckvn_b, dkv_b, N_CHIPS)
    b_kvb, tok_kvb = _reduce_begin("okv", [gw_out, gw_kvb])

    def prep_bwd_fn(pt, dqat, dkat, dvat, dqct, dckt, dkpt, ca_, sla, sha, cb_, slb, shb, gq, gk, gc):
        tabs = ((ca_, sla, sha), (cb_, slb, shb))
        _, vjp = jax.vjp(lambda sg, a1, a2, a3: _prep(sg, tabs, a1, a2, a3), _split_proj(pt), gq, gk, gc)
        (dqa_s, dka_s, dva_s, dqn_s, dqp_s, dck_s, dkp_s), dgq, dgk, dgc = vjp((dqat, dkat, dvat, dqct, dckt, dkpt))
        dproj = jnp.concatenate([*dqa_s, *dka_s, dva_s, *dqn_s, *dqp_s, dck_s, dkp_s], axis=1)
        return (dproj,), (dgq, dgk, dgc)

    (dproj_b,), (d_g_q_a, d_g_k_a, d_g_ckv) = _rowwise(
        "prep_bwd", prep_bwd_fn, [proj, dqa, dka, dva, dqcat, d_ckvn, dkpe, *tabs_a, *tabs_b],
        [g_q_a, g_k_a, g_ckv], [(W_INP, BF16)], [HEAD_DIM, HEAD_DIM, KV_RANK], TB, after=(tok_kvb,))

    r_kvb, tok = _reduce_start(b_kvb, where, dproj_b)
    (dh,) = _mm_nn("d_proj_mm", dproj_b, wpt, [F32], after=(tok,))
    gw_in = _mm_tn_rows("gw_in_mm", dproj_b, h_b, W_IN, _unpermute_pieces()).reshape(N_CHIPS, W_IN // N_CHIPS, D)
    b_in, tok = _reduce_begin("in", [gw_in])

    def pre_attn_bwd_fn(xt, dht, dx1t, g, sc, sh):
        n, r = _rn_parts(xt)
        s = _rows(dht * n)
        return (dx1t + _rn_bwd(n, r, dht * (g * (1.0 + sc))),), ((1.0 + sc) * s, g * s, _rows(dht))

    (grad_x,), (d_g_pre_attn, d_sc_a, d_sh_a) = _rowwise(
        "pre_attn_bwd", pre_attn_bwd_fn, [x2d, dh, dx1], [g_pre_attn, sc_a, sh_a], [(D, F32)], [D] * 3, TB,
        after=(tok,))

    small_names = ["b_ada", "g_pre_attn", "g_q_a", "g_k_a", "g_ckv", "g_out_a", "g_out_b", "g_post_attn",
                   "g_pre_mlp", "g_post_mlp"]
    small_local = jnp.concatenate([d_sh_a, d_sc_a, d_gt_a, d_sh_m, d_sc_m, d_gt_m, d_g_pre_attn, d_g_q_a, d_g_k_a,
                                   d_g_ckv, d_g_out_a, d_g_out_b, d_g_post_attn, d_g_pre_mlp, d_g_post_mlp,
                                   loss_v[:, :8]], axis=1)
    n_sent = small_local.shape[1]
    n_small = n_sent - 8
    small_all = _allgather8("gather_small", small_local.reshape(8, n_sent // 8)).reshape(N_DEV, n_sent)

    def sum8_fn(t):
        return (), (jnp.sum(t, axis=0, keepdims=True),)

    r_in, tok_in = _reduce_start(b_in, where, grad_x, (small_all,))
    _, (sent_sum,) = _rowwise("sum_small", sum8_fn, [small_all], [], [], [n_sent], N_DEV, after=(tok_in,))
    small_sum, loss = sent_sum[:, :n_small], sent_sum[0, n_small]
    dmod_cols = lax.dynamic_slice(small_all, (0, j_me * n_mod), (N_DEV, n_mod))
    dmod_pad = jnp.concatenate([dmod_cols, jnp.zeros_like(dmod_cols)], axis=0).astype(BF16)
    gw_ada = _mm_tn("gw_ada_mm", ca_pad, dmod_pad, 1)[0]

    weights = dict(w_ada=w_ada, b_ada=b_ada, g_pre_attn=g_pre_attn, w_in=w_in, g_q_a=g_q_a, g_k_a=g_k_a, g_ckv=g_ckv,
                   w_kv_b=w_kv_b, g_out_a=g_out_a, g_out_b=g_out_b, w_out=w_out, g_post_attn=g_post_attn,
                   g_pre_mlp=g_pre_mlp, w_mlp_in=w_mlp_in, w_mlp_out=w_mlp_out, g_post_mlp=g_post_mlp)
    ms = dict(w_ada=m_w_ada, b_ada=m_b_ada, g_pre_attn=m_g_pre_attn, w_in=m_w_in, g_q_a=m_g_q_a, g_k_a=m_g_k_a,
              g_ckv=m_g_ckv, w_kv_b=m_w_kv_b, g_out_a=m_g_out_a, g_out_b=m_g_out_b, w_out=m_w_out,
              g_post_attn=m_g_post_attn, g_pre_mlp=m_g_pre_mlp, w_mlp_in=m_w_mlp_in, w_mlp_out=m_w_mlp_out,
              g_post_mlp=m_g_post_mlp)
    vs = dict(w_ada=v_w_ada, b_ada=v_b_ada, g_pre_attn=v_g_pre_attn, w_in=v_w_in, g_q_a=v_g_q_a, g_k_a=v_g_k_a,
              g_ckv=v_g_ckv, w_kv_b=v_w_kv_b, g_out_a=v_g_out_a, g_out_b=v_g_out_b, w_out=v_w_out,
              g_post_attn=v_g_post_attn, g_pre_mlp=v_g_pre_mlp, w_mlp_in=v_w_mlp_in, w_mlp_out=v_w_mlp_out,
              g_post_mlp=v_g_post_mlp)
    order = list(weights)
    grads, deltas, new_m, new_v = {}, {}, {}, {}

    def update(name, g, after=()):
        view = (lambda t: jnp.swapaxes(t, 0, 1)) if name == "w_in" else (lambda t: t)
        d_, m_, v_, *g_ = _adamw("adamw_" + name, view(weights[name][0]), g, view(ms[name][0]), view(vs[name][0]),
                                 after, rewrite_g=name != "w_ada")
        grads[name], deltas[name], new_m[name], new_v[name] = (view(t)[None] for t in ((g_ or [g])[0], d_, m_, v_))
        return d_

    chain = ((("w_mlp_out",), r_down), (("w_mlp_in",), r_up), (("w_out", "w_kv_b"), r_kvb), (("w_in",), r_in))
    swap, tok = _reduce_finish(chain[0][1], where, small_sum)
    last = update("w_ada", gw_ada, (tok,))
    for k, (names, _) in enumerate(chain):
        gs = _reduce_end(swap, last)
        if k + 1 < len(chain):
            swap, tok = _reduce_finish(chain[k + 1][1], where, gs[0])
        for name, g in zip(names, gs):
            last = update(name, g, (tok,))

    cat = lambda d: jnp.concatenate([d[n] for n in small_names], axis=1)
    d_, m_, v_ = _adamw("adamw_small", cat(weights), small_sum, cat(ms), cat(vs))
    off = 0
    for n in small_names:
        w_ = weights[n].shape[1]
        grads[n], deltas[n], new_m[n], new_v[n] = (a[:, off:off + w_] for a in (small_sum, d_, m_, v_))
        off += w_

    return (loss, grad_x[None], *[grads[n] for n in order], *[deltas[n] for n in order],
            *[new_m[n] for n in order], *[new_v[n] for n in order])
```

```python
import functools
import math

import numpy as np
import jax
import jax.numpy as jnp
from jax import lax
from jax.experimental import pallas as pl
from jax.experimental.pallas import tpu as pltpu

F32 = jnp.float32
BF16 = jnp.bfloat16
MESH = pl.DeviceIdType.MESH

EPS = 1e-6
GRID_W = 64
ROPE_THETA = 10000.0
HEAD_DIM = 128
HA = 8
HKV = 2
HB = 8
QK_NOPE = 128
QK_ROPE = 64
V_DIM = 128
KV_RANK = 512
QK_B = QK_NOPE + QK_ROPE
W_QA = HA * HEAD_DIM
W_KA = HKV * HEAD_DIM
W_QB = HB * QK_B
W_IN = W_QA + 2 * W_KA + W_QB + KV_RANK + QK_ROPE
N_CHIPS = 4
N_DEV = 8

LANE = 128
O_QA = 0
O_KA = O_QA + W_QA
O_VA = O_KA + W_KA
O_QN = O_VA + W_KA
O_QP = O_QN + HB * QK_NOPE
O_CKV = O_QP + HB * LANE
O_KPE = O_CKV + KV_RANK
W_INP = O_KPE + LANE

ADAM_LR = 0.001
ADAM_B1 = 0.9
ADAM_B2 = 0.999
ADAM_EPS = 1e-08
ADAM_WD = 0.01
ADAM_STEP = 10

VMEM_LIMIT = 56 * 1024 * 1024
HEADS_PER_STEP = 2


def _params(sem):
    return pltpu.CompilerParams(dimension_semantics=sem, vmem_limit_bytes=VMEM_LIMIT)


def _pick(dim, target, unit):
    best = None
    t = unit
    while t <= min(dim, target):
        if dim % t == 0:
            best = t
        t += unit
    return dim if best is None else best


def _rn(x):
    return x * lax.rsqrt(jnp.mean(x * x, axis=-1, keepdims=True) + EPS)


def _rn_parts(x):
    r = lax.rsqrt(jnp.mean(x * x, axis=-1, keepdims=True) + EPS)
    return x * r, r


def _rn_bwd(n, r, dn):
    return r * (dn - n * jnp.mean(dn * n, axis=-1, keepdims=True))


def _rows(t):
    return jnp.sum(t, axis=0, keepdims=True)


def _rowwise(name, fn, rows, vecs, row_outs, vec_outs, tile, after=()):
    S = rows[0].shape[0]
    assert S % tile == 0
    n_r, n_v, n_ro, n_vo, n_a = len(rows), len(vecs), len(row_outs), len(vec_outs), len(after)

    def body(*refs):
        r_in = refs[:n_r]
        v_in = refs[n_r:n_r + n_v]
        r_out = refs[n_r + n_v + n_a:n_r + n_v + n_a + n_ro]
        v_out = refs[n_r + n_v + n_a + n_ro:]
        ro, vo = fn(*[r[...] for r in r_in], *[v[...] for v in v_in])
        for ref, val in zip(r_out, ro):
            ref[...] = val.astype(ref.dtype)
        if n_vo:
            @pl.when(pl.program_id(0) == 0)
            def _():
                for ref in v_out:
                    ref[...] = jnp.zeros(ref.shape, ref.dtype)
            for ref, val in zip(v_out, vo):
                ref[...] += val

    in_specs = [pl.BlockSpec((tile, a.shape[1]), lambda i: (i, 0)) for a in rows]
    in_specs += [pl.BlockSpec(a.shape, lambda i: (0, 0)) for a in vecs]
    in_specs += [pl.BlockSpec(memory_space=pl.ANY) for _ in after]
    out_specs = [pl.BlockSpec((tile, w), lambda i: (i, 0)) for w, _ in row_outs]
    out_specs += [pl.BlockSpec((1, w), lambda i: (0, 0)) for w in vec_outs]
    out_shape = [jax.ShapeDtypeStruct((S, w), dt) for w, dt in row_outs]
    out_shape += [jax.ShapeDtypeStruct((1, w), F32) for w in vec_outs]
    res = pl.pallas_call(
        body, name=name, grid=(S // tile,), in_specs=in_specs, out_specs=out_specs, out_shape=out_shape,
        compiler_params=_params(("arbitrary",)),
    )(*rows, *vecs, *after)
    return res[:n_ro], res[n_ro:]


CONTRACT_TILE = 2048
MM_TILE_BYTES = 40 * 1024 * 1024


def _rows_that_fit(M, tile_bytes, most=1024):
    t = _pick(M, most, 16)
    while t % 32 == 0 and tile_bytes(t) > MM_TILE_BYTES:
        t //= 2
    return t


def _accumulate(acc, part, step, n_steps, finish):
    if n_steps == 1:
        finish(part)
        return

    @pl.when(step == 0)
    def _():
        acc[...] = part

    @pl.when((step > 0) & (step < n_steps - 1))
    def _():
        acc[...] += part

    @pl.when(step == n_steps - 1)
    def _():
        finish(acc[...] + part)


def _mm_nn(name, a, b3, out_dtypes, epilogue=None, extras=(), after=()):
    M, K = a.shape
    nB, K2, Nsh = b3.shape
    assert K == K2
    N = nB * Nsh
    tn, tk = _pick(Nsh, 512, LANE), _pick(K, CONTRACT_TILE, LANE)
    out_bytes = sum(jnp.dtype(dt).itemsize for dt in out_dtypes) + sum(e.dtype.itemsize for e in extras)
    tm = _rows_that_fit(M, lambda t: 2 * (t * tk * a.dtype.itemsize + tk * tn * b3.dtype.itemsize)
                        + t * tn * (4 * (K > tk) + 2 * out_bytes), most=2048)
    nps, nk, ne, no, na = Nsh // tn, K // tk, len(extras), len(out_dtypes), len(after)

    def body(a_ref, b_ref, *rest):
        e_refs, o_refs = rest[:ne], rest[ne + na:ne + na + no]

        def finish(total):
            vals = (total,) if epilogue is None else epilogue(total, *[e[...] for e in e_refs])
            for ref, val in zip(o_refs, vals):
                ref[...] = val.astype(ref.dtype)

        part = jnp.dot(a_ref[...].astype(BF16), b_ref[...].astype(BF16), preferred_element_type=F32)
        _accumulate(rest[-1], part, pl.program_id(2), nk, finish)

    in_specs = [pl.BlockSpec((tm, tk), lambda i, j, k: (i, k)),
                pl.BlockSpec((None, tk, tn), lambda i, j, k: (j // nps, k, j % nps))]
    in_specs += [pl.BlockSpec((tm, tn), lambda i, j, k: (i, j)) for _ in extras]
    in_specs += [pl.BlockSpec(memory_space=pl.ANY) for _ in after]
    return pl.pallas_call(
        body, name=name, grid=(M // tm, N // tn, nk), in_specs=in_specs,
        out_specs=[pl.BlockSpec((tm, tn), lambda i, j, k: (i, j)) for _ in out_dtypes],
        out_shape=[jax.ShapeDtypeStruct((M, N), dt) for dt in out_dtypes],
        scratch_shapes=[pltpu.VMEM((tm, tn), F32)] if nk > 1 else [],
        compiler_params=_params(("parallel", "parallel", "arbitrary")),
    )(a, b3, *extras, *after)


def _mm_nt(name, a, b3, out_dtypes, epilogue=None, extras=(), after=()):
    M, N = a.shape
    nB, K, Nsh = b3.shape
    assert N == nB * Nsh
    to, tn = _pick(K, 1536, LANE), _pick(Nsh, CONTRACT_TILE, LANE)
    out_bytes = sum(jnp.dtype(dt).itemsize for dt in out_dtypes) + sum(e.dtype.itemsize for e in extras)
    tm = _rows_that_fit(M, lambda t: 2 * (t * tn * a.dtype.itemsize + to * tn * b3.dtype.itemsize)
                        + t * to * (4 * (N > tn) + 2 * out_bytes))
    nps, nn, ne, no, na = Nsh // tn, N // tn, len(extras), len(out_dtypes), len(after)

    def body(a_ref, b_ref, *rest):
        e_refs, o_refs = rest[:ne], rest[ne + na:ne + na + no]

        def finish(total):
            vals = (total,) if epilogue is None else epilogue(total, *[e[...] for e in e_refs])
            for ref, val in zip(o_refs, vals):
                ref[...] = val.astype(ref.dtype)

        part = lax.dot_general(a_ref[...].astype(BF16), b_ref[...].astype(BF16),
                               (((1,), (1,)), ((), ())), preferred_element_type=F32)
        _accumulate(rest[-1], part, pl.program_id(2), nn, finish)

    in_specs = [pl.BlockSpec((tm, tn), lambda i, o, n: (i, n)),
                pl.BlockSpec((None, to, tn), lambda i, o, n: (n // nps, o, n % nps))]
    in_specs += [pl.BlockSpec((tm, to), lambda i, o, n: (i, o)) for _ in extras]
    in_specs += [pl.BlockSpec(memory_space=pl.ANY) for _ in after]
    return pl.pallas_call(
        body, name=name, grid=(M // tm, K // to, nn), in_specs=in_specs,
        out_specs=[pl.BlockSpec((tm, to), lambda i, o, n: (i, o)) for _ in out_dtypes],
        out_shape=[jax.ShapeDtypeStruct((M, K), dt) for dt in out_dtypes],
        scratch_shapes=[pltpu.VMEM((tm, to), F32)] if nn > 1 else [],
        compiler_params=_params(("parallel", "parallel", "arbitrary")),
    )(a, b3, *extras, *after)


def _mm_tn(name, a, g, nB, after=()):
    M, K = a.shape
    M2, N = g.shape
    assert M == M2 and N % nB == 0
    Nsh = N // nB
    tk, tn, tm = _pick(K, 1536, LANE), _pick(Nsh, 1024, LANE), _pick(M, CONTRACT_TILE, 16)
    nps, nm = Nsh // tn, M // tm

    def body(a_ref, g_ref, *rest):
        o_ref = rest[len(after)]

        def finish(total):
            o_ref[...] = total

        part = lax.dot_general(a_ref[...].astype(BF16), g_ref[...].astype(BF16),
                               (((0,), (0,)), ((), ())), preferred_element_type=F32)
        _accumulate(rest[-1], part, pl.program_id(2), nm, finish)

    return pl.pallas_call(
        body, name=name, grid=(K // tk, N // tn, nm),
        in_specs=[pl.BlockSpec((tm, tk), lambda k, n, m: (m, k)),
                  pl.BlockSpec((tm, tn), lambda k, n, m: (m, n))] + [pl.BlockSpec(memory_space=pl.ANY) for _ in after],
        out_specs=pl.BlockSpec((None, tk, tn), lambda k, n, m: (n // nps, k, n % nps)),
        out_shape=jax.ShapeDtypeStruct((nB, K, Nsh), F32),
        scratch_shapes=[pltpu.VMEM((tk, tn), F32)] if nm > 1 else [],
        compiler_params=_params(("parallel", "parallel", "arbitrary")),
    )(a, g, *after)


def _rows_mm(name, fn, rows, vecs, b3, after=()):
    S = rows[0].shape[0]
    _, K, N = b3.shape
    tm, tn = _pick(S, 1024, 16), _pick(N, 512, LANE)
    nr, nv, na = len(rows), len(vecs), len(after)

    def body(*refs):
        b_ref = refs[nr + nv]
        a_out, o_ref, a_scr = refs[nr + nv + 1 + na:]

        @pl.when(pl.program_id(1) == 0)
        def _():
            a = fn(*[r[...] for r in refs[:nr]], *[v[...] for v in refs[nr:nr + nv]]).astype(BF16)
            a_scr[...] = a
            a_out[...] = a

        o_ref[...] = jnp.dot(a_scr[...], b_ref[...], preferred_element_type=F32)

    in_specs = [pl.BlockSpec((tm, r.shape[1]), lambda i, j: (i, 0)) for r in rows]
    in_specs += [pl.BlockSpec(v.shape, lambda i, j: (0, 0)) for v in vecs]
    in_specs += [pl.BlockSpec((None, K, tn), lambda i, j: (0, 0, j))]
    in_specs += [pl.BlockSpec(memory_space=pl.ANY) for _ in after]
    return pl.pallas_call(
        body, name=name, grid=(S // tm, N // tn), in_specs=in_specs,
        out_specs=[pl.BlockSpec((tm, K), lambda i, j: (i, 0)), pl.BlockSpec((tm, tn), lambda i, j: (i, j))],
        out_shape=[jax.ShapeDtypeStruct((S, K), BF16), jax.ShapeDtypeStruct((S, N), F32)],
        scratch_shapes=[pltpu.VMEM((tm, K), BF16)],
        compiler_params=_params(("parallel", "arbitrary")),
    )(*rows, *vecs, b3, *after)


def _kv_mm(name, ckvn, w3, kpe):
    S, K = ckvn.shape
    nB, _, Nsh = w3.shape
    heads = Nsh // (2 * LANE)
    tm = _pick(S, 1024, 16)

    def body(a_ref, b_ref, kpe_ref, kc_ref, v_ref):
        acc = jnp.dot(a_ref[...], b_ref[...], preferred_element_type=F32)
        for h in range(heads):
            kc_ref[:, 2 * h * LANE:(2 * h + 1) * LANE] = acc[:, 2 * h * LANE:(2 * h + 1) * LANE].astype(BF16)
            kc_ref[:, (2 * h + 1) * LANE:(2 * h + 2) * LANE] = kpe_ref[...]
            v_ref[:, h * LANE:(h + 1) * LANE] = acc[:, (2 * h + 1) * LANE:(2 * h + 2) * LANE].astype(BF16)

    return pl.pallas_call(
        body, name=name, grid=(S // tm, nB),
        in_specs=[pl.BlockSpec((tm, K), lambda i, j: (i, 0)), pl.BlockSpec((None, K, Nsh), lambda i, j: (j, 0, 0)),
                  pl.BlockSpec((tm, LANE), lambda i, j: (i, 0))],
        out_specs=[pl.BlockSpec((tm, Nsh), lambda i, j: (i, j)), pl.BlockSpec((tm, Nsh // 2), lambda i, j: (i, j))],
        out_shape=[jax.ShapeDtypeStruct((S, nB * Nsh), BF16), jax.ShapeDtypeStruct((S, nB * Nsh // 2), BF16)],
        compiler_params=_params(("parallel", "arbitrary")),
    )(ckvn, w3, kpe)


def _kv_mm_bwd(name, dkc, dv, w3):
    S = dkc.shape[0]
    nB, K, Nsh = w3.shape
    heads = Nsh // (2 * LANE)
    tm = _pick(S, 1024, 16)

    def body(dkc_ref, dv_ref, b_ref, dck_ref, dkv_ref, dkpe_ref):
        j = pl.program_id(1)
        pieces, pe = [], None
        for h in range(heads):
            pieces += [dkc_ref[:, 2 * h * LANE:(2 * h + 1) * LANE], dv_ref[:, h * LANE:(h + 1) * LANE]]
            piece = dkc_ref[:, (2 * h + 1) * LANE:(2 * h + 2) * LANE]
            pe = piece if pe is None else pe + piece
        dkv = jnp.concatenate(pieces, axis=1).astype(BF16)
        dkv_ref[...] = dkv
        part = lax.dot_general(dkv, b_ref[...], (((1,), (1,)), ((), ())), preferred_element_type=F32)

        @pl.when(j == 0)
        def _():
            dck_ref[...] = part
            dkpe_ref[...] = pe

        @pl.when(j > 0)
        def _():
            dck_ref[...] += part
            dkpe_ref[...] += pe

    return pl.pallas_call(
        body, name=name, grid=(S // tm, nB),
        in_specs=[pl.BlockSpec((tm, Nsh), lambda i, j: (i, j)), pl.BlockSpec((tm, Nsh // 2), lambda i, j: (i, j)),
                  pl.BlockSpec((None, K, Nsh), lambda i, j: (j, 0, 0))],
        out_specs=[pl.BlockSpec((tm, K), lambda i, j: (i, 0)), pl.BlockSpec((tm, Nsh), lambda i, j: (i, j)),
                   pl.BlockSpec((tm, LANE), lambda i, j: (i, 0))],
        out_shape=[jax.ShapeDtypeStruct((S, K), F32), jax.ShapeDtypeStruct((S, nB * Nsh), BF16),
                   jax.ShapeDtypeStruct((S, LANE), F32)],
        compiler_params=_params(("parallel", "arbitrary")),
    )(dkc, dv, w3)


def _mm_tn_rows(name, a, g, n_rows, pieces):
    M, K = a.shape
    M2, N = g.shape
    assert M == M2
    tk, tn, tm = _pick(K, 1536, LANE), _pick(N, 512, LANE), _pick(M, CONTRACT_TILE, 16)
    nk, nm = K // tk, M // tm

    def body(a_ref, g_ref, o_ref, *acc):
        kk = pl.program_id(1)

        def finish(total):
            for t in range(nk):
                @pl.when(kk == t)
                def _(t=t):
                    for src, dst, n in pieces:
                        lo, hi = max(src, t * tk), min(src + n, (t + 1) * tk)
                        if lo < hi:
                            o_ref[dst + lo - src:dst + hi - src, :] = total[lo - t * tk:hi - t * tk, :]

        part = lax.dot_general(a_ref[...].astype(BF16), g_ref[...].astype(BF16),
                               (((0,), (0,)), ((), ())), preferred_element_type=F32)
        _accumulate(acc[0] if acc else None, part, pl.program_id(2), nm, finish)

    return pl.pallas_call(
        body, name=name, grid=(N // tn, nk, nm),
        in_specs=[pl.BlockSpec((tm, tk), lambda n, k, m: (m, k)),
                  pl.BlockSpec((tm, tn), lambda n, k, m: (m, n))],
        out_specs=pl.BlockSpec((n_rows, tn), lambda n, k, m: (0, n)),
        out_shape=jax.ShapeDtypeStruct((n_rows, N), F32),
        scratch_shapes=[pltpu.VMEM((tk, tn), F32)] if nm > 1 else [],
        compiler_params=_params(("arbitrary", "arbitrary", "arbitrary")),
    )(a, g)


def _attn_fwd(name, q, k, v, n_heads, group, dqk, dv, scale, tq, after=()):
    S = q.shape[0]
    hp = 2 * HEADS_PER_STEP
    kvp = max(1, hp // group)

    def body(q_ref, k_ref, v_ref, *rest):
        o_ref, lse_ref = rest[-2:]
        for j in range(hp):
            jk = j * kvp // hp
            s = lax.dot_general(q_ref[:, j * dqk:(j + 1) * dqk], k_ref[:, jk * dqk:(jk + 1) * dqk],
                                (((1,), (1,)), ((), ())), preferred_element_type=F32)
            m = jnp.max(s, axis=-1, keepdims=True)
            e = jnp.exp((s - m) * scale)
            l = jnp.sum(e, axis=-1, keepdims=True)
            o_ref[:, j * dv:(j + 1) * dv] = jnp.dot(e.astype(BF16), v_ref[:, jk * dv:(jk + 1) * dv],
                                                    preferred_element_type=F32) * (1.0 / l)
            lse_ref[:, j * LANE:(j + 1) * LANE] = jnp.broadcast_to(m * scale + jnp.log(l), (tq, LANE))

    kv_block = lambda h, i: (0, (h * hp // group) // kvp)
    return pl.pallas_call(
        body, name=name, grid=(n_heads // hp, S // tq),
        in_specs=[pl.BlockSpec((tq, hp * dqk), lambda h, i: (i, h)),
                  pl.BlockSpec((S, kvp * dqk), kv_block),
                  pl.BlockSpec((S, kvp * dv), kv_block)] + [pl.BlockSpec(memory_space=pl.ANY) for _ in after],
        out_specs=[pl.BlockSpec((tq, hp * dv), lambda h, i: (i, h)),
                   pl.BlockSpec((tq, hp * LANE), lambda h, i: (i, h))],
        out_shape=[jax.ShapeDtypeStruct((S, n_heads * dv), F32), jax.ShapeDtypeStruct((S, n_heads * LANE), F32)],
        compiler_params=_params(("arbitrary", "arbitrary")),
    )(q, k, v, *after)


def _attn_bwd(name, q, k, v, do, o, lse, n_heads, group, dqk, dv, scale, tq, after=()):
    S = q.shape[0]
    n_kv = n_heads // group
    hp, tq = (group, tq // 2) if group > HEADS_PER_STEP else (HEADS_PER_STEP, tq)
    kvp = max(1, hp // group)
    chains = 4
    tr = tq * hp // chains

    def body(q_ref, k_ref, v_ref, do_ref, o_ref, lse_ref, *rest):
        dq_ref, dk_ref, dv_ref = rest[len(after):]
        h, i = pl.program_id(0), pl.program_id(1)

        @pl.when(((h * hp) % group == 0) & (i == 0))
        def _():
            dk_ref[...] = jnp.zeros(dk_ref.shape, F32)
            dv_ref[...] = jnp.zeros(dv_ref.shape, F32)

        for j in range(hp):
            jk = j * kvp // hp
            kb, vb = k_ref[:, jk * dqk:(jk + 1) * dqk], v_ref[:, jk * dv:(jk + 1) * dv]
            for r0 in range(0, tq, tr):
                rows = slice(r0, r0 + tr)
                qb, dob = q_ref[rows, j * dqk:(j + 1) * dqk], do_ref[rows, j * dv:(j + 1) * dv]
                s = lax.dot_general(qb, kb, (((1,), (1,)), ((), ())), preferred_element_type=F32)
                p = jnp.exp(s * scale - jnp.tile(lse_ref[rows, j * LANE:(j + 1) * LANE], (1, S // LANE)))
                dp = lax.dot_general(dob, vb, (((1,), (1,)), ((), ())), preferred_element_type=F32)
                delta = jnp.sum(dob.astype(F32) * o_ref[rows, j * dv:(j + 1) * dv], axis=-1, keepdims=True)
                ds = (p * ((dp - delta) * scale)).astype(BF16)
                dv_ref[:, jk * dv:(jk + 1) * dv] += lax.dot_general(p.astype(BF16), dob, (((0,), (0,)), ((), ())),
                                                                    preferred_element_type=F32)
                dq_ref[rows, j * dqk:(j + 1) * dqk] = jnp.dot(ds, kb, preferred_element_type=F32)
                dk_ref[:, jk * dqk:(jk + 1) * dqk] += lax.dot_general(ds, qb, (((0,), (0,)), ((), ())),
                                                                      preferred_element_type=F32)

    kv_block = lambda h, i: (0, (h * hp // group) // kvp)
    return pl.pallas_call(
        body, name=name, grid=(n_heads // hp, S // tq),
        in_specs=[pl.BlockSpec((tq, hp * dqk), lambda h, i: (i, h)),
                  pl.BlockSpec((S, kvp * dqk), kv_block),
                  pl.BlockSpec((S, kvp * dv), kv_block),
                  pl.BlockSpec((tq, hp * dv), lambda h, i: (i, h)),
                  pl.BlockSpec((tq, hp * dv), lambda h, i: (i, h)),
                  pl.BlockSpec((tq, hp * LANE), lambda h, i: (i, h))] + [pl.BlockSpec(memory_space=pl.ANY) for _ in after],
        out_specs=[pl.BlockSpec((tq, hp * dqk), lambda h, i: (i, h)),
                   pl.BlockSpec((S, kvp * dqk), kv_block),
                   pl.BlockSpec((S, kvp * dv), kv_block)],
        out_shape=[jax.ShapeDtypeStruct((S, n_heads * dqk), F32),
                   jax.ShapeDtypeStruct((S, n_kv * dqk), F32),
                   jax.ShapeDtypeStruct((S, n_kv * dv), F32)],
        compiler_params=_params(("arbitrary", "arbitrary")),
    )(q, k, v, do, o, lse, *after)


def _adamw_step(w, gg, m, v):
    nm = ADAM_B1 * m + (1.0 - ADAM_B1) * gg
    nv = ADAM_B2 * v + (1.0 - ADAM_B2) * (gg * gg)
    c1, c2 = 1.0 - ADAM_B1 ** ADAM_STEP, 1.0 - ADAM_B2 ** ADAM_STEP
    return -ADAM_LR * ((nm / c1) / (jnp.sqrt(nv / c2) + ADAM_EPS) + ADAM_WD * w), nm, nv


def _adamw(name, w, g, m, v, after=(), rewrite_g=False):
    R, C = w.shape
    tile = _pick(R, max(8, (1 << 19) // C // 8 * 8), 8)
    n_out = 4 if rewrite_g else 3

    def body(w_ref, g_ref, m_ref, v_ref, *rest):
        d_ref, nm_ref, nv_ref = rest[len(after):len(after) + 3]
        gg = g_ref[...]
        d_ref[...], nm_ref[...], nv_ref[...] = _adamw_step(w_ref[...], gg, m_ref[...], v_ref[...])
        if rewrite_g:
            rest[-1][...] = gg

    spec = pl.BlockSpec((tile, C), lambda i: (i, 0))
    return pl.pallas_call(
        body, name=name, grid=(R // tile,), in_specs=[spec] * 4 + [pl.BlockSpec(memory_space=pl.ANY) for _ in after],
        out_specs=[spec] * n_out, out_shape=[jax.ShapeDtypeStruct((R, C), F32)] * n_out,
        compiler_params=_params(("parallel",)),
    )(w, g, m, v, *after)


def _adamw_outer(name, w, left, right, m, v, after=()):
    R, C = w.shape
    Kc = left.shape[0]
    tile = _pick(R, max(LANE, (1 << 19) // C // LANE * LANE), LANE)

    def body(w_ref, l_ref, r_ref, m_ref, v_ref, *rest):
        d_ref, nm_ref, nv_ref, g_ref = rest[len(after):]
        gg = lax.dot_general(l_ref[...], r_ref[...], (((0,), (0,)), ((), ())), preferred_element_type=F32)
        d_ref[...], nm_ref[...], nv_ref[...] = _adamw_step(w_ref[...], gg, m_ref[...], v_ref[...])
        g_ref[...] = gg

    spec = pl.BlockSpec((tile, C), lambda i: (i, 0))
    return pl.pallas_call(
        body, name=name, grid=(R // tile,),
        in_specs=[spec, pl.BlockSpec((Kc, tile), lambda i: (0, i)), pl.BlockSpec((Kc, C), lambda i: (0, 0)), spec, spec]
        + [pl.BlockSpec(memory_space=pl.ANY) for _ in after],
        out_specs=[spec] * 4, out_shape=[jax.ShapeDtypeStruct((R, C), F32)] * 4,
        compiler_params=_params(("parallel",)),
    )(w, left, right, m, v, *after)


def _flip(me, rel):
    return tuple(1 - p if r else p for p, r in zip(me, rel))


def _push(name, ins, out_shapes, transfers, aliases=None):
    ni, no, nt = len(ins), len(out_shapes), len(transfers)

    def body(*refs):
        in_refs, out_refs = refs[:ni], refs[ni:ni + no]
        send_sems, recv_sems = refs[ni + no], refs[ni + no + 1]
        me = (lax.axis_index("x"), lax.axis_index("y"), lax.axis_index("c"))
        copies = []
        for t, (si, sfn, oi, dfn, rel) in enumerate(transfers):
            src_ref = out_refs[si[1]] if isinstance(si, tuple) else in_refs[si]
            src = src_ref.at[sfn(*me)]
            dst = out_refs[oi].at[dfn(*me)]
            if rel is None:
                cp = pltpu.make_async_copy(src, dst, send_sems.at[t])
            else:
                cp = pltpu.make_async_remote_copy(src_ref=src, dst_ref=dst, send_sem=send_sems.at[t],
                                                  recv_sem=recv_sems.at[t], device_id=_flip(me, rel),
                                                  device_id_type=MESH)
            cp.start()
            copies.append(cp)
        for cp in copies:
            cp.wait()

    any_spec = pl.BlockSpec(memory_space=pl.ANY)
    return pl.pallas_call(
        body, name=name, in_specs=[any_spec] * ni, out_specs=[any_spec] * no, out_shape=out_shapes,
        scratch_shapes=[pltpu.SemaphoreType.DMA((nt,)), pltpu.SemaphoreType.DMA((nt,))],
        input_output_aliases=aliases or {},
        compiler_params=pltpu.CompilerParams(has_side_effects=True),
    )(*ins)


_HBM = pl.BlockSpec(memory_space=pltpu.HBM)
_SEM = pl.BlockSpec(memory_space=pltpu.SEMAPHORE)
_DATAFLOW = pltpu.SideEffectType.DATAFLOW_SIDE_EFFECTING


def _split_copies(refs, transfers, send_sems, recv_sems):
    me = (lax.axis_index("x"), lax.axis_index("y"), lax.axis_index("c"))
    return [pltpu.make_async_remote_copy(src_ref=refs[sb].at[sfn(*me)], dst_ref=refs[db].at[dfn(*me)],
                                         send_sem=send_sems.at[t], recv_sem=recv_sems.at[t],
                                         device_id=_flip(me, rel), device_id_type=MESH)
            for t, (sb, sfn, db, dfn, rel) in enumerate(transfers)]


def _push_start(name, bufs, transfers, after=()):
    nb, na, nt = len(bufs), len(after), len(transfers)

    def body(*refs):
        send_sems, recv_sems, token = refs[nb + na], refs[nb + na + 1], refs[-1]
        for cp in _split_copies(refs[nb + na + 2:2 * nb + na + 2], transfers, send_sems, recv_sems):
            cp.start()
        token[...] = jnp.zeros(token.shape, token.dtype)

    res = pl.pallas_call(
        body, name=name,
        out_shape=(pltpu.SemaphoreType.DMA((nt,)), pltpu.SemaphoreType.DMA((nt,)),
                   *[pltpu.HBM(b.shape, b.dtype) for b in bufs], jax.ShapeDtypeStruct((8, LANE), F32)),
        in_specs=[_HBM] * nb + [pl.BlockSpec(memory_space=pl.ANY)] * na,
        out_specs=(_SEM, _SEM, *[_HBM] * nb, pl.BlockSpec(memory_space=pltpu.VMEM)),
        input_output_aliases={i: i + 2 for i in range(nb)},
        compiler_params=pltpu.CompilerParams(has_side_effects=_DATAFLOW),
    )(*[pltpu.with_memory_space_constraint(b, pltpu.HBM) for b in bufs], *after)
    return res[0], res[1], list(res[2:2 + nb]), res[-1]


def _push_wait(name, send_sems, recv_sems, bufs, transfers, after):
    nb = len(bufs)

    def body(*refs):
        for cp in _split_copies(refs[:nb], transfers, refs[nb], refs[nb + 1]):
            cp.wait_send()
            cp.wait_recv()

    res = pl.pallas_call(
        body, name=name, out_shape=[pltpu.HBM(b.shape, b.dtype) for b in bufs],
        in_specs=[_HBM] * nb + [_SEM, _SEM, pl.BlockSpec(memory_space=pl.ANY)], out_specs=[_HBM] * nb,
        input_output_aliases={i: i for i in range(nb)},
        compiler_params=pltpu.CompilerParams(has_side_effects=_DATAFLOW),
    )(*bufs, send_sems, recv_sems, after)
    return list(res)


ICI_RELS = ((1, 0, 0), (0, 1, 0), (1, 1, 0))
SIBLING = (0, 0, 1)


def _chip(x, y):
    return 2 * x + y


def _cast_into_slot(name, where, w, after=()):
    R, C = w.shape
    tile = _pick(R, 512, BF16_ROWS)

    def body(where_ref, w_ref, *rest):
        rest[-1][...] = w_ref[...].astype(BF16)

    return pl.pallas_call(
        body, name=name,
        grid_spec=pltpu.PrefetchScalarGridSpec(
            num_scalar_prefetch=1, grid=(R // tile,),
            in_specs=[pl.BlockSpec((tile, C), lambda i, wh: (i, 0))] + [pl.BlockSpec(memory_space=pl.ANY) for _ in after],
            out_specs=pl.BlockSpec((None, tile, C), lambda i, wh: (wh[1], i, 0))),
        out_shape=jax.ShapeDtypeStruct((N_CHIPS, R, C), BF16),
        compiler_params=_params(("parallel",)),
    )(where, w, *after)


BF16_ROWS = 16


def _rows_split(shape2):
    return (shape2[0] // 2) % BF16_ROWS == 0


def _half_shape(shape2):
    R, C = shape2
    return (R // 2, C) if _rows_split(shape2) else (R, C // 2)


def _half(shape2, c):
    R, C = shape2
    if _rows_split(shape2):
        return (pl.ds(c * (R // 2), R // 2), slice(None))
    return (slice(None), pl.ds(pl.multiple_of(c * (C // 2), LANE), C // 2))


def _gather_casts(tag, where, shards, after=()):
    return [_cast_into_slot(f"cast_w_{tag}{w}", where, s, after) for w, s in enumerate(shards)]


def _gather_start(tag, own, after):
    transfers = []
    for w in range(len(own)):
        def idx(x, y, c, shape2=own[w].shape[1:]):
            return (_chip(x, y), *_half(shape2, c))
        transfers += [(w, idx, w, idx, rel) for rel in ICI_RELS]
    send_sems, recv_sems, bufs, token = _push_start(f"gather_{tag}_start", own, transfers, after)
    return (tag, send_sems, recv_sems, bufs, transfers), token


def _forward_transfers(part):
    second = []
    for w, p in enumerate(part):
        for rel in ICI_RELS:
            def idx(x, y, c, shape2=p.shape[1:], rel=rel):
                return (_chip(x ^ rel[0], y ^ rel[1]), *_half(shape2, c))
            second.append((w, idx, w, idx, SIBLING))
    return second


def _gather_finish(handle, after):
    tag, send_sems, recv_sems, bufs, transfers = handle
    part = _push_wait(f"gather_{tag}_wait", send_sems, recv_sems, bufs, transfers, after)
    outs = [jax.ShapeDtypeStruct(p.shape, p.dtype) for p in part]
    second = [(("out", sb), sfn, db, dfn, rel) for sb, sfn, db, dfn, rel in _forward_transfers(part)]
    return _push(f"gather_{tag}_d2d", list(part), outs, second, aliases={w: w for w in range(len(part))})


def _gather_mid(handle, after):
    tag, send_sems, recv_sems, bufs, transfers = handle
    part = _push_wait(f"gather_{tag}_wait", send_sems, recv_sems, bufs, transfers, after)
    second = _forward_transfers(part)
    s2, r2, bufs2, token = _push_start(f"gather_{tag}_d2d_start", part, second)
    return (tag, s2, r2, bufs2, second), token


def _gather_end(handle, after):
    tag, send_sems, recv_sems, bufs, transfers = handle
    return _push_wait(f"gather_{tag}_d2d_wait", send_sems, recv_sems, bufs, transfers, after)


def _reduce_begin(tag, grads, after=()):
    n = len(grads)
    land = [lax.empty((N_CHIPS, *_half_shape(g.shape[1:])), F32) for g in grads]
    t1 = []
    for w in range(n):
        t1.append((w, lambda x, y, c, shape2=grads[w].shape[1:]: (slice(None), *_half(shape2, 1 - c)), n + w,
                   lambda x, y, c: (slice(None), slice(None), slice(None)), SIBLING))
    send_sems, recv_sems, bufs, token = _push_start(f"reduce_{tag}_d2d_start", list(grads) + land, t1, after)
    return (tag, send_sems, recv_sems, bufs, t1), token


def _reduce_start(handle, where, after, after_start=()):
    tag, send_sems, recv_sems, bufs, t1 = handle
    n = len(bufs) // 2
    res = _push_wait(f"reduce_{tag}_d2d_wait", send_sems, recv_sems, bufs, t1, after)
    grads, from_sib = res[:n], res[n:]
    pair = [_pair_sum(f"pair_sum_{tag}{w}", where, grads[w], from_sib[w]) for w in range(n)]
    land = [lax.empty((3, *_half_shape(g.shape[1:])), BF16) for g in grads]
    t3 = []
    for w in range(n):
        for r, rel in enumerate(ICI_RELS):
            t3.append((w, lambda x, y, c, rel=rel: (_chip(x ^ rel[0], y ^ rel[1]), slice(None), slice(None)), n + w,
                       lambda x, y, c, r=r: (r, slice(None), slice(None)), rel))
    send_sems, recv_sems, bufs, token = _push_start(f"reduce_{tag}_start", pair + land, t3, after_start)
    return (tag, send_sems, recv_sems, bufs, t3, list(grads), list(from_sib)), token


def _reduce_finish(handle, where, after):
    tag, send_sems, recv_sems, bufs, t3, grads, from_sib = handle
    n = len(grads)
    from_chips = _push_wait(f"reduce_{tag}_wait", send_sems, recv_sems, bufs, t3, after)[n:]
    mine = [_chip_sum(f"chip_sum_{tag}{w}", where, grads[w], from_sib[w], from_chips[w]) for w in range(n)]
    t5 = []
    for w in range(n):
        def idx(x, y, c, shape2=grads[w].shape[1:]):
            return _half(shape2, c)
        t5.append((w, idx, w, idx, SIBLING))
    send_sems, recv_sems, bufs, token = _push_start(f"reduce_{tag}_swap_start", mine, t5)
    return (tag, send_sems, recv_sems, bufs, t5), token


def _reduce_end(handle, after):
    tag, send_sems, recv_sems, bufs, t5 = handle
    return _push_wait(f"reduce_{tag}_swap_wait", send_sems, recv_sems, bufs, t5, after)


def _pair_sum(name, where, g, from_sib):
    _, hr, hc = from_sib.shape
    tile = _pick(hr, max(16, (1 << 19) // hc // 16 * 16), 16)
    nb = hr // tile
    slot = lambda k, wh: (wh[1] + 1 + k) % N_CHIPS
    if _rows_split(g.shape[1:]):
        mine = lambda k, i, wh: (slot(k, wh), wh[0] * nb + i, 0)
    else:
        mine = lambda k, i, wh: (slot(k, wh), i, wh[0])

    def body(where_ref, g_ref, s_ref, o_ref):
        o_ref[...] = (g_ref[...] + s_ref[...]).astype(BF16)

    return pl.pallas_call(
        body, name=name,
        grid_spec=pltpu.PrefetchScalarGridSpec(
            num_scalar_prefetch=1, grid=(N_CHIPS - 1, nb),
            in_specs=[pl.BlockSpec((None, tile, hc), mine),
                      pl.BlockSpec((None, tile, hc), lambda k, i, wh: (slot(k, wh), i, 0))],
            out_specs=pl.BlockSpec((None, tile, hc), lambda k, i, wh: (slot(k, wh), i, 0))),
        out_shape=jax.ShapeDtypeStruct((N_CHIPS, hr, hc), BF16),
        compiler_params=_params(("parallel", "parallel")),
    )(where, g, from_sib)


def _chip_sum(name, where, g, from_sib, from_chips):
    _, R, C = g.shape
    _, hr, hc = from_sib.shape
    tile = _pick(hr, max(16, (1 << 19) // hc // 16 * 16), 16)
    nb = hr // tile
    if _rows_split((R, C)):
        mine = lambda i, wh: (wh[0] * nb + i, 0)
    else:
        mine = lambda i, wh: (i, wh[0])

    def body(where_ref, g_ref, s_ref, r_ref, o_ref):
        acc = g_ref[...] + s_ref[...]
        for r in range(3):
            acc = acc + r_ref[r].astype(F32)
        o_ref[...] = acc

    return pl.pallas_call(
        body, name=name,
        grid_spec=pltpu.PrefetchScalarGridSpec(
            num_scalar_prefetch=1, grid=(nb,),
            in_specs=[pl.BlockSpec((None, tile, hc), lambda i, wh: (wh[1], *mine(i, wh))),
                      pl.BlockSpec((None, tile, hc), lambda i, wh: (wh[1], i, 0)),
                      pl.BlockSpec((3, tile, hc), lambda i, wh: (0, i, 0))],
            out_specs=pl.BlockSpec((tile, hc), mine)),
        out_shape=jax.ShapeDtypeStruct((R, C), F32),
        compiler_params=_params(("parallel",)),
    )(where, g, from_sib, from_chips)


def _allgather8(name, blk, after=()):
    m_per, n = blk.shape
    rels = [(0, 0, 1), (1, 0, 0), (0, 1, 0), (1, 1, 0), (1, 0, 1), (0, 1, 1), (1, 1, 1)]

    def body(x_ref, *rest):
        out_ref, send_sems, recv_sems = rest[len(after):]
        me = (lax.axis_index("x"), lax.axis_index("y"), lax.axis_index("c"))
        my_rows = out_ref.at[pl.ds((4 * me[0] + 2 * me[1] + me[2]) * m_per, m_per), :]
        out_ref[pl.ds((4 * me[0] + 2 * me[1] + me[2]) * m_per, m_per), :] = x_ref[...]
        copies = []
        for t, rel in enumerate(rels):
            cp = pltpu.make_async_remote_copy(src_ref=x_ref, dst_ref=my_rows, send_sem=send_sems.at[t],
                                              recv_sem=recv_sems.at[t], device_id=_flip(me, rel),
                                              device_id_type=MESH)
            cp.start()
            copies.append(cp)
        for cp in copies:
            cp.wait()

    return pl.pallas_call(
        body, name=name, out_shape=jax.ShapeDtypeStruct((N_DEV * m_per, n), blk.dtype),
        in_specs=[pl.BlockSpec(memory_space=pltpu.VMEM)] + [pl.BlockSpec(memory_space=pl.ANY) for _ in after],
        out_specs=pl.BlockSpec(memory_space=pltpu.VMEM),
        scratch_shapes=[pltpu.SemaphoreType.DMA((7,)), pltpu.SemaphoreType.DMA((7,))],
        compiler_params=pltpu.CompilerParams(has_side_effects=True, vmem_limit_bytes=VMEM_LIMIT),
    )(blk, *after)


def _rope_tables(seq_len, dim, reps):
    rows = seq_len // GRID_W
    t = np.arange(seq_len)
    row, col = (t // GRID_W).astype(np.float32), (t % GRID_W).astype(np.float32)
    half = dim // 2
    inv = jnp.asarray(ROPE_THETA, F32) ** (-jnp.arange(0, half, 2, dtype=F32) / half)
    ang_r = jnp.asarray(row)[:, None] * inv[None, :]
    ang_c = jnp.asarray(col)[:, None] * inv[None, :]
    ang = jnp.concatenate([ang_r, ang_r, ang_c, ang_c], axis=-1)
    cos, sin = jnp.cos(ang), jnp.sin(ang)
    low = (np.arange(dim) % (dim // 2)) < (dim // 4)
    s_lo = jnp.where(jnp.asarray(low)[None, :], -sin, 0.0)
    s_hi = jnp.where(jnp.asarray(low)[None, :], 0.0, sin)
    del rows
    return tuple(jnp.tile(a, (1, reps)) for a in (cos, s_lo, s_hi))


def _make_rope(shift):
    def up(v):
        return pltpu.roll(v, LANE - shift, 1)

    def down(v):
        return pltpu.roll(v, shift, 1)

    @jax.custom_vjp
    def rope(v, cos, s_lo, s_hi):
        return v * cos + up(v) * s_lo + down(v) * s_hi

    def fwd(v, cos, s_lo, s_hi):
        return rope(v, cos, s_lo, s_hi), (cos, s_lo, s_hi)

    def bwd(res, dy):
        cos, s_lo, s_hi = res
        return dy * cos + down(dy * s_lo) + up(dy * s_hi), jnp.zeros_like(cos), jnp.zeros_like(cos), jnp.zeros_like(cos)

    rope.defvjp(fwd, bwd)
    return rope


_rope_a = _make_rope(HEAD_DIM // 4)
_rope_b = _make_rope(QK_ROPE // 4)


def _prep(segs, tabs, g_q, g_k, g_ckv):
    qa, ka, va, qn, qp, ckv, kpe = segs
    ta, tb = tabs
    q_a = jnp.concatenate([_rope_a(_rn(s) * g_q, *ta) for s in qa], axis=1)
    k_a = jnp.concatenate([_rope_a(_rn(s) * g_k, *ta) for s in ka], axis=1)
    cat = []
    for h in range(HB):
        cat += [qn[h], _rope_b(qp[h], *tb)]
    q_cat = jnp.concatenate(cat, axis=1)
    return q_a, k_a, va, q_cat, _rn(ckv) * g_ckv, _rope_b(kpe, *tb)


def _split_proj(proj):
    qa = [proj[:, O_QA + h * LANE:O_QA + (h + 1) * LANE] for h in range(HA)]
    ka = [proj[:, O_KA + h * LANE:O_KA + (h + 1) * LANE] for h in range(HKV)]
    va = proj[:, O_VA:O_QN]
    qn = [proj[:, O_QN + h * LANE:O_QN + (h + 1) * LANE] for h in range(HB)]
    qp = [proj[:, O_QP + h * LANE:O_QP + (h + 1) * LANE] for h in range(HB)]
    return qa, ka, va, qn, qp, proj[:, O_CKV:O_KPE], proj[:, O_KPE:W_INP]


def _permute_w_in(nat):
    K = nat.shape[1]
    qb = nat[W_QA + 2 * W_KA:W_QA + 2 * W_KA + W_QB].reshape(HB, QK_B, K)
    qp = jnp.concatenate([qb[:, QK_NOPE:], jnp.zeros((HB, LANE - QK_ROPE, K), nat.dtype)], axis=1)
    tail = nat[W_QA + 2 * W_KA + W_QB:]
    return jnp.concatenate([nat[:O_QN], qb[:, :QK_NOPE].reshape(HB * QK_NOPE, K), qp.reshape(HB * LANE, K),
                            tail, jnp.zeros((LANE - QK_ROPE, K), nat.dtype)], axis=0)


def _unpermute_pieces():
    pieces = [(0, 0, O_QN)]
    for h in range(HB):
        pieces.append((O_QN + h * QK_NOPE, O_QN + h * QK_B, QK_NOPE))
        pieces.append((O_QP + h * LANE, O_QN + h * QK_B + QK_NOPE, QK_ROPE))
    pieces.append((O_CKV, O_QN + W_QB, KV_RANK))
    pieces.append((O_KPE, O_QN + W_QB + KV_RANK, QK_ROPE))
    return pieces


def kernel(x, c, w_ada, b_ada, g_pre_attn, w_in, g_q_a, g_k_a, g_ckv, w_kv_b, g_out_a, g_out_b, w_out, g_post_attn, g_pre_mlp, w_mlp_in, w_mlp_out, g_post_mlp, loss_target, m_w_ada, m_b_ada, m_g_pre_attn, m_w_in, m_g_q_a, m_g_k_a, m_g_ckv, m_w_kv_b, m_g_out_a, m_g_out_b, m_w_out, m_g_post_attn, m_g_pre_mlp, m_w_mlp_in, m_w_mlp_out, m_g_post_mlp, v_w_ada, v_b_ada, v_g_pre_attn, v_w_in, v_g_q_a, v_g_k_a, v_g_ckv, v_w_kv_b, v_g_out_a, v_g_out_b, v_w_out, v_g_post_attn, v_g_pre_mlp, v_w_mlp_in, v_w_mlp_out, v_g_post_mlp):
    S, D = x.shape[1], x.shape[2]
    x2d, tgt = x[0], loss_target[0]
    ix, iy, ic = lax.axis_index("x"), lax.axis_index("y"), lax.axis_index("c")
    j_me = _chip(ix, iy)
    e_me = 4 * ix + 2 * iy + ic
    T = _pick(S, 512, 8)
    TB = _pick(S, 256, 8)

    where = jnp.stack([ic, j_me]).astype(jnp.int32)
    DFF = w_mlp_out.shape[1] * N_CHIPS

    c_all = _allgather8("gather_c", c.reshape(8, D // 8)).reshape(N_DEV, D)
    w_in_t = jnp.swapaxes(w_in[0], 0, 1)
    h_qkv, tok_a = _gather_start("qkv", _gather_casts("qkv", where, [w_in_t, w_kv_b[0]]), (c_all,))
    later = [_gather_casts(tag, where, [w[0]], (tok_a,))
             for tag, w in (("out", w_out), ("up", w_mlp_in), ("down", w_mlp_out))]

    def silu_fn(cc):
        return (cc * (1.0 / (1.0 + jnp.exp(-cc))),), ()

    (c_act,), _ = _rowwise("silu_c", silu_fn, [c_all], [], [(D, BF16)], [], N_DEV)
    ca_pad = jnp.concatenate([c_act, jnp.zeros_like(c_act)], axis=0)
    n_mod = w_ada.shape[2]
    b_cols = lax.dynamic_slice(b_ada, (0, j_me * n_mod), (1, n_mod))
    (mod_part,) = _mm_nn("mod_mm", ca_pad, w_ada, [F32], epilogue=lambda acc, b: (acc + b,),
                         extras=(jnp.broadcast_to(b_cols, (16, n_mod)),))
    mod_all = _allgather8("gather_mod", mod_part[:8] + tok_a[0, 0],
                          after=[c_[0] for c_ in later]).reshape(N_DEV, 8, n_mod)
    mod = jnp.concatenate([lax.dynamic_slice(mod_all, (2 * j, e_me, 0), (1, 1, n_mod))[0] for j in range(N_CHIPS)],
                          axis=1)

    m_qkv, tok_q = _gather_mid(h_qkv, mod_all)
    h_out, tok_b = _gather_start("out", later[0], (tok_q,))
    h_up, tok_c = _gather_start("up", later[1], (tok_b,))
    h_down, tok_d = _gather_start("down", later[2], (tok_c,))
    mod = mod + tok_d[0, 0]
    sh_a, sc_a, gt_a, sh_m, sc_m, gt_m = [mod[:, i * D:(i + 1) * D] for i in range(6)]

    tabs_a = _rope_tables(S, HEAD_DIM, 1)
    tabs_b = _rope_tables(S, QK_ROPE, LANE // QK_ROPE)

    def pre_attn_fn(xt, g, sc, sh):
        return (_rn(xt) * g * (1.0 + sc) + sh,), ()

    (h_b,), _ = _rowwise("pre_attn", pre_attn_fn, [x2d], [g_pre_attn, sc_a, sh_a], [(D, BF16)], [], T)
    g_in, g_kvb = _gather_end(m_qkv, h_b)
    wpt = _permute_w_in(g_in.reshape(W_IN, D))[None]
    (proj,) = _mm_nt("proj_mm", h_b, wpt, [F32])

    def prep_fn(pt, ca_, sla, sha, cb_, slb, shb, gq, gk, gc):
        return _prep(_split_proj(pt), ((ca_, sla, sha), (cb_, slb, shb)), gq, gk, gc), ()

    (qa_b, ka_b, va_b, qcat_b, ckvn_b, kpe_b), _ = _rowwise(
        "prep", prep_fn, [proj, *tabs_a, *tabs_b], [g_q_a, g_k_a, g_ckv],
        [(W_QA, BF16), (W_KA, BF16), (W_KA, BF16), (2 * HB * LANE, BF16), (KV_RANK, BF16), (LANE, BF16)], [], T)
    kcat_b, vb_b = _kv_mm("kv_mm", ckvn_b, g_kvb, kpe_b)

    TQ = _pick(S, 512, 16)
    sc_a_ = 1.0 / math.sqrt(HEAD_DIM)
    sc_b_ = 1.0 / math.sqrt(QK_B)
    o_a, lse_a = _attn_fwd("attn_a_fwd", qa_b, ka_b, va_b, HA, HA // HKV, HEAD_DIM, HEAD_DIM, sc_a_, TQ)
    m_out, tok_m = _gather_mid(h_out, o_a)
    o_b, lse_b = _attn_fwd("attn_b_fwd", qcat_b, kcat_b, vb_b, HB, 1, 2 * LANE, V_DIM, sc_b_, TQ, after=(tok_m,))
    m_up, tok_m = _gather_mid(h_up, o_b)

    def mix_fn(oa, ob, ga, gb):
        return jnp.concatenate([_rn(oa) * ga, _rn(ob) * gb], axis=1)

    (g_wout,) = _gather_end(m_out, o_b)
    w_out_f = g_wout.reshape(1, N_CHIPS * g_wout.shape[1], D)
    on_b, o2 = _rows_mm("out_mm", mix_fn, [o_a, o_b], [g_out_a, g_out_b], w_out_f, after=(tok_m,))

    def mid(xt, o2t, gta, gpa, gpm, scm, shm):
        x1 = xt + gta * (_rn(o2t) * gpa)
        return x1, _rn(x1) * gpm * (1.0 + scm) + shm

    def mid_fn(*a):
        return mid(*a), ()

    mid_vecs = [gt_a, g_post_attn, g_pre_mlp, sc_m, sh_m]
    (x1, h2_b), _ = _rowwise("mid", mid_fn, [x2d, o2], mid_vecs, [(D, F32), (D, BF16)], [], T)
    (g_mlp_in,) = _gather_end(m_up, h2_b)
    relu_b, a_b = _mm_nn("mlp_in_mm", h2_b, g_mlp_in, [BF16, BF16],
                         epilogue=lambda acc: (jnp.maximum(acc, 0.0), jnp.square(jnp.maximum(acc, 0.0))))
    (g_mlp_out,) = _gather_finish(h_down, a_b)
    w_mlp_out_f = g_mlp_out.reshape(1, DFF, D)
    (y,) = _mm_nn("mlp_out_mm", a_b, w_mlp_out_f, [F32])

    def last_fn(yt, x1t, tt, gtm, gpo):
        n, r = _rn_parts(yt)
        err = x1t + gtm * (n * gpo) - tt
        dx2 = err * (1.0 / D)
        loss = 0.5 * jnp.sum(jnp.mean(err * err, axis=-1))
        s = _rows(dx2 * n)
        return (_rn_bwd(n, r, dx2 * (gtm * gpo)), dx2), (jnp.full((1, LANE), loss, F32), gpo * s, gtm * s)

    (dy_b, dx2), (loss_v, d_gt_m, d_g_post_mlp) = _rowwise(
        "loss_bwd", last_fn, [y, x1, tgt], [gt_m, g_post_mlp], [(D, BF16), (D, F32)], [LANE, D, D], TB)

    gw_mlp_out = _mm_tn("gw_mlp_out_mm", a_b, dy_b, 1).reshape(N_CHIPS, DFF // N_CHIPS, D)
    b_down, tok = _reduce_begin("down", [gw_mlp_out])
    (du_b,) = _mm_nt("d_mlp_out_mm", dy_b, w_mlp_out_f, [BF16],
                     epilogue=lambda acc, rt: (acc * (2.0 * rt.astype(F32)),), extras=(relu_b,), after=(tok,))
    r_down, tok = _reduce_start(b_down, where, du_b)
    gw_mlp_in = _mm_tn("gw_mlp_in_mm", h2_b, du_b, N_CHIPS, after=(tok,))
    b_up, tok = _reduce_begin("up", [gw_mlp_in])
    (dh2,) = _mm_nt("d_mlp_in_mm", du_b, g_mlp_in, [F32], after=(tok,))
    r_up, tok_up = _reduce_start(b_up, where, dh2)

    def mid_bwd_fn(xt, o2t, dh2t, dx2t, gta, gpa, gpm, scm, shm):
        n2, r2 = _rn_parts(o2t)
        n1, r1 = _rn_parts(xt + gta * (n2 * gpa))
        dx1 = dx2t + _rn_bwd(n1, r1, dh2t * (gpm * (1.0 + scm)))
        s1, s2 = _rows(dh2t * n1), _rows(dx1 * n2)
        do2 = _rn_bwd(n2, r2, dx1 * (gta * gpa))
        return (dx1, do2), (gpa * s2, gta * s2, (1.0 + scm) * s1, gpm * s1, _rows(dh2t))

    (dx1, do2_b), (d_gt_a, d_g_post_attn, d_g_pre_mlp, d_sc_m, d_sh_m) = _rowwise(
        "mid_bwd", mid_bwd_fn, [x2d, o2, dh2, dx2], mid_vecs, [(D, F32), (D, BF16)], [D] * 5, TB, after=(tok_up,))

    (d_on,) = _mm_nt("d_out_mm", do2_b, w_out_f, [F32])
    gw_out = _mm_tn("gw_out_mm", on_b, do2_b, 1).reshape(N_CHIPS, w_out.shape[1], D)

    def mix_bwd_fn(oa, ob, dont, ga, gb):
        na, ra = _rn_parts(oa)
        nb, rb = _rn_parts(ob)
        da, db = dont[:, :W_QA], dont[:, W_QA:]
        return (_rn_bwd(na, ra, da * ga), _rn_bwd(nb, rb, db * gb)), (_rows(da * na), _rows(db * nb))

    (doa_b, dob_b), (d_g_out_a, d_g_out_b) = _rowwise(
        "mix_bwd", mix_bwd_fn, [o_a, o_b, d_on], [g_out_a, g_out_b], [(W_QA, BF16), (HB * V_DIM, BF16)],
        [W_QA, HB * V_DIM], TB)

    dqa, dka, dva = _attn_bwd("attn_a_bwd", qa_b, ka_b, va_b, doa_b, o_a, lse_a, HA, HA // HKV, HEAD_DIM, HEAD_DIM,
                              sc_a_, TQ)
    dqcat, dkcat, dvb = _attn_bwd("attn_b_bwd", qcat_b, kcat_b, vb_b, dob_b, o_b, lse_b, HB, 1, 2 * LANE, V_DIM,
                                  sc_b_, TQ)

    d_ckvn, dkv_b, dkpe = _kv_mm_bwd("d_kv_mm", dkcat, dvb, g_kvb)
    gw_kvb = _mm_tn("gw_kv_mm", ckvn_b, dkv_b, N_CHIPS)
    b_kvb, tok_kvb = _reduce_begin("okv", [gw_out, gw_kvb])

    def prep_bwd_fn(pt, dqat, dkat, dvat, dqct, dckt, dkpt, ca_, sla, sha, cb_, slb, shb, gq, gk, gc):
        tabs = ((ca_, sla, sha), (cb_, slb, shb))
        _, vjp = jax.vjp(lambda sg, a1, a2, a3: _prep(sg, tabs, a1, a2, a3), _split_proj(pt), gq, gk, gc)
        (dqa_s, dka_s, dva_s, dqn_s, dqp_s, dck_s, dkp_s), dgq, dgk, dgc = vjp((dqat, dkat, dvat, dqct, dckt, dkpt))
        dproj = jnp.concatenate([*dqa_s, *dka_s, dva_s, *dqn_s, *dqp_s, dck_s, dkp_s], axis=1)
        return (dproj,), (dgq, dgk, dgc)

    (dproj_b,), (d_g_q_a, d_g_k_a, d_g_ckv) = _rowwise(
        "prep_bwd", prep_bwd_fn, [proj, dqa, dka, dva, dqcat, d_ckvn, dkpe, *tabs_a, *tabs_b],
        [g_q_a, g_k_a, g_ckv], [(W_INP, BF16)], [HEAD_DIM, HEAD_DIM, KV_RANK], TB, after=(tok_kvb,))

    r_kvb, tok = _reduce_start(b_kvb, where, dproj_b)
    (dh,) = _mm_nn("d_proj_mm", dproj_b, wpt, [F32], after=(tok,))
    gw_in = _mm_tn_rows("gw_in_mm", dproj_b, h_b, W_IN, _unpermute_pieces()).reshape(N_CHIPS, W_IN // N_CHIPS, D)
    b_in, tok = _reduce_begin("in", [gw_in])

    def pre_attn_bwd_fn(xt, dht, dx1t, g, sc, sh):
        n, r = _rn_parts(xt)
        s = _rows(dht * n)
        return (dx1t + _rn_bwd(n, r, dht * (g * (1.0 + sc))),), ((1.0 + sc) * s, g * s, _rows(dht))

    (grad_x,), (d_g_pre_attn, d_sc_a, d_sh_a) = _rowwise(
        "pre_attn_bwd", pre_attn_bwd_fn, [x2d, dh, dx1], [g_pre_attn, sc_a, sh_a], [(D, F32)], [D] * 3, TB,
        after=(tok,))

    small_names = ["b_ada", "g_pre_attn", "g_q_a", "g_k_a", "g_ckv", "g_out_a", "g_out_b", "g_post_attn",
                   "g_pre_mlp", "g_post_mlp"]
    small_local = jnp.concatenate([d_sh_a, d_sc_a, d_gt_a, d_sh_m, d_sc_m, d_gt_m, d_g_pre_attn, d_g_q_a, d_g_k_a,
                                   d_g_ckv, d_g_out_a, d_g_out_b, d_g_post_attn, d_g_pre_mlp, d_g_post_mlp,
                                   loss_v[:, :8]], axis=1)
    n_sent = small_local.shape[1]
    n_small = n_sent - 8
    small_all = _allgather8("gather_small", small_local.reshape(8, n_sent // 8)).reshape(N_DEV, n_sent)

    def sum8_fn(t):
        return (), (jnp.sum(t, axis=0, keepdims=True),)

    r_in, tok_in = _reduce_start(b_in, where, grad_x, (small_all,))
    _, (sent_sum,) = _rowwise("sum_small", sum8_fn, [small_all], [], [], [n_sent], N_DEV, after=(tok_in,))
    small_sum, loss = sent_sum[:, :n_small], sent_sum[0, n_small]
    dmod_cols = lax.dynamic_slice(small_all, (0, j_me * n_mod), (N_DEV, n_mod))
    dmod_pad = jnp.concatenate([dmod_cols, jnp.zeros_like(dmod_cols)], axis=0).astype(BF16)

    weights = dict(w_ada=w_ada, b_ada=b_ada, g_pre_attn=g_pre_attn, w_in=w_in, g_q_a=g_q_a, g_k_a=g_k_a, g_ckv=g_ckv,
                   w_kv_b=w_kv_b, g_out_a=g_out_a, g_out_b=g_out_b, w_out=w_out, g_post_attn=g_post_attn,
                   g_pre_mlp=g_pre_mlp, w_mlp_in=w_mlp_in, w_mlp_out=w_mlp_out, g_post_mlp=g_post_mlp)
    ms = dict(w_ada=m_w_ada, b_ada=m_b_ada, g_pre_attn=m_g_pre_attn, w_in=m_w_in, g_q_a=m_g_q_a, g_k_a=m_g_k_a,
              g_ckv=m_g_ckv, w_kv_b=m_w_kv_b, g_out_a=m_g_out_a, g_out_b=m_g_out_b, w_out=m_w_out,
              g_post_attn=m_g_post_attn, g_pre_mlp=m_g_pre_mlp, w_mlp_in=m_w_mlp_in, w_mlp_out=m_w_mlp_out,
              g_post_mlp=m_g_post_mlp)
    vs = dict(w_ada=v_w_ada, b_ada=v_b_ada, g_pre_attn=v_g_pre_attn, w_in=v_w_in, g_q_a=v_g_q_a, g_k_a=v_g_k_a,
              g_ckv=v_g_ckv, w_kv_b=v_w_kv_b, g_out_a=v_g_out_a, g_out_b=v_g_out_b, w_out=v_w_out,
              g_post_attn=v_g_post_attn, g_pre_mlp=v_g_pre_mlp, w_mlp_in=v_w_mlp_in, w_mlp_out=v_w_mlp_out,
              g_post_mlp=v_g_post_mlp)
    order = list(weights)
    grads, deltas, new_m, new_v = {}, {}, {}, {}

    def update(name, g, after=()):
        view = (lambda t: jnp.swapaxes(t, 0, 1)) if name == "w_in" else (lambda t: t)
        d_, m_, v_, g_ = _adamw("adamw_" + name, view(weights[name][0]), g, view(ms[name][0]), view(vs[name][0]),
                                after, rewrite_g=True)
        grads[name], deltas[name], new_m[name], new_v[name] = (view(t)[None] for t in (g_, d_, m_, v_))
        return d_

    chain = ((("w_mlp_out",), r_down), (("w_mlp_in",), r_up), (("w_out", "w_kv_b"), r_kvb), (("w_in",), r_in))
    swap, tok = _reduce_finish(chain[0][1], where, small_sum)
    last, m_, v_, g_ = _adamw_outer("adamw_w_ada", w_ada[0], ca_pad, dmod_pad, m_w_ada[0], v_w_ada[0], (tok,))
    grads["w_ada"], deltas["w_ada"], new_m["w_ada"], new_v["w_ada"] = g_[None], last[None], m_[None], v_[None]
    for k, (names, _) in enumerate(chain):
        gs = _reduce_end(swap, last)
        if k + 1 < len(chain):
            swap, tok = _reduce_finish(chain[k + 1][1], where, gs[0])
        for name, g in zip(names, gs):
            last = update(name, g, (tok,))

    cat = lambda d: jnp.concatenate([d[n] for n in small_names], axis=1)
    d_, m_, v_ = _adamw("adamw_small", cat(weights), small_sum, cat(ms), cat(vs))
    off = 0
    for n in small_names:
        w_ = weights[n].shape[1]
        grads[n], deltas[n], new_m[n], new_v[n] = (a[:, off:off + w_] for a in (small_sum, d_, m_, v_))
        off += w_

    return (loss, grad_x[None], *[grads[n] for n in order], *[deltas[n] for n in order],
            *[new_m[n] for n in order], *[new_v[n] for n in order])
```

```python
import functools
import math

import numpy as np
import jax
import jax.numpy as jnp
from jax import lax
from jax.experimental import pallas as pl
from jax.experimental.pallas import tpu as pltpu

F32 = jnp.float32
BF16 = jnp.bfloat16
MESH = pl.DeviceIdType.MESH

EPS = 1e-6
GRID_W = 64
ROPE_THETA = 10000.0
HEAD_DIM = 128
HA = 8
HKV = 2
HB = 8
QK_NOPE = 128
QK_ROPE = 64
V_DIM = 128
KV_RANK = 512
QK_B = QK_NOPE + QK_ROPE
W_QA = HA * HEAD_DIM
W_KA = HKV * HEAD_DIM
W_QB = HB * QK_B
W_IN = W_QA + 2 * W_KA + W_QB + KV_RANK + QK_ROPE
N_CHIPS = 4
N_DEV = 8

LANE = 128
O_QA = 0
O_KA = O_QA + W_QA
O_VA = O_KA + W_KA
O_QN = O_VA + W_KA
O_QP = O_QN + HB * QK_NOPE
O_CKV = O_QP + HB * LANE
O_KPE = O_CKV + KV_RANK
W_INP = O_KPE + LANE

ADAM_LR = 0.001
ADAM_B1 = 0.9
ADAM_B2 = 0.999
ADAM_EPS = 1e-08
ADAM_WD = 0.01
ADAM_STEP = 10

VMEM_LIMIT = 56 * 1024 * 1024
HEADS_PER_STEP = 2


def _params(sem):
    return pltpu.CompilerParams(dimension_semantics=sem, vmem_limit_bytes=VMEM_LIMIT)


def _pick(dim, target, unit):
    best = None
    t = unit
    while t <= min(dim, target):
        if dim % t == 0:
            best = t
        t += unit
    return dim if best is None else best


def _rn(x):
    return x * lax.rsqrt(jnp.mean(x * x, axis=-1, keepdims=True) + EPS)


def _rn_parts(x):
    r = lax.rsqrt(jnp.mean(x * x, axis=-1, keepdims=True) + EPS)
    return x * r, r


def _rn_bwd(n, r, dn):
    return r * (dn - n * jnp.mean(dn * n, axis=-1, keepdims=True))


def _rows(t):
    return jnp.sum(t, axis=0, keepdims=True)


def _rowwise(name, fn, rows, vecs, row_outs, vec_outs, tile, after=()):
    S = rows[0].shape[0]
    assert S % tile == 0
    n_r, n_v, n_ro, n_vo, n_a = len(rows), len(vecs), len(row_outs), len(vec_outs), len(after)

    def body(*refs):
        r_in = refs[:n_r]
        v_in = refs[n_r:n_r + n_v]
        r_out = refs[n_r + n_v + n_a:n_r + n_v + n_a + n_ro]
        v_out = refs[n_r + n_v + n_a + n_ro:]
        ro, vo = fn(*[r[...] for r in r_in], *[v[...] for v in v_in])
        for ref, val in zip(r_out, ro):
            ref[...] = val.astype(ref.dtype)
        if n_vo:
            @pl.when(pl.program_id(0) == 0)
            def _():
                for ref in v_out:
                    ref[...] = jnp.zeros(ref.shape, ref.dtype)
            for ref, val in zip(v_out, vo):
                ref[...] += val

    in_specs = [pl.BlockSpec((tile, a.shape[1]), lambda i: (i, 0)) for a in rows]
    in_specs += [pl.BlockSpec(a.shape, lambda i: (0, 0)) for a in vecs]
    in_specs += [pl.BlockSpec(memory_space=pl.ANY) for _ in after]
    out_specs = [pl.BlockSpec((tile, w), lambda i: (i, 0)) for w, _ in row_outs]
    out_specs += [pl.BlockSpec((1, w), lambda i: (0, 0)) for w in vec_outs]
    out_shape = [jax.ShapeDtypeStruct((S, w), dt) for w, dt in row_outs]
    out_shape += [jax.ShapeDtypeStruct((1, w), F32) for w in vec_outs]
    res = pl.pallas_call(
        body, name=name, grid=(S // tile,), in_specs=in_specs, out_specs=out_specs, out_shape=out_shape,
        compiler_params=_params(("arbitrary",)),
    )(*rows, *vecs, *after)
    return res[:n_ro], res[n_ro:]


CONTRACT_TILE = 2048
MM_TILE_BYTES = 40 * 1024 * 1024


def _rows_that_fit(M, tile_bytes, most=1024):
    t = _pick(M, most, 16)
    while t % 32 == 0 and tile_bytes(t) > MM_TILE_BYTES:
        t //= 2
    return t


def _accumulate(acc, part, step, n_steps, finish):
    if n_steps == 1:
        finish(part)
        return

    @pl.when(step == 0)
    def _():
        acc[...] = part

    @pl.when((step > 0) & (step < n_steps - 1))
    def _():
        acc[...] += part

    @pl.when(step == n_steps - 1)
    def _():
        finish(acc[...] + part)


def _mm_nn(name, a, b3, out_dtypes, epilogue=None, extras=(), after=()):
    M, K = a.shape
    nB, K2, Nsh = b3.shape
    assert K == K2
    N = nB * Nsh
    tn, tk = _pick(Nsh, 512, LANE), _pick(K, CONTRACT_TILE, LANE)
    out_bytes = sum(jnp.dtype(dt).itemsize for dt in out_dtypes) + sum(e.dtype.itemsize for e in extras)
    tm = _rows_that_fit(M, lambda t: 2 * (t * tk * a.dtype.itemsize + tk * tn * b3.dtype.itemsize)
                        + t * tn * (4 * (K > tk) + 2 * out_bytes), most=2048)
    nps, nk, ne, no, na = Nsh // tn, K // tk, len(extras), len(out_dtypes), len(after)

    def body(a_ref, b_ref, *rest):
        e_refs, o_refs = rest[:ne], rest[ne + na:ne + na + no]

        def finish(total):
            vals = (total,) if epilogue is None else epilogue(total, *[e[...] for e in e_refs])
            for ref, val in zip(o_refs, vals):
                ref[...] = val.astype(ref.dtype)

        part = jnp.dot(a_ref[...].astype(BF16), b_ref[...].astype(BF16), preferred_element_type=F32)
        _accumulate(rest[-1], part, pl.program_id(2), nk, finish)

    in_specs = [pl.BlockSpec((tm, tk), lambda i, j, k: (i, k)),
                pl.BlockSpec((None, tk, tn), lambda i, j, k: (j // nps, k, j % nps))]
    in_specs += [pl.BlockSpec((tm, tn), lambda i, j, k: (i, j)) for _ in extras]
    in_specs += [pl.BlockSpec(memory_space=pl.ANY) for _ in after]
    return pl.pallas_call(
        body, name=name, grid=(M // tm, N // tn, nk), in_specs=in_specs,
        out_specs=[pl.BlockSpec((tm, tn), lambda i, j, k: (i, j)) for _ in out_dtypes],
        out_shape=[jax.ShapeDtypeStruct((M, N), dt) for dt in out_dtypes],
        scratch_shapes=[pltpu.VMEM((tm, tn), F32)] if nk > 1 else [],
        compiler_params=_params(("parallel", "parallel", "arbitrary")),
    )(a, b3, *extras, *after)


def _mm_nt(name, a, b3, out_dtypes, epilogue=None, extras=(), after=()):
    M, N = a.shape
    nB, K, Nsh = b3.shape
    assert N == nB * Nsh
    to, tn = _pick(K, 1536, LANE), _pick(Nsh, CONTRACT_TILE, LANE)
    out_bytes = sum(jnp.dtype(dt).itemsize for dt in out_dtypes) + sum(e.dtype.itemsize for e in extras)
    tm = _rows_that_fit(M, lambda t: 2 * (t * tn * a.dtype.itemsize + to * tn * b3.dtype.itemsize)
                        + t * to * (4 * (N > tn) + 2 * out_bytes))
    nps, nn, ne, no, na = Nsh // tn, N // tn, len(extras), len(out_dtypes), len(after)

    def body(a_ref, b_ref, *rest):
        e_refs, o_refs = rest[:ne], rest[ne + na:ne + na + no]

        def finish(total):
            vals = (total,) if epilogue is None else epilogue(total, *[e[...] for e in e_refs])
            for ref, val in zip(o_refs, vals):
                ref[...] = val.astype(ref.dtype)

        part = lax.dot_general(a_ref[...].astype(BF16), b_ref[...].astype(BF16),
                               (((1,), (1,)), ((), ())), preferred_element_type=F32)
        _accumulate(rest[-1], part, pl.program_id(2), nn, finish)

    in_specs = [pl.BlockSpec((tm, tn), lambda i, o, n: (i, n)),
                pl.BlockSpec((None, to, tn), lambda i, o, n: (n // nps, o, n % nps))]
    in_specs += [pl.BlockSpec((tm, to), lambda i, o, n: (i, o)) for _ in extras]
    in_specs += [pl.BlockSpec(memory_space=pl.ANY) for _ in after]
    return pl.pallas_call(
        body, name=name, grid=(M // tm, K // to, nn), in_specs=in_specs,
        out_specs=[pl.BlockSpec((tm, to), lambda i, o, n: (i, o)) for _ in out_dtypes],
        out_shape=[jax.ShapeDtypeStruct((M, K), dt) for dt in out_dtypes],
        scratch_shapes=[pltpu.VMEM((tm, to), F32)] if nn > 1 else [],
        compiler_params=_params(("parallel", "parallel", "arbitrary")),
    )(a, b3, *extras, *after)


def _mm_tn(name, a, g, nB, after=()):
    M, K = a.shape
    M2, N = g.shape
    assert M == M2 and N % nB == 0
    Nsh = N // nB
    tk, tn, tm = _pick(K, 1536, LANE), _pick(Nsh, 1024, LANE), _pick(M, CONTRACT_TILE, 16)
    nps, nm = Nsh // tn, M // tm

    def body(a_ref, g_ref, *rest):
        o_ref = rest[len(after)]

        def finish(total):
            o_ref[...] = total

        part = lax.dot_general(a_ref[...].astype(BF16), g_ref[...].astype(BF16),
                               (((0,), (0,)), ((), ())), preferred_element_type=F32)
        _accumulate(rest[-1], part, pl.program_id(2), nm, finish)

    return pl.pallas_call(
        body, name=name, grid=(K // tk, N // tn, nm),
        in_specs=[pl.BlockSpec((tm, tk), lambda k, n, m: (m, k)),
                  pl.BlockSpec((tm, tn), lambda k, n, m: (m, n))] + [pl.BlockSpec(memory_space=pl.ANY) for _ in after],
        out_specs=pl.BlockSpec((None, tk, tn), lambda k, n, m: (n // nps, k, n % nps)),
        out_shape=jax.ShapeDtypeStruct((nB, K, Nsh), F32),
        scratch_shapes=[pltpu.VMEM((tk, tn), F32)] if nm > 1 else [],
        compiler_params=_params(("parallel", "parallel", "arbitrary")),
    )(a, g, *after)


def _rows_mm(name, fn, rows, vecs, b3, after=()):
    S = rows[0].shape[0]
    _, K, N = b3.shape
    tm, tn = _pick(S, 1024, 16), _pick(N, 512, LANE)
    nr, nv, na = len(rows), len(vecs), len(after)

    def body(*refs):
        b_ref = refs[nr + nv]
        a_out, o_ref, a_scr = refs[nr + nv + 1 + na:]

        @pl.when(pl.program_id(1) == 0)
        def _():
            a = fn(*[r[...] for r in refs[:nr]], *[v[...] for v in refs[nr:nr + nv]]).astype(BF16)
            a_scr[...] = a
            a_out[...] = a

        o_ref[...] = jnp.dot(a_scr[...], b_ref[...], preferred_element_type=F32)

    in_specs = [pl.BlockSpec((tm, r.shape[1]), lambda i, j: (i, 0)) for r in rows]
    in_specs += [pl.BlockSpec(v.shape, lambda i, j: (0, 0)) for v in vecs]
    in_specs += [pl.BlockSpec((None, K, tn), lambda i, j: (0, 0, j))]
    in_specs += [pl.BlockSpec(memory_space=pl.ANY) for _ in after]
    return pl.pallas_call(
        body, name=name, grid=(S // tm, N // tn), in_specs=in_specs,
        out_specs=[pl.BlockSpec((tm, K), lambda i, j: (i, 0)), pl.BlockSpec((tm, tn), lambda i, j: (i, j))],
        out_shape=[jax.ShapeDtypeStruct((S, K), BF16), jax.ShapeDtypeStruct((S, N), F32)],
        scratch_shapes=[pltpu.VMEM((tm, K), BF16)],
        compiler_params=_params(("parallel", "arbitrary")),
    )(*rows, *vecs, b3, *after)


def _kv_mm(name, ckvn, w3, kpe):
    S, K = ckvn.shape
    nB, _, Nsh = w3.shape
    heads = Nsh // (2 * LANE)
    tm = _pick(S, 1024, 16)

    def body(a_ref, b_ref, kpe_ref, kc_ref, v_ref):
        acc = jnp.dot(a_ref[...], b_ref[...], preferred_element_type=F32)
        for h in range(heads):
            kc_ref[:, 2 * h * LANE:(2 * h + 1) * LANE] = acc[:, 2 * h * LANE:(2 * h + 1) * LANE].astype(BF16)
            kc_ref[:, (2 * h + 1) * LANE:(2 * h + 2) * LANE] = kpe_ref[...]
            v_ref[:, h * LANE:(h + 1) * LANE] = acc[:, (2 * h + 1) * LANE:(2 * h + 2) * LANE].astype(BF16)

    return pl.pallas_call(
        body, name=name, grid=(S // tm, nB),
        in_specs=[pl.BlockSpec((tm, K), lambda i, j: (i, 0)), pl.BlockSpec((None, K, Nsh), lambda i, j: (j, 0, 0)),
                  pl.BlockSpec((tm, LANE), lambda i, j: (i, 0))],
        out_specs=[pl.BlockSpec((tm, Nsh), lambda i, j: (i, j)), pl.BlockSpec((tm, Nsh // 2), lambda i, j: (i, j))],
        out_shape=[jax.ShapeDtypeStruct((S, nB * Nsh), BF16), jax.ShapeDtypeStruct((S, nB * Nsh // 2), BF16)],
        compiler_params=_params(("parallel", "arbitrary")),
    )(ckvn, w3, kpe)


def _kv_mm_bwd(name, dkc, dv, w3):
    S = dkc.shape[0]
    nB, K, Nsh = w3.shape
    heads = Nsh // (2 * LANE)
    tm = _pick(S, 1024, 16)

    def body(dkc_ref, dv_ref, b_ref, dck_ref, dkv_ref, dkpe_ref):
        j = pl.program_id(1)
        pieces, pe = [], None
        for h in range(heads):
            pieces += [dkc_ref[:, 2 * h * LANE:(2 * h + 1) * LANE], dv_ref[:, h * LANE:(h + 1) * LANE]]
            piece = dkc_ref[:, (2 * h + 1) * LANE:(2 * h + 2) * LANE]
            pe = piece if pe is None else pe + piece
        dkv = jnp.concatenate(pieces, axis=1).astype(BF16)
        dkv_ref[...] = dkv
        part = lax.dot_general(dkv, b_ref[...], (((1,), (1,)), ((), ())), preferred_element_type=F32)

        @pl.when(j == 0)
        def _():
            dck_ref[...] = part
            dkpe_ref[...] = pe

        @pl.when(j > 0)
        def _():
            dck_ref[...] += part
            dkpe_ref[...] += pe

    return pl.pallas_call(
        body, name=name, grid=(S // tm, nB),
        in_specs=[pl.BlockSpec((tm, Nsh), lambda i, j: (i, j)), pl.BlockSpec((tm, Nsh // 2), lambda i, j: (i, j)),
                  pl.BlockSpec((None, K, Nsh), lambda i, j: (j, 0, 0))],
        out_specs=[pl.BlockSpec((tm, K), lambda i, j: (i, 0)), pl.BlockSpec((tm, Nsh), lambda i, j: (i, j)),
                   pl.BlockSpec((tm, LANE), lambda i, j: (i, 0))],
        out_shape=[jax.ShapeDtypeStruct((S, K), F32), jax.ShapeDtypeStruct((S, nB * Nsh), BF16),
                   jax.ShapeDtypeStruct((S, LANE), F32)],
        compiler_params=_params(("parallel", "arbitrary")),
    )(dkc, dv, w3)


def _mm_tn_rows(name, a, g, n_rows, pieces):
    M, K = a.shape
    M2, N = g.shape
    assert M == M2
    tk, tn, tm = _pick(K, 1536, LANE), _pick(N, 512, LANE), _pick(M, CONTRACT_TILE, 16)
    nk, nm = K // tk, M // tm

    def body(a_ref, g_ref, o_ref, *acc):
        kk = pl.program_id(1)

        def finish(total):
            for t in range(nk):
                @pl.when(kk == t)
                def _(t=t):
                    for src, dst, n in pieces:
                        lo, hi = max(src, t * tk), min(src + n, (t + 1) * tk)
                        if lo < hi:
                            o_ref[dst + lo - src:dst + hi - src, :] = total[lo - t * tk:hi - t * tk, :]

        part = lax.dot_general(a_ref[...].astype(BF16), g_ref[...].astype(BF16),
                               (((0,), (0,)), ((), ())), preferred_element_type=F32)
        _accumulate(acc[0] if acc else None, part, pl.program_id(2), nm, finish)

    return pl.pallas_call(
        body, name=name, grid=(N // tn, nk, nm),
        in_specs=[pl.BlockSpec((tm, tk), lambda n, k, m: (m, k)),
                  pl.BlockSpec((tm, tn), lambda n, k, m: (m, n))],
        out_specs=pl.BlockSpec((n_rows, tn), lambda n, k, m: (0, n)),
        out_shape=jax.ShapeDtypeStruct((n_rows, N), F32),
        scratch_shapes=[pltpu.VMEM((tk, tn), F32)] if nm > 1 else [],
        compiler_params=_params(("arbitrary", "arbitrary", "arbitrary")),
    )(a, g)


def _attn_fwd(name, q, k, v, n_heads, group, dqk, dv, scale, tq, after=()):
    S = q.shape[0]
    hp = 2 * HEADS_PER_STEP
    kvp = max(1, hp // group)

    tr = tq // 2

    def body(q_ref, k_ref, v_ref, *rest):
        o_ref, lse_ref = rest[-2:]
        for j in range(hp):
            jk = j * kvp // hp
            for r0 in range(0, tq, tr):
                rows = slice(r0, r0 + tr)
                s = lax.dot_general(q_ref[rows, j * dqk:(j + 1) * dqk], k_ref[:, jk * dqk:(jk + 1) * dqk],
                                    (((1,), (1,)), ((), ())), preferred_element_type=F32)
                m = jnp.max(s, axis=-1, keepdims=True)
                e = jnp.exp((s - m) * scale)
                l = jnp.sum(e, axis=-1, keepdims=True)
                o_ref[rows, j * dv:(j + 1) * dv] = jnp.dot(e.astype(BF16), v_ref[:, jk * dv:(jk + 1) * dv],
                                                           preferred_element_type=F32) * (1.0 / l)
                lse_ref[rows, j * LANE:(j + 1) * LANE] = jnp.broadcast_to(m * scale + jnp.log(l), (tr, LANE))

    kv_block = lambda h, i: (0, (h * hp // group) // kvp)
    return pl.pallas_call(
        body, name=name, grid=(n_heads // hp, S // tq),
        in_specs=[pl.BlockSpec((tq, hp * dqk), lambda h, i: (i, h)),
                  pl.BlockSpec((S, kvp * dqk), kv_block),
                  pl.BlockSpec((S, kvp * dv), kv_block)] + [pl.BlockSpec(memory_space=pl.ANY) for _ in after],
        out_specs=[pl.BlockSpec((tq, hp * dv), lambda h, i: (i, h)),
                   pl.BlockSpec((tq, hp * LANE), lambda h, i: (i, h))],
        out_shape=[jax.ShapeDtypeStruct((S, n_heads * dv), F32), jax.ShapeDtypeStruct((S, n_heads * LANE), F32)],
        compiler_params=_params(("arbitrary", "arbitrary")),
    )(q, k, v, *after)


def _attn_bwd(name, q, k, v, do, o, lse, n_heads, group, dqk, dv, scale, tq, after=()):
    S = q.shape[0]
    n_kv = n_heads // group
    hp, tq = (group, tq // 2) if group > HEADS_PER_STEP else (HEADS_PER_STEP, tq)
    kvp = max(1, hp // group)
    chains = 4
    tr = tq * hp // chains

    def body(q_ref, k_ref, v_ref, do_ref, o_ref, lse_ref, *rest):
        dq_ref, dk_ref, dv_ref = rest[len(after):]
        h, i = pl.program_id(0), pl.program_id(1)

        @pl.when(((h * hp) % group == 0) & (i == 0))
        def _():
            dk_ref[...] = jnp.zeros(dk_ref.shape, F32)
            dv_ref[...] = jnp.zeros(dv_ref.shape, F32)

        for j in range(hp):
            jk = j * kvp // hp
            kb, vb = k_ref[:, jk * dqk:(jk + 1) * dqk], v_ref[:, jk * dv:(jk + 1) * dv]
            for r0 in range(0, tq, tr):
                rows = slice(r0, r0 + tr)
                qb, dob = q_ref[rows, j * dqk:(j + 1) * dqk], do_ref[rows, j * dv:(j + 1) * dv]
                s = lax.dot_general(qb, kb, (((1,), (1,)), ((), ())), preferred_element_type=F32)
                p = jnp.exp(s * scale - jnp.tile(lse_ref[rows, j * LANE:(j + 1) * LANE], (1, S // LANE)))
                dp = lax.dot_general(dob, vb, (((1,), (1,)), ((), ())), preferred_element_type=F32)
                delta = jnp.sum(dob.astype(F32) * o_ref[rows, j * dv:(j + 1) * dv], axis=-1, keepdims=True)
                ds = (p * ((dp - delta) * scale)).astype(BF16)
                dv_ref[:, jk * dv:(jk + 1) * dv] += lax.dot_general(p.astype(BF16), dob, (((0,), (0,)), ((), ())),
                                                                    preferred_element_type=F32)
                dq_ref[rows, j * dqk:(j + 1) * dqk] = jnp.dot(ds, kb, preferred_element_type=F32)
                dk_ref[:, jk * dqk:(jk + 1) * dqk] += lax.dot_general(ds, qb, (((0,), (0,)), ((), ())),
                                                                      preferred_element_type=F32)

    kv_block = lambda h, i: (0, (h * hp // group) // kvp)
    return pl.pallas_call(
        body, name=name, grid=(n_heads // hp, S // tq),
        in_specs=[pl.BlockSpec((tq, hp * dqk), lambda h, i: (i, h)),
                  pl.BlockSpec((S, kvp * dqk), kv_block),
                  pl.BlockSpec((S, kvp * dv), kv_block),
                  pl.BlockSpec((tq, hp * dv), lambda h, i: (i, h)),
                  pl.BlockSpec((tq, hp * dv), lambda h, i: (i, h)),
                  pl.BlockSpec((tq, hp * LANE), lambda h, i: (i, h))] + [pl.BlockSpec(memory_space=pl.ANY) for _ in after],
        out_specs=[pl.BlockSpec((tq, hp * dqk), lambda h, i: (i, h)),
                   pl.BlockSpec((S, kvp * dqk), kv_block),
                   pl.BlockSpec((S, kvp * dv), kv_block)],
        out_shape=[jax.ShapeDtypeStruct((S, n_heads * dqk), F32),
                   jax.ShapeDtypeStruct((S, n_kv * dqk), F32),
                   jax.ShapeDtypeStruct((S, n_kv * dv), F32)],
        compiler_params=_params(("arbitrary", "arbitrary")),
    )(q, k, v, do, o, lse, *after)


def _adamw_step(w, gg, m, v):
    nm = ADAM_B1 * m + (1.0 - ADAM_B1) * gg
    nv = ADAM_B2 * v + (1.0 - ADAM_B2) * (gg * gg)
    c1, c2 = 1.0 - ADAM_B1 ** ADAM_STEP, 1.0 - ADAM_B2 ** ADAM_STEP
    return -ADAM_LR * ((nm / c1) / (jnp.sqrt(nv / c2) + ADAM_EPS) + ADAM_WD * w), nm, nv


def _adamw(name, w, g, m, v, after=(), rewrite_g=False):
    R, C = w.shape
    tile = _pick(R, max(8, (1 << 19) // C // 8 * 8), 8)
    n_out = 4 if rewrite_g else 3

    def body(w_ref, g_ref, m_ref, v_ref, *rest):
        d_ref, nm_ref, nv_ref = rest[len(after):len(after) + 3]
        gg = g_ref[...]
        d_ref[...], nm_ref[...], nv_ref[...] = _adamw_step(w_ref[...], gg, m_ref[...], v_ref[...])
        if rewrite_g:
            rest[-1][...] = gg

    spec = pl.BlockSpec((tile, C), lambda i: (i, 0))
    return pl.pallas_call(
        body, name=name, grid=(R // tile,), in_specs=[spec] * 4 + [pl.BlockSpec(memory_space=pl.ANY) for _ in after],
        out_specs=[spec] * n_out, out_shape=[jax.ShapeDtypeStruct((R, C), F32)] * n_out,
        compiler_params=_params(("parallel",)),
    )(w, g, m, v, *after)


def _adamw_outer(name, w, left, right, m, v, after=()):
    R, C = w.shape
    Kc = left.shape[0]
    tile = _pick(R, max(LANE, (1 << 19) // C // LANE * LANE), LANE)

    def body(w_ref, l_ref, r_ref, m_ref, v_ref, *rest):
        d_ref, nm_ref, nv_ref, g_ref = rest[len(after):]
        gg = lax.dot_general(l_ref[...], r_ref[...], (((0,), (0,)), ((), ())), preferred_element_type=F32)
        d_ref[...], nm_ref[...], nv_ref[...] = _adamw_step(w_ref[...], gg, m_ref[...], v_ref[...])
        g_ref[...] = gg

    spec = pl.BlockSpec((tile, C), lambda i: (i, 0))
    return pl.pallas_call(
        body, name=name, grid=(R // tile,),
        in_specs=[spec, pl.BlockSpec((Kc, tile), lambda i: (0, i)), pl.BlockSpec((Kc, C), lambda i: (0, 0)), spec, spec]
        + [pl.BlockSpec(memory_space=pl.ANY) for _ in after],
        out_specs=[spec] * 4, out_shape=[jax.ShapeDtypeStruct((R, C), F32)] * 4,
        compiler_params=_params(("parallel",)),
    )(w, left, right, m, v, *after)


def _flip(me, rel):
    return tuple(1 - p if r else p for p, r in zip(me, rel))


def _push(name, ins, out_shapes, transfers, aliases=None):
    ni, no, nt = len(ins), len(out_shapes), len(transfers)

    def body(*refs):
        in_refs, out_refs = refs[:ni], refs[ni:ni + no]
        send_sems, recv_sems = refs[ni + no], refs[ni + no + 1]
        me = (lax.axis_index("x"), lax.axis_index("y"), lax.axis_index("c"))
        copies = []
        for t, (si, sfn, oi, dfn, rel) in enumerate(transfers):
            src_ref = out_refs[si[1]] if isinstance(si, tuple) else in_refs[si]
            src = src_ref.at[sfn(*me)]
            dst = out_refs[oi].at[dfn(*me)]
            if rel is None:
                cp = pltpu.make_async_copy(src, dst, send_sems.at[t])
            else:
                cp = pltpu.make_async_remote_copy(src_ref=src, dst_ref=dst, send_sem=send_sems.at[t],
                                                  recv_sem=recv_sems.at[t], device_id=_flip(me, rel),
                                                  device_id_type=MESH)
            cp.start()
            copies.append(cp)
        for cp in copies:
            cp.wait()

    any_spec = pl.BlockSpec(memory_space=pl.ANY)
    return pl.pallas_call(
        body, name=name, in_specs=[any_spec] * ni, out_specs=[any_spec] * no, out_shape=out_shapes,
        scratch_shapes=[pltpu.SemaphoreType.DMA((nt,)), pltpu.SemaphoreType.DMA((nt,))],
        input_output_aliases=aliases or {},
        compiler_params=pltpu.CompilerParams(has_side_effects=True),
    )(*ins)


_HBM = pl.BlockSpec(memory_space=pltpu.HBM)
_SEM = pl.BlockSpec(memory_space=pltpu.SEMAPHORE)
_DATAFLOW = pltpu.SideEffectType.DATAFLOW_SIDE_EFFECTING


def _split_copies(refs, transfers, send_sems, recv_sems):
    me = (lax.axis_index("x"), lax.axis_index("y"), lax.axis_index("c"))
    return [pltpu.make_async_remote_copy(src_ref=refs[sb].at[sfn(*me)], dst_ref=refs[db].at[dfn(*me)],
                                         send_sem=send_sems.at[t], recv_sem=recv_sems.at[t],
                                         device_id=_flip(me, rel), device_id_type=MESH)
            for t, (sb, sfn, db, dfn, rel) in enumerate(transfers)]


def _push_start(name, bufs, transfers, after=()):
    nb, na, nt = len(bufs), len(after), len(transfers)

    def body(*refs):
        send_sems, recv_sems, token = refs[nb + na], refs[nb + na + 1], refs[-1]
        for cp in _split_copies(refs[nb + na + 2:2 * nb + na + 2], transfers, send_sems, recv_sems):
            cp.start()
        token[...] = jnp.zeros(token.shape, token.dtype)

    res = pl.pallas_call(
        body, name=name,
        out_shape=(pltpu.SemaphoreType.DMA((nt,)), pltpu.SemaphoreType.DMA((nt,)),
                   *[pltpu.HBM(b.shape, b.dtype) for b in bufs], jax.ShapeDtypeStruct((8, LANE), F32)),
        in_specs=[_HBM] * nb + [pl.BlockSpec(memory_space=pl.ANY)] * na,
        out_specs=(_SEM, _SEM, *[_HBM] * nb, pl.BlockSpec(memory_space=pltpu.VMEM)),
        input_output_aliases={i: i + 2 for i in range(nb)},
        compiler_params=pltpu.CompilerParams(has_side_effects=_DATAFLOW),
    )(*[pltpu.with_memory_space_constraint(b, pltpu.HBM) for b in bufs], *after)
    return res[0], res[1], list(res[2:2 + nb]), res[-1]


def _push_wait(name, send_sems, recv_sems, bufs, transfers, after):
    nb = len(bufs)

    def body(*refs):
        for cp in _split_copies(refs[:nb], transfers, refs[nb], refs[nb + 1]):
            cp.wait_send()
            cp.wait_recv()

    res = pl.pallas_call(
        body, name=name, out_shape=[pltpu.HBM(b.shape, b.dtype) for b in bufs],
        in_specs=[_HBM] * nb + [_SEM, _SEM, pl.BlockSpec(memory_space=pl.ANY)], out_specs=[_HBM] * nb,
        input_output_aliases={i: i for i in range(nb)},
        compiler_params=pltpu.CompilerParams(has_side_effects=_DATAFLOW),
    )(*bufs, send_sems, recv_sems, after)
    return list(res)


ICI_RELS = ((1, 0, 0), (0, 1, 0), (1, 1, 0))
SIBLING = (0, 0, 1)


def _chip(x, y):
    return 2 * x + y


def _cast_into_slot(name, where, w, after=()):
    R, C = w.shape
    tile = _pick(R, 512, BF16_ROWS)

    def body(where_ref, w_ref, *rest):
        rest[-1][...] = w_ref[...].astype(BF16)

    return pl.pallas_call(
        body, name=name,
        grid_spec=pltpu.PrefetchScalarGridSpec(
            num_scalar_prefetch=1, grid=(R // tile,),
            in_specs=[pl.BlockSpec((tile, C), lambda i, wh: (i, 0))] + [pl.BlockSpec(memory_space=pl.ANY) for _ in after],
            out_specs=pl.BlockSpec((None, tile, C), lambda i, wh: (wh[1], i, 0))),
        out_shape=jax.ShapeDtypeStruct((N_CHIPS, R, C), BF16),
        compiler_params=_params(("parallel",)),
    )(where, w, *after)


BF16_ROWS = 16


def _rows_split(shape2):
    return (shape2[0] // 2) % BF16_ROWS == 0


def _half_shape(shape2):
    R, C = shape2
    return (R // 2, C) if _rows_split(shape2) else (R, C // 2)


def _half(shape2, c):
    R, C = shape2
    if _rows_split(shape2):
        return (pl.ds(c * (R // 2), R // 2), slice(None))
    return (slice(None), pl.ds(pl.multiple_of(c * (C // 2), LANE), C // 2))


def _gather_casts(tag, where, shards, after=()):
    return [_cast_into_slot(f"cast_w_{tag}{w}", where, s, after) for w, s in enumerate(shards)]


def _gather_start(tag, own, after):
    transfers = []
    for w in range(len(own)):
        def idx(x, y, c, shape2=own[w].shape[1:]):
            return (_chip(x, y), *_half(shape2, c))
        transfers += [(w, idx, w, idx, rel) for rel in ICI_RELS]
    send_sems, recv_sems, bufs, token = _push_start(f"gather_{tag}_start", own, transfers, after)
    return (tag, send_sems, recv_sems, bufs, transfers), token


def _forward_transfers(part):
    second = []
    for w, p in enumerate(part):
        for rel in ICI_RELS:
            def idx(x, y, c, shape2=p.shape[1:], rel=rel):
                return (_chip(x ^ rel[0], y ^ rel[1]), *_half(shape2, c))
            second.append((w, idx, w, idx, SIBLING))
    return second


def _gather_finish(handle, after):
    tag, send_sems, recv_sems, bufs, transfers = handle
    part = _push_wait(f"gather_{tag}_wait", send_sems, recv_sems, bufs, transfers, after)
    outs = [jax.ShapeDtypeStruct(p.shape, p.dtype) for p in part]
    second = [(("out", sb), sfn, db, dfn, rel) for sb, sfn, db, dfn, rel in _forward_transfers(part)]
    return _push(f"gather_{tag}_d2d", list(part), outs, second, aliases={w: w for w in range(len(part))})


def _gather_mid(handle, after):
    tag, send_sems, recv_sems, bufs, transfers = handle
    part = _push_wait(f"gather_{tag}_wait", send_sems, recv_sems, bufs, transfers, after)
    second = _forward_transfers(part)
    s2, r2, bufs2, token = _push_start(f"gather_{tag}_d2d_start", part, second)
    return (tag, s2, r2, bufs2, second), token


def _gather_end(handle, after):
    tag, send_sems, recv_sems, bufs, transfers = handle
    return _push_wait(f"gather_{tag}_d2d_wait", send_sems, recv_sems, bufs, transfers, after)


def _reduce_begin(tag, grads, after=()):
    n = len(grads)
    land = [lax.empty((N_CHIPS, *_half_shape(g.shape[1:])), F32) for g in grads]
    t1 = []
    for w in range(n):
        t1.append((w, lambda x, y, c, shape2=grads[w].shape[1:]: (slice(None), *_half(shape2, 1 - c)), n + w,
                   lambda x, y, c: (slice(None), slice(None), slice(None)), SIBLING))
    send_sems, recv_sems, bufs, token = _push_start(f"reduce_{tag}_d2d_start", list(grads) + land, t1, after)
    return (tag, send_sems, recv_sems, bufs, t1), token


def _reduce_start(handle, where, after, after_start=()):
    tag, send_sems, recv_sems, bufs, t1 = handle
    n = len(bufs) // 2
    res = _push_wait(f"reduce_{tag}_d2d_wait", send_sems, recv_sems, bufs, t1, after)
    grads, from_sib = res[:n], res[n:]
    pair = [_pair_sum(f"pair_sum_{tag}{w}", where, grads[w], from_sib[w]) for w in range(n)]
    land = [lax.empty((3, *_half_shape(g.shape[1:])), BF16) for g in grads]
    t3 = []
    for w in range(n):
        for r, rel in enumerate(ICI_RELS):
            t3.append((w, lambda x, y, c, rel=rel: (_chip(x ^ rel[0], y ^ rel[1]), slice(None), slice(None)), n + w,
                       lambda x, y, c, r=r: (r, slice(None), slice(None)), rel))
    send_sems, recv_sems, bufs, token = _push_start(f"reduce_{tag}_start", pair + land, t3, after_start)
    return (tag, send_sems, recv_sems, bufs, t3, list(grads), list(from_sib)), token


def _reduce_finish(handle, where, after):
    tag, send_sems, recv_sems, bufs, t3, grads, from_sib = handle
    n = len(grads)
    from_chips = _push_wait(f"reduce_{tag}_wait", send_sems, recv_sems, bufs, t3, after)[n:]
    mine = [_chip_sum(f"chip_sum_{tag}{w}", where, grads[w], from_sib[w], from_chips[w]) for w in range(n)]
    t5 = []
    for w in range(n):
        def idx(x, y, c, shape2=grads[w].shape[1:]):
            return _half(shape2, c)
        t5.append((w, idx, w, idx, SIBLING))
    send_sems, recv_sems, bufs, token = _push_start(f"reduce_{tag}_swap_start", mine, t5)
    return (tag, send_sems, recv_sems, bufs, t5), token


def _reduce_end(handle, after):
    tag, send_sems, recv_sems, bufs, t5 = handle
    return _push_wait(f"reduce_{tag}_swap_wait", send_sems, recv_sems, bufs, t5, after)


def _pair_sum(name, where, g, from_sib):
    _, hr, hc = from_sib.shape
    tile = _pick(hr, max(16, (1 << 19) // hc // 16 * 16), 16)
    nb = hr // tile
    slot = lambda k, wh: (wh[1] + 1 + k) % N_CHIPS
    if _rows_split(g.shape[1:]):
        mine = lambda k, i, wh: (slot(k, wh), wh[0] * nb + i, 0)
    else:
        mine = lambda k, i, wh: (slot(k, wh), i, wh[0])

    def body(where_ref, g_ref, s_ref, o_ref):
        o_ref[...] = (g_ref[...] + s_ref[...]).astype(BF16)

    return pl.pallas_call(
        body, name=name,
        grid_spec=pltpu.PrefetchScalarGridSpec(
            num_scalar_prefetch=1, grid=(N_CHIPS - 1, nb),
            in_specs=[pl.BlockSpec((None, tile, hc), mine),
                      pl.BlockSpec((None, tile, hc), lambda k, i, wh: (slot(k, wh), i, 0))],
            out_specs=pl.BlockSpec((None, tile, hc), lambda k, i, wh: (slot(k, wh), i, 0))),
        out_shape=jax.ShapeDtypeStruct((N_CHIPS, hr, hc), BF16),
        compiler_params=_params(("parallel", "parallel")),
    )(where, g, from_sib)


def _chip_sum(name, where, g, from_sib, from_chips):
    _, R, C = g.shape
    _, hr, hc = from_sib.shape
    tile = _pick(hr, max(16, (1 << 19) // hc // 16 * 16), 16)
    nb = hr // tile
    if _rows_split((R, C)):
        mine = lambda i, wh: (wh[0] * nb + i, 0)
    else:
        mine = lambda i, wh: (i, wh[0])

    def body(where_ref, g_ref, s_ref, r_ref, o_ref):
        acc = g_ref[...] + s_ref[...]
        for r in range(3):
            acc = acc + r_ref[r].astype(F32)
        o_ref[...] = acc

    return pl.pallas_call(
        body, name=name,
        grid_spec=pltpu.PrefetchScalarGridSpec(
            num_scalar_prefetch=1, grid=(nb,),
            in_specs=[pl.BlockSpec((None, tile, hc), lambda i, wh: (wh[1], *mine(i, wh))),
                      pl.BlockSpec((None, tile, hc), lambda i, wh: (wh[1], i, 0)),
                      pl.BlockSpec((3, tile, hc), lambda i, wh: (0, i, 0))],
            out_specs=pl.BlockSpec((tile, hc), mine)),
        out_shape=jax.ShapeDtypeStruct((R, C), F32),
        compiler_params=_params(("parallel",)),
    )(where, g, from_sib, from_chips)


def _allgather8(name, blk, after=()):
    m_per, n = blk.shape
    rels = [(0, 0, 1), (1, 0, 0), (0, 1, 0), (1, 1, 0), (1, 0, 1), (0, 1, 1), (1, 1, 1)]

    def body(x_ref, *rest):
        out_ref, send_sems, recv_sems = rest[len(after):]
        me = (lax.axis_index("x"), lax.axis_index("y"), lax.axis_index("c"))
        my_rows = out_ref.at[pl.ds((4 * me[0] + 2 * me[1] + me[2]) * m_per, m_per), :]
        out_ref[pl.ds((4 * me[0] + 2 * me[1] + me[2]) * m_per, m_per), :] = x_ref[...]
        copies = []
        for t, rel in enumerate(rels):
            cp = pltpu.make_async_remote_copy(src_ref=x_ref, dst_ref=my_rows, send_sem=send_sems.at[t],
                                              recv_sem=recv_sems.at[t], device_id=_flip(me, rel),
                                              device_id_type=MESH)
            cp.start()
            copies.append(cp)
        for cp in copies:
            cp.wait()

    return pl.pallas_call(
        body, name=name, out_shape=jax.ShapeDtypeStruct((N_DEV * m_per, n), blk.dtype),
        in_specs=[pl.BlockSpec(memory_space=pltpu.VMEM)] + [pl.BlockSpec(memory_space=pl.ANY) for _ in after],
        out_specs=pl.BlockSpec(memory_space=pltpu.VMEM),
        scratch_shapes=[pltpu.SemaphoreType.DMA((7,)), pltpu.SemaphoreType.DMA((7,))],
        compiler_params=pltpu.CompilerParams(has_side_effects=True, vmem_limit_bytes=VMEM_LIMIT),
    )(blk, *after)


def _rope_tables(seq_len, dim, reps):
    rows = seq_len // GRID_W
    t = np.arange(seq_len)
    row, col = (t // GRID_W).astype(np.float32), (t % GRID_W).astype(np.float32)
    half = dim // 2
    inv = jnp.asarray(ROPE_THETA, F32) ** (-jnp.arange(0, half, 2, dtype=F32) / half)
    ang_r = jnp.asarray(row)[:, None] * inv[None, :]
    ang_c = jnp.asarray(col)[:, None] * inv[None, :]
    ang = jnp.concatenate([ang_r, ang_r, ang_c, ang_c], axis=-1)
    cos, sin = jnp.cos(ang), jnp.sin(ang)
    low = (np.arange(dim) % (dim // 2)) < (dim // 4)
    s_lo = jnp.where(jnp.asarray(low)[None, :], -sin, 0.0)
    s_hi = jnp.where(jnp.asarray(low)[None, :], 0.0, sin)
    del rows
    return tuple(jnp.tile(a, (1, reps)) for a in (cos, s_lo, s_hi))


def _make_rope(shift):
    def up(v):
        return pltpu.roll(v, LANE - shift, 1)

    def down(v):
        return pltpu.roll(v, shift, 1)

    @jax.custom_vjp
    def rope(v, cos, s_lo, s_hi):
        return v * cos + up(v) * s_lo + down(v) * s_hi

    def fwd(v, cos, s_lo, s_hi):
        return rope(v, cos, s_lo, s_hi), (cos, s_lo, s_hi)

    def bwd(res, dy):
        cos, s_lo, s_hi = res
        return dy * cos + down(dy * s_lo) + up(dy * s_hi), jnp.zeros_like(cos), jnp.zeros_like(cos), jnp.zeros_like(cos)

    rope.defvjp(fwd, bwd)
    return rope


_rope_a = _make_rope(HEAD_DIM // 4)
_rope_b = _make_rope(QK_ROPE // 4)


def _prep(segs, tabs, g_q, g_k, g_ckv):
    qa, ka, va, qn, qp, ckv, kpe = segs
    ta, tb = tabs
    q_a = jnp.concatenate([_rope_a(_rn(s) * g_q, *ta) for s in qa], axis=1)
    k_a = jnp.concatenate([_rope_a(_rn(s) * g_k, *ta) for s in ka], axis=1)
    cat = []
    for h in range(HB):
        cat += [qn[h], _rope_b(qp[h], *tb)]
    q_cat = jnp.concatenate(cat, axis=1)
    return q_a, k_a, va, q_cat, _rn(ckv) * g_ckv, _rope_b(kpe, *tb)


def _split_proj(proj):
    qa = [proj[:, O_QA + h * LANE:O_QA + (h + 1) * LANE] for h in range(HA)]
    ka = [proj[:, O_KA + h * LANE:O_KA + (h + 1) * LANE] for h in range(HKV)]
    va = proj[:, O_VA:O_QN]
    qn = [proj[:, O_QN + h * LANE:O_QN + (h + 1) * LANE] for h in range(HB)]
    qp = [proj[:, O_QP + h * LANE:O_QP + (h + 1) * LANE] for h in range(HB)]
    return qa, ka, va, qn, qp, proj[:, O_CKV:O_KPE], proj[:, O_KPE:W_INP]


def _permute_w_in(nat):
    K = nat.shape[1]
    qb = nat[W_QA + 2 * W_KA:W_QA + 2 * W_KA + W_QB].reshape(HB, QK_B, K)
    qp = jnp.concatenate([qb[:, QK_NOPE:], jnp.zeros((HB, LANE - QK_ROPE, K), nat.dtype)], axis=1)
    tail = nat[W_QA + 2 * W_KA + W_QB:]
    return jnp.concatenate([nat[:O_QN], qb[:, :QK_NOPE].reshape(HB * QK_NOPE, K), qp.reshape(HB * LANE, K),
                            tail, jnp.zeros((LANE - QK_ROPE, K), nat.dtype)], axis=0)


def _unpermute_pieces():
    pieces = [(0, 0, O_QN)]
    for h in range(HB):
        pieces.append((O_QN + h * QK_NOPE, O_QN + h * QK_B, QK_NOPE))
        pieces.append((O_QP + h * LANE, O_QN + h * QK_B + QK_NOPE, QK_ROPE))
    pieces.append((O_CKV, O_QN + W_QB, KV_RANK))
    pieces.append((O_KPE, O_QN + W_QB + KV_RANK, QK_ROPE))
    return pieces


def kernel(x, c, w_ada, b_ada, g_pre_attn, w_in, g_q_a, g_k_a, g_ckv, w_kv_b, g_out_a, g_out_b, w_out, g_post_attn, g_pre_mlp, w_mlp_in, w_mlp_out, g_post_mlp, loss_target, m_w_ada, m_b_ada, m_g_pre_attn, m_w_in, m_g_q_a, m_g_k_a, m_g_ckv, m_w_kv_b, m_g_out_a, m_g_out_b, m_w_out, m_g_post_attn, m_g_pre_mlp, m_w_mlp_in, m_w_mlp_out, m_g_post_mlp, v_w_ada, v_b_ada, v_g_pre_attn, v_w_in, v_g_q_a, v_g_k_a, v_g_ckv, v_w_kv_b, v_g_out_a, v_g_out_b, v_w_out, v_g_post_attn, v_g_pre_mlp, v_w_mlp_in, v_w_mlp_out, v_g_post_mlp):
    S, D = x.shape[1], x.shape[2]
    x2d, tgt = x[0], loss_target[0]
    ix, iy, ic = lax.axis_index("x"), lax.axis_index("y"), lax.axis_index("c")
    j_me = _chip(ix, iy)
    e_me = 4 * ix + 2 * iy + ic
    T = _pick(S, 512, 8)
    TB = _pick(S, 256, 8)

    where = jnp.stack([ic, j_me]).astype(jnp.int32)
    DFF = w_mlp_out.shape[1] * N_CHIPS

    c_all = _allgather8("gather_c", c.reshape(8, D // 8)).reshape(N_DEV, D)
    w_in_t = jnp.swapaxes(w_in[0], 0, 1)
    h_qkv, tok_a = _gather_start("qkv", _gather_casts("qkv", where, [w_in_t, w_kv_b[0]]), (c_all,))
    later = [_gather_casts(tag, where, [w[0]], (tok_a,))
             for tag, w in (("out", w_out), ("up", w_mlp_in), ("down", w_mlp_out))]

    def silu_fn(cc):
        return (cc * (1.0 / (1.0 + jnp.exp(-cc))),), ()

    (c_act,), _ = _rowwise("silu_c", silu_fn, [c_all], [], [(D, BF16)], [], N_DEV)
    ca_pad = jnp.concatenate([c_act, jnp.zeros_like(c_act)], axis=0)
    n_mod = w_ada.shape[2]
    b_cols = lax.dynamic_slice(b_ada, (0, j_me * n_mod), (1, n_mod))
    (mod_part,) = _mm_nn("mod_mm", ca_pad, w_ada, [F32], epilogue=lambda acc, b: (acc + b,),
                         extras=(jnp.broadcast_to(b_cols, (16, n_mod)),))
    mod_all = _allgather8("gather_mod", mod_part[:8] + tok_a[0, 0],
                          after=[c_[0] for c_ in later]).reshape(N_DEV, 8, n_mod)
    mod = jnp.concatenate([lax.dynamic_slice(mod_all, (2 * j, e_me, 0), (1, 1, n_mod))[0] for j in range(N_CHIPS)],
                          axis=1)

    m_qkv, tok_q = _gather_mid(h_qkv, mod_all)
    h_out, tok_b = _gather_start("out", later[0], (tok_q,))
    h_up, tok_c = _gather_start("up", later[1], (tok_b,))
    h_down, tok_d = _gather_start("down", later[2], (tok_c,))
    mod = mod + tok_d[0, 0]
    sh_a, sc_a, gt_a, sh_m, sc_m, gt_m = [mod[:, i * D:(i + 1) * D] for i in range(6)]

    tabs_a = _rope_tables(S, HEAD_DIM, 1)
    tabs_b = _rope_tables(S, QK_ROPE, LANE // QK_ROPE)

    def pre_attn_fn(xt, g, sc, sh):
        return (_rn(xt) * g * (1.0 + sc) + sh,), ()

    (h_b,), _ = _rowwise("pre_attn", pre_attn_fn, [x2d], [g_pre_attn, sc_a, sh_a], [(D, BF16)], [], T)
    g_in, g_kvb = _gather_end(m_qkv, h_b)
    wpt = _permute_w_in(g_in.reshape(W_IN, D))[None]
    (proj,) = _mm_nt("proj_mm", h_b, wpt, [F32])

    def prep_fn(pt, ca_, sla, sha, cb_, slb, shb, gq, gk, gc):
        return _prep(_split_proj(pt), ((ca_, sla, sha), (cb_, slb, shb)), gq, gk, gc), ()

    (qa_b, ka_b, va_b, qcat_b, ckvn_b, kpe_b), _ = _rowwise(
        "prep", prep_fn, [proj, *tabs_a, *tabs_b], [g_q_a, g_k_a, g_ckv],
        [(W_QA, BF16), (W_KA, BF16), (W_KA, BF16), (2 * HB * LANE, BF16), (KV_RANK, BF16), (LANE, BF16)], [], T)
    kcat_b, vb_b = _kv_mm("kv_mm", ckvn_b, g_kvb, kpe_b)

    TQ = _pick(S, 512, 16)
    sc_a_ = 1.0 / math.sqrt(HEAD_DIM)
    sc_b_ = 1.0 / math.sqrt(QK_B)
    o_a, lse_a = _attn_fwd("attn_a_fwd", qa_b, ka_b, va_b, HA, HA // HKV, HEAD_DIM, HEAD_DIM, sc_a_, TQ)
    m_out, tok_m = _gather_mid(h_out, o_a)
    o_b, lse_b = _attn_fwd("attn_b_fwd", qcat_b, kcat_b, vb_b, HB, 1, 2 * LANE, V_DIM, sc_b_, TQ, after=(tok_m,))
    m_up, tok_m = _gather_mid(h_up, o_b)

    def mix_fn(oa, ob, ga, gb):
        return jnp.concatenate([_rn(oa) * ga, _rn(ob) * gb], axis=1)

    (g_wout,) = _gather_end(m_out, o_b)
    w_out_f = g_wout.reshape(1, N_CHIPS * g_wout.shape[1], D)
    on_b, o2 = _rows_mm("out_mm", mix_fn, [o_a, o_b], [g_out_a, g_out_b], w_out_f, after=(tok_m,))

    def mid(xt, o2t, gta, gpa, gpm, scm, shm):
        x1 = xt + gta * (_rn(o2t) * gpa)
        return x1, _rn(x1) * gpm * (1.0 + scm) + shm

    def mid_fn(*a):
        return mid(*a), ()

    mid_vecs = [gt_a, g_post_attn, g_pre_mlp, sc_m, sh_m]
    (x1, h2_b), _ = _rowwise("mid", mid_fn, [x2d, o2], mid_vecs, [(D, F32), (D, BF16)], [], T)
    (g_mlp_in,) = _gather_end(m_up, h2_b)
    relu_b, a_b = _mm_nn("mlp_in_mm", h2_b, g_mlp_in, [BF16, BF16],
                         epilogue=lambda acc: (jnp.maximum(acc, 0.0), jnp.square(jnp.maximum(acc, 0.0))))
    (g_mlp_out,) = _gather_finish(h_down, a_b)
    w_mlp_out_f = g_mlp_out.reshape(1, DFF, D)
    (y,) = _mm_nn("mlp_out_mm", a_b, w_mlp_out_f, [F32])

    def last_fn(yt, x1t, tt, gtm, gpo):
        n, r = _rn_parts(yt)
        err = x1t + gtm * (n * gpo) - tt
        dx2 = err * (1.0 / D)
        loss = 0.5 * jnp.sum(jnp.mean(err * err, axis=-1))
        s = _rows(dx2 * n)
        return (_rn_bwd(n, r, dx2 * (gtm * gpo)), dx2), (jnp.full((1, LANE), loss, F32), gpo * s, gtm * s)

    (dy_b, dx2), (loss_v, d_gt_m, d_g_post_mlp) = _rowwise(
        "loss_bwd", last_fn, [y, x1, tgt], [gt_m, g_post_mlp], [(D, BF16), (D, F32)], [LANE, D, D], TB)

    gw_mlp_out = _mm_tn("gw_mlp_out_mm", a_b, dy_b, 1).reshape(N_CHIPS, DFF // N_CHIPS, D)
    b_down, tok = _reduce_begin("down", [gw_mlp_out])
    (du_b,) = _mm_nt("d_mlp_out_mm", dy_b, w_mlp_out_f, [BF16],
                     epilogue=lambda acc, rt: (acc * (2.0 * rt.astype(F32)),), extras=(relu_b,), after=(tok,))
    r_down, tok = _reduce_start(b_down, where, du_b)
    gw_mlp_in = _mm_tn("gw_mlp_in_mm", h2_b, du_b, N_CHIPS, after=(tok,))
    b_up, tok = _reduce_begin("up", [gw_mlp_in])
    (dh2,) = _mm_nt("d_mlp_in_mm", du_b, g_mlp_in, [F32], after=(tok,))
    r_up, tok_up = _reduce_start(b_up, where, dh2)

    def mid_bwd_fn(xt, o2t, dh2t, dx2t, gta, gpa, gpm, scm, shm):
        n2, r2 = _rn_parts(o2t)
        n1, r1 = _rn_parts(xt + gta * (n2 * gpa))
        dx1 = dx2t + _rn_bwd(n1, r1, dh2t * (gpm * (1.0 + scm)))
        s1, s2 = _rows(dh2t * n1), _rows(dx1 * n2)
        do2 = _rn_bwd(n2, r2, dx1 * (gta * gpa))
        return (dx1, do2), (gpa * s2, gta * s2, (1.0 + scm) * s1, gpm * s1, _rows(dh2t))

    (dx1, do2_b), (d_gt_a, d_g_post_attn, d_g_pre_mlp, d_sc_m, d_sh_m) = _rowwise(
        "mid_bwd", mid_bwd_fn, [x2d, o2, dh2, dx2], mid_vecs, [(D, F32), (D, BF16)], [D] * 5, TB, after=(tok_up,))

    (d_on,) = _mm_nt("d_out_mm", do2_b, w_out_f, [F32])
    gw_out = _mm_tn("gw_out_mm", on_b, do2_b, 1).reshape(N_CHIPS, w_out.shape[1], D)

    def mix_bwd_fn(oa, ob, dont, ga, gb):
        na, ra = _rn_parts(oa)
        nb, rb = _rn_parts(ob)
        da, db = dont[:, :W_QA], dont[:, W_QA:]
        return (_rn_bwd(na, ra, da * ga), _rn_bwd(nb, rb, db * gb)), (_rows(da * na), _rows(db * nb))

    (doa_b, dob_b), (d_g_out_a, d_g_out_b) = _rowwise(
        "mix_bwd", mix_bwd_fn, [o_a, o_b, d_on], [g_out_a, g_out_b], [(W_QA, BF16), (HB * V_DIM, BF16)],
        [W_QA, HB * V_DIM], TB)

    dqa, dka, dva = _attn_bwd("attn_a_bwd", qa_b, ka_b, va_b, doa_b, o_a, lse_a, HA, HA // HKV, HEAD_DIM, HEAD_DIM,
                              sc_a_, TQ)
    dqcat, dkcat, dvb = _attn_bwd("attn_b_bwd", qcat_b, kcat_b, vb_b, dob_b, o_b, lse_b, HB, 1, 2 * LANE, V_DIM,
                                  sc_b_, TQ)

    d_ckvn, dkv_b, dkpe = _kv_mm_bwd("d_kv_mm", dkcat, dvb, g_kvb)
    gw_kvb = _mm_tn("gw_kv_mm", ckvn_b, dkv_b, N_CHIPS)
    b_kvb, tok_kvb = _reduce_begin("okv", [gw_out, gw_kvb])

    def prep_bwd_fn(pt, dqat, dkat, dvat, dqct, dckt, dkpt, ca_, sla, sha, cb_, slb, shb, gq, gk, gc):
        tabs = ((ca_, sla, sha), (cb_, slb, shb))
        _, vjp = jax.vjp(lambda sg, a1, a2, a3: _prep(sg, tabs, a1, a2, a3), _split_proj(pt), gq, gk, gc)
        (dqa_s, dka_s, dva_s, dqn_s, dqp_s, dck_s, dkp_s), dgq, dgk, dgc = vjp((dqat, dkat, dvat, dqct, dckt, dkpt))
        dproj = jnp.concatenate([*dqa_s, *dka_s, dva_s, *dqn_s, *dqp_s, dck_s, dkp_s], axis=1)
        return (dproj,), (dgq, dgk, dgc)

    (dproj_b,), (d_g_q_a, d_g_k_a, d_g_ckv) = _rowwise(
        "prep_bwd", prep_bwd_fn, [proj, dqa, dka, dva, dqcat, d_ckvn, dkpe, *tabs_a, *tabs_b],
        [g_q_a, g_k_a, g_ckv], [(W_INP, BF16)], [HEAD_DIM, HEAD_DIM, KV_RANK], TB, after=(tok_kvb,))

    r_kvb, tok = _reduce_start(b_kvb, where, dproj_b)
    (dh,) = _mm_nn("d_proj_mm", dproj_b, wpt, [F32], after=(tok,))
    gw_in = _mm_tn_rows("gw_in_mm", dproj_b, h_b, W_IN, _unpermute_pieces()).reshape(N_CHIPS, W_IN // N_CHIPS, D)
    b_in, tok = _reduce_begin("in", [gw_in])

    def pre_attn_bwd_fn(xt, dht, dx1t, g, sc, sh):
        n, r = _rn_parts(xt)
        s = _rows(dht * n)
        return (dx1t + _rn_bwd(n, r, dht * (g * (1.0 + sc))),), ((1.0 + sc) * s, g * s, _rows(dht))

    (grad_x,), (d_g_pre_attn, d_sc_a, d_sh_a) = _rowwise(
        "pre_attn_bwd", pre_attn_bwd_fn, [x2d, dh, dx1], [g_pre_attn, sc_a, sh_a], [(D, F32)], [D] * 3, TB,
        after=(tok,))

    small_names = ["b_ada", "g_pre_attn", "g_q_a", "g_k_a", "g_ckv", "g_out_a", "g_out_b", "g_post_attn",
                   "g_pre_mlp", "g_post_mlp"]
    small_local = jnp.concatenate([d_sh_a, d_sc_a, d_gt_a, d_sh_m, d_sc_m, d_gt_m, d_g_pre_attn, d_g_q_a, d_g_k_a,
                                   d_g_ckv, d_g_out_a, d_g_out_b, d_g_post_attn, d_g_pre_mlp, d_g_post_mlp,
                                   loss_v[:, :8]], axis=1)
    n_sent = small_local.shape[1]
    n_small = n_sent - 8
    small_all = _allgather8("gather_small", small_local.reshape(8, n_sent // 8)).reshape(N_DEV, n_sent)

    def sum8_fn(t):
        return (), (jnp.sum(t, axis=0, keepdims=True),)

    r_in, tok_in = _reduce_start(b_in, where, grad_x, (small_all,))
    _, (sent_sum,) = _rowwise("sum_small", sum8_fn, [small_all], [], [], [n_sent], N_DEV, after=(tok_in,))
    small_sum, loss = sent_sum[:, :n_small], sent_sum[0, n_small]
    dmod_cols = lax.dynamic_slice(small_all, (0, j_me * n_mod), (N_DEV, n_mod))
    dmod_pad = jnp.concatenate([dmod_cols, jnp.zeros_like(dmod_cols)], axis=0).astype(BF16)

    weights = dict(w_ada=w_ada, b_ada=b_ada, g_pre_attn=g_pre_attn, w_in=w_in, g_q_a=g_q_a, g_k_a=g_k_a, g_ckv=g_ckv,
                   w_kv_b=w_kv_b, g_out_a=g_out_a, g_out_b=g_out_b, w_out=w_out, g_post_attn=g_post_attn,
                   g_pre_mlp=g_pre_mlp, w_mlp_in=w_mlp_in, w_mlp_out=w_mlp_out, g_post_mlp=g_post_mlp)
    ms = dict(w_ada=m_w_ada, b_ada=m_b_ada, g_pre_attn=m_g_pre_attn, w_in=m_w_in, g_q_a=m_g_q_a, g_k_a=m_g_k_a,
              g_ckv=m_g_ckv, w_kv_b=m_w_kv_b, g_out_a=m_g_out_a, g_out_b=m_g_out_b, w_out=m_w_out,
              g_post_attn=m_g_post_attn, g_pre_mlp=m_g_pre_mlp, w_mlp_in=m_w_mlp_in, w_mlp_out=m_w_mlp_out,
              g_post_mlp=m_g_post_mlp)
    vs = dict(w_ada=v_w_ada, b_ada=v_b_ada, g_pre_attn=v_g_pre_attn, w_in=v_w_in, g_q_a=v_g_q_a, g_k_a=v_g_k_a,
              g_ckv=v_g_ckv, w_kv_b=v_w_kv_b, g_out_a=v_g_out_a, g_out_b=v_g_out_b, w_out=v_w_out,
              g_post_attn=v_g_post_attn, g_pre_mlp=v_g_pre_mlp, w_mlp_in=v_w_mlp_in, w_mlp_out=v_w_mlp_out,
              g_post_mlp=v_g_post_mlp)
    order = list(weights)
    grads, deltas, new_m, new_v = {}, {}, {}, {}

    def update(name, g, after=()):
        view = (lambda t: jnp.swapaxes(t, 0, 1)) if name == "w_in" else (lambda t: t)
        d_, m_, v_, g_ = _adamw("adamw_" + name, view(weights[name][0]), g, view(ms[name][0]), view(vs[name][0]),
                                after, rewrite_g=True)
        grads[name], deltas[name], new_m[name], new_v[name] = (view(t)[None] for t in (g_, d_, m_, v_))
        return d_

    chain = ((("w_mlp_out",), r_down), (("w_mlp_in",), r_up), (("w_out", "w_kv_b"), r_kvb), (("w_in",), r_in))
    swap, tok = _reduce_finish(chain[0][1], where, small_sum)
    last, m_, v_, g_ = _adamw_outer("adamw_w_ada", w_ada[0], ca_pad, dmod_pad, m_w_ada[0], v_w_ada[0], (tok,))
    grads["w_ada"], deltas["w_ada"], new_m["w_ada"], new_v["w_ada"] = g_[None], last[None], m_[None], v_[None]
    for k, (names, _) in enumerate(chain):
        gs = _reduce_end(swap, last)
        if k + 1 < len(chain):
            swap, tok = _reduce_finish(chain[k + 1][1], where, gs[0])
        for name, g in zip(names, gs):
            last = update(name, g, (tok,))

    cat = lambda d: jnp.concatenate([d[n] for n in small_names], axis=1)
    d_, m_, v_ = _adamw("adamw_small", cat(weights), small_sum, cat(ms), cat(vs))
    off = 0
    for n in small_names:
        w_ = weights[n].shape[1]
        grads[n], deltas[n], new_m[n], new_v[n] = (a[:, off:off + w_] for a in (small_sum, d_, m_, v_))
        off += w_

    return (loss, grad_x[None], *[grads[n] for n in order], *[deltas[n] for n in order],
            *[new_m[n] for n in order], *[new_v[n] for n in order])
```

```python
import functools
import math

import numpy as np
import jax
import jax.numpy as jnp
from jax import lax
from jax.experimental import pallas as pl
from jax.experimental.pallas import tpu as pltpu

F32 = jnp.float32
BF16 = jnp.bfloat16
MESH = pl.DeviceIdType.MESH

EPS = 1e-6
GRID_W = 64
ROPE_THETA = 10000.0
HEAD_DIM = 128
HA = 8
HKV = 2
HB = 8
QK_NOPE = 128
QK_ROPE = 64
V_DIM = 128
KV_RANK = 512
QK_B = QK_NOPE + QK_ROPE
W_QA = HA * HEAD_DIM
W_KA = HKV * HEAD_DIM
W_QB = HB * QK_B
W_IN = W_QA + 2 * W_KA + W_QB + KV_RANK + QK_ROPE
N_CHIPS = 4
N_DEV = 8

LANE = 128
O_QA = 0
O_KA = O_QA + W_QA
O_VA = O_KA + W_KA
O_QN = O_VA + W_KA
O_QP = O_QN + HB * QK_NOPE
O_CKV = O_QP + HB * LANE
O_KPE = O_CKV + KV_RANK
W_INP = O_KPE + LANE

ADAM_LR = 0.001
ADAM_B1 = 0.9
ADAM_B2 = 0.999
ADAM_EPS = 1e-08
ADAM_WD = 0.01
ADAM_STEP = 10

VMEM_LIMIT = 56 * 1024 * 1024
HEADS_PER_STEP = 2


def _params(sem):
    return pltpu.CompilerParams(dimension_semantics=sem, vmem_limit_bytes=VMEM_LIMIT)


def _pick(dim, target, unit):
    best = None
    t = unit
    while t <= min(dim, target):
        if dim % t == 0:
            best = t
        t += unit
    return dim if best is None else best


def _rn(x):
    return x * lax.rsqrt(jnp.mean(x * x, axis=-1, keepdims=True) + EPS)


def _rn_parts(x):
    r = lax.rsqrt(jnp.mean(x * x, axis=-1, keepdims=True) + EPS)
    return x * r, r


def _rn_bwd(n, r, dn):
    return r * (dn - n * jnp.mean(dn * n, axis=-1, keepdims=True))


def _rows(t):
    return jnp.sum(t, axis=0, keepdims=True)


def _rowwise(name, fn, rows, vecs, row_outs, vec_outs, tile, after=()):
    S = rows[0].shape[0]
    assert S % tile == 0
    n_r, n_v, n_ro, n_vo, n_a = len(rows), len(vecs), len(row_outs), len(vec_outs), len(after)

    def body(*refs):
        r_in = refs[:n_r]
        v_in = refs[n_r:n_r + n_v]
        r_out = refs[n_r + n_v + n_a:n_r + n_v + n_a + n_ro]
        v_out = refs[n_r + n_v + n_a + n_ro:]
        ro, vo = fn(*[r[...] for r in r_in], *[v[...] for v in v_in])
        for ref, val in zip(r_out, ro):
            ref[...] = val.astype(ref.dtype)
        if n_vo:
            @pl.when(pl.program_id(0) == 0)
            def _():
                for ref in v_out:
                    ref[...] = jnp.zeros(ref.shape, ref.dtype)
            for ref, val in zip(v_out, vo):
                ref[...] += val

    in_specs = [pl.BlockSpec((tile, a.shape[1]), lambda i: (i, 0)) for a in rows]
    in_specs += [pl.BlockSpec(a.shape, lambda i: (0, 0)) for a in vecs]
    in_specs += [pl.BlockSpec(memory_space=pl.ANY) for _ in after]
    out_specs = [pl.BlockSpec((tile, w), lambda i: (i, 0)) for w, _ in row_outs]
    out_specs += [pl.BlockSpec((1, w), lambda i: (0, 0)) for w in vec_outs]
    out_shape = [jax.ShapeDtypeStruct((S, w), dt) for w, dt in row_outs]
    out_shape += [jax.ShapeDtypeStruct((1, w), F32) for w in vec_outs]
    res = pl.pallas_call(
        body, name=name, grid=(S // tile,), in_specs=in_specs, out_specs=out_specs, out_shape=out_shape,
        compiler_params=_params(("arbitrary",)),
    )(*rows, *vecs, *after)
    return res[:n_ro], res[n_ro:]


CONTRACT_TILE = 2048
MM_TILE_BYTES = 40 * 1024 * 1024


def _rows_that_fit(M, tile_bytes):
    t = _pick(M, 1024, 16)
    while t % 32 == 0 and tile_bytes(t) > MM_TILE_BYTES:
        t //= 2
    return t


def _accumulate(acc, part, step, n_steps, finish):
    if n_steps == 1:
        finish(part)
        return

    @pl.when(step == 0)
    def _():
        acc[...] = part

    @pl.when((step > 0) & (step < n_steps - 1))
    def _():
        acc[...] += part

    @pl.when(step == n_steps - 1)
    def _():
        finish(acc[...] + part)


def _mm_nn(name, a, b3, out_dtypes, epilogue=None, extras=(), after=()):
    M, K = a.shape
    nB, K2, Nsh = b3.shape
    assert K == K2
    N = nB * Nsh
    tn, tk = _pick(Nsh, 1536, LANE), _pick(K, CONTRACT_TILE, LANE)
    out_bytes = sum(jnp.dtype(dt).itemsize for dt in out_dtypes) + sum(e.dtype.itemsize for e in extras)
    tm = _rows_that_fit(M, lambda t: 2 * (t * tk * a.dtype.itemsize + tk * tn * b3.dtype.itemsize)
                        + t * tn * (4 * (K > tk) + 2 * out_bytes))
    nps, nk, ne, no, na = Nsh // tn, K // tk, len(extras), len(out_dtypes), len(after)

    def body(a_ref, b_ref, *rest):
        e_refs, o_refs = rest[:ne], rest[ne + na:ne + na + no]

        def finish(total):
            vals = (total,) if epilogue is None else epilogue(total, *[e[...] for e in e_refs])
            for ref, val in zip(o_refs, vals):
                ref[...] = val.astype(ref.dtype)

        part = jnp.dot(a_ref[...].astype(BF16), b_ref[...].astype(BF16), preferred_element_type=F32)
        _accumulate(rest[-1], part, pl.program_id(2), nk, finish)

    in_specs = [pl.BlockSpec((tm, tk), lambda i, j, k: (i, k)),
                pl.BlockSpec((None, tk, tn), lambda i, j, k: (j // nps, k, j % nps))]
    in_specs += [pl.BlockSpec((tm, tn), lambda i, j, k: (i, j)) for _ in extras]
    in_specs += [pl.BlockSpec(memory_space=pl.ANY) for _ in after]
    return pl.pallas_call(
        body, name=name, grid=(M // tm, N // tn, nk), in_specs=in_specs,
        out_specs=[pl.BlockSpec((tm, tn), lambda i, j, k: (i, j)) for _ in out_dtypes],
        out_shape=[jax.ShapeDtypeStruct((M, N), dt) for dt in out_dtypes],
        scratch_shapes=[pltpu.VMEM((tm, tn), F32)] if nk > 1 else [],
        compiler_params=_params(("parallel", "parallel", "arbitrary")),
    )(a, b3, *extras, *after)


def _mm_nt(name, a, b3, out_dtypes, epilogue=None, extras=(), after=()):
    M, N = a.shape
    nB, K, Nsh = b3.shape
    assert N == nB * Nsh
    to, tn = _pick(K, 1536, LANE), _pick(Nsh, CONTRACT_TILE, LANE)
    out_bytes = sum(jnp.dtype(dt).itemsize for dt in out_dtypes) + sum(e.dtype.itemsize for e in extras)
    tm = _rows_that_fit(M, lambda t: 2 * (t * tn * a.dtype.itemsize + to * tn * b3.dtype.itemsize)
                        + t * to * (4 * (N > tn) + 2 * out_bytes))
    nps, nn, ne, no, na = Nsh // tn, N // tn, len(extras), len(out_dtypes), len(after)

    def body(a_ref, b_ref, *rest):
        e_refs, o_refs = rest[:ne], rest[ne + na:ne + na + no]

        def finish(total):
            vals = (total,) if epilogue is None else epilogue(total, *[e[...] for e in e_refs])
            for ref, val in zip(o_refs, vals):
                ref[...] = val.astype(ref.dtype)

        part = lax.dot_general(a_ref[...].astype(BF16), b_ref[...].astype(BF16),
                               (((1,), (1,)), ((), ())), preferred_element_type=F32)
        _accumulate(rest[-1], part, pl.program_id(2), nn, finish)

    in_specs = [pl.BlockSpec((tm, tn), lambda i, o, n: (i, n)),
                pl.BlockSpec((None, to, tn), lambda i, o, n: (n // nps, o, n % nps))]
    in_specs += [pl.BlockSpec((tm, to), lambda i, o, n: (i, o)) for _ in extras]
    in_specs += [pl.BlockSpec(memory_space=pl.ANY) for _ in after]
    return pl.pallas_call(
        body, name=name, grid=(M // tm, K // to, nn), in_specs=in_specs,
        out_specs=[pl.BlockSpec((tm, to), lambda i, o, n: (i, o)) for _ in out_dtypes],
        out_shape=[jax.ShapeDtypeStruct((M, K), dt) for dt in out_dtypes],
        scratch_shapes=[pltpu.VMEM((tm, to), F32)] if nn > 1 else [],
        compiler_params=_params(("parallel", "parallel", "arbitrary")),
    )(a, b3, *extras, *after)


def _mm_tn(name, a, g, nB, after=()):
    M, K = a.shape
    M2, N = g.shape
    assert M == M2 and N % nB == 0
    Nsh = N // nB
    tk, tn, tm = _pick(K, 1536, LANE), _pick(Nsh, 1024, LANE), _pick(M, CONTRACT_TILE, 16)
    nps, nm = Nsh // tn, M // tm

    def body(a_ref, g_ref, *rest):
        o_ref = rest[len(after)]

        def finish(total):
            o_ref[...] = total

        part = lax.dot_general(a_ref[...].astype(BF16), g_ref[...].astype(BF16),
                               (((0,), (0,)), ((), ())), preferred_element_type=F32)
        _accumulate(rest[-1], part, pl.program_id(2), nm, finish)

    return pl.pallas_call(
        body, name=name, grid=(K // tk, N // tn, nm),
        in_specs=[pl.BlockSpec((tm, tk), lambda k, n, m: (m, k)),
                  pl.BlockSpec((tm, tn), lambda k, n, m: (m, n))] + [pl.BlockSpec(memory_space=pl.ANY) for _ in after],
        out_specs=pl.BlockSpec((None, tk, tn), lambda k, n, m: (n // nps, k, n % nps)),
        out_shape=jax.ShapeDtypeStruct((nB, K, Nsh), F32),
        scratch_shapes=[pltpu.VMEM((tk, tn), F32)] if nm > 1 else [],
        compiler_params=_params(("parallel", "parallel", "arbitrary")),
    )(a, g, *after)


def _rows_mm(name, fn, rows, vecs, b3, after=()):
    S = rows[0].shape[0]
    _, K, N = b3.shape
    tm, tn = _pick(S, 1024, 16), _pick(N, 512, LANE)
    nr, nv, na = len(rows), len(vecs), len(after)

    def body(*refs):
        b_ref = refs[nr + nv]
        a_out, o_ref, a_scr = refs[nr + nv + 1 + na:]

        @pl.when(pl.program_id(1) == 0)
        def _():
            a = fn(*[r[...] for r in refs[:nr]], *[v[...] for v in refs[nr:nr + nv]]).astype(BF16)
            a_scr[...] = a
            a_out[...] = a

        o_ref[...] = jnp.dot(a_scr[...], b_ref[...], preferred_element_type=F32)

    in_specs = [pl.BlockSpec((tm, r.shape[1]), lambda i, j: (i, 0)) for r in rows]
    in_specs += [pl.BlockSpec(v.shape, lambda i, j: (0, 0)) for v in vecs]
    in_specs += [pl.BlockSpec((None, K, tn), lambda i, j: (0, 0, j))]
    in_specs += [pl.BlockSpec(memory_space=pl.ANY) for _ in after]
    return pl.pallas_call(
        body, name=name, grid=(S // tm, N // tn), in_specs=in_specs,
        out_specs=[pl.BlockSpec((tm, K), lambda i, j: (i, 0)), pl.BlockSpec((tm, tn), lambda i, j: (i, j))],
        out_shape=[jax.ShapeDtypeStruct((S, K), BF16), jax.ShapeDtypeStruct((S, N), F32)],
        scratch_shapes=[pltpu.VMEM((tm, K), BF16)],
        compiler_params=_params(("parallel", "arbitrary")),
    )(*rows, *vecs, b3, *after)


def _kv_mm(name, ckvn, w3, kpe):
    S, K = ckvn.shape
    nB, _, Nsh = w3.shape
    heads = Nsh // (2 * LANE)
    tm = _pick(S, 1024, 16)

    def body(a_ref, b_ref, kpe_ref, kc_ref, v_ref):
        acc = jnp.dot(a_ref[...], b_ref[...], preferred_element_type=F32)
        for h in range(heads):
            kc_ref[:, 2 * h * LANE:(2 * h + 1) * LANE] = acc[:, 2 * h * LANE:(2 * h + 1) * LANE].astype(BF16)
            kc_ref[:, (2 * h + 1) * LANE:(2 * h + 2) * LANE] = kpe_ref[...]
            v_ref[:, h * LANE:(h + 1) * LANE] = acc[:, (2 * h + 1) * LANE:(2 * h + 2) * LANE].astype(BF16)

    return pl.pallas_call(
        body, name=name, grid=(S // tm, nB),
        in_specs=[pl.BlockSpec((tm, K), lambda i, j: (i, 0)), pl.BlockSpec((None, K, Nsh), lambda i, j: (j, 0, 0)),
                  pl.BlockSpec((tm, LANE), lambda i, j: (i, 0))],
        out_specs=[pl.BlockSpec((tm, Nsh), lambda i, j: (i, j)), pl.BlockSpec((tm, Nsh // 2), lambda i, j: (i, j))],
        out_shape=[jax.ShapeDtypeStruct((S, nB * Nsh), BF16), jax.ShapeDtypeStruct((S, nB * Nsh // 2), BF16)],
        compiler_params=_params(("parallel", "arbitrary")),
    )(ckvn, w3, kpe)


def _kv_mm_bwd(name, dkc, dv, w3):
    S = dkc.shape[0]
    nB, K, Nsh = w3.shape
    heads = Nsh // (2 * LANE)
    tm = _pick(S, 1024, 16)

    def body(dkc_ref, dv_ref, b_ref, dck_ref, dkv_ref, dkpe_ref):
        j = pl.program_id(1)
        pieces, pe = [], None
        for h in range(heads):
            pieces += [dkc_ref[:, 2 * h * LANE:(2 * h + 1) * LANE], dv_ref[:, h * LANE:(h + 1) * LANE]]
            piece = dkc_ref[:, (2 * h + 1) * LANE:(2 * h + 2) * LANE]
            pe = piece if pe is None else pe + piece
        dkv = jnp.concatenate(pieces, axis=1).astype(BF16)
        dkv_ref[...] = dkv
        part = lax.dot_general(dkv, b_ref[...], (((1,), (1,)), ((), ())), preferred_element_type=F32)

        @pl.when(j == 0)
        def _():
            dck_ref[...] = part
            dkpe_ref[...] = pe

        @pl.when(j > 0)
        def _():
            dck_ref[...] += part
            dkpe_ref[...] += pe

    return pl.pallas_call(
        body, name=name, grid=(S // tm, nB),
        in_specs=[pl.BlockSpec((tm, Nsh), lambda i, j: (i, j)), pl.BlockSpec((tm, Nsh // 2), lambda i, j: (i, j)),
                  pl.BlockSpec((None, K, Nsh), lambda i, j: (j, 0, 0))],
        out_specs=[pl.BlockSpec((tm, K), lambda i, j: (i, 0)), pl.BlockSpec((tm, Nsh), lambda i, j: (i, j)),
                   pl.BlockSpec((tm, LANE), lambda i, j: (i, 0))],
        out_shape=[jax.ShapeDtypeStruct((S, K), F32), jax.ShapeDtypeStruct((S, nB * Nsh), BF16),
                   jax.ShapeDtypeStruct((S, LANE), F32)],
        compiler_params=_params(("parallel", "arbitrary")),
    )(dkc, dv, w3)


def _mm_tn_rows(name, a, g, n_rows, pieces):
    M, K = a.shape
    M2, N = g.shape
    assert M == M2
    tk, tn, tm = _pick(K, 1536, LANE), _pick(N, 512, LANE), _pick(M, CONTRACT_TILE, 16)
    nk, nm = K // tk, M // tm

    def body(a_ref, g_ref, o_ref, *acc):
        kk = pl.program_id(1)

        def finish(total):
            for t in range(nk):
                @pl.when(kk == t)
                def _(t=t):
                    for src, dst, n in pieces:
                        lo, hi = max(src, t * tk), min(src + n, (t + 1) * tk)
                        if lo < hi:
                            o_ref[dst + lo - src:dst + hi - src, :] = total[lo - t * tk:hi - t * tk, :]

        part = lax.dot_general(a_ref[...].astype(BF16), g_ref[...].astype(BF16),
                               (((0,), (0,)), ((), ())), preferred_element_type=F32)
        _accumulate(acc[0] if acc else None, part, pl.program_id(2), nm, finish)

    return pl.pallas_call(
        body, name=name, grid=(N // tn, nk, nm),
        in_specs=[pl.BlockSpec((tm, tk), lambda n, k, m: (m, k)),
                  pl.BlockSpec((tm, tn), lambda n, k, m: (m, n))],
        out_specs=pl.BlockSpec((n_rows, tn), lambda n, k, m: (0, n)),
        out_shape=jax.ShapeDtypeStruct((n_rows, N), F32),
        scratch_shapes=[pltpu.VMEM((tk, tn), F32)] if nm > 1 else [],
        compiler_params=_params(("arbitrary", "arbitrary", "arbitrary")),
    )(a, g)


def _attn_fwd(name, q, k, v, n_heads, group, dqk, dv, scale, tq, after=()):
    S = q.shape[0]
    hp = 2 * HEADS_PER_STEP
    kvp = max(1, hp // group)

    def body(q_ref, k_ref, v_ref, *rest):
        o_ref, lse_ref = rest[-2:]
        for j in range(hp):
            jk = j * kvp // hp
            s = lax.dot_general(q_ref[:, j * dqk:(j + 1) * dqk], k_ref[:, jk * dqk:(jk + 1) * dqk],
                                (((1,), (1,)), ((), ())), preferred_element_type=F32)
            m = jnp.max(s, axis=-1, keepdims=True)
            e = jnp.exp((s - m) * scale)
            l = jnp.sum(e, axis=-1, keepdims=True)
            o_ref[:, j * dv:(j + 1) * dv] = jnp.dot(e.astype(BF16), v_ref[:, jk * dv:(jk + 1) * dv],
                                                    preferred_element_type=F32) * (1.0 / l)
            lse_ref[:, j * LANE:(j + 1) * LANE] = jnp.broadcast_to(m * scale + jnp.log(l), (tq, LANE))

    kv_block = lambda h, i: (0, (h * hp // group) // kvp)
    return pl.pallas_call(
        body, name=name, grid=(n_heads // hp, S // tq),
        in_specs=[pl.BlockSpec((tq, hp * dqk), lambda h, i: (i, h)),
                  pl.BlockSpec((S, kvp * dqk), kv_block),
                  pl.BlockSpec((S, kvp * dv), kv_block)] + [pl.BlockSpec(memory_space=pl.ANY) for _ in after],
        out_specs=[pl.BlockSpec((tq, hp * dv), lambda h, i: (i, h)),
                   pl.BlockSpec((tq, hp * LANE), lambda h, i: (i, h))],
        out_shape=[jax.ShapeDtypeStruct((S, n_heads * dv), F32), jax.ShapeDtypeStruct((S, n_heads * LANE), F32)],
        compiler_params=_params(("arbitrary", "arbitrary")),
    )(q, k, v, *after)


def _attn_bwd(name, q, k, v, do, o, lse, n_heads, group, dqk, dv, scale, tq, after=()):
    S = q.shape[0]
    n_kv = n_heads // group
    hp, tq = (group, tq // 2) if group > HEADS_PER_STEP else (HEADS_PER_STEP, tq)
    kvp = max(1, hp // group)
    chains = 4
    tr = tq * hp // chains

    def body(q_ref, k_ref, v_ref, do_ref, o_ref, lse_ref, *rest):
        dq_ref, dk_ref, dv_ref = rest[len(after):]
        h, i = pl.program_id(0), pl.program_id(1)

        @pl.when(((h * hp) % group == 0) & (i == 0))
        def _():
            dk_ref[...] = jnp.zeros(dk_ref.shape, F32)
            dv_ref[...] = jnp.zeros(dv_ref.shape, F32)

        for j in range(hp):
            jk = j * kvp // hp
            kb, vb = k_ref[:, jk * dqk:(jk + 1) * dqk], v_ref[:, jk * dv:(jk + 1) * dv]
            for r0 in range(0, tq, tr):
                rows = slice(r0, r0 + tr)
                qb, dob = q_ref[rows, j * dqk:(j + 1) * dqk], do_ref[rows, j * dv:(j + 1) * dv]
                s = lax.dot_general(qb, kb, (((1,), (1,)), ((), ())), preferred_element_type=F32)
                p = jnp.exp(s * scale - jnp.tile(lse_ref[rows, j * LANE:(j + 1) * LANE], (1, S // LANE)))
                dp = lax.dot_general(dob, vb, (((1,), (1,)), ((), ())), preferred_element_type=F32)
                delta = jnp.sum(dob.astype(F32) * o_ref[rows, j * dv:(j + 1) * dv], axis=-1, keepdims=True)
                ds = (p * ((dp - delta) * scale)).astype(BF16)
                dv_ref[:, jk * dv:(jk + 1) * dv] += lax.dot_general(p.astype(BF16), dob, (((0,), (0,)), ((), ())),
                                                                    preferred_element_type=F32)
                dq_ref[rows, j * dqk:(j + 1) * dqk] = jnp.dot(ds, kb, preferred_element_type=F32)
                dk_ref[:, jk * dqk:(jk + 1) * dqk] += lax.dot_general(ds, qb, (((0,), (0,)), ((), ())),
                                                                      preferred_element_type=F32)

    kv_block = lambda h, i: (0, (h * hp // group) // kvp)
    return pl.pallas_call(
        body, name=name, grid=(n_heads // hp, S // tq),
        in_specs=[pl.BlockSpec((tq, hp * dqk), lambda h, i: (i, h)),
                  pl.BlockSpec((S, kvp * dqk), kv_block),
                  pl.BlockSpec((S, kvp * dv), kv_block),
                  pl.BlockSpec((tq, hp * dv), lambda h, i: (i, h)),
                  pl.BlockSpec((tq, hp * dv), lambda h, i: (i, h)),
                  pl.BlockSpec((tq, hp * LANE), lambda h, i: (i, h))] + [pl.BlockSpec(memory_space=pl.ANY) for _ in after],
        out_specs=[pl.BlockSpec((tq, hp * dqk), lambda h, i: (i, h)),
                   pl.BlockSpec((S, kvp * dqk), kv_block),
                   pl.BlockSpec((S, kvp * dv), kv_block)],
        out_shape=[jax.ShapeDtypeStruct((S, n_heads * dqk), F32),
                   jax.ShapeDtypeStruct((S, n_kv * dqk), F32),
                   jax.ShapeDtypeStruct((S, n_kv * dv), F32)],
        compiler_params=_params(("arbitrary", "arbitrary")),
    )(q, k, v, do, o, lse, *after)


def _adamw_step(w, gg, m, v):
    nm = ADAM_B1 * m + (1.0 - ADAM_B1) * gg
    nv = ADAM_B2 * v + (1.0 - ADAM_B2) * (gg * gg)
    c1, c2 = 1.0 - ADAM_B1 ** ADAM_STEP, 1.0 - ADAM_B2 ** ADAM_STEP
    return -ADAM_LR * ((nm / c1) / (jnp.sqrt(nv / c2) + ADAM_EPS) + ADAM_WD * w), nm, nv


def _adamw(name, w, g, m, v, after=(), rewrite_g=False):
    R, C = w.shape
    tile = _pick(R, max(8, (1 << 19) // C // 8 * 8), 8)
    n_out = 4 if rewrite_g else 3

    def body(w_ref, g_ref, m_ref, v_ref, *rest):
        d_ref, nm_ref, nv_ref = rest[len(after):len(after) + 3]
        gg = g_ref[...]
        d_ref[...], nm_ref[...], nv_ref[...] = _adamw_step(w_ref[...], gg, m_ref[...], v_ref[...])
        if rewrite_g:
            rest[-1][...] = gg

    spec = pl.BlockSpec((tile, C), lambda i: (i, 0))
    return pl.pallas_call(
        body, name=name, grid=(R // tile,), in_specs=[spec] * 4 + [pl.BlockSpec(memory_space=pl.ANY) for _ in after],
        out_specs=[spec] * n_out, out_shape=[jax.ShapeDtypeStruct((R, C), F32)] * n_out,
        compiler_params=_params(("parallel",)),
    )(w, g, m, v, *after)


def _adamw_outer(name, w, left, right, m, v, after=()):
    R, C = w.shape
    Kc = left.shape[0]
    tile = _pick(R, max(LANE, (1 << 19) // C // LANE * LANE), LANE)

    def body(w_ref, l_ref, r_ref, m_ref, v_ref, *rest):
        d_ref, nm_ref, nv_ref, g_ref = rest[len(after):]
        gg = lax.dot_general(l_ref[...], r_ref[...], (((0,), (0,)), ((), ())), preferred_element_type=F32)
        d_ref[...], nm_ref[...], nv_ref[...] = _adamw_step(w_ref[...], gg, m_ref[...], v_ref[...])
        g_ref[...] = gg

    spec = pl.BlockSpec((tile, C), lambda i: (i, 0))
    return pl.pallas_call(
        body, name=name, grid=(R // tile,),
        in_specs=[spec, pl.BlockSpec((Kc, tile), lambda i: (0, i)), pl.BlockSpec((Kc, C), lambda i: (0, 0)), spec, spec]
        + [pl.BlockSpec(memory_space=pl.ANY) for _ in after],
        out_specs=[spec] * 4, out_shape=[jax.ShapeDtypeStruct((R, C), F32)] * 4,
        compiler_params=_params(("parallel",)),
    )(w, left, right, m, v, *after)


def _flip(me, rel):
    return tuple(1 - p if r else p for p, r in zip(me, rel))


def _push(name, ins, out_shapes, transfers, aliases=None):
    ni, no, nt = len(ins), len(out_shapes), len(transfers)

    def body(*refs):
        in_refs, out_refs = refs[:ni], refs[ni:ni + no]
        send_sems, recv_sems = refs[ni + no], refs[ni + no + 1]
        me = (lax.axis_index("x"), lax.axis_index("y"), lax.axis_index("c"))
        copies = []
        for t, (si, sfn, oi, dfn, rel) in enumerate(transfers):
            src_ref = out_refs[si[1]] if isinstance(si, tuple) else in_refs[si]
            src = src_ref.at[sfn(*me)]
            dst = out_refs[oi].at[dfn(*me)]
            if rel is None:
                cp = pltpu.make_async_copy(src, dst, send_sems.at[t])
            else:
                cp = pltpu.make_async_remote_copy(src_ref=src, dst_ref=dst, send_sem=send_sems.at[t],
                                                  recv_sem=recv_sems.at[t], device_id=_flip(me, rel),
                                                  device_id_type=MESH)
            cp.start()
            copies.append(cp)
        for cp in copies:
            cp.wait()

    any_spec = pl.BlockSpec(memory_space=pl.ANY)
    return pl.pallas_call(
        body, name=name, in_specs=[any_spec] * ni, out_specs=[any_spec] * no, out_shape=out_shapes,
        scratch_shapes=[pltpu.SemaphoreType.DMA((nt,)), pltpu.SemaphoreType.DMA((nt,))],
        input_output_aliases=aliases or {},
        compiler_params=pltpu.CompilerParams(has_side_effects=True),
    )(*ins)


_HBM = pl.BlockSpec(memory_space=pltpu.HBM)
_SEM = pl.BlockSpec(memory_space=pltpu.SEMAPHORE)
_DATAFLOW = pltpu.SideEffectType.DATAFLOW_SIDE_EFFECTING


def _split_copies(refs, transfers, send_sems, recv_sems):
    me = (lax.axis_index("x"), lax.axis_index("y"), lax.axis_index("c"))
    return [pltpu.make_async_remote_copy(src_ref=refs[sb].at[sfn(*me)], dst_ref=refs[db].at[dfn(*me)],
                                         send_sem=send_sems.at[t], recv_sem=recv_sems.at[t],
                                         device_id=_flip(me, rel), device_id_type=MESH)
            for t, (sb, sfn, db, dfn, rel) in enumerate(transfers)]


def _push_start(name, bufs, transfers, after=()):
    nb, na, nt = len(bufs), len(after), len(transfers)

    def body(*refs):
        send_sems, recv_sems, token = refs[nb + na], refs[nb + na + 1], refs[-1]
        for cp in _split_copies(refs[nb + na + 2:2 * nb + na + 2], transfers, send_sems, recv_sems):
            cp.start()
        token[...] = jnp.zeros(token.shape, token.dtype)

    res = pl.pallas_call(
        body, name=name,
        out_shape=(pltpu.SemaphoreType.DMA((nt,)), pltpu.SemaphoreType.DMA((nt,)),
                   *[pltpu.HBM(b.shape, b.dtype) for b in bufs], jax.ShapeDtypeStruct((8, LANE), F32)),
        in_specs=[_HBM] * nb + [pl.BlockSpec(memory_space=pl.ANY)] * na,
        out_specs=(_SEM, _SEM, *[_HBM] * nb, pl.BlockSpec(memory_space=pltpu.VMEM)),
        input_output_aliases={i: i + 2 for i in range(nb)},
        compiler_params=pltpu.CompilerParams(has_side_effects=_DATAFLOW),
    )(*[pltpu.with_memory_space_constraint(b, pltpu.HBM) for b in bufs], *after)
    return res[0], res[1], list(res[2:2 + nb]), res[-1]


def _push_wait(name, send_sems, recv_sems, bufs, transfers, after):
    nb = len(bufs)

    def body(*refs):
        for cp in _split_copies(refs[:nb], transfers, refs[nb], refs[nb + 1]):
            cp.wait_send()
            cp.wait_recv()

    res = pl.pallas_call(
        body, name=name, out_shape=[pltpu.HBM(b.shape, b.dtype) for b in bufs],
        in_specs=[_HBM] * nb + [_SEM, _SEM, pl.BlockSpec(memory_space=pl.ANY)], out_specs=[_HBM] * nb,
        input_output_aliases={i: i for i in range(nb)},
        compiler_params=pltpu.CompilerParams(has_side_effects=_DATAFLOW),
    )(*bufs, send_sems, recv_sems, after)
    return list(res)


ICI_RELS = ((1, 0, 0), (0, 1, 0), (1, 1, 0))
SIBLING = (0, 0, 1)


def _chip(x, y):
    return 2 * x + y


def _cast_into_slot(name, where, w, after=()):
    R, C = w.shape
    tile = _pick(R, 512, BF16_ROWS)

    def body(where_ref, w_ref, *rest):
        rest[-1][...] = w_ref[...].astype(BF16)

    return pl.pallas_call(
        body, name=name,
        grid_spec=pltpu.PrefetchScalarGridSpec(
            num_scalar_prefetch=1, grid=(R // tile,),
            in_specs=[pl.BlockSpec((tile, C), lambda i, wh: (i, 0))] + [pl.BlockSpec(memory_space=pl.ANY) for _ in after],
            out_specs=pl.BlockSpec((None, tile, C), lambda i, wh: (wh[1], i, 0))),
        out_shape=jax.ShapeDtypeStruct((N_CHIPS, R, C), BF16),
        compiler_params=_params(("parallel",)),
    )(where, w, *after)


BF16_ROWS = 16


def _rows_split(shape2):
    return (shape2[0] // 2) % BF16_ROWS == 0


def _half_shape(shape2):
    R, C = shape2
    return (R // 2, C) if _rows_split(shape2) else (R, C // 2)


def _half(shape2, c):
    R, C = shape2
    if _rows_split(shape2):
        return (pl.ds(c * (R // 2), R // 2), slice(None))
    return (slice(None), pl.ds(pl.multiple_of(c * (C // 2), LANE), C // 2))


def _gather_casts(tag, where, shards, after=()):
    return [_cast_into_slot(f"cast_w_{tag}{w}", where, s, after) for w, s in enumerate(shards)]


def _gather_start(tag, own, after):
    transfers = []
    for w in range(len(own)):
        def idx(x, y, c, shape2=own[w].shape[1:]):
            return (_chip(x, y), *_half(shape2, c))
        transfers += [(w, idx, w, idx, rel) for rel in ICI_RELS]
    send_sems, recv_sems, bufs, token = _push_start(f"gather_{tag}_start", own, transfers, after)
    return (tag, send_sems, recv_sems, bufs, transfers), token


def _forward_transfers(part):
    second = []
    for w, p in enumerate(part):
        for rel in ICI_RELS:
            def idx(x, y, c, shape2=p.shape[1:], rel=rel):
                return (_chip(x ^ rel[0], y ^ rel[1]), *_half(shape2, c))
            second.append((w, idx, w, idx, SIBLING))
    return second


def _gather_finish(handle, after):
    tag, send_sems, recv_sems, bufs, transfers = handle
    part = _push_wait(f"gather_{tag}_wait", send_sems, recv_sems, bufs, transfers, after)
    outs = [jax.ShapeDtypeStruct(p.shape, p.dtype) for p in part]
    second = [(("out", sb), sfn, db, dfn, rel) for sb, sfn, db, dfn, rel in _forward_transfers(part)]
    return _push(f"gather_{tag}_d2d", list(part), outs, second, aliases={w: w for w in range(len(part))})


def _gather_mid(handle, after):
    tag, send_sems, recv_sems, bufs, transfers = handle
    part = _push_wait(f"gather_{tag}_wait", send_sems, recv_sems, bufs, transfers, after)
    second = _forward_transfers(part)
    s2, r2, bufs2, token = _push_start(f"gather_{tag}_d2d_start", part, second)
    return (tag, s2, r2, bufs2, second), token


def _gather_end(handle, after):
    tag, send_sems, recv_sems, bufs, transfers = handle
    return _push_wait(f"gather_{tag}_d2d_wait", send_sems, recv_sems, bufs, transfers, after)


def _reduce_begin(tag, grads, after=()):
    n = len(grads)
    land = [lax.empty((N_CHIPS, *_half_shape(g.shape[1:])), F32) for g in grads]
    t1 = []
    for w in range(n):
        t1.append((w, lambda x, y, c, shape2=grads[w].shape[1:]: (slice(None), *_half(shape2, 1 - c)), n + w,
                   lambda x, y, c: (slice(None), slice(None), slice(None)), SIBLING))
    send_sems, recv_sems, bufs, token = _push_start(f"reduce_{tag}_d2d_start", list(grads) + land, t1, after)
    return (tag, send_sems, recv_sems, bufs, t1), token


def _reduce_start(handle, where, after, after_start=()):
    tag, send_sems, recv_sems, bufs, t1 = handle
    n = len(bufs) // 2
    res = _push_wait(f"reduce_{tag}_d2d_wait", send_sems, recv_sems, bufs, t1, after)
    grads, from_sib = res[:n], res[n:]
    pair = [_pair_sum(f"pair_sum_{tag}{w}", where, grads[w], from_sib[w]) for w in range(n)]
    land = [lax.empty((3, *_half_shape(g.shape[1:])), BF16) for g in grads]
    t3 = []
    for w in range(n):
        for r, rel in enumerate(ICI_RELS):
            t3.append((w, lambda x, y, c, rel=rel: (_chip(x ^ rel[0], y ^ rel[1]), slice(None), slice(None)), n + w,
                       lambda x, y, c, r=r: (r, slice(None), slice(None)), rel))
    send_sems, recv_sems, bufs, token = _push_start(f"reduce_{tag}_start", pair + land, t3, after_start)
    return (tag, send_sems, recv_sems, bufs, t3, list(grads), list(from_sib)), token


def _reduce_finish(handle, where, after):
    tag, send_sems, recv_sems, bufs, t3, grads, from_sib = handle
    n = len(grads)
    from_chips = _push_wait(f"reduce_{tag}_wait", send_sems, recv_sems, bufs, t3, after)[n:]
    mine = [_chip_sum(f"chip_sum_{tag}{w}", where, grads[w], from_sib[w], from_chips[w]) for w in range(n)]
    t5 = []
    for w in range(n):
        def idx(x, y, c, shape2=grads[w].shape[1:]):
            return _half(shape2, c)
        t5.append((w, idx, w, idx, SIBLING))
    send_sems, recv_sems, bufs, token = _push_start(f"reduce_{tag}_swap_start", mine, t5)
    return (tag, send_sems, recv_sems, bufs, t5), token


def _reduce_end(handle, after):
    tag, send_sems, recv_sems, bufs, t5 = handle
    return _push_wait(f"reduce_{tag}_swap_wait", send_sems, recv_sems, bufs, t5, after)


def _pair_sum(name, where, g, from_sib):
    _, hr, hc = from_sib.shape
    tile = _pick(hr, max(16, (1 << 19) // hc // 16 * 16), 16)
    nb = hr // tile
    slot = lambda k, wh: (wh[1] + 1 + k) % N_CHIPS
    if _rows_split(g.shape[1:]):
        mine = lambda k, i, wh: (slot(k, wh), wh[0] * nb + i, 0)
    else:
        mine = lambda k, i, wh: (slot(k, wh), i, wh[0])

    def body(where_ref, g_ref, s_ref, o_ref):
        o_ref[...] = (g_ref[...] + s_ref[...]).astype(BF16)

    return pl.pallas_call(
        body, name=name,
        grid_spec=pltpu.PrefetchScalarGridSpec(
            num_scalar_prefetch=1, grid=(N_CHIPS - 1, nb),
            in_specs=[pl.BlockSpec((None, tile, hc), mine),
                      pl.BlockSpec((None, tile, hc), lambda k, i, wh: (slot(k, wh), i, 0))],
            out_specs=pl.BlockSpec((None, tile, hc), lambda k, i, wh: (slot(k, wh), i, 0))),
        out_shape=jax.ShapeDtypeStruct((N_CHIPS, hr, hc), BF16),
        compiler_params=_params(("parallel", "parallel")),
    )(where, g, from_sib)


def _chip_sum(name, where, g, from_sib, from_chips):
    _, R, C = g.shape
    _, hr, hc = from_sib.shape
    tile = _pick(hr, max(16, (1 << 19) // hc // 16 * 16), 16)
    nb = hr // tile
    if _rows_split((R, C)):
        mine = lambda i, wh: (wh[0] * nb + i, 0)
    else:
        mine = lambda i, wh: (i, wh[0])

    def body(where_ref, g_ref, s_ref, r_ref, o_ref):
        acc = g_ref[...] + s_ref[...]
        for r in range(3):
            acc = acc + r_ref[r].astype(F32)
        o_ref[...] = acc

    return pl.pallas_call(
        body, name=name,
        grid_spec=pltpu.PrefetchScalarGridSpec(
            num_scalar_prefetch=1, grid=(nb,),
            in_specs=[pl.BlockSpec((None, tile, hc), lambda i, wh: (wh[1], *mine(i, wh))),
                      pl.BlockSpec((None, tile, hc), lambda i, wh: (wh[1], i, 0)),
                      pl.BlockSpec((3, tile, hc), lambda i, wh: (0, i, 0))],
            out_specs=pl.BlockSpec((tile, hc), mine)),
        out_shape=jax.ShapeDtypeStruct((R, C), F32),
        compiler_params=_params(("parallel",)),
    )(where, g, from_sib, from_chips)


def _allgather8(name, blk, after=()):
    m_per, n = blk.shape
    rels = [(0, 0, 1), (1, 0, 0), (0, 1, 0), (1, 1, 0), (1, 0, 1), (0, 1, 1), (1, 1, 1)]

    def body(x_ref, *rest):
        out_ref, send_sems, recv_sems = rest[len(after):]
        me = (lax.axis_index("x"), lax.axis_index("y"), lax.axis_index("c"))
        my_rows = out_ref.at[pl.ds((4 * me[0] + 2 * me[1] + me[2]) * m_per, m_per), :]
        out_ref[pl.ds((4 * me[0] + 2 * me[1] + me[2]) * m_per, m_per), :] = x_ref[...]
        copies = []
        for t, rel in enumerate(rels):
            cp = pltpu.make_async_remote_copy(src_ref=x_ref, dst_ref=my_rows, send_sem=send_sems.at[t],
                                              recv_sem=recv_sems.at[t], device_id=_flip(me, rel),
                                              device_id_type=MESH)
            cp.start()
            copies.append(cp)
        for cp in copies:
            cp.wait()

    return pl.pallas_call(
        body, name=name, out_shape=jax.ShapeDtypeStruct((N_DEV * m_per, n), blk.dtype),
        in_specs=[pl.BlockSpec(memory_space=pltpu.VMEM)] + [pl.BlockSpec(memory_space=pl.ANY) for _ in after],
        out_specs=pl.BlockSpec(memory_space=pltpu.VMEM),
        scratch_shapes=[pltpu.SemaphoreType.DMA((7,)), pltpu.SemaphoreType.DMA((7,))],
        compiler_params=pltpu.CompilerParams(has_side_effects=True, vmem_limit_bytes=VMEM_LIMIT),
    )(blk, *after)


def _rope_tables(seq_len, dim, reps):
    rows = seq_len // GRID_W
    t = np.arange(seq_len)
    row, col = (t // GRID_W).astype(np.float32), (t % GRID_W).astype(np.float32)
    half = dim // 2
    inv = jnp.asarray(ROPE_THETA, F32) ** (-jnp.arange(0, half, 2, dtype=F32) / half)
    ang_r = jnp.asarray(row)[:, None] * inv[None, :]
    ang_c = jnp.asarray(col)[:, None] * inv[None, :]
    ang = jnp.concatenate([ang_r, ang_r, ang_c, ang_c], axis=-1)
    cos, sin = jnp.cos(ang), jnp.sin(ang)
    low = (np.arange(dim) % (dim // 2)) < (dim // 4)
    s_lo = jnp.where(jnp.asarray(low)[None, :], -sin, 0.0)
    s_hi = jnp.where(jnp.asarray(low)[None, :], 0.0, sin)
    del rows
    return tuple(jnp.tile(a, (1, reps)) for a in (cos, s_lo, s_hi))


def _make_rope(shift):
    def up(v):
        return pltpu.roll(v, LANE - shift, 1)

    def down(v):
        return pltpu.roll(v, shift, 1)

    @jax.custom_vjp
    def rope(v, cos, s_lo, s_hi):
        return v * cos + up(v) * s_lo + down(v) * s_hi

    def fwd(v, cos, s_lo, s_hi):
        return rope(v, cos, s_lo, s_hi), (cos, s_lo, s_hi)

    def bwd(res, dy):
        cos, s_lo, s_hi = res
        return dy * cos + down(dy * s_lo) + up(dy * s_hi), jnp.zeros_like(cos), jnp.zeros_like(cos), jnp.zeros_like(cos)

    rope.defvjp(fwd, bwd)
    return rope


_rope_a = _make_rope(HEAD_DIM // 4)
_rope_b = _make_rope(QK_ROPE // 4)


def _prep(segs, tabs, g_q, g_k, g_ckv):
    qa, ka, va, qn, qp, ckv, kpe = segs
    ta, tb = tabs
    q_a = jnp.concatenate([_rope_a(_rn(s) * g_q, *ta) for s in qa], axis=1)
    k_a = jnp.concatenate([_rope_a(_rn(s) * g_k, *ta) for s in ka], axis=1)
    cat = []
    for h in range(HB):
        cat += [qn[h], _rope_b(qp[h], *tb)]
    q_cat = jnp.concatenate(cat, axis=1)
    return q_a, k_a, va, q_cat, _rn(ckv) * g_ckv, _rope_b(kpe, *tb)


def _split_proj(proj):
    qa = [proj[:, O_QA + h * LANE:O_QA + (h + 1) * LANE] for h in range(HA)]
    ka = [proj[:, O_KA + h * LANE:O_KA + (h + 1) * LANE] for h in range(HKV)]
    va = proj[:, O_VA:O_QN]
    qn = [proj[:, O_QN + h * LANE:O_QN + (h + 1) * LANE] for h in range(HB)]
    qp = [proj[:, O_QP + h * LANE:O_QP + (h + 1) * LANE] for h in range(HB)]
    return qa, ka, va, qn, qp, proj[:, O_CKV:O_KPE], proj[:, O_KPE:W_INP]


def _permute_w_in(nat):
    K = nat.shape[1]
    qb = nat[W_QA + 2 * W_KA:W_QA + 2 * W_KA + W_QB].reshape(HB, QK_B, K)
    qp = jnp.concatenate([qb[:, QK_NOPE:], jnp.zeros((HB, LANE - QK_ROPE, K), nat.dtype)], axis=1)
    tail = nat[W_QA + 2 * W_KA + W_QB:]
    return jnp.concatenate([nat[:O_QN], qb[:, :QK_NOPE].reshape(HB * QK_NOPE, K), qp.reshape(HB * LANE, K),
                            tail, jnp.zeros((LANE - QK_ROPE, K), nat.dtype)], axis=0)


def _unpermute_pieces():
    pieces = [(0, 0, O_QN)]
    for h in range(HB):
        pieces.append((O_QN + h * QK_NOPE, O_QN + h * QK_B, QK_NOPE))
        pieces.append((O_QP + h * LANE, O_QN + h * QK_B + QK_NOPE, QK_ROPE))
    pieces.append((O_CKV, O_QN + W_QB, KV_RANK))
    pieces.append((O_KPE, O_QN + W_QB + KV_RANK, QK_ROPE))
    return pieces


def kernel(x, c, w_ada, b_ada, g_pre_attn, w_in, g_q_a, g_k_a, g_ckv, w_kv_b, g_out_a, g_out_b, w_out, g_post_attn, g_pre_mlp, w_mlp_in, w_mlp_out, g_post_mlp, loss_target, m_w_ada, m_b_ada, m_g_pre_attn, m_w_in, m_g_q_a, m_g_k_a, m_g_ckv, m_w_kv_b, m_g_out_a, m_g_out_b, m_w_out, m_g_post_attn, m_g_pre_mlp, m_w_mlp_in, m_w_mlp_out, m_g_post_mlp, v_w_ada, v_b_ada, v_g_pre_attn, v_w_in, v_g_q_a, v_g_k_a, v_g_ckv, v_w_kv_b, v_g_out_a, v_g_out_b, v_w_out, v_g_post_attn, v_g_pre_mlp, v_w_mlp_in, v_w_mlp_out, v_g_post_mlp):
    S, D = x.shape[1], x.shape[2]
    x2d, tgt = x[0], loss_target[0]
    ix, iy, ic = lax.axis_index("x"), lax.axis_index("y"), lax.axis_index("c")
    j_me = _chip(ix, iy)
    e_me = 4 * ix + 2 * iy + ic
    T = _pick(S, 512, 8)
    TB = _pick(S, 256, 8)

    where = jnp.stack([ic, j_me]).astype(jnp.int32)
    DFF = w_mlp_out.shape[1] * N_CHIPS

    c_all = _allgather8("gather_c", c.reshape(8, D // 8)).reshape(N_DEV, D)
    w_in_t = jnp.swapaxes(w_in[0], 0, 1)
    h_qkv, tok_a = _gather_start("qkv", _gather_casts("qkv", where, [w_in_t, w_kv_b[0]]), (c_all,))
    later = [_gather_casts(tag, where, [w[0]], (tok_a,))
             for tag, w in (("out", w_out), ("up", w_mlp_in), ("down", w_mlp_out))]

    def silu_fn(cc):
        return (cc * (1.0 / (1.0 + jnp.exp(-cc))),), ()

    (c_act,), _ = _rowwise("silu_c", silu_fn, [c_all], [], [(D, BF16)], [], N_DEV)
    ca_pad = jnp.concatenate([c_act, jnp.zeros_like(c_act)], axis=0)
    n_mod = w_ada.shape[2]
    b_cols = lax.dynamic_slice(b_ada, (0, j_me * n_mod), (1, n_mod))
    (mod_part,) = _mm_nn("mod_mm", ca_pad, w_ada, [F32], epilogue=lambda acc, b: (acc + b,),
                         extras=(jnp.broadcast_to(b_cols, (16, n_mod)),))
    mod_all = _allgather8("gather_mod", mod_part[:8] + tok_a[0, 0],
                          after=[c_[0] for c_ in later]).reshape(N_DEV, 8, n_mod)
    mod = jnp.concatenate([lax.dynamic_slice(mod_all, (2 * j, e_me, 0), (1, 1, n_mod))[0] for j in range(N_CHIPS)],
                          axis=1)

    m_qkv, tok_q = _gather_mid(h_qkv, mod_all)
    h_out, tok_b = _gather_start("out", later[0], (tok_q,))
    h_up, tok_c = _gather_start("up", later[1], (tok_b,))
    h_down, tok_d = _gather_start("down", later[2], (tok_c,))
    mod = mod + tok_d[0, 0]
    sh_a, sc_a, gt_a, sh_m, sc_m, gt_m = [mod[:, i * D:(i + 1) * D] for i in range(6)]

    tabs_a = _rope_tables(S, HEAD_DIM, 1)
    tabs_b = _rope_tables(S, QK_ROPE, LANE // QK_ROPE)

    def pre_attn_fn(xt, g, sc, sh):
        return (_rn(xt) * g * (1.0 + sc) + sh,), ()

    (h_b,), _ = _rowwise("pre_attn", pre_attn_fn, [x2d], [g_pre_attn, sc_a, sh_a], [(D, BF16)], [], T)
    g_in, g_kvb = _gather_end(m_qkv, h_b)
    wpt = _permute_w_in(g_in.reshape(W_IN, D))[None]
    (proj,) = _mm_nt("proj_mm", h_b, wpt, [F32])

    def prep_fn(pt, ca_, sla, sha, cb_, slb, shb, gq, gk, gc):
        return _prep(_split_proj(pt), ((ca_, sla, sha), (cb_, slb, shb)), gq, gk, gc), ()

    (qa_b, ka_b, va_b, qcat_b, ckvn_b, kpe_b), _ = _rowwise(
        "prep", prep_fn, [proj, *tabs_a, *tabs_b], [g_q_a, g_k_a, g_ckv],
        [(W_QA, BF16), (W_KA, BF16), (W_KA, BF16), (2 * HB * LANE, BF16), (KV_RANK, BF16), (LANE, BF16)], [], T)
    kcat_b, vb_b = _kv_mm("kv_mm", ckvn_b, g_kvb, kpe_b)

    TQ = _pick(S, 512, 16)
    sc_a_ = 1.0 / math.sqrt(HEAD_DIM)
    sc_b_ = 1.0 / math.sqrt(QK_B)
    o_a, lse_a = _attn_fwd("attn_a_fwd", qa_b, ka_b, va_b, HA, HA // HKV, HEAD_DIM, HEAD_DIM, sc_a_, TQ)
    m_out, tok_m = _gather_mid(h_out, o_a)
    o_b, lse_b = _attn_fwd("attn_b_fwd", qcat_b, kcat_b, vb_b, HB, 1, 2 * LANE, V_DIM, sc_b_, TQ, after=(tok_m,))
    m_up, tok_m = _gather_mid(h_up, o_b)

    def mix_fn(oa, ob, ga, gb):
        return jnp.concatenate([_rn(oa) * ga, _rn(ob) * gb], axis=1)

    (g_wout,) = _gather_end(m_out, o_b)
    w_out_f = g_wout.reshape(1, N_CHIPS * g_wout.shape[1], D)
    on_b, o2 = _rows_mm("out_mm", mix_fn, [o_a, o_b], [g_out_a, g_out_b], w_out_f, after=(tok_m,))

    def mid(xt, o2t, gta, gpa, gpm, scm, shm):
        x1 = xt + gta * (_rn(o2t) * gpa)
        return x1, _rn(x1) * gpm * (1.0 + scm) + shm

    def mid_fn(*a):
        return mid(*a), ()

    mid_vecs = [gt_a, g_post_attn, g_pre_mlp, sc_m, sh_m]
    (x1, h2_b), _ = _rowwise("mid", mid_fn, [x2d, o2], mid_vecs, [(D, F32), (D, BF16)], [], T)
    (g_mlp_in,) = _gather_end(m_up, h2_b)
    relu_b, a_b = _mm_nn("mlp_in_mm", h2_b, g_mlp_in, [BF16, BF16],
                         epilogue=lambda acc: (jnp.maximum(acc, 0.0), jnp.square(jnp.maximum(acc, 0.0))))
    (g_mlp_out,) = _gather_finish(h_down, a_b)
    w_mlp_out_f = g_mlp_out.reshape(1, DFF, D)
    (y,) = _mm_nn("mlp_out_mm", a_b, w_mlp_out_f, [F32])

    def last_fn(yt, x1t, tt, gtm, gpo):
        n, r = _rn_parts(yt)
        err = x1t + gtm * (n * gpo) - tt
        dx2 = err * (1.0 / D)
        loss = 0.5 * jnp.sum(jnp.mean(err * err, axis=-1))
        s = _rows(dx2 * n)
        return (_rn_bwd(n, r, dx2 * (gtm * gpo)), dx2), (jnp.full((1, LANE), loss, F32), gpo * s, gtm * s)

    (dy_b, dx2), (loss_v, d_gt_m, d_g_post_mlp) = _rowwise(
        "loss_bwd", last_fn, [y, x1, tgt], [gt_m, g_post_mlp], [(D, BF16), (D, F32)], [LANE, D, D], TB)

    gw_mlp_out = _mm_tn("gw_mlp_out_mm", a_b, dy_b, 1).reshape(N_CHIPS, DFF // N_CHIPS, D)
    b_down, tok = _reduce_begin("down", [gw_mlp_out])
    (du_b,) = _mm_nt("d_mlp_out_mm", dy_b, w_mlp_out_f, [BF16],
                     epilogue=lambda acc, rt: (acc * (2.0 * rt.astype(F32)),), extras=(relu_b,), after=(tok,))
    r_down, tok = _reduce_start(b_down, where, du_b)
    gw_mlp_in = _mm_tn("gw_mlp_in_mm", h2_b, du_b, N_CHIPS, after=(tok,))
    b_up, tok = _reduce_begin("up", [gw_mlp_in])
    (dh2,) = _mm_nt("d_mlp_in_mm", du_b, g_mlp_in, [F32], after=(tok,))
    r_up, tok_up = _reduce_start(b_up, where, dh2)

    def mid_bwd_fn(xt, o2t, dh2t, dx2t, gta, gpa, gpm, scm, shm):
        n2, r2 = _rn_parts(o2t)
        n1, r1 = _rn_parts(xt + gta * (n2 * gpa))
        dx1 = dx2t + _rn_bwd(n1, r1, dh2t * (gpm * (1.0 + scm)))
        s1, s2 = _rows(dh2t * n1), _rows(dx1 * n2)
        do2 = _rn_bwd(n2, r2, dx1 * (gta * gpa))
        return (dx1, do2), (gpa * s2, gta * s2, (1.0 + scm) * s1, gpm * s1, _rows(dh2t))

    (dx1, do2_b), (d_gt_a, d_g_post_attn, d_g_pre_mlp, d_sc_m, d_sh_m) = _rowwise(
        "mid_bwd", mid_bwd_fn, [x2d, o2, dh2, dx2], mid_vecs, [(D, F32), (D, BF16)], [D] * 5, TB, after=(tok_up,))

    (d_on,) = _mm_nt("d_out_mm", do2_b, w_out_f, [F32])
    gw_out = _mm_tn("gw_out_mm", on_b, do2_b, 1).reshape(N_CHIPS, w_out.shape[1], D)

    def mix_bwd_fn(oa, ob, dont, ga, gb):
        na, ra = _rn_parts(oa)
        nb, rb = _rn_parts(ob)
        da, db = dont[:, :W_QA], dont[:, W_QA:]
        return (_rn_bwd(na, ra, da * ga), _rn_bwd(nb, rb, db * gb)), (_rows(da * na), _rows(db * nb))

    (doa_b, dob_b), (d_g_out_a, d_g_out_b) = _rowwise(
        "mix_bwd", mix_bwd_fn, [o_a, o_b, d_on], [g_out_a, g_out_b], [(W_QA, BF16), (HB * V_DIM, BF16)],
        [W_QA, HB * V_DIM], TB)

    dqa, dka, dva = _attn_bwd("attn_a_bwd", qa_b, ka_b, va_b, doa_b, o_a, lse_a, HA, HA // HKV, HEAD_DIM, HEAD_DIM,
                              sc_a_, TQ)
    dqcat, dkcat, dvb = _attn_bwd("attn_b_bwd", qcat_b, kcat_b, vb_b, dob_b, o_b, lse_b, HB, 1, 2 * LANE, V_DIM,
                                  sc_b_, TQ)

    d_ckvn, dkv_b, dkpe = _kv_mm_bwd("d_kv_mm", dkcat, dvb, g_kvb)
    gw_kvb = _mm_tn("gw_kv_mm", ckvn_b, dkv_b, N_CHIPS)
    b_kvb, tok_kvb = _reduce_begin("okv", [gw_out, gw_kvb])

    def prep_bwd_fn(pt, dqat, dkat, dvat, dqct, dckt, dkpt, ca_, sla, sha, cb_, slb, shb, gq, gk, gc):
        tabs = ((ca_, sla, sha), (cb_, slb, shb))
        _, vjp = jax.vjp(lambda sg, a1, a2, a3: _prep(sg, tabs, a1, a2, a3), _split_proj(pt), gq, gk, gc)
        (dqa_s, dka_s, dva_s, dqn_s, dqp_s, dck_s, dkp_s), dgq, dgk, dgc = vjp((dqat, dkat, dvat, dqct, dckt, dkpt))
        dproj = jnp.concatenate([*dqa_s, *dka_s, dva_s, *dqn_s, *dqp_s, dck_s, dkp_s], axis=1)
        return (dproj,), (dgq, dgk, dgc)

    (dproj_b,), (d_g_q_a, d_g_k_a, d_g_ckv) = _rowwise(
        "prep_bwd", prep_bwd_fn, [proj, dqa, dka, dva, dqcat, d_ckvn, dkpe, *tabs_a, *tabs_b],
        [g_q_a, g_k_a, g_ckv], [(W_INP, BF16)], [HEAD_DIM, HEAD_DIM, KV_RANK], TB, after=(tok_kvb,))

    r_kvb, tok = _reduce_start(b_kvb, where, dproj_b)
    (dh,) = _mm_nn("d_proj_mm", dproj_b, wpt, [F32], after=(tok,))
    gw_in = _mm_tn_rows("gw_in_mm", dproj_b, h_b, W_IN, _unpermute_pieces()).reshape(N_CHIPS, W_IN // N_CHIPS, D)
    b_in, tok = _reduce_begin("in", [gw_in])

    def pre_attn_bwd_fn(xt, dht, dx1t, g, sc, sh):
        n, r = _rn_parts(xt)
        s = _rows(dht * n)
        return (dx1t + _rn_bwd(n, r, dht * (g * (1.0 + sc))),), ((1.0 + sc) * s, g * s, _rows(dht))

    (grad_x,), (d_g_pre_attn, d_sc_a, d_sh_a) = _rowwise(
        "pre_attn_bwd", pre_attn_bwd_fn, [x2d, dh, dx1], [g_pre_attn, sc_a, sh_a], [(D, F32)], [D] * 3, TB,
        after=(tok,))

    small_names = ["b_ada", "g_pre_attn", "g_q_a", "g_k_a", "g_ckv", "g_out_a", "g_out_b", "g_post_attn",
                   "g_pre_mlp", "g_post_mlp"]
    small_local = jnp.concatenate([d_sh_a, d_sc_a, d_gt_a, d_sh_m, d_sc_m, d_gt_m, d_g_pre_attn, d_g_q_a, d_g_k_a,
                                   d_g_ckv, d_g_out_a, d_g_out_b, d_g_post_attn, d_g_pre_mlp, d_g_post_mlp,
                                   loss_v[:, :8]], axis=1)
    n_sent = small_local.shape[1]
    n_small = n_sent - 8
    small_all = _allgather8("gather_small", small_local.reshape(8, n_sent // 8)).reshape(N_DEV, n_sent)

    def sum8_fn(t):
        return (), (jnp.sum(t, axis=0, keepdims=True),)

    r_in, tok_in = _reduce_start(b_in, where, grad_x, (small_all,))
    _, (sent_sum,) = _rowwise("sum_small", sum8_fn, [small_all], [], [], [n_sent], N_DEV, after=(tok_in,))
    small_sum, loss = sent_sum[:, :n_small], sent_sum[0, n_small]
    dmod_cols = lax.dynamic_slice(small_all, (0, j_me * n_mod), (N_DEV, n_mod))
    dmod_pad = jnp.concatenate([dmod_cols, jnp.zeros_like(dmod_cols)], axis=0).astype(BF16)

    weights = dict(w_ada=w_ada, b_ada=b_ada, g_pre_attn=g_pre_attn, w_in=w_in, g_q_a=g_q_a, g_k_a=g_k_a, g_ckv=g_ckv,
                   w_kv_b=w_kv_b, g_out_a=g_out_a, g_out_b=g_out_b, w_out=w_out, g_post_attn=g_post_attn,
                   g_pre_mlp=g_pre_mlp, w_mlp_in=w_mlp_in, w_mlp_out=w_mlp_out, g_post_mlp=g_post_mlp)
    ms = dict(w_ada=m_w_ada, b_ada=m_b_ada, g_pre_attn=m_g_pre_attn, w_in=m_w_in, g_q_a=m_g_q_a, g_k_a=m_g_k_a,
              g_ckv=m_g_ckv, w_kv_b=m_w_kv_b, g_out_a=m_g_out_a, g_out_b=m_g_out_b, w_out=m_w_out,
              g_post_attn=m_g_post_attn, g_pre_mlp=m_g_pre_mlp, w_mlp_in=m_w_mlp_in, w_mlp_out=m_w_mlp_out,
              g_post_mlp=m_g_post_mlp)
    vs = dict(w_ada=v_w_ada, b_ada=v_b_ada, g_pre_attn=v_g_pre_attn, w_in=v_w_in, g_q_a=v_g_q_a, g_k_a=v_g_k_a,
              g_ckv=v_g_ckv, w_kv_b=v_w_kv_b, g_out_a=v_g_out_a, g_out_b=v_g_out_b, w_out=v_w_out,
              g_post_attn=v_g_post_attn, g_pre_mlp=v_g_pre_mlp, w_mlp_in=v_w_mlp_in, w_mlp_out=v_w_mlp_out,
              g_post_mlp=v_g_post_mlp)
    order = list(weights)
    grads, deltas, new_m, new_v = {}, {}, {}, {}

    def update(name, g, after=()):
        view = (lambda t: jnp.swapaxes(t, 0, 1)) if name == "w_in" else (lambda t: t)
        d_, m_, v_, g_ = _adamw("adamw_" + name, view(weights[name][0]), g, view(ms[name][0]), view(vs[name][0]),
                                after, rewrite_g=True)
        grads[name], deltas[name], new_m[name], new_v[name] = (view(t)[None] for t in (g_, d_, m_, v_))
        return d_

    chain = ((("w_mlp_out",), r_down), (("w_mlp_in",), r_up), (("w_out", "w_kv_b"), r_kvb), (("w_in",), r_in))
    swap, tok = _reduce_finish(chain[0][1], where, small_sum)
    last, m_, v_, g_ = _adamw_outer("adamw_w_ada", w_ada[0], ca_pad, dmod_pad, m_w_ada[0], v_w_ada[0], (tok,))
    grads["w_ada"], deltas["w_ada"], new_m["w_ada"], new_v["w_ada"] = g_[None], last[None], m_[None], v_[None]
    for k, (names, _) in enumerate(chain):
        gs = _reduce_end(swap, last)
        if k + 1 < len(chain):
            swap, tok = _reduce_finish(chain[k + 1][1], where, gs[0])
        for name, g in zip(names, gs):
            last = update(name, g, (tok,))

    cat = lambda d: jnp.concatenate([d[n] for n in small_names], axis=1)
    d_, m_, v_ = _adamw("adamw_small", cat(weights), small_sum, cat(ms), cat(vs))
    off = 0
    for n in small_names:
        w_ = weights[n].shape[1]
        grads[n], deltas[n], new_m[n], new_v[n] = (a[:, off:off + w_] for a in (small_sum, d_, m_, v_))
        off += w_

    return (loss, grad_x[None], *[grads[n] for n in order], *[deltas[n] for n in order],
            *[new_m[n] for n in order], *[new_v[n] for n in order])
```

```python
import functools
import math

import numpy as np
import jax
import jax.numpy as jnp
from jax import lax
from jax.experimental import pallas as pl
from jax.experimental.pallas import tpu as pltpu

F32 = jnp.float32
BF16 = jnp.bfloat16
MESH = pl.DeviceIdType.MESH

EPS = 1e-6
GRID_W = 64
ROPE_THETA = 10000.0
HEAD_DIM = 128
HA = 8
HKV = 2
HB = 8
QK_NOPE = 128
QK_ROPE = 64
V_DIM = 128
KV_RANK = 512
QK_B = QK_NOPE + QK_ROPE
W_QA = HA * HEAD_DIM
W_KA = HKV * HEAD_DIM
W_QB = HB * QK_B
W_IN = W_QA + 2 * W_KA + W_QB + KV_RANK + QK_ROPE
N_CHIPS = 4
N_DEV = 8

LANE = 128
O_QA = 0
O_KA = O_QA + W_QA
O_VA = O_KA + W_KA
O_QN = O_VA + W_KA
O_QP = O_QN + HB * QK_NOPE
O_CKV = O_QP + HB * LANE
O_KPE = O_CKV + KV_RANK
W_INP = O_KPE + LANE

ADAM_LR = 0.001
ADAM_B1 = 0.9
ADAM_B2 = 0.999
ADAM_EPS = 1e-08
ADAM_WD = 0.01
ADAM_STEP = 10

VMEM_LIMIT = 56 * 1024 * 1024
HEADS_PER_STEP = 2


def _params(sem):
    return pltpu.CompilerParams(dimension_semantics=sem, vmem_limit_bytes=VMEM_LIMIT)


def _pick(dim, target, unit):
    best = None
    t = unit
    while t <= min(dim, target):
        if dim % t == 0:
            best = t
        t += unit
    return dim if best is None else best


def _rn(x):
    return x * lax.rsqrt(jnp.mean(x * x, axis=-1, keepdims=True) + EPS)


def _rn_parts(x):
    r = lax.rsqrt(jnp.mean(x * x, axis=-1, keepdims=True) + EPS)
    return x * r, r


def _rn_bwd(n, r, dn):
    return r * (dn - n * jnp.mean(dn * n, axis=-1, keepdims=True))


def _rows(t):
    return jnp.sum(t, axis=0, keepdims=True)


def _rowwise(name, fn, rows, vecs, row_outs, vec_outs, tile, after=()):
    S = rows[0].shape[0]
    assert S % tile == 0
    n_r, n_v, n_ro, n_vo, n_a = len(rows), len(vecs), len(row_outs), len(vec_outs), len(after)

    def body(*refs):
        r_in = refs[:n_r]
        v_in = refs[n_r:n_r + n_v]
        r_out = refs[n_r + n_v + n_a:n_r + n_v + n_a + n_ro]
        v_out = refs[n_r + n_v + n_a + n_ro:]
        ro, vo = fn(*[r[...] for r in r_in], *[v[...] for v in v_in])
        for ref, val in zip(r_out, ro):
            ref[...] = val.astype(ref.dtype)
        if n_vo:
            @pl.when(pl.program_id(0) == 0)
            def _():
                for ref in v_out:
                    ref[...] = jnp.zeros(ref.shape, ref.dtype)
            for ref, val in zip(v_out, vo):
                ref[...] += val

    in_specs = [pl.BlockSpec((tile, a.shape[1]), lambda i: (i, 0)) for a in rows]
    in_specs += [pl.BlockSpec(a.shape, lambda i: (0, 0)) for a in vecs]
    in_specs += [pl.BlockSpec(memory_space=pl.ANY) for _ in after]
    out_specs = [pl.BlockSpec((tile, w), lambda i: (i, 0)) for w, _ in row_outs]
    out_specs += [pl.BlockSpec((1, w), lambda i: (0, 0)) for w in vec_outs]
    out_shape = [jax.ShapeDtypeStruct((S, w), dt) for w, dt in row_outs]
    out_shape += [jax.ShapeDtypeStruct((1, w), F32) for w in vec_outs]
    res = pl.pallas_call(
        body, name=name, grid=(S // tile,), in_specs=in_specs, out_specs=out_specs, out_shape=out_shape,
        compiler_params=_params(("arbitrary",)),
    )(*rows, *vecs, *after)
    return res[:n_ro], res[n_ro:]


CONTRACT_TILE = 2048
MM_TILE_BYTES = 40 * 1024 * 1024


def _rows_that_fit(M, tile_bytes):
    t = _pick(M, 1024, 16)
    while t % 32 == 0 and tile_bytes(t) > MM_TILE_BYTES:
        t //= 2
    return t


def _accumulate(acc, part, step, n_steps, finish):
    if n_steps == 1:
        finish(part)
        return

    @pl.when(step == 0)
    def _():
        acc[...] = part

    @pl.when((step > 0) & (step < n_steps - 1))
    def _():
        acc[...] += part

    @pl.when(step == n_steps - 1)
    def _():
        finish(acc[...] + part)


def _mm_nn(name, a, b3, out_dtypes, epilogue=None, extras=(), after=()):
    M, K = a.shape
    nB, K2, Nsh = b3.shape
    assert K == K2
    N = nB * Nsh
    tn, tk = _pick(Nsh, 1536, LANE), _pick(K, CONTRACT_TILE, LANE)
    out_bytes = sum(jnp.dtype(dt).itemsize for dt in out_dtypes) + sum(e.dtype.itemsize for e in extras)
    tm = _rows_that_fit(M, lambda t: 2 * (t * tk * a.dtype.itemsize + tk * tn * b3.dtype.itemsize)
                        + t * tn * (4 * (K > tk) + 2 * out_bytes))
    nps, nk, ne, no, na = Nsh // tn, K // tk, len(extras), len(out_dtypes), len(after)

    def body(a_ref, b_ref, *rest):
        e_refs, o_refs = rest[:ne], rest[ne + na:ne + na + no]

        def finish(total):
            vals = (total,) if epilogue is None else epilogue(total, *[e[...] for e in e_refs])
            for ref, val in zip(o_refs, vals):
                ref[...] = val.astype(ref.dtype)

        part = jnp.dot(a_ref[...].astype(BF16), b_ref[...].astype(BF16), preferred_element_type=F32)
        _accumulate(rest[-1], part, pl.program_id(2), nk, finish)

    in_specs = [pl.BlockSpec((tm, tk), lambda i, j, k: (i, k)),
                pl.BlockSpec((None, tk, tn), lambda i, j, k: (j // nps, k, j % nps))]
    in_specs += [pl.BlockSpec((tm, tn), lambda i, j, k: (i, j)) for _ in extras]
    in_specs += [pl.BlockSpec(memory_space=pl.ANY) for _ in after]
    return pl.pallas_call(
        body, name=name, grid=(M // tm, N // tn, nk), in_specs=in_specs,
        out_specs=[pl.BlockSpec((tm, tn), lambda i, j, k: (i, j)) for _ in out_dtypes],
        out_shape=[jax.ShapeDtypeStruct((M, N), dt) for dt in out_dtypes],
        scratch_shapes=[pltpu.VMEM((tm, tn), F32)] if nk > 1 else [],
        compiler_params=_params(("parallel", "parallel", "arbitrary")),
    )(a, b3, *extras, *after)


def _mm_nt(name, a, b3, out_dtypes, epilogue=None, extras=(), after=()):
    M, N = a.shape
    nB, K, Nsh = b3.shape
    assert N == nB * Nsh
    to, tn = _pick(K, 1536, LANE), _pick(Nsh, CONTRACT_TILE, LANE)
    out_bytes = sum(jnp.dtype(dt).itemsize for dt in out_dtypes) + sum(e.dtype.itemsize for e in extras)
    tm = _rows_that_fit(M, lambda t: 2 * (t * tn * a.dtype.itemsize + to * tn * b3.dtype.itemsize)
                        + t * to * (4 * (N > tn) + 2 * out_bytes))
    nps, nn, ne, no, na = Nsh // tn, N // tn, len(extras), len(out_dtypes), len(after)

    def body(a_ref, b_ref, *rest):
        e_refs, o_refs = rest[:ne], rest[ne + na:ne + na + no]

        def finish(total):
            vals = (total,) if epilogue is None else epilogue(total, *[e[...] for e in e_refs])
            for ref, val in zip(o_refs, vals):
                ref[...] = val.astype(ref.dtype)

        part = lax.dot_general(a_ref[...].astype(BF16), b_ref[...].astype(BF16),
                               (((1,), (1,)), ((), ())), preferred_element_type=F32)
        _accumulate(rest[-1], part, pl.program_id(2), nn, finish)

    in_specs = [pl.BlockSpec((tm, tn), lambda i, o, n: (i, n)),
                pl.BlockSpec((None, to, tn), lambda i, o, n: (n // nps, o, n % nps))]
    in_specs += [pl.BlockSpec((tm, to), lambda i, o, n: (i, o)) for _ in extras]
    in_specs += [pl.BlockSpec(memory_space=pl.ANY) for _ in after]
    return pl.pallas_call(
        body, name=name, grid=(M // tm, K // to, nn), in_specs=in_specs,
        out_specs=[pl.BlockSpec((tm, to), lambda i, o, n: (i, o)) for _ in out_dtypes],
        out_shape=[jax.ShapeDtypeStruct((M, K), dt) for dt in out_dtypes],
        scratch_shapes=[pltpu.VMEM((tm, to), F32)] if nn > 1 else [],
        compiler_params=_params(("parallel", "parallel", "arbitrary")),
    )(a, b3, *extras, *after)


def _mm_tn(name, a, g, nB, after=()):
    M, K = a.shape
    M2, N = g.shape
    assert M == M2 and N % nB == 0
    Nsh = N // nB
    tk, tn, tm = _pick(K, 1536, LANE), _pick(Nsh, 1024, LANE), _pick(M, CONTRACT_TILE, 16)
    nps, nm = Nsh // tn, M // tm

    def body(a_ref, g_ref, *rest):
        o_ref = rest[len(after)]

        def finish(total):
            o_ref[...] = total

        part = lax.dot_general(a_ref[...].astype(BF16), g_ref[...].astype(BF16),
                               (((0,), (0,)), ((), ())), preferred_element_type=F32)
        _accumulate(rest[-1], part, pl.program_id(2), nm, finish)

    return pl.pallas_call(
        body, name=name, grid=(K // tk, N // tn, nm),
        in_specs=[pl.BlockSpec((tm, tk), lambda k, n, m: (m, k)),
                  pl.BlockSpec((tm, tn), lambda k, n, m: (m, n))] + [pl.BlockSpec(memory_space=pl.ANY) for _ in after],
        out_specs=pl.BlockSpec((None, tk, tn), lambda k, n, m: (n // nps, k, n % nps)),
        out_shape=jax.ShapeDtypeStruct((nB, K, Nsh), F32),
        scratch_shapes=[pltpu.VMEM((tk, tn), F32)] if nm > 1 else [],
        compiler_params=_params(("parallel", "parallel", "arbitrary")),
    )(a, g, *after)


def _rows_mm(name, fn, rows, vecs, b3, after=()):
    S = rows[0].shape[0]
    _, K, N = b3.shape
    tm, tn = _pick(S, 1024, 16), _pick(N, 512, LANE)
    nr, nv, na = len(rows), len(vecs), len(after)

    def body(*refs):
        b_ref = refs[nr + nv]
        a_out, o_ref, a_scr = refs[nr + nv + 1 + na:]

        @pl.when(pl.program_id(1) == 0)
        def _():
            a = fn(*[r[...] for r in refs[:nr]], *[v[...] for v in refs[nr:nr + nv]]).astype(BF16)
            a_scr[...] = a
            a_out[...] = a

        o_ref[...] = jnp.dot(a_scr[...], b_ref[...], preferred_element_type=F32)

    in_specs = [pl.BlockSpec((tm, r.shape[1]), lambda i, j: (i, 0)) for r in rows]
    in_specs += [pl.BlockSpec(v.shape, lambda i, j: (0, 0)) for v in vecs]
    in_specs += [pl.BlockSpec((None, K, tn), lambda i, j: (0, 0, j))]
    in_specs += [pl.BlockSpec(memory_space=pl.ANY) for _ in after]
    return pl.pallas_call(
        body, name=name, grid=(S // tm, N // tn), in_specs=in_specs,
        out_specs=[pl.BlockSpec((tm, K), lambda i, j: (i, 0)), pl.BlockSpec((tm, tn), lambda i, j: (i, j))],
        out_shape=[jax.ShapeDtypeStruct((S, K), BF16), jax.ShapeDtypeStruct((S, N), F32)],
        scratch_shapes=[pltpu.VMEM((tm, K), BF16)],
        compiler_params=_params(("parallel", "arbitrary")),
    )(*rows, *vecs, b3, *after)


def _kv_mm(name, ckvn, w3, kpe):
    S, K = ckvn.shape
    nB, _, Nsh = w3.shape
    heads = Nsh // (2 * LANE)
    tm = _pick(S, 1024, 16)

    def body(a_ref, b_ref, kpe_ref, kc_ref, v_ref):
        acc = jnp.dot(a_ref[...], b_ref[...], preferred_element_type=F32)
        for h in range(heads):
            kc_ref[:, 2 * h * LANE:(2 * h + 1) * LANE] = acc[:, 2 * h * LANE:(2 * h + 1) * LANE].astype(BF16)
            kc_ref[:, (2 * h + 1) * LANE:(2 * h + 2) * LANE] = kpe_ref[...]
            v_ref[:, h * LANE:(h + 1) * LANE] = acc[:, (2 * h + 1) * LANE:(2 * h + 2) * LANE].astype(BF16)

    return pl.pallas_call(
        body, name=name, grid=(S // tm, nB),
        in_specs=[pl.BlockSpec((tm, K), lambda i, j: (i, 0)), pl.BlockSpec((None, K, Nsh), lambda i, j: (j, 0, 0)),
                  pl.BlockSpec((tm, LANE), lambda i, j: (i, 0))],
        out_specs=[pl.BlockSpec((tm, Nsh), lambda i, j: (i, j)), pl.BlockSpec((tm, Nsh // 2), lambda i, j: (i, j))],
        out_shape=[jax.ShapeDtypeStruct((S, nB * Nsh), BF16), jax.ShapeDtypeStruct((S, nB * Nsh // 2), BF16)],
        compiler_params=_params(("parallel", "arbitrary")),
    )(ckvn, w3, kpe)


def _kv_mm_bwd(name, dkc, dv, w3):
    S = dkc.shape[0]
    nB, K, Nsh = w3.shape
    heads = Nsh // (2 * LANE)
    tm = _pick(S, 1024, 16)

    def body(dkc_ref, dv_ref, b_ref, dck_ref, dkv_ref, dkpe_ref):
        j = pl.program_id(1)
        pieces, pe = [], None
        for h in range(heads):
            pieces += [dkc_ref[:, 2 * h * LANE:(2 * h + 1) * LANE], dv_ref[:, h * LANE:(h + 1) * LANE]]
            piece = dkc_ref[:, (2 * h + 1) * LANE:(2 * h + 2) * LANE]
            pe = piece if pe is None else pe + piece
        dkv = jnp.concatenate(pieces, axis=1).astype(BF16)
        dkv_ref[...] = dkv
        part = lax.dot_general(dkv, b_ref[...], (((1,), (1,)), ((), ())), preferred_element_type=F32)

        @pl.when(j == 0)
        def _():
            dck_ref[...] = part
            dkpe_ref[...] = pe

        @pl.when(j > 0)
        def _():
            dck_ref[...] += part
            dkpe_ref[...] += pe

    return pl.pallas_call(
        body, name=name, grid=(S // tm, nB),
        in_specs=[pl.BlockSpec((tm, Nsh), lambda i, j: (i, j)), pl.BlockSpec((tm, Nsh // 2), lambda i, j: (i, j)),
                  pl.BlockSpec((None, K, Nsh), lambda i, j: (j, 0, 0))],
        out_specs=[pl.BlockSpec((tm, K), lambda i, j: (i, 0)), pl.BlockSpec((tm, Nsh), lambda i, j: (i, j)),
                   pl.BlockSpec((tm, LANE), lambda i, j: (i, 0))],
        out_shape=[jax.ShapeDtypeStruct((S, K), F32), jax.ShapeDtypeStruct((S, nB * Nsh), BF16),
                   jax.ShapeDtypeStruct((S, LANE), F32)],
        compiler_params=_params(("parallel", "arbitrary")),
    )(dkc, dv, w3)


def _mm_tn_rows(name, a, g, n_rows, pieces):
    M, K = a.shape
    M2, N = g.shape
    assert M == M2
    tk, tn, tm = _pick(K, 1536, LANE), _pick(N, 512, LANE), _pick(M, CONTRACT_TILE, 16)
    nk, nm = K // tk, M // tm

    def body(a_ref, g_ref, o_ref, *acc):
        kk = pl.program_id(1)

        def finish(total):
            for t in range(nk):
                @pl.when(kk == t)
                def _(t=t):
                    for src, dst, n in pieces:
                        lo, hi = max(src, t * tk), min(src + n, (t + 1) * tk)
                        if lo < hi:
                            o_ref[dst + lo - src:dst + hi - src, :] = total[lo - t * tk:hi - t * tk, :]

        part = lax.dot_general(a_ref[...].astype(BF16), g_ref[...].astype(BF16),
                               (((0,), (0,)), ((), ())), preferred_element_type=F32)
        _accumulate(acc[0] if acc else None, part, pl.program_id(2), nm, finish)

    return pl.pallas_call(
        body, name=name, grid=(N // tn, nk, nm),
        in_specs=[pl.BlockSpec((tm, tk), lambda n, k, m: (m, k)),
                  pl.BlockSpec((tm, tn), lambda n, k, m: (m, n))],
        out_specs=pl.BlockSpec((n_rows, tn), lambda n, k, m: (0, n)),
        out_shape=jax.ShapeDtypeStruct((n_rows, N), F32),
        scratch_shapes=[pltpu.VMEM((tk, tn), F32)] if nm > 1 else [],
        compiler_params=_params(("arbitrary", "arbitrary", "arbitrary")),
    )(a, g)


def _attn_fwd(name, q, k, v, n_heads, group, dqk, dv, scale, tq, after=()):
    S = q.shape[0]
    hp = 2 * HEADS_PER_STEP
    kvp = max(1, hp // group)

    def body(q_ref, k_ref, v_ref, *rest):
        o_ref, lse_ref = rest[-2:]
        for j in range(hp):
            jk = j * kvp // hp
            s = lax.dot_general(q_ref[:, j * dqk:(j + 1) * dqk], k_ref[:, jk * dqk:(jk + 1) * dqk],
                                (((1,), (1,)), ((), ())), preferred_element_type=F32)
            m = jnp.max(s, axis=-1, keepdims=True)
            e = jnp.exp((s - m) * scale)
            l = jnp.sum(e, axis=-1, keepdims=True)
            o_ref[:, j * dv:(j + 1) * dv] = jnp.dot(e.astype(BF16), v_ref[:, jk * dv:(jk + 1) * dv],
                                                    preferred_element_type=F32) * (1.0 / l)
            lse_ref[:, j * LANE:(j + 1) * LANE] = jnp.broadcast_to(m * scale + jnp.log(l), (tq, LANE))

    kv_block = lambda h, i: (0, (h * hp // group) // kvp)
    return pl.pallas_call(
        body, name=name, grid=(n_heads // hp, S // tq),
        in_specs=[pl.BlockSpec((tq, hp * dqk), lambda h, i: (i, h)),
                  pl.BlockSpec((S, kvp * dqk), kv_block),
                  pl.BlockSpec((S, kvp * dv), kv_block)] + [pl.BlockSpec(memory_space=pl.ANY) for _ in after],
        out_specs=[pl.BlockSpec((tq, hp * dv), lambda h, i: (i, h)),
                   pl.BlockSpec((tq, hp * LANE), lambda h, i: (i, h))],
        out_shape=[jax.ShapeDtypeStruct((S, n_heads * dv), F32), jax.ShapeDtypeStruct((S, n_heads * LANE), F32)],
        compiler_params=_params(("arbitrary", "arbitrary")),
    )(q, k, v, *after)


def _attn_bwd(name, q, k, v, do, o, lse, n_heads, group, dqk, dv, scale, tq, after=()):
    S = q.shape[0]
    n_kv = n_heads // group
    hp, tq = (group, tq // 2) if group > HEADS_PER_STEP else (HEADS_PER_STEP, tq)
    kvp = max(1, hp // group)
    chains = 4
    tr = tq * hp // chains

    def body(q_ref, k_ref, v_ref, do_ref, o_ref, lse_ref, *rest):
        dq_ref, dk_ref, dv_ref = rest[len(after):]
        h, i = pl.program_id(0), pl.program_id(1)

        @pl.when(((h * hp) % group == 0) & (i == 0))
        def _():
            dk_ref[...] = jnp.zeros(dk_ref.shape, F32)
            dv_ref[...] = jnp.zeros(dv_ref.shape, F32)

        for j in range(hp):
            jk = j * kvp // hp
            kb, vb = k_ref[:, jk * dqk:(jk + 1) * dqk], v_ref[:, jk * dv:(jk + 1) * dv]
            for r0 in range(0, tq, tr):
                rows = slice(r0, r0 + tr)
                qb, dob = q_ref[rows, j * dqk:(j + 1) * dqk], do_ref[rows, j * dv:(j + 1) * dv]
                s = lax.dot_general(qb, kb, (((1,), (1,)), ((), ())), preferred_element_type=F32)
                p = jnp.exp(s * scale - jnp.tile(lse_ref[rows, j * LANE:(j + 1) * LANE], (1, S // LANE)))
                dp = lax.dot_general(dob, vb, (((1,), (1,)), ((), ())), preferred_element_type=F32)
                delta = jnp.sum(dob.astype(F32) * o_ref[rows, j * dv:(j + 1) * dv], axis=-1, keepdims=True)
                ds = (p * ((dp - delta) * scale)).astype(BF16)
                dv_ref[:, jk * dv:(jk + 1) * dv] += lax.dot_general(p.astype(BF16), dob, (((0,), (0,)), ((), ())),
                                                                    preferred_element_type=F32)
                dq_ref[rows, j * dqk:(j + 1) * dqk] = jnp.dot(ds, kb, preferred_element_type=F32)
                dk_ref[:, jk * dqk:(jk + 1) * dqk] += lax.dot_general(ds, qb, (((0,), (0,)), ((), ())),
                                                                      preferred_element_type=F32)

    kv_block = lambda h, i: (0, (h * hp // group) // kvp)
    return pl.pallas_call(
        body, name=name, grid=(n_heads // hp, S // tq),
        in_specs=[pl.BlockSpec((tq, hp * dqk), lambda h, i: (i, h)),
                  pl.BlockSpec((S, kvp * dqk), kv_block),
                  pl.BlockSpec((S, kvp * dv), kv_block),
                  pl.BlockSpec((tq, hp * dv), lambda h, i: (i, h)),
                  pl.BlockSpec((tq, hp * dv), lambda h, i: (i, h)),
                  pl.BlockSpec((tq, hp * LANE), lambda h, i: (i, h))] + [pl.BlockSpec(memory_space=pl.ANY) for _ in after],
        out_specs=[pl.BlockSpec((tq, hp * dqk), lambda h, i: (i, h)),
                   pl.BlockSpec((S, kvp * dqk), kv_block),
                   pl.BlockSpec((S, kvp * dv), kv_block)],
        out_shape=[jax.ShapeDtypeStruct((S, n_heads * dqk), F32),
                   jax.ShapeDtypeStruct((S, n_kv * dqk), F32),
                   jax.ShapeDtypeStruct((S, n_kv * dv), F32)],
        compiler_params=_params(("arbitrary", "arbitrary")),
    )(q, k, v, do, o, lse, *after)


def _adamw_step(w, gg, m, v):
    nm = ADAM_B1 * m + (1.0 - ADAM_B1) * gg
    nv = ADAM_B2 * v + (1.0 - ADAM_B2) * (gg * gg)
    c1, c2 = 1.0 - ADAM_B1 ** ADAM_STEP, 1.0 - ADAM_B2 ** ADAM_STEP
    return -ADAM_LR * ((nm / c1) / (jnp.sqrt(nv / c2) + ADAM_EPS) + ADAM_WD * w), nm, nv


def _adamw(name, w, g, m, v, after=(), rewrite_g=False):
    R, C = w.shape
    tile = _pick(R, max(8, (1 << 19) // C // 8 * 8), 8)
    n_out = 4 if rewrite_g else 3

    def body(w_ref, g_ref, m_ref, v_ref, *rest):
        d_ref, nm_ref, nv_ref = rest[len(after):len(after) + 3]
        gg = g_ref[...]
        d_ref[...], nm_ref[...], nv_ref[...] = _adamw_step(w_ref[...], gg, m_ref[...], v_ref[...])
        if rewrite_g:
            rest[-1][...] = gg

    spec = pl.BlockSpec((tile, C), lambda i: (i, 0))
    return pl.pallas_call(
        body, name=name, grid=(R // tile,), in_specs=[spec] * 4 + [pl.BlockSpec(memory_space=pl.ANY) for _ in after],
        out_specs=[spec] * n_out, out_shape=[jax.ShapeDtypeStruct((R, C), F32)] * n_out,
        compiler_params=_params(("parallel",)),
    )(w, g, m, v, *after)


def _adamw_outer(name, w, left, right, m, v, after=()):
    R, C = w.shape
    Kc = left.shape[0]
    tile = _pick(R, max(LANE, (1 << 19) // C // LANE * LANE), LANE)

    def body(w_ref, l_ref, r_ref, m_ref, v_ref, *rest):
        d_ref, nm_ref, nv_ref, g_ref = rest[len(after):]
        gg = lax.dot_general(l_ref[...], r_ref[...], (((0,), (0,)), ((), ())), preferred_element_type=F32)
        d_ref[...], nm_ref[...], nv_ref[...] = _adamw_step(w_ref[...], gg, m_ref[...], v_ref[...])
        g_ref[...] = gg

    spec = pl.BlockSpec((tile, C), lambda i: (i, 0))
    return pl.pallas_call(
        body, name=name, grid=(R // tile,),
        in_specs=[spec, pl.BlockSpec((Kc, tile), lambda i: (0, i)), pl.BlockSpec((Kc, C), lambda i: (0, 0)), spec, spec]
        + [pl.BlockSpec(memory_space=pl.ANY) for _ in after],
        out_specs=[spec] * 4, out_shape=[jax.ShapeDtypeStruct((R, C), F32)] * 4,
        compiler_params=_params(("parallel",)),
    )(w, left, right, m, v, *after)


def _flip(me, rel):
    return tuple(1 - p if r else p for p, r in zip(me, rel))


def _push(name, ins, out_shapes, transfers, aliases=None):
    ni, no, nt = len(ins), len(out_shapes), len(transfers)

    def body(*refs):
        in_refs, out_refs = refs[:ni], refs[ni:ni + no]
        send_sems, recv_sems = refs[ni + no], refs[ni + no + 1]
        me = (lax.axis_index("x"), lax.axis_index("y"), lax.axis_index("c"))
        copies = []
        for t, (si, sfn, oi, dfn, rel) in enumerate(transfers):
            src_ref = out_refs[si[1]] if isinstance(si, tuple) else in_refs[si]
            src = src_ref.at[sfn(*me)]
            dst = out_refs[oi].at[dfn(*me)]
            if rel is None:
                cp = pltpu.make_async_copy(src, dst, send_sems.at[t])
            else:
                cp = pltpu.make_async_remote_copy(src_ref=src, dst_ref=dst, send_sem=send_sems.at[t],
                                                  recv_sem=recv_sems.at[t], device_id=_flip(me, rel),
                                                  device_id_type=MESH)
            cp.start()
            copies.append(cp)
        for cp in copies:
            cp.wait()

    any_spec = pl.BlockSpec(memory_space=pl.ANY)
    return pl.pallas_call(
        body, name=name, in_specs=[any_spec] * ni, out_specs=[any_spec] * no, out_shape=out_shapes,
        scratch_shapes=[pltpu.SemaphoreType.DMA((nt,)), pltpu.SemaphoreType.DMA((nt,))],
        input_output_aliases=aliases or {},
        compiler_params=pltpu.CompilerParams(has_side_effects=True),
    )(*ins)


_HBM = pl.BlockSpec(memory_space=pltpu.HBM)
_SEM = pl.BlockSpec(memory_space=pltpu.SEMAPHORE)
_DATAFLOW = pltpu.SideEffectType.DATAFLOW_SIDE_EFFECTING


def _split_copies(refs, transfers, send_sems, recv_sems):
    me = (lax.axis_index("x"), lax.axis_index("y"), lax.axis_index("c"))
    return [pltpu.make_async_remote_copy(src_ref=refs[sb].at[sfn(*me)], dst_ref=refs[db].at[dfn(*me)],
                                         send_sem=send_sems.at[t], recv_sem=recv_sems.at[t],
                                         device_id=_flip(me, rel), device_id_type=MESH)
            for t, (sb, sfn, db, dfn, rel) in enumerate(transfers)]


def _push_start(name, bufs, transfers, after=()):
    nb, na, nt = len(bufs), len(after), len(transfers)

    def body(*refs):
        send_sems, recv_sems, token = refs[nb + na], refs[nb + na + 1], refs[-1]
        for cp in _split_copies(refs[nb + na + 2:2 * nb + na + 2], transfers, send_sems, recv_sems):
            cp.start()
        token[...] = jnp.zeros(token.shape, token.dtype)

    res = pl.pallas_call(
        body, name=name,
        out_shape=(pltpu.SemaphoreType.DMA((nt,)), pltpu.SemaphoreType.DMA((nt,)),
                   *[pltpu.HBM(b.shape, b.dtype) for b in bufs], jax.ShapeDtypeStruct((8, LANE), F32)),
        in_specs=[_HBM] * nb + [pl.BlockSpec(memory_space=pl.ANY)] * na,
        out_specs=(_SEM, _SEM, *[_HBM] * nb, pl.BlockSpec(memory_space=pltpu.VMEM)),
        input_output_aliases={i: i + 2 for i in range(nb)},
        compiler_params=pltpu.CompilerParams(has_side_effects=_DATAFLOW),
    )(*[pltpu.with_memory_space_constraint(b, pltpu.HBM) for b in bufs], *after)
    return res[0], res[1], list(res[2:2 + nb]), res[-1]


def _push_wait(name, send_sems, recv_sems, bufs, transfers, after):
    nb = len(bufs)

    def body(*refs):
        for cp in _split_copies(refs[:nb], transfers, refs[nb], refs[nb + 1]):
            cp.wait_send()
            cp.wait_recv()

    res = pl.pallas_call(
        body, name=name, out_shape=[pltpu.HBM(b.shape, b.dtype) for b in bufs],
        in_specs=[_HBM] * nb + [_SEM, _SEM, pl.BlockSpec(memory_space=pl.ANY)], out_specs=[_HBM] * nb,
        input_output_aliases={i: i for i in range(nb)},
        compiler_params=pltpu.CompilerParams(has_side_effects=_DATAFLOW),
    )(*bufs, send_sems, recv_sems, after)
    return list(res)


ICI_RELS = ((1, 0, 0), (0, 1, 0), (1, 1, 0))
SIBLING = (0, 0, 1)


def _chip(x, y):
    return 2 * x + y


def _cast_into_slot(name, where, w, after=()):
    R, C = w.shape
    tile = _pick(R, 512, BF16_ROWS)

    def body(where_ref, w_ref, *rest):
        rest[-1][...] = w_ref[...].astype(BF16)

    return pl.pallas_call(
        body, name=name,
        grid_spec=pltpu.PrefetchScalarGridSpec(
            num_scalar_prefetch=1, grid=(R // tile,),
            in_specs=[pl.BlockSpec((tile, C), lambda i, wh: (i, 0))] + [pl.BlockSpec(memory_space=pl.ANY) for _ in after],
            out_specs=pl.BlockSpec((None, tile, C), lambda i, wh: (wh[1], i, 0))),
        out_shape=jax.ShapeDtypeStruct((N_CHIPS, R, C), BF16),
        compiler_params=_params(("parallel",)),
    )(where, w, *after)


BF16_ROWS = 16


def _rows_split(shape2):
    return (shape2[0] // 2) % BF16_ROWS == 0


def _half_shape(shape2):
    R, C = shape2
    return (R // 2, C) if _rows_split(shape2) else (R, C // 2)


def _half(shape2, c):
    R, C = shape2
    if _rows_split(shape2):
        return (pl.ds(c * (R // 2), R // 2), slice(None))
    return (slice(None), pl.ds(pl.multiple_of(c * (C // 2), LANE), C // 2))


def _gather_casts(tag, where, shards, after=()):
    return [_cast_into_slot(f"cast_w_{tag}{w}", where, s, after) for w, s in enumerate(shards)]


def _gather_start(tag, own, after):
    transfers = []
    for w in range(len(own)):
        def idx(x, y, c, shape2=own[w].shape[1:]):
            return (_chip(x, y), *_half(shape2, c))
        transfers += [(w, idx, w, idx, rel) for rel in ICI_RELS]
    send_sems, recv_sems, bufs, token = _push_start(f"gather_{tag}_start", own, transfers, after)
    return (tag, send_sems, recv_sems, bufs, transfers), token


def _forward_transfers(part):
    second = []
    for w, p in enumerate(part):
        for rel in ICI_RELS:
            def idx(x, y, c, shape2=p.shape[1:], rel=rel):
                return (_chip(x ^ rel[0], y ^ rel[1]), *_half(shape2, c))
            second.append((w, idx, w, idx, SIBLING))
    return second


def _gather_finish(handle, after):
    tag, send_sems, recv_sems, bufs, transfers = handle
    part = _push_wait(f"gather_{tag}_wait", send_sems, recv_sems, bufs, transfers, after)
    outs = [jax.ShapeDtypeStruct(p.shape, p.dtype) for p in part]
    second = [(("out", sb), sfn, db, dfn, rel) for sb, sfn, db, dfn, rel in _forward_transfers(part)]
    return _push(f"gather_{tag}_d2d", list(part), outs, second, aliases={w: w for w in range(len(part))})


def _gather_mid(handle, after):
    tag, send_sems, recv_sems, bufs, transfers = handle
    part = _push_wait(f"gather_{tag}_wait", send_sems, recv_sems, bufs, transfers, after)
    second = _forward_transfers(part)
    s2, r2, bufs2, token = _push_start(f"gather_{tag}_d2d_start", part, second)
    return (tag, s2, r2, bufs2, second), token


def _gather_end(handle, after):
    tag, send_sems, recv_sems, bufs, transfers = handle
    return _push_wait(f"gather_{tag}_d2d_wait", send_sems, recv_sems, bufs, transfers, after)


def _reduce_begin(tag, grads, after=()):
    n = len(grads)
    land = [lax.empty((N_CHIPS, *_half_shape(g.shape[1:])), F32) for g in grads]
    t1 = []
    for w in range(n):
        t1.append((w, lambda x, y, c, shape2=grads[w].shape[1:]: (slice(None), *_half(shape2, 1 - c)), n + w,
                   lambda x, y, c: (slice(None), slice(None), slice(None)), SIBLING))
    send_sems, recv_sems, bufs, token = _push_start(f"reduce_{tag}_d2d_start", list(grads) + land, t1, after)
    return (tag, send_sems, recv_sems, bufs, t1), token


def _reduce_start(handle, where, after, after_start=()):
    tag, send_sems, recv_sems, bufs, t1 = handle
    n = len(bufs) // 2
    res = _push_wait(f"reduce_{tag}_d2d_wait", send_sems, recv_sems, bufs, t1, after)
    grads, from_sib = res[:n], res[n:]
    pair = [_pair_sum(f"pair_sum_{tag}{w}", where, grads[w], from_sib[w]) for w in range(n)]
    land = [lax.empty((3, *_half_shape(g.shape[1:])), BF16) for g in grads]
    t3 = []
    for w in range(n):
        for r, rel in enumerate(ICI_RELS):
            t3.append((w, lambda x, y, c, rel=rel: (_chip(x ^ rel[0], y ^ rel[1]), slice(None), slice(None)), n + w,
                       lambda x, y, c, r=r: (r, slice(None), slice(None)), rel))
    send_sems, recv_sems, bufs, token = _push_start(f"reduce_{tag}_start", pair + land, t3, after_start)
    return (tag, send_sems, recv_sems, bufs, t3, list(grads), list(from_sib)), token


def _reduce_finish(handle, where, after):
    tag, send_sems, recv_sems, bufs, t3, grads, from_sib = handle
    n = len(grads)
    from_chips = _push_wait(f"reduce_{tag}_wait", send_sems, recv_sems, bufs, t3, after)[n:]
    mine = [_chip_sum(f"chip_sum_{tag}{w}", where, grads[w], from_sib[w], from_chips[w]) for w in range(n)]
    t5 = []
    for w in range(n):
        def idx(x, y, c, shape2=grads[w].shape[1:]):
            return _half(shape2, c)
        t5.append((w, idx, w, idx, SIBLING))
    send_sems, recv_sems, bufs, token = _push_start(f"reduce_{tag}_swap_start", mine, t5)
    return (tag, send_sems, recv_sems, bufs, t5), token


def _reduce_end(handle, after):
    tag, send_sems, recv_sems, bufs, t5 = handle
    return _push_wait(f"reduce_{tag}_swap_wait", send_sems, recv_sems, bufs, t5, after)


def _pair_sum(name, where, g, from_sib):
    _, hr, hc = from_sib.shape
    tile = _pick(hr, max(16, (1 << 19) // hc // 16 * 16), 16)
    nb = hr // tile
    slot = lambda k, wh: (wh[1] + 1 + k) % N_CHIPS
    if _rows_split(g.shape[1:]):
        mine = lambda k, i, wh: (slot(k, wh), wh[0] * nb + i, 0)
    else:
        mine = lambda k, i, wh: (slot(k, wh), i, wh[0])

    def body(where_ref, g_ref, s_ref, o_ref):
        o_ref[...] = (g_ref[...] + s_ref[...]).astype(BF16)

    return pl.pallas_call(
        body, name=name,
        grid_spec=pltpu.PrefetchScalarGridSpec(
            num_scalar_prefetch=1, grid=(N_CHIPS - 1, nb),
            in_specs=[pl.BlockSpec((None, tile, hc), mine),
                      pl.BlockSpec((None, tile, hc), lambda k, i, wh: (slot(k, wh), i, 0))],
            out_specs=pl.BlockSpec((None, tile, hc), lambda k, i, wh: (slot(k, wh), i, 0))),
        out_shape=jax.ShapeDtypeStruct((N_CHIPS, hr, hc), BF16),
        compiler_params=_params(("parallel", "parallel")),
    )(where, g, from_sib)


def _chip_sum(name, where, g, from_sib, from_chips):
    _, R, C = g.shape
    _, hr, hc = from_sib.shape
    tile = _pick(hr, max(16, (1 << 19) // hc // 16 * 16), 16)
    nb = hr // tile
    if _rows_split((R, C)):
        mine = lambda i, wh: (wh[0] * nb + i, 0)
    else:
        mine = lambda i, wh: (i, wh[0])

    def body(where_ref, g_ref, s_ref, r_ref, o_ref):
        acc = g_ref[...] + s_ref[...]
        for r in range(3):
            acc = acc + r_ref[r].astype(F32)
        o_ref[...] = acc

    return pl.pallas_call(
        body, name=name,
        grid_spec=pltpu.PrefetchScalarGridSpec(
            num_scalar_prefetch=1, grid=(nb,),
            in_specs=[pl.BlockSpec((None, tile, hc), lambda i, wh: (wh[1], *mine(i, wh))),
                      pl.BlockSpec((None, tile, hc), lambda i, wh: (wh[1], i, 0)),
                      pl.BlockSpec((3, tile, hc), lambda i, wh: (0, i, 0))],
            out_specs=pl.BlockSpec((tile, hc), mine)),
        out_shape=jax.ShapeDtypeStruct((R, C), F32),
        compiler_params=_params(("parallel",)),
    )(where, g, from_sib, from_chips)


def _allgather8(name, blk, after=()):
    m_per, n = blk.shape
    rels = [(0, 0, 1), (1, 0, 0), (0, 1, 0), (1, 1, 0), (1, 0, 1), (0, 1, 1), (1, 1, 1)]

    def body(x_ref, *rest):
        out_ref, send_sems, recv_sems = rest[len(after):]
        me = (lax.axis_index("x"), lax.axis_index("y"), lax.axis_index("c"))
        my_rows = out_ref.at[pl.ds((4 * me[0] + 2 * me[1] + me[2]) * m_per, m_per), :]
        out_ref[pl.ds((4 * me[0] + 2 * me[1] + me[2]) * m_per, m_per), :] = x_ref[...]
        copies = []
        for t, rel in enumerate(rels):
            cp = pltpu.make_async_remote_copy(src_ref=x_ref, dst_ref=my_rows, send_sem=send_sems.at[t],
                                              recv_sem=recv_sems.at[t], device_id=_flip(me, rel),
                                              device_id_type=MESH)
            cp.start()
            copies.append(cp)
        for cp in copies:
            cp.wait()

    return pl.pallas_call(
        body, name=name, out_shape=jax.ShapeDtypeStruct((N_DEV * m_per, n), blk.dtype),
        in_specs=[pl.BlockSpec(memory_space=pltpu.VMEM)] + [pl.BlockSpec(memory_space=pl.ANY) for _ in after],
        out_specs=pl.BlockSpec(memory_space=pltpu.VMEM),
        scratch_shapes=[pltpu.SemaphoreType.DMA((7,)), pltpu.SemaphoreType.DMA((7,))],
        compiler_params=pltpu.CompilerParams(has_side_effects=True, vmem_limit_bytes=VMEM_LIMIT),
    )(blk, *after)


def _rope_tables(seq_len, dim, reps):
    rows = seq_len // GRID_W
    t = np.arange(seq_len)
    row, col = (t // GRID_W).astype(np.float32), (t % GRID_W).astype(np.float32)
    half = dim // 2
    inv = jnp.asarray(ROPE_THETA, F32) ** (-jnp.arange(0, half, 2, dtype=F32) / half)
    ang_r = jnp.asarray(row)[:, None] * inv[None, :]
    ang_c = jnp.asarray(col)[:, None] * inv[None, :]
    ang = jnp.concatenate([ang_r, ang_r, ang_c, ang_c], axis=-1)
    cos, sin = jnp.cos(ang), jnp.sin(ang)
    low = (np.arange(dim) % (dim // 2)) < (dim // 4)
    s_lo = jnp.where(jnp.asarray(low)[None, :], -sin, 0.0)
    s_hi = jnp.where(jnp.asarray(low)[None, :], 0.0, sin)
    del rows
    return tuple(jnp.tile(a, (1, reps)) for a in (cos, s_lo, s_hi))


def _make_rope(shift):
    def up(v):
        return pltpu.roll(v, LANE - shift, 1)

    def down(v):
        return pltpu.roll(v, shift, 1)

    @jax.custom_vjp
    def rope(v, cos, s_lo, s_hi):
        return v * cos + up(v) * s_lo + down(v) * s_hi

    def fwd(v, cos, s_lo, s_hi):
        return rope(v, cos, s_lo, s_hi), (cos, s_lo, s_hi)

    def bwd(res, dy):
        cos, s_lo, s_hi = res
        return dy * cos + down(dy * s_lo) + up(dy * s_hi), jnp.zeros_like(cos), jnp.zeros_like(cos), jnp.zeros_like(cos)

    rope.defvjp(fwd, bwd)
    return rope


_rope_a = _make_rope(HEAD_DIM // 4)
_rope_b = _make_rope(QK_ROPE // 4)


def _prep(segs, tabs, g_q, g_k, g_ckv):
    qa, ka, va, qn, qp, ckv, kpe = segs
    ta, tb = tabs
    q_a = jnp.concatenate([_rope_a(_rn(s) * g_q, *ta) for s in qa], axis=1)
    k_a = jnp.concatenate([_rope_a(_rn(s) * g_k, *ta) for s in ka], axis=1)
    cat = []
    for h in range(HB):
        cat += [qn[h], _rope_b(qp[h], *tb)]
    q_cat = jnp.concatenate(cat, axis=1)
    return q_a, k_a, va, q_cat, _rn(ckv) * g_ckv, _rope_b(kpe, *tb)


def _split_proj(proj):
    qa = [proj[:, O_QA + h * LANE:O_QA + (h + 1) * LANE] for h in range(HA)]
    ka = [proj[:, O_KA + h * LANE:O_KA + (h + 1) * LANE] for h in range(HKV)]
    va = proj[:, O_VA:O_QN]
    qn = [proj[:, O_QN + h * LANE:O_QN + (h + 1) * LANE] for h in range(HB)]
    qp = [proj[:, O_QP + h * LANE:O_QP + (h + 1) * LANE] for h in range(HB)]
    return qa, ka, va, qn, qp, proj[:, O_CKV:O_KPE], proj[:, O_KPE:W_INP]


def _permute_w_in(nat):
    K = nat.shape[1]
    qb = nat[W_QA + 2 * W_KA:W_QA + 2 * W_KA + W_QB].reshape(HB, QK_B, K)
    qp = jnp.concatenate([qb[:, QK_NOPE:], jnp.zeros((HB, LANE - QK_ROPE, K), nat.dtype)], axis=1)
    tail = nat[W_QA + 2 * W_KA + W_QB:]
    return jnp.concatenate([nat[:O_QN], qb[:, :QK_NOPE].reshape(HB * QK_NOPE, K), qp.reshape(HB * LANE, K),
                            tail, jnp.zeros((LANE - QK_ROPE, K), nat.dtype)], axis=0)


def _unpermute_pieces():
    pieces = [(0, 0, O_QN)]
    for h in range(HB):
        pieces.append((O_QN + h * QK_NOPE, O_QN + h * QK_B, QK_NOPE))
        pieces.append((O_QP + h * LANE, O_QN + h * QK_B + QK_NOPE, QK_ROPE))
    pieces.append((O_CKV, O_QN + W_QB, KV_RANK))
    pieces.append((O_KPE, O_QN + W_QB + KV_RANK, QK_ROPE))
    return pieces


def kernel(x, c, w_ada, b_ada, g_pre_attn, w_in, g_q_a, g_k_a, g_ckv, w_kv_b, g_out_a, g_out_b, w_out, g_post_attn, g_pre_mlp, w_mlp_in, w_mlp_out, g_post_mlp, loss_target, m_w_ada, m_b_ada, m_g_pre_attn, m_w_in, m_g_q_a, m_g_k_a, m_g_ckv, m_w_kv_b, m_g_out_a, m_g_out_b, m_w_out, m_g_post_attn, m_g_pre_mlp, m_w_mlp_in, m_w_mlp_out, m_g_post_mlp, v_w_ada, v_b_ada, v_g_pre_attn, v_w_in, v_g_q_a, v_g_k_a, v_g_ckv, v_w_kv_b, v_g_out_a, v_g_out_b, v_w_out, v_g_post_attn, v_g_pre_mlp, v_w_mlp_in, v_w_mlp_out, v_g_post_mlp):
    S, D = x.shape[1], x.shape[2]
    x2d, tgt = x[0], loss_target[0]
    ix, iy, ic = lax.axis_index("x"), lax.axis_index("y"), lax.axis_index("c")
    j_me = _chip(ix, iy)
    e_me = 4 * ix + 2 * iy + ic
    T = _pick(S, 512, 8)
    TB = _pick(S, 256, 8)

    where = jnp.stack([ic, j_me]).astype(jnp.int32)
    DFF = w_mlp_out.shape[1] * N_CHIPS
    tabs_a = _rope_tables(S, HEAD_DIM, 1)
    tabs_b = _rope_tables(S, QK_ROPE, LANE // QK_ROPE)

    c_all = _allgather8("gather_c", c.reshape(8, D // 8)).reshape(N_DEV, D)
    w_in_t = jnp.swapaxes(w_in[0], 0, 1)
    h_qkv, tok_a = _gather_start("qkv", _gather_casts("qkv", where, [w_in_t, w_kv_b[0]]), (c_all,))
    later = [_gather_casts(tag, where, [w[0]], (tok_a,))
             for tag, w in (("out", w_out), ("up", w_mlp_in), ("down", w_mlp_out))]

    def silu_fn(cc):
        return (cc * (1.0 / (1.0 + jnp.exp(-cc))),), ()

    (c_act,), _ = _rowwise("silu_c", silu_fn, [c_all], [], [(D, BF16)], [], N_DEV)
    ca_pad = jnp.concatenate([c_act, jnp.zeros_like(c_act)], axis=0)
    n_mod = w_ada.shape[2]
    b_cols = lax.dynamic_slice(b_ada, (0, j_me * n_mod), (1, n_mod))
    (mod_part,) = _mm_nn("mod_mm", ca_pad, w_ada, [F32], epilogue=lambda acc, b: (acc + b,),
                         extras=(jnp.broadcast_to(b_cols, (16, n_mod)),))
    mod_all = _allgather8("gather_mod", mod_part[:8] + tok_a[0, 0],
                          after=[c_[0] for c_ in later] + [*tabs_a, *tabs_b]).reshape(N_DEV, 8, n_mod)
    mod = jnp.concatenate([lax.dynamic_slice(mod_all, (2 * j, e_me, 0), (1, 1, n_mod))[0] for j in range(N_CHIPS)],
                          axis=1)

    m_qkv, tok_q = _gather_mid(h_qkv, mod_all)
    h_out, tok_b = _gather_start("out", later[0], (tok_q,))
    h_up, tok_c = _gather_start("up", later[1], (tok_b,))
    h_down, tok_d = _gather_start("down", later[2], (tok_c,))
    mod = mod + tok_d[0, 0]
    sh_a, sc_a, gt_a, sh_m, sc_m, gt_m = [mod[:, i * D:(i + 1) * D] for i in range(6)]

    def pre_attn_fn(xt, g, sc, sh):
        return (_rn(xt) * g * (1.0 + sc) + sh,), ()

    (h_b,), _ = _rowwise("pre_attn", pre_attn_fn, [x2d], [g_pre_attn, sc_a, sh_a], [(D, BF16)], [], T)
    g_in, g_kvb = _gather_end(m_qkv, h_b)
    wpt = _permute_w_in(g_in.reshape(W_IN, D))[None]
    (proj,) = _mm_nt("proj_mm", h_b, wpt, [F32])

    def prep_fn(pt, ca_, sla, sha, cb_, slb, shb, gq, gk, gc):
        return _prep(_split_proj(pt), ((ca_, sla, sha), (cb_, slb, shb)), gq, gk, gc), ()

    (qa_b, ka_b, va_b, qcat_b, ckvn_b, kpe_b), _ = _rowwise(
        "prep", prep_fn, [proj, *tabs_a, *tabs_b], [g_q_a, g_k_a, g_ckv],
        [(W_QA, BF16), (W_KA, BF16), (W_KA, BF16), (2 * HB * LANE, BF16), (KV_RANK, BF16), (LANE, BF16)], [], T)
    kcat_b, vb_b = _kv_mm("kv_mm", ckvn_b, g_kvb, kpe_b)

    TQ = _pick(S, 512, 16)
    sc_a_ = 1.0 / math.sqrt(HEAD_DIM)
    sc_b_ = 1.0 / math.sqrt(QK_B)
    o_a, lse_a = _attn_fwd("attn_a_fwd", qa_b, ka_b, va_b, HA, HA // HKV, HEAD_DIM, HEAD_DIM, sc_a_, TQ)
    m_out, tok_m = _gather_mid(h_out, o_a)
    o_b, lse_b = _attn_fwd("attn_b_fwd", qcat_b, kcat_b, vb_b, HB, 1, 2 * LANE, V_DIM, sc_b_, TQ, after=(tok_m,))
    m_up, tok_m = _gather_mid(h_up, o_b)

    def mix_fn(oa, ob, ga, gb):
        return jnp.concatenate([_rn(oa) * ga, _rn(ob) * gb], axis=1)

    (g_wout,) = _gather_end(m_out, o_b)
    w_out_f = g_wout.reshape(1, N_CHIPS * g_wout.shape[1], D)
    on_b, o2 = _rows_mm("out_mm", mix_fn, [o_a, o_b], [g_out_a, g_out_b], w_out_f, after=(tok_m,))

    def mid(xt, o2t, gta, gpa, gpm, scm, shm):
        x1 = xt + gta * (_rn(o2t) * gpa)
        return x1, _rn(x1) * gpm * (1.0 + scm) + shm

    def mid_fn(*a):
        return mid(*a), ()

    mid_vecs = [gt_a, g_post_attn, g_pre_mlp, sc_m, sh_m]
    (x1, h2_b), _ = _rowwise("mid", mid_fn, [x2d, o2], mid_vecs, [(D, F32), (D, BF16)], [], T)
    (g_mlp_in,) = _gather_end(m_up, h2_b)
    relu_b, a_b = _mm_nn("mlp_in_mm", h2_b, g_mlp_in, [BF16, BF16],
                         epilogue=lambda acc: (jnp.maximum(acc, 0.0), jnp.square(jnp.maximum(acc, 0.0))))
    (g_mlp_out,) = _gather_finish(h_down, a_b)
    w_mlp_out_f = g_mlp_out.reshape(1, DFF, D)
    (y,) = _mm_nn("mlp_out_mm", a_b, w_mlp_out_f, [F32])

    def last_fn(yt, x1t, tt, gtm, gpo):
        n, r = _rn_parts(yt)
        err = x1t + gtm * (n * gpo) - tt
        dx2 = err * (1.0 / D)
        loss = 0.5 * jnp.sum(jnp.mean(err * err, axis=-1))
        s = _rows(dx2 * n)
        return (_rn_bwd(n, r, dx2 * (gtm * gpo)), dx2), (jnp.full((1, LANE), loss, F32), gpo * s, gtm * s)

    (dy_b, dx2), (loss_v, d_gt_m, d_g_post_mlp) = _rowwise(
        "loss_bwd", last_fn, [y, x1, tgt], [gt_m, g_post_mlp], [(D, BF16), (D, F32)], [LANE, D, D], TB)

    gw_mlp_out = _mm_tn("gw_mlp_out_mm", a_b, dy_b, 1).reshape(N_CHIPS, DFF // N_CHIPS, D)
    b_down, tok = _reduce_begin("down", [gw_mlp_out])
    (du_b,) = _mm_nt("d_mlp_out_mm", dy_b, w_mlp_out_f, [BF16],
                     epilogue=lambda acc, rt: (acc * (2.0 * rt.astype(F32)),), extras=(relu_b,), after=(tok,))
    r_down, tok = _reduce_start(b_down, where, du_b)
    gw_mlp_in = _mm_tn("gw_mlp_in_mm", h2_b, du_b, N_CHIPS, after=(tok,))
    b_up, tok = _reduce_begin("up", [gw_mlp_in])
    (dh2,) = _mm_nt("d_mlp_in_mm", du_b, g_mlp_in, [F32], after=(tok,))
    r_up, tok_up = _reduce_start(b_up, where, dh2)

    def mid_bwd_fn(xt, o2t, dh2t, dx2t, gta, gpa, gpm, scm, shm):
        n2, r2 = _rn_parts(o2t)
        n1, r1 = _rn_parts(xt + gta * (n2 * gpa))
        dx1 = dx2t + _rn_bwd(n1, r1, dh2t * (gpm * (1.0 + scm)))
        s1, s2 = _rows(dh2t * n1), _rows(dx1 * n2)
        do2 = _rn_bwd(n2, r2, dx1 * (gta * gpa))
        return (dx1, do2), (gpa * s2, gta * s2, (1.0 + scm) * s1, gpm * s1, _rows(dh2t))

    (dx1, do2_b), (d_gt_a, d_g_post_attn, d_g_pre_mlp, d_sc_m, d_sh_m) = _rowwise(
        "mid_bwd", mid_bwd_fn, [x2d, o2, dh2, dx2], mid_vecs, [(D, F32), (D, BF16)], [D] * 5, TB, after=(tok_up,))

    (d_on,) = _mm_nt("d_out_mm", do2_b, w_out_f, [F32])
    gw_out = _mm_tn("gw_out_mm", on_b, do2_b, 1).reshape(N_CHIPS, w_out.shape[1], D)

    def mix_bwd_fn(oa, ob, dont, ga, gb):
        na, ra = _rn_parts(oa)
        nb, rb = _rn_parts(ob)
        da, db = dont[:, :W_QA], dont[:, W_QA:]
        return (_rn_bwd(na, ra, da * ga), _rn_bwd(nb, rb, db * gb)), (_rows(da * na), _rows(db * nb))

    (doa_b, dob_b), (d_g_out_a, d_g_out_b) = _rowwise(
        "mix_bwd", mix_bwd_fn, [o_a, o_b, d_on], [g_out_a, g_out_b], [(W_QA, BF16), (HB * V_DIM, BF16)],
        [W_QA, HB * V_DIM], TB)

    dqa, dka, dva = _attn_bwd("attn_a_bwd", qa_b, ka_b, va_b, doa_b, o_a, lse_a, HA, HA // HKV, HEAD_DIM, HEAD_DIM,
                              sc_a_, TQ)
    dqcat, dkcat, dvb = _attn_bwd("attn_b_bwd", qcat_b, kcat_b, vb_b, dob_b, o_b, lse_b, HB, 1, 2 * LANE, V_DIM,
                                  sc_b_, TQ)

    d_ckvn, dkv_b, dkpe = _kv_mm_bwd("d_kv_mm", dkcat, dvb, g_kvb)
    gw_kvb = _mm_tn("gw_kv_mm", ckvn_b, dkv_b, N_CHIPS)
    b_kvb, tok_kvb = _reduce_begin("okv", [gw_out, gw_kvb])

    def prep_bwd_fn(pt, dqat, dkat, dvat, dqct, dckt, dkpt, ca_, sla, sha, cb_, slb, shb, gq, gk, gc):
        tabs = ((ca_, sla, sha), (cb_, slb, shb))
        _, vjp = jax.vjp(lambda sg, a1, a2, a3: _prep(sg, tabs, a1, a2, a3), _split_proj(pt), gq, gk, gc)
        (dqa_s, dka_s, dva_s, dqn_s, dqp_s, dck_s, dkp_s), dgq, dgk, dgc = vjp((dqat, dkat, dvat, dqct, dckt, dkpt))
        dproj = jnp.concatenate([*dqa_s, *dka_s, dva_s, *dqn_s, *dqp_s, dck_s, dkp_s], axis=1)
        return (dproj,), (dgq, dgk, dgc)

    (dproj_b,), (d_g_q_a, d_g_k_a, d_g_ckv) = _rowwise(
        "prep_bwd", prep_bwd_fn, [proj, dqa, dka, dva, dqcat, d_ckvn, dkpe, *tabs_a, *tabs_b],
        [g_q_a, g_k_a, g_ckv], [(W_INP, BF16)], [HEAD_DIM, HEAD_DIM, KV_RANK], TB, after=(tok_kvb,))

    r_kvb, tok = _reduce_start(b_kvb, where, dproj_b)
    (dh,) = _mm_nn("d_proj_mm", dproj_b, wpt, [F32], after=(tok,))
    gw_in = _mm_tn_rows("gw_in_mm", dproj_b, h_b, W_IN, _unpermute_pieces()).reshape(N_CHIPS, W_IN // N_CHIPS, D)
    b_in, tok = _reduce_begin("in", [gw_in])

    def pre_attn_bwd_fn(xt, dht, dx1t, g, sc, sh):
        n, r = _rn_parts(xt)
        s = _rows(dht * n)
        return (dx1t + _rn_bwd(n, r, dht * (g * (1.0 + sc))),), ((1.0 + sc) * s, g * s, _rows(dht))

    (grad_x,), (d_g_pre_attn, d_sc_a, d_sh_a) = _rowwise(
        "pre_attn_bwd", pre_attn_bwd_fn, [x2d, dh, dx1], [g_pre_attn, sc_a, sh_a], [(D, F32)], [D] * 3, TB,
        after=(tok,))

    small_names = ["b_ada", "g_pre_attn", "g_q_a", "g_k_a", "g_ckv", "g_out_a", "g_out_b", "g_post_attn",
                   "g_pre_mlp", "g_post_mlp"]
    small_local = jnp.concatenate([d_sh_a, d_sc_a, d_gt_a, d_sh_m, d_sc_m, d_gt_m, d_g_pre_attn, d_g_q_a, d_g_k_a,
                                   d_g_ckv, d_g_out_a, d_g_out_b, d_g_post_attn, d_g_pre_mlp, d_g_post_mlp,
                                   loss_v[:, :8]], axis=1)
    n_sent = small_local.shape[1]
    n_small = n_sent - 8
    small_all = _allgather8("gather_small", small_local.reshape(8, n_sent // 8)).reshape(N_DEV, n_sent)

    def sum8_fn(t):
        return (), (jnp.sum(t, axis=0, keepdims=True),)

    r_in, tok_in = _reduce_start(b_in, where, grad_x, (small_all,))
    _, (sent_sum,) = _rowwise("sum_small", sum8_fn, [small_all], [], [], [n_sent], N_DEV, after=(tok_in,))
    small_sum, loss = sent_sum[:, :n_small], sent_sum[0, n_small]
    dmod_cols = lax.dynamic_slice(small_all, (0, j_me * n_mod), (N_DEV, n_mod))
    dmod_pad = jnp.concatenate([dmod_cols, jnp.zeros_like(dmod_cols)], axis=0).astype(BF16)

    weights = dict(w_ada=w_ada, b_ada=b_ada, g_pre_attn=g_pre_attn, w_in=w_in, g_q_a=g_q_a, g_k_a=g_k_a, g_ckv=g_ckv,
                   w_kv_b=w_kv_b, g_out_a=g_out_a, g_out_b=g_out_b, w_out=w_out, g_post_attn=g_post_attn,
                   g_pre_mlp=g_pre_mlp, w_mlp_in=w_mlp_in, w_mlp_out=w_mlp_out, g_post_mlp=g_post_mlp)
    ms = dict(w_ada=m_w_ada, b_ada=m_b_ada, g_pre_attn=m_g_pre_attn, w_in=m_w_in, g_q_a=m_g_q_a, g_k_a=m_g_k_a,
              g_ckv=m_g_ckv, w_kv_b=m_w_kv_b, g_out_a=m_g_out_a, g_out_b=m_g_out_b, w_out=m_w_out,
              g_post_attn=m_g_post_attn, g_pre_mlp=m_g_pre_mlp, w_mlp_in=m_w_mlp_in, w_mlp_out=m_w_mlp_out,
              g_post_mlp=m_g_post_mlp)
    vs = dict(w_ada=v_w_ada, b_ada=v_b_ada, g_pre_attn=v_g_pre_attn, w_in=v_w_in, g_q_a=v_g_q_a, g_k_a=v_g_k_a,
              g_ckv=v_g_ckv, w_kv_b=v_w_kv_b, g_out_a=v_g_out_a, g_out_b=v_g_out_b, w_out=v_w_out,
              g_post_attn=v_g_post_attn, g_pre_mlp=v_g_pre_mlp, w_mlp_in=v_w_mlp_in, w_mlp_out=v_w_mlp_out,
              g_post_mlp=v_g_post_mlp)
    order = list(weights)
    grads, deltas, new_m, new_v = {}, {}, {}, {}

    def update(name, g, after=()):
        view = (lambda t: jnp.swapaxes(t, 0, 1)) if name == "w_in" else (lambda t: t)
        d_, m_, v_, g_ = _adamw("adamw_" + name, view(weights[name][0]), g, view(ms[name][0]), view(vs[name][0]),
                                after, rewrite_g=True)
        grads[name], deltas[name], new_m[name], new_v[name] = (view(t)[None] for t in (g_, d_, m_, v_))
        return d_

    chain = ((("w_mlp_out",), r_down), (("w_mlp_in",), r_up), (("w_out", "w_kv_b"), r_kvb), (("w_in",), r_in))
    swap, tok = _reduce_finish(chain[0][1], where, small_sum)
    last, m_, v_, g_ = _adamw_outer("adamw_w_ada", w_ada[0], ca_pad, dmod_pad, m_w_ada[0], v_w_ada[0], (tok,))
    grads["w_ada"], deltas["w_ada"], new_m["w_ada"], new_v["w_ada"] = g_[None], last[None], m_[None], v_[None]
    for k, (names, _) in enumerate(chain):
        gs = _reduce_end(swap, last)
        if k + 1 < len(chain):
            swap, tok = _reduce_finish(chain[k + 1][1], where, gs[0])
        for name, g in zip(names, gs):
            last = update(name, g, (tok,))

    cat = lambda d: jnp.concatenate([d[n] for n in small_names], axis=1)
    d_, m_, v_ = _adamw("adamw_small", cat(weights), small_sum, cat(ms), cat(vs))
    off = 0
    for n in small_names:
        w_ = weights[n].shape[1]
        grads[n], deltas[n], new_m[n], new_v[n] = (a[:, off:off + w_] for a in (small_sum, d_, m_, v_))
        off += w_

    return (loss, grad_x[None], *[grads[n] for n in order], *[deltas[n] for n in order],
            *[new_m[n] for n in order], *[new_v[n] for n in order])
```

```python
import functools
import math

import numpy as np
import jax
import jax.numpy as jnp
from jax import lax
from jax.experimental import pallas as pl
from jax.experimental.pallas import tpu as pltpu

F32 = jnp.float32
BF16 = jnp.bfloat16
MESH = pl.DeviceIdType.MESH

EPS = 1e-6
GRID_W = 64
ROPE_THETA = 10000.0
HEAD_DIM = 128
HA = 8
HKV = 2
HB = 8
QK_NOPE = 128
QK_ROPE = 64
V_DIM = 128
KV_RANK = 512
QK_B = QK_NOPE + QK_ROPE
W_QA = HA * HEAD_DIM
W_KA = HKV * HEAD_DIM
W_QB = HB * QK_B
W_IN = W_QA + 2 * W_KA + W_QB + KV_RANK + QK_ROPE
N_CHIPS = 4
N_DEV = 8

LANE = 128
O_QA = 0
O_KA = O_QA + W_QA
O_VA = O_KA + W_KA
O_QN = O_VA + W_KA
O_QP = O_QN + HB * QK_NOPE
O_CKV = O_QP + HB * LANE
O_KPE = O_CKV + KV_RANK
W_INP = O_KPE + LANE

ADAM_LR = 0.001
ADAM_B1 = 0.9
ADAM_B2 = 0.999
ADAM_EPS = 1e-08
ADAM_WD = 0.01
ADAM_STEP = 10

VMEM_LIMIT = 56 * 1024 * 1024
HEADS_PER_STEP = 2


def _params(sem):
    return pltpu.CompilerParams(dimension_semantics=sem, vmem_limit_bytes=VMEM_LIMIT)


def _pick(dim, target, unit):
    best = None
    t = unit
    while t <= min(dim, target):
        if dim % t == 0:
            best = t
        t += unit
    return dim if best is None else best


def _rn(x):
    return x * lax.rsqrt(jnp.mean(x * x, axis=-1, keepdims=True) + EPS)


def _rn_parts(x):
    r = lax.rsqrt(jnp.mean(x * x, axis=-1, keepdims=True) + EPS)
    return x * r, r


def _rn_bwd(n, r, dn):
    return r * (dn - n * jnp.mean(dn * n, axis=-1, keepdims=True))


def _rows(t):
    return jnp.sum(t, axis=0, keepdims=True)


def _rowwise(name, fn, rows, vecs, row_outs, vec_outs, tile, after=()):
    S = rows[0].shape[0]
    assert S % tile == 0
    n_r, n_v, n_ro, n_vo, n_a = len(rows), len(vecs), len(row_outs), len(vec_outs), len(after)

    def body(*refs):
        r_in = refs[:n_r]
        v_in = refs[n_r:n_r + n_v]
        r_out = refs[n_r + n_v + n_a:n_r + n_v + n_a + n_ro]
        v_out = refs[n_r + n_v + n_a + n_ro:]
        ro, vo = fn(*[r[...] for r in r_in], *[v[...] for v in v_in])
        for ref, val in zip(r_out, ro):
            ref[...] = val.astype(ref.dtype)
        if n_vo:
            @pl.when(pl.program_id(0) == 0)
            def _():
                for ref in v_out:
                    ref[...] = jnp.zeros(ref.shape, ref.dtype)
            for ref, val in zip(v_out, vo):
                ref[...] += val

    in_specs = [pl.BlockSpec((tile, a.shape[1]), lambda i: (i, 0)) for a in rows]
    in_specs += [pl.BlockSpec(a.shape, lambda i: (0, 0)) for a in vecs]
    in_specs += [pl.BlockSpec(memory_space=pl.ANY) for _ in after]
    out_specs = [pl.BlockSpec((tile, w), lambda i: (i, 0)) for w, _ in row_outs]
    out_specs += [pl.BlockSpec((1, w), lambda i: (0, 0)) for w in vec_outs]
    out_shape = [jax.ShapeDtypeStruct((S, w), dt) for w, dt in row_outs]
    out_shape += [jax.ShapeDtypeStruct((1, w), F32) for w in vec_outs]
    res = pl.pallas_call(
        body, name=name, grid=(S // tile,), in_specs=in_specs, out_specs=out_specs, out_shape=out_shape,
        compiler_params=_params(("arbitrary",)),
    )(*rows, *vecs, *after)
    return res[:n_ro], res[n_ro:]


CONTRACT_TILE = 4608
MM_TILE_BYTES = 44 * 1024 * 1024


def _rows_that_fit(M, tile_bytes):
    t = _pick(M, 1024, 16)
    while t % 32 == 0 and tile_bytes(t) > MM_TILE_BYTES:
        t //= 2
    return t


def _accumulate(acc, part, step, n_steps, finish):
    if n_steps == 1:
        finish(part)
        return

    @pl.when(step == 0)
    def _():
        acc[...] = part

    @pl.when((step > 0) & (step < n_steps - 1))
    def _():
        acc[...] += part

    @pl.when(step == n_steps - 1)
    def _():
        finish(acc[...] + part)


def _mm_nn(name, a, b3, out_dtypes, epilogue=None, extras=(), after=()):
    M, K = a.shape
    nB, K2, Nsh = b3.shape
    assert K == K2
    N = nB * Nsh
    tn, tk = _pick(Nsh, 1536, LANE), _pick(K, CONTRACT_TILE, LANE)
    out_bytes = sum(jnp.dtype(dt).itemsize for dt in out_dtypes) + sum(e.dtype.itemsize for e in extras)
    tm = _rows_that_fit(M, lambda t: 2 * (t * tk * a.dtype.itemsize + tk * tn * b3.dtype.itemsize)
                        + t * tn * (4 * (K > tk) + 2 * out_bytes))
    nps, nk, ne, no, na = Nsh // tn, K // tk, len(extras), len(out_dtypes), len(after)

    def body(a_ref, b_ref, *rest):
        e_refs, o_refs = rest[:ne], rest[ne + na:ne + na + no]

        def finish(total):
            vals = (total,) if epilogue is None else epilogue(total, *[e[...] for e in e_refs])
            for ref, val in zip(o_refs, vals):
                ref[...] = val.astype(ref.dtype)

        part = jnp.dot(a_ref[...].astype(BF16), b_ref[...].astype(BF16), preferred_element_type=F32)
        _accumulate(rest[-1], part, pl.program_id(2), nk, finish)

    in_specs = [pl.BlockSpec((tm, tk), lambda i, j, k: (i, k)),
                pl.BlockSpec((None, tk, tn), lambda i, j, k: (j // nps, k, j % nps))]
    in_specs += [pl.BlockSpec((tm, tn), lambda i, j, k: (i, j)) for _ in extras]
    in_specs += [pl.BlockSpec(memory_space=pl.ANY) for _ in after]
    return pl.pallas_call(
        body, name=name, grid=(M // tm, N // tn, nk), in_specs=in_specs,
        out_specs=[pl.BlockSpec((tm, tn), lambda i, j, k: (i, j)) for _ in out_dtypes],
        out_shape=[jax.ShapeDtypeStruct((M, N), dt) for dt in out_dtypes],
        scratch_shapes=[pltpu.VMEM((tm, tn), F32)] if nk > 1 else [],
        compiler_params=_params(("parallel", "parallel", "arbitrary")),
    )(a, b3, *extras, *after)


def _mm_nt(name, a, b3, out_dtypes, epilogue=None, extras=(), after=()):
    M, N = a.shape
    nB, K, Nsh = b3.shape
    assert N == nB * Nsh
    to, tn = _pick(K, 1536, LANE), _pick(Nsh, CONTRACT_TILE, LANE)
    out_bytes = sum(jnp.dtype(dt).itemsize for dt in out_dtypes) + sum(e.dtype.itemsize for e in extras)
    tm = _rows_that_fit(M, lambda t: 2 * (t * tn * a.dtype.itemsize + to * tn * b3.dtype.itemsize)
                        + t * to * (4 * (N > tn) + 2 * out_bytes))
    nps, nn, ne, no, na = Nsh // tn, N // tn, len(extras), len(out_dtypes), len(after)

    def body(a_ref, b_ref, *rest):
        e_refs, o_refs = rest[:ne], rest[ne + na:ne + na + no]

        def finish(total):
            vals = (total,) if epilogue is None else epilogue(total, *[e[...] for e in e_refs])
            for ref, val in zip(o_refs, vals):
                ref[...] = val.astype(ref.dtype)

        part = lax.dot_general(a_ref[...].astype(BF16), b_ref[...].astype(BF16),
                               (((1,), (1,)), ((), ())), preferred_element_type=F32)
        _accumulate(rest[-1], part, pl.program_id(2), nn, finish)

    in_specs = [pl.BlockSpec((tm, tn), lambda i, o, n: (i, n)),
                pl.BlockSpec((None, to, tn), lambda i, o, n: (n // nps, o, n % nps))]
    in_specs += [pl.BlockSpec((tm, to), lambda i, o, n: (i, o)) for _ in extras]
    in_specs += [pl.BlockSpec(memory_space=pl.ANY) for _ in after]
    return pl.pallas_call(
        body, name=name, grid=(M // tm, K // to, nn), in_specs=in_specs,
        out_specs=[pl.BlockSpec((tm, to), lambda i, o, n: (i, o)) for _ in out_dtypes],
        out_shape=[jax.ShapeDtypeStruct((M, K), dt) for dt in out_dtypes],
        scratch_shapes=[pltpu.VMEM((tm, to), F32)] if nn > 1 else [],
        compiler_params=_params(("parallel", "parallel", "arbitrary")),
    )(a, b3, *extras, *after)


def _mm_tn(name, a, g, nB, after=()):
    M, K = a.shape
    M2, N = g.shape
    assert M == M2 and N % nB == 0
    Nsh = N // nB
    tk, tn, tm = _pick(K, 1536, LANE), _pick(Nsh, 1024, LANE), _pick(M, CONTRACT_TILE, 16)
    nps, nm = Nsh // tn, M // tm

    def body(a_ref, g_ref, *rest):
        o_ref = rest[len(after)]

        def finish(total):
            o_ref[...] = total

        part = lax.dot_general(a_ref[...].astype(BF16), g_ref[...].astype(BF16),
                               (((0,), (0,)), ((), ())), preferred_element_type=F32)
        _accumulate(rest[-1], part, pl.program_id(2), nm, finish)

    return pl.pallas_call(
        body, name=name, grid=(K // tk, N // tn, nm),
        in_specs=[pl.BlockSpec((tm, tk), lambda k, n, m: (m, k)),
                  pl.BlockSpec((tm, tn), lambda k, n, m: (m, n))] + [pl.BlockSpec(memory_space=pl.ANY) for _ in after],
        out_specs=pl.BlockSpec((None, tk, tn), lambda k, n, m: (n // nps, k, n % nps)),
        out_shape=jax.ShapeDtypeStruct((nB, K, Nsh), F32),
        scratch_shapes=[pltpu.VMEM((tk, tn), F32)] if nm > 1 else [],
        compiler_params=_params(("parallel", "parallel", "arbitrary")),
    )(a, g, *after)


def _rows_mm(name, fn, rows, vecs, b3, after=()):
    S = rows[0].shape[0]
    _, K, N = b3.shape
    tm, tn = _pick(S, 1024, 16), _pick(N, 512, LANE)
    nr, nv, na = len(rows), len(vecs), len(after)

    def body(*refs):
        b_ref = refs[nr + nv]
        a_out, o_ref, a_scr = refs[nr + nv + 1 + na:]

        @pl.when(pl.program_id(1) == 0)
        def _():
            a = fn(*[r[...] for r in refs[:nr]], *[v[...] for v in refs[nr:nr + nv]]).astype(BF16)
            a_scr[...] = a
            a_out[...] = a

        o_ref[...] = jnp.dot(a_scr[...], b_ref[...], preferred_element_type=F32)

    in_specs = [pl.BlockSpec((tm, r.shape[1]), lambda i, j: (i, 0)) for r in rows]
    in_specs += [pl.BlockSpec(v.shape, lambda i, j: (0, 0)) for v in vecs]
    in_specs += [pl.BlockSpec((None, K, tn), lambda i, j: (0, 0, j))]
    in_specs += [pl.BlockSpec(memory_space=pl.ANY) for _ in after]
    return pl.pallas_call(
        body, name=name, grid=(S // tm, N // tn), in_specs=in_specs,
        out_specs=[pl.BlockSpec((tm, K), lambda i, j: (i, 0)), pl.BlockSpec((tm, tn), lambda i, j: (i, j))],
        out_shape=[jax.ShapeDtypeStruct((S, K), BF16), jax.ShapeDtypeStruct((S, N), F32)],
        scratch_shapes=[pltpu.VMEM((tm, K), BF16)],
        compiler_params=_params(("parallel", "arbitrary")),
    )(*rows, *vecs, b3, *after)


def _kv_mm(name, ckvn, w3, kpe):
    S, K = ckvn.shape
    nB, _, Nsh = w3.shape
    heads = Nsh // (2 * LANE)
    tm = _pick(S, 1024, 16)

    def body(a_ref, b_ref, kpe_ref, kc_ref, v_ref):
        acc = jnp.dot(a_ref[...], b_ref[...], preferred_element_type=F32)
        for h in range(heads):
            kc_ref[:, 2 * h * LANE:(2 * h + 1) * LANE] = acc[:, 2 * h * LANE:(2 * h + 1) * LANE].astype(BF16)
            kc_ref[:, (2 * h + 1) * LANE:(2 * h + 2) * LANE] = kpe_ref[...]
            v_ref[:, h * LANE:(h + 1) * LANE] = acc[:, (2 * h + 1) * LANE:(2 * h + 2) * LANE].astype(BF16)

    return pl.pallas_call(
        body, name=name, grid=(S // tm, nB),
        in_specs=[pl.BlockSpec((tm, K), lambda i, j: (i, 0)), pl.BlockSpec((None, K, Nsh), lambda i, j: (j, 0, 0)),
                  pl.BlockSpec((tm, LANE), lambda i, j: (i, 0))],
        out_specs=[pl.BlockSpec((tm, Nsh), lambda i, j: (i, j)), pl.BlockSpec((tm, Nsh // 2), lambda i, j: (i, j))],
        out_shape=[jax.ShapeDtypeStruct((S, nB * Nsh), BF16), jax.ShapeDtypeStruct((S, nB * Nsh // 2), BF16)],
        compiler_params=_params(("parallel", "arbitrary")),
    )(ckvn, w3, kpe)


def _kv_mm_bwd(name, dkc, dv, w3):
    S = dkc.shape[0]
    nB, K, Nsh = w3.shape
    heads = Nsh // (2 * LANE)
    tm = _pick(S, 1024, 16)

    def body(dkc_ref, dv_ref, b_ref, dck_ref, dkv_ref, dkpe_ref):
        j = pl.program_id(1)
        pieces, pe = [], None
        for h in range(heads):
            pieces += [dkc_ref[:, 2 * h * LANE:(2 * h + 1) * LANE], dv_ref[:, h * LANE:(h + 1) * LANE]]
            piece = dkc_ref[:, (2 * h + 1) * LANE:(2 * h + 2) * LANE]
            pe = piece if pe is None else pe + piece
        dkv = jnp.concatenate(pieces, axis=1).astype(BF16)
        dkv_ref[...] = dkv
        part = lax.dot_general(dkv, b_ref[...], (((1,), (1,)), ((), ())), preferred_element_type=F32)

        @pl.when(j == 0)
        def _():
            dck_ref[...] = part
            dkpe_ref[...] = pe

        @pl.when(j > 0)
        def _():
            dck_ref[...] += part
            dkpe_ref[...] += pe

    return pl.pallas_call(
        body, name=name, grid=(S // tm, nB),
        in_specs=[pl.BlockSpec((tm, Nsh), lambda i, j: (i, j)), pl.BlockSpec((tm, Nsh // 2), lambda i, j: (i, j)),
                  pl.BlockSpec((None, K, Nsh), lambda i, j: (j, 0, 0))],
        out_specs=[pl.BlockSpec((tm, K), lambda i, j: (i, 0)), pl.BlockSpec((tm, Nsh), lambda i, j: (i, j)),
                   pl.BlockSpec((tm, LANE), lambda i, j: (i, 0))],
        out_shape=[jax.ShapeDtypeStruct((S, K), F32), jax.ShapeDtypeStruct((S, nB * Nsh), BF16),
                   jax.ShapeDtypeStruct((S, LANE), F32)],
        compiler_params=_params(("parallel", "arbitrary")),
    )(dkc, dv, w3)


def _mm_tn_rows(name, a, g, n_rows, pieces):
    M, K = a.shape
    M2, N = g.shape
    assert M == M2
    tk, tn, tm = _pick(K, 1536, LANE), _pick(N, 512, LANE), _pick(M, CONTRACT_TILE, 16)
    nk, nm = K // tk, M // tm

    def body(a_ref, g_ref, o_ref, *acc):
        kk = pl.program_id(1)

        def finish(total):
            for t in range(nk):
                @pl.when(kk == t)
                def _(t=t):
                    for src, dst, n in pieces:
                        lo, hi = max(src, t * tk), min(src + n, (t + 1) * tk)
                        if lo < hi:
                            o_ref[dst + lo - src:dst + hi - src, :] = total[lo - t * tk:hi - t * tk, :]

        part = lax.dot_general(a_ref[...].astype(BF16), g_ref[...].astype(BF16),
                               (((0,), (0,)), ((), ())), preferred_element_type=F32)
        _accumulate(acc[0] if acc else None, part, pl.program_id(2), nm, finish)

    return pl.pallas_call(
        body, name=name, grid=(N // tn, nk, nm),
        in_specs=[pl.BlockSpec((tm, tk), lambda n, k, m: (m, k)),
                  pl.BlockSpec((tm, tn), lambda n, k, m: (m, n))],
        out_specs=pl.BlockSpec((n_rows, tn), lambda n, k, m: (0, n)),
        out_shape=jax.ShapeDtypeStruct((n_rows, N), F32),
        scratch_shapes=[pltpu.VMEM((tk, tn), F32)] if nm > 1 else [],
        compiler_params=_params(("arbitrary", "arbitrary", "arbitrary")),
    )(a, g)


def _attn_fwd(name, q, k, v, n_heads, group, dqk, dv, scale, tq, after=()):
    S = q.shape[0]
    hp = 2 * HEADS_PER_STEP
    kvp = max(1, hp // group)

    def body(q_ref, k_ref, v_ref, *rest):
        o_ref, lse_ref = rest[-2:]
        for j in range(hp):
            jk = j * kvp // hp
            s = lax.dot_general(q_ref[:, j * dqk:(j + 1) * dqk], k_ref[:, jk * dqk:(jk + 1) * dqk],
                                (((1,), (1,)), ((), ())), preferred_element_type=F32)
            m = jnp.max(s, axis=-1, keepdims=True)
            e = jnp.exp((s - m) * scale)
            l = jnp.sum(e, axis=-1, keepdims=True)
            o_ref[:, j * dv:(j + 1) * dv] = jnp.dot(e.astype(BF16), v_ref[:, jk * dv:(jk + 1) * dv],
                                                    preferred_element_type=F32) * (1.0 / l)
            lse_ref[:, j * LANE:(j + 1) * LANE] = jnp.broadcast_to(m * scale + jnp.log(l), (tq, LANE))

    kv_block = lambda h, i: (0, (h * hp // group) // kvp)
    return pl.pallas_call(
        body, name=name, grid=(n_heads // hp, S // tq),
        in_specs=[pl.BlockSpec((tq, hp * dqk), lambda h, i: (i, h)),
                  pl.BlockSpec((S, kvp * dqk), kv_block),
                  pl.BlockSpec((S, kvp * dv), kv_block)] + [pl.BlockSpec(memory_space=pl.ANY) for _ in after],
        out_specs=[pl.BlockSpec((tq, hp * dv), lambda h, i: (i, h)),
                   pl.BlockSpec((tq, hp * LANE), lambda h, i: (i, h))],
        out_shape=[jax.ShapeDtypeStruct((S, n_heads * dv), F32), jax.ShapeDtypeStruct((S, n_heads * LANE), F32)],
        compiler_params=_params(("arbitrary", "arbitrary")),
    )(q, k, v, *after)


def _attn_bwd(name, q, k, v, do, o, lse, n_heads, group, dqk, dv, scale, tq, after=()):
    S = q.shape[0]
    n_kv = n_heads // group
    hp, tq = (group, tq // 2) if group > HEADS_PER_STEP else (HEADS_PER_STEP, tq)
    kvp = max(1, hp // group)
    chains = 4
    tr = tq * hp // chains

    def body(q_ref, k_ref, v_ref, do_ref, o_ref, lse_ref, *rest):
        dq_ref, dk_ref, dv_ref = rest[len(after):]
        h, i = pl.program_id(0), pl.program_id(1)

        @pl.when(((h * hp) % group == 0) & (i == 0))
        def _():
            dk_ref[...] = jnp.zeros(dk_ref.shape, F32)
            dv_ref[...] = jnp.zeros(dv_ref.shape, F32)

        for j in range(hp):
            jk = j * kvp // hp
            kb, vb = k_ref[:, jk * dqk:(jk + 1) * dqk], v_ref[:, jk * dv:(jk + 1) * dv]
            for r0 in range(0, tq, tr):
                rows = slice(r0, r0 + tr)
                qb, dob = q_ref[rows, j * dqk:(j + 1) * dqk], do_ref[rows, j * dv:(j + 1) * dv]
                s = lax.dot_general(qb, kb, (((1,), (1,)), ((), ())), preferred_element_type=F32)
                p = jnp.exp(s * scale - jnp.tile(lse_ref[rows, j * LANE:(j + 1) * LANE], (1, S // LANE)))
                dp = lax.dot_general(dob, vb, (((1,), (1,)), ((), ())), preferred_element_type=F32)
                delta = jnp.sum(dob.astype(F32) * o_ref[rows, j * dv:(j + 1) * dv], axis=-1, keepdims=True)
                ds = (p * ((dp - delta) * scale)).astype(BF16)
                dv_ref[:, jk * dv:(jk + 1) * dv] += lax.dot_general(p.astype(BF16), dob, (((0,), (0,)), ((), ())),
                                                                    preferred_element_type=F32)
                dq_ref[rows, j * dqk:(j + 1) * dqk] = jnp.dot(ds, kb, preferred_element_type=F32)
                dk_ref[:, jk * dqk:(jk + 1) * dqk] += lax.dot_general(ds, qb, (((0,), (0,)), ((), ())),
                                                                      preferred_element_type=F32)

    kv_block = lambda h, i: (0, (h * hp // group) // kvp)
    return pl.pallas_call(
        body, name=name, grid=(n_heads // hp, S // tq),
        in_specs=[pl.BlockSpec((tq, hp * dqk), lambda h, i: (i, h)),
                  pl.BlockSpec((S, kvp * dqk), kv_block),
                  pl.BlockSpec((S, kvp * dv), kv_block),
                  pl.BlockSpec((tq, hp * dv), lambda h, i: (i, h)),
                  pl.BlockSpec((tq, hp * dv), lambda h, i: (i, h)),
                  pl.BlockSpec((tq, hp * LANE), lambda h, i: (i, h))] + [pl.BlockSpec(memory_space=pl.ANY) for _ in after],
        out_specs=[pl.BlockSpec((tq, hp * dqk), lambda h, i: (i, h)),
                   pl.BlockSpec((S, kvp * dqk), kv_block),
                   pl.BlockSpec((S, kvp * dv), kv_block)],
        out_shape=[jax.ShapeDtypeStruct((S, n_heads * dqk), F32),
                   jax.ShapeDtypeStruct((S, n_kv * dqk), F32),
                   jax.ShapeDtypeStruct((S, n_kv * dv), F32)],
        compiler_params=_params(("arbitrary", "arbitrary")),
    )(q, k, v, do, o, lse, *after)


def _adamw_step(w, gg, m, v):
    nm = ADAM_B1 * m + (1.0 - ADAM_B1) * gg
    nv = ADAM_B2 * v + (1.0 - ADAM_B2) * (gg * gg)
    c1, c2 = 1.0 - ADAM_B1 ** ADAM_STEP, 1.0 - ADAM_B2 ** ADAM_STEP
    return -ADAM_LR * ((nm / c1) / (jnp.sqrt(nv / c2) + ADAM_EPS) + ADAM_WD * w), nm, nv


def _adamw(name, w, g, m, v, after=(), rewrite_g=False):
    R, C = w.shape
    tile = _pick(R, max(8, (1 << 19) // C // 8 * 8), 8)
    n_out = 4 if rewrite_g else 3

    def body(w_ref, g_ref, m_ref, v_ref, *rest):
        d_ref, nm_ref, nv_ref = rest[len(after):len(after) + 3]
        gg = g_ref[...]
        d_ref[...], nm_ref[...], nv_ref[...] = _adamw_step(w_ref[...], gg, m_ref[...], v_ref[...])
        if rewrite_g:
            rest[-1][...] = gg

    spec = pl.BlockSpec((tile, C), lambda i: (i, 0))
    return pl.pallas_call(
        body, name=name, grid=(R // tile,), in_specs=[spec] * 4 + [pl.BlockSpec(memory_space=pl.ANY) for _ in after],
        out_specs=[spec] * n_out, out_shape=[jax.ShapeDtypeStruct((R, C), F32)] * n_out,
        compiler_params=_params(("parallel",)),
    )(w, g, m, v, *after)


def _adamw_outer(name, w, left, right, m, v, after=()):
    R, C = w.shape
    Kc = left.shape[0]
    tile = _pick(R, max(LANE, (1 << 19) // C // LANE * LANE), LANE)

    def body(w_ref, l_ref, r_ref, m_ref, v_ref, *rest):
        d_ref, nm_ref, nv_ref, g_ref = rest[len(after):]
        gg = lax.dot_general(l_ref[...], r_ref[...], (((0,), (0,)), ((), ())), preferred_element_type=F32)
        d_ref[...], nm_ref[...], nv_ref[...] = _adamw_step(w_ref[...], gg, m_ref[...], v_ref[...])
        g_ref[...] = gg

    spec = pl.BlockSpec((tile, C), lambda i: (i, 0))
    return pl.pallas_call(
        body, name=name, grid=(R // tile,),
        in_specs=[spec, pl.BlockSpec((Kc, tile), lambda i: (0, i)), pl.BlockSpec((Kc, C), lambda i: (0, 0)), spec, spec]
        + [pl.BlockSpec(memory_space=pl.ANY) for _ in after],
        out_specs=[spec] * 4, out_shape=[jax.ShapeDtypeStruct((R, C), F32)] * 4,
        compiler_params=_params(("parallel",)),
    )(w, left, right, m, v, *after)


def _flip(me, rel):
    return tuple(1 - p if r else p for p, r in zip(me, rel))


def _push(name, ins, out_shapes, transfers, aliases=None):
    ni, no, nt = len(ins), len(out_shapes), len(transfers)

    def body(*refs):
        in_refs, out_refs = refs[:ni], refs[ni:ni + no]
        send_sems, recv_sems = refs[ni + no], refs[ni + no + 1]
        me = (lax.axis_index("x"), lax.axis_index("y"), lax.axis_index("c"))
        copies = []
        for t, (si, sfn, oi, dfn, rel) in enumerate(transfers):
            src_ref = out_refs[si[1]] if isinstance(si, tuple) else in_refs[si]
            src = src_ref.at[sfn(*me)]
            dst = out_refs[oi].at[dfn(*me)]
            if rel is None:
                cp = pltpu.make_async_copy(src, dst, send_sems.at[t])
            else:
                cp = pltpu.make_async_remote_copy(src_ref=src, dst_ref=dst, send_sem=send_sems.at[t],
                                                  recv_sem=recv_sems.at[t], device_id=_flip(me, rel),
                                                  device_id_type=MESH)
            cp.start()
            copies.append(cp)
        for cp in copies:
            cp.wait()

    any_spec = pl.BlockSpec(memory_space=pl.ANY)
    return pl.pallas_call(
        body, name=name, in_specs=[any_spec] * ni, out_specs=[any_spec] * no, out_shape=out_shapes,
        scratch_shapes=[pltpu.SemaphoreType.DMA((nt,)), pltpu.SemaphoreType.DMA((nt,))],
        input_output_aliases=aliases or {},
        compiler_params=pltpu.CompilerParams(has_side_effects=True),
    )(*ins)


_HBM = pl.BlockSpec(memory_space=pltpu.HBM)
_SEM = pl.BlockSpec(memory_space=pltpu.SEMAPHORE)
_DATAFLOW = pltpu.SideEffectType.DATAFLOW_SIDE_EFFECTING


def _split_copies(refs, transfers, send_sems, recv_sems):
    me = (lax.axis_index("x"), lax.axis_index("y"), lax.axis_index("c"))
    return [pltpu.make_async_remote_copy(src_ref=refs[sb].at[sfn(*me)], dst_ref=refs[db].at[dfn(*me)],
                                         send_sem=send_sems.at[t], recv_sem=recv_sems.at[t],
                                         device_id=_flip(me, rel), device_id_type=MESH)
            for t, (sb, sfn, db, dfn, rel) in enumerate(transfers)]


def _push_start(name, bufs, transfers, after=()):
    nb, na, nt = len(bufs), len(after), len(transfers)

    def body(*refs):
        send_sems, recv_sems, token = refs[nb + na], refs[nb + na + 1], refs[-1]
        for cp in _split_copies(refs[nb + na + 2:2 * nb + na + 2], transfers, send_sems, recv_sems):
            cp.start()
        token[...] = jnp.zeros(token.shape, token.dtype)

    res = pl.pallas_call(
        body, name=name,
        out_shape=(pltpu.SemaphoreType.DMA((nt,)), pltpu.SemaphoreType.DMA((nt,)),
                   *[pltpu.HBM(b.shape, b.dtype) for b in bufs], jax.ShapeDtypeStruct((8, LANE), F32)),
        in_specs=[_HBM] * nb + [pl.BlockSpec(memory_space=pl.ANY)] * na,
        out_specs=(_SEM, _SEM, *[_HBM] * nb, pl.BlockSpec(memory_space=pltpu.VMEM)),
        input_output_aliases={i: i + 2 for i in range(nb)},
        compiler_params=pltpu.CompilerParams(has_side_effects=_DATAFLOW),
    )(*[pltpu.with_memory_space_constraint(b, pltpu.HBM) for b in bufs], *after)
    return res[0], res[1], list(res[2:2 + nb]), res[-1]


def _push_wait(name, send_sems, recv_sems, bufs, transfers, after):
    nb = len(bufs)

    def body(*refs):
        for cp in _split_copies(refs[:nb], transfers, refs[nb], refs[nb + 1]):
            cp.wait_send()
            cp.wait_recv()

    res = pl.pallas_call(
        body, name=name, out_shape=[pltpu.HBM(b.shape, b.dtype) for b in bufs],
        in_specs=[_HBM] * nb + [_SEM, _SEM, pl.BlockSpec(memory_space=pl.ANY)], out_specs=[_HBM] * nb,
        input_output_aliases={i: i for i in range(nb)},
        compiler_params=pltpu.CompilerParams(has_side_effects=_DATAFLOW),
    )(*bufs, send_sems, recv_sems, after)
    return list(res)


ICI_RELS = ((1, 0, 0), (0, 1, 0), (1, 1, 0))
SIBLING = (0, 0, 1)


def _chip(x, y):
    return 2 * x + y


def _cast_into_slot(name, where, w, after=()):
    R, C = w.shape
    tile = _pick(R, 512, BF16_ROWS)

    def body(where_ref, w_ref, *rest):
        rest[-1][...] = w_ref[...].astype(BF16)

    return pl.pallas_call(
        body, name=name,
        grid_spec=pltpu.PrefetchScalarGridSpec(
            num_scalar_prefetch=1, grid=(R // tile,),
            in_specs=[pl.BlockSpec((tile, C), lambda i, wh: (i, 0))] + [pl.BlockSpec(memory_space=pl.ANY) for _ in after],
            out_specs=pl.BlockSpec((None, tile, C), lambda i, wh: (wh[1], i, 0))),
        out_shape=jax.ShapeDtypeStruct((N_CHIPS, R, C), BF16),
        compiler_params=_params(("parallel",)),
    )(where, w, *after)


BF16_ROWS = 16


def _rows_split(shape2):
    return (shape2[0] // 2) % BF16_ROWS == 0


def _half_shape(shape2):
    R, C = shape2
    return (R // 2, C) if _rows_split(shape2) else (R, C // 2)


def _half(shape2, c):
    R, C = shape2
    if _rows_split(shape2):
        return (pl.ds(c * (R // 2), R // 2), slice(None))
    return (slice(None), pl.ds(pl.multiple_of(c * (C // 2), LANE), C // 2))


def _gather_casts(tag, where, shards, after=()):
    return [_cast_into_slot(f"cast_w_{tag}{w}", where, s, after) for w, s in enumerate(shards)]


def _gather_start(tag, own, after):
    transfers = []
    for w in range(len(own)):
        def idx(x, y, c, shape2=own[w].shape[1:]):
            return (_chip(x, y), *_half(shape2, c))
        transfers += [(w, idx, w, idx, rel) for rel in ICI_RELS]
    send_sems, recv_sems, bufs, token = _push_start(f"gather_{tag}_start", own, transfers, after)
    return (tag, send_sems, recv_sems, bufs, transfers), token


def _forward_transfers(part):
    second = []
    for w, p in enumerate(part):
        for rel in ICI_RELS:
            def idx(x, y, c, shape2=p.shape[1:], rel=rel):
                return (_chip(x ^ rel[0], y ^ rel[1]), *_half(shape2, c))
            second.append((w, idx, w, idx, SIBLING))
    return second


def _gather_finish(handle, after):
    tag, send_sems, recv_sems, bufs, transfers = handle
    part = _push_wait(f"gather_{tag}_wait", send_sems, recv_sems, bufs, transfers, after)
    outs = [jax.ShapeDtypeStruct(p.shape, p.dtype) for p in part]
    second = [(("out", sb), sfn, db, dfn, rel) for sb, sfn, db, dfn, rel in _forward_transfers(part)]
    return _push(f"gather_{tag}_d2d", list(part), outs, second, aliases={w: w for w in range(len(part))})


def _gather_mid(handle, after):
    tag, send_sems, recv_sems, bufs, transfers = handle
    part = _push_wait(f"gather_{tag}_wait", send_sems, recv_sems, bufs, transfers, after)
    second = _forward_transfers(part)
    s2, r2, bufs2, token = _push_start(f"gather_{tag}_d2d_start", part, second)
    return (tag, s2, r2, bufs2, second), token


def _gather_end(handle, after):
    tag, send_sems, recv_sems, bufs, transfers = handle
    return _push_wait(f"gather_{tag}_d2d_wait", send_sems, recv_sems, bufs, transfers, after)


def _reduce_begin(tag, grads, after=()):
    n = len(grads)
    land = [lax.empty((N_CHIPS, *_half_shape(g.shape[1:])), F32) for g in grads]
    t1 = []
    for w in range(n):
        t1.append((w, lambda x, y, c, shape2=grads[w].shape[1:]: (slice(None), *_half(shape2, 1 - c)), n + w,
                   lambda x, y, c: (slice(None), slice(None), slice(None)), SIBLING))
    send_sems, recv_sems, bufs, token = _push_start(f"reduce_{tag}_d2d_start", list(grads) + land, t1, after)
    return (tag, send_sems, recv_sems, bufs, t1), token


def _reduce_start(handle, where, after, after_start=()):
    tag, send_sems, recv_sems, bufs, t1 = handle
    n = len(bufs) // 2
    res = _push_wait(f"reduce_{tag}_d2d_wait", send_sems, recv_sems, bufs, t1, after)
    grads, from_sib = res[:n], res[n:]
    pair = [_pair_sum(f"pair_sum_{tag}{w}", where, grads[w], from_sib[w]) for w in range(n)]
    land = [lax.empty((3, *_half_shape(g.shape[1:])), BF16) for g in grads]
    t3 = []
    for w in range(n):
        for r, rel in enumerate(ICI_RELS):
            t3.append((w, lambda x, y, c, rel=rel: (_chip(x ^ rel[0], y ^ rel[1]), slice(None), slice(None)), n + w,
                       lambda x, y, c, r=r: (r, slice(None), slice(None)), rel))
    send_sems, recv_sems, bufs, token = _push_start(f"reduce_{tag}_start", pair + land, t3, after_start)
    return (tag, send_sems, recv_sems, bufs, t3, list(grads), list(from_sib)), token


def _reduce_finish(handle, where, after):
    tag, send_sems, recv_sems, bufs, t3, grads, from_sib = handle
    n = len(grads)
    from_chips = _push_wait(f"reduce_{tag}_wait", send_sems, recv_sems, bufs, t3, after)[n:]
    mine = [_chip_sum(f"chip_sum_{tag}{w}", where, grads[w], from_sib[w], from_chips[w]) for w in range(n)]
    t5 = []
    for w in range(n):
        def idx(x, y, c, shape2=grads[w].shape[1:]):
            return _half(shape2, c)
        t5.append((w, idx, w, idx, SIBLING))
    send_sems, recv_sems, bufs, token = _push_start(f"reduce_{tag}_swap_start", mine, t5)
    return (tag, send_sems, recv_sems, bufs, t5), token


def _reduce_end(handle, after):
    tag, send_sems, recv_sems, bufs, t5 = handle
    return _push_wait(f"reduce_{tag}_swap_wait", send_sems, recv_sems, bufs, t5, after)


def _pair_sum(name, where, g, from_sib):
    _, hr, hc = from_sib.shape
    tile = _pick(hr, max(16, (1 << 19) // hc // 16 * 16), 16)
    nb = hr // tile
    slot = lambda k, wh: (wh[1] + 1 + k) % N_CHIPS
    if _rows_split(g.shape[1:]):
        mine = lambda k, i, wh: (slot(k, wh), wh[0] * nb + i, 0)
    else:
        mine = lambda k, i, wh: (slot(k, wh), i, wh[0])

    def body(where_ref, g_ref, s_ref, o_ref):
        o_ref[...] = (g_ref[...] + s_ref[...]).astype(BF16)

    return pl.pallas_call(
        body, name=name,
        grid_spec=pltpu.PrefetchScalarGridSpec(
            num_scalar_prefetch=1, grid=(N_CHIPS - 1, nb),
            in_specs=[pl.BlockSpec((None, tile, hc), mine),
                      pl.BlockSpec((None, tile, hc), lambda k, i, wh: (slot(k, wh), i, 0))],
            out_specs=pl.BlockSpec((None, tile, hc), lambda k, i, wh: (slot(k, wh), i, 0))),
        out_shape=jax.ShapeDtypeStruct((N_CHIPS, hr, hc), BF16),
        compiler_params=_params(("parallel", "parallel")),
    )(where, g, from_sib)


def _chip_sum(name, where, g, from_sib, from_chips):
    _, R, C = g.shape
    _, hr, hc = from_sib.shape
    tile = _pick(hr, max(16, (1 << 19) // hc // 16 * 16), 16)
    nb = hr // tile
    if _rows_split((R, C)):
        mine = lambda i, wh: (wh[0] * nb + i, 0)
    else:
        mine = lambda i, wh: (i, wh[0])

    def body(where_ref, g_ref, s_ref, r_ref, o_ref):
        acc = g_ref[...] + s_ref[...]
        for r in range(3):
            acc = acc + r_ref[r].astype(F32)
        o_ref[...] = acc

    return pl.pallas_call(
        body, name=name,
        grid_spec=pltpu.PrefetchScalarGridSpec(
            num_scalar_prefetch=1, grid=(nb,),
            in_specs=[pl.BlockSpec((None, tile, hc), lambda i, wh: (wh[1], *mine(i, wh))),
                      pl.BlockSpec((None, tile, hc), lambda i, wh: (wh[1], i, 0)),
                      pl.BlockSpec((3, tile, hc), lambda i, wh: (0, i, 0))],
            out_specs=pl.BlockSpec((tile, hc), mine)),
        out_shape=jax.ShapeDtypeStruct((R, C), F32),
        compiler_params=_params(("parallel",)),
    )(where, g, from_sib, from_chips)


def _allgather8(name, blk, after=()):
    m_per, n = blk.shape
    rels = [(0, 0, 1), (1, 0, 0), (0, 1, 0), (1, 1, 0), (1, 0, 1), (0, 1, 1), (1, 1, 1)]

    def body(x_ref, *rest):
        out_ref, send_sems, recv_sems = rest[len(after):]
        me = (lax.axis_index("x"), lax.axis_index("y"), lax.axis_index("c"))
        my_rows = out_ref.at[pl.ds((4 * me[0] + 2 * me[1] + me[2]) * m_per, m_per), :]
        out_ref[pl.ds((4 * me[0] + 2 * me[1] + me[2]) * m_per, m_per), :] = x_ref[...]
        copies = []
        for t, rel in enumerate(rels):
            cp = pltpu.make_async_remote_copy(src_ref=x_ref, dst_ref=my_rows, send_sem=send_sems.at[t],
                                              recv_sem=recv_sems.at[t], device_id=_flip(me, rel),
                                              device_id_type=MESH)
            cp.start()
            copies.append(cp)
        for cp in copies:
            cp.wait()

    return pl.pallas_call(
        body, name=name, out_shape=jax.ShapeDtypeStruct((N_DEV * m_per, n), blk.dtype),
        in_specs=[pl.BlockSpec(memory_space=pltpu.VMEM)] + [pl.BlockSpec(memory_space=pl.ANY) for _ in after],
        out_specs=pl.BlockSpec(memory_space=pltpu.VMEM),
        scratch_shapes=[pltpu.SemaphoreType.DMA((7,)), pltpu.SemaphoreType.DMA((7,))],
        compiler_params=pltpu.CompilerParams(has_side_effects=True, vmem_limit_bytes=VMEM_LIMIT),
    )(blk, *after)


def _rope_tables(seq_len, dim, reps):
    rows = seq_len // GRID_W
    t = np.arange(seq_len)
    row, col = (t // GRID_W).astype(np.float32), (t % GRID_W).astype(np.float32)
    half = dim // 2
    inv = jnp.asarray(ROPE_THETA, F32) ** (-jnp.arange(0, half, 2, dtype=F32) / half)
    ang_r = jnp.asarray(row)[:, None] * inv[None, :]
    ang_c = jnp.asarray(col)[:, None] * inv[None, :]
    ang = jnp.concatenate([ang_r, ang_r, ang_c, ang_c], axis=-1)
    cos, sin = jnp.cos(ang), jnp.sin(ang)
    low = (np.arange(dim) % (dim // 2)) < (dim // 4)
    s_lo = jnp.where(jnp.asarray(low)[None, :], -sin, 0.0)
    s_hi = jnp.where(jnp.asarray(low)[None, :], 0.0, sin)
    del rows
    return tuple(jnp.tile(a, (1, reps)) for a in (cos, s_lo, s_hi))


def _make_rope(shift):
    def up(v):
        return pltpu.roll(v, LANE - shift, 1)

    def down(v):
        return pltpu.roll(v, shift, 1)

    @jax.custom_vjp
    def rope(v, cos, s_lo, s_hi):
        return v * cos + up(v) * s_lo + down(v) * s_hi

    def fwd(v, cos, s_lo, s_hi):
        return rope(v, cos, s_lo, s_hi), (cos, s_lo, s_hi)

    def bwd(res, dy):
        cos, s_lo, s_hi = res
        return dy * cos + down(dy * s_lo) + up(dy * s_hi), jnp.zeros_like(cos), jnp.zeros_like(cos), jnp.zeros_like(cos)

    rope.defvjp(fwd, bwd)
    return rope


_rope_a = _make_rope(HEAD_DIM // 4)
_rope_b = _make_rope(QK_ROPE // 4)


def _prep(segs, tabs, g_q, g_k, g_ckv):
    qa, ka, va, qn, qp, ckv, kpe = segs
    ta, tb = tabs
    q_a = jnp.concatenate([_rope_a(_rn(s) * g_q, *ta) for s in qa], axis=1)
    k_a = jnp.concatenate([_rope_a(_rn(s) * g_k, *ta) for s in ka], axis=1)
    cat = []
    for h in range(HB):
        cat += [qn[h], _rope_b(qp[h], *tb)]
    q_cat = jnp.concatenate(cat, axis=1)
    return q_a, k_a, va, q_cat, _rn(ckv) * g_ckv, _rope_b(kpe, *tb)


def _split_proj(proj):
    qa = [proj[:, O_QA + h * LANE:O_QA + (h + 1) * LANE] for h in range(HA)]
    ka = [proj[:, O_KA + h * LANE:O_KA + (h + 1) * LANE] for h in range(HKV)]
    va = proj[:, O_VA:O_QN]
    qn = [proj[:, O_QN + h * LANE:O_QN + (h + 1) * LANE] for h in range(HB)]
    qp = [proj[:, O_QP + h * LANE:O_QP + (h + 1) * LANE] for h in range(HB)]
    return qa, ka, va, qn, qp, proj[:, O_CKV:O_KPE], proj[:, O_KPE:W_INP]


def _permute_w_in(nat):
    K = nat.shape[1]
    qb = nat[W_QA + 2 * W_KA:W_QA + 2 * W_KA + W_QB].reshape(HB, QK_B, K)
    qp = jnp.concatenate([qb[:, QK_NOPE:], jnp.zeros((HB, LANE - QK_ROPE, K), nat.dtype)], axis=1)
    tail = nat[W_QA + 2 * W_KA + W_QB:]
    return jnp.concatenate([nat[:O_QN], qb[:, :QK_NOPE].reshape(HB * QK_NOPE, K), qp.reshape(HB * LANE, K),
                            tail, jnp.zeros((LANE - QK_ROPE, K), nat.dtype)], axis=0)


def _unpermute_pieces():
    pieces = [(0, 0, O_QN)]
    for h in range(HB):
        pieces.append((O_QN + h * QK_NOPE, O_QN + h * QK_B, QK_NOPE))
        pieces.append((O_QP + h * LANE, O_QN + h * QK_B + QK_NOPE, QK_ROPE))
    pieces.append((O_CKV, O_QN + W_QB, KV_RANK))
    pieces.append((O_KPE, O_QN + W_QB + KV_RANK, QK_ROPE))
    return pieces


def kernel(x, c, w_ada, b_ada, g_pre_attn, w_in, g_q_a, g_k_a, g_ckv, w_kv_b, g_out_a, g_out_b, w_out, g_post_attn, g_pre_mlp, w_mlp_in, w_mlp_out, g_post_mlp, loss_target, m_w_ada, m_b_ada, m_g_pre_attn, m_w_in, m_g_q_a, m_g_k_a, m_g_ckv, m_w_kv_b, m_g_out_a, m_g_out_b, m_w_out, m_g_post_attn, m_g_pre_mlp, m_w_mlp_in, m_w_mlp_out, m_g_post_mlp, v_w_ada, v_b_ada, v_g_pre_attn, v_w_in, v_g_q_a, v_g_k_a, v_g_ckv, v_w_kv_b, v_g_out_a, v_g_out_b, v_w_out, v_g_post_attn, v_g_pre_mlp, v_w_mlp_in, v_w_mlp_out, v_g_post_mlp):
    S, D = x.shape[1], x.shape[2]
    x2d, tgt = x[0], loss_target[0]
    ix, iy, ic = lax.axis_index("x"), lax.axis_index("y"), lax.axis_index("c")
    j_me = _chip(ix, iy)
    e_me = 4 * ix + 2 * iy + ic
    T = _pick(S, 512, 8)
    TB = _pick(S, 256, 8)

    where = jnp.stack([ic, j_me]).astype(jnp.int32)
    DFF = w_mlp_out.shape[1] * N_CHIPS
    tabs_a = _rope_tables(S, HEAD_DIM, 1)
    tabs_b = _rope_tables(S, QK_ROPE, LANE // QK_ROPE)

    c_all = _allgather8("gather_c", c.reshape(8, D // 8)).reshape(N_DEV, D)
    w_in_t = jnp.swapaxes(w_in[0], 0, 1)
    h_qkv, tok_a = _gather_start("qkv", _gather_casts("qkv", where, [w_in_t, w_kv_b[0]]), (c_all,))
    later = [_gather_casts(tag, where, [w[0]], (tok_a,))
             for tag, w in (("out", w_out), ("up", w_mlp_in), ("down", w_mlp_out))]

    def silu_fn(cc):
        return (cc * (1.0 / (1.0 + jnp.exp(-cc))),), ()

    (c_act,), _ = _rowwise("silu_c", silu_fn, [c_all], [], [(D, BF16)], [], N_DEV)
    ca_pad = jnp.concatenate([c_act, jnp.zeros_like(c_act)], axis=0)
    n_mod = w_ada.shape[2]
    b_cols = lax.dynamic_slice(b_ada, (0, j_me * n_mod), (1, n_mod))
    (mod_part,) = _mm_nn("mod_mm", ca_pad, w_ada, [F32], epilogue=lambda acc, b: (acc + b,),
                         extras=(jnp.broadcast_to(b_cols, (16, n_mod)),))
    mod_all = _allgather8("gather_mod", mod_part[:8] + tok_a[0, 0],
                          after=[c_[0] for c_ in later] + [*tabs_a, *tabs_b]).reshape(N_DEV, 8, n_mod)
    mod = jnp.concatenate([lax.dynamic_slice(mod_all, (2 * j, e_me, 0), (1, 1, n_mod))[0] for j in range(N_CHIPS)],
                          axis=1)

    m_qkv, tok_q = _gather_mid(h_qkv, mod_all)
    h_out, tok_b = _gather_start("out", later[0], (tok_q,))
    h_up, tok_c = _gather_start("up", later[1], (tok_b,))
    h_down, tok_d = _gather_start("down", later[2], (tok_c,))
    mod = mod + tok_d[0, 0]
    sh_a, sc_a, gt_a, sh_m, sc_m, gt_m = [mod[:, i * D:(i + 1) * D] for i in range(6)]

    def pre_attn_fn(xt, g, sc, sh):
        return (_rn(xt) * g * (1.0 + sc) + sh,), ()

    (h_b,), _ = _rowwise("pre_attn", pre_attn_fn, [x2d], [g_pre_attn, sc_a, sh_a], [(D, BF16)], [], T)
    g_in, g_kvb = _gather_end(m_qkv, h_b)
    wpt = _permute_w_in(g_in.reshape(W_IN, D))[None]
    (proj,) = _mm_nt("proj_mm", h_b, wpt, [F32])

    def prep_fn(pt, ca_, sla, sha, cb_, slb, shb, gq, gk, gc):
        return _prep(_split_proj(pt), ((ca_, sla, sha), (cb_, slb, shb)), gq, gk, gc), ()

    (qa_b, ka_b, va_b, qcat_b, ckvn_b, kpe_b), _ = _rowwise(
        "prep", prep_fn, [proj, *tabs_a, *tabs_b], [g_q_a, g_k_a, g_ckv],
        [(W_QA, BF16), (W_KA, BF16), (W_KA, BF16), (2 * HB * LANE, BF16), (KV_RANK, BF16), (LANE, BF16)], [], T)
    kcat_b, vb_b = _kv_mm("kv_mm", ckvn_b, g_kvb, kpe_b)

    TQ = _pick(S, 512, 16)
    sc_a_ = 1.0 / math.sqrt(HEAD_DIM)
    sc_b_ = 1.0 / math.sqrt(QK_B)
    o_a, lse_a = _attn_fwd("attn_a_fwd", qa_b, ka_b, va_b, HA, HA // HKV, HEAD_DIM, HEAD_DIM, sc_a_, TQ)
    m_out, tok_m = _gather_mid(h_out, o_a)
    o_b, lse_b = _attn_fwd("attn_b_fwd", qcat_b, kcat_b, vb_b, HB, 1, 2 * LANE, V_DIM, sc_b_, TQ, after=(tok_m,))
    m_up, tok_m = _gather_mid(h_up, o_b)

    def mix_fn(oa, ob, ga, gb):
        return jnp.concatenate([_rn(oa) * ga, _rn(ob) * gb], axis=1)

    (g_wout,) = _gather_end(m_out, o_b)
    w_out_f = g_wout.reshape(1, N_CHIPS * g_wout.shape[1], D)
    on_b, o2 = _rows_mm("out_mm", mix_fn, [o_a, o_b], [g_out_a, g_out_b], w_out_f, after=(tok_m,))

    def mid(xt, o2t, gta, gpa, gpm, scm, shm):
        x1 = xt + gta * (_rn(o2t) * gpa)
        return x1, _rn(x1) * gpm * (1.0 + scm) + shm

    def mid_fn(*a):
        return mid(*a), ()

    mid_vecs = [gt_a, g_post_attn, g_pre_mlp, sc_m, sh_m]
    (x1, h2_b), _ = _rowwise("mid", mid_fn, [x2d, o2], mid_vecs, [(D, F32), (D, BF16)], [], T)
    (g_mlp_in,) = _gather_end(m_up, h2_b)
    relu_b, a_b = _mm_nn("mlp_in_mm", h2_b, g_mlp_in, [BF16, BF16],
                         epilogue=lambda acc: (jnp.maximum(acc, 0.0), jnp.square(jnp.maximum(acc, 0.0))))
    (g_mlp_out,) = _gather_finish(h_down, a_b)
    w_mlp_out_f = g_mlp_out.reshape(1, DFF, D)
    (y,) = _mm_nn("mlp_out_mm", a_b, w_mlp_out_f, [F32])

    def last_fn(yt, x1t, tt, gtm, gpo):
        n, r = _rn_parts(yt)
        err = x1t + gtm * (n * gpo) - tt
        dx2 = err * (1.0 / D)
        loss = 0.5 * jnp.sum(jnp.mean(err * err, axis=-1))
        s = _rows(dx2 * n)
        return (_rn_bwd(n, r, dx2 * (gtm * gpo)), dx2), (jnp.full((1, LANE), loss, F32), gpo * s, gtm * s)

    (dy_b, dx2), (loss_v, d_gt_m, d_g_post_mlp) = _rowwise(
        "loss_bwd", last_fn, [y, x1, tgt], [gt_m, g_post_mlp], [(D, BF16), (D, F32)], [LANE, D, D], TB)

    gw_mlp_out = _mm_tn("gw_mlp_out_mm", a_b, dy_b, 1).reshape(N_CHIPS, DFF // N_CHIPS, D)
    b_down, tok = _reduce_begin("down", [gw_mlp_out])
    (du_b,) = _mm_nt("d_mlp_out_mm", dy_b, w_mlp_out_f, [BF16],
                     epilogue=lambda acc, rt: (acc * (2.0 * rt.astype(F32)),), extras=(relu_b,), after=(tok,))
    r_down, tok = _reduce_start(b_down, where, du_b)
    gw_mlp_in = _mm_tn("gw_mlp_in_mm", h2_b, du_b, N_CHIPS, after=(tok,))
    b_up, tok = _reduce_begin("up", [gw_mlp_in])
    (dh2,) = _mm_nt("d_mlp_in_mm", du_b, g_mlp_in, [F32], after=(tok,))
    r_up, tok_up = _reduce_start(b_up, where, dh2)

    def mid_bwd_fn(xt, o2t, dh2t, dx2t, gta, gpa, gpm, scm, shm):
        n2, r2 = _rn_parts(o2t)
        n1, r1 = _rn_parts(xt + gta * (n2 * gpa))
        dx1 = dx2t + _rn_bwd(n1, r1, dh2t * (gpm * (1.0 + scm)))
        s1, s2 = _rows(dh2t * n1), _rows(dx1 * n2)
        do2 = _rn_bwd(n2, r2, dx1 * (gta * gpa))
        return (dx1, do2), (gpa * s2, gta * s2, (1.0 + scm) * s1, gpm * s1, _rows(dh2t))

    (dx1, do2_b), (d_gt_a, d_g_post_attn, d_g_pre_mlp, d_sc_m, d_sh_m) = _rowwise(
        "mid_bwd", mid_bwd_fn, [x2d, o2, dh2, dx2], mid_vecs, [(D, F32), (D, BF16)], [D] * 5, TB, after=(tok_up,))

    (d_on,) = _mm_nt("d_out_mm", do2_b, w_out_f, [F32])
    gw_out = _mm_tn("gw_out_mm", on_b, do2_b, 1).reshape(N_CHIPS, w_out.shape[1], D)

    def mix_bwd_fn(oa, ob, dont, ga, gb):
        na, ra = _rn_parts(oa)
        nb, rb = _rn_parts(ob)
        da, db = dont[:, :W_QA], dont[:, W_QA:]
        return (_rn_bwd(na, ra, da * ga), _rn_bwd(nb, rb, db * gb)), (_rows(da * na), _rows(db * nb))

    (doa_b, dob_b), (d_g_out_a, d_g_out_b) = _rowwise(
        "mix_bwd", mix_bwd_fn, [o_a, o_b, d_on], [g_out_a, g_out_b], [(W_QA, BF16), (HB * V_DIM, BF16)],
        [W_QA, HB * V_DIM], TB)

    dqa, dka, dva = _attn_bwd("attn_a_bwd", qa_b, ka_b, va_b, doa_b, o_a, lse_a, HA, HA // HKV, HEAD_DIM, HEAD_DIM,
                              sc_a_, TQ)
    dqcat, dkcat, dvb = _attn_bwd("attn_b_bwd", qcat_b, kcat_b, vb_b, dob_b, o_b, lse_b, HB, 1, 2 * LANE, V_DIM,
                                  sc_b_, TQ)

    d_ckvn, dkv_b, dkpe = _kv_mm_bwd("d_kv_mm", dkcat, dvb, g_kvb)
    gw_kvb = _mm_tn("gw_kv_mm", ckvn_b, dkv_b, N_CHIPS)
    b_kvb, tok_kvb = _reduce_begin("okv", [gw_out, gw_kvb])

    def prep_bwd_fn(pt, dqat, dkat, dvat, dqct, dckt, dkpt, ca_, sla, sha, cb_, slb, shb, gq, gk, gc):
        tabs = ((ca_, sla, sha), (cb_, slb, shb))
        _, vjp = jax.vjp(lambda sg, a1, a2, a3: _prep(sg, tabs, a1, a2, a3), _split_proj(pt), gq, gk, gc)
        (dqa_s, dka_s, dva_s, dqn_s, dqp_s, dck_s, dkp_s), dgq, dgk, dgc = vjp((dqat, dkat, dvat, dqct, dckt, dkpt))
        dproj = jnp.concatenate([*dqa_s, *dka_s, dva_s, *dqn_s, *dqp_s, dck_s, dkp_s], axis=1)
        return (dproj,), (dgq, dgk, dgc)

    (dproj_b,), (d_g_q_a, d_g_k_a, d_g_ckv) = _rowwise(
        "prep_bwd", prep_bwd_fn, [proj, dqa, dka, dva, dqcat, d_ckvn, dkpe, *tabs_a, *tabs_b],
        [g_q_a, g_k_a, g_ckv], [(W_INP, BF16)], [HEAD_DIM, HEAD_DIM, KV_RANK], TB, after=(tok_kvb,))

    r_kvb, tok = _reduce_start(b_kvb, where, dproj_b)
    (dh,) = _mm_nn("d_proj_mm", dproj_b, wpt, [F32], after=(tok,))
    gw_in = _mm_tn_rows("gw_in_mm", dproj_b, h_b, W_IN, _unpermute_pieces()).reshape(N_CHIPS, W_IN // N_CHIPS, D)
    b_in, tok = _reduce_begin("in", [gw_in])

    def pre_attn_bwd_fn(xt, dht, dx1t, g, sc, sh):
        n, r = _rn_parts(xt)
        s = _rows(dht * n)
        return (dx1t + _rn_bwd(n, r, dht * (g * (1.0 + sc))),), ((1.0 + sc) * s, g * s, _rows(dht))

    (grad_x,), (d_g_pre_attn, d_sc_a, d_sh_a) = _rowwise(
        "pre_attn_bwd", pre_attn_bwd_fn, [x2d, dh, dx1], [g_pre_attn, sc_a, sh_a], [(D, F32)], [D] * 3, TB,
        after=(tok,))

    small_names = ["b_ada", "g_pre_attn", "g_q_a", "g_k_a", "g_ckv", "g_out_a", "g_out_b", "g_post_attn",
                   "g_pre_mlp", "g_post_mlp"]
    small_local = jnp.concatenate([d_sh_a, d_sc_a, d_gt_a, d_sh_m, d_sc_m, d_gt_m, d_g_pre_attn, d_g_q_a, d_g_k_a,
                                   d_g_ckv, d_g_out_a, d_g_out_b, d_g_post_attn, d_g_pre_mlp, d_g_post_mlp,
                                   loss_v[:, :8]], axis=1)
    n_sent = small_local.shape[1]
    n_small = n_sent - 8
    small_all = _allgather8("gather_small", small_local.reshape(8, n_sent // 8)).reshape(N_DEV, n_sent)

    def sum8_fn(t):
        return (), (jnp.sum(t, axis=0, keepdims=True),)

    r_in, tok_in = _reduce_start(b_in, where, grad_x, (small_all,))
    _, (sent_sum,) = _rowwise("sum_small", sum8_fn, [small_all], [], [], [n_sent], N_DEV, after=(tok_in,))
    small_sum, loss = sent_sum[:, :n_small], sent_sum[0, n_small]
    dmod_cols = lax.dynamic_slice(small_all, (0, j_me * n_mod), (N_DEV, n_mod))
    dmod_pad = jnp.concatenate([dmod_cols, jnp.zeros_like(dmod_cols)], axis=0).astype(BF16)

    weights = dict(w_ada=w_ada, b_ada=b_ada, g_pre_attn=g_pre_attn, w_in=w_in, g_q_a=g_q_a, g_k_a=g_k_a, g_ckv=g_ckv,
                   w_kv_b=w_kv_b, g_out_a=g_out_a, g_out_b=g_out_b, w_out=w_out, g_post_attn=g_post_attn,
                   g_pre_mlp=g_pre_mlp, w_mlp_in=w_mlp_in, w_mlp_out=w_mlp_out, g_post_mlp=g_post_mlp)
    ms = dict(w_ada=m_w_ada, b_ada=m_b_ada, g_pre_attn=m_g_pre_attn, w_in=m_w_in, g_q_a=m_g_q_a, g_k_a=m_g_k_a,
              g_ckv=m_g_ckv, w_kv_b=m_w_kv_b, g_out_a=m_g_out_a, g_out_b=m_g_out_b, w_out=m_w_out,
              g_post_attn=m_g_post_attn, g_pre_mlp=m_g_pre_mlp, w_mlp_in=m_w_mlp_in, w_mlp_out=m_w_mlp_out,
              g_post_mlp=m_g_post_mlp)
    vs = dict(w_ada=v_w_ada, b_ada=v_b_ada, g_pre_attn=v_g_pre_attn, w_in=v_w_in, g_q_a=v_g_q_a, g_k_a=v_g_k_a,
              g_ckv=v_g_ckv, w_kv_b=v_w_kv_b, g_out_a=v_g_out_a, g_out_b=v_g_out_b, w_out=v_w_out,
              g_post_attn=v_g_post_attn, g_pre_mlp=v_g_pre_mlp, w_mlp_in=v_w_mlp_in, w_mlp_out=v_w_mlp_out,
              g_post_mlp=v_g_post_mlp)
    order = list(weights)
    grads, deltas, new_m, new_v = {}, {}, {}, {}

    def update(name, g, after=()):
        view = (lambda t: jnp.swapaxes(t, 0, 1)) if name == "w_in" else (lambda t: t)
        d_, m_, v_, g_ = _adamw("adamw_" + name, view(weights[name][0]), g, view(ms[name][0]), view(vs[name][0]),
                                after, rewrite_g=True)
        grads[name], deltas[name], new_m[name], new_v[name] = (view(t)[None] for t in (g_, d_, m_, v_))
        return d_

    chain = ((("w_mlp_out",), r_down), (("w_mlp_in",), r_up), (("w_out", "w_kv_b"), r_kvb), (("w_in",), r_in))
    swap, tok = _reduce_finish(chain[0][1], where, small_sum)
    last, m_, v_, g_ = _adamw_outer("adamw_w_ada", w_ada[0], ca_pad, dmod_pad, m_w_ada[0], v_w_ada[0], (tok,))
    grads["w_ada"], deltas["w_ada"], new_m["w_ada"], new_v["w_ada"] = g_[None], last[None], m_[None], v_[None]
    for k, (names, _) in enumerate(chain):
        gs = _reduce_end(swap, last)
        if k + 1 < len(chain):
            swap, tok = _reduce_finish(chain[k + 1][1], where, gs[0])
        for name, g in zip(names, gs):
            last = update(name, g, (tok,))

    cat = lambda d: jnp.concatenate([d[n] for n in small_names], axis=1)
    d_, m_, v_ = _adamw("adamw_small", cat(weights), small_sum, cat(ms), cat(vs))
    off = 0
    for n in small_names:
        w_ = weights[n].shape[1]
        grads[n], deltas[n], new_m[n], new_v[n] = (a[:, off:off + w_] for a in (small_sum, d_, m_, v_))
        off += w_

    return (loss, grad_x[None], *[grads[n] for n in order], *[deltas[n] for n in order],
            *[new_m[n] for n in order], *[new_v[n] for n in order])
```

```python
import functools
import math

import numpy as np
import jax
import jax.numpy as jnp
from jax import lax
from jax.experimental import pallas as pl
from jax.experimental.pallas import tpu as pltpu

F32 = jnp.float32
BF16 = jnp.bfloat16
MESH = pl.DeviceIdType.MESH

EPS = 1e-6
GRID_W = 64
ROPE_THETA = 10000.0
HEAD_DIM = 128
HA = 8
HKV = 2
HB = 8
QK_NOPE = 128
QK_ROPE = 64
V_DIM = 128
KV_RANK = 512
QK_B = QK_NOPE + QK_ROPE
W_QA = HA * HEAD_DIM
W_KA = HKV * HEAD_DIM
W_QB = HB * QK_B
W_IN = W_QA + 2 * W_KA + W_QB + KV_RANK + QK_ROPE
N_CHIPS = 4
N_DEV = 8

LANE = 128
O_QA = 0
O_KA = O_QA + W_QA
O_VA = O_KA + W_KA
O_QN = O_VA + W_KA
O_QP = O_QN + HB * QK_NOPE
O_CKV = O_QP + HB * LANE
O_KPE = O_CKV + KV_RANK
W_INP = O_KPE + LANE

ADAM_LR = 0.001
ADAM_B1 = 0.9
ADAM_B2 = 0.999
ADAM_EPS = 1e-08
ADAM_WD = 0.01
ADAM_STEP = 10

VMEM_LIMIT = 56 * 1024 * 1024
HEADS_PER_STEP = 2


def _params(sem):
    return pltpu.CompilerParams(dimension_semantics=sem, vmem_limit_bytes=VMEM_LIMIT)


def _pick(dim, target, unit):
    best = None
    t = unit
    while t <= min(dim, target):
        if dim % t == 0:
            best = t
        t += unit
    return dim if best is None else best


def _rn(x):
    return x * lax.rsqrt(jnp.mean(x * x, axis=-1, keepdims=True) + EPS)


def _rn_parts(x):
    r = lax.rsqrt(jnp.mean(x * x, axis=-1, keepdims=True) + EPS)
    return x * r, r


def _rn_bwd(n, r, dn):
    return r * (dn - n * jnp.mean(dn * n, axis=-1, keepdims=True))


def _rows(t):
    return jnp.sum(t, axis=0, keepdims=True)


def _rowwise(name, fn, rows, vecs, row_outs, vec_outs, tile, after=()):
    S = rows[0].shape[0]
    assert S % tile == 0
    n_r, n_v, n_ro, n_vo, n_a = len(rows), len(vecs), len(row_outs), len(vec_outs), len(after)

    def body(*refs):
        r_in = refs[:n_r]
        v_in = refs[n_r:n_r + n_v]
        r_out = refs[n_r + n_v + n_a:n_r + n_v + n_a + n_ro]
        v_out = refs[n_r + n_v + n_a + n_ro:]
        ro, vo = fn(*[r[...] for r in r_in], *[v[...] for v in v_in])
        for ref, val in zip(r_out, ro):
            ref[...] = val.astype(ref.dtype)
        if n_vo:
            @pl.when(pl.program_id(0) == 0)
            def _():
                for ref in v_out:
                    ref[...] = jnp.zeros(ref.shape, ref.dtype)
            for ref, val in zip(v_out, vo):
                ref[...] += val

    in_specs = [pl.BlockSpec((tile, a.shape[1]), lambda i: (i, 0)) for a in rows]
    in_specs += [pl.BlockSpec(a.shape, lambda i: (0, 0)) for a in vecs]
    in_specs += [pl.BlockSpec(memory_space=pl.ANY) for _ in after]
    out_specs = [pl.BlockSpec((tile, w), lambda i: (i, 0)) for w, _ in row_outs]
    out_specs += [pl.BlockSpec((1, w), lambda i: (0, 0)) for w in vec_outs]
    out_shape = [jax.ShapeDtypeStruct((S, w), dt) for w, dt in row_outs]
    out_shape += [jax.ShapeDtypeStruct((1, w), F32) for w in vec_outs]
    res = pl.pallas_call(
        body, name=name, grid=(S // tile,), in_specs=in_specs, out_specs=out_specs, out_shape=out_shape,
        compiler_params=_params(("arbitrary",)),
    )(*rows, *vecs, *after)
    return res[:n_ro], res[n_ro:]


CONTRACT_TILE = 4608
MM_TILE_BYTES = 44 * 1024 * 1024


def _rows_that_fit(M, tile_bytes):
    t = _pick(M, 1024, 16)
    while t % 32 == 0 and tile_bytes(t) > MM_TILE_BYTES:
        t //= 2
    return t


def _accumulate(acc, part, step, n_steps, finish):
    if n_steps == 1:
        finish(part)
        return

    @pl.when(step == 0)
    def _():
        acc[...] = part

    @pl.when((step > 0) & (step < n_steps - 1))
    def _():
        acc[...] += part

    @pl.when(step == n_steps - 1)
    def _():
        finish(acc[...] + part)


def _mm_nn(name, a, b3, out_dtypes, epilogue=None, extras=(), after=()):
    M, K = a.shape
    nB, K2, Nsh = b3.shape
    assert K == K2
    N = nB * Nsh
    tn, tk = _pick(Nsh, 1536, LANE), _pick(K, CONTRACT_TILE, LANE)
    out_bytes = sum(jnp.dtype(dt).itemsize for dt in out_dtypes) + sum(e.dtype.itemsize for e in extras)
    tm = _rows_that_fit(M, lambda t: 2 * (t * tk * a.dtype.itemsize + tk * tn * b3.dtype.itemsize)
                        + t * tn * (4 * (K > tk) + 2 * out_bytes))
    nps, nk, ne, no, na = Nsh // tn, K // tk, len(extras), len(out_dtypes), len(after)

    def body(a_ref, b_ref, *rest):
        e_refs, o_refs = rest[:ne], rest[ne + na:ne + na + no]

        def finish(total):
            vals = (total,) if epilogue is None else epilogue(total, *[e[...] for e in e_refs])
            for ref, val in zip(o_refs, vals):
                ref[...] = val.astype(ref.dtype)

        part = jnp.dot(a_ref[...].astype(BF16), b_ref[...].astype(BF16), preferred_element_type=F32)
        _accumulate(rest[-1], part, pl.program_id(2), nk, finish)

    in_specs = [pl.BlockSpec((tm, tk), lambda i, j, k: (i, k)),
                pl.BlockSpec((None, tk, tn), lambda i, j, k: (j // nps, k, j % nps))]
    in_specs += [pl.BlockSpec((tm, tn), lambda i, j, k: (i, j)) for _ in extras]
    in_specs += [pl.BlockSpec(memory_space=pl.ANY) for _ in after]
    return pl.pallas_call(
        body, name=name, grid=(M // tm, N // tn, nk), in_specs=in_specs,
        out_specs=[pl.BlockSpec((tm, tn), lambda i, j, k: (i, j)) for _ in out_dtypes],
        out_shape=[jax.ShapeDtypeStruct((M, N), dt) for dt in out_dtypes],
        scratch_shapes=[pltpu.VMEM((tm, tn), F32)] if nk > 1 else [],
        compiler_params=_params(("parallel", "parallel", "arbitrary")),
    )(a, b3, *extras, *after)


def _mm_nt(name, a, b3, out_dtypes, epilogue=None, extras=(), after=()):
    M, N = a.shape
    nB, K, Nsh = b3.shape
    assert N == nB * Nsh
    to, tn = _pick(K, 1536, LANE), _pick(Nsh, CONTRACT_TILE, LANE)
    out_bytes = sum(jnp.dtype(dt).itemsize for dt in out_dtypes) + sum(e.dtype.itemsize for e in extras)
    tm = _rows_that_fit(M, lambda t: 2 * (t * tn * a.dtype.itemsize + to * tn * b3.dtype.itemsize)
                        + t * to * (4 * (N > tn) + 2 * out_bytes))
    nps, nn, ne, no, na = Nsh // tn, N // tn, len(extras), len(out_dtypes), len(after)

    def body(a_ref, b_ref, *rest):
        e_refs, o_refs = rest[:ne], rest[ne + na:ne + na + no]

        def finish(total):
            vals = (total,) if epilogue is None else epilogue(total, *[e[...] for e in e_refs])
            for ref, val in zip(o_refs, vals):
                ref[...] = val.astype(ref.dtype)

        part = lax.dot_general(a_ref[...].astype(BF16), b_ref[...].astype(BF16),
                               (((1,), (1,)), ((), ())), preferred_element_type=F32)
        _accumulate(rest[-1], part, pl.program_id(2), nn, finish)

    in_specs = [pl.BlockSpec((tm, tn), lambda i, o, n: (i, n)),
                pl.BlockSpec((None, to, tn), lambda i, o, n: (n // nps, o, n % nps))]
    in_specs += [pl.BlockSpec((tm, to), lambda i, o, n: (i, o)) for _ in extras]
    in_specs += [pl.BlockSpec(memory_space=pl.ANY) for _ in after]
    return pl.pallas_call(
        body, name=name, grid=(M // tm, K // to, nn), in_specs=in_specs,
        out_specs=[pl.BlockSpec((tm, to), lambda i, o, n: (i, o)) for _ in out_dtypes],
        out_shape=[jax.ShapeDtypeStruct((M, K), dt) for dt in out_dtypes],
        scratch_shapes=[pltpu.VMEM((tm, to), F32)] if nn > 1 else [],
        compiler_params=_params(("parallel", "parallel", "arbitrary")),
    )(a, b3, *extras, *after)


def _mm_tn(name, a, g, nB, after=()):
    M, K = a.shape
    M2, N = g.shape
    assert M == M2 and N % nB == 0
    Nsh = N // nB
    tk, tn, tm = _pick(K, 2048, LANE), _pick(Nsh, 1024, LANE), _pick(M, CONTRACT_TILE, 16)
    nps, nm = Nsh // tn, M // tm

    def body(a_ref, g_ref, *rest):
        o_ref = rest[len(after)]

        def finish(total):
            o_ref[...] = total

        part = lax.dot_general(a_ref[...].astype(BF16), g_ref[...].astype(BF16),
                               (((0,), (0,)), ((), ())), preferred_element_type=F32)
        _accumulate(rest[-1], part, pl.program_id(2), nm, finish)

    return pl.pallas_call(
        body, name=name, grid=(K // tk, N // tn, nm),
        in_specs=[pl.BlockSpec((tm, tk), lambda k, n, m: (m, k)),
                  pl.BlockSpec((tm, tn), lambda k, n, m: (m, n))] + [pl.BlockSpec(memory_space=pl.ANY) for _ in after],
        out_specs=pl.BlockSpec((None, tk, tn), lambda k, n, m: (n // nps, k, n % nps)),
        out_shape=jax.ShapeDtypeStruct((nB, K, Nsh), F32),
        scratch_shapes=[pltpu.VMEM((tk, tn), F32)] if nm > 1 else [],
        compiler_params=_params(("parallel", "parallel", "arbitrary")),
    )(a, g, *after)


def _rows_mm(name, fn, rows, vecs, b3, after=()):
    S = rows[0].shape[0]
    _, K, N = b3.shape
    tm, tn = _pick(S, 1024, 16), _pick(N, 512, LANE)
    nr, nv, na = len(rows), len(vecs), len(after)

    def body(*refs):
        b_ref = refs[nr + nv]
        a_out, o_ref, a_scr = refs[nr + nv + 1 + na:]

        @pl.when(pl.program_id(1) == 0)
        def _():
            a = fn(*[r[...] for r in refs[:nr]], *[v[...] for v in refs[nr:nr + nv]]).astype(BF16)
            a_scr[...] = a
            a_out[...] = a

        o_ref[...] = jnp.dot(a_scr[...], b_ref[...], preferred_element_type=F32)

    in_specs = [pl.BlockSpec((tm, r.shape[1]), lambda i, j: (i, 0)) for r in rows]
    in_specs += [pl.BlockSpec(v.shape, lambda i, j: (0, 0)) for v in vecs]
    in_specs += [pl.BlockSpec((None, K, tn), lambda i, j: (0, 0, j))]
    in_specs += [pl.BlockSpec(memory_space=pl.ANY) for _ in after]
    return pl.pallas_call(
        body, name=name, grid=(S // tm, N // tn), in_specs=in_specs,
        out_specs=[pl.BlockSpec((tm, K), lambda i, j: (i, 0)), pl.BlockSpec((tm, tn), lambda i, j: (i, j))],
        out_shape=[jax.ShapeDtypeStruct((S, K), BF16), jax.ShapeDtypeStruct((S, N), F32)],
        scratch_shapes=[pltpu.VMEM((tm, K), BF16)],
        compiler_params=_params(("parallel", "arbitrary")),
    )(*rows, *vecs, b3, *after)


def _kv_mm(name, ckvn, w3, kpe):
    S, K = ckvn.shape
    nB, _, Nsh = w3.shape
    heads = Nsh // (2 * LANE)
    tm = _pick(S, 1024, 16)

    def body(a_ref, b_ref, kpe_ref, kc_ref, v_ref):
        acc = jnp.dot(a_ref[...], b_ref[...], preferred_element_type=F32)
        for h in range(heads):
            kc_ref[:, 2 * h * LANE:(2 * h + 1) * LANE] = acc[:, 2 * h * LANE:(2 * h + 1) * LANE].astype(BF16)
            kc_ref[:, (2 * h + 1) * LANE:(2 * h + 2) * LANE] = kpe_ref[...]
            v_ref[:, h * LANE:(h + 1) * LANE] = acc[:, (2 * h + 1) * LANE:(2 * h + 2) * LANE].astype(BF16)

    return pl.pallas_call(
        body, name=name, grid=(S // tm, nB),
        in_specs=[pl.BlockSpec((tm, K), lambda i, j: (i, 0)), pl.BlockSpec((None, K, Nsh), lambda i, j: (j, 0, 0)),
                  pl.BlockSpec((tm, LANE), lambda i, j: (i, 0))],
        out_specs=[pl.BlockSpec((tm, Nsh), lambda i, j: (i, j)), pl.BlockSpec((tm, Nsh // 2), lambda i, j: (i, j))],
        out_shape=[jax.ShapeDtypeStruct((S, nB * Nsh), BF16), jax.ShapeDtypeStruct((S, nB * Nsh // 2), BF16)],
        compiler_params=_params(("parallel", "arbitrary")),
    )(ckvn, w3, kpe)


def _kv_mm_bwd(name, dkc, dv, w3):
    S = dkc.shape[0]
    nB, K, Nsh = w3.shape
    heads = Nsh // (2 * LANE)
    tm = _pick(S, 1024, 16)

    def body(dkc_ref, dv_ref, b_ref, dck_ref, dkv_ref, dkpe_ref):
        j = pl.program_id(1)
        pieces, pe = [], None
        for h in range(heads):
            pieces += [dkc_ref[:, 2 * h * LANE:(2 * h + 1) * LANE], dv_ref[:, h * LANE:(h + 1) * LANE]]
            piece = dkc_ref[:, (2 * h + 1) * LANE:(2 * h + 2) * LANE]
            pe = piece if pe is None else pe + piece
        dkv = jnp.concatenate(pieces, axis=1).astype(BF16)
        dkv_ref[...] = dkv
        part = lax.dot_general(dkv, b_ref[...], (((1,), (1,)), ((), ())), preferred_element_type=F32)

        @pl.when(j == 0)
        def _():
            dck_ref[...] = part
            dkpe_ref[...] = pe

        @pl.when(j > 0)
        def _():
            dck_ref[...] += part
            dkpe_ref[...] += pe

    return pl.pallas_call(
        body, name=name, grid=(S // tm, nB),
        in_specs=[pl.BlockSpec((tm, Nsh), lambda i, j: (i, j)), pl.BlockSpec((tm, Nsh // 2), lambda i, j: (i, j)),
                  pl.BlockSpec((None, K, Nsh), lambda i, j: (j, 0, 0))],
        out_specs=[pl.BlockSpec((tm, K), lambda i, j: (i, 0)), pl.BlockSpec((tm, Nsh), lambda i, j: (i, j)),
                   pl.BlockSpec((tm, LANE), lambda i, j: (i, 0))],
        out_shape=[jax.ShapeDtypeStruct((S, K), F32), jax.ShapeDtypeStruct((S, nB * Nsh), BF16),
                   jax.ShapeDtypeStruct((S, LANE), F32)],
        compiler_params=_params(("parallel", "arbitrary")),
    )(dkc, dv, w3)


def _mm_tn_rows(name, a, g, n_rows, pieces):
    M, K = a.shape
    M2, N = g.shape
    assert M == M2
    tk, tn, tm = _pick(K, 1536, LANE), _pick(N, 512, LANE), _pick(M, CONTRACT_TILE, 16)
    nk, nm = K // tk, M // tm

    def body(a_ref, g_ref, o_ref, *acc):
        kk = pl.program_id(1)

        def finish(total):
            for t in range(nk):
                @pl.when(kk == t)
                def _(t=t):
                    for src, dst, n in pieces:
                        lo, hi = max(src, t * tk), min(src + n, (t + 1) * tk)
                        if lo < hi:
                            o_ref[dst + lo - src:dst + hi - src, :] = total[lo - t * tk:hi - t * tk, :]

        part = lax.dot_general(a_ref[...].astype(BF16), g_ref[...].astype(BF16),
                               (((0,), (0,)), ((), ())), preferred_element_type=F32)
        _accumulate(acc[0] if acc else None, part, pl.program_id(2), nm, finish)

    return pl.pallas_call(
        body, name=name, grid=(N // tn, nk, nm),
        in_specs=[pl.BlockSpec((tm, tk), lambda n, k, m: (m, k)),
                  pl.BlockSpec((tm, tn), lambda n, k, m: (m, n))],
        out_specs=pl.BlockSpec((n_rows, tn), lambda n, k, m: (0, n)),
        out_shape=jax.ShapeDtypeStruct((n_rows, N), F32),
        scratch_shapes=[pltpu.VMEM((tk, tn), F32)] if nm > 1 else [],
        compiler_params=_params(("arbitrary", "arbitrary", "arbitrary")),
    )(a, g)


def _attn_fwd(name, q, k, v, n_heads, group, dqk, dv, scale, tq, after=()):
    S = q.shape[0]
    hp = 2 * HEADS_PER_STEP
    kvp = max(1, hp // group)

    def body(q_ref, k_ref, v_ref, *rest):
        o_ref, lse_ref = rest[-2:]
        for j in range(hp):
            jk = j * kvp // hp
            s = lax.dot_general(q_ref[:, j * dqk:(j + 1) * dqk], k_ref[:, jk * dqk:(jk + 1) * dqk],
                                (((1,), (1,)), ((), ())), preferred_element_type=F32)
            m = jnp.max(s, axis=-1, keepdims=True)
            e = jnp.exp((s - m) * scale)
            l = jnp.sum(e, axis=-1, keepdims=True)
            o_ref[:, j * dv:(j + 1) * dv] = jnp.dot(e.astype(BF16), v_ref[:, jk * dv:(jk + 1) * dv],
                                                    preferred_element_type=F32) * (1.0 / l)
            lse_ref[:, j * LANE:(j + 1) * LANE] = jnp.broadcast_to(m * scale + jnp.log(l), (tq, LANE))

    kv_block = lambda h, i: (0, (h * hp // group) // kvp)
    return pl.pallas_call(
        body, name=name, grid=(n_heads // hp, S // tq),
        in_specs=[pl.BlockSpec((tq, hp * dqk), lambda h, i: (i, h)),
                  pl.BlockSpec((S, kvp * dqk), kv_block),
                  pl.BlockSpec((S, kvp * dv), kv_block)] + [pl.BlockSpec(memory_space=pl.ANY) for _ in after],
        out_specs=[pl.BlockSpec((tq, hp * dv), lambda h, i: (i, h)),
                   pl.BlockSpec((tq, hp * LANE), lambda h, i: (i, h))],
        out_shape=[jax.ShapeDtypeStruct((S, n_heads * dv), F32), jax.ShapeDtypeStruct((S, n_heads * LANE), F32)],
        compiler_params=_params(("arbitrary", "arbitrary")),
    )(q, k, v, *after)


def _attn_bwd(name, q, k, v, do, o, lse, n_heads, group, dqk, dv, scale, tq, after=()):
    S = q.shape[0]
    n_kv = n_heads // group
    hp, tq = (group, tq // 2) if group > HEADS_PER_STEP else (HEADS_PER_STEP, tq)
    kvp = max(1, hp // group)
    chains = 4
    tr = tq * hp // chains

    def body(q_ref, k_ref, v_ref, do_ref, o_ref, lse_ref, *rest):
        dq_ref, dk_ref, dv_ref = rest[len(after):]
        h, i = pl.program_id(0), pl.program_id(1)

        @pl.when(((h * hp) % group == 0) & (i == 0))
        def _():
            dk_ref[...] = jnp.zeros(dk_ref.shape, F32)
            dv_ref[...] = jnp.zeros(dv_ref.shape, F32)

        for j in range(hp):
            jk = j * kvp // hp
            kb, vb = k_ref[:, jk * dqk:(jk + 1) * dqk], v_ref[:, jk * dv:(jk + 1) * dv]
            for r0 in range(0, tq, tr):
                rows = slice(r0, r0 + tr)
                qb, dob = q_ref[rows, j * dqk:(j + 1) * dqk], do_ref[rows, j * dv:(j + 1) * dv]
                s = lax.dot_general(qb, kb, (((1,), (1,)), ((), ())), preferred_element_type=F32)
                p = jnp.exp(s * scale - jnp.tile(lse_ref[rows, j * LANE:(j + 1) * LANE], (1, S // LANE)))
                dp = lax.dot_general(dob, vb, (((1,), (1,)), ((), ())), preferred_element_type=F32)
                delta = jnp.sum(dob.astype(F32) * o_ref[rows, j * dv:(j + 1) * dv], axis=-1, keepdims=True)
                ds = (p * ((dp - delta) * scale)).astype(BF16)
                dv_ref[:, jk * dv:(jk + 1) * dv] += lax.dot_general(p.astype(BF16), dob, (((0,), (0,)), ((), ())),
                                                                    preferred_element_type=F32)
                dq_ref[rows, j * dqk:(j + 1) * dqk] = jnp.dot(ds, kb, preferred_element_type=F32)
                dk_ref[:, jk * dqk:(jk + 1) * dqk] += lax.dot_general(ds, qb, (((0,), (0,)), ((), ())),
                                                                      preferred_element_type=F32)

    kv_block = lambda h, i: (0, (h * hp // group) // kvp)
    return pl.pallas_call(
        body, name=name, grid=(n_heads // hp, S // tq),
        in_specs=[pl.BlockSpec((tq, hp * dqk), lambda h, i: (i, h)),
                  pl.BlockSpec((S, kvp * dqk), kv_block),
                  pl.BlockSpec((S, kvp * dv), kv_block),
                  pl.BlockSpec((tq, hp * dv), lambda h, i: (i, h)),
                  pl.BlockSpec((tq, hp * dv), lambda h, i: (i, h)),
                  pl.BlockSpec((tq, hp * LANE), lambda h, i: (i, h))] + [pl.BlockSpec(memory_space=pl.ANY) for _ in after],
        out_specs=[pl.BlockSpec((tq, hp * dqk), lambda h, i: (i, h)),
                   pl.BlockSpec((S, kvp * dqk), kv_block),
                   pl.BlockSpec((S, kvp * dv), kv_block)],
        out_shape=[jax.ShapeDtypeStruct((S, n_heads * dqk), F32),
                   jax.ShapeDtypeStruct((S, n_kv * dqk), F32),
                   jax.ShapeDtypeStruct((S, n_kv * dv), F32)],
        compiler_params=_params(("arbitrary", "arbitrary")),
    )(q, k, v, do, o, lse, *after)


def _adamw_step(w, gg, m, v):
    nm = ADAM_B1 * m + (1.0 - ADAM_B1) * gg
    nv = ADAM_B2 * v + (1.0 - ADAM_B2) * (gg * gg)
    c1, c2 = 1.0 - ADAM_B1 ** ADAM_STEP, 1.0 - ADAM_B2 ** ADAM_STEP
    return -ADAM_LR * ((nm / c1) / (jnp.sqrt(nv / c2) + ADAM_EPS) + ADAM_WD * w), nm, nv


def _adamw(name, w, g, m, v, after=(), rewrite_g=False):
    R, C = w.shape
    tile = _pick(R, max(8, (1 << 19) // C // 8 * 8), 8)
    n_out = 4 if rewrite_g else 3

    def body(w_ref, g_ref, m_ref, v_ref, *rest):
        d_ref, nm_ref, nv_ref = rest[len(after):len(after) + 3]
        gg = g_ref[...]
        d_ref[...], nm_ref[...], nv_ref[...] = _adamw_step(w_ref[...], gg, m_ref[...], v_ref[...])
        if rewrite_g:
            rest[-1][...] = gg

    spec = pl.BlockSpec((tile, C), lambda i: (i, 0))
    return pl.pallas_call(
        body, name=name, grid=(R // tile,), in_specs=[spec] * 4 + [pl.BlockSpec(memory_space=pl.ANY) for _ in after],
        out_specs=[spec] * n_out, out_shape=[jax.ShapeDtypeStruct((R, C), F32)] * n_out,
        compiler_params=_params(("parallel",)),
    )(w, g, m, v, *after)


def _adamw_outer(name, w, left, right, m, v, after=()):
    R, C = w.shape
    Kc = left.shape[0]
    tile = _pick(R, max(LANE, (1 << 19) // C // LANE * LANE), LANE)

    def body(w_ref, l_ref, r_ref, m_ref, v_ref, *rest):
        d_ref, nm_ref, nv_ref, g_ref = rest[len(after):]
        gg = lax.dot_general(l_ref[...], r_ref[...], (((0,), (0,)), ((), ())), preferred_element_type=F32)
        d_ref[...], nm_ref[...], nv_ref[...] = _adamw_step(w_ref[...], gg, m_ref[...], v_ref[...])
        g_ref[...] = gg

    spec = pl.BlockSpec((tile, C), lambda i: (i, 0))
    return pl.pallas_call(
        body, name=name, grid=(R // tile,),
        in_specs=[spec, pl.BlockSpec((Kc, tile), lambda i: (0, i)), pl.BlockSpec((Kc, C), lambda i: (0, 0)), spec, spec]
        + [pl.BlockSpec(memory_space=pl.ANY) for _ in after],
        out_specs=[spec] * 4, out_shape=[jax.ShapeDtypeStruct((R, C), F32)] * 4,
        compiler_params=_params(("parallel",)),
    )(w, left, right, m, v, *after)


def _flip(me, rel):
    return tuple(1 - p if r else p for p, r in zip(me, rel))


def _push(name, ins, out_shapes, transfers, aliases=None):
    ni, no, nt = len(ins), len(out_shapes), len(transfers)

    def body(*refs):
        in_refs, out_refs = refs[:ni], refs[ni:ni + no]
        send_sems, recv_sems = refs[ni + no], refs[ni + no + 1]
        me = (lax.axis_index("x"), lax.axis_index("y"), lax.axis_index("c"))
        copies = []
        for t, (si, sfn, oi, dfn, rel) in enumerate(transfers):
            src_ref = out_refs[si[1]] if isinstance(si, tuple) else in_refs[si]
            src = src_ref.at[sfn(*me)]
            dst = out_refs[oi].at[dfn(*me)]
            if rel is None:
                cp = pltpu.make_async_copy(src, dst, send_sems.at[t])
            else:
                cp = pltpu.make_async_remote_copy(src_ref=src, dst_ref=dst, send_sem=send_sems.at[t],
                                                  recv_sem=recv_sems.at[t], device_id=_flip(me, rel),
                                                  device_id_type=MESH)
            cp.start()
            copies.append(cp)
        for cp in copies:
            cp.wait()

    any_spec = pl.BlockSpec(memory_space=pl.ANY)
    return pl.pallas_call(
        body, name=name, in_specs=[any_spec] * ni, out_specs=[any_spec] * no, out_shape=out_shapes,
        scratch_shapes=[pltpu.SemaphoreType.DMA((nt,)), pltpu.SemaphoreType.DMA((nt,))],
        input_output_aliases=aliases or {},
        compiler_params=pltpu.CompilerParams(has_side_effects=True),
    )(*ins)


_HBM = pl.BlockSpec(memory_space=pltpu.HBM)
_SEM = pl.BlockSpec(memory_space=pltpu.SEMAPHORE)
_DATAFLOW = pltpu.SideEffectType.DATAFLOW_SIDE_EFFECTING


def _split_copies(refs, transfers, send_sems, recv_sems):
    me = (lax.axis_index("x"), lax.axis_index("y"), lax.axis_index("c"))
    return [pltpu.make_async_remote_copy(src_ref=refs[sb].at[sfn(*me)], dst_ref=refs[db].at[dfn(*me)],
                                         send_sem=send_sems.at[t], recv_sem=recv_sems.at[t],
                                         device_id=_flip(me, rel), device_id_type=MESH)
            for t, (sb, sfn, db, dfn, rel) in enumerate(transfers)]


def _push_start(name, bufs, transfers, after=()):
    nb, na, nt = len(bufs), len(after), len(transfers)

    def body(*refs):
        send_sems, recv_sems, token = refs[nb + na], refs[nb + na + 1], refs[-1]
        for cp in _split_copies(refs[nb + na + 2:2 * nb + na + 2], transfers, send_sems, recv_sems):
            cp.start()
        token[...] = jnp.zeros(token.shape, token.dtype)

    res = pl.pallas_call(
        body, name=name,
        out_shape=(pltpu.SemaphoreType.DMA((nt,)), pltpu.SemaphoreType.DMA((nt,)),
                   *[pltpu.HBM(b.shape, b.dtype) for b in bufs], jax.ShapeDtypeStruct((8, LANE), F32)),
        in_specs=[_HBM] * nb + [pl.BlockSpec(memory_space=pl.ANY)] * na,
        out_specs=(_SEM, _SEM, *[_HBM] * nb, pl.BlockSpec(memory_space=pltpu.VMEM)),
        input_output_aliases={i: i + 2 for i in range(nb)},
        compiler_params=pltpu.CompilerParams(has_side_effects=_DATAFLOW),
    )(*[pltpu.with_memory_space_constraint(b, pltpu.HBM) for b in bufs], *after)
    return res[0], res[1], list(res[2:2 + nb]), res[-1]


def _push_wait(name, send_sems, recv_sems, bufs, transfers, after):
    nb = len(bufs)

    def body(*refs):
        for cp in _split_copies(refs[:nb], transfers, refs[nb], refs[nb + 1]):
            cp.wait_send()
            cp.wait_recv()

    res = pl.pallas_call(
        body, name=name, out_shape=[pltpu.HBM(b.shape, b.dtype) for b in bufs],
        in_specs=[_HBM] * nb + [_SEM, _SEM, pl.BlockSpec(memory_space=pl.ANY)], out_specs=[_HBM] * nb,
        input_output_aliases={i: i for i in range(nb)},
        compiler_params=pltpu.CompilerParams(has_side_effects=_DATAFLOW),
    )(*bufs, send_sems, recv_sems, after)
    return list(res)


ICI_RELS = ((1, 0, 0), (0, 1, 0), (1, 1, 0))
SIBLING = (0, 0, 1)


def _chip(x, y):
    return 2 * x + y


def _cast_into_slot(name, where, w, after=()):
    R, C = w.shape
    tile = _pick(R, 512, BF16_ROWS)

    def body(where_ref, w_ref, *rest):
        rest[-1][...] = w_ref[...].astype(BF16)

    return pl.pallas_call(
        body, name=name,
        grid_spec=pltpu.PrefetchScalarGridSpec(
            num_scalar_prefetch=1, grid=(R // tile,),
            in_specs=[pl.BlockSpec((tile, C), lambda i, wh: (i, 0))] + [pl.BlockSpec(memory_space=pl.ANY) for _ in after],
            out_specs=pl.BlockSpec((None, tile, C), lambda i, wh: (wh[1], i, 0))),
        out_shape=jax.ShapeDtypeStruct((N_CHIPS, R, C), BF16),
        compiler_params=_params(("parallel",)),
    )(where, w, *after)


BF16_ROWS = 16


def _rows_split(shape2):
    return (shape2[0] // 2) % BF16_ROWS == 0


def _half_shape(shape2):
    R, C = shape2
    return (R // 2, C) if _rows_split(shape2) else (R, C // 2)


def _half(shape2, c):
    R, C = shape2
    if _rows_split(shape2):
        return (pl.ds(c * (R // 2), R // 2), slice(None))
    return (slice(None), pl.ds(pl.multiple_of(c * (C // 2), LANE), C // 2))


def _gather_casts(tag, where, shards, after=()):
    return [_cast_into_slot(f"cast_w_{tag}{w}", where, s, after) for w, s in enumerate(shards)]


def _gather_start(tag, own, after):
    transfers = []
    for w in range(len(own)):
        def idx(x, y, c, shape2=own[w].shape[1:]):
            return (_chip(x, y), *_half(shape2, c))
        transfers += [(w, idx, w, idx, rel) for rel in ICI_RELS]
    send_sems, recv_sems, bufs, token = _push_start(f"gather_{tag}_start", own, transfers, after)
    return (tag, send_sems, recv_sems, bufs, transfers), token


def _forward_transfers(part):
    second = []
    for w, p in enumerate(part):
        for rel in ICI_RELS:
            def idx(x, y, c, shape2=p.shape[1:], rel=rel):
                return (_chip(x ^ rel[0], y ^ rel[1]), *_half(shape2, c))
            second.append((w, idx, w, idx, SIBLING))
    return second


def _gather_finish(handle, after):
    tag, send_sems, recv_sems, bufs, transfers = handle
    part = _push_wait(f"gather_{tag}_wait", send_sems, recv_sems, bufs, transfers, after)
    outs = [jax.ShapeDtypeStruct(p.shape, p.dtype) for p in part]
    second = [(("out", sb), sfn, db, dfn, rel) for sb, sfn, db, dfn, rel in _forward_transfers(part)]
    return _push(f"gather_{tag}_d2d", list(part), outs, second, aliases={w: w for w in range(len(part))})


def _gather_mid(handle, after):
    tag, send_sems, recv_sems, bufs, transfers = handle
    part = _push_wait(f"gather_{tag}_wait", send_sems, recv_sems, bufs, transfers, after)
    second = _forward_transfers(part)
    s2, r2, bufs2, token = _push_start(f"gather_{tag}_d2d_start", part, second)
    return (tag, s2, r2, bufs2, second), token


def _gather_end(handle, after):
    tag, send_sems, recv_sems, bufs, transfers = handle
    return _push_wait(f"gather_{tag}_d2d_wait", send_sems, recv_sems, bufs, transfers, after)


def _reduce_begin(tag, grads, after=()):
    n = len(grads)
    land = [lax.empty((N_CHIPS, *_half_shape(g.shape[1:])), F32) for g in grads]
    t1 = []
    for w in range(n):
        t1.append((w, lambda x, y, c, shape2=grads[w].shape[1:]: (slice(None), *_half(shape2, 1 - c)), n + w,
                   lambda x, y, c: (slice(None), slice(None), slice(None)), SIBLING))
    send_sems, recv_sems, bufs, token = _push_start(f"reduce_{tag}_d2d_start", list(grads) + land, t1, after)
    return (tag, send_sems, recv_sems, bufs, t1), token


def _reduce_start(handle, where, after, after_start=()):
    tag, send_sems, recv_sems, bufs, t1 = handle
    n = len(bufs) // 2
    res = _push_wait(f"reduce_{tag}_d2d_wait", send_sems, recv_sems, bufs, t1, after)
    grads, from_sib = res[:n], res[n:]
    pair = [_pair_sum(f"pair_sum_{tag}{w}", where, grads[w], from_sib[w]) for w in range(n)]
    land = [lax.empty((3, *_half_shape(g.shape[1:])), BF16) for g in grads]
    t3 = []
    for w in range(n):
        for r, rel in enumerate(ICI_RELS):
            t3.append((w, lambda x, y, c, rel=rel: (_chip(x ^ rel[0], y ^ rel[1]), slice(None), slice(None)), n + w,
                       lambda x, y, c, r=r: (r, slice(None), slice(None)), rel))
    send_sems, recv_sems, bufs, token = _push_start(f"reduce_{tag}_start", pair + land, t3, after_start)
    return (tag, send_sems, recv_sems, bufs, t3, list(grads), list(from_sib)), token


def _reduce_finish(handle, where, after):
    tag, send_sems, recv_sems, bufs, t3, grads, from_sib = handle
    n = len(grads)
    from_chips = _push_wait(f"reduce_{tag}_wait", send_sems, recv_sems, bufs, t3, after)[n:]
    mine = [_chip_sum(f"chip_sum_{tag}{w}", where, grads[w], from_sib[w], from_chips[w]) for w in range(n)]
    t5 = []
    for w in range(n):
        def idx(x, y, c, shape2=grads[w].shape[1:]):
            return _half(shape2, c)
        t5.append((w, idx, w, idx, SIBLING))
    send_sems, recv_sems, bufs, token = _push_start(f"reduce_{tag}_swap_start", mine, t5)
    return (tag, send_sems, recv_sems, bufs, t5), token


def _reduce_end(handle, after):
    tag, send_sems, recv_sems, bufs, t5 = handle
    return _push_wait(f"reduce_{tag}_swap_wait", send_sems, recv_sems, bufs, t5, after)


def _pair_sum(name, where, g, from_sib):
    _, hr, hc = from_sib.shape
    tile = _pick(hr, max(16, (1 << 19) // hc // 16 * 16), 16)
    nb = hr // tile
    slot = lambda k, wh: (wh[1] + 1 + k) % N_CHIPS
    if _rows_split(g.shape[1:]):
        mine = lambda k, i, wh: (slot(k, wh), wh[0] * nb + i, 0)
    else:
        mine = lambda k, i, wh: (slot(k, wh), i, wh[0])

    def body(where_ref, g_ref, s_ref, o_ref):
        o_ref[...] = (g_ref[...] + s_ref[...]).astype(BF16)

    return pl.pallas_call(
        body, name=name,
        grid_spec=pltpu.PrefetchScalarGridSpec(
            num_scalar_prefetch=1, grid=(N_CHIPS - 1, nb),
            in_specs=[pl.BlockSpec((None, tile, hc), mine),
                      pl.BlockSpec((None, tile, hc), lambda k, i, wh: (slot(k, wh), i, 0))],
            out_specs=pl.BlockSpec((None, tile, hc), lambda k, i, wh: (slot(k, wh), i, 0))),
        out_shape=jax.ShapeDtypeStruct((N_CHIPS, hr, hc), BF16),
        compiler_params=_params(("parallel", "parallel")),
    )(where, g, from_sib)


def _chip_sum(name, where, g, from_sib, from_chips):
    _, R, C = g.shape
    _, hr, hc = from_sib.shape
    tile = _pick(hr, max(16, (1 << 19) // hc // 16 * 16), 16)
    nb = hr // tile
    if _rows_split((R, C)):
        mine = lambda i, wh: (wh[0] * nb + i, 0)
    else:
        mine = lambda i, wh: (i, wh[0])

    def body(where_ref, g_ref, s_ref, r_ref, o_ref):
        acc = g_ref[...] + s_ref[...]
        for r in range(3):
            acc = acc + r_ref[r].astype(F32)
        o_ref[...] = acc

    return pl.pallas_call(
        body, name=name,
        grid_spec=pltpu.PrefetchScalarGridSpec(
            num_scalar_prefetch=1, grid=(nb,),
            in_specs=[pl.BlockSpec((None, tile, hc), lambda i, wh: (wh[1], *mine(i, wh))),
                      pl.BlockSpec((None, tile, hc), lambda i, wh: (wh[1], i, 0)),
                      pl.BlockSpec((3, tile, hc), lambda i, wh: (0, i, 0))],
            out_specs=pl.BlockSpec((tile, hc), mine)),
        out_shape=jax.ShapeDtypeStruct((R, C), F32),
        compiler_params=_params(("parallel",)),
    )(where, g, from_sib, from_chips)


def _allgather8(name, blk, after=()):
    m_per, n = blk.shape
    rels = [(0, 0, 1), (1, 0, 0), (0, 1, 0), (1, 1, 0), (1, 0, 1), (0, 1, 1), (1, 1, 1)]

    def body(x_ref, *rest):
        out_ref, send_sems, recv_sems = rest[len(after):]
        me = (lax.axis_index("x"), lax.axis_index("y"), lax.axis_index("c"))
        my_rows = out_ref.at[pl.ds((4 * me[0] + 2 * me[1] + me[2]) * m_per, m_per), :]
        out_ref[pl.ds((4 * me[0] + 2 * me[1] + me[2]) * m_per, m_per), :] = x_ref[...]
        copies = []
        for t, rel in enumerate(rels):
            cp = pltpu.make_async_remote_copy(src_ref=x_ref, dst_ref=my_rows, send_sem=send_sems.at[t],
                                              recv_sem=recv_sems.at[t], device_id=_flip(me, rel),
                                              device_id_type=MESH)
            cp.start()
            copies.append(cp)
        for cp in copies:
            cp.wait()

    return pl.pallas_call(
        body, name=name, out_shape=jax.ShapeDtypeStruct((N_DEV * m_per, n), blk.dtype),
        in_specs=[pl.BlockSpec(memory_space=pltpu.VMEM)] + [pl.BlockSpec(memory_space=pl.ANY) for _ in after],
        out_specs=pl.BlockSpec(memory_space=pltpu.VMEM),
        scratch_shapes=[pltpu.SemaphoreType.DMA((7,)), pltpu.SemaphoreType.DMA((7,))],
        compiler_params=pltpu.CompilerParams(has_side_effects=True, vmem_limit_bytes=VMEM_LIMIT),
    )(blk, *after)


def _rope_tables(seq_len, dim, reps):
    rows = seq_len // GRID_W
    t = np.arange(seq_len)
    row, col = (t // GRID_W).astype(np.float32), (t % GRID_W).astype(np.float32)
    half = dim // 2
    inv = jnp.asarray(ROPE_THETA, F32) ** (-jnp.arange(0, half, 2, dtype=F32) / half)
    ang_r = jnp.asarray(row)[:, None] * inv[None, :]
    ang_c = jnp.asarray(col)[:, None] * inv[None, :]
    ang = jnp.concatenate([ang_r, ang_r, ang_c, ang_c], axis=-1)
    cos, sin = jnp.cos(ang), jnp.sin(ang)
    low = (np.arange(dim) % (dim // 2)) < (dim // 4)
    s_lo = jnp.where(jnp.asarray(low)[None, :], -sin, 0.0)
    s_hi = jnp.where(jnp.asarray(low)[None, :], 0.0, sin)
    del rows
    return tuple(jnp.tile(a, (1, reps)) for a in (cos, s_lo, s_hi))


def _make_rope(shift):
    def up(v):
        return pltpu.roll(v, LANE - shift, 1)

    def down(v):
        return pltpu.roll(v, shift, 1)

    @jax.custom_vjp
    def rope(v, cos, s_lo, s_hi):
        return v * cos + up(v) * s_lo + down(v) * s_hi

    def fwd(v, cos, s_lo, s_hi):
        return rope(v, cos, s_lo, s_hi), (cos, s_lo, s_hi)

    def bwd(res, dy):
        cos, s_lo, s_hi = res
        return dy * cos + down(dy * s_lo) + up(dy * s_hi), jnp.zeros_like(cos), jnp.zeros_like(cos), jnp.zeros_like(cos)

    rope.defvjp(fwd, bwd)
    return rope


_rope_a = _make_rope(HEAD_DIM // 4)
_rope_b = _make_rope(QK_ROPE // 4)


def _prep(segs, tabs, g_q, g_k, g_ckv):
    qa, ka, va, qn, qp, ckv, kpe = segs
    ta, tb = tabs
    q_a = jnp.concatenate([_rope_a(_rn(s) * g_q, *ta) for s in qa], axis=1)
    k_a = jnp.concatenate([_rope_a(_rn(s) * g_k, *ta) for s in ka], axis=1)
    cat = []
    for h in range(HB):
        cat += [qn[h], _rope_b(qp[h], *tb)]
    q_cat = jnp.concatenate(cat, axis=1)
    return q_a, k_a, va, q_cat, _rn(ckv) * g_ckv, _rope_b(kpe, *tb)


def _split_proj(proj):
    qa = [proj[:, O_QA + h * LANE:O_QA + (h + 1) * LANE] for h in range(HA)]
    ka = [proj[:, O_KA + h * LANE:O_KA + (h + 1) * LANE] for h in range(HKV)]
    va = proj[:, O_VA:O_QN]
    qn = [proj[:, O_QN + h * LANE:O_QN + (h + 1) * LANE] for h in range(HB)]
    qp = [proj[:, O_QP + h * LANE:O_QP + (h + 1) * LANE] for h in range(HB)]
    return qa, ka, va, qn, qp, proj[:, O_CKV:O_KPE], proj[:, O_KPE:W_INP]


def _permute_w_in(nat):
    K = nat.shape[1]
    qb = nat[W_QA + 2 * W_KA:W_QA + 2 * W_KA + W_QB].reshape(HB, QK_B, K)
    qp = jnp.concatenate([qb[:, QK_NOPE:], jnp.zeros((HB, LANE - QK_ROPE, K), nat.dtype)], axis=1)
    tail = nat[W_QA + 2 * W_KA + W_QB:]
    return jnp.concatenate([nat[:O_QN], qb[:, :QK_NOPE].reshape(HB * QK_NOPE, K), qp.reshape(HB * LANE, K),
                            tail, jnp.zeros((LANE - QK_ROPE, K), nat.dtype)], axis=0)


def _unpermute_pieces():
    pieces = [(0, 0, O_QN)]
    for h in range(HB):
        pieces.append((O_QN + h * QK_NOPE, O_QN + h * QK_B, QK_NOPE))
        pieces.append((O_QP + h * LANE, O_QN + h * QK_B + QK_NOPE, QK_ROPE))
    pieces.append((O_CKV, O_QN + W_QB, KV_RANK))
    pieces.append((O_KPE, O_QN + W_QB + KV_RANK, QK_ROPE))
    return pieces


def kernel(x, c, w_ada, b_ada, g_pre_attn, w_in, g_q_a, g_k_a, g_ckv, w_kv_b, g_out_a, g_out_b, w_out, g_post_attn, g_pre_mlp, w_mlp_in, w_mlp_out, g_post_mlp, loss_target, m_w_ada, m_b_ada, m_g_pre_attn, m_w_in, m_g_q_a, m_g_k_a, m_g_ckv, m_w_kv_b, m_g_out_a, m_g_out_b, m_w_out, m_g_post_attn, m_g_pre_mlp, m_w_mlp_in, m_w_mlp_out, m_g_post_mlp, v_w_ada, v_b_ada, v_g_pre_attn, v_w_in, v_g_q_a, v_g_k_a, v_g_ckv, v_w_kv_b, v_g_out_a, v_g_out_b, v_w_out, v_g_post_attn, v_g_pre_mlp, v_w_mlp_in, v_w_mlp_out, v_g_post_mlp):
    S, D = x.shape[1], x.shape[2]
    x2d, tgt = x[0], loss_target[0]
    ix, iy, ic = lax.axis_index("x"), lax.axis_index("y"), lax.axis_index("c")
    j_me = _chip(ix, iy)
    e_me = 4 * ix + 2 * iy + ic
    T = _pick(S, 512, 8)
    TB = _pick(S, 256, 8)

    where = jnp.stack([ic, j_me]).astype(jnp.int32)
    DFF = w_mlp_out.shape[1] * N_CHIPS
    tabs_a = _rope_tables(S, HEAD_DIM, 1)
    tabs_b = _rope_tables(S, QK_ROPE, LANE // QK_ROPE)

    c_all = _allgather8("gather_c", c.reshape(8, D // 8)).reshape(N_DEV, D)
    w_in_t = jnp.swapaxes(w_in[0], 0, 1)
    h_qkv, tok_a = _gather_start("qkv", _gather_casts("qkv", where, [w_in_t, w_kv_b[0]]), (c_all,))
    later = [_gather_casts(tag, where, [w[0]], (tok_a,))
             for tag, w in (("out", w_out), ("up", w_mlp_in), ("down", w_mlp_out))]

    def silu_fn(cc):
        return (cc * (1.0 / (1.0 + jnp.exp(-cc))),), ()

    (c_act,), _ = _rowwise("silu_c", silu_fn, [c_all], [], [(D, BF16)], [], N_DEV)
    ca_pad = jnp.concatenate([c_act, jnp.zeros_like(c_act)], axis=0)
    n_mod = w_ada.shape[2]
    b_cols = lax.dynamic_slice(b_ada, (0, j_me * n_mod), (1, n_mod))
    (mod_part,) = _mm_nn("mod_mm", ca_pad, w_ada, [F32], epilogue=lambda acc, b: (acc + b,),
                         extras=(jnp.broadcast_to(b_cols, (16, n_mod)),))
    mod_all = _allgather8("gather_mod", mod_part[:8] + tok_a[0, 0],
                          after=[c_[0] for c_ in later] + [*tabs_a, *tabs_b]).reshape(N_DEV, 8, n_mod)
    mod = jnp.concatenate([lax.dynamic_slice(mod_all, (2 * j, e_me, 0), (1, 1, n_mod))[0] for j in range(N_CHIPS)],
                          axis=1)

    m_qkv, tok_q = _gather_mid(h_qkv, mod_all)
    h_out, tok_b = _gather_start("out", later[0], (tok_q,))
    h_up, tok_c = _gather_start("up", later[1], (tok_b,))
    h_down, tok_d = _gather_start("down", later[2], (tok_c,))
    mod = mod + tok_d[0, 0]
    sh_a, sc_a, gt_a, sh_m, sc_m, gt_m = [mod[:, i * D:(i + 1) * D] for i in range(6)]

    def pre_attn_fn(xt, g, sc, sh):
        return (_rn(xt) * g * (1.0 + sc) + sh,), ()

    (h_b,), _ = _rowwise("pre_attn", pre_attn_fn, [x2d], [g_pre_attn, sc_a, sh_a], [(D, BF16)], [], T)
    g_in, g_kvb = _gather_end(m_qkv, h_b)
    wpt = _permute_w_in(g_in.reshape(W_IN, D))[None]
    (proj,) = _mm_nt("proj_mm", h_b, wpt, [F32])

    def prep_fn(pt, ca_, sla, sha, cb_, slb, shb, gq, gk, gc):
        return _prep(_split_proj(pt), ((ca_, sla, sha), (cb_, slb, shb)), gq, gk, gc), ()

    (qa_b, ka_b, va_b, qcat_b, ckvn_b, kpe_b), _ = _rowwise(
        "prep", prep_fn, [proj, *tabs_a, *tabs_b], [g_q_a, g_k_a, g_ckv],
        [(W_QA, BF16), (W_KA, BF16), (W_KA, BF16), (2 * HB * LANE, BF16), (KV_RANK, BF16), (LANE, BF16)], [], T)
    kcat_b, vb_b = _kv_mm("kv_mm", ckvn_b, g_kvb, kpe_b)

    TQ = _pick(S, 512, 16)
    sc_a_ = 1.0 / math.sqrt(HEAD_DIM)
    sc_b_ = 1.0 / math.sqrt(QK_B)
    o_a, lse_a = _attn_fwd("attn_a_fwd", qa_b, ka_b, va_b, HA, HA // HKV, HEAD_DIM, HEAD_DIM, sc_a_, TQ)
    m_out, tok_m = _gather_mid(h_out, o_a)
    o_b, lse_b = _attn_fwd("attn_b_fwd", qcat_b, kcat_b, vb_b, HB, 1, 2 * LANE, V_DIM, sc_b_, TQ, after=(tok_m,))
    m_up, tok_m = _gather_mid(h_up, o_b)

    def mix_fn(oa, ob, ga, gb):
        return jnp.concatenate([_rn(oa) * ga, _rn(ob) * gb], axis=1)

    (g_wout,) = _gather_end(m_out, o_b)
    w_out_f = g_wout.reshape(1, N_CHIPS * g_wout.shape[1], D)
    on_b, o2 = _rows_mm("out_mm", mix_fn, [o_a, o_b], [g_out_a, g_out_b], w_out_f, after=(tok_m,))

    def mid(xt, o2t, gta, gpa, gpm, scm, shm):
        x1 = xt + gta * (_rn(o2t) * gpa)
        return x1, _rn(x1) * gpm * (1.0 + scm) + shm

    def mid_fn(*a):
        return mid(*a), ()

    mid_vecs = [gt_a, g_post_attn, g_pre_mlp, sc_m, sh_m]
    (x1, h2_b), _ = _rowwise("mid", mid_fn, [x2d, o2], mid_vecs, [(D, F32), (D, BF16)], [], T)
    (g_mlp_in,) = _gather_end(m_up, h2_b)
    relu_b, a_b = _mm_nn("mlp_in_mm", h2_b, g_mlp_in, [BF16, BF16],
                         epilogue=lambda acc: (jnp.maximum(acc, 0.0), jnp.square(jnp.maximum(acc, 0.0))))
    (g_mlp_out,) = _gather_finish(h_down, a_b)
    w_mlp_out_f = g_mlp_out.reshape(1, DFF, D)
    (y,) = _mm_nn("mlp_out_mm", a_b, w_mlp_out_f, [F32])

    def last_fn(yt, x1t, tt, gtm, gpo):
        n, r = _rn_parts(yt)
        err = x1t + gtm * (n * gpo) - tt
        dx2 = err * (1.0 / D)
        loss = 0.5 * jnp.sum(jnp.mean(err * err, axis=-1))
        s = _rows(dx2 * n)
        return (_rn_bwd(n, r, dx2 * (gtm * gpo)), dx2), (jnp.full((1, LANE), loss, F32), gpo * s, gtm * s)

    (dy_b, dx2), (loss_v, d_gt_m, d_g_post_mlp) = _rowwise(
        "loss_bwd", last_fn, [y, x1, tgt], [gt_m, g_post_mlp], [(D, BF16), (D, F32)], [LANE, D, D], TB)

    gw_mlp_out = _mm_tn("gw_mlp_out_mm", a_b, dy_b, 1).reshape(N_CHIPS, DFF // N_CHIPS, D)
    b_down, tok = _reduce_begin("down", [gw_mlp_out])
    (du_b,) = _mm_nt("d_mlp_out_mm", dy_b, w_mlp_out_f, [BF16],
                     epilogue=lambda acc, rt: (acc * (2.0 * rt.astype(F32)),), extras=(relu_b,), after=(tok,))
    r_down, tok = _reduce_start(b_down, where, du_b)
    gw_mlp_in = _mm_tn("gw_mlp_in_mm", h2_b, du_b, N_CHIPS, after=(tok,))
    b_up, tok = _reduce_begin("up", [gw_mlp_in])
    (dh2,) = _mm_nt("d_mlp_in_mm", du_b, g_mlp_in, [F32], after=(tok,))
    r_up, tok_up = _reduce_start(b_up, where, dh2)

    def mid_bwd_fn(xt, o2t, dh2t, dx2t, gta, gpa, gpm, scm, shm):
        n2, r2 = _rn_parts(o2t)
        n1, r1 = _rn_parts(xt + gta * (n2 * gpa))
        dx1 = dx2t + _rn_bwd(n1, r1, dh2t * (gpm * (1.0 + scm)))
        s1, s2 = _rows(dh2t * n1), _rows(dx1 * n2)
        do2 = _rn_bwd(n2, r2, dx1 * (gta * gpa))
        return (dx1, do2), (gpa * s2, gta * s2, (1.0 + scm) * s1, gpm * s1, _rows(dh2t))

    (dx1, do2_b), (d_gt_a, d_g_post_attn, d_g_pre_mlp, d_sc_m, d_sh_m) = _rowwise(
        "mid_bwd", mid_bwd_fn, [x2d, o2, dh2, dx2], mid_vecs, [(D, F32), (D, BF16)], [D] * 5, TB, after=(tok_up,))

    (d_on,) = _mm_nt("d_out_mm", do2_b, w_out_f, [F32])
    gw_out = _mm_tn("gw_out_mm", on_b, do2_b, 1).reshape(N_CHIPS, w_out.shape[1], D)

    def mix_bwd_fn(oa, ob, dont, ga, gb):
        na, ra = _rn_parts(oa)
        nb, rb = _rn_parts(ob)
        da, db = dont[:, :W_QA], dont[:, W_QA:]
        return (_rn_bwd(na, ra, da * ga), _rn_bwd(nb, rb, db * gb)), (_rows(da * na), _rows(db * nb))

    (doa_b, dob_b), (d_g_out_a, d_g_out_b) = _rowwise(
        "mix_bwd", mix_bwd_fn, [o_a, o_b, d_on], [g_out_a, g_out_b], [(W_QA, BF16), (HB * V_DIM, BF16)],
        [W_QA, HB * V_DIM], TB)

    dqa, dka, dva = _attn_bwd("attn_a_bwd", qa_b, ka_b, va_b, doa_b, o_a, lse_a, HA, HA // HKV, HEAD_DIM, HEAD_DIM,
                              sc_a_, TQ)
    dqcat, dkcat, dvb = _attn_bwd("attn_b_bwd", qcat_b, kcat_b, vb_b, dob_b, o_b, lse_b, HB, 1, 2 * LANE, V_DIM,
                                  sc_b_, TQ)

    d_ckvn, dkv_b, dkpe = _kv_mm_bwd("d_kv_mm", dkcat, dvb, g_kvb)
    gw_kvb = _mm_tn("gw_kv_mm", ckvn_b, dkv_b, N_CHIPS)
    b_kvb, tok_kvb = _reduce_begin("okv", [gw_out, gw_kvb])

    def prep_bwd_fn(pt, dqat, dkat, dvat, dqct, dckt, dkpt, ca_, sla, sha, cb_, slb, shb, gq, gk, gc):
        tabs = ((ca_, sla, sha), (cb_, slb, shb))
        _, vjp = jax.vjp(lambda sg, a1, a2, a3: _prep(sg, tabs, a1, a2, a3), _split_proj(pt), gq, gk, gc)
        (dqa_s, dka_s, dva_s, dqn_s, dqp_s, dck_s, dkp_s), dgq, dgk, dgc = vjp((dqat, dkat, dvat, dqct, dckt, dkpt))
        dproj = jnp.concatenate([*dqa_s, *dka_s, dva_s, *dqn_s, *dqp_s, dck_s, dkp_s], axis=1)
        return (dproj,), (dgq, dgk, dgc)

    (dproj_b,), (d_g_q_a, d_g_k_a, d_g_ckv) = _rowwise(
        "prep_bwd", prep_bwd_fn, [proj, dqa, dka, dva, dqcat, d_ckvn, dkpe, *tabs_a, *tabs_b],
        [g_q_a, g_k_a, g_ckv], [(W_INP, BF16)], [HEAD_DIM, HEAD_DIM, KV_RANK], TB, after=(tok_kvb,))

    r_kvb, tok = _reduce_start(b_kvb, where, dproj_b)
    (dh,) = _mm_nn("d_proj_mm", dproj_b, wpt, [F32], after=(tok,))
    gw_in = _mm_tn_rows("gw_in_mm", dproj_b, h_b, W_IN, _unpermute_pieces()).reshape(N_CHIPS, W_IN // N_CHIPS, D)
    b_in, tok = _reduce_begin("in", [gw_in])

    def pre_attn_bwd_fn(xt, dht, dx1t, g, sc, sh):
        n, r = _rn_parts(xt)
        s = _rows(dht * n)
        return (dx1t + _rn_bwd(n, r, dht * (g * (1.0 + sc))),), ((1.0 + sc) * s, g * s, _rows(dht))

    (grad_x,), (d_g_pre_attn, d_sc_a, d_sh_a) = _rowwise(
        "pre_attn_bwd", pre_attn_bwd_fn, [x2d, dh, dx1], [g_pre_attn, sc_a, sh_a], [(D, F32)], [D] * 3, TB,
        after=(tok,))

    small_names = ["b_ada", "g_pre_attn", "g_q_a", "g_k_a", "g_ckv", "g_out_a", "g_out_b", "g_post_attn",
                   "g_pre_mlp", "g_post_mlp"]
    small_local = jnp.concatenate([d_sh_a, d_sc_a, d_gt_a, d_sh_m, d_sc_m, d_gt_m, d_g_pre_attn, d_g_q_a, d_g_k_a,
                                   d_g_ckv, d_g_out_a, d_g_out_b, d_g_post_attn, d_g_pre_mlp, d_g_post_mlp,
                                   loss_v[:, :8]], axis=1)
    n_sent = small_local.shape[1]
    n_small = n_sent - 8
    small_all = _allgather8("gather_small", small_local.reshape(8, n_sent // 8)).reshape(N_DEV, n_sent)

    def sum8_fn(t):
        return (), (jnp.sum(t, axis=0, keepdims=True),)

    r_in, tok_in = _reduce_start(b_in, where, grad_x, (small_all,))
    _, (sent_sum,) = _rowwise("sum_small", sum8_fn, [small_all], [], [], [n_sent], N_DEV, after=(tok_in,))
    small_sum, loss = sent_sum[:, :n_small], sent_sum[0, n_small]
    dmod_cols = lax.dynamic_slice(small_all, (0, j_me * n_mod), (N_DEV, n_mod))
    dmod_pad = jnp.concatenate([dmod_cols, jnp.zeros_like(dmod_cols)], axis=0).astype(BF16)

    weights = dict(w_ada=w_ada, b_ada=b_ada, g_pre_attn=g_pre_attn, w_in=w_in, g_q_a=g_q_a, g_k_a=g_k_a, g_ckv=g_ckv,
                   w_kv_b=w_kv_b, g_out_a=g_out_a, g_out_b=g_out_b, w_out=w_out, g_post_attn=g_post_attn,
                   g_pre_mlp=g_pre_mlp, w_mlp_in=w_mlp_in, w_mlp_out=w_mlp_out, g_post_mlp=g_post_mlp)
    ms = dict(w_ada=m_w_ada, b_ada=m_b_ada, g_pre_attn=m_g_pre_attn, w_in=m_w_in, g_q_a=m_g_q_a, g_k_a=m_g_k_a,
              g_ckv=m_g_ckv, w_kv_b=m_w_kv_b, g_out_a=m_g_out_a, g_out_b=m_g_out_b, w_out=m_w_out,
              g_post_attn=m_g_post_attn, g_pre_mlp=m_g_pre_mlp, w_mlp_in=m_w_mlp_in, w_mlp_out=m_w_mlp_out,
              g_post_mlp=m_g_post_mlp)
    vs = dict(w_ada=v_w_ada, b_ada=v_b_ada, g_pre_attn=v_g_pre_attn, w_in=v_w_in, g_q_a=v_g_q_a, g_k_a=v_g_k_a,
              g_ckv=v_g_ckv, w_kv_b=v_w_kv_b, g_out_a=v_g_out_a, g_out_b=v_g_out_b, w_out=v_w_out,
              g_post_attn=v_g_post_attn, g_pre_mlp=v_g_pre_mlp, w_mlp_in=v_w_mlp_in, w_mlp_out=v_w_mlp_out,
              g_post_mlp=v_g_post_mlp)
    order = list(weights)
    grads, deltas, new_m, new_v = {}, {}, {}, {}

    def update(name, g, after=()):
        view = (lambda t: jnp.swapaxes(t, 0, 1)) if name == "w_in" else (lambda t: t)
        d_, m_, v_, g_ = _adamw("adamw_" + name, view(weights[name][0]), g, view(ms[name][0]), view(vs[name][0]),
                                after, rewrite_g=True)
        grads[name], deltas[name], new_m[name], new_v[name] = (view(t)[None] for t in (g_, d_, m_, v_))
        return d_

    chain = ((("w_mlp_out",), r_down), (("w_mlp_in",), r_up), (("w_out", "w_kv_b"), r_kvb), (("w_in",), r_in))
    swap, tok = _reduce_finish(chain[0][1], where, small_sum)
    last, m_, v_, g_ = _adamw_outer("adamw_w_ada", w_ada[0], ca_pad, dmod_pad, m_w_ada[0], v_w_ada[0], (tok,))
    grads["w_ada"], deltas["w_ada"], new_m["w_ada"], new_v["w_ada"] = g_[None], last[None], m_[None], v_[None]
    for k, (names, _) in enumerate(chain):
        gs = _reduce_end(swap, last)
        if k + 1 < len(chain):
            swap, tok = _reduce_finish(chain[k + 1][1], where, gs[0])
        for name, g in zip(names, gs):
            last = update(name, g, (tok,))

    cat = lambda d: jnp.concatenate([d[n] for n in small_names], axis=1)
    d_, m_, v_ = _adamw("adamw_small", cat(weights), small_sum, cat(ms), cat(vs))
    off = 0
    for n in small_names:
        w_ = weights[n].shape[1]
        grads[n], deltas[n], new_m[n], new_v[n] = (a[:, off:off + w_] for a in (small_sum, d_, m_, v_))
        off += w_

    return (loss, grad_x[None], *[grads[n] for n in order], *[deltas[n] for n in order],
            *[new_m[n] for n in order], *[new_v[n] for n in order])
```

```python
import functools
import math

import numpy as np
import jax
import jax.numpy as jnp
from jax import lax
from jax.experimental import pallas as pl
from jax.experimental.pallas import tpu as pltpu

F32 = jnp.float32
BF16 = jnp.bfloat16
MESH = pl.DeviceIdType.MESH

EPS = 1e-6
GRID_W = 64
ROPE_THETA = 10000.0
HEAD_DIM = 128
HA = 8
HKV = 2
HB = 8
QK_NOPE = 128
QK_ROPE = 64
V_DIM = 128
KV_RANK = 512
QK_B = QK_NOPE + QK_ROPE
W_QA = HA * HEAD_DIM
W_KA = HKV * HEAD_DIM
W_QB = HB * QK_B
W_IN = W_QA + 2 * W_KA + W_QB + KV_RANK + QK_ROPE
N_CHIPS = 4
N_DEV = 8

LANE = 128
O_QA = 0
O_KA = O_QA + W_QA
O_VA = O_KA + W_KA
O_QN = O_VA + W_KA
O_QP = O_QN + HB * QK_NOPE
O_CKV = O_QP + HB * LANE
O_KPE = O_CKV + KV_RANK
W_INP = O_KPE + LANE

ADAM_LR = 0.001
ADAM_B1 = 0.9
ADAM_B2 = 0.999
ADAM_EPS = 1e-08
ADAM_WD = 0.01
ADAM_STEP = 10

VMEM_LIMIT = 56 * 1024 * 1024
HEADS_PER_STEP = 2


def _params(sem):
    return pltpu.CompilerParams(dimension_semantics=sem, vmem_limit_bytes=VMEM_LIMIT)


def _pick(dim, target, unit):
    best = None
    t = unit
    while t <= min(dim, target):
        if dim % t == 0:
            best = t
        t += unit
    return dim if best is None else best


def _rn(x):
    return x * lax.rsqrt(jnp.mean(x * x, axis=-1, keepdims=True) + EPS)


def _rn_parts(x):
    r = lax.rsqrt(jnp.mean(x * x, axis=-1, keepdims=True) + EPS)
    return x * r, r


def _rn_bwd(n, r, dn):
    return r * (dn - n * jnp.mean(dn * n, axis=-1, keepdims=True))


def _rows(t):
    return jnp.sum(t, axis=0, keepdims=True)


def _rowwise(name, fn, rows, vecs, row_outs, vec_outs, tile, after=()):
    S = rows[0].shape[0]
    assert S % tile == 0
    n_r, n_v, n_ro, n_vo, n_a = len(rows), len(vecs), len(row_outs), len(vec_outs), len(after)

    def body(*refs):
        r_in = refs[:n_r]
        v_in = refs[n_r:n_r + n_v]
        r_out = refs[n_r + n_v + n_a:n_r + n_v + n_a + n_ro]
        v_out = refs[n_r + n_v + n_a + n_ro:]
        ro, vo = fn(*[r[...] for r in r_in], *[v[...] for v in v_in])
        for ref, val in zip(r_out, ro):
            ref[...] = val.astype(ref.dtype)
        if n_vo:
            @pl.when(pl.program_id(0) == 0)
            def _():
                for ref in v_out:
                    ref[...] = jnp.zeros(ref.shape, ref.dtype)
            for ref, val in zip(v_out, vo):
                ref[...] += val

    in_specs = [pl.BlockSpec((tile, a.shape[1]), lambda i: (i, 0)) for a in rows]
    in_specs += [pl.BlockSpec(a.shape, lambda i: (0, 0)) for a in vecs]
    in_specs += [pl.BlockSpec(memory_space=pl.ANY) for _ in after]
    out_specs = [pl.BlockSpec((tile, w), lambda i: (i, 0)) for w, _ in row_outs]
    out_specs += [pl.BlockSpec((1, w), lambda i: (0, 0)) for w in vec_outs]
    out_shape = [jax.ShapeDtypeStruct((S, w), dt) for w, dt in row_outs]
    out_shape += [jax.ShapeDtypeStruct((1, w), F32) for w in vec_outs]
    res = pl.pallas_call(
        body, name=name, grid=(S // tile,), in_specs=in_specs, out_specs=out_specs, out_shape=out_shape,
        compiler_params=_params(("arbitrary",)),
    )(*rows, *vecs, *after)
    return res[:n_ro], res[n_ro:]


CONTRACT_TILE = 4608
MM_TILE_BYTES = 44 * 1024 * 1024


def _rows_that_fit(M, tile_bytes):
    t = _pick(M, 1024, 16)
    while t % 32 == 0 and tile_bytes(t) > MM_TILE_BYTES:
        t //= 2
    return t


def _accumulate(acc, part, step, n_steps, finish):
    if n_steps == 1:
        finish(part)
        return

    @pl.when(step == 0)
    def _():
        acc[...] = part

    @pl.when((step > 0) & (step < n_steps - 1))
    def _():
        acc[...] += part

    @pl.when(step == n_steps - 1)
    def _():
        finish(acc[...] + part)


def _mm_nn(name, a, b3, out_dtypes, epilogue=None, extras=(), after=()):
    M, K = a.shape
    nB, K2, Nsh = b3.shape
    assert K == K2
    N = nB * Nsh
    tn, tk = _pick(Nsh, 1536, LANE), _pick(K, CONTRACT_TILE, LANE)
    out_bytes = sum(jnp.dtype(dt).itemsize for dt in out_dtypes) + sum(e.dtype.itemsize for e in extras)
    tm = _rows_that_fit(M, lambda t: 2 * (t * tk * a.dtype.itemsize + tk * tn * b3.dtype.itemsize)
                        + t * tn * (4 * (K > tk) + 2 * out_bytes))
    nps, nk, ne, no, na = Nsh // tn, K // tk, len(extras), len(out_dtypes), len(after)

    def body(a_ref, b_ref, *rest):
        e_refs, o_refs = rest[:ne], rest[ne + na:ne + na + no]

        def finish(total):
            vals = (total,) if epilogue is None else epilogue(total, *[e[...] for e in e_refs])
            for ref, val in zip(o_refs, vals):
                ref[...] = val.astype(ref.dtype)

        part = jnp.dot(a_ref[...].astype(BF16), b_ref[...].astype(BF16), preferred_element_type=F32)
        _accumulate(rest[-1], part, pl.program_id(2), nk, finish)

    in_specs = [pl.BlockSpec((tm, tk), lambda i, j, k: (i, k)),
                pl.BlockSpec((None, tk, tn), lambda i, j, k: (j // nps, k, j % nps))]
    in_specs += [pl.BlockSpec((tm, tn), lambda i, j, k: (i, j)) for _ in extras]
    in_specs += [pl.BlockSpec(memory_space=pl.ANY) for _ in after]
    return pl.pallas_call(
        body, name=name, grid=(M // tm, N // tn, nk), in_specs=in_specs,
        out_specs=[pl.BlockSpec((tm, tn), lambda i, j, k: (i, j)) for _ in out_dtypes],
        out_shape=[jax.ShapeDtypeStruct((M, N), dt) for dt in out_dtypes],
        scratch_shapes=[pltpu.VMEM((tm, tn), F32)] if nk > 1 else [],
        compiler_params=_params(("parallel", "parallel", "arbitrary")),
    )(a, b3, *extras, *after)


def _mm_nt(name, a, b3, out_dtypes, epilogue=None, extras=(), after=()):
    M, N = a.shape
    nB, K, Nsh = b3.shape
    assert N == nB * Nsh
    to, tn = _pick(K, 1536, LANE), _pick(Nsh, CONTRACT_TILE, LANE)
    out_bytes = sum(jnp.dtype(dt).itemsize for dt in out_dtypes) + sum(e.dtype.itemsize for e in extras)
    tm = _rows_that_fit(M, lambda t: 2 * (t * tn * a.dtype.itemsize + to * tn * b3.dtype.itemsize)
                        + t * to * (4 * (N > tn) + 2 * out_bytes))
    nps, nn, ne, no, na = Nsh // tn, N // tn, len(extras), len(out_dtypes), len(after)

    def body(a_ref, b_ref, *rest):
        e_refs, o_refs = rest[:ne], rest[ne + na:ne + na + no]

        def finish(total):
            vals = (total,) if epilogue is None else epilogue(total, *[e[...] for e in e_refs])
            for ref, val in zip(o_refs, vals):
                ref[...] = val.astype(ref.dtype)

        part = lax.dot_general(a_ref[...].astype(BF16), b_ref[...].astype(BF16),
                               (((1,), (1,)), ((), ())), preferred_element_type=F32)
        _accumulate(rest[-1], part, pl.program_id(2), nn, finish)

    in_specs = [pl.BlockSpec((tm, tn), lambda i, o, n: (i, n)),
                pl.BlockSpec((None, to, tn), lambda i, o, n: (n // nps, o, n % nps))]
    in_specs += [pl.BlockSpec((tm, to), lambda i, o, n: (i, o)) for _ in extras]
    in_specs += [pl.BlockSpec(memory_space=pl.ANY) for _ in after]
    return pl.pallas_call(
        body, name=name, grid=(M // tm, K // to, nn), in_specs=in_specs,
        out_specs=[pl.BlockSpec((tm, to), lambda i, o, n: (i, o)) for _ in out_dtypes],
        out_shape=[jax.ShapeDtypeStruct((M, K), dt) for dt in out_dtypes],
        scratch_shapes=[pltpu.VMEM((tm, to), F32)] if nn > 1 else [],
        compiler_params=_params(("parallel", "parallel", "arbitrary")),
    )(a, b3, *extras, *after)


def _mm_tn(name, a, g, nB, after=()):
    M, K = a.shape
    M2, N = g.shape
    assert M == M2 and N % nB == 0
    Nsh = N // nB
    tk, tn, tm = _pick(K, 1536, LANE), _pick(Nsh, 1024, LANE), _pick(M, CONTRACT_TILE, 16)
    nps, nm = Nsh // tn, M // tm

    def body(a_ref, g_ref, *rest):
        o_ref = rest[len(after)]

        def finish(total):
            o_ref[...] = total

        part = lax.dot_general(a_ref[...].astype(BF16), g_ref[...].astype(BF16),
                               (((0,), (0,)), ((), ())), preferred_element_type=F32)
        _accumulate(rest[-1], part, pl.program_id(2), nm, finish)

    return pl.pallas_call(
        body, name=name, grid=(K // tk, N // tn, nm),
        in_specs=[pl.BlockSpec((tm, tk), lambda k, n, m: (m, k)),
                  pl.BlockSpec((tm, tn), lambda k, n, m: (m, n))] + [pl.BlockSpec(memory_space=pl.ANY) for _ in after],
        out_specs=pl.BlockSpec((None, tk, tn), lambda k, n, m: (n // nps, k, n % nps)),
        out_shape=jax.ShapeDtypeStruct((nB, K, Nsh), F32),
        scratch_shapes=[pltpu.VMEM((tk, tn), F32)] if nm > 1 else [],
        compiler_params=_params(("parallel", "parallel", "arbitrary")),
    )(a, g, *after)


def _rows_mm(name, fn, rows, vecs, b3, after=()):
    S = rows[0].shape[0]
    _, K, N = b3.shape
    tm, tn = _pick(S, 1024, 16), _pick(N, 512, LANE)
    nr, nv, na = len(rows), len(vecs), len(after)

    def body(*refs):
        b_ref = refs[nr + nv]
        a_out, o_ref, a_scr = refs[nr + nv + 1 + na:]

        @pl.when(pl.program_id(1) == 0)
        def _():
            a = fn(*[r[...] for r in refs[:nr]], *[v[...] for v in refs[nr:nr + nv]]).astype(BF16)
            a_scr[...] = a
            a_out[...] = a

        o_ref[...] = jnp.dot(a_scr[...], b_ref[...], preferred_element_type=F32)

    in_specs = [pl.BlockSpec((tm, r.shape[1]), lambda i, j: (i, 0)) for r in rows]
    in_specs += [pl.BlockSpec(v.shape, lambda i, j: (0, 0)) for v in vecs]
    in_specs += [pl.BlockSpec((None, K, tn), lambda i, j: (0, 0, j))]
    in_specs += [pl.BlockSpec(memory_space=pl.ANY) for _ in after]
    return pl.pallas_call(
        body, name=name, grid=(S // tm, N // tn), in_specs=in_specs,
        out_specs=[pl.BlockSpec((tm, K), lambda i, j: (i, 0)), pl.BlockSpec((tm, tn), lambda i, j: (i, j))],
        out_shape=[jax.ShapeDtypeStruct((S, K), BF16), jax.ShapeDtypeStruct((S, N), F32)],
        scratch_shapes=[pltpu.VMEM((tm, K), BF16)],
        compiler_params=_params(("parallel", "arbitrary")),
    )(*rows, *vecs, b3, *after)


def _mm_nt_rows(name, a, b, fn, rows, vecs, row_outs, vec_outs):
    S, K = a.shape
    N = b.shape[0]
    tm = _pick(S, 512, 16)
    nr, nv, nro = len(rows), len(vecs), len(row_outs)

    def body(a_ref, b_ref, *refs):
        prod = lax.dot_general(a_ref[...], b_ref[...], (((1,), (1,)), ((), ())), preferred_element_type=F32)
        ro, vo = fn(prod, *[r[...] for r in refs[:nr]], *[v[...] for v in refs[nr:nr + nv]])
        for ref, val in zip(refs[nr + nv:nr + nv + nro], ro):
            ref[...] = val.astype(ref.dtype)
        v_out = refs[nr + nv + nro:]

        @pl.when(pl.program_id(0) == 0)
        def _():
            for ref in v_out:
                ref[...] = jnp.zeros(ref.shape, ref.dtype)

        for ref, val in zip(v_out, vo):
            ref[...] += val

    in_specs = [pl.BlockSpec((tm, K), lambda i: (i, 0)), pl.BlockSpec((N, K), lambda i: (0, 0))]
    in_specs += [pl.BlockSpec((tm, r.shape[1]), lambda i: (i, 0)) for r in rows]
    in_specs += [pl.BlockSpec(v.shape, lambda i: (0, 0)) for v in vecs]
    res = pl.pallas_call(
        body, name=name, grid=(S // tm,), in_specs=in_specs,
        out_specs=[pl.BlockSpec((tm, w), lambda i: (i, 0)) for w, _ in row_outs]
        + [pl.BlockSpec((1, w), lambda i: (0, 0)) for w in vec_outs],
        out_shape=[jax.ShapeDtypeStruct((S, w), dt) for w, dt in row_outs]
        + [jax.ShapeDtypeStruct((1, w), F32) for w in vec_outs],
        compiler_params=_params(("arbitrary",)),
    )(a, b, *rows, *vecs)
    return res[:nro], res[nro:]


def _kv_mm(name, ckvn, w3, kpe):
    S, K = ckvn.shape
    nB, _, Nsh = w3.shape
    heads = Nsh // (2 * LANE)
    tm = _pick(S, 1024, 16)

    def body(a_ref, b_ref, kpe_ref, kc_ref, v_ref):
        acc = jnp.dot(a_ref[...], b_ref[...], preferred_element_type=F32)
        for h in range(heads):
            kc_ref[:, 2 * h * LANE:(2 * h + 1) * LANE] = acc[:, 2 * h * LANE:(2 * h + 1) * LANE].astype(BF16)
            kc_ref[:, (2 * h + 1) * LANE:(2 * h + 2) * LANE] = kpe_ref[...]
            v_ref[:, h * LANE:(h + 1) * LANE] = acc[:, (2 * h + 1) * LANE:(2 * h + 2) * LANE].astype(BF16)

    return pl.pallas_call(
        body, name=name, grid=(S // tm, nB),
        in_specs=[pl.BlockSpec((tm, K), lambda i, j: (i, 0)), pl.BlockSpec((None, K, Nsh), lambda i, j: (j, 0, 0)),
                  pl.BlockSpec((tm, LANE), lambda i, j: (i, 0))],
        out_specs=[pl.BlockSpec((tm, Nsh), lambda i, j: (i, j)), pl.BlockSpec((tm, Nsh // 2), lambda i, j: (i, j))],
        out_shape=[jax.ShapeDtypeStruct((S, nB * Nsh), BF16), jax.ShapeDtypeStruct((S, nB * Nsh // 2), BF16)],
        compiler_params=_params(("parallel", "arbitrary")),
    )(ckvn, w3, kpe)


def _kv_mm_bwd(name, dkc, dv, w3):
    S = dkc.shape[0]
    nB, K, Nsh = w3.shape
    heads = Nsh // (2 * LANE)
    tm = _pick(S, 1024, 16)

    def body(dkc_ref, dv_ref, b_ref, dck_ref, dkv_ref, dkpe_ref):
        j = pl.program_id(1)
        pieces, pe = [], None
        for h in range(heads):
            pieces += [dkc_ref[:, 2 * h * LANE:(2 * h + 1) * LANE], dv_ref[:, h * LANE:(h + 1) * LANE]]
            piece = dkc_ref[:, (2 * h + 1) * LANE:(2 * h + 2) * LANE]
            pe = piece if pe is None else pe + piece
        dkv = jnp.concatenate(pieces, axis=1).astype(BF16)
        dkv_ref[...] = dkv
        part = lax.dot_general(dkv, b_ref[...], (((1,), (1,)), ((), ())), preferred_element_type=F32)

        @pl.when(j == 0)
        def _():
            dck_ref[...] = part
            dkpe_ref[...] = pe

        @pl.when(j > 0)
        def _():
            dck_ref[...] += part
            dkpe_ref[...] += pe

    return pl.pallas_call(
        body, name=name, grid=(S // tm, nB),
        in_specs=[pl.BlockSpec((tm, Nsh), lambda i, j: (i, j)), pl.BlockSpec((tm, Nsh // 2), lambda i, j: (i, j)),
                  pl.BlockSpec((None, K, Nsh), lambda i, j: (j, 0, 0))],
        out_specs=[pl.BlockSpec((tm, K), lambda i, j: (i, 0)), pl.BlockSpec((tm, Nsh), lambda i, j: (i, j)),
                   pl.BlockSpec((tm, LANE), lambda i, j: (i, 0))],
        out_shape=[jax.ShapeDtypeStruct((S, K), F32), jax.ShapeDtypeStruct((S, nB * Nsh), BF16),
                   jax.ShapeDtypeStruct((S, LANE), F32)],
        compiler_params=_params(("parallel", "arbitrary")),
    )(dkc, dv, w3)


def _mm_tn_rows(name, a, g, n_rows, pieces):
    M, K = a.shape
    M2, N = g.shape
    assert M == M2
    tk, tn, tm = _pick(K, 1536, LANE), _pick(N, 512, LANE), _pick(M, CONTRACT_TILE, 16)
    nk, nm = K // tk, M // tm

    def body(a_ref, g_ref, o_ref, *acc):
        kk = pl.program_id(1)

        def finish(total):
            for t in range(nk):
                @pl.when(kk == t)
                def _(t=t):
                    for src, dst, n in pieces:
                        lo, hi = max(src, t * tk), min(src + n, (t + 1) * tk)
                        if lo < hi:
                            o_ref[dst + lo - src:dst + hi - src, :] = total[lo - t * tk:hi - t * tk, :]

        part = lax.dot_general(a_ref[...].astype(BF16), g_ref[...].astype(BF16),
                               (((0,), (0,)), ((), ())), preferred_element_type=F32)
        _accumulate(acc[0] if acc else None, part, pl.program_id(2), nm, finish)

    return pl.pallas_call(
        body, name=name, grid=(N // tn, nk, nm),
        in_specs=[pl.BlockSpec((tm, tk), lambda n, k, m: (m, k)),
                  pl.BlockSpec((tm, tn), lambda n, k, m: (m, n))],
        out_specs=pl.BlockSpec((n_rows, tn), lambda n, k, m: (0, n)),
        out_shape=jax.ShapeDtypeStruct((n_rows, N), F32),
        scratch_shapes=[pltpu.VMEM((tk, tn), F32)] if nm > 1 else [],
        compiler_params=_params(("arbitrary", "arbitrary", "arbitrary")),
    )(a, g)


def _attn_fwd(name, q, k, v, n_heads, group, dqk, dv, scale, tq, after=()):
    S = q.shape[0]
    hp = 2 * HEADS_PER_STEP
    kvp = max(1, hp // group)

    def body(q_ref, k_ref, v_ref, *rest):
        o_ref, lse_ref = rest[-2:]
        for j in range(hp):
            jk = j * kvp // hp
            s = lax.dot_general(q_ref[:, j * dqk:(j + 1) * dqk], k_ref[:, jk * dqk:(jk + 1) * dqk],
                                (((1,), (1,)), ((), ())), preferred_element_type=F32)
            m = jnp.max(s, axis=-1, keepdims=True)
            e = jnp.exp((s - m) * scale)
            l = jnp.sum(e, axis=-1, keepdims=True)
            o_ref[:, j * dv:(j + 1) * dv] = jnp.dot(e.astype(BF16), v_ref[:, jk * dv:(jk + 1) * dv],
                                                    preferred_element_type=F32) * (1.0 / l)
            lse_ref[:, j * LANE:(j + 1) * LANE] = jnp.broadcast_to(m * scale + jnp.log(l), (tq, LANE))

    kv_block = lambda h, i: (0, (h * hp // group) // kvp)
    return pl.pallas_call(
        body, name=name, grid=(n_heads // hp, S // tq),
        in_specs=[pl.BlockSpec((tq, hp * dqk), lambda h, i: (i, h)),
                  pl.BlockSpec((S, kvp * dqk), kv_block),
                  pl.BlockSpec((S, kvp * dv), kv_block)] + [pl.BlockSpec(memory_space=pl.ANY) for _ in after],
        out_specs=[pl.BlockSpec((tq, hp * dv), lambda h, i: (i, h)),
                   pl.BlockSpec((tq, hp * LANE), lambda h, i: (i, h))],
        out_shape=[jax.ShapeDtypeStruct((S, n_heads * dv), F32), jax.ShapeDtypeStruct((S, n_heads * LANE), F32)],
        compiler_params=_params(("arbitrary", "arbitrary")),
    )(q, k, v, *after)


def _attn_bwd(name, q, k, v, do, o, lse, n_heads, group, dqk, dv, scale, tq, after=()):
    S = q.shape[0]
    n_kv = n_heads // group
    hp, tq = (group, tq // 2) if group > HEADS_PER_STEP else (HEADS_PER_STEP, tq)
    kvp = max(1, hp // group)
    chains = 4
    tr = tq * hp // chains

    def body(q_ref, k_ref, v_ref, do_ref, o_ref, lse_ref, *rest):
        dq_ref, dk_ref, dv_ref = rest[len(after):]
        h, i = pl.program_id(0), pl.program_id(1)

        @pl.when(((h * hp) % group == 0) & (i == 0))
        def _():
            dk_ref[...] = jnp.zeros(dk_ref.shape, F32)
            dv_ref[...] = jnp.zeros(dv_ref.shape, F32)

        for j in range(hp):
            jk = j * kvp // hp
            kb, vb = k_ref[:, jk * dqk:(jk + 1) * dqk], v_ref[:, jk * dv:(jk + 1) * dv]
            for r0 in range(0, tq, tr):
                rows = slice(r0, r0 + tr)
                qb, dob = q_ref[rows, j * dqk:(j + 1) * dqk], do_ref[rows, j * dv:(j + 1) * dv]
                s = lax.dot_general(qb, kb, (((1,), (1,)), ((), ())), preferred_element_type=F32)
                p = jnp.exp(s * scale - jnp.tile(lse_ref[rows, j * LANE:(j + 1) * LANE], (1, S // LANE)))
                dp = lax.dot_general(dob, vb, (((1,), (1,)), ((), ())), preferred_element_type=F32)
                delta = jnp.sum(dob.astype(F32) * o_ref[rows, j * dv:(j + 1) * dv], axis=-1, keepdims=True)
                ds = (p * ((dp - delta) * scale)).astype(BF16)
                dv_ref[:, jk * dv:(jk + 1) * dv] += lax.dot_general(p.astype(BF16), dob, (((0,), (0,)), ((), ())),
                                                                    preferred_element_type=F32)
                dq_ref[rows, j * dqk:(j + 1) * dqk] = jnp.dot(ds, kb, preferred_element_type=F32)
                dk_ref[:, jk * dqk:(jk + 1) * dqk] += lax.dot_general(ds, qb, (((0,), (0,)), ((), ())),
                                                                      preferred_element_type=F32)

    kv_block = lambda h, i: (0, (h * hp // group) // kvp)
    return pl.pallas_call(
        body, name=name, grid=(n_heads // hp, S // tq),
        in_specs=[pl.BlockSpec((tq, hp * dqk), lambda h, i: (i, h)),
                  pl.BlockSpec((S, kvp * dqk), kv_block),
                  pl.BlockSpec((S, kvp * dv), kv_block),
                  pl.BlockSpec((tq, hp * dv), lambda h, i: (i, h)),
                  pl.BlockSpec((tq, hp * dv), lambda h, i: (i, h)),
                  pl.BlockSpec((tq, hp * LANE), lambda h, i: (i, h))] + [pl.BlockSpec(memory_space=pl.ANY) for _ in after],
        out_specs=[pl.BlockSpec((tq, hp * dqk), lambda h, i: (i, h)),
                   pl.BlockSpec((S, kvp * dqk), kv_block),
                   pl.BlockSpec((S, kvp * dv), kv_block)],
        out_shape=[jax.ShapeDtypeStruct((S, n_heads * dqk), F32),
                   jax.ShapeDtypeStruct((S, n_kv * dqk), F32),
                   jax.ShapeDtypeStruct((S, n_kv * dv), F32)],
        compiler_params=_params(("arbitrary", "arbitrary")),
    )(q, k, v, do, o, lse, *after)


def _adamw_step(w, gg, m, v):
    nm = ADAM_B1 * m + (1.0 - ADAM_B1) * gg
    nv = ADAM_B2 * v + (1.0 - ADAM_B2) * (gg * gg)
    c1, c2 = 1.0 - ADAM_B1 ** ADAM_STEP, 1.0 - ADAM_B2 ** ADAM_STEP
    return -ADAM_LR * ((nm / c1) / (jnp.sqrt(nv / c2) + ADAM_EPS) + ADAM_WD * w), nm, nv


def _adamw(name, w, g, m, v, after=(), rewrite_g=False):
    R, C = w.shape
    tile = _pick(R, max(8, (1 << 19) // C // 8 * 8), 8)
    n_out = 4 if rewrite_g else 3

    def body(w_ref, g_ref, m_ref, v_ref, *rest):
        d_ref, nm_ref, nv_ref = rest[len(after):len(after) + 3]
        gg = g_ref[...]
        d_ref[...], nm_ref[...], nv_ref[...] = _adamw_step(w_ref[...], gg, m_ref[...], v_ref[...])
        if rewrite_g:
            rest[-1][...] = gg

    spec = pl.BlockSpec((tile, C), lambda i: (i, 0))
    return pl.pallas_call(
        body, name=name, grid=(R // tile,), in_specs=[spec] * 4 + [pl.BlockSpec(memory_space=pl.ANY) for _ in after],
        out_specs=[spec] * n_out, out_shape=[jax.ShapeDtypeStruct((R, C), F32)] * n_out,
        compiler_params=_params(("parallel",)),
    )(w, g, m, v, *after)


def _adamw_outer(name, w, left, right, m, v, after=()):
    R, C = w.shape
    Kc = left.shape[0]
    tile = _pick(R, max(LANE, (1 << 19) // C // LANE * LANE), LANE)

    def body(w_ref, l_ref, r_ref, m_ref, v_ref, *rest):
        d_ref, nm_ref, nv_ref, g_ref = rest[len(after):]
        gg = lax.dot_general(l_ref[...], r_ref[...], (((0,), (0,)), ((), ())), preferred_element_type=F32)
        d_ref[...], nm_ref[...], nv_ref[...] = _adamw_step(w_ref[...], gg, m_ref[...], v_ref[...])
        g_ref[...] = gg

    spec = pl.BlockSpec((tile, C), lambda i: (i, 0))
    return pl.pallas_call(
        body, name=name, grid=(R // tile,),
        in_specs=[spec, pl.BlockSpec((Kc, tile), lambda i: (0, i)), pl.BlockSpec((Kc, C), lambda i: (0, 0)), spec, spec]
        + [pl.BlockSpec(memory_space=pl.ANY) for _ in after],
        out_specs=[spec] * 4, out_shape=[jax.ShapeDtypeStruct((R, C), F32)] * 4,
        compiler_params=_params(("parallel",)),
    )(w, left, right, m, v, *after)


def _flip(me, rel):
    return tuple(1 - p if r else p for p, r in zip(me, rel))


def _push(name, ins, out_shapes, transfers, aliases=None):
    ni, no, nt = len(ins), len(out_shapes), len(transfers)

    def body(*refs):
        in_refs, out_refs = refs[:ni], refs[ni:ni + no]
        send_sems, recv_sems = refs[ni + no], refs[ni + no + 1]
        me = (lax.axis_index("x"), lax.axis_index("y"), lax.axis_index("c"))
        copies = []
        for t, (si, sfn, oi, dfn, rel) in enumerate(transfers):
            src_ref = out_refs[si[1]] if isinstance(si, tuple) else in_refs[si]
            src = src_ref.at[sfn(*me)]
            dst = out_refs[oi].at[dfn(*me)]
            if rel is None:
                cp = pltpu.make_async_copy(src, dst, send_sems.at[t])
            else:
                cp = pltpu.make_async_remote_copy(src_ref=src, dst_ref=dst, send_sem=send_sems.at[t],
                                                  recv_sem=recv_sems.at[t], device_id=_flip(me, rel),
                                                  device_id_type=MESH)
            cp.start()
            copies.append(cp)
        for cp in copies:
            cp.wait()

    any_spec = pl.BlockSpec(memory_space=pl.ANY)
    return pl.pallas_call(
        body, name=name, in_specs=[any_spec] * ni, out_specs=[any_spec] * no, out_shape=out_shapes,
        scratch_shapes=[pltpu.SemaphoreType.DMA((nt,)), pltpu.SemaphoreType.DMA((nt,))],
        input_output_aliases=aliases or {},
        compiler_params=pltpu.CompilerParams(has_side_effects=True),
    )(*ins)


_HBM = pl.BlockSpec(memory_space=pltpu.HBM)
_SEM = pl.BlockSpec(memory_space=pltpu.SEMAPHORE)
_DATAFLOW = pltpu.SideEffectType.DATAFLOW_SIDE_EFFECTING


def _split_copies(refs, transfers, send_sems, recv_sems):
    me = (lax.axis_index("x"), lax.axis_index("y"), lax.axis_index("c"))
    return [pltpu.make_async_remote_copy(src_ref=refs[sb].at[sfn(*me)], dst_ref=refs[db].at[dfn(*me)],
                                         send_sem=send_sems.at[t], recv_sem=recv_sems.at[t],
                                         device_id=_flip(me, rel), device_id_type=MESH)
            for t, (sb, sfn, db, dfn, rel) in enumerate(transfers)]


def _push_start(name, bufs, transfers, after=()):
    nb, na, nt = len(bufs), len(after), len(transfers)

    def body(*refs):
        send_sems, recv_sems, token = refs[nb + na], refs[nb + na + 1], refs[-1]
        for cp in _split_copies(refs[nb + na + 2:2 * nb + na + 2], transfers, send_sems, recv_sems):
            cp.start()
        token[...] = jnp.zeros(token.shape, token.dtype)

    res = pl.pallas_call(
        body, name=name,
        out_shape=(pltpu.SemaphoreType.DMA((nt,)), pltpu.SemaphoreType.DMA((nt,)),
                   *[pltpu.HBM(b.shape, b.dtype) for b in bufs], jax.ShapeDtypeStruct((8, LANE), F32)),
        in_specs=[_HBM] * nb + [pl.BlockSpec(memory_space=pl.ANY)] * na,
        out_specs=(_SEM, _SEM, *[_HBM] * nb, pl.BlockSpec(memory_space=pltpu.VMEM)),
        input_output_aliases={i: i + 2 for i in range(nb)},
        compiler_params=pltpu.CompilerParams(has_side_effects=_DATAFLOW),
    )(*[pltpu.with_memory_space_constraint(b, pltpu.HBM) for b in bufs], *after)
    return res[0], res[1], list(res[2:2 + nb]), res[-1]


def _push_wait(name, send_sems, recv_sems, bufs, transfers, after):
    nb = len(bufs)

    def body(*refs):
        for cp in _split_copies(refs[:nb], transfers, refs[nb], refs[nb + 1]):
            cp.wait_send()
            cp.wait_recv()

    res = pl.pallas_call(
        body, name=name, out_shape=[pltpu.HBM(b.shape, b.dtype) for b in bufs],
        in_specs=[_HBM] * nb + [_SEM, _SEM, pl.BlockSpec(memory_space=pl.ANY)], out_specs=[_HBM] * nb,
        input_output_aliases={i: i for i in range(nb)},
        compiler_params=pltpu.CompilerParams(has_side_effects=_DATAFLOW),
    )(*bufs, send_sems, recv_sems, after)
    return list(res)


ICI_RELS = ((1, 0, 0), (0, 1, 0), (1, 1, 0))
SIBLING = (0, 0, 1)


def _chip(x, y):
    return 2 * x + y


def _cast_into_slot(name, where, w, after=()):
    R, C = w.shape
    tile = _pick(R, 512, BF16_ROWS)

    def body(where_ref, w_ref, *rest):
        rest[-1][...] = w_ref[...].astype(BF16)

    return pl.pallas_call(
        body, name=name,
        grid_spec=pltpu.PrefetchScalarGridSpec(
            num_scalar_prefetch=1, grid=(R // tile,),
            in_specs=[pl.BlockSpec((tile, C), lambda i, wh: (i, 0))] + [pl.BlockSpec(memory_space=pl.ANY) for _ in after],
            out_specs=pl.BlockSpec((None, tile, C), lambda i, wh: (wh[1], i, 0))),
        out_shape=jax.ShapeDtypeStruct((N_CHIPS, R, C), BF16),
        compiler_params=_params(("parallel",)),
    )(where, w, *after)


BF16_ROWS = 16


def _rows_split(shape2):
    return (shape2[0] // 2) % BF16_ROWS == 0


def _half_shape(shape2):
    R, C = shape2
    return (R // 2, C) if _rows_split(shape2) else (R, C // 2)


def _half(shape2, c):
    R, C = shape2
    if _rows_split(shape2):
        return (pl.ds(c * (R // 2), R // 2), slice(None))
    return (slice(None), pl.ds(pl.multiple_of(c * (C // 2), LANE), C // 2))


def _gather_casts(tag, where, shards, after=()):
    return [_cast_into_slot(f"cast_w_{tag}{w}", where, s, after) for w, s in enumerate(shards)]


def _gather_start(tag, own, after):
    transfers = []
    for w in range(len(own)):
        def idx(x, y, c, shape2=own[w].shape[1:]):
            return (_chip(x, y), *_half(shape2, c))
        transfers += [(w, idx, w, idx, rel) for rel in ICI_RELS]
    send_sems, recv_sems, bufs, token = _push_start(f"gather_{tag}_start", own, transfers, after)
    return (tag, send_sems, recv_sems, bufs, transfers), token


def _forward_transfers(part):
    second = []
    for w, p in enumerate(part):
        for rel in ICI_RELS:
            def idx(x, y, c, shape2=p.shape[1:], rel=rel):
                return (_chip(x ^ rel[0], y ^ rel[1]), *_half(shape2, c))
            second.append((w, idx, w, idx, SIBLING))
    return second


def _gather_finish(handle, after):
    tag, send_sems, recv_sems, bufs, transfers = handle
    part = _push_wait(f"gather_{tag}_wait", send_sems, recv_sems, bufs, transfers, after)
    outs = [jax.ShapeDtypeStruct(p.shape, p.dtype) for p in part]
    second = [(("out", sb), sfn, db, dfn, rel) for sb, sfn, db, dfn, rel in _forward_transfers(part)]
    return _push(f"gather_{tag}_d2d", list(part), outs, second, aliases={w: w for w in range(len(part))})


def _gather_mid(handle, after):
    tag, send_sems, recv_sems, bufs, transfers = handle
    part = _push_wait(f"gather_{tag}_wait", send_sems, recv_sems, bufs, transfers, after)
    second = _forward_transfers(part)
    s2, r2, bufs2, token = _push_start(f"gather_{tag}_d2d_start", part, second)
    return (tag, s2, r2, bufs2, second), token


def _gather_end(handle, after):
    tag, send_sems, recv_sems, bufs, transfers = handle
    return _push_wait(f"gather_{tag}_d2d_wait", send_sems, recv_sems, bufs, transfers, after)


def _reduce_begin(tag, grads, after=()):
    n = len(grads)
    land = [lax.empty((N_CHIPS, *_half_shape(g.shape[1:])), F32) for g in grads]
    t1 = []
    for w in range(n):
        t1.append((w, lambda x, y, c, shape2=grads[w].shape[1:]: (slice(None), *_half(shape2, 1 - c)), n + w,
                   lambda x, y, c: (slice(None), slice(None), slice(None)), SIBLING))
    send_sems, recv_sems, bufs, token = _push_start(f"reduce_{tag}_d2d_start", list(grads) + land, t1, after)
    return (tag, send_sems, recv_sems, bufs, t1), token


def _reduce_start(handle, where, after, after_start=()):
    tag, send_sems, recv_sems, bufs, t1 = handle
    n = len(bufs) // 2
    res = _push_wait(f"reduce_{tag}_d2d_wait", send_sems, recv_sems, bufs, t1, after)
    grads, from_sib = res[:n], res[n:]
    pair = [_pair_sum(f"pair_sum_{tag}{w}", where, grads[w], from_sib[w]) for w in range(n)]
    land = [lax.empty((3, *_half_shape(g.shape[1:])), BF16) for g in grads]
    t3 = []
    for w in range(n):
        for r, rel in enumerate(ICI_RELS):
            t3.append((w, lambda x, y, c, rel=rel: (_chip(x ^ rel[0], y ^ rel[1]), slice(None), slice(None)), n + w,
                       lambda x, y, c, r=r: (r, slice(None), slice(None)), rel))
    send_sems, recv_sems, bufs, token = _push_start(f"reduce_{tag}_start", pair + land, t3, after_start)
    return (tag, send_sems, recv_sems, bufs, t3, list(grads), list(from_sib)), token


def _reduce_finish(handle, where, after):
    tag, send_sems, recv_sems, bufs, t3, grads, from_sib = handle
    n = len(grads)
    from_chips = _push_wait(f"reduce_{tag}_wait", send_sems, recv_sems, bufs, t3, after)[n:]
    mine = [_chip_sum(f"chip_sum_{tag}{w}", where, grads[w], from_sib[w], from_chips[w]) for w in range(n)]
    t5 = []
    for w in range(n):
        def idx(x, y, c, shape2=grads[w].shape[1:]):
            return _half(shape2, c)
        t5.append((w, idx, w, idx, SIBLING))
    send_sems, recv_sems, bufs, token = _push_start(f"reduce_{tag}_swap_start", mine, t5)
    return (tag, send_sems, recv_sems, bufs, t5), token


def _reduce_end(handle, after):
    tag, send_sems, recv_sems, bufs, t5 = handle
    return _push_wait(f"reduce_{tag}_swap_wait", send_sems, recv_sems, bufs, t5, after)


def _pair_sum(name, where, g, from_sib):
    _, hr, hc = from_sib.shape
    tile = _pick(hr, max(16, (1 << 19) // hc // 16 * 16), 16)
    nb = hr // tile
    slot = lambda k, wh: (wh[1] + 1 + k) % N_CHIPS
    if _rows_split(g.shape[1:]):
        mine = lambda k, i, wh: (slot(k, wh), wh[0] * nb + i, 0)
    else:
        mine = lambda k, i, wh: (slot(k, wh), i, wh[0])

    def body(where_ref, g_ref, s_ref, o_ref):
        o_ref[...] = (g_ref[...] + s_ref[...]).astype(BF16)

    return pl.pallas_call(
        body, name=name,
        grid_spec=pltpu.PrefetchScalarGridSpec(
            num_scalar_prefetch=1, grid=(N_CHIPS - 1, nb),
            in_specs=[pl.BlockSpec((None, tile, hc), mine),
                      pl.BlockSpec((None, tile, hc), lambda k, i, wh: (slot(k, wh), i, 0))],
            out_specs=pl.BlockSpec((None, tile, hc), lambda k, i, wh: (slot(k, wh), i, 0))),
        out_shape=jax.ShapeDtypeStruct((N_CHIPS, hr, hc), BF16),
        compiler_params=_params(("parallel", "parallel")),
    )(where, g, from_sib)


def _chip_sum(name, where, g, from_sib, from_chips):
    _, R, C = g.shape
    _, hr, hc = from_sib.shape
    tile = _pick(hr, max(16, (1 << 19) // hc // 16 * 16), 16)
    nb = hr // tile
    if _rows_split((R, C)):
        mine = lambda i, wh: (wh[0] * nb + i, 0)
    else:
        mine = lambda i, wh: (i, wh[0])

    def body(where_ref, g_ref, s_ref, r_ref, o_ref):
        acc = g_ref[...] + s_ref[...]
        for r in range(3):
            acc = acc + r_ref[r].astype(F32)
        o_ref[...] = acc

    return pl.pallas_call(
        body, name=name,
        grid_spec=pltpu.PrefetchScalarGridSpec(
            num_scalar_prefetch=1, grid=(nb,),
            in_specs=[pl.BlockSpec((None, tile, hc), lambda i, wh: (wh[1], *mine(i, wh))),
                      pl.BlockSpec((None, tile, hc), lambda i, wh: (wh[1], i, 0)),
                      pl.BlockSpec((3, tile, hc), lambda i, wh: (0, i, 0))],
            out_specs=pl.BlockSpec((tile, hc), mine)),
        out_shape=jax.ShapeDtypeStruct((R, C), F32),
        compiler_params=_params(("parallel",)),
    )(where, g, from_sib, from_chips)


def _allgather8(name, blk, after=()):
    m_per, n = blk.shape
    rels = [(0, 0, 1), (1, 0, 0), (0, 1, 0), (1, 1, 0), (1, 0, 1), (0, 1, 1), (1, 1, 1)]

    def body(x_ref, *rest):
        out_ref, send_sems, recv_sems = rest[len(after):]
        me = (lax.axis_index("x"), lax.axis_index("y"), lax.axis_index("c"))
        my_rows = out_ref.at[pl.ds((4 * me[0] + 2 * me[1] + me[2]) * m_per, m_per), :]
        out_ref[pl.ds((4 * me[0] + 2 * me[1] + me[2]) * m_per, m_per), :] = x_ref[...]
        copies = []
        for t, rel in enumerate(rels):
            cp = pltpu.make_async_remote_copy(src_ref=x_ref, dst_ref=my_rows, send_sem=send_sems.at[t],
                                              recv_sem=recv_sems.at[t], device_id=_flip(me, rel),
                                              device_id_type=MESH)
            cp.start()
            copies.append(cp)
        for cp in copies:
            cp.wait()

    return pl.pallas_call(
        body, name=name, out_shape=jax.ShapeDtypeStruct((N_DEV * m_per, n), blk.dtype),
        in_specs=[pl.BlockSpec(memory_space=pltpu.VMEM)] + [pl.BlockSpec(memory_space=pl.ANY) for _ in after],
        out_specs=pl.BlockSpec(memory_space=pltpu.VMEM),
        scratch_shapes=[pltpu.SemaphoreType.DMA((7,)), pltpu.SemaphoreType.DMA((7,))],
        compiler_params=pltpu.CompilerParams(has_side_effects=True, vmem_limit_bytes=VMEM_LIMIT),
    )(blk, *after)


def _rope_tables(seq_len, dim, reps):
    rows = seq_len // GRID_W
    t = np.arange(seq_len)
    row, col = (t // GRID_W).astype(np.float32), (t % GRID_W).astype(np.float32)
    half = dim // 2
    inv = jnp.asarray(ROPE_THETA, F32) ** (-jnp.arange(0, half, 2, dtype=F32) / half)
    ang_r = jnp.asarray(row)[:, None] * inv[None, :]
    ang_c = jnp.asarray(col)[:, None] * inv[None, :]
    ang = jnp.concatenate([ang_r, ang_r, ang_c, ang_c], axis=-1)
    cos, sin = jnp.cos(ang), jnp.sin(ang)
    low = (np.arange(dim) % (dim // 2)) < (dim // 4)
    s_lo = jnp.where(jnp.asarray(low)[None, :], -sin, 0.0)
    s_hi = jnp.where(jnp.asarray(low)[None, :], 0.0, sin)
    del rows
    return tuple(jnp.tile(a, (1, reps)) for a in (cos, s_lo, s_hi))


def _make_rope(shift):
    def up(v):
        return pltpu.roll(v, LANE - shift, 1)

    def down(v):
        return pltpu.roll(v, shift, 1)

    @jax.custom_vjp
    def rope(v, cos, s_lo, s_hi):
        return v * cos + up(v) * s_lo + down(v) * s_hi

    def fwd(v, cos, s_lo, s_hi):
        return rope(v, cos, s_lo, s_hi), (cos, s_lo, s_hi)

    def bwd(res, dy):
        cos, s_lo, s_hi = res
        return dy * cos + down(dy * s_lo) + up(dy * s_hi), jnp.zeros_like(cos), jnp.zeros_like(cos), jnp.zeros_like(cos)

    rope.defvjp(fwd, bwd)
    return rope


_rope_a = _make_rope(HEAD_DIM // 4)
_rope_b = _make_rope(QK_ROPE // 4)


def _prep(segs, tabs, g_q, g_k, g_ckv):
    qa, ka, va, qn, qp, ckv, kpe = segs
    ta, tb = tabs
    q_a = jnp.concatenate([_rope_a(_rn(s) * g_q, *ta) for s in qa], axis=1)
    k_a = jnp.concatenate([_rope_a(_rn(s) * g_k, *ta) for s in ka], axis=1)
    cat = []
    for h in range(HB):
        cat += [qn[h], _rope_b(qp[h], *tb)]
    q_cat = jnp.concatenate(cat, axis=1)
    return q_a, k_a, va, q_cat, _rn(ckv) * g_ckv, _rope_b(kpe, *tb)


def _split_proj(proj):
    qa = [proj[:, O_QA + h * LANE:O_QA + (h + 1) * LANE] for h in range(HA)]
    ka = [proj[:, O_KA + h * LANE:O_KA + (h + 1) * LANE] for h in range(HKV)]
    va = proj[:, O_VA:O_QN]
    qn = [proj[:, O_QN + h * LANE:O_QN + (h + 1) * LANE] for h in range(HB)]
    qp = [proj[:, O_QP + h * LANE:O_QP + (h + 1) * LANE] for h in range(HB)]
    return qa, ka, va, qn, qp, proj[:, O_CKV:O_KPE], proj[:, O_KPE:W_INP]


def _permute_w_in(nat):
    K = nat.shape[1]
    qb = nat[W_QA + 2 * W_KA:W_QA + 2 * W_KA + W_QB].reshape(HB, QK_B, K)
    qp = jnp.concatenate([qb[:, QK_NOPE:], jnp.zeros((HB, LANE - QK_ROPE, K), nat.dtype)], axis=1)
    tail = nat[W_QA + 2 * W_KA + W_QB:]
    return jnp.concatenate([nat[:O_QN], qb[:, :QK_NOPE].reshape(HB * QK_NOPE, K), qp.reshape(HB * LANE, K),
                            tail, jnp.zeros((LANE - QK_ROPE, K), nat.dtype)], axis=0)


def _unpermute_pieces():
    pieces = [(0, 0, O_QN)]
    for h in range(HB):
        pieces.append((O_QN + h * QK_NOPE, O_QN + h * QK_B, QK_NOPE))
        pieces.append((O_QP + h * LANE, O_QN + h * QK_B + QK_NOPE, QK_ROPE))
    pieces.append((O_CKV, O_QN + W_QB, KV_RANK))
    pieces.append((O_KPE, O_QN + W_QB + KV_RANK, QK_ROPE))
    return pieces


def kernel(x, c, w_ada, b_ada, g_pre_attn, w_in, g_q_a, g_k_a, g_ckv, w_kv_b, g_out_a, g_out_b, w_out, g_post_attn, g_pre_mlp, w_mlp_in, w_mlp_out, g_post_mlp, loss_target, m_w_ada, m_b_ada, m_g_pre_attn, m_w_in, m_g_q_a, m_g_k_a, m_g_ckv, m_w_kv_b, m_g_out_a, m_g_out_b, m_w_out, m_g_post_attn, m_g_pre_mlp, m_w_mlp_in, m_w_mlp_out, m_g_post_mlp, v_w_ada, v_b_ada, v_g_pre_attn, v_w_in, v_g_q_a, v_g_k_a, v_g_ckv, v_w_kv_b, v_g_out_a, v_g_out_b, v_w_out, v_g_post_attn, v_g_pre_mlp, v_w_mlp_in, v_w_mlp_out, v_g_post_mlp):
    S, D = x.shape[1], x.shape[2]
    x2d, tgt = x[0], loss_target[0]
    ix, iy, ic = lax.axis_index("x"), lax.axis_index("y"), lax.axis_index("c")
    j_me = _chip(ix, iy)
    e_me = 4 * ix + 2 * iy + ic
    T = _pick(S, 512, 8)
    TB = _pick(S, 256, 8)

    where = jnp.stack([ic, j_me]).astype(jnp.int32)
    DFF = w_mlp_out.shape[1] * N_CHIPS
    tabs_a = _rope_tables(S, HEAD_DIM, 1)
    tabs_b = _rope_tables(S, QK_ROPE, LANE // QK_ROPE)

    c_all = _allgather8("gather_c", c.reshape(8, D // 8)).reshape(N_DEV, D)
    w_in_t = jnp.swapaxes(w_in[0], 0, 1)
    h_qkv, tok_a = _gather_start("qkv", _gather_casts("qkv", where, [w_in_t, w_kv_b[0]]), (c_all,))
    later = [_gather_casts(tag, where, [w[0]], (tok_a,))
             for tag, w in (("out", w_out), ("up", w_mlp_in), ("down", w_mlp_out))]

    def silu_fn(cc):
        return (cc * (1.0 / (1.0 + jnp.exp(-cc))),), ()

    (c_act,), _ = _rowwise("silu_c", silu_fn, [c_all], [], [(D, BF16)], [], N_DEV)
    ca_pad = jnp.concatenate([c_act, jnp.zeros_like(c_act)], axis=0)
    n_mod = w_ada.shape[2]
    b_cols = lax.dynamic_slice(b_ada, (0, j_me * n_mod), (1, n_mod))
    (mod_part,) = _mm_nn("mod_mm", ca_pad, w_ada, [F32], epilogue=lambda acc, b: (acc + b,),
                         extras=(jnp.broadcast_to(b_cols, (16, n_mod)),))
    mod_all = _allgather8("gather_mod", mod_part[:8] + tok_a[0, 0],
                          after=[c_[0] for c_ in later] + [*tabs_a, *tabs_b]).reshape(N_DEV, 8, n_mod)
    mod = jnp.concatenate([lax.dynamic_slice(mod_all, (2 * j, e_me, 0), (1, 1, n_mod))[0] for j in range(N_CHIPS)],
                          axis=1)

    m_qkv, tok_q = _gather_mid(h_qkv, mod_all)
    h_out, tok_b = _gather_start("out", later[0], (tok_q,))
    h_up, tok_c = _gather_start("up", later[1], (tok_b,))
    h_down, tok_d = _gather_start("down", later[2], (tok_c,))
    mod = mod + tok_d[0, 0]
    sh_a, sc_a, gt_a, sh_m, sc_m, gt_m = [mod[:, i * D:(i + 1) * D] for i in range(6)]

    def pre_attn_fn(xt, g, sc, sh):
        return (_rn(xt) * g * (1.0 + sc) + sh,), ()

    (h_b,), _ = _rowwise("pre_attn", pre_attn_fn, [x2d], [g_pre_attn, sc_a, sh_a], [(D, BF16)], [], T)
    g_in, g_kvb = _gather_end(m_qkv, h_b)
    wpt = _permute_w_in(g_in.reshape(W_IN, D))[None]
    (proj,) = _mm_nt("proj_mm", h_b, wpt, [F32])

    def prep_fn(pt, ca_, sla, sha, cb_, slb, shb, gq, gk, gc):
        return _prep(_split_proj(pt), ((ca_, sla, sha), (cb_, slb, shb)), gq, gk, gc), ()

    (qa_b, ka_b, va_b, qcat_b, ckvn_b, kpe_b), _ = _rowwise(
        "prep", prep_fn, [proj, *tabs_a, *tabs_b], [g_q_a, g_k_a, g_ckv],
        [(W_QA, BF16), (W_KA, BF16), (W_KA, BF16), (2 * HB * LANE, BF16), (KV_RANK, BF16), (LANE, BF16)], [], T)
    kcat_b, vb_b = _kv_mm("kv_mm", ckvn_b, g_kvb, kpe_b)

    TQ = _pick(S, 512, 16)
    sc_a_ = 1.0 / math.sqrt(HEAD_DIM)
    sc_b_ = 1.0 / math.sqrt(QK_B)
    o_a, lse_a = _attn_fwd("attn_a_fwd", qa_b, ka_b, va_b, HA, HA // HKV, HEAD_DIM, HEAD_DIM, sc_a_, TQ)
    m_out, tok_m = _gather_mid(h_out, o_a)
    o_b, lse_b = _attn_fwd("attn_b_fwd", qcat_b, kcat_b, vb_b, HB, 1, 2 * LANE, V_DIM, sc_b_, TQ, after=(tok_m,))
    m_up, tok_m = _gather_mid(h_up, o_b)

    def mix_fn(oa, ob, ga, gb):
        return jnp.concatenate([_rn(oa) * ga, _rn(ob) * gb], axis=1)

    (g_wout,) = _gather_end(m_out, o_b)
    w_out_f = g_wout.reshape(1, N_CHIPS * g_wout.shape[1], D)
    on_b, o2 = _rows_mm("out_mm", mix_fn, [o_a, o_b], [g_out_a, g_out_b], w_out_f, after=(tok_m,))

    def mid(xt, o2t, gta, gpa, gpm, scm, shm):
        x1 = xt + gta * (_rn(o2t) * gpa)
        return x1, _rn(x1) * gpm * (1.0 + scm) + shm

    def mid_fn(*a):
        return mid(*a), ()

    mid_vecs = [gt_a, g_post_attn, g_pre_mlp, sc_m, sh_m]
    (x1, h2_b), _ = _rowwise("mid", mid_fn, [x2d, o2], mid_vecs, [(D, F32), (D, BF16)], [], T)
    (g_mlp_in,) = _gather_end(m_up, h2_b)
    relu_b, a_b = _mm_nn("mlp_in_mm", h2_b, g_mlp_in, [BF16, BF16],
                         epilogue=lambda acc: (jnp.maximum(acc, 0.0), jnp.square(jnp.maximum(acc, 0.0))))
    (g_mlp_out,) = _gather_finish(h_down, a_b)
    w_mlp_out_f = g_mlp_out.reshape(1, DFF, D)
    (y,) = _mm_nn("mlp_out_mm", a_b, w_mlp_out_f, [F32])

    def last_fn(yt, x1t, tt, gtm, gpo):
        n, r = _rn_parts(yt)
        err = x1t + gtm * (n * gpo) - tt
        dx2 = err * (1.0 / D)
        loss = 0.5 * jnp.sum(jnp.mean(err * err, axis=-1))
        s = _rows(dx2 * n)
        return (_rn_bwd(n, r, dx2 * (gtm * gpo)), dx2), (jnp.full((1, LANE), loss, F32), gpo * s, gtm * s)

    (dy_b, dx2), (loss_v, d_gt_m, d_g_post_mlp) = _rowwise(
        "loss_bwd", last_fn, [y, x1, tgt], [gt_m, g_post_mlp], [(D, BF16), (D, F32)], [LANE, D, D], TB)

    gw_mlp_out = _mm_tn("gw_mlp_out_mm", a_b, dy_b, 1).reshape(N_CHIPS, DFF // N_CHIPS, D)
    b_down, tok = _reduce_begin("down", [gw_mlp_out])
    (du_b,) = _mm_nt("d_mlp_out_mm", dy_b, w_mlp_out_f, [BF16],
                     epilogue=lambda acc, rt: (acc * (2.0 * rt.astype(F32)),), extras=(relu_b,), after=(tok,))
    r_down, tok = _reduce_start(b_down, where, du_b)
    gw_mlp_in = _mm_tn("gw_mlp_in_mm", h2_b, du_b, N_CHIPS, after=(tok,))
    b_up, tok = _reduce_begin("up", [gw_mlp_in])
    (dh2,) = _mm_nt("d_mlp_in_mm", du_b, g_mlp_in, [F32], after=(tok,))
    r_up, tok_up = _reduce_start(b_up, where, dh2)

    def mid_bwd_fn(xt, o2t, dh2t, dx2t, gta, gpa, gpm, scm, shm):
        n2, r2 = _rn_parts(o2t)
        n1, r1 = _rn_parts(xt + gta * (n2 * gpa))
        dx1 = dx2t + _rn_bwd(n1, r1, dh2t * (gpm * (1.0 + scm)))
        s1, s2 = _rows(dh2t * n1), _rows(dx1 * n2)
        do2 = _rn_bwd(n2, r2, dx1 * (gta * gpa))
        return (dx1, do2), (gpa * s2, gta * s2, (1.0 + scm) * s1, gpm * s1, _rows(dh2t))

    (dx1, do2_b), (d_gt_a, d_g_post_attn, d_g_pre_mlp, d_sc_m, d_sh_m) = _rowwise(
        "mid_bwd", mid_bwd_fn, [x2d, o2, dh2, dx2], mid_vecs, [(D, F32), (D, BF16)], [D] * 5, TB, after=(tok_up,))

    gw_out = _mm_tn("gw_out_mm", on_b, do2_b, 1).reshape(N_CHIPS, w_out.shape[1], D)

    def mix_bwd_fn(dont, oa, ob, ga, gb):
        na, ra = _rn_parts(oa)
        nb, rb = _rn_parts(ob)
        da, db = dont[:, :W_QA], dont[:, W_QA:]
        return (_rn_bwd(na, ra, da * ga), _rn_bwd(nb, rb, db * gb)), (_rows(da * na), _rows(db * nb))

    (doa_b, dob_b), (d_g_out_a, d_g_out_b) = _mm_nt_rows(
        "d_out_mm", do2_b, w_out_f[0], mix_bwd_fn, [o_a, o_b], [g_out_a, g_out_b],
        [(W_QA, BF16), (HB * V_DIM, BF16)], [W_QA, HB * V_DIM])

    dqa, dka, dva = _attn_bwd("attn_a_bwd", qa_b, ka_b, va_b, doa_b, o_a, lse_a, HA, HA // HKV, HEAD_DIM, HEAD_DIM,
                              sc_a_, TQ)
    dqcat, dkcat, dvb = _attn_bwd("attn_b_bwd", qcat_b, kcat_b, vb_b, dob_b, o_b, lse_b, HB, 1, 2 * LANE, V_DIM,
                                  sc_b_, TQ)

    d_ckvn, dkv_b, dkpe = _kv_mm_bwd("d_kv_mm", dkcat, dvb, g_kvb)
    gw_kvb = _mm_tn("gw_kv_mm", ckvn_b, dkv_b, N_CHIPS)
    b_kvb, tok_kvb = _reduce_begin("okv", [gw_out, gw_kvb])

    def prep_bwd_fn(pt, dqat, dkat, dvat, dqct, dckt, dkpt, ca_, sla, sha, cb_, slb, shb, gq, gk, gc):
        tabs = ((ca_, sla, sha), (cb_, slb, shb))
        _, vjp = jax.vjp(lambda sg, a1, a2, a3: _prep(sg, tabs, a1, a2, a3), _split_proj(pt), gq, gk, gc)
        (dqa_s, dka_s, dva_s, dqn_s, dqp_s, dck_s, dkp_s), dgq, dgk, dgc = vjp((dqat, dkat, dvat, dqct, dckt, dkpt))
        dproj = jnp.concatenate([*dqa_s, *dka_s, dva_s, *dqn_s, *dqp_s, dck_s, dkp_s], axis=1)
        return (dproj,), (dgq, dgk, dgc)

    (dproj_b,), (d_g_q_a, d_g_k_a, d_g_ckv) = _rowwise(
        "prep_bwd", prep_bwd_fn, [proj, dqa, dka, dva, dqcat, d_ckvn, dkpe, *tabs_a, *tabs_b],
        [g_q_a, g_k_a, g_ckv], [(W_INP, BF16)], [HEAD_DIM, HEAD_DIM, KV_RANK], TB, after=(tok_kvb,))

    r_kvb, tok = _reduce_start(b_kvb, where, dproj_b)
    (dh,) = _mm_nn("d_proj_mm", dproj_b, wpt, [F32], after=(tok,))
    gw_in = _mm_tn_rows("gw_in_mm", dproj_b, h_b, W_IN, _unpermute_pieces()).reshape(N_CHIPS, W_IN // N_CHIPS, D)
    b_in, tok = _reduce_begin("in", [gw_in])

    def pre_attn_bwd_fn(xt, dht, dx1t, g, sc, sh):
        n, r = _rn_parts(xt)
        s = _rows(dht * n)
        return (dx1t + _rn_bwd(n, r, dht * (g * (1.0 + sc))),), ((1.0 + sc) * s, g * s, _rows(dht))

    (grad_x,), (d_g_pre_attn, d_sc_a, d_sh_a) = _rowwise(
        "pre_attn_bwd", pre_attn_bwd_fn, [x2d, dh, dx1], [g_pre_attn, sc_a, sh_a], [(D, F32)], [D] * 3, TB,
        after=(tok,))

    small_names = ["b_ada", "g_pre_attn", "g_q_a", "g_k_a", "g_ckv", "g_out_a", "g_out_b", "g_post_attn",
                   "g_pre_mlp", "g_post_mlp"]
    small_local = jnp.concatenate([d_sh_a, d_sc_a, d_gt_a, d_sh_m, d_sc_m, d_gt_m, d_g_pre_attn, d_g_q_a, d_g_k_a,
                                   d_g_ckv, d_g_out_a, d_g_out_b, d_g_post_attn, d_g_pre_mlp, d_g_post_mlp,
                                   loss_v[:, :8]], axis=1)
    n_sent = small_local.shape[1]
    n_small = n_sent - 8
    small_all = _allgather8("gather_small", small_local.reshape(8, n_sent // 8)).reshape(N_DEV, n_sent)

    def sum8_fn(t):
        return (), (jnp.sum(t, axis=0, keepdims=True),)

    r_in, tok_in = _reduce_start(b_in, where, grad_x, (small_all,))
    _, (sent_sum,) = _rowwise("sum_small", sum8_fn, [small_all], [], [], [n_sent], N_DEV, after=(tok_in,))
    small_sum, loss = sent_sum[:, :n_small], sent_sum[0, n_small]
    dmod_cols = lax.dynamic_slice(small_all, (0, j_me * n_mod), (N_DEV, n_mod))
    dmod_pad = jnp.concatenate([dmod_cols, jnp.zeros_like(dmod_cols)], axis=0).astype(BF16)

    weights = dict(w_ada=w_ada, b_ada=b_ada, g_pre_attn=g_pre_attn, w_in=w_in, g_q_a=g_q_a, g_k_a=g_k_a, g_ckv=g_ckv,
                   w_kv_b=w_kv_b, g_out_a=g_out_a, g_out_b=g_out_b, w_out=w_out, g_post_attn=g_post_attn,
                   g_pre_mlp=g_pre_mlp, w_mlp_in=w_mlp_in, w_mlp_out=w_mlp_out, g_post_mlp=g_post_mlp)
    ms = dict(w_ada=m_w_ada, b_ada=m_b_ada, g_pre_attn=m_g_pre_attn, w_in=m_w_in, g_q_a=m_g_q_a, g_k_a=m_g_k_a,
              g_ckv=m_g_ckv, w_kv_b=m_w_kv_b, g_out_a=m_g_out_a, g_out_b=m_g_out_b, w_out=m_w_out,
              g_post_attn=m_g_post_attn, g_pre_mlp=m_g_pre_mlp, w_mlp_in=m_w_mlp_in, w_mlp_out=m_w_mlp_out,
              g_post_mlp=m_g_post_mlp)
    vs = dict(w_ada=v_w_ada, b_ada=v_b_ada, g_pre_attn=v_g_pre_attn, w_in=v_w_in, g_q_a=v_g_q_a, g_k_a=v_g_k_a,
              g_ckv=v_g_ckv, w_kv_b=v_w_kv_b, g_out_a=v_g_out_a, g_out_b=v_g_out_b, w_out=v_w_out,
              g_post_attn=v_g_post_attn, g_pre_mlp=v_g_pre_mlp, w_mlp_in=v_w_mlp_in, w_mlp_out=v_w_mlp_out,
              g_post_mlp=v_g_post_mlp)
    order = list(weights)
    grads, deltas, new_m, new_v = {}, {}, {}, {}

    def update(name, g, after=()):
        view = (lambda t: jnp.swapaxes(t, 0, 1)) if name == "w_in" else (lambda t: t)
        d_, m_, v_, g_ = _adamw("adamw_" + name, view(weights[name][0]), g, view(ms[name][0]), view(vs[name][0]),
                                after, rewrite_g=True)
        grads[name], deltas[name], new_m[name], new_v[name] = (view(t)[None] for t in (g_, d_, m_, v_))
        return d_

    chain = ((("w_mlp_out",), r_down), (("w_mlp_in",), r_up), (("w_out", "w_kv_b"), r_kvb), (("w_in",), r_in))
    swap, tok = _reduce_finish(chain[0][1], where, small_sum)
    last, m_, v_, g_ = _adamw_outer("adamw_w_ada", w_ada[0], ca_pad, dmod_pad, m_w_ada[0], v_w_ada[0], (tok,))
    grads["w_ada"], deltas["w_ada"], new_m["w_ada"], new_v["w_ada"] = g_[None], last[None], m_[None], v_[None]
    for k, (names, _) in enumerate(chain):
        gs = _reduce_end(swap, last)
        if k + 1 < len(chain):
            swap, tok = _reduce_finish(chain[k + 1][1], where, gs[0])
        for name, g in zip(names, gs):
            last = update(name, g, (tok,))

    cat = lambda d: jnp.concatenate([d[n] for n in small_names], axis=1)
    d_, m_, v_ = _adamw("adamw_small", cat(weights), small_sum, cat(ms), cat(vs))
    off = 0
    for n in small_names:
        w_ = weights[n].shape[1]
        grads[n], deltas[n], new_m[n], new_v[n] = (a[:, off:off + w_] for a in (small_sum, d_, m_, v_))
        off += w_

    return (loss, grad_x[None], *[grads[n] for n in order], *[deltas[n] for n in order],
            *[new_m[n] for n in order], *[new_v[n] for n in order])
```
